```python
import jax, jax.numpy as jnp
from jax import lax
import numpy as np

D_MODEL = 1024
BATCH = 8
SEQ = 2048
DEPTH = 2

CHUNK = 64
D_MIX = D_MODEL
HEAD_DIM = 64
SB_HEADS = D_MIX // 2 // HEAD_DIM
SB_WIDTH = SB_HEADS * HEAD_DIM
SC_GROUPS = D_MIX // 4 // HEAD_DIM
SC_WIDTH = SC_GROUPS * HEAD_DIM
SG_HEADS = D_MIX // 4 // HEAD_DIM
SG_WIDTH = SG_HEADS * HEAD_DIM
SB_BLOCK = 128
SC_KERNEL = 3
SG_CHUNK = 128
D_FF = 4 * D_MODEL
PROJ_WIDTH = 3 * SB_WIDTH + 3 * SC_WIDTH + 2 * SG_WIDTH
N_MOD = 6
EPS = 1e-6

kernel_name = "hybrid_sb_conv_gmlp_adaln_trunk"


def rmsnorm(x, g):
    x32 = x.astype(jnp.float32)
    y = x32 * lax.rsqrt(jnp.mean(x32 * x32, axis=-1, keepdims=True) + EPS)
    return (y * g.astype(jnp.float32)).astype(x.dtype)


def stick_breaking_attention(q, k, v):
    bsz, seq, nh, hd = q.shape
    scale = hd ** -0.5
    q32 = q.astype(jnp.float32)
    k32 = k.astype(jnp.float32)
    v32 = v.astype(jnp.float32)
    outs = []
    for i in range(seq // SB_BLOCK):
        kv_len = (i + 1) * SB_BLOCK
        q_blk = q32[:, i * SB_BLOCK:(i + 1) * SB_BLOCK]
        k_ctx = k32[:, :kv_len]
        v_ctx = v32[:, :kv_len]
        z = jnp.einsum('bqhd,bkhd->bhqk', q_blk, k_ctx) * scale
        t_pos = i * SB_BLOCK + jnp.arange(SB_BLOCK)
        s_pos = jnp.arange(kv_len)
        mask = s_pos[None, :] < t_pos[:, None]
        log_beta = jax.nn.log_sigmoid(z)
        log_stay = jnp.where(mask, jax.nn.log_sigmoid(-z), 0.0)
        log_after = lax.cumsum(log_stay, axis=3, reverse=True) - log_stay
        w = jnp.where(mask, jnp.exp(log_beta + log_after), 0.0)
        outs.append(jnp.einsum('bhqk,bkhd->bqhd', w, v_ctx))
    return jnp.concatenate(outs, axis=1).astype(q.dtype)


def short_gated_conv(b_gate, c_gate, h, conv_w, conv_b):
    u = c_gate * h
    y = lax.conv_general_dilated(
        u, conv_w[:, None, :].astype(u.dtype), window_strides=(1,),
        padding=[(SC_KERNEL - 1, 0)], dimension_numbers=('NWC', 'WIO', 'NWC'),
        feature_group_count=SC_WIDTH)
    return b_gate * (y + conv_b)


def spatial_gating(u, v, norm_g, sw, sb):
    bsz, seq, _ = v.shape
    u = jax.nn.gelu(u)
    v = rmsnorm(jax.nn.gelu(v), norm_g)
    n_win = seq // SG_CHUNK
    v = v.reshape(bsz, n_win, SG_CHUNK, SG_HEADS, HEAD_DIM)
    chunk_id = jnp.arange(SG_CHUNK) // CHUNK
    mask = chunk_id[:, None] >= chunk_id[None, :]
    w = jnp.where(mask[None], sw, jnp.zeros_like(sw))
    mixed = jnp.einsum('hts,bnshd->bnthd', w, v) + sb.T[None, None, :, :, None]
    return u * mixed.reshape(bsz, seq, SG_WIDTH)


def _fwd_setup_inputs(seed: int = 0) -> dict:
    key = jax.random.key(seed)
    ks = jax.random.split(key, 16)
    f32 = jnp.float32
    L = DEPTH
    x = jax.random.normal(ks[0], (BATCH, SEQ, D_MODEL), f32)
    c = jax.random.normal(ks[1], (BATCH, D_MODEL), f32)
    ada_w = jax.random.normal(ks[2], (L, D_MODEL, N_MOD * D_MODEL), f32) * D_MODEL ** -0.5
    ada_b = jax.random.normal(ks[3], (L, N_MOD * D_MODEL), f32) * 0.02
    norm_mix_g = 1.0 + 0.02 * jax.random.normal(ks[4], (L, D_MODEL), f32)
    norm_mlp_g = 1.0 + 0.02 * jax.random.normal(ks[5], (L, D_MODEL), f32)
    w_in = jax.random.normal(ks[6], (L, D_MODEL, PROJ_WIDTH), f32) * D_MODEL ** -0.5
    conv_w = jax.random.normal(ks[7], (L, SC_KERNEL, SC_WIDTH), f32) * SC_KERNEL ** -0.5
    conv_b = jax.random.normal(ks[8], (L, SC_WIDTH), f32) * 0.02
    gmlp_norm_g = 1.0 + 0.02 * jax.random.normal(ks[9], (L, SG_WIDTH), f32)
    spatial_w = jax.random.normal(ks[10], (L, SG_HEADS, SG_CHUNK, SG_CHUNK), f32) * SG_CHUNK ** -0.5
    spatial_b = 1.0 + 0.1 * jax.random.normal(ks[11], (L, SG_HEADS, SG_CHUNK), f32)
    w_out = jax.random.normal(ks[12], (L, D_MIX, D_MODEL), f32) * D_MIX ** -0.5
    mlp_w1 = jax.random.normal(ks[13], (L, D_MODEL, D_FF), f32) * D_MODEL ** -0.5
    mlp_w2 = jax.random.normal(ks[14], (L, D_FF, D_MODEL), f32) * D_FF ** -0.5
    final_norm_g = 1.0 + 0.02 * jax.random.normal(ks[15], (D_MODEL,), f32)
    return {"x": x, "c": c, "ada_w": ada_w, "ada_b": ada_b,
            "norm_mix_g": norm_mix_g, "norm_mlp_g": norm_mlp_g, "w_in": w_in,
            "conv_w": conv_w, "conv_b": conv_b, "gmlp_norm_g": gmlp_norm_g,
            "spatial_w": spatial_w, "spatial_b": spatial_b, "w_out": w_out,
            "mlp_w1": mlp_w1, "mlp_w2": mlp_w2, "final_norm_g": final_norm_g}


def _fwd_reference(x, c, ada_w, ada_b, norm_mix_g, norm_mlp_g, w_in, conv_w, conv_b,
              gmlp_norm_g, spatial_w, spatial_b, w_out, mlp_w1, mlp_w2, final_norm_g):
    bsz, seq, _ = x.shape
    bounds = [SB_WIDTH, 2 * SB_WIDTH, 3 * SB_WIDTH,
              3 * SB_WIDTH + SC_WIDTH, 3 * SB_WIDTH + 2 * SC_WIDTH,
              3 * SB_WIDTH + 3 * SC_WIDTH, 3 * SB_WIDTH + 3 * SC_WIDTH + SG_WIDTH]
    c_act = jax.nn.silu(c)
    for l in range(DEPTH):
        mod = c_act @ ada_w[l] + ada_b[l]
        sh_m, sc_m, g_m, sh_f, sc_f, g_f = jnp.split(mod[:, None, :], N_MOD, axis=-1)

        h = rmsnorm(x, norm_mix_g[l]) * (1.0 + sc_m) + sh_m
        proj = h @ w_in[l]
        q, k, v, b_gate, c_gate, h_conv, u_sg, v_sg = jnp.split(proj, bounds, axis=-1)
        a_out = stick_breaking_attention(
            q.reshape(bsz, seq, SB_HEADS, HEAD_DIM),
            k.reshape(bsz, seq, SB_HEADS, HEAD_DIM),
            v.reshape(bsz, seq, SB_HEADS, HEAD_DIM)).reshape(bsz, seq, SB_WIDTH)
        c_out = short_gated_conv(b_gate, c_gate, h_conv, conv_w[l], conv_b[l])
        s_out = spatial_gating(u_sg, v_sg, gmlp_norm_g[l], spatial_w[l], spatial_b[l])
        mix = jnp.concatenate([a_out, c_out, s_out], axis=-1) @ w_out[l]
        x = x + g_m * mix

        h = rmsnorm(x, norm_mlp_g[l]) * (1.0 + sc_f) + sh_f
        x = x + g_f * (jnp.square(jax.nn.relu(h @ mlp_w1[l])) @ mlp_w2[l])
    return rmsnorm(x, final_norm_g)


import jax as _jax
import jax.numpy as _jnp

TWIN_FORMAT = 'train_step'
FWD_PARAMS = ['x', 'c', 'ada_w', 'ada_b', 'norm_mix_g', 'norm_mlp_g', 'w_in', 'conv_w', 'conv_b', 'gmlp_norm_g', 'spatial_w', 'spatial_b', 'w_out', 'mlp_w1', 'mlp_w2', 'final_norm_g']
TWIN_WEIGHTS = ['ada_w', 'ada_b', 'norm_mix_g', 'norm_mlp_g', 'w_in', 'conv_w', 'conv_b', 'gmlp_norm_g', 'spatial_w', 'spatial_b', 'w_out', 'mlp_w1', 'mlp_w2', 'final_norm_g']
TWIN_DIFF_INPUT = 'x'
TWIN_INPUTS = ['x', 'c', 'ada_w', 'ada_b', 'norm_mix_g', 'norm_mlp_g', 'w_in', 'conv_w', 'conv_b', 'gmlp_norm_g', 'spatial_w', 'spatial_b', 'w_out', 'mlp_w1', 'mlp_w2', 'final_norm_g', 'loss_target', 'm_ada_w', 'm_ada_b', 'm_norm_mix_g', 'm_norm_mlp_g', 'm_w_in', 'm_conv_w', 'm_conv_b', 'm_gmlp_norm_g', 'm_spatial_w', 'm_spatial_b', 'm_w_out', 'm_mlp_w1', 'm_mlp_w2', 'm_final_norm_g', 'v_ada_w', 'v_ada_b', 'v_norm_mix_g', 'v_norm_mlp_g', 'v_w_in', 'v_conv_w', 'v_conv_b', 'v_gmlp_norm_g', 'v_spatial_w', 'v_spatial_b', 'v_w_out', 'v_mlp_w1', 'v_mlp_w2', 'v_final_norm_g']
TWIN_OUTPUTS = ['loss', 'grad_x', 'grad_ada_w', 'grad_ada_b', 'grad_norm_mix_g', 'grad_norm_mlp_g', 'grad_w_in', 'grad_conv_w', 'grad_conv_b', 'grad_gmlp_norm_g', 'grad_spatial_w', 'grad_spatial_b', 'grad_w_out', 'grad_mlp_w1', 'grad_mlp_w2', 'grad_final_norm_g', 'delta_ada_w', 'delta_ada_b', 'delta_norm_mix_g', 'delta_norm_mlp_g', 'delta_w_in', 'delta_conv_w', 'delta_conv_b', 'delta_gmlp_norm_g', 'delta_spatial_w', 'delta_spatial_b', 'delta_w_out', 'delta_mlp_w1', 'delta_mlp_w2', 'delta_final_norm_g', 'new_m_ada_w', 'new_m_ada_b', 'new_m_norm_mix_g', 'new_m_norm_mlp_g', 'new_m_w_in', 'new_m_conv_w', 'new_m_conv_b', 'new_m_gmlp_norm_g', 'new_m_spatial_w', 'new_m_spatial_b', 'new_m_w_out', 'new_m_mlp_w1', 'new_m_mlp_w2', 'new_m_final_norm_g', 'new_v_ada_w', 'new_v_ada_b', 'new_v_norm_mix_g', 'new_v_norm_mlp_g', 'new_v_w_in', 'new_v_conv_w', 'new_v_conv_b', 'new_v_gmlp_norm_g', 'new_v_spatial_w', 'new_v_spatial_b', 'new_v_w_out', 'new_v_mlp_w1', 'new_v_mlp_w2', 'new_v_final_norm_g']
TWIN_LEAF_KINDS = {'loss': 'loss', 'grad_x': 'grad_x', 'grad_ada_w': 'grad_w', 'grad_ada_b': 'grad_w', 'grad_norm_mix_g': 'grad_w', 'grad_norm_mlp_g': 'grad_w', 'grad_w_in': 'grad_w', 'grad_conv_w': 'grad_w', 'grad_conv_b': 'grad_w', 'grad_gmlp_norm_g': 'grad_w', 'grad_spatial_w': 'grad_w', 'grad_spatial_b': 'grad_w', 'grad_w_out': 'grad_w', 'grad_mlp_w1': 'grad_w', 'grad_mlp_w2': 'grad_w', 'grad_final_norm_g': 'grad_w', 'delta_ada_w': 'delta_w', 'delta_ada_b': 'delta_w', 'delta_norm_mix_g': 'delta_w', 'delta_norm_mlp_g': 'delta_w', 'delta_w_in': 'delta_w', 'delta_conv_w': 'delta_w', 'delta_conv_b': 'delta_w', 'delta_gmlp_norm_g': 'delta_w', 'delta_spatial_w': 'delta_w', 'delta_spatial_b': 'delta_w', 'delta_w_out': 'delta_w', 'delta_mlp_w1': 'delta_w', 'delta_mlp_w2': 'delta_w', 'delta_final_norm_g': 'delta_w', 'new_m_ada_w': 'new_m', 'new_m_ada_b': 'new_m', 'new_m_norm_mix_g': 'new_m', 'new_m_norm_mlp_g': 'new_m', 'new_m_w_in': 'new_m', 'new_m_conv_w': 'new_m', 'new_m_conv_b': 'new_m', 'new_m_gmlp_norm_g': 'new_m', 'new_m_spatial_w': 'new_m', 'new_m_spatial_b': 'new_m', 'new_m_w_out': 'new_m', 'new_m_mlp_w1': 'new_m', 'new_m_mlp_w2': 'new_m', 'new_m_final_norm_g': 'new_m', 'new_v_ada_w': 'new_v', 'new_v_ada_b': 'new_v', 'new_v_norm_mix_g': 'new_v', 'new_v_norm_mlp_g': 'new_v', 'new_v_w_in': 'new_v', 'new_v_conv_w': 'new_v', 'new_v_conv_b': 'new_v', 'new_v_gmlp_norm_g': 'new_v', 'new_v_spatial_w': 'new_v', 'new_v_spatial_b': 'new_v', 'new_v_w_out': 'new_v', 'new_v_mlp_w1': 'new_v', 'new_v_mlp_w2': 'new_v', 'new_v_final_norm_g': 'new_v'}


def _forward(args):
    return _fwd_reference(*[args[k] for k in FWD_PARAMS])


def _output_shape():
    out = _jax.eval_shape(lambda: _forward(_fwd_setup_inputs(0)))
    return out.shape, out.dtype

N_MICROBATCH = 1
ADAM_LR = 0.001
ADAM_B1 = 0.9
ADAM_B2 = 0.999
ADAM_EPS = 1e-08
ADAM_WD = 0.01
ADAM_STEP = 10
PER_EXAMPLE_BATCH_AXIS = {'x': 0, 'c': 0, 'loss_target': 0}
SHARED_INPUTS = []
_WEIGHT_DTYPES = {'ada_w': _jnp.float32, 'ada_b': _jnp.float32, 'norm_mix_g': _jnp.float32, 'norm_mlp_g': _jnp.float32, 'w_in': _jnp.float32, 'conv_w': _jnp.float32, 'conv_b': _jnp.float32, 'gmlp_norm_g': _jnp.float32, 'spatial_w': _jnp.float32, 'spatial_b': _jnp.float32, 'w_out': _jnp.float32, 'mlp_w1': _jnp.float32, 'mlp_w2': _jnp.float32, 'final_norm_g': _jnp.float32}
MOMENT_SCALE = {'ada_w': 7.496319e-02, 'ada_b': 1.279697e-01, 'norm_mix_g': 1.125993e-01, 'norm_mlp_g': 8.414595e-02, 'w_in': 7.392129e-02, 'conv_w': 1.180270e-01, 'conv_b': 7.349248e-02, 'gmlp_norm_g': 4.848862e-02, 'spatial_w': 2.878870e-02, 'spatial_b': 3.484747e-02, 'w_out': 8.187990e-02, 'mlp_w1': 5.109883e-02, 'mlp_w2': 1.011362e-01, 'final_norm_g': 1.721496e+01}


def _to_microbatches(a, axis):
    t = _jnp.moveaxis(a, axis, 0)
    t = t.reshape((N_MICROBATCH, t.shape[0] // N_MICROBATCH) + t.shape[1:])
    return _jnp.moveaxis(t, 1, axis + 1)


def setup_inputs(seed: int = 0) -> dict:
    inp = _fwd_setup_inputs(seed)
    key = _jax.random.fold_in(_jax.random.key(seed), 7919)
    shape, _ = _output_shape()
    out = dict(inp)
    out["loss_target"] = _jax.random.normal(_jax.random.fold_in(key, 0), shape, _jnp.float32)
    for i, name in enumerate(TWIN_WEIGHTS):
        w = inp[name].astype(_jnp.float32)
        if MOMENT_SCALE is None:
            s = _jnp.sqrt(_jnp.mean(_jnp.square(w)) + 1e-30)
        else:
            s = MOMENT_SCALE[name]
        km, kv = _jax.random.split(_jax.random.fold_in(key, i + 1))
        out[name] = w
        out["m_" + name] = s * _jax.random.normal(km, w.shape, _jnp.float32)
        out["v_" + name] = (s * s) * _jax.random.uniform(kv, w.shape, _jnp.float32, 0.5, 1.5)
    if N_MICROBATCH > 1:
        for name, axis in PER_EXAMPLE_BATCH_AXIS.items():
            out[name] = _to_microbatches(out[name], axis)
    return {'x': out['x'], 'c': out['c'], 'ada_w': out['ada_w'], 'ada_b': out['ada_b'], 'norm_mix_g': out['norm_mix_g'], 'norm_mlp_g': out['norm_mlp_g'], 'w_in': out['w_in'], 'conv_w': out['conv_w'], 'conv_b': out['conv_b'], 'gmlp_norm_g': out['gmlp_norm_g'], 'spatial_w': out['spatial_w'], 'spatial_b': out['spatial_b'], 'w_out': out['w_out'], 'mlp_w1': out['mlp_w1'], 'mlp_w2': out['mlp_w2'], 'final_norm_g': out['final_norm_g'], 'loss_target': out['loss_target'], 'm_ada_w': out['m_ada_w'], 'm_ada_b': out['m_ada_b'], 'm_norm_mix_g': out['m_norm_mix_g'], 'm_norm_mlp_g': out['m_norm_mlp_g'], 'm_w_in': out['m_w_in'], 'm_conv_w': out['m_conv_w'], 'm_conv_b': out['m_conv_b'], 'm_gmlp_norm_g': out['m_gmlp_norm_g'], 'm_spatial_w': out['m_spatial_w'], 'm_spatial_b': out['m_spatial_b'], 'm_w_out': out['m_w_out'], 'm_mlp_w1': out['m_mlp_w1'], 'm_mlp_w2': out['m_mlp_w2'], 'm_final_norm_g': out['m_final_norm_g'], 'v_ada_w': out['v_ada_w'], 'v_ada_b': out['v_ada_b'], 'v_norm_mix_g': out['v_norm_mix_g'], 'v_norm_mlp_g': out['v_norm_mlp_g'], 'v_w_in': out['v_w_in'], 'v_conv_w': out['v_conv_w'], 'v_conv_b': out['v_conv_b'], 'v_gmlp_norm_g': out['v_gmlp_norm_g'], 'v_spatial_w': out['v_spatial_w'], 'v_spatial_b': out['v_spatial_b'], 'v_w_out': out['v_w_out'], 'v_mlp_w1': out['v_mlp_w1'], 'v_mlp_w2': out['v_mlp_w2'], 'v_final_norm_g': out['v_final_norm_g']}


def _loss(weights, diff, rest, loss_target):
    with _jax.named_scope("forward"):
        args = {**rest, TWIN_DIFF_INPUT: diff, **{k: w.astype(_WEIGHT_DTYPES[k]) for k, w in weights.items()}}
        y = _forward(args)
    with _jax.named_scope("loss_head"):
        err = _jnp.square(y.astype(_jnp.float32) - loss_target)
        return 0.5 * _jnp.sum(_jnp.mean(err, axis=-1)) if err.ndim else 0.5 * err


def _adamw(w, g, m, v):
    m = ADAM_B1 * m + (1.0 - ADAM_B1) * g
    v = ADAM_B2 * v + (1.0 - ADAM_B2) * _jnp.square(g)
    m_hat = m / (1.0 - ADAM_B1 ** ADAM_STEP)
    v_hat = v / (1.0 - ADAM_B2 ** ADAM_STEP)
    delta = -ADAM_LR * (m_hat / (_jnp.sqrt(v_hat) + ADAM_EPS) + ADAM_WD * w)
    return delta, m, v


def reference(x, c, ada_w, ada_b, norm_mix_g, norm_mlp_g, w_in, conv_w, conv_b, gmlp_norm_g, spatial_w, spatial_b, w_out, mlp_w1, mlp_w2, final_norm_g, loss_target, m_ada_w, m_ada_b, m_norm_mix_g, m_norm_mlp_g, m_w_in, m_conv_w, m_conv_b, m_gmlp_norm_g, m_spatial_w, m_spatial_b, m_w_out, m_mlp_w1, m_mlp_w2, m_final_norm_g, v_ada_w, v_ada_b, v_norm_mix_g, v_norm_mlp_g, v_w_in, v_conv_w, v_conv_b, v_gmlp_norm_g, v_spatial_w, v_spatial_b, v_w_out, v_mlp_w1, v_mlp_w2, v_final_norm_g):
    given = dict(x=x, c=c, ada_w=ada_w, ada_b=ada_b, norm_mix_g=norm_mix_g, norm_mlp_g=norm_mlp_g, w_in=w_in, conv_w=conv_w, conv_b=conv_b, gmlp_norm_g=gmlp_norm_g, spatial_w=spatial_w, spatial_b=spatial_b, w_out=w_out, mlp_w1=mlp_w1, mlp_w2=mlp_w2, final_norm_g=final_norm_g, loss_target=loss_target, m_ada_w=m_ada_w, m_ada_b=m_ada_b, m_norm_mix_g=m_norm_mix_g, m_norm_mlp_g=m_norm_mlp_g, m_w_in=m_w_in, m_conv_w=m_conv_w, m_conv_b=m_conv_b, m_gmlp_norm_g=m_gmlp_norm_g, m_spatial_w=m_spatial_w, m_spatial_b=m_spatial_b, m_w_out=m_w_out, m_mlp_w1=m_mlp_w1, m_mlp_w2=m_mlp_w2, m_final_norm_g=m_final_norm_g, v_ada_w=v_ada_w, v_ada_b=v_ada_b, v_norm_mix_g=v_norm_mix_g, v_norm_mlp_g=v_norm_mlp_g, v_w_in=v_w_in, v_conv_w=v_conv_w, v_conv_b=v_conv_b, v_gmlp_norm_g=v_gmlp_norm_g, v_spatial_w=v_spatial_w, v_spatial_b=v_spatial_b, v_w_out=v_w_out, v_mlp_w1=v_mlp_w1, v_mlp_w2=v_mlp_w2, v_final_norm_g=v_final_norm_g)
    weights = {n: given[n] for n in TWIN_WEIGHTS}
    shared = {n: given[n] for n in SHARED_INPUTS}
    per_example = {n: given[n] for n in ['x', 'c']}
    grad_fn = _jax.value_and_grad(_loss, argnums=(0, 1))

    def one_microbatch(ex, loss_target):
        ex = dict(ex)
        diff = ex.pop(TWIN_DIFF_INPUT)
        return grad_fn(weights, diff, {**shared, **ex}, loss_target)

    if N_MICROBATCH == 1:
        loss, (grad_w, grad_x) = one_microbatch(per_example, given["loss_target"])
    else:
        def body(carry, xs):
            loss_sum, grad_sum = carry
            l_k, (gw_k, gx_k) = one_microbatch(xs[0], xs[1])
            with _jax.named_scope("update"):
                return (loss_sum + l_k, _jax.tree.map(_jnp.add, grad_sum, gw_k)), gx_k

        init = (_jnp.zeros((), _jnp.float32), _jax.tree.map(_jnp.zeros_like, weights))
        (loss, grad_w), grad_x = _jax.lax.scan(body, init, (per_example, given["loss_target"]))
    with _jax.named_scope("update"):
        delta_w, new_m, new_v = {}, {}, {}
        for n in TWIN_WEIGHTS:
            delta_w[n], new_m[n], new_v[n] = _adamw(weights[n], grad_w[n], given["m_" + n], given["v_" + n])
    return (loss, grad_x, *[grad_w[n] for n in TWIN_WEIGHTS], *[delta_w[n] for n in TWIN_WEIGHTS],
            *[new_m[n] for n in TWIN_WEIGHTS], *[new_v[n] for n in TWIN_WEIGHTS])
```

```python
import functools
import math

import jax
import jax.numpy as jnp
from jax import lax
from jax.experimental import pallas as pl
from jax.experimental.pallas import tpu as pltpu

F32 = jnp.float32
BF16 = jnp.bfloat16
MESH = pl.DeviceIdType.MESH

S = 2048
D = 1024
L = 2
NDEV = 8
HD = 64
NH = 8
PROJ = 2816
DFF = 4096
NMOD = 6
EPS = 1e-6
T = 128
SG_HEADS = 4

LR, B1, B2, AEPS, WD, STEP = 0.001, 0.9, 0.999, 1e-08, 0.01, 10
BC1 = 1.0 - B1 ** STEP
BC2 = 1.0 - B2 ** STEP

VMEM_LIMIT = 48 * 1024 * 1024

HBM_SPEC = pl.BlockSpec(memory_space=pltpu.HBM)


def _cparams(sem=None):
    return pltpu.CompilerParams(dimension_semantics=sem, vmem_limit_bytes=VMEM_LIMIT)


def _my_pos():
    return lax.axis_index("x"), lax.axis_index("y"), lax.axis_index("c")


def _lin(p):
    return 4 * p[0] + 2 * p[1] + p[2]


def all_gather(arrs, name):
    n = len(arrs)

    def body(*refs):
        ins, outs = refs[:n], refs[n:2 * n]
        send_sems, recv_sems, local_sems = refs[2 * n:]
        x, y, c = _my_pos()
        me, sibling = (x, y, c), (x, y, 1 - c)
        chips = [(1 - x, y), (x, 1 - y), (1 - x, 1 - y)]

        def copy(a, k, block, to, src=None):
            slot = outs[a].at[_lin(block)]
            return pltpu.make_async_remote_copy(
                src_ref=slot if src is None else src, dst_ref=slot,
                send_sem=send_sems.at[a, k], recv_sem=recv_sems.at[a, k],
                device_id=to, device_id_type=MESH)

        mine = [pltpu.make_async_copy(ins[a], outs[a].at[_lin(me)], local_sems.at[a]) for a in range(n)]
        for cp in mine:
            cp.start()
        first = []
        for a in range(n):
            first.append(copy(a, 0, me, sibling, src=ins[a]))
            first += [copy(a, 1 + j, me, (*chip, c), src=ins[a]) for j, chip in enumerate(chips)]
        for cp in first:
            cp.start()
        passed = []
        for j, chip in enumerate(chips):
            for a in range(n):
                copy(a, 1 + j, (*chip, c), me).wait_recv()
                fwd = copy(a, 4 + j, (*chip, c), sibling)
                fwd.start()
                passed.append(fwd)
        for a in range(n):
            copy(a, 0, sibling, me).wait_recv()
        for j, chip in enumerate(chips):
            for a in range(n):
                copy(a, 4 + j, (*chip, 1 - c), me).wait_recv()
        for cp in first + passed:
            cp.wait_send()
        for cp in mine:
            cp.wait()

    outs = pl.pallas_call(
        body, name=name,
        out_shape=[jax.ShapeDtypeStruct((NDEV,) + a.shape, a.dtype) for a in arrs],
        in_specs=[HBM_SPEC] * n, out_specs=[HBM_SPEC] * n,
        scratch_shapes=[pltpu.SemaphoreType.DMA((n, 7)), pltpu.SemaphoreType.DMA((n, 7)),
                        pltpu.SemaphoreType.DMA((n,))],
    )(*arrs)
    return list(outs)


def all_to_all(arrs, name):
    n = len(arrs)

    def body(*refs):
        ins, outs = refs[:n], refs[n:2 * n]
        send_sems, recv_sems, local_sems = refs[2 * n:]
        x, y, c = _my_pos()
        me = (x, y, c)

        def peer(mask):
            return (1 - x if mask & 4 else x, 1 - y if mask & 2 else y, 1 - c if mask & 1 else c)

        def copy(a, mask):
            return pltpu.make_async_remote_copy(
                src_ref=ins[a].at[_lin(peer(mask))], dst_ref=outs[a].at[_lin(me)],
                send_sem=send_sems.at[a, mask - 1], recv_sem=recv_sems.at[a, mask - 1],
                device_id=peer(mask), device_id_type=MESH)

        def arrival(a, mask):
            return pltpu.make_async_remote_copy(
                src_ref=ins[a].at[_lin(me)], dst_ref=outs[a].at[_lin(peer(mask))],
                send_sem=send_sems.at[a, mask - 1], recv_sem=recv_sems.at[a, mask - 1],
                device_id=peer(mask), device_id_type=MESH)

        mine = [pltpu.make_async_copy(ins[a].at[_lin(me)], outs[a].at[_lin(me)], local_sems.at[a])
                for a in range(n)]
        for cp in mine:
            cp.start()
        sends = [copy(a, mask) for mask in (4, 2, 6, 1, 5, 3, 7) for a in range(n)]
        for cp in sends:
            cp.start()
        for mask in range(1, 8):
            for a in range(n):
                arrival(a, mask).wait_recv()
        for cp in sends:
            cp.wait_send()
        for cp in mine:
            cp.wait()

    outs = pl.pallas_call(
        body, name=name,
        out_shape=[jax.ShapeDtypeStruct(a.shape, a.dtype) for a in arrs],
        in_specs=[HBM_SPEC] * n, out_specs=[HBM_SPEC] * n,
        scratch_shapes=[pltpu.SemaphoreType.DMA((n, 7)), pltpu.SemaphoreType.DMA((n, 7)),
                        pltpu.SemaphoreType.DMA((n,))],
    )(*arrs)
    return list(outs)


def mm(a, b, *, l=None, tm, tn, out_dtypes, epilogue=None, extras=(), name, trans_a=False):
    if trans_a:
        kdim, m = a.shape
    else:
        m, kdim = a.shape
    ncols = b.shape[-1]
    assert b.shape[-2] == kdim and m % tm == 0 and ncols % tn == 0
    if trans_a:
        a_spec = pl.BlockSpec((kdim, tm), lambda i, j: (0, i))
    else:
        a_spec = pl.BlockSpec((tm, kdim), lambda i, j: (i, 0))
    if b.ndim == 3:
        b_spec = pl.BlockSpec((None, kdim, tn), lambda i, j: (l, 0, j))
    else:
        b_spec = pl.BlockSpec((kdim, tn), lambda i, j: (0, j))
    ex_specs = []
    for arr, kind in extras:
        if kind == "tile":
            ex_specs.append(pl.BlockSpec((tm, tn), lambda i, j: (i, j)))
        else:
            ex_specs.append(pl.BlockSpec((1, tn), lambda i, j: (0, j)))
    n_ex, n_out = len(extras), len(out_dtypes)

    def body(a_ref, b_ref, *rest):
        ex_refs, out_refs = rest[:n_ex], rest[n_ex:]
        if trans_a:
            acc = lax.dot_general(a_ref[...], b_ref[...], (((0,), (0,)), ((), ())),
                                  preferred_element_type=F32)
        else:
            acc = jnp.dot(a_ref[...], b_ref[...], preferred_element_type=F32)
        outs = (acc,) if epilogue is None else epilogue(acc, *[r[...] for r in ex_refs])
        for o_ref, val in zip(out_refs, outs):
            o_ref[...] = val.astype(o_ref.dtype)

    outs = pl.pallas_call(
        body, name=name, grid=(m // tm, ncols // tn),
        in_specs=[a_spec, b_spec] + ex_specs,
        out_specs=[pl.BlockSpec((tm, tn), lambda i, j: (i, j)) for _ in range(n_out)],
        out_shape=[jax.ShapeDtypeStruct((m, ncols), dt) for dt in out_dtypes],
        compiler_params=_cparams(("parallel", "parallel")),
    )(a, b, *[arr for arr, _ in extras])
    return list(outs)


TR = 256

ROW_SPEC = pl.BlockSpec((TR, D), lambda i: (i, 0))
VEC_SPEC = pl.BlockSpec((1, D), lambda i: (0, 0))


def normmod_fwd(x, g, sc, sh, name):
    def body(x_ref, g_ref, sc_ref, sh_ref, o_ref):
        xv = x_ref[...]
        rstd = lax.rsqrt(jnp.mean(xv * xv, axis=-1, keepdims=True) + EPS)
        n = (xv * rstd) * g_ref[...]
        o_ref[...] = (n * (1.0 + sc_ref[...]) + sh_ref[...]).astype(o_ref.dtype)

    return pl.pallas_call(
        body, name=name, grid=(S // TR,),
        in_specs=[ROW_SPEC, VEC_SPEC, VEC_SPEC, VEC_SPEC], out_specs=ROW_SPEC,
        out_shape=jax.ShapeDtypeStruct((S, D), BF16),
        compiler_params=_cparams(("parallel",)),
    )(x, g, sc, sh)


def normmod_bwd(x, dh, dres, g, sc, name):
    def body(x_ref, dh_ref, dres_ref, g_ref, sc_ref, dx_ref, dsc_ref, dsh_ref, dg_ref):
        @pl.when(pl.program_id(0) == 0)
        def _():
            dsc_ref[...] = jnp.zeros_like(dsc_ref)
            dsh_ref[...] = jnp.zeros_like(dsh_ref)
            dg_ref[...] = jnp.zeros_like(dg_ref)

        xv, dh = x_ref[...], dh_ref[...]
        gv = g_ref[...]
        rstd = lax.rsqrt(jnp.mean(xv * xv, axis=-1, keepdims=True) + EPS)
        xhat = xv * rstd
        dn = dh * (1.0 + sc_ref[...])
        dxhat = dn * gv
        dx_ref[...] = dres_ref[...] + rstd * (dxhat - xhat * jnp.mean(dxhat * xhat, axis=-1, keepdims=True))
        dsc_ref[...] += jnp.sum(dh * (xhat * gv), axis=0, keepdims=True)
        dsh_ref[...] += jnp.sum(dh, axis=0, keepdims=True)
        dg_ref[...] += jnp.sum(dn * xhat, axis=0, keepdims=True)

    vec_out = jax.ShapeDtypeStruct((1, D), F32)
    return pl.pallas_call(
        body, name=name, grid=(S // TR,),
        in_specs=[ROW_SPEC, ROW_SPEC, ROW_SPEC, VEC_SPEC, VEC_SPEC],
        out_specs=[ROW_SPEC, VEC_SPEC, VEC_SPEC, VEC_SPEC],
        out_shape=[jax.ShapeDtypeStruct((S, D), F32), vec_out, vec_out, vec_out],
        compiler_params=_cparams(("arbitrary",)),
    )(x, dh, dres, g, sc)


def gate_bwd(dx, branch, gate, name):
    def body(dx_ref, br_ref, gate_ref, o_ref, dgate_ref):
        @pl.when(pl.program_id(0) == 0)
        def _():
            dgate_ref[...] = jnp.zeros_like(dgate_ref)

        dxv = dx_ref[...]
        o_ref[...] = (dxv * gate_ref[...]).astype(o_ref.dtype)
        dgate_ref[...] += jnp.sum(dxv * br_ref[...], axis=0, keepdims=True)

    return pl.pallas_call(
        body, name=name, grid=(S // TR,),
        in_specs=[ROW_SPEC, ROW_SPEC, VEC_SPEC], out_specs=[ROW_SPEC, VEC_SPEC],
        out_shape=[jax.ShapeDtypeStruct((S, D), BF16), jax.ShapeDtypeStruct((1, D), F32)],
        compiler_params=_cparams(("arbitrary",)),
    )(dx, branch, gate)


def loss_head(x, target, g, name):
    def body(x_ref, t_ref, g_ref, dx_ref, loss_ref, dg_ref):
        @pl.when(pl.program_id(0) == 0)
        def _():
            loss_ref[...] = jnp.zeros_like(loss_ref)
            dg_ref[...] = jnp.zeros_like(dg_ref)

        xv, gv = x_ref[...], g_ref[...]
        rstd = lax.rsqrt(jnp.mean(xv * xv, axis=-1, keepdims=True) + EPS)
        xhat = xv * rstd
        err = xhat * gv - t_ref[...]
        loss_ref[...] += jnp.sum(err * err) * (0.5 / D)
        dy = err * (1.0 / D)
        dg_ref[...] += jnp.sum(dy * xhat, axis=0, keepdims=True)
        dxhat = dy * gv
        dx_ref[...] = rstd * (dxhat - xhat * jnp.mean(dxhat * xhat, axis=-1, keepdims=True))

    return pl.pallas_call(
        body, name=name, grid=(S // TR,),
        in_specs=[ROW_SPEC, ROW_SPEC, VEC_SPEC],
        out_specs=[ROW_SPEC, pl.BlockSpec((1, 128), lambda i: (0, 0)), VEC_SPEC],
        out_shape=[jax.ShapeDtypeStruct((S, D), F32), jax.ShapeDtypeStruct((1, 128), F32),
                   jax.ShapeDtypeStruct((1, D), F32)],
        compiler_params=_cparams(("arbitrary",)),
    )(x, target, g)


def _dot_hilo(a, tri):
    hi = a.astype(BF16)
    lo = (a - hi.astype(F32)).astype(BF16)
    return jnp.dot(hi, tri, preferred_element_type=F32) + jnp.dot(lo, tri, preferred_element_type=F32)


def _log_stay(z):
    return -(jnp.maximum(z, 0.0) + jnp.log(1.0 + jnp.exp(-jnp.abs(z))))


HEAD_Q_SPEC = pl.BlockSpec((None, T, HD), lambda h, i: (h, i, 0))
HEAD_ALL_SPEC = pl.BlockSpec((None, S, HD), lambda h, i: (h, 0, 0))
HEAD_R_SPEC = pl.BlockSpec((None, T, 1), lambda h, i: (h, i, 0))


def attn_fwd(q, k, v, name):
    scale = HD ** -0.5

    def body(q_ref, k_ref, v_ref, o_ref, r_ref):
        i = pl.program_id(1)
        qb = q_ref[...]
        row = lax.broadcasted_iota(jnp.int32, (T, T), 0)
        col = lax.broadcasted_iota(jnp.int32, (T, T), 1)
        upper = (row > col).astype(BF16)
        diag_mask = col < row

        def step(jj, carry):
            acc, later = carry
            start = pl.multiple_of((i - jj) * T, T)
            kb = k_ref[pl.ds(start, T), :]
            vb = v_ref[pl.ds(start, T), :]
            z = lax.dot_general(qb, kb, (((1,), (1,)), ((), ())), preferred_element_type=F32) * scale
            mask = jnp.logical_or(diag_mask, jj > 0)
            ls = _log_stay(z)
            lsm = jnp.where(mask, ls, 0.0)
            log_after = _dot_hilo(lsm, upper) + later
            w = jnp.where(mask, jnp.exp(z + ls + log_after), 0.0)
            acc = acc + jnp.dot(w.astype(BF16), vb, preferred_element_type=F32)
            later = later + jnp.sum(lsm, axis=1, keepdims=True)
            return acc, later

        acc, later = lax.fori_loop(0, i + 1, step, (jnp.zeros((T, HD), F32), jnp.zeros((T, 1), F32)))
        o_ref[...] = acc.astype(o_ref.dtype)
        r_ref[...] = later

    return pl.pallas_call(
        body, name=name, grid=(NH, S // T),
        in_specs=[HEAD_Q_SPEC, HEAD_ALL_SPEC, HEAD_ALL_SPEC],
        out_specs=[HEAD_Q_SPEC, HEAD_R_SPEC],
        out_shape=[jax.ShapeDtypeStruct((NH, S, HD), BF16), jax.ShapeDtypeStruct((NH, S, 1), F32)],
        compiler_params=_cparams(("parallel", "parallel")),
    )(q, k, v)


def attn_bwd(q, k, v, do, r, name):
    scale = HD ** -0.5

    def body(q_ref, k_ref, v_ref, do_ref, r_ref, dq_ref, dk_ref, dv_ref):
        i = pl.program_id(1)

        @pl.when(i == 0)
        def _():
            dk_ref[...] = jnp.zeros_like(dk_ref)
            dv_ref[...] = jnp.zeros_like(dv_ref)

        qb, dob, total = q_ref[...], do_ref[...], r_ref[...]
        row = lax.broadcasted_iota(jnp.int32, (T, T), 0)
        col = lax.broadcasted_iota(jnp.int32, (T, T), 1)
        incl = (row <= col).astype(BF16)
        excl = (row < col).astype(BF16)
        diag_mask = col < row

        def step(j, carry):
            dq, before, dbefore = carry
            start = pl.multiple_of(j * T, T)
            kb = k_ref[pl.ds(start, T), :]
            vb = v_ref[pl.ds(start, T), :]
            z = lax.dot_general(qb, kb, (((1,), (1,)), ((), ())), preferred_element_type=F32) * scale
            mask = jnp.logical_or(diag_mask, j < i)
            ls = _log_stay(z)
            lsm = jnp.where(mask, ls, 0.0)
            log_after = total - (_dot_hilo(lsm, incl) + before)
            log_beta = z + ls
            w = jnp.where(mask, jnp.exp(log_beta + log_after), 0.0)
            dw = lax.dot_general(dob, vb, (((1,), (1,)), ((), ())), preferred_element_type=F32)
            dl = dw * w
            dstay = _dot_hilo(dl, excl) + dbefore
            beta = jnp.where(mask, jnp.exp(log_beta), 0.0)
            dz = ((dl * (1.0 - beta) - beta * dstay) * scale).astype(BF16)
            dq = dq + jnp.dot(dz, kb, preferred_element_type=F32)
            dk_ref[pl.ds(start, T), :] += lax.dot_general(
                dz, qb, (((0,), (0,)), ((), ())), preferred_element_type=F32)
            dv_ref[pl.ds(start, T), :] += lax.dot_general(
                w.astype(BF16), dob, (((0,), (0,)), ((), ())), preferred_element_type=F32)
            before = before + jnp.sum(lsm, axis=1, keepdims=True)
            dbefore = dbefore + jnp.sum(dl, axis=1, keepdims=True)
            return dq, before, dbefore

        init = (jnp.zeros((T, HD), F32), jnp.zeros((T, 1), F32), jnp.zeros((T, 1), F32))
        dq, _, _ = lax.fori_loop(0, i + 1, step, init)
        dq_ref[...] = dq

    full = jax.ShapeDtypeStruct((NH, S, HD), F32)
    return pl.pallas_call(
        body, name=name, grid=(NH, S // T),
        in_specs=[HEAD_Q_SPEC, HEAD_ALL_SPEC, HEAD_ALL_SPEC, HEAD_Q_SPEC, HEAD_R_SPEC],
        out_specs=[HEAD_Q_SPEC, HEAD_ALL_SPEC, HEAD_ALL_SPEC],
        out_shape=[full, full, full],
        compiler_params=_cparams(("parallel", "arbitrary")),
    )(q, k, v, do, r)


CW = 256
LANES = 128


def _proj_cols(first_col):
    base = first_col // LANES
    return pl.BlockSpec((S, LANES), lambda j: (0, base + j))


CONV_OUT_SPEC = pl.BlockSpec((S, LANES), lambda j: (0, j))
CONV_W_SPEC = pl.BlockSpec((8, LANES), lambda j: (0, j))
CONV_B_SPEC = pl.BlockSpec((1, LANES), lambda j: (0, j))


def _shift_down(u, n):
    rows = lax.broadcasted_iota(jnp.int32, u.shape, 0)
    return jnp.where(rows >= n, pltpu.roll(u, n, 0), 0.0)


def _shift_up(u, n):
    rows = lax.broadcasted_iota(jnp.int32, u.shape, 0)
    return jnp.where(rows < S - n, pltpu.roll(u, S - n, 0), 0.0)


def conv_fwd(proj, cw8, cb, name):
    def body(bg_ref, cg_ref, hc_ref, w_ref, b_ref, o_ref):
        u = cg_ref[...] * hc_ref[...]
        w = w_ref[...]
        y = w[0:1, :] * _shift_down(u, 2) + w[1:2, :] * _shift_down(u, 1) + w[2:3, :] * u + b_ref[...]
        o_ref[...] = bg_ref[...] * y

    return pl.pallas_call(
        body, name=name, grid=(CW // LANES,),
        in_specs=[_proj_cols(1536), _proj_cols(1792), _proj_cols(2048), CONV_W_SPEC, CONV_B_SPEC],
        out_specs=CONV_OUT_SPEC, out_shape=jax.ShapeDtypeStruct((S, CW), F32),
        compiler_params=_cparams(("parallel",)),
    )(proj, proj, proj, cw8, cb)


def conv_bwd(proj, dout, cw8, cb, name):
    def body(bg_ref, cg_ref, hc_ref, do_ref, w_ref, b_ref, dbg_ref, dcg_ref, dhc_ref, dw_ref, db_ref):
        cg, hc, do = cg_ref[...], hc_ref[...], do_ref[...]
        w = w_ref[...]
        u = cg * hc
        u1, u2 = _shift_down(u, 1), _shift_down(u, 2)
        y = w[0:1, :] * u2 + w[1:2, :] * u1 + w[2:3, :] * u + b_ref[...]
        dbg_ref[...] = do * y
        dy = do * bg_ref[...]
        db_ref[...] = jnp.sum(dy, axis=0, keepdims=True)
        dw_ref[...] = jnp.concatenate(
            [jnp.sum(dy * u2, axis=0, keepdims=True), jnp.sum(dy * u1, axis=0, keepdims=True),
             jnp.sum(dy * u, axis=0, keepdims=True), jnp.zeros((5, LANES), F32)], axis=0)
        du = w[2:3, :] * dy + w[1:2, :] * _shift_up(dy, 1) + w[0:1, :] * _shift_up(dy, 2)
        dcg_ref[...] = du * hc
        dhc_ref[...] = du * cg

    full = jax.ShapeDtypeStruct((S, CW), F32)
    return pl.pallas_call(
        body, name=name, grid=(CW // LANES,),
        in_specs=[_proj_cols(1536), _proj_cols(1792), _proj_cols(2048), CONV_OUT_SPEC, CONV_W_SPEC, CONV_B_SPEC],
        out_specs=[CONV_OUT_SPEC, CONV_OUT_SPEC, CONV_OUT_SPEC, CONV_W_SPEC, CONV_B_SPEC],
        out_shape=[full, full, full, jax.ShapeDtypeStruct((8, CW), F32), jax.ShapeDtypeStruct((1, CW), F32)],
        compiler_params=_cparams(("parallel",)),
    )(proj, proj, proj, dout, cw8, cb)


GELU_K = math.sqrt(2.0 / math.pi)
GELU_C = 0.044715


def _gelu(x):
    return 0.5 * x * (1.0 + jnp.tanh(GELU_K * (x + GELU_C * (x * x * x))))


def _gelu_grad(x):
    t = jnp.tanh(GELU_K * (x + GELU_C * (x * x * x)))
    return 0.5 * (1.0 + t) + 0.5 * x * (1.0 - t * t) * (GELU_K * (1.0 + 3.0 * GELU_C * (x * x)))


def _sg_masks():
    row = lax.broadcasted_iota(jnp.int32, (T, T), 0)
    col = lax.broadcasted_iota(jnp.int32, (T, T), 1)
    causal = jnp.right_shift(row, 6) >= jnp.right_shift(col, 6)
    head_of_col = jnp.right_shift(lax.broadcasted_iota(jnp.int32, (T, CW), 1), 6)
    return causal, head_of_col


def _sg_mixed(vnb, sw_ref, bias, causal, head_of_col):
    mixed = bias
    for h in range(SG_HEADS):
        wh = jnp.where(causal, sw_ref[h], 0.0).astype(BF16)
        mh = jnp.dot(wh, vnb, preferred_element_type=F32)
        mixed = mixed + jnp.where(head_of_col == h, mh, 0.0)
    return mixed


SG_U_SPEC = pl.BlockSpec((T, CW), lambda n: (n, 2304 // CW))
SG_V_SPEC = pl.BlockSpec((T, CW), lambda n: (n, 2560 // CW))
SG_ROW_SPEC = pl.BlockSpec((T, CW), lambda n: (n, 0))
SG_G_SPEC = pl.BlockSpec((1, CW), lambda n: (0, 0))
SG_W_SPEC = pl.BlockSpec((SG_HEADS, T, T), lambda n: (0, 0, 0))
SG_BIAS_SPEC = pl.BlockSpec((T, CW), lambda n: (0, 0))


def sg_fwd(proj, gn, sw, bias, name):
    def body(u_ref, v_ref, g_ref, sw_ref, bias_ref, o_ref):
        causal, head_of_col = _sg_masks()
        gv = _gelu(v_ref[...])
        rstd = lax.rsqrt(jnp.mean(gv * gv, axis=-1, keepdims=True) + EPS)
        vnb = ((gv * rstd) * g_ref[...]).astype(BF16)
        mixed = _sg_mixed(vnb, sw_ref, bias_ref[...], causal, head_of_col)
        o_ref[...] = _gelu(u_ref[...]) * mixed

    return pl.pallas_call(
        body, name=name, grid=(S // T,),
        in_specs=[SG_U_SPEC, SG_V_SPEC, SG_G_SPEC, SG_W_SPEC, SG_BIAS_SPEC],
        out_specs=SG_ROW_SPEC, out_shape=jax.ShapeDtypeStruct((S, CW), F32),
        compiler_params=_cparams(("parallel",)),
    )(proj, proj, gn, sw, bias)


def sg_bwd(proj, dout, gn, sw, bias, name):
    def body(u_ref, v_ref, do_ref, g_ref, sw_ref, bias_ref, du_ref, dv_ref, dg_ref, dsw_ref, dbias_ref):
        @pl.when(pl.program_id(0) == 0)
        def _():
            dg_ref[...] = jnp.zeros_like(dg_ref)
            dsw_ref[...] = jnp.zeros_like(dsw_ref)
            dbias_ref[...] = jnp.zeros_like(dbias_ref)

        causal, head_of_col = _sg_masks()
        uv, vv, do, gnv = u_ref[...], v_ref[...], do_ref[...], g_ref[...]
        gv = _gelu(vv)
        rstd = lax.rsqrt(jnp.mean(gv * gv, axis=-1, keepdims=True) + EPS)
        xhat = gv * rstd
        vnb = (xhat * gnv).astype(BF16)
        mixed = _sg_mixed(vnb, sw_ref, bias_ref[...], causal, head_of_col)
        du_ref[...] = (do * mixed) * _gelu_grad(uv)
        dmix = do * _gelu(uv)
        dbias_ref[...] += dmix
        dmixb = dmix.astype(BF16)
        dvn = jnp.zeros((T, CW), F32)
        for h in range(SG_HEADS):
            wh = jnp.where(causal, sw_ref[h], 0.0).astype(BF16)
            dvh = lax.dot_general(wh, dmixb, (((0,), (0,)), ((), ())), preferred_element_type=F32)
            dvn = dvn + jnp.where(head_of_col == h, dvh, 0.0)
            dmh = jnp.where(head_of_col == h, dmixb, jnp.zeros_like(dmixb))
            dwh = lax.dot_general(dmh, vnb, (((1,), (1,)), ((), ())), preferred_element_type=F32)
            dsw_ref[h] += jnp.where(causal, dwh, 0.0)
        dg_ref[...] += jnp.sum(dvn * xhat, axis=0, keepdims=True)
        dxhat = dvn * gnv
        dgv = rstd * (dxhat - xhat * jnp.mean(dxhat * xhat, axis=-1, keepdims=True))
        dv_ref[...] = dgv * _gelu_grad(vv)

    full = jax.ShapeDtypeStruct((S, CW), F32)
    return pl.pallas_call(
        body, name=name, grid=(S // T,),
        in_specs=[SG_U_SPEC, SG_V_SPEC, SG_ROW_SPEC, SG_G_SPEC, SG_W_SPEC, SG_BIAS_SPEC],
        out_specs=[SG_ROW_SPEC, SG_ROW_SPEC, SG_G_SPEC, SG_W_SPEC, SG_BIAS_SPEC],
        out_shape=[full, full, jax.ShapeDtypeStruct((1, CW), F32),
                   jax.ShapeDtypeStruct((SG_HEADS, T, T), F32), jax.ShapeDtypeStruct((T, CW), F32)],
        compiler_params=_cparams(("arbitrary",)),
    )(proj, proj, dout, gn, sw, bias)


ADA_COLS = NMOD * D // NDEV


def ada_fwd(c_all, ada_w, ada_b_mine, name):
    def body(c_ref, w_ref, b_ref, o_ref, ca_ref):
        cv = c_ref[...]
        ca = cv * (1.0 / (1.0 + jnp.exp(-cv)))
        ca_ref[...] = ca
        cab = ca.astype(BF16)
        for l in range(L):
            o_ref[l] = jnp.dot(cab, w_ref[l].astype(BF16), preferred_element_type=F32) + b_ref[l]

    return pl.pallas_call(
        body, name=name,
        out_shape=[jax.ShapeDtypeStruct((L, NDEV, ADA_COLS), F32), jax.ShapeDtypeStruct((NDEV, D), F32)],
        compiler_params=_cparams(),
    )(c_all, ada_w, ada_b_mine)


def ada_bwd(ca, dmod_cols, name):
    def body(ca_ref, dm_ref, o_ref):
        cab = ca_ref[...].astype(BF16)
        for l in range(L):
            o_ref[l] = lax.dot_general(cab, dm_ref[l].astype(BF16), (((0,), (0,)), ((), ())),
                                       preferred_element_type=F32)

    return pl.pallas_call(
        body, name=name, out_shape=jax.ShapeDtypeStruct((L, D, ADA_COLS), F32),
        compiler_params=_cparams(),
    )(ca, dmod_cols)


def _adamw(w, g, m, v):
    m = B1 * m + (1.0 - B1) * g
    v = B2 * v + (1.0 - B2) * (g * g)
    m_hat = m / BC1
    v_hat = v / BC2
    delta = -LR * (m_hat / (jnp.sqrt(v_hat) + AEPS) + WD * w)
    return delta, m, v


def sum_gathered(parts, name):
    _, rows, cols = parts.shape

    def body(p_ref, o_ref):
        acc = p_ref[0]
        for d in range(1, NDEV):
            acc = acc + p_ref[d]
        o_ref[...] = acc

    return pl.pallas_call(
        body, name=name, out_shape=jax.ShapeDtypeStruct((rows, cols), F32),
        compiler_params=_cparams(),
    )(parts)


def adamw_plain(w, g, m, v, tr, name):
    rows, cols = w.shape
    spec = pl.BlockSpec((tr, cols), lambda i: (i, 0))

    def body(w_ref, g_ref, m_ref, v_ref, d_ref, nm_ref, nv_ref):
        delta, nm, nv = _adamw(w_ref[...], g_ref[...], m_ref[...], v_ref[...])
        d_ref[...] = delta
        nm_ref[...] = nm
        nv_ref[...] = nv

    shp = jax.ShapeDtypeStruct((rows, cols), F32)
    return pl.pallas_call(
        body, name=name, grid=(rows // tr,), in_specs=[spec] * 4, out_specs=[spec] * 3,
        out_shape=[shp, shp, shp], compiler_params=_cparams(("parallel",)),
    )(w, g, m, v)


def adamw_reduce(w, parts, m, v, tr, name):
    _, rows, cols = w.shape
    spec = pl.BlockSpec((None, tr, cols), lambda l, i: (l, i, 0))
    pspec = pl.BlockSpec((NDEV, None, tr, cols), lambda l, i: (0, l, i, 0))

    def body(w_ref, p_ref, m_ref, v_ref, g_ref, d_ref, nm_ref, nv_ref):
        g = p_ref[0].astype(F32)
        for d in range(1, NDEV):
            g = g + p_ref[d].astype(F32)
        delta, nm, nv = _adamw(w_ref[...], g, m_ref[...], v_ref[...])
        g_ref[...] = g
        d_ref[...] = delta
        nm_ref[...] = nm
        nv_ref[...] = nv

    shp = jax.ShapeDtypeStruct(w.shape, F32)
    return pl.pallas_call(
        body, name=name, grid=(L, rows // tr), in_specs=[spec, pspec, spec, spec], out_specs=[spec] * 4,
        out_shape=[shp] * 4, compiler_params=_cparams(("parallel", "parallel")),
    )(w, parts, m, v)


def _to_heads(a):
    return a.astype(BF16).reshape(S, NH, HD).transpose(1, 0, 2)


def _from_heads(a):
    return a.transpose(1, 0, 2).reshape(S, NH * HD)


def _pad_rows(flat, rows):
    return jnp.pad(flat, (0, rows * LANES - flat.shape[0])).reshape(rows, LANES)


def kernel(x, c, ada_w, ada_b, norm_mix_g, norm_mlp_g, w_in, conv_w, conv_b, gmlp_norm_g, spatial_w, spatial_b, w_out, mlp_w1, mlp_w2, final_norm_g, loss_target, m_ada_w, m_ada_b, m_norm_mix_g, m_norm_mlp_g, m_w_in, m_conv_w, m_conv_b, m_gmlp_norm_g, m_spatial_w, m_spatial_b, m_w_out, m_mlp_w1, m_mlp_w2, m_final_norm_g, v_ada_w, v_ada_b, v_norm_mix_g, v_norm_mlp_g, v_w_in, v_conv_w, v_conv_b, v_gmlp_norm_g, v_spatial_w, v_spatial_b, v_w_out, v_mlp_w1, v_mlp_w2, v_final_norm_g):
    me = _lin(_my_pos())
    x0 = x[0]
    target = loss_target[0]
    conv_shard = conv_w.shape[-1]

    pack0 = _pad_rows(jnp.concatenate([c.reshape(-1), conv_w.reshape(-1)]), 16)
    g0 = all_gather([pack0], "gather_c_conv")[0].reshape(NDEV, 16 * LANES)
    c_all = g0[:, :D]
    conv_full = (g0[:, D:D + L * 3 * conv_shard].reshape(NDEV, L, 3, conv_shard)
                 .transpose(1, 2, 0, 3).reshape(L, 3, CW))
    gw_in, gw_out, gw1, gw2 = all_gather(
        [w_in.astype(BF16), w_out.astype(BF16), mlp_w1.astype(BF16), mlp_w2.astype(BF16)], "gather_weights")
    W_in = gw_in.transpose(1, 2, 0, 3).reshape(L, D, PROJ)
    W_inT = gw_in.transpose(1, 0, 3, 2).reshape(L, PROJ, D)
    W_out = gw_out.transpose(1, 0, 2, 3).reshape(L, D, D)
    W_outT = gw_out.transpose(1, 3, 0, 2).reshape(L, D, D)
    W1 = gw1.transpose(1, 2, 0, 3).reshape(L, D, DFF)
    W1T = gw1.transpose(1, 0, 3, 2).reshape(L, DFF, D)
    W2 = gw2.transpose(1, 0, 2, 3).reshape(L, DFF, D)
    W2T = gw2.transpose(1, 3, 0, 2).reshape(L, D, DFF)

    ada_b_mine = lax.dynamic_slice(ada_b, (0, me * ADA_COLS), (L, ADA_COLS)).reshape(L, 1, ADA_COLS)
    mod_part, c_act = ada_fwd(c_all, ada_w, ada_b_mine, "ada_fwd")
    gmod = all_gather([mod_part], "gather_mod")[0]
    mod = lax.dynamic_index_in_dim(gmod, me, axis=2, keepdims=False)
    mod = mod.transpose(1, 0, 2).reshape(L, NMOD, 1, D)

    cw8 = jnp.pad(conv_full, ((0, 0), (0, 5), (0, 0)))
    sg_bias = jnp.repeat(spatial_b.transpose(0, 2, 1), HD, axis=2)

    saved = []
    xl = x0
    for l in range(L):
        sh_m, sc_m, g_m, sh_f, sc_f, g_f = [mod[l, k] for k in range(NMOD)]
        h1 = normmod_fwd(xl, norm_mix_g[l:l + 1], sc_m, sh_m, f"norm_mix_fwd{l}")
        proj = mm(h1, W_in, l=l, tm=512, tn=PROJ // 2, out_dtypes=[F32], name=f"proj{l}")[0]
        qh, kh, vh = [_to_heads(proj[:, k * 512:(k + 1) * 512]) for k in range(3)]
        a_heads, a_tot = attn_fwd(qh, kh, vh, f"attn_fwd{l}")
        c_out = conv_fwd(proj, cw8[l], conv_b[l:l + 1], f"conv_fwd{l}")
        s_out = sg_fwd(proj, gmlp_norm_g[l:l + 1], spatial_w[l], sg_bias[l], f"sg_fwd{l}")
        cat = jnp.concatenate([_from_heads(a_heads), c_out.astype(BF16), s_out.astype(BF16)], axis=1)
        mix, x1 = mm(cat, W_out, l=l, tm=512, tn=512, out_dtypes=[F32, F32],
                     epilogue=lambda acc, xr, g: (acc, xr + g * acc),
                     extras=[(xl, "tile"), (g_m, "col")], name=f"mix{l}")
        h2 = normmod_fwd(x1, norm_mlp_g[l:l + 1], sc_f, sh_f, f"norm_mlp_fwd{l}")
        a, r = mm(h2, W1, l=l, tm=512, tn=1024, out_dtypes=[F32, BF16],
                  epilogue=lambda acc: (acc, jnp.square(jnp.maximum(acc, 0.0))), name=f"mlp_up{l}")
        m2, x2 = mm(r, W2, l=l, tm=512, tn=512, out_dtypes=[F32, F32],
                    epilogue=lambda acc, xr, g: (acc, xr + g * acc),
                    extras=[(x1, "tile"), (g_f, "col")], name=f"mlp_down{l}")
        saved.append(dict(x=xl, h1=h1, proj=proj, qh=qh, kh=kh, vh=vh, a_tot=a_tot, cat=cat, mix=mix,
                          x1=x1, h2=h2, a=a, r=r, m2=m2))
        xl = x2

    dx, loss_part, d_final_g = loss_head(xl, target, final_norm_g.reshape(1, D), "loss_head")

    dmod = [None] * L
    dW_in, dW_out, dW1, dW2 = [None] * L, [None] * L, [None] * L, [None] * L
    d_norm_mix, d_norm_mlp, d_conv_w, d_conv_b = [None] * L, [None] * L, [None] * L, [None] * L
    d_gn, d_sw, d_sb = [None] * L, [None] * L, [None] * L
    for l in reversed(range(L)):
        sv = saved[l]
        sh_m, sc_m, g_m, sh_f, sc_f, g_f = [mod[l, k] for k in range(NMOD)]
        dm2, dg_f = gate_bwd(dx, sv["m2"], g_f, f"gate_mlp_bwd{l}")
        da = mm(dm2, W2T, l=l, tm=512, tn=1024, out_dtypes=[BF16],
                epilogue=lambda acc, av: (acc * (2.0 * jnp.maximum(av, 0.0)),),
                extras=[(sv["a"], "tile")], name=f"mlp_down_dgrad{l}")[0]
        dW2[l] = mm(sv["r"], dm2, tm=512, tn=1024, out_dtypes=[BF16], trans_a=True, name=f"mlp_down_wgrad{l}")[0]
        dW1[l] = mm(sv["h2"], da, tm=512, tn=1024, out_dtypes=[BF16], trans_a=True, name=f"mlp_up_wgrad{l}")[0]
        dh2 = mm(da, W1T, l=l, tm=512, tn=512, out_dtypes=[F32], name=f"mlp_up_dgrad{l}")[0]
        dx1, dsc_f, dsh_f, d_norm_mlp[l] = normmod_bwd(sv["x1"], dh2, dx, norm_mlp_g[l:l + 1], sc_f,
                                                       f"norm_mlp_bwd{l}")
        dmix, dg_m = gate_bwd(dx1, sv["mix"], g_m, f"gate_mix_bwd{l}")
        dcat = mm(dmix, W_outT, l=l, tm=512, tn=512, out_dtypes=[F32], name=f"mix_dgrad{l}")[0]
        dW_out[l] = mm(sv["cat"], dmix, tm=512, tn=1024, out_dtypes=[BF16], trans_a=True, name=f"mix_wgrad{l}")[0]
        doh = _to_heads(dcat[:, :512])
        dqh, dkh, dvh = attn_bwd(sv["qh"], sv["kh"], sv["vh"], doh, sv["a_tot"], f"attn_bwd{l}")
        dbg, dcg, dhc, dcw8, d_conv_b[l] = conv_bwd(sv["proj"], dcat[:, 512:768], cw8[l], conv_b[l:l + 1],
                                                    f"conv_bwd{l}")
        d_conv_w[l] = dcw8[:3]
        dus, dvs, d_gn[l], dsw, dbias = sg_bwd(sv["proj"], dcat[:, 768:1024], gmlp_norm_g[l:l + 1],
                                               spatial_w[l], sg_bias[l], f"sg_bwd{l}")
        d_sw[l] = dsw
        d_sb[l] = dbias.reshape(T, SG_HEADS, HD).sum(axis=2).T
        dproj = jnp.concatenate([_from_heads(dqh), _from_heads(dkh), _from_heads(dvh),
                                 dbg, dcg, dhc, dus, dvs], axis=1).astype(BF16)
        dW_in[l] = mm(sv["h1"], dproj, tm=512, tn=PROJ // 2, out_dtypes=[BF16], trans_a=True,
                      name=f"proj_wgrad{l}")[0]
        dh1 = mm(dproj, W_inT, l=l, tm=512, tn=512, out_dtypes=[F32], name=f"proj_dgrad{l}")[0]
        dx, dsc_m, dsh_m, d_norm_mix[l] = normmod_bwd(sv["x"], dh1, dx1, norm_mix_g[l:l + 1], sc_m,
                                                      f"norm_mix_bwd{l}")
        dmod[l] = jnp.concatenate([dsh_m, dsc_m, dg_m, dsh_f, dsc_f, dg_f], axis=1)

    grad_x = dx.reshape(1, S, D)

    small_parts = [jnp.concatenate(dmod, axis=0), jnp.concatenate(d_norm_mix, axis=0),
                   jnp.concatenate(d_norm_mlp, axis=0), jnp.stack(d_conv_w), jnp.concatenate(d_conv_b, axis=0),
                   jnp.concatenate(d_gn, axis=0), jnp.stack(d_sw), jnp.stack(d_sb), d_final_g, loss_part[:, :1]]
    sizes = [p.size for p in small_parts]
    small_rows = -(-sum(sizes) // (8 * LANES)) * 8
    small_pack = _pad_rows(jnp.concatenate([p.reshape(-1) for p in small_parts]), small_rows)
    small_all = all_gather([small_pack], "gather_small_grads")[0]
    small_sum = sum_gathered(small_all, "sum_small_grads").reshape(-1)
    offs = [0]
    for sz in sizes:
        offs.append(offs[-1] + sz)
    summed = [small_sum[offs[k]:offs[k + 1]].reshape(small_parts[k].shape) for k in range(len(sizes))]
    (g_ada_b, g_norm_mix, g_norm_mlp, g_conv_w_full, g_conv_b, g_gn, g_sw, g_sb, g_final, loss_sum) = summed
    loss = loss_sum.reshape(())
    g_final = g_final.reshape(D)
    g_conv_w = lax.dynamic_slice(g_conv_w_full, (0, 0, me * conv_shard), (L, 3, conv_shard))

    dmod_all = small_all.reshape(NDEV, -1)[:, :L * NMOD * D].reshape(NDEV, L, NMOD * D)
    dmod_cols = lax.dynamic_slice(dmod_all, (0, 0, me * ADA_COLS), (NDEV, L, ADA_COLS)).transpose(1, 0, 2)
    g_ada_w = ada_bwd(c_act, dmod_cols, "ada_bwd")

    def col_pieces(gs, shard):
        st = jnp.stack(gs)
        return st.reshape(L, st.shape[1], NDEV, shard).transpose(2, 0, 1, 3)

    def row_pieces(gs, shard):
        st = jnp.stack(gs)
        return st.reshape(L, NDEV, shard, st.shape[2]).transpose(1, 0, 2, 3)

    p_in, p_out, p1, p2 = all_to_all(
        [col_pieces(dW_in, PROJ // NDEV), row_pieces(dW_out, D // NDEV),
         col_pieces(dW1, DFF // NDEV), row_pieces(dW2, DFF // NDEV)], "exchange_grads")

    g_w_in, d_w_in, nm_w_in, nv_w_in = adamw_reduce(w_in, p_in, m_w_in, v_w_in, 256, "adamw_w_in")
    g_w_out, d_w_out, nm_w_out, nv_w_out = adamw_reduce(w_out, p_out, m_w_out, v_w_out, 128, "adamw_w_out")
    g_w1, d_w1, nm_w1, nv_w1 = adamw_reduce(mlp_w1, p1, m_mlp_w1, v_mlp_w1, 256, "adamw_mlp_w1")
    g_w2, d_w2, nm_w2, nv_w2 = adamw_reduce(mlp_w2, p2, m_mlp_w2, v_mlp_w2, 256, "adamw_mlp_w2")

    flat2 = lambda t: t.reshape(L * D, ADA_COLS)
    d_ada_w, nm_ada_w, nv_ada_w = [t.reshape(L, D, ADA_COLS) for t in adamw_plain(
        flat2(ada_w), flat2(g_ada_w), flat2(m_ada_w), flat2(v_ada_w), 256, "adamw_ada_w")]

    small_w = [ada_b, norm_mix_g, norm_mlp_g, conv_w, conv_b, gmlp_norm_g, spatial_w, spatial_b, final_norm_g]
    small_m = [m_ada_b, m_norm_mix_g, m_norm_mlp_g, m_conv_w, m_conv_b, m_gmlp_norm_g, m_spatial_w, m_spatial_b,
               m_final_norm_g]
    small_v = [v_ada_b, v_norm_mix_g, v_norm_mlp_g, v_conv_w, v_conv_b, v_gmlp_norm_g, v_spatial_w, v_spatial_b,
               v_final_norm_g]
    small_g = [g_ada_b, g_norm_mix, g_norm_mlp, g_conv_w, g_conv_b, g_gn, g_sw, g_sb, g_final]
    wsizes = [p.size for p in small_w]
    wrows = -(-sum(wsizes) // (8 * LANES)) * 8
    pack = lambda ps: _pad_rows(jnp.concatenate([p.reshape(-1) for p in ps]), wrows)
    sd, snm, snv = adamw_plain(pack(small_w), pack(small_g), pack(small_m), pack(small_v), wrows, "adamw_small")
    woffs = [0]
    for sz in wsizes:
        woffs.append(woffs[-1] + sz)

    def unpack(flat):
        flat = flat.reshape(-1)
        return [flat[woffs[k]:woffs[k + 1]].reshape(small_w[k].shape) for k in range(len(small_w))]

    sd, snm, snv = unpack(sd), unpack(snm), unpack(snv)

    def ordered(big, small):
        ada, win, wout, w1, w2 = big
        return [ada, small[0], small[1], small[2], win, small[3], small[4], small[5], small[6], small[7],
                wout, w1, w2, small[8]]

    grads = ordered([g_ada_w, g_w_in, g_w_out, g_w1, g_w2], small_g)
    deltas = ordered([d_ada_w, d_w_in, d_w_out, d_w1, d_w2], sd)
    new_m = ordered([nm_ada_w, nm_w_in, nm_w_out, nm_w1, nm_w2], snm)
    new_v = ordered([nv_ada_w, nv_w_in, nv_w_out, nv_w1, nv_w2], snv)
    return (loss, grad_x, *grads, *deltas, *new_m, *new_v)
```

```python
import functools
import math

import jax
import jax.numpy as jnp
from jax import lax
from jax.experimental import pallas as pl
from jax.experimental.pallas import tpu as pltpu

F32 = jnp.float32
BF16 = jnp.bfloat16
MESH = pl.DeviceIdType.MESH

S = 2048
D = 1024
L = 2
NDEV = 8
HD = 64
NH = 8
PROJ = 2816
DFF = 4096
NMOD = 6
EPS = 1e-6
T = 128
SG_HEADS = 4
LANES = 128
CW = 256
QKV = 3 * NH * HD
REST = PROJ - QKV

LR, B1, B2, AEPS, WD, STEP = 0.001, 0.9, 0.999, 1e-08, 0.01, 10
BC1 = 1.0 - B1 ** STEP
BC2 = 1.0 - B2 ** STEP

VMEM_LIMIT = 48 * 1024 * 1024

HBM_SPEC = pl.BlockSpec(memory_space=pltpu.HBM)


def _cparams(sem=None):
    return pltpu.CompilerParams(dimension_semantics=sem, vmem_limit_bytes=VMEM_LIMIT)


def _my_pos():
    return lax.axis_index("x"), lax.axis_index("y"), lax.axis_index("c")


def _lin(p):
    return 4 * p[0] + 2 * p[1] + p[2]


def all_gather(arrs, name):
    n = len(arrs)

    def body(*refs):
        ins, outs = refs[:n], refs[n:2 * n]
        send_sems, recv_sems, local_sems = refs[2 * n:]
        x, y, c = _my_pos()
        me, sibling = (x, y, c), (x, y, 1 - c)
        chips = [(1 - x, y), (x, 1 - y), (1 - x, 1 - y)]

        def copy(a, k, block, to, src=None):
            slot = outs[a].at[_lin(block)]
            return pltpu.make_async_remote_copy(
                src_ref=slot if src is None else src, dst_ref=slot,
                send_sem=send_sems.at[a, k], recv_sem=recv_sems.at[a, k],
                device_id=to, device_id_type=MESH)

        mine = [pltpu.make_async_copy(ins[a], outs[a].at[_lin(me)], local_sems.at[a]) for a in range(n)]
        for cp in mine:
            cp.start()
        first = []
        for a in range(n):
            first.append(copy(a, 0, me, sibling, src=ins[a]))
            first += [copy(a, 1 + j, me, (*chip, c), src=ins[a]) for j, chip in enumerate(chips)]
        for cp in first:
            cp.start()
        passed = []
        for j, chip in enumerate(chips):
            for a in range(n):
                copy(a, 1 + j, (*chip, c), me).wait_recv()
                fwd = copy(a, 4 + j, (*chip, c), sibling)
                fwd.start()
                passed.append(fwd)
        for a in range(n):
            copy(a, 0, sibling, me).wait_recv()
        for j, chip in enumerate(chips):
            for a in range(n):
                copy(a, 4 + j, (*chip, 1 - c), me).wait_recv()
        for cp in first + passed:
            cp.wait_send()
        for cp in mine:
            cp.wait()

    outs = pl.pallas_call(
        body, name=name,
        out_shape=[jax.ShapeDtypeStruct((NDEV,) + a.shape, a.dtype) for a in arrs],
        in_specs=[HBM_SPEC] * n, out_specs=[HBM_SPEC] * n,
        scratch_shapes=[pltpu.SemaphoreType.DMA((n, 7)), pltpu.SemaphoreType.DMA((n, 7)),
                        pltpu.SemaphoreType.DMA((n,))],
    )(*arrs)
    return list(outs)


def all_to_all(arrs, name):
    n = len(arrs)

    def body(*refs):
        ins, outs = refs[:n], refs[n:2 * n]
        send_sems, recv_sems, local_sems = refs[2 * n:]
        x, y, c = _my_pos()
        me = (x, y, c)

        def peer(mask):
            return (1 - x if mask & 4 else x, 1 - y if mask & 2 else y, 1 - c if mask & 1 else c)

        def copy(a, mask):
            return pltpu.make_async_remote_copy(
                src_ref=ins[a].at[_lin(peer(mask))], dst_ref=outs[a].at[_lin(me)],
                send_sem=send_sems.at[a, mask - 1], recv_sem=recv_sems.at[a, mask - 1],
                device_id=peer(mask), device_id_type=MESH)

        def arrival(a, mask):
            return pltpu.make_async_remote_copy(
                src_ref=ins[a].at[_lin(me)], dst_ref=outs[a].at[_lin(peer(mask))],
                send_sem=send_sems.at[a, mask - 1], recv_sem=recv_sems.at[a, mask - 1],
                device_id=peer(mask), device_id_type=MESH)

        mine = [pltpu.make_async_copy(ins[a].at[_lin(me)], outs[a].at[_lin(me)], local_sems.at[a])
                for a in range(n)]
        for cp in mine:
            cp.start()
        sends = [copy(a, mask) for mask in (4, 2, 6, 1, 5, 3, 7) for a in range(n)]
        for cp in sends:
            cp.start()
        for mask in range(1, 8):
            for a in range(n):
                arrival(a, mask).wait_recv()
        for cp in sends:
            cp.wait_send()
        for cp in mine:
            cp.wait()

    outs = pl.pallas_call(
        body, name=name,
        out_shape=[jax.ShapeDtypeStruct(a.shape, a.dtype) for a in arrs],
        in_specs=[HBM_SPEC] * n, out_specs=[HBM_SPEC] * n,
        scratch_shapes=[pltpu.SemaphoreType.DMA((n, 7)), pltpu.SemaphoreType.DMA((n, 7)),
                        pltpu.SemaphoreType.DMA((n,))],
    )(*arrs)
    return list(outs)


def mm(a, b, *, l=None, tm, tn, out_dtypes, epilogue=None, extras=(), name, trans_a=False, trans_b=False,
       cols=None):
    if trans_a:
        kdim, m = a.shape
    else:
        m, kdim = a.shape
    first, ncols = cols if cols is not None else (0, b.shape[-2] if trans_b else b.shape[-1])
    assert b.shape[-1 if trans_b else -2] == kdim and m % tm == 0 and ncols % tn == 0 and first % tn == 0
    j0 = first // tn
    if trans_a:
        a_spec = pl.BlockSpec((kdim, tm), lambda i, j: (0, i))
    else:
        a_spec = pl.BlockSpec((tm, kdim), lambda i, j: (i, 0))
    if trans_b:
        assert b.ndim == 3
        b_spec = pl.BlockSpec((None, tn, kdim), lambda i, j: (l, j0 + j, 0))
    elif b.ndim == 3:
        b_spec = pl.BlockSpec((None, kdim, tn), lambda i, j: (l, 0, j0 + j))
    else:
        b_spec = pl.BlockSpec((kdim, tn), lambda i, j: (0, j0 + j))
    ex_specs = []
    for arr, kind in extras:
        if kind == "tile":
            ex_specs.append(pl.BlockSpec((tm, tn), lambda i, j: (i, j)))
        else:
            ex_specs.append(pl.BlockSpec((1, tn), lambda i, j: (0, j)))
    n_ex, n_out = len(extras), len(out_dtypes)

    def body(a_ref, b_ref, *rest):
        ex_refs, out_refs = rest[:n_ex], rest[n_ex:]
        if trans_a:
            acc = lax.dot_general(a_ref[...], b_ref[...], (((0,), (0,)), ((), ())),
                                  preferred_element_type=F32)
        elif trans_b:
            acc = lax.dot_general(a_ref[...], b_ref[...], (((1,), (1,)), ((), ())),
                                  preferred_element_type=F32)
        else:
            acc = jnp.dot(a_ref[...], b_ref[...], preferred_element_type=F32)
        outs = (acc,) if epilogue is None else epilogue(acc, *[r[...] for r in ex_refs])
        for o_ref, val in zip(out_refs, outs):
            o_ref[...] = val.astype(o_ref.dtype)

    outs = pl.pallas_call(
        body, name=name, grid=(m // tm, ncols // tn),
        in_specs=[a_spec, b_spec] + ex_specs,
        out_specs=[pl.BlockSpec((tm, tn), lambda i, j: (i, j)) for _ in range(n_out)],
        out_shape=[jax.ShapeDtypeStruct((m, ncols), dt) for dt in out_dtypes],
        compiler_params=_cparams(("parallel", "parallel")),
    )(a, b, *[arr for arr, _ in extras])
    return list(outs)


TR = 256

ROW_SPEC = pl.BlockSpec((TR, D), lambda i: (i, 0))
VEC_SPEC = pl.BlockSpec((1, D), lambda i: (0, 0))


def normmod_fwd(x, g, sc, sh, name):
    def body(x_ref, g_ref, sc_ref, sh_ref, o_ref):
        xv = x_ref[...]
        rstd = lax.rsqrt(jnp.mean(xv * xv, axis=-1, keepdims=True) + EPS)
        n = (xv * rstd) * g_ref[...]
        o_ref[...] = (n * (1.0 + sc_ref[...]) + sh_ref[...]).astype(o_ref.dtype)

    return pl.pallas_call(
        body, name=name, grid=(S // TR,),
        in_specs=[ROW_SPEC, VEC_SPEC, VEC_SPEC, VEC_SPEC], out_specs=ROW_SPEC,
        out_shape=jax.ShapeDtypeStruct((S, D), BF16),
        compiler_params=_cparams(("parallel",)),
    )(x, g, sc, sh)


def normmod_bwd(x, dh, dres, g, sc, name):
    def body(x_ref, dh_ref, dres_ref, g_ref, sc_ref, dx_ref, dsc_ref, dsh_ref, dg_ref):
        @pl.when(pl.program_id(0) == 0)
        def _():
            dsc_ref[...] = jnp.zeros_like(dsc_ref)
            dsh_ref[...] = jnp.zeros_like(dsh_ref)
            dg_ref[...] = jnp.zeros_like(dg_ref)

        xv, dh = x_ref[...], dh_ref[...]
        gv = g_ref[...]
        rstd = lax.rsqrt(jnp.mean(xv * xv, axis=-1, keepdims=True) + EPS)
        xhat = xv * rstd
        dn = dh * (1.0 + sc_ref[...])
        dxhat = dn * gv
        dx_ref[...] = dres_ref[...] + rstd * (dxhat - xhat * jnp.mean(dxhat * xhat, axis=-1, keepdims=True))
        dsc_ref[...] += jnp.sum(dh * (xhat * gv), axis=0, keepdims=True)
        dsh_ref[...] += jnp.sum(dh, axis=0, keepdims=True)
        dg_ref[...] += jnp.sum(dn * xhat, axis=0, keepdims=True)

    vec_out = jax.ShapeDtypeStruct((1, D), F32)
    return pl.pallas_call(
        body, name=name, grid=(S // TR,),
        in_specs=[ROW_SPEC, ROW_SPEC, ROW_SPEC, VEC_SPEC, VEC_SPEC],
        out_specs=[ROW_SPEC, VEC_SPEC, VEC_SPEC, VEC_SPEC],
        out_shape=[jax.ShapeDtypeStruct((S, D), F32), vec_out, vec_out, vec_out],
        compiler_params=_cparams(("arbitrary",)),
    )(x, dh, dres, g, sc)


def gate_bwd(dx, branch, gate, name):
    def body(dx_ref, br_ref, gate_ref, o_ref, dgate_ref):
        @pl.when(pl.program_id(0) == 0)
        def _():
            dgate_ref[...] = jnp.zeros_like(dgate_ref)

        dxv = dx_ref[...]
        o_ref[...] = (dxv * gate_ref[...]).astype(o_ref.dtype)
        dgate_ref[...] += jnp.sum(dxv * br_ref[...], axis=0, keepdims=True)

    return pl.pallas_call(
        body, name=name, grid=(S // TR,),
        in_specs=[ROW_SPEC, ROW_SPEC, VEC_SPEC], out_specs=[ROW_SPEC, VEC_SPEC],
        out_shape=[jax.ShapeDtypeStruct((S, D), BF16), jax.ShapeDtypeStruct((1, D), F32)],
        compiler_params=_cparams(("arbitrary",)),
    )(dx, branch, gate)


def loss_head(x, target, g, name):
    def body(x_ref, t_ref, g_ref, dx_ref, loss_ref, dg_ref):
        @pl.when(pl.program_id(0) == 0)
        def _():
            loss_ref[...] = jnp.zeros_like(loss_ref)
            dg_ref[...] = jnp.zeros_like(dg_ref)

        xv, gv = x_ref[...], g_ref[...]
        rstd = lax.rsqrt(jnp.mean(xv * xv, axis=-1, keepdims=True) + EPS)
        xhat = xv * rstd
        err = xhat * gv - t_ref[...]
        loss_ref[...] += jnp.sum(err * err) * (0.5 / D)
        dy = err * (1.0 / D)
        dg_ref[...] += jnp.sum(dy * xhat, axis=0, keepdims=True)
        dxhat = dy * gv
        dx_ref[...] = rstd * (dxhat - xhat * jnp.mean(dxhat * xhat, axis=-1, keepdims=True))

    return pl.pallas_call(
        body, name=name, grid=(S // TR,),
        in_specs=[ROW_SPEC, ROW_SPEC, VEC_SPEC],
        out_specs=[ROW_SPEC, pl.BlockSpec((1, 128), lambda i: (0, 0)), VEC_SPEC],
        out_shape=[jax.ShapeDtypeStruct((S, D), F32), jax.ShapeDtypeStruct((1, 128), F32),
                   jax.ShapeDtypeStruct((1, D), F32)],
        compiler_params=_cparams(("arbitrary",)),
    )(x, target, g)


TQ = 512
RS = 128
NSUB = TQ // RS
TK = 128


def _dot_hilo(a, tri):
    hi = a.astype(BF16)
    lo = (a - hi.astype(F32)).astype(BF16)
    return jnp.dot(hi, tri, preferred_element_type=F32) + jnp.dot(lo, tri, preferred_element_type=F32)


def _log_stay(z):
    return -(jnp.maximum(z, 0.0) + jnp.log(1.0 + jnp.exp(-jnp.abs(z))))


def _tri_and_ones(kind):
    row = lax.broadcasted_iota(jnp.int32, (TK, 2 * TK), 0)
    col = lax.broadcasted_iota(jnp.int32, (TK, 2 * TK), 1)
    tri = {"after": row > col, "upto": row <= col, "before": row < col}[kind]
    return jnp.logical_or(col >= TK, tri).astype(BF16)


NPAIR = NH // 2
SCALE = HD ** -0.5


def _pair_specs(first_block):
    rows = pl.BlockSpec((TQ, LANES), lambda p, i: (i, first_block + p))
    whole = pl.BlockSpec((S, LANES), lambda p, i: (0, first_block + p))
    return rows, whole


Q_ROWS_SPEC, _ = _pair_specs(0)
_, K_ALL_SPEC = _pair_specs(NPAIR)
_, V_ALL_SPEC = _pair_specs(2 * NPAIR)
PAIR_ROWS_SPEC = pl.BlockSpec((TQ, LANES), lambda p, i: (i, p))
PAIR_ALL_SPEC = pl.BlockSpec((S, LANES), lambda p, i: (0, p))
PAIR_TOTAL_SPEC = pl.BlockSpec((2, TQ, TK), lambda p, i: (p, i, 0))


def _head_halves(x):
    first = lax.broadcasted_iota(jnp.int32, x.shape, 1) < HD
    zero = jnp.zeros_like(x)
    return jnp.where(first, x, zero), jnp.where(first, zero, x)


def _join_heads(a, b):
    return jnp.where(lax.broadcasted_iota(jnp.int32, a.shape, 1) < HD, a, b)


def attn_fwd(qkv, name):
    def body(q_ref, k_ref, v_ref, o_ref, r_ref):
        i = pl.program_id(1)
        chains = [(sub, h) for sub in range(NSUB) for h in range(2)]
        q_sub = [_head_halves(q_ref[pl.ds(sub * RS, RS), :]) for sub in range(NSUB)]
        s_off = lax.broadcasted_iota(jnp.int32, (RS, TK), 1)
        t_pos = [i * TQ + sub * RS + lax.broadcasted_iota(jnp.int32, (RS, TK), 0) for sub in range(NSUB)]
        after = _tri_and_ones("after")
        nblk = (i + 1) * (TQ // TK)

        def step(jj, carry):
            acc, later = carry
            start = pl.multiple_of((nblk - 1 - jj) * TK, TK)
            kb = k_ref[pl.ds(start, TK), :]
            vb = v_ref[pl.ds(start, TK), :]
            mask = [(start + s_off) < t for t in t_pos]
            z = [lax.dot_general(q_sub[sub][h], kb, (((1,), (1,)), ((), ())), preferred_element_type=F32) * SCALE
                 for sub, h in chains]
            ls, sums = [], []
            for c, (sub, h) in enumerate(chains):
                ls.append(_log_stay(z[c]))
                sums.append(_dot_hilo(jnp.where(mask[sub], ls[c], 0.0), after))
            pv = []
            for c, (sub, h) in enumerate(chains):
                w = jnp.where(mask[sub], jnp.exp(z[c] + ls[c] + (sums[c][:, :TK] + later[c])), 0.0)
                pv.append(jnp.dot(w.astype(BF16), vb, preferred_element_type=F32))
            acc = tuple(acc[sub] + _join_heads(pv[2 * sub], pv[2 * sub + 1]) for sub in range(NSUB))
            return acc, tuple(later[c] + sums[c][:, TK:] for c in range(len(chains)))

        init = (tuple(jnp.zeros((RS, LANES), F32) for _ in range(NSUB)),
                tuple(jnp.zeros((RS, TK), F32) for _ in chains))
        acc, later = lax.fori_loop(0, nblk, step, init)
        for sub in range(NSUB):
            o_ref[pl.ds(sub * RS, RS), :] = acc[sub].astype(o_ref.dtype)
            r_ref[0, pl.ds(sub * RS, RS), :] = later[2 * sub]
            r_ref[1, pl.ds(sub * RS, RS), :] = later[2 * sub + 1]

    return pl.pallas_call(
        body, name=name, grid=(NPAIR, S // TQ),
        in_specs=[Q_ROWS_SPEC, K_ALL_SPEC, V_ALL_SPEC],
        out_specs=[PAIR_ROWS_SPEC, PAIR_TOTAL_SPEC],
        out_shape=[jax.ShapeDtypeStruct((S, NH * HD), BF16), jax.ShapeDtypeStruct((NH, S, TK), F32)],
        compiler_params=_cparams(("parallel", "parallel")),
    )(qkv, qkv, qkv)


def attn_bwd(qkv, dout, totals, name):
    def body(q_ref, k_ref, v_ref, do_ref, r_ref, dq_ref, dk_ref, dv_ref):
        i = pl.program_id(1)

        @pl.when(i == 0)
        def _():
            dk_ref[...] = jnp.zeros_like(dk_ref)
            dv_ref[...] = jnp.zeros_like(dv_ref)

        chains = [(sub, h) for sub in range(NSUB) for h in range(2)]
        nch = len(chains)
        qb = q_ref[...]
        dob = do_ref[...].astype(BF16)
        q_sub = [_head_halves(qb[sub * RS:(sub + 1) * RS]) for sub in range(NSUB)]
        do_sub = [_head_halves(dob[sub * RS:(sub + 1) * RS]) for sub in range(NSUB)]
        totals = [r_ref[h, pl.ds(sub * RS, RS), :] for sub, h in chains]
        s_off = lax.broadcasted_iota(jnp.int32, (RS, TK), 1)
        t_pos = [i * TQ + sub * RS + lax.broadcasted_iota(jnp.int32, (RS, TK), 0) for sub in range(NSUB)]
        upto = _tri_and_ones("upto")
        before_tri = _tri_and_ones("before")
        contract_lanes = (((1,), (1,)), ((), ()))
        contract_rows = (((0,), (0,)), ((), ()))

        def step(j, carry):
            dq, before, dbefore = carry
            start = pl.multiple_of(j * TK, TK)
            kb = k_ref[pl.ds(start, TK), :]
            vb = v_ref[pl.ds(start, TK), :]
            mask = [(start + s_off) < t for t in t_pos]
            z = [lax.dot_general(q_sub[sub][h], kb, contract_lanes, preferred_element_type=F32) * SCALE
                 for sub, h in chains]
            dw = [lax.dot_general(do_sub[sub][h], vb, contract_lanes, preferred_element_type=F32)
                  for sub, h in chains]
            ls, sums = [], []
            for c, (sub, h) in enumerate(chains):
                ls.append(_log_stay(z[c]))
                sums.append(_dot_hilo(jnp.where(mask[sub], ls[c], 0.0), upto))
            w, dl, dsums = [], [], []
            for c, (sub, h) in enumerate(chains):
                log_after = totals[c] - (sums[c][:, :TK] + before[c])
                w.append(jnp.where(mask[sub], jnp.exp((z[c] + ls[c]) + log_after), 0.0))
                dl.append(dw[c] * w[c])
                dsums.append(_dot_hilo(dl[c], before_tri))
            dvs = [lax.dot_general(jnp.concatenate([w[2 * sub + h] for sub in range(NSUB)], axis=0).astype(BF16),
                                   dob, contract_rows, preferred_element_type=F32) for h in range(2)]
            dz = []
            for c, (sub, h) in enumerate(chains):
                beta = jnp.where(mask[sub], jnp.exp(z[c] + ls[c]), 0.0)
                dstay = dsums[c][:, :TK] + dbefore[c]
                dz.append(((dl[c] * (1.0 - beta) - beta * dstay) * SCALE).astype(BF16))
            dqs = [jnp.dot(dz[c], kb, preferred_element_type=F32) for c in range(nch)]
            dks = [lax.dot_general(jnp.concatenate([dz[2 * sub + h] for sub in range(NSUB)], axis=0), qb,
                                   contract_rows, preferred_element_type=F32) for h in range(2)]
            dk_ref[pl.ds(start, TK), :] += _join_heads(*dks)
            dv_ref[pl.ds(start, TK), :] += _join_heads(*dvs)
            dq = tuple(dq[sub] + _join_heads(dqs[2 * sub], dqs[2 * sub + 1]) for sub in range(NSUB))
            return (dq, tuple(before[c] + sums[c][:, TK:] for c in range(nch)),
                    tuple(dbefore[c] + dsums[c][:, TK:] for c in range(nch)))

        zeros = tuple(jnp.zeros((RS, TK), F32) for _ in chains)
        init = (tuple(jnp.zeros((RS, LANES), F32) for _ in range(NSUB)), zeros, zeros)
        dq, _, _ = lax.fori_loop(0, (i + 1) * (TQ // TK), step, init)
        for sub in range(NSUB):
            dq_ref[pl.ds(sub * RS, RS), :] = dq[sub]

    full = jax.ShapeDtypeStruct((S, NH * HD), F32)
    return pl.pallas_call(
        body, name=name, grid=(NPAIR, S // TQ),
        in_specs=[Q_ROWS_SPEC, K_ALL_SPEC, V_ALL_SPEC, PAIR_ROWS_SPEC, PAIR_TOTAL_SPEC],
        out_specs=[PAIR_ROWS_SPEC, PAIR_ALL_SPEC, PAIR_ALL_SPEC],
        out_shape=[full, full, full],
        compiler_params=_cparams(("parallel", "arbitrary")),
    )(qkv, qkv, qkv, dout, totals)


def _proj_cols(first_col):
    base = first_col // LANES
    return pl.BlockSpec((S, LANES), lambda j: (0, base + j))


CONV_OUT_SPEC = pl.BlockSpec((S, LANES), lambda j: (0, j))
CONV_DOUT_SPEC = pl.BlockSpec((S, LANES), lambda j: (0, (NH * HD) // LANES + j))
CONV_W_SPEC = pl.BlockSpec((8, LANES), lambda j: (0, j))
CONV_B_SPEC = pl.BlockSpec((1, LANES), lambda j: (0, j))


def _shift_down(u, n):
    rows = lax.broadcasted_iota(jnp.int32, u.shape, 0)
    return jnp.where(rows >= n, pltpu.roll(u, n, 0), 0.0)


def _shift_up(u, n):
    rows = lax.broadcasted_iota(jnp.int32, u.shape, 0)
    return jnp.where(rows < S - n, pltpu.roll(u, S - n, 0), 0.0)


def conv_fwd(proj, cw8, cb, name):
    def body(bg_ref, cg_ref, hc_ref, w_ref, b_ref, o_ref):
        u = cg_ref[...] * hc_ref[...]
        w = w_ref[...]
        y = w[0:1, :] * _shift_down(u, 2) + w[1:2, :] * _shift_down(u, 1) + w[2:3, :] * u + b_ref[...]
        o_ref[...] = bg_ref[...] * y

    return pl.pallas_call(
        body, name=name, grid=(CW // LANES,),
        in_specs=[_proj_cols(0), _proj_cols(CW), _proj_cols(2 * CW), CONV_W_SPEC, CONV_B_SPEC],
        out_specs=CONV_OUT_SPEC, out_shape=jax.ShapeDtypeStruct((S, CW), F32),
        compiler_params=_cparams(("parallel",)),
    )(proj, proj, proj, cw8, cb)


def conv_bwd(proj, dout, cw8, cb, name):
    def body(bg_ref, cg_ref, hc_ref, do_ref, w_ref, b_ref, dbg_ref, dcg_ref, dhc_ref, dw_ref, db_ref):
        cg, hc, do = cg_ref[...], hc_ref[...], do_ref[...]
        w = w_ref[...]
        u = cg * hc
        u1, u2 = _shift_down(u, 1), _shift_down(u, 2)
        y = w[0:1, :] * u2 + w[1:2, :] * u1 + w[2:3, :] * u + b_ref[...]
        dbg_ref[...] = do * y
        dy = do * bg_ref[...]
        db_ref[...] = jnp.sum(dy, axis=0, keepdims=True)
        dw_ref[...] = jnp.concatenate(
            [jnp.sum(dy * u2, axis=0, keepdims=True), jnp.sum(dy * u1, axis=0, keepdims=True),
             jnp.sum(dy * u, axis=0, keepdims=True), jnp.zeros((5, LANES), F32)], axis=0)
        du = w[2:3, :] * dy + w[1:2, :] * _shift_up(dy, 1) + w[0:1, :] * _shift_up(dy, 2)
        dcg_ref[...] = du * hc
        dhc_ref[...] = du * cg

    full = jax.ShapeDtypeStruct((S, CW), F32)
    return pl.pallas_call(
        body, name=name, grid=(CW // LANES,),
        in_specs=[_proj_cols(0), _proj_cols(CW), _proj_cols(2 * CW), CONV_DOUT_SPEC, CONV_W_SPEC, CONV_B_SPEC],
        out_specs=[CONV_OUT_SPEC, CONV_OUT_SPEC, CONV_OUT_SPEC, CONV_W_SPEC, CONV_B_SPEC],
        out_shape=[full, full, full, jax.ShapeDtypeStruct((8, CW), F32), jax.ShapeDtypeStruct((1, CW), F32)],
        compiler_params=_cparams(("parallel",)),
    )(proj, proj, proj, dout, cw8, cb)


GELU_K = math.sqrt(2.0 / math.pi)
GELU_C = 0.044715


def _gelu(x):
    return 0.5 * x * (1.0 + jnp.tanh(GELU_K * (x + GELU_C * (x * x * x))))


def _gelu_grad(x):
    t = jnp.tanh(GELU_K * (x + GELU_C * (x * x * x)))
    return 0.5 * (1.0 + t) + 0.5 * x * (1.0 - t * t) * (GELU_K * (1.0 + 3.0 * GELU_C * (x * x)))


def _sg_masks():
    row = lax.broadcasted_iota(jnp.int32, (T, T), 0)
    col = lax.broadcasted_iota(jnp.int32, (T, T), 1)
    causal = jnp.right_shift(row, 6) >= jnp.right_shift(col, 6)
    head_of_col = jnp.right_shift(lax.broadcasted_iota(jnp.int32, (T, CW), 1), 6)
    return causal, head_of_col


def _sg_mixed(vnb, sw_ref, bias, causal, head_of_col):
    mixed = bias
    for h in range(SG_HEADS):
        wh = jnp.where(causal, sw_ref[h], 0.0).astype(BF16)
        mh = jnp.dot(wh, vnb, preferred_element_type=F32)
        mixed = mixed + jnp.where(head_of_col == h, mh, 0.0)
    return mixed


SG_U_SPEC = pl.BlockSpec((T, CW), lambda n: (n, 3))
SG_V_SPEC = pl.BlockSpec((T, CW), lambda n: (n, 4))
SG_ROW_SPEC = pl.BlockSpec((T, CW), lambda n: (n, 0))
SG_DOUT_SPEC = pl.BlockSpec((T, CW), lambda n: (n, 3))
SG_G_SPEC = pl.BlockSpec((1, CW), lambda n: (0, 0))
SG_W_SPEC = pl.BlockSpec((SG_HEADS, T, T), lambda n: (0, 0, 0))
SG_BIAS_SPEC = pl.BlockSpec((T, CW), lambda n: (0, 0))


def sg_fwd(proj, gn, sw, bias, name):
    def body(u_ref, v_ref, g_ref, sw_ref, bias_ref, o_ref):
        causal, head_of_col = _sg_masks()
        gv = _gelu(v_ref[...])
        rstd = lax.rsqrt(jnp.mean(gv * gv, axis=-1, keepdims=True) + EPS)
        vnb = ((gv * rstd) * g_ref[...]).astype(BF16)
        mixed = _sg_mixed(vnb, sw_ref, bias_ref[...], causal, head_of_col)
        o_ref[...] = _gelu(u_ref[...]) * mixed

    return pl.pallas_call(
        body, name=name, grid=(S // T,),
        in_specs=[SG_U_SPEC, SG_V_SPEC, SG_G_SPEC, SG_W_SPEC, SG_BIAS_SPEC],
        out_specs=SG_ROW_SPEC, out_shape=jax.ShapeDtypeStruct((S, CW), F32),
        compiler_params=_cparams(("parallel",)),
    )(proj, proj, gn, sw, bias)


def sg_bwd(proj, dout, gn, sw, bias, name):
    def body(u_ref, v_ref, do_ref, g_ref, sw_ref, bias_ref, du_ref, dv_ref, dg_ref, dsw_ref, dbias_ref):
        @pl.when(pl.program_id(0) == 0)
        def _():
            dg_ref[...] = jnp.zeros_like(dg_ref)
            dsw_ref[...] = jnp.zeros_like(dsw_ref)
            dbias_ref[...] = jnp.zeros_like(dbias_ref)

        causal, head_of_col = _sg_masks()
        uv, vv, do, gnv = u_ref[...], v_ref[...], do_ref[...], g_ref[...]
        gv = _gelu(vv)
        rstd = lax.rsqrt(jnp.mean(gv * gv, axis=-1, keepdims=True) + EPS)
        xhat = gv * rstd
        vnb = (xhat * gnv).astype(BF16)
        mixed = _sg_mixed(vnb, sw_ref, bias_ref[...], causal, head_of_col)
        du_ref[...] = (do * mixed) * _gelu_grad(uv)
        dmix = do * _gelu(uv)
        dbias_ref[...] += dmix
        dmixb = dmix.astype(BF16)
        dvn = jnp.zeros((T, CW), F32)
        for h in range(SG_HEADS):
            wh = jnp.where(causal, sw_ref[h], 0.0).astype(BF16)
            dvh = lax.dot_general(wh, dmixb, (((0,), (0,)), ((), ())), preferred_element_type=F32)
            dvn = dvn + jnp.where(head_of_col == h, dvh, 0.0)
            dmh = jnp.where(head_of_col == h, dmixb, jnp.zeros_like(dmixb))
            dwh = lax.dot_general(dmh, vnb, (((1,), (1,)), ((), ())), preferred_element_type=F32)
            dsw_ref[h] += jnp.where(causal, dwh, 0.0)
        dg_ref[...] += jnp.sum(dvn * xhat, axis=0, keepdims=True)
        dxhat = dvn * gnv
        dgv = rstd * (dxhat - xhat * jnp.mean(dxhat * xhat, axis=-1, keepdims=True))
        dv_ref[...] = dgv * _gelu_grad(vv)

    full = jax.ShapeDtypeStruct((S, CW), F32)
    return pl.pallas_call(
        body, name=name, grid=(S // T,),
        in_specs=[SG_U_SPEC, SG_V_SPEC, SG_DOUT_SPEC, SG_G_SPEC, SG_W_SPEC, SG_BIAS_SPEC],
        out_specs=[SG_ROW_SPEC, SG_ROW_SPEC, SG_G_SPEC, SG_W_SPEC, SG_BIAS_SPEC],
        out_shape=[full, full, jax.ShapeDtypeStruct((1, CW), F32),
                   jax.ShapeDtypeStruct((SG_HEADS, T, T), F32), jax.ShapeDtypeStruct((T, CW), F32)],
        compiler_params=_cparams(("arbitrary",)),
    )(proj, proj, dout, gn, sw, bias)


ADA_COLS = NMOD * D // NDEV


def ada_fwd(c_all, ada_w, ada_b_mine, name):
    def body(c_ref, w_ref, b_ref, o_ref, ca_ref):
        cv = c_ref[...]
        ca = cv * (1.0 / (1.0 + jnp.exp(-cv)))
        ca_ref[...] = ca
        cab = ca.astype(BF16)
        for l in range(L):
            o_ref[l] = jnp.dot(cab, w_ref[l].astype(BF16), preferred_element_type=F32) + b_ref[l]

    return pl.pallas_call(
        body, name=name,
        out_shape=[jax.ShapeDtypeStruct((L, NDEV, ADA_COLS), F32), jax.ShapeDtypeStruct((NDEV, D), F32)],
        compiler_params=_cparams(),
    )(c_all, ada_w, ada_b_mine)


def ada_bwd(ca, dmod_cols, name):
    def body(ca_ref, dm_ref, o_ref):
        cab = ca_ref[...].astype(BF16)
        for l in range(L):
            o_ref[l] = lax.dot_general(cab, dm_ref[l].astype(BF16), (((0,), (0,)), ((), ())),
                                       preferred_element_type=F32)

    return pl.pallas_call(
        body, name=name, out_shape=jax.ShapeDtypeStruct((L, D, ADA_COLS), F32),
        compiler_params=_cparams(),
    )(ca, dmod_cols)


def _adamw(w, g, m, v):
    m = B1 * m + (1.0 - B1) * g
    v = B2 * v + (1.0 - B2) * (g * g)
    m_hat = m / BC1
    v_hat = v / BC2
    delta = -LR * (m_hat / (jnp.sqrt(v_hat) + AEPS) + WD * w)
    return delta, m, v


def sum_gathered(parts, name):
    _, rows, cols = parts.shape

    def body(p_ref, o_ref):
        acc = p_ref[0]
        for d in range(1, NDEV):
            acc = acc + p_ref[d]
        o_ref[...] = acc

    return pl.pallas_call(
        body, name=name, out_shape=jax.ShapeDtypeStruct((rows, cols), F32),
        compiler_params=_cparams(),
    )(parts)


def adamw_plain(w, g, m, v, tr, name):
    rows, cols = w.shape
    spec = pl.BlockSpec((tr, cols), lambda i: (i, 0))

    def body(w_ref, g_ref, m_ref, v_ref, d_ref, nm_ref, nv_ref):
        delta, nm, nv = _adamw(w_ref[...], g_ref[...], m_ref[...], v_ref[...])
        d_ref[...] = delta
        nm_ref[...] = nm
        nv_ref[...] = nv

    shp = jax.ShapeDtypeStruct((rows, cols), F32)
    return pl.pallas_call(
        body, name=name, grid=(rows // tr,), in_specs=[spec] * 4, out_specs=[spec] * 3,
        out_shape=[shp, shp, shp], compiler_params=_cparams(("parallel",)),
    )(w, g, m, v)


def adamw_reduce(w, parts, m, v, tr, name):
    _, rows, cols = w.shape
    spec = pl.BlockSpec((None, tr, cols), lambda l, i: (l, i, 0))
    pspec = pl.BlockSpec((NDEV, None, tr, cols), lambda l, i: (0, l, i, 0))

    def body(w_ref, p_ref, m_ref, v_ref, g_ref, d_ref, nm_ref, nv_ref):
        g = p_ref[0].astype(F32)
        for d in range(1, NDEV):
            g = g + p_ref[d].astype(F32)
        delta, nm, nv = _adamw(w_ref[...], g, m_ref[...], v_ref[...])
        g_ref[...] = g
        d_ref[...] = delta
        nm_ref[...] = nm
        nv_ref[...] = nv

    shp = jax.ShapeDtypeStruct(w.shape, F32)
    return pl.pallas_call(
        body, name=name, grid=(L, rows // tr), in_specs=[spec, pspec, spec, spec], out_specs=[spec] * 4,
        out_shape=[shp] * 4, compiler_params=_cparams(("parallel", "parallel")),
    )(w, parts, m, v)


def _pad_rows(flat, rows):
    return jnp.pad(flat, (0, rows * LANES - flat.shape[0])).reshape(rows, LANES)


def kernel(x, c, ada_w, ada_b, norm_mix_g, norm_mlp_g, w_in, conv_w, conv_b, gmlp_norm_g, spatial_w, spatial_b, w_out, mlp_w1, mlp_w2, final_norm_g, loss_target, m_ada_w, m_ada_b, m_norm_mix_g, m_norm_mlp_g, m_w_in, m_conv_w, m_conv_b, m_gmlp_norm_g, m_spatial_w, m_spatial_b, m_w_out, m_mlp_w1, m_mlp_w2, m_final_norm_g, v_ada_w, v_ada_b, v_norm_mix_g, v_norm_mlp_g, v_w_in, v_conv_w, v_conv_b, v_gmlp_norm_g, v_spatial_w, v_spatial_b, v_w_out, v_mlp_w1, v_mlp_w2, v_final_norm_g):
    me = _lin(_my_pos())
    x0 = x[0]
    target = loss_target[0]
    conv_shard = conv_w.shape[-1]

    pack0 = _pad_rows(jnp.concatenate([c.reshape(-1), conv_w.reshape(-1)]), 16)
    g0 = all_gather([pack0], "gather_c_conv")[0].reshape(NDEV, 16 * LANES)
    c_all = g0[:, :D]
    conv_full = (g0[:, D:D + L * 3 * conv_shard].reshape(NDEV, L, 3, conv_shard)
                 .transpose(1, 2, 0, 3).reshape(L, 3, CW))
    gw_in, gw_out, gw1, gw2 = all_gather(
        [w_in.astype(BF16), w_out.astype(BF16), mlp_w1.astype(BF16), mlp_w2.astype(BF16)], "gather_weights")
    W_in = gw_in.transpose(1, 2, 0, 3).reshape(L, D, PROJ)
    W_out = gw_out.transpose(1, 0, 2, 3).reshape(L, D, D)
    W1 = gw1.transpose(1, 2, 0, 3).reshape(L, D, DFF)
    W2 = gw2.transpose(1, 0, 2, 3).reshape(L, DFF, D)

    ada_b_mine = lax.dynamic_slice(ada_b, (0, me * ADA_COLS), (L, ADA_COLS)).reshape(L, 1, ADA_COLS)
    mod_part, c_act = ada_fwd(c_all, ada_w, ada_b_mine, "ada_fwd")
    gmod = all_gather([mod_part], "gather_mod")[0]
    mod = lax.dynamic_index_in_dim(gmod, me, axis=2, keepdims=False)
    mod = mod.transpose(1, 0, 2).reshape(L, NMOD, 1, D)

    cw8 = jnp.pad(conv_full, ((0, 0), (0, 5), (0, 0)))
    sg_bias = jnp.repeat(spatial_b.transpose(0, 2, 1), HD, axis=2)

    saved = []
    xl = x0
    for l in range(L):
        sh_m, sc_m, g_m, sh_f, sc_f, g_f = [mod[l, k] for k in range(NMOD)]
        h1 = normmod_fwd(xl, norm_mix_g[l:l + 1], sc_m, sh_m, f"norm_mix_fwd{l}")
        qkv = mm(h1, W_in, l=l, tm=512, tn=512, out_dtypes=[BF16], cols=(0, QKV), name=f"proj_qkv{l}")[0]
        proj = mm(h1, W_in, l=l, tm=512, tn=256, out_dtypes=[F32], cols=(QKV, REST), name=f"proj_rest{l}")[0]
        a_out, a_tot = attn_fwd(qkv, f"attn_fwd{l}")
        c_out = conv_fwd(proj, cw8[l], conv_b[l:l + 1], f"conv_fwd{l}")
        s_out = sg_fwd(proj, gmlp_norm_g[l:l + 1], spatial_w[l], sg_bias[l], f"sg_fwd{l}")
        cat = jnp.concatenate([a_out, c_out.astype(BF16), s_out.astype(BF16)], axis=1)
        mix, x1 = mm(cat, W_out, l=l, tm=512, tn=512, out_dtypes=[F32, F32],
                     epilogue=lambda acc, xr, g: (acc, xr + g * acc),
                     extras=[(xl, "tile"), (g_m, "col")], name=f"mix{l}")
        h2 = normmod_fwd(x1, norm_mlp_g[l:l + 1], sc_f, sh_f, f"norm_mlp_fwd{l}")
        a, r = mm(h2, W1, l=l, tm=512, tn=1024, out_dtypes=[F32, BF16],
                  epilogue=lambda acc: (acc, jnp.square(jnp.maximum(acc, 0.0))), name=f"mlp_up{l}")
        m2, x2 = mm(r, W2, l=l, tm=512, tn=512, out_dtypes=[F32, F32],
                    epilogue=lambda acc, xr, g: (acc, xr + g * acc),
                    extras=[(x1, "tile"), (g_f, "col")], name=f"mlp_down{l}")
        saved.append(dict(x=xl, h1=h1, proj=proj, qkv=qkv, a_tot=a_tot, cat=cat, mix=mix,
                          x1=x1, h2=h2, a=a, r=r, m2=m2))
        xl = x2

    dx, loss_part, d_final_g = loss_head(xl, target, final_norm_g.reshape(1, D), "loss_head")

    dmod = [None] * L
    dW_in, dW_out, dW1, dW2 = [None] * L, [None] * L, [None] * L, [None] * L
    d_norm_mix, d_norm_mlp, d_conv_w, d_conv_b = [None] * L, [None] * L, [None] * L, [None] * L
    d_gn, d_sw, d_sb = [None] * L, [None] * L, [None] * L
    for l in reversed(range(L)):
        sv = saved[l]
        sh_m, sc_m, g_m, sh_f, sc_f, g_f = [mod[l, k] for k in range(NMOD)]
        dm2, dg_f = gate_bwd(dx, sv["m2"], g_f, f"gate_mlp_bwd{l}")
        da = mm(dm2, W2, l=l, tm=512, tn=1024, out_dtypes=[BF16], trans_b=True,
                epilogue=lambda acc, av: (acc * (2.0 * jnp.maximum(av, 0.0)),),
                extras=[(sv["a"], "tile")], name=f"mlp_down_dgrad{l}")[0]
        dW2[l] = mm(sv["r"], dm2, tm=512, tn=1024, out_dtypes=[BF16], trans_a=True, name=f"mlp_down_wgrad{l}")[0]
        dW1[l] = mm(sv["h2"], da, tm=512, tn=1024, out_dtypes=[BF16], trans_a=True, name=f"mlp_up_wgrad{l}")[0]
        dh2 = mm(da, W1, l=l, tm=512, tn=512, out_dtypes=[F32], trans_b=True, name=f"mlp_up_dgrad{l}")[0]
        dx1, dsc_f, dsh_f, d_norm_mlp[l] = normmod_bwd(sv["x1"], dh2, dx, norm_mlp_g[l:l + 1], sc_f,
                                                       f"norm_mlp_bwd{l}")
        dmix, dg_m = gate_bwd(dx1, sv["mix"], g_m, f"gate_mix_bwd{l}")
        dcat = mm(dmix, W_out, l=l, tm=512, tn=512, out_dtypes=[F32], trans_b=True, name=f"mix_dgrad{l}")[0]
        dW_out[l] = mm(sv["cat"], dmix, tm=512, tn=1024, out_dtypes=[BF16], trans_a=True, name=f"mix_wgrad{l}")[0]
        dq, dk, dv = attn_bwd(sv["qkv"], dcat, sv["a_tot"], f"attn_bwd{l}")
        dbg, dcg, dhc, dcw8, d_conv_b[l] = conv_bwd(sv["proj"], dcat, cw8[l], conv_b[l:l + 1], f"conv_bwd{l}")
        d_conv_w[l] = dcw8[:3]
        dus, dvs, d_gn[l], dsw, dbias = sg_bwd(sv["proj"], dcat, gmlp_norm_g[l:l + 1],
                                               spatial_w[l], sg_bias[l], f"sg_bwd{l}")
        d_sw[l] = dsw
        d_sb[l] = dbias.reshape(T, SG_HEADS, HD).sum(axis=2).T
        dproj = jnp.concatenate([dq, dk, dv, dbg, dcg, dhc, dus, dvs], axis=1).astype(BF16)
        dW_in[l] = mm(sv["h1"], dproj, tm=512, tn=PROJ // 2, out_dtypes=[BF16], trans_a=True,
                      name=f"proj_wgrad{l}")[0]
        dh1 = mm(dproj, W_in, l=l, tm=512, tn=512, out_dtypes=[F32], trans_b=True, name=f"proj_dgrad{l}")[0]
        dx, dsc_m, dsh_m, d_norm_mix[l] = normmod_bwd(sv["x"], dh1, dx1, norm_mix_g[l:l + 1], sc_m,
                                                      f"norm_mix_bwd{l}")
        dmod[l] = jnp.concatenate([dsh_m, dsc_m, dg_m, dsh_f, dsc_f, dg_f], axis=1)

    grad_x = dx.reshape(1, S, D)

    small_parts = [jnp.concatenate(dmod, axis=0), jnp.concatenate(d_norm_mix, axis=0),
                   jnp.concatenate(d_norm_mlp, axis=0), jnp.stack(d_conv_w), jnp.concatenate(d_conv_b, axis=0),
                   jnp.concatenate(d_gn, axis=0), jnp.stack(d_sw), jnp.stack(d_sb), d_final_g, loss_part[:, :1]]
    sizes = [p.size for p in small_parts]
    small_rows = -(-sum(sizes) // (8 * LANES)) * 8
    small_pack = _pad_rows(jnp.concatenate([p.reshape(-1) for p in small_parts]), small_rows)
    small_all = all_gather([small_pack], "gather_small_grads")[0]
    small_sum = sum_gathered(small_all, "sum_small_grads").reshape(-1)
    offs = [0]
    for sz in sizes:
        offs.append(offs[-1] + sz)
    summed = [small_sum[offs[k]:offs[k + 1]].reshape(small_parts[k].shape) for k in range(len(sizes))]
    (g_ada_b, g_norm_mix, g_norm_mlp, g_conv_w_full, g_conv_b, g_gn, g_sw, g_sb, g_final, loss_sum) = summed
    loss = loss_sum.reshape(())
    g_final = g_final.reshape(D)
    g_conv_w = lax.dynamic_slice(g_conv_w_full, (0, 0, me * conv_shard), (L, 3, conv_shard))

    dmod_all = small_all.reshape(NDEV, -1)[:, :L * NMOD * D].reshape(NDEV, L, NMOD * D)
    dmod_cols = lax.dynamic_slice(dmod_all, (0, 0, me * ADA_COLS), (NDEV, L, ADA_COLS)).transpose(1, 0, 2)
    g_ada_w = ada_bwd(c_act, dmod_cols, "ada_bwd")

    def col_pieces(gs, shard):
        st = jnp.stack(gs)
        return st.reshape(L, st.shape[1], NDEV, shard).transpose(2, 0, 1, 3)

    def row_pieces(gs, shard):
        st = jnp.stack(gs)
        return st.reshape(L, NDEV, shard, st.shape[2]).transpose(1, 0, 2, 3)

    p_in, p_out, p1, p2 = all_to_all(
        [col_pieces(dW_in, PROJ // NDEV), row_pieces(dW_out, D // NDEV),
         col_pieces(dW1, DFF // NDEV), row_pieces(dW2, DFF // NDEV)], "exchange_grads")

    g_w_in, d_w_in, nm_w_in, nv_w_in = adamw_reduce(w_in, p_in, m_w_in, v_w_in, 256, "adamw_w_in")
    g_w_out, d_w_out, nm_w_out, nv_w_out = adamw_reduce(w_out, p_out, m_w_out, v_w_out, 128, "adamw_w_out")
    g_w1, d_w1, nm_w1, nv_w1 = adamw_reduce(mlp_w1, p1, m_mlp_w1, v_mlp_w1, 256, "adamw_mlp_w1")
    g_w2, d_w2, nm_w2, nv_w2 = adamw_reduce(mlp_w2, p2, m_mlp_w2, v_mlp_w2, 256, "adamw_mlp_w2")

    flat2 = lambda t: t.reshape(L * D, ADA_COLS)
    d_ada_w, nm_ada_w, nv_ada_w = [t.reshape(L, D, ADA_COLS) for t in adamw_plain(
        flat2(ada_w), flat2(g_ada_w), flat2(m_ada_w), flat2(v_ada_w), 256, "adamw_ada_w")]

    small_w = [ada_b, norm_mix_g, norm_mlp_g, conv_w, conv_b, gmlp_norm_g, spatial_w, spatial_b, final_norm_g]
    small_m = [m_ada_b, m_norm_mix_g, m_norm_mlp_g, m_conv_w, m_conv_b, m_gmlp_norm_g, m_spatial_w, m_spatial_b,
               m_final_norm_g]
    small_v = [v_ada_b, v_norm_mix_g, v_norm_mlp_g, v_conv_w, v_conv_b, v_gmlp_norm_g, v_spatial_w, v_spatial_b,
               v_final_norm_g]
    small_g = [g_ada_b, g_norm_mix, g_norm_mlp, g_conv_w, g_conv_b, g_gn, g_sw, g_sb, g_final]
    wsizes = [p.size for p in small_w]
    wrows = -(-sum(wsizes) // (8 * LANES)) * 8
    pack = lambda ps: _pad_rows(jnp.concatenate([p.reshape(-1) for p in ps]), wrows)
    sd, snm, snv = adamw_plain(pack(small_w), pack(small_g), pack(small_m), pack(small_v), wrows, "adamw_small")
    woffs = [0]
    for sz in wsizes:
        woffs.append(woffs[-1] + sz)

    def unpack(flat):
        flat = flat.reshape(-1)
        return [flat[woffs[k]:woffs[k + 1]].reshape(small_w[k].shape) for k in range(len(small_w))]

    sd, snm, snv = unpack(sd), unpack(snm), unpack(snv)

    def ordered(big, small):
        ada, win, wout, w1, w2 = big
        return [ada, small[0], small[1], small[2], win, small[3], small[4], small[5], small[6], small[7],
                wout, w1, w2, small[8]]

    grads = ordered([g_ada_w, g_w_in, g_w_out, g_w1, g_w2], small_g)
    deltas = ordered([d_ada_w, d_w_in, d_w_out, d_w1, d_w2], sd)
    new_m = ordered([nm_ada_w, nm_w_in, nm_w_out, nm_w1, nm_w2], snm)
    new_v = ordered([nv_ada_w, nv_w_in, nv_w_out, nv_w1, nv_w2], snv)
    return (loss, grad_x, *grads, *deltas, *new_m, *new_v)
```

```python
import functools
import math

import jax
import jax.numpy as jnp
from jax import lax
from jax.experimental import pallas as pl
from jax.experimental.pallas import tpu as pltpu

F32 = jnp.float32
BF16 = jnp.bfloat16
MESH = pl.DeviceIdType.MESH

S = 2048
D = 1024
L = 2
NDEV = 8
HD = 64
NH = 8
PROJ = 2816
DFF = 4096
NMOD = 6
EPS = 1e-6
T = 128
SG_HEADS = 4
LANES = 128
CW = 256
QKV = 3 * NH * HD
REST = PROJ - QKV

LR, B1, B2, AEPS, WD, STEP = 0.001, 0.9, 0.999, 1e-08, 0.01, 10
BC1 = 1.0 - B1 ** STEP
BC2 = 1.0 - B2 ** STEP

VMEM_LIMIT = 48 * 1024 * 1024

HBM_SPEC = pl.BlockSpec(memory_space=pltpu.HBM)


def _cparams(sem=None):
    return pltpu.CompilerParams(dimension_semantics=sem, vmem_limit_bytes=VMEM_LIMIT)


def _my_pos():
    return lax.axis_index("x"), lax.axis_index("y"), lax.axis_index("c")


def _lin(p):
    return 4 * p[0] + 2 * p[1] + p[2]


class Gather:
    def __init__(self, arrs):
        self.arrs = list(arrs)
        n = len(self.arrs)
        self.out_shape = [jax.ShapeDtypeStruct((NDEV,) + a.shape, a.dtype) for a in self.arrs]
        self.scratch = [pltpu.SemaphoreType.DMA((n, 7)), pltpu.SemaphoreType.DMA((n, 7)),
                        pltpu.SemaphoreType.DMA((n,))]

    def phases(self, ins, outs, sems):
        n = len(self.arrs)
        send_sems, recv_sems, local_sems = sems
        x, y, c = _my_pos()
        me, sibling = (x, y, c), (x, y, 1 - c)
        chips = [(1 - x, y), (x, 1 - y), (1 - x, 1 - y)]

        def copy(a, k, block, to, src=None):
            slot = outs[a].at[_lin(block)]
            return pltpu.make_async_remote_copy(
                src_ref=slot if src is None else src, dst_ref=slot,
                send_sem=send_sems.at[a, k], recv_sem=recv_sems.at[a, k],
                device_id=to, device_id_type=MESH)

        def mine(a):
            return pltpu.make_async_copy(ins[a], outs[a].at[_lin(me)], local_sems.at[a])

        def first(a):
            return [copy(a, 0, me, sibling, src=ins[a])] + [
                copy(a, 1 + j, me, (*chip, c), src=ins[a]) for j, chip in enumerate(chips)]

        def passed(a):
            return [copy(a, 4 + j, (*chip, c), sibling) for j, chip in enumerate(chips)]

        def start():
            for a in range(n):
                mine(a).start()
                for cp in first(a):
                    cp.start()

        def relay():
            for j, chip in enumerate(chips):
                for a in range(n):
                    copy(a, 1 + j, (*chip, c), me).wait_recv()
                    passed(a)[j].start()

        def finish():
            for a in range(n):
                copy(a, 0, sibling, me).wait_recv()
            for j, chip in enumerate(chips):
                for a in range(n):
                    copy(a, 4 + j, (*chip, 1 - c), me).wait_recv()
            for a in range(n):
                for cp in first(a) + passed(a):
                    cp.wait_send()
                mine(a).wait()

        return start, relay, finish


class Exchange:
    def __init__(self, arrs):
        self.arrs = list(arrs)
        n = len(self.arrs)
        self.out_shape = [jax.ShapeDtypeStruct(a.shape, a.dtype) for a in self.arrs]
        self.scratch = [pltpu.SemaphoreType.DMA((n, 7)), pltpu.SemaphoreType.DMA((n, 7)),
                        pltpu.SemaphoreType.DMA((n,))]

    def phases(self, ins, outs, sems):
        n = len(self.arrs)
        send_sems, recv_sems, local_sems = sems
        x, y, c = _my_pos()
        me = (x, y, c)

        def peer(mask):
            return (1 - x if mask & 4 else x, 1 - y if mask & 2 else y, 1 - c if mask & 1 else c)

        def copy(a, mask):
            return pltpu.make_async_remote_copy(
                src_ref=ins[a].at[_lin(peer(mask))], dst_ref=outs[a].at[_lin(me)],
                send_sem=send_sems.at[a, mask - 1], recv_sem=recv_sems.at[a, mask - 1],
                device_id=peer(mask), device_id_type=MESH)

        def arrival(a, mask):
            return pltpu.make_async_remote_copy(
                src_ref=ins[a].at[_lin(me)], dst_ref=outs[a].at[_lin(peer(mask))],
                send_sem=send_sems.at[a, mask - 1], recv_sem=recv_sems.at[a, mask - 1],
                device_id=peer(mask), device_id_type=MESH)

        def mine(a):
            return pltpu.make_async_copy(ins[a].at[_lin(me)], outs[a].at[_lin(me)], local_sems.at[a])

        def start():
            for a in range(n):
                mine(a).start()
            for mask in (4, 2, 6, 1, 5, 3, 7):
                for a in range(n):
                    copy(a, mask).start()

        def relay():
            pass

        def finish():
            for mask in range(1, 8):
                for a in range(n):
                    arrival(a, mask).wait_recv()
            for mask in range(1, 8):
                for a in range(n):
                    copy(a, mask).wait_send()
            for a in range(n):
                mine(a).wait()

        return start, relay, finish


def run_comm(plan, name):
    n = len(plan.arrs)

    def body(*refs):
        start, relay, finish = plan.phases(refs[:n], refs[n:2 * n], refs[2 * n:])
        start()
        relay()
        finish()

    outs = pl.pallas_call(
        body, name=name, out_shape=plan.out_shape,
        in_specs=[HBM_SPEC] * n, out_specs=[HBM_SPEC] * n, scratch_shapes=plan.scratch,
    )(*plan.arrs)
    return list(outs)


def mm(a, b, *, tm, tn, out_dtypes, epilogue=None, extras=(), name, trans_a=False, trans_b=False,
       cols=None, b_blocks=False, out_blocks=False):
    if trans_a:
        kdim, m = a.shape
    else:
        m, kdim = a.shape
    shard = b.shape[-1] if b_blocks else None
    if b_blocks:
        full = (b.shape[1], NDEV * shard)
    else:
        full = b.shape
    first, ncols = cols if cols is not None else (0, full[0] if trans_b else full[1])
    assert full[1 if trans_b else 0] == kdim and m % tm == 0 and ncols % tn == 0 and first % tn == 0
    j0 = first // tn
    if trans_a:
        a_spec = pl.BlockSpec((kdim, tm), lambda i, j: (0, i))
    else:
        a_spec = pl.BlockSpec((tm, kdim), lambda i, j: (i, 0))
    if b_blocks and trans_b:
        b_spec = pl.BlockSpec((NDEV, tn, shard), lambda i, j: (0, j0 + j, 0))
    elif b_blocks:
        assert tn == shard
        b_spec = pl.BlockSpec((None, kdim, tn), lambda i, j: (j0 + j, 0, 0))
    elif trans_b:
        b_spec = pl.BlockSpec((tn, kdim), lambda i, j: (j0 + j, 0))
    else:
        b_spec = pl.BlockSpec((kdim, tn), lambda i, j: (0, j0 + j))
    if out_blocks:
        assert tn * NDEV == ncols
        out_spec = pl.BlockSpec((None, tm, tn), lambda i, j: (j, i, 0))
        out_dims = (NDEV, m, tn)
    else:
        out_spec = pl.BlockSpec((tm, tn), lambda i, j: (i, j))
        out_dims = (m, ncols)
    ex_specs = []
    for arr, kind in extras:
        if kind == "tile":
            ex_specs.append(pl.BlockSpec((tm, tn), lambda i, j: (i, j)))
        else:
            ex_specs.append(pl.BlockSpec((1, tn), lambda i, j: (0, j)))
    n_ex, n_out = len(extras), len(out_dtypes)

    def body(a_ref, b_ref, *rest):
        ex_refs, out_refs = rest[:n_ex], rest[n_ex:]
        if trans_a:
            acc = lax.dot_general(a_ref[...], b_ref[...], (((0,), (0,)), ((), ())),
                                  preferred_element_type=F32)
        elif trans_b and b_blocks:
            acc = jnp.zeros((tm, tn), F32)
            for d in range(NDEV):
                acc = acc + lax.dot_general(a_ref[:, d * shard:(d + 1) * shard], b_ref[d],
                                            (((1,), (1,)), ((), ())), preferred_element_type=F32)
        elif trans_b:
            acc = lax.dot_general(a_ref[...], b_ref[...], (((1,), (1,)), ((), ())),
                                  preferred_element_type=F32)
        else:
            acc = jnp.dot(a_ref[...], b_ref[...], preferred_element_type=F32)
        outs = (acc,) if epilogue is None else epilogue(acc, *[r[...] for r in ex_refs])
        for o_ref, val in zip(out_refs, outs):
            o_ref[...] = val.astype(o_ref.dtype)

    outs = pl.pallas_call(
        body, name=name, grid=(m // tm, ncols // tn),
        in_specs=[a_spec, b_spec] + ex_specs,
        out_specs=[out_spec for _ in range(n_out)],
        out_shape=[jax.ShapeDtypeStruct(out_dims, dt) for dt in out_dtypes],
        compiler_params=_cparams(("parallel", "parallel")),
    )(a, b, *[arr for arr, _ in extras])
    return list(outs)


TR = 256

ROW_SPEC = pl.BlockSpec((TR, D), lambda i: (i, 0))
VEC_SPEC = pl.BlockSpec((1, D), lambda i: (0, 0))


def normmod_fwd(x, g, sc, sh, name):
    def body(x_ref, g_ref, sc_ref, sh_ref, o_ref):
        xv = x_ref[...]
        rstd = lax.rsqrt(jnp.mean(xv * xv, axis=-1, keepdims=True) + EPS)
        n = (xv * rstd) * g_ref[...]
        o_ref[...] = (n * (1.0 + sc_ref[...]) + sh_ref[...]).astype(o_ref.dtype)

    return pl.pallas_call(
        body, name=name, grid=(S // TR,),
        in_specs=[ROW_SPEC, VEC_SPEC, VEC_SPEC, VEC_SPEC], out_specs=ROW_SPEC,
        out_shape=jax.ShapeDtypeStruct((S, D), BF16),
        compiler_params=_cparams(("parallel",)),
    )(x, g, sc, sh)


def normmod_bwd(x, dh, dres, g, sc, name):
    def body(x_ref, dh_ref, dres_ref, g_ref, sc_ref, dx_ref, dsc_ref, dsh_ref, dg_ref):
        @pl.when(pl.program_id(0) == 0)
        def _():
            dsc_ref[...] = jnp.zeros_like(dsc_ref)
            dsh_ref[...] = jnp.zeros_like(dsh_ref)
            dg_ref[...] = jnp.zeros_like(dg_ref)

        xv, dh = x_ref[...], dh_ref[...]
        gv = g_ref[...]
        rstd = lax.rsqrt(jnp.mean(xv * xv, axis=-1, keepdims=True) + EPS)
        xhat = xv * rstd
        dn = dh * (1.0 + sc_ref[...])
        dxhat = dn * gv
        dx_ref[...] = dres_ref[...] + rstd * (dxhat - xhat * jnp.mean(dxhat * xhat, axis=-1, keepdims=True))
        dsc_ref[...] += jnp.sum(dh * (xhat * gv), axis=0, keepdims=True)
        dsh_ref[...] += jnp.sum(dh, axis=0, keepdims=True)
        dg_ref[...] += jnp.sum(dn * xhat, axis=0, keepdims=True)

    vec_out = jax.ShapeDtypeStruct((1, D), F32)
    return pl.pallas_call(
        body, name=name, grid=(S // TR,),
        in_specs=[ROW_SPEC, ROW_SPEC, ROW_SPEC, VEC_SPEC, VEC_SPEC],
        out_specs=[ROW_SPEC, VEC_SPEC, VEC_SPEC, VEC_SPEC],
        out_shape=[jax.ShapeDtypeStruct((S, D), F32), vec_out, vec_out, vec_out],
        compiler_params=_cparams(("arbitrary",)),
    )(x, dh, dres, g, sc)


def gate_bwd(dx, branch, gate, name):
    def body(dx_ref, br_ref, gate_ref, o_ref, dgate_ref):
        @pl.when(pl.program_id(0) == 0)
        def _():
            dgate_ref[...] = jnp.zeros_like(dgate_ref)

        dxv = dx_ref[...]
        o_ref[...] = (dxv * gate_ref[...]).astype(o_ref.dtype)
        dgate_ref[...] += jnp.sum(dxv * br_ref[...], axis=0, keepdims=True)

    return pl.pallas_call(
        body, name=name, grid=(S // TR,),
        in_specs=[ROW_SPEC, ROW_SPEC, VEC_SPEC], out_specs=[ROW_SPEC, VEC_SPEC],
        out_shape=[jax.ShapeDtypeStruct((S, D), BF16), jax.ShapeDtypeStruct((1, D), F32)],
        compiler_params=_cparams(("arbitrary",)),
    )(dx, branch, gate)


def loss_head(x, target, g, name):
    def body(x_ref, t_ref, g_ref, dx_ref, loss_ref, dg_ref):
        @pl.when(pl.program_id(0) == 0)
        def _():
            loss_ref[...] = jnp.zeros_like(loss_ref)
            dg_ref[...] = jnp.zeros_like(dg_ref)

        xv, gv = x_ref[...], g_ref[...]
        rstd = lax.rsqrt(jnp.mean(xv * xv, axis=-1, keepdims=True) + EPS)
        xhat = xv * rstd
        err = xhat * gv - t_ref[...]
        loss_ref[...] += jnp.sum(err * err) * (0.5 / D)
        dy = err * (1.0 / D)
        dg_ref[...] += jnp.sum(dy * xhat, axis=0, keepdims=True)
        dxhat = dy * gv
        dx_ref[...] = rstd * (dxhat - xhat * jnp.mean(dxhat * xhat, axis=-1, keepdims=True))

    return pl.pallas_call(
        body, name=name, grid=(S // TR,),
        in_specs=[ROW_SPEC, ROW_SPEC, VEC_SPEC],
        out_specs=[ROW_SPEC, pl.BlockSpec((1, 128), lambda i: (0, 0)), VEC_SPEC],
        out_shape=[jax.ShapeDtypeStruct((S, D), F32), jax.ShapeDtypeStruct((1, 128), F32),
                   jax.ShapeDtypeStruct((1, D), F32)],
        compiler_params=_cparams(("arbitrary",)),
    )(x, target, g)


TQ = 512
RS = 128
NSUB = TQ // RS
TK = 128


def _dot_hilo(a, tri):
    hi = a.astype(BF16)
    lo = (a - hi.astype(F32)).astype(BF16)
    return jnp.dot(hi, tri, preferred_element_type=F32) + jnp.dot(lo, tri, preferred_element_type=F32)


def _log_stay(z):
    return -(jnp.maximum(z, 0.0) + jnp.log(1.0 + jnp.exp(-jnp.abs(z))))


def _tri_and_ones(kind):
    row = lax.broadcasted_iota(jnp.int32, (TK, 2 * TK), 0)
    col = lax.broadcasted_iota(jnp.int32, (TK, 2 * TK), 1)
    tri = {"after": row > col, "upto": row <= col, "before": row < col}[kind]
    return jnp.logical_or(col >= TK, tri).astype(BF16)


NPAIR = NH // 2
SCALE = HD ** -0.5


def _pair_specs(first_block):
    rows = pl.BlockSpec((TQ, LANES), lambda p, i: (i, first_block + p))
    whole = pl.BlockSpec((S, LANES), lambda p, i: (0, first_block + p))
    return rows, whole


Q_ROWS_SPEC, _ = _pair_specs(0)
_, K_ALL_SPEC = _pair_specs(NPAIR)
_, V_ALL_SPEC = _pair_specs(2 * NPAIR)
PAIR_ROWS_SPEC = pl.BlockSpec((TQ, LANES), lambda p, i: (i, p))
PAIR_ALL_SPEC = pl.BlockSpec((S, LANES), lambda p, i: (0, p))
PAIR_TOTAL_SPEC = pl.BlockSpec((2, TQ, TK), lambda p, i: (p, i, 0))


def _head_halves(x):
    first = lax.broadcasted_iota(jnp.int32, x.shape, 1) < HD
    zero = jnp.zeros_like(x)
    return jnp.where(first, x, zero), jnp.where(first, zero, x)


def _join_heads(a, b):
    return jnp.where(lax.broadcasted_iota(jnp.int32, a.shape, 1) < HD, a, b)


def _comm_hooks(comm, refs, n_in, n_out):
    nc = len(comm.arrs) if comm is not None else 0
    ins, cin = refs[:n_in], refs[n_in:n_in + nc]
    outs = refs[n_in + nc:n_in + nc + n_out]
    cout = refs[n_in + nc + n_out:n_in + 2 * nc + n_out]
    sems = refs[n_in + 2 * nc + n_out:]
    phases = comm.phases(cin, cout, sems) if comm is not None else None
    return ins, outs, phases


def _with_comm(comm, in_specs, out_specs, out_shape, operands):
    if comm is None:
        return dict(in_specs=in_specs, out_specs=out_specs, out_shape=out_shape), operands
    nc = len(comm.arrs)
    return dict(in_specs=in_specs + [HBM_SPEC] * nc, out_specs=out_specs + [HBM_SPEC] * nc,
                out_shape=out_shape + comm.out_shape, scratch_shapes=comm.scratch), operands + comm.arrs


def attn_fwd(qkv, name, comm=None):
    n_steps = S // TQ

    def body(*refs):
        (q_ref, k_ref, v_ref), (o_ref, r_ref), phases = _comm_hooks(comm, refs, 3, 2)
        p = pl.program_id(0)
        i = pl.program_id(1)
        if phases is not None:
            pl.when(jnp.logical_and(p == 0, i == 0))(phases[0])
            pl.when(jnp.logical_and(p == NPAIR - 1, i == n_steps - 2))(phases[1])
        chains = [(sub, h) for sub in range(NSUB) for h in range(2)]
        q_sub = [_head_halves(q_ref[pl.ds(sub * RS, RS), :]) for sub in range(NSUB)]
        s_off = lax.broadcasted_iota(jnp.int32, (RS, TK), 1)
        t_pos = [i * TQ + sub * RS + lax.broadcasted_iota(jnp.int32, (RS, TK), 0) for sub in range(NSUB)]
        after = _tri_and_ones("after")
        nblk = (i + 1) * (TQ // TK)

        def step(jj, carry):
            acc, later = carry
            start = pl.multiple_of((nblk - 1 - jj) * TK, TK)
            kb = k_ref[pl.ds(start, TK), :]
            vb = v_ref[pl.ds(start, TK), :]
            mask = [(start + s_off) < t for t in t_pos]
            z = [lax.dot_general(q_sub[sub][h], kb, (((1,), (1,)), ((), ())), preferred_element_type=F32) * SCALE
                 for sub, h in chains]
            ls, sums = [], []
            for c, (sub, h) in enumerate(chains):
                ls.append(_log_stay(z[c]))
                sums.append(_dot_hilo(jnp.where(mask[sub], ls[c], 0.0), after))
            pv = []
            for c, (sub, h) in enumerate(chains):
                w = jnp.where(mask[sub], jnp.exp(z[c] + ls[c] + (sums[c][:, :TK] + later[c])), 0.0)
                pv.append(jnp.dot(w.astype(BF16), vb, preferred_element_type=F32))
            acc = tuple(acc[sub] + _join_heads(pv[2 * sub], pv[2 * sub + 1]) for sub in range(NSUB))
            return acc, tuple(later[c] + sums[c][:, TK:] for c in range(len(chains)))

        init = (tuple(jnp.zeros((RS, LANES), F32) for _ in range(NSUB)),
                tuple(jnp.zeros((RS, TK), F32) for _ in chains))
        acc, later = lax.fori_loop(0, nblk, step, init)
        for sub in range(NSUB):
            o_ref[pl.ds(sub * RS, RS), :] = acc[sub].astype(o_ref.dtype)
            r_ref[0, pl.ds(sub * RS, RS), :] = later[2 * sub]
            r_ref[1, pl.ds(sub * RS, RS), :] = later[2 * sub + 1]
        if phases is not None:
            pl.when(jnp.logical_and(p == NPAIR - 1, i == n_steps - 1))(phases[2])

    kwargs, operands = _with_comm(
        comm, [Q_ROWS_SPEC, K_ALL_SPEC, V_ALL_SPEC], [PAIR_ROWS_SPEC, PAIR_TOTAL_SPEC],
        [jax.ShapeDtypeStruct((S, NH * HD), BF16), jax.ShapeDtypeStruct((NH, S, TK), F32)], [qkv, qkv, qkv])
    return pl.pallas_call(
        body, name=name, grid=(NPAIR, n_steps),
        compiler_params=_cparams(("arbitrary", "arbitrary")), **kwargs,
    )(*operands)


def attn_bwd(qkv, dout, totals, name, comm=None):
    n_steps = S // TQ

    def body(*refs):
        (q_ref, k_ref, v_ref, do_ref, r_ref), (dq_ref, dk_ref, dv_ref), phases = _comm_hooks(comm, refs, 5, 3)
        p = pl.program_id(0)
        i = pl.program_id(1)
        if phases is not None:
            pl.when(jnp.logical_and(p == 0, i == 0))(phases[0])
            pl.when(jnp.logical_and(p == NPAIR - 1, i == n_steps - 2))(phases[1])

        @pl.when(i == 0)
        def _():
            dk_ref[...] = jnp.zeros_like(dk_ref)
            dv_ref[...] = jnp.zeros_like(dv_ref)

        chains = [(sub, h) for sub in range(NSUB) for h in range(2)]
        nch = len(chains)
        qb = q_ref[...]
        dob = do_ref[...].astype(BF16)
        q_sub = [_head_halves(qb[sub * RS:(sub + 1) * RS]) for sub in range(NSUB)]
        do_sub = [_head_halves(dob[sub * RS:(sub + 1) * RS]) for sub in range(NSUB)]
        totals = [r_ref[h, pl.ds(sub * RS, RS), :] for sub, h in chains]
        s_off = lax.broadcasted_iota(jnp.int32, (RS, TK), 1)
        t_pos = [i * TQ + sub * RS + lax.broadcasted_iota(jnp.int32, (RS, TK), 0) for sub in range(NSUB)]
        upto = _tri_and_ones("upto")
        before_tri = _tri_and_ones("before")
        contract_lanes = (((1,), (1,)), ((), ()))
        contract_rows = (((0,), (0,)), ((), ()))

        def step(j, carry):
            dq, before, dbefore = carry
            start = pl.multiple_of(j * TK, TK)
            kb = k_ref[pl.ds(start, TK), :]
            vb = v_ref[pl.ds(start, TK), :]
            mask = [(start + s_off) < t for t in t_pos]
            z = [lax.dot_general(q_sub[sub][h], kb, contract_lanes, preferred_element_type=F32) * SCALE
                 for sub, h in chains]
            dw = [lax.dot_general(do_sub[sub][h], vb, contract_lanes, preferred_element_type=F32)
                  for sub, h in chains]
            ls, sums = [], []
            for c, (sub, h) in enumerate(chains):
                ls.append(_log_stay(z[c]))
                sums.append(_dot_hilo(jnp.where(mask[sub], ls[c], 0.0), upto))
            w, dl, dsums = [], [], []
            for c, (sub, h) in enumerate(chains):
                log_after = totals[c] - (sums[c][:, :TK] + before[c])
                w.append(jnp.where(mask[sub], jnp.exp((z[c] + ls[c]) + log_after), 0.0))
                dl.append(dw[c] * w[c])
                dsums.append(_dot_hilo(dl[c], before_tri))
            dvs = [lax.dot_general(jnp.concatenate([w[2 * sub + h] for sub in range(NSUB)], axis=0).astype(BF16),
                                   dob, contract_rows, preferred_element_type=F32) for h in range(2)]
            dz = []
            for c, (sub, h) in enumerate(chains):
                beta = jnp.where(mask[sub], jnp.exp(z[c] + ls[c]), 0.0)
                dstay = dsums[c][:, :TK] + dbefore[c]
                dz.append(((dl[c] * (1.0 - beta) - beta * dstay) * SCALE).astype(BF16))
            dqs = [jnp.dot(dz[c], kb, preferred_element_type=F32) for c in range(nch)]
            dks = [lax.dot_general(jnp.concatenate([dz[2 * sub + h] for sub in range(NSUB)], axis=0), qb,
                                   contract_rows, preferred_element_type=F32) for h in range(2)]
            dk_ref[pl.ds(start, TK), :] += _join_heads(*dks)
            dv_ref[pl.ds(start, TK), :] += _join_heads(*dvs)
            dq = tuple(dq[sub] + _join_heads(dqs[2 * sub], dqs[2 * sub + 1]) for sub in range(NSUB))
            return (dq, tuple(before[c] + sums[c][:, TK:] for c in range(nch)),
                    tuple(dbefore[c] + dsums[c][:, TK:] for c in range(nch)))

        zeros = tuple(jnp.zeros((RS, TK), F32) for _ in chains)
        init = (tuple(jnp.zeros((RS, LANES), F32) for _ in range(NSUB)), zeros, zeros)
        dq, _, _ = lax.fori_loop(0, (i + 1) * (TQ // TK), step, init)
        for sub in range(NSUB):
            dq_ref[pl.ds(sub * RS, RS), :] = dq[sub]
        if phases is not None:
            pl.when(jnp.logical_and(p == NPAIR - 1, i == n_steps - 1))(phases[2])

    full = jax.ShapeDtypeStruct((S, NH * HD), F32)
    kwargs, operands = _with_comm(
        comm, [Q_ROWS_SPEC, K_ALL_SPEC, V_ALL_SPEC, PAIR_ROWS_SPEC, PAIR_TOTAL_SPEC],
        [PAIR_ROWS_SPEC, PAIR_ALL_SPEC, PAIR_ALL_SPEC], [full, full, full], [qkv, qkv, qkv, dout, totals])
    return pl.pallas_call(
        body, name=name, grid=(NPAIR, n_steps),
        compiler_params=_cparams(("arbitrary", "arbitrary")), **kwargs,
    )(*operands)


def _proj_cols(first_col):
    base = first_col // LANES
    return pl.BlockSpec((S, LANES), lambda j: (0, base + j))


CONV_OUT_SPEC = pl.BlockSpec((S, LANES), lambda j: (0, j))
CONV_DOUT_SPEC = pl.BlockSpec((S, LANES), lambda j: (0, (NH * HD) // LANES + j))
CONV_W_SPEC = pl.BlockSpec((8, LANES), lambda j: (0, j))
CONV_B_SPEC = pl.BlockSpec((1, LANES), lambda j: (0, j))


def _shift_down(u, n):
    rows = lax.broadcasted_iota(jnp.int32, u.shape, 0)
    return jnp.where(rows >= n, pltpu.roll(u, n, 0), 0.0)


def _shift_up(u, n):
    rows = lax.broadcasted_iota(jnp.int32, u.shape, 0)
    return jnp.where(rows < S - n, pltpu.roll(u, S - n, 0), 0.0)


def conv_fwd(proj, cw8, cb, name):
    def body(bg_ref, cg_ref, hc_ref, w_ref, b_ref, o_ref):
        u = cg_ref[...] * hc_ref[...]
        w = w_ref[...]
        y = w[0:1, :] * _shift_down(u, 2) + w[1:2, :] * _shift_down(u, 1) + w[2:3, :] * u + b_ref[...]
        o_ref[...] = bg_ref[...] * y

    return pl.pallas_call(
        body, name=name, grid=(CW // LANES,),
        in_specs=[_proj_cols(0), _proj_cols(CW), _proj_cols(2 * CW), CONV_W_SPEC, CONV_B_SPEC],
        out_specs=CONV_OUT_SPEC, out_shape=jax.ShapeDtypeStruct((S, CW), F32),
        compiler_params=_cparams(("parallel",)),
    )(proj, proj, proj, cw8, cb)


def conv_bwd(proj, dout, cw8, cb, name):
    def body(bg_ref, cg_ref, hc_ref, do_ref, w_ref, b_ref, dbg_ref, dcg_ref, dhc_ref, dw_ref, db_ref):
        cg, hc, do = cg_ref[...], hc_ref[...], do_ref[...]
        w = w_ref[...]
        u = cg * hc
        u1, u2 = _shift_down(u, 1), _shift_down(u, 2)
        y = w[0:1, :] * u2 + w[1:2, :] * u1 + w[2:3, :] * u + b_ref[...]
        dbg_ref[...] = do * y
        dy = do * bg_ref[...]
        db_ref[...] = jnp.sum(dy, axis=0, keepdims=True)
        dw_ref[...] = jnp.concatenate(
            [jnp.sum(dy * u2, axis=0, keepdims=True), jnp.sum(dy * u1, axis=0, keepdims=True),
             jnp.sum(dy * u, axis=0, keepdims=True), jnp.zeros((5, LANES), F32)], axis=0)
        du = w[2:3, :] * dy + w[1:2, :] * _shift_up(dy, 1) + w[0:1, :] * _shift_up(dy, 2)
        dcg_ref[...] = du * hc
        dhc_ref[...] = du * cg

    full = jax.ShapeDtypeStruct((S, CW), F32)
    return pl.pallas_call(
        body, name=name, grid=(CW // LANES,),
        in_specs=[_proj_cols(0), _proj_cols(CW), _proj_cols(2 * CW), CONV_DOUT_SPEC, CONV_W_SPEC, CONV_B_SPEC],
        out_specs=[CONV_OUT_SPEC, CONV_OUT_SPEC, CONV_OUT_SPEC, CONV_W_SPEC, CONV_B_SPEC],
        out_shape=[full, full, full, jax.ShapeDtypeStruct((8, CW), F32), jax.ShapeDtypeStruct((1, CW), F32)],
        compiler_params=_cparams(("parallel",)),
    )(proj, proj, proj, dout, cw8, cb)


GELU_K = math.sqrt(2.0 / math.pi)
GELU_C = 0.044715


def _gelu(x):
    return 0.5 * x * (1.0 + jnp.tanh(GELU_K * (x + GELU_C * (x * x * x))))


def _gelu_grad(x):
    t = jnp.tanh(GELU_K * (x + GELU_C * (x * x * x)))
    return 0.5 * (1.0 + t) + 0.5 * x * (1.0 - t * t) * (GELU_K * (1.0 + 3.0 * GELU_C * (x * x)))


def _sg_masks():
    row = lax.broadcasted_iota(jnp.int32, (T, T), 0)
    col = lax.broadcasted_iota(jnp.int32, (T, T), 1)
    causal = jnp.right_shift(row, 6) >= jnp.right_shift(col, 6)
    head_of_col = jnp.right_shift(lax.broadcasted_iota(jnp.int32, (T, CW), 1), 6)
    return causal, head_of_col


def _sg_mixed(vnb, sw_ref, bias, causal, head_of_col):
    mixed = bias
    for h in range(SG_HEADS):
        wh = jnp.where(causal, sw_ref[h], 0.0).astype(BF16)
        mh = jnp.dot(wh, vnb, preferred_element_type=F32)
        mixed = mixed + jnp.where(head_of_col == h, mh, 0.0)
    return mixed


SG_U_SPEC = pl.BlockSpec((T, CW), lambda n: (n, 3))
SG_V_SPEC = pl.BlockSpec((T, CW), lambda n: (n, 4))
SG_ROW_SPEC = pl.BlockSpec((T, CW), lambda n: (n, 0))
SG_DOUT_SPEC = pl.BlockSpec((T, CW), lambda n: (n, 3))
SG_G_SPEC = pl.BlockSpec((1, CW), lambda n: (0, 0))
SG_W_SPEC = pl.BlockSpec((SG_HEADS, T, T), lambda n: (0, 0, 0))
SG_BIAS_SPEC = pl.BlockSpec((T, CW), lambda n: (0, 0))


def sg_fwd(proj, gn, sw, bias, name):
    def body(u_ref, v_ref, g_ref, sw_ref, bias_ref, o_ref):
        causal, head_of_col = _sg_masks()
        gv = _gelu(v_ref[...])
        rstd = lax.rsqrt(jnp.mean(gv * gv, axis=-1, keepdims=True) + EPS)
        vnb = ((gv * rstd) * g_ref[...]).astype(BF16)
        mixed = _sg_mixed(vnb, sw_ref, bias_ref[...], causal, head_of_col)
        o_ref[...] = _gelu(u_ref[...]) * mixed

    return pl.pallas_call(
        body, name=name, grid=(S // T,),
        in_specs=[SG_U_SPEC, SG_V_SPEC, SG_G_SPEC, SG_W_SPEC, SG_BIAS_SPEC],
        out_specs=SG_ROW_SPEC, out_shape=jax.ShapeDtypeStruct((S, CW), F32),
        compiler_params=_cparams(("parallel",)),
    )(proj, proj, gn, sw, bias)


def sg_bwd(proj, dout, gn, sw, bias, name):
    def body(u_ref, v_ref, do_ref, g_ref, sw_ref, bias_ref, du_ref, dv_ref, dg_ref, dsw_ref, dbias_ref):
        @pl.when(pl.program_id(0) == 0)
        def _():
            dg_ref[...] = jnp.zeros_like(dg_ref)
            dsw_ref[...] = jnp.zeros_like(dsw_ref)
            dbias_ref[...] = jnp.zeros_like(dbias_ref)

        causal, head_of_col = _sg_masks()
        uv, vv, do, gnv = u_ref[...], v_ref[...], do_ref[...], g_ref[...]
        gv = _gelu(vv)
        rstd = lax.rsqrt(jnp.mean(gv * gv, axis=-1, keepdims=True) + EPS)
        xhat = gv * rstd
        vnb = (xhat * gnv).astype(BF16)
        mixed = _sg_mixed(vnb, sw_ref, bias_ref[...], causal, head_of_col)
        du_ref[...] = (do * mixed) * _gelu_grad(uv)
        dmix = do * _gelu(uv)
        dbias_ref[...] += dmix
        dmixb = dmix.astype(BF16)
        dvn = jnp.zeros((T, CW), F32)
        for h in range(SG_HEADS):
            wh = jnp.where(causal, sw_ref[h], 0.0).astype(BF16)
            dvh = lax.dot_general(wh, dmixb, (((0,), (0,)), ((), ())), preferred_element_type=F32)
            dvn = dvn + jnp.where(head_of_col == h, dvh, 0.0)
            dmh = jnp.where(head_of_col == h, dmixb, jnp.zeros_like(dmixb))
            dwh = lax.dot_general(dmh, vnb, (((1,), (1,)), ((), ())), preferred_element_type=F32)
            dsw_ref[h] += jnp.where(causal, dwh, 0.0)
        dg_ref[...] += jnp.sum(dvn * xhat, axis=0, keepdims=True)
        dxhat = dvn * gnv
        dgv = rstd * (dxhat - xhat * jnp.mean(dxhat * xhat, axis=-1, keepdims=True))
        dv_ref[...] = dgv * _gelu_grad(vv)

    full = jax.ShapeDtypeStruct((S, CW), F32)
    return pl.pallas_call(
        body, name=name, grid=(S // T,),
        in_specs=[SG_U_SPEC, SG_V_SPEC, SG_DOUT_SPEC, SG_G_SPEC, SG_W_SPEC, SG_BIAS_SPEC],
        out_specs=[SG_ROW_SPEC, SG_ROW_SPEC, SG_G_SPEC, SG_W_SPEC, SG_BIAS_SPEC],
        out_shape=[full, full, jax.ShapeDtypeStruct((1, CW), F32),
                   jax.ShapeDtypeStruct((SG_HEADS, T, T), F32), jax.ShapeDtypeStruct((T, CW), F32)],
        compiler_params=_cparams(("arbitrary",)),
    )(proj, proj, dout, gn, sw, bias)


ADA_COLS = NMOD * D // NDEV


def ada_fwd(c_all, ada_w, ada_b_mine, name):
    def body(c_ref, w_ref, b_ref, o_ref, ca_ref):
        cv = c_ref[...]
        ca = cv * (1.0 / (1.0 + jnp.exp(-cv)))
        ca_ref[...] = ca
        cab = ca.astype(BF16)
        for l in range(L):
            o_ref[l] = jnp.dot(cab, w_ref[l].astype(BF16), preferred_element_type=F32) + b_ref[l]

    return pl.pallas_call(
        body, name=name,
        out_shape=[jax.ShapeDtypeStruct((L, NDEV, ADA_COLS), F32), jax.ShapeDtypeStruct((NDEV, D), F32)],
        compiler_params=_cparams(),
    )(c_all, ada_w, ada_b_mine)


def ada_bwd(ca, dmod_cols, name):
    def body(ca_ref, dm_ref, o_ref):
        cab = ca_ref[...].astype(BF16)
        for l in range(L):
            o_ref[l] = lax.dot_general(cab, dm_ref[l].astype(BF16), (((0,), (0,)), ((), ())),
                                       preferred_element_type=F32)

    return pl.pallas_call(
        body, name=name, out_shape=jax.ShapeDtypeStruct((L, D, ADA_COLS), F32),
        compiler_params=_cparams(),
    )(ca, dmod_cols)


def _adamw(w, g, m, v):
    m = B1 * m + (1.0 - B1) * g
    v = B2 * v + (1.0 - B2) * (g * g)
    m_hat = m / BC1
    v_hat = v / BC2
    delta = -LR * (m_hat / (jnp.sqrt(v_hat) + AEPS) + WD * w)
    return delta, m, v


def sum_gathered(parts, name):
    _, rows, cols = parts.shape

    def body(p_ref, o_ref):
        acc = p_ref[0]
        for d in range(1, NDEV):
            acc = acc + p_ref[d]
        o_ref[...] = acc

    return pl.pallas_call(
        body, name=name, out_shape=jax.ShapeDtypeStruct((rows, cols), F32),
        compiler_params=_cparams(),
    )(parts)


def adamw_plain(w, g, m, v, tr, name):
    rows, cols = w.shape
    spec = pl.BlockSpec((tr, cols), lambda i: (i, 0))

    def body(w_ref, g_ref, m_ref, v_ref, d_ref, nm_ref, nv_ref):
        delta, nm, nv = _adamw(w_ref[...], g_ref[...], m_ref[...], v_ref[...])
        d_ref[...] = delta
        nm_ref[...] = nm
        nv_ref[...] = nv

    shp = jax.ShapeDtypeStruct((rows, cols), F32)
    return pl.pallas_call(
        body, name=name, grid=(rows // tr,), in_specs=[spec] * 4, out_specs=[spec] * 3,
        out_shape=[shp, shp, shp], compiler_params=_cparams(("parallel",)),
    )(w, g, m, v)


def adamw_reduce(w, parts, m, v, tr, name):
    _, rows, cols = w.shape
    spec = pl.BlockSpec((None, tr, cols), lambda l, i: (l, i, 0))
    pspecs = [pl.BlockSpec((NDEV, tr, cols), lambda l, i, k=k: (0, jnp.where(l == k, i, 0), 0)) for k in range(L)]

    def body(w_ref, p0_ref, p1_ref, m_ref, v_ref, g_ref, d_ref, nm_ref, nv_ref):
        first_layer = pl.program_id(0) == 0
        g = jnp.zeros((tr, cols), F32)
        for d in range(NDEV):
            g = g + jnp.where(first_layer, p0_ref[d], p1_ref[d]).astype(F32)
        delta, nm, nv = _adamw(w_ref[...], g, m_ref[...], v_ref[...])
        g_ref[...] = g
        d_ref[...] = delta
        nm_ref[...] = nm
        nv_ref[...] = nv

    shp = jax.ShapeDtypeStruct(w.shape, F32)
    return pl.pallas_call(
        body, name=name, grid=(L, rows // tr), in_specs=[spec] + pspecs + [spec, spec], out_specs=[spec] * 4,
        out_shape=[shp] * 4, compiler_params=_cparams(("parallel", "parallel")),
    )(w, *parts, m, v)


def _pad_rows(flat, rows):
    return jnp.pad(flat, (0, rows * LANES - flat.shape[0])).reshape(rows, LANES)


def kernel(x, c, ada_w, ada_b, norm_mix_g, norm_mlp_g, w_in, conv_w, conv_b, gmlp_norm_g, spatial_w, spatial_b, w_out, mlp_w1, mlp_w2, final_norm_g, loss_target, m_ada_w, m_ada_b, m_norm_mix_g, m_norm_mlp_g, m_w_in, m_conv_w, m_conv_b, m_gmlp_norm_g, m_spatial_w, m_spatial_b, m_w_out, m_mlp_w1, m_mlp_w2, m_final_norm_g, v_ada_w, v_ada_b, v_norm_mix_g, v_norm_mlp_g, v_w_in, v_conv_w, v_conv_b, v_gmlp_norm_g, v_spatial_w, v_spatial_b, v_w_out, v_mlp_w1, v_mlp_w2, v_final_norm_g):
    me = _lin(_my_pos())
    x0 = x[0]
    target = loss_target[0]
    conv_shard = conv_w.shape[-1]

    w_in_b, w_out_b, w1_b, w2_b = [w.astype(BF16) for w in (w_in, w_out, mlp_w1, mlp_w2)]
    pack0 = _pad_rows(jnp.concatenate([c.reshape(-1), conv_w.reshape(-1)]), 16)
    g0, gw_in0 = run_comm(Gather([pack0, w_in_b[0]]), "gather_first")
    g0 = g0.reshape(NDEV, 16 * LANES)
    c_all = g0[:, :D]
    conv_full = (g0[:, D:D + L * 3 * conv_shard].reshape(NDEV, L, 3, conv_shard)
                 .transpose(1, 2, 0, 3).reshape(L, 3, CW))

    def canonical_w_in(gathered):
        return gathered.transpose(1, 0, 2).reshape(D, PROJ)

    weight_plans = [Gather([w_out_b[0], w1_b[0], w2_b[0], w_in_b[1]]), Gather([w_out_b[1], w1_b[1], w2_b[1]])]
    W_in = [canonical_w_in(gw_in0), None]
    W_out, W1, W2 = [None] * L, [None] * L, [None] * L

    ada_b_mine = lax.dynamic_slice(ada_b, (0, me * ADA_COLS), (L, ADA_COLS)).reshape(L, 1, ADA_COLS)
    mod_part, c_act = ada_fwd(c_all, ada_w, ada_b_mine, "ada_fwd")
    gmod = run_comm(Gather([mod_part]), "gather_mod")[0]
    mod = lax.dynamic_index_in_dim(gmod, me, axis=2, keepdims=False)
    mod = mod.transpose(1, 0, 2).reshape(L, NMOD, 1, D)

    cw8 = jnp.pad(conv_full, ((0, 0), (0, 5), (0, 0)))
    sg_bias = jnp.repeat(spatial_b.transpose(0, 2, 1), HD, axis=2)

    saved = []
    xl = x0
    for l in range(L):
        sh_m, sc_m, g_m, sh_f, sc_f, g_f = [mod[l, k] for k in range(NMOD)]
        h1 = normmod_fwd(xl, norm_mix_g[l:l + 1], sc_m, sh_m, f"norm_mix_fwd{l}")
        qkv = mm(h1, W_in[l], tm=512, tn=512, out_dtypes=[BF16], cols=(0, QKV), name=f"proj_qkv{l}")[0]
        proj = mm(h1, W_in[l], tm=512, tn=256, out_dtypes=[F32], cols=(QKV, REST), name=f"proj_rest{l}")[0]
        a_out, a_tot, *gathered = attn_fwd(qkv, f"attn_fwd{l}", comm=weight_plans[l])
        W_out[l] = gathered[0].reshape(D, D)
        W1[l] = gathered[1]
        W2[l] = gathered[2].reshape(DFF, D)
        if l + 1 < L:
            W_in[l + 1] = canonical_w_in(gathered[3])
        c_out = conv_fwd(proj, cw8[l], conv_b[l:l + 1], f"conv_fwd{l}")
        s_out = sg_fwd(proj, gmlp_norm_g[l:l + 1], spatial_w[l], sg_bias[l], f"sg_fwd{l}")
        cat = jnp.concatenate([a_out, c_out.astype(BF16), s_out.astype(BF16)], axis=1)
        mix, x1 = mm(cat, W_out[l], tm=512, tn=512, out_dtypes=[F32, F32],
                     epilogue=lambda acc, xr, g: (acc, xr + g * acc),
                     extras=[(xl, "tile"), (g_m, "col")], name=f"mix{l}")
        h2 = normmod_fwd(x1, norm_mlp_g[l:l + 1], sc_f, sh_f, f"norm_mlp_fwd{l}")
        a, r = mm(h2, W1[l], tm=512, tn=512, out_dtypes=[F32, BF16], b_blocks=True,
                  epilogue=lambda acc: (acc, jnp.square(jnp.maximum(acc, 0.0))), name=f"mlp_up{l}")
        m2, x2 = mm(r, W2[l], tm=512, tn=512, out_dtypes=[F32, F32],
                    epilogue=lambda acc, xr, g: (acc, xr + g * acc),
                    extras=[(x1, "tile"), (g_f, "col")], name=f"mlp_down{l}")
        saved.append(dict(x=xl, h1=h1, proj=proj, qkv=qkv, a_tot=a_tot, cat=cat, mix=mix,
                          x1=x1, h2=h2, a=a, r=r, m2=m2))
        xl = x2

    dx, loss_part, d_final_g = loss_head(xl, target, final_norm_g.reshape(1, D), "loss_head")

    dmod = [None] * L
    p_in, p_out, p_w1, p_w2 = [None] * L, [None] * L, [None] * L, [None] * L
    pending_w_in = None
    d_norm_mix, d_norm_mlp, d_conv_w, d_conv_b = [None] * L, [None] * L, [None] * L, [None] * L
    d_gn, d_sw, d_sb = [None] * L, [None] * L, [None] * L
    for l in reversed(range(L)):
        sv = saved[l]
        sh_m, sc_m, g_m, sh_f, sc_f, g_f = [mod[l, k] for k in range(NMOD)]
        dm2, dg_f = gate_bwd(dx, sv["m2"], g_f, f"gate_mlp_bwd{l}")
        da = mm(dm2, W2[l], tm=512, tn=1024, out_dtypes=[BF16], trans_b=True,
                epilogue=lambda acc, av: (acc * (2.0 * jnp.maximum(av, 0.0)),),
                extras=[(sv["a"], "tile")], name=f"mlp_down_dgrad{l}")[0]
        dW2 = mm(sv["r"], dm2, tm=512, tn=1024, out_dtypes=[BF16], trans_a=True, name=f"mlp_down_wgrad{l}")[0]
        dW1 = mm(sv["h2"], da, tm=512, tn=512, out_dtypes=[BF16], trans_a=True, out_blocks=True,
                 name=f"mlp_up_wgrad{l}")[0]
        dh2 = mm(da, W1[l], tm=512, tn=512, out_dtypes=[F32], trans_b=True, b_blocks=True,
                 name=f"mlp_up_dgrad{l}")[0]
        dx1, dsc_f, dsh_f, d_norm_mlp[l] = normmod_bwd(sv["x1"], dh2, dx, norm_mlp_g[l:l + 1], sc_f,
                                                       f"norm_mlp_bwd{l}")
        dmix, dg_m = gate_bwd(dx1, sv["mix"], g_m, f"gate_mix_bwd{l}")
        dcat = mm(dmix, W_out[l], tm=512, tn=512, out_dtypes=[F32], trans_b=True, name=f"mix_dgrad{l}")[0]
        dW_out = mm(sv["cat"], dmix, tm=512, tn=1024, out_dtypes=[BF16], trans_a=True, name=f"mix_wgrad{l}")[0]
        ready = [dW2.reshape(NDEV, DFF // NDEV, D), dW1, dW_out.reshape(NDEV, D // NDEV, D)]
        if pending_w_in is not None:
            ready = [pending_w_in] + ready
        dq, dk, dv, *arrived = attn_bwd(sv["qkv"], dcat, sv["a_tot"], f"attn_bwd{l}", comm=Exchange(ready))
        if pending_w_in is not None:
            p_in[l + 1] = arrived.pop(0)
        p_w2[l], p_w1[l], p_out[l] = arrived
        dbg, dcg, dhc, dcw8, d_conv_b[l] = conv_bwd(sv["proj"], dcat, cw8[l], conv_b[l:l + 1], f"conv_bwd{l}")
        d_conv_w[l] = dcw8[:3]
        dus, dvs, d_gn[l], dsw, dbias = sg_bwd(sv["proj"], dcat, gmlp_norm_g[l:l + 1],
                                               spatial_w[l], sg_bias[l], f"sg_bwd{l}")
        d_sw[l] = dsw
        d_sb[l] = dbias.reshape(T, SG_HEADS, HD).sum(axis=2).T
        dproj = jnp.concatenate([dq, dk, dv, dbg, dcg, dhc, dus, dvs], axis=1).astype(BF16)
        dW_in = mm(sv["h1"], dproj, tm=512, tn=PROJ // 2, out_dtypes=[BF16], trans_a=True,
                   name=f"proj_wgrad{l}")[0]
        pending_w_in = dW_in.reshape(D, NDEV, PROJ // NDEV).transpose(1, 0, 2)
        dh1 = mm(dproj, W_in[l], tm=512, tn=512, out_dtypes=[F32], trans_b=True, name=f"proj_dgrad{l}")[0]
        dx, dsc_m, dsh_m, d_norm_mix[l] = normmod_bwd(sv["x"], dh1, dx1, norm_mix_g[l:l + 1], sc_m,
                                                      f"norm_mix_bwd{l}")
        dmod[l] = jnp.concatenate([dsh_m, dsc_m, dg_m, dsh_f, dsc_f, dg_f], axis=1)

    grad_x = dx.reshape(1, S, D)

    small_parts = [jnp.concatenate(dmod, axis=0), jnp.concatenate(d_norm_mix, axis=0),
                   jnp.concatenate(d_norm_mlp, axis=0), jnp.stack(d_conv_w), jnp.concatenate(d_conv_b, axis=0),
                   jnp.concatenate(d_gn, axis=0), jnp.stack(d_sw), jnp.stack(d_sb), d_final_g, loss_part[:, :1]]
    sizes = [p.size for p in small_parts]
    small_rows = -(-sum(sizes) // (8 * LANES)) * 8
    small_pack = _pad_rows(jnp.concatenate([p.reshape(-1) for p in small_parts]), small_rows)
    small_all = run_comm(Gather([small_pack]), "gather_small_grads")[0]
    small_sum = sum_gathered(small_all, "sum_small_grads").reshape(-1)
    offs = [0]
    for sz in sizes:
        offs.append(offs[-1] + sz)
    summed = [small_sum[offs[k]:offs[k + 1]].reshape(small_parts[k].shape) for k in range(len(sizes))]
    (g_ada_b, g_norm_mix, g_norm_mlp, g_conv_w_full, g_conv_b, g_gn, g_sw, g_sb, g_final, loss_sum) = summed
    loss = loss_sum.reshape(())
    g_final = g_final.reshape(D)
    g_conv_w = lax.dynamic_slice(g_conv_w_full, (0, 0, me * conv_shard), (L, 3, conv_shard))

    dmod_all = small_all.reshape(NDEV, -1)[:, :L * NMOD * D].reshape(NDEV, L, NMOD * D)
    dmod_cols = lax.dynamic_slice(dmod_all, (0, 0, me * ADA_COLS), (NDEV, L, ADA_COLS)).transpose(1, 0, 2)
    g_ada_w = ada_bwd(c_act, dmod_cols, "ada_bwd")

    p_in[0] = run_comm(Exchange([pending_w_in]), "exchange_last")[0]
    g_w_in, d_w_in, nm_w_in, nv_w_in = adamw_reduce(w_in, p_in, m_w_in, v_w_in, 256, "adamw_w_in")
    g_w_out, d_w_out, nm_w_out, nv_w_out = adamw_reduce(w_out, p_out, m_w_out, v_w_out, 128, "adamw_w_out")
    g_w1, d_w1, nm_w1, nv_w1 = adamw_reduce(mlp_w1, p_w1, m_mlp_w1, v_mlp_w1, 256, "adamw_mlp_w1")
    g_w2, d_w2, nm_w2, nv_w2 = adamw_reduce(mlp_w2, p_w2, m_mlp_w2, v_mlp_w2, 256, "adamw_mlp_w2")

    flat2 = lambda t: t.reshape(L * D, ADA_COLS)
    d_ada_w, nm_ada_w, nv_ada_w = [t.reshape(L, D, ADA_COLS) for t in adamw_plain(
        flat2(ada_w), flat2(g_ada_w), flat2(m_ada_w), flat2(v_ada_w), 256, "adamw_ada_w")]

    small_w = [ada_b, norm_mix_g, norm_mlp_g, conv_w, conv_b, gmlp_norm_g, spatial_w, spatial_b, final_norm_g]
    small_m = [m_ada_b, m_norm_mix_g, m_norm_mlp_g, m_conv_w, m_conv_b, m_gmlp_norm_g, m_spatial_w, m_spatial_b,
               m_final_norm_g]
    small_v = [v_ada_b, v_norm_mix_g, v_norm_mlp_g, v_conv_w, v_conv_b, v_gmlp_norm_g, v_spatial_w, v_spatial_b,
               v_final_norm_g]
    small_g = [g_ada_b, g_norm_mix, g_norm_mlp, g_conv_w, g_conv_b, g_gn, g_sw, g_sb, g_final]
    wsizes = [p.size for p in small_w]
    wrows = -(-sum(wsizes) // (8 * LANES)) * 8
    pack = lambda ps: _pad_rows(jnp.concatenate([p.reshape(-1) for p in ps]), wrows)
    sd, snm, snv = adamw_plain(pack(small_w), pack(small_g), pack(small_m), pack(small_v), wrows, "adamw_small")
    woffs = [0]
    for sz in wsizes:
        woffs.append(woffs[-1] + sz)

    def unpack(flat):
        flat = flat.reshape(-1)
        return [flat[woffs[k]:woffs[k + 1]].reshape(small_w[k].shape) for k in range(len(small_w))]

    sd, snm, snv = unpack(sd), unpack(snm), unpack(snv)

    def ordered(big, small):
        ada, win, wout, w1, w2 = big
        return [ada, small[0], small[1], small[2], win, small[3], small[4], small[5], small[6], small[7],
                wout, w1, w2, small[8]]

    grads = ordered([g_ada_w, g_w_in, g_w_out, g_w1, g_w2], small_g)
    deltas = ordered([d_ada_w, d_w_in, d_w_out, d_w1, d_w2], sd)
    new_m = ordered([nm_ada_w, nm_w_in, nm_w_out, nm_w1, nm_w2], snm)
    new_v = ordered([nv_ada_w, nv_w_in, nv_w_out, nv_w1, nv_w2], snv)
    return (loss, grad_x, *grads, *deltas, *new_m, *new_v)
```

```python
import functools
import math

import jax
import jax.numpy as jnp
from jax import lax
from jax.experimental import pallas as pl
from jax.experimental.pallas import tpu as pltpu

F32 = jnp.float32
BF16 = jnp.bfloat16
MESH = pl.DeviceIdType.MESH

S = 2048
D = 1024
L = 2
NDEV = 8
HD = 64
NH = 8
PROJ = 2816
DFF = 4096
NMOD = 6
EPS = 1e-6
T = 128
SG_HEADS = 4
LANES = 128
CW = 256
QKV = 3 * NH * HD
REST = PROJ - QKV

LR, B1, B2, AEPS, WD, STEP = 0.001, 0.9, 0.999, 1e-08, 0.01, 10
BC1 = 1.0 - B1 ** STEP
BC2 = 1.0 - B2 ** STEP

VMEM_LIMIT = 48 * 1024 * 1024

HBM_SPEC = pl.BlockSpec(memory_space=pltpu.HBM)


def _cparams(sem=None):
    return pltpu.CompilerParams(dimension_semantics=sem, vmem_limit_bytes=VMEM_LIMIT)


def _my_pos():
    return lax.axis_index("x"), lax.axis_index("y"), lax.axis_index("c")


def _lin(p):
    return 4 * p[0] + 2 * p[1] + p[2]


class Gather:
    def __init__(self, arrs):
        self.arrs = list(arrs)
        n = len(self.arrs)
        self.out_shape = [jax.ShapeDtypeStruct((NDEV,) + a.shape, a.dtype) for a in self.arrs]
        self.scratch = [pltpu.SemaphoreType.DMA((n, 7)), pltpu.SemaphoreType.DMA((n, 7)),
                        pltpu.SemaphoreType.DMA((n,))]

    def phases(self, ins, outs, sems):
        n = len(self.arrs)
        send_sems, recv_sems, local_sems = sems
        x, y, c = _my_pos()
        me, sibling = (x, y, c), (x, y, 1 - c)
        chips = [(1 - x, y), (x, 1 - y), (1 - x, 1 - y)]

        def copy(a, k, block, to, src=None):
            slot = outs[a].at[_lin(block)]
            return pltpu.make_async_remote_copy(
                src_ref=slot if src is None else src, dst_ref=slot,
                send_sem=send_sems.at[a, k], recv_sem=recv_sems.at[a, k],
                device_id=to, device_id_type=MESH)

        def mine(a):
            return pltpu.make_async_copy(ins[a], outs[a].at[_lin(me)], local_sems.at[a])

        def first(a):
            return [copy(a, 0, me, sibling, src=ins[a])] + [
                copy(a, 1 + j, me, (*chip, c), src=ins[a]) for j, chip in enumerate(chips)]

        def passed(a):
            return [copy(a, 4 + j, (*chip, c), sibling) for j, chip in enumerate(chips)]

        def start():
            for a in range(n):
                mine(a).start()
                for cp in first(a):
                    cp.start()

        def relay():
            for j, chip in enumerate(chips):
                for a in range(n):
                    copy(a, 1 + j, (*chip, c), me).wait_recv()
                    passed(a)[j].start()

        def finish():
            for a in range(n):
                copy(a, 0, sibling, me).wait_recv()
            for j, chip in enumerate(chips):
                for a in range(n):
                    copy(a, 4 + j, (*chip, 1 - c), me).wait_recv()
            for a in range(n):
                for cp in first(a) + passed(a):
                    cp.wait_send()
                mine(a).wait()

        return start, relay, finish


class Exchange:
    def __init__(self, arrs):
        self.arrs = list(arrs)
        n = len(self.arrs)
        self.out_shape = [jax.ShapeDtypeStruct(a.shape, a.dtype) for a in self.arrs]
        self.scratch = [pltpu.SemaphoreType.DMA((n, 7)), pltpu.SemaphoreType.DMA((n, 7)),
                        pltpu.SemaphoreType.DMA((n,))]

    def phases(self, ins, outs, sems):
        n = len(self.arrs)
        send_sems, recv_sems, local_sems = sems
        x, y, c = _my_pos()
        me = (x, y, c)

        def peer(mask):
            return (1 - x if mask & 4 else x, 1 - y if mask & 2 else y, 1 - c if mask & 1 else c)

        def copy(a, mask):
            return pltpu.make_async_remote_copy(
                src_ref=ins[a].at[_lin(peer(mask))], dst_ref=outs[a].at[_lin(me)],
                send_sem=send_sems.at[a, mask - 1], recv_sem=recv_sems.at[a, mask - 1],
                device_id=peer(mask), device_id_type=MESH)

        def arrival(a, mask):
            return pltpu.make_async_remote_copy(
                src_ref=ins[a].at[_lin(me)], dst_ref=outs[a].at[_lin(peer(mask))],
                send_sem=send_sems.at[a, mask - 1], recv_sem=recv_sems.at[a, mask - 1],
                device_id=peer(mask), device_id_type=MESH)

        def mine(a):
            return pltpu.make_async_copy(ins[a].at[_lin(me)], outs[a].at[_lin(me)], local_sems.at[a])

        def start():
            for a in range(n):
                mine(a).start()
            for mask in (4, 2, 6, 1, 5, 3, 7):
                for a in range(n):
                    copy(a, mask).start()

        def relay():
            pass

        def finish():
            for mask in range(1, 8):
                for a in range(n):
                    arrival(a, mask).wait_recv()
            for mask in range(1, 8):
                for a in range(n):
                    copy(a, mask).wait_send()
            for a in range(n):
                mine(a).wait()

        return start, relay, finish


def run_comm(plan, name):
    n = len(plan.arrs)

    def body(*refs):
        start, relay, finish = plan.phases(refs[:n], refs[n:2 * n], refs[2 * n:])
        start()
        relay()
        finish()

    outs = pl.pallas_call(
        body, name=name, out_shape=plan.out_shape,
        in_specs=[HBM_SPEC] * n, out_specs=[HBM_SPEC] * n, scratch_shapes=plan.scratch,
    )(*plan.arrs)
    return list(outs)


MM_TILES = {
    "proj_qkv": (S, 512), "proj_rest": (S, 256), "mix": (1024, 512), "mlp_up": (S, 512), "mlp_down": (1024, 256),
    "mlp_down_dgrad": (1024, 1024), "mlp_down_wgrad": (1024, 1024), "mlp_up_wgrad": (1024, 512),
    "mlp_up_dgrad": (1024, 512), "mix_dgrad": (1024, 512), "mix_wgrad": (512, 1024),
    "proj_wgrad": (1024, PROJ // 2), "proj_dgrad": (1024, 512),
}


def mm_layer(kind, l, a, b, **kw):
    tm, tn = MM_TILES[kind]
    return mm(a, b, tm=tm, tn=tn, name=f"{kind}{l}", **kw)


def mm(a, b, *, tm, tn, out_dtypes, epilogue=None, extras=(), name, trans_a=False, trans_b=False,
       cols=None, b_blocks=False, out_blocks=False):
    if trans_a:
        kdim, m = a.shape
    else:
        m, kdim = a.shape
    shard = b.shape[-1] if b_blocks else None
    if b_blocks:
        full = (b.shape[1], NDEV * shard)
    else:
        full = b.shape
    first, ncols = cols if cols is not None else (0, full[0] if trans_b else full[1])
    assert full[1 if trans_b else 0] == kdim and m % tm == 0 and ncols % tn == 0 and first % tn == 0
    j0 = first // tn
    if trans_a:
        a_spec = pl.BlockSpec((kdim, tm), lambda i, j: (0, i))
    else:
        a_spec = pl.BlockSpec((tm, kdim), lambda i, j: (i, 0))
    if b_blocks and trans_b:
        b_spec = pl.BlockSpec((NDEV, tn, shard), lambda i, j: (0, j0 + j, 0))
    elif b_blocks:
        assert tn == shard
        b_spec = pl.BlockSpec((None, kdim, tn), lambda i, j: (j0 + j, 0, 0))
    elif trans_b:
        b_spec = pl.BlockSpec((tn, kdim), lambda i, j: (j0 + j, 0))
    else:
        b_spec = pl.BlockSpec((kdim, tn), lambda i, j: (0, j0 + j))
    if out_blocks:
        assert tn * NDEV == ncols
        out_spec = pl.BlockSpec((None, tm, tn), lambda i, j: (j, i, 0))
        out_dims = (NDEV, m, tn)
    else:
        out_spec = pl.BlockSpec((tm, tn), lambda i, j: (i, j))
        out_dims = (m, ncols)
    ex_specs = []
    for arr, kind in extras:
        if kind == "tile":
            ex_specs.append(pl.BlockSpec((tm, tn), lambda i, j: (i, j)))
        else:
            ex_specs.append(pl.BlockSpec((1, tn), lambda i, j: (0, j)))
    n_ex, n_out = len(extras), len(out_dtypes)

    def body(a_ref, b_ref, *rest):
        ex_refs, out_refs = rest[:n_ex], rest[n_ex:]
        if trans_a:
            acc = lax.dot_general(a_ref[...], b_ref[...], (((0,), (0,)), ((), ())),
                                  preferred_element_type=F32)
        elif trans_b and b_blocks:
            acc = jnp.zeros((tm, tn), F32)
            for d in range(NDEV):
                acc = acc + lax.dot_general(a_ref[:, d * shard:(d + 1) * shard], b_ref[d],
                                            (((1,), (1,)), ((), ())), preferred_element_type=F32)
        elif trans_b:
            acc = lax.dot_general(a_ref[...], b_ref[...], (((1,), (1,)), ((), ())),
                                  preferred_element_type=F32)
        else:
            acc = jnp.dot(a_ref[...], b_ref[...], preferred_element_type=F32)
        outs = (acc,) if epilogue is None else epilogue(acc, *[r[...] for r in ex_refs])
        for o_ref, val in zip(out_refs, outs):
            o_ref[...] = val.astype(o_ref.dtype)

    outs = pl.pallas_call(
        body, name=name, grid=(m // tm, ncols // tn),
        in_specs=[a_spec, b_spec] + ex_specs,
        out_specs=[out_spec for _ in range(n_out)],
        out_shape=[jax.ShapeDtypeStruct(out_dims, dt) for dt in out_dtypes],
        compiler_params=_cparams(("parallel", "parallel")),
    )(a, b, *[arr for arr, _ in extras])
    return list(outs)


TR = 256

ROW_SPEC = pl.BlockSpec((TR, D), lambda i: (i, 0))
VEC_SPEC = pl.BlockSpec((1, D), lambda i: (0, 0))


def normmod_fwd(x, g, sc, sh, name):
    def body(x_ref, g_ref, sc_ref, sh_ref, o_ref):
        xv = x_ref[...]
        rstd = lax.rsqrt(jnp.mean(xv * xv, axis=-1, keepdims=True) + EPS)
        n = (xv * rstd) * g_ref[...]
        o_ref[...] = (n * (1.0 + sc_ref[...]) + sh_ref[...]).astype(o_ref.dtype)

    return pl.pallas_call(
        body, name=name, grid=(S // TR,),
        in_specs=[ROW_SPEC, VEC_SPEC, VEC_SPEC, VEC_SPEC], out_specs=ROW_SPEC,
        out_shape=jax.ShapeDtypeStruct((S, D), BF16),
        compiler_params=_cparams(("parallel",)),
    )(x, g, sc, sh)


def normmod_bwd(x, dh, dres, g, sc, name):
    def body(x_ref, dh_ref, dres_ref, g_ref, sc_ref, dx_ref, dsc_ref, dsh_ref, dg_ref):
        @pl.when(pl.program_id(0) == 0)
        def _():
            dsc_ref[...] = jnp.zeros_like(dsc_ref)
            dsh_ref[...] = jnp.zeros_like(dsh_ref)
            dg_ref[...] = jnp.zeros_like(dg_ref)

        xv, dh = x_ref[...], dh_ref[...]
        gv = g_ref[...]
        rstd = lax.rsqrt(jnp.mean(xv * xv, axis=-1, keepdims=True) + EPS)
        xhat = xv * rstd
        dn = dh * (1.0 + sc_ref[...])
        dxhat = dn * gv
        dx_ref[...] = dres_ref[...] + rstd * (dxhat - xhat * jnp.mean(dxhat * xhat, axis=-1, keepdims=True))
        dsc_ref[...] += jnp.sum(dh * (xhat * gv), axis=0, keepdims=True)
        dsh_ref[...] += jnp.sum(dh, axis=0, keepdims=True)
        dg_ref[...] += jnp.sum(dn * xhat, axis=0, keepdims=True)

    vec_out = jax.ShapeDtypeStruct((1, D), F32)
    return pl.pallas_call(
        body, name=name, grid=(S // TR,),
        in_specs=[ROW_SPEC, ROW_SPEC, ROW_SPEC, VEC_SPEC, VEC_SPEC],
        out_specs=[ROW_SPEC, VEC_SPEC, VEC_SPEC, VEC_SPEC],
        out_shape=[jax.ShapeDtypeStruct((S, D), F32), vec_out, vec_out, vec_out],
        compiler_params=_cparams(("arbitrary",)),
    )(x, dh, dres, g, sc)


def gate_bwd(dx, branch, gate, name):
    def body(dx_ref, br_ref, gate_ref, o_ref, dgate_ref):
        @pl.when(pl.program_id(0) == 0)
        def _():
            dgate_ref[...] = jnp.zeros_like(dgate_ref)

        dxv = dx_ref[...]
        o_ref[...] = (dxv * gate_ref[...]).astype(o_ref.dtype)
        dgate_ref[...] += jnp.sum(dxv * br_ref[...], axis=0, keepdims=True)

    return pl.pallas_call(
        body, name=name, grid=(S // TR,),
        in_specs=[ROW_SPEC, ROW_SPEC, VEC_SPEC], out_specs=[ROW_SPEC, VEC_SPEC],
        out_shape=[jax.ShapeDtypeStruct((S, D), BF16), jax.ShapeDtypeStruct((1, D), F32)],
        compiler_params=_cparams(("arbitrary",)),
    )(dx, branch, gate)


def loss_head(x, target, g, name):
    def body(x_ref, t_ref, g_ref, dx_ref, loss_ref, dg_ref):
        @pl.when(pl.program_id(0) == 0)
        def _():
            loss_ref[...] = jnp.zeros_like(loss_ref)
            dg_ref[...] = jnp.zeros_like(dg_ref)

        xv, gv = x_ref[...], g_ref[...]
        rstd = lax.rsqrt(jnp.mean(xv * xv, axis=-1, keepdims=True) + EPS)
        xhat = xv * rstd
        err = xhat * gv - t_ref[...]
        loss_ref[...] += jnp.sum(err * err) * (0.5 / D)
        dy = err * (1.0 / D)
        dg_ref[...] += jnp.sum(dy * xhat, axis=0, keepdims=True)
        dxhat = dy * gv
        dx_ref[...] = rstd * (dxhat - xhat * jnp.mean(dxhat * xhat, axis=-1, keepdims=True))

    return pl.pallas_call(
        body, name=name, grid=(S // TR,),
        in_specs=[ROW_SPEC, ROW_SPEC, VEC_SPEC],
        out_specs=[ROW_SPEC, pl.BlockSpec((1, 128), lambda i: (0, 0)), VEC_SPEC],
        out_shape=[jax.ShapeDtypeStruct((S, D), F32), jax.ShapeDtypeStruct((1, 128), F32),
                   jax.ShapeDtypeStruct((1, D), F32)],
        compiler_params=_cparams(("arbitrary",)),
    )(x, target, g)


TQ = 512
RS = 128
NSUB = TQ // RS
TK = 128


def _dot_hilo(a, tri):
    hi = a.astype(BF16)
    lo = (a - hi.astype(F32)).astype(BF16)
    return jnp.dot(hi, tri, preferred_element_type=F32) + jnp.dot(lo, tri, preferred_element_type=F32)


def _log_stay(z):
    return -(jnp.maximum(z, 0.0) + jnp.log(1.0 + jnp.exp(-jnp.abs(z))))


def _tri_and_ones(kind):
    row = lax.broadcasted_iota(jnp.int32, (TK, 2 * TK), 0)
    col = lax.broadcasted_iota(jnp.int32, (TK, 2 * TK), 1)
    tri = {"after": row > col, "upto": row <= col, "before": row < col}[kind]
    return jnp.logical_or(col >= TK, tri).astype(BF16)


NPAIR = NH // 2
SCALE = HD ** -0.5


def _pair_specs(first_block):
    rows = pl.BlockSpec((TQ, LANES), lambda p, i: (i, first_block + p))
    whole = pl.BlockSpec((S, LANES), lambda p, i: (0, first_block + p))
    return rows, whole


Q_ROWS_SPEC, _ = _pair_specs(0)
_, K_ALL_SPEC = _pair_specs(NPAIR)
_, V_ALL_SPEC = _pair_specs(2 * NPAIR)
PAIR_ROWS_SPEC = pl.BlockSpec((TQ, LANES), lambda p, i: (i, p))
PAIR_ALL_SPEC = pl.BlockSpec((S, LANES), lambda p, i: (0, p))
PAIR_TOTAL_SPEC = pl.BlockSpec((2, TQ, TK), lambda p, i: (p, i, 0))


def _head_halves(x):
    first = lax.broadcasted_iota(jnp.int32, x.shape, 1) < HD
    zero = jnp.zeros_like(x)
    return jnp.where(first, x, zero), jnp.where(first, zero, x)


def _join_heads(a, b):
    return jnp.where(lax.broadcasted_iota(jnp.int32, a.shape, 1) < HD, a, b)


def _comm_hooks(comm, refs, n_in, n_out, n_scratch):
    nc = len(comm.arrs) if comm is not None else 0
    ins, cin = refs[:n_in], refs[n_in:n_in + nc]
    outs = refs[n_in + nc:n_in + nc + n_out]
    cout = refs[n_in + nc + n_out:n_in + 2 * nc + n_out]
    scratch = refs[n_in + 2 * nc + n_out:n_in + 2 * nc + n_out + n_scratch]
    sems = refs[n_in + 2 * nc + n_out + n_scratch:]
    phases = comm.phases(cin, cout, sems) if comm is not None else None
    return ins, outs, scratch, phases


def _with_comm(comm, in_specs, out_specs, out_shape, operands, scratch):
    if comm is None:
        return dict(in_specs=in_specs, out_specs=out_specs, out_shape=out_shape, scratch_shapes=scratch), operands
    nc = len(comm.arrs)
    return dict(in_specs=in_specs + [HBM_SPEC] * nc, out_specs=out_specs + [HBM_SPEC] * nc,
                out_shape=out_shape + comm.out_shape, scratch_shapes=scratch + comm.scratch), operands + comm.arrs


def attn_fwd(qkv, name, comm=None):
    n_steps = S // TQ

    def body(*refs):
        (q_ref, k_ref, v_ref), (o_ref, r_ref), (acc_ref, z_even, z_odd, w_ref), phases = _comm_hooks(
            comm, refs, 3, 2, 4)
        p = pl.program_id(0)
        i = pl.program_id(1)
        if phases is not None:
            pl.when(jnp.logical_and(p == 0, i == 0))(phases[0])
            pl.when(jnp.logical_and(p == NPAIR - 1, i == n_steps - 2))(phases[1])
        chains = [(sub, h) for sub in range(NSUB) for h in range(2)]
        q_sub = [_head_halves(q_ref[pl.ds(sub * RS, RS), :]) for sub in range(NSUB)]
        s_off = lax.broadcasted_iota(jnp.int32, (RS, TK), 1)
        t_pos = [i * TQ + sub * RS + lax.broadcasted_iota(jnp.int32, (RS, TK), 0) for sub in range(NSUB)]
        after = _tri_and_ones("after")
        nblk = (i + 1) * (TQ // TK)

        acc_ref[...] = jnp.zeros_like(acc_ref)
        r_ref[...] = jnp.zeros_like(r_ref)

        def key_rows(block):
            return pl.ds(pl.multiple_of(block * TK, TK), TK)

        def store_scores(z_ref, block):
            kb = k_ref[key_rows(block), :]
            for c, (sub, h) in enumerate(chains):
                z_ref[c] = lax.dot_general(q_sub[sub][h], kb, (((1,), (1,)), ((), ())),
                                           preferred_element_type=F32) * SCALE

        def add_weighted_values(block):
            vb = v_ref[key_rows(block), :]
            pv = [jnp.dot(w_ref[c], vb, preferred_element_type=F32) for c in range(len(chains))]
            for sub in range(NSUB):
                acc_ref[pl.ds(sub * RS, RS), :] += _join_heads(pv[2 * sub], pv[2 * sub + 1])

        w_ref[...] = jnp.zeros_like(w_ref)
        store_scores(z_even, nblk - 1)

        def step(block, z_ref, z_next_ref):
            add_weighted_values(jnp.minimum(block + 1, nblk - 1))
            store_scores(z_next_ref, jnp.maximum(block - 1, 0))
            mask = [(block * TK + s_off) < t for t in t_pos]
            ls, sums = [], []
            for c, (sub, h) in enumerate(chains):
                ls.append(_log_stay(z_ref[c]))
                sums.append(_dot_hilo(jnp.where(mask[sub], ls[c], 0.0), after))
            for c, (sub, h) in enumerate(chains):
                rows = pl.ds(sub * RS, RS)
                later = r_ref[h, rows, :]
                w = jnp.where(mask[sub], jnp.exp(z_ref[c] + ls[c] + (sums[c][:, :TK] + later)), 0.0)
                w_ref[c] = w.astype(BF16)
                r_ref[h, rows, :] = later + sums[c][:, TK:]

        @pl.loop(0, nblk // 2)
        def _(pair):
            block = nblk - 1 - 2 * pair
            step(block, z_even, z_odd)
            step(block - 1, z_odd, z_even)

        add_weighted_values(0)
        o_ref[...] = acc_ref[...].astype(o_ref.dtype)
        if phases is not None:
            pl.when(jnp.logical_and(p == NPAIR - 1, i == n_steps - 1))(phases[2])

    kwargs, operands = _with_comm(
        comm, [Q_ROWS_SPEC, K_ALL_SPEC, V_ALL_SPEC], [PAIR_ROWS_SPEC, PAIR_TOTAL_SPEC],
        [jax.ShapeDtypeStruct((S, NH * HD), BF16), jax.ShapeDtypeStruct((NH, S, TK), F32)], [qkv, qkv, qkv],
        [pltpu.VMEM((TQ, LANES), F32), pltpu.VMEM((2 * NSUB, RS, TK), F32), pltpu.VMEM((2 * NSUB, RS, TK), F32),
         pltpu.VMEM((2 * NSUB, RS, TK), BF16)])
    return pl.pallas_call(
        body, name=name, grid=(NPAIR, n_steps),
        compiler_params=_cparams(("arbitrary", "arbitrary")), **kwargs,
    )(*operands)


def attn_bwd(qkv, dout, totals, name, comm=None):
    n_steps = S // TQ

    def body(*refs):
        (q_ref, k_ref, v_ref, do_ref, r_ref), (dq_ref, dk_ref, dv_ref), _, phases = _comm_hooks(comm, refs, 5, 3, 0)
        p = pl.program_id(0)
        i = pl.program_id(1)
        if phases is not None:
            pl.when(jnp.logical_and(p == 0, i == 0))(phases[0])
            pl.when(jnp.logical_and(p == NPAIR - 1, i == n_steps - 2))(phases[1])

        @pl.when(i == 0)
        def _():
            dk_ref[...] = jnp.zeros_like(dk_ref)
            dv_ref[...] = jnp.zeros_like(dv_ref)

        chains = [(sub, h) for sub in range(NSUB) for h in range(2)]
        nch = len(chains)
        qb = q_ref[...]
        dob = do_ref[...].astype(BF16)
        q_sub = [_head_halves(qb[sub * RS:(sub + 1) * RS]) for sub in range(NSUB)]
        do_sub = [_head_halves(dob[sub * RS:(sub + 1) * RS]) for sub in range(NSUB)]
        totals = [r_ref[h, pl.ds(sub * RS, RS), :] for sub, h in chains]
        s_off = lax.broadcasted_iota(jnp.int32, (RS, TK), 1)
        t_pos = [i * TQ + sub * RS + lax.broadcasted_iota(jnp.int32, (RS, TK), 0) for sub in range(NSUB)]
        upto = _tri_and_ones("upto")
        before_tri = _tri_and_ones("before")
        contract_lanes = (((1,), (1,)), ((), ()))
        contract_rows = (((0,), (0,)), ((), ()))

        def step(j, carry):
            dq, before, dbefore = carry
            start = pl.multiple_of(j * TK, TK)
            kb = k_ref[pl.ds(start, TK), :]
            vb = v_ref[pl.ds(start, TK), :]
            mask = [(start + s_off) < t for t in t_pos]
            z = [lax.dot_general(q_sub[sub][h], kb, contract_lanes, preferred_element_type=F32) * SCALE
                 for sub, h in chains]
            dw = [lax.dot_general(do_sub[sub][h], vb, contract_lanes, preferred_element_type=F32)
                  for sub, h in chains]
            ls, sums = [], []
            for c, (sub, h) in enumerate(chains):
                ls.append(_log_stay(z[c]))
                sums.append(_dot_hilo(jnp.where(mask[sub], ls[c], 0.0), upto))
            w, dl, dsums = [], [], []
            for c, (sub, h) in enumerate(chains):
                log_after = totals[c] - (sums[c][:, :TK] + before[c])
                w.append(jnp.where(mask[sub], jnp.exp((z[c] + ls[c]) + log_after), 0.0))
                dl.append(dw[c] * w[c])
                dsums.append(_dot_hilo(dl[c], before_tri))
            dvs = [lax.dot_general(jnp.concatenate([w[2 * sub + h] for sub in range(NSUB)], axis=0).astype(BF16),
                                   dob, contract_rows, preferred_element_type=F32) for h in range(2)]
            dz = []
            for c, (sub, h) in enumerate(chains):
                beta = jnp.where(mask[sub], jnp.exp(z[c] + ls[c]), 0.0)
                dstay = dsums[c][:, :TK] + dbefore[c]
                dz.append(((dl[c] * (1.0 - beta) - beta * dstay) * SCALE).astype(BF16))
            dqs = [jnp.dot(dz[c], kb, preferred_element_type=F32) for c in range(nch)]
            dks = [lax.dot_general(jnp.concatenate([dz[2 * sub + h] for sub in range(NSUB)], axis=0), qb,
                                   contract_rows, preferred_element_type=F32) for h in range(2)]
            dk_ref[pl.ds(start, TK), :] += _join_heads(*dks)
            dv_ref[pl.ds(start, TK), :] += _join_heads(*dvs)
            dq = tuple(dq[sub] + _join_heads(dqs[2 * sub], dqs[2 * sub + 1]) for sub in range(NSUB))
            return (dq, tuple(before[c] + sums[c][:, TK:] for c in range(nch)),
                    tuple(dbefore[c] + dsums[c][:, TK:] for c in range(nch)))

        zeros = tuple(jnp.zeros((RS, TK), F32) for _ in chains)
        init = (tuple(jnp.zeros((RS, LANES), F32) for _ in range(NSUB)), zeros, zeros)
        dq, _, _ = lax.fori_loop(0, (i + 1) * (TQ // TK), step, init)
        for sub in range(NSUB):
            dq_ref[pl.ds(sub * RS, RS), :] = dq[sub]
        if phases is not None:
            pl.when(jnp.logical_and(p == NPAIR - 1, i == n_steps - 1))(phases[2])

    full = jax.ShapeDtypeStruct((S, NH * HD), F32)
    kwargs, operands = _with_comm(
        comm, [Q_ROWS_SPEC, K_ALL_SPEC, V_ALL_SPEC, PAIR_ROWS_SPEC, PAIR_TOTAL_SPEC],
        [PAIR_ROWS_SPEC, PAIR_ALL_SPEC, PAIR_ALL_SPEC], [full, full, full], [qkv, qkv, qkv, dout, totals], [])
    return pl.pallas_call(
        body, name=name, grid=(NPAIR, n_steps),
        compiler_params=_cparams(("arbitrary", "arbitrary")), **kwargs,
    )(*operands)


def _proj_cols(first_col):
    base = first_col // LANES
    return pl.BlockSpec((S, LANES), lambda j: (0, base + j))


CONV_OUT_SPEC = pl.BlockSpec((S, LANES), lambda j: (0, j))
CONV_DOUT_SPEC = pl.BlockSpec((S, LANES), lambda j: (0, (NH * HD) // LANES + j))
CONV_W_SPEC = pl.BlockSpec((8, LANES), lambda j: (0, j))
CONV_B_SPEC = pl.BlockSpec((1, LANES), lambda j: (0, j))


def _shift_down(u, n):
    rows = lax.broadcasted_iota(jnp.int32, u.shape, 0)
    return jnp.where(rows >= n, pltpu.roll(u, n, 0), 0.0)


def _shift_up(u, n):
    rows = lax.broadcasted_iota(jnp.int32, u.shape, 0)
    return jnp.where(rows < S - n, pltpu.roll(u, S - n, 0), 0.0)


def conv_fwd(proj, cw8, cb, name):
    def body(bg_ref, cg_ref, hc_ref, w_ref, b_ref, o_ref):
        u = cg_ref[...] * hc_ref[...]
        w = w_ref[...]
        y = w[0:1, :] * _shift_down(u, 2) + w[1:2, :] * _shift_down(u, 1) + w[2:3, :] * u + b_ref[...]
        o_ref[...] = bg_ref[...] * y

    return pl.pallas_call(
        body, name=name, grid=(CW // LANES,),
        in_specs=[_proj_cols(0), _proj_cols(CW), _proj_cols(2 * CW), CONV_W_SPEC, CONV_B_SPEC],
        out_specs=CONV_OUT_SPEC, out_shape=jax.ShapeDtypeStruct((S, CW), F32),
        compiler_params=_cparams(("parallel",)),
    )(proj, proj, proj, cw8, cb)


def conv_bwd(proj, dout, cw8, cb, name):
    def body(bg_ref, cg_ref, hc_ref, do_ref, w_ref, b_ref, dbg_ref, dcg_ref, dhc_ref, dw_ref, db_ref):
        cg, hc, do = cg_ref[...], hc_ref[...], do_ref[...]
        w = w_ref[...]
        u = cg * hc
        u1, u2 = _shift_down(u, 1), _shift_down(u, 2)
        y = w[0:1, :] * u2 + w[1:2, :] * u1 + w[2:3, :] * u + b_ref[...]
        dbg_ref[...] = do * y
        dy = do * bg_ref[...]
        db_ref[...] = jnp.sum(dy, axis=0, keepdims=True)
        dw_ref[...] = jnp.concatenate(
            [jnp.sum(dy * u2, axis=0, keepdims=True), jnp.sum(dy * u1, axis=0, keepdims=True),
             jnp.sum(dy * u, axis=0, keepdims=True), jnp.zeros((5, LANES), F32)], axis=0)
        du = w[2:3, :] * dy + w[1:2, :] * _shift_up(dy, 1) + w[0:1, :] * _shift_up(dy, 2)
        dcg_ref[...] = du * hc
        dhc_ref[...] = du * cg

    full = jax.ShapeDtypeStruct((S, CW), F32)
    return pl.pallas_call(
        body, name=name, grid=(CW // LANES,),
        in_specs=[_proj_cols(0), _proj_cols(CW), _proj_cols(2 * CW), CONV_DOUT_SPEC, CONV_W_SPEC, CONV_B_SPEC],
        out_specs=[CONV_OUT_SPEC, CONV_OUT_SPEC, CONV_OUT_SPEC, CONV_W_SPEC, CONV_B_SPEC],
        out_shape=[full, full, full, jax.ShapeDtypeStruct((8, CW), F32), jax.ShapeDtypeStruct((1, CW), F32)],
        compiler_params=_cparams(("parallel",)),
    )(proj, proj, proj, dout, cw8, cb)


GELU_K = math.sqrt(2.0 / math.pi)
GELU_C = 0.044715


def _gelu(x):
    return 0.5 * x * (1.0 + jnp.tanh(GELU_K * (x + GELU_C * (x * x * x))))


def _gelu_grad(x):
    t = jnp.tanh(GELU_K * (x + GELU_C * (x * x * x)))
    return 0.5 * (1.0 + t) + 0.5 * x * (1.0 - t * t) * (GELU_K * (1.0 + 3.0 * GELU_C * (x * x)))


def _sg_masks():
    row = lax.broadcasted_iota(jnp.int32, (T, T), 0)
    col = lax.broadcasted_iota(jnp.int32, (T, T), 1)
    causal = jnp.right_shift(row, 6) >= jnp.right_shift(col, 6)
    head_of_col = jnp.right_shift(lax.broadcasted_iota(jnp.int32, (T, CW), 1), 6)
    return causal, head_of_col


def _sg_mixed(vnb, sw_ref, bias, causal, head_of_col):
    mixed = bias
    for h in range(SG_HEADS):
        wh = jnp.where(causal, sw_ref[h], 0.0).astype(BF16)
        mh = jnp.dot(wh, vnb, preferred_element_type=F32)
        mixed = mixed + jnp.where(head_of_col == h, mh, 0.0)
    return mixed


SG_U_SPEC = pl.BlockSpec((T, CW), lambda n: (n, 3))
SG_V_SPEC = pl.BlockSpec((T, CW), lambda n: (n, 4))
SG_ROW_SPEC = pl.BlockSpec((T, CW), lambda n: (n, 0))
SG_DOUT_SPEC = pl.BlockSpec((T, CW), lambda n: (n, 3))
SG_G_SPEC = pl.BlockSpec((1, CW), lambda n: (0, 0))
SG_W_SPEC = pl.BlockSpec((SG_HEADS, T, T), lambda n: (0, 0, 0))
SG_BIAS_SPEC = pl.BlockSpec((T, CW), lambda n: (0, 0))


def sg_fwd(proj, gn, sw, bias, name):
    def body(u_ref, v_ref, g_ref, sw_ref, bias_ref, o_ref):
        causal, head_of_col = _sg_masks()
        gv = _gelu(v_ref[...])
        rstd = lax.rsqrt(jnp.mean(gv * gv, axis=-1, keepdims=True) + EPS)
        vnb = ((gv * rstd) * g_ref[...]).astype(BF16)
        mixed = _sg_mixed(vnb, sw_ref, bias_ref[...], causal, head_of_col)
        o_ref[...] = _gelu(u_ref[...]) * mixed

    return pl.pallas_call(
        body, name=name, grid=(S // T,),
        in_specs=[SG_U_SPEC, SG_V_SPEC, SG_G_SPEC, SG_W_SPEC, SG_BIAS_SPEC],
        out_specs=SG_ROW_SPEC, out_shape=jax.ShapeDtypeStruct((S, CW), F32),
        compiler_params=_cparams(("parallel",)),
    )(proj, proj, gn, sw, bias)


def sg_bwd(proj, dout, gn, sw, bias, name):
    def body(u_ref, v_ref, do_ref, g_ref, sw_ref, bias_ref, du_ref, dv_ref, dg_ref, dsw_ref, dbias_ref):
        @pl.when(pl.program_id(0) == 0)
        def _():
            dg_ref[...] = jnp.zeros_like(dg_ref)
            dsw_ref[...] = jnp.zeros_like(dsw_ref)
            dbias_ref[...] = jnp.zeros_like(dbias_ref)

        causal, head_of_col = _sg_masks()
        uv, vv, do, gnv = u_ref[...], v_ref[...], do_ref[...], g_ref[...]
        gv = _gelu(vv)
        rstd = lax.rsqrt(jnp.mean(gv * gv, axis=-1, keepdims=True) + EPS)
        xhat = gv * rstd
        vnb = (xhat * gnv).astype(BF16)
        mixed = _sg_mixed(vnb, sw_ref, bias_ref[...], causal, head_of_col)
        du_ref[...] = (do * mixed) * _gelu_grad(uv)
        dmix = do * _gelu(uv)
        dbias_ref[...] += dmix
        dmixb = dmix.astype(BF16)
        dvn = jnp.zeros((T, CW), F32)
        for h in range(SG_HEADS):
            wh = jnp.where(causal, sw_ref[h], 0.0).astype(BF16)
            dvh = lax.dot_general(wh, dmixb, (((0,), (0,)), ((), ())), preferred_element_type=F32)
            dvn = dvn + jnp.where(head_of_col == h, dvh, 0.0)
            dmh = jnp.where(head_of_col == h, dmixb, jnp.zeros_like(dmixb))
            dwh = lax.dot_general(dmh, vnb, (((1,), (1,)), ((), ())), preferred_element_type=F32)
            dsw_ref[h] += jnp.where(causal, dwh, 0.0)
        dg_ref[...] += jnp.sum(dvn * xhat, axis=0, keepdims=True)
        dxhat = dvn * gnv
        dgv = rstd * (dxhat - xhat * jnp.mean(dxhat * xhat, axis=-1, keepdims=True))
        dv_ref[...] = dgv * _gelu_grad(vv)

    full = jax.ShapeDtypeStruct((S, CW), F32)
    return pl.pallas_call(
        body, name=name, grid=(S // T,),
        in_specs=[SG_U_SPEC, SG_V_SPEC, SG_DOUT_SPEC, SG_G_SPEC, SG_W_SPEC, SG_BIAS_SPEC],
        out_specs=[SG_ROW_SPEC, SG_ROW_SPEC, SG_G_SPEC, SG_W_SPEC, SG_BIAS_SPEC],
        out_shape=[full, full, jax.ShapeDtypeStruct((1, CW), F32),
                   jax.ShapeDtypeStruct((SG_HEADS, T, T), F32), jax.ShapeDtypeStruct((T, CW), F32)],
        compiler_params=_cparams(("arbitrary",)),
    )(proj, proj, dout, gn, sw, bias)


ADA_COLS = NMOD * D // NDEV


def ada_fwd(c_all, ada_w, ada_b_mine, name):
    def body(c_ref, w_ref, b_ref, o_ref, ca_ref):
        cv = c_ref[...]
        ca = cv * (1.0 / (1.0 + jnp.exp(-cv)))
        ca_ref[...] = ca
        cab = ca.astype(BF16)
        for l in range(L):
            o_ref[l] = jnp.dot(cab, w_ref[l].astype(BF16), preferred_element_type=F32) + b_ref[l]

    return pl.pallas_call(
        body, name=name,
        out_shape=[jax.ShapeDtypeStruct((L, NDEV, ADA_COLS), F32), jax.ShapeDtypeStruct((NDEV, D), F32)],
        compiler_params=_cparams(),
    )(c_all, ada_w, ada_b_mine)


def ada_bwd(ca, dmod_cols, name):
    def body(ca_ref, dm_ref, o_ref):
        cab = ca_ref[...].astype(BF16)
        for l in range(L):
            o_ref[l] = lax.dot_general(cab, dm_ref[l].astype(BF16), (((0,), (0,)), ((), ())),
                                       preferred_element_type=F32)

    return pl.pallas_call(
        body, name=name, out_shape=jax.ShapeDtypeStruct((L, D, ADA_COLS), F32),
        compiler_params=_cparams(),
    )(ca, dmod_cols)


def _adamw(w, g, m, v):
    m = B1 * m + (1.0 - B1) * g
    v = B2 * v + (1.0 - B2) * (g * g)
    m_hat = m / BC1
    v_hat = v / BC2
    delta = -LR * (m_hat / (jnp.sqrt(v_hat) + AEPS) + WD * w)
    return delta, m, v


def sum_gathered(parts, name):
    _, rows, cols = parts.shape

    def body(p_ref, o_ref):
        acc = p_ref[0]
        for d in range(1, NDEV):
            acc = acc + p_ref[d]
        o_ref[...] = acc

    return pl.pallas_call(
        body, name=name, out_shape=jax.ShapeDtypeStruct((rows, cols), F32),
        compiler_params=_cparams(),
    )(parts)


def adamw_plain(w, g, m, v, tr, name):
    rows, cols = w.shape
    spec = pl.BlockSpec((tr, cols), lambda i: (i, 0))

    def body(w_ref, g_ref, m_ref, v_ref, d_ref, nm_ref, nv_ref):
        delta, nm, nv = _adamw(w_ref[...], g_ref[...], m_ref[...], v_ref[...])
        d_ref[...] = delta
        nm_ref[...] = nm
        nv_ref[...] = nv

    shp = jax.ShapeDtypeStruct((rows, cols), F32)
    return pl.pallas_call(
        body, name=name, grid=(rows // tr,), in_specs=[spec] * 4, out_specs=[spec] * 3,
        out_shape=[shp, shp, shp], compiler_params=_cparams(("parallel",)),
    )(w, g, m, v)


def adamw_reduce(w, parts, m, v, tr, name):
    _, rows, cols = w.shape
    spec = pl.BlockSpec((None, tr, cols), lambda l, i: (l, i, 0))
    pspecs = [pl.BlockSpec((NDEV, tr, cols), lambda l, i, k=k: (0, jnp.where(l == k, i, 0), 0)) for k in range(L)]

    def body(w_ref, p0_ref, p1_ref, m_ref, v_ref, g_ref, d_ref, nm_ref, nv_ref):
        first_layer = pl.program_id(0) == 0
        g = jnp.zeros((tr, cols), F32)
        for d in range(NDEV):
            g = g + jnp.where(first_layer, p0_ref[d], p1_ref[d]).astype(F32)
        delta, nm, nv = _adamw(w_ref[...], g, m_ref[...], v_ref[...])
        g_ref[...] = g
        d_ref[...] = delta
        nm_ref[...] = nm
        nv_ref[...] = nv

    shp = jax.ShapeDtypeStruct(w.shape, F32)
    return pl.pallas_call(
        body, name=name, grid=(L, rows // tr), in_specs=[spec] + pspecs + [spec, spec], out_specs=[spec] * 4,
        out_shape=[shp] * 4, compiler_params=_cparams(("parallel", "parallel")),
    )(w, *parts, m, v)


def _pad_rows(flat, rows):
    return jnp.pad(flat, (0, rows * LANES - flat.shape[0])).reshape(rows, LANES)


def kernel(x, c, ada_w, ada_b, norm_mix_g, norm_mlp_g, w_in, conv_w, conv_b, gmlp_norm_g, spatial_w, spatial_b, w_out, mlp_w1, mlp_w2, final_norm_g, loss_target, m_ada_w, m_ada_b, m_norm_mix_g, m_norm_mlp_g, m_w_in, m_conv_w, m_conv_b, m_gmlp_norm_g, m_spatial_w, m_spatial_b, m_w_out, m_mlp_w1, m_mlp_w2, m_final_norm_g, v_ada_w, v_ada_b, v_norm_mix_g, v_norm_mlp_g, v_w_in, v_conv_w, v_conv_b, v_gmlp_norm_g, v_spatial_w, v_spatial_b, v_w_out, v_mlp_w1, v_mlp_w2, v_final_norm_g):
    me = _lin(_my_pos())
    x0 = x[0]
    target = loss_target[0]
    conv_shard = conv_w.shape[-1]

    w_in_b, w_out_b, w1_b, w2_b = [w.astype(BF16) for w in (w_in, w_out, mlp_w1, mlp_w2)]
    pack0 = _pad_rows(jnp.concatenate([c.reshape(-1), conv_w.reshape(-1)]), 16)
    g0, gw_in0 = run_comm(Gather([pack0, w_in_b[0]]), "gather_first")
    g0 = g0.reshape(NDEV, 16 * LANES)
    c_all = g0[:, :D]
    conv_full = (g0[:, D:D + L * 3 * conv_shard].reshape(NDEV, L, 3, conv_shard)
                 .transpose(1, 2, 0, 3).reshape(L, 3, CW))

    def canonical_w_in(gathered):
        return gathered.transpose(1, 0, 2).reshape(D, PROJ)

    weight_plans = [Gather([w_out_b[0], w1_b[0], w2_b[0], w_in_b[1]]), Gather([w_out_b[1], w1_b[1], w2_b[1]])]
    W_in = [canonical_w_in(gw_in0), None]
    W_out, W1, W2 = [None] * L, [None] * L, [None] * L

    ada_b_mine = lax.dynamic_slice(ada_b, (0, me * ADA_COLS), (L, ADA_COLS)).reshape(L, 1, ADA_COLS)
    mod_part, c_act = ada_fwd(c_all, ada_w, ada_b_mine, "ada_fwd")
    gmod = run_comm(Gather([mod_part]), "gather_mod")[0]
    mod = lax.dynamic_index_in_dim(gmod, me, axis=2, keepdims=False)
    mod = mod.transpose(1, 0, 2).reshape(L, NMOD, 1, D)

    cw8 = jnp.pad(conv_full, ((0, 0), (0, 5), (0, 0)))
    sg_bias = jnp.repeat(spatial_b.transpose(0, 2, 1), HD, axis=2)

    saved = []
    xl = x0
    for l in range(L):
        sh_m, sc_m, g_m, sh_f, sc_f, g_f = [mod[l, k] for k in range(NMOD)]
        h1 = normmod_fwd(xl, norm_mix_g[l:l + 1], sc_m, sh_m, f"norm_mix_fwd{l}")
        qkv = mm_layer("proj_qkv", l, h1, W_in[l], out_dtypes=[BF16], cols=(0, QKV))[0]
        proj = mm_layer("proj_rest", l, h1, W_in[l], out_dtypes=[F32], cols=(QKV, REST))[0]
        a_out, a_tot, *gathered = attn_fwd(qkv, f"attn_fwd{l}", comm=weight_plans[l])
        W_out[l] = gathered[0].reshape(D, D)
        W1[l] = gathered[1]
        W2[l] = gathered[2].reshape(DFF, D)
        if l + 1 < L:
            W_in[l + 1] = canonical_w_in(gathered[3])
        c_out = conv_fwd(proj, cw8[l], conv_b[l:l + 1], f"conv_fwd{l}")
        s_out = sg_fwd(proj, gmlp_norm_g[l:l + 1], spatial_w[l], sg_bias[l], f"sg_fwd{l}")
        cat = jnp.concatenate([a_out, c_out.astype(BF16), s_out.astype(BF16)], axis=1)
        mix, x1 = mm_layer("mix", l, cat, W_out[l], out_dtypes=[F32, F32],
                           epilogue=lambda acc, xr, g: (acc, xr + g * acc), extras=[(xl, "tile"), (g_m, "col")])
        h2 = normmod_fwd(x1, norm_mlp_g[l:l + 1], sc_f, sh_f, f"norm_mlp_fwd{l}")
        ra, r = mm_layer("mlp_up", l, h2, W1[l], out_dtypes=[BF16, BF16], b_blocks=True,
                         epilogue=lambda acc: (jnp.maximum(acc, 0.0), jnp.square(jnp.maximum(acc, 0.0))))
        m2, x2 = mm_layer("mlp_down", l, r, W2[l], out_dtypes=[F32, F32],
                          epilogue=lambda acc, xr, g: (acc, xr + g * acc), extras=[(x1, "tile"), (g_f, "col")])
        saved.append(dict(x=xl, h1=h1, proj=proj, qkv=qkv, a_tot=a_tot, cat=cat, mix=mix,
                          x1=x1, h2=h2, ra=ra, r=r, m2=m2))
        xl = x2

    dx, loss_part, d_final_g = loss_head(xl, target, final_norm_g.reshape(1, D), "loss_head")

    dmod = [None] * L
    p_in, p_out, p_w1, p_w2 = [None] * L, [None] * L, [None] * L, [None] * L
    pending_w_in = None
    d_norm_mix, d_norm_mlp, d_conv_w, d_conv_b = [None] * L, [None] * L, [None] * L, [None] * L
    d_gn, d_sw, d_sb = [None] * L, [None] * L, [None] * L
    for l in reversed(range(L)):
        sv = saved[l]
        sh_m, sc_m, g_m, sh_f, sc_f, g_f = [mod[l, k] for k in range(NMOD)]
        dm2, dg_f = gate_bwd(dx, sv["m2"], g_f, f"gate_mlp_bwd{l}")
        da = mm_layer("mlp_down_dgrad", l, dm2, W2[l], out_dtypes=[BF16], trans_b=True,
                      epilogue=lambda acc, rav: (acc * (2.0 * rav.astype(F32)),), extras=[(sv["ra"], "tile")])[0]
        dW2 = mm_layer("mlp_down_wgrad", l, sv["r"], dm2, out_dtypes=[BF16], trans_a=True)[0]
        dW1 = mm_layer("mlp_up_wgrad", l, sv["h2"], da, out_dtypes=[BF16], trans_a=True, out_blocks=True)[0]
        dh2 = mm_layer("mlp_up_dgrad", l, da, W1[l], out_dtypes=[F32], trans_b=True, b_blocks=True)[0]
        dx1, dsc_f, dsh_f, d_norm_mlp[l] = normmod_bwd(sv["x1"], dh2, dx, norm_mlp_g[l:l + 1], sc_f,
                                                       f"norm_mlp_bwd{l}")
        dmix, dg_m = gate_bwd(dx1, sv["mix"], g_m, f"gate_mix_bwd{l}")
        dcat = mm_layer("mix_dgrad", l, dmix, W_out[l], out_dtypes=[F32], trans_b=True)[0]
        dW_out = mm_layer("mix_wgrad", l, sv["cat"], dmix, out_dtypes=[BF16], trans_a=True)[0]
        ready = [dW2.reshape(NDEV, DFF // NDEV, D), dW1, dW_out.reshape(NDEV, D // NDEV, D)]
        if pending_w_in is not None:
            ready = [pending_w_in] + ready
        dq, dk, dv, *arrived = attn_bwd(sv["qkv"], dcat, sv["a_tot"], f"attn_bwd{l}", comm=Exchange(ready))
        if pending_w_in is not None:
            p_in[l + 1] = arrived.pop(0)
        p_w2[l], p_w1[l], p_out[l] = arrived
        dbg, dcg, dhc, dcw8, d_conv_b[l] = conv_bwd(sv["proj"], dcat, cw8[l], conv_b[l:l + 1], f"conv_bwd{l}")
        d_conv_w[l] = dcw8[:3]
        dus, dvs, d_gn[l], dsw, dbias = sg_bwd(sv["proj"], dcat, gmlp_norm_g[l:l + 1],
                                               spatial_w[l], sg_bias[l], f"sg_bwd{l}")
        d_sw[l] = dsw
        d_sb[l] = dbias.reshape(T, SG_HEADS, HD).sum(axis=2).T
        dproj = jnp.concatenate([dq, dk, dv, dbg, dcg, dhc, dus, dvs], axis=1).astype(BF16)
        dW_in = mm_layer("proj_wgrad", l, sv["h1"], dproj, out_dtypes=[BF16], trans_a=True)[0]
        pending_w_in = dW_in.reshape(D, NDEV, PROJ // NDEV).transpose(1, 0, 2)
        dh1 = mm_layer("proj_dgrad", l, dproj, W_in[l], out_dtypes=[F32], trans_b=True)[0]
        dx, dsc_m, dsh_m, d_norm_mix[l] = normmod_bwd(sv["x"], dh1, dx1, norm_mix_g[l:l + 1], sc_m,
                                                      f"norm_mix_bwd{l}")
        dmod[l] = jnp.concatenate([dsh_m, dsc_m, dg_m, dsh_f, dsc_f, dg_f], axis=1)

    grad_x = dx.reshape(1, S, D)

    small_parts = [jnp.concatenate(dmod, axis=0), jnp.concatenate(d_norm_mix, axis=0),
                   jnp.concatenate(d_norm_mlp, axis=0), jnp.stack(d_conv_w), jnp.concatenate(d_conv_b, axis=0),
                   jnp.concatenate(d_gn, axis=0), jnp.stack(d_sw), jnp.stack(d_sb), d_final_g, loss_part[:, :1]]
    sizes = [p.size for p in small_parts]
    small_rows = -(-sum(sizes) // (8 * LANES)) * 8
    small_pack = _pad_rows(jnp.concatenate([p.reshape(-1) for p in small_parts]), small_rows)
    small_all = run_comm(Gather([small_pack]), "gather_small_grads")[0]
    small_sum = sum_gathered(small_all, "sum_small_grads").reshape(-1)
    offs = [0]
    for sz in sizes:
        offs.append(offs[-1] + sz)
    summed = [small_sum[offs[k]:offs[k + 1]].reshape(small_parts[k].shape) for k in range(len(sizes))]
    (g_ada_b, g_norm_mix, g_norm_mlp, g_conv_w_full, g_conv_b, g_gn, g_sw, g_sb, g_final, loss_sum) = summed
    loss = loss_sum.reshape(())
    g_final = g_final.reshape(D)
    g_conv_w = lax.dynamic_slice(g_conv_w_full, (0, 0, me * conv_shard), (L, 3, conv_shard))

    dmod_all = small_all.reshape(NDEV, -1)[:, :L * NMOD * D].reshape(NDEV, L, NMOD * D)
    dmod_cols = lax.dynamic_slice(dmod_all, (0, 0, me * ADA_COLS), (NDEV, L, ADA_COLS)).transpose(1, 0, 2)
    g_ada_w = ada_bwd(c_act, dmod_cols, "ada_bwd")

    p_in[0] = run_comm(Exchange([pending_w_in]), "exchange_last")[0]
    g_w_in, d_w_in, nm_w_in, nv_w_in = adamw_reduce(w_in, p_in, m_w_in, v_w_in, 256, "adamw_w_in")
    g_w_out, d_w_out, nm_w_out, nv_w_out = adamw_reduce(w_out, p_out, m_w_out, v_w_out, 128, "adamw_w_out")
    g_w1, d_w1, nm_w1, nv_w1 = adamw_reduce(mlp_w1, p_w1, m_mlp_w1, v_mlp_w1, 256, "adamw_mlp_w1")
    g_w2, d_w2, nm_w2, nv_w2 = adamw_reduce(mlp_w2, p_w2, m_mlp_w2, v_mlp_w2, 256, "adamw_mlp_w2")

    flat2 = lambda t: t.reshape(L * D, ADA_COLS)
    d_ada_w, nm_ada_w, nv_ada_w = [t.reshape(L, D, ADA_COLS) for t in adamw_plain(
        flat2(ada_w), flat2(g_ada_w), flat2(m_ada_w), flat2(v_ada_w), 256, "adamw_ada_w")]

    small_w = [ada_b, norm_mix_g, norm_mlp_g, conv_w, conv_b, gmlp_norm_g, spatial_w, spatial_b, final_norm_g]
    small_m = [m_ada_b, m_norm_mix_g, m_norm_mlp_g, m_conv_w, m_conv_b, m_gmlp_norm_g, m_spatial_w, m_spatial_b,
               m_final_norm_g]
    small_v = [v_ada_b, v_norm_mix_g, v_norm_mlp_g, v_conv_w, v_conv_b, v_gmlp_norm_g, v_spatial_w, v_spatial_b,
               v_final_norm_g]
    small_g = [g_ada_b, g_norm_mix, g_norm_mlp, g_conv_w, g_conv_b, g_gn, g_sw, g_sb, g_final]
    wsizes = [p.size for p in small_w]
    wrows = -(-sum(wsizes) // (8 * LANES)) * 8
    pack = lambda ps: _pad_rows(jnp.concatenate([p.reshape(-1) for p in ps]), wrows)
    sd, snm, snv = adamw_plain(pack(small_w), pack(small_g), pack(small_m), pack(small_v), wrows, "adamw_small")
    woffs = [0]
    for sz in wsizes:
        woffs.append(woffs[-1] + sz)

    def unpack(flat):
        flat = flat.reshape(-1)
        return [flat[woffs[k]:woffs[k + 1]].reshape(small_w[k].shape) for k in range(len(small_w))]

    sd, snm, snv = unpack(sd), unpack(snm), unpack(snv)

    def ordered(big, small):
        ada, win, wout, w1, w2 = big
        return [ada, small[0], small[1], small[2], win, small[3], small[4], small[5], small[6], small[7],
                wout, w1, w2, small[8]]

    grads = ordered([g_ada_w, g_w_in, g_w_out, g_w1, g_w2], small_g)
    deltas = ordered([d_ada_w, d_w_in, d_w_out, d_w1, d_w2], sd)
    new_m = ordered([nm_ada_w, nm_w_in, nm_w_out, nm_w1, nm_w2], snm)
    new_v = ordered([nv_ada_w, nv_w_in, nv_w_out, nv_w1, nv_w2], snv)
    return (loss, grad_x, *grads, *deltas, *new_m, *new_v)
```

```python
import functools
import math

import jax
import jax.numpy as jnp
from jax import lax
from jax.experimental import pallas as pl
from jax.experimental.pallas import tpu as pltpu

F32 = jnp.float32
BF16 = jnp.bfloat16
MESH = pl.DeviceIdType.MESH

S = 2048
D = 1024
L = 2
NDEV = 8
HD = 64
NH = 8
PROJ = 2816
DFF = 4096
NMOD = 6
EPS = 1e-6
T = 128
SG_HEADS = 4
LANES = 128
CW = 256
QKV = 3 * NH * HD
REST = PROJ - QKV

LR, B1, B2, AEPS, WD, STEP = 0.001, 0.9, 0.999, 1e-08, 0.01, 10
BC1 = 1.0 - B1 ** STEP
BC2 = 1.0 - B2 ** STEP

VMEM_LIMIT = 48 * 1024 * 1024

HBM_SPEC = pl.BlockSpec(memory_space=pltpu.HBM)


def _cparams(sem=None):
    return pltpu.CompilerParams(dimension_semantics=sem, vmem_limit_bytes=VMEM_LIMIT)


def _my_pos():
    return lax.axis_index("x"), lax.axis_index("y"), lax.axis_index("c")


def _lin(p):
    return 4 * p[0] + 2 * p[1] + p[2]


class Gather:
    def __init__(self, arrs):
        self.arrs = list(arrs)
        n = len(self.arrs)
        self.out_shape = [jax.ShapeDtypeStruct((NDEV,) + a.shape, a.dtype) for a in self.arrs]
        self.scratch = [pltpu.SemaphoreType.DMA((n, 7)), pltpu.SemaphoreType.DMA((n, 7)),
                        pltpu.SemaphoreType.DMA((n,))]

    def phases(self, ins, outs, sems):
        n = len(self.arrs)
        send_sems, recv_sems, local_sems = sems
        x, y, c = _my_pos()
        me, sibling = (x, y, c), (x, y, 1 - c)
        chips = [(1 - x, y), (x, 1 - y), (1 - x, 1 - y)]

        def copy(a, k, block, to, src=None):
            slot = outs[a].at[_lin(block)]
            return pltpu.make_async_remote_copy(
                src_ref=slot if src is None else src, dst_ref=slot,
                send_sem=send_sems.at[a, k], recv_sem=recv_sems.at[a, k],
                device_id=to, device_id_type=MESH)

        def mine(a):
            return pltpu.make_async_copy(ins[a], outs[a].at[_lin(me)], local_sems.at[a])

        def first(a):
            return [copy(a, 0, me, sibling, src=ins[a])] + [
                copy(a, 1 + j, me, (*chip, c), src=ins[a]) for j, chip in enumerate(chips)]

        def passed(a):
            return [copy(a, 4 + j, (*chip, c), sibling) for j, chip in enumerate(chips)]

        def start():
            for a in range(n):
                mine(a).start()
                for cp in first(a):
                    cp.start()

        def relay():
            for j, chip in enumerate(chips):
                for a in range(n):
                    copy(a, 1 + j, (*chip, c), me).wait_recv()
                    passed(a)[j].start()

        def finish():
            for a in range(n):
                copy(a, 0, sibling, me).wait_recv()
            for j, chip in enumerate(chips):
                for a in range(n):
                    copy(a, 4 + j, (*chip, 1 - c), me).wait_recv()
            for a in range(n):
                for cp in first(a) + passed(a):
                    cp.wait_send()
                mine(a).wait()

        return start, relay, finish


class Exchange:
    def __init__(self, arrs):
        self.arrs = list(arrs)
        n = len(self.arrs)
        self.out_shape = [jax.ShapeDtypeStruct(a.shape, a.dtype) for a in self.arrs]
        self.scratch = [pltpu.SemaphoreType.DMA((n, 7)), pltpu.SemaphoreType.DMA((n, 7)),
                        pltpu.SemaphoreType.DMA((n,))]

    def phases(self, ins, outs, sems):
        n = len(self.arrs)
        send_sems, recv_sems, local_sems = sems
        x, y, c = _my_pos()
        me = (x, y, c)

        def peer(mask):
            return (1 - x if mask & 4 else x, 1 - y if mask & 2 else y, 1 - c if mask & 1 else c)

        def copy(a, mask):
            return pltpu.make_async_remote_copy(
                src_ref=ins[a].at[_lin(peer(mask))], dst_ref=outs[a].at[_lin(me)],
                send_sem=send_sems.at[a, mask - 1], recv_sem=recv_sems.at[a, mask - 1],
                device_id=peer(mask), device_id_type=MESH)

        def arrival(a, mask):
            return pltpu.make_async_remote_copy(
                src_ref=ins[a].at[_lin(me)], dst_ref=outs[a].at[_lin(peer(mask))],
                send_sem=send_sems.at[a, mask - 1], recv_sem=recv_sems.at[a, mask - 1],
                device_id=peer(mask), device_id_type=MESH)

        def mine(a):
            return pltpu.make_async_copy(ins[a].at[_lin(me)], outs[a].at[_lin(me)], local_sems.at[a])

        def start():
            for a in range(n):
                mine(a).start()
            for mask in (4, 2, 6, 1, 5, 3, 7):
                for a in range(n):
                    copy(a, mask).start()

        def relay():
            pass

        def finish():
            for mask in range(1, 8):
                for a in range(n):
                    arrival(a, mask).wait_recv()
            for mask in range(1, 8):
                for a in range(n):
                    copy(a, mask).wait_send()
            for a in range(n):
                mine(a).wait()

        return start, relay, finish


def run_comm(plan, name):
    n = len(plan.arrs)

    def body(*refs):
        start, relay, finish = plan.phases(refs[:n], refs[n:2 * n], refs[2 * n:])
        start()
        relay()
        finish()

    outs = pl.pallas_call(
        body, name=name, out_shape=plan.out_shape,
        in_specs=[HBM_SPEC] * n, out_specs=[HBM_SPEC] * n, scratch_shapes=plan.scratch,
    )(*plan.arrs)
    return list(outs)


MM_TILES = {
    "proj_qkv": (S, 512), "proj_rest": (S, 256), "mix": (1024, 512), "mlp_up": (S, 512), "mlp_down": (1024, 256),
    "mlp_down_dgrad": (1024, 1024), "mlp_down_wgrad": (1024, 1024), "mlp_up_wgrad": (1024, 512),
    "mlp_up_dgrad": (1024, 512), "mix_dgrad": (1024, 512), "mix_wgrad": (512, 1024),
    "proj_wgrad": (1024, PROJ // 2), "proj_dgrad": (1024, 512),
}


def mm_layer(kind, l, a, b, **kw):
    tm, tn = MM_TILES[kind]
    return mm(a, b, tm=tm, tn=tn, name=f"{kind}{l}", **kw)


def mm(a, b, *, tm, tn, out_dtypes, epilogue=None, extras=(), name, trans_a=False, trans_b=False,
       cols=None, b_blocks=False, out_blocks=False):
    if trans_a:
        kdim, m = a.shape
    else:
        m, kdim = a.shape
    shard = b.shape[-1] if b_blocks else None
    if b_blocks:
        full = (b.shape[1], NDEV * shard)
    else:
        full = b.shape
    first, ncols = cols if cols is not None else (0, full[0] if trans_b else full[1])
    assert full[1 if trans_b else 0] == kdim and m % tm == 0 and ncols % tn == 0 and first % tn == 0
    j0 = first // tn
    if trans_a:
        a_spec = pl.BlockSpec((kdim, tm), lambda i, j: (0, i))
    else:
        a_spec = pl.BlockSpec((tm, kdim), lambda i, j: (i, 0))
    if b_blocks and trans_b:
        b_spec = pl.BlockSpec((NDEV, tn, shard), lambda i, j: (0, j0 + j, 0))
    elif b_blocks:
        assert tn == shard
        b_spec = pl.BlockSpec((None, kdim, tn), lambda i, j: (j0 + j, 0, 0))
    elif trans_b:
        b_spec = pl.BlockSpec((tn, kdim), lambda i, j: (j0 + j, 0))
    else:
        b_spec = pl.BlockSpec((kdim, tn), lambda i, j: (0, j0 + j))
    if out_blocks:
        assert tn * NDEV == ncols
        out_spec = pl.BlockSpec((None, tm, tn), lambda i, j: (j, i, 0))
        out_dims = (NDEV, m, tn)
    else:
        out_spec = pl.BlockSpec((tm, tn), lambda i, j: (i, j))
        out_dims = (m, ncols)
    ex_specs = []
    for arr, kind in extras:
        if kind == "tile":
            ex_specs.append(pl.BlockSpec((tm, tn), lambda i, j: (i, j)))
        else:
            ex_specs.append(pl.BlockSpec((1, tn), lambda i, j: (0, j)))
    n_ex, n_out = len(extras), len(out_dtypes)

    def body(a_ref, b_ref, *rest):
        ex_refs, out_refs = rest[:n_ex], rest[n_ex:]
        if trans_a:
            acc = lax.dot_general(a_ref[...], b_ref[...], (((0,), (0,)), ((), ())),
                                  preferred_element_type=F32)
        elif trans_b and b_blocks:
            acc = jnp.zeros((tm, tn), F32)
            for d in range(NDEV):
                acc = acc + lax.dot_general(a_ref[:, d * shard:(d + 1) * shard], b_ref[d],
                                            (((1,), (1,)), ((), ())), preferred_element_type=F32)
        elif trans_b:
            acc = lax.dot_general(a_ref[...], b_ref[...], (((1,), (1,)), ((), ())),
                                  preferred_element_type=F32)
        else:
            acc = jnp.dot(a_ref[...], b_ref[...], preferred_element_type=F32)
        outs = (acc,) if epilogue is None else epilogue(acc, *[r[...] for r in ex_refs])
        for o_ref, val in zip(out_refs, outs):
            o_ref[...] = val.astype(o_ref.dtype)

    outs = pl.pallas_call(
        body, name=name, grid=(m // tm, ncols // tn),
        in_specs=[a_spec, b_spec] + ex_specs,
        out_specs=[out_spec for _ in range(n_out)],
        out_shape=[jax.ShapeDtypeStruct(out_dims, dt) for dt in out_dtypes],
        compiler_params=_cparams(("parallel", "parallel")),
    )(a, b, *[arr for arr, _ in extras])
    return list(outs)


TR = 256

ROW_SPEC = pl.BlockSpec((TR, D), lambda i: (i, 0))
VEC_SPEC = pl.BlockSpec((1, D), lambda i: (0, 0))


def normmod_fwd(x, g, sc, sh, name):
    def body(x_ref, g_ref, sc_ref, sh_ref, o_ref):
        xv = x_ref[...]
        rstd = lax.rsqrt(jnp.mean(xv * xv, axis=-1, keepdims=True) + EPS)
        n = (xv * rstd) * g_ref[...]
        o_ref[...] = (n * (1.0 + sc_ref[...]) + sh_ref[...]).astype(o_ref.dtype)

    return pl.pallas_call(
        body, name=name, grid=(S // TR,),
        in_specs=[ROW_SPEC, VEC_SPEC, VEC_SPEC, VEC_SPEC], out_specs=ROW_SPEC,
        out_shape=jax.ShapeDtypeStruct((S, D), BF16),
        compiler_params=_cparams(("parallel",)),
    )(x, g, sc, sh)


def normmod_bwd(x, dh, dres, g, sc, name):
    def body(x_ref, dh_ref, dres_ref, g_ref, sc_ref, dx_ref, dsc_ref, dsh_ref, dg_ref):
        @pl.when(pl.program_id(0) == 0)
        def _():
            dsc_ref[...] = jnp.zeros_like(dsc_ref)
            dsh_ref[...] = jnp.zeros_like(dsh_ref)
            dg_ref[...] = jnp.zeros_like(dg_ref)

        xv, dh = x_ref[...], dh_ref[...]
        gv = g_ref[...]
        rstd = lax.rsqrt(jnp.mean(xv * xv, axis=-1, keepdims=True) + EPS)
        xhat = xv * rstd
        dn = dh * (1.0 + sc_ref[...])
        dxhat = dn * gv
        dx_ref[...] = dres_ref[...] + rstd * (dxhat - xhat * jnp.mean(dxhat * xhat, axis=-1, keepdims=True))
        dsc_ref[...] += jnp.sum(dh * (xhat * gv), axis=0, keepdims=True)
        dsh_ref[...] += jnp.sum(dh, axis=0, keepdims=True)
        dg_ref[...] += jnp.sum(dn * xhat, axis=0, keepdims=True)

    vec_out = jax.ShapeDtypeStruct((1, D), F32)
    return pl.pallas_call(
        body, name=name, grid=(S // TR,),
        in_specs=[ROW_SPEC, ROW_SPEC, ROW_SPEC, VEC_SPEC, VEC_SPEC],
        out_specs=[ROW_SPEC, VEC_SPEC, VEC_SPEC, VEC_SPEC],
        out_shape=[jax.ShapeDtypeStruct((S, D), F32), vec_out, vec_out, vec_out],
        compiler_params=_cparams(("arbitrary",)),
    )(x, dh, dres, g, sc)


def gate_bwd(dx, branch, gate, name):
    def body(dx_ref, br_ref, gate_ref, o_ref, dgate_ref):
        @pl.when(pl.program_id(0) == 0)
        def _():
            dgate_ref[...] = jnp.zeros_like(dgate_ref)

        dxv = dx_ref[...]
        o_ref[...] = (dxv * gate_ref[...]).astype(o_ref.dtype)
        dgate_ref[...] += jnp.sum(dxv * br_ref[...], axis=0, keepdims=True)

    return pl.pallas_call(
        body, name=name, grid=(S // TR,),
        in_specs=[ROW_SPEC, ROW_SPEC, VEC_SPEC], out_specs=[ROW_SPEC, VEC_SPEC],
        out_shape=[jax.ShapeDtypeStruct((S, D), BF16), jax.ShapeDtypeStruct((1, D), F32)],
        compiler_params=_cparams(("arbitrary",)),
    )(dx, branch, gate)


def loss_head(x, target, g, name):
    def body(x_ref, t_ref, g_ref, dx_ref, loss_ref, dg_ref):
        @pl.when(pl.program_id(0) == 0)
        def _():
            loss_ref[...] = jnp.zeros_like(loss_ref)
            dg_ref[...] = jnp.zeros_like(dg_ref)

        xv, gv = x_ref[...], g_ref[...]
        rstd = lax.rsqrt(jnp.mean(xv * xv, axis=-1, keepdims=True) + EPS)
        xhat = xv * rstd
        err = xhat * gv - t_ref[...]
        loss_ref[...] += jnp.sum(err * err) * (0.5 / D)
        dy = err * (1.0 / D)
        dg_ref[...] += jnp.sum(dy * xhat, axis=0, keepdims=True)
        dxhat = dy * gv
        dx_ref[...] = rstd * (dxhat - xhat * jnp.mean(dxhat * xhat, axis=-1, keepdims=True))

    return pl.pallas_call(
        body, name=name, grid=(S // TR,),
        in_specs=[ROW_SPEC, ROW_SPEC, VEC_SPEC],
        out_specs=[ROW_SPEC, pl.BlockSpec((1, 128), lambda i: (0, 0)), VEC_SPEC],
        out_shape=[jax.ShapeDtypeStruct((S, D), F32), jax.ShapeDtypeStruct((1, 128), F32),
                   jax.ShapeDtypeStruct((1, D), F32)],
        compiler_params=_cparams(("arbitrary",)),
    )(x, target, g)


TQ = 512
RS = 128
NSUB = TQ // RS
TK = 128


def _dot_hilo(a, tri_twice):
    hi = a.astype(BF16)
    lo = (a - hi.astype(F32)).astype(BF16)
    return jnp.dot(jnp.concatenate([hi, lo], axis=1), tri_twice, preferred_element_type=F32)


def _log_stay(z):
    return -(jnp.maximum(z, 0.0) + jnp.log(1.0 + jnp.exp(-jnp.abs(z))))


def _tri_and_ones(kind):
    row = jnp.bitwise_and(lax.broadcasted_iota(jnp.int32, (2 * TK, 2 * TK), 0), TK - 1)
    col = lax.broadcasted_iota(jnp.int32, (2 * TK, 2 * TK), 1)
    tri = {"after": row > col, "upto": row <= col, "before": row < col}[kind]
    return jnp.logical_or(col >= TK, tri).astype(BF16)


NPAIR = NH // 2
SCALE = HD ** -0.5


def _pair_specs(first_block):
    rows = pl.BlockSpec((TQ, LANES), lambda p, i: (i, first_block + p))
    whole = pl.BlockSpec((S, LANES), lambda p, i: (0, first_block + p))
    return rows, whole


Q_ROWS_SPEC, _ = _pair_specs(0)
_, K_ALL_SPEC = _pair_specs(NPAIR)
_, V_ALL_SPEC = _pair_specs(2 * NPAIR)
PAIR_ROWS_SPEC = pl.BlockSpec((TQ, LANES), lambda p, i: (i, p))
PAIR_ALL_SPEC = pl.BlockSpec((S, LANES), lambda p, i: (0, p))
PAIR_TOTAL_SPEC = pl.BlockSpec((2, TQ, TK), lambda p, i: (p, i, 0))


def _head_halves(x):
    first = lax.broadcasted_iota(jnp.int32, x.shape, 1) < HD
    zero = jnp.zeros_like(x)
    return jnp.where(first, x, zero), jnp.where(first, zero, x)


def _join_heads(a, b):
    return jnp.where(lax.broadcasted_iota(jnp.int32, a.shape, 1) < HD, a, b)


def _comm_hooks(comm, refs, n_in, n_out, n_scratch):
    nc = len(comm.arrs) if comm is not None else 0
    ins, cin = refs[:n_in], refs[n_in:n_in + nc]
    outs = refs[n_in + nc:n_in + nc + n_out]
    cout = refs[n_in + nc + n_out:n_in + 2 * nc + n_out]
    scratch = refs[n_in + 2 * nc + n_out:n_in + 2 * nc + n_out + n_scratch]
    sems = refs[n_in + 2 * nc + n_out + n_scratch:]
    phases = comm.phases(cin, cout, sems) if comm is not None else None
    return ins, outs, scratch, phases


def _with_comm(comm, in_specs, out_specs, out_shape, operands, scratch):
    if comm is None:
        return dict(in_specs=in_specs, out_specs=out_specs, out_shape=out_shape, scratch_shapes=scratch), operands
    nc = len(comm.arrs)
    return dict(in_specs=in_specs + [HBM_SPEC] * nc, out_specs=out_specs + [HBM_SPEC] * nc,
                out_shape=out_shape + comm.out_shape, scratch_shapes=scratch + comm.scratch), operands + comm.arrs


def attn_fwd(qkv, name, comm=None):
    n_steps = S // TQ

    def body(*refs):
        (q_ref, k_ref, v_ref), (o_ref, r_ref), (acc_ref, z_even, z_odd, w_ref), phases = _comm_hooks(
            comm, refs, 3, 2, 4)
        p = pl.program_id(0)
        i = pl.program_id(1)
        if phases is not None:
            pl.when(jnp.logical_and(p == 0, i == 0))(phases[0])
            pl.when(jnp.logical_and(p == NPAIR - 1, i == n_steps - 2))(phases[1])
        chains = [(sub, h) for sub in range(NSUB) for h in range(2)]
        q_sub = [_head_halves(q_ref[pl.ds(sub * RS, RS), :] * SCALE) for sub in range(NSUB)]
        s_off = lax.broadcasted_iota(jnp.int32, (RS, TK), 1)
        t_pos = [i * TQ + sub * RS + lax.broadcasted_iota(jnp.int32, (RS, TK), 0) for sub in range(NSUB)]
        after = _tri_and_ones("after")
        nblk = (i + 1) * (TQ // TK)

        acc_ref[...] = jnp.zeros_like(acc_ref)
        r_ref[...] = jnp.zeros_like(r_ref)

        def key_rows(block):
            return pl.ds(pl.multiple_of(block * TK, TK), TK)

        def store_scores(z_ref, block):
            kb = k_ref[key_rows(block), :]
            for c, (sub, h) in enumerate(chains):
                z_ref[c] = lax.dot_general(q_sub[sub][h], kb, (((1,), (1,)), ((), ())),
                                           preferred_element_type=F32)

        def add_weighted_values(block):
            vb = v_ref[key_rows(block), :]
            pv = [jnp.dot(w_ref[c], vb, preferred_element_type=F32) for c in range(len(chains))]
            for sub in range(NSUB):
                acc_ref[pl.ds(sub * RS, RS), :] += _join_heads(pv[2 * sub], pv[2 * sub + 1])

        w_ref[...] = jnp.zeros_like(w_ref)
        store_scores(z_even, nblk - 1)

        def step(block, z_ref, z_next_ref):
            add_weighted_values(jnp.minimum(block + 1, nblk - 1))
            store_scores(z_next_ref, jnp.maximum(block - 1, 0))
            mask = [(block * TK + s_off) < t for t in t_pos]
            ls, sums = [], []
            for c, (sub, h) in enumerate(chains):
                ls.append(_log_stay(z_ref[c]))
                sums.append(_dot_hilo(jnp.where(mask[sub], ls[c], 0.0), after))
            for c, (sub, h) in enumerate(chains):
                rows = pl.ds(sub * RS, RS)
                later = r_ref[h, rows, :]
                w = jnp.where(mask[sub], jnp.exp(z_ref[c] + ls[c] + (sums[c][:, :TK] + later)), 0.0)
                w_ref[c] = w.astype(BF16)
                r_ref[h, rows, :] = later + sums[c][:, TK:]

        @pl.loop(0, nblk // 2)
        def _(pair):
            block = nblk - 1 - 2 * pair
            step(block, z_even, z_odd)
            step(block - 1, z_odd, z_even)

        add_weighted_values(0)
        o_ref[...] = acc_ref[...].astype(o_ref.dtype)
        if phases is not None:
            pl.when(jnp.logical_and(p == NPAIR - 1, i == n_steps - 1))(phases[2])

    kwargs, operands = _with_comm(
        comm, [Q_ROWS_SPEC, K_ALL_SPEC, V_ALL_SPEC], [PAIR_ROWS_SPEC, PAIR_TOTAL_SPEC],
        [jax.ShapeDtypeStruct((S, NH * HD), BF16), jax.ShapeDtypeStruct((NH, S, TK), F32)], [qkv, qkv, qkv],
        [pltpu.VMEM((TQ, LANES), F32), pltpu.VMEM((2 * NSUB, RS, TK), F32), pltpu.VMEM((2 * NSUB, RS, TK), F32),
         pltpu.VMEM((2 * NSUB, RS, TK), BF16)])
    return pl.pallas_call(
        body, name=name, grid=(NPAIR, n_steps),
        compiler_params=_cparams(("arbitrary", "arbitrary")), **kwargs,
    )(*operands)


def attn_bwd(qkv, dout, totals, name, comm=None):
    n_steps = S // TQ

    def body(*refs):
        ((q_ref, k_ref, v_ref, do_ref, r_ref), (dq_ref, dk_ref, dv_ref),
         (z_even, z_odd, dw_even, dw_odd, before_ref, dbefore_ref, dz_ref, w_ref), phases) = _comm_hooks(
            comm, refs, 5, 3, 8)
        p = pl.program_id(0)
        i = pl.program_id(1)
        if phases is not None:
            pl.when(jnp.logical_and(p == 0, i == 0))(phases[0])
            pl.when(jnp.logical_and(p == NPAIR - 1, i == n_steps - 2))(phases[1])

        @pl.when(i == 0)
        def _():
            dk_ref[...] = jnp.zeros_like(dk_ref)
            dv_ref[...] = jnp.zeros_like(dv_ref)

        chains = [(sub, h) for sub in range(NSUB) for h in range(2)]
        nch = len(chains)
        qb = q_ref[...]
        dob = do_ref[...].astype(BF16)
        q_sub = [_head_halves(qb[sub * RS:(sub + 1) * RS] * SCALE) for sub in range(NSUB)]
        do_sub = [_head_halves(dob[sub * RS:(sub + 1) * RS]) for sub in range(NSUB)]
        s_off = lax.broadcasted_iota(jnp.int32, (RS, TK), 1)
        t_pos = [i * TQ + sub * RS + lax.broadcasted_iota(jnp.int32, (RS, TK), 0) for sub in range(NSUB)]
        upto = _tri_and_ones("upto")
        before_tri = _tri_and_ones("before")
        contract_lanes = (((1,), (1,)), ((), ()))
        contract_rows = (((0,), (0,)), ((), ()))

        nblk = (i + 1) * (TQ // TK)

        def key_rows(block):
            return pl.ds(pl.multiple_of(block * TK, TK), TK)

        def store_products(z_ref, dw_ref, block):
            kb = k_ref[key_rows(block), :]
            vb = v_ref[key_rows(block), :]
            for c, (sub, h) in enumerate(chains):
                z_ref[c] = lax.dot_general(q_sub[sub][h], kb, contract_lanes, preferred_element_type=F32)
                dw_ref[c] = lax.dot_general(do_sub[sub][h], vb, contract_lanes, preferred_element_type=F32)

        def add_gradients(block):
            kb = k_ref[key_rows(block), :]
            for sub in range(NSUB):
                rows = pl.ds(sub * RS, RS)
                dq_ref[rows, :] += _join_heads(*[jnp.dot(dz_ref[h, rows, :], kb, preferred_element_type=F32)
                                                 for h in range(2)])
            dk_ref[key_rows(block), :] += _join_heads(*[
                lax.dot_general(dz_ref[h], qb, contract_rows, preferred_element_type=F32) for h in range(2)])
            dv_ref[key_rows(block), :] += _join_heads(*[
                lax.dot_general(w_ref[h], dob, contract_rows, preferred_element_type=F32) for h in range(2)])

        for ref in (dq_ref, before_ref, dbefore_ref, dz_ref, w_ref):
            ref[...] = jnp.zeros_like(ref)
        store_products(z_even, dw_even, 0)

        def step(block, z_ref, dw_ref, z_next_ref, dw_next_ref):
            add_gradients(jnp.maximum(block - 1, 0))
            store_products(z_next_ref, dw_next_ref, jnp.minimum(block + 1, nblk - 1))
            mask = [(block * TK + s_off) < t for t in t_pos]
            ls, sums = [], []
            for c, (sub, h) in enumerate(chains):
                ls.append(_log_stay(z_ref[c]))
                sums.append(_dot_hilo(jnp.where(mask[sub], ls[c], 0.0), upto))
            dl, dsums = [], []
            for c, (sub, h) in enumerate(chains):
                rows = pl.ds(sub * RS, RS)
                before = before_ref[c]
                log_after = r_ref[h, rows, :] - (sums[c][:, :TK] + before)
                w = jnp.where(mask[sub], jnp.exp((z_ref[c] + ls[c]) + log_after), 0.0)
                dl.append(dw_ref[c] * w)
                dsums.append(_dot_hilo(dl[c], before_tri))
                w_ref[h, rows, :] = w.astype(BF16)
                before_ref[c] = before + sums[c][:, TK:]
            for c, (sub, h) in enumerate(chains):
                rows = pl.ds(sub * RS, RS)
                dbefore = dbefore_ref[c]
                beta = jnp.where(mask[sub], jnp.exp(z_ref[c] + ls[c]), 0.0)
                dstay = dsums[c][:, :TK] + dbefore
                dz_ref[h, rows, :] = ((dl[c] * (1.0 - beta) - beta * dstay) * SCALE).astype(BF16)
                dbefore_ref[c] = dbefore + dsums[c][:, TK:]

        @pl.loop(0, nblk // 2)
        def _(pair):
            step(2 * pair, z_even, dw_even, z_odd, dw_odd)
            step(2 * pair + 1, z_odd, dw_odd, z_even, dw_even)

        add_gradients(nblk - 1)
        if phases is not None:
            pl.when(jnp.logical_and(p == NPAIR - 1, i == n_steps - 1))(phases[2])

    full = jax.ShapeDtypeStruct((S, NH * HD), F32)
    kwargs, operands = _with_comm(
        comm, [Q_ROWS_SPEC, K_ALL_SPEC, V_ALL_SPEC, PAIR_ROWS_SPEC, PAIR_TOTAL_SPEC],
        [PAIR_ROWS_SPEC, PAIR_ALL_SPEC, PAIR_ALL_SPEC], [full, full, full], [qkv, qkv, qkv, dout, totals],
        [pltpu.VMEM((2 * NSUB, RS, TK), F32)] * 6 + [pltpu.VMEM((2, TQ, TK), BF16)] * 2)
    return pl.pallas_call(
        body, name=name, grid=(NPAIR, n_steps),
        compiler_params=_cparams(("arbitrary", "arbitrary")), **kwargs,
    )(*operands)


def _proj_cols(first_col):
    base = first_col // LANES
    return pl.BlockSpec((S, LANES), lambda j: (0, base + j))


CONV_OUT_SPEC = pl.BlockSpec((S, LANES), lambda j: (0, j))
CONV_DOUT_SPEC = pl.BlockSpec((S, LANES), lambda j: (0, (NH * HD) // LANES + j))
CONV_W_SPEC = pl.BlockSpec((8, LANES), lambda j: (0, j))
CONV_B_SPEC = pl.BlockSpec((1, LANES), lambda j: (0, j))


def _shift_down(u, n):
    rows = lax.broadcasted_iota(jnp.int32, u.shape, 0)
    return jnp.where(rows >= n, pltpu.roll(u, n, 0), 0.0)


def _shift_up(u, n):
    rows = lax.broadcasted_iota(jnp.int32, u.shape, 0)
    return jnp.where(rows < S - n, pltpu.roll(u, S - n, 0), 0.0)


def conv_fwd(proj, cw8, cb, name):
    def body(bg_ref, cg_ref, hc_ref, w_ref, b_ref, o_ref):
        u = cg_ref[...] * hc_ref[...]
        w = w_ref[...]
        y = w[0:1, :] * _shift_down(u, 2) + w[1:2, :] * _shift_down(u, 1) + w[2:3, :] * u + b_ref[...]
        o_ref[...] = bg_ref[...] * y

    return pl.pallas_call(
        body, name=name, grid=(CW // LANES,),
        in_specs=[_proj_cols(0), _proj_cols(CW), _proj_cols(2 * CW), CONV_W_SPEC, CONV_B_SPEC],
        out_specs=CONV_OUT_SPEC, out_shape=jax.ShapeDtypeStruct((S, CW), F32),
        compiler_params=_cparams(("parallel",)),
    )(proj, proj, proj, cw8, cb)


def conv_bwd(proj, dout, cw8, cb, name):
    def body(bg_ref, cg_ref, hc_ref, do_ref, w_ref, b_ref, dbg_ref, dcg_ref, dhc_ref, dw_ref, db_ref):
        cg, hc, do = cg_ref[...], hc_ref[...], do_ref[...]
        w = w_ref[...]
        u = cg * hc
        u1, u2 = _shift_down(u, 1), _shift_down(u, 2)
        y = w[0:1, :] * u2 + w[1:2, :] * u1 + w[2:3, :] * u + b_ref[...]
        dbg_ref[...] = do * y
        dy = do * bg_ref[...]
        db_ref[...] = jnp.sum(dy, axis=0, keepdims=True)
        dw_ref[...] = jnp.concatenate(
            [jnp.sum(dy * u2, axis=0, keepdims=True), jnp.sum(dy * u1, axis=0, keepdims=True),
             jnp.sum(dy * u, axis=0, keepdims=True), jnp.zeros((5, LANES), F32)], axis=0)
        du = w[2:3, :] * dy + w[1:2, :] * _shift_up(dy, 1) + w[0:1, :] * _shift_up(dy, 2)
        dcg_ref[...] = du * hc
        dhc_ref[...] = du * cg

    full = jax.ShapeDtypeStruct((S, CW), F32)
    return pl.pallas_call(
        body, name=name, grid=(CW // LANES,),
        in_specs=[_proj_cols(0), _proj_cols(CW), _proj_cols(2 * CW), CONV_DOUT_SPEC, CONV_W_SPEC, CONV_B_SPEC],
        out_specs=[CONV_OUT_SPEC, CONV_OUT_SPEC, CONV_OUT_SPEC, CONV_W_SPEC, CONV_B_SPEC],
        out_shape=[full, full, full, jax.ShapeDtypeStruct((8, CW), F32), jax.ShapeDtypeStruct((1, CW), F32)],
        compiler_params=_cparams(("parallel",)),
    )(proj, proj, proj, dout, cw8, cb)


GELU_K = math.sqrt(2.0 / math.pi)
GELU_C = 0.044715


def _gelu(x):
    return 0.5 * x * (1.0 + jnp.tanh(GELU_K * (x + GELU_C * (x * x * x))))


def _gelu_grad(x):
    t = jnp.tanh(GELU_K * (x + GELU_C * (x * x * x)))
    return 0.5 * (1.0 + t) + 0.5 * x * (1.0 - t * t) * (GELU_K * (1.0 + 3.0 * GELU_C * (x * x)))


def _sg_masks():
    row = lax.broadcasted_iota(jnp.int32, (T, T), 0)
    col = lax.broadcasted_iota(jnp.int32, (T, T), 1)
    causal = jnp.right_shift(row, 6) >= jnp.right_shift(col, 6)
    head_of_col = jnp.right_shift(lax.broadcasted_iota(jnp.int32, (T, CW), 1), 6)
    return causal, head_of_col


def _sg_mixed(vnb, sw_ref, bias, causal, head_of_col):
    mixed = bias
    for h in range(SG_HEADS):
        wh = jnp.where(causal, sw_ref[h], 0.0).astype(BF16)
        mh = jnp.dot(wh, vnb, preferred_element_type=F32)
        mixed = mixed + jnp.where(head_of_col == h, mh, 0.0)
    return mixed


SG_U_SPEC = pl.BlockSpec((T, CW), lambda n: (n, 3))
SG_V_SPEC = pl.BlockSpec((T, CW), lambda n: (n, 4))
SG_ROW_SPEC = pl.BlockSpec((T, CW), lambda n: (n, 0))
SG_DOUT_SPEC = pl.BlockSpec((T, CW), lambda n: (n, 3))
SG_G_SPEC = pl.BlockSpec((1, CW), lambda n: (0, 0))
SG_W_SPEC = pl.BlockSpec((SG_HEADS, T, T), lambda n: (0, 0, 0))
SG_BIAS_SPEC = pl.BlockSpec((T, CW), lambda n: (0, 0))


def sg_fwd(proj, gn, sw, bias, name):
    def body(u_ref, v_ref, g_ref, sw_ref, bias_ref, o_ref):
        causal, head_of_col = _sg_masks()
        gv = _gelu(v_ref[...])
        rstd = lax.rsqrt(jnp.mean(gv * gv, axis=-1, keepdims=True) + EPS)
        vnb = ((gv * rstd) * g_ref[...]).astype(BF16)
        mixed = _sg_mixed(vnb, sw_ref, bias_ref[...], causal, head_of_col)
        o_ref[...] = _gelu(u_ref[...]) * mixed

    return pl.pallas_call(
        body, name=name, grid=(S // T,),
        in_specs=[SG_U_SPEC, SG_V_SPEC, SG_G_SPEC, SG_W_SPEC, SG_BIAS_SPEC],
        out_specs=SG_ROW_SPEC, out_shape=jax.ShapeDtypeStruct((S, CW), F32),
        compiler_params=_cparams(("parallel",)),
    )(proj, proj, gn, sw, bias)


def sg_bwd(proj, dout, gn, sw, bias, name):
    def body(u_ref, v_ref, do_ref, g_ref, sw_ref, bias_ref, du_ref, dv_ref, dg_ref, dsw_ref, dbias_ref):
        @pl.when(pl.program_id(0) == 0)
        def _():
            dg_ref[...] = jnp.zeros_like(dg_ref)
            dsw_ref[...] = jnp.zeros_like(dsw_ref)
            dbias_ref[...] = jnp.zeros_like(dbias_ref)

        causal, head_of_col = _sg_masks()
        uv, vv, do, gnv = u_ref[...], v_ref[...], do_ref[...], g_ref[...]
        gv = _gelu(vv)
        rstd = lax.rsqrt(jnp.mean(gv * gv, axis=-1, keepdims=True) + EPS)
        xhat = gv * rstd
        vnb = (xhat * gnv).astype(BF16)
        mixed = _sg_mixed(vnb, sw_ref, bias_ref[...], causal, head_of_col)
        du_ref[...] = (do * mixed) * _gelu_grad(uv)
        dmix = do * _gelu(uv)
        dbias_ref[...] += dmix
        dmixb = dmix.astype(BF16)
        dvn = jnp.zeros((T, CW), F32)
        for h in range(SG_HEADS):
            wh = jnp.where(causal, sw_ref[h], 0.0).astype(BF16)
            dvh = lax.dot_general(wh, dmixb, (((0,), (0,)), ((), ())), preferred_element_type=F32)
            dvn = dvn + jnp.where(head_of_col == h, dvh, 0.0)
            dmh = jnp.where(head_of_col == h, dmixb, jnp.zeros_like(dmixb))
            dwh = lax.dot_general(dmh, vnb, (((1,), (1,)), ((), ())), preferred_element_type=F32)
            dsw_ref[h] += jnp.where(causal, dwh, 0.0)
        dg_ref[...] += jnp.sum(dvn * xhat, axis=0, keepdims=True)
        dxhat = dvn * gnv
        dgv = rstd * (dxhat - xhat * jnp.mean(dxhat * xhat, axis=-1, keepdims=True))
        dv_ref[...] = dgv * _gelu_grad(vv)

    full = jax.ShapeDtypeStruct((S, CW), F32)
    return pl.pallas_call(
        body, name=name, grid=(S // T,),
        in_specs=[SG_U_SPEC, SG_V_SPEC, SG_DOUT_SPEC, SG_G_SPEC, SG_W_SPEC, SG_BIAS_SPEC],
        out_specs=[SG_ROW_SPEC, SG_ROW_SPEC, SG_G_SPEC, SG_W_SPEC, SG_BIAS_SPEC],
        out_shape=[full, full, jax.ShapeDtypeStruct((1, CW), F32),
                   jax.ShapeDtypeStruct((SG_HEADS, T, T), F32), jax.ShapeDtypeStruct((T, CW), F32)],
        compiler_params=_cparams(("arbitrary",)),
    )(proj, proj, dout, gn, sw, bias)


ADA_COLS = NMOD * D // NDEV


def ada_fwd(c_all, ada_w, ada_b_mine, name):
    def body(c_ref, w_ref, b_ref, o_ref, ca_ref):
        cv = c_ref[...]
        ca = cv * (1.0 / (1.0 + jnp.exp(-cv)))
        ca_ref[...] = ca
        cab = ca.astype(BF16)
        for l in range(L):
            o_ref[l] = jnp.dot(cab, w_ref[l].astype(BF16), preferred_element_type=F32) + b_ref[l]

    return pl.pallas_call(
        body, name=name,
        out_shape=[jax.ShapeDtypeStruct((L, NDEV, ADA_COLS), F32), jax.ShapeDtypeStruct((NDEV, D), F32)],
        compiler_params=_cparams(),
    )(c_all, ada_w, ada_b_mine)


def ada_bwd(ca, dmod_cols, name):
    def body(ca_ref, dm_ref, o_ref):
        cab = ca_ref[...].astype(BF16)
        for l in range(L):
            o_ref[l] = lax.dot_general(cab, dm_ref[l].astype(BF16), (((0,), (0,)), ((), ())),
                                       preferred_element_type=F32)

    return pl.pallas_call(
        body, name=name, out_shape=jax.ShapeDtypeStruct((L, D, ADA_COLS), F32),
        compiler_params=_cparams(),
    )(ca, dmod_cols)


def _adamw(w, g, m, v):
    m = B1 * m + (1.0 - B1) * g
    v = B2 * v + (1.0 - B2) * (g * g)
    m_hat = m / BC1
    v_hat = v / BC2
    delta = -LR * (m_hat / (jnp.sqrt(v_hat) + AEPS) + WD * w)
    return delta, m, v


def sum_gathered(parts, name):
    _, rows, cols = parts.shape

    def body(p_ref, o_ref):
        acc = p_ref[0]
        for d in range(1, NDEV):
            acc = acc + p_ref[d]
        o_ref[...] = acc

    return pl.pallas_call(
        body, name=name, out_shape=jax.ShapeDtypeStruct((rows, cols), F32),
        compiler_params=_cparams(),
    )(parts)


def adamw_plain(w, g, m, v, tr, name):
    rows, cols = w.shape
    spec = pl.BlockSpec((tr, cols), lambda i: (i, 0))

    def body(w_ref, g_ref, m_ref, v_ref, d_ref, nm_ref, nv_ref):
        delta, nm, nv = _adamw(w_ref[...], g_ref[...], m_ref[...], v_ref[...])
        d_ref[...] = delta
        nm_ref[...] = nm
        nv_ref[...] = nv

    shp = jax.ShapeDtypeStruct((rows, cols), F32)
    return pl.pallas_call(
        body, name=name, grid=(rows // tr,), in_specs=[spec] * 4, out_specs=[spec] * 3,
        out_shape=[shp, shp, shp], compiler_params=_cparams(("parallel",)),
    )(w, g, m, v)


def adamw_reduce(w, parts, m, v, tr, name):
    _, rows, cols = w.shape
    spec = pl.BlockSpec((None, tr, cols), lambda l, i: (l, i, 0))
    pspecs = [pl.BlockSpec((NDEV, tr, cols), lambda l, i, k=k: (0, jnp.where(l == k, i, 0), 0)) for k in range(L)]

    def body(w_ref, p0_ref, p1_ref, m_ref, v_ref, g_ref, d_ref, nm_ref, nv_ref):
        first_layer = pl.program_id(0) == 0
        g = jnp.zeros((tr, cols), F32)
        for d in range(NDEV):
            g = g + jnp.where(first_layer, p0_ref[d], p1_ref[d]).astype(F32)
        delta, nm, nv = _adamw(w_ref[...], g, m_ref[...], v_ref[...])
        g_ref[...] = g
        d_ref[...] = delta
        nm_ref[...] = nm
        nv_ref[...] = nv

    shp = jax.ShapeDtypeStruct(w.shape, F32)
    return pl.pallas_call(
        body, name=name, grid=(L, rows // tr), in_specs=[spec] + pspecs + [spec, spec], out_specs=[spec] * 4,
        out_shape=[shp] * 4, compiler_params=_cparams(("parallel", "parallel")),
    )(w, *parts, m, v)


def _pad_rows(flat, rows):
    return jnp.pad(flat, (0, rows * LANES - flat.shape[0])).reshape(rows, LANES)


def kernel(x, c, ada_w, ada_b, norm_mix_g, norm_mlp_g, w_in, conv_w, conv_b, gmlp_norm_g, spatial_w, spatial_b, w_out, mlp_w1, mlp_w2, final_norm_g, loss_target, m_ada_w, m_ada_b, m_norm_mix_g, m_norm_mlp_g, m_w_in, m_conv_w, m_conv_b, m_gmlp_norm_g, m_spatial_w, m_spatial_b, m_w_out, m_mlp_w1, m_mlp_w2, m_final_norm_g, v_ada_w, v_ada_b, v_norm_mix_g, v_norm_mlp_g, v_w_in, v_conv_w, v_conv_b, v_gmlp_norm_g, v_spatial_w, v_spatial_b, v_w_out, v_mlp_w1, v_mlp_w2, v_final_norm_g):
    me = _lin(_my_pos())
    x0 = x[0]
    target = loss_target[0]
    conv_shard = conv_w.shape[-1]

    w_in_b, w_out_b, w1_b, w2_b = [w.astype(BF16) for w in (w_in, w_out, mlp_w1, mlp_w2)]
    pack0 = _pad_rows(jnp.concatenate([c.reshape(-1), conv_w.reshape(-1)]), 16)
    g0, gw_in0 = run_comm(Gather([pack0, w_in_b[0]]), "gather_first")
    g0 = g0.reshape(NDEV, 16 * LANES)
    c_all = g0[:, :D]
    conv_full = (g0[:, D:D + L * 3 * conv_shard].reshape(NDEV, L, 3, conv_shard)
                 .transpose(1, 2, 0, 3).reshape(L, 3, CW))

    def canonical_w_in(gathered):
        return gathered.transpose(1, 0, 2).reshape(D, PROJ)

    weight_plans = [Gather([w_out_b[0], w1_b[0], w2_b[0], w_in_b[1]]), Gather([w_out_b[1], w1_b[1], w2_b[1]])]
    W_in = [canonical_w_in(gw_in0), None]
    W_out, W1, W2 = [None] * L, [None] * L, [None] * L

    ada_b_mine = lax.dynamic_slice(ada_b, (0, me * ADA_COLS), (L, ADA_COLS)).reshape(L, 1, ADA_COLS)
    mod_part, c_act = ada_fwd(c_all, ada_w, ada_b_mine, "ada_fwd")
    gmod = run_comm(Gather([mod_part]), "gather_mod")[0]
    mod = lax.dynamic_index_in_dim(gmod, me, axis=2, keepdims=False)
    mod = mod.transpose(1, 0, 2).reshape(L, NMOD, 1, D)

    cw8 = jnp.pad(conv_full, ((0, 0), (0, 5), (0, 0)))
    sg_bias = jnp.repeat(spatial_b.transpose(0, 2, 1), HD, axis=2)

    saved = []
    xl = x0
    for l in range(L):
        sh_m, sc_m, g_m, sh_f, sc_f, g_f = [mod[l, k] for k in range(NMOD)]
        h1 = normmod_fwd(xl, norm_mix_g[l:l + 1], sc_m, sh_m, f"norm_mix_fwd{l}")
        qkv = mm_layer("proj_qkv", l, h1, W_in[l], out_dtypes=[BF16], cols=(0, QKV))[0]
        proj = mm_layer("proj_rest", l, h1, W_in[l], out_dtypes=[F32], cols=(QKV, REST))[0]
        a_out, a_tot, *gathered = attn_fwd(qkv, f"attn_fwd{l}", comm=weight_plans[l])
        W_out[l] = gathered[0].reshape(D, D)
        W1[l] = gathered[1]
        W2[l] = gathered[2].reshape(DFF, D)
        if l + 1 < L:
            W_in[l + 1] = canonical_w_in(gathered[3])
        c_out = conv_fwd(proj, cw8[l], conv_b[l:l + 1], f"conv_fwd{l}")
        s_out = sg_fwd(proj, gmlp_norm_g[l:l + 1], spatial_w[l], sg_bias[l], f"sg_fwd{l}")
        cat = jnp.concatenate([a_out, c_out.astype(BF16), s_out.astype(BF16)], axis=1)
        mix, x1 = mm_layer("mix", l, cat, W_out[l], out_dtypes=[F32, F32],
                           epilogue=lambda acc, xr, g: (acc, xr + g * acc), extras=[(xl, "tile"), (g_m, "col")])
        h2 = normmod_fwd(x1, norm_mlp_g[l:l + 1], sc_f, sh_f, f"norm_mlp_fwd{l}")
        ra, r = mm_layer("mlp_up", l, h2, W1[l], out_dtypes=[BF16, BF16], b_blocks=True,
                         epilogue=lambda acc: (jnp.maximum(acc, 0.0), jnp.square(jnp.maximum(acc, 0.0))))
        m2, x2 = mm_layer("mlp_down", l, r, W2[l], out_dtypes=[F32, F32],
                          epilogue=lambda acc, xr, g: (acc, xr + g * acc), extras=[(x1, "tile"), (g_f, "col")])
        saved.append(dict(x=xl, h1=h1, proj=proj, qkv=qkv, a_tot=a_tot, cat=cat, mix=mix,
                          x1=x1, h2=h2, ra=ra, r=r, m2=m2))
        xl = x2

    dx, loss_part, d_final_g = loss_head(xl, target, final_norm_g.reshape(1, D), "loss_head")

    dmod = [None] * L
    p_in, p_out, p_w1, p_w2 = [None] * L, [None] * L, [None] * L, [None] * L
    pending_w_in = None
    d_norm_mix, d_norm_mlp, d_conv_w, d_conv_b = [None] * L, [None] * L, [None] * L, [None] * L
    d_gn, d_sw, d_sb = [None] * L, [None] * L, [None] * L
    for l in reversed(range(L)):
        sv = saved[l]
        sh_m, sc_m, g_m, sh_f, sc_f, g_f = [mod[l, k] for k in range(NMOD)]
        dm2, dg_f = gate_bwd(dx, sv["m2"], g_f, f"gate_mlp_bwd{l}")
        da = mm_layer("mlp_down_dgrad", l, dm2, W2[l], out_dtypes=[BF16], trans_b=True,
                      epilogue=lambda acc, rav: (acc * (2.0 * rav.astype(F32)),), extras=[(sv["ra"], "tile")])[0]
        dW2 = mm_layer("mlp_down_wgrad", l, sv["r"], dm2, out_dtypes=[BF16], trans_a=True)[0]
        dW1 = mm_layer("mlp_up_wgrad", l, sv["h2"], da, out_dtypes=[BF16], trans_a=True, out_blocks=True)[0]
        dh2 = mm_layer("mlp_up_dgrad", l, da, W1[l], out_dtypes=[F32], trans_b=True, b_blocks=True)[0]
        dx1, dsc_f, dsh_f, d_norm_mlp[l] = normmod_bwd(sv["x1"], dh2, dx, norm_mlp_g[l:l + 1], sc_f,
                                                       f"norm_mlp_bwd{l}")
        dmix, dg_m = gate_bwd(dx1, sv["mix"], g_m, f"gate_mix_bwd{l}")
        dcat = mm_layer("mix_dgrad", l, dmix, W_out[l], out_dtypes=[F32], trans_b=True)[0]
        dW_out = mm_layer("mix_wgrad", l, sv["cat"], dmix, out_dtypes=[BF16], trans_a=True)[0]
        ready = [dW2.reshape(NDEV, DFF // NDEV, D), dW1, dW_out.reshape(NDEV, D // NDEV, D)]
        if pending_w_in is not None:
            ready = [pending_w_in] + ready
        dq, dk, dv, *arrived = attn_bwd(sv["qkv"], dcat, sv["a_tot"], f"attn_bwd{l}", comm=Exchange(ready))
        if pending_w_in is not None:
            p_in[l + 1] = arrived.pop(0)
        p_w2[l], p_w1[l], p_out[l] = arrived
        dbg, dcg, dhc, dcw8, d_conv_b[l] = conv_bwd(sv["proj"], dcat, cw8[l], conv_b[l:l + 1], f"conv_bwd{l}")
        d_conv_w[l] = dcw8[:3]
        dus, dvs, d_gn[l], dsw, dbias = sg_bwd(sv["proj"], dcat, gmlp_norm_g[l:l + 1],
                                               spatial_w[l], sg_bias[l], f"sg_bwd{l}")
        d_sw[l] = dsw
        d_sb[l] = dbias.reshape(T, SG_HEADS, HD).sum(axis=2).T
        dproj = jnp.concatenate([dq, dk, dv, dbg, dcg, dhc, dus, dvs], axis=1).astype(BF16)
        dW_in = mm_layer("proj_wgrad", l, sv["h1"], dproj, out_dtypes=[BF16], trans_a=True)[0]
        pending_w_in = dW_in.reshape(D, NDEV, PROJ // NDEV).transpose(1, 0, 2)
        dh1 = mm_layer("proj_dgrad", l, dproj, W_in[l], out_dtypes=[F32], trans_b=True)[0]
        dx, dsc_m, dsh_m, d_norm_mix[l] = normmod_bwd(sv["x"], dh1, dx1, norm_mix_g[l:l + 1], sc_m,
                                                      f"norm_mix_bwd{l}")
        dmod[l] = jnp.concatenate([dsh_m, dsc_m, dg_m, dsh_f, dsc_f, dg_f], axis=1)

    grad_x = dx.reshape(1, S, D)

    small_parts = [jnp.concatenate(dmod, axis=0), jnp.concatenate(d_norm_mix, axis=0),
                   jnp.concatenate(d_norm_mlp, axis=0), jnp.stack(d_conv_w), jnp.concatenate(d_conv_b, axis=0),
                   jnp.concatenate(d_gn, axis=0), jnp.stack(d_sw), jnp.stack(d_sb), d_final_g, loss_part[:, :1]]
    sizes = [p.size for p in small_parts]
    small_rows = -(-sum(sizes) // (8 * LANES)) * 8
    small_pack = _pad_rows(jnp.concatenate([p.reshape(-1) for p in small_parts]), small_rows)
    small_all = run_comm(Gather([small_pack]), "gather_small_grads")[0]
    small_sum = sum_gathered(small_all, "sum_small_grads").reshape(-1)
    offs = [0]
    for sz in sizes:
        offs.append(offs[-1] + sz)
    summed = [small_sum[offs[k]:offs[k + 1]].reshape(small_parts[k].shape) for k in range(len(sizes))]
    (g_ada_b, g_norm_mix, g_norm_mlp, g_conv_w_full, g_conv_b, g_gn, g_sw, g_sb, g_final, loss_sum) = summed
    loss = loss_sum.reshape(())
    g_final = g_final.reshape(D)
    g_conv_w = lax.dynamic_slice(g_conv_w_full, (0, 0, me * conv_shard), (L, 3, conv_shard))

    dmod_all = small_all.reshape(NDEV, -1)[:, :L * NMOD * D].reshape(NDEV, L, NMOD * D)
    dmod_cols = lax.dynamic_slice(dmod_all, (0, 0, me * ADA_COLS), (NDEV, L, ADA_COLS)).transpose(1, 0, 2)
    g_ada_w = ada_bwd(c_act, dmod_cols, "ada_bwd")

    p_in[0] = run_comm(Exchange([pending_w_in]), "exchange_last")[0]
    g_w_in, d_w_in, nm_w_in, nv_w_in = adamw_reduce(w_in, p_in, m_w_in, v_w_in, 256, "adamw_w_in")
    g_w_out, d_w_out, nm_w_out, nv_w_out = adamw_reduce(w_out, p_out, m_w_out, v_w_out, 128, "adamw_w_out")
    g_w1, d_w1, nm_w1, nv_w1 = adamw_reduce(mlp_w1, p_w1, m_mlp_w1, v_mlp_w1, 256, "adamw_mlp_w1")
    g_w2, d_w2, nm_w2, nv_w2 = adamw_reduce(mlp_w2, p_w2, m_mlp_w2, v_mlp_w2, 256, "adamw_mlp_w2")

    flat2 = lambda t: t.reshape(L * D, ADA_COLS)
    d_ada_w, nm_ada_w, nv_ada_w = [t.reshape(L, D, ADA_COLS) for t in adamw_plain(
        flat2(ada_w), flat2(g_ada_w), flat2(m_ada_w), flat2(v_ada_w), 256, "adamw_ada_w")]

    small_w = [ada_b, norm_mix_g, norm_mlp_g, conv_w, conv_b, gmlp_norm_g, spatial_w, spatial_b, final_norm_g]
    small_m = [m_ada_b, m_norm_mix_g, m_norm_mlp_g, m_conv_w, m_conv_b, m_gmlp_norm_g, m_spatial_w, m_spatial_b,
               m_final_norm_g]
    small_v = [v_ada_b, v_norm_mix_g, v_norm_mlp_g, v_conv_w, v_conv_b, v_gmlp_norm_g, v_spatial_w, v_spatial_b,
               v_final_norm_g]
    small_g = [g_ada_b, g_norm_mix, g_norm_mlp, g_conv_w, g_conv_b, g_gn, g_sw, g_sb, g_final]
    wsizes = [p.size for p in small_w]
    wrows = -(-sum(wsizes) // (8 * LANES)) * 8
    pack = lambda ps: _pad_rows(jnp.concatenate([p.reshape(-1) for p in ps]), wrows)
    sd, snm, snv = adamw_plain(pack(small_w), pack(small_g), pack(small_m), pack(small_v), wrows, "adamw_small")
    woffs = [0]
    for sz in wsizes:
        woffs.append(woffs[-1] + sz)

    def unpack(flat):
        flat = flat.reshape(-1)
        return [flat[woffs[k]:woffs[k + 1]].reshape(small_w[k].shape) for k in range(len(small_w))]

    sd, snm, snv = unpack(sd), unpack(snm), unpack(snv)

    def ordered(big, small):
        ada, win, wout, w1, w2 = big
        return [ada, small[0], small[1], small[2], win, small[3], small[4], small[5], small[6], small[7],
                wout, w1, w2, small[8]]

    grads = ordered([g_ada_w, g_w_in, g_w_out, g_w1, g_w2], small_g)
    deltas = ordered([d_ada_w, d_w_in, d_w_out, d_w1, d_w2], sd)
    new_m = ordered([nm_ada_w, nm_w_in, nm_w_out, nm_w1, nm_w2], snm)
    new_v = ordered([nv_ada_w, nv_w_in, nv_w_out, nv_w1, nv_w2], snv)
    return (loss, grad_x, *grads, *deltas, *new_m, *new_v)
```

```python
import functools
import math

import jax
import jax.numpy as jnp
from jax import lax
from jax.experimental import pallas as pl
from jax.experimental.pallas import tpu as pltpu

F32 = jnp.float32
BF16 = jnp.bfloat16
MESH = pl.DeviceIdType.MESH

S = 2048
D = 1024
L = 2
NDEV = 8
HD = 64
NH = 8
PROJ = 2816
DFF = 4096
NMOD = 6
EPS = 1e-6
T = 128
SG_HEADS = 4
LANES = 128
CW = 256
QKV = 3 * NH * HD
REST = PROJ - QKV

LR, B1, B2, AEPS, WD, STEP = 0.001, 0.9, 0.999, 1e-08, 0.01, 10
BC1 = 1.0 - B1 ** STEP
BC2 = 1.0 - B2 ** STEP

VMEM_LIMIT = 48 * 1024 * 1024

HBM_SPEC = pl.BlockSpec(memory_space=pltpu.HBM)


def _cparams(sem=None):
    return pltpu.CompilerParams(dimension_semantics=sem, vmem_limit_bytes=VMEM_LIMIT)


def _my_pos():
    return lax.axis_index("x"), lax.axis_index("y"), lax.axis_index("c")


def _lin(p):
    return 4 * p[0] + 2 * p[1] + p[2]


class Gather:
    def __init__(self, arrs):
        self.arrs = list(arrs)
        n = len(self.arrs)
        self.out_shape = [jax.ShapeDtypeStruct((NDEV,) + a.shape, a.dtype) for a in self.arrs]
        self.scratch = [pltpu.SemaphoreType.DMA((n, 7)), pltpu.SemaphoreType.DMA((n, 7)),
                        pltpu.SemaphoreType.DMA((n,))]

    def phases(self, ins, outs, sems):
        n = len(self.arrs)
        send_sems, recv_sems, local_sems = sems
        x, y, c = _my_pos()
        me, sibling = (x, y, c), (x, y, 1 - c)
        chips = [(1 - x, y), (x, 1 - y), (1 - x, 1 - y)]

        def copy(a, k, block, to, src=None):
            slot = outs[a].at[_lin(block)]
            return pltpu.make_async_remote_copy(
                src_ref=slot if src is None else src, dst_ref=slot,
                send_sem=send_sems.at[a, k], recv_sem=recv_sems.at[a, k],
                device_id=to, device_id_type=MESH)

        def mine(a):
            return pltpu.make_async_copy(ins[a], outs[a].at[_lin(me)], local_sems.at[a])

        def first(a):
            return [copy(a, 0, me, sibling, src=ins[a])] + [
                copy(a, 1 + j, me, (*chip, c), src=ins[a]) for j, chip in enumerate(chips)]

        def passed(a):
            return [copy(a, 4 + j, (*chip, c), sibling) for j, chip in enumerate(chips)]

        def start():
            for a in range(n):
                mine(a).start()
                for cp in first(a):
                    cp.start()

        def relay():
            for j, chip in enumerate(chips):
                for a in range(n):
                    copy(a, 1 + j, (*chip, c), me).wait_recv()
                    passed(a)[j].start()

        def finish():
            for a in range(n):
                copy(a, 0, sibling, me).wait_recv()
            for j, chip in enumerate(chips):
                for a in range(n):
                    copy(a, 4 + j, (*chip, 1 - c), me).wait_recv()
            for a in range(n):
                for cp in first(a) + passed(a):
                    cp.wait_send()
                mine(a).wait()

        return start, relay, finish


class Exchange:
    def __init__(self, arrs):
        self.arrs = list(arrs)
        n = len(self.arrs)
        self.out_shape = [jax.ShapeDtypeStruct(a.shape, a.dtype) for a in self.arrs]
        self.scratch = [pltpu.SemaphoreType.DMA((n, 7)), pltpu.SemaphoreType.DMA((n, 7)),
                        pltpu.SemaphoreType.DMA((n,))]

    def phases(self, ins, outs, sems):
        n = len(self.arrs)
        send_sems, recv_sems, local_sems = sems
        x, y, c = _my_pos()
        me = (x, y, c)

        def peer(mask):
            return (1 - x if mask & 4 else x, 1 - y if mask & 2 else y, 1 - c if mask & 1 else c)

        def copy(a, mask):
            return pltpu.make_async_remote_copy(
                src_ref=ins[a].at[_lin(peer(mask))], dst_ref=outs[a].at[_lin(me)],
                send_sem=send_sems.at[a, mask - 1], recv_sem=recv_sems.at[a, mask - 1],
                device_id=peer(mask), device_id_type=MESH)

        def arrival(a, mask):
            return pltpu.make_async_remote_copy(
                src_ref=ins[a].at[_lin(me)], dst_ref=outs[a].at[_lin(peer(mask))],
                send_sem=send_sems.at[a, mask - 1], recv_sem=recv_sems.at[a, mask - 1],
                device_id=peer(mask), device_id_type=MESH)

        def mine(a):
            return pltpu.make_async_copy(ins[a].at[_lin(me)], outs[a].at[_lin(me)], local_sems.at[a])

        def start():
            for a in range(n):
                mine(a).start()
            for mask in (4, 2, 6, 1, 5, 3, 7):
                for a in range(n):
                    copy(a, mask).start()

        def relay():
            pass

        def finish():
            for mask in range(1, 8):
                for a in range(n):
                    arrival(a, mask).wait_recv()
            for mask in range(1, 8):
                for a in range(n):
                    copy(a, mask).wait_send()
            for a in range(n):
                mine(a).wait()

        return start, relay, finish


def run_comm(plan, name):
    n = len(plan.arrs)

    def body(*refs):
        start, relay, finish = plan.phases(refs[:n], refs[n:2 * n], refs[2 * n:])
        start()
        relay()
        finish()

    outs = pl.pallas_call(
        body, name=name, out_shape=plan.out_shape,
        in_specs=[HBM_SPEC] * n, out_specs=[HBM_SPEC] * n, scratch_shapes=plan.scratch,
    )(*plan.arrs)
    return list(outs)


SEM_SPEC = pl.BlockSpec(memory_space=pltpu.SEMAPHORE)
DATAFLOW = pltpu.SideEffectType.DATAFLOW_SIDE_EFFECTING


def _exchange_copies(src_ref, land_ref, send_sems, recv_sems):
    x, y, c = _my_pos()
    me = (x, y, c)
    sends, arrivals = [], []
    for mask in (4, 2, 6, 1, 5, 3, 7):
        peer = (1 - x if mask & 4 else x, 1 - y if mask & 2 else y, 1 - c if mask & 1 else c)
        sends.append(pltpu.make_async_remote_copy(
            src_ref=src_ref.at[_lin(peer)], dst_ref=land_ref.at[_lin(me)],
            send_sem=send_sems.at[mask - 1], recv_sem=recv_sems.at[mask - 1], device_id=peer, device_id_type=MESH))
        arrivals.append(pltpu.make_async_remote_copy(
            src_ref=src_ref.at[_lin(me)], dst_ref=land_ref.at[_lin(peer)],
            send_sem=send_sems.at[mask - 1], recv_sem=recv_sems.at[mask - 1], device_id=peer, device_id_type=MESH))
    return sends, arrivals


def exchange_start(pieces, landing, name):
    def body(src_ref, land_ref, send_sems, recv_sems, src_thru, land_thru, token):
        sends, _ = _exchange_copies(src_ref, land_ref, send_sems, recv_sems)
        for cp in sends:
            cp.start()
        token[...] = jnp.zeros_like(token)

    hbm = lambda a: pltpu.HBM(a.shape, a.dtype)
    return pl.pallas_call(
        body, name=name,
        out_shape=(pltpu.SemaphoreType.DMA((7,)), pltpu.SemaphoreType.DMA((7,)), hbm(pieces), hbm(landing),
                   jax.ShapeDtypeStruct((8, LANES), F32)),
        in_specs=(HBM_SPEC, HBM_SPEC),
        out_specs=(SEM_SPEC, SEM_SPEC, HBM_SPEC, HBM_SPEC, pl.BlockSpec(memory_space=pltpu.VMEM)),
        input_output_aliases={0: 2, 1: 3},
        compiler_params=pltpu.CompilerParams(has_side_effects=DATAFLOW),
    )(pltpu.with_memory_space_constraint(pieces, pltpu.HBM), pltpu.with_memory_space_constraint(landing, pltpu.HBM))


def exchange_wait(send_sems, recv_sems, pieces, landing, after, name):
    def body(src_ref, land_ref, send_sems, recv_sems, after_ref, src_dead, got_ref):
        sends, arrivals = _exchange_copies(src_ref, land_ref, send_sems, recv_sems)
        for cp in sends:
            cp.wait_send()
        for cp in arrivals:
            cp.wait_recv()

    hbm = lambda a: pltpu.HBM(a.shape, a.dtype)
    return pl.pallas_call(
        body, name=name, out_shape=(hbm(pieces), hbm(landing)),
        in_specs=(HBM_SPEC, HBM_SPEC, SEM_SPEC, SEM_SPEC, pl.BlockSpec(memory_space=pl.ANY)),
        out_specs=(HBM_SPEC, HBM_SPEC), input_output_aliases={0: 0, 1: 1},
        compiler_params=pltpu.CompilerParams(has_side_effects=DATAFLOW),
    )(pieces, landing, send_sems, recv_sems, after)[1]


MM_TILES = {
    "proj_qkv": (S, 512), "proj_rest": (S, 256), "mix": (1024, 512), "mlp_up": (S, 512), "mlp_down": (1024, 256),
    "mlp_down_dgrad": (1024, 1024), "mlp_down_wgrad": (1024, 1024), "mlp_up_wgrad": (1024, 512),
    "mlp_up_dgrad": (1024, 512), "mix_dgrad": (1024, 512), "mix_wgrad": (512, 1024),
    "proj_wgrad": (1024, PROJ // 2), "proj_dgrad": (1024, 512),
}


def mm_layer(kind, l, a, b, **kw):
    tm, tn = MM_TILES[kind]
    return mm(a, b, tm=tm, tn=tn, name=f"{kind}{l}", **kw)


def mm(a, b, *, tm, tn, out_dtypes, epilogue=None, extras=(), name, trans_a=False, trans_b=False,
       cols=None, b_blocks=False, out_blocks=False):
    if trans_a:
        kdim, m = a.shape
    else:
        m, kdim = a.shape
    shard = b.shape[-1] if b_blocks else None
    if b_blocks:
        full = (b.shape[1], NDEV * shard)
    else:
        full = b.shape
    first, ncols = cols if cols is not None else (0, full[0] if trans_b else full[1])
    assert full[1 if trans_b else 0] == kdim and m % tm == 0 and ncols % tn == 0 and first % tn == 0
    j0 = first // tn
    if trans_a:
        a_spec = pl.BlockSpec((kdim, tm), lambda i, j: (0, i))
    else:
        a_spec = pl.BlockSpec((tm, kdim), lambda i, j: (i, 0))
    if b_blocks and trans_b:
        b_spec = pl.BlockSpec((NDEV, tn, shard), lambda i, j: (0, j0 + j, 0))
    elif b_blocks:
        assert tn == shard
        b_spec = pl.BlockSpec((None, kdim, tn), lambda i, j: (j0 + j, 0, 0))
    elif trans_b:
        b_spec = pl.BlockSpec((tn, kdim), lambda i, j: (j0 + j, 0))
    else:
        b_spec = pl.BlockSpec((kdim, tn), lambda i, j: (0, j0 + j))
    if out_blocks:
        assert tn * NDEV == ncols
        out_spec = pl.BlockSpec((None, tm, tn), lambda i, j: (j, i, 0))
        out_dims = (NDEV, m, tn)
    else:
        out_spec = pl.BlockSpec((tm, tn), lambda i, j: (i, j))
        out_dims = (m, ncols)
    ex_specs = []
    for arr, kind in extras:
        if kind == "tile":
            ex_specs.append(pl.BlockSpec((tm, tn), lambda i, j: (i, j)))
        else:
            ex_specs.append(pl.BlockSpec((1, tn), lambda i, j: (0, j)))
    n_ex, n_out = len(extras), len(out_dtypes)

    def body(a_ref, b_ref, *rest):
        ex_refs, out_refs = rest[:n_ex], rest[n_ex:]
        if trans_a:
            acc = lax.dot_general(a_ref[...], b_ref[...], (((0,), (0,)), ((), ())),
                                  preferred_element_type=F32)
        elif trans_b and b_blocks:
            acc = jnp.zeros((tm, tn), F32)
            for d in range(NDEV):
                acc = acc + lax.dot_general(a_ref[:, d * shard:(d + 1) * shard], b_ref[d],
                                            (((1,), (1,)), ((), ())), preferred_element_type=F32)
        elif trans_b:
            acc = lax.dot_general(a_ref[...], b_ref[...], (((1,), (1,)), ((), ())),
                                  preferred_element_type=F32)
        else:
            acc = jnp.dot(a_ref[...], b_ref[...], preferred_element_type=F32)
        outs = (acc,) if epilogue is None else epilogue(acc, *[r[...] for r in ex_refs])
        for o_ref, val in zip(out_refs, outs):
            o_ref[...] = val.astype(o_ref.dtype)

    outs = pl.pallas_call(
        body, name=name, grid=(m // tm, ncols // tn),
        in_specs=[a_spec, b_spec] + ex_specs,
        out_specs=[out_spec for _ in range(n_out)],
        out_shape=[jax.ShapeDtypeStruct(out_dims, dt) for dt in out_dtypes],
        compiler_params=_cparams(("parallel", "parallel")),
    )(a, b, *[arr for arr, _ in extras])
    return list(outs)


TR = 256

ROW_SPEC = pl.BlockSpec((TR, D), lambda i: (i, 0))
VEC_SPEC = pl.BlockSpec((1, D), lambda i: (0, 0))


def normmod_fwd(x, g, sc, sh, name):
    def body(x_ref, g_ref, sc_ref, sh_ref, o_ref):
        xv = x_ref[...]
        rstd = lax.rsqrt(jnp.mean(xv * xv, axis=-1, keepdims=True) + EPS)
        n = (xv * rstd) * g_ref[...]
        o_ref[...] = (n * (1.0 + sc_ref[...]) + sh_ref[...]).astype(o_ref.dtype)

    return pl.pallas_call(
        body, name=name, grid=(S // TR,),
        in_specs=[ROW_SPEC, VEC_SPEC, VEC_SPEC, VEC_SPEC], out_specs=ROW_SPEC,
        out_shape=jax.ShapeDtypeStruct((S, D), BF16),
        compiler_params=_cparams(("parallel",)),
    )(x, g, sc, sh)


def normmod_bwd(x, dh, dres, g, sc, name):
    def body(x_ref, dh_ref, dres_ref, g_ref, sc_ref, dx_ref, dsc_ref, dsh_ref, dg_ref):
        @pl.when(pl.program_id(0) == 0)
        def _():
            dsc_ref[...] = jnp.zeros_like(dsc_ref)
            dsh_ref[...] = jnp.zeros_like(dsh_ref)
            dg_ref[...] = jnp.zeros_like(dg_ref)

        xv, dh = x_ref[...], dh_ref[...]
        gv = g_ref[...]
        rstd = lax.rsqrt(jnp.mean(xv * xv, axis=-1, keepdims=True) + EPS)
        xhat = xv * rstd
        dn = dh * (1.0 + sc_ref[...])
        dxhat = dn * gv
        dx_ref[...] = dres_ref[...] + rstd * (dxhat - xhat * jnp.mean(dxhat * xhat, axis=-1, keepdims=True))
        dsc_ref[...] += jnp.sum(dh * (xhat * gv), axis=0, keepdims=True)
        dsh_ref[...] += jnp.sum(dh, axis=0, keepdims=True)
        dg_ref[...] += jnp.sum(dn * xhat, axis=0, keepdims=True)

    vec_out = jax.ShapeDtypeStruct((1, D), F32)
    return pl.pallas_call(
        body, name=name, grid=(S // TR,),
        in_specs=[ROW_SPEC, ROW_SPEC, ROW_SPEC, VEC_SPEC, VEC_SPEC],
        out_specs=[ROW_SPEC, VEC_SPEC, VEC_SPEC, VEC_SPEC],
        out_shape=[jax.ShapeDtypeStruct((S, D), F32), vec_out, vec_out, vec_out],
        compiler_params=_cparams(("arbitrary",)),
    )(x, dh, dres, g, sc)


def gate_bwd(dx, branch, gate, name):
    def body(dx_ref, br_ref, gate_ref, o_ref, dgate_ref):
        @pl.when(pl.program_id(0) == 0)
        def _():
            dgate_ref[...] = jnp.zeros_like(dgate_ref)

        dxv = dx_ref[...]
        o_ref[...] = (dxv * gate_ref[...]).astype(o_ref.dtype)
        dgate_ref[...] += jnp.sum(dxv * br_ref[...], axis=0, keepdims=True)

    return pl.pallas_call(
        body, name=name, grid=(S // TR,),
        in_specs=[ROW_SPEC, ROW_SPEC, VEC_SPEC], out_specs=[ROW_SPEC, VEC_SPEC],
        out_shape=[jax.ShapeDtypeStruct((S, D), BF16), jax.ShapeDtypeStruct((1, D), F32)],
        compiler_params=_cparams(("arbitrary",)),
    )(dx, branch, gate)


def loss_head(x, target, g, name):
    def body(x_ref, t_ref, g_ref, dx_ref, loss_ref, dg_ref):
        @pl.when(pl.program_id(0) == 0)
        def _():
            loss_ref[...] = jnp.zeros_like(loss_ref)
            dg_ref[...] = jnp.zeros_like(dg_ref)

        xv, gv = x_ref[...], g_ref[...]
        rstd = lax.rsqrt(jnp.mean(xv * xv, axis=-1, keepdims=True) + EPS)
        xhat = xv * rstd
        err = xhat * gv - t_ref[...]
        loss_ref[...] += jnp.sum(err * err) * (0.5 / D)
        dy = err * (1.0 / D)
        dg_ref[...] += jnp.sum(dy * xhat, axis=0, keepdims=True)
        dxhat = dy * gv
        dx_ref[...] = rstd * (dxhat - xhat * jnp.mean(dxhat * xhat, axis=-1, keepdims=True))

    return pl.pallas_call(
        body, name=name, grid=(S // TR,),
        in_specs=[ROW_SPEC, ROW_SPEC, VEC_SPEC],
        out_specs=[ROW_SPEC, VEC_SPEC, VEC_SPEC],
        out_shape=[jax.ShapeDtypeStruct((S, D), F32), jax.ShapeDtypeStruct((1, D), F32),
                   jax.ShapeDtypeStruct((1, D), F32)],
        compiler_params=_cparams(("arbitrary",)),
    )(x, target, g)


TQ = 512
RS = 128
NSUB = TQ // RS
TK = 128


def _dot_hilo(a, tri_twice):
    hi = a.astype(BF16)
    lo = (a - hi.astype(F32)).astype(BF16)
    return jnp.dot(jnp.concatenate([hi, lo], axis=1), tri_twice, preferred_element_type=F32)


def _log_stay(z):
    return -(jnp.maximum(z, 0.0) + jnp.log(1.0 + jnp.exp(-jnp.abs(z))))


def _tri_and_ones(kind):
    row = jnp.bitwise_and(lax.broadcasted_iota(jnp.int32, (2 * TK, 2 * TK), 0), TK - 1)
    col = lax.broadcasted_iota(jnp.int32, (2 * TK, 2 * TK), 1)
    tri = {"after": row > col, "upto": row <= col, "before": row < col}[kind]
    return jnp.logical_or(col >= TK, tri).astype(BF16)


NPAIR = NH // 2
SCALE = HD ** -0.5


def _pair_specs(first_block):
    rows = pl.BlockSpec((TQ, LANES), lambda p, i: (i, first_block + p))
    whole = pl.BlockSpec((S, LANES), lambda p, i: (0, first_block + p))
    return rows, whole


Q_ROWS_SPEC, _ = _pair_specs(0)
_, K_ALL_SPEC = _pair_specs(NPAIR)
_, V_ALL_SPEC = _pair_specs(2 * NPAIR)
PAIR_ROWS_SPEC = pl.BlockSpec((TQ, LANES), lambda p, i: (i, p))
PAIR_ALL_SPEC = pl.BlockSpec((S, LANES), lambda p, i: (0, p))
PAIR_TOTAL_SPEC = pl.BlockSpec((2, TQ, TK), lambda p, i: (p, i, 0))


def _head_halves(x):
    first = lax.broadcasted_iota(jnp.int32, x.shape, 1) < HD
    zero = jnp.zeros_like(x)
    return jnp.where(first, x, zero), jnp.where(first, zero, x)


def _join_heads(a, b):
    return jnp.where(lax.broadcasted_iota(jnp.int32, a.shape, 1) < HD, a, b)


def _comm_hooks(comm, refs, n_in, n_out, n_scratch):
    nc = len(comm.arrs) if comm is not None else 0
    ins, cin = refs[:n_in], refs[n_in:n_in + nc]
    outs = refs[n_in + nc:n_in + nc + n_out]
    cout = refs[n_in + nc + n_out:n_in + 2 * nc + n_out]
    scratch = refs[n_in + 2 * nc + n_out:n_in + 2 * nc + n_out + n_scratch]
    sems = refs[n_in + 2 * nc + n_out + n_scratch:]
    phases = comm.phases(cin, cout, sems) if comm is not None else None
    return ins, outs, scratch, phases


def _with_comm(comm, in_specs, out_specs, out_shape, operands, scratch):
    if comm is None:
        return dict(in_specs=in_specs, out_specs=out_specs, out_shape=out_shape, scratch_shapes=scratch), operands
    nc = len(comm.arrs)
    return dict(in_specs=in_specs + [HBM_SPEC] * nc, out_specs=out_specs + [HBM_SPEC] * nc,
                out_shape=out_shape + comm.out_shape, scratch_shapes=scratch + comm.scratch), operands + comm.arrs


def attn_fwd(qkv, name, comm=None):
    n_steps = S // TQ

    def body(*refs):
        (q_ref, k_ref, v_ref), (o_ref, r_ref), (acc_ref, z_even, z_odd, w_ref), phases = _comm_hooks(
            comm, refs, 3, 2, 4)
        p = pl.program_id(0)
        i = pl.program_id(1)
        if phases is not None:
            pl.when(jnp.logical_and(p == 0, i == 0))(phases[0])
            pl.when(jnp.logical_and(p == NPAIR - 1, i == n_steps - 2))(phases[1])
        chains = [(sub, h) for sub in range(NSUB) for h in range(2)]
        q_sub = [_head_halves(q_ref[pl.ds(sub * RS, RS), :] * SCALE) for sub in range(NSUB)]
        s_off = lax.broadcasted_iota(jnp.int32, (RS, TK), 1)
        t_pos = [i * TQ + sub * RS + lax.broadcasted_iota(jnp.int32, (RS, TK), 0) for sub in range(NSUB)]
        after = _tri_and_ones("after")
        nblk = (i + 1) * (TQ // TK)

        acc_ref[...] = jnp.zeros_like(acc_ref)
        r_ref[...] = jnp.zeros_like(r_ref)

        def key_rows(block):
            return pl.ds(pl.multiple_of(block * TK, TK), TK)

        def store_scores(z_ref, block):
            kb = k_ref[key_rows(block), :]
            for c, (sub, h) in enumerate(chains):
                z_ref[c] = lax.dot_general(q_sub[sub][h], kb, (((1,), (1,)), ((), ())),
                                           preferred_element_type=F32)

        def add_weighted_values(block):
            vb = v_ref[key_rows(block), :]
            pv = [jnp.dot(w_ref[c], vb, preferred_element_type=F32) for c in range(len(chains))]
            for sub in range(NSUB):
                acc_ref[pl.ds(sub * RS, RS), :] += _join_heads(pv[2 * sub], pv[2 * sub + 1])

        w_ref[...] = jnp.zeros_like(w_ref)
        store_scores(z_even, nblk - 1)

        def step(block, z_ref, z_next_ref):
            add_weighted_values(jnp.minimum(block + 1, nblk - 1))
            store_scores(z_next_ref, jnp.maximum(block - 1, 0))
            mask = [(block * TK + s_off) < t for t in t_pos]
            ls, sums = [], []
            for c, (sub, h) in enumerate(chains):
                ls.append(_log_stay(z_ref[c]))
                sums.append(_dot_hilo(jnp.where(mask[sub], ls[c], 0.0), after))
            for c, (sub, h) in enumerate(chains):
                rows = pl.ds(sub * RS, RS)
                later = r_ref[h, rows, :]
                w = jnp.where(mask[sub], jnp.exp(z_ref[c] + ls[c] + (sums[c][:, :TK] + later)), 0.0)
                w_ref[c] = w.astype(BF16)
                r_ref[h, rows, :] = later + sums[c][:, TK:]

        @pl.loop(0, nblk // 2)
        def _(pair):
            block = nblk - 1 - 2 * pair
            step(block, z_even, z_odd)
            step(block - 1, z_odd, z_even)

        add_weighted_values(0)
        o_ref[...] = acc_ref[...].astype(o_ref.dtype)
        if phases is not None:
            pl.when(jnp.logical_and(p == NPAIR - 1, i == n_steps - 1))(phases[2])

    kwargs, operands = _with_comm(
        comm, [Q_ROWS_SPEC, K_ALL_SPEC, V_ALL_SPEC], [PAIR_ROWS_SPEC, PAIR_TOTAL_SPEC],
        [jax.ShapeDtypeStruct((S, NH * HD), BF16), jax.ShapeDtypeStruct((NH, S, TK), F32)], [qkv, qkv, qkv],
        [pltpu.VMEM((TQ, LANES), F32), pltpu.VMEM((2 * NSUB, RS, TK), F32), pltpu.VMEM((2 * NSUB, RS, TK), F32),
         pltpu.VMEM((2 * NSUB, RS, TK), BF16)])
    return pl.pallas_call(
        body, name=name, grid=(NPAIR, n_steps),
        compiler_params=_cparams(("arbitrary", "arbitrary")), **kwargs,
    )(*operands)


def attn_bwd(qkv, dout, totals, name, comm=None):
    n_steps = S // TQ

    def body(*refs):
        ((q_ref, k_ref, v_ref, do_ref, r_ref), (dq_ref, dk_ref, dv_ref),
         (z_even, z_odd, dw_even, dw_odd, before_ref, dbefore_ref, dz_ref, w_ref), phases) = _comm_hooks(
            comm, refs, 5, 3, 8)
        p = pl.program_id(0)
        i = pl.program_id(1)
        if phases is not None:
            pl.when(jnp.logical_and(p == 0, i == 0))(phases[0])
            pl.when(jnp.logical_and(p == NPAIR - 1, i == n_steps - 2))(phases[1])

        @pl.when(i == 0)
        def _():
            dk_ref[...] = jnp.zeros_like(dk_ref)
            dv_ref[...] = jnp.zeros_like(dv_ref)

        chains = [(sub, h) for sub in range(NSUB) for h in range(2)]
        nch = len(chains)
        qb = q_ref[...]
        dob = do_ref[...].astype(BF16)
        q_sub = [_head_halves(qb[sub * RS:(sub + 1) * RS] * SCALE) for sub in range(NSUB)]
        do_sub = [_head_halves(dob[sub * RS:(sub + 1) * RS]) for sub in range(NSUB)]
        s_off = lax.broadcasted_iota(jnp.int32, (RS, TK), 1)
        t_pos = [i * TQ + sub * RS + lax.broadcasted_iota(jnp.int32, (RS, TK), 0) for sub in range(NSUB)]
        upto = _tri_and_ones("upto")
        before_tri = _tri_and_ones("before")
        contract_lanes = (((1,), (1,)), ((), ()))
        contract_rows = (((0,), (0,)), ((), ()))

        nblk = (i + 1) * (TQ // TK)

        def key_rows(block):
            return pl.ds(pl.multiple_of(block * TK, TK), TK)

        def store_products(z_ref, dw_ref, block):
            kb = k_ref[key_rows(block), :]
            vb = v_ref[key_rows(block), :]
            for c, (sub, h) in enumerate(chains):
                z_ref[c] = lax.dot_general(q_sub[sub][h], kb, contract_lanes, preferred_element_type=F32)
                dw_ref[c] = lax.dot_general(do_sub[sub][h], vb, contract_lanes, preferred_element_type=F32)

        def add_gradients(block):
            kb = k_ref[key_rows(block), :]
            for sub in range(NSUB):
                rows = pl.ds(sub * RS, RS)
                dq_ref[rows, :] += _join_heads(*[jnp.dot(dz_ref[h, rows, :], kb, preferred_element_type=F32)
                                                 for h in range(2)])
            dk_ref[key_rows(block), :] += _join_heads(*[
                lax.dot_general(dz_ref[h], qb, contract_rows, preferred_element_type=F32) for h in range(2)])
            dv_ref[key_rows(block), :] += _join_heads(*[
                lax.dot_general(w_ref[h], dob, contract_rows, preferred_element_type=F32) for h in range(2)])

        for ref in (dq_ref, before_ref, dbefore_ref, dz_ref, w_ref):
            ref[...] = jnp.zeros_like(ref)
        store_products(z_even, dw_even, 0)

        def step(block, z_ref, dw_ref, z_next_ref, dw_next_ref):
            add_gradients(jnp.maximum(block - 1, 0))
            store_products(z_next_ref, dw_next_ref, jnp.minimum(block + 1, nblk - 1))
            mask = [(block * TK + s_off) < t for t in t_pos]
            ls, sums = [], []
            for c, (sub, h) in enumerate(chains):
                ls.append(_log_stay(z_ref[c]))
                sums.append(_dot_hilo(jnp.where(mask[sub], ls[c], 0.0), upto))
            dl, dsums = [], []
            for c, (sub, h) in enumerate(chains):
                rows = pl.ds(sub * RS, RS)
                before = before_ref[c]
                log_after = r_ref[h, rows, :] - (sums[c][:, :TK] + before)
                w = jnp.where(mask[sub], jnp.exp((z_ref[c] + ls[c]) + log_after), 0.0)
                dl.append(dw_ref[c] * w)
                dsums.append(_dot_hilo(dl[c], before_tri))
                w_ref[h, rows, :] = w.astype(BF16)
                before_ref[c] = before + sums[c][:, TK:]
            for c, (sub, h) in enumerate(chains):
                rows = pl.ds(sub * RS, RS)
                dbefore = dbefore_ref[c]
                beta = jnp.where(mask[sub], jnp.exp(z_ref[c] + ls[c]), 0.0)
                dstay = dsums[c][:, :TK] + dbefore
                dz_ref[h, rows, :] = ((dl[c] * (1.0 - beta) - beta * dstay) * SCALE).astype(BF16)
                dbefore_ref[c] = dbefore + dsums[c][:, TK:]

        @pl.loop(0, nblk // 2)
        def _(pair):
            step(2 * pair, z_even, dw_even, z_odd, dw_odd)
            step(2 * pair + 1, z_odd, dw_odd, z_even, dw_even)

        add_gradients(nblk - 1)
        if phases is not None:
            pl.when(jnp.logical_and(p == NPAIR - 1, i == n_steps - 1))(phases[2])

    full = jax.ShapeDtypeStruct((S, NH * HD), F32)
    kwargs, operands = _with_comm(
        comm, [Q_ROWS_SPEC, K_ALL_SPEC, V_ALL_SPEC, PAIR_ROWS_SPEC, PAIR_TOTAL_SPEC],
        [PAIR_ROWS_SPEC, PAIR_ALL_SPEC, PAIR_ALL_SPEC], [full, full, full], [qkv, qkv, qkv, dout, totals],
        [pltpu.VMEM((2 * NSUB, RS, TK), F32)] * 6 + [pltpu.VMEM((2, TQ, TK), BF16)] * 2)
    return pl.pallas_call(
        body, name=name, grid=(NPAIR, n_steps),
        compiler_params=_cparams(("arbitrary", "arbitrary")), **kwargs,
    )(*operands)


def _proj_cols(first_col):
    base = first_col // LANES
    return pl.BlockSpec((S, LANES), lambda j: (0, base + j))


CONV_OUT_SPEC = pl.BlockSpec((S, LANES), lambda j: (0, j))
CONV_DOUT_SPEC = pl.BlockSpec((S, LANES), lambda j: (0, (NH * HD) // LANES + j))
CONV_W_SPEC = pl.BlockSpec((8, LANES), lambda j: (0, j))
CONV_B_SPEC = pl.BlockSpec((1, LANES), lambda j: (0, j))


def _shift_down(u, n):
    rows = lax.broadcasted_iota(jnp.int32, u.shape, 0)
    return jnp.where(rows >= n, pltpu.roll(u, n, 0), 0.0)


def _shift_up(u, n):
    rows = lax.broadcasted_iota(jnp.int32, u.shape, 0)
    return jnp.where(rows < S - n, pltpu.roll(u, S - n, 0), 0.0)


def conv_fwd(proj, cw8, cb, name):
    def body(bg_ref, cg_ref, hc_ref, w_ref, b_ref, o_ref):
        u = cg_ref[...] * hc_ref[...]
        w = w_ref[...]
        y = w[0:1, :] * _shift_down(u, 2) + w[1:2, :] * _shift_down(u, 1) + w[2:3, :] * u + b_ref[...]
        o_ref[...] = bg_ref[...] * y

    return pl.pallas_call(
        body, name=name, grid=(CW // LANES,),
        in_specs=[_proj_cols(0), _proj_cols(CW), _proj_cols(2 * CW), CONV_W_SPEC, CONV_B_SPEC],
        out_specs=CONV_OUT_SPEC, out_shape=jax.ShapeDtypeStruct((S, CW), F32),
        compiler_params=_cparams(("parallel",)),
    )(proj, proj, proj, cw8, cb)


def conv_bwd(proj, dout, cw8, cb, name):
    def body(bg_ref, cg_ref, hc_ref, do_ref, w_ref, b_ref, dbg_ref, dcg_ref, dhc_ref, dw_ref, db_ref):
        cg, hc, do = cg_ref[...], hc_ref[...], do_ref[...]
        w = w_ref[...]
        u = cg * hc
        u1, u2 = _shift_down(u, 1), _shift_down(u, 2)
        y = w[0:1, :] * u2 + w[1:2, :] * u1 + w[2:3, :] * u + b_ref[...]
        dbg_ref[...] = do * y
        dy = do * bg_ref[...]
        db_ref[...] = jnp.sum(dy, axis=0, keepdims=True)
        dw_ref[...] = jnp.concatenate(
            [jnp.sum(dy * u2, axis=0, keepdims=True), jnp.sum(dy * u1, axis=0, keepdims=True),
             jnp.sum(dy * u, axis=0, keepdims=True), jnp.zeros((5, LANES), F32)], axis=0)
        du = w[2:3, :] * dy + w[1:2, :] * _shift_up(dy, 1) + w[0:1, :] * _shift_up(dy, 2)
        dcg_ref[...] = du * hc
        dhc_ref[...] = du * cg

    full = jax.ShapeDtypeStruct((S, CW), F32)
    return pl.pallas_call(
        body, name=name, grid=(CW // LANES,),
        in_specs=[_proj_cols(0), _proj_cols(CW), _proj_cols(2 * CW), CONV_DOUT_SPEC, CONV_W_SPEC, CONV_B_SPEC],
        out_specs=[CONV_OUT_SPEC, CONV_OUT_SPEC, CONV_OUT_SPEC, CONV_W_SPEC, CONV_B_SPEC],
        out_shape=[full, full, full, jax.ShapeDtypeStruct((8, CW), F32), jax.ShapeDtypeStruct((1, CW), F32)],
        compiler_params=_cparams(("parallel",)),
    )(proj, proj, proj, dout, cw8, cb)


GELU_K = math.sqrt(2.0 / math.pi)
GELU_C = 0.044715


def _gelu(x):
    return 0.5 * x * (1.0 + jnp.tanh(GELU_K * (x + GELU_C * (x * x * x))))


def _gelu_grad(x):
    t = jnp.tanh(GELU_K * (x + GELU_C * (x * x * x)))
    return 0.5 * (1.0 + t) + 0.5 * x * (1.0 - t * t) * (GELU_K * (1.0 + 3.0 * GELU_C * (x * x)))


def _sg_masks():
    row = lax.broadcasted_iota(jnp.int32, (T, T), 0)
    col = lax.broadcasted_iota(jnp.int32, (T, T), 1)
    causal = jnp.right_shift(row, 6) >= jnp.right_shift(col, 6)
    head_of_col = jnp.right_shift(lax.broadcasted_iota(jnp.int32, (T, CW), 1), 6)
    return causal, head_of_col


def _sg_mixed(vnb, sw_ref, bias, causal, head_of_col):
    mixed = bias
    for h in range(SG_HEADS):
        wh = jnp.where(causal, sw_ref[h], 0.0).astype(BF16)
        mh = jnp.dot(wh, vnb, preferred_element_type=F32)
        mixed = mixed + jnp.where(head_of_col == h, mh, 0.0)
    return mixed


SG_U_SPEC = pl.BlockSpec((T, CW), lambda n: (n, 3))
SG_V_SPEC = pl.BlockSpec((T, CW), lambda n: (n, 4))
SG_ROW_SPEC = pl.BlockSpec((T, CW), lambda n: (n, 0))
SG_DOUT_SPEC = pl.BlockSpec((T, CW), lambda n: (n, 3))
SG_G_SPEC = pl.BlockSpec((1, CW), lambda n: (0, 0))
SG_W_SPEC = pl.BlockSpec((SG_HEADS, T, T), lambda n: (0, 0, 0))
SG_BIAS_SPEC = pl.BlockSpec((T, CW), lambda n: (0, 0))


def sg_fwd(proj, gn, sw, bias, name):
    def body(u_ref, v_ref, g_ref, sw_ref, bias_ref, o_ref):
        causal, head_of_col = _sg_masks()
        gv = _gelu(v_ref[...])
        rstd = lax.rsqrt(jnp.mean(gv * gv, axis=-1, keepdims=True) + EPS)
        vnb = ((gv * rstd) * g_ref[...]).astype(BF16)
        mixed = _sg_mixed(vnb, sw_ref, bias_ref[...], causal, head_of_col)
        o_ref[...] = _gelu(u_ref[...]) * mixed

    return pl.pallas_call(
        body, name=name, grid=(S // T,),
        in_specs=[SG_U_SPEC, SG_V_SPEC, SG_G_SPEC, SG_W_SPEC, SG_BIAS_SPEC],
        out_specs=SG_ROW_SPEC, out_shape=jax.ShapeDtypeStruct((S, CW), F32),
        compiler_params=_cparams(("parallel",)),
    )(proj, proj, gn, sw, bias)


def sg_bwd(proj, dout, gn, sw, bias, name):
    def body(u_ref, v_ref, do_ref, g_ref, sw_ref, bias_ref, du_ref, dv_ref, dg_ref, dsw_ref, dbias_ref):
        @pl.when(pl.program_id(0) == 0)
        def _():
            dg_ref[...] = jnp.zeros_like(dg_ref)
            dsw_ref[...] = jnp.zeros_like(dsw_ref)
            dbias_ref[...] = jnp.zeros_like(dbias_ref)

        causal, head_of_col = _sg_masks()
        uv, vv, do, gnv = u_ref[...], v_ref[...], do_ref[...], g_ref[...]
        gv = _gelu(vv)
        rstd = lax.rsqrt(jnp.mean(gv * gv, axis=-1, keepdims=True) + EPS)
        xhat = gv * rstd
        vnb = (xhat * gnv).astype(BF16)
        mixed = _sg_mixed(vnb, sw_ref, bias_ref[...], causal, head_of_col)
        du_ref[...] = (do * mixed) * _gelu_grad(uv)
        dmix = do * _gelu(uv)
        dbias_ref[...] += dmix
        dmixb = dmix.astype(BF16)
        dvn = jnp.zeros((T, CW), F32)
        for h in range(SG_HEADS):
            wh = jnp.where(causal, sw_ref[h], 0.0).astype(BF16)
            dvh = lax.dot_general(wh, dmixb, (((0,), (0,)), ((), ())), preferred_element_type=F32)
            dvn = dvn + jnp.where(head_of_col == h, dvh, 0.0)
            dmh = jnp.where(head_of_col == h, dmixb, jnp.zeros_like(dmixb))
            dwh = lax.dot_general(dmh, vnb, (((1,), (1,)), ((), ())), preferred_element_type=F32)
            dsw_ref[h] += jnp.where(causal, dwh, 0.0)
        dg_ref[...] += jnp.sum(dvn * xhat, axis=0, keepdims=True)
        dxhat = dvn * gnv
        dgv = rstd * (dxhat - xhat * jnp.mean(dxhat * xhat, axis=-1, keepdims=True))
        dv_ref[...] = dgv * _gelu_grad(vv)

    full = jax.ShapeDtypeStruct((S, CW), F32)
    return pl.pallas_call(
        body, name=name, grid=(S // T,),
        in_specs=[SG_U_SPEC, SG_V_SPEC, SG_DOUT_SPEC, SG_G_SPEC, SG_W_SPEC, SG_BIAS_SPEC],
        out_specs=[SG_ROW_SPEC, SG_ROW_SPEC, SG_G_SPEC, SG_W_SPEC, SG_BIAS_SPEC],
        out_shape=[full, full, jax.ShapeDtypeStruct((1, CW), F32),
                   jax.ShapeDtypeStruct((SG_HEADS, T, T), F32), jax.ShapeDtypeStruct((T, CW), F32)],
        compiler_params=_cparams(("arbitrary",)),
    )(proj, proj, dout, gn, sw, bias)


ADA_COLS = NMOD * D // NDEV


def ada_fwd(c_all, ada_w, ada_b_mine, name):
    def body(c_ref, w_ref, b_ref, o_ref, ca_ref):
        cv = c_ref[...]
        ca = cv * (1.0 / (1.0 + jnp.exp(-cv)))
        ca_ref[...] = ca
        cab = ca.astype(BF16)
        for l in range(L):
            o_ref[l] = jnp.dot(cab, w_ref[l].astype(BF16), preferred_element_type=F32) + b_ref[l]

    return pl.pallas_call(
        body, name=name,
        out_shape=[jax.ShapeDtypeStruct((L, NDEV, ADA_COLS), F32), jax.ShapeDtypeStruct((NDEV, D), F32)],
        compiler_params=_cparams(),
    )(c_all, ada_w, ada_b_mine)


def ada_bwd(ca, dmod_cols, name):
    def body(ca_ref, dm_ref, o_ref):
        cab = ca_ref[...].astype(BF16)
        for l in range(L):
            o_ref[l] = lax.dot_general(cab, dm_ref[l].astype(BF16), (((0,), (0,)), ((), ())),
                                       preferred_element_type=F32)

    return pl.pallas_call(
        body, name=name, out_shape=jax.ShapeDtypeStruct((L, D, ADA_COLS), F32),
        compiler_params=_cparams(),
    )(ca, dmod_cols)


def _adamw(w, g, m, v):
    m = B1 * m + (1.0 - B1) * g
    v = B2 * v + (1.0 - B2) * (g * g)
    m_hat = m / BC1
    v_hat = v / BC2
    delta = -LR * (m_hat / (jnp.sqrt(v_hat) + AEPS) + WD * w)
    return delta, m, v


VEC_ROWS_PER_LAYER = 8
VEC_FINAL_ROW = L * VEC_ROWS_PER_LAYER
VEC_ROWS = VEC_FINAL_ROW + 8
W256_TAPS, W256_CONV_B, W256_GN, W256_BIAS = 0, 8, 9, 16
W256_ROWS_PER_LAYER = W256_BIAS + T


def small_update(vec_all, w256_all, sw_all, params, name):
    n_par = len(params)

    def body(*refs):
        vec_ref, w256_ref = refs[:2]
        sw_refs = refs[2:2 + L]
        par_refs = [refs[2 + L + 3 * k:2 + L + 3 * k + 3] for k in range(n_par)]
        out = refs[2 + L + 3 * n_par:]
        out_par = [out[4 * k:4 * k + 4] for k in range(n_par)]
        loss_ref, taps_ref, bias_ref = out[4 * n_par:]

        def total(ref, idx):
            acc = ref[(0,) + idx]
            for d in range(1, NDEV):
                acc = acc + ref[(d,) + idx]
            return acc

        def update(k, region, g):
            w_ref, m_ref, v_ref = par_refs[k]
            g_ref, d_ref, nm_ref, nv_ref = out_par[k]
            delta, nm, nv = _adamw(w_ref[region], g, m_ref[region], v_ref[region])
            g_ref[region] = g
            d_ref[region] = delta
            nm_ref[region] = nm
            nv_ref[region] = nv

        for l in range(L):
            base = l * VEC_ROWS_PER_LAYER
            for k in range(NMOD):
                update(0, (slice(l, l + 1), slice(k * D, (k + 1) * D)), total(vec_ref, (slice(base + k, base + k + 1),)))
            update(1, (slice(l, l + 1),), total(vec_ref, (slice(base + 6, base + 7),)))
            update(2, (slice(l, l + 1),), total(vec_ref, (slice(base + 7, base + 8),)))
            wbase = l * W256_ROWS_PER_LAYER
            update(4, (slice(l, l + 1),), total(w256_ref, (slice(wbase + W256_CONV_B, wbase + W256_CONV_B + 1),)))
            update(5, (slice(l, l + 1),), total(w256_ref, (slice(wbase + W256_GN, wbase + W256_GN + 1),)))
            update(6, (l,), total(sw_refs[l], ()))
            taps_ref[l] = total(w256_ref, (slice(wbase + W256_TAPS, wbase + W256_TAPS + 8),))
            bias_ref[l] = total(w256_ref, (slice(wbase + W256_BIAS, wbase + W256_BIAS + T),))
        update(3, (slice(0, 1),), total(vec_ref, (slice(VEC_FINAL_ROW, VEC_FINAL_ROW + 1),)))
        loss_ref[...] = total(vec_ref, (slice(VEC_FINAL_ROW + 1, VEC_FINAL_ROW + 2), slice(0, LANES)))

    out_shape = []
    for w, _, _ in params:
        out_shape += [jax.ShapeDtypeStruct(w.shape, F32)] * 4
    out_shape += [jax.ShapeDtypeStruct((1, LANES), F32), jax.ShapeDtypeStruct((L, 8, CW), F32),
                  jax.ShapeDtypeStruct((L, T, CW), F32)]
    outs = pl.pallas_call(body, name=name, out_shape=out_shape, compiler_params=_cparams())(
        vec_all, w256_all, *sw_all, *[a for p in params for a in p])
    return [outs[4 * k:4 * k + 4] for k in range(n_par)], outs[4 * n_par:]


def adamw_plain(w, g, m, v, tr, name):
    rows, cols = w.shape
    spec = pl.BlockSpec((tr, cols), lambda i: (i, 0))

    def body(w_ref, g_ref, m_ref, v_ref, d_ref, nm_ref, nv_ref):
        delta, nm, nv = _adamw(w_ref[...], g_ref[...], m_ref[...], v_ref[...])
        d_ref[...] = delta
        nm_ref[...] = nm
        nv_ref[...] = nv

    shp = jax.ShapeDtypeStruct((rows, cols), F32)
    return pl.pallas_call(
        body, name=name, grid=(rows // tr,), in_specs=[spec] * 4, out_specs=[spec] * 3,
        out_shape=[shp, shp, shp], compiler_params=_cparams(("parallel",)),
    )(w, g, m, v)


def adamw_reduce(w, parts, m, v, tr, name):
    _, rows, cols = w.shape
    spec = pl.BlockSpec((None, tr, cols), lambda l, i: (l, i, 0))
    pspecs = [pl.BlockSpec((NDEV, tr, cols), lambda l, i, k=k: (0, jnp.where(l == k, i, 0), 0)) for k in range(L)]

    def body(w_ref, p0_ref, p1_ref, m_ref, v_ref, g_ref, d_ref, nm_ref, nv_ref):
        first_layer = pl.program_id(0) == 0
        g = jnp.zeros((tr, cols), F32)
        for d in range(NDEV):
            g = g + jnp.where(first_layer, p0_ref[d], p1_ref[d]).astype(F32)
        delta, nm, nv = _adamw(w_ref[...], g, m_ref[...], v_ref[...])
        g_ref[...] = g
        d_ref[...] = delta
        nm_ref[...] = nm
        nv_ref[...] = nv

    shp = jax.ShapeDtypeStruct(w.shape, F32)
    return pl.pallas_call(
        body, name=name, grid=(L, rows // tr), in_specs=[spec] + pspecs + [spec, spec], out_specs=[spec] * 4,
        out_shape=[shp] * 4, compiler_params=_cparams(("parallel", "parallel")),
    )(w, *parts, m, v)


def _pad_rows(flat, rows):
    return jnp.pad(flat, (0, rows * LANES - flat.shape[0])).reshape(rows, LANES)


def kernel(x, c, ada_w, ada_b, norm_mix_g, norm_mlp_g, w_in, conv_w, conv_b, gmlp_norm_g, spatial_w, spatial_b, w_out, mlp_w1, mlp_w2, final_norm_g, loss_target, m_ada_w, m_ada_b, m_norm_mix_g, m_norm_mlp_g, m_w_in, m_conv_w, m_conv_b, m_gmlp_norm_g, m_spatial_w, m_spatial_b, m_w_out, m_mlp_w1, m_mlp_w2, m_final_norm_g, v_ada_w, v_ada_b, v_norm_mix_g, v_norm_mlp_g, v_w_in, v_conv_w, v_conv_b, v_gmlp_norm_g, v_spatial_w, v_spatial_b, v_w_out, v_mlp_w1, v_mlp_w2, v_final_norm_g):
    me = _lin(_my_pos())
    x0 = x[0]
    target = loss_target[0]
    conv_shard = conv_w.shape[-1]

    w_in_b, w_out_b, w1_b, w2_b = [w.astype(BF16) for w in (w_in, w_out, mlp_w1, mlp_w2)]
    pack0 = _pad_rows(jnp.concatenate([c.reshape(-1), conv_w.reshape(-1)]), 16)
    g0, gw_in0 = run_comm(Gather([pack0, w_in_b[0]]), "gather_first")
    g0 = g0.reshape(NDEV, 16 * LANES)
    c_all = g0[:, :D]
    conv_full = (g0[:, D:D + L * 3 * conv_shard].reshape(NDEV, L, 3, conv_shard)
                 .transpose(1, 2, 0, 3).reshape(L, 3, CW))

    def canonical_w_in(gathered):
        return gathered.transpose(1, 0, 2).reshape(D, PROJ)

    weight_plans = [Gather([w_out_b[0], w1_b[0], w2_b[0], w_in_b[1]]), Gather([w_out_b[1], w1_b[1], w2_b[1]])]
    W_in = [canonical_w_in(gw_in0), None]
    W_out, W1, W2 = [None] * L, [None] * L, [None] * L

    ada_b_mine = lax.dynamic_slice(ada_b, (0, me * ADA_COLS), (L, ADA_COLS)).reshape(L, 1, ADA_COLS)
    mod_part, c_act = ada_fwd(c_all, ada_w, ada_b_mine, "ada_fwd")
    gmod = run_comm(Gather([mod_part]), "gather_mod")[0]
    mod = lax.dynamic_index_in_dim(gmod, me, axis=2, keepdims=False)
    mod = mod.transpose(1, 0, 2).reshape(L, NMOD, 1, D)

    cw8 = jnp.pad(conv_full, ((0, 0), (0, 5), (0, 0)))
    sg_bias = jnp.repeat(spatial_b.transpose(0, 2, 1), HD, axis=2)

    saved = []
    xl = x0
    for l in range(L):
        sh_m, sc_m, g_m, sh_f, sc_f, g_f = [mod[l, k] for k in range(NMOD)]
        h1 = normmod_fwd(xl, norm_mix_g[l:l + 1], sc_m, sh_m, f"norm_mix_fwd{l}")
        qkv = mm_layer("proj_qkv", l, h1, W_in[l], out_dtypes=[BF16], cols=(0, QKV))[0]
        proj = mm_layer("proj_rest", l, h1, W_in[l], out_dtypes=[F32], cols=(QKV, REST))[0]
        a_out, a_tot, *gathered = attn_fwd(qkv, f"attn_fwd{l}", comm=weight_plans[l])
        W_out[l] = gathered[0].reshape(D, D)
        W1[l] = gathered[1]
        W2[l] = gathered[2].reshape(DFF, D)
        if l + 1 < L:
            W_in[l + 1] = canonical_w_in(gathered[3])
        c_out = conv_fwd(proj, cw8[l], conv_b[l:l + 1], f"conv_fwd{l}")
        s_out = sg_fwd(proj, gmlp_norm_g[l:l + 1], spatial_w[l], sg_bias[l], f"sg_fwd{l}")
        cat = jnp.concatenate([a_out, c_out.astype(BF16), s_out.astype(BF16)], axis=1)
        mix, x1 = mm_layer("mix", l, cat, W_out[l], out_dtypes=[F32, F32],
                           epilogue=lambda acc, xr, g: (acc, xr + g * acc), extras=[(xl, "tile"), (g_m, "col")])
        h2 = normmod_fwd(x1, norm_mlp_g[l:l + 1], sc_f, sh_f, f"norm_mlp_fwd{l}")
        ra, r = mm_layer("mlp_up", l, h2, W1[l], out_dtypes=[BF16, BF16], b_blocks=True,
                         epilogue=lambda acc: (jnp.maximum(acc, 0.0), jnp.square(jnp.maximum(acc, 0.0))))
        m2, x2 = mm_layer("mlp_down", l, r, W2[l], out_dtypes=[F32, F32],
                          epilogue=lambda acc, xr, g: (acc, xr + g * acc), extras=[(x1, "tile"), (g_f, "col")])
        saved.append(dict(x=xl, h1=h1, proj=proj, qkv=qkv, a_tot=a_tot, cat=cat, mix=mix,
                          x1=x1, h2=h2, ra=ra, r=r, m2=m2))
        xl = x2

    dx, loss_part, d_final_g = loss_head(xl, target, final_norm_g.reshape(1, D), "loss_head")

    p_in, p_out, p_w1, p_w2 = [None] * L, [None] * L, [None] * L, [None] * L
    pending_w_in = None
    vec_rows, d_norm_mix, d_norm_mlp = [None] * L, [None] * L, [None] * L
    dcw8, d_conv_b, d_gn, d_sw, d_bias = [None] * L, [None] * L, [None] * L, [None] * L, [None] * L
    for l in reversed(range(L)):
        sv = saved[l]
        sh_m, sc_m, g_m, sh_f, sc_f, g_f = [mod[l, k] for k in range(NMOD)]
        dm2, dg_f = gate_bwd(dx, sv["m2"], g_f, f"gate_mlp_bwd{l}")
        da = mm_layer("mlp_down_dgrad", l, dm2, W2[l], out_dtypes=[BF16], trans_b=True,
                      epilogue=lambda acc, rav: (acc * (2.0 * rav.astype(F32)),), extras=[(sv["ra"], "tile")])[0]
        dW2 = mm_layer("mlp_down_wgrad", l, sv["r"], dm2, out_dtypes=[BF16], trans_a=True)[0]
        dW1 = mm_layer("mlp_up_wgrad", l, sv["h2"], da, out_dtypes=[BF16], trans_a=True, out_blocks=True)[0]
        dh2 = mm_layer("mlp_up_dgrad", l, da, W1[l], out_dtypes=[F32], trans_b=True, b_blocks=True)[0]
        dx1, dsc_f, dsh_f, d_norm_mlp[l] = normmod_bwd(sv["x1"], dh2, dx, norm_mlp_g[l:l + 1], sc_f,
                                                       f"norm_mlp_bwd{l}")
        dmix, dg_m = gate_bwd(dx1, sv["mix"], g_m, f"gate_mix_bwd{l}")
        dcat = mm_layer("mix_dgrad", l, dmix, W_out[l], out_dtypes=[F32], trans_b=True)[0]
        dW_out = mm_layer("mix_wgrad", l, sv["cat"], dmix, out_dtypes=[BF16], trans_a=True)[0]
        ready = [dW2.reshape(NDEV, DFF // NDEV, D), dW1, dW_out.reshape(NDEV, D // NDEV, D)]
        if pending_w_in is not None:
            ready = [pending_w_in] + ready
        dq, dk, dv, *arrived = attn_bwd(sv["qkv"], dcat, sv["a_tot"], f"attn_bwd{l}", comm=Exchange(ready))
        if pending_w_in is not None:
            p_in[l + 1] = arrived.pop(0)
        p_w2[l], p_w1[l], p_out[l] = arrived
        dbg, dcg, dhc, dcw8[l], d_conv_b[l] = conv_bwd(sv["proj"], dcat, cw8[l], conv_b[l:l + 1], f"conv_bwd{l}")
        dus, dvs, d_gn[l], d_sw[l], d_bias[l] = sg_bwd(sv["proj"], dcat, gmlp_norm_g[l:l + 1],
                                                       spatial_w[l], sg_bias[l], f"sg_bwd{l}")
        dproj = jnp.concatenate([dq, dk, dv, dbg, dcg, dhc, dus, dvs], axis=1).astype(BF16)
        dW_in = mm_layer("proj_wgrad", l, sv["h1"], dproj, out_dtypes=[BF16], trans_a=True)[0]
        pending_w_in = dW_in.reshape(D, NDEV, PROJ // NDEV).transpose(1, 0, 2)
        norm_g = norm_mix_g[l:l + 1]
        if l == 0:
            own = lax.dynamic_index_in_dim(pending_w_in, me, axis=0, keepdims=True)
            landing = lax.dynamic_update_slice(lax.empty(pending_w_in.shape, BF16), own, (me, 0, 0))
            *last_exchange, token = exchange_start(pending_w_in, landing, "exchange_last_start")
            norm_g = norm_g + token[0:1, 0:1]
        dh1 = mm_layer("proj_dgrad", l, dproj, W_in[l], out_dtypes=[F32], trans_b=True)[0]
        dx, dsc_m, dsh_m, d_norm_mix[l] = normmod_bwd(sv["x"], dh1, dx1, norm_g, sc_m, f"norm_mix_bwd{l}")
        vec_rows[l] = [dsh_m, dsc_m, dg_m, dsh_f, dsc_f, dg_f, d_norm_mix[l], d_norm_mlp[l]]

    grad_x = dx.reshape(1, S, D)

    vec_pack = jnp.concatenate([row for l in range(L) for row in vec_rows[l]]
                               + [d_final_g, loss_part, jnp.zeros((VEC_ROWS - VEC_FINAL_ROW - 2, D), F32)], axis=0)
    w256_pack = jnp.concatenate([blk for l in range(L) for blk in (
        dcw8[l], d_conv_b[l], d_gn[l], jnp.zeros((W256_BIAS - W256_GN - 1, CW), F32), d_bias[l])], axis=0)
    vec_all, w256_all, *sw_all = run_comm(Gather([vec_pack, w256_pack] + d_sw), "gather_small_grads")

    dmod_all = (vec_all[:, :VEC_FINAL_ROW].reshape(NDEV, L, VEC_ROWS_PER_LAYER, D)[:, :, :NMOD]
                .reshape(NDEV, L, NMOD * D))
    dmod_cols = lax.dynamic_slice(dmod_all, (0, 0, me * ADA_COLS), (NDEV, L, ADA_COLS)).transpose(1, 0, 2)
    g_ada_w = ada_bwd(c_act, dmod_cols, "ada_bwd")

    g_w_out, d_w_out, nm_w_out, nv_w_out = adamw_reduce(w_out, p_out, m_w_out, v_w_out, 128, "adamw_w_out")
    g_w1, d_w1, nm_w1, nv_w1 = adamw_reduce(mlp_w1, p_w1, m_mlp_w1, v_mlp_w1, 256, "adamw_mlp_w1")
    g_w2, d_w2, nm_w2, nv_w2 = adamw_reduce(mlp_w2, p_w2, m_mlp_w2, v_mlp_w2, 256, "adamw_mlp_w2")

    flat2 = lambda t: t.reshape(L * D, ADA_COLS)
    d_ada_w, nm_ada_w, nv_ada_w = [t.reshape(L, D, ADA_COLS) for t in adamw_plain(
        flat2(ada_w), flat2(g_ada_w), flat2(m_ada_w), flat2(v_ada_w), 256, "adamw_ada_w")]

    after = jnp.concatenate([t.reshape(-1)[:1] for t in (d_w_out, d_w1, d_w2, d_ada_w)])
    p_in[0] = exchange_wait(*last_exchange, after, "exchange_last_wait")
    g_w_in, d_w_in, nm_w_in, nv_w_in = adamw_reduce(w_in, p_in, m_w_in, v_w_in, 256, "adamw_w_in")

    as_row = lambda t: t.reshape(1, D)
    small_params = [(ada_b, m_ada_b, v_ada_b), (norm_mix_g, m_norm_mix_g, v_norm_mix_g),
                    (norm_mlp_g, m_norm_mlp_g, v_norm_mlp_g),
                    (as_row(final_norm_g), as_row(m_final_norm_g), as_row(v_final_norm_g)),
                    (conv_b, m_conv_b, v_conv_b), (gmlp_norm_g, m_gmlp_norm_g, v_gmlp_norm_g),
                    (spatial_w, m_spatial_w, v_spatial_w)]
    updated, (loss_sum, taps_sum, bias_sum) = small_update(vec_all, w256_all, sw_all, small_params, "small_update")
    loss = loss_sum[0, 0]
    u_ada_b, u_norm_mix, u_norm_mlp, u_final, u_conv_b, u_gn, u_sw = updated
    u_final = [t.reshape(D) for t in u_final]
    g_conv_w = lax.dynamic_slice(taps_sum, (0, 0, me * conv_shard), (L, 3, conv_shard))
    g_sb = bias_sum.reshape(L, T, SG_HEADS, HD).sum(axis=3).transpose(0, 2, 1)
    flat_cw = lambda t: t.reshape(L * 3, conv_shard)
    u_conv_w = [g_conv_w] + [t.reshape(L, 3, conv_shard) for t in adamw_plain(
        flat_cw(conv_w), flat_cw(g_conv_w), flat_cw(m_conv_w), flat_cw(v_conv_w), L * 3, "adamw_conv_w")]
    flat_sb = lambda t: t.reshape(L * SG_HEADS, T)
    u_sb = [g_sb] + [t.reshape(L, SG_HEADS, T) for t in adamw_plain(
        flat_sb(spatial_b), flat_sb(g_sb), flat_sb(m_spatial_b), flat_sb(v_spatial_b), L * SG_HEADS, "adamw_spatial_b")]
    small_sets = [u_ada_b, u_norm_mix, u_norm_mlp, u_conv_w, u_conv_b, u_gn, u_sw, u_sb, u_final]
    small_g, sd, snm, snv = [[u[k] for u in small_sets] for k in range(4)]

    def ordered(big, small):
        ada, win, wout, w1, w2 = big
        return [ada, small[0], small[1], small[2], win, small[3], small[4], small[5], small[6], small[7],
                wout, w1, w2, small[8]]

    grads = ordered([g_ada_w, g_w_in, g_w_out, g_w1, g_w2], small_g)
    deltas = ordered([d_ada_w, d_w_in, d_w_out, d_w1, d_w2], sd)
    new_m = ordered([nm_ada_w, nm_w_in, nm_w_out, nm_w1, nm_w2], snm)
    new_v = ordered([nv_ada_w, nv_w_in, nv_w_out, nv_w1, nv_w2], snv)
    return (loss, grad_x, *grads, *deltas, *new_m, *new_v)
```

```python
import functools
import math

import jax
import jax.numpy as jnp
from jax import lax
from jax.experimental import pallas as pl
from jax.experimental.pallas import tpu as pltpu

F32 = jnp.float32
BF16 = jnp.bfloat16
MESH = pl.DeviceIdType.MESH

S = 2048
D = 1024
L = 2
NDEV = 8
HD = 64
NH = 8
PROJ = 2816
DFF = 4096
NMOD = 6
EPS = 1e-6
T = 128
SG_HEADS = 4
LANES = 128
CW = 256
QKV = 3 * NH * HD
REST = PROJ - QKV

LR, B1, B2, AEPS, WD, STEP = 0.001, 0.9, 0.999, 1e-08, 0.01, 10
BC1 = 1.0 - B1 ** STEP
BC2 = 1.0 - B2 ** STEP

VMEM_LIMIT = 48 * 1024 * 1024

HBM_SPEC = pl.BlockSpec(memory_space=pltpu.HBM)


def _cparams(sem=None):
    return pltpu.CompilerParams(dimension_semantics=sem, vmem_limit_bytes=VMEM_LIMIT)


def _my_pos():
    return lax.axis_index("x"), lax.axis_index("y"), lax.axis_index("c")


def _lin(p):
    return 4 * p[0] + 2 * p[1] + p[2]


class Gather:
    def __init__(self, arrs):
        self.arrs = list(arrs)
        n = len(self.arrs)
        self.out_shape = [jax.ShapeDtypeStruct((NDEV,) + a.shape, a.dtype) for a in self.arrs]
        self.scratch = [pltpu.SemaphoreType.DMA((n, 7)), pltpu.SemaphoreType.DMA((n, 7)),
                        pltpu.SemaphoreType.DMA((n,))]

    def phases(self, ins, outs, sems):
        n = len(self.arrs)
        send_sems, recv_sems, local_sems = sems
        x, y, c = _my_pos()
        me, sibling = (x, y, c), (x, y, 1 - c)
        chips = [(1 - x, y), (x, 1 - y), (1 - x, 1 - y)]

        def copy(a, k, block, to, src=None):
            slot = outs[a].at[_lin(block)]
            return pltpu.make_async_remote_copy(
                src_ref=slot if src is None else src, dst_ref=slot,
                send_sem=send_sems.at[a, k], recv_sem=recv_sems.at[a, k],
                device_id=to, device_id_type=MESH)

        def mine(a):
            return pltpu.make_async_copy(ins[a], outs[a].at[_lin(me)], local_sems.at[a])

        def first(a):
            return [copy(a, 0, me, sibling, src=ins[a])] + [
                copy(a, 1 + j, me, (*chip, c), src=ins[a]) for j, chip in enumerate(chips)]

        def passed(a):
            return [copy(a, 4 + j, (*chip, c), sibling) for j, chip in enumerate(chips)]

        def start():
            for a in range(n):
                mine(a).start()
                for cp in first(a):
                    cp.start()

        def relay():
            for j, chip in enumerate(chips):
                for a in range(n):
                    copy(a, 1 + j, (*chip, c), me).wait_recv()
                    passed(a)[j].start()

        def finish():
            for a in range(n):
                copy(a, 0, sibling, me).wait_recv()
            for j, chip in enumerate(chips):
                for a in range(n):
                    copy(a, 4 + j, (*chip, 1 - c), me).wait_recv()
            for a in range(n):
                for cp in first(a) + passed(a):
                    cp.wait_send()
                mine(a).wait()

        return start, relay, finish


class Exchange:
    def __init__(self, arrs):
        self.arrs = list(arrs)
        n = len(self.arrs)
        self.out_shape = [jax.ShapeDtypeStruct(a.shape, a.dtype) for a in self.arrs]
        self.scratch = [pltpu.SemaphoreType.DMA((n, 7)), pltpu.SemaphoreType.DMA((n, 7)),
                        pltpu.SemaphoreType.DMA((n,))]

    def phases(self, ins, outs, sems):
        n = len(self.arrs)
        send_sems, recv_sems, local_sems = sems
        x, y, c = _my_pos()
        me = (x, y, c)

        def peer(mask):
            return (1 - x if mask & 4 else x, 1 - y if mask & 2 else y, 1 - c if mask & 1 else c)

        def copy(a, mask):
            return pltpu.make_async_remote_copy(
                src_ref=ins[a].at[_lin(peer(mask))], dst_ref=outs[a].at[_lin(me)],
                send_sem=send_sems.at[a, mask - 1], recv_sem=recv_sems.at[a, mask - 1],
                device_id=peer(mask), device_id_type=MESH)

        def arrival(a, mask):
            return pltpu.make_async_remote_copy(
                src_ref=ins[a].at[_lin(me)], dst_ref=outs[a].at[_lin(peer(mask))],
                send_sem=send_sems.at[a, mask - 1], recv_sem=recv_sems.at[a, mask - 1],
                device_id=peer(mask), device_id_type=MESH)

        def mine(a):
            return pltpu.make_async_copy(ins[a].at[_lin(me)], outs[a].at[_lin(me)], local_sems.at[a])

        def start():
            for a in range(n):
                mine(a).start()
            for mask in (4, 2, 6, 1, 5, 3, 7):
                for a in range(n):
                    copy(a, mask).start()

        def relay():
            pass

        def finish():
            for mask in range(1, 8):
                for a in range(n):
                    arrival(a, mask).wait_recv()
            for mask in range(1, 8):
                for a in range(n):
                    copy(a, mask).wait_send()
            for a in range(n):
                mine(a).wait()

        return start, relay, finish


def run_comm(plan, name):
    n = len(plan.arrs)

    def body(*refs):
        start, relay, finish = plan.phases(refs[:n], refs[n:2 * n], refs[2 * n:])
        start()
        relay()
        finish()

    outs = pl.pallas_call(
        body, name=name, out_shape=plan.out_shape,
        in_specs=[HBM_SPEC] * n, out_specs=[HBM_SPEC] * n, scratch_shapes=plan.scratch,
    )(*plan.arrs)
    return list(outs)


SEM_SPEC = pl.BlockSpec(memory_space=pltpu.SEMAPHORE)
DATAFLOW = pltpu.SideEffectType.DATAFLOW_SIDE_EFFECTING


def _peer_copies(src_ref, land_ref, send_sems, recv_sems, same_block):
    x, y, c = _my_pos()
    me = (x, y, c)
    sends, arrivals = [], []
    for mask in (4, 2, 6, 1, 5, 3, 7):
        peer = (1 - x if mask & 4 else x, 1 - y if mask & 2 else y, 1 - c if mask & 1 else c)
        sends.append(pltpu.make_async_remote_copy(
            src_ref=src_ref if same_block else src_ref.at[_lin(peer)], dst_ref=land_ref.at[_lin(me)],
            send_sem=send_sems.at[mask - 1], recv_sem=recv_sems.at[mask - 1], device_id=peer, device_id_type=MESH))
        arrivals.append(pltpu.make_async_remote_copy(
            src_ref=src_ref if same_block else src_ref.at[_lin(me)], dst_ref=land_ref.at[_lin(peer)],
            send_sem=send_sems.at[mask - 1], recv_sem=recv_sems.at[mask - 1], device_id=peer, device_id_type=MESH))
    return sends, arrivals


def start_copies(src, me, name, same_block, after=None):
    own = src[None] if same_block else lax.dynamic_index_in_dim(src, me, axis=0, keepdims=True)
    landing = lax.dynamic_update_slice(lax.empty((NDEV,) + own.shape[1:], src.dtype), own, (me,) + (0,) * (own.ndim - 1))

    def body(src_ref, land_ref, *rest):
        send_sems, recv_sems, _, _, token = rest[-5:]
        sends, _ = _peer_copies(src_ref, land_ref, send_sems, recv_sems, same_block)
        for cp in sends:
            cp.start()
        token[...] = jnp.zeros_like(token)

    hbm = lambda a: pltpu.HBM(a.shape, a.dtype)
    extra = [] if after is None else [after]
    *handle, token = pl.pallas_call(
        body, name=name,
        out_shape=(pltpu.SemaphoreType.DMA((7,)), pltpu.SemaphoreType.DMA((7,)), hbm(src), hbm(landing),
                   jax.ShapeDtypeStruct((8, LANES), F32)),
        in_specs=[HBM_SPEC, HBM_SPEC] + [pl.BlockSpec(memory_space=pl.ANY)] * len(extra),
        out_specs=(SEM_SPEC, SEM_SPEC, HBM_SPEC, HBM_SPEC, pl.BlockSpec(memory_space=pltpu.VMEM)),
        input_output_aliases={0: 2, 1: 3},
        compiler_params=pltpu.CompilerParams(has_side_effects=DATAFLOW),
    )(pltpu.with_memory_space_constraint(src, pltpu.HBM), pltpu.with_memory_space_constraint(landing, pltpu.HBM),
      *extra)
    return (handle, same_block), token


def finish_copies(handle, after, name):
    (send_sems, recv_sems, src, landing), same_block = handle

    def body(src_ref, land_ref, send_sems, recv_sems, after_ref, src_dead, got_ref):
        sends, arrivals = _peer_copies(src_ref, land_ref, send_sems, recv_sems, same_block)
        for cp in sends:
            cp.wait_send()
        for cp in arrivals:
            cp.wait_recv()

    hbm = lambda a: pltpu.HBM(a.shape, a.dtype)
    return pl.pallas_call(
        body, name=name, out_shape=(hbm(src), hbm(landing)),
        in_specs=(HBM_SPEC, HBM_SPEC, SEM_SPEC, SEM_SPEC, pl.BlockSpec(memory_space=pl.ANY)),
        out_specs=(HBM_SPEC, HBM_SPEC), input_output_aliases={0: 0, 1: 1},
        compiler_params=pltpu.CompilerParams(has_side_effects=DATAFLOW),
    )(src, landing, send_sems, recv_sems, after)[1]


def tied(x, token):
    return x + token[0:1, 0:1].astype(x.dtype)


MM_TILES = {
    "proj_qkv": (S, 512), "proj_rest": (S, 256), "mix": (1024, 512), "mlp_up": (S, 512), "mlp_down": (1024, 256),
    "mlp_down_dgrad": (1024, 1024), "mlp_down_wgrad": (1024, 1024), "mlp_up_wgrad": (1024, 512),
    "mlp_up_dgrad": (1024, 512), "mix_dgrad": (1024, 512), "mix_wgrad": (512, 1024),
    "proj_wgrad": (1024, PROJ // 2), "proj_dgrad": (1024, 512),
}


def mm_layer(kind, l, a, b, **kw):
    tm, tn = MM_TILES[kind]
    return mm(a, b, tm=tm, tn=tn, name=f"{kind}{l}", **kw)


def mm(a, b, *, tm, tn, out_dtypes, epilogue=None, extras=(), name, trans_a=False, trans_b=False,
       cols=None, b_blocks=False, out_blocks=False):
    if trans_a:
        kdim, m = a.shape
    else:
        m, kdim = a.shape
    shard = b.shape[-1] if b_blocks else None
    if b_blocks:
        full = (b.shape[1], NDEV * shard)
    else:
        full = b.shape
    first, ncols = cols if cols is not None else (0, full[0] if trans_b else full[1])
    assert full[1 if trans_b else 0] == kdim and m % tm == 0 and ncols % tn == 0 and first % tn == 0
    j0 = first // tn
    if trans_a:
        a_spec = pl.BlockSpec((kdim, tm), lambda i, j: (0, i))
    else:
        a_spec = pl.BlockSpec((tm, kdim), lambda i, j: (i, 0))
    if b_blocks and trans_b:
        b_spec = pl.BlockSpec((NDEV, tn, shard), lambda i, j: (0, j0 + j, 0))
    elif b_blocks:
        assert tn == shard
        b_spec = pl.BlockSpec((None, kdim, tn), lambda i, j: (j0 + j, 0, 0))
    elif trans_b:
        b_spec = pl.BlockSpec((tn, kdim), lambda i, j: (j0 + j, 0))
    else:
        b_spec = pl.BlockSpec((kdim, tn), lambda i, j: (0, j0 + j))
    if out_blocks:
        assert tn * NDEV == ncols
        out_spec = pl.BlockSpec((None, tm, tn), lambda i, j: (j, i, 0))
        out_dims = (NDEV, m, tn)
    else:
        out_spec = pl.BlockSpec((tm, tn), lambda i, j: (i, j))
        out_dims = (m, ncols)
    ex_specs = []
    for arr, kind in extras:
        if kind == "tile":
            ex_specs.append(pl.BlockSpec((tm, tn), lambda i, j: (i, j)))
        elif kind == "col":
            ex_specs.append(pl.BlockSpec((1, tn), lambda i, j: (0, j)))
        else:
            ex_specs.append(pl.BlockSpec(arr.shape, lambda i, j: (0, 0)))
    n_ex, n_out = len(extras), len(out_dtypes)
    used = [k for k, (_, kind) in enumerate(extras) if kind != "tie"]

    def body(a_ref, b_ref, *rest):
        ex_refs, out_refs = rest[:n_ex], rest[n_ex:]
        if trans_a:
            acc = lax.dot_general(a_ref[...], b_ref[...], (((0,), (0,)), ((), ())),
                                  preferred_element_type=F32)
        elif trans_b and b_blocks:
            acc = jnp.zeros((tm, tn), F32)
            for d in range(NDEV):
                acc = acc + lax.dot_general(a_ref[:, d * shard:(d + 1) * shard], b_ref[d],
                                            (((1,), (1,)), ((), ())), preferred_element_type=F32)
        elif trans_b:
            acc = lax.dot_general(a_ref[...], b_ref[...], (((1,), (1,)), ((), ())),
                                  preferred_element_type=F32)
        else:
            acc = jnp.dot(a_ref[...], b_ref[...], preferred_element_type=F32)
        outs = (acc,) if epilogue is None else epilogue(acc, *[ex_refs[k][...] for k in used])
        for o_ref, val in zip(out_refs, outs):
            o_ref[...] = val.astype(o_ref.dtype)

    outs = pl.pallas_call(
        body, name=name, grid=(m // tm, ncols // tn),
        in_specs=[a_spec, b_spec] + ex_specs,
        out_specs=[out_spec for _ in range(n_out)],
        out_shape=[jax.ShapeDtypeStruct(out_dims, dt) for dt in out_dtypes],
        compiler_params=_cparams(("parallel", "parallel")),
    )(a, b, *[arr for arr, _ in extras])
    return list(outs)


TR = 256

ROW_SPEC = pl.BlockSpec((TR, D), lambda i: (i, 0))
VEC_SPEC = pl.BlockSpec((1, D), lambda i: (0, 0))


def normmod_fwd(x, g, sc, sh, name):
    def body(x_ref, g_ref, sc_ref, sh_ref, o_ref):
        xv = x_ref[...]
        rstd = lax.rsqrt(jnp.mean(xv * xv, axis=-1, keepdims=True) + EPS)
        n = (xv * rstd) * g_ref[...]
        o_ref[...] = (n * (1.0 + sc_ref[...]) + sh_ref[...]).astype(o_ref.dtype)

    return pl.pallas_call(
        body, name=name, grid=(S // TR,),
        in_specs=[ROW_SPEC, VEC_SPEC, VEC_SPEC, VEC_SPEC], out_specs=ROW_SPEC,
        out_shape=jax.ShapeDtypeStruct((S, D), BF16),
        compiler_params=_cparams(("parallel",)),
    )(x, g, sc, sh)


def normmod_bwd(x, dh, dres, g, sc, name):
    def body(x_ref, dh_ref, dres_ref, g_ref, sc_ref, dx_ref, dsc_ref, dsh_ref, dg_ref):
        @pl.when(pl.program_id(0) == 0)
        def _():
            dsc_ref[...] = jnp.zeros_like(dsc_ref)
            dsh_ref[...] = jnp.zeros_like(dsh_ref)
            dg_ref[...] = jnp.zeros_like(dg_ref)

        xv, dh = x_ref[...], dh_ref[...]
        gv = g_ref[...]
        rstd = lax.rsqrt(jnp.mean(xv * xv, axis=-1, keepdims=True) + EPS)
        xhat = xv * rstd
        dn = dh * (1.0 + sc_ref[...])
        dxhat = dn * gv
        dx_ref[...] = dres_ref[...] + rstd * (dxhat - xhat * jnp.mean(dxhat * xhat, axis=-1, keepdims=True))
        dsc_ref[...] += jnp.sum(dh * (xhat * gv), axis=0, keepdims=True)
        dsh_ref[...] += jnp.sum(dh, axis=0, keepdims=True)
        dg_ref[...] += jnp.sum(dn * xhat, axis=0, keepdims=True)

    vec_out = jax.ShapeDtypeStruct((1, D), F32)
    return pl.pallas_call(
        body, name=name, grid=(S // TR,),
        in_specs=[ROW_SPEC, ROW_SPEC, ROW_SPEC, VEC_SPEC, VEC_SPEC],
        out_specs=[ROW_SPEC, VEC_SPEC, VEC_SPEC, VEC_SPEC],
        out_shape=[jax.ShapeDtypeStruct((S, D), F32), vec_out, vec_out, vec_out],
        compiler_params=_cparams(("arbitrary",)),
    )(x, dh, dres, g, sc)


def gate_bwd(dx, branch, gate, name):
    def body(dx_ref, br_ref, gate_ref, o_ref, dgate_ref):
        @pl.when(pl.program_id(0) == 0)
        def _():
            dgate_ref[...] = jnp.zeros_like(dgate_ref)

        dxv = dx_ref[...]
        o_ref[...] = (dxv * gate_ref[...]).astype(o_ref.dtype)
        dgate_ref[...] += jnp.sum(dxv * br_ref[...], axis=0, keepdims=True)

    return pl.pallas_call(
        body, name=name, grid=(S // TR,),
        in_specs=[ROW_SPEC, ROW_SPEC, VEC_SPEC], out_specs=[ROW_SPEC, VEC_SPEC],
        out_shape=[jax.ShapeDtypeStruct((S, D), BF16), jax.ShapeDtypeStruct((1, D), F32)],
        compiler_params=_cparams(("arbitrary",)),
    )(dx, branch, gate)


def loss_head(x, target, g, name):
    def body(x_ref, t_ref, g_ref, dx_ref, loss_ref, dg_ref):
        @pl.when(pl.program_id(0) == 0)
        def _():
            loss_ref[...] = jnp.zeros_like(loss_ref)
            dg_ref[...] = jnp.zeros_like(dg_ref)

        xv, gv = x_ref[...], g_ref[...]
        rstd = lax.rsqrt(jnp.mean(xv * xv, axis=-1, keepdims=True) + EPS)
        xhat = xv * rstd
        err = xhat * gv - t_ref[...]
        loss_ref[...] += jnp.sum(err * err) * (0.5 / D)
        dy = err * (1.0 / D)
        dg_ref[...] += jnp.sum(dy * xhat, axis=0, keepdims=True)
        dxhat = dy * gv
        dx_ref[...] = rstd * (dxhat - xhat * jnp.mean(dxhat * xhat, axis=-1, keepdims=True))

    return pl.pallas_call(
        body, name=name, grid=(S // TR,),
        in_specs=[ROW_SPEC, ROW_SPEC, VEC_SPEC],
        out_specs=[ROW_SPEC, VEC_SPEC, VEC_SPEC],
        out_shape=[jax.ShapeDtypeStruct((S, D), F32), jax.ShapeDtypeStruct((1, D), F32),
                   jax.ShapeDtypeStruct((1, D), F32)],
        compiler_params=_cparams(("arbitrary",)),
    )(x, target, g)


TQ = 512
RS = 128
NSUB = TQ // RS
TK = 128


def _dot_hilo(a, tri_twice):
    hi = a.astype(BF16)
    lo = (a - hi.astype(F32)).astype(BF16)
    return jnp.dot(jnp.concatenate([hi, lo], axis=1), tri_twice, preferred_element_type=F32)


def _log_stay(z):
    return -(jnp.maximum(z, 0.0) + jnp.log(1.0 + jnp.exp(-jnp.abs(z))))


def _tri_and_ones(kind):
    row = jnp.bitwise_and(lax.broadcasted_iota(jnp.int32, (2 * TK, 2 * TK), 0), TK - 1)
    col = lax.broadcasted_iota(jnp.int32, (2 * TK, 2 * TK), 1)
    tri = {"after": row > col, "upto": row <= col, "before": row < col}[kind]
    return jnp.logical_or(col >= TK, tri).astype(BF16)


NPAIR = NH // 2
SCALE = HD ** -0.5


def _pair_specs(first_block):
    rows = pl.BlockSpec((TQ, LANES), lambda p, i: (i, first_block + p))
    whole = pl.BlockSpec((S, LANES), lambda p, i: (0, first_block + p))
    return rows, whole


Q_ROWS_SPEC, _ = _pair_specs(0)
_, K_ALL_SPEC = _pair_specs(NPAIR)
_, V_ALL_SPEC = _pair_specs(2 * NPAIR)
PAIR_ROWS_SPEC = pl.BlockSpec((TQ, LANES), lambda p, i: (i, p))
PAIR_ALL_SPEC = pl.BlockSpec((S, LANES), lambda p, i: (0, p))
PAIR_TOTAL_SPEC = pl.BlockSpec((2, TQ, TK), lambda p, i: (p, i, 0))


def _head_halves(x):
    first = lax.broadcasted_iota(jnp.int32, x.shape, 1) < HD
    zero = jnp.zeros_like(x)
    return jnp.where(first, x, zero), jnp.where(first, zero, x)


def _join_heads(a, b):
    return jnp.where(lax.broadcasted_iota(jnp.int32, a.shape, 1) < HD, a, b)


def _comm_hooks(comm, refs, n_in, n_out, n_scratch):
    nc = len(comm.arrs) if comm is not None else 0
    ins, cin = refs[:n_in], refs[n_in:n_in + nc]
    outs = refs[n_in + nc:n_in + nc + n_out]
    cout = refs[n_in + nc + n_out:n_in + 2 * nc + n_out]
    scratch = refs[n_in + 2 * nc + n_out:n_in + 2 * nc + n_out + n_scratch]
    sems = refs[n_in + 2 * nc + n_out + n_scratch:]
    phases = comm.phases(cin, cout, sems) if comm is not None else None
    return ins, outs, scratch, phases


def _with_comm(comm, in_specs, out_specs, out_shape, operands, scratch):
    if comm is None:
        return dict(in_specs=in_specs, out_specs=out_specs, out_shape=out_shape, scratch_shapes=scratch), operands
    nc = len(comm.arrs)
    return dict(in_specs=in_specs + [HBM_SPEC] * nc, out_specs=out_specs + [HBM_SPEC] * nc,
                out_shape=out_shape + comm.out_shape, scratch_shapes=scratch + comm.scratch), operands + comm.arrs


def attn_fwd(qkv, name, comm=None):
    n_steps = S // TQ

    def body(*refs):
        (q_ref, k_ref, v_ref), (o_ref, r_ref), (acc_ref, z_even, z_odd, w_ref), phases = _comm_hooks(
            comm, refs, 3, 2, 4)
        p = pl.program_id(0)
        i = pl.program_id(1)
        if phases is not None:
            pl.when(jnp.logical_and(p == 0, i == 0))(phases[0])
            pl.when(jnp.logical_and(p == NPAIR - 1, i == n_steps - 2))(phases[1])
        chains = [(sub, h) for sub in range(NSUB) for h in range(2)]
        q_sub = [_head_halves(q_ref[pl.ds(sub * RS, RS), :] * SCALE) for sub in range(NSUB)]
        s_off = lax.broadcasted_iota(jnp.int32, (RS, TK), 1)
        t_pos = [i * TQ + sub * RS + lax.broadcasted_iota(jnp.int32, (RS, TK), 0) for sub in range(NSUB)]
        after = _tri_and_ones("after")
        nblk = (i + 1) * (TQ // TK)

        acc_ref[...] = jnp.zeros_like(acc_ref)
        r_ref[...] = jnp.zeros_like(r_ref)

        def key_rows(block):
            return pl.ds(pl.multiple_of(block * TK, TK), TK)

        def store_scores(z_ref, block):
            kb = k_ref[key_rows(block), :]
            for c, (sub, h) in enumerate(chains):
                z_ref[c] = lax.dot_general(q_sub[sub][h], kb, (((1,), (1,)), ((), ())),
                                           preferred_element_type=F32)

        def add_weighted_values(block):
            vb = v_ref[key_rows(block), :]
            pv = [jnp.dot(w_ref[c], vb, preferred_element_type=F32) for c in range(len(chains))]
            for sub in range(NSUB):
                acc_ref[pl.ds(sub * RS, RS), :] += _join_heads(pv[2 * sub], pv[2 * sub + 1])

        w_ref[...] = jnp.zeros_like(w_ref)
        store_scores(z_even, nblk - 1)

        def step(block, z_ref, z_next_ref):
            add_weighted_values(jnp.minimum(block + 1, nblk - 1))
            store_scores(z_next_ref, jnp.maximum(block - 1, 0))
            mask = [(block * TK + s_off) < t for t in t_pos]
            ls, sums = [], []
            for c, (sub, h) in enumerate(chains):
                ls.append(_log_stay(z_ref[c]))
                sums.append(_dot_hilo(jnp.where(mask[sub], ls[c], 0.0), after))
            for c, (sub, h) in enumerate(chains):
                rows = pl.ds(sub * RS, RS)
                later = r_ref[h, rows, :]
                w = jnp.where(mask[sub], jnp.exp(z_ref[c] + ls[c] + (sums[c][:, :TK] + later)), 0.0)
                w_ref[c] = w.astype(BF16)
                r_ref[h, rows, :] = later + sums[c][:, TK:]

        @pl.loop(0, nblk // 2)
        def _(pair):
            block = nblk - 1 - 2 * pair
            step(block, z_even, z_odd)
            step(block - 1, z_odd, z_even)

        add_weighted_values(0)
        o_ref[...] = acc_ref[...].astype(o_ref.dtype)
        if phases is not None:
            pl.when(jnp.logical_and(p == NPAIR - 1, i == n_steps - 1))(phases[2])

    kwargs, operands = _with_comm(
        comm, [Q_ROWS_SPEC, K_ALL_SPEC, V_ALL_SPEC], [PAIR_ROWS_SPEC, PAIR_TOTAL_SPEC],
        [jax.ShapeDtypeStruct((S, NH * HD), BF16), jax.ShapeDtypeStruct((NH, S, TK), F32)], [qkv, qkv, qkv],
        [pltpu.VMEM((TQ, LANES), F32), pltpu.VMEM((2 * NSUB, RS, TK), F32), pltpu.VMEM((2 * NSUB, RS, TK), F32),
         pltpu.VMEM((2 * NSUB, RS, TK), BF16)])
    return pl.pallas_call(
        body, name=name, grid=(NPAIR, n_steps),
        compiler_params=_cparams(("arbitrary", "arbitrary")), **kwargs,
    )(*operands)


def attn_bwd(qkv, dout, totals, name, comm=None):
    n_steps = S // TQ

    def body(*refs):
        ((q_ref, k_ref, v_ref, do_ref, r_ref), (dq_ref, dk_ref, dv_ref),
         (z_even, z_odd, dw_even, dw_odd, before_ref, dbefore_ref, dz_ref, w_ref), phases) = _comm_hooks(
            comm, refs, 5, 3, 8)
        p = pl.program_id(0)
        i = pl.program_id(1)
        if phases is not None:
            pl.when(jnp.logical_and(p == 0, i == 0))(phases[0])
            pl.when(jnp.logical_and(p == NPAIR - 1, i == n_steps - 2))(phases[1])

        @pl.when(i == 0)
        def _():
            dk_ref[...] = jnp.zeros_like(dk_ref)
            dv_ref[...] = jnp.zeros_like(dv_ref)

        chains = [(sub, h) for sub in range(NSUB) for h in range(2)]
        nch = len(chains)
        qb = q_ref[...]
        dob = do_ref[...].astype(BF16)
        q_sub = [_head_halves(qb[sub * RS:(sub + 1) * RS] * SCALE) for sub in range(NSUB)]
        do_sub = [_head_halves(dob[sub * RS:(sub + 1) * RS]) for sub in range(NSUB)]
        s_off = lax.broadcasted_iota(jnp.int32, (RS, TK), 1)
        t_pos = [i * TQ + sub * RS + lax.broadcasted_iota(jnp.int32, (RS, TK), 0) for sub in range(NSUB)]
        upto = _tri_and_ones("upto")
        before_tri = _tri_and_ones("before")
        contract_lanes = (((1,), (1,)), ((), ()))
        contract_rows = (((0,), (0,)), ((), ()))

        nblk = (i + 1) * (TQ // TK)

        def key_rows(block):
            return pl.ds(pl.multiple_of(block * TK, TK), TK)

        def store_products(z_ref, dw_ref, block):
            kb = k_ref[key_rows(block), :]
            vb = v_ref[key_rows(block), :]
            for c, (sub, h) in enumerate(chains):
                z_ref[c] = lax.dot_general(q_sub[sub][h], kb, contract_lanes, preferred_element_type=F32)
                dw_ref[c] = lax.dot_general(do_sub[sub][h], vb, contract_lanes, preferred_element_type=F32)

        def add_gradients(block):
            kb = k_ref[key_rows(block), :]
            for sub in range(NSUB):
                rows = pl.ds(sub * RS, RS)
                dq_ref[rows, :] += _join_heads(*[jnp.dot(dz_ref[h, rows, :], kb, preferred_element_type=F32)
                                                 for h in range(2)])
            dk_ref[key_rows(block), :] += _join_heads(*[
                lax.dot_general(dz_ref[h], qb, contract_rows, preferred_element_type=F32) for h in range(2)])
            dv_ref[key_rows(block), :] += _join_heads(*[
                lax.dot_general(w_ref[h], dob, contract_rows, preferred_element_type=F32) for h in range(2)])

        for ref in (dq_ref, before_ref, dbefore_ref, dz_ref, w_ref):
            ref[...] = jnp.zeros_like(ref)
        store_products(z_even, dw_even, 0)

        def step(block, z_ref, dw_ref, z_next_ref, dw_next_ref):
            add_gradients(jnp.maximum(block - 1, 0))
            store_products(z_next_ref, dw_next_ref, jnp.minimum(block + 1, nblk - 1))
            mask = [(block * TK + s_off) < t for t in t_pos]
            ls, sums = [], []
            for c, (sub, h) in enumerate(chains):
                ls.append(_log_stay(z_ref[c]))
                sums.append(_dot_hilo(jnp.where(mask[sub], ls[c], 0.0), upto))
            dl, dsums = [], []
            for c, (sub, h) in enumerate(chains):
                rows = pl.ds(sub * RS, RS)
                before = before_ref[c]
                log_after = r_ref[h, rows, :] - (sums[c][:, :TK] + before)
                w = jnp.where(mask[sub], jnp.exp((z_ref[c] + ls[c]) + log_after), 0.0)
                dl.append(dw_ref[c] * w)
                dsums.append(_dot_hilo(dl[c], before_tri))
                w_ref[h, rows, :] = w.astype(BF16)
                before_ref[c] = before + sums[c][:, TK:]
            for c, (sub, h) in enumerate(chains):
                rows = pl.ds(sub * RS, RS)
                dbefore = dbefore_ref[c]
                beta = jnp.where(mask[sub], jnp.exp(z_ref[c] + ls[c]), 0.0)
                dstay = dsums[c][:, :TK] + dbefore
                dz_ref[h, rows, :] = ((dl[c] * (1.0 - beta) - beta * dstay) * SCALE).astype(BF16)
                dbefore_ref[c] = dbefore + dsums[c][:, TK:]

        @pl.loop(0, nblk // 2)
        def _(pair):
            step(2 * pair, z_even, dw_even, z_odd, dw_odd)
            step(2 * pair + 1, z_odd, dw_odd, z_even, dw_even)

        add_gradients(nblk - 1)
        if phases is not None:
            pl.when(jnp.logical_and(p == NPAIR - 1, i == n_steps - 1))(phases[2])

    full = jax.ShapeDtypeStruct((S, NH * HD), F32)
    kwargs, operands = _with_comm(
        comm, [Q_ROWS_SPEC, K_ALL_SPEC, V_ALL_SPEC, PAIR_ROWS_SPEC, PAIR_TOTAL_SPEC],
        [PAIR_ROWS_SPEC, PAIR_ALL_SPEC, PAIR_ALL_SPEC], [full, full, full], [qkv, qkv, qkv, dout, totals],
        [pltpu.VMEM((2 * NSUB, RS, TK), F32)] * 6 + [pltpu.VMEM((2, TQ, TK), BF16)] * 2)
    return pl.pallas_call(
        body, name=name, grid=(NPAIR, n_steps),
        compiler_params=_cparams(("arbitrary", "arbitrary")), **kwargs,
    )(*operands)


def _proj_cols(first_col):
    base = first_col // LANES
    return pl.BlockSpec((S, LANES), lambda j: (0, base + j))


CONV_OUT_SPEC = pl.BlockSpec((S, LANES), lambda j: (0, j))
CONV_DOUT_SPEC = pl.BlockSpec((S, LANES), lambda j: (0, (NH * HD) // LANES + j))
CONV_W_SPEC = pl.BlockSpec((8, LANES), lambda j: (0, j))
CONV_B_SPEC = pl.BlockSpec((1, LANES), lambda j: (0, j))


def _shift_down(u, n):
    rows = lax.broadcasted_iota(jnp.int32, u.shape, 0)
    return jnp.where(rows >= n, pltpu.roll(u, n, 0), 0.0)


def _shift_up(u, n):
    rows = lax.broadcasted_iota(jnp.int32, u.shape, 0)
    return jnp.where(rows < S - n, pltpu.roll(u, S - n, 0), 0.0)


def conv_fwd(proj, cw8, cb, name):
    def body(bg_ref, cg_ref, hc_ref, w_ref, b_ref, o_ref):
        u = cg_ref[...] * hc_ref[...]
        w = w_ref[...]
        y = w[0:1, :] * _shift_down(u, 2) + w[1:2, :] * _shift_down(u, 1) + w[2:3, :] * u + b_ref[...]
        o_ref[...] = bg_ref[...] * y

    return pl.pallas_call(
        body, name=name, grid=(CW // LANES,),
        in_specs=[_proj_cols(0), _proj_cols(CW), _proj_cols(2 * CW), CONV_W_SPEC, CONV_B_SPEC],
        out_specs=CONV_OUT_SPEC, out_shape=jax.ShapeDtypeStruct((S, CW), F32),
        compiler_params=_cparams(("parallel",)),
    )(proj, proj, proj, cw8, cb)


def conv_bwd(proj, dout, cw8, cb, name):
    def body(bg_ref, cg_ref, hc_ref, do_ref, w_ref, b_ref, dbg_ref, dcg_ref, dhc_ref, dw_ref, db_ref):
        cg, hc, do = cg_ref[...], hc_ref[...], do_ref[...]
        w = w_ref[...]
        u = cg * hc
        u1, u2 = _shift_down(u, 1), _shift_down(u, 2)
        y = w[0:1, :] * u2 + w[1:2, :] * u1 + w[2:3, :] * u + b_ref[...]
        dbg_ref[...] = do * y
        dy = do * bg_ref[...]
        db_ref[...] = jnp.sum(dy, axis=0, keepdims=True)
        dw_ref[...] = jnp.concatenate(
            [jnp.sum(dy * u2, axis=0, keepdims=True), jnp.sum(dy * u1, axis=0, keepdims=True),
             jnp.sum(dy * u, axis=0, keepdims=True), jnp.zeros((5, LANES), F32)], axis=0)
        du = w[2:3, :] * dy + w[1:2, :] * _shift_up(dy, 1) + w[0:1, :] * _shift_up(dy, 2)
        dcg_ref[...] = du * hc
        dhc_ref[...] = du * cg

    full = jax.ShapeDtypeStruct((S, CW), F32)
    return pl.pallas_call(
        body, name=name, grid=(CW // LANES,),
        in_specs=[_proj_cols(0), _proj_cols(CW), _proj_cols(2 * CW), CONV_DOUT_SPEC, CONV_W_SPEC, CONV_B_SPEC],
        out_specs=[CONV_OUT_SPEC, CONV_OUT_SPEC, CONV_OUT_SPEC, CONV_W_SPEC, CONV_B_SPEC],
        out_shape=[full, full, full, jax.ShapeDtypeStruct((8, CW), F32), jax.ShapeDtypeStruct((1, CW), F32)],
        compiler_params=_cparams(("parallel",)),
    )(proj, proj, proj, dout, cw8, cb)


GELU_K = math.sqrt(2.0 / math.pi)
GELU_C = 0.044715


def _gelu(x):
    return 0.5 * x * (1.0 + jnp.tanh(GELU_K * (x + GELU_C * (x * x * x))))


def _gelu_grad(x):
    t = jnp.tanh(GELU_K * (x + GELU_C * (x * x * x)))
    return 0.5 * (1.0 + t) + 0.5 * x * (1.0 - t * t) * (GELU_K * (1.0 + 3.0 * GELU_C * (x * x)))


def _sg_masks():
    row = lax.broadcasted_iota(jnp.int32, (T, T), 0)
    col = lax.broadcasted_iota(jnp.int32, (T, T), 1)
    causal = jnp.right_shift(row, 6) >= jnp.right_shift(col, 6)
    head_of_col = jnp.right_shift(lax.broadcasted_iota(jnp.int32, (T, CW), 1), 6)
    return causal, head_of_col


def _sg_mixed(vnb, sw_ref, bias, causal, head_of_col):
    mixed = bias
    for h in range(SG_HEADS):
        wh = jnp.where(causal, sw_ref[h], 0.0).astype(BF16)
        mh = jnp.dot(wh, vnb, preferred_element_type=F32)
        mixed = mixed + jnp.where(head_of_col == h, mh, 0.0)
    return mixed


SG_U_SPEC = pl.BlockSpec((T, CW), lambda n: (n, 3))
SG_V_SPEC = pl.BlockSpec((T, CW), lambda n: (n, 4))
SG_ROW_SPEC = pl.BlockSpec((T, CW), lambda n: (n, 0))
SG_DOUT_SPEC = pl.BlockSpec((T, CW), lambda n: (n, 3))
SG_G_SPEC = pl.BlockSpec((1, CW), lambda n: (0, 0))
SG_W_SPEC = pl.BlockSpec((SG_HEADS, T, T), lambda n: (0, 0, 0))
SG_BIAS_SPEC = pl.BlockSpec((T, CW), lambda n: (0, 0))


def sg_fwd(proj, gn, sw, bias, name):
    def body(u_ref, v_ref, g_ref, sw_ref, bias_ref, o_ref):
        causal, head_of_col = _sg_masks()
        gv = _gelu(v_ref[...])
        rstd = lax.rsqrt(jnp.mean(gv * gv, axis=-1, keepdims=True) + EPS)
        vnb = ((gv * rstd) * g_ref[...]).astype(BF16)
        mixed = _sg_mixed(vnb, sw_ref, bias_ref[...], causal, head_of_col)
        o_ref[...] = _gelu(u_ref[...]) * mixed

    return pl.pallas_call(
        body, name=name, grid=(S // T,),
        in_specs=[SG_U_SPEC, SG_V_SPEC, SG_G_SPEC, SG_W_SPEC, SG_BIAS_SPEC],
        out_specs=SG_ROW_SPEC, out_shape=jax.ShapeDtypeStruct((S, CW), F32),
        compiler_params=_cparams(("parallel",)),
    )(proj, proj, gn, sw, bias)


def sg_bwd(proj, dout, gn, sw, bias, name):
    def body(u_ref, v_ref, do_ref, g_ref, sw_ref, bias_ref, du_ref, dv_ref, dg_ref, dsw_ref, dbias_ref):
        @pl.when(pl.program_id(0) == 0)
        def _():
            dg_ref[...] = jnp.zeros_like(dg_ref)
            dsw_ref[...] = jnp.zeros_like(dsw_ref)
            dbias_ref[...] = jnp.zeros_like(dbias_ref)

        causal, head_of_col = _sg_masks()
        uv, vv, do, gnv = u_ref[...], v_ref[...], do_ref[...], g_ref[...]
        gv = _gelu(vv)
        rstd = lax.rsqrt(jnp.mean(gv * gv, axis=-1, keepdims=True) + EPS)
        xhat = gv * rstd
        vnb = (xhat * gnv).astype(BF16)
        mixed = _sg_mixed(vnb, sw_ref, bias_ref[...], causal, head_of_col)
        du_ref[...] = (do * mixed) * _gelu_grad(uv)
        dmix = do * _gelu(uv)
        dbias_ref[...] += dmix
        dmixb = dmix.astype(BF16)
        dvn = jnp.zeros((T, CW), F32)
        for h in range(SG_HEADS):
            wh = jnp.where(causal, sw_ref[h], 0.0).astype(BF16)
            dvh = lax.dot_general(wh, dmixb, (((0,), (0,)), ((), ())), preferred_element_type=F32)
            dvn = dvn + jnp.where(head_of_col == h, dvh, 0.0)
            dmh = jnp.where(head_of_col == h, dmixb, jnp.zeros_like(dmixb))
            dwh = lax.dot_general(dmh, vnb, (((1,), (1,)), ((), ())), preferred_element_type=F32)
            dsw_ref[h] += jnp.where(causal, dwh, 0.0)
        dg_ref[...] += jnp.sum(dvn * xhat, axis=0, keepdims=True)
        dxhat = dvn * gnv
        dgv = rstd * (dxhat - xhat * jnp.mean(dxhat * xhat, axis=-1, keepdims=True))
        dv_ref[...] = dgv * _gelu_grad(vv)

    full = jax.ShapeDtypeStruct((S, CW), F32)
    return pl.pallas_call(
        body, name=name, grid=(S // T,),
        in_specs=[SG_U_SPEC, SG_V_SPEC, SG_DOUT_SPEC, SG_G_SPEC, SG_W_SPEC, SG_BIAS_SPEC],
        out_specs=[SG_ROW_SPEC, SG_ROW_SPEC, SG_G_SPEC, SG_W_SPEC, SG_BIAS_SPEC],
        out_shape=[full, full, jax.ShapeDtypeStruct((1, CW), F32),
                   jax.ShapeDtypeStruct((SG_HEADS, T, T), F32), jax.ShapeDtypeStruct((T, CW), F32)],
        compiler_params=_cparams(("arbitrary",)),
    )(proj, proj, dout, gn, sw, bias)


ADA_COLS = NMOD * D // NDEV


def ada_fwd(c_all, ada_w, ada_b_mine, name):
    def body(c_ref, w_ref, b_ref, o_ref, ca_ref):
        cv = c_ref[...]
        ca = cv * (1.0 / (1.0 + jnp.exp(-cv)))
        ca_ref[...] = ca
        cab = ca.astype(BF16)
        for l in range(L):
            o_ref[l] = jnp.dot(cab, w_ref[l].astype(BF16), preferred_element_type=F32) + b_ref[l]

    return pl.pallas_call(
        body, name=name,
        out_shape=[jax.ShapeDtypeStruct((L, NDEV, ADA_COLS), F32), jax.ShapeDtypeStruct((NDEV, D), F32)],
        compiler_params=_cparams(),
    )(c_all, ada_w, ada_b_mine)


def ada_bwd(ca, dmod_cols, name):
    def body(ca_ref, dm_ref, o_ref):
        cab = ca_ref[...].astype(BF16)
        for l in range(L):
            o_ref[l] = lax.dot_general(cab, dm_ref[l].astype(BF16), (((0,), (0,)), ((), ())),
                                       preferred_element_type=F32)

    return pl.pallas_call(
        body, name=name, out_shape=jax.ShapeDtypeStruct((L, D, ADA_COLS), F32),
        compiler_params=_cparams(),
    )(ca, dmod_cols)


def _adamw(w, g, m, v):
    m = B1 * m + (1.0 - B1) * g
    v = B2 * v + (1.0 - B2) * (g * g)
    m_hat = m / BC1
    v_hat = v / BC2
    delta = -LR * (m_hat / (jnp.sqrt(v_hat) + AEPS) + WD * w)
    return delta, m, v


VEC_ROWS_PER_LAYER = 8
VEC_FINAL_ROW = L * VEC_ROWS_PER_LAYER
VEC_ROWS = VEC_FINAL_ROW + 8
W256_TAPS, W256_CONV_B, W256_GN, W256_BIAS = 0, 8, 9, 16
W256_ROWS_PER_LAYER = W256_BIAS + T


def small_update(vec_all, w256_all, sw_all, params, name):
    n_par = len(params)

    def body(*refs):
        vec_ref, w256_ref = refs[:2]
        sw_refs = refs[2:2 + L]
        par_refs = [refs[2 + L + 3 * k:2 + L + 3 * k + 3] for k in range(n_par)]
        out = refs[2 + L + 3 * n_par:]
        out_par = [out[4 * k:4 * k + 4] for k in range(n_par)]
        loss_ref, taps_ref, bias_ref = out[4 * n_par:]

        def total(ref, idx):
            acc = ref[(0,) + idx]
            for d in range(1, NDEV):
                acc = acc + ref[(d,) + idx]
            return acc

        def update(k, region, g):
            w_ref, m_ref, v_ref = par_refs[k]
            g_ref, d_ref, nm_ref, nv_ref = out_par[k]
            delta, nm, nv = _adamw(w_ref[region], g, m_ref[region], v_ref[region])
            g_ref[region] = g
            d_ref[region] = delta
            nm_ref[region] = nm
            nv_ref[region] = nv

        for l in range(L):
            base = l * VEC_ROWS_PER_LAYER
            for k in range(NMOD):
                update(0, (slice(l, l + 1), slice(k * D, (k + 1) * D)), total(vec_ref, (slice(base + k, base + k + 1),)))
            update(1, (slice(l, l + 1),), total(vec_ref, (slice(base + 6, base + 7),)))
            update(2, (slice(l, l + 1),), total(vec_ref, (slice(base + 7, base + 8),)))
            wbase = l * W256_ROWS_PER_LAYER
            update(4, (slice(l, l + 1),), total(w256_ref, (slice(wbase + W256_CONV_B, wbase + W256_CONV_B + 1),)))
            update(5, (slice(l, l + 1),), total(w256_ref, (slice(wbase + W256_GN, wbase + W256_GN + 1),)))
            update(6, (l,), total(sw_refs[l], ()))
            taps_ref[l] = total(w256_ref, (slice(wbase + W256_TAPS, wbase + W256_TAPS + 8),))
            bias_ref[l] = total(w256_ref, (slice(wbase + W256_BIAS, wbase + W256_BIAS + T),))
        update(3, (slice(0, 1),), total(vec_ref, (slice(VEC_FINAL_ROW, VEC_FINAL_ROW + 1),)))
        loss_ref[...] = total(vec_ref, (slice(VEC_FINAL_ROW + 1, VEC_FINAL_ROW + 2), slice(0, LANES)))

    out_shape = []
    for w, _, _ in params:
        out_shape += [jax.ShapeDtypeStruct(w.shape, F32)] * 4
    out_shape += [jax.ShapeDtypeStruct((1, LANES), F32), jax.ShapeDtypeStruct((L, 8, CW), F32),
                  jax.ShapeDtypeStruct((L, T, CW), F32)]
    outs = pl.pallas_call(body, name=name, out_shape=out_shape, compiler_params=_cparams())(
        vec_all, w256_all, *sw_all, *[a for p in params for a in p])
    return [outs[4 * k:4 * k + 4] for k in range(n_par)], outs[4 * n_par:]


def adamw_plain(w, g, m, v, tr, name):
    rows, cols = w.shape
    spec = pl.BlockSpec((tr, cols), lambda i: (i, 0))

    def body(w_ref, g_ref, m_ref, v_ref, d_ref, nm_ref, nv_ref):
        delta, nm, nv = _adamw(w_ref[...], g_ref[...], m_ref[...], v_ref[...])
        d_ref[...] = delta
        nm_ref[...] = nm
        nv_ref[...] = nv

    shp = jax.ShapeDtypeStruct((rows, cols), F32)
    return pl.pallas_call(
        body, name=name, grid=(rows // tr,), in_specs=[spec] * 4, out_specs=[spec] * 3,
        out_shape=[shp, shp, shp], compiler_params=_cparams(("parallel",)),
    )(w, g, m, v)


def adamw_reduce(w, parts, m, v, tr, name, tie=None):
    _, rows, cols = w.shape
    spec = pl.BlockSpec((None, tr, cols), lambda l, i: (l, i, 0))
    pspecs = [pl.BlockSpec((NDEV, tr, cols), lambda l, i, k=k: (0, jnp.where(l == k, i, 0), 0)) for k in range(L)]

    ties = [] if tie is None else [tie]

    def body(w_ref, p0_ref, p1_ref, m_ref, v_ref, *rest):
        g_ref, d_ref, nm_ref, nv_ref = rest[len(ties):]
        first_layer = pl.program_id(0) == 0
        g = jnp.zeros((tr, cols), F32)
        for d in range(NDEV):
            g = g + jnp.where(first_layer, p0_ref[d], p1_ref[d]).astype(F32)
        delta, nm, nv = _adamw(w_ref[...], g, m_ref[...], v_ref[...])
        g_ref[...] = g
        d_ref[...] = delta
        nm_ref[...] = nm
        nv_ref[...] = nv

    shp = jax.ShapeDtypeStruct(w.shape, F32)
    return pl.pallas_call(
        body, name=name, grid=(L, rows // tr),
        in_specs=[spec] + pspecs + [spec, spec] + [pl.BlockSpec(t.shape, lambda l, i: (0, 0)) for t in ties],
        out_specs=[spec] * 4, out_shape=[shp] * 4, compiler_params=_cparams(("parallel", "parallel")),
    )(w, *parts, m, v, *ties)


def _pad_rows(flat, rows):
    return jnp.pad(flat, (0, rows * LANES - flat.shape[0])).reshape(rows, LANES)


def kernel(x, c, ada_w, ada_b, norm_mix_g, norm_mlp_g, w_in, conv_w, conv_b, gmlp_norm_g, spatial_w, spatial_b, w_out, mlp_w1, mlp_w2, final_norm_g, loss_target, m_ada_w, m_ada_b, m_norm_mix_g, m_norm_mlp_g, m_w_in, m_conv_w, m_conv_b, m_gmlp_norm_g, m_spatial_w, m_spatial_b, m_w_out, m_mlp_w1, m_mlp_w2, m_final_norm_g, v_ada_w, v_ada_b, v_norm_mix_g, v_norm_mlp_g, v_w_in, v_conv_w, v_conv_b, v_gmlp_norm_g, v_spatial_w, v_spatial_b, v_w_out, v_mlp_w1, v_mlp_w2, v_final_norm_g):
    me = _lin(_my_pos())
    x0 = x[0]
    target = loss_target[0]
    conv_shard = conv_w.shape[-1]

    w_in_b, w_out_b, w1_b, w2_b = [w.astype(BF16) for w in (w_in, w_out, mlp_w1, mlp_w2)]
    pack0 = _pad_rows(jnp.concatenate([c.reshape(-1), conv_w.reshape(-1)]), 16)
    g0, gw_in0 = run_comm(Gather([pack0, w_in_b[0]]), "gather_first")
    g0 = g0.reshape(NDEV, 16 * LANES)
    c_all = g0[:, :D]
    conv_full = (g0[:, D:D + L * 3 * conv_shard].reshape(NDEV, L, 3, conv_shard)
                 .transpose(1, 2, 0, 3).reshape(L, 3, CW))

    def canonical_w_in(gathered):
        return gathered.transpose(1, 0, 2).reshape(D, PROJ)

    weight_plans = [Gather([w_out_b[l], w1_b[l], w2_b[l]]) for l in range(L)]
    W_in = [canonical_w_in(gw_in0), None]
    W_out, W1, W2 = [None] * L, [None] * L, [None] * L

    ada_b_mine = lax.dynamic_slice(ada_b, (0, me * ADA_COLS), (L, ADA_COLS)).reshape(L, 1, ADA_COLS)
    mod_part, c_act = ada_fwd(c_all, ada_w, ada_b_mine, "ada_fwd")
    gmod = run_comm(Gather([mod_part]), "gather_mod")[0]
    mod = lax.dynamic_index_in_dim(gmod, me, axis=2, keepdims=False)
    mod = mod.transpose(1, 0, 2).reshape(L, NMOD, 1, D)

    cw8 = jnp.pad(conv_full, ((0, 0), (0, 5), (0, 0)))
    sg_bias = jnp.repeat(spatial_b.transpose(0, 2, 1), HD, axis=2)

    saved = []
    xl = x0
    for l in range(L):
        sh_m, sc_m, g_m, sh_f, sc_f, g_f = [mod[l, k] for k in range(NMOD)]
        h1 = normmod_fwd(xl, norm_mix_g[l:l + 1], sc_m, sh_m, f"norm_mix_fwd{l}")
        if l > 0:
            W_in[l] = canonical_w_in(finish_copies(w_in_handle, xl, f"gather_w_in{l}_wait"))
        qkv = mm_layer("proj_qkv", l, h1, W_in[l], out_dtypes=[BF16], cols=(0, QKV))[0]
        proj = mm_layer("proj_rest", l, h1, W_in[l], out_dtypes=[F32], cols=(QKV, REST))[0]
        a_out, a_tot, *gathered = attn_fwd(qkv, f"attn_fwd{l}", comm=weight_plans[l])
        W_out[l] = gathered[0].reshape(D, D)
        W1[l] = gathered[1]
        W2[l] = gathered[2].reshape(DFF, D)
        if l + 1 < L:
            w_in_handle, token = start_copies(w_in_b[l + 1], me, f"gather_w_in{l + 1}_start", True, after=a_out)
            g_m = tied(g_m, token)
        c_out = conv_fwd(proj, cw8[l], conv_b[l:l + 1], f"conv_fwd{l}")
        s_out = sg_fwd(proj, gmlp_norm_g[l:l + 1], spatial_w[l], sg_bias[l], f"sg_fwd{l}")
        cat = jnp.concatenate([a_out, c_out.astype(BF16), s_out.astype(BF16)], axis=1)
        mix, x1 = mm_layer("mix", l, cat, W_out[l], out_dtypes=[F32, F32],
                           epilogue=lambda acc, xr, g: (acc, xr + g * acc), extras=[(xl, "tile"), (g_m, "col")])
        h2 = normmod_fwd(x1, norm_mlp_g[l:l + 1], sc_f, sh_f, f"norm_mlp_fwd{l}")
        ra, r = mm_layer("mlp_up", l, h2, W1[l], out_dtypes=[BF16, BF16], b_blocks=True,
                         epilogue=lambda acc: (jnp.maximum(acc, 0.0), jnp.square(jnp.maximum(acc, 0.0))))
        m2, x2 = mm_layer("mlp_down", l, r, W2[l], out_dtypes=[F32, F32],
                          epilogue=lambda acc, xr, g: (acc, xr + g * acc), extras=[(x1, "tile"), (g_f, "col")])
        saved.append(dict(x=xl, h1=h1, proj=proj, qkv=qkv, a_tot=a_tot, cat=cat, mix=mix,
                          x1=x1, h2=h2, ra=ra, r=r, m2=m2))
        xl = x2

    dx, loss_part, d_final_g = loss_head(xl, target, final_norm_g.reshape(1, D), "loss_head")

    p_in, p_out, p_w1, p_w2 = [None] * L, [None] * L, [None] * L, [None] * L
    w_in_grads = [None] * L
    vec_rows, d_norm_mix, d_norm_mlp = [None] * L, [None] * L, [None] * L
    dcw8, d_conv_b, d_gn, d_sw, d_bias = [None] * L, [None] * L, [None] * L, [None] * L, [None] * L
    for l in reversed(range(L)):
        sv = saved[l]
        sh_m, sc_m, g_m, sh_f, sc_f, g_f = [mod[l, k] for k in range(NMOD)]
        dm2, dg_f = gate_bwd(dx, sv["m2"], g_f, f"gate_mlp_bwd{l}")
        da = mm_layer("mlp_down_dgrad", l, dm2, W2[l], out_dtypes=[BF16], trans_b=True,
                      epilogue=lambda acc, rav: (acc * (2.0 * rav.astype(F32)),), extras=[(sv["ra"], "tile")])[0]
        dW2 = mm_layer("mlp_down_wgrad", l, sv["r"], dm2, out_dtypes=[BF16], trans_a=True)[0]
        dW1 = mm_layer("mlp_up_wgrad", l, sv["h2"], da, out_dtypes=[BF16], trans_a=True, out_blocks=True)[0]
        dh2 = mm_layer("mlp_up_dgrad", l, da, W1[l], out_dtypes=[F32], trans_b=True, b_blocks=True)[0]
        dx1, dsc_f, dsh_f, d_norm_mlp[l] = normmod_bwd(sv["x1"], dh2, dx, norm_mlp_g[l:l + 1], sc_f,
                                                       f"norm_mlp_bwd{l}")
        dmix, dg_m = gate_bwd(dx1, sv["mix"], g_m, f"gate_mix_bwd{l}")
        dcat = mm_layer("mix_dgrad", l, dmix, W_out[l], out_dtypes=[F32], trans_b=True)[0]
        dW_out = mm_layer("mix_wgrad", l, sv["cat"], dmix, out_dtypes=[BF16], trans_a=True)[0]
        ready = [dW2.reshape(NDEV, DFF // NDEV, D), dW1, dW_out.reshape(NDEV, D // NDEV, D)]
        dq, dk, dv, p_w2[l], p_w1[l], p_out[l] = attn_bwd(sv["qkv"], dcat, sv["a_tot"], f"attn_bwd{l}",
                                                          comm=Exchange(ready))
        dbg, dcg, dhc, dcw8[l], d_conv_b[l] = conv_bwd(sv["proj"], dcat, cw8[l], conv_b[l:l + 1], f"conv_bwd{l}")
        dus, dvs, d_gn[l], d_sw[l], d_bias[l] = sg_bwd(sv["proj"], dcat, gmlp_norm_g[l:l + 1],
                                                       spatial_w[l], sg_bias[l], f"sg_bwd{l}")
        dproj = jnp.concatenate([dq, dk, dv, dbg, dcg, dhc, dus, dvs], axis=1).astype(BF16)
        dW_in = mm_layer("proj_wgrad", l, sv["h1"], dproj, out_dtypes=[BF16], trans_a=True)[0]
        pieces = dW_in.reshape(D, NDEV, PROJ // NDEV).transpose(1, 0, 2)
        w_in_grads[l], token = start_copies(pieces, me, f"exchange_w_in{l}_start", False)
        dh1 = mm_layer("proj_dgrad", l, dproj, W_in[l], out_dtypes=[F32], trans_b=True, extras=[(token, "tie")])[0]
        dx, dsc_m, dsh_m, d_norm_mix[l] = normmod_bwd(sv["x"], dh1, dx1, tied(norm_mix_g[l:l + 1], token), sc_m,
                                                      f"norm_mix_bwd{l}")
        vec_rows[l] = [dsh_m, dsc_m, dg_m, dsh_f, dsc_f, dg_f, d_norm_mix[l], d_norm_mlp[l]]

    grad_x = dx.reshape(1, S, D)

    vec_pack = jnp.concatenate([row for l in range(L) for row in vec_rows[l]]
                               + [d_final_g, loss_part, jnp.zeros((VEC_ROWS - VEC_FINAL_ROW - 2, D), F32)], axis=0)
    w256_pack = jnp.concatenate([blk for l in range(L) for blk in (
        dcw8[l], d_conv_b[l], d_gn[l], jnp.zeros((W256_BIAS - W256_GN - 1, CW), F32), d_bias[l])], axis=0)
    vec_all, w256_all, *sw_all = run_comm(Gather([vec_pack, w256_pack] + d_sw), "gather_small_grads")

    dmod_all = (vec_all[:, :VEC_FINAL_ROW].reshape(NDEV, L, VEC_ROWS_PER_LAYER, D)[:, :, :NMOD]
                .reshape(NDEV, L, NMOD * D))
    dmod_cols = lax.dynamic_slice(dmod_all, (0, 0, me * ADA_COLS), (NDEV, L, ADA_COLS)).transpose(1, 0, 2)
    g_ada_w = ada_bwd(c_act, dmod_cols, "ada_bwd")

    g_w_out, d_w_out, nm_w_out, nv_w_out = adamw_reduce(w_out, p_out, m_w_out, v_w_out, 128, "adamw_w_out", tie=token)
    g_w1, d_w1, nm_w1, nv_w1 = adamw_reduce(mlp_w1, p_w1, m_mlp_w1, v_mlp_w1, 256, "adamw_mlp_w1", tie=token)
    g_w2, d_w2, nm_w2, nv_w2 = adamw_reduce(mlp_w2, p_w2, m_mlp_w2, v_mlp_w2, 256, "adamw_mlp_w2", tie=token)

    flat2 = lambda t: t.reshape(L * D, ADA_COLS)
    d_ada_w, nm_ada_w, nv_ada_w = [t.reshape(L, D, ADA_COLS) for t in adamw_plain(
        flat2(ada_w), flat2(g_ada_w), flat2(m_ada_w), flat2(v_ada_w), 256, "adamw_ada_w")]

    after = jnp.concatenate([t.reshape(-1)[:1] for t in (d_w_out, d_w1, d_w2, d_ada_w)])
    p_in = [finish_copies(w_in_grads[l], after, f"exchange_w_in{l}_wait") for l in range(L)]
    g_w_in, d_w_in, nm_w_in, nv_w_in = adamw_reduce(w_in, p_in, m_w_in, v_w_in, 256, "adamw_w_in")

    as_row = lambda t: t.reshape(1, D)
    small_params = [(ada_b, m_ada_b, v_ada_b), (norm_mix_g, m_norm_mix_g, v_norm_mix_g),
                    (norm_mlp_g, m_norm_mlp_g, v_norm_mlp_g),
                    (as_row(final_norm_g), as_row(m_final_norm_g), as_row(v_final_norm_g)),
                    (conv_b, m_conv_b, v_conv_b), (gmlp_norm_g, m_gmlp_norm_g, v_gmlp_norm_g),
                    (spatial_w, m_spatial_w, v_spatial_w)]
    updated, (loss_sum, taps_sum, bias_sum) = small_update(vec_all, w256_all, sw_all, small_params, "small_update")
    loss = loss_sum[0, 0]
    u_ada_b, u_norm_mix, u_norm_mlp, u_final, u_conv_b, u_gn, u_sw = updated
    u_final = [t.reshape(D) for t in u_final]
    g_conv_w = lax.dynamic_slice(taps_sum, (0, 0, me * conv_shard), (L, 3, conv_shard))
    g_sb = bias_sum.reshape(L, T, SG_HEADS, HD).sum(axis=3).transpose(0, 2, 1)
    flat_cw = lambda t: t.reshape(L * 3, conv_shard)
    u_conv_w = [g_conv_w] + [t.reshape(L, 3, conv_shard) for t in adamw_plain(
        flat_cw(conv_w), flat_cw(g_conv_w), flat_cw(m_conv_w), flat_cw(v_conv_w), L * 3, "adamw_conv_w")]
    flat_sb = lambda t: t.reshape(L * SG_HEADS, T)
    u_sb = [g_sb] + [t.reshape(L, SG_HEADS, T) for t in adamw_plain(
        flat_sb(spatial_b), flat_sb(g_sb), flat_sb(m_spatial_b), flat_sb(v_spatial_b), L * SG_HEADS, "adamw_spatial_b")]
    small_sets = [u_ada_b, u_norm_mix, u_norm_mlp, u_conv_w, u_conv_b, u_gn, u_sw, u_sb, u_final]
    small_g, sd, snm, snv = [[u[k] for u in small_sets] for k in range(4)]

    def ordered(big, small):
        ada, win, wout, w1, w2 = big
        return [ada, small[0], small[1], small[2], win, small[3], small[4], small[5], small[6], small[7],
                wout, w1, w2, small[8]]

    grads = ordered([g_ada_w, g_w_in, g_w_out, g_w1, g_w2], small_g)
    deltas = ordered([d_ada_w, d_w_in, d_w_out, d_w1, d_w2], sd)
    new_m = ordered([nm_ada_w, nm_w_in, nm_w_out, nm_w1, nm_w2], snm)
    new_v = ordered([nv_ada_w, nv_w_in, nv_w_out, nv_w1, nv_w2], snv)
    return (loss, grad_x, *grads, *deltas, *new_m, *new_v)
```

```python
import functools
import math

import jax
import jax.numpy as jnp
from jax import lax
from jax.experimental import pallas as pl
from jax.experimental.pallas import tpu as pltpu

F32 = jnp.float32
BF16 = jnp.bfloat16
MESH = pl.DeviceIdType.MESH

S = 2048
D = 1024
L = 2
NDEV = 8
HD = 64
NH = 8
PROJ = 2816
DFF = 4096
NMOD = 6
EPS = 1e-6
T = 128
SG_HEADS = 4
LANES = 128
CW = 256
QKV = 3 * NH * HD
REST = PROJ - QKV

LR, B1, B2, AEPS, WD, STEP = 0.001, 0.9, 0.999, 1e-08, 0.01, 10
BC1 = 1.0 - B1 ** STEP
BC2 = 1.0 - B2 ** STEP

VMEM_LIMIT = 48 * 1024 * 1024

HBM_SPEC = pl.BlockSpec(memory_space=pltpu.HBM)


def _cparams(sem=None):
    return pltpu.CompilerParams(dimension_semantics=sem, vmem_limit_bytes=VMEM_LIMIT)


def _my_pos():
    return lax.axis_index("x"), lax.axis_index("y"), lax.axis_index("c")


def _lin(p):
    return 4 * p[0] + 2 * p[1] + p[2]


class Gather:
    def __init__(self, arrs):
        self.arrs = list(arrs)
        n = len(self.arrs)
        self.out_shape = [jax.ShapeDtypeStruct((NDEV,) + a.shape, a.dtype) for a in self.arrs]
        self.scratch = [pltpu.SemaphoreType.DMA((n, 7)), pltpu.SemaphoreType.DMA((n, 7)),
                        pltpu.SemaphoreType.DMA((n,))]

    def phases(self, ins, outs, sems):
        n = len(self.arrs)
        send_sems, recv_sems, local_sems = sems
        x, y, c = _my_pos()
        me, sibling = (x, y, c), (x, y, 1 - c)
        chips = [(1 - x, y), (x, 1 - y), (1 - x, 1 - y)]

        def copy(a, k, block, to, src=None):
            slot = outs[a].at[_lin(block)]
            return pltpu.make_async_remote_copy(
                src_ref=slot if src is None else src, dst_ref=slot,
                send_sem=send_sems.at[a, k], recv_sem=recv_sems.at[a, k],
                device_id=to, device_id_type=MESH)

        def mine(a):
            return pltpu.make_async_copy(ins[a], outs[a].at[_lin(me)], local_sems.at[a])

        def first(a):
            return [copy(a, 0, me, sibling, src=ins[a])] + [
                copy(a, 1 + j, me, (*chip, c), src=ins[a]) for j, chip in enumerate(chips)]

        def passed(a):
            return [copy(a, 4 + j, (*chip, c), sibling) for j, chip in enumerate(chips)]

        def start():
            for a in range(n):
                mine(a).start()
                for cp in first(a):
                    cp.start()

        def relay():
            for j, chip in enumerate(chips):
                for a in range(n):
                    copy(a, 1 + j, (*chip, c), me).wait_recv()
                    passed(a)[j].start()

        def finish():
            for a in range(n):
                copy(a, 0, sibling, me).wait_recv()
            for j, chip in enumerate(chips):
                for a in range(n):
                    copy(a, 4 + j, (*chip, 1 - c), me).wait_recv()
            for a in range(n):
                for cp in first(a) + passed(a):
                    cp.wait_send()
                mine(a).wait()

        return start, relay, finish


class Exchange:
    def __init__(self, arrs):
        self.arrs = list(arrs)
        n = len(self.arrs)
        self.out_shape = [jax.ShapeDtypeStruct(a.shape, a.dtype) for a in self.arrs]
        self.scratch = [pltpu.SemaphoreType.DMA((n, 7)), pltpu.SemaphoreType.DMA((n, 7)),
                        pltpu.SemaphoreType.DMA((n,))]

    def phases(self, ins, outs, sems):
        n = len(self.arrs)
        send_sems, recv_sems, local_sems = sems
        x, y, c = _my_pos()
        me = (x, y, c)

        def peer(mask):
            return (1 - x if mask & 4 else x, 1 - y if mask & 2 else y, 1 - c if mask & 1 else c)

        def copy(a, mask):
            return pltpu.make_async_remote_copy(
                src_ref=ins[a].at[_lin(peer(mask))], dst_ref=outs[a].at[_lin(me)],
                send_sem=send_sems.at[a, mask - 1], recv_sem=recv_sems.at[a, mask - 1],
                device_id=peer(mask), device_id_type=MESH)

        def arrival(a, mask):
            return pltpu.make_async_remote_copy(
                src_ref=ins[a].at[_lin(me)], dst_ref=outs[a].at[_lin(peer(mask))],
                send_sem=send_sems.at[a, mask - 1], recv_sem=recv_sems.at[a, mask - 1],
                device_id=peer(mask), device_id_type=MESH)

        def mine(a):
            return pltpu.make_async_copy(ins[a].at[_lin(me)], outs[a].at[_lin(me)], local_sems.at[a])

        def start():
            for a in range(n):
                mine(a).start()
            for mask in (4, 2, 6, 1, 5, 3, 7):
                for a in range(n):
                    copy(a, mask).start()

        def relay():
            pass

        def finish():
            for mask in range(1, 8):
                for a in range(n):
                    arrival(a, mask).wait_recv()
            for mask in range(1, 8):
                for a in range(n):
                    copy(a, mask).wait_send()
            for a in range(n):
                mine(a).wait()

        return start, relay, finish


def run_comm(plan, name):
    n = len(plan.arrs)

    def body(*refs):
        start, relay, finish = plan.phases(refs[:n], refs[n:2 * n], refs[2 * n:])
        start()
        relay()
        finish()

    outs = pl.pallas_call(
        body, name=name, out_shape=plan.out_shape,
        in_specs=[HBM_SPEC] * n, out_specs=[HBM_SPEC] * n, scratch_shapes=plan.scratch,
    )(*plan.arrs)
    return list(outs)


SEM_SPEC = pl.BlockSpec(memory_space=pltpu.SEMAPHORE)
DATAFLOW = pltpu.SideEffectType.DATAFLOW_SIDE_EFFECTING


def _peer_copies(src_ref, land_ref, send_sems, recv_sems, same_block):
    x, y, c = _my_pos()
    me = (x, y, c)
    sends, arrivals = [], []
    for mask in (4, 2, 6, 1, 5, 3, 7):
        peer = (1 - x if mask & 4 else x, 1 - y if mask & 2 else y, 1 - c if mask & 1 else c)
        sends.append(pltpu.make_async_remote_copy(
            src_ref=src_ref if same_block else src_ref.at[_lin(peer)], dst_ref=land_ref.at[_lin(me)],
            send_sem=send_sems.at[mask - 1], recv_sem=recv_sems.at[mask - 1], device_id=peer, device_id_type=MESH))
        arrivals.append(pltpu.make_async_remote_copy(
            src_ref=src_ref if same_block else src_ref.at[_lin(me)], dst_ref=land_ref.at[_lin(peer)],
            send_sem=send_sems.at[mask - 1], recv_sem=recv_sems.at[mask - 1], device_id=peer, device_id_type=MESH))
    return sends, arrivals


def start_copies(src, me, name, same_block, after=None):
    own = src[None] if same_block else lax.dynamic_index_in_dim(src, me, axis=0, keepdims=True)
    landing = lax.dynamic_update_slice(lax.empty((NDEV,) + own.shape[1:], src.dtype), own, (me,) + (0,) * (own.ndim - 1))

    def body(src_ref, land_ref, *rest):
        send_sems, recv_sems, _, _, token = rest[-5:]
        sends, _ = _peer_copies(src_ref, land_ref, send_sems, recv_sems, same_block)
        for cp in sends:
            cp.start()
        token[...] = jnp.zeros_like(token)

    hbm = lambda a: pltpu.HBM(a.shape, a.dtype)
    extra = [] if after is None else [after]
    *handle, token = pl.pallas_call(
        body, name=name,
        out_shape=(pltpu.SemaphoreType.DMA((7,)), pltpu.SemaphoreType.DMA((7,)), hbm(src), hbm(landing),
                   jax.ShapeDtypeStruct((8, LANES), F32)),
        in_specs=[HBM_SPEC, HBM_SPEC] + [pl.BlockSpec(memory_space=pl.ANY)] * len(extra),
        out_specs=(SEM_SPEC, SEM_SPEC, HBM_SPEC, HBM_SPEC, pl.BlockSpec(memory_space=pltpu.VMEM)),
        input_output_aliases={0: 2, 1: 3},
        compiler_params=pltpu.CompilerParams(has_side_effects=DATAFLOW),
    )(pltpu.with_memory_space_constraint(src, pltpu.HBM), pltpu.with_memory_space_constraint(landing, pltpu.HBM),
      *extra)
    return (handle, same_block), token


def finish_copies(handle, after, name):
    (send_sems, recv_sems, src, landing), same_block = handle

    def body(src_ref, land_ref, send_sems, recv_sems, after_ref, src_dead, got_ref):
        sends, arrivals = _peer_copies(src_ref, land_ref, send_sems, recv_sems, same_block)
        for cp in sends:
            cp.wait_send()
        for cp in arrivals:
            cp.wait_recv()

    hbm = lambda a: pltpu.HBM(a.shape, a.dtype)
    return pl.pallas_call(
        body, name=name, out_shape=(hbm(src), hbm(landing)),
        in_specs=(HBM_SPEC, HBM_SPEC, SEM_SPEC, SEM_SPEC, pl.BlockSpec(memory_space=pl.ANY)),
        out_specs=(HBM_SPEC, HBM_SPEC), input_output_aliases={0: 0, 1: 1},
        compiler_params=pltpu.CompilerParams(has_side_effects=DATAFLOW),
    )(src, landing, send_sems, recv_sems, after)[1]


def tied(x, token):
    return x + token[0:1, 0:1].astype(x.dtype)


MM_TILES = {
    "proj_qkv": (S, 512), "proj_rest": (S, 256), "mix": (1024, 512), "mlp_up": (S, 512), "mlp_down": (1024, 256),
    "mlp_down_dgrad": (1024, 1024), "mlp_down_wgrad": (1024, 1024), "mlp_up_wgrad": (1024, 512),
    "mlp_up_dgrad": (1024, 512), "mix_dgrad": (1024, 512), "mix_wgrad": (512, 1024),
    "proj_wgrad": (1024, PROJ // 2), "proj_dgrad": (1024, 512),
}


def mm_layer(kind, l, a, b, **kw):
    tm, tn = MM_TILES[kind]
    return mm(a, b, tm=tm, tn=tn, name=f"{kind}{l}", **kw)


def mm(a, b, *, tm, tn, out_dtypes, epilogue=None, extras=(), name, trans_a=False, trans_b=False,
       cols=None, b_blocks=False, out_blocks=False):
    if trans_a:
        kdim, m = a.shape
    else:
        m, kdim = a.shape
    shard = b.shape[-1] if b_blocks else None
    if b_blocks:
        full = (b.shape[1], NDEV * shard)
    else:
        full = b.shape
    first, ncols = cols if cols is not None else (0, full[0] if trans_b else full[1])
    assert full[1 if trans_b else 0] == kdim and m % tm == 0 and ncols % tn == 0 and first % tn == 0
    j0 = first // tn
    if trans_a:
        a_spec = pl.BlockSpec((kdim, tm), lambda i, j: (0, i))
    else:
        a_spec = pl.BlockSpec((tm, kdim), lambda i, j: (i, 0))
    if b_blocks and trans_b:
        b_spec = pl.BlockSpec((NDEV, tn, shard), lambda i, j: (0, j0 + j, 0))
    elif b_blocks:
        assert tn == shard
        b_spec = pl.BlockSpec((None, kdim, tn), lambda i, j: (j0 + j, 0, 0))
    elif trans_b:
        b_spec = pl.BlockSpec((tn, kdim), lambda i, j: (j0 + j, 0))
    else:
        b_spec = pl.BlockSpec((kdim, tn), lambda i, j: (0, j0 + j))
    if out_blocks:
        assert tn * NDEV == ncols
        out_spec = pl.BlockSpec((None, tm, tn), lambda i, j: (j, i, 0))
        out_dims = (NDEV, m, tn)
    else:
        out_spec = pl.BlockSpec((tm, tn), lambda i, j: (i, j))
        out_dims = (m, ncols)
    ex_specs = []
    for arr, kind in extras:
        if kind == "tile":
            ex_specs.append(pl.BlockSpec((tm, tn), lambda i, j: (i, j)))
        elif kind == "col":
            ex_specs.append(pl.BlockSpec((1, tn), lambda i, j: (0, j)))
        else:
            ex_specs.append(pl.BlockSpec(arr.shape, lambda i, j: (0, 0)))
    n_ex, n_out = len(extras), len(out_dtypes)
    used = [k for k, (_, kind) in enumerate(extras) if kind != "tie"]

    def body(a_ref, b_ref, *rest):
        ex_refs, out_refs = rest[:n_ex], rest[n_ex:]
        if trans_a:
            acc = lax.dot_general(a_ref[...], b_ref[...], (((0,), (0,)), ((), ())),
                                  preferred_element_type=F32)
        elif trans_b and b_blocks:
            acc = jnp.zeros((tm, tn), F32)
            for d in range(NDEV):
                acc = acc + lax.dot_general(a_ref[:, d * shard:(d + 1) * shard], b_ref[d],
                                            (((1,), (1,)), ((), ())), preferred_element_type=F32)
        elif trans_b:
            acc = lax.dot_general(a_ref[...], b_ref[...], (((1,), (1,)), ((), ())),
                                  preferred_element_type=F32)
        else:
            acc = jnp.dot(a_ref[...], b_ref[...], preferred_element_type=F32)
        outs = (acc,) if epilogue is None else epilogue(acc, *[ex_refs[k][...] for k in used])
        for o_ref, val in zip(out_refs, outs):
            o_ref[...] = val.astype(o_ref.dtype)

    outs = pl.pallas_call(
        body, name=name, grid=(m // tm, ncols // tn),
        in_specs=[a_spec, b_spec] + ex_specs,
        out_specs=[out_spec for _ in range(n_out)],
        out_shape=[jax.ShapeDtypeStruct(out_dims, dt) for dt in out_dtypes],
        compiler_params=_cparams(("parallel", "parallel")),
    )(a, b, *[arr for arr, _ in extras])
    return list(outs)


TR = 256

ROW_SPEC = pl.BlockSpec((TR, D), lambda i: (i, 0))
VEC_SPEC = pl.BlockSpec((1, D), lambda i: (0, 0))


def normmod_fwd(x, g, sc, sh, name):
    def body(x_ref, g_ref, sc_ref, sh_ref, o_ref):
        xv = x_ref[...]
        rstd = lax.rsqrt(jnp.mean(xv * xv, axis=-1, keepdims=True) + EPS)
        n = (xv * rstd) * g_ref[...]
        o_ref[...] = (n * (1.0 + sc_ref[...]) + sh_ref[...]).astype(o_ref.dtype)

    return pl.pallas_call(
        body, name=name, grid=(S // TR,),
        in_specs=[ROW_SPEC, VEC_SPEC, VEC_SPEC, VEC_SPEC], out_specs=ROW_SPEC,
        out_shape=jax.ShapeDtypeStruct((S, D), BF16),
        compiler_params=_cparams(("parallel",)),
    )(x, g, sc, sh)


def normmod_bwd(x, dh, dres, g, sc, name):
    def body(x_ref, dh_ref, dres_ref, g_ref, sc_ref, dx_ref, dsc_ref, dsh_ref, dg_ref):
        @pl.when(pl.program_id(0) == 0)
        def _():
            dsc_ref[...] = jnp.zeros_like(dsc_ref)
            dsh_ref[...] = jnp.zeros_like(dsh_ref)
            dg_ref[...] = jnp.zeros_like(dg_ref)

        xv, dh = x_ref[...], dh_ref[...]
        gv = g_ref[...]
        rstd = lax.rsqrt(jnp.mean(xv * xv, axis=-1, keepdims=True) + EPS)
        xhat = xv * rstd
        dn = dh * (1.0 + sc_ref[...])
        dxhat = dn * gv
        dx_ref[...] = dres_ref[...] + rstd * (dxhat - xhat * jnp.mean(dxhat * xhat, axis=-1, keepdims=True))
        dsc_ref[...] += jnp.sum(dh * (xhat * gv), axis=0, keepdims=True)
        dsh_ref[...] += jnp.sum(dh, axis=0, keepdims=True)
        dg_ref[...] += jnp.sum(dn * xhat, axis=0, keepdims=True)

    vec_out = jax.ShapeDtypeStruct((1, D), F32)
    return pl.pallas_call(
        body, name=name, grid=(S // TR,),
        in_specs=[ROW_SPEC, ROW_SPEC, ROW_SPEC, VEC_SPEC, VEC_SPEC],
        out_specs=[ROW_SPEC, VEC_SPEC, VEC_SPEC, VEC_SPEC],
        out_shape=[jax.ShapeDtypeStruct((S, D), F32), vec_out, vec_out, vec_out],
        compiler_params=_cparams(("arbitrary",)),
    )(x, dh, dres, g, sc)


def gate_bwd(dx, branch, gate, name):
    def body(dx_ref, br_ref, gate_ref, o_ref, dgate_ref):
        @pl.when(pl.program_id(0) == 0)
        def _():
            dgate_ref[...] = jnp.zeros_like(dgate_ref)

        dxv = dx_ref[...]
        o_ref[...] = (dxv * gate_ref[...]).astype(o_ref.dtype)
        dgate_ref[...] += jnp.sum(dxv * br_ref[...], axis=0, keepdims=True)

    return pl.pallas_call(
        body, name=name, grid=(S // TR,),
        in_specs=[ROW_SPEC, ROW_SPEC, VEC_SPEC], out_specs=[ROW_SPEC, VEC_SPEC],
        out_shape=[jax.ShapeDtypeStruct((S, D), BF16), jax.ShapeDtypeStruct((1, D), F32)],
        compiler_params=_cparams(("arbitrary",)),
    )(dx, branch, gate)


def loss_head(x, target, g, name):
    def body(x_ref, t_ref, g_ref, dx_ref, loss_ref, dg_ref):
        @pl.when(pl.program_id(0) == 0)
        def _():
            loss_ref[...] = jnp.zeros_like(loss_ref)
            dg_ref[...] = jnp.zeros_like(dg_ref)

        xv, gv = x_ref[...], g_ref[...]
        rstd = lax.rsqrt(jnp.mean(xv * xv, axis=-1, keepdims=True) + EPS)
        xhat = xv * rstd
        err = xhat * gv - t_ref[...]
        loss_ref[...] += jnp.sum(err * err) * (0.5 / D)
        dy = err * (1.0 / D)
        dg_ref[...] += jnp.sum(dy * xhat, axis=0, keepdims=True)
        dxhat = dy * gv
        dx_ref[...] = rstd * (dxhat - xhat * jnp.mean(dxhat * xhat, axis=-1, keepdims=True))

    return pl.pallas_call(
        body, name=name, grid=(S // TR,),
        in_specs=[ROW_SPEC, ROW_SPEC, VEC_SPEC],
        out_specs=[ROW_SPEC, VEC_SPEC, VEC_SPEC],
        out_shape=[jax.ShapeDtypeStruct((S, D), F32), jax.ShapeDtypeStruct((1, D), F32),
                   jax.ShapeDtypeStruct((1, D), F32)],
        compiler_params=_cparams(("arbitrary",)),
    )(x, target, g)


TQ = 512
RS = 128
NSUB = TQ // RS
TK = 128


def _dot_hilo(a, tri_twice):
    hi = a.astype(BF16)
    lo = (a - hi.astype(F32)).astype(BF16)
    return jnp.dot(jnp.concatenate([hi, lo], axis=1), tri_twice, preferred_element_type=F32)


def _log_stay(z):
    return -(jnp.maximum(z, 0.0) + jnp.log(1.0 + jnp.exp(-jnp.abs(z))))


def _tri_and_ones(kind):
    row = jnp.bitwise_and(lax.broadcasted_iota(jnp.int32, (2 * TK, 2 * TK), 0), TK - 1)
    col = lax.broadcasted_iota(jnp.int32, (2 * TK, 2 * TK), 1)
    tri = {"after": row > col, "upto": row <= col, "before": row < col}[kind]
    return jnp.logical_or(col >= TK, tri).astype(BF16)


NPAIR = NH // 2
SCALE = HD ** -0.5


def _pair_specs(first_block):
    rows = pl.BlockSpec((TQ, LANES), lambda p, i: (i, first_block + p))
    whole = pl.BlockSpec((S, LANES), lambda p, i: (0, first_block + p))
    return rows, whole


Q_ROWS_SPEC, _ = _pair_specs(0)
_, K_ALL_SPEC = _pair_specs(NPAIR)
_, V_ALL_SPEC = _pair_specs(2 * NPAIR)
PAIR_ROWS_SPEC = pl.BlockSpec((TQ, LANES), lambda p, i: (i, p))
PAIR_ALL_SPEC = pl.BlockSpec((S, LANES), lambda p, i: (0, p))
PAIR_TOTAL_SPEC = pl.BlockSpec((2, TQ, TK), lambda p, i: (p, i, 0))


def _head_halves(x):
    first = lax.broadcasted_iota(jnp.int32, x.shape, 1) < HD
    zero = jnp.zeros_like(x)
    return jnp.where(first, x, zero), jnp.where(first, zero, x)


def _join_heads(a, b):
    return jnp.where(lax.broadcasted_iota(jnp.int32, a.shape, 1) < HD, a, b)


def _comm_hooks(comm, refs, n_in, n_out, n_scratch):
    nc = len(comm.arrs) if comm is not None else 0
    ins, cin = refs[:n_in], refs[n_in:n_in + nc]
    outs = refs[n_in + nc:n_in + nc + n_out]
    cout = refs[n_in + nc + n_out:n_in + 2 * nc + n_out]
    scratch = refs[n_in + 2 * nc + n_out:n_in + 2 * nc + n_out + n_scratch]
    sems = refs[n_in + 2 * nc + n_out + n_scratch:]
    phases = comm.phases(cin, cout, sems) if comm is not None else None
    return ins, outs, scratch, phases


def _with_comm(comm, in_specs, out_specs, out_shape, operands, scratch):
    if comm is None:
        return dict(in_specs=in_specs, out_specs=out_specs, out_shape=out_shape, scratch_shapes=scratch), operands
    nc = len(comm.arrs)
    return dict(in_specs=in_specs + [HBM_SPEC] * nc, out_specs=out_specs + [HBM_SPEC] * nc,
                out_shape=out_shape + comm.out_shape, scratch_shapes=scratch + comm.scratch), operands + comm.arrs


def attn_fwd(qkv, name, comm=None):
    n_steps = S // TQ

    def body(*refs):
        (q_ref, k_ref, v_ref), (o_ref, r_ref), (acc_ref, z_even, z_odd, w_ref), phases = _comm_hooks(
            comm, refs, 3, 2, 4)
        p = pl.program_id(0)
        i = pl.program_id(1)
        if phases is not None:
            pl.when(jnp.logical_and(p == 0, i == 0))(phases[0])
            pl.when(jnp.logical_and(p == NPAIR - 1, i == n_steps - 2))(phases[1])
        chains = [(sub, h) for sub in range(NSUB) for h in range(2)]
        q_sub = [_head_halves(q_ref[pl.ds(sub * RS, RS), :] * SCALE) for sub in range(NSUB)]
        after = _tri_and_ones("after")
        below_diagonal = (lax.broadcasted_iota(jnp.int32, (RS, TK), 1)
                          < lax.broadcasted_iota(jnp.int32, (RS, TK), 0))
        base = i * NSUB
        all_subs = list(range(NSUB))

        acc_ref[...] = jnp.zeros_like(acc_ref)
        r_ref[...] = jnp.zeros_like(r_ref)
        w_ref[...] = jnp.zeros_like(w_ref)

        def key_rows(block):
            return pl.ds(pl.multiple_of(block * TK, TK), TK)

        def store_scores(z_ref, block, subs):
            kb = k_ref[key_rows(block), :]
            for c, (sub, h) in enumerate(chains):
                if sub in subs:
                    z_ref[c] = lax.dot_general(q_sub[sub][h], kb, (((1,), (1,)), ((), ())),
                                               preferred_element_type=F32)

        def add_weighted_values(block, subs):
            vb = v_ref[key_rows(block), :]
            for sub in subs:
                acc_ref[pl.ds(sub * RS, RS), :] += _join_heads(*[
                    jnp.dot(w_ref[2 * sub + h], vb, preferred_element_type=F32) for h in range(2)])

        def step(block, z_ref, z_next_ref, subs, diagonal_sub, prev_subs, next_subs):
            if prev_subs:
                add_weighted_values(block + 1, prev_subs)
            if next_subs:
                store_scores(z_next_ref, jnp.maximum(block - 1, 0), next_subs)
            active = [(c, sub, h) for c, (sub, h) in enumerate(chains) if sub in subs]
            ls, sums = {}, {}
            for c, sub, h in active:
                ls[c] = _log_stay(z_ref[c])
                sums[c] = _dot_hilo(jnp.where(below_diagonal, ls[c], 0.0) if sub == diagonal_sub else ls[c], after)
            for c, sub, h in active:
                rows = pl.ds(sub * RS, RS)
                later = r_ref[h, rows, :]
                w = jnp.exp(z_ref[c] + ls[c] + (sums[c][:, :TK] + later))
                if sub == diagonal_sub:
                    w = jnp.where(below_diagonal, w, 0.0)
                w_ref[c] = w.astype(BF16)
                r_ref[h, rows, :] = later + sums[c][:, TK:]

        store_scores(z_even, base + NSUB - 1, [NSUB - 1])
        buffers = (z_even, z_odd)
        for j in reversed(range(NSUB)):
            subs = all_subs[j:]
            step(base + j, buffers[0], buffers[1], subs, j, all_subs[j + 1:], all_subs[j - 1:] if j else all_subs)
            buffers = buffers[::-1]
        assert buffers[0] is z_even

        @pl.loop(0, base // 2)
        def _(pair):
            block = base - 1 - 2 * pair
            step(block, z_even, z_odd, all_subs, None, all_subs, all_subs)
            step(block - 1, z_odd, z_even, all_subs, None, all_subs, all_subs)

        add_weighted_values(0, all_subs)
        o_ref[...] = acc_ref[...].astype(o_ref.dtype)
        if phases is not None:
            pl.when(jnp.logical_and(p == NPAIR - 1, i == n_steps - 1))(phases[2])

    kwargs, operands = _with_comm(
        comm, [Q_ROWS_SPEC, K_ALL_SPEC, V_ALL_SPEC], [PAIR_ROWS_SPEC, PAIR_TOTAL_SPEC],
        [jax.ShapeDtypeStruct((S, NH * HD), BF16), jax.ShapeDtypeStruct((NH, S, TK), F32)], [qkv, qkv, qkv],
        [pltpu.VMEM((TQ, LANES), F32), pltpu.VMEM((2 * NSUB, RS, TK), F32), pltpu.VMEM((2 * NSUB, RS, TK), F32),
         pltpu.VMEM((2 * NSUB, RS, TK), BF16)])
    return pl.pallas_call(
        body, name=name, grid=(NPAIR, n_steps),
        compiler_params=_cparams(("arbitrary", "arbitrary")), **kwargs,
    )(*operands)


def attn_bwd(qkv, dout, totals, name, comm=None):
    n_steps = S // TQ

    def body(*refs):
        ((q_ref, k_ref, v_ref, do_ref, r_ref), (dq_ref, dk_ref, dv_ref),
         (z_even, z_odd, dw_even, dw_odd, before_ref, dbefore_ref, dz_ref, w_ref), phases) = _comm_hooks(
            comm, refs, 5, 3, 8)
        p = pl.program_id(0)
        i = pl.program_id(1)
        if phases is not None:
            pl.when(jnp.logical_and(p == 0, i == 0))(phases[0])
            pl.when(jnp.logical_and(p == NPAIR - 1, i == n_steps - 2))(phases[1])

        @pl.when(i == 0)
        def _():
            dk_ref[...] = jnp.zeros_like(dk_ref)
            dv_ref[...] = jnp.zeros_like(dv_ref)

        chains = [(sub, h) for sub in range(NSUB) for h in range(2)]
        nch = len(chains)
        qb = q_ref[...]
        dob = do_ref[...].astype(BF16)
        q_sub = [_head_halves(qb[sub * RS:(sub + 1) * RS] * SCALE) for sub in range(NSUB)]
        do_sub = [_head_halves(dob[sub * RS:(sub + 1) * RS]) for sub in range(NSUB)]
        upto = _tri_and_ones("upto")
        before_tri = _tri_and_ones("before")
        below_diagonal = (lax.broadcasted_iota(jnp.int32, (RS, TK), 1)
                          < lax.broadcasted_iota(jnp.int32, (RS, TK), 0))
        contract_lanes = (((1,), (1,)), ((), ()))
        contract_rows = (((0,), (0,)), ((), ()))
        base = i * NSUB
        all_subs = list(range(NSUB))

        def key_rows(block):
            return pl.ds(pl.multiple_of(block * TK, TK), TK)

        def store_products(bufs, block, subs):
            z_ref, dw_ref = bufs
            kb = k_ref[key_rows(block), :]
            vb = v_ref[key_rows(block), :]
            for c, (sub, h) in enumerate(chains):
                if sub in subs:
                    z_ref[c] = lax.dot_general(q_sub[sub][h], kb, contract_lanes, preferred_element_type=F32)
                    dw_ref[c] = lax.dot_general(do_sub[sub][h], vb, contract_lanes, preferred_element_type=F32)

        def add_gradients(block, subs):
            kb = k_ref[key_rows(block), :]
            for sub in subs:
                rows = pl.ds(sub * RS, RS)
                dq_ref[rows, :] += _join_heads(*[jnp.dot(dz_ref[h, rows, :], kb, preferred_element_type=F32)
                                                 for h in range(2)])
            dk_ref[key_rows(block), :] += _join_heads(*[
                lax.dot_general(dz_ref[h], qb, contract_rows, preferred_element_type=F32) for h in range(2)])
            dv_ref[key_rows(block), :] += _join_heads(*[
                lax.dot_general(w_ref[h], dob, contract_rows, preferred_element_type=F32) for h in range(2)])

        for ref in (dq_ref, before_ref, dbefore_ref, dz_ref, w_ref):
            ref[...] = jnp.zeros_like(ref)
        even, odd = (z_even, dw_even), (z_odd, dw_odd)
        store_products(even, 0, all_subs)

        def step(block, bufs, next_bufs, subs, diagonal_sub, prev_subs, next_subs):
            z_ref, dw_ref = bufs
            add_gradients(jnp.maximum(block - 1, 0), prev_subs)
            for sub in prev_subs:
                if sub not in subs:
                    dz_ref[:, pl.ds(sub * RS, RS), :] = jnp.zeros((2, RS, TK), BF16)
                    w_ref[:, pl.ds(sub * RS, RS), :] = jnp.zeros((2, RS, TK), BF16)
            if next_subs:
                store_products(next_bufs, block + 1, next_subs)
            active = [(c, sub, h) for c, (sub, h) in enumerate(chains) if sub in subs]
            ls, sums, dl, dsums = {}, {}, {}, {}
            for c, sub, h in active:
                ls[c] = _log_stay(z_ref[c])
                sums[c] = _dot_hilo(jnp.where(below_diagonal, ls[c], 0.0) if sub == diagonal_sub else ls[c], upto)
            for c, sub, h in active:
                rows = pl.ds(sub * RS, RS)
                before = before_ref[c]
                log_after = r_ref[h, rows, :] - (sums[c][:, :TK] + before)
                w = jnp.exp((z_ref[c] + ls[c]) + log_after)
                if sub == diagonal_sub:
                    w = jnp.where(below_diagonal, w, 0.0)
                dl[c] = dw_ref[c] * w
                dsums[c] = _dot_hilo(dl[c], before_tri)
                w_ref[h, rows, :] = w.astype(BF16)
                before_ref[c] = before + sums[c][:, TK:]
            for c, sub, h in active:
                rows = pl.ds(sub * RS, RS)
                dbefore = dbefore_ref[c]
                beta = jnp.exp(z_ref[c] + ls[c])
                if sub == diagonal_sub:
                    beta = jnp.where(below_diagonal, beta, 0.0)
                dstay = dsums[c][:, :TK] + dbefore
                dz_ref[h, rows, :] = ((dl[c] * (1.0 - beta) - beta * dstay) * SCALE).astype(BF16)
                dbefore_ref[c] = dbefore + dsums[c][:, TK:]

        @pl.loop(0, base // 2)
        def _(pair):
            step(2 * pair, even, odd, all_subs, None, all_subs, all_subs)
            step(2 * pair + 1, odd, even, all_subs, None, all_subs, all_subs)

        bufs = (even, odd)
        for j in range(NSUB):
            step(base + j, bufs[0], bufs[1], all_subs[j:], j, all_subs[j - 1:] if j else all_subs, all_subs[j + 1:])
            bufs = bufs[::-1]

        add_gradients(base + NSUB - 1, all_subs[NSUB - 1:])
        if phases is not None:
            pl.when(jnp.logical_and(p == NPAIR - 1, i == n_steps - 1))(phases[2])

    full = jax.ShapeDtypeStruct((S, NH * HD), F32)
    kwargs, operands = _with_comm(
        comm, [Q_ROWS_SPEC, K_ALL_SPEC, V_ALL_SPEC, PAIR_ROWS_SPEC, PAIR_TOTAL_SPEC],
        [PAIR_ROWS_SPEC, PAIR_ALL_SPEC, PAIR_ALL_SPEC], [full, full, full], [qkv, qkv, qkv, dout, totals],
        [pltpu.VMEM((2 * NSUB, RS, TK), F32)] * 6 + [pltpu.VMEM((2, TQ, TK), BF16)] * 2)
    return pl.pallas_call(
        body, name=name, grid=(NPAIR, n_steps),
        compiler_params=_cparams(("arbitrary", "arbitrary")), **kwargs,
    )(*operands)


def _proj_cols(first_col):
    base = first_col // LANES
    return pl.BlockSpec((S, LANES), lambda j: (0, base + j))


CONV_OUT_SPEC = pl.BlockSpec((S, LANES), lambda j: (0, j))
CONV_DOUT_SPEC = pl.BlockSpec((S, LANES), lambda j: (0, (NH * HD) // LANES + j))
CONV_W_SPEC = pl.BlockSpec((8, LANES), lambda j: (0, j))
CONV_B_SPEC = pl.BlockSpec((1, LANES), lambda j: (0, j))


def _shift_down(u, n):
    rows = lax.broadcasted_iota(jnp.int32, u.shape, 0)
    return jnp.where(rows >= n, pltpu.roll(u, n, 0), 0.0)


def _shift_up(u, n):
    rows = lax.broadcasted_iota(jnp.int32, u.shape, 0)
    return jnp.where(rows < S - n, pltpu.roll(u, S - n, 0), 0.0)


def conv_fwd(proj, cw8, cb, name):
    def body(bg_ref, cg_ref, hc_ref, w_ref, b_ref, o_ref):
        u = cg_ref[...] * hc_ref[...]
        w = w_ref[...]
        y = w[0:1, :] * _shift_down(u, 2) + w[1:2, :] * _shift_down(u, 1) + w[2:3, :] * u + b_ref[...]
        o_ref[...] = bg_ref[...] * y

    return pl.pallas_call(
        body, name=name, grid=(CW // LANES,),
        in_specs=[_proj_cols(0), _proj_cols(CW), _proj_cols(2 * CW), CONV_W_SPEC, CONV_B_SPEC],
        out_specs=CONV_OUT_SPEC, out_shape=jax.ShapeDtypeStruct((S, CW), F32),
        compiler_params=_cparams(("parallel",)),
    )(proj, proj, proj, cw8, cb)


def conv_bwd(proj, dout, cw8, cb, name):
    def body(bg_ref, cg_ref, hc_ref, do_ref, w_ref, b_ref, dbg_ref, dcg_ref, dhc_ref, dw_ref, db_ref):
        cg, hc, do = cg_ref[...], hc_ref[...], do_ref[...]
        w = w_ref[...]
        u = cg * hc
        u1, u2 = _shift_down(u, 1), _shift_down(u, 2)
        y = w[0:1, :] * u2 + w[1:2, :] * u1 + w[2:3, :] * u + b_ref[...]
        dbg_ref[...] = do * y
        dy = do * bg_ref[...]
        db_ref[...] = jnp.sum(dy, axis=0, keepdims=True)
        dw_ref[...] = jnp.concatenate(
            [jnp.sum(dy * u2, axis=0, keepdims=True), jnp.sum(dy * u1, axis=0, keepdims=True),
             jnp.sum(dy * u, axis=0, keepdims=True), jnp.zeros((5, LANES), F32)], axis=0)
        du = w[2:3, :] * dy + w[1:2, :] * _shift_up(dy, 1) + w[0:1, :] * _shift_up(dy, 2)
        dcg_ref[...] = du * hc
        dhc_ref[...] = du * cg

    full = jax.ShapeDtypeStruct((S, CW), F32)
    return pl.pallas_call(
        body, name=name, grid=(CW // LANES,),
        in_specs=[_proj_cols(0), _proj_cols(CW), _proj_cols(2 * CW), CONV_DOUT_SPEC, CONV_W_SPEC, CONV_B_SPEC],
        out_specs=[CONV_OUT_SPEC, CONV_OUT_SPEC, CONV_OUT_SPEC, CONV_W_SPEC, CONV_B_SPEC],
        out_shape=[full, full, full, jax.ShapeDtypeStruct((8, CW), F32), jax.ShapeDtypeStruct((1, CW), F32)],
        compiler_params=_cparams(("parallel",)),
    )(proj, proj, proj, dout, cw8, cb)


GELU_K = math.sqrt(2.0 / math.pi)
GELU_C = 0.044715


def _gelu(x):
    return 0.5 * x * (1.0 + jnp.tanh(GELU_K * (x + GELU_C * (x * x * x))))


def _gelu_grad(x):
    t = jnp.tanh(GELU_K * (x + GELU_C * (x * x * x)))
    return 0.5 * (1.0 + t) + 0.5 * x * (1.0 - t * t) * (GELU_K * (1.0 + 3.0 * GELU_C * (x * x)))


def _sg_masks():
    row = lax.broadcasted_iota(jnp.int32, (T, T), 0)
    col = lax.broadcasted_iota(jnp.int32, (T, T), 1)
    causal = jnp.right_shift(row, 6) >= jnp.right_shift(col, 6)
    head_of_col = jnp.right_shift(lax.broadcasted_iota(jnp.int32, (T, CW), 1), 6)
    return causal, head_of_col


def _sg_mixed(vnb, sw_ref, bias, causal, head_of_col):
    mixed = bias
    for h in range(SG_HEADS):
        wh = jnp.where(causal, sw_ref[h], 0.0).astype(BF16)
        mh = jnp.dot(wh, vnb, preferred_element_type=F32)
        mixed = mixed + jnp.where(head_of_col == h, mh, 0.0)
    return mixed


SG_U_SPEC = pl.BlockSpec((T, CW), lambda n: (n, 3))
SG_V_SPEC = pl.BlockSpec((T, CW), lambda n: (n, 4))
SG_ROW_SPEC = pl.BlockSpec((T, CW), lambda n: (n, 0))
SG_DOUT_SPEC = pl.BlockSpec((T, CW), lambda n: (n, 3))
SG_G_SPEC = pl.BlockSpec((1, CW), lambda n: (0, 0))
SG_W_SPEC = pl.BlockSpec((SG_HEADS, T, T), lambda n: (0, 0, 0))
SG_BIAS_SPEC = pl.BlockSpec((T, CW), lambda n: (0, 0))


def sg_fwd(proj, gn, sw, bias, name):
    def body(u_ref, v_ref, g_ref, sw_ref, bias_ref, o_ref):
        causal, head_of_col = _sg_masks()
        gv = _gelu(v_ref[...])
        rstd = lax.rsqrt(jnp.mean(gv * gv, axis=-1, keepdims=True) + EPS)
        vnb = ((gv * rstd) * g_ref[...]).astype(BF16)
        mixed = _sg_mixed(vnb, sw_ref, bias_ref[...], causal, head_of_col)
        o_ref[...] = _gelu(u_ref[...]) * mixed

    return pl.pallas_call(
        body, name=name, grid=(S // T,),
        in_specs=[SG_U_SPEC, SG_V_SPEC, SG_G_SPEC, SG_W_SPEC, SG_BIAS_SPEC],
        out_specs=SG_ROW_SPEC, out_shape=jax.ShapeDtypeStruct((S, CW), F32),
        compiler_params=_cparams(("parallel",)),
    )(proj, proj, gn, sw, bias)


def sg_bwd(proj, dout, gn, sw, bias, name):
    def body(u_ref, v_ref, do_ref, g_ref, sw_ref, bias_ref, du_ref, dv_ref, dg_ref, dsw_ref, dbias_ref):
        @pl.when(pl.program_id(0) == 0)
        def _():
            dg_ref[...] = jnp.zeros_like(dg_ref)
            dsw_ref[...] = jnp.zeros_like(dsw_ref)
            dbias_ref[...] = jnp.zeros_like(dbias_ref)

        causal, head_of_col = _sg_masks()
        uv, vv, do, gnv = u_ref[...], v_ref[...], do_ref[...], g_ref[...]
        gv = _gelu(vv)
        rstd = lax.rsqrt(jnp.mean(gv * gv, axis=-1, keepdims=True) + EPS)
        xhat = gv * rstd
        vnb = (xhat * gnv).astype(BF16)
        mixed = _sg_mixed(vnb, sw_ref, bias_ref[...], causal, head_of_col)
        du_ref[...] = (do * mixed) * _gelu_grad(uv)
        dmix = do * _gelu(uv)
        dbias_ref[...] += dmix
        dmixb = dmix.astype(BF16)
        dvn = jnp.zeros((T, CW), F32)
        for h in range(SG_HEADS):
            wh = jnp.where(causal, sw_ref[h], 0.0).astype(BF16)
            dvh = lax.dot_general(wh, dmixb, (((0,), (0,)), ((), ())), preferred_element_type=F32)
            dvn = dvn + jnp.where(head_of_col == h, dvh, 0.0)
            dmh = jnp.where(head_of_col == h, dmixb, jnp.zeros_like(dmixb))
            dwh = lax.dot_general(dmh, vnb, (((1,), (1,)), ((), ())), preferred_element_type=F32)
            dsw_ref[h] += jnp.where(causal, dwh, 0.0)
        dg_ref[...] += jnp.sum(dvn * xhat, axis=0, keepdims=True)
        dxhat = dvn * gnv
        dgv = rstd * (dxhat - xhat * jnp.mean(dxhat * xhat, axis=-1, keepdims=True))
        dv_ref[...] = dgv * _gelu_grad(vv)

    full = jax.ShapeDtypeStruct((S, CW), F32)
    return pl.pallas_call(
        body, name=name, grid=(S // T,),
        in_specs=[SG_U_SPEC, SG_V_SPEC, SG_DOUT_SPEC, SG_G_SPEC, SG_W_SPEC, SG_BIAS_SPEC],
        out_specs=[SG_ROW_SPEC, SG_ROW_SPEC, SG_G_SPEC, SG_W_SPEC, SG_BIAS_SPEC],
        out_shape=[full, full, jax.ShapeDtypeStruct((1, CW), F32),
                   jax.ShapeDtypeStruct((SG_HEADS, T, T), F32), jax.ShapeDtypeStruct((T, CW), F32)],
        compiler_params=_cparams(("arbitrary",)),
    )(proj, proj, dout, gn, sw, bias)


ADA_COLS = NMOD * D // NDEV


def ada_fwd(c_all, ada_w, ada_b_mine, name):
    def body(c_ref, w_ref, b_ref, o_ref, ca_ref):
        cv = c_ref[...]
        ca = cv * (1.0 / (1.0 + jnp.exp(-cv)))
        ca_ref[...] = ca
        cab = ca.astype(BF16)
        for l in range(L):
            o_ref[l] = jnp.dot(cab, w_ref[l].astype(BF16), preferred_element_type=F32) + b_ref[l]

    return pl.pallas_call(
        body, name=name,
        out_shape=[jax.ShapeDtypeStruct((L, NDEV, ADA_COLS), F32), jax.ShapeDtypeStruct((NDEV, D), F32)],
        compiler_params=_cparams(),
    )(c_all, ada_w, ada_b_mine)


def ada_bwd(ca, dmod_cols, name):
    def body(ca_ref, dm_ref, o_ref):
        cab = ca_ref[...].astype(BF16)
        for l in range(L):
            o_ref[l] = lax.dot_general(cab, dm_ref[l].astype(BF16), (((0,), (0,)), ((), ())),
                                       preferred_element_type=F32)

    return pl.pallas_call(
        body, name=name, out_shape=jax.ShapeDtypeStruct((L, D, ADA_COLS), F32),
        compiler_params=_cparams(),
    )(ca, dmod_cols)


def _adamw(w, g, m, v):
    m = B1 * m + (1.0 - B1) * g
    v = B2 * v + (1.0 - B2) * (g * g)
    m_hat = m / BC1
    v_hat = v / BC2
    delta = -LR * (m_hat / (jnp.sqrt(v_hat) + AEPS) + WD * w)
    return delta, m, v


VEC_ROWS_PER_LAYER = 8
VEC_FINAL_ROW = L * VEC_ROWS_PER_LAYER
VEC_ROWS = VEC_FINAL_ROW + 8
W256_TAPS, W256_CONV_B, W256_GN, W256_BIAS = 0, 8, 9, 16
W256_ROWS_PER_LAYER = W256_BIAS + T


def small_update(vec_all, w256_all, sw_all, params, name):
    n_par = len(params)

    def body(*refs):
        vec_ref, w256_ref = refs[:2]
        sw_refs = refs[2:2 + L]
        par_refs = [refs[2 + L + 3 * k:2 + L + 3 * k + 3] for k in range(n_par)]
        out = refs[2 + L + 3 * n_par:]
        out_par = [out[4 * k:4 * k + 4] for k in range(n_par)]
        loss_ref, taps_ref, bias_ref = out[4 * n_par:]

        def total(ref, idx):
            acc = ref[(0,) + idx]
            for d in range(1, NDEV):
                acc = acc + ref[(d,) + idx]
            return acc

        def update(k, region, g):
            w_ref, m_ref, v_ref = par_refs[k]
            g_ref, d_ref, nm_ref, nv_ref = out_par[k]
            delta, nm, nv = _adamw(w_ref[region], g, m_ref[region], v_ref[region])
            g_ref[region] = g
            d_ref[region] = delta
            nm_ref[region] = nm
            nv_ref[region] = nv

        for l in range(L):
            base = l * VEC_ROWS_PER_LAYER
            for k in range(NMOD):
                update(0, (slice(l, l + 1), slice(k * D, (k + 1) * D)), total(vec_ref, (slice(base + k, base + k + 1),)))
            update(1, (slice(l, l + 1),), total(vec_ref, (slice(base + 6, base + 7),)))
            update(2, (slice(l, l + 1),), total(vec_ref, (slice(base + 7, base + 8),)))
            wbase = l * W256_ROWS_PER_LAYER
            update(4, (slice(l, l + 1),), total(w256_ref, (slice(wbase + W256_CONV_B, wbase + W256_CONV_B + 1),)))
            update(5, (slice(l, l + 1),), total(w256_ref, (slice(wbase + W256_GN, wbase + W256_GN + 1),)))
            update(6, (l,), total(sw_refs[l], ()))
            taps_ref[l] = total(w256_ref, (slice(wbase + W256_TAPS, wbase + W256_TAPS + 8),))
            bias_ref[l] = total(w256_ref, (slice(wbase + W256_BIAS, wbase + W256_BIAS + T),))
        update(3, (slice(0, 1),), total(vec_ref, (slice(VEC_FINAL_ROW, VEC_FINAL_ROW + 1),)))
        loss_ref[...] = total(vec_ref, (slice(VEC_FINAL_ROW + 1, VEC_FINAL_ROW + 2), slice(0, LANES)))

    out_shape = []
    for w, _, _ in params:
        out_shape += [jax.ShapeDtypeStruct(w.shape, F32)] * 4
    out_shape += [jax.ShapeDtypeStruct((1, LANES), F32), jax.ShapeDtypeStruct((L, 8, CW), F32),
                  jax.ShapeDtypeStruct((L, T, CW), F32)]
    outs = pl.pallas_call(body, name=name, out_shape=out_shape, compiler_params=_cparams())(
        vec_all, w256_all, *sw_all, *[a for p in params for a in p])
    return [outs[4 * k:4 * k + 4] for k in range(n_par)], outs[4 * n_par:]


def adamw_plain(w, g, m, v, tr, name):
    rows, cols = w.shape
    spec = pl.BlockSpec((tr, cols), lambda i: (i, 0))

    def body(w_ref, g_ref, m_ref, v_ref, d_ref, nm_ref, nv_ref):
        delta, nm, nv = _adamw(w_ref[...], g_ref[...], m_ref[...], v_ref[...])
        d_ref[...] = delta
        nm_ref[...] = nm
        nv_ref[...] = nv

    shp = jax.ShapeDtypeStruct((rows, cols), F32)
    return pl.pallas_call(
        body, name=name, grid=(rows // tr,), in_specs=[spec] * 4, out_specs=[spec] * 3,
        out_shape=[shp, shp, shp], compiler_params=_cparams(("parallel",)),
    )(w, g, m, v)


def adamw_reduce(w, parts, m, v, tr, name, tie=None):
    _, rows, cols = w.shape
    spec = pl.BlockSpec((None, tr, cols), lambda l, i: (l, i, 0))
    pspecs = [pl.BlockSpec((NDEV, tr, cols), lambda l, i, k=k: (0, jnp.where(l == k, i, 0), 0)) for k in range(L)]

    ties = [] if tie is None else [tie]

    def body(w_ref, p0_ref, p1_ref, m_ref, v_ref, *rest):
        g_ref, d_ref, nm_ref, nv_ref = rest[len(ties):]
        first_layer = pl.program_id(0) == 0
        g = jnp.zeros((tr, cols), F32)
        for d in range(NDEV):
            g = g + jnp.where(first_layer, p0_ref[d], p1_ref[d]).astype(F32)
        delta, nm, nv = _adamw(w_ref[...], g, m_ref[...], v_ref[...])
        g_ref[...] = g
        d_ref[...] = delta
        nm_ref[...] = nm
        nv_ref[...] = nv

    shp = jax.ShapeDtypeStruct(w.shape, F32)
    return pl.pallas_call(
        body, name=name, grid=(L, rows // tr),
        in_specs=[spec] + pspecs + [spec, spec] + [pl.BlockSpec(t.shape, lambda l, i: (0, 0)) for t in ties],
        out_specs=[spec] * 4, out_shape=[shp] * 4, compiler_params=_cparams(("parallel", "parallel")),
    )(w, *parts, m, v, *ties)


def _pad_rows(flat, rows):
    return jnp.pad(flat, (0, rows * LANES - flat.shape[0])).reshape(rows, LANES)


def kernel(x, c, ada_w, ada_b, norm_mix_g, norm_mlp_g, w_in, conv_w, conv_b, gmlp_norm_g, spatial_w, spatial_b, w_out, mlp_w1, mlp_w2, final_norm_g, loss_target, m_ada_w, m_ada_b, m_norm_mix_g, m_norm_mlp_g, m_w_in, m_conv_w, m_conv_b, m_gmlp_norm_g, m_spatial_w, m_spatial_b, m_w_out, m_mlp_w1, m_mlp_w2, m_final_norm_g, v_ada_w, v_ada_b, v_norm_mix_g, v_norm_mlp_g, v_w_in, v_conv_w, v_conv_b, v_gmlp_norm_g, v_spatial_w, v_spatial_b, v_w_out, v_mlp_w1, v_mlp_w2, v_final_norm_g):
    me = _lin(_my_pos())
    x0 = x[0]
    target = loss_target[0]
    conv_shard = conv_w.shape[-1]

    w_in_b, w_out_b, w1_b, w2_b = [w.astype(BF16) for w in (w_in, w_out, mlp_w1, mlp_w2)]
    pack0 = _pad_rows(jnp.concatenate([c.reshape(-1), conv_w.reshape(-1)]), 16)
    g0, gw_in0 = run_comm(Gather([pack0, w_in_b[0]]), "gather_first")
    g0 = g0.reshape(NDEV, 16 * LANES)
    c_all = g0[:, :D]
    conv_full = (g0[:, D:D + L * 3 * conv_shard].reshape(NDEV, L, 3, conv_shard)
                 .transpose(1, 2, 0, 3).reshape(L, 3, CW))

    def canonical_w_in(gathered):
        return gathered.transpose(1, 0, 2).reshape(D, PROJ)

    weight_plans = [Gather([w_out_b[l], w1_b[l], w2_b[l]]) for l in range(L)]
    W_in = [canonical_w_in(gw_in0), None]
    W_out, W1, W2 = [None] * L, [None] * L, [None] * L

    ada_b_mine = lax.dynamic_slice(ada_b, (0, me * ADA_COLS), (L, ADA_COLS)).reshape(L, 1, ADA_COLS)
    mod_part, c_act = ada_fwd(c_all, ada_w, ada_b_mine, "ada_fwd")
    gmod = run_comm(Gather([mod_part]), "gather_mod")[0]
    mod = lax.dynamic_index_in_dim(gmod, me, axis=2, keepdims=False)
    mod = mod.transpose(1, 0, 2).reshape(L, NMOD, 1, D)

    cw8 = jnp.pad(conv_full, ((0, 0), (0, 5), (0, 0)))
    sg_bias = jnp.repeat(spatial_b.transpose(0, 2, 1), HD, axis=2)

    saved = []
    xl = x0
    for l in range(L):
        sh_m, sc_m, g_m, sh_f, sc_f, g_f = [mod[l, k] for k in range(NMOD)]
        h1 = normmod_fwd(xl, norm_mix_g[l:l + 1], sc_m, sh_m, f"norm_mix_fwd{l}")
        if l > 0:
            W_in[l] = canonical_w_in(finish_copies(w_in_handle, xl, f"gather_w_in{l}_wait"))
        qkv = mm_layer("proj_qkv", l, h1, W_in[l], out_dtypes=[BF16], cols=(0, QKV))[0]
        proj = mm_layer("proj_rest", l, h1, W_in[l], out_dtypes=[F32], cols=(QKV, REST))[0]
        a_out, a_tot, *gathered = attn_fwd(qkv, f"attn_fwd{l}", comm=weight_plans[l])
        W_out[l] = gathered[0].reshape(D, D)
        W1[l] = gathered[1]
        W2[l] = gathered[2].reshape(DFF, D)
        if l + 1 < L:
            w_in_handle, token = start_copies(w_in_b[l + 1], me, f"gather_w_in{l + 1}_start", True, after=a_out)
            g_m = tied(g_m, token)
        c_out = conv_fwd(proj, cw8[l], conv_b[l:l + 1], f"conv_fwd{l}")
        s_out = sg_fwd(proj, gmlp_norm_g[l:l + 1], spatial_w[l], sg_bias[l], f"sg_fwd{l}")
        cat = jnp.concatenate([a_out, c_out.astype(BF16), s_out.astype(BF16)], axis=1)
        mix, x1 = mm_layer("mix", l, cat, W_out[l], out_dtypes=[F32, F32],
                           epilogue=lambda acc, xr, g: (acc, xr + g * acc), extras=[(xl, "tile"), (g_m, "col")])
        h2 = normmod_fwd(x1, norm_mlp_g[l:l + 1], sc_f, sh_f, f"norm_mlp_fwd{l}")
        ra, r = mm_layer("mlp_up", l, h2, W1[l], out_dtypes=[BF16, BF16], b_blocks=True,
                         epilogue=lambda acc: (jnp.maximum(acc, 0.0), jnp.square(jnp.maximum(acc, 0.0))))
        m2, x2 = mm_layer("mlp_down", l, r, W2[l], out_dtypes=[F32, F32],
                          epilogue=lambda acc, xr, g: (acc, xr + g * acc), extras=[(x1, "tile"), (g_f, "col")])
        saved.append(dict(x=xl, h1=h1, proj=proj, qkv=qkv, a_tot=a_tot, cat=cat, mix=mix,
                          x1=x1, h2=h2, ra=ra, r=r, m2=m2))
        xl = x2

    dx, loss_part, d_final_g = loss_head(xl, target, final_norm_g.reshape(1, D), "loss_head")

    p_in, p_out, p_w1, p_w2 = [None] * L, [None] * L, [None] * L, [None] * L
    w_in_grads = [None] * L
    vec_rows, d_norm_mix, d_norm_mlp = [None] * L, [None] * L, [None] * L
    dcw8, d_conv_b, d_gn, d_sw, d_bias = [None] * L, [None] * L, [None] * L, [None] * L, [None] * L
    for l in reversed(range(L)):
        sv = saved[l]
        sh_m, sc_m, g_m, sh_f, sc_f, g_f = [mod[l, k] for k in range(NMOD)]
        dm2, dg_f = gate_bwd(dx, sv["m2"], g_f, f"gate_mlp_bwd{l}")
        da = mm_layer("mlp_down_dgrad", l, dm2, W2[l], out_dtypes=[BF16], trans_b=True,
                      epilogue=lambda acc, rav: (acc * (2.0 * rav.astype(F32)),), extras=[(sv["ra"], "tile")])[0]
        dW2 = mm_layer("mlp_down_wgrad", l, sv["r"], dm2, out_dtypes=[BF16], trans_a=True)[0]
        dW1 = mm_layer("mlp_up_wgrad", l, sv["h2"], da, out_dtypes=[BF16], trans_a=True, out_blocks=True)[0]
        dh2 = mm_layer("mlp_up_dgrad", l, da, W1[l], out_dtypes=[F32], trans_b=True, b_blocks=True)[0]
        dx1, dsc_f, dsh_f, d_norm_mlp[l] = normmod_bwd(sv["x1"], dh2, dx, norm_mlp_g[l:l + 1], sc_f,
                                                       f"norm_mlp_bwd{l}")
        dmix, dg_m = gate_bwd(dx1, sv["mix"], g_m, f"gate_mix_bwd{l}")
        dcat = mm_layer("mix_dgrad", l, dmix, W_out[l], out_dtypes=[F32], trans_b=True)[0]
        dW_out = mm_layer("mix_wgrad", l, sv["cat"], dmix, out_dtypes=[BF16], trans_a=True)[0]
        ready = [dW2.reshape(NDEV, DFF // NDEV, D), dW1, dW_out.reshape(NDEV, D // NDEV, D)]
        dq, dk, dv, p_w2[l], p_w1[l], p_out[l] = attn_bwd(sv["qkv"], dcat, sv["a_tot"], f"attn_bwd{l}",
                                                          comm=Exchange(ready))
        dbg, dcg, dhc, dcw8[l], d_conv_b[l] = conv_bwd(sv["proj"], dcat, cw8[l], conv_b[l:l + 1], f"conv_bwd{l}")
        dus, dvs, d_gn[l], d_sw[l], d_bias[l] = sg_bwd(sv["proj"], dcat, gmlp_norm_g[l:l + 1],
                                                       spatial_w[l], sg_bias[l], f"sg_bwd{l}")
        dproj = jnp.concatenate([dq, dk, dv, dbg, dcg, dhc, dus, dvs], axis=1).astype(BF16)
        dW_in = mm_layer("proj_wgrad", l, sv["h1"], dproj, out_dtypes=[BF16], trans_a=True)[0]
        pieces = dW_in.reshape(D, NDEV, PROJ // NDEV).transpose(1, 0, 2)
        w_in_grads[l], token = start_copies(pieces, me, f"exchange_w_in{l}_start", False)
        dh1 = mm_layer("proj_dgrad", l, dproj, W_in[l], out_dtypes=[F32], trans_b=True, extras=[(token, "tie")])[0]
        dx, dsc_m, dsh_m, d_norm_mix[l] = normmod_bwd(sv["x"], dh1, dx1, tied(norm_mix_g[l:l + 1], token), sc_m,
                                                      f"norm_mix_bwd{l}")
        vec_rows[l] = [dsh_m, dsc_m, dg_m, dsh_f, dsc_f, dg_f, d_norm_mix[l], d_norm_mlp[l]]

    grad_x = dx.reshape(1, S, D)

    g_w_out, d_w_out, nm_w_out, nv_w_out = adamw_reduce(w_out, p_out, m_w_out, v_w_out, 128, "adamw_w_out", tie=token)
    g_w1, d_w1, nm_w1, nv_w1 = adamw_reduce(mlp_w1, p_w1, m_mlp_w1, v_mlp_w1, 256, "adamw_mlp_w1", tie=token)
    g_w2, d_w2, nm_w2, nv_w2 = adamw_reduce(mlp_w2, p_w2, m_mlp_w2, v_mlp_w2, 256, "adamw_mlp_w2", tie=token)

    vec_pack = jnp.concatenate([row for l in range(L) for row in vec_rows[l]]
                               + [d_final_g, loss_part, jnp.zeros((VEC_ROWS - VEC_FINAL_ROW - 2, D), F32)], axis=0)
    vec_pack, _ = lax.optimization_barrier((vec_pack, (d_w_out, d_w1, d_w2)))
    w256_pack = jnp.concatenate([blk for l in range(L) for blk in (
        dcw8[l], d_conv_b[l], d_gn[l], jnp.zeros((W256_BIAS - W256_GN - 1, CW), F32), d_bias[l])], axis=0)
    vec_all, w256_all, *sw_all = run_comm(Gather([vec_pack, w256_pack] + d_sw), "gather_small_grads")

    dmod_all = (vec_all[:, :VEC_FINAL_ROW].reshape(NDEV, L, VEC_ROWS_PER_LAYER, D)[:, :, :NMOD]
                .reshape(NDEV, L, NMOD * D))
    dmod_cols = lax.dynamic_slice(dmod_all, (0, 0, me * ADA_COLS), (NDEV, L, ADA_COLS)).transpose(1, 0, 2)
    g_ada_w = ada_bwd(c_act, dmod_cols, "ada_bwd")

    flat2 = lambda t: t.reshape(L * D, ADA_COLS)
    d_ada_w, nm_ada_w, nv_ada_w = [t.reshape(L, D, ADA_COLS) for t in adamw_plain(
        flat2(ada_w), flat2(g_ada_w), flat2(m_ada_w), flat2(v_ada_w), 256, "adamw_ada_w")]

    after = jnp.concatenate([t.reshape(-1)[:1] for t in (d_w_out, d_w1, d_w2, d_ada_w)])
    p_in = [finish_copies(w_in_grads[l], after, f"exchange_w_in{l}_wait") for l in range(L)]
    g_w_in, d_w_in, nm_w_in, nv_w_in = adamw_reduce(w_in, p_in, m_w_in, v_w_in, 256, "adamw_w_in")

    as_row = lambda t: t.reshape(1, D)
    small_params = [(ada_b, m_ada_b, v_ada_b), (norm_mix_g, m_norm_mix_g, v_norm_mix_g),
                    (norm_mlp_g, m_norm_mlp_g, v_norm_mlp_g),
                    (as_row(final_norm_g), as_row(m_final_norm_g), as_row(v_final_norm_g)),
                    (conv_b, m_conv_b, v_conv_b), (gmlp_norm_g, m_gmlp_norm_g, v_gmlp_norm_g),
                    (spatial_w, m_spatial_w, v_spatial_w)]
    updated, (loss_sum, taps_sum, bias_sum) = small_update(vec_all, w256_all, sw_all, small_params, "small_update")
    loss = loss_sum[0, 0]
    u_ada_b, u_norm_mix, u_norm_mlp, u_final, u_conv_b, u_gn, u_sw = updated
    u_final = [t.reshape(D) for t in u_final]
    g_conv_w = lax.dynamic_slice(taps_sum, (0, 0, me * conv_shard), (L, 3, conv_shard))
    g_sb = bias_sum.reshape(L, T, SG_HEADS, HD).sum(axis=3).transpose(0, 2, 1)
    flat_cw = lambda t: t.reshape(L * 3, conv_shard)
    u_conv_w = [g_conv_w] + [t.reshape(L, 3, conv_shard) for t in adamw_plain(
        flat_cw(conv_w), flat_cw(g_conv_w), flat_cw(m_conv_w), flat_cw(v_conv_w), L * 3, "adamw_conv_w")]
    flat_sb = lambda t: t.reshape(L * SG_HEADS, T)
    u_sb = [g_sb] + [t.reshape(L, SG_HEADS, T) for t in adamw_plain(
        flat_sb(spatial_b), flat_sb(g_sb), flat_sb(m_spatial_b), flat_sb(v_spatial_b), L * SG_HEADS, "adamw_spatial_b")]
    small_sets = [u_ada_b, u_norm_mix, u_norm_mlp, u_conv_w, u_conv_b, u_gn, u_sw, u_sb, u_final]
    small_g, sd, snm, snv = [[u[k] for u in small_sets] for k in range(4)]

    def ordered(big, small):
        ada, win, wout, w1, w2 = big
        return [ada, small[0], small[1], small[2], win, small[3], small[4], small[5], small[6], small[7],
                wout, w1, w2, small[8]]

    grads = ordered([g_ada_w, g_w_in, g_w_out, g_w1, g_w2], small_g)
    deltas = ordered([d_ada_w, d_w_in, d_w_out, d_w1, d_w2], sd)
    new_m = ordered([nm_ada_w, nm_w_in, nm_w_out, nm_w1, nm_w2], snm)
    new_v = ordered([nv_ada_w, nv_w_in, nv_w_out, nv_w1, nv_w2], snv)
    return (loss, grad_x, *grads, *deltas, *new_m, *new_v)
```

```python
import functools
import math

import jax
import jax.numpy as jnp
from jax import lax
from jax.experimental import pallas as pl
from jax.experimental.pallas import tpu as pltpu

F32 = jnp.float32
BF16 = jnp.bfloat16
MESH = pl.DeviceIdType.MESH

S = 2048
D = 1024
L = 2
NDEV = 8
HD = 64
NH = 8
PROJ = 2816
DFF = 4096
NMOD = 6
EPS = 1e-6
T = 128
SG_HEADS = 4
LANES = 128
CW = 256
QKV = 3 * NH * HD
REST = PROJ - QKV

LR, B1, B2, AEPS, WD, STEP = 0.001, 0.9, 0.999, 1e-08, 0.01, 10
BC1 = 1.0 - B1 ** STEP
BC2 = 1.0 - B2 ** STEP

VMEM_LIMIT = 48 * 1024 * 1024

HBM_SPEC = pl.BlockSpec(memory_space=pltpu.HBM)


def _cparams(sem=None):
    return pltpu.CompilerParams(dimension_semantics=sem, vmem_limit_bytes=VMEM_LIMIT)


def _my_pos():
    return lax.axis_index("x"), lax.axis_index("y"), lax.axis_index("c")


def _lin(p):
    return 4 * p[0] + 2 * p[1] + p[2]


class Gather:
    def __init__(self, arrs):
        self.arrs = list(arrs)
        n = len(self.arrs)
        self.out_shape = [jax.ShapeDtypeStruct((NDEV,) + a.shape, a.dtype) for a in self.arrs]
        self.scratch = [pltpu.SemaphoreType.DMA((n, 7)), pltpu.SemaphoreType.DMA((n, 7)),
                        pltpu.SemaphoreType.DMA((n,))]

    def phases(self, ins, outs, sems):
        n = len(self.arrs)
        send_sems, recv_sems, local_sems = sems
        x, y, c = _my_pos()
        me, sibling = (x, y, c), (x, y, 1 - c)
        chips = [(1 - x, y), (x, 1 - y), (1 - x, 1 - y)]

        def copy(a, k, block, to, src=None):
            slot = outs[a].at[_lin(block)]
            return pltpu.make_async_remote_copy(
                src_ref=slot if src is None else src, dst_ref=slot,
                send_sem=send_sems.at[a, k], recv_sem=recv_sems.at[a, k],
                device_id=to, device_id_type=MESH)

        def mine(a):
            return pltpu.make_async_copy(ins[a], outs[a].at[_lin(me)], local_sems.at[a])

        def first(a):
            return [copy(a, 0, me, sibling, src=ins[a])] + [
                copy(a, 1 + j, me, (*chip, c), src=ins[a]) for j, chip in enumerate(chips)]

        def passed(a):
            return [copy(a, 4 + j, (*chip, c), sibling) for j, chip in enumerate(chips)]

        def start():
            for a in range(n):
                mine(a).start()
                for cp in first(a):
                    cp.start()

        def relay():
            for j, chip in enumerate(chips):
                for a in range(n):
                    copy(a, 1 + j, (*chip, c), me).wait_recv()
                    passed(a)[j].start()

        def finish():
            for a in range(n):
                copy(a, 0, sibling, me).wait_recv()
            for j, chip in enumerate(chips):
                for a in range(n):
                    copy(a, 4 + j, (*chip, 1 - c), me).wait_recv()
            for a in range(n):
                for cp in first(a) + passed(a):
                    cp.wait_send()
                mine(a).wait()

        return start, relay, finish


class Exchange:
    def __init__(self, arrs):
        self.arrs = list(arrs)
        n = len(self.arrs)
        self.out_shape = [jax.ShapeDtypeStruct(a.shape, a.dtype) for a in self.arrs]
        self.scratch = [pltpu.SemaphoreType.DMA((n, 7)), pltpu.SemaphoreType.DMA((n, 7)),
                        pltpu.SemaphoreType.DMA((n,))]

    def phases(self, ins, outs, sems):
        n = len(self.arrs)
        send_sems, recv_sems, local_sems = sems
        x, y, c = _my_pos()
        me = (x, y, c)

        def peer(mask):
            return (1 - x if mask & 4 else x, 1 - y if mask & 2 else y, 1 - c if mask & 1 else c)

        def copy(a, mask):
            return pltpu.make_async_remote_copy(
                src_ref=ins[a].at[_lin(peer(mask))], dst_ref=outs[a].at[_lin(me)],
                send_sem=send_sems.at[a, mask - 1], recv_sem=recv_sems.at[a, mask - 1],
                device_id=peer(mask), device_id_type=MESH)

        def arrival(a, mask):
            return pltpu.make_async_remote_copy(
                src_ref=ins[a].at[_lin(me)], dst_ref=outs[a].at[_lin(peer(mask))],
                send_sem=send_sems.at[a, mask - 1], recv_sem=recv_sems.at[a, mask - 1],
                device_id=peer(mask), device_id_type=MESH)

        def mine(a):
            return pltpu.make_async_copy(ins[a].at[_lin(me)], outs[a].at[_lin(me)], local_sems.at[a])

        def start():
            for a in range(n):
                mine(a).start()
            for mask in (4, 2, 6, 1, 5, 3, 7):
                for a in range(n):
                    copy(a, mask).start()

        def relay():
            pass

        def finish():
            for mask in range(1, 8):
                for a in range(n):
                    arrival(a, mask).wait_recv()
            for mask in range(1, 8):
                for a in range(n):
                    copy(a, mask).wait_send()
            for a in range(n):
                mine(a).wait()

        return start, relay, finish


def run_comm(plan, name):
    n = len(plan.arrs)

    def body(*refs):
        start, relay, finish = plan.phases(refs[:n], refs[n:2 * n], refs[2 * n:])
        start()
        relay()
        finish()

    outs = pl.pallas_call(
        body, name=name, out_shape=plan.out_shape,
        in_specs=[HBM_SPEC] * n, out_specs=[HBM_SPEC] * n, scratch_shapes=plan.scratch,
    )(*plan.arrs)
    return list(outs)


SEM_SPEC = pl.BlockSpec(memory_space=pltpu.SEMAPHORE)
DATAFLOW = pltpu.SideEffectType.DATAFLOW_SIDE_EFFECTING


def _peer_copies(src_ref, land_ref, send_sems, recv_sems, same_block):
    x, y, c = _my_pos()
    me = (x, y, c)
    sends, arrivals = [], []
    for mask in (4, 2, 6, 1, 5, 3, 7):
        peer = (1 - x if mask & 4 else x, 1 - y if mask & 2 else y, 1 - c if mask & 1 else c)
        sends.append(pltpu.make_async_remote_copy(
            src_ref=src_ref if same_block else src_ref.at[_lin(peer)], dst_ref=land_ref.at[_lin(me)],
            send_sem=send_sems.at[mask - 1], recv_sem=recv_sems.at[mask - 1], device_id=peer, device_id_type=MESH))
        arrivals.append(pltpu.make_async_remote_copy(
            src_ref=src_ref if same_block else src_ref.at[_lin(me)], dst_ref=land_ref.at[_lin(peer)],
            send_sem=send_sems.at[mask - 1], recv_sem=recv_sems.at[mask - 1], device_id=peer, device_id_type=MESH))
    return sends, arrivals


def start_copies(src, me, name, same_block, after=None):
    own = src[None] if same_block else lax.dynamic_index_in_dim(src, me, axis=0, keepdims=True)
    landing = lax.dynamic_update_slice(lax.empty((NDEV,) + own.shape[1:], src.dtype), own, (me,) + (0,) * (own.ndim - 1))

    def body(src_ref, land_ref, *rest):
        send_sems, recv_sems, _, _, token = rest[-5:]
        sends, _ = _peer_copies(src_ref, land_ref, send_sems, recv_sems, same_block)
        for cp in sends:
            cp.start()
        token[...] = jnp.zeros_like(token)

    hbm = lambda a: pltpu.HBM(a.shape, a.dtype)
    extra = [] if after is None else [after]
    *handle, token = pl.pallas_call(
        body, name=name,
        out_shape=(pltpu.SemaphoreType.DMA((7,)), pltpu.SemaphoreType.DMA((7,)), hbm(src), hbm(landing),
                   jax.ShapeDtypeStruct((8, LANES), F32)),
        in_specs=[HBM_SPEC, HBM_SPEC] + [pl.BlockSpec(memory_space=pl.ANY)] * len(extra),
        out_specs=(SEM_SPEC, SEM_SPEC, HBM_SPEC, HBM_SPEC, pl.BlockSpec(memory_space=pltpu.VMEM)),
        input_output_aliases={0: 2, 1: 3},
        compiler_params=pltpu.CompilerParams(has_side_effects=DATAFLOW),
    )(pltpu.with_memory_space_constraint(src, pltpu.HBM), pltpu.with_memory_space_constraint(landing, pltpu.HBM),
      *extra)
    return (handle, same_block), token


def finish_copies(handle, after, name):
    (send_sems, recv_sems, src, landing), same_block = handle

    def body(src_ref, land_ref, send_sems, recv_sems, after_ref, src_dead, got_ref):
        sends, arrivals = _peer_copies(src_ref, land_ref, send_sems, recv_sems, same_block)
        for cp in sends:
            cp.wait_send()
        for cp in arrivals:
            cp.wait_recv()

    hbm = lambda a: pltpu.HBM(a.shape, a.dtype)
    return pl.pallas_call(
        body, name=name, out_shape=(hbm(src), hbm(landing)),
        in_specs=(HBM_SPEC, HBM_SPEC, SEM_SPEC, SEM_SPEC, pl.BlockSpec(memory_space=pl.ANY)),
        out_specs=(HBM_SPEC, HBM_SPEC), input_output_aliases={0: 0, 1: 1},
        compiler_params=pltpu.CompilerParams(has_side_effects=DATAFLOW),
    )(src, landing, send_sems, recv_sems, after)[1]


def tied(x, token):
    return x + token[0:1, 0:1].astype(x.dtype)


MM_TILES = {
    "proj_qkv": (S, 512), "proj_rest": (S, 256), "mix": (1024, 512), "mlp_up": (S, 512), "mlp_down": (1024, 256),
    "mlp_down_dgrad": (1024, 1024), "mlp_down_wgrad": (1024, 1024), "mlp_up_wgrad": (1024, 512),
    "mlp_up_dgrad": (1024, 512), "mix_dgrad": (1024, 512), "mix_wgrad": (512, 1024),
    "proj_wgrad": (1024, PROJ // 2), "proj_dgrad": (1024, 512),
}


def mm_layer(kind, l, a, b, **kw):
    tm, tn = MM_TILES[kind]
    return mm(a, b, tm=tm, tn=tn, name=f"{kind}{l}", **kw)


def mm(a, b, *, tm, tn, out_dtypes, epilogue=None, extras=(), name, trans_a=False, trans_b=False,
       cols=None, b_blocks=False, out_blocks=False):
    if trans_a:
        kdim, m = a.shape
    else:
        m, kdim = a.shape
    shard = b.shape[-1] if b_blocks else None
    if b_blocks:
        full = (b.shape[1], NDEV * shard)
    else:
        full = b.shape
    first, ncols = cols if cols is not None else (0, full[0] if trans_b else full[1])
    assert full[1 if trans_b else 0] == kdim and m % tm == 0 and ncols % tn == 0 and first % tn == 0
    j0 = first // tn
    if trans_a:
        a_spec = pl.BlockSpec((kdim, tm), lambda i, j: (0, i))
    else:
        a_spec = pl.BlockSpec((tm, kdim), lambda i, j: (i, 0))
    if b_blocks and trans_b:
        b_spec = pl.BlockSpec((NDEV, tn, shard), lambda i, j: (0, j0 + j, 0))
    elif b_blocks:
        assert tn == shard
        b_spec = pl.BlockSpec((None, kdim, tn), lambda i, j: (j0 + j, 0, 0))
    elif trans_b:
        b_spec = pl.BlockSpec((tn, kdim), lambda i, j: (j0 + j, 0))
    else:
        b_spec = pl.BlockSpec((kdim, tn), lambda i, j: (0, j0 + j))
    if out_blocks:
        assert tn * NDEV == ncols
        out_spec = pl.BlockSpec((None, tm, tn), lambda i, j: (j, i, 0))
        out_dims = (NDEV, m, tn)
    else:
        out_spec = pl.BlockSpec((tm, tn), lambda i, j: (i, j))
        out_dims = (m, ncols)
    ex_specs = []
    for arr, kind in extras:
        if kind == "tile":
            ex_specs.append(pl.BlockSpec((tm, tn), lambda i, j: (i, j)))
        elif kind == "col":
            ex_specs.append(pl.BlockSpec((1, tn), lambda i, j: (0, j)))
        else:
            ex_specs.append(pl.BlockSpec(arr.shape, lambda i, j: (0, 0)))
    n_ex, n_out = len(extras), len(out_dtypes)
    used = [k for k, (_, kind) in enumerate(extras) if kind != "tie"]

    def body(a_ref, b_ref, *rest):
        ex_refs, out_refs = rest[:n_ex], rest[n_ex:]
        if trans_a:
            acc = lax.dot_general(a_ref[...], b_ref[...], (((0,), (0,)), ((), ())),
                                  preferred_element_type=F32)
        elif trans_b and b_blocks:
            acc = jnp.zeros((tm, tn), F32)
            for d in range(NDEV):
                acc = acc + lax.dot_general(a_ref[:, d * shard:(d + 1) * shard], b_ref[d],
                                            (((1,), (1,)), ((), ())), preferred_element_type=F32)
        elif trans_b:
            acc = lax.dot_general(a_ref[...], b_ref[...], (((1,), (1,)), ((), ())),
                                  preferred_element_type=F32)
        else:
            acc = jnp.dot(a_ref[...], b_ref[...], preferred_element_type=F32)
        outs = (acc,) if epilogue is None else epilogue(acc, *[ex_refs[k][...] for k in used])
        for o_ref, val in zip(out_refs, outs):
            o_ref[...] = val.astype(o_ref.dtype)

    outs = pl.pallas_call(
        body, name=name, grid=(m // tm, ncols // tn),
        in_specs=[a_spec, b_spec] + ex_specs,
        out_specs=[out_spec for _ in range(n_out)],
        out_shape=[jax.ShapeDtypeStruct(out_dims, dt) for dt in out_dtypes],
        compiler_params=_cparams(("parallel", "parallel")),
    )(a, b, *[arr for arr, _ in extras])
    return list(outs)


TR = 256

ROW_SPEC = pl.BlockSpec((TR, D), lambda i: (i, 0))
VEC_SPEC = pl.BlockSpec((1, D), lambda i: (0, 0))


def normmod_fwd(x, g, sc, sh, name):
    def body(x_ref, g_ref, sc_ref, sh_ref, o_ref):
        xv = x_ref[...]
        rstd = lax.rsqrt(jnp.mean(xv * xv, axis=-1, keepdims=True) + EPS)
        n = (xv * rstd) * g_ref[...]
        o_ref[...] = (n * (1.0 + sc_ref[...]) + sh_ref[...]).astype(o_ref.dtype)

    return pl.pallas_call(
        body, name=name, grid=(S // TR,),
        in_specs=[ROW_SPEC, VEC_SPEC, VEC_SPEC, VEC_SPEC], out_specs=ROW_SPEC,
        out_shape=jax.ShapeDtypeStruct((S, D), BF16),
        compiler_params=_cparams(("parallel",)),
    )(x, g, sc, sh)


def normmod_bwd(x, dh, dres, g, sc, name):
    def body(x_ref, dh_ref, dres_ref, g_ref, sc_ref, dx_ref, dsc_ref, dsh_ref, dg_ref):
        @pl.when(pl.program_id(0) == 0)
        def _():
            dsc_ref[...] = jnp.zeros_like(dsc_ref)
            dsh_ref[...] = jnp.zeros_like(dsh_ref)
            dg_ref[...] = jnp.zeros_like(dg_ref)

        xv, dh = x_ref[...], dh_ref[...]
        gv = g_ref[...]
        rstd = lax.rsqrt(jnp.mean(xv * xv, axis=-1, keepdims=True) + EPS)
        xhat = xv * rstd
        dn = dh * (1.0 + sc_ref[...])
        dxhat = dn * gv
        dx_ref[...] = dres_ref[...] + rstd * (dxhat - xhat * jnp.mean(dxhat * xhat, axis=-1, keepdims=True))
        dsc_ref[...] += jnp.sum(dh * (xhat * gv), axis=0, keepdims=True)
        dsh_ref[...] += jnp.sum(dh, axis=0, keepdims=True)
        dg_ref[...] += jnp.sum(dn * xhat, axis=0, keepdims=True)

    vec_out = jax.ShapeDtypeStruct((1, D), F32)
    return pl.pallas_call(
        body, name=name, grid=(S // TR,),
        in_specs=[ROW_SPEC, ROW_SPEC, ROW_SPEC, VEC_SPEC, VEC_SPEC],
        out_specs=[ROW_SPEC, VEC_SPEC, VEC_SPEC, VEC_SPEC],
        out_shape=[jax.ShapeDtypeStruct((S, D), F32), vec_out, vec_out, vec_out],
        compiler_params=_cparams(("arbitrary",)),
    )(x, dh, dres, g, sc)


def gate_bwd(dx, branch, gate, name):
    def body(dx_ref, br_ref, gate_ref, o_ref, dgate_ref):
        @pl.when(pl.program_id(0) == 0)
        def _():
            dgate_ref[...] = jnp.zeros_like(dgate_ref)

        dxv = dx_ref[...]
        o_ref[...] = (dxv * gate_ref[...]).astype(o_ref.dtype)
        dgate_ref[...] += jnp.sum(dxv * br_ref[...], axis=0, keepdims=True)

    return pl.pallas_call(
        body, name=name, grid=(S // TR,),
        in_specs=[ROW_SPEC, ROW_SPEC, VEC_SPEC], out_specs=[ROW_SPEC, VEC_SPEC],
        out_shape=[jax.ShapeDtypeStruct((S, D), BF16), jax.ShapeDtypeStruct((1, D), F32)],
        compiler_params=_cparams(("arbitrary",)),
    )(dx, branch, gate)


def loss_head(x, target, g, name):
    def body(x_ref, t_ref, g_ref, dx_ref, loss_ref, dg_ref):
        @pl.when(pl.program_id(0) == 0)
        def _():
            loss_ref[...] = jnp.zeros_like(loss_ref)
            dg_ref[...] = jnp.zeros_like(dg_ref)

        xv, gv = x_ref[...], g_ref[...]
        rstd = lax.rsqrt(jnp.mean(xv * xv, axis=-1, keepdims=True) + EPS)
        xhat = xv * rstd
        err = xhat * gv - t_ref[...]
        loss_ref[...] += jnp.sum(err * err) * (0.5 / D)
        dy = err * (1.0 / D)
        dg_ref[...] += jnp.sum(dy * xhat, axis=0, keepdims=True)
        dxhat = dy * gv
        dx_ref[...] = rstd * (dxhat - xhat * jnp.mean(dxhat * xhat, axis=-1, keepdims=True))

    return pl.pallas_call(
        body, name=name, grid=(S // TR,),
        in_specs=[ROW_SPEC, ROW_SPEC, VEC_SPEC],
        out_specs=[ROW_SPEC, VEC_SPEC, VEC_SPEC],
        out_shape=[jax.ShapeDtypeStruct((S, D), F32), jax.ShapeDtypeStruct((1, D), F32),
                   jax.ShapeDtypeStruct((1, D), F32)],
        compiler_params=_cparams(("arbitrary",)),
    )(x, target, g)


TQ = 512
RS = 128
NSUB = TQ // RS
TK = 128


def _dot_hilo(a, tri_twice):
    hi = a.astype(BF16)
    lo = (a - hi.astype(F32)).astype(BF16)
    return jnp.dot(jnp.concatenate([hi, lo], axis=1), tri_twice, preferred_element_type=F32)


def _log_stay(z):
    return -(jnp.maximum(z, 0.0) + jnp.log(1.0 + jnp.exp(-jnp.abs(z))))


def _tri_and_ones(kind):
    row = jnp.bitwise_and(lax.broadcasted_iota(jnp.int32, (2 * TK, 2 * TK), 0), TK - 1)
    col = lax.broadcasted_iota(jnp.int32, (2 * TK, 2 * TK), 1)
    tri = {"after": row > col, "upto": row <= col, "before": row < col}[kind]
    return jnp.logical_or(col >= TK, tri).astype(BF16)


NPAIR = NH // 2
SCALE = HD ** -0.5


def _pair_specs(first_block):
    rows = pl.BlockSpec((TQ, LANES), lambda p, i: (i, first_block + p))
    whole = pl.BlockSpec((S, LANES), lambda p, i: (0, first_block + p))
    return rows, whole


Q_ROWS_SPEC, _ = _pair_specs(0)
_, K_ALL_SPEC = _pair_specs(NPAIR)
_, V_ALL_SPEC = _pair_specs(2 * NPAIR)
PAIR_ROWS_SPEC = pl.BlockSpec((TQ, LANES), lambda p, i: (i, p))
PAIR_ALL_SPEC = pl.BlockSpec((S, LANES), lambda p, i: (0, p))
PAIR_TOTAL_SPEC = pl.BlockSpec((2, TQ, TK), lambda p, i: (p, i, 0))


def _head_halves(x):
    first = lax.broadcasted_iota(jnp.int32, x.shape, 1) < HD
    zero = jnp.zeros_like(x)
    return jnp.where(first, x, zero), jnp.where(first, zero, x)


def _join_heads(a, b):
    return jnp.where(lax.broadcasted_iota(jnp.int32, a.shape, 1) < HD, a, b)


def _comm_hooks(comm, refs, n_in, n_out, n_scratch):
    nc = len(comm.arrs) if comm is not None else 0
    ins, cin = refs[:n_in], refs[n_in:n_in + nc]
    outs = refs[n_in + nc:n_in + nc + n_out]
    cout = refs[n_in + nc + n_out:n_in + 2 * nc + n_out]
    scratch = refs[n_in + 2 * nc + n_out:n_in + 2 * nc + n_out + n_scratch]
    sems = refs[n_in + 2 * nc + n_out + n_scratch:]
    phases = comm.phases(cin, cout, sems) if comm is not None else None
    return ins, outs, scratch, phases


def _with_comm(comm, in_specs, out_specs, out_shape, operands, scratch):
    if comm is None:
        return dict(in_specs=in_specs, out_specs=out_specs, out_shape=out_shape, scratch_shapes=scratch), operands
    nc = len(comm.arrs)
    return dict(in_specs=in_specs + [HBM_SPEC] * nc, out_specs=out_specs + [HBM_SPEC] * nc,
                out_shape=out_shape + comm.out_shape, scratch_shapes=scratch + comm.scratch), operands + comm.arrs


def attn_fwd(qkv, name, comm=None):
    n_steps = S // TQ

    def body(*refs):
        (q_ref, k_ref, v_ref), (o_ref, r_ref), (acc_ref, z_even, z_odd, w_ref), phases = _comm_hooks(
            comm, refs, 3, 2, 4)
        p = pl.program_id(0)
        i = pl.program_id(1)
        if phases is not None:
            pl.when(jnp.logical_and(p == 0, i == 0))(phases[0])
            pl.when(jnp.logical_and(p == NPAIR - 1, i == n_steps - 2))(phases[1])
        chains = [(sub, h) for sub in range(NSUB) for h in range(2)]
        q_sub = [_head_halves(q_ref[pl.ds(sub * RS, RS), :] * SCALE) for sub in range(NSUB)]
        after = _tri_and_ones("after")
        below_diagonal = (lax.broadcasted_iota(jnp.int32, (RS, TK), 1)
                          < lax.broadcasted_iota(jnp.int32, (RS, TK), 0))
        base = i * NSUB
        all_subs = list(range(NSUB))

        acc_ref[...] = jnp.zeros_like(acc_ref)
        r_ref[...] = jnp.zeros_like(r_ref)
        w_ref[...] = jnp.zeros_like(w_ref)

        def key_rows(block):
            return pl.ds(pl.multiple_of(block * TK, TK), TK)

        def store_scores(z_ref, block, subs):
            kb = k_ref[key_rows(block), :]
            for c, (sub, h) in enumerate(chains):
                if sub in subs:
                    z_ref[c] = lax.dot_general(q_sub[sub][h], kb, (((1,), (1,)), ((), ())),
                                               preferred_element_type=F32)

        def add_weighted_values(block, subs):
            vb = v_ref[key_rows(block), :]
            for sub in subs:
                acc_ref[pl.ds(sub * RS, RS), :] += _join_heads(*[
                    jnp.dot(w_ref[2 * sub + h], vb, preferred_element_type=F32) for h in range(2)])

        def step(block, z_ref, z_next_ref, subs, diagonal_sub, prev_subs, next_subs):
            if prev_subs:
                add_weighted_values(block + 1, prev_subs)
            if next_subs:
                store_scores(z_next_ref, jnp.maximum(block - 1, 0), next_subs)
            active = [(c, sub, h) for c, (sub, h) in enumerate(chains) if sub in subs]
            ls, sums = {}, {}
            for c, sub, h in active:
                ls[c] = _log_stay(z_ref[c])
                sums[c] = _dot_hilo(jnp.where(below_diagonal, ls[c], 0.0) if sub == diagonal_sub else ls[c], after)
            for c, sub, h in active:
                rows = pl.ds(sub * RS, RS)
                later = r_ref[h, rows, :]
                w = jnp.exp(z_ref[c] + ls[c] + (sums[c][:, :TK] + later))
                if sub == diagonal_sub:
                    w = jnp.where(below_diagonal, w, 0.0)
                w_ref[c] = w.astype(BF16)
                r_ref[h, rows, :] = later + sums[c][:, TK:]

        store_scores(z_even, base + NSUB - 1, [NSUB - 1])
        buffers = (z_even, z_odd)
        for j in reversed(range(NSUB)):
            subs = all_subs[j:]
            step(base + j, buffers[0], buffers[1], subs, j, all_subs[j + 1:], all_subs[j - 1:] if j else all_subs)
            buffers = buffers[::-1]
        assert buffers[0] is z_even

        @pl.loop(0, base // 2)
        def _(pair):
            block = base - 1 - 2 * pair
            step(block, z_even, z_odd, all_subs, None, all_subs, all_subs)
            step(block - 1, z_odd, z_even, all_subs, None, all_subs, all_subs)

        add_weighted_values(0, all_subs)
        o_ref[...] = acc_ref[...].astype(o_ref.dtype)
        if phases is not None:
            pl.when(jnp.logical_and(p == NPAIR - 1, i == n_steps - 1))(phases[2])

    kwargs, operands = _with_comm(
        comm, [Q_ROWS_SPEC, K_ALL_SPEC, V_ALL_SPEC], [PAIR_ROWS_SPEC, PAIR_TOTAL_SPEC],
        [jax.ShapeDtypeStruct((S, NH * HD), BF16), jax.ShapeDtypeStruct((NH, S, TK), F32)], [qkv, qkv, qkv],
        [pltpu.VMEM((TQ, LANES), F32), pltpu.VMEM((2 * NSUB, RS, TK), F32), pltpu.VMEM((2 * NSUB, RS, TK), F32),
         pltpu.VMEM((2 * NSUB, RS, TK), BF16)])
    return pl.pallas_call(
        body, name=name, grid=(NPAIR, n_steps),
        compiler_params=_cparams(("arbitrary", "arbitrary")), **kwargs,
    )(*operands)


def attn_bwd(qkv, dout, totals, name, comm=None):
    n_steps = S // TQ

    def body(*refs):
        ((q_ref, k_ref, v_ref, do_ref, r_ref), (dq_ref, dk_ref, dv_ref),
         (z_even, z_odd, dw_even, dw_odd, before_ref, dbefore_ref, dz_ref, w_ref), phases) = _comm_hooks(
            comm, refs, 5, 3, 8)
        p = pl.program_id(0)
        i = pl.program_id(1)
        if phases is not None:
            pl.when(jnp.logical_and(p == 0, i == 0))(phases[0])
            pl.when(jnp.logical_and(p == NPAIR - 1, i == n_steps - 2))(phases[1])

        @pl.when(i == 0)
        def _():
            dk_ref[...] = jnp.zeros_like(dk_ref)
            dv_ref[...] = jnp.zeros_like(dv_ref)

        chains = [(sub, h) for sub in range(NSUB) for h in range(2)]
        nch = len(chains)
        qb = q_ref[...]
        dob = do_ref[...].astype(BF16)
        q_sub = [_head_halves(qb[sub * RS:(sub + 1) * RS] * SCALE) for sub in range(NSUB)]
        do_sub = [_head_halves(dob[sub * RS:(sub + 1) * RS]) for sub in range(NSUB)]
        upto = _tri_and_ones("upto")
        before_tri = _tri_and_ones("before")
        below_diagonal = (lax.broadcasted_iota(jnp.int32, (RS, TK), 1)
                          < lax.broadcasted_iota(jnp.int32, (RS, TK), 0))
        contract_lanes = (((1,), (1,)), ((), ()))
        contract_rows = (((0,), (0,)), ((), ()))
        base = i * NSUB
        all_subs = list(range(NSUB))

        def key_rows(block):
            return pl.ds(pl.multiple_of(block * TK, TK), TK)

        def store_products(bufs, block, subs):
            z_ref, dw_ref = bufs
            kb = k_ref[key_rows(block), :]
            vb = v_ref[key_rows(block), :]
            for c, (sub, h) in enumerate(chains):
                if sub in subs:
                    z_ref[c] = lax.dot_general(q_sub[sub][h], kb, contract_lanes, preferred_element_type=F32)
                    dw_ref[c] = lax.dot_general(do_sub[sub][h], vb, contract_lanes, preferred_element_type=F32)

        def add_gradients(block, subs):
            kb = k_ref[key_rows(block), :]
            for sub in subs:
                rows = pl.ds(sub * RS, RS)
                dq_ref[rows, :] += _join_heads(*[jnp.dot(dz_ref[h, rows, :], kb, preferred_element_type=F32)
                                                 for h in range(2)])
            dk_ref[key_rows(block), :] += _join_heads(*[
                lax.dot_general(dz_ref[h], qb, contract_rows, preferred_element_type=F32) for h in range(2)])
            dv_ref[key_rows(block), :] += _join_heads(*[
                lax.dot_general(w_ref[h], dob, contract_rows, preferred_element_type=F32) for h in range(2)])

        for ref in (dq_ref, before_ref, dbefore_ref, dz_ref, w_ref):
            ref[...] = jnp.zeros_like(ref)
        even, odd = (z_even, dw_even), (z_odd, dw_odd)
        store_products(even, 0, all_subs)

        def step(block, bufs, next_bufs, subs, diagonal_sub, prev_subs, next_subs):
            z_ref, dw_ref = bufs
            add_gradients(jnp.maximum(block - 1, 0), prev_subs)
            for sub in prev_subs:
                if sub not in subs:
                    dz_ref[:, pl.ds(sub * RS, RS), :] = jnp.zeros((2, RS, TK), BF16)
                    w_ref[:, pl.ds(sub * RS, RS), :] = jnp.zeros((2, RS, TK), BF16)
            if next_subs:
                store_products(next_bufs, block + 1, next_subs)
            active = [(c, sub, h) for c, (sub, h) in enumerate(chains) if sub in subs]
            ls, sums, dl, dsums = {}, {}, {}, {}
            for c, sub, h in active:
                ls[c] = _log_stay(z_ref[c])
                sums[c] = _dot_hilo(jnp.where(below_diagonal, ls[c], 0.0) if sub == diagonal_sub else ls[c], upto)
            for c, sub, h in active:
                rows = pl.ds(sub * RS, RS)
                before = before_ref[c]
                log_after = r_ref[h, rows, :] - (sums[c][:, :TK] + before)
                w = jnp.exp((z_ref[c] + ls[c]) + log_after)
                if sub == diagonal_sub:
                    w = jnp.where(below_diagonal, w, 0.0)
                dl[c] = dw_ref[c] * w
                dsums[c] = _dot_hilo(dl[c], before_tri)
                w_ref[h, rows, :] = w.astype(BF16)
                before_ref[c] = before + sums[c][:, TK:]
            for c, sub, h in active:
                rows = pl.ds(sub * RS, RS)
                dbefore = dbefore_ref[c]
                beta = jnp.exp(z_ref[c] + ls[c])
                if sub == diagonal_sub:
                    beta = jnp.where(below_diagonal, beta, 0.0)
                dstay = dsums[c][:, :TK] + dbefore
                dz_ref[h, rows, :] = ((dl[c] * (1.0 - beta) - beta * dstay) * SCALE).astype(BF16)
                dbefore_ref[c] = dbefore + dsums[c][:, TK:]

        @pl.loop(0, base // 2)
        def _(pair):
            step(2 * pair, even, odd, all_subs, None, all_subs, all_subs)
            step(2 * pair + 1, odd, even, all_subs, None, all_subs, all_subs)

        bufs = (even, odd)
        for j in range(NSUB):
            step(base + j, bufs[0], bufs[1], all_subs[j:], j, all_subs[j - 1:] if j else all_subs, all_subs[j + 1:])
            bufs = bufs[::-1]

        add_gradients(base + NSUB - 1, all_subs[NSUB - 1:])
        if phases is not None:
            pl.when(jnp.logical_and(p == NPAIR - 1, i == n_steps - 1))(phases[2])

    full = jax.ShapeDtypeStruct((S, NH * HD), F32)
    kwargs, operands = _with_comm(
        comm, [Q_ROWS_SPEC, K_ALL_SPEC, V_ALL_SPEC, PAIR_ROWS_SPEC, PAIR_TOTAL_SPEC],
        [PAIR_ROWS_SPEC, PAIR_ALL_SPEC, PAIR_ALL_SPEC], [full, full, full], [qkv, qkv, qkv, dout, totals],
        [pltpu.VMEM((2 * NSUB, RS, TK), F32)] * 6 + [pltpu.VMEM((2, TQ, TK), BF16)] * 2)
    return pl.pallas_call(
        body, name=name, grid=(NPAIR, n_steps),
        compiler_params=_cparams(("arbitrary", "arbitrary")), **kwargs,
    )(*operands)


def _proj_cols(first_col):
    base = first_col // LANES
    return pl.BlockSpec((S, LANES), lambda j: (0, base + j))


CONV_OUT_SPEC = pl.BlockSpec((S, LANES), lambda j: (0, j))
CONV_DOUT_SPEC = pl.BlockSpec((S, LANES), lambda j: (0, (NH * HD) // LANES + j))
CONV_W_SPEC = pl.BlockSpec((8, LANES), lambda j: (0, j))
CONV_B_SPEC = pl.BlockSpec((1, LANES), lambda j: (0, j))


def _shift_down(u, n):
    rows = lax.broadcasted_iota(jnp.int32, u.shape, 0)
    return jnp.where(rows >= n, pltpu.roll(u, n, 0), 0.0)


def _shift_up(u, n):
    rows = lax.broadcasted_iota(jnp.int32, u.shape, 0)
    return jnp.where(rows < S - n, pltpu.roll(u, S - n, 0), 0.0)


def conv_fwd(proj, cw8, cb, name):
    def body(bg_ref, cg_ref, hc_ref, w_ref, b_ref, o_ref):
        u = cg_ref[...] * hc_ref[...]
        w = w_ref[...]
        y = w[0:1, :] * _shift_down(u, 2) + w[1:2, :] * _shift_down(u, 1) + w[2:3, :] * u + b_ref[...]
        o_ref[...] = bg_ref[...] * y

    return pl.pallas_call(
        body, name=name, grid=(CW // LANES,),
        in_specs=[_proj_cols(0), _proj_cols(CW), _proj_cols(2 * CW), CONV_W_SPEC, CONV_B_SPEC],
        out_specs=CONV_OUT_SPEC, out_shape=jax.ShapeDtypeStruct((S, CW), F32),
        compiler_params=_cparams(("parallel",)),
    )(proj, proj, proj, cw8, cb)


def conv_bwd(proj, dout, cw8, cb, name):
    def body(bg_ref, cg_ref, hc_ref, do_ref, w_ref, b_ref, dbg_ref, dcg_ref, dhc_ref, dw_ref, db_ref):
        cg, hc, do = cg_ref[...], hc_ref[...], do_ref[...]
        w = w_ref[...]
        u = cg * hc
        u1, u2 = _shift_down(u, 1), _shift_down(u, 2)
        y = w[0:1, :] * u2 + w[1:2, :] * u1 + w[2:3, :] * u + b_ref[...]
        dbg_ref[...] = do * y
        dy = do * bg_ref[...]
        db_ref[...] = jnp.sum(dy, axis=0, keepdims=True)
        dw_ref[...] = jnp.concatenate(
            [jnp.sum(dy * u2, axis=0, keepdims=True), jnp.sum(dy * u1, axis=0, keepdims=True),
             jnp.sum(dy * u, axis=0, keepdims=True), jnp.zeros((5, LANES), F32)], axis=0)
        du = w[2:3, :] * dy + w[1:2, :] * _shift_up(dy, 1) + w[0:1, :] * _shift_up(dy, 2)
        dcg_ref[...] = du * hc
        dhc_ref[...] = du * cg

    full = jax.ShapeDtypeStruct((S, CW), F32)
    return pl.pallas_call(
        body, name=name, grid=(CW // LANES,),
        in_specs=[_proj_cols(0), _proj_cols(CW), _proj_cols(2 * CW), CONV_DOUT_SPEC, CONV_W_SPEC, CONV_B_SPEC],
        out_specs=[CONV_OUT_SPEC, CONV_OUT_SPEC, CONV_OUT_SPEC, CONV_W_SPEC, CONV_B_SPEC],
        out_shape=[full, full, full, jax.ShapeDtypeStruct((8, CW), F32), jax.ShapeDtypeStruct((1, CW), F32)],
        compiler_params=_cparams(("parallel",)),
    )(proj, proj, proj, dout, cw8, cb)


GELU_K = math.sqrt(2.0 / math.pi)
GELU_C = 0.044715


def _gelu(x):
    return 0.5 * x * (1.0 + jnp.tanh(GELU_K * (x + GELU_C * (x * x * x))))


def _gelu_grad(x):
    t = jnp.tanh(GELU_K * (x + GELU_C * (x * x * x)))
    return 0.5 * (1.0 + t) + 0.5 * x * (1.0 - t * t) * (GELU_K * (1.0 + 3.0 * GELU_C * (x * x)))


def _sg_masks():
    row = lax.broadcasted_iota(jnp.int32, (T, T), 0)
    col = lax.broadcasted_iota(jnp.int32, (T, T), 1)
    causal = jnp.right_shift(row, 6) >= jnp.right_shift(col, 6)
    head_of_col = jnp.right_shift(lax.broadcasted_iota(jnp.int32, (T, CW), 1), 6)
    return causal, head_of_col


def _sg_mixed(vnb, sw_ref, bias, causal, head_of_col):
    mixed = bias
    for h in range(SG_HEADS):
        wh = jnp.where(causal, sw_ref[h], 0.0).astype(BF16)
        mh = jnp.dot(wh, vnb, preferred_element_type=F32)
        mixed = mixed + jnp.where(head_of_col == h, mh, 0.0)
    return mixed


SG_U_SPEC = pl.BlockSpec((T, CW), lambda n: (n, 3))
SG_V_SPEC = pl.BlockSpec((T, CW), lambda n: (n, 4))
SG_ROW_SPEC = pl.BlockSpec((T, CW), lambda n: (n, 0))
SG_DOUT_SPEC = pl.BlockSpec((T, CW), lambda n: (n, 3))
SG_G_SPEC = pl.BlockSpec((1, CW), lambda n: (0, 0))
SG_W_SPEC = pl.BlockSpec((SG_HEADS, T, T), lambda n: (0, 0, 0))
SG_BIAS_SPEC = pl.BlockSpec((T, CW), lambda n: (0, 0))


def sg_fwd(proj, gn, sw, bias, name):
    def body(u_ref, v_ref, g_ref, sw_ref, bias_ref, o_ref):
        causal, head_of_col = _sg_masks()
        gv = _gelu(v_ref[...])
        rstd = lax.rsqrt(jnp.mean(gv * gv, axis=-1, keepdims=True) + EPS)
        vnb = ((gv * rstd) * g_ref[...]).astype(BF16)
        mixed = _sg_mixed(vnb, sw_ref, bias_ref[...], causal, head_of_col)
        o_ref[...] = _gelu(u_ref[...]) * mixed

    return pl.pallas_call(
        body, name=name, grid=(S // T,),
        in_specs=[SG_U_SPEC, SG_V_SPEC, SG_G_SPEC, SG_W_SPEC, SG_BIAS_SPEC],
        out_specs=SG_ROW_SPEC, out_shape=jax.ShapeDtypeStruct((S, CW), F32),
        compiler_params=_cparams(("parallel",)),
    )(proj, proj, gn, sw, bias)


def sg_bwd(proj, dout, gn, sw, bias, name):
    def body(u_ref, v_ref, do_ref, g_ref, sw_ref, bias_ref, du_ref, dv_ref, dg_ref, dsw_ref, dbias_ref):
        @pl.when(pl.program_id(0) == 0)
        def _():
            dg_ref[...] = jnp.zeros_like(dg_ref)
            dsw_ref[...] = jnp.zeros_like(dsw_ref)
            dbias_ref[...] = jnp.zeros_like(dbias_ref)

        causal, head_of_col = _sg_masks()
        uv, vv, do, gnv = u_ref[...], v_ref[...], do_ref[...], g_ref[...]
        gv = _gelu(vv)
        rstd = lax.rsqrt(jnp.mean(gv * gv, axis=-1, keepdims=True) + EPS)
        xhat = gv * rstd
        vnb = (xhat * gnv).astype(BF16)
        mixed = _sg_mixed(vnb, sw_ref, bias_ref[...], causal, head_of_col)
        du_ref[...] = (do * mixed) * _gelu_grad(uv)
        dmix = do * _gelu(uv)
        dbias_ref[...] += dmix
        dmixb = dmix.astype(BF16)
        dvn = jnp.zeros((T, CW), F32)
        for h in range(SG_HEADS):
            wh = jnp.where(causal, sw_ref[h], 0.0).astype(BF16)
            dvh = lax.dot_general(wh, dmixb, (((0,), (0,)), ((), ())), preferred_element_type=F32)
            dvn = dvn + jnp.where(head_of_col == h, dvh, 0.0)
            dmh = jnp.where(head_of_col == h, dmixb, jnp.zeros_like(dmixb))
            dwh = lax.dot_general(dmh, vnb, (((1,), (1,)), ((), ())), preferred_element_type=F32)
            dsw_ref[h] += jnp.where(causal, dwh, 0.0)
        dg_ref[...] += jnp.sum(dvn * xhat, axis=0, keepdims=True)
        dxhat = dvn * gnv
        dgv = rstd * (dxhat - xhat * jnp.mean(dxhat * xhat, axis=-1, keepdims=True))
        dv_ref[...] = dgv * _gelu_grad(vv)

    full = jax.ShapeDtypeStruct((S, CW), F32)
    return pl.pallas_call(
        body, name=name, grid=(S // T,),
        in_specs=[SG_U_SPEC, SG_V_SPEC, SG_DOUT_SPEC, SG_G_SPEC, SG_W_SPEC, SG_BIAS_SPEC],
        out_specs=[SG_ROW_SPEC, SG_ROW_SPEC, SG_G_SPEC, SG_W_SPEC, SG_BIAS_SPEC],
        out_shape=[full, full, jax.ShapeDtypeStruct((1, CW), F32),
                   jax.ShapeDtypeStruct((SG_HEADS, T, T), F32), jax.ShapeDtypeStruct((T, CW), F32)],
        compiler_params=_cparams(("arbitrary",)),
    )(proj, proj, dout, gn, sw, bias)


ADA_COLS = NMOD * D // NDEV


def ada_fwd(c_all, ada_w, ada_b_mine, name):
    def body(c_ref, w_ref, b_ref, o_ref, ca_ref):
        cv = c_ref[...]
        ca = cv * (1.0 / (1.0 + jnp.exp(-cv)))
        ca_ref[...] = ca
        cab = ca.astype(BF16)
        for l in range(L):
            o_ref[l] = jnp.dot(cab, w_ref[l].astype(BF16), preferred_element_type=F32) + b_ref[l]

    return pl.pallas_call(
        body, name=name,
        out_shape=[jax.ShapeDtypeStruct((L, NDEV, ADA_COLS), F32), jax.ShapeDtypeStruct((NDEV, D), F32)],
        compiler_params=_cparams(),
    )(c_all, ada_w, ada_b_mine)


def ada_bwd(ca, dmod_cols, name):
    def body(ca_ref, dm_ref, o_ref):
        cab = ca_ref[...].astype(BF16)
        for l in range(L):
            o_ref[l] = lax.dot_general(cab, dm_ref[l].astype(BF16), (((0,), (0,)), ((), ())),
                                       preferred_element_type=F32)

    return pl.pallas_call(
        body, name=name, out_shape=jax.ShapeDtypeStruct((L, D, ADA_COLS), F32),
        compiler_params=_cparams(),
    )(ca, dmod_cols)


def _adamw(w, g, m, v):
    m = B1 * m + (1.0 - B1) * g
    v = B2 * v + (1.0 - B2) * (g * g)
    m_hat = m / BC1
    v_hat = v / BC2
    delta = -LR * (m_hat / (jnp.sqrt(v_hat) + AEPS) + WD * w)
    return delta, m, v


VEC_ROWS_PER_LAYER = 8
VEC_FINAL_ROW = L * VEC_ROWS_PER_LAYER
VEC_ROWS = VEC_FINAL_ROW + 8
W256_TAPS, W256_CONV_B, W256_GN = 0, 8, 9
W256_ROWS_PER_LAYER = 16


def small_update(vec_all, w256_all, sb_all, sw_all, params, name):
    n_par = len(params)

    def body(*refs):
        vec_ref, w256_ref, sb_ref = refs[:3]
        sw_refs = refs[3:3 + L]
        par_refs = [refs[3 + L + 3 * k:3 + L + 3 * k + 3] for k in range(n_par)]
        out = refs[3 + L + 3 * n_par:]
        out_par = [out[4 * k:4 * k + 4] for k in range(n_par)]
        loss_ref, taps_ref = out[4 * n_par:]

        def total(ref, idx):
            acc = ref[(0,) + idx].astype(F32)
            for d in range(1, NDEV):
                acc = acc + ref[(d,) + idx].astype(F32)
            return acc

        def update(k, region, g):
            w_ref, m_ref, v_ref = par_refs[k]
            g_ref, d_ref, nm_ref, nv_ref = out_par[k]
            delta, nm, nv = _adamw(w_ref[region], g, m_ref[region], v_ref[region])
            g_ref[region] = g
            d_ref[region] = delta
            nm_ref[region] = nm
            nv_ref[region] = nv

        for l in range(L):
            base = l * VEC_ROWS_PER_LAYER
            for k in range(NMOD):
                update(0, (slice(l, l + 1), slice(k * D, (k + 1) * D)), total(vec_ref, (slice(base + k, base + k + 1),)))
            update(1, (slice(l, l + 1),), total(vec_ref, (slice(base + 6, base + 7),)))
            update(2, (slice(l, l + 1),), total(vec_ref, (slice(base + 7, base + 8),)))
            wbase = l * W256_ROWS_PER_LAYER
            update(4, (slice(l, l + 1),), total(w256_ref, (slice(wbase + W256_CONV_B, wbase + W256_CONV_B + 1),)))
            update(5, (slice(l, l + 1),), total(w256_ref, (slice(wbase + W256_GN, wbase + W256_GN + 1),)))
            update(6, (l,), total(sw_refs[l], ()))
            update(7, (l,), total(sb_ref, (slice(l * SG_HEADS, (l + 1) * SG_HEADS),)))
            taps_ref[l] = total(w256_ref, (slice(wbase + W256_TAPS, wbase + W256_TAPS + 8),))
        update(3, (slice(0, 1),), total(vec_ref, (slice(VEC_FINAL_ROW, VEC_FINAL_ROW + 1),)))
        loss_ref[...] = total(vec_ref, (slice(VEC_FINAL_ROW + 1, VEC_FINAL_ROW + 2), slice(0, LANES)))

    out_shape = []
    for w, _, _ in params:
        out_shape += [jax.ShapeDtypeStruct(w.shape, F32)] * 4
    out_shape += [jax.ShapeDtypeStruct((1, LANES), F32), jax.ShapeDtypeStruct((L, 8, CW), F32)]
    outs = pl.pallas_call(body, name=name, out_shape=out_shape, compiler_params=_cparams())(
        vec_all, w256_all, sb_all, *sw_all, *[a for p in params for a in p])
    return [outs[4 * k:4 * k + 4] for k in range(n_par)], outs[4 * n_par:]


def adamw_plain(w, g, m, v, tr, name):
    rows, cols = w.shape
    spec = pl.BlockSpec((tr, cols), lambda i: (i, 0))

    def body(w_ref, g_ref, m_ref, v_ref, d_ref, nm_ref, nv_ref):
        delta, nm, nv = _adamw(w_ref[...], g_ref[...], m_ref[...], v_ref[...])
        d_ref[...] = delta
        nm_ref[...] = nm
        nv_ref[...] = nv

    shp = jax.ShapeDtypeStruct((rows, cols), F32)
    return pl.pallas_call(
        body, name=name, grid=(rows // tr,), in_specs=[spec] * 4, out_specs=[spec] * 3,
        out_shape=[shp, shp, shp], compiler_params=_cparams(("parallel",)),
    )(w, g, m, v)


def adamw_reduce(w, parts, m, v, tr, name, tie=None):
    _, rows, cols = w.shape
    spec = pl.BlockSpec((None, tr, cols), lambda l, i: (l, i, 0))
    pspecs = [pl.BlockSpec((NDEV, tr, cols), lambda l, i, k=k: (0, jnp.where(l == k, i, 0), 0)) for k in range(L)]

    ties = [] if tie is None else [tie]

    def body(w_ref, p0_ref, p1_ref, m_ref, v_ref, *rest):
        g_ref, d_ref, nm_ref, nv_ref = rest[len(ties):]
        first_layer = pl.program_id(0) == 0
        g = jnp.zeros((tr, cols), F32)
        for d in range(NDEV):
            g = g + jnp.where(first_layer, p0_ref[d], p1_ref[d]).astype(F32)
        delta, nm, nv = _adamw(w_ref[...], g, m_ref[...], v_ref[...])
        g_ref[...] = g
        d_ref[...] = delta
        nm_ref[...] = nm
        nv_ref[...] = nv

    shp = jax.ShapeDtypeStruct(w.shape, F32)
    return pl.pallas_call(
        body, name=name, grid=(L, rows // tr),
        in_specs=[spec] + pspecs + [spec, spec] + [pl.BlockSpec(t.shape, lambda l, i: (0, 0)) for t in ties],
        out_specs=[spec] * 4, out_shape=[shp] * 4, compiler_params=_cparams(("parallel", "parallel")),
    )(w, *parts, m, v, *ties)


def _pad_rows(flat, rows):
    return jnp.pad(flat, (0, rows * LANES - flat.shape[0])).reshape(rows, LANES)


def kernel(x, c, ada_w, ada_b, norm_mix_g, norm_mlp_g, w_in, conv_w, conv_b, gmlp_norm_g, spatial_w, spatial_b, w_out, mlp_w1, mlp_w2, final_norm_g, loss_target, m_ada_w, m_ada_b, m_norm_mix_g, m_norm_mlp_g, m_w_in, m_conv_w, m_conv_b, m_gmlp_norm_g, m_spatial_w, m_spatial_b, m_w_out, m_mlp_w1, m_mlp_w2, m_final_norm_g, v_ada_w, v_ada_b, v_norm_mix_g, v_norm_mlp_g, v_w_in, v_conv_w, v_conv_b, v_gmlp_norm_g, v_spatial_w, v_spatial_b, v_w_out, v_mlp_w1, v_mlp_w2, v_final_norm_g):
    me = _lin(_my_pos())
    x0 = x[0]
    target = loss_target[0]
    conv_shard = conv_w.shape[-1]

    w_in_b, w_out_b, w1_b, w2_b = [w.astype(BF16) for w in (w_in, w_out, mlp_w1, mlp_w2)]
    pack0 = _pad_rows(jnp.concatenate([c.reshape(-1), conv_w.reshape(-1)]), 16)
    g0, gw_in0 = run_comm(Gather([pack0, w_in_b[0]]), "gather_first")
    g0 = g0.reshape(NDEV, 16 * LANES)
    c_all = g0[:, :D]
    conv_full = (g0[:, D:D + L * 3 * conv_shard].reshape(NDEV, L, 3, conv_shard)
                 .transpose(1, 2, 0, 3).reshape(L, 3, CW))

    def canonical_w_in(gathered):
        return gathered.transpose(1, 0, 2).reshape(D, PROJ)

    weight_plans = [Gather([w_out_b[l], w1_b[l], w2_b[l]]) for l in range(L)]
    W_in = [canonical_w_in(gw_in0), None]
    W_out, W1, W2 = [None] * L, [None] * L, [None] * L

    ada_b_mine = lax.dynamic_slice(ada_b, (0, me * ADA_COLS), (L, ADA_COLS)).reshape(L, 1, ADA_COLS)
    mod_part, c_act = ada_fwd(c_all, ada_w, ada_b_mine, "ada_fwd")
    gmod = run_comm(Gather([mod_part]), "gather_mod")[0]
    mod = lax.dynamic_index_in_dim(gmod, me, axis=2, keepdims=False)
    mod = mod.transpose(1, 0, 2).reshape(L, NMOD, 1, D)

    cw8 = jnp.pad(conv_full, ((0, 0), (0, 5), (0, 0)))
    sg_bias = jnp.repeat(spatial_b.transpose(0, 2, 1), HD, axis=2)

    saved = []
    xl = x0
    for l in range(L):
        sh_m, sc_m, g_m, sh_f, sc_f, g_f = [mod[l, k] for k in range(NMOD)]
        h1 = normmod_fwd(xl, norm_mix_g[l:l + 1], sc_m, sh_m, f"norm_mix_fwd{l}")
        if l > 0:
            W_in[l] = canonical_w_in(finish_copies(w_in_handle, xl, f"gather_w_in{l}_wait"))
        qkv = mm_layer("proj_qkv", l, h1, W_in[l], out_dtypes=[BF16], cols=(0, QKV))[0]
        proj = mm_layer("proj_rest", l, h1, W_in[l], out_dtypes=[F32], cols=(QKV, REST))[0]
        a_out, a_tot, *gathered = attn_fwd(qkv, f"attn_fwd{l}", comm=weight_plans[l])
        W_out[l] = gathered[0].reshape(D, D)
        W1[l] = gathered[1]
        W2[l] = gathered[2].reshape(DFF, D)
        if l + 1 < L:
            w_in_handle, token = start_copies(w_in_b[l + 1], me, f"gather_w_in{l + 1}_start", True, after=a_out)
            g_m = tied(g_m, token)
        c_out = conv_fwd(proj, cw8[l], conv_b[l:l + 1], f"conv_fwd{l}")
        s_out = sg_fwd(proj, gmlp_norm_g[l:l + 1], spatial_w[l], sg_bias[l], f"sg_fwd{l}")
        cat = jnp.concatenate([a_out, c_out.astype(BF16), s_out.astype(BF16)], axis=1)
        mix, x1 = mm_layer("mix", l, cat, W_out[l], out_dtypes=[F32, F32],
                           epilogue=lambda acc, xr, g: (acc, xr + g * acc), extras=[(xl, "tile"), (g_m, "col")])
        h2 = normmod_fwd(x1, norm_mlp_g[l:l + 1], sc_f, sh_f, f"norm_mlp_fwd{l}")
        ra, r = mm_layer("mlp_up", l, h2, W1[l], out_dtypes=[BF16, BF16], b_blocks=True,
                         epilogue=lambda acc: (jnp.maximum(acc, 0.0), jnp.square(jnp.maximum(acc, 0.0))))
        m2, x2 = mm_layer("mlp_down", l, r, W2[l], out_dtypes=[F32, F32],
                          epilogue=lambda acc, xr, g: (acc, xr + g * acc), extras=[(x1, "tile"), (g_f, "col")])
        saved.append(dict(x=xl, h1=h1, proj=proj, qkv=qkv, a_tot=a_tot, cat=cat, mix=mix,
                          x1=x1, h2=h2, ra=ra, r=r, m2=m2))
        xl = x2

    dx, loss_part, d_final_g = loss_head(xl, target, final_norm_g.reshape(1, D), "loss_head")

    p_in, p_out, p_w1, p_w2 = [None] * L, [None] * L, [None] * L, [None] * L
    w_in_grads = [None] * L
    vec_rows, d_norm_mix, d_norm_mlp = [None] * L, [None] * L, [None] * L
    dcw8, d_conv_b, d_gn, d_sw, d_sb = [None] * L, [None] * L, [None] * L, [None] * L, [None] * L
    late_grads = [None] * L
    for l in reversed(range(L)):
        sv = saved[l]
        sh_m, sc_m, g_m, sh_f, sc_f, g_f = [mod[l, k] for k in range(NMOD)]
        dm2, dg_f = gate_bwd(dx, sv["m2"], g_f, f"gate_mlp_bwd{l}")
        da = mm_layer("mlp_down_dgrad", l, dm2, W2[l], out_dtypes=[BF16], trans_b=True,
                      epilogue=lambda acc, rav: (acc * (2.0 * rav.astype(F32)),), extras=[(sv["ra"], "tile")])[0]
        dW2 = mm_layer("mlp_down_wgrad", l, sv["r"], dm2, out_dtypes=[BF16], trans_a=True)[0]
        dW1 = mm_layer("mlp_up_wgrad", l, sv["h2"], da, out_dtypes=[BF16], trans_a=True, out_blocks=True)[0]
        dh2 = mm_layer("mlp_up_dgrad", l, da, W1[l], out_dtypes=[F32], trans_b=True, b_blocks=True)[0]
        dx1, dsc_f, dsh_f, d_norm_mlp[l] = normmod_bwd(sv["x1"], dh2, dx, norm_mlp_g[l:l + 1], sc_f,
                                                       f"norm_mlp_bwd{l}")
        dmix, dg_m = gate_bwd(dx1, sv["mix"], g_m, f"gate_mix_bwd{l}")
        dcat = mm_layer("mix_dgrad", l, dmix, W_out[l], out_dtypes=[F32], trans_b=True)[0]
        dW_out = mm_layer("mix_wgrad", l, sv["cat"], dmix, out_dtypes=[BF16], trans_a=True)[0]
        pieces_w2, pieces_out = dW2.reshape(NDEV, DFF // NDEV, D), dW_out.reshape(NDEV, D // NDEV, D)
        ride, late = ([pieces_w2, pieces_out], dW1) if l == L - 1 else ([pieces_w2, dW1], pieces_out)
        dq, dk, dv, *arrived = attn_bwd(sv["qkv"], dcat, sv["a_tot"], f"attn_bwd{l}", comm=Exchange(ride))
        p_w2[l] = arrived[0]
        (p_out if l == L - 1 else p_w1)[l] = arrived[1]
        late_grads[l], late_token = start_copies(late, me, f"exchange_late{l}_start", False, after=dq)
        dbg, dcg, dhc, dcw8[l], d_conv_b[l] = conv_bwd(sv["proj"], dcat, cw8[l], conv_b[l:l + 1], f"conv_bwd{l}")
        dus, dvs, d_gn[l], dsw, dbias = sg_bwd(sv["proj"], dcat, gmlp_norm_g[l:l + 1], spatial_w[l], sg_bias[l],
                                               f"sg_bwd{l}")
        d_sw[l] = dsw.astype(BF16)
        d_sb[l] = dbias.reshape(T, SG_HEADS, HD).sum(axis=2).T
        dproj = jnp.concatenate([dq, dk, dv, dbg, dcg, dhc, dus, dvs], axis=1).astype(BF16)
        dW_in = mm_layer("proj_wgrad", l, sv["h1"], dproj, out_dtypes=[BF16], trans_a=True,
                         extras=[(late_token, "tie")])[0]
        pieces = dW_in.reshape(D, NDEV, PROJ // NDEV).transpose(1, 0, 2)
        w_in_grads[l], token = start_copies(pieces, me, f"exchange_w_in{l}_start", False)
        dh1 = mm_layer("proj_dgrad", l, dproj, W_in[l], out_dtypes=[F32], trans_b=True, extras=[(token, "tie")])[0]
        dx, dsc_m, dsh_m, d_norm_mix[l] = normmod_bwd(sv["x"], dh1, dx1, tied(norm_mix_g[l:l + 1], token), sc_m,
                                                      f"norm_mix_bwd{l}")
        vec_rows[l] = [dsh_m, dsc_m, dg_m, dsh_f, dsc_f, dg_f, d_norm_mix[l], d_norm_mlp[l]]

    grad_x = dx.reshape(1, S, D)

    g_w2, d_w2, nm_w2, nv_w2 = adamw_reduce(mlp_w2, p_w2, m_mlp_w2, v_mlp_w2, 256, "adamw_mlp_w2", tie=token)
    p_w1[L - 1] = finish_copies(late_grads[L - 1], d_w2, f"exchange_late{L - 1}_wait")
    g_w1, d_w1, nm_w1, nv_w1 = adamw_reduce(mlp_w1, p_w1, m_mlp_w1, v_mlp_w1, 256, "adamw_mlp_w1", tie=token)

    vec_pack = jnp.concatenate([row for l in range(L) for row in vec_rows[l]]
                               + [d_final_g, loss_part, jnp.zeros((VEC_ROWS - VEC_FINAL_ROW - 2, D), F32)], axis=0)
    vec_pack, _ = lax.optimization_barrier((vec_pack, (d_w1, d_w2)))
    w256_pack = jnp.concatenate([blk for l in range(L) for blk in (
        dcw8[l], d_conv_b[l], d_gn[l], jnp.zeros((W256_ROWS_PER_LAYER - W256_GN - 1, CW), F32))], axis=0)
    vec_all, w256_all, sb_all, *sw_all = run_comm(
        Gather([vec_pack, w256_pack, jnp.concatenate(d_sb, axis=0)] + d_sw), "gather_small_grads")

    dmod_all = (vec_all[:, :VEC_FINAL_ROW].reshape(NDEV, L, VEC_ROWS_PER_LAYER, D)[:, :, :NMOD]
                .reshape(NDEV, L, NMOD * D))
    dmod_cols = lax.dynamic_slice(dmod_all, (0, 0, me * ADA_COLS), (NDEV, L, ADA_COLS)).transpose(1, 0, 2)
    g_ada_w = ada_bwd(c_act, dmod_cols, "ada_bwd")

    flat2 = lambda t: t.reshape(L * D, ADA_COLS)
    d_ada_w, nm_ada_w, nv_ada_w = [t.reshape(L, D, ADA_COLS) for t in adamw_plain(
        flat2(ada_w), flat2(g_ada_w), flat2(m_ada_w), flat2(v_ada_w), 256, "adamw_ada_w")]

    after = jnp.concatenate([t.reshape(-1)[:1] for t in (d_w1, d_w2, d_ada_w)])
    p_in = [finish_copies(w_in_grads[l], after, f"exchange_w_in{l}_wait") for l in range(L)]
    p_out[0] = finish_copies(late_grads[0], after, "exchange_late0_wait")
    g_w_in, d_w_in, nm_w_in, nv_w_in = adamw_reduce(w_in, p_in, m_w_in, v_w_in, 256, "adamw_w_in")
    g_w_out, d_w_out, nm_w_out, nv_w_out = adamw_reduce(w_out, p_out, m_w_out, v_w_out, 128, "adamw_w_out")

    as_row = lambda t: t.reshape(1, D)
    small_params = [(ada_b, m_ada_b, v_ada_b), (norm_mix_g, m_norm_mix_g, v_norm_mix_g),
                    (norm_mlp_g, m_norm_mlp_g, v_norm_mlp_g),
                    (as_row(final_norm_g), as_row(m_final_norm_g), as_row(v_final_norm_g)),
                    (conv_b, m_conv_b, v_conv_b), (gmlp_norm_g, m_gmlp_norm_g, v_gmlp_norm_g),
                    (spatial_w, m_spatial_w, v_spatial_w), (spatial_b, m_spatial_b, v_spatial_b)]
    updated, (loss_sum, taps_sum) = small_update(vec_all, w256_all, sb_all, sw_all, small_params, "small_update")
    loss = loss_sum[0, 0]
    u_ada_b, u_norm_mix, u_norm_mlp, u_final, u_conv_b, u_gn, u_sw, u_sb = updated
    u_final = [t.reshape(D) for t in u_final]
    g_conv_w = lax.dynamic_slice(taps_sum, (0, 0, me * conv_shard), (L, 3, conv_shard))
    flat_cw = lambda t: t.reshape(L * 3, conv_shard)
    u_conv_w = [g_conv_w] + [t.reshape(L, 3, conv_shard) for t in adamw_plain(
        flat_cw(conv_w), flat_cw(g_conv_w), flat_cw(m_conv_w), flat_cw(v_conv_w), L * 3, "adamw_conv_w")]
    small_sets = [u_ada_b, u_norm_mix, u_norm_mlp, u_conv_w, u_conv_b, u_gn, u_sw, u_sb, u_final]
    small_g, sd, snm, snv = [[u[k] for u in small_sets] for k in range(4)]

    def ordered(big, small):
        ada, win, wout, w1, w2 = big
        return [ada, small[0], small[1], small[2], win, small[3], small[4], small[5], small[6], small[7],
                wout, w1, w2, small[8]]

    grads = ordered([g_ada_w, g_w_in, g_w_out, g_w1, g_w2], small_g)
    deltas = ordered([d_ada_w, d_w_in, d_w_out, d_w1, d_w2], sd)
    new_m = ordered([nm_ada_w, nm_w_in, nm_w_out, nm_w1, nm_w2], snm)
    new_v = ordered([nv_ada_w, nv_w_in, nv_w_out, nv_w1, nv_w2], snv)
    return (loss, grad_x, *grads, *deltas, *new_m, *new_v)
```

```python
import functools
import math

import jax
import jax.numpy as jnp
from jax import lax
from jax.experimental import pallas as pl
from jax.experimental.pallas import tpu as pltpu

F32 = jnp.float32
BF16 = jnp.bfloat16
MESH = pl.DeviceIdType.MESH

S = 2048
D = 1024
L = 2
NDEV = 8
HD = 64
NH = 8
PROJ = 2816
DFF = 4096
NMOD = 6
EPS = 1e-6
T = 128
SG_HEADS = 4
LANES = 128
CW = 256
QKV = 3 * NH * HD
REST = PROJ - QKV

LR, B1, B2, AEPS, WD, STEP = 0.001, 0.9, 0.999, 1e-08, 0.01, 10
BC1 = 1.0 - B1 ** STEP
BC2 = 1.0 - B2 ** STEP

VMEM_LIMIT = 48 * 1024 * 1024

HBM_SPEC = pl.BlockSpec(memory_space=pltpu.HBM)


def _cparams(sem=None):
    return pltpu.CompilerParams(dimension_semantics=sem, vmem_limit_bytes=VMEM_LIMIT)


def _my_pos():
    return lax.axis_index("x"), lax.axis_index("y"), lax.axis_index("c")


def _lin(p):
    return 4 * p[0] + 2 * p[1] + p[2]


class Gather:
    def __init__(self, arrs):
        self.arrs = list(arrs)
        n = len(self.arrs)
        self.out_shape = [jax.ShapeDtypeStruct((NDEV,) + a.shape, a.dtype) for a in self.arrs]
        self.scratch = [pltpu.SemaphoreType.DMA((n, 7)), pltpu.SemaphoreType.DMA((n, 7)),
                        pltpu.SemaphoreType.DMA((n,))]

    def phases(self, ins, outs, sems):
        n = len(self.arrs)
        send_sems, recv_sems, local_sems = sems
        x, y, c = _my_pos()
        me, sibling = (x, y, c), (x, y, 1 - c)
        chips = [(1 - x, y), (x, 1 - y), (1 - x, 1 - y)]

        def copy(a, k, block, to, src=None):
            slot = outs[a].at[_lin(block)]
            return pltpu.make_async_remote_copy(
                src_ref=slot if src is None else src, dst_ref=slot,
                send_sem=send_sems.at[a, k], recv_sem=recv_sems.at[a, k],
                device_id=to, device_id_type=MESH)

        def mine(a):
            return pltpu.make_async_copy(ins[a], outs[a].at[_lin(me)], local_sems.at[a])

        def first(a):
            return [copy(a, 0, me, sibling, src=ins[a])] + [
                copy(a, 1 + j, me, (*chip, c), src=ins[a]) for j, chip in enumerate(chips)]

        def passed(a):
            return [copy(a, 4 + j, (*chip, c), sibling) for j, chip in enumerate(chips)]

        def start():
            for a in range(n):
                mine(a).start()
                for cp in first(a):
                    cp.start()

        def relay():
            for j, chip in enumerate(chips):
                for a in range(n):
                    copy(a, 1 + j, (*chip, c), me).wait_recv()
                    passed(a)[j].start()

        def finish():
            for a in range(n):
                copy(a, 0, sibling, me).wait_recv()
            for j, chip in enumerate(chips):
                for a in range(n):
                    copy(a, 4 + j, (*chip, 1 - c), me).wait_recv()
            for a in range(n):
                for cp in first(a) + passed(a):
                    cp.wait_send()
                mine(a).wait()

        return start, relay, finish


class Exchange:
    def __init__(self, arrs):
        self.arrs = list(arrs)
        n = len(self.arrs)
        self.out_shape = [jax.ShapeDtypeStruct(a.shape, a.dtype) for a in self.arrs]
        self.scratch = [pltpu.SemaphoreType.DMA((n, 7)), pltpu.SemaphoreType.DMA((n, 7)),
                        pltpu.SemaphoreType.DMA((n,))]

    def phases(self, ins, outs, sems):
        n = len(self.arrs)
        send_sems, recv_sems, local_sems = sems
        x, y, c = _my_pos()
        me = (x, y, c)

        def peer(mask):
            return (1 - x if mask & 4 else x, 1 - y if mask & 2 else y, 1 - c if mask & 1 else c)

        def copy(a, mask):
            return pltpu.make_async_remote_copy(
                src_ref=ins[a].at[_lin(peer(mask))], dst_ref=outs[a].at[_lin(me)],
                send_sem=send_sems.at[a, mask - 1], recv_sem=recv_sems.at[a, mask - 1],
                device_id=peer(mask), device_id_type=MESH)

        def arrival(a, mask):
            return pltpu.make_async_remote_copy(
                src_ref=ins[a].at[_lin(me)], dst_ref=outs[a].at[_lin(peer(mask))],
                send_sem=send_sems.at[a, mask - 1], recv_sem=recv_sems.at[a, mask - 1],
                device_id=peer(mask), device_id_type=MESH)

        def mine(a):
            return pltpu.make_async_copy(ins[a].at[_lin(me)], outs[a].at[_lin(me)], local_sems.at[a])

        def start():
            for a in range(n):
                mine(a).start()
            for mask in (4, 2, 6, 1, 5, 3, 7):
                for a in range(n):
                    copy(a, mask).start()

        def relay():
            pass

        def finish():
            for mask in range(1, 8):
                for a in range(n):
                    arrival(a, mask).wait_recv()
            for mask in range(1, 8):
                for a in range(n):
                    copy(a, mask).wait_send()
            for a in range(n):
                mine(a).wait()

        return start, relay, finish


def run_comm(plan, name):
    n = len(plan.arrs)

    def body(*refs):
        start, relay, finish = plan.phases(refs[:n], refs[n:2 * n], refs[2 * n:])
        start()
        relay()
        finish()

    outs = pl.pallas_call(
        body, name=name, out_shape=plan.out_shape,
        in_specs=[HBM_SPEC] * n, out_specs=[HBM_SPEC] * n, scratch_shapes=plan.scratch,
    )(*plan.arrs)
    return list(outs)


SEM_SPEC = pl.BlockSpec(memory_space=pltpu.SEMAPHORE)
DATAFLOW = pltpu.SideEffectType.DATAFLOW_SIDE_EFFECTING


def _peer_copies(src_ref, land_ref, send_sems, recv_sems, first, same_block):
    x, y, c = _my_pos()
    me = (x, y, c)
    sends, arrivals = [], []
    for mask in (4, 2, 6, 1, 5, 3, 7):
        peer = (1 - x if mask & 4 else x, 1 - y if mask & 2 else y, 1 - c if mask & 1 else c)
        sends.append(pltpu.make_async_remote_copy(
            src_ref=src_ref if same_block else src_ref.at[_lin(peer)], dst_ref=land_ref.at[_lin(me)],
            send_sem=send_sems.at[first + mask - 1], recv_sem=recv_sems.at[first + mask - 1], device_id=peer,
            device_id_type=MESH))
        arrivals.append(pltpu.make_async_remote_copy(
            src_ref=src_ref if same_block else src_ref.at[_lin(me)], dst_ref=land_ref.at[_lin(peer)],
            send_sem=send_sems.at[first + mask - 1], recv_sem=recv_sems.at[first + mask - 1], device_id=peer,
            device_id_type=MESH))
    return sends, arrivals


def start_copies(srcs, me, name, same_block, after=None):
    n = len(srcs)
    landings = []
    for src in srcs:
        own = src[None] if same_block else lax.dynamic_index_in_dim(src, me, axis=0, keepdims=True)
        landings.append(lax.dynamic_update_slice(lax.empty((NDEV,) + own.shape[1:], src.dtype), own,
                                                 (me,) + (0,) * (own.ndim - 1)))

    def body(*refs):
        send_sems, recv_sems = refs[-2 * n - 3], refs[-2 * n - 2]
        token = refs[-1]
        for k in range(n):
            sends, _ = _peer_copies(refs[2 * k], refs[2 * k + 1], send_sems, recv_sems, 7 * k, same_block)
            for cp in sends:
                cp.start()
        token[...] = jnp.zeros_like(token)

    hbm = lambda a: pltpu.HBM(a.shape, a.dtype)
    pairs = [a for pair in zip(srcs, landings) for a in pair]
    extra = [] if after is None else [after]
    sems = pltpu.SemaphoreType.DMA((7 * n,))
    send_sems, recv_sems, *thru, token = pl.pallas_call(
        body, name=name,
        out_shape=(sems, sems, *[hbm(a) for a in pairs], jax.ShapeDtypeStruct((8, LANES), F32)),
        in_specs=[HBM_SPEC] * (2 * n) + [pl.BlockSpec(memory_space=pl.ANY)] * len(extra),
        out_specs=(SEM_SPEC, SEM_SPEC, *[HBM_SPEC] * (2 * n), pl.BlockSpec(memory_space=pltpu.VMEM)),
        input_output_aliases={k: 2 + k for k in range(2 * n)},
        compiler_params=pltpu.CompilerParams(has_side_effects=DATAFLOW),
    )(*[pltpu.with_memory_space_constraint(a, pltpu.HBM) for a in pairs], *extra)
    return (send_sems, recv_sems, thru, same_block), token


def finish_copies(handle, after, name):
    send_sems, recv_sems, thru, same_block = handle
    n = len(thru) // 2

    def body(*refs):
        send_sems, recv_sems = refs[2 * n], refs[2 * n + 1]
        for k in range(n):
            sends, arrivals = _peer_copies(refs[2 * k], refs[2 * k + 1], send_sems, recv_sems, 7 * k, same_block)
            for cp in sends:
                cp.wait_send()
            for cp in arrivals:
                cp.wait_recv()

    hbm = lambda a: pltpu.HBM(a.shape, a.dtype)
    outs = pl.pallas_call(
        body, name=name, out_shape=tuple(hbm(a) for a in thru),
        in_specs=[HBM_SPEC] * (2 * n) + [SEM_SPEC, SEM_SPEC, pl.BlockSpec(memory_space=pl.ANY)],
        out_specs=tuple([HBM_SPEC] * (2 * n)), input_output_aliases={k: k for k in range(2 * n)},
        compiler_params=pltpu.CompilerParams(has_side_effects=DATAFLOW),
    )(*thru, send_sems, recv_sems, after)
    return [outs[2 * k + 1] for k in range(n)]


def tied(x, token):
    return x + token[0:1, 0:1].astype(x.dtype)


MM_TILES = {
    "proj_qkv": (S, 512), "proj_rest": (S, 256), "mix": (1024, 512), "mlp_up": (S, 512), "mlp_down": (1024, 256),
    "mlp_down_dgrad": (1024, 1024), "mlp_down_wgrad": (1024, 1024), "mlp_up_wgrad": (1024, 512),
    "mlp_up_dgrad": (1024, 512), "mix_dgrad": (1024, 512), "mix_wgrad": (512, 1024),
    "proj_wgrad": (1024, PROJ // 2), "proj_dgrad": (1024, 512),
}


def mm_layer(kind, l, a, b, **kw):
    tm, tn = MM_TILES[kind]
    return mm(a, b, tm=tm, tn=tn, name=f"{kind}{l}", **kw)


def mm(a, b, *, tm, tn, out_dtypes, epilogue=None, extras=(), name, trans_a=False, trans_b=False,
       cols=None, b_blocks=False, out_blocks=False):
    if trans_a:
        kdim, m = a.shape
    else:
        m, kdim = a.shape
    shard = b.shape[-1] if b_blocks else None
    if b_blocks:
        full = (b.shape[1], NDEV * shard)
    else:
        full = b.shape
    first, ncols = cols if cols is not None else (0, full[0] if trans_b else full[1])
    assert full[1 if trans_b else 0] == kdim and m % tm == 0 and ncols % tn == 0 and first % tn == 0
    j0 = first // tn
    if trans_a:
        a_spec = pl.BlockSpec((kdim, tm), lambda i, j: (0, i))
    else:
        a_spec = pl.BlockSpec((tm, kdim), lambda i, j: (i, 0))
    if b_blocks and trans_b:
        b_spec = pl.BlockSpec((NDEV, tn, shard), lambda i, j: (0, j0 + j, 0))
    elif b_blocks:
        assert tn == shard
        b_spec = pl.BlockSpec((None, kdim, tn), lambda i, j: (j0 + j, 0, 0))
    elif trans_b:
        b_spec = pl.BlockSpec((tn, kdim), lambda i, j: (j0 + j, 0))
    else:
        b_spec = pl.BlockSpec((kdim, tn), lambda i, j: (0, j0 + j))
    if out_blocks:
        assert tn * NDEV == ncols
        out_spec = pl.BlockSpec((None, tm, tn), lambda i, j: (j, i, 0))
        out_dims = (NDEV, m, tn)
    else:
        out_spec = pl.BlockSpec((tm, tn), lambda i, j: (i, j))
        out_dims = (m, ncols)
    ex_specs = []
    for arr, kind in extras:
        if kind == "tile":
            ex_specs.append(pl.BlockSpec((tm, tn), lambda i, j: (i, j)))
        elif kind == "col":
            ex_specs.append(pl.BlockSpec((1, tn), lambda i, j: (0, j)))
        else:
            ex_specs.append(pl.BlockSpec(arr.shape, lambda i, j: (0, 0)))
    n_ex, n_out = len(extras), len(out_dtypes)
    used = [k for k, (_, kind) in enumerate(extras) if kind != "tie"]

    def body(a_ref, b_ref, *rest):
        ex_refs, out_refs = rest[:n_ex], rest[n_ex:]
        if trans_a:
            acc = lax.dot_general(a_ref[...], b_ref[...], (((0,), (0,)), ((), ())),
                                  preferred_element_type=F32)
        elif trans_b and b_blocks:
            acc = jnp.zeros((tm, tn), F32)
            for d in range(NDEV):
                acc = acc + lax.dot_general(a_ref[:, d * shard:(d + 1) * shard], b_ref[d],
                                            (((1,), (1,)), ((), ())), preferred_element_type=F32)
        elif trans_b:
            acc = lax.dot_general(a_ref[...], b_ref[...], (((1,), (1,)), ((), ())),
                                  preferred_element_type=F32)
        else:
            acc = jnp.dot(a_ref[...], b_ref[...], preferred_element_type=F32)
        outs = (acc,) if epilogue is None else epilogue(acc, *[ex_refs[k][...] for k in used])
        for o_ref, val in zip(out_refs, outs):
            o_ref[...] = val.astype(o_ref.dtype)

    outs = pl.pallas_call(
        body, name=name, grid=(m // tm, ncols // tn),
        in_specs=[a_spec, b_spec] + ex_specs,
        out_specs=[out_spec for _ in range(n_out)],
        out_shape=[jax.ShapeDtypeStruct(out_dims, dt) for dt in out_dtypes],
        compiler_params=_cparams(("parallel", "parallel")),
    )(a, b, *[arr for arr, _ in extras])
    return list(outs)


TR = 256

ROW_SPEC = pl.BlockSpec((TR, D), lambda i: (i, 0))
VEC_SPEC = pl.BlockSpec((1, D), lambda i: (0, 0))


def normmod_fwd(x, g, sc, sh, name):
    def body(x_ref, g_ref, sc_ref, sh_ref, o_ref):
        xv = x_ref[...]
        rstd = lax.rsqrt(jnp.mean(xv * xv, axis=-1, keepdims=True) + EPS)
        n = (xv * rstd) * g_ref[...]
        o_ref[...] = (n * (1.0 + sc_ref[...]) + sh_ref[...]).astype(o_ref.dtype)

    return pl.pallas_call(
        body, name=name, grid=(S // TR,),
        in_specs=[ROW_SPEC, VEC_SPEC, VEC_SPEC, VEC_SPEC], out_specs=ROW_SPEC,
        out_shape=jax.ShapeDtypeStruct((S, D), BF16),
        compiler_params=_cparams(("parallel",)),
    )(x, g, sc, sh)


def normmod_bwd(x, dh, dres, g, sc, name):
    def body(x_ref, dh_ref, dres_ref, g_ref, sc_ref, dx_ref, dsc_ref, dsh_ref, dg_ref):
        @pl.when(pl.program_id(0) == 0)
        def _():
            dsc_ref[...] = jnp.zeros_like(dsc_ref)
            dsh_ref[...] = jnp.zeros_like(dsh_ref)
            dg_ref[...] = jnp.zeros_like(dg_ref)

        xv, dh = x_ref[...], dh_ref[...]
        gv = g_ref[...]
        rstd = lax.rsqrt(jnp.mean(xv * xv, axis=-1, keepdims=True) + EPS)
        xhat = xv * rstd
        dn = dh * (1.0 + sc_ref[...])
        dxhat = dn * gv
        dx_ref[...] = dres_ref[...] + rstd * (dxhat - xhat * jnp.mean(dxhat * xhat, axis=-1, keepdims=True))
        dsc_ref[...] += jnp.sum(dh * (xhat * gv), axis=0, keepdims=True)
        dsh_ref[...] += jnp.sum(dh, axis=0, keepdims=True)
        dg_ref[...] += jnp.sum(dn * xhat, axis=0, keepdims=True)

    vec_out = jax.ShapeDtypeStruct((1, D), F32)
    return pl.pallas_call(
        body, name=name, grid=(S // TR,),
        in_specs=[ROW_SPEC, ROW_SPEC, ROW_SPEC, VEC_SPEC, VEC_SPEC],
        out_specs=[ROW_SPEC, VEC_SPEC, VEC_SPEC, VEC_SPEC],
        out_shape=[jax.ShapeDtypeStruct((S, D), F32), vec_out, vec_out, vec_out],
        compiler_params=_cparams(("arbitrary",)),
    )(x, dh, dres, g, sc)


def gate_bwd(dx, branch, gate, name):
    def body(dx_ref, br_ref, gate_ref, o_ref, dgate_ref):
        @pl.when(pl.program_id(0) == 0)
        def _():
            dgate_ref[...] = jnp.zeros_like(dgate_ref)

        dxv = dx_ref[...]
        o_ref[...] = (dxv * gate_ref[...]).astype(o_ref.dtype)
        dgate_ref[...] += jnp.sum(dxv * br_ref[...], axis=0, keepdims=True)

    return pl.pallas_call(
        body, name=name, grid=(S // TR,),
        in_specs=[ROW_SPEC, ROW_SPEC, VEC_SPEC], out_specs=[ROW_SPEC, VEC_SPEC],
        out_shape=[jax.ShapeDtypeStruct((S, D), BF16), jax.ShapeDtypeStruct((1, D), F32)],
        compiler_params=_cparams(("arbitrary",)),
    )(dx, branch, gate)


def loss_head(x, target, g, name):
    def body(x_ref, t_ref, g_ref, dx_ref, loss_ref, dg_ref):
        @pl.when(pl.program_id(0) == 0)
        def _():
            loss_ref[...] = jnp.zeros_like(loss_ref)
            dg_ref[...] = jnp.zeros_like(dg_ref)

        xv, gv = x_ref[...], g_ref[...]
        rstd = lax.rsqrt(jnp.mean(xv * xv, axis=-1, keepdims=True) + EPS)
        xhat = xv * rstd
        err = xhat * gv - t_ref[...]
        loss_ref[...] += jnp.sum(err * err) * (0.5 / D)
        dy = err * (1.0 / D)
        dg_ref[...] += jnp.sum(dy * xhat, axis=0, keepdims=True)
        dxhat = dy * gv
        dx_ref[...] = rstd * (dxhat - xhat * jnp.mean(dxhat * xhat, axis=-1, keepdims=True))

    return pl.pallas_call(
        body, name=name, grid=(S // TR,),
        in_specs=[ROW_SPEC, ROW_SPEC, VEC_SPEC],
        out_specs=[ROW_SPEC, VEC_SPEC, VEC_SPEC],
        out_shape=[jax.ShapeDtypeStruct((S, D), F32), jax.ShapeDtypeStruct((1, D), F32),
                   jax.ShapeDtypeStruct((1, D), F32)],
        compiler_params=_cparams(("arbitrary",)),
    )(x, target, g)


TQ = 512
RS = 128
NSUB = TQ // RS
TK = 128


def _dot_hilo(a, tri_twice):
    hi = a.astype(BF16)
    lo = (a - hi.astype(F32)).astype(BF16)
    return jnp.dot(jnp.concatenate([hi, lo], axis=1), tri_twice, preferred_element_type=F32)


def _log_stay(z):
    return -(jnp.maximum(z, 0.0) + jnp.log(1.0 + jnp.exp(-jnp.abs(z))))


def _tri_and_ones(kind):
    row = jnp.bitwise_and(lax.broadcasted_iota(jnp.int32, (2 * TK, 2 * TK), 0), TK - 1)
    col = lax.broadcasted_iota(jnp.int32, (2 * TK, 2 * TK), 1)
    tri = {"after": row > col, "upto": row <= col, "before": row < col}[kind]
    return jnp.logical_or(col >= TK, tri).astype(BF16)


NPAIR = NH // 2
SCALE = HD ** -0.5


def _pair_specs(first_block):
    rows = pl.BlockSpec((TQ, LANES), lambda p, i: (i, first_block + p))
    whole = pl.BlockSpec((S, LANES), lambda p, i: (0, first_block + p))
    return rows, whole


Q_ROWS_SPEC, _ = _pair_specs(0)
_, K_ALL_SPEC = _pair_specs(NPAIR)
_, V_ALL_SPEC = _pair_specs(2 * NPAIR)
PAIR_ROWS_SPEC = pl.BlockSpec((TQ, LANES), lambda p, i: (i, p))
PAIR_ALL_SPEC = pl.BlockSpec((S, LANES), lambda p, i: (0, p))
PAIR_TOTAL_SPEC = pl.BlockSpec((2, TQ, TK), lambda p, i: (p, i, 0))


def _head_halves(x):
    first = lax.broadcasted_iota(jnp.int32, x.shape, 1) < HD
    zero = jnp.zeros_like(x)
    return jnp.where(first, x, zero), jnp.where(first, zero, x)


def _join_heads(a, b):
    return jnp.where(lax.broadcasted_iota(jnp.int32, a.shape, 1) < HD, a, b)


def _comm_hooks(comm, refs, n_in, n_out, n_scratch):
    nc = len(comm.arrs) if comm is not None else 0
    ins, cin = refs[:n_in], refs[n_in:n_in + nc]
    outs = refs[n_in + nc:n_in + nc + n_out]
    cout = refs[n_in + nc + n_out:n_in + 2 * nc + n_out]
    scratch = refs[n_in + 2 * nc + n_out:n_in + 2 * nc + n_out + n_scratch]
    sems = refs[n_in + 2 * nc + n_out + n_scratch:]
    phases = comm.phases(cin, cout, sems) if comm is not None else None
    return ins, outs, scratch, phases


def _with_comm(comm, in_specs, out_specs, out_shape, operands, scratch):
    if comm is None:
        return dict(in_specs=in_specs, out_specs=out_specs, out_shape=out_shape, scratch_shapes=scratch), operands
    nc = len(comm.arrs)
    return dict(in_specs=in_specs + [HBM_SPEC] * nc, out_specs=out_specs + [HBM_SPEC] * nc,
                out_shape=out_shape + comm.out_shape, scratch_shapes=scratch + comm.scratch), operands + comm.arrs


def attn_fwd(qkv, name, comm=None):
    n_steps = S // TQ

    def body(*refs):
        (q_ref, k_ref, v_ref), (o_ref, r_ref), (acc_ref, z_even, z_odd, w_ref), phases = _comm_hooks(
            comm, refs, 3, 2, 4)
        p = pl.program_id(0)
        i = pl.program_id(1)
        if phases is not None:
            pl.when(jnp.logical_and(p == 0, i == 0))(phases[0])
            pl.when(jnp.logical_and(p == NPAIR - 1, i == n_steps - 2))(phases[1])
        chains = [(sub, h) for sub in range(NSUB) for h in range(2)]
        q_sub = [_head_halves(q_ref[pl.ds(sub * RS, RS), :] * SCALE) for sub in range(NSUB)]
        after = _tri_and_ones("after")
        below_diagonal = (lax.broadcasted_iota(jnp.int32, (RS, TK), 1)
                          < lax.broadcasted_iota(jnp.int32, (RS, TK), 0))
        base = i * NSUB
        all_subs = list(range(NSUB))

        acc_ref[...] = jnp.zeros_like(acc_ref)
        r_ref[...] = jnp.zeros_like(r_ref)
        w_ref[...] = jnp.zeros_like(w_ref)

        def key_rows(block):
            return pl.ds(pl.multiple_of(block * TK, TK), TK)

        def store_scores(z_ref, block, subs):
            kb = k_ref[key_rows(block), :]
            for c, (sub, h) in enumerate(chains):
                if sub in subs:
                    z_ref[c] = lax.dot_general(q_sub[sub][h], kb, (((1,), (1,)), ((), ())),
                                               preferred_element_type=F32)

        def add_weighted_values(block, subs):
            vb = v_ref[key_rows(block), :]
            for sub in subs:
                acc_ref[pl.ds(sub * RS, RS), :] += _join_heads(*[
                    jnp.dot(w_ref[2 * sub + h], vb, preferred_element_type=F32) for h in range(2)])

        def step(block, z_ref, z_next_ref, subs, diagonal_sub, prev_subs, next_subs):
            if prev_subs:
                add_weighted_values(block + 1, prev_subs)
            if next_subs:
                store_scores(z_next_ref, jnp.maximum(block - 1, 0), next_subs)
            active = [(c, sub, h) for c, (sub, h) in enumerate(chains) if sub in subs]
            ls, sums = {}, {}
            for c, sub, h in active:
                ls[c] = _log_stay(z_ref[c])
                sums[c] = _dot_hilo(jnp.where(below_diagonal, ls[c], 0.0) if sub == diagonal_sub else ls[c], after)
            for c, sub, h in active:
                rows = pl.ds(sub * RS, RS)
                later = r_ref[h, rows, :]
                w = jnp.exp(z_ref[c] + ls[c] + (sums[c][:, :TK] + later))
                if sub == diagonal_sub:
                    w = jnp.where(below_diagonal, w, 0.0)
                w_ref[c] = w.astype(BF16)
                r_ref[h, rows, :] = later + sums[c][:, TK:]

        store_scores(z_even, base + NSUB - 1, [NSUB - 1])
        buffers = (z_even, z_odd)
        for j in reversed(range(NSUB)):
            subs = all_subs[j:]
            step(base + j, buffers[0], buffers[1], subs, j, all_subs[j + 1:], all_subs[j - 1:] if j else all_subs)
            buffers = buffers[::-1]
        assert buffers[0] is z_even

        @pl.loop(0, base // 2)
        def _(pair):
            block = base - 1 - 2 * pair
            step(block, z_even, z_odd, all_subs, None, all_subs, all_subs)
            step(block - 1, z_odd, z_even, all_subs, None, all_subs, all_subs)

        add_weighted_values(0, all_subs)
        o_ref[...] = acc_ref[...].astype(o_ref.dtype)
        if phases is not None:
            pl.when(jnp.logical_and(p == NPAIR - 1, i == n_steps - 1))(phases[2])

    kwargs, operands = _with_comm(
        comm, [Q_ROWS_SPEC, K_ALL_SPEC, V_ALL_SPEC], [PAIR_ROWS_SPEC, PAIR_TOTAL_SPEC],
        [jax.ShapeDtypeStruct((S, NH * HD), BF16), jax.ShapeDtypeStruct((NH, S, TK), F32)], [qkv, qkv, qkv],
        [pltpu.VMEM((TQ, LANES), F32), pltpu.VMEM((2 * NSUB, RS, TK), F32), pltpu.VMEM((2 * NSUB, RS, TK), F32),
         pltpu.VMEM((2 * NSUB, RS, TK), BF16)])
    return pl.pallas_call(
        body, name=name, grid=(NPAIR, n_steps),
        compiler_params=_cparams(("arbitrary", "arbitrary")), **kwargs,
    )(*operands)


def attn_bwd(qkv, dout, totals, name, comm=None):
    n_steps = S // TQ

    def body(*refs):
        ((q_ref, k_ref, v_ref, do_ref, r_ref), (dq_ref, dk_ref, dv_ref),
         (z_even, z_odd, dw_even, dw_odd, before_ref, dbefore_ref, dz_ref, w_ref), phases) = _comm_hooks(
            comm, refs, 5, 3, 8)
        p = pl.program_id(0)
        i = pl.program_id(1)
        if phases is not None:
            pl.when(jnp.logical_and(p == 0, i == 0))(phases[0])
            pl.when(jnp.logical_and(p == NPAIR - 1, i == n_steps - 2))(phases[1])

        @pl.when(i == 0)
        def _():
            dk_ref[...] = jnp.zeros_like(dk_ref)
            dv_ref[...] = jnp.zeros_like(dv_ref)

        chains = [(sub, h) for sub in range(NSUB) for h in range(2)]
        nch = len(chains)
        qb = q_ref[...]
        dob = do_ref[...].astype(BF16)
        q_sub = [_head_halves(qb[sub * RS:(sub + 1) * RS] * SCALE) for sub in range(NSUB)]
        do_sub = [_head_halves(dob[sub * RS:(sub + 1) * RS]) for sub in range(NSUB)]
        upto = _tri_and_ones("upto")
        before_tri = _tri_and_ones("before")
        below_diagonal = (lax.broadcasted_iota(jnp.int32, (RS, TK), 1)
                          < lax.broadcasted_iota(jnp.int32, (RS, TK), 0))
        contract_lanes = (((1,), (1,)), ((), ()))
        contract_rows = (((0,), (0,)), ((), ()))
        base = i * NSUB
        all_subs = list(range(NSUB))

        def key_rows(block):
            return pl.ds(pl.multiple_of(block * TK, TK), TK)

        def store_products(bufs, block, subs):
            z_ref, dw_ref = bufs
            kb = k_ref[key_rows(block), :]
            vb = v_ref[key_rows(block), :]
            for c, (sub, h) in enumerate(chains):
                if sub in subs:
                    z_ref[c] = lax.dot_general(q_sub[sub][h], kb, contract_lanes, preferred_element_type=F32)
                    dw_ref[c] = lax.dot_general(do_sub[sub][h], vb, contract_lanes, preferred_element_type=F32)

        def add_gradients(block, subs):
            kb = k_ref[key_rows(block), :]
            for sub in subs:
                rows = pl.ds(sub * RS, RS)
                dq_ref[rows, :] += _join_heads(*[jnp.dot(dz_ref[h, rows, :], kb, preferred_element_type=F32)
                                                 for h in range(2)])
            dk_ref[key_rows(block), :] += _join_heads(*[
                lax.dot_general(dz_ref[h], qb, contract_rows, preferred_element_type=F32) for h in range(2)])
            dv_ref[key_rows(block), :] += _join_heads(*[
                lax.dot_general(w_ref[h], dob, contract_rows, preferred_element_type=F32) for h in range(2)])

        for ref in (dq_ref, before_ref, dbefore_ref, dz_ref, w_ref):
            ref[...] = jnp.zeros_like(ref)
        even, odd = (z_even, dw_even), (z_odd, dw_odd)
        store_products(even, 0, all_subs)

        def step(block, bufs, next_bufs, subs, diagonal_sub, prev_subs, next_subs):
            z_ref, dw_ref = bufs
            add_gradients(jnp.maximum(block - 1, 0), prev_subs)
            for sub in prev_subs:
                if sub not in subs:
                    dz_ref[:, pl.ds(sub * RS, RS), :] = jnp.zeros((2, RS, TK), BF16)
                    w_ref[:, pl.ds(sub * RS, RS), :] = jnp.zeros((2, RS, TK), BF16)
            if next_subs:
                store_products(next_bufs, block + 1, next_subs)
            active = [(c, sub, h) for c, (sub, h) in enumerate(chains) if sub in subs]
            ls, sums, dl, dsums = {}, {}, {}, {}
            for c, sub, h in active:
                ls[c] = _log_stay(z_ref[c])
                sums[c] = _dot_hilo(jnp.where(below_diagonal, ls[c], 0.0) if sub == diagonal_sub else ls[c], upto)
            for c, sub, h in active:
                rows = pl.ds(sub * RS, RS)
                before = before_ref[c]
                log_after = r_ref[h, rows, :] - (sums[c][:, :TK] + before)
                w = jnp.exp((z_ref[c] + ls[c]) + log_after)
                if sub == diagonal_sub:
                    w = jnp.where(below_diagonal, w, 0.0)
                dl[c] = dw_ref[c] * w
                dsums[c] = _dot_hilo(dl[c], before_tri)
                w_ref[h, rows, :] = w.astype(BF16)
                before_ref[c] = before + sums[c][:, TK:]
            for c, sub, h in active:
                rows = pl.ds(sub * RS, RS)
                dbefore = dbefore_ref[c]
                beta = jnp.exp(z_ref[c] + ls[c])
                if sub == diagonal_sub:
                    beta = jnp.where(below_diagonal, beta, 0.0)
                dstay = dsums[c][:, :TK] + dbefore
                dz_ref[h, rows, :] = ((dl[c] * (1.0 - beta) - beta * dstay) * SCALE).astype(BF16)
                dbefore_ref[c] = dbefore + dsums[c][:, TK:]

        @pl.loop(0, base // 2)
        def _(pair):
            step(2 * pair, even, odd, all_subs, None, all_subs, all_subs)
            step(2 * pair + 1, odd, even, all_subs, None, all_subs, all_subs)

        bufs = (even, odd)
        for j in range(NSUB):
            step(base + j, bufs[0], bufs[1], all_subs[j:], j, all_subs[j - 1:] if j else all_subs, all_subs[j + 1:])
            bufs = bufs[::-1]

        add_gradients(base + NSUB - 1, all_subs[NSUB - 1:])
        if phases is not None:
            pl.when(jnp.logical_and(p == NPAIR - 1, i == n_steps - 1))(phases[2])

    full = jax.ShapeDtypeStruct((S, NH * HD), F32)
    kwargs, operands = _with_comm(
        comm, [Q_ROWS_SPEC, K_ALL_SPEC, V_ALL_SPEC, PAIR_ROWS_SPEC, PAIR_TOTAL_SPEC],
        [PAIR_ROWS_SPEC, PAIR_ALL_SPEC, PAIR_ALL_SPEC], [full, full, full], [qkv, qkv, qkv, dout, totals],
        [pltpu.VMEM((2 * NSUB, RS, TK), F32)] * 6 + [pltpu.VMEM((2, TQ, TK), BF16)] * 2)
    return pl.pallas_call(
        body, name=name, grid=(NPAIR, n_steps),
        compiler_params=_cparams(("arbitrary", "arbitrary")), **kwargs,
    )(*operands)


def _proj_cols(first_col):
    base = first_col // LANES
    return pl.BlockSpec((S, LANES), lambda j: (0, base + j))


CONV_OUT_SPEC = pl.BlockSpec((S, LANES), lambda j: (0, j))
CONV_DOUT_SPEC = pl.BlockSpec((S, LANES), lambda j: (0, (NH * HD) // LANES + j))
CONV_W_SPEC = pl.BlockSpec((8, LANES), lambda j: (0, j))
CONV_B_SPEC = pl.BlockSpec((1, LANES), lambda j: (0, j))


def _shift_down(u, n):
    rows = lax.broadcasted_iota(jnp.int32, u.shape, 0)
    return jnp.where(rows >= n, pltpu.roll(u, n, 0), 0.0)


def _shift_up(u, n):
    rows = lax.broadcasted_iota(jnp.int32, u.shape, 0)
    return jnp.where(rows < S - n, pltpu.roll(u, S - n, 0), 0.0)


def conv_fwd(proj, cw8, cb, name):
    def body(bg_ref, cg_ref, hc_ref, w_ref, b_ref, o_ref):
        u = cg_ref[...] * hc_ref[...]
        w = w_ref[...]
        y = w[0:1, :] * _shift_down(u, 2) + w[1:2, :] * _shift_down(u, 1) + w[2:3, :] * u + b_ref[...]
        o_ref[...] = bg_ref[...] * y

    return pl.pallas_call(
        body, name=name, grid=(CW // LANES,),
        in_specs=[_proj_cols(0), _proj_cols(CW), _proj_cols(2 * CW), CONV_W_SPEC, CONV_B_SPEC],
        out_specs=CONV_OUT_SPEC, out_shape=jax.ShapeDtypeStruct((S, CW), F32),
        compiler_params=_cparams(("parallel",)),
    )(proj, proj, proj, cw8, cb)


def conv_bwd(proj, dout, cw8, cb, name):
    def body(bg_ref, cg_ref, hc_ref, do_ref, w_ref, b_ref, dbg_ref, dcg_ref, dhc_ref, dw_ref, db_ref):
        cg, hc, do = cg_ref[...], hc_ref[...], do_ref[...]
        w = w_ref[...]
        u = cg * hc
        u1, u2 = _shift_down(u, 1), _shift_down(u, 2)
        y = w[0:1, :] * u2 + w[1:2, :] * u1 + w[2:3, :] * u + b_ref[...]
        dbg_ref[...] = do * y
        dy = do * bg_ref[...]
        db_ref[...] = jnp.sum(dy, axis=0, keepdims=True)
        dw_ref[...] = jnp.concatenate(
            [jnp.sum(dy * u2, axis=0, keepdims=True), jnp.sum(dy * u1, axis=0, keepdims=True),
             jnp.sum(dy * u, axis=0, keepdims=True), jnp.zeros((5, LANES), F32)], axis=0)
        du = w[2:3, :] * dy + w[1:2, :] * _shift_up(dy, 1) + w[0:1, :] * _shift_up(dy, 2)
        dcg_ref[...] = du * hc
        dhc_ref[...] = du * cg

    full = jax.ShapeDtypeStruct((S, CW), F32)
    return pl.pallas_call(
        body, name=name, grid=(CW // LANES,),
        in_specs=[_proj_cols(0), _proj_cols(CW), _proj_cols(2 * CW), CONV_DOUT_SPEC, CONV_W_SPEC, CONV_B_SPEC],
        out_specs=[CONV_OUT_SPEC, CONV_OUT_SPEC, CONV_OUT_SPEC, CONV_W_SPEC, CONV_B_SPEC],
        out_shape=[full, full, full, jax.ShapeDtypeStruct((8, CW), F32), jax.ShapeDtypeStruct((1, CW), F32)],
        compiler_params=_cparams(("parallel",)),
    )(proj, proj, proj, dout, cw8, cb)


GELU_K = math.sqrt(2.0 / math.pi)
GELU_C = 0.044715


def _gelu(x):
    return 0.5 * x * (1.0 + jnp.tanh(GELU_K * (x + GELU_C * (x * x * x))))


def _gelu_grad(x):
    t = jnp.tanh(GELU_K * (x + GELU_C * (x * x * x)))
    return 0.5 * (1.0 + t) + 0.5 * x * (1.0 - t * t) * (GELU_K * (1.0 + 3.0 * GELU_C * (x * x)))


def _sg_masks():
    row = lax.broadcasted_iota(jnp.int32, (T, T), 0)
    col = lax.broadcasted_iota(jnp.int32, (T, T), 1)
    causal = jnp.right_shift(row, 6) >= jnp.right_shift(col, 6)
    head_of_col = jnp.right_shift(lax.broadcasted_iota(jnp.int32, (T, CW), 1), 6)
    return causal, head_of_col


def _sg_mixed(vnb, sw_ref, bias, causal, head_of_col):
    mixed = bias
    for h in range(SG_HEADS):
        wh = jnp.where(causal, sw_ref[h], 0.0).astype(BF16)
        mh = jnp.dot(wh, vnb, preferred_element_type=F32)
        mixed = mixed + jnp.where(head_of_col == h, mh, 0.0)
    return mixed


SG_U_SPEC = pl.BlockSpec((T, CW), lambda n: (n, 3))
SG_V_SPEC = pl.BlockSpec((T, CW), lambda n: (n, 4))
SG_ROW_SPEC = pl.BlockSpec((T, CW), lambda n: (n, 0))
SG_DOUT_SPEC = pl.BlockSpec((T, CW), lambda n: (n, 3))
SG_G_SPEC = pl.BlockSpec((1, CW), lambda n: (0, 0))
SG_W_SPEC = pl.BlockSpec((SG_HEADS, T, T), lambda n: (0, 0, 0))
SG_BIAS_SPEC = pl.BlockSpec((T, CW), lambda n: (0, 0))


def sg_fwd(proj, gn, sw, bias, name):
    def body(u_ref, v_ref, g_ref, sw_ref, bias_ref, o_ref):
        causal, head_of_col = _sg_masks()
        gv = _gelu(v_ref[...])
        rstd = lax.rsqrt(jnp.mean(gv * gv, axis=-1, keepdims=True) + EPS)
        vnb = ((gv * rstd) * g_ref[...]).astype(BF16)
        mixed = _sg_mixed(vnb, sw_ref, bias_ref[...], causal, head_of_col)
        o_ref[...] = _gelu(u_ref[...]) * mixed

    return pl.pallas_call(
        body, name=name, grid=(S // T,),
        in_specs=[SG_U_SPEC, SG_V_SPEC, SG_G_SPEC, SG_W_SPEC, SG_BIAS_SPEC],
        out_specs=SG_ROW_SPEC, out_shape=jax.ShapeDtypeStruct((S, CW), F32),
        compiler_params=_cparams(("parallel",)),
    )(proj, proj, gn, sw, bias)


def sg_bwd(proj, dout, gn, sw, bias, name):
    def body(u_ref, v_ref, do_ref, g_ref, sw_ref, bias_ref, du_ref, dv_ref, dg_ref, dsw_ref, dbias_ref):
        @pl.when(pl.program_id(0) == 0)
        def _():
            dg_ref[...] = jnp.zeros_like(dg_ref)
            dsw_ref[...] = jnp.zeros_like(dsw_ref)
            dbias_ref[...] = jnp.zeros_like(dbias_ref)

        causal, head_of_col = _sg_masks()
        uv, vv, do, gnv = u_ref[...], v_ref[...], do_ref[...], g_ref[...]
        gv = _gelu(vv)
        rstd = lax.rsqrt(jnp.mean(gv * gv, axis=-1, keepdims=True) + EPS)
        xhat = gv * rstd
        vnb = (xhat * gnv).astype(BF16)
        mixed = _sg_mixed(vnb, sw_ref, bias_ref[...], causal, head_of_col)
        du_ref[...] = (do * mixed) * _gelu_grad(uv)
        dmix = do * _gelu(uv)
        dbias_ref[...] += dmix
        dmixb = dmix.astype(BF16)
        dvn = jnp.zeros((T, CW), F32)
        for h in range(SG_HEADS):
            wh = jnp.where(causal, sw_ref[h], 0.0).astype(BF16)
            dvh = lax.dot_general(wh, dmixb, (((0,), (0,)), ((), ())), preferred_element_type=F32)
            dvn = dvn + jnp.where(head_of_col == h, dvh, 0.0)
            dmh = jnp.where(head_of_col == h, dmixb, jnp.zeros_like(dmixb))
            dwh = lax.dot_general(dmh, vnb, (((1,), (1,)), ((), ())), preferred_element_type=F32)
            dsw_ref[h] += jnp.where(causal, dwh, 0.0)
        dg_ref[...] += jnp.sum(dvn * xhat, axis=0, keepdims=True)
        dxhat = dvn * gnv
        dgv = rstd * (dxhat - xhat * jnp.mean(dxhat * xhat, axis=-1, keepdims=True))
        dv_ref[...] = dgv * _gelu_grad(vv)

    full = jax.ShapeDtypeStruct((S, CW), F32)
    return pl.pallas_call(
        body, name=name, grid=(S // T,),
        in_specs=[SG_U_SPEC, SG_V_SPEC, SG_DOUT_SPEC, SG_G_SPEC, SG_W_SPEC, SG_BIAS_SPEC],
        out_specs=[SG_ROW_SPEC, SG_ROW_SPEC, SG_G_SPEC, SG_W_SPEC, SG_BIAS_SPEC],
        out_shape=[full, full, jax.ShapeDtypeStruct((1, CW), F32),
                   jax.ShapeDtypeStruct((SG_HEADS, T, T), F32), jax.ShapeDtypeStruct((T, CW), F32)],
        compiler_params=_cparams(("arbitrary",)),
    )(proj, proj, dout, gn, sw, bias)


ADA_COLS = NMOD * D // NDEV


def ada_fwd(c_all, ada_w, ada_b_mine, name):
    def body(c_ref, w_ref, b_ref, o_ref, ca_ref):
        cv = c_ref[...]
        ca = cv * (1.0 / (1.0 + jnp.exp(-cv)))
        ca_ref[...] = ca
        cab = ca.astype(BF16)
        for l in range(L):
            o_ref[l] = jnp.dot(cab, w_ref[l].astype(BF16), preferred_element_type=F32) + b_ref[l]

    return pl.pallas_call(
        body, name=name,
        out_shape=[jax.ShapeDtypeStruct((L, NDEV, ADA_COLS), F32), jax.ShapeDtypeStruct((NDEV, D), F32)],
        compiler_params=_cparams(),
    )(c_all, ada_w, ada_b_mine)


def ada_bwd(ca, dmod_cols, name):
    def body(ca_ref, dm_ref, o_ref):
        cab = ca_ref[...].astype(BF16)
        for l in range(L):
            o_ref[l] = lax.dot_general(cab, dm_ref[l].astype(BF16), (((0,), (0,)), ((), ())),
                                       preferred_element_type=F32)

    return pl.pallas_call(
        body, name=name, out_shape=jax.ShapeDtypeStruct((L, D, ADA_COLS), F32),
        compiler_params=_cparams(),
    )(ca, dmod_cols)


def _adamw(w, g, m, v):
    m = B1 * m + (1.0 - B1) * g
    v = B2 * v + (1.0 - B2) * (g * g)
    m_hat = m / BC1
    v_hat = v / BC2
    delta = -LR * (m_hat / (jnp.sqrt(v_hat) + AEPS) + WD * w)
    return delta, m, v


VEC_ROWS_PER_LAYER = 8
VEC_FINAL_ROW = L * VEC_ROWS_PER_LAYER
VEC_ROWS = VEC_FINAL_ROW + 8
W256_TAPS, W256_CONV_B, W256_GN = 0, 8, 9
W256_ROWS_PER_LAYER = 16


def small_update(vec_all, w256_all, sb_all, sw_all, params, name):
    n_par = len(params)

    def body(*refs):
        vec_ref, w256_ref, sb_ref = refs[:3]
        sw_refs = refs[3:3 + L]
        par_refs = [refs[3 + L + 3 * k:3 + L + 3 * k + 3] for k in range(n_par)]
        out = refs[3 + L + 3 * n_par:]
        out_par = [out[4 * k:4 * k + 4] for k in range(n_par)]
        loss_ref, taps_ref = out[4 * n_par:]

        def total(ref, idx):
            acc = ref[(0,) + idx].astype(F32)
            for d in range(1, NDEV):
                acc = acc + ref[(d,) + idx].astype(F32)
            return acc

        def update(k, region, g):
            w_ref, m_ref, v_ref = par_refs[k]
            g_ref, d_ref, nm_ref, nv_ref = out_par[k]
            delta, nm, nv = _adamw(w_ref[region], g, m_ref[region], v_ref[region])
            g_ref[region] = g
            d_ref[region] = delta
            nm_ref[region] = nm
            nv_ref[region] = nv

        for l in range(L):
            base = l * VEC_ROWS_PER_LAYER
            for k in range(NMOD):
                update(0, (slice(l, l + 1), slice(k * D, (k + 1) * D)), total(vec_ref, (slice(base + k, base + k + 1),)))
            update(1, (slice(l, l + 1),), total(vec_ref, (slice(base + 6, base + 7),)))
            update(2, (slice(l, l + 1),), total(vec_ref, (slice(base + 7, base + 8),)))
            wbase = l * W256_ROWS_PER_LAYER
            update(4, (slice(l, l + 1),), total(w256_ref, (slice(wbase + W256_CONV_B, wbase + W256_CONV_B + 1),)))
            update(5, (slice(l, l + 1),), total(w256_ref, (slice(wbase + W256_GN, wbase + W256_GN + 1),)))
            update(6, (l,), total(sw_refs[l], ()))
            update(7, (l,), total(sb_ref, (slice(l * SG_HEADS, (l + 1) * SG_HEADS),)))
            taps_ref[l] = total(w256_ref, (slice(wbase + W256_TAPS, wbase + W256_TAPS + 8),))
        update(3, (slice(0, 1),), total(vec_ref, (slice(VEC_FINAL_ROW, VEC_FINAL_ROW + 1),)))
        loss_ref[...] = total(vec_ref, (slice(VEC_FINAL_ROW + 1, VEC_FINAL_ROW + 2), slice(0, LANES)))

    out_shape = []
    for w, _, _ in params:
        out_shape += [jax.ShapeDtypeStruct(w.shape, F32)] * 4
    out_shape += [jax.ShapeDtypeStruct((1, LANES), F32), jax.ShapeDtypeStruct((L, 8, CW), F32)]
    outs = pl.pallas_call(body, name=name, out_shape=out_shape, compiler_params=_cparams())(
        vec_all, w256_all, sb_all, *sw_all, *[a for p in params for a in p])
    return [outs[4 * k:4 * k + 4] for k in range(n_par)], outs[4 * n_par:]


def adamw_plain(w, g, m, v, tr, name):
    rows, cols = w.shape
    spec = pl.BlockSpec((tr, cols), lambda i: (i, 0))

    def body(w_ref, g_ref, m_ref, v_ref, d_ref, nm_ref, nv_ref):
        delta, nm, nv = _adamw(w_ref[...], g_ref[...], m_ref[...], v_ref[...])
        d_ref[...] = delta
        nm_ref[...] = nm
        nv_ref[...] = nv

    shp = jax.ShapeDtypeStruct((rows, cols), F32)
    return pl.pallas_call(
        body, name=name, grid=(rows // tr,), in_specs=[spec] * 4, out_specs=[spec] * 3,
        out_shape=[shp, shp, shp], compiler_params=_cparams(("parallel",)),
    )(w, g, m, v)


def adamw_reduce(w, parts, m, v, tr, name, tie=None):
    _, rows, cols = w.shape
    spec = pl.BlockSpec((None, tr, cols), lambda l, i: (l, i, 0))
    pspecs = [pl.BlockSpec((NDEV, tr, cols), lambda l, i, k=k: (0, jnp.where(l == k, i, 0), 0)) for k in range(L)]

    ties = [] if tie is None else [tie]

    def body(w_ref, p0_ref, p1_ref, m_ref, v_ref, *rest):
        g_ref, d_ref, nm_ref, nv_ref = rest[len(ties):]
        first_layer = pl.program_id(0) == 0
        g = jnp.zeros((tr, cols), F32)
        for d in range(NDEV):
            g = g + jnp.where(first_layer, p0_ref[d], p1_ref[d]).astype(F32)
        delta, nm, nv = _adamw(w_ref[...], g, m_ref[...], v_ref[...])
        g_ref[...] = g
        d_ref[...] = delta
        nm_ref[...] = nm
        nv_ref[...] = nv

    shp = jax.ShapeDtypeStruct(w.shape, F32)
    return pl.pallas_call(
        body, name=name, grid=(L, rows // tr),
        in_specs=[spec] + pspecs + [spec, spec] + [pl.BlockSpec(t.shape, lambda l, i: (0, 0)) for t in ties],
        out_specs=[spec] * 4, out_shape=[shp] * 4, compiler_params=_cparams(("parallel", "parallel")),
    )(w, *parts, m, v, *ties)


def _pad_rows(flat, rows):
    return jnp.pad(flat, (0, rows * LANES - flat.shape[0])).reshape(rows, LANES)


def kernel(x, c, ada_w, ada_b, norm_mix_g, norm_mlp_g, w_in, conv_w, conv_b, gmlp_norm_g, spatial_w, spatial_b, w_out, mlp_w1, mlp_w2, final_norm_g, loss_target, m_ada_w, m_ada_b, m_norm_mix_g, m_norm_mlp_g, m_w_in, m_conv_w, m_conv_b, m_gmlp_norm_g, m_spatial_w, m_spatial_b, m_w_out, m_mlp_w1, m_mlp_w2, m_final_norm_g, v_ada_w, v_ada_b, v_norm_mix_g, v_norm_mlp_g, v_w_in, v_conv_w, v_conv_b, v_gmlp_norm_g, v_spatial_w, v_spatial_b, v_w_out, v_mlp_w1, v_mlp_w2, v_final_norm_g):
    me = _lin(_my_pos())
    x0 = x[0]
    target = loss_target[0]
    conv_shard = conv_w.shape[-1]

    w_in_b, w_out_b, w1_b, w2_b = [w.astype(BF16) for w in (w_in, w_out, mlp_w1, mlp_w2)]
    pack0 = _pad_rows(jnp.concatenate([c.reshape(-1), conv_w.reshape(-1)]), 16)
    g0, gw_in0 = run_comm(Gather([pack0, w_in_b[0]]), "gather_first")
    g0 = g0.reshape(NDEV, 16 * LANES)
    c_all = g0[:, :D]
    conv_full = (g0[:, D:D + L * 3 * conv_shard].reshape(NDEV, L, 3, conv_shard)
                 .transpose(1, 2, 0, 3).reshape(L, 3, CW))

    def canonical_w_in(gathered):
        return gathered.transpose(1, 0, 2).reshape(D, PROJ)

    W_in = [canonical_w_in(gw_in0), None]
    W_out, W1, W2 = [None] * L, [None] * L, [None] * L
    early_weights, token = start_copies([w_out_b[0], w1_b[0]], me, "gather_early0_start", True, after=g0)
    c_all = tied(c_all, token)

    ada_b_mine = lax.dynamic_slice(ada_b, (0, me * ADA_COLS), (L, ADA_COLS)).reshape(L, 1, ADA_COLS)
    mod_part, c_act = ada_fwd(c_all, ada_w, ada_b_mine, "ada_fwd")
    gmod = run_comm(Gather([mod_part]), "gather_mod")[0]
    mod = lax.dynamic_index_in_dim(gmod, me, axis=2, keepdims=False)
    mod = mod.transpose(1, 0, 2).reshape(L, NMOD, 1, D)

    cw8 = jnp.pad(conv_full, ((0, 0), (0, 5), (0, 0)))
    sg_bias = jnp.repeat(spatial_b.transpose(0, 2, 1), HD, axis=2)

    saved = []
    xl = x0
    for l in range(L):
        sh_m, sc_m, g_m, sh_f, sc_f, g_f = [mod[l, k] for k in range(NMOD)]
        h1 = normmod_fwd(xl, norm_mix_g[l:l + 1], sc_m, sh_m, f"norm_mix_fwd{l}")
        if l > 0:
            W_in[l] = canonical_w_in(finish_copies(w_in_handle, xl, f"gather_w_in{l}_wait")[0])
        qkv = mm_layer("proj_qkv", l, h1, W_in[l], out_dtypes=[BF16], cols=(0, QKV))[0]
        proj = mm_layer("proj_rest", l, h1, W_in[l], out_dtypes=[F32], cols=(QKV, REST))[0]
        a_out, a_tot, gw2 = attn_fwd(qkv, f"attn_fwd{l}", comm=Gather([w2_b[l]]))
        gw_out, gw1 = finish_copies(early_weights, a_out, f"gather_early{l}_wait")
        W_out[l] = gw_out.reshape(D, D)
        W1[l] = gw1
        W2[l] = gw2.reshape(DFF, D)
        if l + 1 < L:
            w_in_handle, token = start_copies([w_in_b[l + 1]], me, f"gather_w_in{l + 1}_start", True, after=a_out)
            early_weights, token = start_copies([w_out_b[l + 1], w1_b[l + 1]], me, f"gather_early{l + 1}_start", True,
                                                after=token)
            g_m = tied(g_m, token)
        c_out = conv_fwd(proj, cw8[l], conv_b[l:l + 1], f"conv_fwd{l}")
        s_out = sg_fwd(proj, gmlp_norm_g[l:l + 1], spatial_w[l], sg_bias[l], f"sg_fwd{l}")
        cat = jnp.concatenate([a_out, c_out.astype(BF16), s_out.astype(BF16)], axis=1)
        mix, x1 = mm_layer("mix", l, cat, W_out[l], out_dtypes=[F32, F32],
                           epilogue=lambda acc, xr, g: (acc, xr + g * acc), extras=[(xl, "tile"), (g_m, "col")])
        h2 = normmod_fwd(x1, norm_mlp_g[l:l + 1], sc_f, sh_f, f"norm_mlp_fwd{l}")
        ra, r = mm_layer("mlp_up", l, h2, W1[l], out_dtypes=[BF16, BF16], b_blocks=True,
                         epilogue=lambda acc: (jnp.maximum(acc, 0.0), jnp.square(jnp.maximum(acc, 0.0))))
        m2, x2 = mm_layer("mlp_down", l, r, W2[l], out_dtypes=[F32, F32],
                          epilogue=lambda acc, xr, g: (acc, xr + g * acc), extras=[(x1, "tile"), (g_f, "col")])
        saved.append(dict(x=xl, h1=h1, proj=proj, qkv=qkv, a_tot=a_tot, cat=cat, mix=mix,
                          x1=x1, h2=h2, ra=ra, r=r, m2=m2))
        xl = x2

    dx, loss_part, d_final_g = loss_head(xl, target, final_norm_g.reshape(1, D), "loss_head")

    p_in, p_out, p_w1, p_w2 = [None] * L, [None] * L, [None] * L, [None] * L
    w_in_grads = [None] * L
    vec_rows, d_norm_mix, d_norm_mlp = [None] * L, [None] * L, [None] * L
    dcw8, d_conv_b, d_gn, d_sw, d_sb = [None] * L, [None] * L, [None] * L, [None] * L, [None] * L
    late_grads = [None] * L
    for l in reversed(range(L)):
        sv = saved[l]
        sh_m, sc_m, g_m, sh_f, sc_f, g_f = [mod[l, k] for k in range(NMOD)]
        dm2, dg_f = gate_bwd(dx, sv["m2"], g_f, f"gate_mlp_bwd{l}")
        da = mm_layer("mlp_down_dgrad", l, dm2, W2[l], out_dtypes=[BF16], trans_b=True,
                      epilogue=lambda acc, rav: (acc * (2.0 * rav.astype(F32)),), extras=[(sv["ra"], "tile")])[0]
        dW2 = mm_layer("mlp_down_wgrad", l, sv["r"], dm2, out_dtypes=[BF16], trans_a=True)[0]
        dW1 = mm_layer("mlp_up_wgrad", l, sv["h2"], da, out_dtypes=[BF16], trans_a=True, out_blocks=True)[0]
        dh2 = mm_layer("mlp_up_dgrad", l, da, W1[l], out_dtypes=[F32], trans_b=True, b_blocks=True)[0]
        dx1, dsc_f, dsh_f, d_norm_mlp[l] = normmod_bwd(sv["x1"], dh2, dx, norm_mlp_g[l:l + 1], sc_f,
                                                       f"norm_mlp_bwd{l}")
        dmix, dg_m = gate_bwd(dx1, sv["mix"], g_m, f"gate_mix_bwd{l}")
        dcat = mm_layer("mix_dgrad", l, dmix, W_out[l], out_dtypes=[F32], trans_b=True)[0]
        dW_out = mm_layer("mix_wgrad", l, sv["cat"], dmix, out_dtypes=[BF16], trans_a=True)[0]
        pieces_w2, pieces_out = dW2.reshape(NDEV, DFF // NDEV, D), dW_out.reshape(NDEV, D // NDEV, D)
        ride, late = ([pieces_w2, pieces_out], dW1) if l == L - 1 else ([pieces_w2, dW1], pieces_out)
        dq, dk, dv, *arrived = attn_bwd(sv["qkv"], dcat, sv["a_tot"], f"attn_bwd{l}", comm=Exchange(ride))
        p_w2[l] = arrived[0]
        (p_out if l == L - 1 else p_w1)[l] = arrived[1]
        late_grads[l], late_token = start_copies([late], me, f"exchange_late{l}_start", False, after=dq)
        dbg, dcg, dhc, dcw8[l], d_conv_b[l] = conv_bwd(sv["proj"], dcat, cw8[l], conv_b[l:l + 1], f"conv_bwd{l}")
        dus, dvs, d_gn[l], dsw, dbias = sg_bwd(sv["proj"], dcat, gmlp_norm_g[l:l + 1], spatial_w[l], sg_bias[l],
                                               f"sg_bwd{l}")
        d_sw[l] = dsw.astype(BF16)
        d_sb[l] = dbias.reshape(T, SG_HEADS, HD).sum(axis=2).T
        dproj = jnp.concatenate([dq, dk, dv, dbg, dcg, dhc, dus, dvs], axis=1).astype(BF16)
        dW_in = mm_layer("proj_wgrad", l, sv["h1"], dproj, out_dtypes=[BF16], trans_a=True,
                         extras=[(late_token, "tie")])[0]
        pieces = dW_in.reshape(D, NDEV, PROJ // NDEV).transpose(1, 0, 2)
        w_in_grads[l], token = start_copies([pieces], me, f"exchange_w_in{l}_start", False)
        dh1 = mm_layer("proj_dgrad", l, dproj, W_in[l], out_dtypes=[F32], trans_b=True, extras=[(token, "tie")])[0]
        dx, dsc_m, dsh_m, d_norm_mix[l] = normmod_bwd(sv["x"], dh1, dx1, tied(norm_mix_g[l:l + 1], token), sc_m,
                                                      f"norm_mix_bwd{l}")
        vec_rows[l] = [dsh_m, dsc_m, dg_m, dsh_f, dsc_f, dg_f, d_norm_mix[l], d_norm_mlp[l]]

    grad_x = dx.reshape(1, S, D)

    g_w2, d_w2, nm_w2, nv_w2 = adamw_reduce(mlp_w2, p_w2, m_mlp_w2, v_mlp_w2, 256, "adamw_mlp_w2", tie=token)
    p_w1[L - 1] = finish_copies(late_grads[L - 1], d_w2, f"exchange_late{L - 1}_wait")[0]
    g_w1, d_w1, nm_w1, nv_w1 = adamw_reduce(mlp_w1, p_w1, m_mlp_w1, v_mlp_w1, 256, "adamw_mlp_w1", tie=token)

    vec_pack = jnp.concatenate([row for l in range(L) for row in vec_rows[l]]
                               + [d_final_g, loss_part, jnp.zeros((VEC_ROWS - VEC_FINAL_ROW - 2, D), F32)], axis=0)
    vec_pack, _ = lax.optimization_barrier((vec_pack, (d_w1, d_w2)))
    w256_pack = jnp.concatenate([blk for l in range(L) for blk in (
        dcw8[l], d_conv_b[l], d_gn[l], jnp.zeros((W256_ROWS_PER_LAYER - W256_GN - 1, CW), F32))], axis=0)
    vec_all, w256_all, sb_all, *sw_all = run_comm(
        Gather([vec_pack, w256_pack, jnp.concatenate(d_sb, axis=0)] + d_sw), "gather_small_grads")

    dmod_all = (vec_all[:, :VEC_FINAL_ROW].reshape(NDEV, L, VEC_ROWS_PER_LAYER, D)[:, :, :NMOD]
                .reshape(NDEV, L, NMOD * D))
    dmod_cols = lax.dynamic_slice(dmod_all, (0, 0, me * ADA_COLS), (NDEV, L, ADA_COLS)).transpose(1, 0, 2)
    g_ada_w = ada_bwd(c_act, dmod_cols, "ada_bwd")

    flat2 = lambda t: t.reshape(L * D, ADA_COLS)
    d_ada_w, nm_ada_w, nv_ada_w = [t.reshape(L, D, ADA_COLS) for t in adamw_plain(
        flat2(ada_w), flat2(g_ada_w), flat2(m_ada_w), flat2(v_ada_w), 256, "adamw_ada_w")]

    after = jnp.concatenate([t.reshape(-1)[:1] for t in (d_w1, d_w2, d_ada_w)])
    p_in = [finish_copies(w_in_grads[l], after, f"exchange_w_in{l}_wait")[0] for l in range(L)]
    p_out[0] = finish_copies(late_grads[0], after, "exchange_late0_wait")[0]
    g_w_in, d_w_in, nm_w_in, nv_w_in = adamw_reduce(w_in, p_in, m_w_in, v_w_in, 256, "adamw_w_in")
    g_w_out, d_w_out, nm_w_out, nv_w_out = adamw_reduce(w_out, p_out, m_w_out, v_w_out, 128, "adamw_w_out")

    as_row = lambda t: t.reshape(1, D)
    small_params = [(ada_b, m_ada_b, v_ada_b), (norm_mix_g, m_norm_mix_g, v_norm_mix_g),
                    (norm_mlp_g, m_norm_mlp_g, v_norm_mlp_g),
                    (as_row(final_norm_g), as_row(m_final_norm_g), as_row(v_final_norm_g)),
                    (conv_b, m_conv_b, v_conv_b), (gmlp_norm_g, m_gmlp_norm_g, v_gmlp_norm_g),
                    (spatial_w, m_spatial_w, v_spatial_w), (spatial_b, m_spatial_b, v_spatial_b)]
    updated, (loss_sum, taps_sum) = small_update(vec_all, w256_all, sb_all, sw_all, small_params, "small_update")
    loss = loss_sum[0, 0]
    u_ada_b, u_norm_mix, u_norm_mlp, u_final, u_conv_b, u_gn, u_sw, u_sb = updated
    u_final = [t.reshape(D) for t in u_final]
    g_conv_w = lax.dynamic_slice(taps_sum, (0, 0, me * conv_shard), (L, 3, conv_shard))
    flat_cw = lambda t: t.reshape(L * 3, conv_shard)
    u_conv_w = [g_conv_w] + [t.reshape(L, 3, conv_shard) for t in adamw_plain(
        flat_cw(conv_w), flat_cw(g_conv_w), flat_cw(m_conv_w), flat_cw(v_conv_w), L * 3, "adamw_conv_w")]
    small_sets = [u_ada_b, u_norm_mix, u_norm_mlp, u_conv_w, u_conv_b, u_gn, u_sw, u_sb, u_final]
    small_g, sd, snm, snv = [[u[k] for u in small_sets] for k in range(4)]

    def ordered(big, small):
        ada, win, wout, w1, w2 = big
        return [ada, small[0], small[1], small[2], win, small[3], small[4], small[5], small[6], small[7],
                wout, w1, w2, small[8]]

    grads = ordered([g_ada_w, g_w_in, g_w_out, g_w1, g_w2], small_g)
    deltas = ordered([d_ada_w, d_w_in, d_w_out, d_w1, d_w2], sd)
    new_m = ordered([nm_ada_w, nm_w_in, nm_w_out, nm_w1, nm_w2], snm)
    new_v = ordered([nv_ada_w, nv_w_in, nv_w_out, nv_w1, nv_w2], snv)
    return (loss, grad_x, *grads, *deltas, *new_m, *new_v)
```

```python
import functools
import math

import jax
import jax.numpy as jnp
from jax import lax
from jax.experimental import pallas as pl
from jax.experimental.pallas import tpu as pltpu

F32 = jnp.float32
BF16 = jnp.bfloat16
MESH = pl.DeviceIdType.MESH

S = 2048
D = 1024
L = 2
NDEV = 8
HD = 64
NH = 8
PROJ = 2816
DFF = 4096
NMOD = 6
EPS = 1e-6
T = 128
SG_HEADS = 4
LANES = 128
CW = 256
QKV = 3 * NH * HD
REST = PROJ - QKV

LR, B1, B2, AEPS, WD, STEP = 0.001, 0.9, 0.999, 1e-08, 0.01, 10
BC1 = 1.0 - B1 ** STEP
BC2 = 1.0 - B2 ** STEP

VMEM_LIMIT = 48 * 1024 * 1024

HBM_SPEC = pl.BlockSpec(memory_space=pltpu.HBM)


def _cparams(sem=None):
    return pltpu.CompilerParams(dimension_semantics=sem, vmem_limit_bytes=VMEM_LIMIT)


def _my_pos():
    return lax.axis_index("x"), lax.axis_index("y"), lax.axis_index("c")


def _lin(p):
    return 4 * p[0] + 2 * p[1] + p[2]


class Gather:
    def __init__(self, arrs):
        self.arrs = list(arrs)
        n = len(self.arrs)
        self.out_shape = [jax.ShapeDtypeStruct((NDEV,) + a.shape, a.dtype) for a in self.arrs]
        self.scratch = [pltpu.SemaphoreType.DMA((n, 7)), pltpu.SemaphoreType.DMA((n, 7)),
                        pltpu.SemaphoreType.DMA((n,))]

    def phases(self, ins, outs, sems):
        n = len(self.arrs)
        send_sems, recv_sems, local_sems = sems
        x, y, c = _my_pos()
        me, sibling = (x, y, c), (x, y, 1 - c)
        chips = [(1 - x, y), (x, 1 - y), (1 - x, 1 - y)]

        def copy(a, k, block, to, src=None):
            slot = outs[a].at[_lin(block)]
            return pltpu.make_async_remote_copy(
                src_ref=slot if src is None else src, dst_ref=slot,
                send_sem=send_sems.at[a, k], recv_sem=recv_sems.at[a, k],
                device_id=to, device_id_type=MESH)

        def mine(a):
            return pltpu.make_async_copy(ins[a], outs[a].at[_lin(me)], local_sems.at[a])

        def first(a):
            return [copy(a, 0, me, sibling, src=ins[a])] + [
                copy(a, 1 + j, me, (*chip, c), src=ins[a]) for j, chip in enumerate(chips)]

        def passed(a):
            return [copy(a, 4 + j, (*chip, c), sibling) for j, chip in enumerate(chips)]

        def start():
            for a in range(n):
                mine(a).start()
                for cp in first(a):
                    cp.start()

        def relay():
            for j, chip in enumerate(chips):
                for a in range(n):
                    copy(a, 1 + j, (*chip, c), me).wait_recv()
                    passed(a)[j].start()

        def finish():
            for a in range(n):
                copy(a, 0, sibling, me).wait_recv()
            for j, chip in enumerate(chips):
                for a in range(n):
                    copy(a, 4 + j, (*chip, 1 - c), me).wait_recv()
            for a in range(n):
                for cp in first(a) + passed(a):
                    cp.wait_send()
                mine(a).wait()

        return start, relay, finish


class Exchange:
    def __init__(self, arrs):
        self.arrs = list(arrs)
        n = len(self.arrs)
        self.out_shape = [jax.ShapeDtypeStruct(a.shape, a.dtype) for a in self.arrs]
        self.scratch = [pltpu.SemaphoreType.DMA((n, 7)), pltpu.SemaphoreType.DMA((n, 7)),
                        pltpu.SemaphoreType.DMA((n,))]

    def phases(self, ins, outs, sems):
        n = len(self.arrs)
        send_sems, recv_sems, local_sems = sems
        x, y, c = _my_pos()
        me = (x, y, c)

        def peer(mask):
            return (1 - x if mask & 4 else x, 1 - y if mask & 2 else y, 1 - c if mask & 1 else c)

        def copy(a, mask):
            return pltpu.make_async_remote_copy(
                src_ref=ins[a].at[_lin(peer(mask))], dst_ref=outs[a].at[_lin(me)],
                send_sem=send_sems.at[a, mask - 1], recv_sem=recv_sems.at[a, mask - 1],
                device_id=peer(mask), device_id_type=MESH)

        def arrival(a, mask):
            return pltpu.make_async_remote_copy(
                src_ref=ins[a].at[_lin(me)], dst_ref=outs[a].at[_lin(peer(mask))],
                send_sem=send_sems.at[a, mask - 1], recv_sem=recv_sems.at[a, mask - 1],
                device_id=peer(mask), device_id_type=MESH)

        def mine(a):
            return pltpu.make_async_copy(ins[a].at[_lin(me)], outs[a].at[_lin(me)], local_sems.at[a])

        def start():
            for a in range(n):
                mine(a).start()
            for mask in (4, 2, 6, 1, 5, 3, 7):
                for a in range(n):
                    copy(a, mask).start()

        def relay():
            pass

        def finish():
            for mask in range(1, 8):
                for a in range(n):
                    arrival(a, mask).wait_recv()
            for mask in range(1, 8):
                for a in range(n):
                    copy(a, mask).wait_send()
            for a in range(n):
                mine(a).wait()

        return start, relay, finish


def run_comm(plan, name):
    n = len(plan.arrs)

    def body(*refs):
        start, relay, finish = plan.phases(refs[:n], refs[n:2 * n], refs[2 * n:])
        start()
        relay()
        finish()

    outs = pl.pallas_call(
        body, name=name, out_shape=plan.out_shape,
        in_specs=[HBM_SPEC] * n, out_specs=[HBM_SPEC] * n, scratch_shapes=plan.scratch,
    )(*plan.arrs)
    return list(outs)


SEM_SPEC = pl.BlockSpec(memory_space=pltpu.SEMAPHORE)
DATAFLOW = pltpu.SideEffectType.DATAFLOW_SIDE_EFFECTING


def _peer_copies(src_ref, land_ref, send_sems, recv_sems, first, same_block):
    x, y, c = _my_pos()
    me = (x, y, c)
    sends, arrivals = [], []
    for mask in (4, 2, 6, 1, 5, 3, 7):
        peer = (1 - x if mask & 4 else x, 1 - y if mask & 2 else y, 1 - c if mask & 1 else c)
        sends.append(pltpu.make_async_remote_copy(
            src_ref=src_ref if same_block else src_ref.at[_lin(peer)], dst_ref=land_ref.at[_lin(me)],
            send_sem=send_sems.at[first + mask - 1], recv_sem=recv_sems.at[first + mask - 1], device_id=peer,
            device_id_type=MESH))
        arrivals.append(pltpu.make_async_remote_copy(
            src_ref=src_ref if same_block else src_ref.at[_lin(me)], dst_ref=land_ref.at[_lin(peer)],
            send_sem=send_sems.at[first + mask - 1], recv_sem=recv_sems.at[first + mask - 1], device_id=peer,
            device_id_type=MESH))
    return sends, arrivals


def start_copies(srcs, me, name, same_block, after=None):
    n = len(srcs)
    landings = []
    for src in srcs:
        own = src[None] if same_block else lax.dynamic_index_in_dim(src, me, axis=0, keepdims=True)
        landings.append(lax.dynamic_update_slice(lax.empty((NDEV,) + own.shape[1:], src.dtype), own,
                                                 (me,) + (0,) * (own.ndim - 1)))

    def body(*refs):
        send_sems, recv_sems = refs[-2 * n - 3], refs[-2 * n - 2]
        token = refs[-1]
        for k in range(n):
            sends, _ = _peer_copies(refs[2 * k], refs[2 * k + 1], send_sems, recv_sems, 7 * k, same_block)
            for cp in sends:
                cp.start()
        token[...] = jnp.zeros_like(token)

    hbm = lambda a: pltpu.HBM(a.shape, a.dtype)
    pairs = [a for pair in zip(srcs, landings) for a in pair]
    extra = [] if after is None else [after]
    sems = pltpu.SemaphoreType.DMA((7 * n,))
    send_sems, recv_sems, *thru, token = pl.pallas_call(
        body, name=name,
        out_shape=(sems, sems, *[hbm(a) for a in pairs], jax.ShapeDtypeStruct((8, LANES), F32)),
        in_specs=[HBM_SPEC] * (2 * n) + [pl.BlockSpec(memory_space=pl.ANY)] * len(extra),
        out_specs=(SEM_SPEC, SEM_SPEC, *[HBM_SPEC] * (2 * n), pl.BlockSpec(memory_space=pltpu.VMEM)),
        input_output_aliases={k: 2 + k for k in range(2 * n)},
        compiler_params=pltpu.CompilerParams(has_side_effects=DATAFLOW),
    )(*[pltpu.with_memory_space_constraint(a, pltpu.HBM) for a in pairs], *extra)
    return (send_sems, recv_sems, thru, same_block), token


def finish_copies(handle, after, name):
    send_sems, recv_sems, thru, same_block = handle
    n = len(thru) // 2

    def body(*refs):
        send_sems, recv_sems = refs[2 * n], refs[2 * n + 1]
        for k in range(n):
            sends, arrivals = _peer_copies(refs[2 * k], refs[2 * k + 1], send_sems, recv_sems, 7 * k, same_block)
            for cp in sends:
                cp.wait_send()
            for cp in arrivals:
                cp.wait_recv()

    hbm = lambda a: pltpu.HBM(a.shape, a.dtype)
    outs = pl.pallas_call(
        body, name=name, out_shape=tuple(hbm(a) for a in thru),
        in_specs=[HBM_SPEC] * (2 * n) + [SEM_SPEC, SEM_SPEC, pl.BlockSpec(memory_space=pl.ANY)],
        out_specs=tuple([HBM_SPEC] * (2 * n)), input_output_aliases={k: k for k in range(2 * n)},
        compiler_params=pltpu.CompilerParams(has_side_effects=DATAFLOW),
    )(*thru, send_sems, recv_sems, after)
    return [outs[2 * k + 1] for k in range(n)]


def tied(x, token):
    return x + token[0:1, 0:1].astype(x.dtype)


MM_TILES = {
    "proj_qkv": (S, 512), "proj_rest": (S, 256), "mix": (1024, 512), "mlp_up": (S, 512), "mlp_down": (1024, 256),
    "mlp_down_dgrad": (1024, 1024), "mlp_down_wgrad": (1024, 1024), "mlp_up_wgrad": (1024, 512),
    "mlp_up_dgrad": (1024, 512), "mix_dgrad": (1024, 512), "mix_wgrad": (512, 1024),
    "proj_wgrad": (1024, PROJ // 2), "proj_dgrad": (1024, 512),
}


def mm_layer(kind, l, a, b, **kw):
    tm, tn = MM_TILES[kind]
    return mm(a, b, tm=tm, tn=tn, name=f"{kind}{l}", **kw)


def mm(a, b, *, tm, tn, out_dtypes, epilogue=None, extras=(), name, trans_a=False, trans_b=False,
       cols=None, b_blocks=False, out_blocks=False):
    if trans_a:
        kdim, m = a.shape
    else:
        m, kdim = a.shape
    shard = b.shape[-1] if b_blocks else None
    if b_blocks:
        full = (b.shape[1], NDEV * shard)
    else:
        full = b.shape
    first, ncols = cols if cols is not None else (0, full[0] if trans_b else full[1])
    assert full[1 if trans_b else 0] == kdim and m % tm == 0 and ncols % tn == 0 and first % tn == 0
    j0 = first // tn
    if trans_a:
        a_spec = pl.BlockSpec((kdim, tm), lambda i, j: (0, i))
    else:
        a_spec = pl.BlockSpec((tm, kdim), lambda i, j: (i, 0))
    if b_blocks and trans_b:
        b_spec = pl.BlockSpec((NDEV, tn, shard), lambda i, j: (0, j0 + j, 0))
    elif b_blocks:
        assert tn == shard
        b_spec = pl.BlockSpec((None, kdim, tn), lambda i, j: (j0 + j, 0, 0))
    elif trans_b:
        b_spec = pl.BlockSpec((tn, kdim), lambda i, j: (j0 + j, 0))
    else:
        b_spec = pl.BlockSpec((kdim, tn), lambda i, j: (0, j0 + j))
    if out_blocks:
        assert tn * NDEV == ncols
        out_spec = pl.BlockSpec((None, tm, tn), lambda i, j: (j, i, 0))
        out_dims = (NDEV, m, tn)
    else:
        out_spec = pl.BlockSpec((tm, tn), lambda i, j: (i, j))
        out_dims = (m, ncols)
    ex_specs = []
    for arr, kind in extras:
        if kind == "tile":
            ex_specs.append(pl.BlockSpec((tm, tn), lambda i, j: (i, j)))
        elif kind == "col":
            ex_specs.append(pl.BlockSpec((1, tn), lambda i, j: (0, j)))
        else:
            ex_specs.append(pl.BlockSpec(arr.shape, lambda i, j: (0, 0)))
    n_ex, n_out = len(extras), len(out_dtypes)
    used = [k for k, (_, kind) in enumerate(extras) if kind != "tie"]

    def body(a_ref, b_ref, *rest):
        ex_refs, out_refs = rest[:n_ex], rest[n_ex:]
        if trans_a:
            acc = lax.dot_general(a_ref[...], b_ref[...], (((0,), (0,)), ((), ())),
                                  preferred_element_type=F32)
        elif trans_b and b_blocks:
            acc = jnp.zeros((tm, tn), F32)
            for d in range(NDEV):
                acc = acc + lax.dot_general(a_ref[:, d * shard:(d + 1) * shard], b_ref[d],
                                            (((1,), (1,)), ((), ())), preferred_element_type=F32)
        elif trans_b:
            acc = lax.dot_general(a_ref[...], b_ref[...], (((1,), (1,)), ((), ())),
                                  preferred_element_type=F32)
        else:
            acc = jnp.dot(a_ref[...], b_ref[...], preferred_element_type=F32)
        outs = (acc,) if epilogue is None else epilogue(acc, *[ex_refs[k][...] for k in used])
        for o_ref, val in zip(out_refs, outs):
            o_ref[...] = val.astype(o_ref.dtype)

    outs = pl.pallas_call(
        body, name=name, grid=(m // tm, ncols // tn),
        in_specs=[a_spec, b_spec] + ex_specs,
        out_specs=[out_spec for _ in range(n_out)],
        out_shape=[jax.ShapeDtypeStruct(out_dims, dt) for dt in out_dtypes],
        compiler_params=_cparams(("parallel", "parallel")),
    )(a, b, *[arr for arr, _ in extras])
    return list(outs)


TR = 256

ROW_SPEC = pl.BlockSpec((TR, D), lambda i: (i, 0))
VEC_SPEC = pl.BlockSpec((1, D), lambda i: (0, 0))


def normmod_fwd(x, g, sc, sh, name):
    def body(x_ref, g_ref, sc_ref, sh_ref, o_ref):
        xv = x_ref[...]
        rstd = lax.rsqrt(jnp.mean(xv * xv, axis=-1, keepdims=True) + EPS)
        n = (xv * rstd) * g_ref[...]
        o_ref[...] = (n * (1.0 + sc_ref[...]) + sh_ref[...]).astype(o_ref.dtype)

    return pl.pallas_call(
        body, name=name, grid=(S // TR,),
        in_specs=[ROW_SPEC, VEC_SPEC, VEC_SPEC, VEC_SPEC], out_specs=ROW_SPEC,
        out_shape=jax.ShapeDtypeStruct((S, D), BF16),
        compiler_params=_cparams(("parallel",)),
    )(x, g, sc, sh)


def normmod_bwd(x, dh, dres, g, sc, name):
    def body(x_ref, dh_ref, dres_ref, g_ref, sc_ref, dx_ref, dsc_ref, dsh_ref, dg_ref):
        @pl.when(pl.program_id(0) == 0)
        def _():
            dsc_ref[...] = jnp.zeros_like(dsc_ref)
            dsh_ref[...] = jnp.zeros_like(dsh_ref)
            dg_ref[...] = jnp.zeros_like(dg_ref)

        xv, dh = x_ref[...], dh_ref[...]
        gv = g_ref[...]
        rstd = lax.rsqrt(jnp.mean(xv * xv, axis=-1, keepdims=True) + EPS)
        xhat = xv * rstd
        dn = dh * (1.0 + sc_ref[...])
        dxhat = dn * gv
        dx_ref[...] = dres_ref[...] + rstd * (dxhat - xhat * jnp.mean(dxhat * xhat, axis=-1, keepdims=True))
        dsc_ref[...] += jnp.sum(dh * (xhat * gv), axis=0, keepdims=True)
        dsh_ref[...] += jnp.sum(dh, axis=0, keepdims=True)
        dg_ref[...] += jnp.sum(dn * xhat, axis=0, keepdims=True)

    vec_out = jax.ShapeDtypeStruct((1, D), F32)
    return pl.pallas_call(
        body, name=name, grid=(S // TR,),
        in_specs=[ROW_SPEC, ROW_SPEC, ROW_SPEC, VEC_SPEC, VEC_SPEC],
        out_specs=[ROW_SPEC, VEC_SPEC, VEC_SPEC, VEC_SPEC],
        out_shape=[jax.ShapeDtypeStruct((S, D), F32), vec_out, vec_out, vec_out],
        compiler_params=_cparams(("arbitrary",)),
    )(x, dh, dres, g, sc)


def gate_bwd(dx, branch, gate, name):
    def body(dx_ref, br_ref, gate_ref, o_ref, dgate_ref):
        @pl.when(pl.program_id(0) == 0)
        def _():
            dgate_ref[...] = jnp.zeros_like(dgate_ref)

        dxv = dx_ref[...]
        o_ref[...] = (dxv * gate_ref[...]).astype(o_ref.dtype)
        dgate_ref[...] += jnp.sum(dxv * br_ref[...], axis=0, keepdims=True)

    return pl.pallas_call(
        body, name=name, grid=(S // TR,),
        in_specs=[ROW_SPEC, ROW_SPEC, VEC_SPEC], out_specs=[ROW_SPEC, VEC_SPEC],
        out_shape=[jax.ShapeDtypeStruct((S, D), BF16), jax.ShapeDtypeStruct((1, D), F32)],
        compiler_params=_cparams(("arbitrary",)),
    )(dx, branch, gate)


def loss_head(x, target, g, name):
    def body(x_ref, t_ref, g_ref, dx_ref, loss_ref, dg_ref):
        @pl.when(pl.program_id(0) == 0)
        def _():
            loss_ref[...] = jnp.zeros_like(loss_ref)
            dg_ref[...] = jnp.zeros_like(dg_ref)

        xv, gv = x_ref[...], g_ref[...]
        rstd = lax.rsqrt(jnp.mean(xv * xv, axis=-1, keepdims=True) + EPS)
        xhat = xv * rstd
        err = xhat * gv - t_ref[...]
        loss_ref[...] += jnp.sum(err * err) * (0.5 / D)
        dy = err * (1.0 / D)
        dg_ref[...] += jnp.sum(dy * xhat, axis=0, keepdims=True)
        dxhat = dy * gv
        dx_ref[...] = rstd * (dxhat - xhat * jnp.mean(dxhat * xhat, axis=-1, keepdims=True))

    return pl.pallas_call(
        body, name=name, grid=(S // TR,),
        in_specs=[ROW_SPEC, ROW_SPEC, VEC_SPEC],
        out_specs=[ROW_SPEC, VEC_SPEC, VEC_SPEC],
        out_shape=[jax.ShapeDtypeStruct((S, D), F32), jax.ShapeDtypeStruct((1, D), F32),
                   jax.ShapeDtypeStruct((1, D), F32)],
        compiler_params=_cparams(("arbitrary",)),
    )(x, target, g)


TQ = 512
RS = 128
NSUB = TQ // RS
TK = 128


def _dot_hilo(a, tri_twice):
    hi = a.astype(BF16)
    lo = (a - hi.astype(F32)).astype(BF16)
    return jnp.dot(jnp.concatenate([hi, lo], axis=1), tri_twice, preferred_element_type=F32)


def _log_stay(z):
    return -(jnp.maximum(z, 0.0) + jnp.log(1.0 + jnp.exp(-jnp.abs(z))))


def _tri_and_ones(kind):
    row = jnp.bitwise_and(lax.broadcasted_iota(jnp.int32, (2 * TK, 2 * TK), 0), TK - 1)
    col = lax.broadcasted_iota(jnp.int32, (2 * TK, 2 * TK), 1)
    tri = {"after": row > col, "upto": row <= col, "before": row < col}[kind]
    return jnp.logical_or(col >= TK, tri).astype(BF16)


NPAIR = NH // 2
SCALE = HD ** -0.5


def _pair_specs(first_block):
    rows = pl.BlockSpec((TQ, LANES), lambda p, i: (i, first_block + p))
    whole = pl.BlockSpec((S, LANES), lambda p, i: (0, first_block + p))
    return rows, whole


Q_ROWS_SPEC, _ = _pair_specs(0)
_, K_ALL_SPEC = _pair_specs(NPAIR)
_, V_ALL_SPEC = _pair_specs(2 * NPAIR)
PAIR_ROWS_SPEC = pl.BlockSpec((TQ, LANES), lambda p, i: (i, p))
PAIR_ALL_SPEC = pl.BlockSpec((S, LANES), lambda p, i: (0, p))
PAIR_TOTAL_SPEC = pl.BlockSpec((2, TQ, TK), lambda p, i: (p, i, 0))


def _head_halves(x):
    first = lax.broadcasted_iota(jnp.int32, x.shape, 1) < HD
    zero = jnp.zeros_like(x)
    return jnp.where(first, x, zero), jnp.where(first, zero, x)


def _join_heads(a, b):
    return jnp.where(lax.broadcasted_iota(jnp.int32, a.shape, 1) < HD, a, b)


def _comm_hooks(comm, refs, n_in, n_out, n_scratch):
    nc = len(comm.arrs) if comm is not None else 0
    ins, cin = refs[:n_in], refs[n_in:n_in + nc]
    outs = refs[n_in + nc:n_in + nc + n_out]
    cout = refs[n_in + nc + n_out:n_in + 2 * nc + n_out]
    scratch = refs[n_in + 2 * nc + n_out:n_in + 2 * nc + n_out + n_scratch]
    sems = refs[n_in + 2 * nc + n_out + n_scratch:]
    phases = comm.phases(cin, cout, sems) if comm is not None else None
    return ins, outs, scratch, phases


def _with_comm(comm, in_specs, out_specs, out_shape, operands, scratch):
    if comm is None:
        return dict(in_specs=in_specs, out_specs=out_specs, out_shape=out_shape, scratch_shapes=scratch), operands
    nc = len(comm.arrs)
    return dict(in_specs=in_specs + [HBM_SPEC] * nc, out_specs=out_specs + [HBM_SPEC] * nc,
                out_shape=out_shape + comm.out_shape, scratch_shapes=scratch + comm.scratch), operands + comm.arrs


def attn_fwd(qkv, name, comm=None):
    n_steps = S // TQ

    def body(*refs):
        (q_ref, k_ref, v_ref), (o_ref, r_ref), (acc_ref, z_even, z_odd, w_ref), phases = _comm_hooks(
            comm, refs, 3, 2, 4)
        p = pl.program_id(0)
        i = pl.program_id(1)
        if phases is not None:
            pl.when(jnp.logical_and(p == 0, i == 0))(phases[0])
            pl.when(jnp.logical_and(p == NPAIR - 1, i == n_steps - 2))(phases[1])
        chains = [(sub, h) for sub in range(NSUB) for h in range(2)]
        q_sub = [_head_halves(q_ref[pl.ds(sub * RS, RS), :] * SCALE) for sub in range(NSUB)]
        after = _tri_and_ones("after")
        below_diagonal = (lax.broadcasted_iota(jnp.int32, (RS, TK), 1)
                          < lax.broadcasted_iota(jnp.int32, (RS, TK), 0))
        base = i * NSUB
        all_subs = list(range(NSUB))

        acc_ref[...] = jnp.zeros_like(acc_ref)
        r_ref[...] = jnp.zeros_like(r_ref)
        w_ref[...] = jnp.zeros_like(w_ref)

        def key_rows(block):
            return pl.ds(pl.multiple_of(block * TK, TK), TK)

        def store_scores(z_ref, block, subs):
            kb = k_ref[key_rows(block), :]
            for c, (sub, h) in enumerate(chains):
                if sub in subs:
                    z_ref[c] = lax.dot_general(q_sub[sub][h], kb, (((1,), (1,)), ((), ())),
                                               preferred_element_type=F32)

        def add_weighted_values(block, subs):
            vb = v_ref[key_rows(block), :]
            for sub in subs:
                acc_ref[pl.ds(sub * RS, RS), :] += _join_heads(*[
                    jnp.dot(w_ref[2 * sub + h], vb, preferred_element_type=F32) for h in range(2)])

        def step(block, z_ref, z_next_ref, subs, diagonal_sub, prev_subs, next_subs):
            if prev_subs:
                add_weighted_values(block + 1, prev_subs)
            if next_subs:
                store_scores(z_next_ref, jnp.maximum(block - 1, 0), next_subs)
            active = [(c, sub, h) for c, (sub, h) in enumerate(chains) if sub in subs]
            ls, sums = {}, {}
            for c, sub, h in active:
                ls[c] = _log_stay(z_ref[c])
                sums[c] = _dot_hilo(jnp.where(below_diagonal, ls[c], 0.0) if sub == diagonal_sub else ls[c], after)
            for c, sub, h in active:
                rows = pl.ds(sub * RS, RS)
                later = r_ref[h, rows, :]
                w = jnp.exp(z_ref[c] + ls[c] + (sums[c][:, :TK] + later))
                if sub == diagonal_sub:
                    w = jnp.where(below_diagonal, w, 0.0)
                w_ref[c] = w.astype(BF16)
                r_ref[h, rows, :] = later + sums[c][:, TK:]

        store_scores(z_even, base + NSUB - 1, [NSUB - 1])
        buffers = (z_even, z_odd)
        for j in reversed(range(NSUB)):
            subs = all_subs[j:]
            step(base + j, buffers[0], buffers[1], subs, j, all_subs[j + 1:], all_subs[j - 1:] if j else all_subs)
            buffers = buffers[::-1]
        assert buffers[0] is z_even

        @pl.loop(0, base // 2)
        def _(pair):
            block = base - 1 - 2 * pair
            step(block, z_even, z_odd, all_subs, None, all_subs, all_subs)
            step(block - 1, z_odd, z_even, all_subs, None, all_subs, all_subs)

        add_weighted_values(0, all_subs)
        o_ref[...] = acc_ref[...].astype(o_ref.dtype)
        if phases is not None:
            pl.when(jnp.logical_and(p == NPAIR - 1, i == n_steps - 1))(phases[2])

    kwargs, operands = _with_comm(
        comm, [Q_ROWS_SPEC, K_ALL_SPEC, V_ALL_SPEC], [PAIR_ROWS_SPEC, PAIR_TOTAL_SPEC],
        [jax.ShapeDtypeStruct((S, NH * HD), BF16), jax.ShapeDtypeStruct((NH, S, TK), F32)], [qkv, qkv, qkv],
        [pltpu.VMEM((TQ, LANES), F32), pltpu.VMEM((2 * NSUB, RS, TK), F32), pltpu.VMEM((2 * NSUB, RS, TK), F32),
         pltpu.VMEM((2 * NSUB, RS, TK), BF16)])
    return pl.pallas_call(
        body, name=name, grid=(NPAIR, n_steps),
        compiler_params=_cparams(("arbitrary", "arbitrary")), **kwargs,
    )(*operands)


def attn_bwd(qkv, dout, totals, name, comm=None):
    n_steps = S // TQ

    def body(*refs):
        ((q_ref, k_ref, v_ref, do_ref, r_ref), (dq_ref, dk_ref, dv_ref),
         (z_even, z_odd, dw_even, dw_odd, before_ref, dbefore_ref, dz_ref, w_ref), phases) = _comm_hooks(
            comm, refs, 5, 3, 8)
        p = pl.program_id(0)
        i = pl.program_id(1)
        if phases is not None:
            pl.when(jnp.logical_and(p == 0, i == 0))(phases[0])
            pl.when(jnp.logical_and(p == NPAIR - 1, i == n_steps - 2))(phases[1])

        @pl.when(i == 0)
        def _():
            dk_ref[...] = jnp.zeros_like(dk_ref)
            dv_ref[...] = jnp.zeros_like(dv_ref)

        chains = [(sub, h) for sub in range(NSUB) for h in range(2)]
        nch = len(chains)
        qb = q_ref[...]
        dob = do_ref[...].astype(BF16)
        q_sub = [_head_halves(qb[sub * RS:(sub + 1) * RS] * SCALE) for sub in range(NSUB)]
        do_sub = [_head_halves(dob[sub * RS:(sub + 1) * RS]) for sub in range(NSUB)]
        upto = _tri_and_ones("upto")
        before_tri = _tri_and_ones("before")
        below_diagonal = (lax.broadcasted_iota(jnp.int32, (RS, TK), 1)
                          < lax.broadcasted_iota(jnp.int32, (RS, TK), 0))
        contract_lanes = (((1,), (1,)), ((), ()))
        contract_rows = (((0,), (0,)), ((), ()))
        base = i * NSUB
        all_subs = list(range(NSUB))

        def key_rows(block):
            return pl.ds(pl.multiple_of(block * TK, TK), TK)

        def store_products(bufs, block, subs):
            z_ref, dw_ref = bufs
            kb = k_ref[key_rows(block), :]
            vb = v_ref[key_rows(block), :]
            for c, (sub, h) in enumerate(chains):
                if sub in subs:
                    z_ref[c] = lax.dot_general(q_sub[sub][h], kb, contract_lanes, preferred_element_type=F32)
                    dw_ref[c] = lax.dot_general(do_sub[sub][h], vb, contract_lanes, preferred_element_type=F32)

        def add_gradients(block, subs):
            kb = k_ref[key_rows(block), :]
            for sub in subs:
                rows = pl.ds(sub * RS, RS)
                dq_ref[rows, :] += _join_heads(*[jnp.dot(dz_ref[h, rows, :], kb, preferred_element_type=F32)
                                                 for h in range(2)])
            dk_ref[key_rows(block), :] += _join_heads(*[
                lax.dot_general(dz_ref[h], qb, contract_rows, preferred_element_type=F32) for h in range(2)])
            dv_ref[key_rows(block), :] += _join_heads(*[
                lax.dot_general(w_ref[h], dob, contract_rows, preferred_element_type=F32) for h in range(2)])

        for ref in (dq_ref, before_ref, dbefore_ref, dz_ref, w_ref):
            ref[...] = jnp.zeros_like(ref)
        even, odd = (z_even, dw_even), (z_odd, dw_odd)
        store_products(even, 0, all_subs)

        def step(block, bufs, next_bufs, subs, diagonal_sub, prev_subs, next_subs):
            z_ref, dw_ref = bufs
            add_gradients(jnp.maximum(block - 1, 0), prev_subs)
            for sub in prev_subs:
                if sub not in subs:
                    dz_ref[:, pl.ds(sub * RS, RS), :] = jnp.zeros((2, RS, TK), BF16)
                    w_ref[:, pl.ds(sub * RS, RS), :] = jnp.zeros((2, RS, TK), BF16)
            if next_subs:
                store_products(next_bufs, block + 1, next_subs)
            active = [(c, sub, h) for c, (sub, h) in enumerate(chains) if sub in subs]
            ls, sums, dl, dsums = {}, {}, {}, {}
            for c, sub, h in active:
                ls[c] = _log_stay(z_ref[c])
                sums[c] = _dot_hilo(jnp.where(below_diagonal, ls[c], 0.0) if sub == diagonal_sub else ls[c], upto)
            for c, sub, h in active:
                rows = pl.ds(sub * RS, RS)
                before = before_ref[c]
                log_after = r_ref[h, rows, :] - (sums[c][:, :TK] + before)
                w = jnp.exp((z_ref[c] + ls[c]) + log_after)
                if sub == diagonal_sub:
                    w = jnp.where(below_diagonal, w, 0.0)
                dl[c] = dw_ref[c] * w
                dsums[c] = _dot_hilo(dl[c], before_tri)
                w_ref[h, rows, :] = w.astype(BF16)
                before_ref[c] = before + sums[c][:, TK:]
            for c, sub, h in active:
                rows = pl.ds(sub * RS, RS)
                dbefore = dbefore_ref[c]
                beta = jnp.exp(z_ref[c] + ls[c])
                if sub == diagonal_sub:
                    beta = jnp.where(below_diagonal, beta, 0.0)
                dstay = dsums[c][:, :TK] + dbefore
                dz_ref[h, rows, :] = ((dl[c] * (1.0 - beta) - beta * dstay) * SCALE).astype(BF16)
                dbefore_ref[c] = dbefore + dsums[c][:, TK:]

        @pl.loop(0, base // 2)
        def _(pair):
            step(2 * pair, even, odd, all_subs, None, all_subs, all_subs)
            step(2 * pair + 1, odd, even, all_subs, None, all_subs, all_subs)

        bufs = (even, odd)
        for j in range(NSUB):
            step(base + j, bufs[0], bufs[1], all_subs[j:], j, all_subs[j - 1:] if j else all_subs, all_subs[j + 1:])
            bufs = bufs[::-1]

        add_gradients(base + NSUB - 1, all_subs[NSUB - 1:])
        if phases is not None:
            pl.when(jnp.logical_and(p == NPAIR - 1, i == n_steps - 1))(phases[2])

    full = jax.ShapeDtypeStruct((S, NH * HD), F32)
    kwargs, operands = _with_comm(
        comm, [Q_ROWS_SPEC, K_ALL_SPEC, V_ALL_SPEC, PAIR_ROWS_SPEC, PAIR_TOTAL_SPEC],
        [PAIR_ROWS_SPEC, PAIR_ALL_SPEC, PAIR_ALL_SPEC], [full, full, full], [qkv, qkv, qkv, dout, totals],
        [pltpu.VMEM((2 * NSUB, RS, TK), F32)] * 6 + [pltpu.VMEM((2, TQ, TK), BF16)] * 2)
    return pl.pallas_call(
        body, name=name, grid=(NPAIR, n_steps),
        compiler_params=_cparams(("arbitrary", "arbitrary")), **kwargs,
    )(*operands)


def _proj_cols(first_col):
    base = first_col // LANES
    return pl.BlockSpec((S, LANES), lambda j: (0, base + j))


CONV_OUT_SPEC = pl.BlockSpec((S, LANES), lambda j: (0, j))
CONV_DOUT_SPEC = pl.BlockSpec((S, LANES), lambda j: (0, (NH * HD) // LANES + j))
CONV_W_SPEC = pl.BlockSpec((8, LANES), lambda j: (0, j))
CONV_B_SPEC = pl.BlockSpec((1, LANES), lambda j: (0, j))


def _shift_down(u, n):
    rows = lax.broadcasted_iota(jnp.int32, u.shape, 0)
    return jnp.where(rows >= n, pltpu.roll(u, n, 0), 0.0)


def _shift_up(u, n):
    rows = lax.broadcasted_iota(jnp.int32, u.shape, 0)
    return jnp.where(rows < S - n, pltpu.roll(u, S - n, 0), 0.0)


def conv_fwd(proj, cw8, cb, name):
    def body(bg_ref, cg_ref, hc_ref, w_ref, b_ref, o_ref):
        u = cg_ref[...] * hc_ref[...]
        w = w_ref[...]
        y = w[0:1, :] * _shift_down(u, 2) + w[1:2, :] * _shift_down(u, 1) + w[2:3, :] * u + b_ref[...]
        o_ref[...] = bg_ref[...] * y

    return pl.pallas_call(
        body, name=name, grid=(CW // LANES,),
        in_specs=[_proj_cols(0), _proj_cols(CW), _proj_cols(2 * CW), CONV_W_SPEC, CONV_B_SPEC],
        out_specs=CONV_OUT_SPEC, out_shape=jax.ShapeDtypeStruct((S, CW), F32),
        compiler_params=_cparams(("parallel",)),
    )(proj, proj, proj, cw8, cb)


def conv_bwd(proj, dout, cw8, cb, name):
    def body(bg_ref, cg_ref, hc_ref, do_ref, w_ref, b_ref, dbg_ref, dcg_ref, dhc_ref, dw_ref, db_ref):
        cg, hc, do = cg_ref[...], hc_ref[...], do_ref[...]
        w = w_ref[...]
        u = cg * hc
        u1, u2 = _shift_down(u, 1), _shift_down(u, 2)
        y = w[0:1, :] * u2 + w[1:2, :] * u1 + w[2:3, :] * u + b_ref[...]
        dbg_ref[...] = do * y
        dy = do * bg_ref[...]
        db_ref[...] = jnp.sum(dy, axis=0, keepdims=True)
        dw_ref[...] = jnp.concatenate(
            [jnp.sum(dy * u2, axis=0, keepdims=True), jnp.sum(dy * u1, axis=0, keepdims=True),
             jnp.sum(dy * u, axis=0, keepdims=True), jnp.zeros((5, LANES), F32)], axis=0)
        du = w[2:3, :] * dy + w[1:2, :] * _shift_up(dy, 1) + w[0:1, :] * _shift_up(dy, 2)
        dcg_ref[...] = du * hc
        dhc_ref[...] = du * cg

    full = jax.ShapeDtypeStruct((S, CW), F32)
    return pl.pallas_call(
        body, name=name, grid=(CW // LANES,),
        in_specs=[_proj_cols(0), _proj_cols(CW), _proj_cols(2 * CW), CONV_DOUT_SPEC, CONV_W_SPEC, CONV_B_SPEC],
        out_specs=[CONV_OUT_SPEC, CONV_OUT_SPEC, CONV_OUT_SPEC, CONV_W_SPEC, CONV_B_SPEC],
        out_shape=[full, full, full, jax.ShapeDtypeStruct((8, CW), F32), jax.ShapeDtypeStruct((1, CW), F32)],
        compiler_params=_cparams(("parallel",)),
    )(proj, proj, proj, dout, cw8, cb)


GELU_K = math.sqrt(2.0 / math.pi)
GELU_C = 0.044715


def _gelu(x):
    return 0.5 * x * (1.0 + jnp.tanh(GELU_K * (x + GELU_C * (x * x * x))))


def _gelu_grad(x):
    t = jnp.tanh(GELU_K * (x + GELU_C * (x * x * x)))
    return 0.5 * (1.0 + t) + 0.5 * x * (1.0 - t * t) * (GELU_K * (1.0 + 3.0 * GELU_C * (x * x)))


def _sg_masks():
    row = lax.broadcasted_iota(jnp.int32, (T, T), 0)
    col = lax.broadcasted_iota(jnp.int32, (T, T), 1)
    causal = jnp.right_shift(row, 6) >= jnp.right_shift(col, 6)
    head_of_col = jnp.right_shift(lax.broadcasted_iota(jnp.int32, (T, CW), 1), 6)
    return causal, head_of_col


def _sg_mixed(vnb, sw_ref, bias, causal, head_of_col):
    mixed = bias
    for h in range(SG_HEADS):
        wh = jnp.where(causal, sw_ref[h], 0.0).astype(BF16)
        mh = jnp.dot(wh, vnb, preferred_element_type=F32)
        mixed = mixed + jnp.where(head_of_col == h, mh, 0.0)
    return mixed


SG_U_SPEC = pl.BlockSpec((T, CW), lambda n: (n, 3))
SG_V_SPEC = pl.BlockSpec((T, CW), lambda n: (n, 4))
SG_ROW_SPEC = pl.BlockSpec((T, CW), lambda n: (n, 0))
SG_DOUT_SPEC = pl.BlockSpec((T, CW), lambda n: (n, 3))
SG_G_SPEC = pl.BlockSpec((1, CW), lambda n: (0, 0))
SG_W_SPEC = pl.BlockSpec((SG_HEADS, T, T), lambda n: (0, 0, 0))
SG_BIAS_SPEC = pl.BlockSpec((T, CW), lambda n: (0, 0))


def sg_fwd(proj, gn, sw, bias, name):
    def body(u_ref, v_ref, g_ref, sw_ref, bias_ref, o_ref):
        causal, head_of_col = _sg_masks()
        gv = _gelu(v_ref[...])
        rstd = lax.rsqrt(jnp.mean(gv * gv, axis=-1, keepdims=True) + EPS)
        vnb = ((gv * rstd) * g_ref[...]).astype(BF16)
        mixed = _sg_mixed(vnb, sw_ref, bias_ref[...], causal, head_of_col)
        o_ref[...] = _gelu(u_ref[...]) * mixed

    return pl.pallas_call(
        body, name=name, grid=(S // T,),
        in_specs=[SG_U_SPEC, SG_V_SPEC, SG_G_SPEC, SG_W_SPEC, SG_BIAS_SPEC],
        out_specs=SG_ROW_SPEC, out_shape=jax.ShapeDtypeStruct((S, CW), F32),
        compiler_params=_cparams(("parallel",)),
    )(proj, proj, gn, sw, bias)


def sg_bwd(proj, dout, gn, sw, bias, name):
    def body(u_ref, v_ref, do_ref, g_ref, sw_ref, bias_ref, du_ref, dv_ref, dg_ref, dsw_ref, dbias_ref):
        @pl.when(pl.program_id(0) == 0)
        def _():
            dg_ref[...] = jnp.zeros_like(dg_ref)
            dsw_ref[...] = jnp.zeros_like(dsw_ref)
            dbias_ref[...] = jnp.zeros_like(dbias_ref)

        causal, head_of_col = _sg_masks()
        uv, vv, do, gnv = u_ref[...], v_ref[...], do_ref[...], g_ref[...]
        gv = _gelu(vv)
        rstd = lax.rsqrt(jnp.mean(gv * gv, axis=-1, keepdims=True) + EPS)
        xhat = gv * rstd
        vnb = (xhat * gnv).astype(BF16)
        mixed = _sg_mixed(vnb, sw_ref, bias_ref[...], causal, head_of_col)
        du_ref[...] = (do * mixed) * _gelu_grad(uv)
        dmix = do * _gelu(uv)
        dbias_ref[...] += dmix
        dmixb = dmix.astype(BF16)
        dvn = jnp.zeros((T, CW), F32)
        for h in range(SG_HEADS):
            wh = jnp.where(causal, sw_ref[h], 0.0).astype(BF16)
            dvh = lax.dot_general(wh, dmixb, (((0,), (0,)), ((), ())), preferred_element_type=F32)
            dvn = dvn + jnp.where(head_of_col == h, dvh, 0.0)
            dmh = jnp.where(head_of_col == h, dmixb, jnp.zeros_like(dmixb))
            dwh = lax.dot_general(dmh, vnb, (((1,), (1,)), ((), ())), preferred_element_type=F32)
            dsw_ref[h] += jnp.where(causal, dwh, 0.0)
        dg_ref[...] += jnp.sum(dvn * xhat, axis=0, keepdims=True)
        dxhat = dvn * gnv
        dgv = rstd * (dxhat - xhat * jnp.mean(dxhat * xhat, axis=-1, keepdims=True))
        dv_ref[...] = dgv * _gelu_grad(vv)

    full = jax.ShapeDtypeStruct((S, CW), F32)
    return pl.pallas_call(
        body, name=name, grid=(S // T,),
        in_specs=[SG_U_SPEC, SG_V_SPEC, SG_DOUT_SPEC, SG_G_SPEC, SG_W_SPEC, SG_BIAS_SPEC],
        out_specs=[SG_ROW_SPEC, SG_ROW_SPEC, SG_G_SPEC, SG_W_SPEC, SG_BIAS_SPEC],
        out_shape=[full, full, jax.ShapeDtypeStruct((1, CW), F32),
                   jax.ShapeDtypeStruct((SG_HEADS, T, T), F32), jax.ShapeDtypeStruct((T, CW), F32)],
        compiler_params=_cparams(("arbitrary",)),
    )(proj, proj, dout, gn, sw, bias)


ADA_COLS = NMOD * D // NDEV


def ada_fwd(c_all, ada_w, ada_b_mine, name):
    def body(c_ref, w_ref, b_ref, o_ref, ca_ref):
        cv = c_ref[...]
        ca = cv * (1.0 / (1.0 + jnp.exp(-cv)))
        ca_ref[...] = ca
        cab = ca.astype(BF16)
        for l in range(L):
            o_ref[l] = jnp.dot(cab, w_ref[l].astype(BF16), preferred_element_type=F32) + b_ref[l]

    return pl.pallas_call(
        body, name=name,
        out_shape=[jax.ShapeDtypeStruct((L, NDEV, ADA_COLS), F32), jax.ShapeDtypeStruct((NDEV, D), F32)],
        compiler_params=_cparams(),
    )(c_all, ada_w, ada_b_mine)


def ada_bwd(ca, dmod_cols, name):
    def body(ca_ref, dm_ref, o_ref):
        cab = ca_ref[...].astype(BF16)
        for l in range(L):
            o_ref[l] = lax.dot_general(cab, dm_ref[l].astype(BF16), (((0,), (0,)), ((), ())),
                                       preferred_element_type=F32)

    return pl.pallas_call(
        body, name=name, out_shape=jax.ShapeDtypeStruct((L, D, ADA_COLS), F32),
        compiler_params=_cparams(),
    )(ca, dmod_cols)


def _adamw(w, g, m, v):
    m = B1 * m + (1.0 - B1) * g
    v = B2 * v + (1.0 - B2) * (g * g)
    m_hat = m / BC1
    v_hat = v / BC2
    delta = -LR * (m_hat / (jnp.sqrt(v_hat) + AEPS) + WD * w)
    return delta, m, v


VEC_ROWS_PER_LAYER = 8
VEC_FINAL_ROW = L * VEC_ROWS_PER_LAYER
VEC_ROWS = VEC_FINAL_ROW + 8
W256_TAPS, W256_CONV_B, W256_GN = 0, 8, 9
W256_ROWS_PER_LAYER = 16


def small_update(vec_all, w256_all, sb_all, sw_all, params, name):
    n_par = len(params)

    def body(*refs):
        vec_ref, w256_ref, sb_ref = refs[:3]
        sw_refs = refs[3:3 + L]
        par_refs = [refs[3 + L + 3 * k:3 + L + 3 * k + 3] for k in range(n_par)]
        out = refs[3 + L + 3 * n_par:]
        out_par = [out[4 * k:4 * k + 4] for k in range(n_par)]
        loss_ref, taps_ref = out[4 * n_par:]

        def total(ref, idx):
            acc = ref[(0,) + idx].astype(F32)
            for d in range(1, NDEV):
                acc = acc + ref[(d,) + idx].astype(F32)
            return acc

        def update(k, region, g):
            w_ref, m_ref, v_ref = par_refs[k]
            g_ref, d_ref, nm_ref, nv_ref = out_par[k]
            delta, nm, nv = _adamw(w_ref[region], g, m_ref[region], v_ref[region])
            g_ref[region] = g
            d_ref[region] = delta
            nm_ref[region] = nm
            nv_ref[region] = nv

        for l in range(L):
            base = l * VEC_ROWS_PER_LAYER
            for k in range(NMOD):
                update(0, (slice(l, l + 1), slice(k * D, (k + 1) * D)), total(vec_ref, (slice(base + k, base + k + 1),)))
            update(1, (slice(l, l + 1),), total(vec_ref, (slice(base + 6, base + 7),)))
            update(2, (slice(l, l + 1),), total(vec_ref, (slice(base + 7, base + 8),)))
            wbase = l * W256_ROWS_PER_LAYER
            update(4, (slice(l, l + 1),), total(w256_ref, (slice(wbase + W256_CONV_B, wbase + W256_CONV_B + 1),)))
            update(5, (slice(l, l + 1),), total(w256_ref, (slice(wbase + W256_GN, wbase + W256_GN + 1),)))
            update(6, (l,), total(sw_refs[l], ()))
            update(7, (l,), total(sb_ref, (slice(l * SG_HEADS, (l + 1) * SG_HEADS),)))
            taps_ref[l] = total(w256_ref, (slice(wbase + W256_TAPS, wbase + W256_TAPS + 8),))
        update(3, (slice(0, 1),), total(vec_ref, (slice(VEC_FINAL_ROW, VEC_FINAL_ROW + 1),)))
        loss_ref[...] = total(vec_ref, (slice(VEC_FINAL_ROW + 1, VEC_FINAL_ROW + 2), slice(0, LANES)))

    out_shape = []
    for w, _, _ in params:
        out_shape += [jax.ShapeDtypeStruct(w.shape, F32)] * 4
    out_shape += [jax.ShapeDtypeStruct((1, LANES), F32), jax.ShapeDtypeStruct((L, 8, CW), F32)]
    outs = pl.pallas_call(body, name=name, out_shape=out_shape, compiler_params=_cparams())(
        vec_all, w256_all, sb_all, *sw_all, *[a for p in params for a in p])
    return [outs[4 * k:4 * k + 4] for k in range(n_par)], outs[4 * n_par:]


def adamw_plain(w, g, m, v, tr, name):
    rows, cols = w.shape
    spec = pl.BlockSpec((tr, cols), lambda i: (i, 0))

    def body(w_ref, g_ref, m_ref, v_ref, d_ref, nm_ref, nv_ref):
        delta, nm, nv = _adamw(w_ref[...], g_ref[...], m_ref[...], v_ref[...])
        d_ref[...] = delta
        nm_ref[...] = nm
        nv_ref[...] = nv

    shp = jax.ShapeDtypeStruct((rows, cols), F32)
    return pl.pallas_call(
        body, name=name, grid=(rows // tr,), in_specs=[spec] * 4, out_specs=[spec] * 3,
        out_shape=[shp, shp, shp], compiler_params=_cparams(("parallel",)),
    )(w, g, m, v)


def adamw_reduce(w, parts, m, v, tr, name, tie=None):
    _, rows, cols = w.shape
    spec = pl.BlockSpec((None, tr, cols), lambda l, i: (l, i, 0))
    pspecs = [pl.BlockSpec((NDEV, tr, cols), lambda l, i, k=k: (0, jnp.where(l == k, i, 0), 0)) for k in range(L)]

    ties = [] if tie is None else [tie]

    def body(w_ref, p0_ref, p1_ref, m_ref, v_ref, *rest):
        g_ref, d_ref, nm_ref, nv_ref = rest[len(ties):]
        first_layer = pl.program_id(0) == 0
        g = jnp.zeros((tr, cols), F32)
        for d in range(NDEV):
            g = g + jnp.where(first_layer, p0_ref[d], p1_ref[d]).astype(F32)
        delta, nm, nv = _adamw(w_ref[...], g, m_ref[...], v_ref[...])
        g_ref[...] = g
        d_ref[...] = delta
        nm_ref[...] = nm
        nv_ref[...] = nv

    shp = jax.ShapeDtypeStruct(w.shape, F32)
    return pl.pallas_call(
        body, name=name, grid=(L, rows // tr),
        in_specs=[spec] + pspecs + [spec, spec] + [pl.BlockSpec(t.shape, lambda l, i: (0, 0)) for t in ties],
        out_specs=[spec] * 4, out_shape=[shp] * 4, compiler_params=_cparams(("parallel", "parallel")),
    )(w, *parts, m, v, *ties)


def _pad_rows(flat, rows):
    return jnp.pad(flat, (0, rows * LANES - flat.shape[0])).reshape(rows, LANES)


def kernel(x, c, ada_w, ada_b, norm_mix_g, norm_mlp_g, w_in, conv_w, conv_b, gmlp_norm_g, spatial_w, spatial_b, w_out, mlp_w1, mlp_w2, final_norm_g, loss_target, m_ada_w, m_ada_b, m_norm_mix_g, m_norm_mlp_g, m_w_in, m_conv_w, m_conv_b, m_gmlp_norm_g, m_spatial_w, m_spatial_b, m_w_out, m_mlp_w1, m_mlp_w2, m_final_norm_g, v_ada_w, v_ada_b, v_norm_mix_g, v_norm_mlp_g, v_w_in, v_conv_w, v_conv_b, v_gmlp_norm_g, v_spatial_w, v_spatial_b, v_w_out, v_mlp_w1, v_mlp_w2, v_final_norm_g):
    me = _lin(_my_pos())
    x0 = x[0]
    target = loss_target[0]
    conv_shard = conv_w.shape[-1]

    w_in_b, w_out_b, w1_b, w2_b = [w.astype(BF16) for w in (w_in, w_out, mlp_w1, mlp_w2)]
    pack0 = _pad_rows(jnp.concatenate([c.reshape(-1), conv_w.reshape(-1)]), 16)
    g0, gw_in0 = run_comm(Gather([pack0, w_in_b[0]]), "gather_first")
    g0 = g0.reshape(NDEV, 16 * LANES)
    c_all = g0[:, :D]
    conv_full = (g0[:, D:D + L * 3 * conv_shard].reshape(NDEV, L, 3, conv_shard)
                 .transpose(1, 2, 0, 3).reshape(L, 3, CW))

    def canonical_w_in(gathered):
        return gathered.transpose(1, 0, 2).reshape(D, PROJ)

    W_in = [canonical_w_in(gw_in0), None]
    W_out, W1, W2 = [None] * L, [None] * L, [None] * L

    ada_b_mine = lax.dynamic_slice(ada_b, (0, me * ADA_COLS), (L, ADA_COLS)).reshape(L, 1, ADA_COLS)
    mod_part, c_act = ada_fwd(c_all, ada_w, ada_b_mine, "ada_fwd")
    gmod = run_comm(Gather([mod_part]), "gather_mod")[0]
    mod = lax.dynamic_index_in_dim(gmod, me, axis=2, keepdims=False)
    mod = mod.transpose(1, 0, 2).reshape(L, NMOD, 1, D)
    early_weights, token = start_copies([w_out_b[0], w1_b[0]], me, "gather_early0_start", True, after=gmod)
    mod = tied(mod, token)

    cw8 = jnp.pad(conv_full, ((0, 0), (0, 5), (0, 0)))
    sg_bias = jnp.repeat(spatial_b.transpose(0, 2, 1), HD, axis=2)

    saved = []
    xl = x0
    for l in range(L):
        sh_m, sc_m, g_m, sh_f, sc_f, g_f = [mod[l, k] for k in range(NMOD)]
        h1 = normmod_fwd(xl, norm_mix_g[l:l + 1], sc_m, sh_m, f"norm_mix_fwd{l}")
        if l > 0:
            W_in[l] = canonical_w_in(finish_copies(w_in_handle, xl, f"gather_w_in{l}_wait")[0])
        qkv = mm_layer("proj_qkv", l, h1, W_in[l], out_dtypes=[BF16], cols=(0, QKV))[0]
        proj = mm_layer("proj_rest", l, h1, W_in[l], out_dtypes=[F32], cols=(QKV, REST))[0]
        a_out, a_tot, gw2 = attn_fwd(qkv, f"attn_fwd{l}", comm=Gather([w2_b[l]]))
        gw_out, gw1 = finish_copies(early_weights, a_out, f"gather_early{l}_wait")
        W_out[l] = gw_out.reshape(D, D)
        W1[l] = gw1
        W2[l] = gw2.reshape(DFF, D)
        if l + 1 < L:
            w_in_handle, token = start_copies([w_in_b[l + 1]], me, f"gather_w_in{l + 1}_start", True, after=a_out)
            early_weights, token = start_copies([w_out_b[l + 1], w1_b[l + 1]], me, f"gather_early{l + 1}_start", True,
                                                after=token)
            g_m = tied(g_m, token)
        c_out = conv_fwd(proj, cw8[l], conv_b[l:l + 1], f"conv_fwd{l}")
        s_out = sg_fwd(proj, gmlp_norm_g[l:l + 1], spatial_w[l], sg_bias[l], f"sg_fwd{l}")
        cat = jnp.concatenate([a_out, c_out.astype(BF16), s_out.astype(BF16)], axis=1)
        mix, x1 = mm_layer("mix", l, cat, W_out[l], out_dtypes=[F32, F32],
                           epilogue=lambda acc, xr, g: (acc, xr + g * acc), extras=[(xl, "tile"), (g_m, "col")])
        h2 = normmod_fwd(x1, norm_mlp_g[l:l + 1], sc_f, sh_f, f"norm_mlp_fwd{l}")
        ra, r = mm_layer("mlp_up", l, h2, W1[l], out_dtypes=[BF16, BF16], b_blocks=True,
                         epilogue=lambda acc: (jnp.maximum(acc, 0.0), jnp.square(jnp.maximum(acc, 0.0))))
        m2, x2 = mm_layer("mlp_down", l, r, W2[l], out_dtypes=[F32, F32],
                          epilogue=lambda acc, xr, g: (acc, xr + g * acc), extras=[(x1, "tile"), (g_f, "col")])
        saved.append(dict(x=xl, h1=h1, proj=proj, qkv=qkv, a_tot=a_tot, cat=cat, mix=mix,
                          x1=x1, h2=h2, ra=ra, r=r, m2=m2))
        xl = x2

    dx, loss_part, d_final_g = loss_head(xl, target, final_norm_g.reshape(1, D), "loss_head")

    p_in, p_out, p_w1, p_w2 = [None] * L, [None] * L, [None] * L, [None] * L
    w_in_grads = [None] * L
    vec_rows, d_norm_mix, d_norm_mlp = [None] * L, [None] * L, [None] * L
    dcw8, d_conv_b, d_gn, d_sw, d_sb = [None] * L, [None] * L, [None] * L, [None] * L, [None] * L
    late_grads = [None] * L
    for l in reversed(range(L)):
        sv = saved[l]
        sh_m, sc_m, g_m, sh_f, sc_f, g_f = [mod[l, k] for k in range(NMOD)]
        dm2, dg_f = gate_bwd(dx, sv["m2"], g_f, f"gate_mlp_bwd{l}")
        da = mm_layer("mlp_down_dgrad", l, dm2, W2[l], out_dtypes=[BF16], trans_b=True,
                      epilogue=lambda acc, rav: (acc * (2.0 * rav.astype(F32)),), extras=[(sv["ra"], "tile")])[0]
        dW2 = mm_layer("mlp_down_wgrad", l, sv["r"], dm2, out_dtypes=[BF16], trans_a=True)[0]
        dW1 = mm_layer("mlp_up_wgrad", l, sv["h2"], da, out_dtypes=[BF16], trans_a=True, out_blocks=True)[0]
        dh2 = mm_layer("mlp_up_dgrad", l, da, W1[l], out_dtypes=[F32], trans_b=True, b_blocks=True)[0]
        dx1, dsc_f, dsh_f, d_norm_mlp[l] = normmod_bwd(sv["x1"], dh2, dx, norm_mlp_g[l:l + 1], sc_f,
                                                       f"norm_mlp_bwd{l}")
        dmix, dg_m = gate_bwd(dx1, sv["mix"], g_m, f"gate_mix_bwd{l}")
        dcat = mm_layer("mix_dgrad", l, dmix, W_out[l], out_dtypes=[F32], trans_b=True)[0]
        dW_out = mm_layer("mix_wgrad", l, sv["cat"], dmix, out_dtypes=[BF16], trans_a=True)[0]
        pieces_w2, pieces_out = dW2.reshape(NDEV, DFF // NDEV, D), dW_out.reshape(NDEV, D // NDEV, D)
        ride, late = ([pieces_w2, pieces_out], dW1) if l == L - 1 else ([pieces_w2, dW1], pieces_out)
        dq, dk, dv, *arrived = attn_bwd(sv["qkv"], dcat, sv["a_tot"], f"attn_bwd{l}", comm=Exchange(ride))
        p_w2[l] = arrived[0]
        (p_out if l == L - 1 else p_w1)[l] = arrived[1]
        late_grads[l], late_token = start_copies([late], me, f"exchange_late{l}_start", False, after=dq)
        dbg, dcg, dhc, dcw8[l], d_conv_b[l] = conv_bwd(sv["proj"], dcat, cw8[l], conv_b[l:l + 1], f"conv_bwd{l}")
        dus, dvs, d_gn[l], dsw, dbias = sg_bwd(sv["proj"], dcat, gmlp_norm_g[l:l + 1], spatial_w[l], sg_bias[l],
                                               f"sg_bwd{l}")
        d_sw[l] = dsw.astype(BF16)
        d_sb[l] = dbias.reshape(T, SG_HEADS, HD).sum(axis=2).T
        dproj = jnp.concatenate([dq, dk, dv, dbg, dcg, dhc, dus, dvs], axis=1).astype(BF16)
        dW_in = mm_layer("proj_wgrad", l, sv["h1"], dproj, out_dtypes=[BF16], trans_a=True,
                         extras=[(late_token, "tie")])[0]
        pieces = dW_in.reshape(D, NDEV, PROJ // NDEV).transpose(1, 0, 2)
        w_in_grads[l], token = start_copies([pieces], me, f"exchange_w_in{l}_start", False)
        dh1 = mm_layer("proj_dgrad", l, dproj, W_in[l], out_dtypes=[F32], trans_b=True, extras=[(token, "tie")])[0]
        dx, dsc_m, dsh_m, d_norm_mix[l] = normmod_bwd(sv["x"], dh1, dx1, tied(norm_mix_g[l:l + 1], token), sc_m,
                                                      f"norm_mix_bwd{l}")
        vec_rows[l] = [dsh_m, dsc_m, dg_m, dsh_f, dsc_f, dg_f, d_norm_mix[l], d_norm_mlp[l]]

    grad_x = dx.reshape(1, S, D)

    g_w2, d_w2, nm_w2, nv_w2 = adamw_reduce(mlp_w2, p_w2, m_mlp_w2, v_mlp_w2, 256, "adamw_mlp_w2", tie=token)
    p_w1[L - 1] = finish_copies(late_grads[L - 1], d_w2, f"exchange_late{L - 1}_wait")[0]
    g_w1, d_w1, nm_w1, nv_w1 = adamw_reduce(mlp_w1, p_w1, m_mlp_w1, v_mlp_w1, 256, "adamw_mlp_w1", tie=token)

    vec_pack = jnp.concatenate([row for l in range(L) for row in vec_rows[l]]
                               + [d_final_g, loss_part, jnp.zeros((VEC_ROWS - VEC_FINAL_ROW - 2, D), F32)], axis=0)
    vec_pack, _ = lax.optimization_barrier((vec_pack, (d_w1, d_w2)))
    w256_pack = jnp.concatenate([blk for l in range(L) for blk in (
        dcw8[l], d_conv_b[l], d_gn[l], jnp.zeros((W256_ROWS_PER_LAYER - W256_GN - 1, CW), F32))], axis=0)
    vec_all, w256_all, sb_all, *sw_all = run_comm(
        Gather([vec_pack, w256_pack, jnp.concatenate(d_sb, axis=0)] + d_sw), "gather_small_grads")

    dmod_all = (vec_all[:, :VEC_FINAL_ROW].reshape(NDEV, L, VEC_ROWS_PER_LAYER, D)[:, :, :NMOD]
                .reshape(NDEV, L, NMOD * D))
    dmod_cols = lax.dynamic_slice(dmod_all, (0, 0, me * ADA_COLS), (NDEV, L, ADA_COLS)).transpose(1, 0, 2)
    g_ada_w = ada_bwd(c_act, dmod_cols, "ada_bwd")

    flat2 = lambda t: t.reshape(L * D, ADA_COLS)
    d_ada_w, nm_ada_w, nv_ada_w = [t.reshape(L, D, ADA_COLS) for t in adamw_plain(
        flat2(ada_w), flat2(g_ada_w), flat2(m_ada_w), flat2(v_ada_w), 256, "adamw_ada_w")]

    after = jnp.concatenate([t.reshape(-1)[:1] for t in (d_w1, d_w2, d_ada_w)])
    p_in = [finish_copies(w_in_grads[l], after, f"exchange_w_in{l}_wait")[0] for l in range(L)]
    p_out[0] = finish_copies(late_grads[0], after, "exchange_late0_wait")[0]
    g_w_in, d_w_in, nm_w_in, nv_w_in = adamw_reduce(w_in, p_in, m_w_in, v_w_in, 256, "adamw_w_in")
    g_w_out, d_w_out, nm_w_out, nv_w_out = adamw_reduce(w_out, p_out, m_w_out, v_w_out, 128, "adamw_w_out")

    as_row = lambda t: t.reshape(1, D)
    small_params = [(ada_b, m_ada_b, v_ada_b), (norm_mix_g, m_norm_mix_g, v_norm_mix_g),
                    (norm_mlp_g, m_norm_mlp_g, v_norm_mlp_g),
                    (as_row(final_norm_g), as_row(m_final_norm_g), as_row(v_final_norm_g)),
                    (conv_b, m_conv_b, v_conv_b), (gmlp_norm_g, m_gmlp_norm_g, v_gmlp_norm_g),
                    (spatial_w, m_spatial_w, v_spatial_w), (spatial_b, m_spatial_b, v_spatial_b)]
    updated, (loss_sum, taps_sum) = small_update(vec_all, w256_all, sb_all, sw_all, small_params, "small_update")
    loss = loss_sum[0, 0]
    u_ada_b, u_norm_mix, u_norm_mlp, u_final, u_conv_b, u_gn, u_sw, u_sb = updated
    u_final = [t.reshape(D) for t in u_final]
    g_conv_w = lax.dynamic_slice(taps_sum, (0, 0, me * conv_shard), (L, 3, conv_shard))
    flat_cw = lambda t: t.reshape(L * 3, conv_shard)
    u_conv_w = [g_conv_w] + [t.reshape(L, 3, conv_shard) for t in adamw_plain(
        flat_cw(conv_w), flat_cw(g_conv_w), flat_cw(m_conv_w), flat_cw(v_conv_w), L * 3, "adamw_conv_w")]
    small_sets = [u_ada_b, u_norm_mix, u_norm_mlp, u_conv_w, u_conv_b, u_gn, u_sw, u_sb, u_final]
    small_g, sd, snm, snv = [[u[k] for u in small_sets] for k in range(4)]

    def ordered(big, small):
        ada, win, wout, w1, w2 = big
        return [ada, small[0], small[1], small[2], win, small[3], small[4], small[5], small[6], small[7],
                wout, w1, w2, small[8]]

    grads = ordered([g_ada_w, g_w_in, g_w_out, g_w1, g_w2], small_g)
    deltas = ordered([d_ada_w, d_w_in, d_w_out, d_w1, d_w2], sd)
    new_m = ordered([nm_ada_w, nm_w_in, nm_w_out, nm_w1, nm_w2], snm)
    new_v = ordered([nv_ada_w, nv_w_in, nv_w_out, nv_w1, nv_w2], snv)
    return (loss, grad_x, *grads, *deltas, *new_m, *new_v)
```

```python
import functools
import math

import jax
import jax.numpy as jnp
from jax import lax
from jax.experimental import pallas as pl
from jax.experimental.pallas import tpu as pltpu

F32 = jnp.float32
BF16 = jnp.bfloat16
MESH = pl.DeviceIdType.MESH

S = 2048
D = 1024
L = 2
NDEV = 8
HD = 64
NH = 8
PROJ = 2816
DFF = 4096
NMOD = 6
EPS = 1e-6
T = 128
SG_HEADS = 4
LANES = 128
CW = 256
QKV = 3 * NH * HD
REST = PROJ - QKV

LR, B1, B2, AEPS, WD, STEP = 0.001, 0.9, 0.999, 1e-08, 0.01, 10
BC1 = 1.0 - B1 ** STEP
BC2 = 1.0 - B2 ** STEP

VMEM_LIMIT = 48 * 1024 * 1024

HBM_SPEC = pl.BlockSpec(memory_space=pltpu.HBM)


def _cparams(sem=None):
    return pltpu.CompilerParams(dimension_semantics=sem, vmem_limit_bytes=VMEM_LIMIT)


def _my_pos():
    return lax.axis_index("x"), lax.axis_index("y"), lax.axis_index("c")


def _lin(p):
    return 4 * p[0] + 2 * p[1] + p[2]


class Gather:
    def __init__(self, arrs):
        self.arrs = list(arrs)
        n = len(self.arrs)
        self.out_shape = [jax.ShapeDtypeStruct((NDEV,) + a.shape, a.dtype) for a in self.arrs]
        self.scratch = [pltpu.SemaphoreType.DMA((n, 7)), pltpu.SemaphoreType.DMA((n, 7)),
                        pltpu.SemaphoreType.DMA((n,))]

    def phases(self, ins, outs, sems):
        n = len(self.arrs)
        send_sems, recv_sems, local_sems = sems
        x, y, c = _my_pos()
        me, sibling = (x, y, c), (x, y, 1 - c)
        chips = [(1 - x, y), (x, 1 - y), (1 - x, 1 - y)]

        def copy(a, k, block, to, src=None):
            slot = outs[a].at[_lin(block)]
            return pltpu.make_async_remote_copy(
                src_ref=slot if src is None else src, dst_ref=slot,
                send_sem=send_sems.at[a, k], recv_sem=recv_sems.at[a, k],
                device_id=to, device_id_type=MESH)

        def mine(a):
            return pltpu.make_async_copy(ins[a], outs[a].at[_lin(me)], local_sems.at[a])

        def first(a):
            return [copy(a, 0, me, sibling, src=ins[a])] + [
                copy(a, 1 + j, me, (*chip, c), src=ins[a]) for j, chip in enumerate(chips)]

        def passed(a):
            return [copy(a, 4 + j, (*chip, c), sibling) for j, chip in enumerate(chips)]

        def start():
            for a in range(n):
                mine(a).start()
                for cp in first(a):
                    cp.start()

        def relay():
            for j, chip in enumerate(chips):
                for a in range(n):
                    copy(a, 1 + j, (*chip, c), me).wait_recv()
                    passed(a)[j].start()

        def finish():
            for a in range(n):
                copy(a, 0, sibling, me).wait_recv()
            for j, chip in enumerate(chips):
                for a in range(n):
                    copy(a, 4 + j, (*chip, 1 - c), me).wait_recv()
            for a in range(n):
                for cp in first(a) + passed(a):
                    cp.wait_send()
                mine(a).wait()

        return start, relay, finish


class Exchange:
    def __init__(self, arrs):
        self.arrs = list(arrs)
        n = len(self.arrs)
        self.out_shape = [jax.ShapeDtypeStruct(a.shape, a.dtype) for a in self.arrs]
        self.scratch = [pltpu.SemaphoreType.DMA((n, 7)), pltpu.SemaphoreType.DMA((n, 7)),
                        pltpu.SemaphoreType.DMA((n,))]

    def phases(self, ins, outs, sems):
        n = len(self.arrs)
        send_sems, recv_sems, local_sems = sems
        x, y, c = _my_pos()
        me = (x, y, c)

        def peer(mask):
            return (1 - x if mask & 4 else x, 1 - y if mask & 2 else y, 1 - c if mask & 1 else c)

        def copy(a, mask):
            return pltpu.make_async_remote_copy(
                src_ref=ins[a].at[_lin(peer(mask))], dst_ref=outs[a].at[_lin(me)],
                send_sem=send_sems.at[a, mask - 1], recv_sem=recv_sems.at[a, mask - 1],
                device_id=peer(mask), device_id_type=MESH)

        def arrival(a, mask):
            return pltpu.make_async_remote_copy(
                src_ref=ins[a].at[_lin(me)], dst_ref=outs[a].at[_lin(peer(mask))],
                send_sem=send_sems.at[a, mask - 1], recv_sem=recv_sems.at[a, mask - 1],
                device_id=peer(mask), device_id_type=MESH)

        def mine(a):
            return pltpu.make_async_copy(ins[a].at[_lin(me)], outs[a].at[_lin(me)], local_sems.at[a])

        def start():
            for a in range(n):
                mine(a).start()
            for mask in (4, 2, 6, 1, 5, 3, 7):
                for a in range(n):
                    copy(a, mask).start()

        def relay():
            pass

        def finish():
            for mask in range(1, 8):
                for a in range(n):
                    arrival(a, mask).wait_recv()
            for mask in range(1, 8):
                for a in range(n):
                    copy(a, mask).wait_send()
            for a in range(n):
                mine(a).wait()

        return start, relay, finish


def run_comm(plan, name):
    n = len(plan.arrs)

    def body(*refs):
        start, relay, finish = plan.phases(refs[:n], refs[n:2 * n], refs[2 * n:])
        start()
        relay()
        finish()

    outs = pl.pallas_call(
        body, name=name, out_shape=plan.out_shape,
        in_specs=[HBM_SPEC] * n, out_specs=[HBM_SPEC] * n, scratch_shapes=plan.scratch,
    )(*plan.arrs)
    return list(outs)


SEM_SPEC = pl.BlockSpec(memory_space=pltpu.SEMAPHORE)
DATAFLOW = pltpu.SideEffectType.DATAFLOW_SIDE_EFFECTING


def _peer_copies(src_ref, land_ref, send_sems, recv_sems, first, same_block):
    x, y, c = _my_pos()
    me = (x, y, c)
    sends, arrivals = [], []
    for mask in (4, 2, 6, 1, 5, 3, 7):
        peer = (1 - x if mask & 4 else x, 1 - y if mask & 2 else y, 1 - c if mask & 1 else c)
        sends.append(pltpu.make_async_remote_copy(
            src_ref=src_ref if same_block else src_ref.at[_lin(peer)], dst_ref=land_ref.at[_lin(me)],
            send_sem=send_sems.at[first + mask - 1], recv_sem=recv_sems.at[first + mask - 1], device_id=peer,
            device_id_type=MESH))
        arrivals.append(pltpu.make_async_remote_copy(
            src_ref=src_ref if same_block else src_ref.at[_lin(me)], dst_ref=land_ref.at[_lin(peer)],
            send_sem=send_sems.at[first + mask - 1], recv_sem=recv_sems.at[first + mask - 1], device_id=peer,
            device_id_type=MESH))
    return sends, arrivals


def start_copies(srcs, me, name, same_block, after=None):
    n = len(srcs)
    landings = []
    for src in srcs:
        own = src[None] if same_block else lax.dynamic_index_in_dim(src, me, axis=0, keepdims=True)
        landings.append(lax.dynamic_update_slice(lax.empty((NDEV,) + own.shape[1:], src.dtype), own,
                                                 (me,) + (0,) * (own.ndim - 1)))

    def body(*refs):
        send_sems, recv_sems = refs[-2 * n - 3], refs[-2 * n - 2]
        token = refs[-1]
        for k in range(n):
            sends, _ = _peer_copies(refs[2 * k], refs[2 * k + 1], send_sems, recv_sems, 7 * k, same_block)
            for cp in sends:
                cp.start()
        token[...] = jnp.zeros_like(token)

    hbm = lambda a: pltpu.HBM(a.shape, a.dtype)
    pairs = [a for pair in zip(srcs, landings) for a in pair]
    extra = [] if after is None else [after]
    sems = pltpu.SemaphoreType.DMA((7 * n,))
    send_sems, recv_sems, *thru, token = pl.pallas_call(
        body, name=name,
        out_shape=(sems, sems, *[hbm(a) for a in pairs], jax.ShapeDtypeStruct((8, LANES), F32)),
        in_specs=[HBM_SPEC] * (2 * n) + [pl.BlockSpec(memory_space=pl.ANY)] * len(extra),
        out_specs=(SEM_SPEC, SEM_SPEC, *[HBM_SPEC] * (2 * n), pl.BlockSpec(memory_space=pltpu.VMEM)),
        input_output_aliases={k: 2 + k for k in range(2 * n)},
        compiler_params=pltpu.CompilerParams(has_side_effects=DATAFLOW),
    )(*[pltpu.with_memory_space_constraint(a, pltpu.HBM) for a in pairs], *extra)
    return (send_sems, recv_sems, thru, same_block), token


def finish_copies(handle, after, name):
    send_sems, recv_sems, thru, same_block = handle
    n = len(thru) // 2

    def body(*refs):
        send_sems, recv_sems = refs[2 * n], refs[2 * n + 1]
        for k in range(n):
            sends, arrivals = _peer_copies(refs[2 * k], refs[2 * k + 1], send_sems, recv_sems, 7 * k, same_block)
            for cp in sends:
                cp.wait_send()
            for cp in arrivals:
                cp.wait_recv()

    hbm = lambda a: pltpu.HBM(a.shape, a.dtype)
    outs = pl.pallas_call(
        body, name=name, out_shape=tuple(hbm(a) for a in thru),
        in_specs=[HBM_SPEC] * (2 * n) + [SEM_SPEC, SEM_SPEC, pl.BlockSpec(memory_space=pl.ANY)],
        out_specs=tuple([HBM_SPEC] * (2 * n)), input_output_aliases={k: k for k in range(2 * n)},
        compiler_params=pltpu.CompilerParams(has_side_effects=DATAFLOW),
    )(*thru, send_sems, recv_sems, after)
    return [outs[2 * k + 1] for k in range(n)]


def tied(x, token):
    return x + token[0:1, 0:1].astype(x.dtype)


MM_TILES = {
    "proj_qkv": (S, 512), "proj_rest": (S, 256), "mix": (1024, 512), "mlp_up": (S, 512), "mlp_down": (1024, 256),
    "mlp_down_dgrad": (1024, 1024), "mlp_down_wgrad": (1024, 1024), "mlp_up_wgrad": (1024, 512),
    "mlp_up_dgrad": (1024, 512), "mix_dgrad": (1024, 512), "mix_wgrad": (512, 1024),
    "proj_wgrad": (1024, PROJ // 2), "proj_dgrad": (1024, 512),
}


def mm_layer(kind, l, a, b, **kw):
    tm, tn = MM_TILES[kind]
    return mm(a, b, tm=tm, tn=tn, name=f"{kind}{l}", **kw)


def mm(a, b, *, tm, tn, out_dtypes, epilogue=None, extras=(), name, trans_a=False, trans_b=False,
       cols=None, b_blocks=False, out_blocks=False):
    if trans_a:
        kdim, m = a.shape
    else:
        m, kdim = a.shape
    shard = b.shape[-1] if b_blocks else None
    if b_blocks:
        full = (b.shape[1], NDEV * shard)
    else:
        full = b.shape
    first, ncols = cols if cols is not None else (0, full[0] if trans_b else full[1])
    assert full[1 if trans_b else 0] == kdim and m % tm == 0 and ncols % tn == 0 and first % tn == 0
    j0 = first // tn
    if trans_a:
        a_spec = pl.BlockSpec((kdim, tm), lambda i, j: (0, i))
    else:
        a_spec = pl.BlockSpec((tm, kdim), lambda i, j: (i, 0))
    if b_blocks and trans_b:
        b_spec = pl.BlockSpec((NDEV, tn, shard), lambda i, j: (0, j0 + j, 0))
    elif b_blocks:
        assert tn == shard
        b_spec = pl.BlockSpec((None, kdim, tn), lambda i, j: (j0 + j, 0, 0))
    elif trans_b:
        b_spec = pl.BlockSpec((tn, kdim), lambda i, j: (j0 + j, 0))
    else:
        b_spec = pl.BlockSpec((kdim, tn), lambda i, j: (0, j0 + j))
    if out_blocks:
        assert tn * NDEV == ncols
        out_spec = pl.BlockSpec((None, tm, tn), lambda i, j: (j, i, 0))
        out_dims = (NDEV, m, tn)
    else:
        out_spec = pl.BlockSpec((tm, tn), lambda i, j: (i, j))
        out_dims = (m, ncols)
    ex_specs = []
    for arr, kind in extras:
        if kind == "tile":
            ex_specs.append(pl.BlockSpec((tm, tn), lambda i, j: (i, j)))
        elif kind == "col":
            ex_specs.append(pl.BlockSpec((1, tn), lambda i, j: (0, j)))
        else:
            ex_specs.append(pl.BlockSpec(arr.shape, lambda i, j: (0, 0)))
    n_ex, n_out = len(extras), len(out_dtypes)
    used = [k for k, (_, kind) in enumerate(extras) if kind != "tie"]

    def body(a_ref, b_ref, *rest):
        ex_refs, out_refs = rest[:n_ex], rest[n_ex:]
        if trans_a:
            acc = lax.dot_general(a_ref[...], b_ref[...], (((0,), (0,)), ((), ())),
                                  preferred_element_type=F32)
        elif trans_b and b_blocks:
            acc = jnp.zeros((tm, tn), F32)
            for d in range(NDEV):
                acc = acc + lax.dot_general(a_ref[:, d * shard:(d + 1) * shard], b_ref[d],
                                            (((1,), (1,)), ((), ())), preferred_element_type=F32)
        elif trans_b:
            acc = lax.dot_general(a_ref[...], b_ref[...], (((1,), (1,)), ((), ())),
                                  preferred_element_type=F32)
        else:
            acc = jnp.dot(a_ref[...], b_ref[...], preferred_element_type=F32)
        outs = (acc,) if epilogue is None else epilogue(acc, *[ex_refs[k][...] for k in used])
        for o_ref, val in zip(out_refs, outs):
            o_ref[...] = val.astype(o_ref.dtype)

    outs = pl.pallas_call(
        body, name=name, grid=(m // tm, ncols // tn),
        in_specs=[a_spec, b_spec] + ex_specs,
        out_specs=[out_spec for _ in range(n_out)],
        out_shape=[jax.ShapeDtypeStruct(out_dims, dt) for dt in out_dtypes],
        compiler_params=_cparams(("parallel", "parallel")),
    )(a, b, *[arr for arr, _ in extras])
    return list(outs)


TR = 256

ROW_SPEC = pl.BlockSpec((TR, D), lambda i: (i, 0))
VEC_SPEC = pl.BlockSpec((1, D), lambda i: (0, 0))


def normmod_fwd(x, g, sc, sh, name):
    def body(x_ref, g_ref, sc_ref, sh_ref, o_ref):
        xv = x_ref[...]
        rstd = lax.rsqrt(jnp.mean(xv * xv, axis=-1, keepdims=True) + EPS)
        n = (xv * rstd) * g_ref[...]
        o_ref[...] = (n * (1.0 + sc_ref[...]) + sh_ref[...]).astype(o_ref.dtype)

    return pl.pallas_call(
        body, name=name, grid=(S // TR,),
        in_specs=[ROW_SPEC, VEC_SPEC, VEC_SPEC, VEC_SPEC], out_specs=ROW_SPEC,
        out_shape=jax.ShapeDtypeStruct((S, D), BF16),
        compiler_params=_cparams(("parallel",)),
    )(x, g, sc, sh)


def normmod_bwd(x, dh, dres, g, sc, name):
    def body(x_ref, dh_ref, dres_ref, g_ref, sc_ref, dx_ref, dsc_ref, dsh_ref, dg_ref):
        @pl.when(pl.program_id(0) == 0)
        def _():
            dsc_ref[...] = jnp.zeros_like(dsc_ref)
            dsh_ref[...] = jnp.zeros_like(dsh_ref)
            dg_ref[...] = jnp.zeros_like(dg_ref)

        xv, dh = x_ref[...], dh_ref[...]
        gv = g_ref[...]
        rstd = lax.rsqrt(jnp.mean(xv * xv, axis=-1, keepdims=True) + EPS)
        xhat = xv * rstd
        dn = dh * (1.0 + sc_ref[...])
        dxhat = dn * gv
        dx_ref[...] = dres_ref[...] + rstd * (dxhat - xhat * jnp.mean(dxhat * xhat, axis=-1, keepdims=True))
        dsc_ref[...] += jnp.sum(dh * (xhat * gv), axis=0, keepdims=True)
        dsh_ref[...] += jnp.sum(dh, axis=0, keepdims=True)
        dg_ref[...] += jnp.sum(dn * xhat, axis=0, keepdims=True)

    vec_out = jax.ShapeDtypeStruct((1, D), F32)
    return pl.pallas_call(
        body, name=name, grid=(S // TR,),
        in_specs=[ROW_SPEC, ROW_SPEC, ROW_SPEC, VEC_SPEC, VEC_SPEC],
        out_specs=[ROW_SPEC, VEC_SPEC, VEC_SPEC, VEC_SPEC],
        out_shape=[jax.ShapeDtypeStruct((S, D), F32), vec_out, vec_out, vec_out],
        compiler_params=_cparams(("arbitrary",)),
    )(x, dh, dres, g, sc)


def gate_bwd(dx, branch, gate, name):
    def body(dx_ref, br_ref, gate_ref, o_ref, dgate_ref):
        @pl.when(pl.program_id(0) == 0)
        def _():
            dgate_ref[...] = jnp.zeros_like(dgate_ref)

        dxv = dx_ref[...]
        o_ref[...] = (dxv * gate_ref[...]).astype(o_ref.dtype)
        dgate_ref[...] += jnp.sum(dxv * br_ref[...], axis=0, keepdims=True)

    return pl.pallas_call(
        body, name=name, grid=(S // TR,),
        in_specs=[ROW_SPEC, ROW_SPEC, VEC_SPEC], out_specs=[ROW_SPEC, VEC_SPEC],
        out_shape=[jax.ShapeDtypeStruct((S, D), BF16), jax.ShapeDtypeStruct((1, D), F32)],
        compiler_params=_cparams(("arbitrary",)),
    )(dx, branch, gate)


def loss_head(x, target, g, name):
    def body(x_ref, t_ref, g_ref, dx_ref, loss_ref, dg_ref):
        @pl.when(pl.program_id(0) == 0)
        def _():
            loss_ref[...] = jnp.zeros_like(loss_ref)
            dg_ref[...] = jnp.zeros_like(dg_ref)

        xv, gv = x_ref[...], g_ref[...]
        rstd = lax.rsqrt(jnp.mean(xv * xv, axis=-1, keepdims=True) + EPS)
        xhat = xv * rstd
        err = xhat * gv - t_ref[...]
        loss_ref[...] += jnp.sum(err * err) * (0.5 / D)
        dy = err * (1.0 / D)
        dg_ref[...] += jnp.sum(dy * xhat, axis=0, keepdims=True)
        dxhat = dy * gv
        dx_ref[...] = rstd * (dxhat - xhat * jnp.mean(dxhat * xhat, axis=-1, keepdims=True))

    return pl.pallas_call(
        body, name=name, grid=(S // TR,),
        in_specs=[ROW_SPEC, ROW_SPEC, VEC_SPEC],
        out_specs=[ROW_SPEC, VEC_SPEC, VEC_SPEC],
        out_shape=[jax.ShapeDtypeStruct((S, D), F32), jax.ShapeDtypeStruct((1, D), F32),
                   jax.ShapeDtypeStruct((1, D), F32)],
        compiler_params=_cparams(("arbitrary",)),
    )(x, target, g)


TQ = 512
RS = 128
NSUB = TQ // RS
TK = 128


def _dot_hilo(a, tri_twice):
    hi = a.astype(BF16)
    lo = (a - hi.astype(F32)).astype(BF16)
    return jnp.dot(jnp.concatenate([hi, lo], axis=1), tri_twice, preferred_element_type=F32)


def _log_stay(z):
    return -(jnp.maximum(z, 0.0) + jnp.log(1.0 + jnp.exp(-jnp.abs(z))))


def _tri_and_ones(kind):
    row = jnp.bitwise_and(lax.broadcasted_iota(jnp.int32, (2 * TK, 2 * TK), 0), TK - 1)
    col = lax.broadcasted_iota(jnp.int32, (2 * TK, 2 * TK), 1)
    tri = {"after": row > col, "upto": row <= col, "before": row < col}[kind]
    return jnp.logical_or(col >= TK, tri).astype(BF16)


NPAIR = NH // 2
SCALE = HD ** -0.5


def _pair_specs(first_block):
    rows = pl.BlockSpec((TQ, LANES), lambda p, i: (i, first_block + p))
    whole = pl.BlockSpec((S, LANES), lambda p, i: (0, first_block + p))
    return rows, whole


Q_ROWS_SPEC, _ = _pair_specs(0)
_, K_ALL_SPEC = _pair_specs(NPAIR)
_, V_ALL_SPEC = _pair_specs(2 * NPAIR)
PAIR_ROWS_SPEC = pl.BlockSpec((TQ, LANES), lambda p, i: (i, p))
PAIR_ALL_SPEC = pl.BlockSpec((S, LANES), lambda p, i: (0, p))
PAIR_TOTAL_SPEC = pl.BlockSpec((2, TQ, TK), lambda p, i: (p, i, 0))


def _head_halves(x):
    first = lax.broadcasted_iota(jnp.int32, x.shape, 1) < HD
    zero = jnp.zeros_like(x)
    return jnp.where(first, x, zero), jnp.where(first, zero, x)


def _join_heads(a, b):
    return jnp.where(lax.broadcasted_iota(jnp.int32, a.shape, 1) < HD, a, b)


def _comm_hooks(comm, refs, n_in, n_out, n_scratch):
    nc = len(comm.arrs) if comm is not None else 0
    ins, cin = refs[:n_in], refs[n_in:n_in + nc]
    outs = refs[n_in + nc:n_in + nc + n_out]
    cout = refs[n_in + nc + n_out:n_in + 2 * nc + n_out]
    scratch = refs[n_in + 2 * nc + n_out:n_in + 2 * nc + n_out + n_scratch]
    sems = refs[n_in + 2 * nc + n_out + n_scratch:]
    phases = comm.phases(cin, cout, sems) if comm is not None else None
    return ins, outs, scratch, phases


def _with_comm(comm, in_specs, out_specs, out_shape, operands, scratch):
    if comm is None:
        return dict(in_specs=in_specs, out_specs=out_specs, out_shape=out_shape, scratch_shapes=scratch), operands
    nc = len(comm.arrs)
    return dict(in_specs=in_specs + [HBM_SPEC] * nc, out_specs=out_specs + [HBM_SPEC] * nc,
                out_shape=out_shape + comm.out_shape, scratch_shapes=scratch + comm.scratch), operands + comm.arrs


def attn_fwd(qkv, name, comm=None):
    n_steps = S // TQ

    def body(*refs):
        (q_ref, k_ref, v_ref), (o_ref, r_ref), (acc_ref, z_even, z_odd, w_ref), phases = _comm_hooks(
            comm, refs, 3, 2, 4)
        p = pl.program_id(0)
        i = pl.program_id(1)
        if phases is not None:
            pl.when(jnp.logical_and(p == 0, i == 0))(phases[0])
            pl.when(jnp.logical_and(p == NPAIR - 1, i == n_steps - 2))(phases[1])
        chains = [(sub, h) for sub in range(NSUB) for h in range(2)]
        q_sub = [_head_halves(q_ref[pl.ds(sub * RS, RS), :] * SCALE) for sub in range(NSUB)]
        after = _tri_and_ones("after")
        below_diagonal = (lax.broadcasted_iota(jnp.int32, (RS, TK), 1)
                          < lax.broadcasted_iota(jnp.int32, (RS, TK), 0))
        base = i * NSUB
        all_subs = list(range(NSUB))

        acc_ref[...] = jnp.zeros_like(acc_ref)
        r_ref[...] = jnp.zeros_like(r_ref)
        w_ref[...] = jnp.zeros_like(w_ref)

        def key_rows(block):
            return pl.ds(pl.multiple_of(block * TK, TK), TK)

        def store_scores(z_ref, block, subs):
            kb = k_ref[key_rows(block), :]
            for c, (sub, h) in enumerate(chains):
                if sub in subs:
                    z_ref[c] = lax.dot_general(q_sub[sub][h], kb, (((1,), (1,)), ((), ())),
                                               preferred_element_type=F32)

        def add_weighted_values(block, subs):
            vb = v_ref[key_rows(block), :]
            for sub in subs:
                acc_ref[pl.ds(sub * RS, RS), :] += _join_heads(*[
                    jnp.dot(w_ref[2 * sub + h], vb, preferred_element_type=F32) for h in range(2)])

        def step(block, z_ref, z_next_ref, subs, diagonal_sub, prev_subs, next_subs):
            if prev_subs:
                add_weighted_values(block + 1, prev_subs)
            if next_subs:
                store_scores(z_next_ref, jnp.maximum(block - 1, 0), next_subs)
            active = [(c, sub, h) for c, (sub, h) in enumerate(chains) if sub in subs]
            ls, sums = {}, {}
            for c, sub, h in active:
                ls[c] = _log_stay(z_ref[c])
                sums[c] = _dot_hilo(jnp.where(below_diagonal, ls[c], 0.0) if sub == diagonal_sub else ls[c], after)
            for c, sub, h in active:
                rows = pl.ds(sub * RS, RS)
                later = r_ref[h, rows, :]
                w = jnp.exp(z_ref[c] + ls[c] + (sums[c][:, :TK] + later))
                if sub == diagonal_sub:
                    w = jnp.where(below_diagonal, w, 0.0)
                w_ref[c] = w.astype(BF16)
                r_ref[h, rows, :] = later + sums[c][:, TK:]

        store_scores(z_even, base + NSUB - 1, [NSUB - 1])
        buffers = (z_even, z_odd)
        for j in reversed(range(NSUB)):
            subs = all_subs[j:]
            step(base + j, buffers[0], buffers[1], subs, j, all_subs[j + 1:], all_subs[j - 1:] if j else all_subs)
            buffers = buffers[::-1]
        assert buffers[0] is z_even

        @pl.loop(0, base // 2)
        def _(pair):
            block = base - 1 - 2 * pair
            step(block, z_even, z_odd, all_subs, None, all_subs, all_subs)
            step(block - 1, z_odd, z_even, all_subs, None, all_subs, all_subs)

        add_weighted_values(0, all_subs)
        o_ref[...] = acc_ref[...].astype(o_ref.dtype)
        if phases is not None:
            pl.when(jnp.logical_and(p == NPAIR - 1, i == n_steps - 1))(phases[2])

    kwargs, operands = _with_comm(
        comm, [Q_ROWS_SPEC, K_ALL_SPEC, V_ALL_SPEC], [PAIR_ROWS_SPEC, PAIR_TOTAL_SPEC],
        [jax.ShapeDtypeStruct((S, NH * HD), BF16), jax.ShapeDtypeStruct((NH, S, TK), F32)], [qkv, qkv, qkv],
        [pltpu.VMEM((TQ, LANES), F32), pltpu.VMEM((2 * NSUB, RS, TK), F32), pltpu.VMEM((2 * NSUB, RS, TK), F32),
         pltpu.VMEM((2 * NSUB, RS, TK), BF16)])
    return pl.pallas_call(
        body, name=name, grid=(NPAIR, n_steps),
        compiler_params=_cparams(("arbitrary", "arbitrary")), **kwargs,
    )(*operands)


def attn_bwd(qkv, dout, totals, name, comm=None):
    n_steps = S // TQ

    def body(*refs):
        ((q_ref, k_ref, v_ref, do_ref, r_ref), (dq_ref, dk_ref, dv_ref),
         (z_even, z_odd, dw_even, dw_odd, before_ref, dbefore_ref, dz_ref, w_ref), phases) = _comm_hooks(
            comm, refs, 5, 3, 8)
        p = pl.program_id(0)
        i = pl.program_id(1)
        if phases is not None:
            pl.when(jnp.logical_and(p == 0, i == 0))(phases[0])
            pl.when(jnp.logical_and(p == NPAIR - 1, i == n_steps - 2))(phases[1])

        @pl.when(i == 0)
        def _():
            dk_ref[...] = jnp.zeros_like(dk_ref)
            dv_ref[...] = jnp.zeros_like(dv_ref)

        chains = [(sub, h) for sub in range(NSUB) for h in range(2)]
        nch = len(chains)
        qb = q_ref[...]
        dob = do_ref[...].astype(BF16)
        q_sub = [_head_halves(qb[sub * RS:(sub + 1) * RS] * SCALE) for sub in range(NSUB)]
        do_sub = [_head_halves(dob[sub * RS:(sub + 1) * RS]) for sub in range(NSUB)]
        upto = _tri_and_ones("upto")
        before_tri = _tri_and_ones("before")
        below_diagonal = (lax.broadcasted_iota(jnp.int32, (RS, TK), 1)
                          < lax.broadcasted_iota(jnp.int32, (RS, TK), 0))
        contract_lanes = (((1,), (1,)), ((), ()))
        contract_rows = (((0,), (0,)), ((), ()))
        base = i * NSUB
        all_subs = list(range(NSUB))

        def key_rows(block):
            return pl.ds(pl.multiple_of(block * TK, TK), TK)

        def store_products(bufs, block, subs):
            z_ref, dw_ref = bufs
            kb = k_ref[key_rows(block), :]
            vb = v_ref[key_rows(block), :]
            for c, (sub, h) in enumerate(chains):
                if sub in subs:
                    z_ref[c] = lax.dot_general(q_sub[sub][h], kb, contract_lanes, preferred_element_type=F32)
                    dw_ref[c] = lax.dot_general(do_sub[sub][h], vb, contract_lanes, preferred_element_type=F32)

        def add_gradients(block, subs):
            kb = k_ref[key_rows(block), :]
            for sub in subs:
                rows = pl.ds(sub * RS, RS)
                dq_ref[rows, :] += _join_heads(*[jnp.dot(dz_ref[h, rows, :], kb, preferred_element_type=F32)
                                                 for h in range(2)])
            dk_ref[key_rows(block), :] += _join_heads(*[
                lax.dot_general(dz_ref[h], qb, contract_rows, preferred_element_type=F32) for h in range(2)])
            dv_ref[key_rows(block), :] += _join_heads(*[
                lax.dot_general(w_ref[h], dob, contract_rows, preferred_element_type=F32) for h in range(2)])

        for ref in (dq_ref, before_ref, dbefore_ref, dz_ref, w_ref):
            ref[...] = jnp.zeros_like(ref)
        even, odd = (z_even, dw_even), (z_odd, dw_odd)
        store_products(even, 0, all_subs)

        def step(block, bufs, next_bufs, subs, diagonal_sub, prev_subs, next_subs):
            z_ref, dw_ref = bufs
            add_gradients(jnp.maximum(block - 1, 0), prev_subs)
            for sub in prev_subs:
                if sub not in subs:
                    dz_ref[:, pl.ds(sub * RS, RS), :] = jnp.zeros((2, RS, TK), BF16)
                    w_ref[:, pl.ds(sub * RS, RS), :] = jnp.zeros((2, RS, TK), BF16)
            if next_subs:
                store_products(next_bufs, block + 1, next_subs)
            active = [(c, sub, h) for c, (sub, h) in enumerate(chains) if sub in subs]
            ls, sums, dl, dsums = {}, {}, {}, {}
            for c, sub, h in active:
                ls[c] = _log_stay(z_ref[c])
                sums[c] = _dot_hilo(jnp.where(below_diagonal, ls[c], 0.0) if sub == diagonal_sub else ls[c], upto)
            for c, sub, h in active:
                rows = pl.ds(sub * RS, RS)
                before = before_ref[c]
                log_after = r_ref[h, rows, :] - (sums[c][:, :TK] + before)
                w = jnp.exp((z_ref[c] + ls[c]) + log_after)
                if sub == diagonal_sub:
                    w = jnp.where(below_diagonal, w, 0.0)
                dl[c] = dw_ref[c] * w
                dsums[c] = _dot_hilo(dl[c], before_tri)
                w_ref[h, rows, :] = w.astype(BF16)
                before_ref[c] = before + sums[c][:, TK:]
            for c, sub, h in active:
                rows = pl.ds(sub * RS, RS)
                dbefore = dbefore_ref[c]
                beta = jnp.exp(z_ref[c] + ls[c])
                if sub == diagonal_sub:
                    beta = jnp.where(below_diagonal, beta, 0.0)
                dstay = dsums[c][:, :TK] + dbefore
                dz_ref[h, rows, :] = ((dl[c] * (1.0 - beta) - beta * dstay) * SCALE).astype(BF16)
                dbefore_ref[c] = dbefore + dsums[c][:, TK:]

        @pl.loop(0, base // 2)
        def _(pair):
            step(2 * pair, even, odd, all_subs, None, all_subs, all_subs)
            step(2 * pair + 1, odd, even, all_subs, None, all_subs, all_subs)

        bufs = (even, odd)
        for j in range(NSUB):
            step(base + j, bufs[0], bufs[1], all_subs[j:], j, all_subs[j - 1:] if j else all_subs, all_subs[j + 1:])
            bufs = bufs[::-1]

        add_gradients(base + NSUB - 1, all_subs[NSUB - 1:])
        if phases is not None:
            pl.when(jnp.logical_and(p == NPAIR - 1, i == n_steps - 1))(phases[2])

    full = jax.ShapeDtypeStruct((S, NH * HD), F32)
    kwargs, operands = _with_comm(
        comm, [Q_ROWS_SPEC, K_ALL_SPEC, V_ALL_SPEC, PAIR_ROWS_SPEC, PAIR_TOTAL_SPEC],
        [PAIR_ROWS_SPEC, PAIR_ALL_SPEC, PAIR_ALL_SPEC], [full, full, full], [qkv, qkv, qkv, dout, totals],
        [pltpu.VMEM((2 * NSUB, RS, TK), F32)] * 6 + [pltpu.VMEM((2, TQ, TK), BF16)] * 2)
    return pl.pallas_call(
        body, name=name, grid=(NPAIR, n_steps),
        compiler_params=_cparams(("arbitrary", "arbitrary")), **kwargs,
    )(*operands)


def _proj_cols(first_col):
    base = first_col // LANES
    return pl.BlockSpec((S, LANES), lambda j: (0, base + j))


CONV_OUT_SPEC = pl.BlockSpec((S, LANES), lambda j: (0, j))
CONV_DOUT_SPEC = pl.BlockSpec((S, LANES), lambda j: (0, (NH * HD) // LANES + j))
CONV_W_SPEC = pl.BlockSpec((8, LANES), lambda j: (0, j))
CONV_B_SPEC = pl.BlockSpec((1, LANES), lambda j: (0, j))


def _shift_down(u, n):
    rows = lax.broadcasted_iota(jnp.int32, u.shape, 0)
    return jnp.where(rows >= n, pltpu.roll(u, n, 0), 0.0)


def _shift_up(u, n):
    rows = lax.broadcasted_iota(jnp.int32, u.shape, 0)
    return jnp.where(rows < S - n, pltpu.roll(u, S - n, 0), 0.0)


def conv_fwd(proj, cw8, cb, name):
    def body(bg_ref, cg_ref, hc_ref, w_ref, b_ref, o_ref):
        u = cg_ref[...] * hc_ref[...]
        w = w_ref[...]
        y = w[0:1, :] * _shift_down(u, 2) + w[1:2, :] * _shift_down(u, 1) + w[2:3, :] * u + b_ref[...]
        o_ref[...] = bg_ref[...] * y

    return pl.pallas_call(
        body, name=name, grid=(CW // LANES,),
        in_specs=[_proj_cols(0), _proj_cols(CW), _proj_cols(2 * CW), CONV_W_SPEC, CONV_B_SPEC],
        out_specs=CONV_OUT_SPEC, out_shape=jax.ShapeDtypeStruct((S, CW), F32),
        compiler_params=_cparams(("parallel",)),
    )(proj, proj, proj, cw8, cb)


def conv_bwd(proj, dout, cw8, cb, name):
    def body(bg_ref, cg_ref, hc_ref, do_ref, w_ref, b_ref, dbg_ref, dcg_ref, dhc_ref, dw_ref, db_ref):
        cg, hc, do = cg_ref[...], hc_ref[...], do_ref[...]
        w = w_ref[...]
        u = cg * hc
        u1, u2 = _shift_down(u, 1), _shift_down(u, 2)
        y = w[0:1, :] * u2 + w[1:2, :] * u1 + w[2:3, :] * u + b_ref[...]
        dbg_ref[...] = do * y
        dy = do * bg_ref[...]
        db_ref[...] = jnp.sum(dy, axis=0, keepdims=True)
        dw_ref[...] = jnp.concatenate(
            [jnp.sum(dy * u2, axis=0, keepdims=True), jnp.sum(dy * u1, axis=0, keepdims=True),
             jnp.sum(dy * u, axis=0, keepdims=True), jnp.zeros((5, LANES), F32)], axis=0)
        du = w[2:3, :] * dy + w[1:2, :] * _shift_up(dy, 1) + w[0:1, :] * _shift_up(dy, 2)
        dcg_ref[...] = du * hc
        dhc_ref[...] = du * cg

    full = jax.ShapeDtypeStruct((S, CW), F32)
    return pl.pallas_call(
        body, name=name, grid=(CW // LANES,),
        in_specs=[_proj_cols(0), _proj_cols(CW), _proj_cols(2 * CW), CONV_DOUT_SPEC, CONV_W_SPEC, CONV_B_SPEC],
        out_specs=[CONV_OUT_SPEC, CONV_OUT_SPEC, CONV_OUT_SPEC, CONV_W_SPEC, CONV_B_SPEC],
        out_shape=[full, full, full, jax.ShapeDtypeStruct((8, CW), F32), jax.ShapeDtypeStruct((1, CW), F32)],
        compiler_params=_cparams(("parallel",)),
    )(proj, proj, proj, dout, cw8, cb)


GELU_K = math.sqrt(2.0 / math.pi)
GELU_C = 0.044715


def _gelu(x):
    return 0.5 * x * (1.0 + jnp.tanh(GELU_K * (x + GELU_C * (x * x * x))))


def _gelu_grad(x):
    t = jnp.tanh(GELU_K * (x + GELU_C * (x * x * x)))
    return 0.5 * (1.0 + t) + 0.5 * x * (1.0 - t * t) * (GELU_K * (1.0 + 3.0 * GELU_C * (x * x)))


def _sg_masks():
    row = lax.broadcasted_iota(jnp.int32, (T, T), 0)
    col = lax.broadcasted_iota(jnp.int32, (T, T), 1)
    causal = jnp.right_shift(row, 6) >= jnp.right_shift(col, 6)
    head_of_col = jnp.right_shift(lax.broadcasted_iota(jnp.int32, (T, CW), 1), 6)
    return causal, head_of_col


def _sg_mixed(vnb, sw_ref, bias, causal, head_of_col):
    mixed = bias
    for h in range(SG_HEADS):
        wh = jnp.where(causal, sw_ref[h], 0.0).astype(BF16)
        mh = jnp.dot(wh, vnb, preferred_element_type=F32)
        mixed = mixed + jnp.where(head_of_col == h, mh, 0.0)
    return mixed


SG_U_SPEC = pl.BlockSpec((T, CW), lambda n: (n, 3))
SG_V_SPEC = pl.BlockSpec((T, CW), lambda n: (n, 4))
SG_ROW_SPEC = pl.BlockSpec((T, CW), lambda n: (n, 0))
SG_DOUT_SPEC = pl.BlockSpec((T, CW), lambda n: (n, 3))
SG_G_SPEC = pl.BlockSpec((1, CW), lambda n: (0, 0))
SG_W_SPEC = pl.BlockSpec((SG_HEADS, T, T), lambda n: (0, 0, 0))
SG_BIAS_SPEC = pl.BlockSpec((T, CW), lambda n: (0, 0))


def sg_fwd(proj, gn, sw, bias, name):
    def body(u_ref, v_ref, g_ref, sw_ref, bias_ref, o_ref):
        causal, head_of_col = _sg_masks()
        gv = _gelu(v_ref[...])
        rstd = lax.rsqrt(jnp.mean(gv * gv, axis=-1, keepdims=True) + EPS)
        vnb = ((gv * rstd) * g_ref[...]).astype(BF16)
        mixed = _sg_mixed(vnb, sw_ref, bias_ref[...], causal, head_of_col)
        o_ref[...] = _gelu(u_ref[...]) * mixed

    return pl.pallas_call(
        body, name=name, grid=(S // T,),
        in_specs=[SG_U_SPEC, SG_V_SPEC, SG_G_SPEC, SG_W_SPEC, SG_BIAS_SPEC],
        out_specs=SG_ROW_SPEC, out_shape=jax.ShapeDtypeStruct((S, CW), F32),
        compiler_params=_cparams(("parallel",)),
    )(proj, proj, gn, sw, bias)


def sg_bwd(proj, dout, gn, sw, bias, name):
    def body(u_ref, v_ref, do_ref, g_ref, sw_ref, bias_ref, du_ref, dv_ref, dg_ref, dsw_ref, dbias_ref):
        @pl.when(pl.program_id(0) == 0)
        def _():
            dg_ref[...] = jnp.zeros_like(dg_ref)
            dsw_ref[...] = jnp.zeros_like(dsw_ref)
            dbias_ref[...] = jnp.zeros_like(dbias_ref)

        causal, head_of_col = _sg_masks()
        uv, vv, do, gnv = u_ref[...], v_ref[...], do_ref[...], g_ref[...]
        gv = _gelu(vv)
        rstd = lax.rsqrt(jnp.mean(gv * gv, axis=-1, keepdims=True) + EPS)
        xhat = gv * rstd
        vnb = (xhat * gnv).astype(BF16)
        mixed = _sg_mixed(vnb, sw_ref, bias_ref[...], causal, head_of_col)
        du_ref[...] = (do * mixed) * _gelu_grad(uv)
        dmix = do * _gelu(uv)
        dbias_ref[...] += dmix
        dmixb = dmix.astype(BF16)
        dvn = jnp.zeros((T, CW), F32)
        for h in range(SG_HEADS):
            wh = jnp.where(causal, sw_ref[h], 0.0).astype(BF16)
            dvh = lax.dot_general(wh, dmixb, (((0,), (0,)), ((), ())), preferred_element_type=F32)
            dvn = dvn + jnp.where(head_of_col == h, dvh, 0.0)
            dmh = jnp.where(head_of_col == h, dmixb, jnp.zeros_like(dmixb))
            dwh = lax.dot_general(dmh, vnb, (((1,), (1,)), ((), ())), preferred_element_type=F32)
            dsw_ref[h] += jnp.where(causal, dwh, 0.0)
        dg_ref[...] += jnp.sum(dvn * xhat, axis=0, keepdims=True)
        dxhat = dvn * gnv
        dgv = rstd * (dxhat - xhat * jnp.mean(dxhat * xhat, axis=-1, keepdims=True))
        dv_ref[...] = dgv * _gelu_grad(vv)

    full = jax.ShapeDtypeStruct((S, CW), F32)
    return pl.pallas_call(
        body, name=name, grid=(S // T,),
        in_specs=[SG_U_SPEC, SG_V_SPEC, SG_DOUT_SPEC, SG_G_SPEC, SG_W_SPEC, SG_BIAS_SPEC],
        out_specs=[SG_ROW_SPEC, SG_ROW_SPEC, SG_G_SPEC, SG_W_SPEC, SG_BIAS_SPEC],
        out_shape=[full, full, jax.ShapeDtypeStruct((1, CW), F32),
                   jax.ShapeDtypeStruct((SG_HEADS, T, T), F32), jax.ShapeDtypeStruct((T, CW), F32)],
        compiler_params=_cparams(("arbitrary",)),
    )(proj, proj, dout, gn, sw, bias)


ADA_COLS = NMOD * D // NDEV


def ada_fwd(c_all, ada_w, ada_b_mine, name):
    def body(c_ref, w_ref, b_ref, o_ref, ca_ref):
        cv = c_ref[...]
        ca = cv * (1.0 / (1.0 + jnp.exp(-cv)))
        ca_ref[...] = ca
        cab = ca.astype(BF16)
        for l in range(L):
            o_ref[l] = jnp.dot(cab, w_ref[l].astype(BF16), preferred_element_type=F32) + b_ref[l]

    return pl.pallas_call(
        body, name=name,
        out_shape=[jax.ShapeDtypeStruct((L, NDEV, ADA_COLS), F32), jax.ShapeDtypeStruct((NDEV, D), F32)],
        compiler_params=_cparams(),
    )(c_all, ada_w, ada_b_mine)


def ada_bwd(ca, dmod_cols, name):
    def body(ca_ref, dm_ref, o_ref):
        cab = ca_ref[...].astype(BF16)
        for l in range(L):
            o_ref[l] = lax.dot_general(cab, dm_ref[l].astype(BF16), (((0,), (0,)), ((), ())),
                                       preferred_element_type=F32)

    return pl.pallas_call(
        body, name=name, out_shape=jax.ShapeDtypeStruct((L, D, ADA_COLS), F32),
        compiler_params=_cparams(),
    )(ca, dmod_cols)


def _adamw(w, g, m, v):
    m = B1 * m + (1.0 - B1) * g
    v = B2 * v + (1.0 - B2) * (g * g)
    m_hat = m / BC1
    v_hat = v / BC2
    delta = -LR * (m_hat / (jnp.sqrt(v_hat) + AEPS) + WD * w)
    return delta, m, v


VEC_ROWS_PER_LAYER = 8
VEC_FINAL_ROW = L * VEC_ROWS_PER_LAYER
VEC_ROWS = VEC_FINAL_ROW + 8
W256_TAPS, W256_CONV_B, W256_GN = 0, 8, 9
W256_ROWS_PER_LAYER = 16


def small_update(vec_all, w256_all, sb_all, sw_all, params, name):
    n_par = len(params)

    def body(*refs):
        vec_ref, w256_ref, sb_ref = refs[:3]
        sw_refs = refs[3:3 + L]
        par_refs = [refs[3 + L + 3 * k:3 + L + 3 * k + 3] for k in range(n_par)]
        out = refs[3 + L + 3 * n_par:]
        out_par = [out[4 * k:4 * k + 4] for k in range(n_par)]
        loss_ref, taps_ref = out[4 * n_par:]

        def total(ref, idx):
            acc = ref[(0,) + idx].astype(F32)
            for d in range(1, NDEV):
                acc = acc + ref[(d,) + idx].astype(F32)
            return acc

        def update(k, region, g):
            w_ref, m_ref, v_ref = par_refs[k]
            g_ref, d_ref, nm_ref, nv_ref = out_par[k]
            delta, nm, nv = _adamw(w_ref[region], g, m_ref[region], v_ref[region])
            g_ref[region] = g
            d_ref[region] = delta
            nm_ref[region] = nm
            nv_ref[region] = nv

        for l in range(L):
            base = l * VEC_ROWS_PER_LAYER
            for k in range(NMOD):
                update(0, (slice(l, l + 1), slice(k * D, (k + 1) * D)), total(vec_ref, (slice(base + k, base + k + 1),)))
            update(1, (slice(l, l + 1),), total(vec_ref, (slice(base + 6, base + 7),)))
            update(2, (slice(l, l + 1),), total(vec_ref, (slice(base + 7, base + 8),)))
            wbase = l * W256_ROWS_PER_LAYER
            update(4, (slice(l, l + 1),), total(w256_ref, (slice(wbase + W256_CONV_B, wbase + W256_CONV_B + 1),)))
            update(5, (slice(l, l + 1),), total(w256_ref, (slice(wbase + W256_GN, wbase + W256_GN + 1),)))
            update(6, (l,), total(sw_refs[l], ()))
            update(7, (l,), total(sb_ref, (slice(l * SG_HEADS, (l + 1) * SG_HEADS),)))
            taps_ref[l] = total(w256_ref, (slice(wbase + W256_TAPS, wbase + W256_TAPS + 8),))
        update(3, (slice(0, 1),), total(vec_ref, (slice(VEC_FINAL_ROW, VEC_FINAL_ROW + 1),)))
        loss_ref[...] = total(vec_ref, (slice(VEC_FINAL_ROW + 1, VEC_FINAL_ROW + 2), slice(0, LANES)))

    out_shape = []
    for w, _, _ in params:
        out_shape += [jax.ShapeDtypeStruct(w.shape, F32)] * 4
    out_shape += [jax.ShapeDtypeStruct((1, LANES), F32), jax.ShapeDtypeStruct((L, 8, CW), F32)]
    outs = pl.pallas_call(body, name=name, out_shape=out_shape, compiler_params=_cparams())(
        vec_all, w256_all, sb_all, *sw_all, *[a for p in params for a in p])
    return [outs[4 * k:4 * k + 4] for k in range(n_par)], outs[4 * n_par:]


def adamw_plain(w, g, m, v, tr, name):
    rows, cols = w.shape
    spec = pl.BlockSpec((tr, cols), lambda i: (i, 0))

    def body(w_ref, g_ref, m_ref, v_ref, d_ref, nm_ref, nv_ref):
        delta, nm, nv = _adamw(w_ref[...], g_ref[...], m_ref[...], v_ref[...])
        d_ref[...] = delta
        nm_ref[...] = nm
        nv_ref[...] = nv

    shp = jax.ShapeDtypeStruct((rows, cols), F32)
    return pl.pallas_call(
        body, name=name, grid=(rows // tr,), in_specs=[spec] * 4, out_specs=[spec] * 3,
        out_shape=[shp, shp, shp], compiler_params=_cparams(("parallel",)),
    )(w, g, m, v)


def adamw_reduce(w, parts, m, v, tr, name, tie=None):
    _, rows, cols = w.shape
    spec = pl.BlockSpec((None, tr, cols), lambda l, i: (l, i, 0))
    pspecs = [pl.BlockSpec((NDEV, tr, cols), lambda l, i, k=k: (0, jnp.where(l == k, i, 0), 0)) for k in range(L)]

    ties = [] if tie is None else [tie]

    def body(w_ref, p0_ref, p1_ref, m_ref, v_ref, *rest):
        g_ref, d_ref, nm_ref, nv_ref = rest[len(ties):]
        first_layer = pl.program_id(0) == 0
        g = jnp.zeros((tr, cols), F32)
        for d in range(NDEV):
            g = g + jnp.where(first_layer, p0_ref[d], p1_ref[d]).astype(F32)
        delta, nm, nv = _adamw(w_ref[...], g, m_ref[...], v_ref[...])
        g_ref[...] = g
        d_ref[...] = delta
        nm_ref[...] = nm
        nv_ref[...] = nv

    shp = jax.ShapeDtypeStruct(w.shape, F32)
    return pl.pallas_call(
        body, name=name, grid=(L, rows // tr),
        in_specs=[spec] + pspecs + [spec, spec] + [pl.BlockSpec(t.shape, lambda l, i: (0, 0)) for t in ties],
        out_specs=[spec] * 4, out_shape=[shp] * 4, compiler_params=_cparams(("parallel", "parallel")),
    )(w, *parts, m, v, *ties)


def _pad_rows(flat, rows):
    return jnp.pad(flat, (0, rows * LANES - flat.shape[0])).reshape(rows, LANES)


def kernel(x, c, ada_w, ada_b, norm_mix_g, norm_mlp_g, w_in, conv_w, conv_b, gmlp_norm_g, spatial_w, spatial_b, w_out, mlp_w1, mlp_w2, final_norm_g, loss_target, m_ada_w, m_ada_b, m_norm_mix_g, m_norm_mlp_g, m_w_in, m_conv_w, m_conv_b, m_gmlp_norm_g, m_spatial_w, m_spatial_b, m_w_out, m_mlp_w1, m_mlp_w2, m_final_norm_g, v_ada_w, v_ada_b, v_norm_mix_g, v_norm_mlp_g, v_w_in, v_conv_w, v_conv_b, v_gmlp_norm_g, v_spatial_w, v_spatial_b, v_w_out, v_mlp_w1, v_mlp_w2, v_final_norm_g):
    me = _lin(_my_pos())
    x0 = x[0]
    target = loss_target[0]
    conv_shard = conv_w.shape[-1]

    w_in_b, w_out_b, w1_b, w2_b = [w.astype(BF16) for w in (w_in, w_out, mlp_w1, mlp_w2)]
    pack0 = _pad_rows(jnp.concatenate([c.reshape(-1), conv_w.reshape(-1)]), 16)
    g0, gw_in0 = run_comm(Gather([pack0, w_in_b[0]]), "gather_first")
    g0 = g0.reshape(NDEV, 16 * LANES)
    c_all = g0[:, :D]
    conv_full = (g0[:, D:D + L * 3 * conv_shard].reshape(NDEV, L, 3, conv_shard)
                 .transpose(1, 2, 0, 3).reshape(L, 3, CW))

    def canonical_w_in(gathered):
        return gathered.transpose(1, 0, 2).reshape(D, PROJ)

    W_in = [canonical_w_in(gw_in0), None]
    W_out, W1, W2 = [None] * L, [None] * L, [None] * L

    ada_b_mine = lax.dynamic_slice(ada_b, (0, me * ADA_COLS), (L, ADA_COLS)).reshape(L, 1, ADA_COLS)
    mod_part, c_act = ada_fwd(c_all, ada_w, ada_b_mine, "ada_fwd")
    gmod = run_comm(Gather([mod_part]), "gather_mod")[0]
    mod = lax.dynamic_index_in_dim(gmod, me, axis=2, keepdims=False)
    mod = mod.transpose(1, 0, 2).reshape(L, NMOD, 1, D)
    early_weights, token = start_copies([w_out_b[0]], me, "gather_early0_start", True, after=gmod)
    mod = tied(mod, token)

    cw8 = jnp.pad(conv_full, ((0, 0), (0, 5), (0, 0)))
    sg_bias = jnp.repeat(spatial_b.transpose(0, 2, 1), HD, axis=2)

    saved = []
    xl = x0
    for l in range(L):
        sh_m, sc_m, g_m, sh_f, sc_f, g_f = [mod[l, k] for k in range(NMOD)]
        h1 = normmod_fwd(xl, norm_mix_g[l:l + 1], sc_m, sh_m, f"norm_mix_fwd{l}")
        if l > 0:
            W_in[l] = canonical_w_in(finish_copies(w_in_handle, xl, f"gather_w_in{l}_wait")[0])
        qkv = mm_layer("proj_qkv", l, h1, W_in[l], out_dtypes=[BF16], cols=(0, QKV))[0]
        proj = mm_layer("proj_rest", l, h1, W_in[l], out_dtypes=[F32], cols=(QKV, REST))[0]
        riders = [w2_b[l]] if l > 0 else [w2_b[l], w1_b[l]]
        a_out, a_tot, gw2, *rode = attn_fwd(qkv, f"attn_fwd{l}", comm=Gather(riders))
        gw_out, gw1 = (finish_copies(early_weights, a_out, f"gather_early{l}_wait") + rode)[:2]
        W_out[l] = gw_out.reshape(D, D)
        W1[l] = gw1
        W2[l] = gw2.reshape(DFF, D)
        if l + 1 < L:
            w_in_handle, token = start_copies([w_in_b[l + 1]], me, f"gather_w_in{l + 1}_start", True, after=a_out)
            early_weights, token = start_copies([w_out_b[l + 1], w1_b[l + 1]], me, f"gather_early{l + 1}_start", True,
                                                after=token)
            g_m = tied(g_m, token)
        c_out = conv_fwd(proj, cw8[l], conv_b[l:l + 1], f"conv_fwd{l}")
        s_out = sg_fwd(proj, gmlp_norm_g[l:l + 1], spatial_w[l], sg_bias[l], f"sg_fwd{l}")
        cat = jnp.concatenate([a_out, c_out.astype(BF16), s_out.astype(BF16)], axis=1)
        mix, x1 = mm_layer("mix", l, cat, W_out[l], out_dtypes=[F32, F32],
                           epilogue=lambda acc, xr, g: (acc, xr + g * acc), extras=[(xl, "tile"), (g_m, "col")])
        h2 = normmod_fwd(x1, norm_mlp_g[l:l + 1], sc_f, sh_f, f"norm_mlp_fwd{l}")
        ra, r = mm_layer("mlp_up", l, h2, W1[l], out_dtypes=[BF16, BF16], b_blocks=True,
                         epilogue=lambda acc: (jnp.maximum(acc, 0.0), jnp.square(jnp.maximum(acc, 0.0))))
        m2, x2 = mm_layer("mlp_down", l, r, W2[l], out_dtypes=[F32, F32],
                          epilogue=lambda acc, xr, g: (acc, xr + g * acc), extras=[(x1, "tile"), (g_f, "col")])
        saved.append(dict(x=xl, h1=h1, proj=proj, qkv=qkv, a_tot=a_tot, cat=cat, mix=mix,
                          x1=x1, h2=h2, ra=ra, r=r, m2=m2))
        xl = x2

    dx, loss_part, d_final_g = loss_head(xl, target, final_norm_g.reshape(1, D), "loss_head")

    p_in, p_out, p_w1, p_w2 = [None] * L, [None] * L, [None] * L, [None] * L
    w_in_grads = [None] * L
    vec_rows, d_norm_mix, d_norm_mlp = [None] * L, [None] * L, [None] * L
    dcw8, d_conv_b, d_gn, d_sw, d_sb = [None] * L, [None] * L, [None] * L, [None] * L, [None] * L
    late_grads = [None] * L
    for l in reversed(range(L)):
        sv = saved[l]
        sh_m, sc_m, g_m, sh_f, sc_f, g_f = [mod[l, k] for k in range(NMOD)]
        dm2, dg_f = gate_bwd(dx, sv["m2"], g_f, f"gate_mlp_bwd{l}")
        da = mm_layer("mlp_down_dgrad", l, dm2, W2[l], out_dtypes=[BF16], trans_b=True,
                      epilogue=lambda acc, rav: (acc * (2.0 * rav.astype(F32)),), extras=[(sv["ra"], "tile")])[0]
        dW2 = mm_layer("mlp_down_wgrad", l, sv["r"], dm2, out_dtypes=[BF16], trans_a=True)[0]
        dW1 = mm_layer("mlp_up_wgrad", l, sv["h2"], da, out_dtypes=[BF16], trans_a=True, out_blocks=True)[0]
        dh2 = mm_layer("mlp_up_dgrad", l, da, W1[l], out_dtypes=[F32], trans_b=True, b_blocks=True)[0]
        dx1, dsc_f, dsh_f, d_norm_mlp[l] = normmod_bwd(sv["x1"], dh2, dx, norm_mlp_g[l:l + 1], sc_f,
                                                       f"norm_mlp_bwd{l}")
        dmix, dg_m = gate_bwd(dx1, sv["mix"], g_m, f"gate_mix_bwd{l}")
        dcat = mm_layer("mix_dgrad", l, dmix, W_out[l], out_dtypes=[F32], trans_b=True)[0]
        dW_out = mm_layer("mix_wgrad", l, sv["cat"], dmix, out_dtypes=[BF16], trans_a=True)[0]
        pieces_w2, pieces_out = dW2.reshape(NDEV, DFF // NDEV, D), dW_out.reshape(NDEV, D // NDEV, D)
        ride, late = ([pieces_w2, pieces_out], dW1) if l == L - 1 else ([pieces_w2, dW1], pieces_out)
        dq, dk, dv, *arrived = attn_bwd(sv["qkv"], dcat, sv["a_tot"], f"attn_bwd{l}", comm=Exchange(ride))
        p_w2[l] = arrived[0]
        (p_out if l == L - 1 else p_w1)[l] = arrived[1]
        late_grads[l], late_token = start_copies([late], me, f"exchange_late{l}_start", False, after=dq)
        dbg, dcg, dhc, dcw8[l], d_conv_b[l] = conv_bwd(sv["proj"], dcat, cw8[l], conv_b[l:l + 1], f"conv_bwd{l}")
        dus, dvs, d_gn[l], dsw, dbias = sg_bwd(sv["proj"], dcat, gmlp_norm_g[l:l + 1], spatial_w[l], sg_bias[l],
                                               f"sg_bwd{l}")
        d_sw[l] = dsw.astype(BF16)
        d_sb[l] = dbias.reshape(T, SG_HEADS, HD).sum(axis=2).T
        dproj = jnp.concatenate([dq, dk, dv, dbg, dcg, dhc, dus, dvs], axis=1).astype(BF16)
        dW_in = mm_layer("proj_wgrad", l, sv["h1"], dproj, out_dtypes=[BF16], trans_a=True,
                         extras=[(late_token, "tie")])[0]
        pieces = dW_in.reshape(D, NDEV, PROJ // NDEV).transpose(1, 0, 2)
        w_in_grads[l], token = start_copies([pieces], me, f"exchange_w_in{l}_start", False)
        dh1 = mm_layer("proj_dgrad", l, dproj, W_in[l], out_dtypes=[F32], trans_b=True, extras=[(token, "tie")])[0]
        dx, dsc_m, dsh_m, d_norm_mix[l] = normmod_bwd(sv["x"], dh1, dx1, tied(norm_mix_g[l:l + 1], token), sc_m,
                                                      f"norm_mix_bwd{l}")
        vec_rows[l] = [dsh_m, dsc_m, dg_m, dsh_f, dsc_f, dg_f, d_norm_mix[l], d_norm_mlp[l]]

    grad_x = dx.reshape(1, S, D)

    g_w2, d_w2, nm_w2, nv_w2 = adamw_reduce(mlp_w2, p_w2, m_mlp_w2, v_mlp_w2, 256, "adamw_mlp_w2", tie=token)
    p_w1[L - 1] = finish_copies(late_grads[L - 1], d_w2, f"exchange_late{L - 1}_wait")[0]
    g_w1, d_w1, nm_w1, nv_w1 = adamw_reduce(mlp_w1, p_w1, m_mlp_w1, v_mlp_w1, 256, "adamw_mlp_w1", tie=token)

    vec_pack = jnp.concatenate([row for l in range(L) for row in vec_rows[l]]
                               + [d_final_g, loss_part, jnp.zeros((VEC_ROWS - VEC_FINAL_ROW - 2, D), F32)], axis=0)
    vec_pack, _ = lax.optimization_barrier((vec_pack, (d_w1, d_w2)))
    w256_pack = jnp.concatenate([blk for l in range(L) for blk in (
        dcw8[l], d_conv_b[l], d_gn[l], jnp.zeros((W256_ROWS_PER_LAYER - W256_GN - 1, CW), F32))], axis=0)
    vec_all, w256_all, sb_all, *sw_all = run_comm(
        Gather([vec_pack, w256_pack, jnp.concatenate(d_sb, axis=0)] + d_sw), "gather_small_grads")

    dmod_all = (vec_all[:, :VEC_FINAL_ROW].reshape(NDEV, L, VEC_ROWS_PER_LAYER, D)[:, :, :NMOD]
                .reshape(NDEV, L, NMOD * D))
    dmod_cols = lax.dynamic_slice(dmod_all, (0, 0, me * ADA_COLS), (NDEV, L, ADA_COLS)).transpose(1, 0, 2)
    g_ada_w = ada_bwd(c_act, dmod_cols, "ada_bwd")

    flat2 = lambda t: t.reshape(L * D, ADA_COLS)
    d_ada_w, nm_ada_w, nv_ada_w = [t.reshape(L, D, ADA_COLS) for t in adamw_plain(
        flat2(ada_w), flat2(g_ada_w), flat2(m_ada_w), flat2(v_ada_w), 256, "adamw_ada_w")]

    after = jnp.concatenate([t.reshape(-1)[:1] for t in (d_w1, d_w2, d_ada_w)])
    p_in = [finish_copies(w_in_grads[l], after, f"exchange_w_in{l}_wait")[0] for l in range(L)]
    p_out[0] = finish_copies(late_grads[0], after, "exchange_late0_wait")[0]
    g_w_in, d_w_in, nm_w_in, nv_w_in = adamw_reduce(w_in, p_in, m_w_in, v_w_in, 256, "adamw_w_in")
    g_w_out, d_w_out, nm_w_out, nv_w_out = adamw_reduce(w_out, p_out, m_w_out, v_w_out, 128, "adamw_w_out")

    as_row = lambda t: t.reshape(1, D)
    small_params = [(ada_b, m_ada_b, v_ada_b), (norm_mix_g, m_norm_mix_g, v_norm_mix_g),
                    (norm_mlp_g, m_norm_mlp_g, v_norm_mlp_g),
                    (as_row(final_norm_g), as_row(m_final_norm_g), as_row(v_final_norm_g)),
                    (conv_b, m_conv_b, v_conv_b), (gmlp_norm_g, m_gmlp_norm_g, v_gmlp_norm_g),
                    (spatial_w, m_spatial_w, v_spatial_w), (spatial_b, m_spatial_b, v_spatial_b)]
    updated, (loss_sum, taps_sum) = small_update(vec_all, w256_all, sb_all, sw_all, small_params, "small_update")
    loss = loss_sum[0, 0]
    u_ada_b, u_norm_mix, u_norm_mlp, u_final, u_conv_b, u_gn, u_sw, u_sb = updated
    u_final = [t.reshape(D) for t in u_final]
    g_conv_w = lax.dynamic_slice(taps_sum, (0, 0, me * conv_shard), (L, 3, conv_shard))
    flat_cw = lambda t: t.reshape(L * 3, conv_shard)
    u_conv_w = [g_conv_w] + [t.reshape(L, 3, conv_shard) for t in adamw_plain(
        flat_cw(conv_w), flat_cw(g_conv_w), flat_cw(m_conv_w), flat_cw(v_conv_w), L * 3, "adamw_conv_w")]
    small_sets = [u_ada_b, u_norm_mix, u_norm_mlp, u_conv_w, u_conv_b, u_gn, u_sw, u_sb, u_final]
    small_g, sd, snm, snv = [[u[k] for u in small_sets] for k in range(4)]

    def ordered(big, small):
        ada, win, wout, w1, w2 = big
        return [ada, small[0], small[1], small[2], win, small[3], small[4], small[5], small[6], small[7],
                wout, w1, w2, small[8]]

    grads = ordered([g_ada_w, g_w_in, g_w_out, g_w1, g_w2], small_g)
    deltas = ordered([d_ada_w, d_w_in, d_w_out, d_w1, d_w2], sd)
    new_m = ordered([nm_ada_w, nm_w_in, nm_w_out, nm_w1, nm_w2], snm)
    new_v = ordered([nv_ada_w, nv_w_in, nv_w_out, nv_w1, nv_w2], snv)
    return (loss, grad_x, *grads, *deltas, *new_m, *new_v)
```

```python
import functools
import math

import jax
import jax.numpy as jnp
from jax import lax
from jax.experimental import pallas as pl
from jax.experimental.pallas import tpu as pltpu

F32 = jnp.float32
BF16 = jnp.bfloat16
MESH = pl.DeviceIdType.MESH

S = 2048
D = 1024
L = 2
NDEV = 8
HD = 64
NH = 8
PROJ = 2816
DFF = 4096
NMOD = 6
EPS = 1e-6
T = 128
SG_HEADS = 4
LANES = 128
CW = 256
QKV = 3 * NH * HD
REST = PROJ - QKV

LR, B1, B2, AEPS, WD, STEP = 0.001, 0.9, 0.999, 1e-08, 0.01, 10
BC1 = 1.0 - B1 ** STEP
BC2 = 1.0 - B2 ** STEP

VMEM_LIMIT = 48 * 1024 * 1024

HBM_SPEC = pl.BlockSpec(memory_space=pltpu.HBM)


def _cparams(sem=None):
    return pltpu.CompilerParams(dimension_semantics=sem, vmem_limit_bytes=VMEM_LIMIT)


def _my_pos():
    return lax.axis_index("x"), lax.axis_index("y"), lax.axis_index("c")


def _lin(p):
    return 4 * p[0] + 2 * p[1] + p[2]


class Gather:
    def __init__(self, arrs):
        self.arrs = list(arrs)
        n = len(self.arrs)
        self.out_shape = [jax.ShapeDtypeStruct((NDEV,) + a.shape, a.dtype) for a in self.arrs]
        self.scratch = [pltpu.SemaphoreType.DMA((n, 7)), pltpu.SemaphoreType.DMA((n, 7)),
                        pltpu.SemaphoreType.DMA((n,))]

    def phases(self, ins, outs, sems):
        n = len(self.arrs)
        send_sems, recv_sems, local_sems = sems
        x, y, c = _my_pos()
        me, sibling = (x, y, c), (x, y, 1 - c)
        chips = [(1 - x, y), (x, 1 - y), (1 - x, 1 - y)]

        def copy(a, k, block, to, src=None):
            slot = outs[a].at[_lin(block)]
            return pltpu.make_async_remote_copy(
                src_ref=slot if src is None else src, dst_ref=slot,
                send_sem=send_sems.at[a, k], recv_sem=recv_sems.at[a, k],
                device_id=to, device_id_type=MESH)

        def mine(a):
            return pltpu.make_async_copy(ins[a], outs[a].at[_lin(me)], local_sems.at[a])

        def first(a):
            return [copy(a, 0, me, sibling, src=ins[a])] + [
                copy(a, 1 + j, me, (*chip, c), src=ins[a]) for j, chip in enumerate(chips)]

        def passed(a):
            return [copy(a, 4 + j, (*chip, c), sibling) for j, chip in enumerate(chips)]

        def start():
            for a in range(n):
                mine(a).start()
                for cp in first(a):
                    cp.start()

        def relay():
            for j, chip in enumerate(chips):
                for a in range(n):
                    copy(a, 1 + j, (*chip, c), me).wait_recv()
                    passed(a)[j].start()

        def finish():
            for a in range(n):
                copy(a, 0, sibling, me).wait_recv()
            for j, chip in enumerate(chips):
                for a in range(n):
                    copy(a, 4 + j, (*chip, 1 - c), me).wait_recv()
            for a in range(n):
                for cp in first(a) + passed(a):
                    cp.wait_send()
                mine(a).wait()

        return start, relay, finish


class Exchange:
    def __init__(self, arrs):
        self.arrs = list(arrs)
        n = len(self.arrs)
        self.out_shape = [jax.ShapeDtypeStruct(a.shape, a.dtype) for a in self.arrs]
        self.scratch = [pltpu.SemaphoreType.DMA((n, 7)), pltpu.SemaphoreType.DMA((n, 7)),
                        pltpu.SemaphoreType.DMA((n,))]

    def phases(self, ins, outs, sems):
        n = len(self.arrs)
        send_sems, recv_sems, local_sems = sems
        x, y, c = _my_pos()
        me = (x, y, c)

        def peer(mask):
            return (1 - x if mask & 4 else x, 1 - y if mask & 2 else y, 1 - c if mask & 1 else c)

        def copy(a, mask):
            return pltpu.make_async_remote_copy(
                src_ref=ins[a].at[_lin(peer(mask))], dst_ref=outs[a].at[_lin(me)],
                send_sem=send_sems.at[a, mask - 1], recv_sem=recv_sems.at[a, mask - 1],
                device_id=peer(mask), device_id_type=MESH)

        def arrival(a, mask):
            return pltpu.make_async_remote_copy(
                src_ref=ins[a].at[_lin(me)], dst_ref=outs[a].at[_lin(peer(mask))],
                send_sem=send_sems.at[a, mask - 1], recv_sem=recv_sems.at[a, mask - 1],
                device_id=peer(mask), device_id_type=MESH)

        def mine(a):
            return pltpu.make_async_copy(ins[a].at[_lin(me)], outs[a].at[_lin(me)], local_sems.at[a])

        def start():
            for a in range(n):
                mine(a).start()
            for mask in (4, 2, 6, 1, 5, 3, 7):
                for a in range(n):
                    copy(a, mask).start()

        def relay():
            pass

        def finish():
            for mask in range(1, 8):
                for a in range(n):
                    arrival(a, mask).wait_recv()
            for mask in range(1, 8):
                for a in range(n):
                    copy(a, mask).wait_send()
            for a in range(n):
                mine(a).wait()

        return start, relay, finish


def run_comm(plan, name):
    n = len(plan.arrs)

    def body(*refs):
        start, relay, finish = plan.phases(refs[:n], refs[n:2 * n], refs[2 * n:])
        start()
        relay()
        finish()

    outs = pl.pallas_call(
        body, name=name, out_shape=plan.out_shape,
        in_specs=[HBM_SPEC] * n, out_specs=[HBM_SPEC] * n, scratch_shapes=plan.scratch,
    )(*plan.arrs)
    return list(outs)


SEM_SPEC = pl.BlockSpec(memory_space=pltpu.SEMAPHORE)
DATAFLOW = pltpu.SideEffectType.DATAFLOW_SIDE_EFFECTING


def _peer_copies(src_ref, land_ref, send_sems, recv_sems, first, same_block):
    x, y, c = _my_pos()
    me = (x, y, c)
    sends, arrivals = [], []
    for mask in (4, 2, 6, 1, 5, 3, 7):
        peer = (1 - x if mask & 4 else x, 1 - y if mask & 2 else y, 1 - c if mask & 1 else c)
        sends.append(pltpu.make_async_remote_copy(
            src_ref=src_ref if same_block else src_ref.at[_lin(peer)], dst_ref=land_ref.at[_lin(me)],
            send_sem=send_sems.at[first + mask - 1], recv_sem=recv_sems.at[first + mask - 1], device_id=peer,
            device_id_type=MESH))
        arrivals.append(pltpu.make_async_remote_copy(
            src_ref=src_ref if same_block else src_ref.at[_lin(me)], dst_ref=land_ref.at[_lin(peer)],
            send_sem=send_sems.at[first + mask - 1], recv_sem=recv_sems.at[first + mask - 1], device_id=peer,
            device_id_type=MESH))
    return sends, arrivals


def start_copies(srcs, me, name, same_block, after=None):
    n = len(srcs)
    landings = []
    for src in srcs:
        own = src[None] if same_block else lax.dynamic_index_in_dim(src, me, axis=0, keepdims=True)
        landings.append(lax.dynamic_update_slice(lax.empty((NDEV,) + own.shape[1:], src.dtype), own,
                                                 (me,) + (0,) * (own.ndim - 1)))

    def body(*refs):
        send_sems, recv_sems = refs[-2 * n - 3], refs[-2 * n - 2]
        token = refs[-1]
        for k in range(n):
            sends, _ = _peer_copies(refs[2 * k], refs[2 * k + 1], send_sems, recv_sems, 7 * k, same_block)
            for cp in sends:
                cp.start()
        token[...] = jnp.zeros_like(token)

    hbm = lambda a: pltpu.HBM(a.shape, a.dtype)
    pairs = [a for pair in zip(srcs, landings) for a in pair]
    extra = [] if after is None else [after]
    sems = pltpu.SemaphoreType.DMA((7 * n,))
    send_sems, recv_sems, *thru, token = pl.pallas_call(
        body, name=name,
        out_shape=(sems, sems, *[hbm(a) for a in pairs], jax.ShapeDtypeStruct((8, LANES), F32)),
        in_specs=[HBM_SPEC] * (2 * n) + [pl.BlockSpec(memory_space=pl.ANY)] * len(extra),
        out_specs=(SEM_SPEC, SEM_SPEC, *[HBM_SPEC] * (2 * n), pl.BlockSpec(memory_space=pltpu.VMEM)),
        input_output_aliases={k: 2 + k for k in range(2 * n)},
        compiler_params=pltpu.CompilerParams(has_side_effects=DATAFLOW),
    )(*[pltpu.with_memory_space_constraint(a, pltpu.HBM) for a in pairs], *extra)
    return (send_sems, recv_sems, thru, same_block), token


def finish_copies(handle, after, name):
    send_sems, recv_sems, thru, same_block = handle
    n = len(thru) // 2

    def body(*refs):
        send_sems, recv_sems = refs[2 * n], refs[2 * n + 1]
        for k in range(n):
            sends, arrivals = _peer_copies(refs[2 * k], refs[2 * k + 1], send_sems, recv_sems, 7 * k, same_block)
            for cp in sends:
                cp.wait_send()
            for cp in arrivals:
                cp.wait_recv()

    hbm = lambda a: pltpu.HBM(a.shape, a.dtype)
    outs = pl.pallas_call(
        body, name=name, out_shape=tuple(hbm(a) for a in thru),
        in_specs=[HBM_SPEC] * (2 * n) + [SEM_SPEC, SEM_SPEC, pl.BlockSpec(memory_space=pl.ANY)],
        out_specs=tuple([HBM_SPEC] * (2 * n)), input_output_aliases={k: k for k in range(2 * n)},
        compiler_params=pltpu.CompilerParams(has_side_effects=DATAFLOW),
    )(*thru, send_sems, recv_sems, after)
    return [outs[2 * k + 1] for k in range(n)]


def tied(x, token):
    return x + token[0:1, 0:1].astype(x.dtype)


MM_TILES = {
    "proj_qkv": (S, 512), "proj_rest": (S, 256), "mix": (1024, 512), "mlp_up": (S, 512), "mlp_down": (1024, 512),
    "mlp_down_dgrad": (1024, 1024), "mlp_down_wgrad": (1024, 1024), "mlp_up_wgrad": (1024, 512),
    "mlp_up_dgrad": (1024, 512), "mix_dgrad": (1024, 512), "mix_wgrad": (512, 1024),
    "proj_wgrad": (1024, PROJ // 2), "proj_dgrad": (1024, 512),
}


def mm_layer(kind, l, a, b, **kw):
    tm, tn = MM_TILES[kind]
    return mm(a, b, tm=tm, tn=tn, name=f"{kind}{l}", **kw)


def mm(a, b, *, tm, tn, out_dtypes, epilogue=None, extras=(), name, trans_a=False, trans_b=False,
       cols=None, b_blocks=False, out_blocks=False):
    if trans_a:
        kdim, m = a.shape
    else:
        m, kdim = a.shape
    shard = b.shape[-1] if b_blocks else None
    if b_blocks:
        full = (b.shape[1], NDEV * shard)
    else:
        full = b.shape
    first, ncols = cols if cols is not None else (0, full[0] if trans_b else full[1])
    assert full[1 if trans_b else 0] == kdim and m % tm == 0 and ncols % tn == 0 and first % tn == 0
    j0 = first // tn
    if trans_a:
        a_spec = pl.BlockSpec((kdim, tm), lambda i, j: (0, i))
    else:
        a_spec = pl.BlockSpec((tm, kdim), lambda i, j: (i, 0))
    if b_blocks and trans_b:
        b_spec = pl.BlockSpec((NDEV, tn, shard), lambda i, j: (0, j0 + j, 0))
    elif b_blocks:
        assert tn == shard
        b_spec = pl.BlockSpec((None, kdim, tn), lambda i, j: (j0 + j, 0, 0))
    elif trans_b:
        b_spec = pl.BlockSpec((tn, kdim), lambda i, j: (j0 + j, 0))
    else:
        b_spec = pl.BlockSpec((kdim, tn), lambda i, j: (0, j0 + j))
    if out_blocks:
        assert tn * NDEV == ncols
        out_spec = pl.BlockSpec((None, tm, tn), lambda i, j: (j, i, 0))
        out_dims = (NDEV, m, tn)
    else:
        out_spec = pl.BlockSpec((tm, tn), lambda i, j: (i, j))
        out_dims = (m, ncols)
    ex_specs = []
    for arr, kind in extras:
        if kind == "tile":
            ex_specs.append(pl.BlockSpec((tm, tn), lambda i, j: (i, j)))
        elif kind == "col":
            ex_specs.append(pl.BlockSpec((1, tn), lambda i, j: (0, j)))
        else:
            ex_specs.append(pl.BlockSpec(arr.shape, lambda i, j: (0, 0)))
    n_ex, n_out = len(extras), len(out_dtypes)
    used = [k for k, (_, kind) in enumerate(extras) if kind != "tie"]

    def body(a_ref, b_ref, *rest):
        ex_refs, out_refs = rest[:n_ex], rest[n_ex:]
        if trans_a:
            acc = lax.dot_general(a_ref[...], b_ref[...], (((0,), (0,)), ((), ())),
                                  preferred_element_type=F32)
        elif trans_b and b_blocks:
            acc = jnp.zeros((tm, tn), F32)
            for d in range(NDEV):
                acc = acc + lax.dot_general(a_ref[:, d * shard:(d + 1) * shard], b_ref[d],
                                            (((1,), (1,)), ((), ())), preferred_element_type=F32)
        elif trans_b:
            acc = lax.dot_general(a_ref[...], b_ref[...], (((1,), (1,)), ((), ())),
                                  preferred_element_type=F32)
        else:
            acc = jnp.dot(a_ref[...], b_ref[...], preferred_element_type=F32)
        outs = (acc,) if epilogue is None else epilogue(acc, *[ex_refs[k][...] for k in used])
        for o_ref, val in zip(out_refs, outs):
            o_ref[...] = val.astype(o_ref.dtype)

    outs = pl.pallas_call(
        body, name=name, grid=(m // tm, ncols // tn),
        in_specs=[a_spec, b_spec] + ex_specs,
        out_specs=[out_spec for _ in range(n_out)],
        out_shape=[jax.ShapeDtypeStruct(out_dims, dt) for dt in out_dtypes],
        compiler_params=_cparams(("parallel", "parallel")),
    )(a, b, *[arr for arr, _ in extras])
    return list(outs)


TR = 256

ROW_SPEC = pl.BlockSpec((TR, D), lambda i: (i, 0))
VEC_SPEC = pl.BlockSpec((1, D), lambda i: (0, 0))


def normmod_fwd(x, g, sc, sh, name):
    def body(x_ref, g_ref, sc_ref, sh_ref, o_ref):
        xv = x_ref[...]
        rstd = lax.rsqrt(jnp.mean(xv * xv, axis=-1, keepdims=True) + EPS)
        n = (xv * rstd) * g_ref[...]
        o_ref[...] = (n * (1.0 + sc_ref[...]) + sh_ref[...]).astype(o_ref.dtype)

    return pl.pallas_call(
        body, name=name, grid=(S // TR,),
        in_specs=[ROW_SPEC, VEC_SPEC, VEC_SPEC, VEC_SPEC], out_specs=ROW_SPEC,
        out_shape=jax.ShapeDtypeStruct((S, D), BF16),
        compiler_params=_cparams(("parallel",)),
    )(x, g, sc, sh)


def normmod_bwd(x, dh, dres, g, sc, name):
    def body(x_ref, dh_ref, dres_ref, g_ref, sc_ref, dx_ref, dsc_ref, dsh_ref, dg_ref):
        @pl.when(pl.program_id(0) == 0)
        def _():
            dsc_ref[...] = jnp.zeros_like(dsc_ref)
            dsh_ref[...] = jnp.zeros_like(dsh_ref)
            dg_ref[...] = jnp.zeros_like(dg_ref)

        xv, dh = x_ref[...], dh_ref[...]
        gv = g_ref[...]
        rstd = lax.rsqrt(jnp.mean(xv * xv, axis=-1, keepdims=True) + EPS)
        xhat = xv * rstd
        dn = dh * (1.0 + sc_ref[...])
        dxhat = dn * gv
        dx_ref[...] = dres_ref[...] + rstd * (dxhat - xhat * jnp.mean(dxhat * xhat, axis=-1, keepdims=True))
        dsc_ref[...] += jnp.sum(dh * (xhat * gv), axis=0, keepdims=True)
        dsh_ref[...] += jnp.sum(dh, axis=0, keepdims=True)
        dg_ref[...] += jnp.sum(dn * xhat, axis=0, keepdims=True)

    vec_out = jax.ShapeDtypeStruct((1, D), F32)
    return pl.pallas_call(
        body, name=name, grid=(S // TR,),
        in_specs=[ROW_SPEC, ROW_SPEC, ROW_SPEC, VEC_SPEC, VEC_SPEC],
        out_specs=[ROW_SPEC, VEC_SPEC, VEC_SPEC, VEC_SPEC],
        out_shape=[jax.ShapeDtypeStruct((S, D), F32), vec_out, vec_out, vec_out],
        compiler_params=_cparams(("arbitrary",)),
    )(x, dh, dres, g, sc)


def gate_bwd(dx, branch, gate, name):
    def body(dx_ref, br_ref, gate_ref, o_ref, dgate_ref):
        @pl.when(pl.program_id(0) == 0)
        def _():
            dgate_ref[...] = jnp.zeros_like(dgate_ref)

        dxv = dx_ref[...]
        o_ref[...] = (dxv * gate_ref[...]).astype(o_ref.dtype)
        dgate_ref[...] += jnp.sum(dxv * br_ref[...], axis=0, keepdims=True)

    return pl.pallas_call(
        body, name=name, grid=(S // TR,),
        in_specs=[ROW_SPEC, ROW_SPEC, VEC_SPEC], out_specs=[ROW_SPEC, VEC_SPEC],
        out_shape=[jax.ShapeDtypeStruct((S, D), BF16), jax.ShapeDtypeStruct((1, D), F32)],
        compiler_params=_cparams(("arbitrary",)),
    )(dx, branch, gate)


def loss_head(x, target, g, name):
    def body(x_ref, t_ref, g_ref, dx_ref, loss_ref, dg_ref):
        @pl.when(pl.program_id(0) == 0)
        def _():
            loss_ref[...] = jnp.zeros_like(loss_ref)
            dg_ref[...] = jnp.zeros_like(dg_ref)

        xv, gv = x_ref[...], g_ref[...]
        rstd = lax.rsqrt(jnp.mean(xv * xv, axis=-1, keepdims=True) + EPS)
        xhat = xv * rstd
        err = xhat * gv - t_ref[...]
        loss_ref[...] += jnp.sum(err * err) * (0.5 / D)
        dy = err * (1.0 / D)
        dg_ref[...] += jnp.sum(dy * xhat, axis=0, keepdims=True)
        dxhat = dy * gv
        dx_ref[...] = rstd * (dxhat - xhat * jnp.mean(dxhat * xhat, axis=-1, keepdims=True))

    return pl.pallas_call(
        body, name=name, grid=(S // TR,),
        in_specs=[ROW_SPEC, ROW_SPEC, VEC_SPEC],
        out_specs=[ROW_SPEC, VEC_SPEC, VEC_SPEC],
        out_shape=[jax.ShapeDtypeStruct((S, D), F32), jax.ShapeDtypeStruct((1, D), F32),
                   jax.ShapeDtypeStruct((1, D), F32)],
        compiler_params=_cparams(("arbitrary",)),
    )(x, target, g)


TQ = 512
RS = 128
NSUB = TQ // RS
TK = 128


def _dot_hilo(a, tri_twice):
    hi = a.astype(BF16)
    lo = (a - hi.astype(F32)).astype(BF16)
    return jnp.dot(jnp.concatenate([hi, lo], axis=1), tri_twice, preferred_element_type=F32)


def _log_stay(z):
    return -(jnp.maximum(z, 0.0) + jnp.log(1.0 + jnp.exp(-jnp.abs(z))))


def _tri_and_ones(kind):
    row = jnp.bitwise_and(lax.broadcasted_iota(jnp.int32, (2 * TK, 2 * TK), 0), TK - 1)
    col = lax.broadcasted_iota(jnp.int32, (2 * TK, 2 * TK), 1)
    tri = {"after": row > col, "upto": row <= col, "before": row < col}[kind]
    return jnp.logical_or(col >= TK, tri).astype(BF16)


NPAIR = NH // 2
SCALE = HD ** -0.5


def _pair_specs(first_block):
    rows = pl.BlockSpec((TQ, LANES), lambda p, i: (i, first_block + p))
    whole = pl.BlockSpec((S, LANES), lambda p, i: (0, first_block + p))
    return rows, whole


Q_ROWS_SPEC, _ = _pair_specs(0)
_, K_ALL_SPEC = _pair_specs(NPAIR)
_, V_ALL_SPEC = _pair_specs(2 * NPAIR)
PAIR_ROWS_SPEC = pl.BlockSpec((TQ, LANES), lambda p, i: (i, p))
PAIR_ALL_SPEC = pl.BlockSpec((S, LANES), lambda p, i: (0, p))
PAIR_TOTAL_SPEC = pl.BlockSpec((2, TQ, TK), lambda p, i: (p, i, 0))


def _head_halves(x):
    first = lax.broadcasted_iota(jnp.int32, x.shape, 1) < HD
    zero = jnp.zeros_like(x)
    return jnp.where(first, x, zero), jnp.where(first, zero, x)


def _join_heads(a, b):
    return jnp.where(lax.broadcasted_iota(jnp.int32, a.shape, 1) < HD, a, b)


def _comm_hooks(comm, refs, n_in, n_out, n_scratch):
    nc = len(comm.arrs) if comm is not None else 0
    ins, cin = refs[:n_in], refs[n_in:n_in + nc]
    outs = refs[n_in + nc:n_in + nc + n_out]
    cout = refs[n_in + nc + n_out:n_in + 2 * nc + n_out]
    scratch = refs[n_in + 2 * nc + n_out:n_in + 2 * nc + n_out + n_scratch]
    sems = refs[n_in + 2 * nc + n_out + n_scratch:]
    phases = comm.phases(cin, cout, sems) if comm is not None else None
    return ins, outs, scratch, phases


def _with_comm(comm, in_specs, out_specs, out_shape, operands, scratch):
    if comm is None:
        return dict(in_specs=in_specs, out_specs=out_specs, out_shape=out_shape, scratch_shapes=scratch), operands
    nc = len(comm.arrs)
    return dict(in_specs=in_specs + [HBM_SPEC] * nc, out_specs=out_specs + [HBM_SPEC] * nc,
                out_shape=out_shape + comm.out_shape, scratch_shapes=scratch + comm.scratch), operands + comm.arrs


def attn_fwd(qkv, name, comm=None):
    n_steps = S // TQ

    def body(*refs):
        (q_ref, k_ref, v_ref), (o_ref, r_ref), (acc_ref, z_even, z_odd, w_ref), phases = _comm_hooks(
            comm, refs, 3, 2, 4)
        p = pl.program_id(0)
        i = pl.program_id(1)
        if phases is not None:
            pl.when(jnp.logical_and(p == 0, i == 0))(phases[0])
            pl.when(jnp.logical_and(p == NPAIR - 1, i == n_steps - 2))(phases[1])
        chains = [(sub, h) for sub in range(NSUB) for h in range(2)]
        q_sub = [_head_halves(q_ref[pl.ds(sub * RS, RS), :] * SCALE) for sub in range(NSUB)]
        after = _tri_and_ones("after")
        below_diagonal = (lax.broadcasted_iota(jnp.int32, (RS, TK), 1)
                          < lax.broadcasted_iota(jnp.int32, (RS, TK), 0))
        base = i * NSUB
        all_subs = list(range(NSUB))

        acc_ref[...] = jnp.zeros_like(acc_ref)
        r_ref[...] = jnp.zeros_like(r_ref)
        w_ref[...] = jnp.zeros_like(w_ref)

        def key_rows(block):
            return pl.ds(pl.multiple_of(block * TK, TK), TK)

        def store_scores(z_ref, block, subs):
            kb = k_ref[key_rows(block), :]
            for c, (sub, h) in enumerate(chains):
                if sub in subs:
                    z_ref[c] = lax.dot_general(q_sub[sub][h], kb, (((1,), (1,)), ((), ())),
                                               preferred_element_type=F32)

        def add_weighted_values(block, subs):
            vb = v_ref[key_rows(block), :]
            for sub in subs:
                acc_ref[pl.ds(sub * RS, RS), :] += _join_heads(*[
                    jnp.dot(w_ref[2 * sub + h], vb, preferred_element_type=F32) for h in range(2)])

        def step(block, z_ref, z_next_ref, subs, diagonal_sub, prev_subs, next_subs):
            if prev_subs:
                add_weighted_values(block + 1, prev_subs)
            if next_subs:
                store_scores(z_next_ref, jnp.maximum(block - 1, 0), next_subs)
            active = [(c, sub, h) for c, (sub, h) in enumerate(chains) if sub in subs]
            ls, sums = {}, {}
            for c, sub, h in active:
                ls[c] = _log_stay(z_ref[c])
                sums[c] = _dot_hilo(jnp.where(below_diagonal, ls[c], 0.0) if sub == diagonal_sub else ls[c], after)
            for c, sub, h in active:
                rows = pl.ds(sub * RS, RS)
                later = r_ref[h, rows, :]
                w = jnp.exp(z_ref[c] + ls[c] + (sums[c][:, :TK] + later))
                if sub == diagonal_sub:
                    w = jnp.where(below_diagonal, w, 0.0)
                w_ref[c] = w.astype(BF16)
                r_ref[h, rows, :] = later + sums[c][:, TK:]

        store_scores(z_even, base + NSUB - 1, [NSUB - 1])
        buffers = (z_even, z_odd)
        for j in reversed(range(NSUB)):
            subs = all_subs[j:]
            step(base + j, buffers[0], buffers[1], subs, j, all_subs[j + 1:], all_subs[j - 1:] if j else all_subs)
            buffers = buffers[::-1]
        assert buffers[0] is z_even

        @pl.loop(0, base // 2)
        def _(pair):
            block = base - 1 - 2 * pair
            step(block, z_even, z_odd, all_subs, None, all_subs, all_subs)
            step(block - 1, z_odd, z_even, all_subs, None, all_subs, all_subs)

        add_weighted_values(0, all_subs)
        o_ref[...] = acc_ref[...].astype(o_ref.dtype)
        if phases is not None:
            pl.when(jnp.logical_and(p == NPAIR - 1, i == n_steps - 1))(phases[2])

    kwargs, operands = _with_comm(
        comm, [Q_ROWS_SPEC, K_ALL_SPEC, V_ALL_SPEC], [PAIR_ROWS_SPEC, PAIR_TOTAL_SPEC],
        [jax.ShapeDtypeStruct((S, NH * HD), BF16), jax.ShapeDtypeStruct((NH, S, TK), F32)], [qkv, qkv, qkv],
        [pltpu.VMEM((TQ, LANES), F32), pltpu.VMEM((2 * NSUB, RS, TK), F32), pltpu.VMEM((2 * NSUB, RS, TK), F32),
         pltpu.VMEM((2 * NSUB, RS, TK), BF16)])
    return pl.pallas_call(
        body, name=name, grid=(NPAIR, n_steps),
        compiler_params=_cparams(("arbitrary", "arbitrary")), **kwargs,
    )(*operands)


def attn_bwd(qkv, dout, totals, name, comm=None):
    n_steps = S // TQ

    def body(*refs):
        ((q_ref, k_ref, v_ref, do_ref, r_ref), (dq_ref, dk_ref, dv_ref),
         (z_even, z_odd, dw_even, dw_odd, before_ref, dbefore_ref, dz_ref, w_ref), phases) = _comm_hooks(
            comm, refs, 5, 3, 8)
        p = pl.program_id(0)
        i = pl.program_id(1)
        if phases is not None:
            pl.when(jnp.logical_and(p == 0, i == 0))(phases[0])
            pl.when(jnp.logical_and(p == NPAIR - 1, i == n_steps - 2))(phases[1])

        @pl.when(i == 0)
        def _():
            dk_ref[...] = jnp.zeros_like(dk_ref)
            dv_ref[...] = jnp.zeros_like(dv_ref)

        chains = [(sub, h) for sub in range(NSUB) for h in range(2)]
        nch = len(chains)
        qb = q_ref[...]
        dob = do_ref[...].astype(BF16)
        q_sub = [_head_halves(qb[sub * RS:(sub + 1) * RS] * SCALE) for sub in range(NSUB)]
        do_sub = [_head_halves(dob[sub * RS:(sub + 1) * RS]) for sub in range(NSUB)]
        upto = _tri_and_ones("upto")
        before_tri = _tri_and_ones("before")
        below_diagonal = (lax.broadcasted_iota(jnp.int32, (RS, TK), 1)
                          < lax.broadcasted_iota(jnp.int32, (RS, TK), 0))
        contract_lanes = (((1,), (1,)), ((), ()))
        contract_rows = (((0,), (0,)), ((), ()))
        base = i * NSUB
        all_subs = list(range(NSUB))

        def key_rows(block):
            return pl.ds(pl.multiple_of(block * TK, TK), TK)

        def store_products(bufs, block, subs):
            z_ref, dw_ref = bufs
            kb = k_ref[key_rows(block), :]
            vb = v_ref[key_rows(block), :]
            for c, (sub, h) in enumerate(chains):
                if sub in subs:
                    z_ref[c] = lax.dot_general(q_sub[sub][h], kb, contract_lanes, preferred_element_type=F32)
                    dw_ref[c] = lax.dot_general(do_sub[sub][h], vb, contract_lanes, preferred_element_type=F32)

        def add_gradients(block, subs):
            kb = k_ref[key_rows(block), :]
            for sub in subs:
                rows = pl.ds(sub * RS, RS)
                dq_ref[rows, :] += _join_heads(*[jnp.dot(dz_ref[h, rows, :], kb, preferred_element_type=F32)
                                                 for h in range(2)])
            dk_ref[key_rows(block), :] += _join_heads(*[
                lax.dot_general(dz_ref[h], qb, contract_rows, preferred_element_type=F32) for h in range(2)])
            dv_ref[key_rows(block), :] += _join_heads(*[
                lax.dot_general(w_ref[h], dob, contract_rows, preferred_element_type=F32) for h in range(2)])

        for ref in (dq_ref, before_ref, dbefore_ref, dz_ref, w_ref):
            ref[...] = jnp.zeros_like(ref)
        even, odd = (z_even, dw_even), (z_odd, dw_odd)
        store_products(even, 0, all_subs)

        def step(block, bufs, next_bufs, subs, diagonal_sub, prev_subs, next_subs):
            z_ref, dw_ref = bufs
            add_gradients(jnp.maximum(block - 1, 0), prev_subs)
            for sub in prev_subs:
                if sub not in subs:
                    dz_ref[:, pl.ds(sub * RS, RS), :] = jnp.zeros((2, RS, TK), BF16)
                    w_ref[:, pl.ds(sub * RS, RS), :] = jnp.zeros((2, RS, TK), BF16)
            if next_subs:
                store_products(next_bufs, block + 1, next_subs)
            active = [(c, sub, h) for c, (sub, h) in enumerate(chains) if sub in subs]
            ls, sums, dl, dsums = {}, {}, {}, {}
            for c, sub, h in active:
                ls[c] = _log_stay(z_ref[c])
                sums[c] = _dot_hilo(jnp.where(below_diagonal, ls[c], 0.0) if sub == diagonal_sub else ls[c], upto)
            for c, sub, h in active:
                rows = pl.ds(sub * RS, RS)
                before = before_ref[c]
                log_after = r_ref[h, rows, :] - (sums[c][:, :TK] + before)
                w = jnp.exp((z_ref[c] + ls[c]) + log_after)
                if sub == diagonal_sub:
                    w = jnp.where(below_diagonal, w, 0.0)
                dl[c] = dw_ref[c] * w
                dsums[c] = _dot_hilo(dl[c], before_tri)
                w_ref[h, rows, :] = w.astype(BF16)
                before_ref[c] = before + sums[c][:, TK:]
            for c, sub, h in active:
                rows = pl.ds(sub * RS, RS)
                dbefore = dbefore_ref[c]
                beta = jnp.exp(z_ref[c] + ls[c])
                if sub == diagonal_sub:
                    beta = jnp.where(below_diagonal, beta, 0.0)
                dstay = dsums[c][:, :TK] + dbefore
                dz_ref[h, rows, :] = ((dl[c] * (1.0 - beta) - beta * dstay) * SCALE).astype(BF16)
                dbefore_ref[c] = dbefore + dsums[c][:, TK:]

        @pl.loop(0, base // 2)
        def _(pair):
            step(2 * pair, even, odd, all_subs, None, all_subs, all_subs)
            step(2 * pair + 1, odd, even, all_subs, None, all_subs, all_subs)

        bufs = (even, odd)
        for j in range(NSUB):
            step(base + j, bufs[0], bufs[1], all_subs[j:], j, all_subs[j - 1:] if j else all_subs, all_subs[j + 1:])
            bufs = bufs[::-1]

        add_gradients(base + NSUB - 1, all_subs[NSUB - 1:])
        if phases is not None:
            pl.when(jnp.logical_and(p == NPAIR - 1, i == n_steps - 1))(phases[2])

    full = jax.ShapeDtypeStruct((S, NH * HD), F32)
    kwargs, operands = _with_comm(
        comm, [Q_ROWS_SPEC, K_ALL_SPEC, V_ALL_SPEC, PAIR_ROWS_SPEC, PAIR_TOTAL_SPEC],
        [PAIR_ROWS_SPEC, PAIR_ALL_SPEC, PAIR_ALL_SPEC], [full, full, full], [qkv, qkv, qkv, dout, totals],
        [pltpu.VMEM((2 * NSUB, RS, TK), F32)] * 6 + [pltpu.VMEM((2, TQ, TK), BF16)] * 2)
    return pl.pallas_call(
        body, name=name, grid=(NPAIR, n_steps),
        compiler_params=_cparams(("arbitrary", "arbitrary")), **kwargs,
    )(*operands)


def _proj_cols(first_col):
    base = first_col // LANES
    return pl.BlockSpec((S, LANES), lambda j: (0, base + j))


CONV_OUT_SPEC = pl.BlockSpec((S, LANES), lambda j: (0, j))
CONV_DOUT_SPEC = pl.BlockSpec((S, LANES), lambda j: (0, (NH * HD) // LANES + j))
CONV_W_SPEC = pl.BlockSpec((8, LANES), lambda j: (0, j))
CONV_B_SPEC = pl.BlockSpec((1, LANES), lambda j: (0, j))


def _shift_down(u, n):
    rows = lax.broadcasted_iota(jnp.int32, u.shape, 0)
    return jnp.where(rows >= n, pltpu.roll(u, n, 0), 0.0)


def _shift_up(u, n):
    rows = lax.broadcasted_iota(jnp.int32, u.shape, 0)
    return jnp.where(rows < S - n, pltpu.roll(u, S - n, 0), 0.0)


def conv_fwd(proj, cw8, cb, name):
    def body(bg_ref, cg_ref, hc_ref, w_ref, b_ref, o_ref):
        u = cg_ref[...] * hc_ref[...]
        w = w_ref[...]
        y = w[0:1, :] * _shift_down(u, 2) + w[1:2, :] * _shift_down(u, 1) + w[2:3, :] * u + b_ref[...]
        o_ref[...] = bg_ref[...] * y

    return pl.pallas_call(
        body, name=name, grid=(CW // LANES,),
        in_specs=[_proj_cols(0), _proj_cols(CW), _proj_cols(2 * CW), CONV_W_SPEC, CONV_B_SPEC],
        out_specs=CONV_OUT_SPEC, out_shape=jax.ShapeDtypeStruct((S, CW), F32),
        compiler_params=_cparams(("parallel",)),
    )(proj, proj, proj, cw8, cb)


def conv_bwd(proj, dout, cw8, cb, name):
    def body(bg_ref, cg_ref, hc_ref, do_ref, w_ref, b_ref, dbg_ref, dcg_ref, dhc_ref, dw_ref, db_ref):
        cg, hc, do = cg_ref[...], hc_ref[...], do_ref[...]
        w = w_ref[...]
        u = cg * hc
        u1, u2 = _shift_down(u, 1), _shift_down(u, 2)
        y = w[0:1, :] * u2 + w[1:2, :] * u1 + w[2:3, :] * u + b_ref[...]
        dbg_ref[...] = do * y
        dy = do * bg_ref[...]
        db_ref[...] = jnp.sum(dy, axis=0, keepdims=True)
        dw_ref[...] = jnp.concatenate(
            [jnp.sum(dy * u2, axis=0, keepdims=True), jnp.sum(dy * u1, axis=0, keepdims=True),
             jnp.sum(dy * u, axis=0, keepdims=True), jnp.zeros((5, LANES), F32)], axis=0)
        du = w[2:3, :] * dy + w[1:2, :] * _shift_up(dy, 1) + w[0:1, :] * _shift_up(dy, 2)
        dcg_ref[...] = du * hc
        dhc_ref[...] = du * cg

    full = jax.ShapeDtypeStruct((S, CW), F32)
    return pl.pallas_call(
        body, name=name, grid=(CW // LANES,),
        in_specs=[_proj_cols(0), _proj_cols(CW), _proj_cols(2 * CW), CONV_DOUT_SPEC, CONV_W_SPEC, CONV_B_SPEC],
        out_specs=[CONV_OUT_SPEC, CONV_OUT_SPEC, CONV_OUT_SPEC, CONV_W_SPEC, CONV_B_SPEC],
        out_shape=[full, full, full, jax.ShapeDtypeStruct((8, CW), F32), jax.ShapeDtypeStruct((1, CW), F32)],
        compiler_params=_cparams(("parallel",)),
    )(proj, proj, proj, dout, cw8, cb)


GELU_K = math.sqrt(2.0 / math.pi)
GELU_C = 0.044715


def _gelu(x):
    return 0.5 * x * (1.0 + jnp.tanh(GELU_K * (x + GELU_C * (x * x * x))))


def _gelu_grad(x):
    t = jnp.tanh(GELU_K * (x + GELU_C * (x * x * x)))
    return 0.5 * (1.0 + t) + 0.5 * x * (1.0 - t * t) * (GELU_K * (1.0 + 3.0 * GELU_C * (x * x)))


def _sg_masks():
    row = lax.broadcasted_iota(jnp.int32, (T, T), 0)
    col = lax.broadcasted_iota(jnp.int32, (T, T), 1)
    causal = jnp.right_shift(row, 6) >= jnp.right_shift(col, 6)
    head_of_col = jnp.right_shift(lax.broadcasted_iota(jnp.int32, (T, CW), 1), 6)
    return causal, head_of_col


def _sg_mixed(vnb, sw_ref, bias, causal, head_of_col):
    mixed = bias
    for h in range(SG_HEADS):
        wh = jnp.where(causal, sw_ref[h], 0.0).astype(BF16)
        mh = jnp.dot(wh, vnb, preferred_element_type=F32)
        mixed = mixed + jnp.where(head_of_col == h, mh, 0.0)
    return mixed


SG_U_SPEC = pl.BlockSpec((T, CW), lambda n: (n, 3))
SG_V_SPEC = pl.BlockSpec((T, CW), lambda n: (n, 4))
SG_ROW_SPEC = pl.BlockSpec((T, CW), lambda n: (n, 0))
SG_DOUT_SPEC = pl.BlockSpec((T, CW), lambda n: (n, 3))
SG_G_SPEC = pl.BlockSpec((1, CW), lambda n: (0, 0))
SG_W_SPEC = pl.BlockSpec((SG_HEADS, T, T), lambda n: (0, 0, 0))
SG_BIAS_SPEC = pl.BlockSpec((T, CW), lambda n: (0, 0))


def sg_fwd(proj, gn, sw, bias, name):
    def body(u_ref, v_ref, g_ref, sw_ref, bias_ref, o_ref):
        causal, head_of_col = _sg_masks()
        gv = _gelu(v_ref[...])
        rstd = lax.rsqrt(jnp.mean(gv * gv, axis=-1, keepdims=True) + EPS)
        vnb = ((gv * rstd) * g_ref[...]).astype(BF16)
        mixed = _sg_mixed(vnb, sw_ref, bias_ref[...], causal, head_of_col)
        o_ref[...] = _gelu(u_ref[...]) * mixed

    return pl.pallas_call(
        body, name=name, grid=(S // T,),
        in_specs=[SG_U_SPEC, SG_V_SPEC, SG_G_SPEC, SG_W_SPEC, SG_BIAS_SPEC],
        out_specs=SG_ROW_SPEC, out_shape=jax.ShapeDtypeStruct((S, CW), F32),
        compiler_params=_cparams(("parallel",)),
    )(proj, proj, gn, sw, bias)


def sg_bwd(proj, dout, gn, sw, bias, name):
    def body(u_ref, v_ref, do_ref, g_ref, sw_ref, bias_ref, du_ref, dv_ref, dg_ref, dsw_ref, dbias_ref):
        @pl.when(pl.program_id(0) == 0)
        def _():
            dg_ref[...] = jnp.zeros_like(dg_ref)
            dsw_ref[...] = jnp.zeros_like(dsw_ref)
            dbias_ref[...] = jnp.zeros_like(dbias_ref)

        causal, head_of_col = _sg_masks()
        uv, vv, do, gnv = u_ref[...], v_ref[...], do_ref[...], g_ref[...]
        gv = _gelu(vv)
        rstd = lax.rsqrt(jnp.mean(gv * gv, axis=-1, keepdims=True) + EPS)
        xhat = gv * rstd
        vnb = (xhat * gnv).astype(BF16)
        mixed = _sg_mixed(vnb, sw_ref, bias_ref[...], causal, head_of_col)
        du_ref[...] = (do * mixed) * _gelu_grad(uv)
        dmix = do * _gelu(uv)
        dbias_ref[...] += dmix
        dmixb = dmix.astype(BF16)
        dvn = jnp.zeros((T, CW), F32)
        for h in range(SG_HEADS):
            wh = jnp.where(causal, sw_ref[h], 0.0).astype(BF16)
            dvh = lax.dot_general(wh, dmixb, (((0,), (0,)), ((), ())), preferred_element_type=F32)
            dvn = dvn + jnp.where(head_of_col == h, dvh, 0.0)
            dmh = jnp.where(head_of_col == h, dmixb, jnp.zeros_like(dmixb))
            dwh = lax.dot_general(dmh, vnb, (((1,), (1,)), ((), ())), preferred_element_type=F32)
            dsw_ref[h] += jnp.where(causal, dwh, 0.0)
        dg_ref[...] += jnp.sum(dvn * xhat, axis=0, keepdims=True)
        dxhat = dvn * gnv
        dgv = rstd * (dxhat - xhat * jnp.mean(dxhat * xhat, axis=-1, keepdims=True))
        dv_ref[...] = dgv * _gelu_grad(vv)

    full = jax.ShapeDtypeStruct((S, CW), F32)
    return pl.pallas_call(
        body, name=name, grid=(S // T,),
        in_specs=[SG_U_SPEC, SG_V_SPEC, SG_DOUT_SPEC, SG_G_SPEC, SG_W_SPEC, SG_BIAS_SPEC],
        out_specs=[SG_ROW_SPEC, SG_ROW_SPEC, SG_G_SPEC, SG_W_SPEC, SG_BIAS_SPEC],
        out_shape=[full, full, jax.ShapeDtypeStruct((1, CW), F32),
                   jax.ShapeDtypeStruct((SG_HEADS, T, T), F32), jax.ShapeDtypeStruct((T, CW), F32)],
        compiler_params=_cparams(("arbitrary",)),
    )(proj, proj, dout, gn, sw, bias)


ADA_COLS = NMOD * D // NDEV


def ada_fwd(c_all, ada_w, ada_b_mine, name):
    def body(c_ref, w_ref, b_ref, o_ref, ca_ref):
        cv = c_ref[...]
        ca = cv * (1.0 / (1.0 + jnp.exp(-cv)))
        ca_ref[...] = ca
        cab = ca.astype(BF16)
        for l in range(L):
            o_ref[l] = jnp.dot(cab, w_ref[l].astype(BF16), preferred_element_type=F32) + b_ref[l]

    return pl.pallas_call(
        body, name=name,
        out_shape=[jax.ShapeDtypeStruct((L, NDEV, ADA_COLS), F32), jax.ShapeDtypeStruct((NDEV, D), F32)],
        compiler_params=_cparams(),
    )(c_all, ada_w, ada_b_mine)


def ada_bwd(ca, dmod_cols, name):
    def body(ca_ref, dm_ref, o_ref):
        cab = ca_ref[...].astype(BF16)
        for l in range(L):
            o_ref[l] = lax.dot_general(cab, dm_ref[l].astype(BF16), (((0,), (0,)), ((), ())),
                                       preferred_element_type=F32)

    return pl.pallas_call(
        body, name=name, out_shape=jax.ShapeDtypeStruct((L, D, ADA_COLS), F32),
        compiler_params=_cparams(),
    )(ca, dmod_cols)


def _adamw(w, g, m, v):
    m = B1 * m + (1.0 - B1) * g
    v = B2 * v + (1.0 - B2) * (g * g)
    m_hat = m / BC1
    v_hat = v / BC2
    delta = -LR * (m_hat / (jnp.sqrt(v_hat) + AEPS) + WD * w)
    return delta, m, v


VEC_ROWS_PER_LAYER = 8
VEC_FINAL_ROW = L * VEC_ROWS_PER_LAYER
VEC_ROWS = VEC_FINAL_ROW + 8
W256_TAPS, W256_CONV_B, W256_GN = 0, 8, 9
W256_ROWS_PER_LAYER = 16


def small_update(vec_all, w256_all, sb_all, sw_all, params, name):
    n_par = len(params)

    def body(*refs):
        vec_ref, w256_ref, sb_ref = refs[:3]
        sw_refs = refs[3:3 + L]
        par_refs = [refs[3 + L + 3 * k:3 + L + 3 * k + 3] for k in range(n_par)]
        out = refs[3 + L + 3 * n_par:]
        out_par = [out[4 * k:4 * k + 4] for k in range(n_par)]
        loss_ref, taps_ref = out[4 * n_par:]

        def total(ref, idx):
            acc = ref[(0,) + idx].astype(F32)
            for d in range(1, NDEV):
                acc = acc + ref[(d,) + idx].astype(F32)
            return acc

        def update(k, region, g):
            w_ref, m_ref, v_ref = par_refs[k]
            g_ref, d_ref, nm_ref, nv_ref = out_par[k]
            delta, nm, nv = _adamw(w_ref[region], g, m_ref[region], v_ref[region])
            g_ref[region] = g
            d_ref[region] = delta
            nm_ref[region] = nm
            nv_ref[region] = nv

        for l in range(L):
            base = l * VEC_ROWS_PER_LAYER
            for k in range(NMOD):
                update(0, (slice(l, l + 1), slice(k * D, (k + 1) * D)), total(vec_ref, (slice(base + k, base + k + 1),)))
            update(1, (slice(l, l + 1),), total(vec_ref, (slice(base + 6, base + 7),)))
            update(2, (slice(l, l + 1),), total(vec_ref, (slice(base + 7, base + 8),)))
            wbase = l * W256_ROWS_PER_LAYER
            update(4, (slice(l, l + 1),), total(w256_ref, (slice(wbase + W256_CONV_B, wbase + W256_CONV_B + 1),)))
            update(5, (slice(l, l + 1),), total(w256_ref, (slice(wbase + W256_GN, wbase + W256_GN + 1),)))
            update(6, (l,), total(sw_refs[l], ()))
            update(7, (l,), total(sb_ref, (slice(l * SG_HEADS, (l + 1) * SG_HEADS),)))
            taps_ref[l] = total(w256_ref, (slice(wbase + W256_TAPS, wbase + W256_TAPS + 8),))
        update(3, (slice(0, 1),), total(vec_ref, (slice(VEC_FINAL_ROW, VEC_FINAL_ROW + 1),)))
        loss_ref[...] = total(vec_ref, (slice(VEC_FINAL_ROW + 1, VEC_FINAL_ROW + 2), slice(0, LANES)))

    out_shape = []
    for w, _, _ in params:
        out_shape += [jax.ShapeDtypeStruct(w.shape, F32)] * 4
    out_shape += [jax.ShapeDtypeStruct((1, LANES), F32), jax.ShapeDtypeStruct((L, 8, CW), F32)]
    outs = pl.pallas_call(body, name=name, out_shape=out_shape, compiler_params=_cparams())(
        vec_all, w256_all, sb_all, *sw_all, *[a for p in params for a in p])
    return [outs[4 * k:4 * k + 4] for k in range(n_par)], outs[4 * n_par:]


def adamw_plain(w, g, m, v, tr, name):
    rows, cols = w.shape
    spec = pl.BlockSpec((tr, cols), lambda i: (i, 0))

    def body(w_ref, g_ref, m_ref, v_ref, d_ref, nm_ref, nv_ref):
        delta, nm, nv = _adamw(w_ref[...], g_ref[...], m_ref[...], v_ref[...])
        d_ref[...] = delta
        nm_ref[...] = nm
        nv_ref[...] = nv

    shp = jax.ShapeDtypeStruct((rows, cols), F32)
    return pl.pallas_call(
        body, name=name, grid=(rows // tr,), in_specs=[spec] * 4, out_specs=[spec] * 3,
        out_shape=[shp, shp, shp], compiler_params=_cparams(("parallel",)),
    )(w, g, m, v)


def adamw_reduce(w, parts, m, v, tr, name, tie=None):
    _, rows, cols = w.shape
    spec = pl.BlockSpec((None, tr, cols), lambda l, i: (l, i, 0))
    pspecs = [pl.BlockSpec((NDEV, tr, cols), lambda l, i, k=k: (0, jnp.where(l == k, i, 0), 0)) for k in range(L)]

    ties = [] if tie is None else [tie]

    def body(w_ref, p0_ref, p1_ref, m_ref, v_ref, *rest):
        g_ref, d_ref, nm_ref, nv_ref = rest[len(ties):]
        first_layer = pl.program_id(0) == 0
        g = jnp.zeros((tr, cols), F32)
        for d in range(NDEV):
            g = g + jnp.where(first_layer, p0_ref[d], p1_ref[d]).astype(F32)
        delta, nm, nv = _adamw(w_ref[...], g, m_ref[...], v_ref[...])
        g_ref[...] = g
        d_ref[...] = delta
        nm_ref[...] = nm
        nv_ref[...] = nv

    shp = jax.ShapeDtypeStruct(w.shape, F32)
    return pl.pallas_call(
        body, name=name, grid=(L, rows // tr),
        in_specs=[spec] + pspecs + [spec, spec] + [pl.BlockSpec(t.shape, lambda l, i: (0, 0)) for t in ties],
        out_specs=[spec] * 4, out_shape=[shp] * 4, compiler_params=_cparams(("parallel", "parallel")),
    )(w, *parts, m, v, *ties)


def _pad_rows(flat, rows):
    return jnp.pad(flat, (0, rows * LANES - flat.shape[0])).reshape(rows, LANES)


def kernel(x, c, ada_w, ada_b, norm_mix_g, norm_mlp_g, w_in, conv_w, conv_b, gmlp_norm_g, spatial_w, spatial_b, w_out, mlp_w1, mlp_w2, final_norm_g, loss_target, m_ada_w, m_ada_b, m_norm_mix_g, m_norm_mlp_g, m_w_in, m_conv_w, m_conv_b, m_gmlp_norm_g, m_spatial_w, m_spatial_b, m_w_out, m_mlp_w1, m_mlp_w2, m_final_norm_g, v_ada_w, v_ada_b, v_norm_mix_g, v_norm_mlp_g, v_w_in, v_conv_w, v_conv_b, v_gmlp_norm_g, v_spatial_w, v_spatial_b, v_w_out, v_mlp_w1, v_mlp_w2, v_final_norm_g):
    me = _lin(_my_pos())
    x0 = x[0]
    target = loss_target[0]
    conv_shard = conv_w.shape[-1]

    w_in_b, w_out_b, w1_b, w2_b = [w.astype(BF16) for w in (w_in, w_out, mlp_w1, mlp_w2)]
    pack0 = _pad_rows(jnp.concatenate([c.reshape(-1), conv_w.reshape(-1)]), 16)
    g0 = run_comm(Gather([pack0]), "gather_c_conv")[0]
    w_in_handle, token = start_copies([w_in_b[0]], me, "gather_w_in0_start", True, after=g0)
    g0 = tied(g0.reshape(NDEV, 16 * LANES), token)
    c_all = g0[:, :D]
    conv_full = (g0[:, D:D + L * 3 * conv_shard].reshape(NDEV, L, 3, conv_shard)
                 .transpose(1, 2, 0, 3).reshape(L, 3, CW))

    def canonical_w_in(gathered):
        return gathered.transpose(1, 0, 2).reshape(D, PROJ)

    W_in, W_out, W1, W2 = [None] * L, [None] * L, [None] * L, [None] * L

    ada_b_mine = lax.dynamic_slice(ada_b, (0, me * ADA_COLS), (L, ADA_COLS)).reshape(L, 1, ADA_COLS)
    mod_part, c_act = ada_fwd(c_all, ada_w, ada_b_mine, "ada_fwd")
    gmod = run_comm(Gather([mod_part]), "gather_mod")[0]
    mod = lax.dynamic_index_in_dim(gmod, me, axis=2, keepdims=False)
    mod = mod.transpose(1, 0, 2).reshape(L, NMOD, 1, D)
    early_weights, token = start_copies([w_out_b[0]], me, "gather_early0_start", True, after=gmod)
    mod = tied(mod, token)

    cw8 = jnp.pad(conv_full, ((0, 0), (0, 5), (0, 0)))
    sg_bias = jnp.repeat(spatial_b.transpose(0, 2, 1), HD, axis=2)

    saved = []
    xl = x0
    for l in range(L):
        sh_m, sc_m, g_m, sh_f, sc_f, g_f = [mod[l, k] for k in range(NMOD)]
        h1 = normmod_fwd(xl, norm_mix_g[l:l + 1], sc_m, sh_m, f"norm_mix_fwd{l}")
        W_in[l] = canonical_w_in(finish_copies(w_in_handle, h1, f"gather_w_in{l}_wait")[0])
        qkv = mm_layer("proj_qkv", l, h1, W_in[l], out_dtypes=[BF16], cols=(0, QKV))[0]
        proj = mm_layer("proj_rest", l, h1, W_in[l], out_dtypes=[F32], cols=(QKV, REST))[0]
        riders = [w2_b[l]] if l > 0 else [w2_b[l], w1_b[l]]
        a_out, a_tot, gw2, *rode = attn_fwd(qkv, f"attn_fwd{l}", comm=Gather(riders))
        gw_out, gw1 = (finish_copies(early_weights, a_out, f"gather_early{l}_wait") + rode)[:2]
        W_out[l] = gw_out.reshape(D, D)
        W1[l] = gw1
        W2[l] = gw2.reshape(DFF, D)
        if l + 1 < L:
            w_in_handle, token = start_copies([w_in_b[l + 1]], me, f"gather_w_in{l + 1}_start", True, after=a_out)
            early_weights, token = start_copies([w_out_b[l + 1], w1_b[l + 1]], me, f"gather_early{l + 1}_start", True,
                                                after=token)
            g_m = tied(g_m, token)
        c_out = conv_fwd(proj, cw8[l], conv_b[l:l + 1], f"conv_fwd{l}")
        s_out = sg_fwd(proj, gmlp_norm_g[l:l + 1], spatial_w[l], sg_bias[l], f"sg_fwd{l}")
        cat = jnp.concatenate([a_out, c_out.astype(BF16), s_out.astype(BF16)], axis=1)
        mix, x1 = mm_layer("mix", l, cat, W_out[l], out_dtypes=[F32, F32],
                           epilogue=lambda acc, xr, g: (acc, xr + g * acc), extras=[(xl, "tile"), (g_m, "col")])
        h2 = normmod_fwd(x1, norm_mlp_g[l:l + 1], sc_f, sh_f, f"norm_mlp_fwd{l}")
        ra, r = mm_layer("mlp_up", l, h2, W1[l], out_dtypes=[BF16, BF16], b_blocks=True,
                         epilogue=lambda acc: (jnp.maximum(acc, 0.0), jnp.square(jnp.maximum(acc, 0.0))))
        m2, x2 = mm_layer("mlp_down", l, r, W2[l], out_dtypes=[F32, F32],
                          epilogue=lambda acc, xr, g: (acc, xr + g * acc), extras=[(x1, "tile"), (g_f, "col")])
        saved.append(dict(x=xl, h1=h1, proj=proj, qkv=qkv, a_tot=a_tot, cat=cat, mix=mix,
                          x1=x1, h2=h2, ra=ra, r=r, m2=m2))
        xl = x2

    dx, loss_part, d_final_g = loss_head(xl, target, final_norm_g.reshape(1, D), "loss_head")

    p_in, p_out, p_w1, p_w2 = [None] * L, [None] * L, [None] * L, [None] * L
    w_in_grads = [None] * L
    vec_rows, d_norm_mix, d_norm_mlp = [None] * L, [None] * L, [None] * L
    dcw8, d_conv_b, d_gn, d_sw, d_sb = [None] * L, [None] * L, [None] * L, [None] * L, [None] * L
    late_grads = [None] * L
    for l in reversed(range(L)):
        sv = saved[l]
        sh_m, sc_m, g_m, sh_f, sc_f, g_f = [mod[l, k] for k in range(NMOD)]
        dm2, dg_f = gate_bwd(dx, sv["m2"], g_f, f"gate_mlp_bwd{l}")
        da = mm_layer("mlp_down_dgrad", l, dm2, W2[l], out_dtypes=[BF16], trans_b=True,
                      epilogue=lambda acc, rav: (acc * (2.0 * rav.astype(F32)),), extras=[(sv["ra"], "tile")])[0]
        dW2 = mm_layer("mlp_down_wgrad", l, sv["r"], dm2, out_dtypes=[BF16], trans_a=True)[0]
        dW1 = mm_layer("mlp_up_wgrad", l, sv["h2"], da, out_dtypes=[BF16], trans_a=True, out_blocks=True)[0]
        dh2 = mm_layer("mlp_up_dgrad", l, da, W1[l], out_dtypes=[F32], trans_b=True, b_blocks=True)[0]
        dx1, dsc_f, dsh_f, d_norm_mlp[l] = normmod_bwd(sv["x1"], dh2, dx, norm_mlp_g[l:l + 1], sc_f,
                                                       f"norm_mlp_bwd{l}")
        dmix, dg_m = gate_bwd(dx1, sv["mix"], g_m, f"gate_mix_bwd{l}")
        dcat = mm_layer("mix_dgrad", l, dmix, W_out[l], out_dtypes=[F32], trans_b=True)[0]
        dW_out = mm_layer("mix_wgrad", l, sv["cat"], dmix, out_dtypes=[BF16], trans_a=True)[0]
        pieces_w2, pieces_out = dW2.reshape(NDEV, DFF // NDEV, D), dW_out.reshape(NDEV, D // NDEV, D)
        ride, late = ([pieces_w2, pieces_out], dW1) if l == L - 1 else ([pieces_w2, dW1], pieces_out)
        dq, dk, dv, *arrived = attn_bwd(sv["qkv"], dcat, sv["a_tot"], f"attn_bwd{l}", comm=Exchange(ride))
        p_w2[l] = arrived[0]
        (p_out if l == L - 1 else p_w1)[l] = arrived[1]
        late_grads[l], late_token = start_copies([late], me, f"exchange_late{l}_start", False, after=dq)
        dbg, dcg, dhc, dcw8[l], d_conv_b[l] = conv_bwd(sv["proj"], dcat, cw8[l], conv_b[l:l + 1], f"conv_bwd{l}")
        dus, dvs, d_gn[l], dsw, dbias = sg_bwd(sv["proj"], dcat, gmlp_norm_g[l:l + 1], spatial_w[l], sg_bias[l],
                                               f"sg_bwd{l}")
        d_sw[l] = dsw.astype(BF16)
        d_sb[l] = dbias.reshape(T, SG_HEADS, HD).sum(axis=2).T
        dproj = jnp.concatenate([dq, dk, dv, dbg, dcg, dhc, dus, dvs], axis=1).astype(BF16)
        dW_in = mm_layer("proj_wgrad", l, sv["h1"], dproj, out_dtypes=[BF16], trans_a=True,
                         extras=[(late_token, "tie")])[0]
        pieces = dW_in.reshape(D, NDEV, PROJ // NDEV).transpose(1, 0, 2)
        w_in_grads[l], token = start_copies([pieces], me, f"exchange_w_in{l}_start", False)
        dh1 = mm_layer("proj_dgrad", l, dproj, W_in[l], out_dtypes=[F32], trans_b=True, extras=[(token, "tie")])[0]
        dx, dsc_m, dsh_m, d_norm_mix[l] = normmod_bwd(sv["x"], dh1, dx1, tied(norm_mix_g[l:l + 1], token), sc_m,
                                                      f"norm_mix_bwd{l}")
        vec_rows[l] = [dsh_m, dsc_m, dg_m, dsh_f, dsc_f, dg_f, d_norm_mix[l], d_norm_mlp[l]]

    grad_x = dx.reshape(1, S, D)

    g_w2, d_w2, nm_w2, nv_w2 = adamw_reduce(mlp_w2, p_w2, m_mlp_w2, v_mlp_w2, 256, "adamw_mlp_w2", tie=token)
    p_w1[L - 1] = finish_copies(late_grads[L - 1], d_w2, f"exchange_late{L - 1}_wait")[0]
    g_w1, d_w1, nm_w1, nv_w1 = adamw_reduce(mlp_w1, p_w1, m_mlp_w1, v_mlp_w1, 256, "adamw_mlp_w1", tie=token)

    vec_pack = jnp.concatenate([row for l in range(L) for row in vec_rows[l]]
                               + [d_final_g, loss_part, jnp.zeros((VEC_ROWS - VEC_FINAL_ROW - 2, D), F32)], axis=0)
    vec_pack, _ = lax.optimization_barrier((vec_pack, (d_w1, d_w2)))
    w256_pack = jnp.concatenate([blk for l in range(L) for blk in (
        dcw8[l], d_conv_b[l], d_gn[l], jnp.zeros((W256_ROWS_PER_LAYER - W256_GN - 1, CW), F32))], axis=0)
    vec_all, w256_all, sb_all, *sw_all = run_comm(
        Gather([vec_pack, w256_pack, jnp.concatenate(d_sb, axis=0)] + d_sw), "gather_small_grads")

    dmod_all = (vec_all[:, :VEC_FINAL_ROW].reshape(NDEV, L, VEC_ROWS_PER_LAYER, D)[:, :, :NMOD]
                .reshape(NDEV, L, NMOD * D))
    dmod_cols = lax.dynamic_slice(dmod_all, (0, 0, me * ADA_COLS), (NDEV, L, ADA_COLS)).transpose(1, 0, 2)
    g_ada_w = ada_bwd(c_act, dmod_cols, "ada_bwd")

    flat2 = lambda t: t.reshape(L * D, ADA_COLS)
    d_ada_w, nm_ada_w, nv_ada_w = [t.reshape(L, D, ADA_COLS) for t in adamw_plain(
        flat2(ada_w), flat2(g_ada_w), flat2(m_ada_w), flat2(v_ada_w), 256, "adamw_ada_w")]

    after = jnp.concatenate([t.reshape(-1)[:1] for t in (d_w1, d_w2, d_ada_w)])
    p_in = [finish_copies(w_in_grads[l], after, f"exchange_w_in{l}_wait")[0] for l in range(L)]
    p_out[0] = finish_copies(late_grads[0], after, "exchange_late0_wait")[0]
    g_w_in, d_w_in, nm_w_in, nv_w_in = adamw_reduce(w_in, p_in, m_w_in, v_w_in, 256, "adamw_w_in")
    g_w_out, d_w_out, nm_w_out, nv_w_out = adamw_reduce(w_out, p_out, m_w_out, v_w_out, 128, "adamw_w_out")

    as_row = lambda t: t.reshape(1, D)
    small_params = [(ada_b, m_ada_b, v_ada_b), (norm_mix_g, m_norm_mix_g, v_norm_mix_g),
                    (norm_mlp_g, m_norm_mlp_g, v_norm_mlp_g),
                    (as_row(final_norm_g), as_row(m_final_norm_g), as_row(v_final_norm_g)),
                    (conv_b, m_conv_b, v_conv_b), (gmlp_norm_g, m_gmlp_norm_g, v_gmlp_norm_g),
                    (spatial_w, m_spatial_w, v_spatial_w), (spatial_b, m_spatial_b, v_spatial_b)]
    updated, (loss_sum, taps_sum) = small_update(vec_all, w256_all, sb_all, sw_all, small_params, "small_update")
    loss = loss_sum[0, 0]
    u_ada_b, u_norm_mix, u_norm_mlp, u_final, u_conv_b, u_gn, u_sw, u_sb = updated
    u_final = [t.reshape(D) for t in u_final]
    g_conv_w = lax.dynamic_slice(taps_sum, (0, 0, me * conv_shard), (L, 3, conv_shard))
    flat_cw = lambda t: t.reshape(L * 3, conv_shard)
    u_conv_w = [g_conv_w] + [t.reshape(L, 3, conv_shard) for t in adamw_plain(
        flat_cw(conv_w), flat_cw(g_conv_w), flat_cw(m_conv_w), flat_cw(v_conv_w), L * 3, "adamw_conv_w")]
    small_sets = [u_ada_b, u_norm_mix, u_norm_mlp, u_conv_w, u_conv_b, u_gn, u_sw, u_sb, u_final]
    small_g, sd, snm, snv = [[u[k] for u in small_sets] for k in range(4)]

    def ordered(big, small):
        ada, win, wout, w1, w2 = big
        return [ada, small[0], small[1], small[2], win, small[3], small[4], small[5], small[6], small[7],
                wout, w1, w2, small[8]]

    grads = ordered([g_ada_w, g_w_in, g_w_out, g_w1, g_w2], small_g)
    deltas = ordered([d_ada_w, d_w_in, d_w_out, d_w1, d_w2], sd)
    new_m = ordered([nm_ada_w, nm_w_in, nm_w_out, nm_w1, nm_w2], snm)
    new_v = ordered([nv_ada_w, nv_w_in, nv_w_out, nv_w1, nv_w2], snv)
    return (loss, grad_x, *grads, *deltas, *new_m, *new_v)
```

```python
import functools
import math

import jax
import jax.numpy as jnp
from jax import lax
from jax.experimental import pallas as pl
from jax.experimental.pallas import tpu as pltpu

F32 = jnp.float32
BF16 = jnp.bfloat16
MESH = pl.DeviceIdType.MESH

S = 2048
D = 1024
L = 2
NDEV = 8
HD = 64
NH = 8
PROJ = 2816
DFF = 4096
NMOD = 6
EPS = 1e-6
T = 128
SG_HEADS = 4
LANES = 128
CW = 256
QKV = 3 * NH * HD
REST = PROJ - QKV

LR, B1, B2, AEPS, WD, STEP = 0.001, 0.9, 0.999, 1e-08, 0.01, 10
BC1 = 1.0 - B1 ** STEP
BC2 = 1.0 - B2 ** STEP

VMEM_LIMIT = 48 * 1024 * 1024

HBM_SPEC = pl.BlockSpec(memory_space=pltpu.HBM)


def _cparams(sem=None):
    return pltpu.CompilerParams(dimension_semantics=sem, vmem_limit_bytes=VMEM_LIMIT)


def _my_pos():
    return lax.axis_index("x"), lax.axis_index("y"), lax.axis_index("c")


def _lin(p):
    return 4 * p[0] + 2 * p[1] + p[2]


class Gather:
    def __init__(self, arrs):
        self.arrs = list(arrs)
        n = len(self.arrs)
        self.out_shape = [jax.ShapeDtypeStruct((NDEV,) + a.shape, a.dtype) for a in self.arrs]
        self.scratch = [pltpu.SemaphoreType.DMA((n, 7)), pltpu.SemaphoreType.DMA((n, 7)),
                        pltpu.SemaphoreType.DMA((n,))]

    def phases(self, ins, outs, sems):
        n = len(self.arrs)
        send_sems, recv_sems, local_sems = sems
        x, y, c = _my_pos()
        me, sibling = (x, y, c), (x, y, 1 - c)
        chips = [(1 - x, y), (x, 1 - y), (1 - x, 1 - y)]

        def copy(a, k, block, to, src=None):
            slot = outs[a].at[_lin(block)]
            return pltpu.make_async_remote_copy(
                src_ref=slot if src is None else src, dst_ref=slot,
                send_sem=send_sems.at[a, k], recv_sem=recv_sems.at[a, k],
                device_id=to, device_id_type=MESH)

        def mine(a):
            return pltpu.make_async_copy(ins[a], outs[a].at[_lin(me)], local_sems.at[a])

        def first(a):
            return [copy(a, 0, me, sibling, src=ins[a])] + [
                copy(a, 1 + j, me, (*chip, c), src=ins[a]) for j, chip in enumerate(chips)]

        def passed(a):
            return [copy(a, 4 + j, (*chip, c), sibling) for j, chip in enumerate(chips)]

        def start():
            for a in range(n):
                mine(a).start()
                for cp in first(a):
                    cp.start()

        def relay():
            for j, chip in enumerate(chips):
                for a in range(n):
                    copy(a, 1 + j, (*chip, c), me).wait_recv()
                    passed(a)[j].start()

        def finish():
            for a in range(n):
                copy(a, 0, sibling, me).wait_recv()
            for j, chip in enumerate(chips):
                for a in range(n):
                    copy(a, 4 + j, (*chip, 1 - c), me).wait_recv()
            for a in range(n):
                for cp in first(a) + passed(a):
                    cp.wait_send()
                mine(a).wait()

        return start, relay, finish


class Exchange:
    def __init__(self, arrs):
        self.arrs = list(arrs)
        n = len(self.arrs)
        self.out_shape = [jax.ShapeDtypeStruct(a.shape, a.dtype) for a in self.arrs]
        self.scratch = [pltpu.SemaphoreType.DMA((n, 7)), pltpu.SemaphoreType.DMA((n, 7)),
                        pltpu.SemaphoreType.DMA((n,))]

    def phases(self, ins, outs, sems):
        n = len(self.arrs)
        send_sems, recv_sems, local_sems = sems
        x, y, c = _my_pos()
        me = (x, y, c)

        def peer(mask):
            return (1 - x if mask & 4 else x, 1 - y if mask & 2 else y, 1 - c if mask & 1 else c)

        def copy(a, mask):
            return pltpu.make_async_remote_copy(
                src_ref=ins[a].at[_lin(peer(mask))], dst_ref=outs[a].at[_lin(me)],
                send_sem=send_sems.at[a, mask - 1], recv_sem=recv_sems.at[a, mask - 1],
                device_id=peer(mask), device_id_type=MESH)

        def arrival(a, mask):
            return pltpu.make_async_remote_copy(
                src_ref=ins[a].at[_lin(me)], dst_ref=outs[a].at[_lin(peer(mask))],
                send_sem=send_sems.at[a, mask - 1], recv_sem=recv_sems.at[a, mask - 1],
                device_id=peer(mask), device_id_type=MESH)

        def mine(a):
            return pltpu.make_async_copy(ins[a].at[_lin(me)], outs[a].at[_lin(me)], local_sems.at[a])

        def start():
            for a in range(n):
                mine(a).start()
            for mask in (4, 2, 6, 1, 5, 3, 7):
                for a in range(n):
                    copy(a, mask).start()

        def relay():
            pass

        def finish():
            for mask in range(1, 8):
                for a in range(n):
                    arrival(a, mask).wait_recv()
            for mask in range(1, 8):
                for a in range(n):
                    copy(a, mask).wait_send()
            for a in range(n):
                mine(a).wait()

        return start, relay, finish


def run_comm(plan, name):
    n = len(plan.arrs)

    def body(*refs):
        start, relay, finish = plan.phases(refs[:n], refs[n:2 * n], refs[2 * n:])
        start()
        relay()
        finish()

    outs = pl.pallas_call(
        body, name=name, out_shape=plan.out_shape,
        in_specs=[HBM_SPEC] * n, out_specs=[HBM_SPEC] * n, scratch_shapes=plan.scratch,
    )(*plan.arrs)
    return list(outs)


SEM_SPEC = pl.BlockSpec(memory_space=pltpu.SEMAPHORE)
DATAFLOW = pltpu.SideEffectType.DATAFLOW_SIDE_EFFECTING


def _peer_copies(src_ref, land_ref, send_sems, recv_sems, first, same_block):
    x, y, c = _my_pos()
    me = (x, y, c)
    sends, arrivals = [], []
    for mask in (4, 2, 6, 1, 5, 3, 7):
        peer = (1 - x if mask & 4 else x, 1 - y if mask & 2 else y, 1 - c if mask & 1 else c)
        sends.append(pltpu.make_async_remote_copy(
            src_ref=src_ref if same_block else src_ref.at[_lin(peer)], dst_ref=land_ref.at[_lin(me)],
            send_sem=send_sems.at[first + mask - 1], recv_sem=recv_sems.at[first + mask - 1], device_id=peer,
            device_id_type=MESH))
        arrivals.append(pltpu.make_async_remote_copy(
            src_ref=src_ref if same_block else src_ref.at[_lin(me)], dst_ref=land_ref.at[_lin(peer)],
            send_sem=send_sems.at[first + mask - 1], recv_sem=recv_sems.at[first + mask - 1], device_id=peer,
            device_id_type=MESH))
    return sends, arrivals


def start_copies(srcs, me, name, same_block, after=None):
    n = len(srcs)
    landings = []
    for src in srcs:
        own = src[None] if same_block else lax.dynamic_index_in_dim(src, me, axis=0, keepdims=True)
        landings.append(lax.dynamic_update_slice(lax.empty((NDEV,) + own.shape[1:], src.dtype), own,
                                                 (me,) + (0,) * (own.ndim - 1)))

    def body(*refs):
        send_sems, recv_sems = refs[-2 * n - 3], refs[-2 * n - 2]
        token = refs[-1]
        for k in range(n):
            sends, _ = _peer_copies(refs[2 * k], refs[2 * k + 1], send_sems, recv_sems, 7 * k, same_block)
            for cp in sends:
                cp.start()
        token[...] = jnp.zeros_like(token)

    hbm = lambda a: pltpu.HBM(a.shape, a.dtype)
    pairs = [a for pair in zip(srcs, landings) for a in pair]
    extra = [] if after is None else [after]
    sems = pltpu.SemaphoreType.DMA((7 * n,))
    send_sems, recv_sems, *thru, token = pl.pallas_call(
        body, name=name,
        out_shape=(sems, sems, *[hbm(a) for a in pairs], jax.ShapeDtypeStruct((8, LANES), F32)),
        in_specs=[HBM_SPEC] * (2 * n) + [pl.BlockSpec(memory_space=pl.ANY)] * len(extra),
        out_specs=(SEM_SPEC, SEM_SPEC, *[HBM_SPEC] * (2 * n), pl.BlockSpec(memory_space=pltpu.VMEM)),
        input_output_aliases={k: 2 + k for k in range(2 * n)},
        compiler_params=pltpu.CompilerParams(has_side_effects=DATAFLOW),
    )(*[pltpu.with_memory_space_constraint(a, pltpu.HBM) for a in pairs], *extra)
    return (send_sems, recv_sems, thru, same_block), token


def finish_copies(handle, after, name):
    send_sems, recv_sems, thru, same_block = handle
    n = len(thru) // 2

    def body(*refs):
        send_sems, recv_sems = refs[2 * n], refs[2 * n + 1]
        for k in range(n):
            sends, arrivals = _peer_copies(refs[2 * k], refs[2 * k + 1], send_sems, recv_sems, 7 * k, same_block)
            for cp in sends:
                cp.wait_send()
            for cp in arrivals:
                cp.wait_recv()

    hbm = lambda a: pltpu.HBM(a.shape, a.dtype)
    outs = pl.pallas_call(
        body, name=name, out_shape=tuple(hbm(a) for a in thru),
        in_specs=[HBM_SPEC] * (2 * n) + [SEM_SPEC, SEM_SPEC, pl.BlockSpec(memory_space=pl.ANY)],
        out_specs=tuple([HBM_SPEC] * (2 * n)), input_output_aliases={k: k for k in range(2 * n)},
        compiler_params=pltpu.CompilerParams(has_side_effects=DATAFLOW),
    )(*thru, send_sems, recv_sems, after)
    return [outs[2 * k + 1] for k in range(n)]


def tied(x, token):
    return x + token[0:1, 0:1].astype(x.dtype)


MM_TILES = {
    "proj_qkv": (S, 512), "proj_rest": (S, 256), "mix": (1024, 512), "mlp_up": (S, 512), "mlp_down": (1024, 256),
    "mlp_down_dgrad": (S, 1024), "mlp_down_wgrad": (1024, 1024), "mlp_up_wgrad": (1024, 512),
    "mlp_up_dgrad": (1024, 512), "mix_dgrad": (1024, 512), "mix_wgrad": (512, 1024),
    "proj_wgrad": (1024, PROJ // 2), "proj_dgrad": (1024, 512),
}


def mm_layer(kind, l, a, b, **kw):
    tm, tn = MM_TILES[kind]
    return mm(a, b, tm=tm, tn=tn, name=f"{kind}{l}", **kw)


def mm(a, b, *, tm, tn, out_dtypes, epilogue=None, extras=(), name, trans_a=False, trans_b=False,
       cols=None, b_blocks=False, out_blocks=False):
    if trans_a:
        kdim, m = a.shape
    else:
        m, kdim = a.shape
    shard = b.shape[-1] if b_blocks else None
    if b_blocks:
        full = (b.shape[1], NDEV * shard)
    else:
        full = b.shape
    first, ncols = cols if cols is not None else (0, full[0] if trans_b else full[1])
    assert full[1 if trans_b else 0] == kdim and m % tm == 0 and ncols % tn == 0 and first % tn == 0
    j0 = first // tn
    if trans_a:
        a_spec = pl.BlockSpec((kdim, tm), lambda i, j: (0, i))
    else:
        a_spec = pl.BlockSpec((tm, kdim), lambda i, j: (i, 0))
    if b_blocks and trans_b:
        b_spec = pl.BlockSpec((NDEV, tn, shard), lambda i, j: (0, j0 + j, 0))
    elif b_blocks:
        assert tn == shard
        b_spec = pl.BlockSpec((None, kdim, tn), lambda i, j: (j0 + j, 0, 0))
    elif trans_b:
        b_spec = pl.BlockSpec((tn, kdim), lambda i, j: (j0 + j, 0))
    else:
        b_spec = pl.BlockSpec((kdim, tn), lambda i, j: (0, j0 + j))
    if out_blocks:
        assert tn * NDEV == ncols
        out_spec = pl.BlockSpec((None, tm, tn), lambda i, j: (j, i, 0))
        out_dims = (NDEV, m, tn)
    else:
        out_spec = pl.BlockSpec((tm, tn), lambda i, j: (i, j))
        out_dims = (m, ncols)
    ex_specs = []
    for arr, kind in extras:
        if kind == "tile":
            ex_specs.append(pl.BlockSpec((tm, tn), lambda i, j: (i, j)))
        elif kind == "col":
            ex_specs.append(pl.BlockSpec((1, tn), lambda i, j: (0, j)))
        else:
            ex_specs.append(pl.BlockSpec(arr.shape, lambda i, j: (0, 0)))
    n_ex, n_out = len(extras), len(out_dtypes)
    used = [k for k, (_, kind) in enumerate(extras) if kind != "tie"]

    def body(a_ref, b_ref, *rest):
        ex_refs, out_refs = rest[:n_ex], rest[n_ex:]
        if trans_a:
            acc = lax.dot_general(a_ref[...], b_ref[...], (((0,), (0,)), ((), ())),
                                  preferred_element_type=F32)
        elif trans_b and b_blocks:
            acc = jnp.zeros((tm, tn), F32)
            for d in range(NDEV):
                acc = acc + lax.dot_general(a_ref[:, d * shard:(d + 1) * shard], b_ref[d],
                                            (((1,), (1,)), ((), ())), preferred_element_type=F32)
        elif trans_b:
            acc = lax.dot_general(a_ref[...], b_ref[...], (((1,), (1,)), ((), ())),
                                  preferred_element_type=F32)
        else:
            acc = jnp.dot(a_ref[...], b_ref[...], preferred_element_type=F32)
        outs = (acc,) if epilogue is None else epilogue(acc, *[ex_refs[k][...] for k in used])
        for o_ref, val in zip(out_refs, outs):
            o_ref[...] = val.astype(o_ref.dtype)

    outs = pl.pallas_call(
        body, name=name, grid=(m // tm, ncols // tn),
        in_specs=[a_spec, b_spec] + ex_specs,
        out_specs=[out_spec for _ in range(n_out)],
        out_shape=[jax.ShapeDtypeStruct(out_dims, dt) for dt in out_dtypes],
        compiler_params=_cparams(("parallel", "parallel")),
    )(a, b, *[arr for arr, _ in extras])
    return list(outs)


TR = 256

ROW_SPEC = pl.BlockSpec((TR, D), lambda i: (i, 0))
VEC_SPEC = pl.BlockSpec((1, D), lambda i: (0, 0))


def normmod_fwd(x, g, sc, sh, name):
    def body(x_ref, g_ref, sc_ref, sh_ref, o_ref):
        xv = x_ref[...]
        rstd = lax.rsqrt(jnp.mean(xv * xv, axis=-1, keepdims=True) + EPS)
        n = (xv * rstd) * g_ref[...]
        o_ref[...] = (n * (1.0 + sc_ref[...]) + sh_ref[...]).astype(o_ref.dtype)

    return pl.pallas_call(
        body, name=name, grid=(S // TR,),
        in_specs=[ROW_SPEC, VEC_SPEC, VEC_SPEC, VEC_SPEC], out_specs=ROW_SPEC,
        out_shape=jax.ShapeDtypeStruct((S, D), BF16),
        compiler_params=_cparams(("parallel",)),
    )(x, g, sc, sh)


def _gate_next(dxv, refs):
    br_ref, gate_ref, dbr_ref, dgate_ref = refs

    @pl.when(pl.program_id(0) == 0)
    def _():
        dgate_ref[...] = jnp.zeros_like(dgate_ref)

    dbr_ref[...] = (dxv * gate_ref[...]).astype(dbr_ref.dtype)
    dgate_ref[...] += jnp.sum(dxv * br_ref[...], axis=0, keepdims=True)


GATE_NEXT_IN = [ROW_SPEC, VEC_SPEC]
GATE_NEXT_OUT = [ROW_SPEC, VEC_SPEC]
GATE_NEXT_SHAPES = [jax.ShapeDtypeStruct((S, D), BF16), jax.ShapeDtypeStruct((1, D), F32)]


def normmod_bwd(x, dh, dres, g, sc, name, gate_next=None):
    nxt = 2 if gate_next else 0

    def body(x_ref, dh_ref, dres_ref, g_ref, sc_ref, *rest):
        nxt_in, (dx_ref, dsc_ref, dsh_ref, dg_ref), nxt_out = rest[:nxt], rest[nxt:nxt + 4], rest[nxt + 4:]

        @pl.when(pl.program_id(0) == 0)
        def _():
            dsc_ref[...] = jnp.zeros_like(dsc_ref)
            dsh_ref[...] = jnp.zeros_like(dsh_ref)
            dg_ref[...] = jnp.zeros_like(dg_ref)

        xv, dh = x_ref[...], dh_ref[...]
        gv = g_ref[...]
        rstd = lax.rsqrt(jnp.mean(xv * xv, axis=-1, keepdims=True) + EPS)
        xhat = xv * rstd
        dn = dh * (1.0 + sc_ref[...])
        dxhat = dn * gv
        dxv = dres_ref[...] + rstd * (dxhat - xhat * jnp.mean(dxhat * xhat, axis=-1, keepdims=True))
        dx_ref[...] = dxv
        dsc_ref[...] += jnp.sum(dh * (xhat * gv), axis=0, keepdims=True)
        dsh_ref[...] += jnp.sum(dh, axis=0, keepdims=True)
        dg_ref[...] += jnp.sum(dn * xhat, axis=0, keepdims=True)
        if gate_next:
            _gate_next(dxv, nxt_in + nxt_out)

    vec_out = jax.ShapeDtypeStruct((1, D), F32)
    on = bool(gate_next)
    return pl.pallas_call(
        body, name=name, grid=(S // TR,),
        in_specs=[ROW_SPEC, ROW_SPEC, ROW_SPEC, VEC_SPEC, VEC_SPEC] + GATE_NEXT_IN * on,
        out_specs=[ROW_SPEC, VEC_SPEC, VEC_SPEC, VEC_SPEC] + GATE_NEXT_OUT * on,
        out_shape=[jax.ShapeDtypeStruct((S, D), F32), vec_out, vec_out, vec_out] + GATE_NEXT_SHAPES * on,
        compiler_params=_cparams(("arbitrary",)),
    )(x, dh, dres, g, sc, *(gate_next or ()))


def loss_head(x, target, g, gate_next, name):
    def body(x_ref, t_ref, g_ref, br_ref, gate_ref, dx_ref, loss_ref, dg_ref, dbr_ref, dgate_ref):
        @pl.when(pl.program_id(0) == 0)
        def _():
            loss_ref[...] = jnp.zeros_like(loss_ref)
            dg_ref[...] = jnp.zeros_like(dg_ref)

        xv, gv = x_ref[...], g_ref[...]
        rstd = lax.rsqrt(jnp.mean(xv * xv, axis=-1, keepdims=True) + EPS)
        xhat = xv * rstd
        err = xhat * gv - t_ref[...]
        loss_ref[...] += jnp.sum(err * err) * (0.5 / D)
        dy = err * (1.0 / D)
        dg_ref[...] += jnp.sum(dy * xhat, axis=0, keepdims=True)
        dxhat = dy * gv
        dxv = rstd * (dxhat - xhat * jnp.mean(dxhat * xhat, axis=-1, keepdims=True))
        dx_ref[...] = dxv
        _gate_next(dxv, (br_ref, gate_ref, dbr_ref, dgate_ref))

    return pl.pallas_call(
        body, name=name, grid=(S // TR,),
        in_specs=[ROW_SPEC, ROW_SPEC, VEC_SPEC] + GATE_NEXT_IN,
        out_specs=[ROW_SPEC, VEC_SPEC, VEC_SPEC] + GATE_NEXT_OUT,
        out_shape=[jax.ShapeDtypeStruct((S, D), F32), jax.ShapeDtypeStruct((1, D), F32),
                   jax.ShapeDtypeStruct((1, D), F32)] + GATE_NEXT_SHAPES,
        compiler_params=_cparams(("arbitrary",)),
    )(x, target, g, *gate_next)


TQ = 512
RS = 128
NSUB = TQ // RS
TK = 128


def _dot_hilo(a, tri_twice):
    hi = a.astype(BF16)
    lo = (a - hi.astype(F32)).astype(BF16)
    return jnp.dot(jnp.concatenate([hi, lo], axis=1), tri_twice, preferred_element_type=F32)


def _log_stay(z):
    return -(jnp.maximum(z, 0.0) + jnp.log(1.0 + jnp.exp(-jnp.abs(z))))


def _tri_and_ones(kind):
    row = jnp.bitwise_and(lax.broadcasted_iota(jnp.int32, (2 * TK, 2 * TK), 0), TK - 1)
    col = lax.broadcasted_iota(jnp.int32, (2 * TK, 2 * TK), 1)
    tri = {"after": row > col, "upto": row <= col, "before": row < col}[kind]
    return jnp.logical_or(col >= TK, tri).astype(BF16)


NPAIR = NH // 2
SCALE = HD ** -0.5


def _pair_specs(first_block):
    rows = pl.BlockSpec((TQ, LANES), lambda p, i: (i, first_block + p))
    whole = pl.BlockSpec((S, LANES), lambda p, i: (0, first_block + p))
    return rows, whole


Q_ROWS_SPEC, _ = _pair_specs(0)
_, K_ALL_SPEC = _pair_specs(NPAIR)
_, V_ALL_SPEC = _pair_specs(2 * NPAIR)
PAIR_ROWS_SPEC = pl.BlockSpec((TQ, LANES), lambda p, i: (i, p))
PAIR_ALL_SPEC = pl.BlockSpec((S, LANES), lambda p, i: (0, p))
PAIR_TOTAL_SPEC = pl.BlockSpec((2, TQ, TK), lambda p, i: (p, i, 0))


def _head_halves(x):
    first = lax.broadcasted_iota(jnp.int32, x.shape, 1) < HD
    zero = jnp.zeros_like(x)
    return jnp.where(first, x, zero), jnp.where(first, zero, x)


def _join_heads(a, b):
    return jnp.where(lax.broadcasted_iota(jnp.int32, a.shape, 1) < HD, a, b)


def _comm_hooks(comm, refs, n_in, n_out, n_scratch):
    nc = len(comm.arrs) if comm is not None else 0
    ins, cin = refs[:n_in], refs[n_in:n_in + nc]
    outs = refs[n_in + nc:n_in + nc + n_out]
    cout = refs[n_in + nc + n_out:n_in + 2 * nc + n_out]
    scratch = refs[n_in + 2 * nc + n_out:n_in + 2 * nc + n_out + n_scratch]
    sems = refs[n_in + 2 * nc + n_out + n_scratch:]
    phases = comm.phases(cin, cout, sems) if comm is not None else None
    return ins, outs, scratch, phases


def _with_comm(comm, in_specs, out_specs, out_shape, operands, scratch):
    if comm is None:
        return dict(in_specs=in_specs, out_specs=out_specs, out_shape=out_shape, scratch_shapes=scratch), operands
    nc = len(comm.arrs)
    return dict(in_specs=in_specs + [HBM_SPEC] * nc, out_specs=out_specs + [HBM_SPEC] * nc,
                out_shape=out_shape + comm.out_shape, scratch_shapes=scratch + comm.scratch), operands + comm.arrs


def attn_fwd(qkv, name, comm=None):
    n_steps = S // TQ

    def body(*refs):
        (q_ref, k_ref, v_ref), (o_ref, r_ref), (acc_ref, z_even, z_odd, w_ref), phases = _comm_hooks(
            comm, refs, 3, 2, 4)
        p = pl.program_id(0)
        i = pl.program_id(1)
        if phases is not None:
            pl.when(jnp.logical_and(p == 0, i == 0))(phases[0])
            pl.when(jnp.logical_and(p == NPAIR - 1, i == n_steps - 2))(phases[1])
        chains = [(sub, h) for sub in range(NSUB) for h in range(2)]
        q_sub = [_head_halves(q_ref[pl.ds(sub * RS, RS), :] * SCALE) for sub in range(NSUB)]
        after = _tri_and_ones("after")
        below_diagonal = (lax.broadcasted_iota(jnp.int32, (RS, TK), 1)
                          < lax.broadcasted_iota(jnp.int32, (RS, TK), 0))
        base = i * NSUB
        all_subs = list(range(NSUB))

        acc_ref[...] = jnp.zeros_like(acc_ref)
        r_ref[...] = jnp.zeros_like(r_ref)
        w_ref[...] = jnp.zeros_like(w_ref)

        def key_rows(block):
            return pl.ds(pl.multiple_of(block * TK, TK), TK)

        def store_scores(z_ref, block, subs):
            kb = k_ref[key_rows(block), :]
            for c, (sub, h) in enumerate(chains):
                if sub in subs:
                    z_ref[c] = lax.dot_general(q_sub[sub][h], kb, (((1,), (1,)), ((), ())),
                                               preferred_element_type=F32)

        def add_weighted_values(block, subs):
            vb = v_ref[key_rows(block), :]
            for sub in subs:
                acc_ref[pl.ds(sub * RS, RS), :] += _join_heads(*[
                    jnp.dot(w_ref[2 * sub + h], vb, preferred_element_type=F32) for h in range(2)])

        def step(block, z_ref, z_next_ref, subs, diagonal_sub, prev_subs, next_subs):
            if prev_subs:
                add_weighted_values(block + 1, prev_subs)
            if next_subs:
                store_scores(z_next_ref, jnp.maximum(block - 1, 0), next_subs)
            active = [(c, sub, h) for c, (sub, h) in enumerate(chains) if sub in subs]
            ls, sums = {}, {}
            for c, sub, h in active:
                ls[c] = _log_stay(z_ref[c])
                sums[c] = _dot_hilo(jnp.where(below_diagonal, ls[c], 0.0) if sub == diagonal_sub else ls[c], after)
            for c, sub, h in active:
                rows = pl.ds(sub * RS, RS)
                later = r_ref[h, rows, :]
                w = jnp.exp(z_ref[c] + ls[c] + (sums[c][:, :TK] + later))
                if sub == diagonal_sub:
                    w = jnp.where(below_diagonal, w, 0.0)
                w_ref[c] = w.astype(BF16)
                r_ref[h, rows, :] = later + sums[c][:, TK:]

        store_scores(z_even, base + NSUB - 1, [NSUB - 1])
        buffers = (z_even, z_odd)
        for j in reversed(range(NSUB)):
            subs = all_subs[j:]
            step(base + j, buffers[0], buffers[1], subs, j, all_subs[j + 1:], all_subs[j - 1:] if j else all_subs)
            buffers = buffers[::-1]
        assert buffers[0] is z_even

        @pl.loop(0, base // 2)
        def _(pair):
            block = base - 1 - 2 * pair
            step(block, z_even, z_odd, all_subs, None, all_subs, all_subs)
            step(block - 1, z_odd, z_even, all_subs, None, all_subs, all_subs)

        add_weighted_values(0, all_subs)
        o_ref[...] = acc_ref[...].astype(o_ref.dtype)
        if phases is not None:
            pl.when(jnp.logical_and(p == NPAIR - 1, i == n_steps - 1))(phases[2])

    kwargs, operands = _with_comm(
        comm, [Q_ROWS_SPEC, K_ALL_SPEC, V_ALL_SPEC], [PAIR_ROWS_SPEC, PAIR_TOTAL_SPEC],
        [jax.ShapeDtypeStruct((S, NH * HD), BF16), jax.ShapeDtypeStruct((NH, S, TK), F32)], [qkv, qkv, qkv],
        [pltpu.VMEM((TQ, LANES), F32), pltpu.VMEM((2 * NSUB, RS, TK), F32), pltpu.VMEM((2 * NSUB, RS, TK), F32),
         pltpu.VMEM((2 * NSUB, RS, TK), BF16)])
    return pl.pallas_call(
        body, name=name, grid=(NPAIR, n_steps),
        compiler_params=_cparams(("arbitrary", "arbitrary")), **kwargs,
    )(*operands)


def attn_bwd(qkv, dout, totals, name, comm=None):
    n_steps = S // TQ

    def body(*refs):
        ((q_ref, k_ref, v_ref, do_ref, r_ref), (dq_ref, dk_ref, dv_ref),
         (z_even, z_odd, dw_even, dw_odd, before_ref, dbefore_ref, dz_ref, w_ref), phases) = _comm_hooks(
            comm, refs, 5, 3, 8)
        p = pl.program_id(0)
        i = pl.program_id(1)
        if phases is not None:
            pl.when(jnp.logical_and(p == 0, i == 0))(phases[0])
            pl.when(jnp.logical_and(p == NPAIR - 1, i == n_steps - 2))(phases[1])

        @pl.when(i == 0)
        def _():
            dk_ref[...] = jnp.zeros_like(dk_ref)
            dv_ref[...] = jnp.zeros_like(dv_ref)

        chains = [(sub, h) for sub in range(NSUB) for h in range(2)]
        nch = len(chains)
        qb = q_ref[...]
        dob = do_ref[...].astype(BF16)
        q_sub = [_head_halves(qb[sub * RS:(sub + 1) * RS] * SCALE) for sub in range(NSUB)]
        do_sub = [_head_halves(dob[sub * RS:(sub + 1) * RS]) for sub in range(NSUB)]
        upto = _tri_and_ones("upto")
        before_tri = _tri_and_ones("before")
        below_diagonal = (lax.broadcasted_iota(jnp.int32, (RS, TK), 1)
                          < lax.broadcasted_iota(jnp.int32, (RS, TK), 0))
        contract_lanes = (((1,), (1,)), ((), ()))
        contract_rows = (((0,), (0,)), ((), ()))
        base = i * NSUB
        all_subs = list(range(NSUB))

        def key_rows(block):
            return pl.ds(pl.multiple_of(block * TK, TK), TK)

        def store_products(bufs, block, subs):
            z_ref, dw_ref = bufs
            kb = k_ref[key_rows(block), :]
            vb = v_ref[key_rows(block), :]
            for c, (sub, h) in enumerate(chains):
                if sub in subs:
                    z_ref[c] = lax.dot_general(q_sub[sub][h], kb, contract_lanes, preferred_element_type=F32)
                    dw_ref[c] = lax.dot_general(do_sub[sub][h], vb, contract_lanes, preferred_element_type=F32)

        def add_gradients(block, subs):
            kb = k_ref[key_rows(block), :]
            for sub in subs:
                rows = pl.ds(sub * RS, RS)
                dq_ref[rows, :] += _join_heads(*[jnp.dot(dz_ref[h, rows, :], kb, preferred_element_type=F32)
                                                 for h in range(2)])
            dk_ref[key_rows(block), :] += _join_heads(*[
                lax.dot_general(dz_ref[h], qb, contract_rows, preferred_element_type=F32) for h in range(2)])
            dv_ref[key_rows(block), :] += _join_heads(*[
                lax.dot_general(w_ref[h], dob, contract_rows, preferred_element_type=F32) for h in range(2)])

        for ref in (dq_ref, before_ref, dbefore_ref, dz_ref, w_ref):
            ref[...] = jnp.zeros_like(ref)
        even, odd = (z_even, dw_even), (z_odd, dw_odd)
        store_products(even, 0, all_subs)

        def step(block, bufs, next_bufs, subs, diagonal_sub, prev_subs, next_subs):
            z_ref, dw_ref = bufs
            add_gradients(jnp.maximum(block - 1, 0), prev_subs)
            for sub in prev_subs:
                if sub not in subs:
                    dz_ref[:, pl.ds(sub * RS, RS), :] = jnp.zeros((2, RS, TK), BF16)
                    w_ref[:, pl.ds(sub * RS, RS), :] = jnp.zeros((2, RS, TK), BF16)
            if next_subs:
                store_products(next_bufs, block + 1, next_subs)
            active = [(c, sub, h) for c, (sub, h) in enumerate(chains) if sub in subs]
            ls, sums, dl, dsums = {}, {}, {}, {}
            for c, sub, h in active:
                ls[c] = _log_stay(z_ref[c])
                sums[c] = _dot_hilo(jnp.where(below_diagonal, ls[c], 0.0) if sub == diagonal_sub else ls[c], upto)
            for c, sub, h in active:
                rows = pl.ds(sub * RS, RS)
                before = before_ref[c]
                log_after = r_ref[h, rows, :] - (sums[c][:, :TK] + before)
                w = jnp.exp((z_ref[c] + ls[c]) + log_after)
                if sub == diagonal_sub:
                    w = jnp.where(below_diagonal, w, 0.0)
                dl[c] = dw_ref[c] * w
                dsums[c] = _dot_hilo(dl[c], before_tri)
                w_ref[h, rows, :] = w.astype(BF16)
                before_ref[c] = before + sums[c][:, TK:]
            for c, sub, h in active:
                rows = pl.ds(sub * RS, RS)
                dbefore = dbefore_ref[c]
                beta = jnp.exp(z_ref[c] + ls[c])
                if sub == diagonal_sub:
                    beta = jnp.where(below_diagonal, beta, 0.0)
                dstay = dsums[c][:, :TK] + dbefore
                dz_ref[h, rows, :] = ((dl[c] * (1.0 - beta) - beta * dstay) * SCALE).astype(BF16)
                dbefore_ref[c] = dbefore + dsums[c][:, TK:]

        @pl.loop(0, base // 2)
        def _(pair):
            step(2 * pair, even, odd, all_subs, None, all_subs, all_subs)
            step(2 * pair + 1, odd, even, all_subs, None, all_subs, all_subs)

        bufs = (even, odd)
        for j in range(NSUB):
            step(base + j, bufs[0], bufs[1], all_subs[j:], j, all_subs[j - 1:] if j else all_subs, all_subs[j + 1:])
            bufs = bufs[::-1]

        add_gradients(base + NSUB - 1, all_subs[NSUB - 1:])
        if phases is not None:
            pl.when(jnp.logical_and(p == NPAIR - 1, i == n_steps - 1))(phases[2])

    full = jax.ShapeDtypeStruct((S, NH * HD), F32)
    kwargs, operands = _with_comm(
        comm, [Q_ROWS_SPEC, K_ALL_SPEC, V_ALL_SPEC, PAIR_ROWS_SPEC, PAIR_TOTAL_SPEC],
        [PAIR_ROWS_SPEC, PAIR_ALL_SPEC, PAIR_ALL_SPEC], [full, full, full], [qkv, qkv, qkv, dout, totals],
        [pltpu.VMEM((2 * NSUB, RS, TK), F32)] * 6 + [pltpu.VMEM((2, TQ, TK), BF16)] * 2)
    return pl.pallas_call(
        body, name=name, grid=(NPAIR, n_steps),
        compiler_params=_cparams(("arbitrary", "arbitrary")), **kwargs,
    )(*operands)


def _proj_cols(first_col):
    base = first_col // LANES
    return pl.BlockSpec((S, LANES), lambda j: (0, base + j))


CONV_OUT_SPEC = pl.BlockSpec((S, LANES), lambda j: (0, j))
CONV_DOUT_SPEC = pl.BlockSpec((S, LANES), lambda j: (0, (NH * HD) // LANES + j))
CONV_W_SPEC = pl.BlockSpec((8, LANES), lambda j: (0, j))
CONV_B_SPEC = pl.BlockSpec((1, LANES), lambda j: (0, j))


def _shift_down(u, n):
    rows = lax.broadcasted_iota(jnp.int32, u.shape, 0)
    return jnp.where(rows >= n, pltpu.roll(u, n, 0), 0.0)


def _shift_up(u, n):
    rows = lax.broadcasted_iota(jnp.int32, u.shape, 0)
    return jnp.where(rows < S - n, pltpu.roll(u, S - n, 0), 0.0)


def conv_fwd(proj, cw8, cb, name):
    def body(bg_ref, cg_ref, hc_ref, w_ref, b_ref, o_ref):
        u = cg_ref[...] * hc_ref[...]
        w = w_ref[...]
        y = w[0:1, :] * _shift_down(u, 2) + w[1:2, :] * _shift_down(u, 1) + w[2:3, :] * u + b_ref[...]
        o_ref[...] = bg_ref[...] * y

    return pl.pallas_call(
        body, name=name, grid=(CW // LANES,),
        in_specs=[_proj_cols(0), _proj_cols(CW), _proj_cols(2 * CW), CONV_W_SPEC, CONV_B_SPEC],
        out_specs=CONV_OUT_SPEC, out_shape=jax.ShapeDtypeStruct((S, CW), F32),
        compiler_params=_cparams(("parallel",)),
    )(proj, proj, proj, cw8, cb)


def conv_bwd(proj, dout, cw8, cb, name):
    def body(bg_ref, cg_ref, hc_ref, do_ref, w_ref, b_ref, dbg_ref, dcg_ref, dhc_ref, dw_ref, db_ref):
        cg, hc, do = cg_ref[...], hc_ref[...], do_ref[...]
        w = w_ref[...]
        u = cg * hc
        u1, u2 = _shift_down(u, 1), _shift_down(u, 2)
        y = w[0:1, :] * u2 + w[1:2, :] * u1 + w[2:3, :] * u + b_ref[...]
        dbg_ref[...] = do * y
        dy = do * bg_ref[...]
        db_ref[...] = jnp.sum(dy, axis=0, keepdims=True)
        dw_ref[...] = jnp.concatenate(
            [jnp.sum(dy * u2, axis=0, keepdims=True), jnp.sum(dy * u1, axis=0, keepdims=True),
             jnp.sum(dy * u, axis=0, keepdims=True), jnp.zeros((5, LANES), F32)], axis=0)
        du = w[2:3, :] * dy + w[1:2, :] * _shift_up(dy, 1) + w[0:1, :] * _shift_up(dy, 2)
        dcg_ref[...] = du * hc
        dhc_ref[...] = du * cg

    full = jax.ShapeDtypeStruct((S, CW), F32)
    return pl.pallas_call(
        body, name=name, grid=(CW // LANES,),
        in_specs=[_proj_cols(0), _proj_cols(CW), _proj_cols(2 * CW), CONV_DOUT_SPEC, CONV_W_SPEC, CONV_B_SPEC],
        out_specs=[CONV_OUT_SPEC, CONV_OUT_SPEC, CONV_OUT_SPEC, CONV_W_SPEC, CONV_B_SPEC],
        out_shape=[full, full, full, jax.ShapeDtypeStruct((8, CW), F32), jax.ShapeDtypeStruct((1, CW), F32)],
        compiler_params=_cparams(("parallel",)),
    )(proj, proj, proj, dout, cw8, cb)


GELU_K = math.sqrt(2.0 / math.pi)
GELU_C = 0.044715


def _gelu(x):
    return 0.5 * x * (1.0 + jnp.tanh(GELU_K * (x + GELU_C * (x * x * x))))


def _gelu_grad(x):
    t = jnp.tanh(GELU_K * (x + GELU_C * (x * x * x)))
    return 0.5 * (1.0 + t) + 0.5 * x * (1.0 - t * t) * (GELU_K * (1.0 + 3.0 * GELU_C * (x * x)))


def _sg_masks():
    row = lax.broadcasted_iota(jnp.int32, (T, T), 0)
    col = lax.broadcasted_iota(jnp.int32, (T, T), 1)
    causal = jnp.right_shift(row, 6) >= jnp.right_shift(col, 6)
    head_of_col = jnp.right_shift(lax.broadcasted_iota(jnp.int32, (T, CW), 1), 6)
    return causal, head_of_col


def _sg_mixed(vnb, sw_ref, bias, causal, head_of_col):
    mixed = bias
    for h in range(SG_HEADS):
        wh = jnp.where(causal, sw_ref[h], 0.0).astype(BF16)
        mh = jnp.dot(wh, vnb, preferred_element_type=F32)
        mixed = mixed + jnp.where(head_of_col == h, mh, 0.0)
    return mixed


SG_U_SPEC = pl.BlockSpec((T, CW), lambda n: (n, 3))
SG_V_SPEC = pl.BlockSpec((T, CW), lambda n: (n, 4))
SG_ROW_SPEC = pl.BlockSpec((T, CW), lambda n: (n, 0))
SG_DOUT_SPEC = pl.BlockSpec((T, CW), lambda n: (n, 3))
SG_G_SPEC = pl.BlockSpec((1, CW), lambda n: (0, 0))
SG_W_SPEC = pl.BlockSpec((SG_HEADS, T, T), lambda n: (0, 0, 0))
SG_BIAS_SPEC = pl.BlockSpec((T, CW), lambda n: (0, 0))


def sg_fwd(proj, gn, sw, bias, name):
    def body(u_ref, v_ref, g_ref, sw_ref, bias_ref, o_ref):
        causal, head_of_col = _sg_masks()
        gv = _gelu(v_ref[...])
        rstd = lax.rsqrt(jnp.mean(gv * gv, axis=-1, keepdims=True) + EPS)
        vnb = ((gv * rstd) * g_ref[...]).astype(BF16)
        mixed = _sg_mixed(vnb, sw_ref, bias_ref[...], causal, head_of_col)
        o_ref[...] = _gelu(u_ref[...]) * mixed

    return pl.pallas_call(
        body, name=name, grid=(S // T,),
        in_specs=[SG_U_SPEC, SG_V_SPEC, SG_G_SPEC, SG_W_SPEC, SG_BIAS_SPEC],
        out_specs=SG_ROW_SPEC, out_shape=jax.ShapeDtypeStruct((S, CW), F32),
        compiler_params=_cparams(("parallel",)),
    )(proj, proj, gn, sw, bias)


def sg_bwd(proj, dout, gn, sw, bias, name):
    def body(u_ref, v_ref, do_ref, g_ref, sw_ref, bias_ref, du_ref, dv_ref, dg_ref, dsw_ref, dbias_ref):
        @pl.when(pl.program_id(0) == 0)
        def _():
            dg_ref[...] = jnp.zeros_like(dg_ref)
            dsw_ref[...] = jnp.zeros_like(dsw_ref)
            dbias_ref[...] = jnp.zeros_like(dbias_ref)

        causal, head_of_col = _sg_masks()
        uv, vv, do, gnv = u_ref[...], v_ref[...], do_ref[...], g_ref[...]
        gv = _gelu(vv)
        rstd = lax.rsqrt(jnp.mean(gv * gv, axis=-1, keepdims=True) + EPS)
        xhat = gv * rstd
        vnb = (xhat * gnv).astype(BF16)
        mixed = _sg_mixed(vnb, sw_ref, bias_ref[...], causal, head_of_col)
        du_ref[...] = (do * mixed) * _gelu_grad(uv)
        dmix = do * _gelu(uv)
        dbias_ref[...] += dmix
        dmixb = dmix.astype(BF16)
        dvn = jnp.zeros((T, CW), F32)
        for h in range(SG_HEADS):
            wh = jnp.where(causal, sw_ref[h], 0.0).astype(BF16)
            dvh = lax.dot_general(wh, dmixb, (((0,), (0,)), ((), ())), preferred_element_type=F32)
            dvn = dvn + jnp.where(head_of_col == h, dvh, 0.0)
            dmh = jnp.where(head_of_col == h, dmixb, jnp.zeros_like(dmixb))
            dwh = lax.dot_general(dmh, vnb, (((1,), (1,)), ((), ())), preferred_element_type=F32)
            dsw_ref[h] += jnp.where(causal, dwh, 0.0)
        dg_ref[...] += jnp.sum(dvn * xhat, axis=0, keepdims=True)
        dxhat = dvn * gnv
        dgv = rstd * (dxhat - xhat * jnp.mean(dxhat * xhat, axis=-1, keepdims=True))
        dv_ref[...] = dgv * _gelu_grad(vv)

    full = jax.ShapeDtypeStruct((S, CW), F32)
    return pl.pallas_call(
        body, name=name, grid=(S // T,),
        in_specs=[SG_U_SPEC, SG_V_SPEC, SG_DOUT_SPEC, SG_G_SPEC, SG_W_SPEC, SG_BIAS_SPEC],
        out_specs=[SG_ROW_SPEC, SG_ROW_SPEC, SG_G_SPEC, SG_W_SPEC, SG_BIAS_SPEC],
        out_shape=[full, full, jax.ShapeDtypeStruct((1, CW), F32),
                   jax.ShapeDtypeStruct((SG_HEADS, T, T), F32), jax.ShapeDtypeStruct((T, CW), F32)],
        compiler_params=_cparams(("arbitrary",)),
    )(proj, proj, dout, gn, sw, bias)


ADA_COLS = NMOD * D // NDEV


def ada_fwd(c_all, ada_w, ada_b_mine, name):
    def body(c_ref, w_ref, b_ref, o_ref, ca_ref):
        cv = c_ref[...]
        ca = cv * (1.0 / (1.0 + jnp.exp(-cv)))
        ca_ref[...] = ca
        cab = ca.astype(BF16)
        for l in range(L):
            o_ref[l] = jnp.dot(cab, w_ref[l].astype(BF16), preferred_element_type=F32) + b_ref[l]

    return pl.pallas_call(
        body, name=name,
        out_shape=[jax.ShapeDtypeStruct((L, NDEV, ADA_COLS), F32), jax.ShapeDtypeStruct((NDEV, D), F32)],
        compiler_params=_cparams(),
    )(c_all, ada_w, ada_b_mine)


def ada_bwd(ca, dmod_cols, name):
    def body(ca_ref, dm_ref, o_ref):
        cab = ca_ref[...].astype(BF16)
        for l in range(L):
            o_ref[l] = lax.dot_general(cab, dm_ref[l].astype(BF16), (((0,), (0,)), ((), ())),
                                       preferred_element_type=F32)

    return pl.pallas_call(
        body, name=name, out_shape=jax.ShapeDtypeStruct((L, D, ADA_COLS), F32),
        compiler_params=_cparams(),
    )(ca, dmod_cols)


def _adamw(w, g, m, v):
    m = B1 * m + (1.0 - B1) * g
    v = B2 * v + (1.0 - B2) * (g * g)
    m_hat = m / BC1
    v_hat = v / BC2
    delta = -LR * (m_hat / (jnp.sqrt(v_hat) + AEPS) + WD * w)
    return delta, m, v


VEC_ROWS_PER_LAYER = 8
VEC_FINAL_ROW = L * VEC_ROWS_PER_LAYER
VEC_ROWS = VEC_FINAL_ROW + 8
W256_TAPS, W256_CONV_B, W256_GN = 0, 8, 9
W256_ROWS_PER_LAYER = 16


def small_update(vec_all, w256_all, sb_all, sw_all, params, name):
    n_par = len(params)

    def body(*refs):
        vec_ref, w256_ref, sb_ref = refs[:3]
        sw_refs = refs[3:3 + L]
        par_refs = [refs[3 + L + 3 * k:3 + L + 3 * k + 3] for k in range(n_par)]
        out = refs[3 + L + 3 * n_par:]
        out_par = [out[4 * k:4 * k + 4] for k in range(n_par)]
        loss_ref, taps_ref = out[4 * n_par:]

        def total(ref, idx):
            acc = ref[(0,) + idx].astype(F32)
            for d in range(1, NDEV):
                acc = acc + ref[(d,) + idx].astype(F32)
            return acc

        def update(k, region, g):
            w_ref, m_ref, v_ref = par_refs[k]
            g_ref, d_ref, nm_ref, nv_ref = out_par[k]
            delta, nm, nv = _adamw(w_ref[region], g, m_ref[region], v_ref[region])
            g_ref[region] = g
            d_ref[region] = delta
            nm_ref[region] = nm
            nv_ref[region] = nv

        for l in range(L):
            base = l * VEC_ROWS_PER_LAYER
            for k in range(NMOD):
                update(0, (slice(l, l + 1), slice(k * D, (k + 1) * D)), total(vec_ref, (slice(base + k, base + k + 1),)))
            update(1, (slice(l, l + 1),), total(vec_ref, (slice(base + 6, base + 7),)))
            update(2, (slice(l, l + 1),), total(vec_ref, (slice(base + 7, base + 8),)))
            wbase = l * W256_ROWS_PER_LAYER
            update(4, (slice(l, l + 1),), total(w256_ref, (slice(wbase + W256_CONV_B, wbase + W256_CONV_B + 1),)))
            update(5, (slice(l, l + 1),), total(w256_ref, (slice(wbase + W256_GN, wbase + W256_GN + 1),)))
            update(6, (l,), total(sw_refs[l], ()))
            update(7, (l,), total(sb_ref, (slice(l * SG_HEADS, (l + 1) * SG_HEADS),)))
            taps_ref[l] = total(w256_ref, (slice(wbase + W256_TAPS, wbase + W256_TAPS + 8),))
        update(3, (slice(0, 1),), total(vec_ref, (slice(VEC_FINAL_ROW, VEC_FINAL_ROW + 1),)))
        loss_ref[...] = total(vec_ref, (slice(VEC_FINAL_ROW + 1, VEC_FINAL_ROW + 2), slice(0, LANES)))

    out_shape = []
    for w, _, _ in params:
        out_shape += [jax.ShapeDtypeStruct(w.shape, F32)] * 4
    out_shape += [jax.ShapeDtypeStruct((1, LANES), F32), jax.ShapeDtypeStruct((L, 8, CW), F32)]
    outs = pl.pallas_call(body, name=name, out_shape=out_shape, compiler_params=_cparams())(
        vec_all, w256_all, sb_all, *sw_all, *[a for p in params for a in p])
    return [outs[4 * k:4 * k + 4] for k in range(n_par)], outs[4 * n_par:]


def adamw_plain(w, g, m, v, tr, name):
    rows, cols = w.shape
    spec = pl.BlockSpec((tr, cols), lambda i: (i, 0))

    def body(w_ref, g_ref, m_ref, v_ref, d_ref, nm_ref, nv_ref):
        delta, nm, nv = _adamw(w_ref[...], g_ref[...], m_ref[...], v_ref[...])
        d_ref[...] = delta
        nm_ref[...] = nm
        nv_ref[...] = nv

    shp = jax.ShapeDtypeStruct((rows, cols), F32)
    return pl.pallas_call(
        body, name=name, grid=(rows // tr,), in_specs=[spec] * 4, out_specs=[spec] * 3,
        out_shape=[shp, shp, shp], compiler_params=_cparams(("parallel",)),
    )(w, g, m, v)


def adamw_reduce(w, parts, m, v, tr, name, tie=None):
    _, rows, cols = w.shape
    spec = pl.BlockSpec((None, tr, cols), lambda l, i: (l, i, 0))
    pspecs = [pl.BlockSpec((NDEV, tr, cols), lambda l, i, k=k: (0, jnp.where(l == k, i, 0), 0)) for k in range(L)]

    ties = [] if tie is None else [tie]

    def body(w_ref, p0_ref, p1_ref, m_ref, v_ref, *rest):
        g_ref, d_ref, nm_ref, nv_ref = rest[len(ties):]
        first_layer = pl.program_id(0) == 0
        g = jnp.zeros((tr, cols), F32)
        for d in range(NDEV):
            g = g + jnp.where(first_layer, p0_ref[d], p1_ref[d]).astype(F32)
        delta, nm, nv = _adamw(w_ref[...], g, m_ref[...], v_ref[...])
        g_ref[...] = g
        d_ref[...] = delta
        nm_ref[...] = nm
        nv_ref[...] = nv

    shp = jax.ShapeDtypeStruct(w.shape, F32)
    return pl.pallas_call(
        body, name=name, grid=(L, rows // tr),
        in_specs=[spec] + pspecs + [spec, spec] + [pl.BlockSpec(t.shape, lambda l, i: (0, 0)) for t in ties],
        out_specs=[spec] * 4, out_shape=[shp] * 4, compiler_params=_cparams(("parallel", "parallel")),
    )(w, *parts, m, v, *ties)


def _pad_rows(flat, rows):
    return jnp.pad(flat, (0, rows * LANES - flat.shape[0])).reshape(rows, LANES)


def kernel(x, c, ada_w, ada_b, norm_mix_g, norm_mlp_g, w_in, conv_w, conv_b, gmlp_norm_g, spatial_w, spatial_b, w_out, mlp_w1, mlp_w2, final_norm_g, loss_target, m_ada_w, m_ada_b, m_norm_mix_g, m_norm_mlp_g, m_w_in, m_conv_w, m_conv_b, m_gmlp_norm_g, m_spatial_w, m_spatial_b, m_w_out, m_mlp_w1, m_mlp_w2, m_final_norm_g, v_ada_w, v_ada_b, v_norm_mix_g, v_norm_mlp_g, v_w_in, v_conv_w, v_conv_b, v_gmlp_norm_g, v_spatial_w, v_spatial_b, v_w_out, v_mlp_w1, v_mlp_w2, v_final_norm_g):
    me = _lin(_my_pos())
    x0 = x[0]
    target = loss_target[0]
    conv_shard = conv_w.shape[-1]

    w_in_b, w_out_b, w1_b, w2_b = [w.astype(BF16) for w in (w_in, w_out, mlp_w1, mlp_w2)]
    pack0 = _pad_rows(jnp.concatenate([c.reshape(-1), conv_w.reshape(-1)]), 16)
    g0, gw_in0 = run_comm(Gather([pack0, w_in_b[0]]), "gather_first")
    g0 = g0.reshape(NDEV, 16 * LANES)
    c_all = g0[:, :D]
    conv_full = (g0[:, D:D + L * 3 * conv_shard].reshape(NDEV, L, 3, conv_shard)
                 .transpose(1, 2, 0, 3).reshape(L, 3, CW))

    def canonical_w_in(gathered):
        return gathered.transpose(1, 0, 2).reshape(D, PROJ)

    W_in = [canonical_w_in(gw_in0), None]
    W_out, W1, W2 = [None] * L, [None] * L, [None] * L

    ada_b_mine = lax.dynamic_slice(ada_b, (0, me * ADA_COLS), (L, ADA_COLS)).reshape(L, 1, ADA_COLS)
    mod_part, c_act = ada_fwd(c_all, ada_w, ada_b_mine, "ada_fwd")
    gmod = run_comm(Gather([mod_part]), "gather_mod")[0]
    mod = lax.dynamic_index_in_dim(gmod, me, axis=2, keepdims=False)
    mod = mod.transpose(1, 0, 2).reshape(L, NMOD, 1, D)
    early_weights, token = start_copies([w_out_b[0]], me, "gather_early0_start", True, after=gmod)
    mod = tied(mod, token)

    cw8 = jnp.pad(conv_full, ((0, 0), (0, 5), (0, 0)))
    sg_bias = jnp.repeat(spatial_b.transpose(0, 2, 1), HD, axis=2)

    saved = []
    xl = x0
    for l in range(L):
        sh_m, sc_m, g_m, sh_f, sc_f, g_f = [mod[l, k] for k in range(NMOD)]
        h1 = normmod_fwd(xl, norm_mix_g[l:l + 1], sc_m, sh_m, f"norm_mix_fwd{l}")
        if l > 0:
            W_in[l] = canonical_w_in(finish_copies(w_in_handle, xl, f"gather_w_in{l}_wait")[0])
        qkv = mm_layer("proj_qkv", l, h1, W_in[l], out_dtypes=[BF16], cols=(0, QKV))[0]
        proj = mm_layer("proj_rest", l, h1, W_in[l], out_dtypes=[F32], cols=(QKV, REST))[0]
        riders = [w2_b[l]] if l > 0 else [w2_b[l], w1_b[l]]
        a_out, a_tot, gw2, *rode = attn_fwd(qkv, f"attn_fwd{l}", comm=Gather(riders))
        gw_out, gw1 = (finish_copies(early_weights, a_out, f"gather_early{l}_wait") + rode)[:2]
        W_out[l] = gw_out.reshape(D, D)
        W1[l] = gw1
        W2[l] = gw2.reshape(DFF, D)
        if l + 1 < L:
            w_in_handle, token = start_copies([w_in_b[l + 1]], me, f"gather_w_in{l + 1}_start", True, after=a_out)
            early_weights, token = start_copies([w_out_b[l + 1], w1_b[l + 1]], me, f"gather_early{l + 1}_start", True,
                                                after=token)
            g_m = tied(g_m, token)
        c_out = conv_fwd(proj, cw8[l], conv_b[l:l + 1], f"conv_fwd{l}")
        s_out = sg_fwd(proj, gmlp_norm_g[l:l + 1], spatial_w[l], sg_bias[l], f"sg_fwd{l}")
        cat = jnp.concatenate([a_out, c_out.astype(BF16), s_out.astype(BF16)], axis=1)
        mix, x1 = mm_layer("mix", l, cat, W_out[l], out_dtypes=[F32, F32],
                           epilogue=lambda acc, xr, g: (acc, xr + g * acc), extras=[(xl, "tile"), (g_m, "col")])
        h2 = normmod_fwd(x1, norm_mlp_g[l:l + 1], sc_f, sh_f, f"norm_mlp_fwd{l}")
        ra, r = mm_layer("mlp_up", l, h2, W1[l], out_dtypes=[BF16, BF16], b_blocks=True,
                         epilogue=lambda acc: (jnp.maximum(acc, 0.0), jnp.square(jnp.maximum(acc, 0.0))))
        m2, x2 = mm_layer("mlp_down", l, r, W2[l], out_dtypes=[F32, F32],
                          epilogue=lambda acc, xr, g: (acc, xr + g * acc), extras=[(x1, "tile"), (g_f, "col")])
        saved.append(dict(x=xl, h1=h1, proj=proj, qkv=qkv, a_tot=a_tot, cat=cat, mix=mix,
                          x1=x1, h2=h2, ra=ra, r=r, m2=m2))
        xl = x2

    dx, loss_part, d_final_g, dm2, dg_f = loss_head(xl, target, final_norm_g.reshape(1, D),
                                                    (saved[L - 1]["m2"], mod[L - 1, NMOD - 1]), "loss_head")

    p_in, p_out, p_w1, p_w2 = [None] * L, [None] * L, [None] * L, [None] * L
    w_in_grads = [None] * L
    vec_rows, d_norm_mix, d_norm_mlp = [None] * L, [None] * L, [None] * L
    dcw8, d_conv_b, d_gn, d_sw, d_sb = [None] * L, [None] * L, [None] * L, [None] * L, [None] * L
    late_grads = [None] * L
    for l in reversed(range(L)):
        sv = saved[l]
        sh_m, sc_m, g_m, sh_f, sc_f, g_f = [mod[l, k] for k in range(NMOD)]
        da = mm_layer("mlp_down_dgrad", l, dm2, W2[l], out_dtypes=[BF16], trans_b=True,
                      epilogue=lambda acc, rav: (acc * (2.0 * rav.astype(F32)),), extras=[(sv["ra"], "tile")])[0]
        dW2 = mm_layer("mlp_down_wgrad", l, sv["r"], dm2, out_dtypes=[BF16], trans_a=True)[0]
        dW1 = mm_layer("mlp_up_wgrad", l, sv["h2"], da, out_dtypes=[BF16], trans_a=True, out_blocks=True)[0]
        dh2 = mm_layer("mlp_up_dgrad", l, da, W1[l], out_dtypes=[F32], trans_b=True, b_blocks=True)[0]
        dx1, dsc_f, dsh_f, d_norm_mlp[l], dmix, dg_m = normmod_bwd(
            sv["x1"], dh2, dx, norm_mlp_g[l:l + 1], sc_f, f"norm_mlp_bwd{l}", gate_next=(sv["mix"], g_m))
        dcat = mm_layer("mix_dgrad", l, dmix, W_out[l], out_dtypes=[F32], trans_b=True)[0]
        dW_out = mm_layer("mix_wgrad", l, sv["cat"], dmix, out_dtypes=[BF16], trans_a=True)[0]
        pieces_w2, pieces_out = dW2.reshape(NDEV, DFF // NDEV, D), dW_out.reshape(NDEV, D // NDEV, D)
        ride, late = ([pieces_w2, pieces_out], dW1) if l == L - 1 else ([pieces_w2, dW1], pieces_out)
        dq, dk, dv, *arrived = attn_bwd(sv["qkv"], dcat, sv["a_tot"], f"attn_bwd{l}", comm=Exchange(ride))
        p_w2[l] = arrived[0]
        (p_out if l == L - 1 else p_w1)[l] = arrived[1]
        late_grads[l], late_token = start_copies([late], me, f"exchange_late{l}_start", False, after=dq)
        dbg, dcg, dhc, dcw8[l], d_conv_b[l] = conv_bwd(sv["proj"], dcat, cw8[l], conv_b[l:l + 1], f"conv_bwd{l}")
        dus, dvs, d_gn[l], dsw, dbias = sg_bwd(sv["proj"], dcat, gmlp_norm_g[l:l + 1], spatial_w[l], sg_bias[l],
                                               f"sg_bwd{l}")
        d_sw[l] = dsw.astype(BF16)
        d_sb[l] = dbias.reshape(T, SG_HEADS, HD).sum(axis=2).T
        dproj = jnp.concatenate([dq, dk, dv, dbg, dcg, dhc, dus, dvs], axis=1).astype(BF16)
        dW_in = mm_layer("proj_wgrad", l, sv["h1"], dproj, out_dtypes=[BF16], trans_a=True,
                         extras=[(late_token, "tie")])[0]
        pieces = dW_in.reshape(D, NDEV, PROJ // NDEV).transpose(1, 0, 2)
        w_in_grads[l], token = start_copies([pieces], me, f"exchange_w_in{l}_start", False)
        dh1 = mm_layer("proj_dgrad", l, dproj, W_in[l], out_dtypes=[F32], trans_b=True, extras=[(token, "tie")])[0]
        below = (saved[l - 1]["m2"], mod[l - 1, NMOD - 1]) if l > 0 else None
        dx, dsc_m, dsh_m, d_norm_mix[l], *gated_below = normmod_bwd(
            sv["x"], dh1, dx1, tied(norm_mix_g[l:l + 1], token), sc_m, f"norm_mix_bwd{l}", gate_next=below)
        vec_rows[l] = [dsh_m, dsc_m, dg_m, dsh_f, dsc_f, dg_f, d_norm_mix[l], d_norm_mlp[l]]
        if l > 0:
            dm2, dg_f = gated_below

    grad_x = dx.reshape(1, S, D)

    g_w2, d_w2, nm_w2, nv_w2 = adamw_reduce(mlp_w2, p_w2, m_mlp_w2, v_mlp_w2, 256, "adamw_mlp_w2", tie=token)
    p_w1[L - 1] = finish_copies(late_grads[L - 1], d_w2, f"exchange_late{L - 1}_wait")[0]
    g_w1, d_w1, nm_w1, nv_w1 = adamw_reduce(mlp_w1, p_w1, m_mlp_w1, v_mlp_w1, 256, "adamw_mlp_w1", tie=token)

    vec_pack = jnp.concatenate([row for l in range(L) for row in vec_rows[l]]
                               + [d_final_g, loss_part, jnp.zeros((VEC_ROWS - VEC_FINAL_ROW - 2, D), F32)], axis=0)
    vec_pack, _ = lax.optimization_barrier((vec_pack, (d_w1, d_w2)))
    w256_pack = jnp.concatenate([blk for l in range(L) for blk in (
        dcw8[l], d_conv_b[l], d_gn[l], jnp.zeros((W256_ROWS_PER_LAYER - W256_GN - 1, CW), F32))], axis=0)
    vec_all, w256_all, sb_all, *sw_all = run_comm(
        Gather([vec_pack, w256_pack, jnp.concatenate(d_sb, axis=0)] + d_sw), "gather_small_grads")

    dmod_all = (vec_all[:, :VEC_FINAL_ROW].reshape(NDEV, L, VEC_ROWS_PER_LAYER, D)[:, :, :NMOD]
                .reshape(NDEV, L, NMOD * D))
    dmod_cols = lax.dynamic_slice(dmod_all, (0, 0, me * ADA_COLS), (NDEV, L, ADA_COLS)).transpose(1, 0, 2)
    g_ada_w = ada_bwd(c_act, dmod_cols, "ada_bwd")

    flat2 = lambda t: t.reshape(L * D, ADA_COLS)
    d_ada_w, nm_ada_w, nv_ada_w = [t.reshape(L, D, ADA_COLS) for t in adamw_plain(
        flat2(ada_w), flat2(g_ada_w), flat2(m_ada_w), flat2(v_ada_w), 256, "adamw_ada_w")]

    after = jnp.concatenate([t.reshape(-1)[:1] for t in (d_w1, d_w2, d_ada_w)])
    p_in = [finish_copies(w_in_grads[l], after, f"exchange_w_in{l}_wait")[0] for l in range(L)]
    p_out[0] = finish_copies(late_grads[0], after, "exchange_late0_wait")[0]
    g_w_in, d_w_in, nm_w_in, nv_w_in = adamw_reduce(w_in, p_in, m_w_in, v_w_in, 256, "adamw_w_in")
    g_w_out, d_w_out, nm_w_out, nv_w_out = adamw_reduce(w_out, p_out, m_w_out, v_w_out, 128, "adamw_w_out")

    as_row = lambda t: t.reshape(1, D)
    small_params = [(ada_b, m_ada_b, v_ada_b), (norm_mix_g, m_norm_mix_g, v_norm_mix_g),
                    (norm_mlp_g, m_norm_mlp_g, v_norm_mlp_g),
                    (as_row(final_norm_g), as_row(m_final_norm_g), as_row(v_final_norm_g)),
                    (conv_b, m_conv_b, v_conv_b), (gmlp_norm_g, m_gmlp_norm_g, v_gmlp_norm_g),
                    (spatial_w, m_spatial_w, v_spatial_w), (spatial_b, m_spatial_b, v_spatial_b)]
    updated, (loss_sum, taps_sum) = small_update(vec_all, w256_all, sb_all, sw_all, small_params, "small_update")
    loss = loss_sum[0, 0]
    u_ada_b, u_norm_mix, u_norm_mlp, u_final, u_conv_b, u_gn, u_sw, u_sb = updated
    u_final = [t.reshape(D) for t in u_final]
    g_conv_w = lax.dynamic_slice(taps_sum, (0, 0, me * conv_shard), (L, 3, conv_shard))
    flat_cw = lambda t: t.reshape(L * 3, conv_shard)
    u_conv_w = [g_conv_w] + [t.reshape(L, 3, conv_shard) for t in adamw_plain(
        flat_cw(conv_w), flat_cw(g_conv_w), flat_cw(m_conv_w), flat_cw(v_conv_w), L * 3, "adamw_conv_w")]
    small_sets = [u_ada_b, u_norm_mix, u_norm_mlp, u_conv_w, u_conv_b, u_gn, u_sw, u_sb, u_final]
    small_g, sd, snm, snv = [[u[k] for u in small_sets] for k in range(4)]

    def ordered(big, small):
        ada, win, wout, w1, w2 = big
        return [ada, small[0], small[1], small[2], win, small[3], small[4], small[5], small[6], small[7],
                wout, w1, w2, small[8]]

    grads = ordered([g_ada_w, g_w_in, g_w_out, g_w1, g_w2], small_g)
    deltas = ordered([d_ada_w, d_w_in, d_w_out, d_w1, d_w2], sd)
    new_m = ordered([nm_ada_w, nm_w_in, nm_w_out, nm_w1, nm_w2], snm)
    new_v = ordered([nv_ada_w, nv_w_in, nv_w_out, nv_w1, nv_w2], snv)
    return (loss, grad_x, *grads, *deltas, *new_m, *new_v)
```

```python
import functools
import math

import jax
import jax.numpy as jnp
from jax import lax
from jax.experimental import pallas as pl
from jax.experimental.pallas import tpu as pltpu

F32 = jnp.float32
BF16 = jnp.bfloat16
MESH = pl.DeviceIdType.MESH

S = 2048
D = 1024
L = 2
NDEV = 8
HD = 64
NH = 8
PROJ = 2816
DFF = 4096
NMOD = 6
EPS = 1e-6
T = 128
SG_HEADS = 4
LANES = 128
CW = 256
QKV = 3 * NH * HD
REST = PROJ - QKV

LR, B1, B2, AEPS, WD, STEP = 0.001, 0.9, 0.999, 1e-08, 0.01, 10
BC1 = 1.0 - B1 ** STEP
BC2 = 1.0 - B2 ** STEP

VMEM_LIMIT = 48 * 1024 * 1024

HBM_SPEC = pl.BlockSpec(memory_space=pltpu.HBM)


def _cparams(sem=None):
    return pltpu.CompilerParams(dimension_semantics=sem, vmem_limit_bytes=VMEM_LIMIT)


def _my_pos():
    return lax.axis_index("x"), lax.axis_index("y"), lax.axis_index("c")


def _lin(p):
    return 4 * p[0] + 2 * p[1] + p[2]


class Gather:
    def __init__(self, arrs):
        self.arrs = list(arrs)
        n = len(self.arrs)
        self.out_shape = [jax.ShapeDtypeStruct((NDEV,) + a.shape, a.dtype) for a in self.arrs]
        self.scratch = [pltpu.SemaphoreType.DMA((n, 7)), pltpu.SemaphoreType.DMA((n, 7)),
                        pltpu.SemaphoreType.DMA((n,))]

    def phases(self, ins, outs, sems):
        n = len(self.arrs)
        send_sems, recv_sems, local_sems = sems
        x, y, c = _my_pos()
        me, sibling = (x, y, c), (x, y, 1 - c)
        chips = [(1 - x, y), (x, 1 - y), (1 - x, 1 - y)]

        def copy(a, k, block, to, src=None):
            slot = outs[a].at[_lin(block)]
            return pltpu.make_async_remote_copy(
                src_ref=slot if src is None else src, dst_ref=slot,
                send_sem=send_sems.at[a, k], recv_sem=recv_sems.at[a, k],
                device_id=to, device_id_type=MESH)

        def mine(a):
            return pltpu.make_async_copy(ins[a], outs[a].at[_lin(me)], local_sems.at[a])

        def first(a):
            return [copy(a, 0, me, sibling, src=ins[a])] + [
                copy(a, 1 + j, me, (*chip, c), src=ins[a]) for j, chip in enumerate(chips)]

        def passed(a):
            return [copy(a, 4 + j, (*chip, c), sibling) for j, chip in enumerate(chips)]

        def start():
            for a in range(n):
                mine(a).start()
                for cp in first(a):
                    cp.start()

        def relay():
            for j, chip in enumerate(chips):
                for a in range(n):
                    copy(a, 1 + j, (*chip, c), me).wait_recv()
                    passed(a)[j].start()

        def finish():
            for a in range(n):
                copy(a, 0, sibling, me).wait_recv()
            for j, chip in enumerate(chips):
                for a in range(n):
                    copy(a, 4 + j, (*chip, 1 - c), me).wait_recv()
            for a in range(n):
                for cp in first(a) + passed(a):
                    cp.wait_send()
                mine(a).wait()

        return start, relay, finish


class Exchange:
    def __init__(self, arrs):
        self.arrs = list(arrs)
        n = len(self.arrs)
        self.out_shape = [jax.ShapeDtypeStruct(a.shape, a.dtype) for a in self.arrs]
        self.scratch = [pltpu.SemaphoreType.DMA((n, 7)), pltpu.SemaphoreType.DMA((n, 7)),
                        pltpu.SemaphoreType.DMA((n,))]

    def phases(self, ins, outs, sems):
        n = len(self.arrs)
        send_sems, recv_sems, local_sems = sems
        x, y, c = _my_pos()
        me = (x, y, c)

        def peer(mask):
            return (1 - x if mask & 4 else x, 1 - y if mask & 2 else y, 1 - c if mask & 1 else c)

        def copy(a, mask):
            return pltpu.make_async_remote_copy(
                src_ref=ins[a].at[_lin(peer(mask))], dst_ref=outs[a].at[_lin(me)],
                send_sem=send_sems.at[a, mask - 1], recv_sem=recv_sems.at[a, mask - 1],
                device_id=peer(mask), device_id_type=MESH)

        def arrival(a, mask):
            return pltpu.make_async_remote_copy(
                src_ref=ins[a].at[_lin(me)], dst_ref=outs[a].at[_lin(peer(mask))],
                send_sem=send_sems.at[a, mask - 1], recv_sem=recv_sems.at[a, mask - 1],
                device_id=peer(mask), device_id_type=MESH)

        def mine(a):
            return pltpu.make_async_copy(ins[a].at[_lin(me)], outs[a].at[_lin(me)], local_sems.at[a])

        def start():
            for a in range(n):
                mine(a).start()
            for mask in (4, 2, 6, 1, 5, 3, 7):
                for a in range(n):
                    copy(a, mask).start()

        def relay():
            pass

        def finish():
            for mask in range(1, 8):
                for a in range(n):
                    arrival(a, mask).wait_recv()
            for mask in range(1, 8):
                for a in range(n):
                    copy(a, mask).wait_send()
            for a in range(n):
                mine(a).wait()

        return start, relay, finish


def run_comm(plan, name):
    n = len(plan.arrs)

    def body(*refs):
        start, relay, finish = plan.phases(refs[:n], refs[n:2 * n], refs[2 * n:])
        start()
        relay()
        finish()

    outs = pl.pallas_call(
        body, name=name, out_shape=plan.out_shape,
        in_specs=[HBM_SPEC] * n, out_specs=[HBM_SPEC] * n, scratch_shapes=plan.scratch,
    )(*plan.arrs)
    return list(outs)


SEM_SPEC = pl.BlockSpec(memory_space=pltpu.SEMAPHORE)
DATAFLOW = pltpu.SideEffectType.DATAFLOW_SIDE_EFFECTING


def _peer_copies(src_ref, land_ref, send_sems, recv_sems, first, same_block):
    x, y, c = _my_pos()
    me = (x, y, c)
    sends, arrivals = [], []
    for mask in (4, 2, 6, 1, 5, 3, 7):
        peer = (1 - x if mask & 4 else x, 1 - y if mask & 2 else y, 1 - c if mask & 1 else c)
        sends.append(pltpu.make_async_remote_copy(
            src_ref=src_ref if same_block else src_ref.at[_lin(peer)], dst_ref=land_ref.at[_lin(me)],
            send_sem=send_sems.at[first + mask - 1], recv_sem=recv_sems.at[first + mask - 1], device_id=peer,
            device_id_type=MESH))
        arrivals.append(pltpu.make_async_remote_copy(
            src_ref=src_ref if same_block else src_ref.at[_lin(me)], dst_ref=land_ref.at[_lin(peer)],
            send_sem=send_sems.at[first + mask - 1], recv_sem=recv_sems.at[first + mask - 1], device_id=peer,
            device_id_type=MESH))
    return sends, arrivals


def start_copies(srcs, me, name, same_block, after=None):
    n = len(srcs)
    landings = []
    for src in srcs:
        own = src[None] if same_block else lax.dynamic_index_in_dim(src, me, axis=0, keepdims=True)
        landings.append(lax.dynamic_update_slice(lax.empty((NDEV,) + own.shape[1:], src.dtype), own,
                                                 (me,) + (0,) * (own.ndim - 1)))

    def body(*refs):
        send_sems, recv_sems = refs[-2 * n - 3], refs[-2 * n - 2]
        token = refs[-1]
        for k in range(n):
            sends, _ = _peer_copies(refs[2 * k], refs[2 * k + 1], send_sems, recv_sems, 7 * k, same_block)
            for cp in sends:
                cp.start()
        token[...] = jnp.zeros_like(token)

    hbm = lambda a: pltpu.HBM(a.shape, a.dtype)
    pairs = [a for pair in zip(srcs, landings) for a in pair]
    extra = [] if after is None else [after]
    sems = pltpu.SemaphoreType.DMA((7 * n,))
    send_sems, recv_sems, *thru, token = pl.pallas_call(
        body, name=name,
        out_shape=(sems, sems, *[hbm(a) for a in pairs], jax.ShapeDtypeStruct((8, LANES), F32)),
        in_specs=[HBM_SPEC] * (2 * n) + [pl.BlockSpec(memory_space=pl.ANY)] * len(extra),
        out_specs=(SEM_SPEC, SEM_SPEC, *[HBM_SPEC] * (2 * n), pl.BlockSpec(memory_space=pltpu.VMEM)),
        input_output_aliases={k: 2 + k for k in range(2 * n)},
        compiler_params=pltpu.CompilerParams(has_side_effects=DATAFLOW),
    )(*[pltpu.with_memory_space_constraint(a, pltpu.HBM) for a in pairs], *extra)
    return (send_sems, recv_sems, thru, same_block), token


def finish_copies(handle, after, name):
    send_sems, recv_sems, thru, same_block = handle
    n = len(thru) // 2

    def body(*refs):
        send_sems, recv_sems = refs[2 * n], refs[2 * n + 1]
        for k in range(n):
            sends, arrivals = _peer_copies(refs[2 * k], refs[2 * k + 1], send_sems, recv_sems, 7 * k, same_block)
            for cp in sends:
                cp.wait_send()
            for cp in arrivals:
                cp.wait_recv()

    hbm = lambda a: pltpu.HBM(a.shape, a.dtype)
    outs = pl.pallas_call(
        body, name=name, out_shape=tuple(hbm(a) for a in thru),
        in_specs=[HBM_SPEC] * (2 * n) + [SEM_SPEC, SEM_SPEC, pl.BlockSpec(memory_space=pl.ANY)],
        out_specs=tuple([HBM_SPEC] * (2 * n)), input_output_aliases={k: k for k in range(2 * n)},
        compiler_params=pltpu.CompilerParams(has_side_effects=DATAFLOW),
    )(*thru, send_sems, recv_sems, after)
    return [outs[2 * k + 1] for k in range(n)]


def tied(x, token):
    return x + token[0:1, 0:1].astype(x.dtype)


MM_TILES = {
    "proj_qkv": (S, 512), "proj_rest": (S, 256), "mix": (512, D), "mlp_up": (S, 512), "mlp_down": (1024, 256),
    "mlp_down_dgrad": (S, 1024), "mlp_down_wgrad": (1024, 1024), "mlp_up_wgrad": (1024, 512),
    "mlp_up_dgrad": (1024, 512), "mix_dgrad": (1024, 512), "mix_wgrad": (512, 1024),
    "proj_wgrad": (1024, PROJ // 2), "proj_dgrad": (1024, 512),
}


def mm_layer(kind, l, a, b, **kw):
    tm, tn = MM_TILES[kind]
    return mm(a, b, tm=tm, tn=tn, name=f"{kind}{l}", **kw)


def mm(a, b, *, tm, tn, out_dtypes, epilogue=None, extras=(), name, trans_a=False, trans_b=False,
       cols=None, b_blocks=False, out_blocks=False):
    if trans_a:
        kdim, m = a.shape
    else:
        m, kdim = a.shape
    shard = b.shape[-1] if b_blocks else None
    if b_blocks:
        full = (b.shape[1], NDEV * shard)
    else:
        full = b.shape
    first, ncols = cols if cols is not None else (0, full[0] if trans_b else full[1])
    assert full[1 if trans_b else 0] == kdim and m % tm == 0 and ncols % tn == 0 and first % tn == 0
    j0 = first // tn
    if trans_a:
        a_spec = pl.BlockSpec((kdim, tm), lambda i, j: (0, i))
    else:
        a_spec = pl.BlockSpec((tm, kdim), lambda i, j: (i, 0))
    if b_blocks and trans_b:
        b_spec = pl.BlockSpec((NDEV, tn, shard), lambda i, j: (0, j0 + j, 0))
    elif b_blocks:
        assert tn == shard
        b_spec = pl.BlockSpec((None, kdim, tn), lambda i, j: (j0 + j, 0, 0))
    elif trans_b:
        b_spec = pl.BlockSpec((tn, kdim), lambda i, j: (j0 + j, 0))
    else:
        b_spec = pl.BlockSpec((kdim, tn), lambda i, j: (0, j0 + j))
    if out_blocks:
        assert tn * NDEV == ncols
        out_spec = pl.BlockSpec((None, tm, tn), lambda i, j: (j, i, 0))
        out_dims = (NDEV, m, tn)
    else:
        out_spec = pl.BlockSpec((tm, tn), lambda i, j: (i, j))
        out_dims = (m, ncols)
    ex_specs = []
    for arr, kind in extras:
        if kind == "tile":
            ex_specs.append(pl.BlockSpec((tm, tn), lambda i, j: (i, j)))
        elif kind == "col":
            ex_specs.append(pl.BlockSpec((1, tn), lambda i, j: (0, j)))
        else:
            ex_specs.append(pl.BlockSpec(arr.shape, lambda i, j: (0, 0)))
    n_ex, n_out = len(extras), len(out_dtypes)
    used = [k for k, (_, kind) in enumerate(extras) if kind != "tie"]

    def body(a_ref, b_ref, *rest):
        ex_refs, out_refs = rest[:n_ex], rest[n_ex:]
        if trans_a:
            acc = lax.dot_general(a_ref[...], b_ref[...], (((0,), (0,)), ((), ())),
                                  preferred_element_type=F32)
        elif trans_b and b_blocks:
            acc = jnp.zeros((tm, tn), F32)
            for d in range(NDEV):
                acc = acc + lax.dot_general(a_ref[:, d * shard:(d + 1) * shard], b_ref[d],
                                            (((1,), (1,)), ((), ())), preferred_element_type=F32)
        elif trans_b:
            acc = lax.dot_general(a_ref[...], b_ref[...], (((1,), (1,)), ((), ())),
                                  preferred_element_type=F32)
        else:
            acc = jnp.dot(a_ref[...], b_ref[...], preferred_element_type=F32)
        outs = (acc,) if epilogue is None else epilogue(acc, *[ex_refs[k][...] for k in used])
        for o_ref, val in zip(out_refs, outs):
            o_ref[...] = val.astype(o_ref.dtype)

    outs = pl.pallas_call(
        body, name=name, grid=(m // tm, ncols // tn),
        in_specs=[a_spec, b_spec] + ex_specs,
        out_specs=[out_spec for _ in range(n_out)],
        out_shape=[jax.ShapeDtypeStruct(out_dims, dt) for dt in out_dtypes],
        compiler_params=_cparams(("parallel", "parallel")),
    )(a, b, *[arr for arr, _ in extras])
    return list(outs)


TR = 512

ROW_SPEC = pl.BlockSpec((TR, D), lambda i: (i, 0))
VEC_SPEC = pl.BlockSpec((1, D), lambda i: (0, 0))


def _residual_then_norm(acc, xr, gate, g, sc, sh):
    x_new = xr + gate * acc
    rstd = lax.rsqrt(jnp.mean(x_new * x_new, axis=-1, keepdims=True) + EPS)
    return acc, x_new, ((x_new * rstd) * g) * (1.0 + sc) + sh


def normmod_fwd(x, g, sc, sh, name):
    def body(x_ref, g_ref, sc_ref, sh_ref, o_ref):
        xv = x_ref[...]
        rstd = lax.rsqrt(jnp.mean(xv * xv, axis=-1, keepdims=True) + EPS)
        n = (xv * rstd) * g_ref[...]
        o_ref[...] = (n * (1.0 + sc_ref[...]) + sh_ref[...]).astype(o_ref.dtype)

    return pl.pallas_call(
        body, name=name, grid=(S // TR,),
        in_specs=[ROW_SPEC, VEC_SPEC, VEC_SPEC, VEC_SPEC], out_specs=ROW_SPEC,
        out_shape=jax.ShapeDtypeStruct((S, D), BF16),
        compiler_params=_cparams(("parallel",)),
    )(x, g, sc, sh)


def _gate_next(dxv, refs):
    br_ref, gate_ref, dbr_ref, dgate_ref = refs

    @pl.when(pl.program_id(0) == 0)
    def _():
        dgate_ref[...] = jnp.zeros_like(dgate_ref)

    dbr_ref[...] = (dxv * gate_ref[...]).astype(dbr_ref.dtype)
    dgate_ref[...] += jnp.sum(dxv * br_ref[...], axis=0, keepdims=True)


GATE_NEXT_IN = [ROW_SPEC, VEC_SPEC]
GATE_NEXT_OUT = [ROW_SPEC, VEC_SPEC]
GATE_NEXT_SHAPES = [jax.ShapeDtypeStruct((S, D), BF16), jax.ShapeDtypeStruct((1, D), F32)]


def normmod_bwd(x, dh, dres, g, sc, name, gate_next=None):
    nxt = 2 if gate_next else 0

    def body(x_ref, dh_ref, dres_ref, g_ref, sc_ref, *rest):
        nxt_in, (dx_ref, dsc_ref, dsh_ref, dg_ref), nxt_out = rest[:nxt], rest[nxt:nxt + 4], rest[nxt + 4:]

        @pl.when(pl.program_id(0) == 0)
        def _():
            dsc_ref[...] = jnp.zeros_like(dsc_ref)
            dsh_ref[...] = jnp.zeros_like(dsh_ref)
            dg_ref[...] = jnp.zeros_like(dg_ref)

        xv, dh = x_ref[...], dh_ref[...]
        gv = g_ref[...]
        rstd = lax.rsqrt(jnp.mean(xv * xv, axis=-1, keepdims=True) + EPS)
        xhat = xv * rstd
        dn = dh * (1.0 + sc_ref[...])
        dxhat = dn * gv
        dxv = dres_ref[...] + rstd * (dxhat - xhat * jnp.mean(dxhat * xhat, axis=-1, keepdims=True))
        dx_ref[...] = dxv
        dsc_ref[...] += jnp.sum(dh * (xhat * gv), axis=0, keepdims=True)
        dsh_ref[...] += jnp.sum(dh, axis=0, keepdims=True)
        dg_ref[...] += jnp.sum(dn * xhat, axis=0, keepdims=True)
        if gate_next:
            _gate_next(dxv, nxt_in + nxt_out)

    vec_out = jax.ShapeDtypeStruct((1, D), F32)
    on = bool(gate_next)
    return pl.pallas_call(
        body, name=name, grid=(S // TR,),
        in_specs=[ROW_SPEC, ROW_SPEC, ROW_SPEC, VEC_SPEC, VEC_SPEC] + GATE_NEXT_IN * on,
        out_specs=[ROW_SPEC, VEC_SPEC, VEC_SPEC, VEC_SPEC] + GATE_NEXT_OUT * on,
        out_shape=[jax.ShapeDtypeStruct((S, D), F32), vec_out, vec_out, vec_out] + GATE_NEXT_SHAPES * on,
        compiler_params=_cparams(("arbitrary",)),
    )(x, dh, dres, g, sc, *(gate_next or ()))


def loss_head(x, target, g, gate_next, name):
    def body(x_ref, t_ref, g_ref, br_ref, gate_ref, dx_ref, loss_ref, dg_ref, dbr_ref, dgate_ref):
        @pl.when(pl.program_id(0) == 0)
        def _():
            loss_ref[...] = jnp.zeros_like(loss_ref)
            dg_ref[...] = jnp.zeros_like(dg_ref)

        xv, gv = x_ref[...], g_ref[...]
        rstd = lax.rsqrt(jnp.mean(xv * xv, axis=-1, keepdims=True) + EPS)
        xhat = xv * rstd
        err = xhat * gv - t_ref[...]
        loss_ref[...] += jnp.sum(err * err) * (0.5 / D)
        dy = err * (1.0 / D)
        dg_ref[...] += jnp.sum(dy * xhat, axis=0, keepdims=True)
        dxhat = dy * gv
        dxv = rstd * (dxhat - xhat * jnp.mean(dxhat * xhat, axis=-1, keepdims=True))
        dx_ref[...] = dxv
        _gate_next(dxv, (br_ref, gate_ref, dbr_ref, dgate_ref))

    return pl.pallas_call(
        body, name=name, grid=(S // TR,),
        in_specs=[ROW_SPEC, ROW_SPEC, VEC_SPEC] + GATE_NEXT_IN,
        out_specs=[ROW_SPEC, VEC_SPEC, VEC_SPEC] + GATE_NEXT_OUT,
        out_shape=[jax.ShapeDtypeStruct((S, D), F32), jax.ShapeDtypeStruct((1, D), F32),
                   jax.ShapeDtypeStruct((1, D), F32)] + GATE_NEXT_SHAPES,
        compiler_params=_cparams(("arbitrary",)),
    )(x, target, g, *gate_next)


TQ = 512
RS = 128
NSUB = TQ // RS
TK = 128


def _dot_hilo(a, tri_twice):
    hi = a.astype(BF16)
    lo = (a - hi.astype(F32)).astype(BF16)
    return jnp.dot(jnp.concatenate([hi, lo], axis=1), tri_twice, preferred_element_type=F32)


def _log_stay(z):
    return -(jnp.maximum(z, 0.0) + jnp.log(1.0 + jnp.exp(-jnp.abs(z))))


def _tri_and_ones(kind):
    row = jnp.bitwise_and(lax.broadcasted_iota(jnp.int32, (2 * TK, 2 * TK), 0), TK - 1)
    col = lax.broadcasted_iota(jnp.int32, (2 * TK, 2 * TK), 1)
    tri = {"after": row > col, "upto": row <= col, "before": row < col}[kind]
    return jnp.logical_or(col >= TK, tri).astype(BF16)


NPAIR = NH // 2
SCALE = HD ** -0.5


def _pair_specs(first_block):
    rows = pl.BlockSpec((TQ, LANES), lambda p, i: (i, first_block + p))
    whole = pl.BlockSpec((S, LANES), lambda p, i: (0, first_block + p))
    return rows, whole


Q_ROWS_SPEC, _ = _pair_specs(0)
_, K_ALL_SPEC = _pair_specs(NPAIR)
_, V_ALL_SPEC = _pair_specs(2 * NPAIR)
PAIR_ROWS_SPEC = pl.BlockSpec((TQ, LANES), lambda p, i: (i, p))
PAIR_ALL_SPEC = pl.BlockSpec((S, LANES), lambda p, i: (0, p))
PAIR_TOTAL_SPEC = pl.BlockSpec((2, TQ, TK), lambda p, i: (p, i, 0))


def _head_halves(x):
    first = lax.broadcasted_iota(jnp.int32, x.shape, 1) < HD
    zero = jnp.zeros_like(x)
    return jnp.where(first, x, zero), jnp.where(first, zero, x)


def _join_heads(a, b):
    return jnp.where(lax.broadcasted_iota(jnp.int32, a.shape, 1) < HD, a, b)


def _comm_hooks(comm, refs, n_in, n_out, n_scratch):
    nc = len(comm.arrs) if comm is not None else 0
    ins, cin = refs[:n_in], refs[n_in:n_in + nc]
    outs = refs[n_in + nc:n_in + nc + n_out]
    cout = refs[n_in + nc + n_out:n_in + 2 * nc + n_out]
    scratch = refs[n_in + 2 * nc + n_out:n_in + 2 * nc + n_out + n_scratch]
    sems = refs[n_in + 2 * nc + n_out + n_scratch:]
    phases = comm.phases(cin, cout, sems) if comm is not None else None
    return ins, outs, scratch, phases


def _with_comm(comm, in_specs, out_specs, out_shape, operands, scratch):
    if comm is None:
        return dict(in_specs=in_specs, out_specs=out_specs, out_shape=out_shape, scratch_shapes=scratch), operands
    nc = len(comm.arrs)
    return dict(in_specs=in_specs + [HBM_SPEC] * nc, out_specs=out_specs + [HBM_SPEC] * nc,
                out_shape=out_shape + comm.out_shape, scratch_shapes=scratch + comm.scratch), operands + comm.arrs


def attn_fwd(qkv, name, comm=None):
    n_steps = S // TQ

    def body(*refs):
        (q_ref, k_ref, v_ref), (o_ref, r_ref), (acc_ref, z_even, z_odd, w_ref), phases = _comm_hooks(
            comm, refs, 3, 2, 4)
        p = pl.program_id(0)
        i = pl.program_id(1)
        if phases is not None:
            pl.when(jnp.logical_and(p == 0, i == 0))(phases[0])
            pl.when(jnp.logical_and(p == NPAIR - 1, i == n_steps - 2))(phases[1])
        chains = [(sub, h) for sub in range(NSUB) for h in range(2)]
        q_sub = [_head_halves(q_ref[pl.ds(sub * RS, RS), :] * SCALE) for sub in range(NSUB)]
        after = _tri_and_ones("after")
        below_diagonal = (lax.broadcasted_iota(jnp.int32, (RS, TK), 1)
                          < lax.broadcasted_iota(jnp.int32, (RS, TK), 0))
        base = i * NSUB
        all_subs = list(range(NSUB))

        acc_ref[...] = jnp.zeros_like(acc_ref)
        r_ref[...] = jnp.zeros_like(r_ref)
        w_ref[...] = jnp.zeros_like(w_ref)

        def key_rows(block):
            return pl.ds(pl.multiple_of(block * TK, TK), TK)

        def store_scores(z_ref, block, subs):
            kb = k_ref[key_rows(block), :]
            for c, (sub, h) in enumerate(chains):
                if sub in subs:
                    z_ref[c] = lax.dot_general(q_sub[sub][h], kb, (((1,), (1,)), ((), ())),
                                               preferred_element_type=F32)

        def add_weighted_values(block, subs):
            vb = v_ref[key_rows(block), :]
            for sub in subs:
                acc_ref[pl.ds(sub * RS, RS), :] += _join_heads(*[
                    jnp.dot(w_ref[2 * sub + h], vb, preferred_element_type=F32) for h in range(2)])

        def step(block, z_ref, z_next_ref, subs, diagonal_sub, prev_subs, next_subs):
            if prev_subs:
                add_weighted_values(block + 1, prev_subs)
            if next_subs:
                store_scores(z_next_ref, jnp.maximum(block - 1, 0), next_subs)
            active = [(c, sub, h) for c, (sub, h) in enumerate(chains) if sub in subs]
            ls, sums = {}, {}
            for c, sub, h in active:
                ls[c] = _log_stay(z_ref[c])
                sums[c] = _dot_hilo(jnp.where(below_diagonal, ls[c], 0.0) if sub == diagonal_sub else ls[c], after)
            for c, sub, h in active:
                rows = pl.ds(sub * RS, RS)
                later = r_ref[h, rows, :]
                w = jnp.exp(z_ref[c] + ls[c] + (sums[c][:, :TK] + later))
                if sub == diagonal_sub:
                    w = jnp.where(below_diagonal, w, 0.0)
                w_ref[c] = w.astype(BF16)
                r_ref[h, rows, :] = later + sums[c][:, TK:]

        store_scores(z_even, base + NSUB - 1, [NSUB - 1])
        buffers = (z_even, z_odd)
        for j in reversed(range(NSUB)):
            subs = all_subs[j:]
            step(base + j, buffers[0], buffers[1], subs, j, all_subs[j + 1:], all_subs[j - 1:] if j else all_subs)
            buffers = buffers[::-1]
        assert buffers[0] is z_even

        @pl.loop(0, base // 2)
        def _(pair):
            block = base - 1 - 2 * pair
            step(block, z_even, z_odd, all_subs, None, all_subs, all_subs)
            step(block - 1, z_odd, z_even, all_subs, None, all_subs, all_subs)

        add_weighted_values(0, all_subs)
        o_ref[...] = acc_ref[...].astype(o_ref.dtype)
        if phases is not None:
            pl.when(jnp.logical_and(p == NPAIR - 1, i == n_steps - 1))(phases[2])

    kwargs, operands = _with_comm(
        comm, [Q_ROWS_SPEC, K_ALL_SPEC, V_ALL_SPEC], [PAIR_ROWS_SPEC, PAIR_TOTAL_SPEC],
        [jax.ShapeDtypeStruct((S, NH * HD), BF16), jax.ShapeDtypeStruct((NH, S, TK), F32)], [qkv, qkv, qkv],
        [pltpu.VMEM((TQ, LANES), F32), pltpu.VMEM((2 * NSUB, RS, TK), F32), pltpu.VMEM((2 * NSUB, RS, TK), F32),
         pltpu.VMEM((2 * NSUB, RS, TK), BF16)])
    return pl.pallas_call(
        body, name=name, grid=(NPAIR, n_steps),
        compiler_params=_cparams(("arbitrary", "arbitrary")), **kwargs,
    )(*operands)


def attn_bwd(qkv, dout, totals, name, comm=None):
    n_steps = S // TQ

    def body(*refs):
        ((q_ref, k_ref, v_ref, do_ref, r_ref), (dq_ref, dk_ref, dv_ref),
         (z_even, z_odd, dw_even, dw_odd, before_ref, dbefore_ref, dz_ref, w_ref), phases) = _comm_hooks(
            comm, refs, 5, 3, 8)
        p = pl.program_id(0)
        i = pl.program_id(1)
        if phases is not None:
            pl.when(jnp.logical_and(p == 0, i == 0))(phases[0])
            pl.when(jnp.logical_and(p == NPAIR - 1, i == n_steps - 2))(phases[1])

        @pl.when(i == 0)
        def _():
            dk_ref[...] = jnp.zeros_like(dk_ref)
            dv_ref[...] = jnp.zeros_like(dv_ref)

        chains = [(sub, h) for sub in range(NSUB) for h in range(2)]
        nch = len(chains)
        qb = q_ref[...]
        dob = do_ref[...].astype(BF16)
        q_sub = [_head_halves(qb[sub * RS:(sub + 1) * RS] * SCALE) for sub in range(NSUB)]
        do_sub = [_head_halves(dob[sub * RS:(sub + 1) * RS]) for sub in range(NSUB)]
        upto = _tri_and_ones("upto")
        before_tri = _tri_and_ones("before")
        below_diagonal = (lax.broadcasted_iota(jnp.int32, (RS, TK), 1)
                          < lax.broadcasted_iota(jnp.int32, (RS, TK), 0))
        contract_lanes = (((1,), (1,)), ((), ()))
        contract_rows = (((0,), (0,)), ((), ()))
        base = i * NSUB
        all_subs = list(range(NSUB))

        def key_rows(block):
            return pl.ds(pl.multiple_of(block * TK, TK), TK)

        def store_products(bufs, block, subs):
            z_ref, dw_ref = bufs
            kb = k_ref[key_rows(block), :]
            vb = v_ref[key_rows(block), :]
            for c, (sub, h) in enumerate(chains):
                if sub in subs:
                    z_ref[c] = lax.dot_general(q_sub[sub][h], kb, contract_lanes, preferred_element_type=F32)
                    dw_ref[c] = lax.dot_general(do_sub[sub][h], vb, contract_lanes, preferred_element_type=F32)

        def add_gradients(block, subs):
            kb = k_ref[key_rows(block), :]
            for sub in subs:
                rows = pl.ds(sub * RS, RS)
                dq_ref[rows, :] += _join_heads(*[jnp.dot(dz_ref[h, rows, :], kb, preferred_element_type=F32)
                                                 for h in range(2)])
            dk_ref[key_rows(block), :] += _join_heads(*[
                lax.dot_general(dz_ref[h], qb, contract_rows, preferred_element_type=F32) for h in range(2)])
            dv_ref[key_rows(block), :] += _join_heads(*[
                lax.dot_general(w_ref[h], dob, contract_rows, preferred_element_type=F32) for h in range(2)])

        for ref in (dq_ref, before_ref, dbefore_ref, dz_ref, w_ref):
            ref[...] = jnp.zeros_like(ref)
        even, odd = (z_even, dw_even), (z_odd, dw_odd)
        store_products(even, 0, all_subs)

        def step(block, bufs, next_bufs, subs, diagonal_sub, prev_subs, next_subs):
            z_ref, dw_ref = bufs
            add_gradients(jnp.maximum(block - 1, 0), prev_subs)
            for sub in prev_subs:
                if sub not in subs:
                    dz_ref[:, pl.ds(sub * RS, RS), :] = jnp.zeros((2, RS, TK), BF16)
                    w_ref[:, pl.ds(sub * RS, RS), :] = jnp.zeros((2, RS, TK), BF16)
            if next_subs:
                store_products(next_bufs, block + 1, next_subs)
            active = [(c, sub, h) for c, (sub, h) in enumerate(chains) if sub in subs]
            ls, sums, dl, dsums = {}, {}, {}, {}
            for c, sub, h in active:
                ls[c] = _log_stay(z_ref[c])
                sums[c] = _dot_hilo(jnp.where(below_diagonal, ls[c], 0.0) if sub == diagonal_sub else ls[c], upto)
            for c, sub, h in active:
                rows = pl.ds(sub * RS, RS)
                before = before_ref[c]
                log_after = r_ref[h, rows, :] - (sums[c][:, :TK] + before)
                w = jnp.exp((z_ref[c] + ls[c]) + log_after)
                if sub == diagonal_sub:
                    w = jnp.where(below_diagonal, w, 0.0)
                dl[c] = dw_ref[c] * w
                dsums[c] = _dot_hilo(dl[c], before_tri)
                w_ref[h, rows, :] = w.astype(BF16)
                before_ref[c] = before + sums[c][:, TK:]
            for c, sub, h in active:
                rows = pl.ds(sub * RS, RS)
                dbefore = dbefore_ref[c]
                beta = jnp.exp(z_ref[c] + ls[c])
                if sub == diagonal_sub:
                    beta = jnp.where(below_diagonal, beta, 0.0)
                dstay = dsums[c][:, :TK] + dbefore
                dz_ref[h, rows, :] = ((dl[c] * (1.0 - beta) - beta * dstay) * SCALE).astype(BF16)
                dbefore_ref[c] = dbefore + dsums[c][:, TK:]

        @pl.loop(0, base // 2)
        def _(pair):
            step(2 * pair, even, odd, all_subs, None, all_subs, all_subs)
            step(2 * pair + 1, odd, even, all_subs, None, all_subs, all_subs)

        bufs = (even, odd)
        for j in range(NSUB):
            step(base + j, bufs[0], bufs[1], all_subs[j:], j, all_subs[j - 1:] if j else all_subs, all_subs[j + 1:])
            bufs = bufs[::-1]

        add_gradients(base + NSUB - 1, all_subs[NSUB - 1:])
        if phases is not None:
            pl.when(jnp.logical_and(p == NPAIR - 1, i == n_steps - 1))(phases[2])

    full = jax.ShapeDtypeStruct((S, NH * HD), F32)
    kwargs, operands = _with_comm(
        comm, [Q_ROWS_SPEC, K_ALL_SPEC, V_ALL_SPEC, PAIR_ROWS_SPEC, PAIR_TOTAL_SPEC],
        [PAIR_ROWS_SPEC, PAIR_ALL_SPEC, PAIR_ALL_SPEC], [full, full, full], [qkv, qkv, qkv, dout, totals],
        [pltpu.VMEM((2 * NSUB, RS, TK), F32)] * 6 + [pltpu.VMEM((2, TQ, TK), BF16)] * 2)
    return pl.pallas_call(
        body, name=name, grid=(NPAIR, n_steps),
        compiler_params=_cparams(("arbitrary", "arbitrary")), **kwargs,
    )(*operands)


def _proj_cols(first_col):
    base = first_col // LANES
    return pl.BlockSpec((S, LANES), lambda j: (0, base + j))


CONV_OUT_SPEC = pl.BlockSpec((S, LANES), lambda j: (0, j))
CONV_DOUT_SPEC = pl.BlockSpec((S, LANES), lambda j: (0, (NH * HD) // LANES + j))
CONV_W_SPEC = pl.BlockSpec((8, LANES), lambda j: (0, j))
CONV_B_SPEC = pl.BlockSpec((1, LANES), lambda j: (0, j))


def _shift_down(u, n):
    rows = lax.broadcasted_iota(jnp.int32, u.shape, 0)
    return jnp.where(rows >= n, pltpu.roll(u, n, 0), 0.0)


def _shift_up(u, n):
    rows = lax.broadcasted_iota(jnp.int32, u.shape, 0)
    return jnp.where(rows < S - n, pltpu.roll(u, S - n, 0), 0.0)


def conv_fwd(proj, cw8, cb, name):
    def body(bg_ref, cg_ref, hc_ref, w_ref, b_ref, o_ref):
        u = cg_ref[...] * hc_ref[...]
        w = w_ref[...]
        y = w[0:1, :] * _shift_down(u, 2) + w[1:2, :] * _shift_down(u, 1) + w[2:3, :] * u + b_ref[...]
        o_ref[...] = bg_ref[...] * y

    return pl.pallas_call(
        body, name=name, grid=(CW // LANES,),
        in_specs=[_proj_cols(0), _proj_cols(CW), _proj_cols(2 * CW), CONV_W_SPEC, CONV_B_SPEC],
        out_specs=CONV_OUT_SPEC, out_shape=jax.ShapeDtypeStruct((S, CW), F32),
        compiler_params=_cparams(("parallel",)),
    )(proj, proj, proj, cw8, cb)


def conv_bwd(proj, dout, cw8, cb, name):
    def body(bg_ref, cg_ref, hc_ref, do_ref, w_ref, b_ref, dbg_ref, dcg_ref, dhc_ref, dw_ref, db_ref):
        cg, hc, do = cg_ref[...], hc_ref[...], do_ref[...]
        w = w_ref[...]
        u = cg * hc
        u1, u2 = _shift_down(u, 1), _shift_down(u, 2)
        y = w[0:1, :] * u2 + w[1:2, :] * u1 + w[2:3, :] * u + b_ref[...]
        dbg_ref[...] = do * y
        dy = do * bg_ref[...]
        db_ref[...] = jnp.sum(dy, axis=0, keepdims=True)
        dw_ref[...] = jnp.concatenate(
            [jnp.sum(dy * u2, axis=0, keepdims=True), jnp.sum(dy * u1, axis=0, keepdims=True),
             jnp.sum(dy * u, axis=0, keepdims=True), jnp.zeros((5, LANES), F32)], axis=0)
        du = w[2:3, :] * dy + w[1:2, :] * _shift_up(dy, 1) + w[0:1, :] * _shift_up(dy, 2)
        dcg_ref[...] = du * hc
        dhc_ref[...] = du * cg

    full = jax.ShapeDtypeStruct((S, CW), F32)
    return pl.pallas_call(
        body, name=name, grid=(CW // LANES,),
        in_specs=[_proj_cols(0), _proj_cols(CW), _proj_cols(2 * CW), CONV_DOUT_SPEC, CONV_W_SPEC, CONV_B_SPEC],
        out_specs=[CONV_OUT_SPEC, CONV_OUT_SPEC, CONV_OUT_SPEC, CONV_W_SPEC, CONV_B_SPEC],
        out_shape=[full, full, full, jax.ShapeDtypeStruct((8, CW), F32), jax.ShapeDtypeStruct((1, CW), F32)],
        compiler_params=_cparams(("parallel",)),
    )(proj, proj, proj, dout, cw8, cb)


GELU_K = math.sqrt(2.0 / math.pi)
GELU_C = 0.044715


def _gelu(x):
    return 0.5 * x * (1.0 + jnp.tanh(GELU_K * (x + GELU_C * (x * x * x))))


def _gelu_grad(x):
    t = jnp.tanh(GELU_K * (x + GELU_C * (x * x * x)))
    return 0.5 * (1.0 + t) + 0.5 * x * (1.0 - t * t) * (GELU_K * (1.0 + 3.0 * GELU_C * (x * x)))


def _sg_masks():
    row = lax.broadcasted_iota(jnp.int32, (T, T), 0)
    col = lax.broadcasted_iota(jnp.int32, (T, T), 1)
    causal = jnp.right_shift(row, 6) >= jnp.right_shift(col, 6)
    head_of_col = jnp.right_shift(lax.broadcasted_iota(jnp.int32, (T, CW), 1), 6)
    return causal, head_of_col


def _sg_mixed(vnb, sw_ref, bias, causal, head_of_col):
    mixed = bias
    for h in range(SG_HEADS):
        wh = jnp.where(causal, sw_ref[h], 0.0).astype(BF16)
        mh = jnp.dot(wh, vnb, preferred_element_type=F32)
        mixed = mixed + jnp.where(head_of_col == h, mh, 0.0)
    return mixed


SG_U_SPEC = pl.BlockSpec((T, CW), lambda n: (n, 3))
SG_V_SPEC = pl.BlockSpec((T, CW), lambda n: (n, 4))
SG_ROW_SPEC = pl.BlockSpec((T, CW), lambda n: (n, 0))
SG_DOUT_SPEC = pl.BlockSpec((T, CW), lambda n: (n, 3))
SG_G_SPEC = pl.BlockSpec((1, CW), lambda n: (0, 0))
SG_W_SPEC = pl.BlockSpec((SG_HEADS, T, T), lambda n: (0, 0, 0))
SG_BIAS_SPEC = pl.BlockSpec((T, CW), lambda n: (0, 0))


def sg_fwd(proj, gn, sw, bias, name):
    def body(u_ref, v_ref, g_ref, sw_ref, bias_ref, o_ref):
        causal, head_of_col = _sg_masks()
        gv = _gelu(v_ref[...])
        rstd = lax.rsqrt(jnp.mean(gv * gv, axis=-1, keepdims=True) + EPS)
        vnb = ((gv * rstd) * g_ref[...]).astype(BF16)
        mixed = _sg_mixed(vnb, sw_ref, bias_ref[...], causal, head_of_col)
        o_ref[...] = _gelu(u_ref[...]) * mixed

    return pl.pallas_call(
        body, name=name, grid=(S // T,),
        in_specs=[SG_U_SPEC, SG_V_SPEC, SG_G_SPEC, SG_W_SPEC, SG_BIAS_SPEC],
        out_specs=SG_ROW_SPEC, out_shape=jax.ShapeDtypeStruct((S, CW), F32),
        compiler_params=_cparams(("parallel",)),
    )(proj, proj, gn, sw, bias)


def sg_bwd(proj, dout, gn, sw, bias, name):
    def body(u_ref, v_ref, do_ref, g_ref, sw_ref, bias_ref, du_ref, dv_ref, dg_ref, dsw_ref, dbias_ref):
        @pl.when(pl.program_id(0) == 0)
        def _():
            dg_ref[...] = jnp.zeros_like(dg_ref)
            dsw_ref[...] = jnp.zeros_like(dsw_ref)
            dbias_ref[...] = jnp.zeros_like(dbias_ref)

        causal, head_of_col = _sg_masks()
        uv, vv, do, gnv = u_ref[...], v_ref[...], do_ref[...], g_ref[...]
        gv = _gelu(vv)
        rstd = lax.rsqrt(jnp.mean(gv * gv, axis=-1, keepdims=True) + EPS)
        xhat = gv * rstd
        vnb = (xhat * gnv).astype(BF16)
        mixed = _sg_mixed(vnb, sw_ref, bias_ref[...], causal, head_of_col)
        du_ref[...] = (do * mixed) * _gelu_grad(uv)
        dmix = do * _gelu(uv)
        dbias_ref[...] += dmix
        dmixb = dmix.astype(BF16)
        dvn = jnp.zeros((T, CW), F32)
        for h in range(SG_HEADS):
            wh = jnp.where(causal, sw_ref[h], 0.0).astype(BF16)
            dvh = lax.dot_general(wh, dmixb, (((0,), (0,)), ((), ())), preferred_element_type=F32)
            dvn = dvn + jnp.where(head_of_col == h, dvh, 0.0)
            dmh = jnp.where(head_of_col == h, dmixb, jnp.zeros_like(dmixb))
            dwh = lax.dot_general(dmh, vnb, (((1,), (1,)), ((), ())), preferred_element_type=F32)
            dsw_ref[h] += jnp.where(causal, dwh, 0.0)
        dg_ref[...] += jnp.sum(dvn * xhat, axis=0, keepdims=True)
        dxhat = dvn * gnv
        dgv = rstd * (dxhat - xhat * jnp.mean(dxhat * xhat, axis=-1, keepdims=True))
        dv_ref[...] = dgv * _gelu_grad(vv)

    full = jax.ShapeDtypeStruct((S, CW), F32)
    return pl.pallas_call(
        body, name=name, grid=(S // T,),
        in_specs=[SG_U_SPEC, SG_V_SPEC, SG_DOUT_SPEC, SG_G_SPEC, SG_W_SPEC, SG_BIAS_SPEC],
        out_specs=[SG_ROW_SPEC, SG_ROW_SPEC, SG_G_SPEC, SG_W_SPEC, SG_BIAS_SPEC],
        out_shape=[full, full, jax.ShapeDtypeStruct((1, CW), F32),
                   jax.ShapeDtypeStruct((SG_HEADS, T, T), F32), jax.ShapeDtypeStruct((T, CW), F32)],
        compiler_params=_cparams(("arbitrary",)),
    )(proj, proj, dout, gn, sw, bias)


ADA_COLS = NMOD * D // NDEV


def ada_fwd(c_all, ada_w, ada_b_mine, name):
    def body(c_ref, w_ref, b_ref, o_ref, ca_ref):
        cv = c_ref[...]
        ca = cv * (1.0 / (1.0 + jnp.exp(-cv)))
        ca_ref[...] = ca
        cab = ca.astype(BF16)
        for l in range(L):
            o_ref[l] = jnp.dot(cab, w_ref[l].astype(BF16), preferred_element_type=F32) + b_ref[l]

    return pl.pallas_call(
        body, name=name,
        out_shape=[jax.ShapeDtypeStruct((L, NDEV, ADA_COLS), F32), jax.ShapeDtypeStruct((NDEV, D), F32)],
        compiler_params=_cparams(),
    )(c_all, ada_w, ada_b_mine)


def ada_bwd(ca, dmod_cols, name):
    def body(ca_ref, dm_ref, o_ref):
        cab = ca_ref[...].astype(BF16)
        for l in range(L):
            o_ref[l] = lax.dot_general(cab, dm_ref[l].astype(BF16), (((0,), (0,)), ((), ())),
                                       preferred_element_type=F32)

    return pl.pallas_call(
        body, name=name, out_shape=jax.ShapeDtypeStruct((L, D, ADA_COLS), F32),
        compiler_params=_cparams(),
    )(ca, dmod_cols)


def _adamw(w, g, m, v):
    m = B1 * m + (1.0 - B1) * g
    v = B2 * v + (1.0 - B2) * (g * g)
    m_hat = m / BC1
    v_hat = v / BC2
    delta = -LR * (m_hat / (jnp.sqrt(v_hat) + AEPS) + WD * w)
    return delta, m, v


VEC_ROWS_PER_LAYER = 8
VEC_FINAL_ROW = L * VEC_ROWS_PER_LAYER
VEC_ROWS = VEC_FINAL_ROW + 8
W256_TAPS, W256_CONV_B, W256_GN = 0, 8, 9
W256_ROWS_PER_LAYER = 16


def small_update(vec_all, w256_all, sb_all, sw_all, params, name):
    n_par = len(params)

    def body(*refs):
        vec_ref, w256_ref, sb_ref = refs[:3]
        sw_refs = refs[3:3 + L]
        par_refs = [refs[3 + L + 3 * k:3 + L + 3 * k + 3] for k in range(n_par)]
        out = refs[3 + L + 3 * n_par:]
        out_par = [out[4 * k:4 * k + 4] for k in range(n_par)]
        loss_ref, taps_ref = out[4 * n_par:]

        def total(ref, idx):
            acc = ref[(0,) + idx].astype(F32)
            for d in range(1, NDEV):
                acc = acc + ref[(d,) + idx].astype(F32)
            return acc

        def update(k, region, g):
            w_ref, m_ref, v_ref = par_refs[k]
            g_ref, d_ref, nm_ref, nv_ref = out_par[k]
            delta, nm, nv = _adamw(w_ref[region], g, m_ref[region], v_ref[region])
            g_ref[region] = g
            d_ref[region] = delta
            nm_ref[region] = nm
            nv_ref[region] = nv

        for l in range(L):
            base = l * VEC_ROWS_PER_LAYER
            for k in range(NMOD):
                update(0, (slice(l, l + 1), slice(k * D, (k + 1) * D)), total(vec_ref, (slice(base + k, base + k + 1),)))
            update(1, (slice(l, l + 1),), total(vec_ref, (slice(base + 6, base + 7),)))
            update(2, (slice(l, l + 1),), total(vec_ref, (slice(base + 7, base + 8),)))
            wbase = l * W256_ROWS_PER_LAYER
            update(4, (slice(l, l + 1),), total(w256_ref, (slice(wbase + W256_CONV_B, wbase + W256_CONV_B + 1),)))
            update(5, (slice(l, l + 1),), total(w256_ref, (slice(wbase + W256_GN, wbase + W256_GN + 1),)))
            update(6, (l,), total(sw_refs[l], ()))
            update(7, (l,), total(sb_ref, (slice(l * SG_HEADS, (l + 1) * SG_HEADS),)))
            taps_ref[l] = total(w256_ref, (slice(wbase + W256_TAPS, wbase + W256_TAPS + 8),))
        update(3, (slice(0, 1),), total(vec_ref, (slice(VEC_FINAL_ROW, VEC_FINAL_ROW + 1),)))
        loss_ref[...] = total(vec_ref, (slice(VEC_FINAL_ROW + 1, VEC_FINAL_ROW + 2), slice(0, LANES)))

    out_shape = []
    for w, _, _ in params:
        out_shape += [jax.ShapeDtypeStruct(w.shape, F32)] * 4
    out_shape += [jax.ShapeDtypeStruct((1, LANES), F32), jax.ShapeDtypeStruct((L, 8, CW), F32)]
    outs = pl.pallas_call(body, name=name, out_shape=out_shape, compiler_params=_cparams())(
        vec_all, w256_all, sb_all, *sw_all, *[a for p in params for a in p])
    return [outs[4 * k:4 * k + 4] for k in range(n_par)], outs[4 * n_par:]


def adamw_plain(w, g, m, v, tr, name):
    rows, cols = w.shape
    spec = pl.BlockSpec((tr, cols), lambda i: (i, 0))

    def body(w_ref, g_ref, m_ref, v_ref, d_ref, nm_ref, nv_ref):
        delta, nm, nv = _adamw(w_ref[...], g_ref[...], m_ref[...], v_ref[...])
        d_ref[...] = delta
        nm_ref[...] = nm
        nv_ref[...] = nv

    shp = jax.ShapeDtypeStruct((rows, cols), F32)
    return pl.pallas_call(
        body, name=name, grid=(rows // tr,), in_specs=[spec] * 4, out_specs=[spec] * 3,
        out_shape=[shp, shp, shp], compiler_params=_cparams(("parallel",)),
    )(w, g, m, v)


def adamw_reduce(w, parts, m, v, tr, name, tie=None):
    _, rows, cols = w.shape
    spec = pl.BlockSpec((None, tr, cols), lambda l, i: (l, i, 0))
    pspecs = [pl.BlockSpec((NDEV, tr, cols), lambda l, i, k=k: (0, jnp.where(l == k, i, 0), 0)) for k in range(L)]

    ties = [] if tie is None else [tie]

    def body(w_ref, p0_ref, p1_ref, m_ref, v_ref, *rest):
        g_ref, d_ref, nm_ref, nv_ref = rest[len(ties):]
        first_layer = pl.program_id(0) == 0
        g = jnp.zeros((tr, cols), F32)
        for d in range(NDEV):
            g = g + jnp.where(first_layer, p0_ref[d], p1_ref[d]).astype(F32)
        delta, nm, nv = _adamw(w_ref[...], g, m_ref[...], v_ref[...])
        g_ref[...] = g
        d_ref[...] = delta
        nm_ref[...] = nm
        nv_ref[...] = nv

    shp = jax.ShapeDtypeStruct(w.shape, F32)
    return pl.pallas_call(
        body, name=name, grid=(L, rows // tr),
        in_specs=[spec] + pspecs + [spec, spec] + [pl.BlockSpec(t.shape, lambda l, i: (0, 0)) for t in ties],
        out_specs=[spec] * 4, out_shape=[shp] * 4, compiler_params=_cparams(("parallel", "parallel")),
    )(w, *parts, m, v, *ties)


def _pad_rows(flat, rows):
    return jnp.pad(flat, (0, rows * LANES - flat.shape[0])).reshape(rows, LANES)


def kernel(x, c, ada_w, ada_b, norm_mix_g, norm_mlp_g, w_in, conv_w, conv_b, gmlp_norm_g, spatial_w, spatial_b, w_out, mlp_w1, mlp_w2, final_norm_g, loss_target, m_ada_w, m_ada_b, m_norm_mix_g, m_norm_mlp_g, m_w_in, m_conv_w, m_conv_b, m_gmlp_norm_g, m_spatial_w, m_spatial_b, m_w_out, m_mlp_w1, m_mlp_w2, m_final_norm_g, v_ada_w, v_ada_b, v_norm_mix_g, v_norm_mlp_g, v_w_in, v_conv_w, v_conv_b, v_gmlp_norm_g, v_spatial_w, v_spatial_b, v_w_out, v_mlp_w1, v_mlp_w2, v_final_norm_g):
    me = _lin(_my_pos())
    x0 = x[0]
    target = loss_target[0]
    conv_shard = conv_w.shape[-1]

    w_in_b, w_out_b, w1_b, w2_b = [w.astype(BF16) for w in (w_in, w_out, mlp_w1, mlp_w2)]
    pack0 = _pad_rows(jnp.concatenate([c.reshape(-1), conv_w.reshape(-1)]), 16)
    g0, gw_in0 = run_comm(Gather([pack0, w_in_b[0]]), "gather_first")
    g0 = g0.reshape(NDEV, 16 * LANES)
    c_all = g0[:, :D]
    conv_full = (g0[:, D:D + L * 3 * conv_shard].reshape(NDEV, L, 3, conv_shard)
                 .transpose(1, 2, 0, 3).reshape(L, 3, CW))

    def canonical_w_in(gathered):
        return gathered.transpose(1, 0, 2).reshape(D, PROJ)

    W_in = [canonical_w_in(gw_in0), None]
    W_out, W1, W2 = [None] * L, [None] * L, [None] * L

    ada_b_mine = lax.dynamic_slice(ada_b, (0, me * ADA_COLS), (L, ADA_COLS)).reshape(L, 1, ADA_COLS)
    mod_part, c_act = ada_fwd(c_all, ada_w, ada_b_mine, "ada_fwd")
    gmod = run_comm(Gather([mod_part]), "gather_mod")[0]
    mod = lax.dynamic_index_in_dim(gmod, me, axis=2, keepdims=False)
    mod = mod.transpose(1, 0, 2).reshape(L, NMOD, 1, D)
    early_weights, token = start_copies([w_out_b[0]], me, "gather_early0_start", True, after=gmod)
    mod = tied(mod, token)

    cw8 = jnp.pad(conv_full, ((0, 0), (0, 5), (0, 0)))
    sg_bias = jnp.repeat(spatial_b.transpose(0, 2, 1), HD, axis=2)

    saved = []
    xl = x0
    for l in range(L):
        sh_m, sc_m, g_m, sh_f, sc_f, g_f = [mod[l, k] for k in range(NMOD)]
        h1 = normmod_fwd(xl, norm_mix_g[l:l + 1], sc_m, sh_m, f"norm_mix_fwd{l}")
        if l > 0:
            W_in[l] = canonical_w_in(finish_copies(w_in_handle, xl, f"gather_w_in{l}_wait")[0])
        qkv = mm_layer("proj_qkv", l, h1, W_in[l], out_dtypes=[BF16], cols=(0, QKV))[0]
        proj = mm_layer("proj_rest", l, h1, W_in[l], out_dtypes=[F32], cols=(QKV, REST))[0]
        riders = [w2_b[l]] if l > 0 else [w2_b[l], w1_b[l]]
        a_out, a_tot, gw2, *rode = attn_fwd(qkv, f"attn_fwd{l}", comm=Gather(riders))
        gw_out, gw1 = (finish_copies(early_weights, a_out, f"gather_early{l}_wait") + rode)[:2]
        W_out[l] = gw_out.reshape(D, D)
        W1[l] = gw1
        W2[l] = gw2.reshape(DFF, D)
        if l + 1 < L:
            w_in_handle, token = start_copies([w_in_b[l + 1]], me, f"gather_w_in{l + 1}_start", True, after=a_out)
            early_weights, token = start_copies([w_out_b[l + 1], w1_b[l + 1]], me, f"gather_early{l + 1}_start", True,
                                                after=token)
            g_m = tied(g_m, token)
        c_out = conv_fwd(proj, cw8[l], conv_b[l:l + 1], f"conv_fwd{l}")
        s_out = sg_fwd(proj, gmlp_norm_g[l:l + 1], spatial_w[l], sg_bias[l], f"sg_fwd{l}")
        cat = jnp.concatenate([a_out, c_out.astype(BF16), s_out.astype(BF16)], axis=1)
        mix, x1, h2 = mm_layer("mix", l, cat, W_out[l], out_dtypes=[F32, F32, BF16], epilogue=_residual_then_norm,
                               extras=[(xl, "tile"), (g_m, "col"), (norm_mlp_g[l:l + 1], "col"), (sc_f, "col"),
                                       (sh_f, "col")])
        ra, r = mm_layer("mlp_up", l, h2, W1[l], out_dtypes=[BF16, BF16], b_blocks=True,
                         epilogue=lambda acc: (jnp.maximum(acc, 0.0), jnp.square(jnp.maximum(acc, 0.0))))
        m2, x2 = mm_layer("mlp_down", l, r, W2[l], out_dtypes=[F32, F32],
                          epilogue=lambda acc, xr, g: (acc, xr + g * acc), extras=[(x1, "tile"), (g_f, "col")])
        saved.append(dict(x=xl, h1=h1, proj=proj, qkv=qkv, a_tot=a_tot, cat=cat, mix=mix,
                          x1=x1, h2=h2, ra=ra, r=r, m2=m2))
        xl = x2

    dx, loss_part, d_final_g, dm2, dg_f = loss_head(xl, target, final_norm_g.reshape(1, D),
                                                    (saved[L - 1]["m2"], mod[L - 1, NMOD - 1]), "loss_head")

    p_in, p_out, p_w1, p_w2 = [None] * L, [None] * L, [None] * L, [None] * L
    w_in_grads = [None] * L
    vec_rows, d_norm_mix, d_norm_mlp = [None] * L, [None] * L, [None] * L
    dcw8, d_conv_b, d_gn, d_sw, d_sb = [None] * L, [None] * L, [None] * L, [None] * L, [None] * L
    late_grads = [None] * L
    for l in reversed(range(L)):
        sv = saved[l]
        sh_m, sc_m, g_m, sh_f, sc_f, g_f = [mod[l, k] for k in range(NMOD)]
        da = mm_layer("mlp_down_dgrad", l, dm2, W2[l], out_dtypes=[BF16], trans_b=True,
                      epilogue=lambda acc, rav: (acc * (2.0 * rav.astype(F32)),), extras=[(sv["ra"], "tile")])[0]
        dW2 = mm_layer("mlp_down_wgrad", l, sv["r"], dm2, out_dtypes=[BF16], trans_a=True)[0]
        dW1 = mm_layer("mlp_up_wgrad", l, sv["h2"], da, out_dtypes=[BF16], trans_a=True, out_blocks=True)[0]
        dh2 = mm_layer("mlp_up_dgrad", l, da, W1[l], out_dtypes=[F32], trans_b=True, b_blocks=True)[0]
        dx1, dsc_f, dsh_f, d_norm_mlp[l], dmix, dg_m = normmod_bwd(
            sv["x1"], dh2, dx, norm_mlp_g[l:l + 1], sc_f, f"norm_mlp_bwd{l}", gate_next=(sv["mix"], g_m))
        dcat = mm_layer("mix_dgrad", l, dmix, W_out[l], out_dtypes=[F32], trans_b=True)[0]
        dW_out = mm_layer("mix_wgrad", l, sv["cat"], dmix, out_dtypes=[BF16], trans_a=True)[0]
        pieces_w2, pieces_out = dW2.reshape(NDEV, DFF // NDEV, D), dW_out.reshape(NDEV, D // NDEV, D)
        ride, late = ([pieces_w2, pieces_out], dW1) if l == L - 1 else ([pieces_w2, dW1], pieces_out)
        dq, dk, dv, *arrived = attn_bwd(sv["qkv"], dcat, sv["a_tot"], f"attn_bwd{l}", comm=Exchange(ride))
        p_w2[l] = arrived[0]
        (p_out if l == L - 1 else p_w1)[l] = arrived[1]
        late_grads[l], late_token = start_copies([late], me, f"exchange_late{l}_start", False, after=dq)
        dbg, dcg, dhc, dcw8[l], d_conv_b[l] = conv_bwd(sv["proj"], dcat, cw8[l], conv_b[l:l + 1], f"conv_bwd{l}")
        dus, dvs, d_gn[l], dsw, dbias = sg_bwd(sv["proj"], dcat, gmlp_norm_g[l:l + 1], spatial_w[l], sg_bias[l],
                                               f"sg_bwd{l}")
        d_sw[l] = dsw.astype(BF16)
        d_sb[l] = dbias.reshape(T, SG_HEADS, HD).sum(axis=2).T
        dproj = jnp.concatenate([dq, dk, dv, dbg, dcg, dhc, dus, dvs], axis=1).astype(BF16)
        dW_in = mm_layer("proj_wgrad", l, sv["h1"], dproj, out_dtypes=[BF16], trans_a=True,
                         extras=[(late_token, "tie")])[0]
        pieces = dW_in.reshape(D, NDEV, PROJ // NDEV).transpose(1, 0, 2)
        w_in_grads[l], token = start_copies([pieces], me, f"exchange_w_in{l}_start", False)
        dh1 = mm_layer("proj_dgrad", l, dproj, W_in[l], out_dtypes=[F32], trans_b=True, extras=[(token, "tie")])[0]
        below = (saved[l - 1]["m2"], mod[l - 1, NMOD - 1]) if l > 0 else None
        dx, dsc_m, dsh_m, d_norm_mix[l], *gated_below = normmod_bwd(
            sv["x"], dh1, dx1, tied(norm_mix_g[l:l + 1], token), sc_m, f"norm_mix_bwd{l}", gate_next=below)
        vec_rows[l] = [dsh_m, dsc_m, dg_m, dsh_f, dsc_f, dg_f, d_norm_mix[l], d_norm_mlp[l]]
        if l > 0:
            dm2, dg_f = gated_below

    grad_x = dx.reshape(1, S, D)

    g_w2, d_w2, nm_w2, nv_w2 = adamw_reduce(mlp_w2, p_w2, m_mlp_w2, v_mlp_w2, 256, "adamw_mlp_w2", tie=token)
    p_w1[L - 1] = finish_copies(late_grads[L - 1], d_w2, f"exchange_late{L - 1}_wait")[0]
    g_w1, d_w1, nm_w1, nv_w1 = adamw_reduce(mlp_w1, p_w1, m_mlp_w1, v_mlp_w1, 256, "adamw_mlp_w1", tie=token)

    vec_pack = jnp.concatenate([row for l in range(L) for row in vec_rows[l]]
                               + [d_final_g, loss_part, jnp.zeros((VEC_ROWS - VEC_FINAL_ROW - 2, D), F32)], axis=0)
    vec_pack, _ = lax.optimization_barrier((vec_pack, (d_w1, d_w2)))
    w256_pack = jnp.concatenate([blk for l in range(L) for blk in (
        dcw8[l], d_conv_b[l], d_gn[l], jnp.zeros((W256_ROWS_PER_LAYER - W256_GN - 1, CW), F32))], axis=0)
    vec_all, w256_all, sb_all, *sw_all = run_comm(
        Gather([vec_pack, w256_pack, jnp.concatenate(d_sb, axis=0)] + d_sw), "gather_small_grads")

    dmod_all = (vec_all[:, :VEC_FINAL_ROW].reshape(NDEV, L, VEC_ROWS_PER_LAYER, D)[:, :, :NMOD]
                .reshape(NDEV, L, NMOD * D))
    dmod_cols = lax.dynamic_slice(dmod_all, (0, 0, me * ADA_COLS), (NDEV, L, ADA_COLS)).transpose(1, 0, 2)
    g_ada_w = ada_bwd(c_act, dmod_cols, "ada_bwd")

    flat2 = lambda t: t.reshape(L * D, ADA_COLS)
    d_ada_w, nm_ada_w, nv_ada_w = [t.reshape(L, D, ADA_COLS) for t in adamw_plain(
        flat2(ada_w), flat2(g_ada_w), flat2(m_ada_w), flat2(v_ada_w), 256, "adamw_ada_w")]

    after = jnp.concatenate([t.reshape(-1)[:1] for t in (d_w1, d_w2, d_ada_w)])
    p_in = [finish_copies(w_in_grads[l], after, f"exchange_w_in{l}_wait")[0] for l in range(L)]
    p_out[0] = finish_copies(late_grads[0], after, "exchange_late0_wait")[0]
    g_w_in, d_w_in, nm_w_in, nv_w_in = adamw_reduce(w_in, p_in, m_w_in, v_w_in, 256, "adamw_w_in")
    g_w_out, d_w_out, nm_w_out, nv_w_out = adamw_reduce(w_out, p_out, m_w_out, v_w_out, 128, "adamw_w_out")

    as_row = lambda t: t.reshape(1, D)
    small_params = [(ada_b, m_ada_b, v_ada_b), (norm_mix_g, m_norm_mix_g, v_norm_mix_g),
                    (norm_mlp_g, m_norm_mlp_g, v_norm_mlp_g),
                    (as_row(final_norm_g), as_row(m_final_norm_g), as_row(v_final_norm_g)),
                    (conv_b, m_conv_b, v_conv_b), (gmlp_norm_g, m_gmlp_norm_g, v_gmlp_norm_g),
                    (spatial_w, m_spatial_w, v_spatial_w), (spatial_b, m_spatial_b, v_spatial_b)]
    updated, (loss_sum, taps_sum) = small_update(vec_all, w256_all, sb_all, sw_all, small_params, "small_update")
    loss = loss_sum[0, 0]
    u_ada_b, u_norm_mix, u_norm_mlp, u_final, u_conv_b, u_gn, u_sw, u_sb = updated
    u_final = [t.reshape(D) for t in u_final]
    g_conv_w = lax.dynamic_slice(taps_sum, (0, 0, me * conv_shard), (L, 3, conv_shard))
    flat_cw = lambda t: t.reshape(L * 3, conv_shard)
    u_conv_w = [g_conv_w] + [t.reshape(L, 3, conv_shard) for t in adamw_plain(
        flat_cw(conv_w), flat_cw(g_conv_w), flat_cw(m_conv_w), flat_cw(v_conv_w), L * 3, "adamw_conv_w")]
    small_sets = [u_ada_b, u_norm_mix, u_norm_mlp, u_conv_w, u_conv_b, u_gn, u_sw, u_sb, u_final]
    small_g, sd, snm, snv = [[u[k] for u in small_sets] for k in range(4)]

    def ordered(big, small):
        ada, win, wout, w1, w2 = big
        return [ada, small[0], small[1], small[2], win, small[3], small[4], small[5], small[6], small[7],
                wout, w1, w2, small[8]]

    grads = ordered([g_ada_w, g_w_in, g_w_out, g_w1, g_w2], small_g)
    deltas = ordered([d_ada_w, d_w_in, d_w_out, d_w1, d_w2], sd)
    new_m = ordered([nm_ada_w, nm_w_in, nm_w_out, nm_w1, nm_w2], snm)
    new_v = ordered([nv_ada_w, nv_w_in, nv_w_out, nv_w1, nv_w2], snv)
    return (loss, grad_x, *grads, *deltas, *new_m, *new_v)
```

```python
import functools
import math

import jax
import jax.numpy as jnp
from jax import lax
from jax.experimental import pallas as pl
from jax.experimental.pallas import tpu as pltpu

F32 = jnp.float32
BF16 = jnp.bfloat16
MESH = pl.DeviceIdType.MESH

S = 2048
D = 1024
L = 2
NDEV = 8
HD = 64
NH = 8
PROJ = 2816
DFF = 4096
NMOD = 6
EPS = 1e-6
T = 128
SG_HEADS = 4
LANES = 128
CW = 256
QKV = 3 * NH * HD
REST = PROJ - QKV

LR, B1, B2, AEPS, WD, STEP = 0.001, 0.9, 0.999, 1e-08, 0.01, 10
BC1 = 1.0 - B1 ** STEP
BC2 = 1.0 - B2 ** STEP

VMEM_LIMIT = 48 * 1024 * 1024

HBM_SPEC = pl.BlockSpec(memory_space=pltpu.HBM)


def _cparams(sem=None):
    return pltpu.CompilerParams(dimension_semantics=sem, vmem_limit_bytes=VMEM_LIMIT)


def _my_pos():
    return lax.axis_index("x"), lax.axis_index("y"), lax.axis_index("c")


def _lin(p):
    return 4 * p[0] + 2 * p[1] + p[2]


class Gather:
    def __init__(self, arrs):
        self.arrs = list(arrs)
        n = len(self.arrs)
        self.out_shape = [jax.ShapeDtypeStruct((NDEV,) + a.shape, a.dtype) for a in self.arrs]
        self.scratch = [pltpu.SemaphoreType.DMA((n, 7)), pltpu.SemaphoreType.DMA((n, 7)),
                        pltpu.SemaphoreType.DMA((n,))]

    def phases(self, ins, outs, sems):
        n = len(self.arrs)
        send_sems, recv_sems, local_sems = sems
        x, y, c = _my_pos()
        me, sibling = (x, y, c), (x, y, 1 - c)
        chips = [(1 - x, y), (x, 1 - y), (1 - x, 1 - y)]

        def copy(a, k, block, to, src=None):
            slot = outs[a].at[_lin(block)]
            return pltpu.make_async_remote_copy(
                src_ref=slot if src is None else src, dst_ref=slot,
                send_sem=send_sems.at[a, k], recv_sem=recv_sems.at[a, k],
                device_id=to, device_id_type=MESH)

        def mine(a):
            return pltpu.make_async_copy(ins[a], outs[a].at[_lin(me)], local_sems.at[a])

        def first(a):
            return [copy(a, 0, me, sibling, src=ins[a])] + [
                copy(a, 1 + j, me, (*chip, c), src=ins[a]) for j, chip in enumerate(chips)]

        def passed(a):
            return [copy(a, 4 + j, (*chip, c), sibling) for j, chip in enumerate(chips)]

        def start():
            for a in range(n):
                mine(a).start()
                for cp in first(a):
                    cp.start()

        def relay():
            for j, chip in enumerate(chips):
                for a in range(n):
                    copy(a, 1 + j, (*chip, c), me).wait_recv()
                    passed(a)[j].start()

        def finish():
            for a in range(n):
                copy(a, 0, sibling, me).wait_recv()
            for j, chip in enumerate(chips):
                for a in range(n):
                    copy(a, 4 + j, (*chip, 1 - c), me).wait_recv()
            for a in range(n):
                for cp in first(a) + passed(a):
                    cp.wait_send()
                mine(a).wait()

        return start, relay, finish


class Exchange:
    def __init__(self, arrs):
        self.arrs = list(arrs)
        n = len(self.arrs)
        self.out_shape = [jax.ShapeDtypeStruct(a.shape, a.dtype) for a in self.arrs]
        self.scratch = [pltpu.SemaphoreType.DMA((n, 7)), pltpu.SemaphoreType.DMA((n, 7)),
                        pltpu.SemaphoreType.DMA((n,))]

    def phases(self, ins, outs, sems):
        n = len(self.arrs)
        send_sems, recv_sems, local_sems = sems
        x, y, c = _my_pos()
        me = (x, y, c)

        def peer(mask):
            return (1 - x if mask & 4 else x, 1 - y if mask & 2 else y, 1 - c if mask & 1 else c)

        def copy(a, mask):
            return pltpu.make_async_remote_copy(
                src_ref=ins[a].at[_lin(peer(mask))], dst_ref=outs[a].at[_lin(me)],
                send_sem=send_sems.at[a, mask - 1], recv_sem=recv_sems.at[a, mask - 1],
                device_id=peer(mask), device_id_type=MESH)

        def arrival(a, mask):
            return pltpu.make_async_remote_copy(
                src_ref=ins[a].at[_lin(me)], dst_ref=outs[a].at[_lin(peer(mask))],
                send_sem=send_sems.at[a, mask - 1], recv_sem=recv_sems.at[a, mask - 1],
                device_id=peer(mask), device_id_type=MESH)

        def mine(a):
            return pltpu.make_async_copy(ins[a].at[_lin(me)], outs[a].at[_lin(me)], local_sems.at[a])

        def start():
            for a in range(n):
                mine(a).start()
            for mask in (4, 2, 6, 1, 5, 3, 7):
                for a in range(n):
                    copy(a, mask).start()

        def relay():
            pass

        def finish():
            for mask in range(1, 8):
                for a in range(n):
                    arrival(a, mask).wait_recv()
            for mask in range(1, 8):
                for a in range(n):
                    copy(a, mask).wait_send()
            for a in range(n):
                mine(a).wait()

        return start, relay, finish


def run_comm(plan, name):
    n = len(plan.arrs)

    def body(*refs):
        start, relay, finish = plan.phases(refs[:n], refs[n:2 * n], refs[2 * n:])
        start()
        relay()
        finish()

    outs = pl.pallas_call(
        body, name=name, out_shape=plan.out_shape,
        in_specs=[HBM_SPEC] * n, out_specs=[HBM_SPEC] * n, scratch_shapes=plan.scratch,
    )(*plan.arrs)
    return list(outs)


SEM_SPEC = pl.BlockSpec(memory_space=pltpu.SEMAPHORE)
DATAFLOW = pltpu.SideEffectType.DATAFLOW_SIDE_EFFECTING


def _peer_copies(src_ref, land_ref, send_sems, recv_sems, first, same_block):
    x, y, c = _my_pos()
    me = (x, y, c)
    sends, arrivals = [], []
    for mask in (4, 2, 6, 1, 5, 3, 7):
        peer = (1 - x if mask & 4 else x, 1 - y if mask & 2 else y, 1 - c if mask & 1 else c)
        sends.append(pltpu.make_async_remote_copy(
            src_ref=src_ref if same_block else src_ref.at[_lin(peer)], dst_ref=land_ref.at[_lin(me)],
            send_sem=send_sems.at[first + mask - 1], recv_sem=recv_sems.at[first + mask - 1], device_id=peer,
            device_id_type=MESH))
        arrivals.append(pltpu.make_async_remote_copy(
            src_ref=src_ref if same_block else src_ref.at[_lin(me)], dst_ref=land_ref.at[_lin(peer)],
            send_sem=send_sems.at[first + mask - 1], recv_sem=recv_sems.at[first + mask - 1], device_id=peer,
            device_id_type=MESH))
    return sends, arrivals


def start_copies(srcs, me, name, same_block, after=None):
    n = len(srcs)
    landings = []
    for src in srcs:
        own = src[None] if same_block else lax.dynamic_index_in_dim(src, me, axis=0, keepdims=True)
        landings.append(lax.dynamic_update_slice(lax.empty((NDEV,) + own.shape[1:], src.dtype), own,
                                                 (me,) + (0,) * (own.ndim - 1)))

    def body(*refs):
        send_sems, recv_sems = refs[-2 * n - 3], refs[-2 * n - 2]
        token = refs[-1]
        for k in range(n):
            sends, _ = _peer_copies(refs[2 * k], refs[2 * k + 1], send_sems, recv_sems, 7 * k, same_block)
            for cp in sends:
                cp.start()
        token[...] = jnp.zeros_like(token)

    hbm = lambda a: pltpu.HBM(a.shape, a.dtype)
    pairs = [a for pair in zip(srcs, landings) for a in pair]
    extra = [] if after is None else [after]
    sems = pltpu.SemaphoreType.DMA((7 * n,))
    send_sems, recv_sems, *thru, token = pl.pallas_call(
        body, name=name,
        out_shape=(sems, sems, *[hbm(a) for a in pairs], jax.ShapeDtypeStruct((8, LANES), F32)),
        in_specs=[HBM_SPEC] * (2 * n) + [pl.BlockSpec(memory_space=pl.ANY)] * len(extra),
        out_specs=(SEM_SPEC, SEM_SPEC, *[HBM_SPEC] * (2 * n), pl.BlockSpec(memory_space=pltpu.VMEM)),
        input_output_aliases={k: 2 + k for k in range(2 * n)},
        compiler_params=pltpu.CompilerParams(has_side_effects=DATAFLOW),
    )(*[pltpu.with_memory_space_constraint(a, pltpu.HBM) for a in pairs], *extra)
    return (send_sems, recv_sems, thru, same_block), token


def finish_copies(handle, after, name):
    send_sems, recv_sems, thru, same_block = handle
    n = len(thru) // 2

    def body(*refs):
        send_sems, recv_sems = refs[2 * n], refs[2 * n + 1]
        for k in range(n):
            sends, arrivals = _peer_copies(refs[2 * k], refs[2 * k + 1], send_sems, recv_sems, 7 * k, same_block)
            for cp in sends:
                cp.wait_send()
            for cp in arrivals:
                cp.wait_recv()

    hbm = lambda a: pltpu.HBM(a.shape, a.dtype)
    outs = pl.pallas_call(
        body, name=name, out_shape=tuple(hbm(a) for a in thru),
        in_specs=[HBM_SPEC] * (2 * n) + [SEM_SPEC, SEM_SPEC, pl.BlockSpec(memory_space=pl.ANY)],
        out_specs=tuple([HBM_SPEC] * (2 * n)), input_output_aliases={k: k for k in range(2 * n)},
        compiler_params=pltpu.CompilerParams(has_side_effects=DATAFLOW),
    )(*thru, send_sems, recv_sems, after)
    return [outs[2 * k + 1] for k in range(n)]


def tied(x, token):
    return x + token[0:1, 0:1].astype(x.dtype)


MM_TILES = {
    "proj_qkv": (S, 512), "proj_rest": (S, 256), "mix": (512, D), "mlp_up": (S, 512), "mlp_down": (1024, 256),
    "mlp_down_dgrad": (S, 1024), "mlp_down_wgrad": (1024, 1024), "mlp_up_wgrad": (1024, 512),
    "mlp_up_dgrad": (1024, 512), "mix_dgrad": (1024, 512), "mix_wgrad": (512, 1024),
    "proj_wgrad": (1024, PROJ // 2), "proj_dgrad": (1024, 512),
}


def mm_layer(kind, l, a, b, **kw):
    tm, tn = MM_TILES[kind]
    return mm(a, b, tm=tm, tn=tn, name=f"{kind}{l}", **kw)


def mm(a, b, *, tm, tn, out_dtypes, epilogue=None, extras=(), name, trans_a=False, trans_b=False,
       cols=None, b_blocks=False, out_blocks=False):
    if trans_a:
        kdim, m = a.shape
    else:
        m, kdim = a.shape
    shard = b.shape[-1] if b_blocks else None
    if b_blocks:
        full = (b.shape[1], NDEV * shard)
    else:
        full = b.shape
    first, ncols = cols if cols is not None else (0, full[0] if trans_b else full[1])
    assert full[1 if trans_b else 0] == kdim and m % tm == 0 and ncols % tn == 0 and first % tn == 0
    j0 = first // tn
    if trans_a:
        a_spec = pl.BlockSpec((kdim, tm), lambda i, j: (0, i))
    else:
        a_spec = pl.BlockSpec((tm, kdim), lambda i, j: (i, 0))
    if b_blocks and trans_b:
        b_spec = pl.BlockSpec((NDEV, tn, shard), lambda i, j: (0, j0 + j, 0))
    elif b_blocks:
        assert tn == shard
        b_spec = pl.BlockSpec((None, kdim, tn), lambda i, j: (j0 + j, 0, 0))
    elif trans_b:
        b_spec = pl.BlockSpec((tn, kdim), lambda i, j: (j0 + j, 0))
    else:
        b_spec = pl.BlockSpec((kdim, tn), lambda i, j: (0, j0 + j))
    if out_blocks:
        assert tn * NDEV == ncols
        out_spec = pl.BlockSpec((None, tm, tn), lambda i, j: (j, i, 0))
        out_dims = (NDEV, m, tn)
    else:
        out_spec = pl.BlockSpec((tm, tn), lambda i, j: (i, j))
        out_dims = (m, ncols)
    ex_specs = []
    for arr, kind in extras:
        if kind == "tile":
            ex_specs.append(pl.BlockSpec((tm, tn), lambda i, j: (i, j)))
        elif kind == "col":
            ex_specs.append(pl.BlockSpec((1, tn), lambda i, j: (0, j)))
        else:
            ex_specs.append(pl.BlockSpec(arr.shape, lambda i, j: (0, 0)))
    n_ex, n_out = len(extras), len(out_dtypes)
    used = [k for k, (_, kind) in enumerate(extras) if kind != "tie"]

    def body(a_ref, b_ref, *rest):
        ex_refs, out_refs = rest[:n_ex], rest[n_ex:]
        if trans_a:
            acc = lax.dot_general(a_ref[...], b_ref[...], (((0,), (0,)), ((), ())),
                                  preferred_element_type=F32)
        elif trans_b and b_blocks:
            acc = jnp.zeros((tm, tn), F32)
            for d in range(NDEV):
                acc = acc + lax.dot_general(a_ref[:, d * shard:(d + 1) * shard], b_ref[d],
                                            (((1,), (1,)), ((), ())), preferred_element_type=F32)
        elif trans_b:
            acc = lax.dot_general(a_ref[...], b_ref[...], (((1,), (1,)), ((), ())),
                                  preferred_element_type=F32)
        else:
            acc = jnp.dot(a_ref[...], b_ref[...], preferred_element_type=F32)
        outs = (acc,) if epilogue is None else epilogue(acc, *[ex_refs[k][...] for k in used])
        for o_ref, val in zip(out_refs, outs):
            o_ref[...] = val.astype(o_ref.dtype)

    outs = pl.pallas_call(
        body, name=name, grid=(m // tm, ncols // tn),
        in_specs=[a_spec, b_spec] + ex_specs,
        out_specs=[out_spec for _ in range(n_out)],
        out_shape=[jax.ShapeDtypeStruct(out_dims, dt) for dt in out_dtypes],
        compiler_params=_cparams(("parallel", "parallel")),
    )(a, b, *[arr for arr, _ in extras])
    return list(outs)


TR = 512

ROW_SPEC = pl.BlockSpec((TR, D), lambda i: (i, 0))
VEC_SPEC = pl.BlockSpec((1, D), lambda i: (0, 0))


def _residual_then_norm(acc, xr, gate, g, sc, sh):
    x_new = xr + gate * acc
    rstd = lax.rsqrt(jnp.mean(x_new * x_new, axis=-1, keepdims=True) + EPS)
    return acc, x_new, ((x_new * rstd) * g) * (1.0 + sc) + sh


def normmod_fwd(x, g, sc, sh, name):
    def body(x_ref, g_ref, sc_ref, sh_ref, o_ref):
        xv = x_ref[...]
        rstd = lax.rsqrt(jnp.mean(xv * xv, axis=-1, keepdims=True) + EPS)
        n = (xv * rstd) * g_ref[...]
        o_ref[...] = (n * (1.0 + sc_ref[...]) + sh_ref[...]).astype(o_ref.dtype)

    return pl.pallas_call(
        body, name=name, grid=(S // TR,),
        in_specs=[ROW_SPEC, VEC_SPEC, VEC_SPEC, VEC_SPEC], out_specs=ROW_SPEC,
        out_shape=jax.ShapeDtypeStruct((S, D), BF16),
        compiler_params=_cparams(("parallel",)),
    )(x, g, sc, sh)


def _gate_next(dxv, refs):
    br_ref, gate_ref, dbr_ref, dgate_ref = refs

    @pl.when(pl.program_id(0) == 0)
    def _():
        dgate_ref[...] = jnp.zeros_like(dgate_ref)

    dbr_ref[...] = (dxv * gate_ref[...]).astype(dbr_ref.dtype)
    dgate_ref[...] += jnp.sum(dxv * br_ref[...], axis=0, keepdims=True)


GATE_NEXT_IN = [ROW_SPEC, VEC_SPEC]
GATE_NEXT_OUT = [ROW_SPEC, VEC_SPEC]
GATE_NEXT_SHAPES = [jax.ShapeDtypeStruct((S, D), BF16), jax.ShapeDtypeStruct((1, D), F32)]


def normmod_bwd(x, dh, dres, g, sc, name, gate_next=None):
    nxt = 2 if gate_next else 0

    def body(x_ref, dh_ref, dres_ref, g_ref, sc_ref, *rest):
        nxt_in, (dx_ref, dsc_ref, dsh_ref, dg_ref), nxt_out = rest[:nxt], rest[nxt:nxt + 4], rest[nxt + 4:]

        @pl.when(pl.program_id(0) == 0)
        def _():
            dsc_ref[...] = jnp.zeros_like(dsc_ref)
            dsh_ref[...] = jnp.zeros_like(dsh_ref)
            dg_ref[...] = jnp.zeros_like(dg_ref)

        xv, dh = x_ref[...], dh_ref[...]
        gv = g_ref[...]
        rstd = lax.rsqrt(jnp.mean(xv * xv, axis=-1, keepdims=True) + EPS)
        xhat = xv * rstd
        dn = dh * (1.0 + sc_ref[...])
        dxhat = dn * gv
        dxv = dres_ref[...] + rstd * (dxhat - xhat * jnp.mean(dxhat * xhat, axis=-1, keepdims=True))
        dx_ref[...] = dxv
        dsc_ref[...] += jnp.sum(dh * (xhat * gv), axis=0, keepdims=True)
        dsh_ref[...] += jnp.sum(dh, axis=0, keepdims=True)
        dg_ref[...] += jnp.sum(dn * xhat, axis=0, keepdims=True)
        if gate_next:
            _gate_next(dxv, nxt_in + nxt_out)

    vec_out = jax.ShapeDtypeStruct((1, D), F32)
    on = bool(gate_next)
    return pl.pallas_call(
        body, name=name, grid=(S // TR,),
        in_specs=[ROW_SPEC, ROW_SPEC, ROW_SPEC, VEC_SPEC, VEC_SPEC] + GATE_NEXT_IN * on,
        out_specs=[ROW_SPEC, VEC_SPEC, VEC_SPEC, VEC_SPEC] + GATE_NEXT_OUT * on,
        out_shape=[jax.ShapeDtypeStruct((S, D), F32), vec_out, vec_out, vec_out] + GATE_NEXT_SHAPES * on,
        compiler_params=_cparams(("arbitrary",)),
    )(x, dh, dres, g, sc, *(gate_next or ()))


def loss_head(x, target, g, gate_next, name):
    def body(x_ref, t_ref, g_ref, br_ref, gate_ref, dx_ref, loss_ref, dg_ref, dbr_ref, dgate_ref):
        @pl.when(pl.program_id(0) == 0)
        def _():
            loss_ref[...] = jnp.zeros_like(loss_ref)
            dg_ref[...] = jnp.zeros_like(dg_ref)

        xv, gv = x_ref[...], g_ref[...]
        rstd = lax.rsqrt(jnp.mean(xv * xv, axis=-1, keepdims=True) + EPS)
        xhat = xv * rstd
        err = xhat * gv - t_ref[...]
        loss_ref[...] += jnp.sum(err * err) * (0.5 / D)
        dy = err * (1.0 / D)
        dg_ref[...] += jnp.sum(dy * xhat, axis=0, keepdims=True)
        dxhat = dy * gv
        dxv = rstd * (dxhat - xhat * jnp.mean(dxhat * xhat, axis=-1, keepdims=True))
        dx_ref[...] = dxv
        _gate_next(dxv, (br_ref, gate_ref, dbr_ref, dgate_ref))

    return pl.pallas_call(
        body, name=name, grid=(S // TR,),
        in_specs=[ROW_SPEC, ROW_SPEC, VEC_SPEC] + GATE_NEXT_IN,
        out_specs=[ROW_SPEC, VEC_SPEC, VEC_SPEC] + GATE_NEXT_OUT,
        out_shape=[jax.ShapeDtypeStruct((S, D), F32), jax.ShapeDtypeStruct((1, D), F32),
                   jax.ShapeDtypeStruct((1, D), F32)] + GATE_NEXT_SHAPES,
        compiler_params=_cparams(("arbitrary",)),
    )(x, target, g, *gate_next)


TQ = 512
RS = 128
NSUB = TQ // RS
TK = 128


def _dot_hilo(a, tri_twice):
    hi = a.astype(BF16)
    lo = (a - hi.astype(F32)).astype(BF16)
    return jnp.dot(jnp.concatenate([hi, lo], axis=1), tri_twice, preferred_element_type=F32)


def _log_stay(z):
    return -(jnp.maximum(z, 0.0) + jnp.log(1.0 + jnp.exp(-jnp.abs(z))))


def _tri_and_ones(kind):
    row = jnp.bitwise_and(lax.broadcasted_iota(jnp.int32, (2 * TK, 2 * TK), 0), TK - 1)
    col = lax.broadcasted_iota(jnp.int32, (2 * TK, 2 * TK), 1)
    tri = {"after": row > col, "upto": row <= col, "before": row < col}[kind]
    return jnp.logical_or(col >= TK, tri).astype(BF16)


NPAIR = NH // 2
SCALE = HD ** -0.5


def _pair_specs(first_block):
    rows = pl.BlockSpec((TQ, LANES), lambda p, i: (i, first_block + p))
    whole = pl.BlockSpec((S, LANES), lambda p, i: (0, first_block + p))
    return rows, whole


Q_ROWS_SPEC, _ = _pair_specs(0)
_, K_ALL_SPEC = _pair_specs(NPAIR)
_, V_ALL_SPEC = _pair_specs(2 * NPAIR)
PAIR_ROWS_SPEC = pl.BlockSpec((TQ, LANES), lambda p, i: (i, p))
PAIR_ALL_SPEC = pl.BlockSpec((S, LANES), lambda p, i: (0, p))
PAIR_TOTAL_SPEC = pl.BlockSpec((2, TQ, TK), lambda p, i: (p, i, 0))


def _head_halves(x):
    first = lax.broadcasted_iota(jnp.int32, x.shape, 1) < HD
    zero = jnp.zeros_like(x)
    return jnp.where(first, x, zero), jnp.where(first, zero, x)


def _join_heads(a, b):
    return jnp.where(lax.broadcasted_iota(jnp.int32, a.shape, 1) < HD, a, b)


def _comm_hooks(comm, refs, n_in, n_out, n_scratch):
    nc = len(comm.arrs) if comm is not None else 0
    ins, cin = refs[:n_in], refs[n_in:n_in + nc]
    outs = refs[n_in + nc:n_in + nc + n_out]
    cout = refs[n_in + nc + n_out:n_in + 2 * nc + n_out]
    scratch = refs[n_in + 2 * nc + n_out:n_in + 2 * nc + n_out + n_scratch]
    sems = refs[n_in + 2 * nc + n_out + n_scratch:]
    phases = comm.phases(cin, cout, sems) if comm is not None else None
    return ins, outs, scratch, phases


def _with_comm(comm, in_specs, out_specs, out_shape, operands, scratch):
    if comm is None:
        return dict(in_specs=in_specs, out_specs=out_specs, out_shape=out_shape, scratch_shapes=scratch), operands
    nc = len(comm.arrs)
    return dict(in_specs=in_specs + [HBM_SPEC] * nc, out_specs=out_specs + [HBM_SPEC] * nc,
                out_shape=out_shape + comm.out_shape, scratch_shapes=scratch + comm.scratch), operands + comm.arrs


def attn_fwd(qkv, name, comm=None):
    n_steps = S // TQ

    def body(*refs):
        (q_ref, k_ref, v_ref), (o_ref, r_ref), (acc_ref, z_even, z_odd, w_ref), phases = _comm_hooks(
            comm, refs, 3, 2, 4)
        p = pl.program_id(0)
        i = pl.program_id(1)
        if phases is not None:
            pl.when(jnp.logical_and(p == 0, i == 0))(phases[0])
            pl.when(jnp.logical_and(p == NPAIR - 1, i == n_steps - 2))(phases[1])
        chains = [(sub, h) for sub in range(NSUB) for h in range(2)]
        q_sub = [_head_halves(q_ref[pl.ds(sub * RS, RS), :] * SCALE) for sub in range(NSUB)]
        after = _tri_and_ones("after")
        below_diagonal = (lax.broadcasted_iota(jnp.int32, (RS, TK), 1)
                          < lax.broadcasted_iota(jnp.int32, (RS, TK), 0))
        base = i * NSUB
        all_subs = list(range(NSUB))

        acc_ref[...] = jnp.zeros_like(acc_ref)
        r_ref[...] = jnp.zeros_like(r_ref)
        w_ref[...] = jnp.zeros_like(w_ref)

        def key_rows(block):
            return pl.ds(pl.multiple_of(block * TK, TK), TK)

        def store_scores(z_ref, block, subs):
            kb = k_ref[key_rows(block), :]
            for c, (sub, h) in enumerate(chains):
                if sub in subs:
                    z_ref[c] = lax.dot_general(q_sub[sub][h], kb, (((1,), (1,)), ((), ())),
                                               preferred_element_type=F32)

        def add_weighted_values(block, subs):
            vb = v_ref[key_rows(block), :]
            for sub in subs:
                acc_ref[pl.ds(sub * RS, RS), :] += _join_heads(*[
                    jnp.dot(w_ref[2 * sub + h], vb, preferred_element_type=F32) for h in range(2)])

        def step(block, z_ref, z_next_ref, subs, diagonal_sub, prev_subs, next_subs):
            if prev_subs:
                add_weighted_values(block + 1, prev_subs)
            if next_subs:
                store_scores(z_next_ref, jnp.maximum(block - 1, 0), next_subs)
            active = [(c, sub, h) for c, (sub, h) in enumerate(chains) if sub in subs]
            ls, sums = {}, {}
            for c, sub, h in active:
                ls[c] = _log_stay(z_ref[c])
                sums[c] = _dot_hilo(jnp.where(below_diagonal, ls[c], 0.0) if sub == diagonal_sub else ls[c], after)
            for c, sub, h in active:
                rows = pl.ds(sub * RS, RS)
                later = r_ref[h, rows, :]
                w = jnp.exp(z_ref[c] + ls[c] + (sums[c][:, :TK] + later))
                if sub == diagonal_sub:
                    w = jnp.where(below_diagonal, w, 0.0)
                w_ref[c] = w.astype(BF16)
                r_ref[h, rows, :] = later + sums[c][:, TK:]

        store_scores(z_even, base + NSUB - 1, [NSUB - 1])
        buffers = (z_even, z_odd)
        for j in reversed(range(NSUB)):
            subs = all_subs[j:]
            step(base + j, buffers[0], buffers[1], subs, j, all_subs[j + 1:], all_subs[j - 1:] if j else all_subs)
            buffers = buffers[::-1]
        assert buffers[0] is z_even

        @pl.loop(0, base // 2)
        def _(pair):
            block = base - 1 - 2 * pair
            step(block, z_even, z_odd, all_subs, None, all_subs, all_subs)
            step(block - 1, z_odd, z_even, all_subs, None, all_subs, all_subs)

        add_weighted_values(0, all_subs)
        o_ref[...] = acc_ref[...].astype(o_ref.dtype)
        if phases is not None:
            pl.when(jnp.logical_and(p == NPAIR - 1, i == n_steps - 1))(phases[2])

    kwargs, operands = _with_comm(
        comm, [Q_ROWS_SPEC, K_ALL_SPEC, V_ALL_SPEC], [PAIR_ROWS_SPEC, PAIR_TOTAL_SPEC],
        [jax.ShapeDtypeStruct((S, NH * HD), BF16), jax.ShapeDtypeStruct((NH, S, TK), F32)], [qkv, qkv, qkv],
        [pltpu.VMEM((TQ, LANES), F32), pltpu.VMEM((2 * NSUB, RS, TK), F32), pltpu.VMEM((2 * NSUB, RS, TK), F32),
         pltpu.VMEM((2 * NSUB, RS, TK), BF16)])
    return pl.pallas_call(
        body, name=name, grid=(NPAIR, n_steps),
        compiler_params=_cparams(("arbitrary", "arbitrary")), **kwargs,
    )(*operands)


def attn_bwd(qkv, dout, totals, name, comm=None):
    n_steps = S // TQ

    def body(*refs):
        ((q_ref, k_ref, v_ref, do_ref, r_ref), (dq_ref, dk_ref, dv_ref),
         (z_even, z_odd, dw_even, dw_odd, before_ref, dbefore_ref, dz_ref, w_ref), phases) = _comm_hooks(
            comm, refs, 5, 3, 8)
        p = pl.program_id(0)
        i = pl.program_id(1)
        if phases is not None:
            pl.when(jnp.logical_and(p == 0, i == 0))(phases[0])
            pl.when(jnp.logical_and(p == NPAIR - 1, i == n_steps - 2))(phases[1])

        @pl.when(i == 0)
        def _():
            dk_ref[...] = jnp.zeros_like(dk_ref)
            dv_ref[...] = jnp.zeros_like(dv_ref)

        chains = [(sub, h) for sub in range(NSUB) for h in range(2)]
        nch = len(chains)
        qb = q_ref[...]
        dob = do_ref[...].astype(BF16)
        q_sub = [_head_halves(qb[sub * RS:(sub + 1) * RS] * SCALE) for sub in range(NSUB)]
        do_sub = [_head_halves(dob[sub * RS:(sub + 1) * RS]) for sub in range(NSUB)]
        upto = _tri_and_ones("upto")
        before_tri = _tri_and_ones("before")
        below_diagonal = (lax.broadcasted_iota(jnp.int32, (RS, TK), 1)
                          < lax.broadcasted_iota(jnp.int32, (RS, TK), 0))
        contract_lanes = (((1,), (1,)), ((), ()))
        contract_rows = (((0,), (0,)), ((), ()))
        base = i * NSUB
        all_subs = list(range(NSUB))

        def key_rows(block):
            return pl.ds(pl.multiple_of(block * TK, TK), TK)

        def store_products(bufs, block, subs):
            z_ref, dw_ref = bufs
            kb = k_ref[key_rows(block), :]
            vb = v_ref[key_rows(block), :]
            for c, (sub, h) in enumerate(chains):
                if sub in subs:
                    z_ref[c] = lax.dot_general(q_sub[sub][h], kb, contract_lanes, preferred_element_type=F32)
                    dw_ref[c] = lax.dot_general(do_sub[sub][h], vb, contract_lanes, preferred_element_type=F32)

        def add_gradients(block, subs):
            kb = k_ref[key_rows(block), :]
            for sub in subs:
                rows = pl.ds(sub * RS, RS)
                dq_ref[rows, :] += _join_heads(*[jnp.dot(dz_ref[h, rows, :], kb, preferred_element_type=F32)
                                                 for h in range(2)])
            dk_ref[key_rows(block), :] += _join_heads(*[
                lax.dot_general(dz_ref[h], qb, contract_rows, preferred_element_type=F32) for h in range(2)])
            dv_ref[key_rows(block), :] += _join_heads(*[
                lax.dot_general(w_ref[h], dob, contract_rows, preferred_element_type=F32) for h in range(2)])

        for ref in (dq_ref, before_ref, dbefore_ref, dz_ref, w_ref):
            ref[...] = jnp.zeros_like(ref)
        even, odd = (z_even, dw_even), (z_odd, dw_odd)
        store_products(even, 0, all_subs)

        def step(block, bufs, next_bufs, subs, diagonal_sub, prev_subs, next_subs):
            z_ref, dw_ref = bufs
            add_gradients(jnp.maximum(block - 1, 0), prev_subs)
            for sub in prev_subs:
                if sub not in subs:
                    dz_ref[:, pl.ds(sub * RS, RS), :] = jnp.zeros((2, RS, TK), BF16)
                    w_ref[:, pl.ds(sub * RS, RS), :] = jnp.zeros((2, RS, TK), BF16)
            if next_subs:
                store_products(next_bufs, block + 1, next_subs)
            active = [(c, sub, h) for c, (sub, h) in enumerate(chains) if sub in subs]
            ls, sums, dl, dsums = {}, {}, {}, {}
            for c, sub, h in active:
                ls[c] = _log_stay(z_ref[c])
                sums[c] = _dot_hilo(jnp.where(below_diagonal, ls[c], 0.0) if sub == diagonal_sub else ls[c], upto)
            for c, sub, h in active:
                rows = pl.ds(sub * RS, RS)
                before = before_ref[c]
                log_after = r_ref[h, rows, :] - (sums[c][:, :TK] + before)
                w = jnp.exp((z_ref[c] + ls[c]) + log_after)
                if sub == diagonal_sub:
                    w = jnp.where(below_diagonal, w, 0.0)
                dl[c] = dw_ref[c] * w
                dsums[c] = _dot_hilo(dl[c], before_tri)
                w_ref[h, rows, :] = w.astype(BF16)
                before_ref[c] = before + sums[c][:, TK:]
            for c, sub, h in active:
                rows = pl.ds(sub * RS, RS)
                dbefore = dbefore_ref[c]
                beta = jnp.exp(z_ref[c] + ls[c])
                if sub == diagonal_sub:
                    beta = jnp.where(below_diagonal, beta, 0.0)
                dstay = dsums[c][:, :TK] + dbefore
                dz_ref[h, rows, :] = ((dl[c] * (1.0 - beta) - beta * dstay) * SCALE).astype(BF16)
                dbefore_ref[c] = dbefore + dsums[c][:, TK:]

        @pl.loop(0, base // 2)
        def _(pair):
            step(2 * pair, even, odd, all_subs, None, all_subs, all_subs)
            step(2 * pair + 1, odd, even, all_subs, None, all_subs, all_subs)

        bufs = (even, odd)
        for j in range(NSUB):
            step(base + j, bufs[0], bufs[1], all_subs[j:], j, all_subs[j - 1:] if j else all_subs, all_subs[j + 1:])
            bufs = bufs[::-1]

        add_gradients(base + NSUB - 1, all_subs[NSUB - 1:])
        if phases is not None:
            pl.when(jnp.logical_and(p == NPAIR - 1, i == n_steps - 1))(phases[2])

    full = jax.ShapeDtypeStruct((S, NH * HD), F32)
    kwargs, operands = _with_comm(
        comm, [Q_ROWS_SPEC, K_ALL_SPEC, V_ALL_SPEC, PAIR_ROWS_SPEC, PAIR_TOTAL_SPEC],
        [PAIR_ROWS_SPEC, PAIR_ALL_SPEC, PAIR_ALL_SPEC], [full, full, full], [qkv, qkv, qkv, dout, totals],
        [pltpu.VMEM((2 * NSUB, RS, TK), F32)] * 6 + [pltpu.VMEM((2, TQ, TK), BF16)] * 2)
    return pl.pallas_call(
        body, name=name, grid=(NPAIR, n_steps),
        compiler_params=_cparams(("arbitrary", "arbitrary")), **kwargs,
    )(*operands)


def _proj_cols(first_col):
    base = first_col // LANES
    return pl.BlockSpec((S, LANES), lambda j: (0, base + j))


CONV_OUT_SPEC = pl.BlockSpec((S, LANES), lambda j: (0, j))
CONV_DOUT_SPEC = pl.BlockSpec((S, LANES), lambda j: (0, (NH * HD) // LANES + j))
CONV_W_SPEC = pl.BlockSpec((8, LANES), lambda j: (0, j))
CONV_B_SPEC = pl.BlockSpec((1, LANES), lambda j: (0, j))


def _shift_down(u, n):
    rows = lax.broadcasted_iota(jnp.int32, u.shape, 0)
    return jnp.where(rows >= n, pltpu.roll(u, n, 0), 0.0)


def _shift_up(u, n):
    rows = lax.broadcasted_iota(jnp.int32, u.shape, 0)
    return jnp.where(rows < S - n, pltpu.roll(u, S - n, 0), 0.0)


def conv_fwd(proj, cw8, cb, name):
    def body(bg_ref, cg_ref, hc_ref, w_ref, b_ref, o_ref):
        u = cg_ref[...] * hc_ref[...]
        w = w_ref[...]
        y = w[0:1, :] * _shift_down(u, 2) + w[1:2, :] * _shift_down(u, 1) + w[2:3, :] * u + b_ref[...]
        o_ref[...] = bg_ref[...] * y

    return pl.pallas_call(
        body, name=name, grid=(CW // LANES,),
        in_specs=[_proj_cols(0), _proj_cols(CW), _proj_cols(2 * CW), CONV_W_SPEC, CONV_B_SPEC],
        out_specs=CONV_OUT_SPEC, out_shape=jax.ShapeDtypeStruct((S, CW), F32),
        compiler_params=_cparams(("parallel",)),
    )(proj, proj, proj, cw8, cb)


def conv_bwd(proj, dout, cw8, cb, name):
    def body(bg_ref, cg_ref, hc_ref, do_ref, w_ref, b_ref, dbg_ref, dcg_ref, dhc_ref, dw_ref, db_ref):
        cg, hc, do = cg_ref[...], hc_ref[...], do_ref[...]
        w = w_ref[...]
        u = cg * hc
        u1, u2 = _shift_down(u, 1), _shift_down(u, 2)
        y = w[0:1, :] * u2 + w[1:2, :] * u1 + w[2:3, :] * u + b_ref[...]
        dbg_ref[...] = do * y
        dy = do * bg_ref[...]
        db_ref[...] = jnp.sum(dy, axis=0, keepdims=True)
        dw_ref[...] = jnp.concatenate(
            [jnp.sum(dy * u2, axis=0, keepdims=True), jnp.sum(dy * u1, axis=0, keepdims=True),
             jnp.sum(dy * u, axis=0, keepdims=True), jnp.zeros((5, LANES), F32)], axis=0)
        du = w[2:3, :] * dy + w[1:2, :] * _shift_up(dy, 1) + w[0:1, :] * _shift_up(dy, 2)
        dcg_ref[...] = du * hc
        dhc_ref[...] = du * cg

    full = jax.ShapeDtypeStruct((S, CW), F32)
    return pl.pallas_call(
        body, name=name, grid=(CW // LANES,),
        in_specs=[_proj_cols(0), _proj_cols(CW), _proj_cols(2 * CW), CONV_DOUT_SPEC, CONV_W_SPEC, CONV_B_SPEC],
        out_specs=[CONV_OUT_SPEC, CONV_OUT_SPEC, CONV_OUT_SPEC, CONV_W_SPEC, CONV_B_SPEC],
        out_shape=[full, full, full, jax.ShapeDtypeStruct((8, CW), F32), jax.ShapeDtypeStruct((1, CW), F32)],
        compiler_params=_cparams(("parallel",)),
    )(proj, proj, proj, dout, cw8, cb)


GELU_K = math.sqrt(2.0 / math.pi)
GELU_C = 0.044715


def _gelu(x):
    return 0.5 * x * (1.0 + jnp.tanh(GELU_K * (x + GELU_C * (x * x * x))))


def _gelu_grad(x):
    t = jnp.tanh(GELU_K * (x + GELU_C * (x * x * x)))
    return 0.5 * (1.0 + t) + 0.5 * x * (1.0 - t * t) * (GELU_K * (1.0 + 3.0 * GELU_C * (x * x)))


def _sg_masks():
    row = lax.broadcasted_iota(jnp.int32, (T, T), 0)
    col = lax.broadcasted_iota(jnp.int32, (T, T), 1)
    causal = jnp.right_shift(row, 6) >= jnp.right_shift(col, 6)
    head_of_col = jnp.right_shift(lax.broadcasted_iota(jnp.int32, (T, CW), 1), 6)
    return causal, head_of_col


def _sg_mixed(vnb, sw_ref, bias, causal, head_of_col):
    mixed = bias
    for h in range(SG_HEADS):
        wh = jnp.where(causal, sw_ref[h], 0.0).astype(BF16)
        mh = jnp.dot(wh, vnb, preferred_element_type=F32)
        mixed = mixed + jnp.where(head_of_col == h, mh, 0.0)
    return mixed


SG_U_SPEC = pl.BlockSpec((T, CW), lambda n: (n, 3))
SG_V_SPEC = pl.BlockSpec((T, CW), lambda n: (n, 4))
SG_ROW_SPEC = pl.BlockSpec((T, CW), lambda n: (n, 0))
SG_DOUT_SPEC = pl.BlockSpec((T, CW), lambda n: (n, 3))
SG_G_SPEC = pl.BlockSpec((1, CW), lambda n: (0, 0))
SG_W_SPEC = pl.BlockSpec((SG_HEADS, T, T), lambda n: (0, 0, 0))
SG_BIAS_SPEC = pl.BlockSpec((T, CW), lambda n: (0, 0))


def sg_fwd(proj, gn, sw, bias, name):
    def body(u_ref, v_ref, g_ref, sw_ref, bias_ref, o_ref):
        causal, head_of_col = _sg_masks()
        gv = _gelu(v_ref[...])
        rstd = lax.rsqrt(jnp.mean(gv * gv, axis=-1, keepdims=True) + EPS)
        vnb = ((gv * rstd) * g_ref[...]).astype(BF16)
        mixed = _sg_mixed(vnb, sw_ref, bias_ref[...], causal, head_of_col)
        o_ref[...] = _gelu(u_ref[...]) * mixed

    return pl.pallas_call(
        body, name=name, grid=(S // T,),
        in_specs=[SG_U_SPEC, SG_V_SPEC, SG_G_SPEC, SG_W_SPEC, SG_BIAS_SPEC],
        out_specs=SG_ROW_SPEC, out_shape=jax.ShapeDtypeStruct((S, CW), F32),
        compiler_params=_cparams(("parallel",)),
    )(proj, proj, gn, sw, bias)


def sg_bwd(proj, dout, gn, sw, bias, name):
    def body(u_ref, v_ref, do_ref, g_ref, sw_ref, bias_ref, du_ref, dv_ref, dg_ref, dsw_ref, dbias_ref):
        @pl.when(pl.program_id(0) == 0)
        def _():
            dg_ref[...] = jnp.zeros_like(dg_ref)
            dsw_ref[...] = jnp.zeros_like(dsw_ref)
            dbias_ref[...] = jnp.zeros_like(dbias_ref)

        causal, head_of_col = _sg_masks()
        uv, vv, do, gnv = u_ref[...], v_ref[...], do_ref[...], g_ref[...]
        gv = _gelu(vv)
        rstd = lax.rsqrt(jnp.mean(gv * gv, axis=-1, keepdims=True) + EPS)
        xhat = gv * rstd
        vnb = (xhat * gnv).astype(BF16)
        mixed = _sg_mixed(vnb, sw_ref, bias_ref[...], causal, head_of_col)
        du_ref[...] = (do * mixed) * _gelu_grad(uv)
        dmix = do * _gelu(uv)
        dbias_ref[...] += dmix
        dmixb = dmix.astype(BF16)
        dvn = jnp.zeros((T, CW), F32)
        for h in range(SG_HEADS):
            wh = jnp.where(causal, sw_ref[h], 0.0).astype(BF16)
            dvh = lax.dot_general(wh, dmixb, (((0,), (0,)), ((), ())), preferred_element_type=F32)
            dvn = dvn + jnp.where(head_of_col == h, dvh, 0.0)
            dmh = jnp.where(head_of_col == h, dmixb, jnp.zeros_like(dmixb))
            dwh = lax.dot_general(dmh, vnb, (((1,), (1,)), ((), ())), preferred_element_type=F32)
            dsw_ref[h] += jnp.where(causal, dwh, 0.0)
        dg_ref[...] += jnp.sum(dvn * xhat, axis=0, keepdims=True)
        dxhat = dvn * gnv
        dgv = rstd * (dxhat - xhat * jnp.mean(dxhat * xhat, axis=-1, keepdims=True))
        dv_ref[...] = dgv * _gelu_grad(vv)

    full = jax.ShapeDtypeStruct((S, CW), F32)
    return pl.pallas_call(
        body, name=name, grid=(S // T,),
        in_specs=[SG_U_SPEC, SG_V_SPEC, SG_DOUT_SPEC, SG_G_SPEC, SG_W_SPEC, SG_BIAS_SPEC],
        out_specs=[SG_ROW_SPEC, SG_ROW_SPEC, SG_G_SPEC, SG_W_SPEC, SG_BIAS_SPEC],
        out_shape=[full, full, jax.ShapeDtypeStruct((1, CW), F32),
                   jax.ShapeDtypeStruct((SG_HEADS, T, T), F32), jax.ShapeDtypeStruct((T, CW), F32)],
        compiler_params=_cparams(("arbitrary",)),
    )(proj, proj, dout, gn, sw, bias)


ADA_COLS = NMOD * D // NDEV


def ada_fwd(c_all, ada_w, ada_b_mine, name):
    def body(c_ref, w_ref, b_ref, o_ref, ca_ref):
        cv = c_ref[...]
        ca = cv * (1.0 / (1.0 + jnp.exp(-cv)))
        ca_ref[...] = ca
        cab = ca.astype(BF16)
        for l in range(L):
            o_ref[l] = jnp.dot(cab, w_ref[l].astype(BF16), preferred_element_type=F32) + b_ref[l]

    return pl.pallas_call(
        body, name=name,
        out_shape=[jax.ShapeDtypeStruct((L, NDEV, ADA_COLS), F32), jax.ShapeDtypeStruct((NDEV, D), F32)],
        compiler_params=_cparams(),
    )(c_all, ada_w, ada_b_mine)


def ada_bwd(ca, dmod_cols, name):
    def body(ca_ref, dm_ref, o_ref):
        cab = ca_ref[...].astype(BF16)
        for l in range(L):
            o_ref[l] = lax.dot_general(cab, dm_ref[l].astype(BF16), (((0,), (0,)), ((), ())),
                                       preferred_element_type=F32)

    return pl.pallas_call(
        body, name=name, out_shape=jax.ShapeDtypeStruct((L, D, ADA_COLS), F32),
        compiler_params=_cparams(),
    )(ca, dmod_cols)


def _adamw(w, g, m, v):
    m = B1 * m + (1.0 - B1) * g
    v = B2 * v + (1.0 - B2) * (g * g)
    m_hat = m / BC1
    v_hat = v / BC2
    delta = -LR * (m_hat / (jnp.sqrt(v_hat) + AEPS) + WD * w)
    return delta, m, v


VEC_ROWS_PER_LAYER = 8
VEC_FINAL_ROW = L * VEC_ROWS_PER_LAYER
VEC_ROWS = VEC_FINAL_ROW + 8
W256_TAPS, W256_CONV_B, W256_GN = 0, 8, 9
W256_ROWS_PER_LAYER = 16


def small_update(vec_all, w256_all, sb_all, sw_all, params, name):
    n_par = len(params)

    def body(*refs):
        vec_ref, w256_ref, sb_ref = refs[:3]
        sw_refs = refs[3:3 + L]
        par_refs = [refs[3 + L + 3 * k:3 + L + 3 * k + 3] for k in range(n_par)]
        out = refs[3 + L + 3 * n_par:]
        out_par = [out[4 * k:4 * k + 4] for k in range(n_par)]
        loss_ref, taps_ref = out[4 * n_par:]

        def total(ref, idx):
            acc = ref[(0,) + idx].astype(F32)
            for d in range(1, NDEV):
                acc = acc + ref[(d,) + idx].astype(F32)
            return acc

        def update(k, region, g):
            w_ref, m_ref, v_ref = par_refs[k]
            g_ref, d_ref, nm_ref, nv_ref = out_par[k]
            delta, nm, nv = _adamw(w_ref[region], g, m_ref[region], v_ref[region])
            g_ref[region] = g
            d_ref[region] = delta
            nm_ref[region] = nm
            nv_ref[region] = nv

        for l in range(L):
            base = l * VEC_ROWS_PER_LAYER
            for k in range(NMOD):
                update(0, (slice(l, l + 1), slice(k * D, (k + 1) * D)), total(vec_ref, (slice(base + k, base + k + 1),)))
            update(1, (slice(l, l + 1),), total(vec_ref, (slice(base + 6, base + 7),)))
            update(2, (slice(l, l + 1),), total(vec_ref, (slice(base + 7, base + 8),)))
            wbase = l * W256_ROWS_PER_LAYER
            update(4, (slice(l, l + 1),), total(w256_ref, (slice(wbase + W256_CONV_B, wbase + W256_CONV_B + 1),)))
            update(5, (slice(l, l + 1),), total(w256_ref, (slice(wbase + W256_GN, wbase + W256_GN + 1),)))
            update(6, (l,), total(sw_refs[l], ()))
            update(7, (l,), total(sb_ref, (slice(l * SG_HEADS, (l + 1) * SG_HEADS),)))
            taps_ref[l] = total(w256_ref, (slice(wbase + W256_TAPS, wbase + W256_TAPS + 8),))
        update(3, (slice(0, 1),), total(vec_ref, (slice(VEC_FINAL_ROW, VEC_FINAL_ROW + 1),)))
        loss_ref[...] = total(vec_ref, (slice(VEC_FINAL_ROW + 1, VEC_FINAL_ROW + 2), slice(0, LANES)))

    out_shape = []
    for w, _, _ in params:
        out_shape += [jax.ShapeDtypeStruct(w.shape, F32)] * 4
    out_shape += [jax.ShapeDtypeStruct((1, LANES), F32), jax.ShapeDtypeStruct((L, 8, CW), F32)]
    outs = pl.pallas_call(body, name=name, out_shape=out_shape, compiler_params=_cparams())(
        vec_all, w256_all, sb_all, *sw_all, *[a for p in params for a in p])
    return [outs[4 * k:4 * k + 4] for k in range(n_par)], outs[4 * n_par:]


def adamw_plain(w, g, m, v, tr, name):
    rows, cols = w.shape
    spec = pl.BlockSpec((tr, cols), lambda i: (i, 0))

    def body(w_ref, g_ref, m_ref, v_ref, d_ref, nm_ref, nv_ref):
        delta, nm, nv = _adamw(w_ref[...], g_ref[...], m_ref[...], v_ref[...])
        d_ref[...] = delta
        nm_ref[...] = nm
        nv_ref[...] = nv

    shp = jax.ShapeDtypeStruct((rows, cols), F32)
    return pl.pallas_call(
        body, name=name, grid=(rows // tr,), in_specs=[spec] * 4, out_specs=[spec] * 3,
        out_shape=[shp, shp, shp], compiler_params=_cparams(("parallel",)),
    )(w, g, m, v)


def adamw_reduce(w, parts, m, v, tr, name, tie=None):
    _, rows, cols = w.shape
    spec = pl.BlockSpec((None, tr, cols), lambda l, i: (l, i, 0))
    pspecs = [pl.BlockSpec((NDEV, tr, cols), lambda l, i, k=k: (0, jnp.where(l == k, i, 0), 0)) for k in range(L)]

    ties = [] if tie is None else [tie]

    def body(w_ref, p0_ref, p1_ref, m_ref, v_ref, *rest):
        g_ref, d_ref, nm_ref, nv_ref = rest[len(ties):]
        first_layer = pl.program_id(0) == 0
        g = jnp.zeros((tr, cols), F32)
        for d in range(NDEV):
            g = g + jnp.where(first_layer, p0_ref[d], p1_ref[d]).astype(F32)
        delta, nm, nv = _adamw(w_ref[...], g, m_ref[...], v_ref[...])
        g_ref[...] = g
        d_ref[...] = delta
        nm_ref[...] = nm
        nv_ref[...] = nv

    shp = jax.ShapeDtypeStruct(w.shape, F32)
    return pl.pallas_call(
        body, name=name, grid=(L, rows // tr),
        in_specs=[spec] + pspecs + [spec, spec] + [pl.BlockSpec(t.shape, lambda l, i: (0, 0)) for t in ties],
        out_specs=[spec] * 4, out_shape=[shp] * 4, compiler_params=_cparams(("parallel", "parallel")),
    )(w, *parts, m, v, *ties)


SHARD_IN = PROJ // NDEV


def shards_to_columns(shards, name):
    tr = 256

    def body(i_ref, o_ref):
        for d in range(NDEV):
            o_ref[:, d * SHARD_IN:(d + 1) * SHARD_IN] = i_ref[d]

    return pl.pallas_call(
        body, name=name, grid=(D // tr,),
        in_specs=[pl.BlockSpec((NDEV, tr, SHARD_IN), lambda i: (0, i, 0))],
        out_specs=pl.BlockSpec((tr, PROJ), lambda i: (i, 0)),
        out_shape=jax.ShapeDtypeStruct((D, PROJ), shards.dtype), compiler_params=_cparams(("parallel",)),
    )(shards)


def columns_to_shards(mat, name):
    tr = 256

    def body(i_ref, o_ref):
        for d in range(NDEV):
            o_ref[d] = i_ref[:, d * SHARD_IN:(d + 1) * SHARD_IN]

    return pl.pallas_call(
        body, name=name, grid=(D // tr,),
        in_specs=[pl.BlockSpec((tr, PROJ), lambda i: (i, 0))],
        out_specs=pl.BlockSpec((NDEV, tr, SHARD_IN), lambda i: (0, i, 0)),
        out_shape=jax.ShapeDtypeStruct((NDEV, D, SHARD_IN), mat.dtype), compiler_params=_cparams(("parallel",)),
    )(mat)


def _pad_rows(flat, rows):
    return jnp.pad(flat, (0, rows * LANES - flat.shape[0])).reshape(rows, LANES)


def kernel(x, c, ada_w, ada_b, norm_mix_g, norm_mlp_g, w_in, conv_w, conv_b, gmlp_norm_g, spatial_w, spatial_b, w_out, mlp_w1, mlp_w2, final_norm_g, loss_target, m_ada_w, m_ada_b, m_norm_mix_g, m_norm_mlp_g, m_w_in, m_conv_w, m_conv_b, m_gmlp_norm_g, m_spatial_w, m_spatial_b, m_w_out, m_mlp_w1, m_mlp_w2, m_final_norm_g, v_ada_w, v_ada_b, v_norm_mix_g, v_norm_mlp_g, v_w_in, v_conv_w, v_conv_b, v_gmlp_norm_g, v_spatial_w, v_spatial_b, v_w_out, v_mlp_w1, v_mlp_w2, v_final_norm_g):
    me = _lin(_my_pos())
    x0 = x[0]
    target = loss_target[0]
    conv_shard = conv_w.shape[-1]

    w_in_b, w_out_b, w1_b, w2_b = [w.astype(BF16) for w in (w_in, w_out, mlp_w1, mlp_w2)]
    pack0 = _pad_rows(jnp.concatenate([c.reshape(-1), conv_w.reshape(-1)]), 16)
    g0, gw_in0 = run_comm(Gather([pack0, w_in_b[0]]), "gather_first")
    g0 = g0.reshape(NDEV, 16 * LANES)
    c_all = g0[:, :D]
    conv_full = (g0[:, D:D + L * 3 * conv_shard].reshape(NDEV, L, 3, conv_shard)
                 .transpose(1, 2, 0, 3).reshape(L, 3, CW))


    W_in = [shards_to_columns(gw_in0, "w_in_columns0"), None]
    W_out, W1, W2 = [None] * L, [None] * L, [None] * L

    ada_b_mine = lax.dynamic_slice(ada_b, (0, me * ADA_COLS), (L, ADA_COLS)).reshape(L, 1, ADA_COLS)
    mod_part, c_act = ada_fwd(c_all, ada_w, ada_b_mine, "ada_fwd")
    gmod = run_comm(Gather([mod_part]), "gather_mod")[0]
    mod = lax.dynamic_index_in_dim(gmod, me, axis=2, keepdims=False)
    mod = mod.transpose(1, 0, 2).reshape(L, NMOD, 1, D)
    early_weights, token = start_copies([w_out_b[0]], me, "gather_early0_start", True, after=gmod)
    mod = tied(mod, token)

    cw8 = jnp.pad(conv_full, ((0, 0), (0, 5), (0, 0)))
    sg_bias = jnp.repeat(spatial_b.transpose(0, 2, 1), HD, axis=2)

    saved = []
    xl = x0
    for l in range(L):
        sh_m, sc_m, g_m, sh_f, sc_f, g_f = [mod[l, k] for k in range(NMOD)]
        h1 = normmod_fwd(xl, norm_mix_g[l:l + 1], sc_m, sh_m, f"norm_mix_fwd{l}")
        if l > 0:
            W_in[l] = shards_to_columns(finish_copies(w_in_handle, xl, f"gather_w_in{l}_wait")[0],
                                        f"w_in_columns{l}")
        qkv = mm_layer("proj_qkv", l, h1, W_in[l], out_dtypes=[BF16], cols=(0, QKV))[0]
        proj = mm_layer("proj_rest", l, h1, W_in[l], out_dtypes=[F32], cols=(QKV, REST))[0]
        riders = [w2_b[l]] if l > 0 else [w2_b[l], w1_b[l]]
        a_out, a_tot, gw2, *rode = attn_fwd(qkv, f"attn_fwd{l}", comm=Gather(riders))
        gw_out, gw1 = (finish_copies(early_weights, a_out, f"gather_early{l}_wait") + rode)[:2]
        W_out[l] = gw_out.reshape(D, D)
        W1[l] = gw1
        W2[l] = gw2.reshape(DFF, D)
        if l + 1 < L:
            w_in_handle, token = start_copies([w_in_b[l + 1]], me, f"gather_w_in{l + 1}_start", True, after=a_out)
            early_weights, token = start_copies([w_out_b[l + 1], w1_b[l + 1]], me, f"gather_early{l + 1}_start", True,
                                                after=token)
            g_m = tied(g_m, token)
        c_out = conv_fwd(proj, cw8[l], conv_b[l:l + 1], f"conv_fwd{l}")
        s_out = sg_fwd(proj, gmlp_norm_g[l:l + 1], spatial_w[l], sg_bias[l], f"sg_fwd{l}")
        cat = jnp.concatenate([a_out, c_out.astype(BF16), s_out.astype(BF16)], axis=1)
        mix, x1, h2 = mm_layer("mix", l, cat, W_out[l], out_dtypes=[F32, F32, BF16], epilogue=_residual_then_norm,
                               extras=[(xl, "tile"), (g_m, "col"), (norm_mlp_g[l:l + 1], "col"), (sc_f, "col"),
                                       (sh_f, "col")])
        ra, r = mm_layer("mlp_up", l, h2, W1[l], out_dtypes=[BF16, BF16], b_blocks=True,
                         epilogue=lambda acc: (jnp.maximum(acc, 0.0), jnp.square(jnp.maximum(acc, 0.0))))
        m2, x2 = mm_layer("mlp_down", l, r, W2[l], out_dtypes=[F32, F32],
                          epilogue=lambda acc, xr, g: (acc, xr + g * acc), extras=[(x1, "tile"), (g_f, "col")])
        saved.append(dict(x=xl, h1=h1, proj=proj, qkv=qkv, a_tot=a_tot, cat=cat, mix=mix,
                          x1=x1, h2=h2, ra=ra, r=r, m2=m2))
        xl = x2

    dx, loss_part, d_final_g, dm2, dg_f = loss_head(xl, target, final_norm_g.reshape(1, D),
                                                    (saved[L - 1]["m2"], mod[L - 1, NMOD - 1]), "loss_head")

    p_in, p_out, p_w1, p_w2 = [None] * L, [None] * L, [None] * L, [None] * L
    w_in_grads = [None] * L
    vec_rows, d_norm_mix, d_norm_mlp = [None] * L, [None] * L, [None] * L
    dcw8, d_conv_b, d_gn, d_sw, d_sb = [None] * L, [None] * L, [None] * L, [None] * L, [None] * L
    late_grads = [None] * L
    for l in reversed(range(L)):
        sv = saved[l]
        sh_m, sc_m, g_m, sh_f, sc_f, g_f = [mod[l, k] for k in range(NMOD)]
        da = mm_layer("mlp_down_dgrad", l, dm2, W2[l], out_dtypes=[BF16], trans_b=True,
                      epilogue=lambda acc, rav: (acc * (2.0 * rav.astype(F32)),), extras=[(sv["ra"], "tile")])[0]
        dW2 = mm_layer("mlp_down_wgrad", l, sv["r"], dm2, out_dtypes=[BF16], trans_a=True)[0]
        dW1 = mm_layer("mlp_up_wgrad", l, sv["h2"], da, out_dtypes=[BF16], trans_a=True, out_blocks=True)[0]
        dh2 = mm_layer("mlp_up_dgrad", l, da, W1[l], out_dtypes=[F32], trans_b=True, b_blocks=True)[0]
        dx1, dsc_f, dsh_f, d_norm_mlp[l], dmix, dg_m = normmod_bwd(
            sv["x1"], dh2, dx, norm_mlp_g[l:l + 1], sc_f, f"norm_mlp_bwd{l}", gate_next=(sv["mix"], g_m))
        dcat = mm_layer("mix_dgrad", l, dmix, W_out[l], out_dtypes=[F32], trans_b=True)[0]
        dW_out = mm_layer("mix_wgrad", l, sv["cat"], dmix, out_dtypes=[BF16], trans_a=True)[0]
        pieces_w2, pieces_out = dW2.reshape(NDEV, DFF // NDEV, D), dW_out.reshape(NDEV, D // NDEV, D)
        ride, late = ([pieces_w2, pieces_out], dW1) if l == L - 1 else ([pieces_w2, dW1], pieces_out)
        dq, dk, dv, *arrived = attn_bwd(sv["qkv"], dcat, sv["a_tot"], f"attn_bwd{l}", comm=Exchange(ride))
        p_w2[l] = arrived[0]
        (p_out if l == L - 1 else p_w1)[l] = arrived[1]
        late_grads[l], late_token = start_copies([late], me, f"exchange_late{l}_start", False, after=dq)
        dbg, dcg, dhc, dcw8[l], d_conv_b[l] = conv_bwd(sv["proj"], dcat, cw8[l], conv_b[l:l + 1], f"conv_bwd{l}")
        dus, dvs, d_gn[l], dsw, dbias = sg_bwd(sv["proj"], dcat, gmlp_norm_g[l:l + 1], spatial_w[l], sg_bias[l],
                                               f"sg_bwd{l}")
        d_sw[l] = dsw.astype(BF16)
        d_sb[l] = dbias.reshape(T, SG_HEADS, HD).sum(axis=2).T
        dproj = jnp.concatenate([dq, dk, dv, dbg, dcg, dhc, dus, dvs], axis=1).astype(BF16)
        dW_in = mm_layer("proj_wgrad", l, sv["h1"], dproj, out_dtypes=[BF16], trans_a=True,
                         extras=[(late_token, "tie")])[0]
        pieces = columns_to_shards(dW_in, f"w_in_grad_shards{l}")
        w_in_grads[l], token = start_copies([pieces], me, f"exchange_w_in{l}_start", False)
        dh1 = mm_layer("proj_dgrad", l, dproj, W_in[l], out_dtypes=[F32], trans_b=True, extras=[(token, "tie")])[0]
        below = (saved[l - 1]["m2"], mod[l - 1, NMOD - 1]) if l > 0 else None
        dx, dsc_m, dsh_m, d_norm_mix[l], *gated_below = normmod_bwd(
            sv["x"], dh1, dx1, tied(norm_mix_g[l:l + 1], token), sc_m, f"norm_mix_bwd{l}", gate_next=below)
        vec_rows[l] = [dsh_m, dsc_m, dg_m, dsh_f, dsc_f, dg_f, d_norm_mix[l], d_norm_mlp[l]]
        if l > 0:
            dm2, dg_f = gated_below

    grad_x = dx.reshape(1, S, D)

    g_w2, d_w2, nm_w2, nv_w2 = adamw_reduce(mlp_w2, p_w2, m_mlp_w2, v_mlp_w2, 256, "adamw_mlp_w2", tie=token)
    p_w1[L - 1] = finish_copies(late_grads[L - 1], d_w2, f"exchange_late{L - 1}_wait")[0]
    g_w1, d_w1, nm_w1, nv_w1 = adamw_reduce(mlp_w1, p_w1, m_mlp_w1, v_mlp_w1, 256, "adamw_mlp_w1", tie=token)

    vec_pack = jnp.concatenate([row for l in range(L) for row in vec_rows[l]]
                               + [d_final_g, loss_part, jnp.zeros((VEC_ROWS - VEC_FINAL_ROW - 2, D), F32)], axis=0)
    vec_pack, _ = lax.optimization_barrier((vec_pack, (d_w1, d_w2)))
    w256_pack = jnp.concatenate([blk for l in range(L) for blk in (
        dcw8[l], d_conv_b[l], d_gn[l], jnp.zeros((W256_ROWS_PER_LAYER - W256_GN - 1, CW), F32))], axis=0)
    vec_all, w256_all, sb_all, *sw_all = run_comm(
        Gather([vec_pack, w256_pack, jnp.concatenate(d_sb, axis=0)] + d_sw), "gather_small_grads")

    dmod_all = (vec_all[:, :VEC_FINAL_ROW].reshape(NDEV, L, VEC_ROWS_PER_LAYER, D)[:, :, :NMOD]
                .reshape(NDEV, L, NMOD * D))
    dmod_cols = lax.dynamic_slice(dmod_all, (0, 0, me * ADA_COLS), (NDEV, L, ADA_COLS)).transpose(1, 0, 2)
    g_ada_w = ada_bwd(c_act, dmod_cols, "ada_bwd")

    flat2 = lambda t: t.reshape(L * D, ADA_COLS)
    d_ada_w, nm_ada_w, nv_ada_w = [t.reshape(L, D, ADA_COLS) for t in adamw_plain(
        flat2(ada_w), flat2(g_ada_w), flat2(m_ada_w), flat2(v_ada_w), 256, "adamw_ada_w")]

    after = jnp.concatenate([t.reshape(-1)[:1] for t in (d_w1, d_w2, d_ada_w)])
    p_in = [finish_copies(w_in_grads[l], after, f"exchange_w_in{l}_wait")[0] for l in range(L)]
    p_out[0] = finish_copies(late_grads[0], after, "exchange_late0_wait")[0]
    g_w_in, d_w_in, nm_w_in, nv_w_in = adamw_reduce(w_in, p_in, m_w_in, v_w_in, 256, "adamw_w_in")
    g_w_out, d_w_out, nm_w_out, nv_w_out = adamw_reduce(w_out, p_out, m_w_out, v_w_out, 128, "adamw_w_out")

    as_row = lambda t: t.reshape(1, D)
    small_params = [(ada_b, m_ada_b, v_ada_b), (norm_mix_g, m_norm_mix_g, v_norm_mix_g),
                    (norm_mlp_g, m_norm_mlp_g, v_norm_mlp_g),
                    (as_row(final_norm_g), as_row(m_final_norm_g), as_row(v_final_norm_g)),
                    (conv_b, m_conv_b, v_conv_b), (gmlp_norm_g, m_gmlp_norm_g, v_gmlp_norm_g),
                    (spatial_w, m_spatial_w, v_spatial_w), (spatial_b, m_spatial_b, v_spatial_b)]
    updated, (loss_sum, taps_sum) = small_update(vec_all, w256_all, sb_all, sw_all, small_params, "small_update")
    loss = loss_sum[0, 0]
    u_ada_b, u_norm_mix, u_norm_mlp, u_final, u_conv_b, u_gn, u_sw, u_sb = updated
    u_final = [t.reshape(D) for t in u_final]
    g_conv_w = lax.dynamic_slice(taps_sum, (0, 0, me * conv_shard), (L, 3, conv_shard))
    flat_cw = lambda t: t.reshape(L * 3, conv_shard)
    u_conv_w = [g_conv_w] + [t.reshape(L, 3, conv_shard) for t in adamw_plain(
        flat_cw(conv_w), flat_cw(g_conv_w), flat_cw(m_conv_w), flat_cw(v_conv_w), L * 3, "adamw_conv_w")]
    small_sets = [u_ada_b, u_norm_mix, u_norm_mlp, u_conv_w, u_conv_b, u_gn, u_sw, u_sb, u_final]
    small_g, sd, snm, snv = [[u[k] for u in small_sets] for k in range(4)]

    def ordered(big, small):
        ada, win, wout, w1, w2 = big
        return [ada, small[0], small[1], small[2], win, small[3], small[4], small[5], small[6], small[7],
                wout, w1, w2, small[8]]

    grads = ordered([g_ada_w, g_w_in, g_w_out, g_w1, g_w2], small_g)
    deltas = ordered([d_ada_w, d_w_in, d_w_out, d_w1, d_w2], sd)
    new_m = ordered([nm_ada_w, nm_w_in, nm_w_out, nm_w1, nm_w2], snm)
    new_v = ordered([nv_ada_w, nv_w_in, nv_w_out, nv_w1, nv_w2], snv)
    return (loss, grad_x, *grads, *deltas, *new_m, *new_v)
```

```python
import functools
import math

import jax
import jax.numpy as jnp
from jax import lax
from jax.experimental import pallas as pl
from jax.experimental.pallas import tpu as pltpu

F32 = jnp.float32
BF16 = jnp.bfloat16
MESH = pl.DeviceIdType.MESH

S = 2048
D = 1024
L = 2
NDEV = 8
HD = 64
NH = 8
PROJ = 2816
DFF = 4096
NMOD = 6
EPS = 1e-6
T = 128
SG_HEADS = 4
LANES = 128
CW = 256
QKV = 3 * NH * HD
REST = PROJ - QKV

LR, B1, B2, AEPS, WD, STEP = 0.001, 0.9, 0.999, 1e-08, 0.01, 10
BC1 = 1.0 - B1 ** STEP
BC2 = 1.0 - B2 ** STEP

VMEM_LIMIT = 48 * 1024 * 1024

HBM_SPEC = pl.BlockSpec(memory_space=pltpu.HBM)


def _cparams(sem=None):
    return pltpu.CompilerParams(dimension_semantics=sem, vmem_limit_bytes=VMEM_LIMIT)


def _my_pos():
    return lax.axis_index("x"), lax.axis_index("y"), lax.axis_index("c")


def _lin(p):
    return 4 * p[0] + 2 * p[1] + p[2]


class Gather:
    def __init__(self, arrs):
        self.arrs = list(arrs)
        n = len(self.arrs)
        self.out_shape = [jax.ShapeDtypeStruct((NDEV,) + a.shape, a.dtype) for a in self.arrs]
        self.scratch = [pltpu.SemaphoreType.DMA((n, 7)), pltpu.SemaphoreType.DMA((n, 7)),
                        pltpu.SemaphoreType.DMA((n,))]

    def phases(self, ins, outs, sems):
        n = len(self.arrs)
        send_sems, recv_sems, local_sems = sems
        x, y, c = _my_pos()
        me, sibling = (x, y, c), (x, y, 1 - c)
        chips = [(1 - x, y), (x, 1 - y), (1 - x, 1 - y)]

        def copy(a, k, block, to, src=None):
            slot = outs[a].at[_lin(block)]
            return pltpu.make_async_remote_copy(
                src_ref=slot if src is None else src, dst_ref=slot,
                send_sem=send_sems.at[a, k], recv_sem=recv_sems.at[a, k],
                device_id=to, device_id_type=MESH)

        def mine(a):
            return pltpu.make_async_copy(ins[a], outs[a].at[_lin(me)], local_sems.at[a])

        def first(a):
            return [copy(a, 0, me, sibling, src=ins[a])] + [
                copy(a, 1 + j, me, (*chip, c), src=ins[a]) for j, chip in enumerate(chips)]

        def passed(a):
            return [copy(a, 4 + j, (*chip, c), sibling) for j, chip in enumerate(chips)]

        def start():
            for a in range(n):
                mine(a).start()
                for cp in first(a):
                    cp.start()

        def relay():
            for j, chip in enumerate(chips):
                for a in range(n):
                    copy(a, 1 + j, (*chip, c), me).wait_recv()
                    passed(a)[j].start()

        def finish():
            for a in range(n):
                copy(a, 0, sibling, me).wait_recv()
            for j, chip in enumerate(chips):
                for a in range(n):
                    copy(a, 4 + j, (*chip, 1 - c), me).wait_recv()
            for a in range(n):
                for cp in first(a) + passed(a):
                    cp.wait_send()
                mine(a).wait()

        return start, relay, finish


class Exchange:
    def __init__(self, arrs):
        self.arrs = list(arrs)
        n = len(self.arrs)
        self.out_shape = [jax.ShapeDtypeStruct(a.shape, a.dtype) for a in self.arrs]
        self.scratch = [pltpu.SemaphoreType.DMA((n, 7)), pltpu.SemaphoreType.DMA((n, 7)),
                        pltpu.SemaphoreType.DMA((n,))]

    def phases(self, ins, outs, sems):
        n = len(self.arrs)
        send_sems, recv_sems, local_sems = sems
        x, y, c = _my_pos()
        me = (x, y, c)

        def peer(mask):
            return (1 - x if mask & 4 else x, 1 - y if mask & 2 else y, 1 - c if mask & 1 else c)

        def copy(a, mask):
            return pltpu.make_async_remote_copy(
                src_ref=ins[a].at[_lin(peer(mask))], dst_ref=outs[a].at[_lin(me)],
                send_sem=send_sems.at[a, mask - 1], recv_sem=recv_sems.at[a, mask - 1],
                device_id=peer(mask), device_id_type=MESH)

        def arrival(a, mask):
            return pltpu.make_async_remote_copy(
                src_ref=ins[a].at[_lin(me)], dst_ref=outs[a].at[_lin(peer(mask))],
                send_sem=send_sems.at[a, mask - 1], recv_sem=recv_sems.at[a, mask - 1],
                device_id=peer(mask), device_id_type=MESH)

        def mine(a):
            return pltpu.make_async_copy(ins[a].at[_lin(me)], outs[a].at[_lin(me)], local_sems.at[a])

        def start():
            for a in range(n):
                mine(a).start()
            for mask in (4, 2, 6, 1, 5, 3, 7):
                for a in range(n):
                    copy(a, mask).start()

        def relay():
            pass

        def finish():
            for mask in range(1, 8):
                for a in range(n):
                    arrival(a, mask).wait_recv()
            for mask in range(1, 8):
                for a in range(n):
                    copy(a, mask).wait_send()
            for a in range(n):
                mine(a).wait()

        return start, relay, finish


def run_comm(plan, name):
    n = len(plan.arrs)

    def body(*refs):
        start, relay, finish = plan.phases(refs[:n], refs[n:2 * n], refs[2 * n:])
        start()
        relay()
        finish()

    outs = pl.pallas_call(
        body, name=name, out_shape=plan.out_shape,
        in_specs=[HBM_SPEC] * n, out_specs=[HBM_SPEC] * n, scratch_shapes=plan.scratch,
    )(*plan.arrs)
    return list(outs)


SEM_SPEC = pl.BlockSpec(memory_space=pltpu.SEMAPHORE)
DATAFLOW = pltpu.SideEffectType.DATAFLOW_SIDE_EFFECTING


def _peer_copies(src_ref, land_ref, send_sems, recv_sems, first, same_block):
    x, y, c = _my_pos()
    me = (x, y, c)
    sends, arrivals = [], []
    for mask in (4, 2, 6, 1, 5, 3, 7):
        peer = (1 - x if mask & 4 else x, 1 - y if mask & 2 else y, 1 - c if mask & 1 else c)
        sends.append(pltpu.make_async_remote_copy(
            src_ref=src_ref if same_block else src_ref.at[_lin(peer)], dst_ref=land_ref.at[_lin(me)],
            send_sem=send_sems.at[first + mask - 1], recv_sem=recv_sems.at[first + mask - 1], device_id=peer,
            device_id_type=MESH))
        arrivals.append(pltpu.make_async_remote_copy(
            src_ref=src_ref if same_block else src_ref.at[_lin(me)], dst_ref=land_ref.at[_lin(peer)],
            send_sem=send_sems.at[first + mask - 1], recv_sem=recv_sems.at[first + mask - 1], device_id=peer,
            device_id_type=MESH))
    return sends, arrivals


def start_copies(srcs, me, name, same_block, after=None):
    n = len(srcs)
    landings = []
    for src in srcs:
        own = src[None] if same_block else lax.dynamic_index_in_dim(src, me, axis=0, keepdims=True)
        landings.append(lax.dynamic_update_slice(lax.empty((NDEV,) + own.shape[1:], src.dtype), own,
                                                 (me,) + (0,) * (own.ndim - 1)))

    def body(*refs):
        send_sems, recv_sems = refs[-2 * n - 3], refs[-2 * n - 2]
        token = refs[-1]
        for k in range(n):
            sends, _ = _peer_copies(refs[2 * k], refs[2 * k + 1], send_sems, recv_sems, 7 * k, same_block)
            for cp in sends:
                cp.start()
        token[...] = jnp.zeros_like(token)

    hbm = lambda a: pltpu.HBM(a.shape, a.dtype)
    pairs = [a for pair in zip(srcs, landings) for a in pair]
    extra = [] if after is None else [after]
    sems = pltpu.SemaphoreType.DMA((7 * n,))
    send_sems, recv_sems, *thru, token = pl.pallas_call(
        body, name=name,
        out_shape=(sems, sems, *[hbm(a) for a in pairs], jax.ShapeDtypeStruct((8, LANES), F32)),
        in_specs=[HBM_SPEC] * (2 * n) + [pl.BlockSpec(memory_space=pl.ANY)] * len(extra),
        out_specs=(SEM_SPEC, SEM_SPEC, *[HBM_SPEC] * (2 * n), pl.BlockSpec(memory_space=pltpu.VMEM)),
        input_output_aliases={k: 2 + k for k in range(2 * n)},
        compiler_params=pltpu.CompilerParams(has_side_effects=DATAFLOW),
    )(*[pltpu.with_memory_space_constraint(a, pltpu.HBM) for a in pairs], *extra)
    return (send_sems, recv_sems, thru, same_block), token


def finish_copies(handle, after, name):
    send_sems, recv_sems, thru, same_block = handle
    n = len(thru) // 2

    def body(*refs):
        send_sems, recv_sems = refs[2 * n], refs[2 * n + 1]
        for k in range(n):
            sends, arrivals = _peer_copies(refs[2 * k], refs[2 * k + 1], send_sems, recv_sems, 7 * k, same_block)
            for cp in sends:
                cp.wait_send()
            for cp in arrivals:
                cp.wait_recv()

    hbm = lambda a: pltpu.HBM(a.shape, a.dtype)
    outs = pl.pallas_call(
        body, name=name, out_shape=tuple(hbm(a) for a in thru),
        in_specs=[HBM_SPEC] * (2 * n) + [SEM_SPEC, SEM_SPEC, pl.BlockSpec(memory_space=pl.ANY)],
        out_specs=tuple([HBM_SPEC] * (2 * n)), input_output_aliases={k: k for k in range(2 * n)},
        compiler_params=pltpu.CompilerParams(has_side_effects=DATAFLOW),
    )(*thru, send_sems, recv_sems, after)
    return [outs[2 * k + 1] for k in range(n)]


def tied(x, token):
    return x + token[0:1, 0:1].astype(x.dtype)


MM_TILES = {
    "proj_qkv": (S, 512), "proj_rest": (S, 256), "mix": (512, D), "mlp_up": (S, 512), "mlp_down": (1024, 256),
    "mlp_down_dgrad": (S, 1024), "mlp_down_wgrad": (1024, 1024), "mlp_up_wgrad": (1024, 512),
    "mlp_up_dgrad": (1024, 512), "mix_dgrad": (1024, 512), "mix_wgrad": (512, 1024),
    "proj_wgrad": (1024, PROJ // 2), "proj_dgrad": (1024, 512),
}


def mm_layer(kind, l, a, b, **kw):
    tm, tn = MM_TILES[kind]
    return mm(a, b, tm=tm, tn=tn, name=f"{kind}{l}", **kw)


def mm(a, b, *, tm, tn, out_dtypes, epilogue=None, extras=(), name, trans_a=False, trans_b=False,
       cols=None, b_blocks=False, out_blocks=False):
    if trans_a:
        kdim, m = a.shape
    else:
        m, kdim = a.shape
    shard = b.shape[-1] if b_blocks else None
    if b_blocks:
        full = (b.shape[1], NDEV * shard)
    else:
        full = b.shape
    first, ncols = cols if cols is not None else (0, full[0] if trans_b else full[1])
    assert full[1 if trans_b else 0] == kdim and m % tm == 0 and ncols % tn == 0 and first % tn == 0
    j0 = first // tn
    if trans_a:
        a_spec = pl.BlockSpec((kdim, tm), lambda i, j: (0, i))
    else:
        a_spec = pl.BlockSpec((tm, kdim), lambda i, j: (i, 0))
    if b_blocks and trans_b:
        b_spec = pl.BlockSpec((NDEV, tn, shard), lambda i, j: (0, j0 + j, 0))
    elif b_blocks:
        assert tn == shard
        b_spec = pl.BlockSpec((None, kdim, tn), lambda i, j: (j0 + j, 0, 0))
    elif trans_b:
        b_spec = pl.BlockSpec((tn, kdim), lambda i, j: (j0 + j, 0))
    else:
        b_spec = pl.BlockSpec((kdim, tn), lambda i, j: (0, j0 + j))
    if out_blocks:
        assert tn * NDEV == ncols
        out_spec = pl.BlockSpec((None, tm, tn), lambda i, j: (j, i, 0))
        out_dims = (NDEV, m, tn)
    else:
        out_spec = pl.BlockSpec((tm, tn), lambda i, j: (i, j))
        out_dims = (m, ncols)
    ex_specs = []
    for arr, kind in extras:
        if kind == "tile":
            ex_specs.append(pl.BlockSpec((tm, tn), lambda i, j: (i, j)))
        elif kind == "col":
            ex_specs.append(pl.BlockSpec((1, tn), lambda i, j: (0, j)))
        else:
            ex_specs.append(pl.BlockSpec(arr.shape, lambda i, j: (0, 0)))
    n_ex, n_out = len(extras), len(out_dtypes)
    used = [k for k, (_, kind) in enumerate(extras) if kind != "tie"]

    def body(a_ref, b_ref, *rest):
        ex_refs, out_refs = rest[:n_ex], rest[n_ex:]
        if trans_a:
            acc = lax.dot_general(a_ref[...], b_ref[...], (((0,), (0,)), ((), ())),
                                  preferred_element_type=F32)
        elif trans_b and b_blocks:
            acc = jnp.zeros((tm, tn), F32)
            for d in range(NDEV):
                acc = acc + lax.dot_general(a_ref[:, d * shard:(d + 1) * shard], b_ref[d],
                                            (((1,), (1,)), ((), ())), preferred_element_type=F32)
        elif trans_b:
            acc = lax.dot_general(a_ref[...], b_ref[...], (((1,), (1,)), ((), ())),
                                  preferred_element_type=F32)
        else:
            acc = jnp.dot(a_ref[...], b_ref[...], preferred_element_type=F32)
        outs = (acc,) if epilogue is None else epilogue(acc, *[ex_refs[k][...] for k in used])
        for o_ref, val in zip(out_refs, outs):
            o_ref[...] = val.astype(o_ref.dtype)

    outs = pl.pallas_call(
        body, name=name, grid=(m // tm, ncols // tn),
        in_specs=[a_spec, b_spec] + ex_specs,
        out_specs=[out_spec for _ in range(n_out)],
        out_shape=[jax.ShapeDtypeStruct(out_dims, dt) for dt in out_dtypes],
        compiler_params=_cparams(("parallel", "parallel")),
    )(a, b, *[arr for arr, _ in extras])
    return list(outs)


TR = 512

ROW_SPEC = pl.BlockSpec((TR, D), lambda i: (i, 0))
VEC_SPEC = pl.BlockSpec((1, D), lambda i: (0, 0))


def _residual_then_norm(acc, xr, gate, g, sc, sh):
    x_new = xr + gate * acc
    rstd = lax.rsqrt(jnp.mean(x_new * x_new, axis=-1, keepdims=True) + EPS)
    return acc, x_new, ((x_new * rstd) * g) * (1.0 + sc) + sh


def normmod_fwd(x, g, sc, sh, name):
    def body(x_ref, g_ref, sc_ref, sh_ref, o_ref):
        xv = x_ref[...]
        rstd = lax.rsqrt(jnp.mean(xv * xv, axis=-1, keepdims=True) + EPS)
        n = (xv * rstd) * g_ref[...]
        o_ref[...] = (n * (1.0 + sc_ref[...]) + sh_ref[...]).astype(o_ref.dtype)

    return pl.pallas_call(
        body, name=name, grid=(S // TR,),
        in_specs=[ROW_SPEC, VEC_SPEC, VEC_SPEC, VEC_SPEC], out_specs=ROW_SPEC,
        out_shape=jax.ShapeDtypeStruct((S, D), BF16),
        compiler_params=_cparams(("parallel",)),
    )(x, g, sc, sh)


def _gate_next(dxv, refs):
    br_ref, gate_ref, dbr_ref, dgate_ref = refs

    @pl.when(pl.program_id(0) == 0)
    def _():
        dgate_ref[...] = jnp.zeros_like(dgate_ref)

    dbr_ref[...] = (dxv * gate_ref[...]).astype(dbr_ref.dtype)
    dgate_ref[...] += jnp.sum(dxv * br_ref[...], axis=0, keepdims=True)


GATE_NEXT_IN = [ROW_SPEC, VEC_SPEC]
GATE_NEXT_OUT = [ROW_SPEC, VEC_SPEC]
GATE_NEXT_SHAPES = [jax.ShapeDtypeStruct((S, D), BF16), jax.ShapeDtypeStruct((1, D), F32)]


def normmod_bwd(x, dh, dres, g, sc, name, gate_next=None):
    nxt = 2 if gate_next else 0

    def body(x_ref, dh_ref, dres_ref, g_ref, sc_ref, *rest):
        nxt_in, (dx_ref, dsc_ref, dsh_ref, dg_ref), nxt_out = rest[:nxt], rest[nxt:nxt + 4], rest[nxt + 4:]

        @pl.when(pl.program_id(0) == 0)
        def _():
            dsc_ref[...] = jnp.zeros_like(dsc_ref)
            dsh_ref[...] = jnp.zeros_like(dsh_ref)
            dg_ref[...] = jnp.zeros_like(dg_ref)

        xv, dh = x_ref[...], dh_ref[...]
        gv = g_ref[...]
        rstd = lax.rsqrt(jnp.mean(xv * xv, axis=-1, keepdims=True) + EPS)
        xhat = xv * rstd
        dn = dh * (1.0 + sc_ref[...])
        dxhat = dn * gv
        dxv = dres_ref[...] + rstd * (dxhat - xhat * jnp.mean(dxhat * xhat, axis=-1, keepdims=True))
        dx_ref[...] = dxv
        dsc_ref[...] += jnp.sum(dh * (xhat * gv), axis=0, keepdims=True)
        dsh_ref[...] += jnp.sum(dh, axis=0, keepdims=True)
        dg_ref[...] += jnp.sum(dn * xhat, axis=0, keepdims=True)
        if gate_next:
            _gate_next(dxv, nxt_in + nxt_out)

    vec_out = jax.ShapeDtypeStruct((1, D), F32)
    on = bool(gate_next)
    return pl.pallas_call(
        body, name=name, grid=(S // TR,),
        in_specs=[ROW_SPEC, ROW_SPEC, ROW_SPEC, VEC_SPEC, VEC_SPEC] + GATE_NEXT_IN * on,
        out_specs=[ROW_SPEC, VEC_SPEC, VEC_SPEC, VEC_SPEC] + GATE_NEXT_OUT * on,
        out_shape=[jax.ShapeDtypeStruct((S, D), F32), vec_out, vec_out, vec_out] + GATE_NEXT_SHAPES * on,
        compiler_params=_cparams(("arbitrary",)),
    )(x, dh, dres, g, sc, *(gate_next or ()))


def loss_head(x, target, g, gate_next, name):
    def body(x_ref, t_ref, g_ref, br_ref, gate_ref, dx_ref, loss_ref, dg_ref, dbr_ref, dgate_ref):
        @pl.when(pl.program_id(0) == 0)
        def _():
            loss_ref[...] = jnp.zeros_like(loss_ref)
            dg_ref[...] = jnp.zeros_like(dg_ref)

        xv, gv = x_ref[...], g_ref[...]
        rstd = lax.rsqrt(jnp.mean(xv * xv, axis=-1, keepdims=True) + EPS)
        xhat = xv * rstd
        err = xhat * gv - t_ref[...]
        loss_ref[...] += jnp.sum(err * err) * (0.5 / D)
        dy = err * (1.0 / D)
        dg_ref[...] += jnp.sum(dy * xhat, axis=0, keepdims=True)
        dxhat = dy * gv
        dxv = rstd * (dxhat - xhat * jnp.mean(dxhat * xhat, axis=-1, keepdims=True))
        dx_ref[...] = dxv
        _gate_next(dxv, (br_ref, gate_ref, dbr_ref, dgate_ref))

    return pl.pallas_call(
        body, name=name, grid=(S // TR,),
        in_specs=[ROW_SPEC, ROW_SPEC, VEC_SPEC] + GATE_NEXT_IN,
        out_specs=[ROW_SPEC, VEC_SPEC, VEC_SPEC] + GATE_NEXT_OUT,
        out_shape=[jax.ShapeDtypeStruct((S, D), F32), jax.ShapeDtypeStruct((1, D), F32),
                   jax.ShapeDtypeStruct((1, D), F32)] + GATE_NEXT_SHAPES,
        compiler_params=_cparams(("arbitrary",)),
    )(x, target, g, *gate_next)


TQ = 512
RS = 128
NSUB = TQ // RS
TK = 128


def _dot_hilo(a, tri_twice):
    hi = a.astype(BF16)
    lo = (a - hi.astype(F32)).astype(BF16)
    return jnp.dot(jnp.concatenate([hi, lo], axis=1), tri_twice, preferred_element_type=F32)


def _log_stay(z):
    neg = -z
    return jnp.minimum(neg, 0.0) - jnp.log(1.0 + jnp.exp(jnp.minimum(z, neg)))


def _tri_and_ones(kind):
    row = jnp.bitwise_and(lax.broadcasted_iota(jnp.int32, (2 * TK, 2 * TK), 0), TK - 1)
    col = lax.broadcasted_iota(jnp.int32, (2 * TK, 2 * TK), 1)
    tri = {"after": row > col, "upto": row <= col, "before": row < col}[kind]
    return jnp.logical_or(col >= TK, tri).astype(BF16)


NPAIR = NH // 2
SCALE = HD ** -0.5


def _pair_specs(first_block):
    rows = pl.BlockSpec((TQ, LANES), lambda p, i: (i, first_block + p))
    whole = pl.BlockSpec((S, LANES), lambda p, i: (0, first_block + p))
    return rows, whole


Q_ROWS_SPEC, _ = _pair_specs(0)
_, K_ALL_SPEC = _pair_specs(NPAIR)
_, V_ALL_SPEC = _pair_specs(2 * NPAIR)
PAIR_ROWS_SPEC = pl.BlockSpec((TQ, LANES), lambda p, i: (i, p))
PAIR_ALL_SPEC = pl.BlockSpec((S, LANES), lambda p, i: (0, p))
PAIR_TOTAL_SPEC = pl.BlockSpec((2, TQ, TK), lambda p, i: (p, i, 0))


def _head_halves(x):
    first = lax.broadcasted_iota(jnp.int32, x.shape, 1) < HD
    zero = jnp.zeros_like(x)
    return jnp.where(first, x, zero), jnp.where(first, zero, x)


def _join_heads(a, b):
    return jnp.where(lax.broadcasted_iota(jnp.int32, a.shape, 1) < HD, a, b)


def _comm_hooks(comm, refs, n_in, n_out, n_scratch):
    nc = len(comm.arrs) if comm is not None else 0
    ins, cin = refs[:n_in], refs[n_in:n_in + nc]
    outs = refs[n_in + nc:n_in + nc + n_out]
    cout = refs[n_in + nc + n_out:n_in + 2 * nc + n_out]
    scratch = refs[n_in + 2 * nc + n_out:n_in + 2 * nc + n_out + n_scratch]
    sems = refs[n_in + 2 * nc + n_out + n_scratch:]
    phases = comm.phases(cin, cout, sems) if comm is not None else None
    return ins, outs, scratch, phases


def _with_comm(comm, in_specs, out_specs, out_shape, operands, scratch):
    if comm is None:
        return dict(in_specs=in_specs, out_specs=out_specs, out_shape=out_shape, scratch_shapes=scratch), operands
    nc = len(comm.arrs)
    return dict(in_specs=in_specs + [HBM_SPEC] * nc, out_specs=out_specs + [HBM_SPEC] * nc,
                out_shape=out_shape + comm.out_shape, scratch_shapes=scratch + comm.scratch), operands + comm.arrs


def attn_fwd(qkv, name, comm=None):
    n_steps = S // TQ

    def body(*refs):
        (q_ref, k_ref, v_ref), (o_ref, r_ref), (acc_ref, z_even, z_odd, w_ref), phases = _comm_hooks(
            comm, refs, 3, 2, 4)
        p = pl.program_id(0)
        i = pl.program_id(1)
        if phases is not None:
            pl.when(jnp.logical_and(p == 0, i == 0))(phases[0])
            pl.when(jnp.logical_and(p == NPAIR - 1, i == n_steps - 2))(phases[1])
        chains = [(sub, h) for sub in range(NSUB) for h in range(2)]
        q_sub = [_head_halves(q_ref[pl.ds(sub * RS, RS), :] * SCALE) for sub in range(NSUB)]
        after = _tri_and_ones("after")
        below_diagonal = (lax.broadcasted_iota(jnp.int32, (RS, TK), 1)
                          < lax.broadcasted_iota(jnp.int32, (RS, TK), 0))
        base = i * NSUB
        all_subs = list(range(NSUB))

        acc_ref[...] = jnp.zeros_like(acc_ref)
        r_ref[...] = jnp.zeros_like(r_ref)
        w_ref[...] = jnp.zeros_like(w_ref)

        def key_rows(block):
            return pl.ds(pl.multiple_of(block * TK, TK), TK)

        def store_scores(z_ref, block, subs):
            kb = k_ref[key_rows(block), :]
            for c, (sub, h) in enumerate(chains):
                if sub in subs:
                    z_ref[c] = lax.dot_general(q_sub[sub][h], kb, (((1,), (1,)), ((), ())),
                                               preferred_element_type=F32)

        def add_weighted_values(block, subs):
            vb = v_ref[key_rows(block), :]
            for sub in subs:
                acc_ref[pl.ds(sub * RS, RS), :] += _join_heads(*[
                    jnp.dot(w_ref[2 * sub + h], vb, preferred_element_type=F32) for h in range(2)])

        def step(block, z_ref, z_next_ref, subs, diagonal_sub, prev_subs, next_subs):
            if prev_subs:
                add_weighted_values(block + 1, prev_subs)
            if next_subs:
                store_scores(z_next_ref, jnp.maximum(block - 1, 0), next_subs)
            active = [(c, sub, h) for c, (sub, h) in enumerate(chains) if sub in subs]
            ls, sums = {}, {}
            for c, sub, h in active:
                ls[c] = _log_stay(z_ref[c])
                sums[c] = _dot_hilo(jnp.where(below_diagonal, ls[c], 0.0) if sub == diagonal_sub else ls[c], after)
            for c, sub, h in active:
                rows = pl.ds(sub * RS, RS)
                later = r_ref[h, rows, :]
                w = jnp.exp(z_ref[c] + ls[c] + (sums[c][:, :TK] + later))
                if sub == diagonal_sub:
                    w = jnp.where(below_diagonal, w, 0.0)
                w_ref[c] = w.astype(BF16)
                r_ref[h, rows, :] = later + sums[c][:, TK:]

        store_scores(z_even, base + NSUB - 1, [NSUB - 1])
        buffers = (z_even, z_odd)
        for j in reversed(range(NSUB)):
            subs = all_subs[j:]
            step(base + j, buffers[0], buffers[1], subs, j, all_subs[j + 1:], all_subs[j - 1:] if j else all_subs)
            buffers = buffers[::-1]
        assert buffers[0] is z_even

        @pl.loop(0, base // 2)
        def _(pair):
            block = base - 1 - 2 * pair
            step(block, z_even, z_odd, all_subs, None, all_subs, all_subs)
            step(block - 1, z_odd, z_even, all_subs, None, all_subs, all_subs)

        add_weighted_values(0, all_subs)
        o_ref[...] = acc_ref[...].astype(o_ref.dtype)
        if phases is not None:
            pl.when(jnp.logical_and(p == NPAIR - 1, i == n_steps - 1))(phases[2])

    kwargs, operands = _with_comm(
        comm, [Q_ROWS_SPEC, K_ALL_SPEC, V_ALL_SPEC], [PAIR_ROWS_SPEC, PAIR_TOTAL_SPEC],
        [jax.ShapeDtypeStruct((S, NH * HD), BF16), jax.ShapeDtypeStruct((NH, S, TK), F32)], [qkv, qkv, qkv],
        [pltpu.VMEM((TQ, LANES), F32), pltpu.VMEM((2 * NSUB, RS, TK), F32), pltpu.VMEM((2 * NSUB, RS, TK), F32),
         pltpu.VMEM((2 * NSUB, RS, TK), BF16)])
    return pl.pallas_call(
        body, name=name, grid=(NPAIR, n_steps),
        compiler_params=_cparams(("arbitrary", "arbitrary")), **kwargs,
    )(*operands)


def attn_bwd(qkv, dout, totals, name, comm=None):
    n_steps = S // TQ

    def body(*refs):
        ((q_ref, k_ref, v_ref, do_ref, r_ref), (dq_ref, dk_ref, dv_ref),
         (z_even, z_odd, dw_even, dw_odd, before_ref, dbefore_ref, dz_ref, w_ref), phases) = _comm_hooks(
            comm, refs, 5, 3, 8)
        p = pl.program_id(0)
        i = pl.program_id(1)
        if phases is not None:
            pl.when(jnp.logical_and(p == 0, i == 0))(phases[0])
            pl.when(jnp.logical_and(p == NPAIR - 1, i == n_steps - 2))(phases[1])

        @pl.when(i == 0)
        def _():
            dk_ref[...] = jnp.zeros_like(dk_ref)
            dv_ref[...] = jnp.zeros_like(dv_ref)

        chains = [(sub, h) for sub in range(NSUB) for h in range(2)]
        nch = len(chains)
        qb = q_ref[...]
        dob = do_ref[...].astype(BF16)
        q_sub = [_head_halves(qb[sub * RS:(sub + 1) * RS] * SCALE) for sub in range(NSUB)]
        do_sub = [_head_halves(dob[sub * RS:(sub + 1) * RS]) for sub in range(NSUB)]
        upto = _tri_and_ones("upto")
        before_tri = _tri_and_ones("before")
        below_diagonal = (lax.broadcasted_iota(jnp.int32, (RS, TK), 1)
                          < lax.broadcasted_iota(jnp.int32, (RS, TK), 0))
        contract_lanes = (((1,), (1,)), ((), ()))
        contract_rows = (((0,), (0,)), ((), ()))
        base = i * NSUB
        all_subs = list(range(NSUB))

        def key_rows(block):
            return pl.ds(pl.multiple_of(block * TK, TK), TK)

        def store_products(bufs, block, subs):
            z_ref, dw_ref = bufs
            kb = k_ref[key_rows(block), :]
            vb = v_ref[key_rows(block), :]
            for c, (sub, h) in enumerate(chains):
                if sub in subs:
                    z_ref[c] = lax.dot_general(q_sub[sub][h], kb, contract_lanes, preferred_element_type=F32)
                    dw_ref[c] = lax.dot_general(do_sub[sub][h], vb, contract_lanes, preferred_element_type=F32)

        def add_gradients(block, subs):
            kb = k_ref[key_rows(block), :]
            for sub in subs:
                rows = pl.ds(sub * RS, RS)
                dq_ref[rows, :] += _join_heads(*[jnp.dot(dz_ref[h, rows, :], kb, preferred_element_type=F32)
                                                 for h in range(2)])
            dk_ref[key_rows(block), :] += _join_heads(*[
                lax.dot_general(dz_ref[h], qb, contract_rows, preferred_element_type=F32) for h in range(2)])
            dv_ref[key_rows(block), :] += _join_heads(*[
                lax.dot_general(w_ref[h], dob, contract_rows, preferred_element_type=F32) for h in range(2)])

        for ref in (dq_ref, before_ref, dbefore_ref, dz_ref, w_ref):
            ref[...] = jnp.zeros_like(ref)
        even, odd = (z_even, dw_even), (z_odd, dw_odd)
        store_products(even, 0, all_subs)

        def step(block, bufs, next_bufs, subs, diagonal_sub, prev_subs, next_subs):
            z_ref, dw_ref = bufs
            add_gradients(jnp.maximum(block - 1, 0), prev_subs)
            for sub in prev_subs:
                if sub not in subs:
                    dz_ref[:, pl.ds(sub * RS, RS), :] = jnp.zeros((2, RS, TK), BF16)
                    w_ref[:, pl.ds(sub * RS, RS), :] = jnp.zeros((2, RS, TK), BF16)
            if next_subs:
                store_products(next_bufs, block + 1, next_subs)
            active = [(c, sub, h) for c, (sub, h) in enumerate(chains) if sub in subs]
            ls, sums, dl, dsums = {}, {}, {}, {}
            for c, sub, h in active:
                ls[c] = _log_stay(z_ref[c])
                sums[c] = _dot_hilo(jnp.where(below_diagonal, ls[c], 0.0) if sub == diagonal_sub else ls[c], upto)
            for c, sub, h in active:
                rows = pl.ds(sub * RS, RS)
                before = before_ref[c]
                log_after = r_ref[h, rows, :] - (sums[c][:, :TK] + before)
                w = jnp.exp((z_ref[c] + ls[c]) + log_after)
                if sub == diagonal_sub:
                    w = jnp.where(below_diagonal, w, 0.0)
                dl[c] = dw_ref[c] * w
                dsums[c] = _dot_hilo(dl[c], before_tri)
                w_ref[h, rows, :] = w.astype(BF16)
                before_ref[c] = before + sums[c][:, TK:]
            for c, sub, h in active:
                rows = pl.ds(sub * RS, RS)
                dbefore = dbefore_ref[c]
                beta = jnp.exp(z_ref[c] + ls[c])
                if sub == diagonal_sub:
                    beta = jnp.where(below_diagonal, beta, 0.0)
                dstay = dsums[c][:, :TK] + dbefore
                dz_ref[h, rows, :] = ((dl[c] - beta * (dl[c] + dstay)) * SCALE).astype(BF16)
                dbefore_ref[c] = dbefore + dsums[c][:, TK:]

        @pl.loop(0, base // 2)
        def _(pair):
            step(2 * pair, even, odd, all_subs, None, all_subs, all_subs)
            step(2 * pair + 1, odd, even, all_subs, None, all_subs, all_subs)

        bufs = (even, odd)
        for j in range(NSUB):
            step(base + j, bufs[0], bufs[1], all_subs[j:], j, all_subs[j - 1:] if j else all_subs, all_subs[j + 1:])
            bufs = bufs[::-1]

        add_gradients(base + NSUB - 1, all_subs[NSUB - 1:])
        if phases is not None:
            pl.when(jnp.logical_and(p == NPAIR - 1, i == n_steps - 1))(phases[2])

    full = jax.ShapeDtypeStruct((S, NH * HD), F32)
    kwargs, operands = _with_comm(
        comm, [Q_ROWS_SPEC, K_ALL_SPEC, V_ALL_SPEC, PAIR_ROWS_SPEC, PAIR_TOTAL_SPEC],
        [PAIR_ROWS_SPEC, PAIR_ALL_SPEC, PAIR_ALL_SPEC], [full, full, full], [qkv, qkv, qkv, dout, totals],
        [pltpu.VMEM((2 * NSUB, RS, TK), F32)] * 6 + [pltpu.VMEM((2, TQ, TK), BF16)] * 2)
    return pl.pallas_call(
        body, name=name, grid=(NPAIR, n_steps),
        compiler_params=_cparams(("arbitrary", "arbitrary")), **kwargs,
    )(*operands)


def _proj_cols(first_col):
    base = first_col // LANES
    return pl.BlockSpec((S, LANES), lambda j: (0, base + j))


CONV_OUT_SPEC = pl.BlockSpec((S, LANES), lambda j: (0, j))
CONV_DOUT_SPEC = pl.BlockSpec((S, LANES), lambda j: (0, (NH * HD) // LANES + j))
CONV_W_SPEC = pl.BlockSpec((8, LANES), lambda j: (0, j))
CONV_B_SPEC = pl.BlockSpec((1, LANES), lambda j: (0, j))


def _shift_down(u, n):
    rows = lax.broadcasted_iota(jnp.int32, u.shape, 0)
    return jnp.where(rows >= n, pltpu.roll(u, n, 0), 0.0)


def _shift_up(u, n):
    rows = lax.broadcasted_iota(jnp.int32, u.shape, 0)
    return jnp.where(rows < S - n, pltpu.roll(u, S - n, 0), 0.0)


def conv_fwd(proj, cw8, cb, name):
    def body(bg_ref, cg_ref, hc_ref, w_ref, b_ref, o_ref):
        u = cg_ref[...] * hc_ref[...]
        w = w_ref[...]
        y = w[0:1, :] * _shift_down(u, 2) + w[1:2, :] * _shift_down(u, 1) + w[2:3, :] * u + b_ref[...]
        o_ref[...] = bg_ref[...] * y

    return pl.pallas_call(
        body, name=name, grid=(CW // LANES,),
        in_specs=[_proj_cols(0), _proj_cols(CW), _proj_cols(2 * CW), CONV_W_SPEC, CONV_B_SPEC],
        out_specs=CONV_OUT_SPEC, out_shape=jax.ShapeDtypeStruct((S, CW), F32),
        compiler_params=_cparams(("parallel",)),
    )(proj, proj, proj, cw8, cb)


def conv_bwd(proj, dout, cw8, cb, name):
    def body(bg_ref, cg_ref, hc_ref, do_ref, w_ref, b_ref, dbg_ref, dcg_ref, dhc_ref, dw_ref, db_ref):
        cg, hc, do = cg_ref[...], hc_ref[...], do_ref[...]
        w = w_ref[...]
        u = cg * hc
        u1, u2 = _shift_down(u, 1), _shift_down(u, 2)
        y = w[0:1, :] * u2 + w[1:2, :] * u1 + w[2:3, :] * u + b_ref[...]
        dbg_ref[...] = do * y
        dy = do * bg_ref[...]
        db_ref[...] = jnp.sum(dy, axis=0, keepdims=True)
        dw_ref[...] = jnp.concatenate(
            [jnp.sum(dy * u2, axis=0, keepdims=True), jnp.sum(dy * u1, axis=0, keepdims=True),
             jnp.sum(dy * u, axis=0, keepdims=True), jnp.zeros((5, LANES), F32)], axis=0)
        du = w[2:3, :] * dy + w[1:2, :] * _shift_up(dy, 1) + w[0:1, :] * _shift_up(dy, 2)
        dcg_ref[...] = du * hc
        dhc_ref[...] = du * cg

    full = jax.ShapeDtypeStruct((S, CW), F32)
    return pl.pallas_call(
        body, name=name, grid=(CW // LANES,),
        in_specs=[_proj_cols(0), _proj_cols(CW), _proj_cols(2 * CW), CONV_DOUT_SPEC, CONV_W_SPEC, CONV_B_SPEC],
        out_specs=[CONV_OUT_SPEC, CONV_OUT_SPEC, CONV_OUT_SPEC, CONV_W_SPEC, CONV_B_SPEC],
        out_shape=[full, full, full, jax.ShapeDtypeStruct((8, CW), F32), jax.ShapeDtypeStruct((1, CW), F32)],
        compiler_params=_cparams(("parallel",)),
    )(proj, proj, proj, dout, cw8, cb)


GELU_K = math.sqrt(2.0 / math.pi)
GELU_C = 0.044715


def _gelu(x):
    return 0.5 * x * (1.0 + jnp.tanh(GELU_K * (x + GELU_C * (x * x * x))))


def _gelu_grad(x):
    t = jnp.tanh(GELU_K * (x + GELU_C * (x * x * x)))
    return 0.5 * (1.0 + t) + 0.5 * x * (1.0 - t * t) * (GELU_K * (1.0 + 3.0 * GELU_C * (x * x)))


def _sg_masks():
    row = lax.broadcasted_iota(jnp.int32, (T, T), 0)
    col = lax.broadcasted_iota(jnp.int32, (T, T), 1)
    causal = jnp.right_shift(row, 6) >= jnp.right_shift(col, 6)
    head_of_col = jnp.right_shift(lax.broadcasted_iota(jnp.int32, (T, CW), 1), 6)
    return causal, head_of_col


def _sg_weights(sw_ref, causal):
    return [jnp.where(causal, sw_ref[h], 0.0).astype(BF16) for h in range(SG_HEADS)]


def _sg_mixed(vnb, weights, bias, head_of_col):
    mixed = bias
    for h in range(SG_HEADS):
        mh = jnp.dot(weights[h], vnb, preferred_element_type=F32)
        mixed = mixed + jnp.where(head_of_col == h, mh, 0.0)
    return mixed


SG_WINDOWS = 4
SG_ROWS = SG_WINDOWS * T
SG_U_SPEC = pl.BlockSpec((SG_ROWS, CW), lambda n: (n, 3))
SG_V_SPEC = pl.BlockSpec((SG_ROWS, CW), lambda n: (n, 4))
SG_ROW_SPEC = pl.BlockSpec((SG_ROWS, CW), lambda n: (n, 0))
SG_DOUT_SPEC = pl.BlockSpec((SG_ROWS, CW), lambda n: (n, 3))
SG_G_SPEC = pl.BlockSpec((1, CW), lambda n: (0, 0))
SG_W_SPEC = pl.BlockSpec((SG_HEADS, T, T), lambda n: (0, 0, 0))
SG_BIAS_SPEC = pl.BlockSpec((T, CW), lambda n: (0, 0))


def sg_fwd(proj, gn, sw, bias, name):
    def body(u_ref, v_ref, g_ref, sw_ref, bias_ref, o_ref):
        causal, head_of_col = _sg_masks()
        weights = _sg_weights(sw_ref, causal)
        for wdw in range(SG_WINDOWS):
            rows = pl.ds(wdw * T, T)
            gv = _gelu(v_ref[rows, :])
            rstd = lax.rsqrt(jnp.mean(gv * gv, axis=-1, keepdims=True) + EPS)
            vnb = ((gv * rstd) * g_ref[...]).astype(BF16)
            mixed = _sg_mixed(vnb, weights, bias_ref[...], head_of_col)
            o_ref[rows, :] = _gelu(u_ref[rows, :]) * mixed

    return pl.pallas_call(
        body, name=name, grid=(S // SG_ROWS,),
        in_specs=[SG_U_SPEC, SG_V_SPEC, SG_G_SPEC, SG_W_SPEC, SG_BIAS_SPEC],
        out_specs=SG_ROW_SPEC, out_shape=jax.ShapeDtypeStruct((S, CW), F32),
        compiler_params=_cparams(("parallel",)),
    )(proj, proj, gn, sw, bias)


def sg_bwd(proj, dout, gn, sw, bias, name):
    def body(u_ref, v_ref, do_ref, g_ref, sw_ref, bias_ref, du_ref, dv_ref, dg_ref, dsw_ref, dbias_ref):
        @pl.when(pl.program_id(0) == 0)
        def _():
            dg_ref[...] = jnp.zeros_like(dg_ref)
            dsw_ref[...] = jnp.zeros_like(dsw_ref)
            dbias_ref[...] = jnp.zeros_like(dbias_ref)

        causal, head_of_col = _sg_masks()
        weights = _sg_weights(sw_ref, causal)
        gnv = g_ref[...]
        for wdw in range(SG_WINDOWS):
            rows = pl.ds(wdw * T, T)
            uv, vv, do = u_ref[rows, :], v_ref[rows, :], do_ref[rows, :]
            gv = _gelu(vv)
            rstd = lax.rsqrt(jnp.mean(gv * gv, axis=-1, keepdims=True) + EPS)
            xhat = gv * rstd
            vnb = (xhat * gnv).astype(BF16)
            mixed = _sg_mixed(vnb, weights, bias_ref[...], head_of_col)
            du_ref[rows, :] = (do * mixed) * _gelu_grad(uv)
            dmix = do * _gelu(uv)
            dbias_ref[...] += dmix
            dmixb = dmix.astype(BF16)
            dvn = jnp.zeros((T, CW), F32)
            for h in range(SG_HEADS):
                dvh = lax.dot_general(weights[h], dmixb, (((0,), (0,)), ((), ())), preferred_element_type=F32)
                dvn = dvn + jnp.where(head_of_col == h, dvh, 0.0)
                dmh = jnp.where(head_of_col == h, dmixb, jnp.zeros_like(dmixb))
                dwh = lax.dot_general(dmh, vnb, (((1,), (1,)), ((), ())), preferred_element_type=F32)
                dsw_ref[h] += jnp.where(causal, dwh, 0.0)
            dg_ref[...] += jnp.sum(dvn * xhat, axis=0, keepdims=True)
            dxhat = dvn * gnv
            dgv = rstd * (dxhat - xhat * jnp.mean(dxhat * xhat, axis=-1, keepdims=True))
            dv_ref[rows, :] = dgv * _gelu_grad(vv)

    full = jax.ShapeDtypeStruct((S, CW), F32)
    return pl.pallas_call(
        body, name=name, grid=(S // SG_ROWS,),
        in_specs=[SG_U_SPEC, SG_V_SPEC, SG_DOUT_SPEC, SG_G_SPEC, SG_W_SPEC, SG_BIAS_SPEC],
        out_specs=[SG_ROW_SPEC, SG_ROW_SPEC, SG_G_SPEC, SG_W_SPEC, SG_BIAS_SPEC],
        out_shape=[full, full, jax.ShapeDtypeStruct((1, CW), F32),
                   jax.ShapeDtypeStruct((SG_HEADS, T, T), F32), jax.ShapeDtypeStruct((T, CW), F32)],
        compiler_params=_cparams(("arbitrary",)),
    )(proj, proj, dout, gn, sw, bias)


ADA_COLS = NMOD * D // NDEV


def ada_fwd(c_all, ada_w, ada_b_mine, name):
    def body(c_ref, w_ref, b_ref, o_ref, ca_ref):
        cv = c_ref[...]
        ca = cv * (1.0 / (1.0 + jnp.exp(-cv)))
        ca_ref[...] = ca
        cab = ca.astype(BF16)
        for l in range(L):
            o_ref[l] = jnp.dot(cab, w_ref[l].astype(BF16), preferred_element_type=F32) + b_ref[l]

    return pl.pallas_call(
        body, name=name,
        out_shape=[jax.ShapeDtypeStruct((L, NDEV, ADA_COLS), F32), jax.ShapeDtypeStruct((NDEV, D), F32)],
        compiler_params=_cparams(),
    )(c_all, ada_w, ada_b_mine)


def ada_bwd(ca, dmod_cols, name):
    def body(ca_ref, dm_ref, o_ref):
        cab = ca_ref[...].astype(BF16)
        for l in range(L):
            o_ref[l] = lax.dot_general(cab, dm_ref[l].astype(BF16), (((0,), (0,)), ((), ())),
                                       preferred_element_type=F32)

    return pl.pallas_call(
        body, name=name, out_shape=jax.ShapeDtypeStruct((L, D, ADA_COLS), F32),
        compiler_params=_cparams(),
    )(ca, dmod_cols)


def _adamw(w, g, m, v):
    m = B1 * m + (1.0 - B1) * g
    v = B2 * v + (1.0 - B2) * (g * g)
    m_hat = m / BC1
    v_hat = v / BC2
    delta = -LR * (m_hat / (jnp.sqrt(v_hat) + AEPS) + WD * w)
    return delta, m, v


VEC_ROWS_PER_LAYER = 8
VEC_FINAL_ROW = L * VEC_ROWS_PER_LAYER
VEC_ROWS = VEC_FINAL_ROW + 8
W256_TAPS, W256_CONV_B, W256_GN = 0, 8, 9
W256_ROWS_PER_LAYER = 16


def small_update(vec_all, w256_all, sb_all, sw_all, params, name):
    n_par = len(params)

    def body(*refs):
        vec_ref, w256_ref, sb_ref = refs[:3]
        sw_refs = refs[3:3 + L]
        par_refs = [refs[3 + L + 3 * k:3 + L + 3 * k + 3] for k in range(n_par)]
        out = refs[3 + L + 3 * n_par:]
        out_par = [out[4 * k:4 * k + 4] for k in range(n_par)]
        loss_ref, taps_ref = out[4 * n_par:]

        def total(ref, idx):
            acc = ref[(0,) + idx].astype(F32)
            for d in range(1, NDEV):
                acc = acc + ref[(d,) + idx].astype(F32)
            return acc

        def update(k, region, g):
            w_ref, m_ref, v_ref = par_refs[k]
            g_ref, d_ref, nm_ref, nv_ref = out_par[k]
            delta, nm, nv = _adamw(w_ref[region], g, m_ref[region], v_ref[region])
            g_ref[region] = g
            d_ref[region] = delta
            nm_ref[region] = nm
            nv_ref[region] = nv

        for l in range(L):
            base = l * VEC_ROWS_PER_LAYER
            for k in range(NMOD):
                update(0, (slice(l, l + 1), slice(k * D, (k + 1) * D)), total(vec_ref, (slice(base + k, base + k + 1),)))
            update(1, (slice(l, l + 1),), total(vec_ref, (slice(base + 6, base + 7),)))
            update(2, (slice(l, l + 1),), total(vec_ref, (slice(base + 7, base + 8),)))
            wbase = l * W256_ROWS_PER_LAYER
            update(4, (slice(l, l + 1),), total(w256_ref, (slice(wbase + W256_CONV_B, wbase + W256_CONV_B + 1),)))
            update(5, (slice(l, l + 1),), total(w256_ref, (slice(wbase + W256_GN, wbase + W256_GN + 1),)))
            update(6, (l,), total(sw_refs[l], ()))
            update(7, (l,), total(sb_ref, (slice(l * SG_HEADS, (l + 1) * SG_HEADS),)))
            taps_ref[l] = total(w256_ref, (slice(wbase + W256_TAPS, wbase + W256_TAPS + 8),))
        update(3, (slice(0, 1),), total(vec_ref, (slice(VEC_FINAL_ROW, VEC_FINAL_ROW + 1),)))
        loss_ref[...] = total(vec_ref, (slice(VEC_FINAL_ROW + 1, VEC_FINAL_ROW + 2), slice(0, LANES)))

    out_shape = []
    for w, _, _ in params:
        out_shape += [jax.ShapeDtypeStruct(w.shape, F32)] * 4
    out_shape += [jax.ShapeDtypeStruct((1, LANES), F32), jax.ShapeDtypeStruct((L, 8, CW), F32)]
    outs = pl.pallas_call(body, name=name, out_shape=out_shape, compiler_params=_cparams())(
        vec_all, w256_all, sb_all, *sw_all, *[a for p in params for a in p])
    return [outs[4 * k:4 * k + 4] for k in range(n_par)], outs[4 * n_par:]


def adamw_plain(w, g, m, v, tr, name):
    rows, cols = w.shape
    spec = pl.BlockSpec((tr, cols), lambda i: (i, 0))

    def body(w_ref, g_ref, m_ref, v_ref, d_ref, nm_ref, nv_ref):
        delta, nm, nv = _adamw(w_ref[...], g_ref[...], m_ref[...], v_ref[...])
        d_ref[...] = delta
        nm_ref[...] = nm
        nv_ref[...] = nv

    shp = jax.ShapeDtypeStruct((rows, cols), F32)
    return pl.pallas_call(
        body, name=name, grid=(rows // tr,), in_specs=[spec] * 4, out_specs=[spec] * 3,
        out_shape=[shp, shp, shp], compiler_params=_cparams(("parallel",)),
    )(w, g, m, v)


def adamw_reduce(w, parts, m, v, tr, name, tie=None):
    _, rows, cols = w.shape
    spec = pl.BlockSpec((None, tr, cols), lambda l, i: (l, i, 0))
    pspecs = [pl.BlockSpec((NDEV, tr, cols), lambda l, i, k=k: (0, jnp.where(l == k, i, 0), 0)) for k in range(L)]

    ties = [] if tie is None else [tie]

    def body(w_ref, p0_ref, p1_ref, m_ref, v_ref, *rest):
        g_ref, d_ref, nm_ref, nv_ref = rest[len(ties):]
        first_layer = pl.program_id(0) == 0
        g = jnp.zeros((tr, cols), F32)
        for d in range(NDEV):
            g = g + jnp.where(first_layer, p0_ref[d], p1_ref[d]).astype(F32)
        delta, nm, nv = _adamw(w_ref[...], g, m_ref[...], v_ref[...])
        g_ref[...] = g
        d_ref[...] = delta
        nm_ref[...] = nm
        nv_ref[...] = nv

    shp = jax.ShapeDtypeStruct(w.shape, F32)
    return pl.pallas_call(
        body, name=name, grid=(L, rows // tr),
        in_specs=[spec] + pspecs + [spec, spec] + [pl.BlockSpec(t.shape, lambda l, i: (0, 0)) for t in ties],
        out_specs=[spec] * 4, out_shape=[shp] * 4, compiler_params=_cparams(("parallel", "parallel")),
    )(w, *parts, m, v, *ties)


SHARD_IN = PROJ // NDEV


def shards_to_columns(shards, name):
    tr = 256

    def body(i_ref, o_ref):
        for d in range(NDEV):
            o_ref[:, d * SHARD_IN:(d + 1) * SHARD_IN] = i_ref[d]

    return pl.pallas_call(
        body, name=name, grid=(D // tr,),
        in_specs=[pl.BlockSpec((NDEV, tr, SHARD_IN), lambda i: (0, i, 0))],
        out_specs=pl.BlockSpec((tr, PROJ), lambda i: (i, 0)),
        out_shape=jax.ShapeDtypeStruct((D, PROJ), shards.dtype), compiler_params=_cparams(("parallel",)),
    )(shards)


def columns_to_shards(mat, name):
    tr = 256

    def body(i_ref, o_ref):
        for d in range(NDEV):
            o_ref[d] = i_ref[:, d * SHARD_IN:(d + 1) * SHARD_IN]

    return pl.pallas_call(
        body, name=name, grid=(D // tr,),
        in_specs=[pl.BlockSpec((tr, PROJ), lambda i: (i, 0))],
        out_specs=pl.BlockSpec((NDEV, tr, SHARD_IN), lambda i: (0, i, 0)),
        out_shape=jax.ShapeDtypeStruct((NDEV, D, SHARD_IN), mat.dtype), compiler_params=_cparams(("parallel",)),
    )(mat)


def _pad_rows(flat, rows):
    return jnp.pad(flat, (0, rows * LANES - flat.shape[0])).reshape(rows, LANES)


def kernel(x, c, ada_w, ada_b, norm_mix_g, norm_mlp_g, w_in, conv_w, conv_b, gmlp_norm_g, spatial_w, spatial_b, w_out, mlp_w1, mlp_w2, final_norm_g, loss_target, m_ada_w, m_ada_b, m_norm_mix_g, m_norm_mlp_g, m_w_in, m_conv_w, m_conv_b, m_gmlp_norm_g, m_spatial_w, m_spatial_b, m_w_out, m_mlp_w1, m_mlp_w2, m_final_norm_g, v_ada_w, v_ada_b, v_norm_mix_g, v_norm_mlp_g, v_w_in, v_conv_w, v_conv_b, v_gmlp_norm_g, v_spatial_w, v_spatial_b, v_w_out, v_mlp_w1, v_mlp_w2, v_final_norm_g):
    me = _lin(_my_pos())
    x0 = x[0]
    target = loss_target[0]
    conv_shard = conv_w.shape[-1]

    w_in_b, w_out_b, w1_b, w2_b = [w.astype(BF16) for w in (w_in, w_out, mlp_w1, mlp_w2)]
    pack0 = _pad_rows(jnp.concatenate([c.reshape(-1), conv_w.reshape(-1)]), 16)
    g0, gw_in0 = run_comm(Gather([pack0, w_in_b[0]]), "gather_first")
    g0 = g0.reshape(NDEV, 16 * LANES)
    c_all = g0[:, :D]
    conv_full = (g0[:, D:D + L * 3 * conv_shard].reshape(NDEV, L, 3, conv_shard)
                 .transpose(1, 2, 0, 3).reshape(L, 3, CW))


    W_in = [shards_to_columns(gw_in0, "w_in_columns0"), None]
    W_out, W1, W2 = [None] * L, [None] * L, [None] * L

    ada_b_mine = lax.dynamic_slice(ada_b, (0, me * ADA_COLS), (L, ADA_COLS)).reshape(L, 1, ADA_COLS)
    mod_part, c_act = ada_fwd(c_all, ada_w, ada_b_mine, "ada_fwd")
    gmod = run_comm(Gather([mod_part]), "gather_mod")[0]
    mod = lax.dynamic_index_in_dim(gmod, me, axis=2, keepdims=False)
    mod = mod.transpose(1, 0, 2).reshape(L, NMOD, 1, D)
    early_weights, token = start_copies([w_out_b[0]], me, "gather_early0_start", True, after=gmod)
    mod = tied(mod, token)

    cw8 = jnp.pad(conv_full, ((0, 0), (0, 5), (0, 0)))
    sg_bias = jnp.repeat(spatial_b.transpose(0, 2, 1), HD, axis=2)

    saved = []
    xl = x0
    for l in range(L):
        sh_m, sc_m, g_m, sh_f, sc_f, g_f = [mod[l, k] for k in range(NMOD)]
        h1 = normmod_fwd(xl, norm_mix_g[l:l + 1], sc_m, sh_m, f"norm_mix_fwd{l}")
        if l > 0:
            W_in[l] = shards_to_columns(finish_copies(w_in_handle, xl, f"gather_w_in{l}_wait")[0],
                                        f"w_in_columns{l}")
        qkv = mm_layer("proj_qkv", l, h1, W_in[l], out_dtypes=[BF16], cols=(0, QKV))[0]
        proj = mm_layer("proj_rest", l, h1, W_in[l], out_dtypes=[F32], cols=(QKV, REST))[0]
        riders = [w2_b[l]] if l > 0 else [w2_b[l], w1_b[l]]
        a_out, a_tot, gw2, *rode = attn_fwd(qkv, f"attn_fwd{l}", comm=Gather(riders))
        gw_out, gw1 = (finish_copies(early_weights, a_out, f"gather_early{l}_wait") + rode)[:2]
        W_out[l] = gw_out.reshape(D, D)
        W1[l] = gw1
        W2[l] = gw2.reshape(DFF, D)
        if l + 1 < L:
            w_in_handle, token = start_copies([w_in_b[l + 1]], me, f"gather_w_in{l + 1}_start", True, after=a_out)
            early_weights, token = start_copies([w_out_b[l + 1], w1_b[l + 1]], me, f"gather_early{l + 1}_start", True,
                                                after=token)
            g_m = tied(g_m, token)
        c_out = conv_fwd(proj, cw8[l], conv_b[l:l + 1], f"conv_fwd{l}")
        s_out = sg_fwd(proj, gmlp_norm_g[l:l + 1], spatial_w[l], sg_bias[l], f"sg_fwd{l}")
        cat = jnp.concatenate([a_out, c_out.astype(BF16), s_out.astype(BF16)], axis=1)
        mix, x1, h2 = mm_layer("mix", l, cat, W_out[l], out_dtypes=[F32, F32, BF16], epilogue=_residual_then_norm,
                               extras=[(xl, "tile"), (g_m, "col"), (norm_mlp_g[l:l + 1], "col"), (sc_f, "col"),
                                       (sh_f, "col")])
        ra, r = mm_layer("mlp_up", l, h2, W1[l], out_dtypes=[BF16, BF16], b_blocks=True,
                         epilogue=lambda acc: (jnp.maximum(acc, 0.0), jnp.square(jnp.maximum(acc, 0.0))))
        m2, x2 = mm_layer("mlp_down", l, r, W2[l], out_dtypes=[F32, F32],
                          epilogue=lambda acc, xr, g: (acc, xr + g * acc), extras=[(x1, "tile"), (g_f, "col")])
        saved.append(dict(x=xl, h1=h1, proj=proj, qkv=qkv, a_tot=a_tot, cat=cat, mix=mix,
                          x1=x1, h2=h2, ra=ra, r=r, m2=m2))
        xl = x2

    dx, loss_part, d_final_g, dm2, dg_f = loss_head(xl, target, final_norm_g.reshape(1, D),
                                                    (saved[L - 1]["m2"], mod[L - 1, NMOD - 1]), "loss_head")

    p_in, p_out, p_w1, p_w2 = [None] * L, [None] * L, [None] * L, [None] * L
    w_in_grads = [None] * L
    vec_rows, d_norm_mix, d_norm_mlp = [None] * L, [None] * L, [None] * L
    dcw8, d_conv_b, d_gn, d_sw, d_sb = [None] * L, [None] * L, [None] * L, [None] * L, [None] * L
    late_grads = [None] * L
    for l in reversed(range(L)):
        sv = saved[l]
        sh_m, sc_m, g_m, sh_f, sc_f, g_f = [mod[l, k] for k in range(NMOD)]
        da = mm_layer("mlp_down_dgrad", l, dm2, W2[l], out_dtypes=[BF16], trans_b=True,
                      epilogue=lambda acc, rav: (acc * (2.0 * rav.astype(F32)),), extras=[(sv["ra"], "tile")])[0]
        dW2 = mm_layer("mlp_down_wgrad", l, sv["r"], dm2, out_dtypes=[BF16], trans_a=True)[0]
        dW1 = mm_layer("mlp_up_wgrad", l, sv["h2"], da, out_dtypes=[BF16], trans_a=True, out_blocks=True)[0]
        dh2 = mm_layer("mlp_up_dgrad", l, da, W1[l], out_dtypes=[F32], trans_b=True, b_blocks=True)[0]
        dx1, dsc_f, dsh_f, d_norm_mlp[l], dmix, dg_m = normmod_bwd(
            sv["x1"], dh2, dx, norm_mlp_g[l:l + 1], sc_f, f"norm_mlp_bwd{l}", gate_next=(sv["mix"], g_m))
        dcat = mm_layer("mix_dgrad", l, dmix, W_out[l], out_dtypes=[F32], trans_b=True)[0]
        dW_out = mm_layer("mix_wgrad", l, sv["cat"], dmix, out_dtypes=[BF16], trans_a=True)[0]
        pieces_w2, pieces_out = dW2.reshape(NDEV, DFF // NDEV, D), dW_out.reshape(NDEV, D // NDEV, D)
        ride, late = ([pieces_w2, pieces_out], dW1) if l == L - 1 else ([pieces_w2, dW1], pieces_out)
        dq, dk, dv, *arrived = attn_bwd(sv["qkv"], dcat, sv["a_tot"], f"attn_bwd{l}", comm=Exchange(ride))
        p_w2[l] = arrived[0]
        (p_out if l == L - 1 else p_w1)[l] = arrived[1]
        late_grads[l], late_token = start_copies([late], me, f"exchange_late{l}_start", False, after=dq)
        dbg, dcg, dhc, dcw8[l], d_conv_b[l] = conv_bwd(sv["proj"], dcat, cw8[l], conv_b[l:l + 1], f"conv_bwd{l}")
        dus, dvs, d_gn[l], dsw, dbias = sg_bwd(sv["proj"], dcat, gmlp_norm_g[l:l + 1], spatial_w[l], sg_bias[l],
                                               f"sg_bwd{l}")
        d_sw[l] = dsw.astype(BF16)
        d_sb[l] = dbias.reshape(T, SG_HEADS, HD).sum(axis=2).T
        dproj = jnp.concatenate([dq, dk, dv, dbg, dcg, dhc, dus, dvs], axis=1).astype(BF16)
        dW_in = mm_layer("proj_wgrad", l, sv["h1"], dproj, out_dtypes=[BF16], trans_a=True,
                         extras=[(late_token, "tie")])[0]
        pieces = columns_to_shards(dW_in, f"w_in_grad_shards{l}")
        w_in_grads[l], token = start_copies([pieces], me, f"exchange_w_in{l}_start", False)
        dh1 = mm_layer("proj_dgrad", l, dproj, W_in[l], out_dtypes=[F32], trans_b=True, extras=[(token, "tie")])[0]
        below = (saved[l - 1]["m2"], mod[l - 1, NMOD - 1]) if l > 0 else None
        dx, dsc_m, dsh_m, d_norm_mix[l], *gated_below = normmod_bwd(
            sv["x"], dh1, dx1, tied(norm_mix_g[l:l + 1], token), sc_m, f"norm_mix_bwd{l}", gate_next=below)
        vec_rows[l] = [dsh_m, dsc_m, dg_m, dsh_f, dsc_f, dg_f, d_norm_mix[l], d_norm_mlp[l]]
        if l > 0:
            dm2, dg_f = gated_below

    grad_x = dx.reshape(1, S, D)

    g_w2, d_w2, nm_w2, nv_w2 = adamw_reduce(mlp_w2, p_w2, m_mlp_w2, v_mlp_w2, 256, "adamw_mlp_w2", tie=token)
    p_w1[L - 1] = finish_copies(late_grads[L - 1], d_w2, f"exchange_late{L - 1}_wait")[0]
    g_w1, d_w1, nm_w1, nv_w1 = adamw_reduce(mlp_w1, p_w1, m_mlp_w1, v_mlp_w1, 256, "adamw_mlp_w1", tie=token)

    vec_pack = jnp.concatenate([row for l in range(L) for row in vec_rows[l]]
                               + [d_final_g, loss_part, jnp.zeros((VEC_ROWS - VEC_FINAL_ROW - 2, D), F32)], axis=0)
    vec_pack, _ = lax.optimization_barrier((vec_pack, (d_w1, d_w2)))
    w256_pack = jnp.concatenate([blk for l in range(L) for blk in (
        dcw8[l], d_conv_b[l], d_gn[l], jnp.zeros((W256_ROWS_PER_LAYER - W256_GN - 1, CW), F32))], axis=0)
    vec_all, w256_all, sb_all, *sw_all = run_comm(
        Gather([vec_pack, w256_pack, jnp.concatenate(d_sb, axis=0)] + d_sw), "gather_small_grads")

    dmod_all = (vec_all[:, :VEC_FINAL_ROW].reshape(NDEV, L, VEC_ROWS_PER_LAYER, D)[:, :, :NMOD]
                .reshape(NDEV, L, NMOD * D))
    dmod_cols = lax.dynamic_slice(dmod_all, (0, 0, me * ADA_COLS), (NDEV, L, ADA_COLS)).transpose(1, 0, 2)
    g_ada_w = ada_bwd(c_act, dmod_cols, "ada_bwd")

    flat2 = lambda t: t.reshape(L * D, ADA_COLS)
    d_ada_w, nm_ada_w, nv_ada_w = [t.reshape(L, D, ADA_COLS) for t in adamw_plain(
        flat2(ada_w), flat2(g_ada_w), flat2(m_ada_w), flat2(v_ada_w), 256, "adamw_ada_w")]

    after = jnp.concatenate([t.reshape(-1)[:1] for t in (d_w1, d_w2, d_ada_w)])
    p_in = [finish_copies(w_in_grads[l], after, f"exchange_w_in{l}_wait")[0] for l in range(L)]
    p_out[0] = finish_copies(late_grads[0], after, "exchange_late0_wait")[0]
    g_w_in, d_w_in, nm_w_in, nv_w_in = adamw_reduce(w_in, p_in, m_w_in, v_w_in, 256, "adamw_w_in")
    g_w_out, d_w_out, nm_w_out, nv_w_out = adamw_reduce(w_out, p_out, m_w_out, v_w_out, 128, "adamw_w_out")

    as_row = lambda t: t.reshape(1, D)
    small_params = [(ada_b, m_ada_b, v_ada_b), (norm_mix_g, m_norm_mix_g, v_norm_mix_g),
                    (norm_mlp_g, m_norm_mlp_g, v_norm_mlp_g),
                    (as_row(final_norm_g), as_row(m_final_norm_g), as_row(v_final_norm_g)),
                    (conv_b, m_conv_b, v_conv_b), (gmlp_norm_g, m_gmlp_norm_g, v_gmlp_norm_g),
                    (spatial_w, m_spatial_w, v_spatial_w), (spatial_b, m_spatial_b, v_spatial_b)]
    updated, (loss_sum, taps_sum) = small_update(vec_all, w256_all, sb_all, sw_all, small_params, "small_update")
    loss = loss_sum[0, 0]
    u_ada_b, u_norm_mix, u_norm_mlp, u_final, u_conv_b, u_gn, u_sw, u_sb = updated
    u_final = [t.reshape(D) for t in u_final]
    g_conv_w = lax.dynamic_slice(taps_sum, (0, 0, me * conv_shard), (L, 3, conv_shard))
    flat_cw = lambda t: t.reshape(L * 3, conv_shard)
    u_conv_w = [g_conv_w] + [t.reshape(L, 3, conv_shard) for t in adamw_plain(
        flat_cw(conv_w), flat_cw(g_conv_w), flat_cw(m_conv_w), flat_cw(v_conv_w), L * 3, "adamw_conv_w")]
    small_sets = [u_ada_b, u_norm_mix, u_norm_mlp, u_conv_w, u_conv_b, u_gn, u_sw, u_sb, u_final]
    small_g, sd, snm, snv = [[u[k] for u in small_sets] for k in range(4)]

    def ordered(big, small):
        ada, win, wout, w1, w2 = big
        return [ada, small[0], small[1], small[2], win, small[3], small[4], small[5], small[6], small[7],
                wout, w1, w2, small[8]]

    grads = ordered([g_ada_w, g_w_in, g_w_out, g_w1, g_w2], small_g)
    deltas = ordered([d_ada_w, d_w_in, d_w_out, d_w1, d_w2], sd)
    new_m = ordered([nm_ada_w, nm_w_in, nm_w_out, nm_w1, nm_w2], snm)
    new_v = ordered([nv_ada_w, nv_w_in, nv_w_out, nv_w1, nv_w2], snv)
    return (loss, grad_x, *grads, *deltas, *new_m, *new_v)
```

```python
import functools
import math

import jax
import jax.numpy as jnp
from jax import lax
from jax.experimental import pallas as pl
from jax.experimental.pallas import tpu as pltpu

F32 = jnp.float32
BF16 = jnp.bfloat16
MESH = pl.DeviceIdType.MESH

S = 2048
D = 1024
L = 2
NDEV = 8
HD = 64
NH = 8
PROJ = 2816
DFF = 4096
NMOD = 6
EPS = 1e-6
T = 128
SG_HEADS = 4
LANES = 128
CW = 256
QKV = 3 * NH * HD
REST = PROJ - QKV

LR, B1, B2, AEPS, WD, STEP = 0.001, 0.9, 0.999, 1e-08, 0.01, 10
BC1 = 1.0 - B1 ** STEP
BC2 = 1.0 - B2 ** STEP

VMEM_LIMIT = 48 * 1024 * 1024

HBM_SPEC = pl.BlockSpec(memory_space=pltpu.HBM)


def _cparams(sem=None):
    return pltpu.CompilerParams(dimension_semantics=sem, vmem_limit_bytes=VMEM_LIMIT)


def _my_pos():
    return lax.axis_index("x"), lax.axis_index("y"), lax.axis_index("c")


def _lin(p):
    return 4 * p[0] + 2 * p[1] + p[2]


class Gather:
    def __init__(self, arrs):
        self.arrs = list(arrs)
        n = len(self.arrs)
        self.out_shape = [jax.ShapeDtypeStruct((NDEV,) + a.shape, a.dtype) for a in self.arrs]
        self.scratch = [pltpu.SemaphoreType.DMA((n, 7)), pltpu.SemaphoreType.DMA((n, 7)),
                        pltpu.SemaphoreType.DMA((n,))]

    def phases(self, ins, outs, sems):
        n = len(self.arrs)
        send_sems, recv_sems, local_sems = sems
        x, y, c = _my_pos()
        me, sibling = (x, y, c), (x, y, 1 - c)
        chips = [(1 - x, y), (x, 1 - y), (1 - x, 1 - y)]

        def copy(a, k, block, to, src=None):
            slot = outs[a].at[_lin(block)]
            return pltpu.make_async_remote_copy(
                src_ref=slot if src is None else src, dst_ref=slot,
                send_sem=send_sems.at[a, k], recv_sem=recv_sems.at[a, k],
                device_id=to, device_id_type=MESH)

        def mine(a):
            return pltpu.make_async_copy(ins[a], outs[a].at[_lin(me)], local_sems.at[a])

        def first(a):
            return [copy(a, 0, me, sibling, src=ins[a])] + [
                copy(a, 1 + j, me, (*chip, c), src=ins[a]) for j, chip in enumerate(chips)]

        def passed(a):
            return [copy(a, 4 + j, (*chip, c), sibling) for j, chip in enumerate(chips)]

        def start():
            for a in range(n):
                mine(a).start()
                for cp in first(a):
                    cp.start()

        def relay():
            for j, chip in enumerate(chips):
                for a in range(n):
                    copy(a, 1 + j, (*chip, c), me).wait_recv()
                    passed(a)[j].start()

        def finish():
            for a in range(n):
                copy(a, 0, sibling, me).wait_recv()
            for j, chip in enumerate(chips):
                for a in range(n):
                    copy(a, 4 + j, (*chip, 1 - c), me).wait_recv()
            for a in range(n):
                for cp in first(a) + passed(a):
                    cp.wait_send()
                mine(a).wait()

        return start, relay, finish


class Exchange:
    def __init__(self, arrs):
        self.arrs = list(arrs)
        n = len(self.arrs)
        self.out_shape = [jax.ShapeDtypeStruct(a.shape, a.dtype) for a in self.arrs]
        self.scratch = [pltpu.SemaphoreType.DMA((n, 7)), pltpu.SemaphoreType.DMA((n, 7)),
                        pltpu.SemaphoreType.DMA((n,))]

    def phases(self, ins, outs, sems):
        n = len(self.arrs)
        send_sems, recv_sems, local_sems = sems
        x, y, c = _my_pos()
        me = (x, y, c)

        def peer(mask):
            return (1 - x if mask & 4 else x, 1 - y if mask & 2 else y, 1 - c if mask & 1 else c)

        def copy(a, mask):
            return pltpu.make_async_remote_copy(
                src_ref=ins[a].at[_lin(peer(mask))], dst_ref=outs[a].at[_lin(me)],
                send_sem=send_sems.at[a, mask - 1], recv_sem=recv_sems.at[a, mask - 1],
                device_id=peer(mask), device_id_type=MESH)

        def arrival(a, mask):
            return pltpu.make_async_remote_copy(
                src_ref=ins[a].at[_lin(me)], dst_ref=outs[a].at[_lin(peer(mask))],
                send_sem=send_sems.at[a, mask - 1], recv_sem=recv_sems.at[a, mask - 1],
                device_id=peer(mask), device_id_type=MESH)

        def mine(a):
            return pltpu.make_async_copy(ins[a].at[_lin(me)], outs[a].at[_lin(me)], local_sems.at[a])

        def start():
            for a in range(n):
                mine(a).start()
            for mask in (4, 2, 6, 1, 5, 3, 7):
                for a in range(n):
                    copy(a, mask).start()

        def relay():
            pass

        def finish():
            for mask in range(1, 8):
                for a in range(n):
                    arrival(a, mask).wait_recv()
            for mask in range(1, 8):
                for a in range(n):
                    copy(a, mask).wait_send()
            for a in range(n):
                mine(a).wait()

        return start, relay, finish


def run_comm(plan, name):
    n = len(plan.arrs)

    def body(*refs):
        start, relay, finish = plan.phases(refs[:n], refs[n:2 * n], refs[2 * n:])
        start()
        relay()
        finish()

    outs = pl.pallas_call(
        body, name=name, out_shape=plan.out_shape,
        in_specs=[HBM_SPEC] * n, out_specs=[HBM_SPEC] * n, scratch_shapes=plan.scratch,
    )(*plan.arrs)
    return list(outs)


SEM_SPEC = pl.BlockSpec(memory_space=pltpu.SEMAPHORE)
DATAFLOW = pltpu.SideEffectType.DATAFLOW_SIDE_EFFECTING


def _peer_copies(src_ref, land_ref, send_sems, recv_sems, first, same_block):
    x, y, c = _my_pos()
    me = (x, y, c)
    sends, arrivals = [], []
    for mask in (4, 2, 6, 1, 5, 3, 7):
        peer = (1 - x if mask & 4 else x, 1 - y if mask & 2 else y, 1 - c if mask & 1 else c)
        sends.append(pltpu.make_async_remote_copy(
            src_ref=src_ref if same_block else src_ref.at[_lin(peer)], dst_ref=land_ref.at[_lin(me)],
            send_sem=send_sems.at[first + mask - 1], recv_sem=recv_sems.at[first + mask - 1], device_id=peer,
            device_id_type=MESH))
        arrivals.append(pltpu.make_async_remote_copy(
            src_ref=src_ref if same_block else src_ref.at[_lin(me)], dst_ref=land_ref.at[_lin(peer)],
            send_sem=send_sems.at[first + mask - 1], recv_sem=recv_sems.at[first + mask - 1], device_id=peer,
            device_id_type=MESH))
    return sends, arrivals


def start_copies(srcs, me, name, same_block, after=None):
    n = len(srcs)
    landings = []
    for src in srcs:
        own = src[None] if same_block else lax.dynamic_index_in_dim(src, me, axis=0, keepdims=True)
        landings.append(lax.dynamic_update_slice(lax.empty((NDEV,) + own.shape[1:], src.dtype), own,
                                                 (me,) + (0,) * (own.ndim - 1)))

    def body(*refs):
        send_sems, recv_sems = refs[-2 * n - 3], refs[-2 * n - 2]
        token = refs[-1]
        for k in range(n):
            sends, _ = _peer_copies(refs[2 * k], refs[2 * k + 1], send_sems, recv_sems, 7 * k, same_block)
            for cp in sends:
                cp.start()
        token[...] = jnp.zeros_like(token)

    hbm = lambda a: pltpu.HBM(a.shape, a.dtype)
    pairs = [a for pair in zip(srcs, landings) for a in pair]
    extra = [] if after is None else [after]
    sems = pltpu.SemaphoreType.DMA((7 * n,))
    send_sems, recv_sems, *thru, token = pl.pallas_call(
        body, name=name,
        out_shape=(sems, sems, *[hbm(a) for a in pairs], jax.ShapeDtypeStruct((8, LANES), F32)),
        in_specs=[HBM_SPEC] * (2 * n) + [pl.BlockSpec(memory_space=pl.ANY)] * len(extra),
        out_specs=(SEM_SPEC, SEM_SPEC, *[HBM_SPEC] * (2 * n), pl.BlockSpec(memory_space=pltpu.VMEM)),
        input_output_aliases={k: 2 + k for k in range(2 * n)},
        compiler_params=pltpu.CompilerParams(has_side_effects=DATAFLOW),
    )(*[pltpu.with_memory_space_constraint(a, pltpu.HBM) for a in pairs], *extra)
    return (send_sems, recv_sems, thru, same_block), token


def finish_copies(handle, after, name):
    send_sems, recv_sems, thru, same_block = handle
    n = len(thru) // 2

    def body(*refs):
        send_sems, recv_sems = refs[2 * n], refs[2 * n + 1]
        for k in range(n):
            sends, arrivals = _peer_copies(refs[2 * k], refs[2 * k + 1], send_sems, recv_sems, 7 * k, same_block)
            for cp in sends:
                cp.wait_send()
            for cp in arrivals:
                cp.wait_recv()

    hbm = lambda a: pltpu.HBM(a.shape, a.dtype)
    outs = pl.pallas_call(
        body, name=name, out_shape=tuple(hbm(a) for a in thru),
        in_specs=[HBM_SPEC] * (2 * n) + [SEM_SPEC, SEM_SPEC, pl.BlockSpec(memory_space=pl.ANY)],
        out_specs=tuple([HBM_SPEC] * (2 * n)), input_output_aliases={k: k for k in range(2 * n)},
        compiler_params=pltpu.CompilerParams(has_side_effects=DATAFLOW),
    )(*thru, send_sems, recv_sems, after)
    return [outs[2 * k + 1] for k in range(n)]


def tied(x, token):
    return x + token[0:1, 0:1].astype(x.dtype)


MM_TILES = {
    "proj_qkv": (S, 512), "proj_rest": (S, 256), "mix": (512, D), "mlp_up": (S, 512), "mlp_down": (1024, 256),
    "mlp_down_dgrad": (S, 1024), "mlp_down_wgrad": (1024, 1024), "mlp_up_wgrad": (1024, 512),
    "mlp_up_dgrad": (1024, 512), "mix_dgrad": (1024, 512), "mix_wgrad": (512, 1024),
    "proj_wgrad": (1024, PROJ // 2), "proj_dgrad": (1024, 512),
}


def mm_layer(kind, l, a, b, **kw):
    tm, tn = MM_TILES[kind]
    return mm(a, b, tm=tm, tn=tn, name=f"{kind}{l}", **kw)


def mm(a, b, *, tm, tn, out_dtypes, epilogue=None, extras=(), name, trans_a=False, trans_b=False,
       cols=None, b_blocks=False, out_blocks=False):
    if trans_a:
        kdim, m = a.shape
    else:
        m, kdim = a.shape
    shard = b.shape[-1] if b_blocks else None
    if b_blocks:
        full = (b.shape[1], NDEV * shard)
    else:
        full = b.shape
    first, ncols = cols if cols is not None else (0, full[0] if trans_b else full[1])
    assert full[1 if trans_b else 0] == kdim and m % tm == 0 and ncols % tn == 0 and first % tn == 0
    j0 = first // tn
    if trans_a:
        a_spec = pl.BlockSpec((kdim, tm), lambda i, j: (0, i))
    else:
        a_spec = pl.BlockSpec((tm, kdim), lambda i, j: (i, 0))
    if b_blocks and trans_b:
        b_spec = pl.BlockSpec((NDEV, tn, shard), lambda i, j: (0, j0 + j, 0))
    elif b_blocks:
        assert tn == shard
        b_spec = pl.BlockSpec((None, kdim, tn), lambda i, j: (j0 + j, 0, 0))
    elif trans_b:
        b_spec = pl.BlockSpec((tn, kdim), lambda i, j: (j0 + j, 0))
    else:
        b_spec = pl.BlockSpec((kdim, tn), lambda i, j: (0, j0 + j))
    if out_blocks:
        assert tn * NDEV == ncols
        out_spec = pl.BlockSpec((None, tm, tn), lambda i, j: (j, i, 0))
        out_dims = (NDEV, m, tn)
    else:
        out_spec = pl.BlockSpec((tm, tn), lambda i, j: (i, j))
        out_dims = (m, ncols)
    ex_specs = []
    for arr, kind in extras:
        if kind == "tile":
            ex_specs.append(pl.BlockSpec((tm, tn), lambda i, j: (i, j)))
        elif kind == "col":
            ex_specs.append(pl.BlockSpec((1, tn), lambda i, j: (0, j)))
        else:
            ex_specs.append(pl.BlockSpec(arr.shape, lambda i, j: (0, 0)))
    n_ex, n_out = len(extras), len(out_dtypes)
    used = [k for k, (_, kind) in enumerate(extras) if kind != "tie"]

    def body(a_ref, b_ref, *rest):
        ex_refs, out_refs = rest[:n_ex], rest[n_ex:]
        if trans_a:
            acc = lax.dot_general(a_ref[...], b_ref[...], (((0,), (0,)), ((), ())),
                                  preferred_element_type=F32)
        elif trans_b and b_blocks:
            acc = jnp.zeros((tm, tn), F32)
            for d in range(NDEV):
                acc = acc + lax.dot_general(a_ref[:, d * shard:(d + 1) * shard], b_ref[d],
                                            (((1,), (1,)), ((), ())), preferred_element_type=F32)
        elif trans_b:
            acc = lax.dot_general(a_ref[...], b_ref[...], (((1,), (1,)), ((), ())),
                                  preferred_element_type=F32)
        else:
            acc = jnp.dot(a_ref[...], b_ref[...], preferred_element_type=F32)
        outs = (acc,) if epilogue is None else epilogue(acc, *[ex_refs[k][...] for k in used])
        for o_ref, val in zip(out_refs, outs):
            o_ref[...] = val.astype(o_ref.dtype)

    outs = pl.pallas_call(
        body, name=name, grid=(m // tm, ncols // tn),
        in_specs=[a_spec, b_spec] + ex_specs,
        out_specs=[out_spec for _ in range(n_out)],
        out_shape=[jax.ShapeDtypeStruct(out_dims, dt) for dt in out_dtypes],
        compiler_params=_cparams(("parallel", "parallel")),
    )(a, b, *[arr for arr, _ in extras])
    return list(outs)


TR = 512

ROW_SPEC = pl.BlockSpec((TR, D), lambda i: (i, 0))
VEC_SPEC = pl.BlockSpec((1, D), lambda i: (0, 0))


def _residual_then_norm(acc, xr, gate, g, sc, sh):
    x_new = xr + gate * acc
    rstd = lax.rsqrt(jnp.mean(x_new * x_new, axis=-1, keepdims=True) + EPS)
    return acc, x_new, ((x_new * rstd) * g) * (1.0 + sc) + sh


def normmod_fwd(x, g, sc, sh, name):
    def body(x_ref, g_ref, sc_ref, sh_ref, o_ref):
        xv = x_ref[...]
        rstd = lax.rsqrt(jnp.mean(xv * xv, axis=-1, keepdims=True) + EPS)
        n = (xv * rstd) * g_ref[...]
        o_ref[...] = (n * (1.0 + sc_ref[...]) + sh_ref[...]).astype(o_ref.dtype)

    return pl.pallas_call(
        body, name=name, grid=(S // TR,),
        in_specs=[ROW_SPEC, VEC_SPEC, VEC_SPEC, VEC_SPEC], out_specs=ROW_SPEC,
        out_shape=jax.ShapeDtypeStruct((S, D), BF16),
        compiler_params=_cparams(("parallel",)),
    )(x, g, sc, sh)


def _gate_next(dxv, refs):
    br_ref, gate_ref, dbr_ref, dgate_ref = refs

    @pl.when(pl.program_id(0) == 0)
    def _():
        dgate_ref[...] = jnp.zeros_like(dgate_ref)

    dbr_ref[...] = (dxv * gate_ref[...]).astype(dbr_ref.dtype)
    dgate_ref[...] += jnp.sum(dxv * br_ref[...], axis=0, keepdims=True)


GATE_NEXT_IN = [ROW_SPEC, VEC_SPEC]
GATE_NEXT_OUT = [ROW_SPEC, VEC_SPEC]
GATE_NEXT_SHAPES = [jax.ShapeDtypeStruct((S, D), BF16), jax.ShapeDtypeStruct((1, D), F32)]


def normmod_bwd(x, dh, dres, g, sc, name, gate_next=None):
    nxt = 2 if gate_next else 0

    def body(x_ref, dh_ref, dres_ref, g_ref, sc_ref, *rest):
        nxt_in, (dx_ref, dsc_ref, dsh_ref, dg_ref), nxt_out = rest[:nxt], rest[nxt:nxt + 4], rest[nxt + 4:]

        @pl.when(pl.program_id(0) == 0)
        def _():
            dsc_ref[...] = jnp.zeros_like(dsc_ref)
            dsh_ref[...] = jnp.zeros_like(dsh_ref)
            dg_ref[...] = jnp.zeros_like(dg_ref)

        xv, dh = x_ref[...], dh_ref[...]
        gv = g_ref[...]
        rstd = lax.rsqrt(jnp.mean(xv * xv, axis=-1, keepdims=True) + EPS)
        xhat = xv * rstd
        dn = dh * (1.0 + sc_ref[...])
        dxhat = dn * gv
        dxv = dres_ref[...] + rstd * (dxhat - xhat * jnp.mean(dxhat * xhat, axis=-1, keepdims=True))
        dx_ref[...] = dxv
        dsc_ref[...] += jnp.sum(dh * (xhat * gv), axis=0, keepdims=True)
        dsh_ref[...] += jnp.sum(dh, axis=0, keepdims=True)
        dg_ref[...] += jnp.sum(dn * xhat, axis=0, keepdims=True)
        if gate_next:
            _gate_next(dxv, nxt_in + nxt_out)

    vec_out = jax.ShapeDtypeStruct((1, D), F32)
    on = bool(gate_next)
    return pl.pallas_call(
        body, name=name, grid=(S // TR,),
        in_specs=[ROW_SPEC, ROW_SPEC, ROW_SPEC, VEC_SPEC, VEC_SPEC] + GATE_NEXT_IN * on,
        out_specs=[ROW_SPEC, VEC_SPEC, VEC_SPEC, VEC_SPEC] + GATE_NEXT_OUT * on,
        out_shape=[jax.ShapeDtypeStruct((S, D), F32), vec_out, vec_out, vec_out] + GATE_NEXT_SHAPES * on,
        compiler_params=_cparams(("arbitrary",)),
    )(x, dh, dres, g, sc, *(gate_next or ()))


def loss_head(x, target, g, gate_next, name):
    def body(x_ref, t_ref, g_ref, br_ref, gate_ref, dx_ref, loss_ref, dg_ref, dbr_ref, dgate_ref):
        @pl.when(pl.program_id(0) == 0)
        def _():
            loss_ref[...] = jnp.zeros_like(loss_ref)
            dg_ref[...] = jnp.zeros_like(dg_ref)

        xv, gv = x_ref[...], g_ref[...]
        rstd = lax.rsqrt(jnp.mean(xv * xv, axis=-1, keepdims=True) + EPS)
        xhat = xv * rstd
        err = xhat * gv - t_ref[...]
        loss_ref[...] += jnp.sum(err * err) * (0.5 / D)
        dy = err * (1.0 / D)
        dg_ref[...] += jnp.sum(dy * xhat, axis=0, keepdims=True)
        dxhat = dy * gv
        dxv = rstd * (dxhat - xhat * jnp.mean(dxhat * xhat, axis=-1, keepdims=True))
        dx_ref[...] = dxv
        _gate_next(dxv, (br_ref, gate_ref, dbr_ref, dgate_ref))

    return pl.pallas_call(
        body, name=name, grid=(S // TR,),
        in_specs=[ROW_SPEC, ROW_SPEC, VEC_SPEC] + GATE_NEXT_IN,
        out_specs=[ROW_SPEC, VEC_SPEC, VEC_SPEC] + GATE_NEXT_OUT,
        out_shape=[jax.ShapeDtypeStruct((S, D), F32), jax.ShapeDtypeStruct((1, D), F32),
                   jax.ShapeDtypeStruct((1, D), F32)] + GATE_NEXT_SHAPES,
        compiler_params=_cparams(("arbitrary",)),
    )(x, target, g, *gate_next)


TQ = 512
RS = 128
NSUB = TQ // RS
TK = 128


def _dot_hilo(a, tri_twice):
    hi = a.astype(BF16)
    lo = (a - hi.astype(F32)).astype(BF16)
    return jnp.dot(jnp.concatenate([hi, lo], axis=1), tri_twice, preferred_element_type=F32)


def _log_stay(z):
    neg = -z
    return jnp.minimum(neg, 0.0) - jnp.log(1.0 + jnp.exp(jnp.minimum(z, neg)))


def _tri_and_ones(kind):
    row = jnp.bitwise_and(lax.broadcasted_iota(jnp.int32, (2 * TK, 2 * TK), 0), TK - 1)
    col = lax.broadcasted_iota(jnp.int32, (2 * TK, 2 * TK), 1)
    tri = {"after": row > col, "upto": row <= col, "before": row < col}[kind]
    return jnp.logical_or(col >= TK, tri).astype(BF16)


NPAIR = NH // 2
SCALE = HD ** -0.5


def _pair_specs(first_block):
    rows = pl.BlockSpec((TQ, LANES), lambda p, i: (i, first_block + p))
    whole = pl.BlockSpec((S, LANES), lambda p, i: (0, first_block + p))
    return rows, whole


Q_ROWS_SPEC, _ = _pair_specs(0)
_, K_ALL_SPEC = _pair_specs(NPAIR)
_, V_ALL_SPEC = _pair_specs(2 * NPAIR)
PAIR_ROWS_SPEC = pl.BlockSpec((TQ, LANES), lambda p, i: (i, p))
PAIR_ALL_SPEC = pl.BlockSpec((S, LANES), lambda p, i: (0, p))
PAIR_TOTAL_SPEC = pl.BlockSpec((2, TQ, TK), lambda p, i: (p, i, 0))


def _head_halves(x):
    first = lax.broadcasted_iota(jnp.int32, x.shape, 1) < HD
    zero = jnp.zeros_like(x)
    return jnp.where(first, x, zero), jnp.where(first, zero, x)


def _join_heads(a, b):
    return jnp.where(lax.broadcasted_iota(jnp.int32, a.shape, 1) < HD, a, b)


def _comm_hooks(comm, refs, n_in, n_out, n_scratch):
    nc = len(comm.arrs) if comm is not None else 0
    ins, cin = refs[:n_in], refs[n_in:n_in + nc]
    outs = refs[n_in + nc:n_in + nc + n_out]
    cout = refs[n_in + nc + n_out:n_in + 2 * nc + n_out]
    scratch = refs[n_in + 2 * nc + n_out:n_in + 2 * nc + n_out + n_scratch]
    sems = refs[n_in + 2 * nc + n_out + n_scratch:]
    phases = comm.phases(cin, cout, sems) if comm is not None else None
    return ins, outs, scratch, phases


def _with_comm(comm, in_specs, out_specs, out_shape, operands, scratch):
    if comm is None:
        return dict(in_specs=in_specs, out_specs=out_specs, out_shape=out_shape, scratch_shapes=scratch), operands
    nc = len(comm.arrs)
    return dict(in_specs=in_specs + [HBM_SPEC] * nc, out_specs=out_specs + [HBM_SPEC] * nc,
                out_shape=out_shape + comm.out_shape, scratch_shapes=scratch + comm.scratch), operands + comm.arrs


def attn_fwd(qkv, name, comm=None):
    n_steps = S // TQ

    def body(*refs):
        (q_ref, k_ref, v_ref), (o_ref, r_ref), (acc_ref, z_even, z_odd, w_ref), phases = _comm_hooks(
            comm, refs, 3, 2, 4)
        p = pl.program_id(0)
        i = pl.program_id(1)
        if phases is not None:
            pl.when(jnp.logical_and(p == 0, i == 0))(phases[0])
            pl.when(jnp.logical_and(p == NPAIR - 1, i == n_steps - 2))(phases[1])
        chains = [(sub, h) for sub in range(NSUB) for h in range(2)]
        q_sub = [_head_halves(q_ref[pl.ds(sub * RS, RS), :] * SCALE) for sub in range(NSUB)]
        after = _tri_and_ones("after")
        below_diagonal = (lax.broadcasted_iota(jnp.int32, (RS, TK), 1)
                          < lax.broadcasted_iota(jnp.int32, (RS, TK), 0))
        base = i * NSUB
        all_subs = list(range(NSUB))

        acc_ref[...] = jnp.zeros_like(acc_ref)
        r_ref[...] = jnp.zeros_like(r_ref)
        w_ref[...] = jnp.zeros_like(w_ref)

        def key_rows(block):
            return pl.ds(pl.multiple_of(block * TK, TK), TK)

        def store_scores(z_ref, block, subs):
            kb = k_ref[key_rows(block), :]
            for c, (sub, h) in enumerate(chains):
                if sub in subs:
                    z_ref[c] = lax.dot_general(q_sub[sub][h], kb, (((1,), (1,)), ((), ())),
                                               preferred_element_type=F32)

        def add_weighted_values(block, subs):
            vb = v_ref[key_rows(block), :]
            for sub in subs:
                acc_ref[pl.ds(sub * RS, RS), :] += _join_heads(*[
                    jnp.dot(w_ref[2 * sub + h], vb, preferred_element_type=F32) for h in range(2)])

        def step(block, z_ref, z_next_ref, subs, diagonal_sub, prev_subs, next_subs):
            if prev_subs:
                add_weighted_values(block + 1, prev_subs)
            if next_subs:
                store_scores(z_next_ref, jnp.maximum(block - 1, 0), next_subs)
            active = [(c, sub, h) for c, (sub, h) in enumerate(chains) if sub in subs]
            ls, sums = {}, {}
            for c, sub, h in active:
                ls[c] = _log_stay(z_ref[c])
                sums[c] = _dot_hilo(jnp.where(below_diagonal, ls[c], 0.0) if sub == diagonal_sub else ls[c], after)
            for c, sub, h in active:
                rows = pl.ds(sub * RS, RS)
                later = r_ref[h, rows, :]
                w = jnp.exp(z_ref[c] + ls[c] + (sums[c][:, :TK] + later))
                if sub == diagonal_sub:
                    w = jnp.where(below_diagonal, w, 0.0)
                w_ref[c] = w.astype(BF16)
                r_ref[h, rows, :] = later + sums[c][:, TK:]

        store_scores(z_even, base + NSUB - 1, [NSUB - 1])
        buffers = (z_even, z_odd)
        for j in reversed(range(NSUB)):
            subs = all_subs[j:]
            step(base + j, buffers[0], buffers[1], subs, j, all_subs[j + 1:], all_subs[j - 1:] if j else all_subs)
            buffers = buffers[::-1]
        assert buffers[0] is z_even

        @pl.loop(0, base // 2)
        def _(pair):
            block = base - 1 - 2 * pair
            step(block, z_even, z_odd, all_subs, None, all_subs, all_subs)
            step(block - 1, z_odd, z_even, all_subs, None, all_subs, all_subs)

        add_weighted_values(0, all_subs)
        o_ref[...] = acc_ref[...].astype(o_ref.dtype)
        if phases is not None:
            pl.when(jnp.logical_and(p == NPAIR - 1, i == n_steps - 1))(phases[2])

    kwargs, operands = _with_comm(
        comm, [Q_ROWS_SPEC, K_ALL_SPEC, V_ALL_SPEC], [PAIR_ROWS_SPEC, PAIR_TOTAL_SPEC],
        [jax.ShapeDtypeStruct((S, NH * HD), BF16), jax.ShapeDtypeStruct((NH, S, TK), F32)], [qkv, qkv, qkv],
        [pltpu.VMEM((TQ, LANES), F32), pltpu.VMEM((2 * NSUB, RS, TK), F32), pltpu.VMEM((2 * NSUB, RS, TK), F32),
         pltpu.VMEM((2 * NSUB, RS, TK), BF16)])
    return pl.pallas_call(
        body, name=name, grid=(NPAIR, n_steps),
        compiler_params=_cparams(("arbitrary", "arbitrary")), **kwargs,
    )(*operands)


def attn_bwd(qkv, dout, totals, name, comm=None):
    n_steps = S // TQ

    def body(*refs):
        ((q_ref, k_ref, v_ref, do_ref, r_ref), (dq_ref, dk_ref, dv_ref),
         (z_even, z_odd, dw_even, dw_odd, before_ref, dbefore_ref, dz_ref, w_ref), phases) = _comm_hooks(
            comm, refs, 5, 3, 8)
        p = pl.program_id(0)
        i = pl.program_id(1)
        if phases is not None:
            pl.when(jnp.logical_and(p == 0, i == 0))(phases[0])
            pl.when(jnp.logical_and(p == NPAIR - 1, i == n_steps - 2))(phases[1])

        @pl.when(i == 0)
        def _():
            dk_ref[...] = jnp.zeros_like(dk_ref)
            dv_ref[...] = jnp.zeros_like(dv_ref)

        chains = [(sub, h) for sub in range(NSUB) for h in range(2)]
        nch = len(chains)
        qb = q_ref[...]
        dob = do_ref[...].astype(BF16)
        q_sub = [_head_halves(qb[sub * RS:(sub + 1) * RS] * SCALE) for sub in range(NSUB)]
        do_sub = [_head_halves(dob[sub * RS:(sub + 1) * RS]) for sub in range(NSUB)]
        upto = _tri_and_ones("upto")
        before_tri = _tri_and_ones("before")
        below_diagonal = (lax.broadcasted_iota(jnp.int32, (RS, TK), 1)
                          < lax.broadcasted_iota(jnp.int32, (RS, TK), 0))
        contract_lanes = (((1,), (1,)), ((), ()))
        contract_rows = (((0,), (0,)), ((), ()))
        base = i * NSUB
        all_subs = list(range(NSUB))

        def key_rows(block):
            return pl.ds(pl.multiple_of(block * TK, TK), TK)

        def store_products(bufs, block, subs):
            z_ref, dw_ref = bufs
            kb = k_ref[key_rows(block), :]
            vb = v_ref[key_rows(block), :]
            for c, (sub, h) in enumerate(chains):
                if sub in subs:
                    z_ref[c] = lax.dot_general(q_sub[sub][h], kb, contract_lanes, preferred_element_type=F32)
                    dw_ref[c] = lax.dot_general(do_sub[sub][h], vb, contract_lanes, preferred_element_type=F32)

        def add_gradients(block, subs):
            kb = k_ref[key_rows(block), :]
            for sub in subs:
                rows = pl.ds(sub * RS, RS)
                dq_ref[rows, :] += _join_heads(*[jnp.dot(dz_ref[h, rows, :], kb, preferred_element_type=F32)
                                                 for h in range(2)])
            dk_ref[key_rows(block), :] += _join_heads(*[
                lax.dot_general(dz_ref[h], qb, contract_rows, preferred_element_type=F32) for h in range(2)])
            dv_ref[key_rows(block), :] += _join_heads(*[
                lax.dot_general(w_ref[h], dob, contract_rows, preferred_element_type=F32) for h in range(2)])

        for ref in (dq_ref, before_ref, dbefore_ref, dz_ref, w_ref):
            ref[...] = jnp.zeros_like(ref)
        even, odd = (z_even, dw_even), (z_odd, dw_odd)
        store_products(even, 0, all_subs)

        def step(block, bufs, next_bufs, subs, diagonal_sub, prev_subs, next_subs):
            z_ref, dw_ref = bufs
            add_gradients(jnp.maximum(block - 1, 0), prev_subs)
            for sub in prev_subs:
                if sub not in subs:
                    dz_ref[:, pl.ds(sub * RS, RS), :] = jnp.zeros((2, RS, TK), BF16)
                    w_ref[:, pl.ds(sub * RS, RS), :] = jnp.zeros((2, RS, TK), BF16)
            if next_subs:
                store_products(next_bufs, block + 1, next_subs)
            active = [(c, sub, h) for c, (sub, h) in enumerate(chains) if sub in subs]
            ls, sums, dl, dsums = {}, {}, {}, {}
            for c, sub, h in active:
                ls[c] = _log_stay(z_ref[c])
                sums[c] = _dot_hilo(jnp.where(below_diagonal, ls[c], 0.0) if sub == diagonal_sub else ls[c], upto)
            for c, sub, h in active:
                rows = pl.ds(sub * RS, RS)
                before = before_ref[c]
                log_after = r_ref[h, rows, :] - (sums[c][:, :TK] + before)
                w = jnp.exp((z_ref[c] + ls[c]) + log_after)
                if sub == diagonal_sub:
                    w = jnp.where(below_diagonal, w, 0.0)
                dl[c] = dw_ref[c] * w
                dsums[c] = _dot_hilo(dl[c], before_tri)
                w_ref[h, rows, :] = w.astype(BF16)
                before_ref[c] = before + sums[c][:, TK:]
            for c, sub, h in active:
                rows = pl.ds(sub * RS, RS)
                dbefore = dbefore_ref[c]
                beta = jnp.exp(z_ref[c] + ls[c])
                if sub == diagonal_sub:
                    beta = jnp.where(below_diagonal, beta, 0.0)
                dstay = dsums[c][:, :TK] + dbefore
                dz_ref[h, rows, :] = ((dl[c] - beta * (dl[c] + dstay)) * SCALE).astype(BF16)
                dbefore_ref[c] = dbefore + dsums[c][:, TK:]

        @pl.loop(0, base // 2)
        def _(pair):
            step(2 * pair, even, odd, all_subs, None, all_subs, all_subs)
            step(2 * pair + 1, odd, even, all_subs, None, all_subs, all_subs)

        bufs = (even, odd)
        for j in range(NSUB):
            step(base + j, bufs[0], bufs[1], all_subs[j:], j, all_subs[j - 1:] if j else all_subs, all_subs[j + 1:])
            bufs = bufs[::-1]

        add_gradients(base + NSUB - 1, all_subs[NSUB - 1:])
        if phases is not None:
            pl.when(jnp.logical_and(p == NPAIR - 1, i == n_steps - 1))(phases[2])

    full = jax.ShapeDtypeStruct((S, NH * HD), F32)
    kwargs, operands = _with_comm(
        comm, [Q_ROWS_SPEC, K_ALL_SPEC, V_ALL_SPEC, PAIR_ROWS_SPEC, PAIR_TOTAL_SPEC],
        [PAIR_ROWS_SPEC, PAIR_ALL_SPEC, PAIR_ALL_SPEC], [full, full, full], [qkv, qkv, qkv, dout, totals],
        [pltpu.VMEM((2 * NSUB, RS, TK), F32)] * 6 + [pltpu.VMEM((2, TQ, TK), BF16)] * 2)
    return pl.pallas_call(
        body, name=name, grid=(NPAIR, n_steps),
        compiler_params=_cparams(("arbitrary", "arbitrary")), **kwargs,
    )(*operands)


def _proj_cols(first_col):
    base = first_col // LANES
    return pl.BlockSpec((S, LANES), lambda j: (0, base + j))


CONV_OUT_SPEC = pl.BlockSpec((S, LANES), lambda j: (0, j))
CONV_DOUT_SPEC = pl.BlockSpec((S, LANES), lambda j: (0, (NH * HD) // LANES + j))
CONV_W_SPEC = pl.BlockSpec((8, LANES), lambda j: (0, j))
CONV_B_SPEC = pl.BlockSpec((1, LANES), lambda j: (0, j))


def _shift_down(u, n):
    rows = lax.broadcasted_iota(jnp.int32, u.shape, 0)
    return jnp.where(rows >= n, pltpu.roll(u, n, 0), 0.0)


def _shift_up(u, n):
    rows = lax.broadcasted_iota(jnp.int32, u.shape, 0)
    return jnp.where(rows < S - n, pltpu.roll(u, S - n, 0), 0.0)


def conv_fwd(proj, cw8, cb, name):
    def body(bg_ref, cg_ref, hc_ref, w_ref, b_ref, o_ref):
        u = cg_ref[...] * hc_ref[...]
        w = w_ref[...]
        y = w[0:1, :] * _shift_down(u, 2) + w[1:2, :] * _shift_down(u, 1) + w[2:3, :] * u + b_ref[...]
        o_ref[...] = bg_ref[...] * y

    return pl.pallas_call(
        body, name=name, grid=(CW // LANES,),
        in_specs=[_proj_cols(0), _proj_cols(CW), _proj_cols(2 * CW), CONV_W_SPEC, CONV_B_SPEC],
        out_specs=CONV_OUT_SPEC, out_shape=jax.ShapeDtypeStruct((S, CW), F32),
        compiler_params=_cparams(("parallel",)),
    )(proj, proj, proj, cw8, cb)


def conv_bwd(proj, dout, cw8, cb, name):
    def body(bg_ref, cg_ref, hc_ref, do_ref, w_ref, b_ref, dbg_ref, dcg_ref, dhc_ref, dw_ref, db_ref):
        cg, hc, do = cg_ref[...], hc_ref[...], do_ref[...]
        w = w_ref[...]
        u = cg * hc
        u1, u2 = _shift_down(u, 1), _shift_down(u, 2)
        y = w[0:1, :] * u2 + w[1:2, :] * u1 + w[2:3, :] * u + b_ref[...]
        dbg_ref[...] = do * y
        dy = do * bg_ref[...]
        db_ref[...] = jnp.sum(dy, axis=0, keepdims=True)
        dw_ref[...] = jnp.concatenate(
            [jnp.sum(dy * u2, axis=0, keepdims=True), jnp.sum(dy * u1, axis=0, keepdims=True),
             jnp.sum(dy * u, axis=0, keepdims=True), jnp.zeros((5, LANES), F32)], axis=0)
        du = w[2:3, :] * dy + w[1:2, :] * _shift_up(dy, 1) + w[0:1, :] * _shift_up(dy, 2)
        dcg_ref[...] = du * hc
        dhc_ref[...] = du * cg

    full = jax.ShapeDtypeStruct((S, CW), F32)
    return pl.pallas_call(
        body, name=name, grid=(CW // LANES,),
        in_specs=[_proj_cols(0), _proj_cols(CW), _proj_cols(2 * CW), CONV_DOUT_SPEC, CONV_W_SPEC, CONV_B_SPEC],
        out_specs=[CONV_OUT_SPEC, CONV_OUT_SPEC, CONV_OUT_SPEC, CONV_W_SPEC, CONV_B_SPEC],
        out_shape=[full, full, full, jax.ShapeDtypeStruct((8, CW), F32), jax.ShapeDtypeStruct((1, CW), F32)],
        compiler_params=_cparams(("parallel",)),
    )(proj, proj, proj, dout, cw8, cb)


GELU_K = math.sqrt(2.0 / math.pi)
GELU_C = 0.044715


def _gelu(x):
    return 0.5 * x * (1.0 + jnp.tanh(GELU_K * (x + GELU_C * (x * x * x))))


def _gelu_grad(x):
    t = jnp.tanh(GELU_K * (x + GELU_C * (x * x * x)))
    return 0.5 * (1.0 + t) + 0.5 * x * (1.0 - t * t) * (GELU_K * (1.0 + 3.0 * GELU_C * (x * x)))


def _sg_masks():
    row = lax.broadcasted_iota(jnp.int32, (T, T), 0)
    col = lax.broadcasted_iota(jnp.int32, (T, T), 1)
    causal = jnp.right_shift(row, 6) >= jnp.right_shift(col, 6)
    head_of_col = jnp.right_shift(lax.broadcasted_iota(jnp.int32, (T, CW), 1), 6)
    return causal, head_of_col


def _sg_weights(sw_ref, causal):
    return [jnp.where(causal, sw_ref[h], 0.0).astype(BF16) for h in range(SG_HEADS)]


def _sg_mixed(vnb, weights, bias, head_of_col):
    mixed = bias
    for h in range(SG_HEADS):
        mh = jnp.dot(weights[h], vnb, preferred_element_type=F32)
        mixed = mixed + jnp.where(head_of_col == h, mh, 0.0)
    return mixed


SG_WINDOWS = 4
SG_ROWS = SG_WINDOWS * T
SG_U_SPEC = pl.BlockSpec((SG_ROWS, CW), lambda n: (n, 3))
SG_V_SPEC = pl.BlockSpec((SG_ROWS, CW), lambda n: (n, 4))
SG_ROW_SPEC = pl.BlockSpec((SG_ROWS, CW), lambda n: (n, 0))
SG_DOUT_SPEC = pl.BlockSpec((SG_ROWS, CW), lambda n: (n, 3))
SG_G_SPEC = pl.BlockSpec((1, CW), lambda n: (0, 0))
SG_W_SPEC = pl.BlockSpec((SG_HEADS, T, T), lambda n: (0, 0, 0))
SG_BIAS_SPEC = pl.BlockSpec((T, CW), lambda n: (0, 0))


def sg_fwd(proj, gn, sw, bias, name):
    def body(u_ref, v_ref, g_ref, sw_ref, bias_ref, o_ref):
        causal, head_of_col = _sg_masks()
        weights = _sg_weights(sw_ref, causal)
        for wdw in range(SG_WINDOWS):
            rows = pl.ds(wdw * T, T)
            gv = _gelu(v_ref[rows, :])
            rstd = lax.rsqrt(jnp.mean(gv * gv, axis=-1, keepdims=True) + EPS)
            vnb = ((gv * rstd) * g_ref[...]).astype(BF16)
            mixed = _sg_mixed(vnb, weights, bias_ref[...], head_of_col)
            o_ref[rows, :] = _gelu(u_ref[rows, :]) * mixed

    return pl.pallas_call(
        body, name=name, grid=(S // SG_ROWS,),
        in_specs=[SG_U_SPEC, SG_V_SPEC, SG_G_SPEC, SG_W_SPEC, SG_BIAS_SPEC],
        out_specs=SG_ROW_SPEC, out_shape=jax.ShapeDtypeStruct((S, CW), F32),
        compiler_params=_cparams(("parallel",)),
    )(proj, proj, gn, sw, bias)


def sg_bwd(proj, dout, gn, sw, bias, name):
    def body(u_ref, v_ref, do_ref, g_ref, sw_ref, bias_ref, du_ref, dv_ref, dg_ref, dsw_ref, dbias_ref):
        @pl.when(pl.program_id(0) == 0)
        def _():
            dg_ref[...] = jnp.zeros_like(dg_ref)
            dsw_ref[...] = jnp.zeros_like(dsw_ref)
            dbias_ref[...] = jnp.zeros_like(dbias_ref)

        causal, head_of_col = _sg_masks()
        weights = _sg_weights(sw_ref, causal)
        gnv = g_ref[...]
        for wdw in range(SG_WINDOWS):
            rows = pl.ds(wdw * T, T)
            uv, vv, do = u_ref[rows, :], v_ref[rows, :], do_ref[rows, :]
            gv = _gelu(vv)
            rstd = lax.rsqrt(jnp.mean(gv * gv, axis=-1, keepdims=True) + EPS)
            xhat = gv * rstd
            vnb = (xhat * gnv).astype(BF16)
            mixed = _sg_mixed(vnb, weights, bias_ref[...], head_of_col)
            du_ref[rows, :] = (do * mixed) * _gelu_grad(uv)
            dmix = do * _gelu(uv)
            dbias_ref[...] += dmix
            dmixb = dmix.astype(BF16)
            dvn = jnp.zeros((T, CW), F32)
            for h in range(SG_HEADS):
                dvh = lax.dot_general(weights[h], dmixb, (((0,), (0,)), ((), ())), preferred_element_type=F32)
                dvn = dvn + jnp.where(head_of_col == h, dvh, 0.0)
                dmh = jnp.where(head_of_col == h, dmixb, jnp.zeros_like(dmixb))
                dwh = lax.dot_general(dmh, vnb, (((1,), (1,)), ((), ())), preferred_element_type=F32)
                dsw_ref[h] += jnp.where(causal, dwh, 0.0)
            dg_ref[...] += jnp.sum(dvn * xhat, axis=0, keepdims=True)
            dxhat = dvn * gnv
            dgv = rstd * (dxhat - xhat * jnp.mean(dxhat * xhat, axis=-1, keepdims=True))
            dv_ref[rows, :] = dgv * _gelu_grad(vv)

    full = jax.ShapeDtypeStruct((S, CW), F32)
    return pl.pallas_call(
        body, name=name, grid=(S // SG_ROWS,),
        in_specs=[SG_U_SPEC, SG_V_SPEC, SG_DOUT_SPEC, SG_G_SPEC, SG_W_SPEC, SG_BIAS_SPEC],
        out_specs=[SG_ROW_SPEC, SG_ROW_SPEC, SG_G_SPEC, SG_W_SPEC, SG_BIAS_SPEC],
        out_shape=[full, full, jax.ShapeDtypeStruct((1, CW), F32),
                   jax.ShapeDtypeStruct((SG_HEADS, T, T), F32), jax.ShapeDtypeStruct((T, CW), F32)],
        compiler_params=_cparams(("arbitrary",)),
    )(proj, proj, dout, gn, sw, bias)


ADA_COLS = NMOD * D // NDEV


def ada_fwd(c_all, ada_w, ada_b_mine, name):
    def body(c_ref, w_ref, b_ref, o_ref, ca_ref):
        cv = c_ref[...]
        ca = cv * (1.0 / (1.0 + jnp.exp(-cv)))
        ca_ref[...] = ca
        cab = ca.astype(BF16)
        for l in range(L):
            o_ref[l] = jnp.dot(cab, w_ref[l].astype(BF16), preferred_element_type=F32) + b_ref[l]

    return pl.pallas_call(
        body, name=name,
        out_shape=[jax.ShapeDtypeStruct((L, NDEV, ADA_COLS), F32), jax.ShapeDtypeStruct((NDEV, D), F32)],
        compiler_params=_cparams(),
    )(c_all, ada_w, ada_b_mine)


def ada_bwd(ca, dmod_cols, name):
    def body(ca_ref, dm_ref, o_ref):
        cab = ca_ref[...].astype(BF16)
        for l in range(L):
            o_ref[l] = lax.dot_general(cab, dm_ref[l].astype(BF16), (((0,), (0,)), ((), ())),
                                       preferred_element_type=F32)

    return pl.pallas_call(
        body, name=name, out_shape=jax.ShapeDtypeStruct((L, D, ADA_COLS), F32),
        compiler_params=_cparams(),
    )(ca, dmod_cols)


def _adamw(w, g, m, v):
    m = B1 * m + (1.0 - B1) * g
    v = B2 * v + (1.0 - B2) * (g * g)
    m_hat = m / BC1
    v_hat = v / BC2
    delta = -LR * (m_hat / (jnp.sqrt(v_hat) + AEPS) + WD * w)
    return delta, m, v


VEC_ROWS_PER_LAYER = 8
VEC_FINAL_ROW = L * VEC_ROWS_PER_LAYER
VEC_ROWS = VEC_FINAL_ROW + 8
W256_TAPS, W256_CONV_B, W256_GN = 0, 8, 9
W256_ROWS_PER_LAYER = 16


def small_update(vec_all, w256_all, sb_all, sw_all, params, name):
    n_par = len(params)

    def body(*refs):
        vec_ref, w256_ref, sb_ref = refs[:3]
        sw_refs = refs[3:3 + L]
        par_refs = [refs[3 + L + 3 * k:3 + L + 3 * k + 3] for k in range(n_par)]
        out = refs[3 + L + 3 * n_par:]
        out_par = [out[4 * k:4 * k + 4] for k in range(n_par)]
        loss_ref, taps_ref = out[4 * n_par:]

        def total(ref, idx):
            acc = ref[(0,) + idx].astype(F32)
            for d in range(1, NDEV):
                acc = acc + ref[(d,) + idx].astype(F32)
            return acc

        def update(k, region, g):
            w_ref, m_ref, v_ref = par_refs[k]
            g_ref, d_ref, nm_ref, nv_ref = out_par[k]
            delta, nm, nv = _adamw(w_ref[region], g, m_ref[region], v_ref[region])
            g_ref[region] = g
            d_ref[region] = delta
            nm_ref[region] = nm
            nv_ref[region] = nv

        for l in range(L):
            base = l * VEC_ROWS_PER_LAYER
            for k in range(NMOD):
                update(0, (slice(l, l + 1), slice(k * D, (k + 1) * D)), total(vec_ref, (slice(base + k, base + k + 1),)))
            update(1, (slice(l, l + 1),), total(vec_ref, (slice(base + 6, base + 7),)))
            update(2, (slice(l, l + 1),), total(vec_ref, (slice(base + 7, base + 8),)))
            wbase = l * W256_ROWS_PER_LAYER
            update(4, (slice(l, l + 1),), total(w256_ref, (slice(wbase + W256_CONV_B, wbase + W256_CONV_B + 1),)))
            update(5, (slice(l, l + 1),), total(w256_ref, (slice(wbase + W256_GN, wbase + W256_GN + 1),)))
            update(6, (l,), total(sw_refs[l], ()))
            update(7, (l,), total(sb_ref, (slice(l * SG_HEADS, (l + 1) * SG_HEADS),)))
            taps_ref[l] = total(w256_ref, (slice(wbase + W256_TAPS, wbase + W256_TAPS + 8),))
        update(3, (slice(0, 1),), total(vec_ref, (slice(VEC_FINAL_ROW, VEC_FINAL_ROW + 1),)))
        loss_ref[...] = total(vec_ref, (slice(VEC_FINAL_ROW + 1, VEC_FINAL_ROW + 2), slice(0, LANES)))

    out_shape = []
    for w, _, _ in params:
        out_shape += [jax.ShapeDtypeStruct(w.shape, F32)] * 4
    out_shape += [jax.ShapeDtypeStruct((1, LANES), F32), jax.ShapeDtypeStruct((L, 8, CW), F32)]
    outs = pl.pallas_call(body, name=name, out_shape=out_shape, compiler_params=_cparams())(
        vec_all, w256_all, sb_all, *sw_all, *[a for p in params for a in p])
    return [outs[4 * k:4 * k + 4] for k in range(n_par)], outs[4 * n_par:]


def adamw_plain(w, g, m, v, tr, name):
    rows, cols = w.shape
    spec = pl.BlockSpec((tr, cols), lambda i: (i, 0))

    def body(w_ref, g_ref, m_ref, v_ref, d_ref, nm_ref, nv_ref):
        delta, nm, nv = _adamw(w_ref[...], g_ref[...], m_ref[...], v_ref[...])
        d_ref[...] = delta
        nm_ref[...] = nm
        nv_ref[...] = nv

    shp = jax.ShapeDtypeStruct((rows, cols), F32)
    return pl.pallas_call(
        body, name=name, grid=(rows // tr,), in_specs=[spec] * 4, out_specs=[spec] * 3,
        out_shape=[shp, shp, shp], compiler_params=_cparams(("parallel",)),
    )(w, g, m, v)


def adamw_reduce(w, parts, m, v, tr, name, tie=None):
    _, rows, cols = w.shape
    spec = pl.BlockSpec((None, tr, cols), lambda l, i: (l, i, 0))
    pspecs = [pl.BlockSpec((NDEV, tr, cols), lambda l, i, k=k: (0, jnp.where(l == k, i, 0), 0)) for k in range(L)]

    ties = [] if tie is None else [tie]

    def body(w_ref, p0_ref, p1_ref, m_ref, v_ref, *rest):
        g_ref, d_ref, nm_ref, nv_ref = rest[len(ties):]
        first_layer = pl.program_id(0) == 0
        g = jnp.zeros((tr, cols), F32)
        for d in range(NDEV):
            g = g + jnp.where(first_layer, p0_ref[d], p1_ref[d]).astype(F32)
        delta, nm, nv = _adamw(w_ref[...], g, m_ref[...], v_ref[...])
        g_ref[...] = g
        d_ref[...] = delta
        nm_ref[...] = nm
        nv_ref[...] = nv

    shp = jax.ShapeDtypeStruct(w.shape, F32)
    return pl.pallas_call(
        body, name=name, grid=(L, rows // tr),
        in_specs=[spec] + pspecs + [spec, spec] + [pl.BlockSpec(t.shape, lambda l, i: (0, 0)) for t in ties],
        out_specs=[spec] * 4, out_shape=[shp] * 4, compiler_params=_cparams(("parallel", "parallel")),
    )(w, *parts, m, v, *ties)


SHARD_IN = PROJ // NDEV


def shards_to_columns(shards, name):
    tr = 256

    def body(i_ref, o_ref):
        for d in range(NDEV):
            o_ref[:, d * SHARD_IN:(d + 1) * SHARD_IN] = i_ref[d]

    return pl.pallas_call(
        body, name=name, grid=(D // tr,),
        in_specs=[pl.BlockSpec((NDEV, tr, SHARD_IN), lambda i: (0, i, 0))],
        out_specs=pl.BlockSpec((tr, PROJ), lambda i: (i, 0)),
        out_shape=jax.ShapeDtypeStruct((D, PROJ), shards.dtype), compiler_params=_cparams(("parallel",)),
    )(shards)


def columns_to_shards(mat, name):
    tr = 256

    def body(i_ref, o_ref):
        for d in range(NDEV):
            o_ref[d] = i_ref[:, d * SHARD_IN:(d + 1) * SHARD_IN]

    return pl.pallas_call(
        body, name=name, grid=(D // tr,),
        in_specs=[pl.BlockSpec((tr, PROJ), lambda i: (i, 0))],
        out_specs=pl.BlockSpec((NDEV, tr, SHARD_IN), lambda i: (0, i, 0)),
        out_shape=jax.ShapeDtypeStruct((NDEV, D, SHARD_IN), mat.dtype), compiler_params=_cparams(("parallel",)),
    )(mat)


def _pad_rows(flat, rows):
    return jnp.pad(flat, (0, rows * LANES - flat.shape[0])).reshape(rows, LANES)


def kernel(x, c, ada_w, ada_b, norm_mix_g, norm_mlp_g, w_in, conv_w, conv_b, gmlp_norm_g, spatial_w, spatial_b, w_out, mlp_w1, mlp_w2, final_norm_g, loss_target, m_ada_w, m_ada_b, m_norm_mix_g, m_norm_mlp_g, m_w_in, m_conv_w, m_conv_b, m_gmlp_norm_g, m_spatial_w, m_spatial_b, m_w_out, m_mlp_w1, m_mlp_w2, m_final_norm_g, v_ada_w, v_ada_b, v_norm_mix_g, v_norm_mlp_g, v_w_in, v_conv_w, v_conv_b, v_gmlp_norm_g, v_spatial_w, v_spatial_b, v_w_out, v_mlp_w1, v_mlp_w2, v_final_norm_g):
    me = _lin(_my_pos())
    x0 = x[0]
    target = loss_target[0]
    conv_shard = conv_w.shape[-1]

    w_in_b, w_out_b, w1_b, w2_b = [w.astype(BF16) for w in (w_in, w_out, mlp_w1, mlp_w2)]
    pack0 = _pad_rows(jnp.concatenate([c.reshape(-1), conv_w.reshape(-1)]), 16)
    g0, gw_in0 = run_comm(Gather([pack0, w_in_b[0]]), "gather_first")
    g0 = g0.reshape(NDEV, 16 * LANES)
    c_all = g0[:, :D]
    conv_full = (g0[:, D:D + L * 3 * conv_shard].reshape(NDEV, L, 3, conv_shard)
                 .transpose(1, 2, 0, 3).reshape(L, 3, CW))


    W_in = [shards_to_columns(gw_in0, "w_in_columns0"), None]
    W_out, W1, W2 = [None] * L, [None] * L, [None] * L

    ada_b_mine = lax.dynamic_slice(ada_b, (0, me * ADA_COLS), (L, ADA_COLS)).reshape(L, 1, ADA_COLS)
    mod_part, c_act = ada_fwd(c_all, ada_w, ada_b_mine, "ada_fwd")
    gmod = run_comm(Gather([mod_part]), "gather_mod")[0]
    mod = lax.dynamic_index_in_dim(gmod, me, axis=2, keepdims=False)
    mod = mod.transpose(1, 0, 2).reshape(L, NMOD, 1, D)
    early_weights, token = start_copies([w_out_b[0]], me, "gather_early0_start", True, after=gmod)
    mod = tied(mod, token)

    cw8 = jnp.pad(conv_full, ((0, 0), (0, 5), (0, 0)))
    sg_bias = jnp.repeat(spatial_b.transpose(0, 2, 1), HD, axis=2)

    saved = []
    xl = x0
    for l in range(L):
        sh_m, sc_m, g_m, sh_f, sc_f, g_f = [mod[l, k] for k in range(NMOD)]
        h1 = normmod_fwd(xl, norm_mix_g[l:l + 1], sc_m, sh_m, f"norm_mix_fwd{l}")
        if l > 0:
            W_in[l] = shards_to_columns(finish_copies(w_in_handle, xl, f"gather_w_in{l}_wait")[0],
                                        f"w_in_columns{l}")
        qkv = mm_layer("proj_qkv", l, h1, W_in[l], out_dtypes=[BF16], cols=(0, QKV))[0]
        proj = mm_layer("proj_rest", l, h1, W_in[l], out_dtypes=[F32], cols=(QKV, REST))[0]
        a_out, a_tot, gw2, gw1 = attn_fwd(qkv, f"attn_fwd{l}", comm=Gather([w2_b[l], w1_b[l]]))
        gw_out, = finish_copies(early_weights, a_out, f"gather_early{l}_wait")
        W_out[l] = gw_out.reshape(D, D)
        W1[l] = gw1
        W2[l] = gw2.reshape(DFF, D)
        if l + 1 < L:
            w_in_handle, token = start_copies([w_in_b[l + 1]], me, f"gather_w_in{l + 1}_start", True, after=a_out)
            early_weights, token = start_copies([w_out_b[l + 1]], me, f"gather_early{l + 1}_start", True, after=token)
            g_m = tied(g_m, token)
        c_out = conv_fwd(proj, cw8[l], conv_b[l:l + 1], f"conv_fwd{l}")
        s_out = sg_fwd(proj, gmlp_norm_g[l:l + 1], spatial_w[l], sg_bias[l], f"sg_fwd{l}")
        cat = jnp.concatenate([a_out, c_out.astype(BF16), s_out.astype(BF16)], axis=1)
        mix, x1, h2 = mm_layer("mix", l, cat, W_out[l], out_dtypes=[F32, F32, BF16], epilogue=_residual_then_norm,
                               extras=[(xl, "tile"), (g_m, "col"), (norm_mlp_g[l:l + 1], "col"), (sc_f, "col"),
                                       (sh_f, "col")])
        ra, r = mm_layer("mlp_up", l, h2, W1[l], out_dtypes=[BF16, BF16], b_blocks=True,
                         epilogue=lambda acc: (jnp.maximum(acc, 0.0), jnp.square(jnp.maximum(acc, 0.0))))
        m2, x2 = mm_layer("mlp_down", l, r, W2[l], out_dtypes=[F32, F32],
                          epilogue=lambda acc, xr, g: (acc, xr + g * acc), extras=[(x1, "tile"), (g_f, "col")])
        saved.append(dict(x=xl, h1=h1, proj=proj, qkv=qkv, a_tot=a_tot, cat=cat, mix=mix,
                          x1=x1, h2=h2, ra=ra, r=r, m2=m2))
        xl = x2

    dx, loss_part, d_final_g, dm2, dg_f = loss_head(xl, target, final_norm_g.reshape(1, D),
                                                    (saved[L - 1]["m2"], mod[L - 1, NMOD - 1]), "loss_head")

    p_in, p_out, p_w1, p_w2 = [None] * L, [None] * L, [None] * L, [None] * L
    w_in_grads = [None] * L
    vec_rows, d_norm_mix, d_norm_mlp = [None] * L, [None] * L, [None] * L
    dcw8, d_conv_b, d_gn, d_sw, d_sb = [None] * L, [None] * L, [None] * L, [None] * L, [None] * L
    late_grads = [None] * L
    for l in reversed(range(L)):
        sv = saved[l]
        sh_m, sc_m, g_m, sh_f, sc_f, g_f = [mod[l, k] for k in range(NMOD)]
        da = mm_layer("mlp_down_dgrad", l, dm2, W2[l], out_dtypes=[BF16], trans_b=True,
                      epilogue=lambda acc, rav: (acc * (2.0 * rav.astype(F32)),), extras=[(sv["ra"], "tile")])[0]
        dW2 = mm_layer("mlp_down_wgrad", l, sv["r"], dm2, out_dtypes=[BF16], trans_a=True)[0]
        dW1 = mm_layer("mlp_up_wgrad", l, sv["h2"], da, out_dtypes=[BF16], trans_a=True, out_blocks=True)[0]
        dh2 = mm_layer("mlp_up_dgrad", l, da, W1[l], out_dtypes=[F32], trans_b=True, b_blocks=True)[0]
        dx1, dsc_f, dsh_f, d_norm_mlp[l], dmix, dg_m = normmod_bwd(
            sv["x1"], dh2, dx, norm_mlp_g[l:l + 1], sc_f, f"norm_mlp_bwd{l}", gate_next=(sv["mix"], g_m))
        dcat = mm_layer("mix_dgrad", l, dmix, W_out[l], out_dtypes=[F32], trans_b=True)[0]
        dW_out = mm_layer("mix_wgrad", l, sv["cat"], dmix, out_dtypes=[BF16], trans_a=True)[0]
        pieces_w2, pieces_out = dW2.reshape(NDEV, DFF // NDEV, D), dW_out.reshape(NDEV, D // NDEV, D)
        ride, late = ([pieces_w2, pieces_out], dW1) if l == L - 1 else ([pieces_w2, dW1], pieces_out)
        dq, dk, dv, *arrived = attn_bwd(sv["qkv"], dcat, sv["a_tot"], f"attn_bwd{l}", comm=Exchange(ride))
        p_w2[l] = arrived[0]
        (p_out if l == L - 1 else p_w1)[l] = arrived[1]
        late_grads[l], late_token = start_copies([late], me, f"exchange_late{l}_start", False, after=dq)
        dbg, dcg, dhc, dcw8[l], d_conv_b[l] = conv_bwd(sv["proj"], dcat, cw8[l], conv_b[l:l + 1], f"conv_bwd{l}")
        dus, dvs, d_gn[l], dsw, dbias = sg_bwd(sv["proj"], dcat, gmlp_norm_g[l:l + 1], spatial_w[l], sg_bias[l],
                                               f"sg_bwd{l}")
        d_sw[l] = dsw.astype(BF16)
        d_sb[l] = dbias.reshape(T, SG_HEADS, HD).sum(axis=2).T
        dproj = jnp.concatenate([dq, dk, dv, dbg, dcg, dhc, dus, dvs], axis=1).astype(BF16)
        dW_in = mm_layer("proj_wgrad", l, sv["h1"], dproj, out_dtypes=[BF16], trans_a=True,
                         extras=[(late_token, "tie")])[0]
        pieces = columns_to_shards(dW_in, f"w_in_grad_shards{l}")
        w_in_grads[l], token = start_copies([pieces], me, f"exchange_w_in{l}_start", False)
        dh1 = mm_layer("proj_dgrad", l, dproj, W_in[l], out_dtypes=[F32], trans_b=True, extras=[(token, "tie")])[0]
        below = (saved[l - 1]["m2"], mod[l - 1, NMOD - 1]) if l > 0 else None
        dx, dsc_m, dsh_m, d_norm_mix[l], *gated_below = normmod_bwd(
            sv["x"], dh1, dx1, tied(norm_mix_g[l:l + 1], token), sc_m, f"norm_mix_bwd{l}", gate_next=below)
        vec_rows[l] = [dsh_m, dsc_m, dg_m, dsh_f, dsc_f, dg_f, d_norm_mix[l], d_norm_mlp[l]]
        if l > 0:
            dm2, dg_f = gated_below

    grad_x = dx.reshape(1, S, D)

    g_w2, d_w2, nm_w2, nv_w2 = adamw_reduce(mlp_w2, p_w2, m_mlp_w2, v_mlp_w2, 256, "adamw_mlp_w2", tie=token)
    p_w1[L - 1] = finish_copies(late_grads[L - 1], d_w2, f"exchange_late{L - 1}_wait")[0]
    g_w1, d_w1, nm_w1, nv_w1 = adamw_reduce(mlp_w1, p_w1, m_mlp_w1, v_mlp_w1, 256, "adamw_mlp_w1", tie=token)

    vec_pack = jnp.concatenate([row for l in range(L) for row in vec_rows[l]]
                               + [d_final_g, loss_part, jnp.zeros((VEC_ROWS - VEC_FINAL_ROW - 2, D), F32)], axis=0)
    vec_pack, _ = lax.optimization_barrier((vec_pack, (d_w1, d_w2)))
    w256_pack = jnp.concatenate([blk for l in range(L) for blk in (
        dcw8[l], d_conv_b[l], d_gn[l], jnp.zeros((W256_ROWS_PER_LAYER - W256_GN - 1, CW), F32))], axis=0)
    vec_all, w256_all, sb_all, *sw_all = run_comm(
        Gather([vec_pack, w256_pack, jnp.concatenate(d_sb, axis=0)] + d_sw), "gather_small_grads")

    dmod_all = (vec_all[:, :VEC_FINAL_ROW].reshape(NDEV, L, VEC_ROWS_PER_LAYER, D)[:, :, :NMOD]
                .reshape(NDEV, L, NMOD * D))
    dmod_cols = lax.dynamic_slice(dmod_all, (0, 0, me * ADA_COLS), (NDEV, L, ADA_COLS)).transpose(1, 0, 2)
    g_ada_w = ada_bwd(c_act, dmod_cols, "ada_bwd")

    flat2 = lambda t: t.reshape(L * D, ADA_COLS)
    d_ada_w, nm_ada_w, nv_ada_w = [t.reshape(L, D, ADA_COLS) for t in adamw_plain(
        flat2(ada_w), flat2(g_ada_w), flat2(m_ada_w), flat2(v_ada_w), 256, "adamw_ada_w")]

    after = jnp.concatenate([t.reshape(-1)[:1] for t in (d_w1, d_w2, d_ada_w)])
    p_in = [finish_copies(w_in_grads[l], after, f"exchange_w_in{l}_wait")[0] for l in range(L)]
    p_out[0] = finish_copies(late_grads[0], after, "exchange_late0_wait")[0]
    g_w_in, d_w_in, nm_w_in, nv_w_in = adamw_reduce(w_in, p_in, m_w_in, v_w_in, 256, "adamw_w_in")
    g_w_out, d_w_out, nm_w_out, nv_w_out = adamw_reduce(w_out, p_out, m_w_out, v_w_out, 128, "adamw_w_out")

    as_row = lambda t: t.reshape(1, D)
    small_params = [(ada_b, m_ada_b, v_ada_b), (norm_mix_g, m_norm_mix_g, v_norm_mix_g),
                    (norm_mlp_g, m_norm_mlp_g, v_norm_mlp_g),
                    (as_row(final_norm_g), as_row(m_final_norm_g), as_row(v_final_norm_g)),
                    (conv_b, m_conv_b, v_conv_b), (gmlp_norm_g, m_gmlp_norm_g, v_gmlp_norm_g),
                    (spatial_w, m_spatial_w, v_spatial_w), (spatial_b, m_spatial_b, v_spatial_b)]
    updated, (loss_sum, taps_sum) = small_update(vec_all, w256_all, sb_all, sw_all, small_params, "small_update")
    loss = loss_sum[0, 0]
    u_ada_b, u_norm_mix, u_norm_mlp, u_final, u_conv_b, u_gn, u_sw, u_sb = updated
    u_final = [t.reshape(D) for t in u_final]
    g_conv_w = lax.dynamic_slice(taps_sum, (0, 0, me * conv_shard), (L, 3, conv_shard))
    flat_cw = lambda t: t.reshape(L * 3, conv_shard)
    u_conv_w = [g_conv_w] + [t.reshape(L, 3, conv_shard) for t in adamw_plain(
        flat_cw(conv_w), flat_cw(g_conv_w), flat_cw(m_conv_w), flat_cw(v_conv_w), L * 3, "adamw_conv_w")]
    small_sets = [u_ada_b, u_norm_mix, u_norm_mlp, u_conv_w, u_conv_b, u_gn, u_sw, u_sb, u_final]
    small_g, sd, snm, snv = [[u[k] for u in small_sets] for k in range(4)]

    def ordered(big, small):
        ada, win, wout, w1, w2 = big
        return [ada, small[0], small[1], small[2], win, small[3], small[4], small[5], small[6], small[7],
                wout, w1, w2, small[8]]

    grads = ordered([g_ada_w, g_w_in, g_w_out, g_w1, g_w2], small_g)
    deltas = ordered([d_ada_w, d_w_in, d_w_out, d_w1, d_w2], sd)
    new_m = ordered([nm_ada_w, nm_w_in, nm_w_out, nm_w1, nm_w2], snm)
    new_v = ordered([nv_ada_w, nv_w_in, nv_w_out, nv_w1, nv_w2], snv)
    return (loss, grad_x, *grads, *deltas, *new_m, *new_v)
```

```python
import functools
import math

import jax
import jax.numpy as jnp
from jax import lax
from jax.experimental import pallas as pl
from jax.experimental.pallas import tpu as pltpu

F32 = jnp.float32
BF16 = jnp.bfloat16
MESH = pl.DeviceIdType.MESH

S = 2048
D = 1024
L = 2
NDEV = 8
HD = 64
NH = 8
PROJ = 2816
DFF = 4096
NMOD = 6
EPS = 1e-6
T = 128
SG_HEADS = 4
LANES = 128
CW = 256
QKV = 3 * NH * HD
REST = PROJ - QKV

LR, B1, B2, AEPS, WD, STEP = 0.001, 0.9, 0.999, 1e-08, 0.01, 10
BC1 = 1.0 - B1 ** STEP
BC2 = 1.0 - B2 ** STEP

VMEM_LIMIT = 48 * 1024 * 1024

HBM_SPEC = pl.BlockSpec(memory_space=pltpu.HBM)


def _cparams(sem=None):
    return pltpu.CompilerParams(dimension_semantics=sem, vmem_limit_bytes=VMEM_LIMIT)


def _my_pos():
    return lax.axis_index("x"), lax.axis_index("y"), lax.axis_index("c")


def _lin(p):
    return 4 * p[0] + 2 * p[1] + p[2]


class Gather:
    def __init__(self, arrs):
        self.arrs = list(arrs)
        n = len(self.arrs)
        self.out_shape = [jax.ShapeDtypeStruct((NDEV,) + a.shape, a.dtype) for a in self.arrs]
        self.scratch = [pltpu.SemaphoreType.DMA((n, 7)), pltpu.SemaphoreType.DMA((n, 7)),
                        pltpu.SemaphoreType.DMA((n,))]

    def phases(self, ins, outs, sems):
        n = len(self.arrs)
        send_sems, recv_sems, local_sems = sems
        x, y, c = _my_pos()
        me, sibling = (x, y, c), (x, y, 1 - c)
        chips = [(1 - x, y), (x, 1 - y), (1 - x, 1 - y)]

        def copy(a, k, block, to, src=None):
            slot = outs[a].at[_lin(block)]
            return pltpu.make_async_remote_copy(
                src_ref=slot if src is None else src, dst_ref=slot,
                send_sem=send_sems.at[a, k], recv_sem=recv_sems.at[a, k],
                device_id=to, device_id_type=MESH)

        def mine(a):
            return pltpu.make_async_copy(ins[a], outs[a].at[_lin(me)], local_sems.at[a])

        def first(a):
            return [copy(a, 0, me, sibling, src=ins[a])] + [
                copy(a, 1 + j, me, (*chip, c), src=ins[a]) for j, chip in enumerate(chips)]

        def passed(a):
            return [copy(a, 4 + j, (*chip, c), sibling) for j, chip in enumerate(chips)]

        def start():
            for a in range(n):
                mine(a).start()
                for cp in first(a):
                    cp.start()

        def relay():
            for j, chip in enumerate(chips):
                for a in range(n):
                    copy(a, 1 + j, (*chip, c), me).wait_recv()
                    passed(a)[j].start()

        def finish():
            for a in range(n):
                copy(a, 0, sibling, me).wait_recv()
            for j, chip in enumerate(chips):
                for a in range(n):
                    copy(a, 4 + j, (*chip, 1 - c), me).wait_recv()
            for a in range(n):
                for cp in first(a) + passed(a):
                    cp.wait_send()
                mine(a).wait()

        return start, relay, finish


class Exchange:
    def __init__(self, arrs):
        self.arrs = list(arrs)
        n = len(self.arrs)
        self.out_shape = [jax.ShapeDtypeStruct(a.shape, a.dtype) for a in self.arrs]
        self.scratch = [pltpu.SemaphoreType.DMA((n, 7)), pltpu.SemaphoreType.DMA((n, 7)),
                        pltpu.SemaphoreType.DMA((n,))]

    def phases(self, ins, outs, sems):
        n = len(self.arrs)
        send_sems, recv_sems, local_sems = sems
        x, y, c = _my_pos()
        me = (x, y, c)

        def peer(mask):
            return (1 - x if mask & 4 else x, 1 - y if mask & 2 else y, 1 - c if mask & 1 else c)

        def copy(a, mask):
            return pltpu.make_async_remote_copy(
                src_ref=ins[a].at[_lin(peer(mask))], dst_ref=outs[a].at[_lin(me)],
                send_sem=send_sems.at[a, mask - 1], recv_sem=recv_sems.at[a, mask - 1],
                device_id=peer(mask), device_id_type=MESH)

        def arrival(a, mask):
            return pltpu.make_async_remote_copy(
                src_ref=ins[a].at[_lin(me)], dst_ref=outs[a].at[_lin(peer(mask))],
                send_sem=send_sems.at[a, mask - 1], recv_sem=recv_sems.at[a, mask - 1],
                device_id=peer(mask), device_id_type=MESH)

        def mine(a):
            return pltpu.make_async_copy(ins[a].at[_lin(me)], outs[a].at[_lin(me)], local_sems.at[a])

        def start():
            for a in range(n):
                mine(a).start()
            for mask in (4, 2, 6, 1, 5, 3, 7):
                for a in range(n):
                    copy(a, mask).start()

        def relay():
            pass

        def finish():
            for mask in range(1, 8):
                for a in range(n):
                    arrival(a, mask).wait_recv()
            for mask in range(1, 8):
                for a in range(n):
                    copy(a, mask).wait_send()
            for a in range(n):
                mine(a).wait()

        return start, relay, finish


def run_comm(plan, name):
    n = len(plan.arrs)

    def body(*refs):
        start, relay, finish = plan.phases(refs[:n], refs[n:2 * n], refs[2 * n:])
        start()
        relay()
        finish()

    outs = pl.pallas_call(
        body, name=name, out_shape=plan.out_shape,
        in_specs=[HBM_SPEC] * n, out_specs=[HBM_SPEC] * n, scratch_shapes=plan.scratch,
    )(*plan.arrs)
    return list(outs)


SEM_SPEC = pl.BlockSpec(memory_space=pltpu.SEMAPHORE)
DATAFLOW = pltpu.SideEffectType.DATAFLOW_SIDE_EFFECTING


def _peer_copies(src_ref, land_ref, send_sems, recv_sems, first, same_block):
    x, y, c = _my_pos()
    me = (x, y, c)
    sends, arrivals = [], []
    for mask in (4, 2, 6, 1, 5, 3, 7):
        peer = (1 - x if mask & 4 else x, 1 - y if mask & 2 else y, 1 - c if mask & 1 else c)
        sends.append(pltpu.make_async_remote_copy(
            src_ref=src_ref if same_block else src_ref.at[_lin(peer)], dst_ref=land_ref.at[_lin(me)],
            send_sem=send_sems.at[first + mask - 1], recv_sem=recv_sems.at[first + mask - 1], device_id=peer,
            device_id_type=MESH))
        arrivals.append(pltpu.make_async_remote_copy(
            src_ref=src_ref if same_block else src_ref.at[_lin(me)], dst_ref=land_ref.at[_lin(peer)],
            send_sem=send_sems.at[first + mask - 1], recv_sem=recv_sems.at[first + mask - 1], device_id=peer,
            device_id_type=MESH))
    return sends, arrivals


def start_copies(srcs, me, name, same_block, after=None):
    n = len(srcs)
    landings = []
    for src in srcs:
        own = src[None] if same_block else lax.dynamic_index_in_dim(src, me, axis=0, keepdims=True)
        landings.append(lax.dynamic_update_slice(lax.empty((NDEV,) + own.shape[1:], src.dtype), own,
                                                 (me,) + (0,) * (own.ndim - 1)))

    def body(*refs):
        send_sems, recv_sems = refs[-2 * n - 3], refs[-2 * n - 2]
        token = refs[-1]
        for k in range(n):
            sends, _ = _peer_copies(refs[2 * k], refs[2 * k + 1], send_sems, recv_sems, 7 * k, same_block)
            for cp in sends:
                cp.start()
        token[...] = jnp.zeros_like(token)

    hbm = lambda a: pltpu.HBM(a.shape, a.dtype)
    pairs = [a for pair in zip(srcs, landings) for a in pair]
    extra = [] if after is None else [after]
    sems = pltpu.SemaphoreType.DMA((7 * n,))
    send_sems, recv_sems, *thru, token = pl.pallas_call(
        body, name=name,
        out_shape=(sems, sems, *[hbm(a) for a in pairs], jax.ShapeDtypeStruct((8, LANES), F32)),
        in_specs=[HBM_SPEC] * (2 * n) + [pl.BlockSpec(memory_space=pl.ANY)] * len(extra),
        out_specs=(SEM_SPEC, SEM_SPEC, *[HBM_SPEC] * (2 * n), pl.BlockSpec(memory_space=pltpu.VMEM)),
        input_output_aliases={k: 2 + k for k in range(2 * n)},
        compiler_params=pltpu.CompilerParams(has_side_effects=DATAFLOW),
    )(*[pltpu.with_memory_space_constraint(a, pltpu.HBM) for a in pairs], *extra)
    return (send_sems, recv_sems, thru, same_block), token


def finish_copies(handle, after, name):
    send_sems, recv_sems, thru, same_block = handle
    n = len(thru) // 2

    def body(*refs):
        send_sems, recv_sems = refs[2 * n], refs[2 * n + 1]
        for k in range(n):
            sends, arrivals = _peer_copies(refs[2 * k], refs[2 * k + 1], send_sems, recv_sems, 7 * k, same_block)
            for cp in sends:
                cp.wait_send()
            for cp in arrivals:
                cp.wait_recv()

    hbm = lambda a: pltpu.HBM(a.shape, a.dtype)
    outs = pl.pallas_call(
        body, name=name, out_shape=tuple(hbm(a) for a in thru),
        in_specs=[HBM_SPEC] * (2 * n) + [SEM_SPEC, SEM_SPEC, pl.BlockSpec(memory_space=pl.ANY)],
        out_specs=tuple([HBM_SPEC] * (2 * n)), input_output_aliases={k: k for k in range(2 * n)},
        compiler_params=pltpu.CompilerParams(has_side_effects=DATAFLOW),
    )(*thru, send_sems, recv_sems, after)
    return [outs[2 * k + 1] for k in range(n)]


def tied(x, token):
    return x + token[0:1, 0:1].astype(x.dtype)


MM_TILES = {
    "proj_qkv": (S, 512), "proj_rest": (S, 256), "mix": (512, D), "mlp_up": (S, 512), "mlp_down": (1024, 256),
    "mlp_down_dgrad": (S, 1024), "mlp_down_wgrad": (1024, 1024), "mlp_up_wgrad": (1024, 512),
    "mlp_up_dgrad": (1024, 512), "mix_dgrad": (1024, 512), "mix_wgrad": (512, 1024),
    "proj_wgrad": (1024, PROJ // 2), "proj_dgrad": (1024, 512),
}


def mm_layer(kind, l, a, b, **kw):
    tm, tn = MM_TILES[kind]
    return mm(a, b, tm=tm, tn=tn, name=f"{kind}{l}", **kw)


def mm(a, b, *, tm, tn, out_dtypes, epilogue=None, extras=(), name, trans_a=False, trans_b=False,
       cols=None, b_blocks=False, out_blocks=False):
    if trans_a:
        kdim, m = a.shape
    else:
        m, kdim = a.shape
    shard = b.shape[-1] if b_blocks else None
    if b_blocks:
        full = (b.shape[1], NDEV * shard)
    else:
        full = b.shape
    first, ncols = cols if cols is not None else (0, full[0] if trans_b else full[1])
    assert full[1 if trans_b else 0] == kdim and m % tm == 0 and ncols % tn == 0 and first % tn == 0
    j0 = first // tn
    if trans_a:
        a_spec = pl.BlockSpec((kdim, tm), lambda i, j: (0, i))
    else:
        a_spec = pl.BlockSpec((tm, kdim), lambda i, j: (i, 0))
    if b_blocks and trans_b:
        b_spec = pl.BlockSpec((NDEV, tn, shard), lambda i, j: (0, j0 + j, 0))
    elif b_blocks:
        assert tn == shard
        b_spec = pl.BlockSpec((None, kdim, tn), lambda i, j: (j0 + j, 0, 0))
    elif trans_b:
        b_spec = pl.BlockSpec((tn, kdim), lambda i, j: (j0 + j, 0))
    else:
        b_spec = pl.BlockSpec((kdim, tn), lambda i, j: (0, j0 + j))
    if out_blocks:
        assert tn * NDEV == ncols
        out_spec = pl.BlockSpec((None, tm, tn), lambda i, j: (j, i, 0))
        out_dims = (NDEV, m, tn)
    else:
        out_spec = pl.BlockSpec((tm, tn), lambda i, j: (i, j))
        out_dims = (m, ncols)
    ex_specs = []
    for arr, kind in extras:
        if kind == "tile":
            ex_specs.append(pl.BlockSpec((tm, tn), lambda i, j: (i, j)))
        elif kind == "col":
            ex_specs.append(pl.BlockSpec((1, tn), lambda i, j: (0, j)))
        else:
            ex_specs.append(pl.BlockSpec(arr.shape, lambda i, j: (0, 0)))
    n_ex, n_out = len(extras), len(out_dtypes)
    used = [k for k, (_, kind) in enumerate(extras) if kind != "tie"]

    def body(a_ref, b_ref, *rest):
        ex_refs, out_refs = rest[:n_ex], rest[n_ex:]
        if trans_a:
            acc = lax.dot_general(a_ref[...], b_ref[...], (((0,), (0,)), ((), ())),
                                  preferred_element_type=F32)
        elif trans_b and b_blocks:
            acc = jnp.zeros((tm, tn), F32)
            for d in range(NDEV):
                acc = acc + lax.dot_general(a_ref[:, d * shard:(d + 1) * shard], b_ref[d],
                                            (((1,), (1,)), ((), ())), preferred_element_type=F32)
        elif trans_b:
            acc = lax.dot_general(a_ref[...], b_ref[...], (((1,), (1,)), ((), ())),
                                  preferred_element_type=F32)
        else:
            acc = jnp.dot(a_ref[...], b_ref[...], preferred_element_type=F32)
        outs = (acc,) if epilogue is None else epilogue(acc, *[ex_refs[k][...] for k in used])
        for o_ref, val in zip(out_refs, outs):
            o_ref[...] = val.astype(o_ref.dtype)

    outs = pl.pallas_call(
        body, name=name, grid=(m // tm, ncols // tn),
        in_specs=[a_spec, b_spec] + ex_specs,
        out_specs=[out_spec for _ in range(n_out)],
        out_shape=[jax.ShapeDtypeStruct(out_dims, dt) for dt in out_dtypes],
        compiler_params=_cparams(("parallel", "parallel")),
    )(a, b, *[arr for arr, _ in extras])
    return list(outs)


TR = 512

ROW_SPEC = pl.BlockSpec((TR, D), lambda i: (i, 0))
VEC_SPEC = pl.BlockSpec((1, D), lambda i: (0, 0))


def _residual_then_norm(acc, xr, gate, g, sc, sh):
    x_new = xr + gate * acc
    rstd = lax.rsqrt(jnp.mean(x_new * x_new, axis=-1, keepdims=True) + EPS)
    return acc, x_new, ((x_new * rstd) * g) * (1.0 + sc) + sh


def normmod_fwd(x, g, sc, sh, name):
    def body(x_ref, g_ref, sc_ref, sh_ref, o_ref):
        xv = x_ref[...]
        rstd = lax.rsqrt(jnp.mean(xv * xv, axis=-1, keepdims=True) + EPS)
        n = (xv * rstd) * g_ref[...]
        o_ref[...] = (n * (1.0 + sc_ref[...]) + sh_ref[...]).astype(o_ref.dtype)

    return pl.pallas_call(
        body, name=name, grid=(S // TR,),
        in_specs=[ROW_SPEC, VEC_SPEC, VEC_SPEC, VEC_SPEC], out_specs=ROW_SPEC,
        out_shape=jax.ShapeDtypeStruct((S, D), BF16),
        compiler_params=_cparams(("parallel",)),
    )(x, g, sc, sh)


def _gate_next(dxv, refs):
    br_ref, gate_ref, dbr_ref, dgate_ref = refs

    @pl.when(pl.program_id(0) == 0)
    def _():
        dgate_ref[...] = jnp.zeros_like(dgate_ref)

    dbr_ref[...] = (dxv * gate_ref[...]).astype(dbr_ref.dtype)
    dgate_ref[...] += jnp.sum(dxv * br_ref[...], axis=0, keepdims=True)


GATE_NEXT_IN = [ROW_SPEC, VEC_SPEC]
GATE_NEXT_OUT = [ROW_SPEC, VEC_SPEC]
GATE_NEXT_SHAPES = [jax.ShapeDtypeStruct((S, D), BF16), jax.ShapeDtypeStruct((1, D), F32)]


def normmod_bwd(x, dh, dres, g, sc, name, gate_next=None):
    nxt = 2 if gate_next else 0

    def body(x_ref, dh_ref, dres_ref, g_ref, sc_ref, *rest):
        nxt_in, (dx_ref, dsc_ref, dsh_ref, dg_ref), nxt_out = rest[:nxt], rest[nxt:nxt + 4], rest[nxt + 4:]

        @pl.when(pl.program_id(0) == 0)
        def _():
            dsc_ref[...] = jnp.zeros_like(dsc_ref)
            dsh_ref[...] = jnp.zeros_like(dsh_ref)
            dg_ref[...] = jnp.zeros_like(dg_ref)

        xv, dh = x_ref[...], dh_ref[...]
        gv = g_ref[...]
        rstd = lax.rsqrt(jnp.mean(xv * xv, axis=-1, keepdims=True) + EPS)
        xhat = xv * rstd
        dn = dh * (1.0 + sc_ref[...])
        dxhat = dn * gv
        dxv = dres_ref[...] + rstd * (dxhat - xhat * jnp.mean(dxhat * xhat, axis=-1, keepdims=True))
        dx_ref[...] = dxv
        dsc_ref[...] += jnp.sum(dh * (xhat * gv), axis=0, keepdims=True)
        dsh_ref[...] += jnp.sum(dh, axis=0, keepdims=True)
        dg_ref[...] += jnp.sum(dn * xhat, axis=0, keepdims=True)
        if gate_next:
            _gate_next(dxv, nxt_in + nxt_out)

    vec_out = jax.ShapeDtypeStruct((1, D), F32)
    on = bool(gate_next)
    return pl.pallas_call(
        body, name=name, grid=(S // TR,),
        in_specs=[ROW_SPEC, ROW_SPEC, ROW_SPEC, VEC_SPEC, VEC_SPEC] + GATE_NEXT_IN * on,
        out_specs=[ROW_SPEC, VEC_SPEC, VEC_SPEC, VEC_SPEC] + GATE_NEXT_OUT * on,
        out_shape=[jax.ShapeDtypeStruct((S, D), F32), vec_out, vec_out, vec_out] + GATE_NEXT_SHAPES * on,
        compiler_params=_cparams(("arbitrary",)),
    )(x, dh, dres, g, sc, *(gate_next or ()))


def loss_head(x, target, g, gate_next, name):
    def body(x_ref, t_ref, g_ref, br_ref, gate_ref, dx_ref, loss_ref, dg_ref, dbr_ref, dgate_ref):
        @pl.when(pl.program_id(0) == 0)
        def _():
            loss_ref[...] = jnp.zeros_like(loss_ref)
            dg_ref[...] = jnp.zeros_like(dg_ref)

        xv, gv = x_ref[...], g_ref[...]
        rstd = lax.rsqrt(jnp.mean(xv * xv, axis=-1, keepdims=True) + EPS)
        xhat = xv * rstd
        err = xhat * gv - t_ref[...]
        loss_ref[...] += jnp.sum(err * err) * (0.5 / D)
        dy = err * (1.0 / D)
        dg_ref[...] += jnp.sum(dy * xhat, axis=0, keepdims=True)
        dxhat = dy * gv
        dxv = rstd * (dxhat - xhat * jnp.mean(dxhat * xhat, axis=-1, keepdims=True))
        dx_ref[...] = dxv
        _gate_next(dxv, (br_ref, gate_ref, dbr_ref, dgate_ref))

    return pl.pallas_call(
        body, name=name, grid=(S // TR,),
        in_specs=[ROW_SPEC, ROW_SPEC, VEC_SPEC] + GATE_NEXT_IN,
        out_specs=[ROW_SPEC, VEC_SPEC, VEC_SPEC] + GATE_NEXT_OUT,
        out_shape=[jax.ShapeDtypeStruct((S, D), F32), jax.ShapeDtypeStruct((1, D), F32),
                   jax.ShapeDtypeStruct((1, D), F32)] + GATE_NEXT_SHAPES,
        compiler_params=_cparams(("arbitrary",)),
    )(x, target, g, *gate_next)


TQ = 512
RS = 128
NSUB = TQ // RS
TK = 128


def _dot_hilo(a, tri_twice):
    hi = a.astype(BF16)
    lo = (a - hi.astype(F32)).astype(BF16)
    return jnp.dot(jnp.concatenate([hi, lo], axis=1), tri_twice, preferred_element_type=F32)


def _log_stay(z):
    neg = -z
    return jnp.minimum(neg, 0.0) - jnp.log(1.0 + jnp.exp(jnp.minimum(z, neg)))


def _tri_and_ones(kind):
    row = jnp.bitwise_and(lax.broadcasted_iota(jnp.int32, (2 * TK, 2 * TK), 0), TK - 1)
    col = lax.broadcasted_iota(jnp.int32, (2 * TK, 2 * TK), 1)
    tri = {"after": row > col, "upto": row <= col, "before": row < col}[kind]
    return jnp.logical_or(col >= TK, tri).astype(BF16)


NPAIR = NH // 2
SCALE = HD ** -0.5


def _pair_specs(first_block):
    rows = pl.BlockSpec((TQ, LANES), lambda p, i: (i, first_block + p))
    whole = pl.BlockSpec((S, LANES), lambda p, i: (0, first_block + p))
    return rows, whole


Q_ROWS_SPEC, _ = _pair_specs(0)
_, K_ALL_SPEC = _pair_specs(NPAIR)
_, V_ALL_SPEC = _pair_specs(2 * NPAIR)
PAIR_ROWS_SPEC = pl.BlockSpec((TQ, LANES), lambda p, i: (i, p))
PAIR_ALL_SPEC = pl.BlockSpec((S, LANES), lambda p, i: (0, p))
PAIR_TOTAL_SPEC = pl.BlockSpec((2, TQ, TK), lambda p, i: (p, i, 0))


def _head_halves(x):
    first = lax.broadcasted_iota(jnp.int32, x.shape, 1) < HD
    zero = jnp.zeros_like(x)
    return jnp.where(first, x, zero), jnp.where(first, zero, x)


def _join_heads(a, b):
    return jnp.where(lax.broadcasted_iota(jnp.int32, a.shape, 1) < HD, a, b)


def _comm_hooks(comm, refs, n_in, n_out, n_scratch):
    nc = len(comm.arrs) if comm is not None else 0
    ins, cin = refs[:n_in], refs[n_in:n_in + nc]
    outs = refs[n_in + nc:n_in + nc + n_out]
    cout = refs[n_in + nc + n_out:n_in + 2 * nc + n_out]
    scratch = refs[n_in + 2 * nc + n_out:n_in + 2 * nc + n_out + n_scratch]
    sems = refs[n_in + 2 * nc + n_out + n_scratch:]
    phases = comm.phases(cin, cout, sems) if comm is not None else None
    return ins, outs, scratch, phases


def _with_comm(comm, in_specs, out_specs, out_shape, operands, scratch):
    if comm is None:
        return dict(in_specs=in_specs, out_specs=out_specs, out_shape=out_shape, scratch_shapes=scratch), operands
    nc = len(comm.arrs)
    return dict(in_specs=in_specs + [HBM_SPEC] * nc, out_specs=out_specs + [HBM_SPEC] * nc,
                out_shape=out_shape + comm.out_shape, scratch_shapes=scratch + comm.scratch), operands + comm.arrs


def attn_fwd(qkv, name, comm=None):
    n_steps = S // TQ

    def body(*refs):
        (q_ref, k_ref, v_ref), (o_ref, r_ref), (acc_ref, z_even, z_odd, w_ref), phases = _comm_hooks(
            comm, refs, 3, 2, 4)
        p = pl.program_id(0)
        i = pl.program_id(1)
        if phases is not None:
            pl.when(jnp.logical_and(p == 0, i == 0))(phases[0])
            pl.when(jnp.logical_and(p == NPAIR - 1, i == n_steps - 1))(phases[1])
        chains = [(sub, h) for sub in range(NSUB) for h in range(2)]
        q_sub = [_head_halves(q_ref[pl.ds(sub * RS, RS), :] * SCALE) for sub in range(NSUB)]
        after = _tri_and_ones("after")
        below_diagonal = (lax.broadcasted_iota(jnp.int32, (RS, TK), 1)
                          < lax.broadcasted_iota(jnp.int32, (RS, TK), 0))
        base = i * NSUB
        all_subs = list(range(NSUB))

        acc_ref[...] = jnp.zeros_like(acc_ref)
        r_ref[...] = jnp.zeros_like(r_ref)
        w_ref[...] = jnp.zeros_like(w_ref)

        def key_rows(block):
            return pl.ds(pl.multiple_of(block * TK, TK), TK)

        def store_scores(z_ref, block, subs):
            kb = k_ref[key_rows(block), :]
            for c, (sub, h) in enumerate(chains):
                if sub in subs:
                    z_ref[c] = lax.dot_general(q_sub[sub][h], kb, (((1,), (1,)), ((), ())),
                                               preferred_element_type=F32)

        def add_weighted_values(block, subs):
            vb = v_ref[key_rows(block), :]
            for sub in subs:
                acc_ref[pl.ds(sub * RS, RS), :] += _join_heads(*[
                    jnp.dot(w_ref[2 * sub + h], vb, preferred_element_type=F32) for h in range(2)])

        def step(block, z_ref, z_next_ref, subs, diagonal_sub, prev_subs, next_subs):
            if prev_subs:
                add_weighted_values(block + 1, prev_subs)
            if next_subs:
                store_scores(z_next_ref, jnp.maximum(block - 1, 0), next_subs)
            active = [(c, sub, h) for c, (sub, h) in enumerate(chains) if sub in subs]
            ls, sums = {}, {}
            for c, sub, h in active:
                ls[c] = _log_stay(z_ref[c])
                sums[c] = _dot_hilo(jnp.where(below_diagonal, ls[c], 0.0) if sub == diagonal_sub else ls[c], after)
            for c, sub, h in active:
                rows = pl.ds(sub * RS, RS)
                later = r_ref[h, rows, :]
                w = jnp.exp(z_ref[c] + ls[c] + (sums[c][:, :TK] + later))
                if sub == diagonal_sub:
                    w = jnp.where(below_diagonal, w, 0.0)
                w_ref[c] = w.astype(BF16)
                r_ref[h, rows, :] = later + sums[c][:, TK:]

        store_scores(z_even, base + NSUB - 1, [NSUB - 1])
        buffers = (z_even, z_odd)
        for j in reversed(range(NSUB)):
            subs = all_subs[j:]
            step(base + j, buffers[0], buffers[1], subs, j, all_subs[j + 1:], all_subs[j - 1:] if j else all_subs)
            buffers = buffers[::-1]
        assert buffers[0] is z_even

        @pl.loop(0, base // 2)
        def _(pair):
            block = base - 1 - 2 * pair
            step(block, z_even, z_odd, all_subs, None, all_subs, all_subs)
            step(block - 1, z_odd, z_even, all_subs, None, all_subs, all_subs)

        add_weighted_values(0, all_subs)
        o_ref[...] = acc_ref[...].astype(o_ref.dtype)
        if phases is not None:
            pl.when(jnp.logical_and(p == NPAIR - 1, i == n_steps - 1))(phases[2])

    kwargs, operands = _with_comm(
        comm, [Q_ROWS_SPEC, K_ALL_SPEC, V_ALL_SPEC], [PAIR_ROWS_SPEC, PAIR_TOTAL_SPEC],
        [jax.ShapeDtypeStruct((S, NH * HD), BF16), jax.ShapeDtypeStruct((NH, S, TK), F32)], [qkv, qkv, qkv],
        [pltpu.VMEM((TQ, LANES), F32), pltpu.VMEM((2 * NSUB, RS, TK), F32), pltpu.VMEM((2 * NSUB, RS, TK), F32),
         pltpu.VMEM((2 * NSUB, RS, TK), BF16)])
    return pl.pallas_call(
        body, name=name, grid=(NPAIR, n_steps),
        compiler_params=_cparams(("arbitrary", "arbitrary")), **kwargs,
    )(*operands)


def attn_bwd(qkv, dout, totals, name, comm=None):
    n_steps = S // TQ

    def body(*refs):
        ((q_ref, k_ref, v_ref, do_ref, r_ref), (dq_ref, dk_ref, dv_ref),
         (z_even, z_odd, dw_even, dw_odd, before_ref, dbefore_ref, dz_ref, w_ref), phases) = _comm_hooks(
            comm, refs, 5, 3, 8)
        p = pl.program_id(0)
        i = pl.program_id(1)
        if phases is not None:
            pl.when(jnp.logical_and(p == 0, i == 0))(phases[0])
            pl.when(jnp.logical_and(p == NPAIR - 1, i == n_steps - 2))(phases[1])

        @pl.when(i == 0)
        def _():
            dk_ref[...] = jnp.zeros_like(dk_ref)
            dv_ref[...] = jnp.zeros_like(dv_ref)

        chains = [(sub, h) for sub in range(NSUB) for h in range(2)]
        nch = len(chains)
        qb = q_ref[...]
        dob = do_ref[...].astype(BF16)
        q_sub = [_head_halves(qb[sub * RS:(sub + 1) * RS] * SCALE) for sub in range(NSUB)]
        do_sub = [_head_halves(dob[sub * RS:(sub + 1) * RS]) for sub in range(NSUB)]
        upto = _tri_and_ones("upto")
        before_tri = _tri_and_ones("before")
        below_diagonal = (lax.broadcasted_iota(jnp.int32, (RS, TK), 1)
                          < lax.broadcasted_iota(jnp.int32, (RS, TK), 0))
        contract_lanes = (((1,), (1,)), ((), ()))
        contract_rows = (((0,), (0,)), ((), ()))
        base = i * NSUB
        all_subs = list(range(NSUB))

        def key_rows(block):
            return pl.ds(pl.multiple_of(block * TK, TK), TK)

        def store_products(bufs, block, subs):
            z_ref, dw_ref = bufs
            kb = k_ref[key_rows(block), :]
            vb = v_ref[key_rows(block), :]
            for c, (sub, h) in enumerate(chains):
                if sub in subs:
                    z_ref[c] = lax.dot_general(q_sub[sub][h], kb, contract_lanes, preferred_element_type=F32)
                    dw_ref[c] = lax.dot_general(do_sub[sub][h], vb, contract_lanes, preferred_element_type=F32)

        def add_gradients(block, subs):
            kb = k_ref[key_rows(block), :]
            for sub in subs:
                rows = pl.ds(sub * RS, RS)
                dq_ref[rows, :] += _join_heads(*[jnp.dot(dz_ref[h, rows, :], kb, preferred_element_type=F32)
                                                 for h in range(2)])
            dk_ref[key_rows(block), :] += _join_heads(*[
                lax.dot_general(dz_ref[h], qb, contract_rows, preferred_element_type=F32) for h in range(2)])
            dv_ref[key_rows(block), :] += _join_heads(*[
                lax.dot_general(w_ref[h], dob, contract_rows, preferred_element_type=F32) for h in range(2)])

        for ref in (dq_ref, before_ref, dbefore_ref, dz_ref, w_ref):
            ref[...] = jnp.zeros_like(ref)
        even, odd = (z_even, dw_even), (z_odd, dw_odd)
        store_products(even, 0, all_subs)

        def step(block, bufs, next_bufs, subs, diagonal_sub, prev_subs, next_subs):
            z_ref, dw_ref = bufs
            add_gradients(jnp.maximum(block - 1, 0), prev_subs)
            for sub in prev_subs:
                if sub not in subs:
                    dz_ref[:, pl.ds(sub * RS, RS), :] = jnp.zeros((2, RS, TK), BF16)
                    w_ref[:, pl.ds(sub * RS, RS), :] = jnp.zeros((2, RS, TK), BF16)
            if next_subs:
                store_products(next_bufs, block + 1, next_subs)
            active = [(c, sub, h) for c, (sub, h) in enumerate(chains) if sub in subs]
            ls, sums, dl, dsums = {}, {}, {}, {}
            for c, sub, h in active:
                ls[c] = _log_stay(z_ref[c])
                sums[c] = _dot_hilo(jnp.where(below_diagonal, ls[c], 0.0) if sub == diagonal_sub else ls[c], upto)
            for c, sub, h in active:
                rows = pl.ds(sub * RS, RS)
                before = before_ref[c]
                log_after = r_ref[h, rows, :] - (sums[c][:, :TK] + before)
                w = jnp.exp((z_ref[c] + ls[c]) + log_after)
                if sub == diagonal_sub:
                    w = jnp.where(below_diagonal, w, 0.0)
                dl[c] = dw_ref[c] * w
                dsums[c] = _dot_hilo(dl[c], before_tri)
                w_ref[h, rows, :] = w.astype(BF16)
                before_ref[c] = before + sums[c][:, TK:]
            for c, sub, h in active:
                rows = pl.ds(sub * RS, RS)
                dbefore = dbefore_ref[c]
                beta = jnp.exp(z_ref[c] + ls[c])
                if sub == diagonal_sub:
                    beta = jnp.where(below_diagonal, beta, 0.0)
                dstay = dsums[c][:, :TK] + dbefore
                dz_ref[h, rows, :] = ((dl[c] - beta * (dl[c] + dstay)) * SCALE).astype(BF16)
                dbefore_ref[c] = dbefore + dsums[c][:, TK:]

        @pl.loop(0, base // 2)
        def _(pair):
            step(2 * pair, even, odd, all_subs, None, all_subs, all_subs)
            step(2 * pair + 1, odd, even, all_subs, None, all_subs, all_subs)

        bufs = (even, odd)
        for j in range(NSUB):
            step(base + j, bufs[0], bufs[1], all_subs[j:], j, all_subs[j - 1:] if j else all_subs, all_subs[j + 1:])
            bufs = bufs[::-1]

        add_gradients(base + NSUB - 1, all_subs[NSUB - 1:])
        if phases is not None:
            pl.when(jnp.logical_and(p == NPAIR - 1, i == n_steps - 1))(phases[2])

    full = jax.ShapeDtypeStruct((S, NH * HD), F32)
    kwargs, operands = _with_comm(
        comm, [Q_ROWS_SPEC, K_ALL_SPEC, V_ALL_SPEC, PAIR_ROWS_SPEC, PAIR_TOTAL_SPEC],
        [PAIR_ROWS_SPEC, PAIR_ALL_SPEC, PAIR_ALL_SPEC], [full, full, full], [qkv, qkv, qkv, dout, totals],
        [pltpu.VMEM((2 * NSUB, RS, TK), F32)] * 6 + [pltpu.VMEM((2, TQ, TK), BF16)] * 2)
    return pl.pallas_call(
        body, name=name, grid=(NPAIR, n_steps),
        compiler_params=_cparams(("arbitrary", "arbitrary")), **kwargs,
    )(*operands)


def _proj_cols(first_col):
    base = first_col // LANES
    return pl.BlockSpec((S, LANES), lambda j: (0, base + j))


CONV_OUT_SPEC = pl.BlockSpec((S, LANES), lambda j: (0, j))
CONV_DOUT_SPEC = pl.BlockSpec((S, LANES), lambda j: (0, (NH * HD) // LANES + j))
CONV_W_SPEC = pl.BlockSpec((8, LANES), lambda j: (0, j))
CONV_B_SPEC = pl.BlockSpec((1, LANES), lambda j: (0, j))


def _shift_down(u, n):
    rows = lax.broadcasted_iota(jnp.int32, u.shape, 0)
    return jnp.where(rows >= n, pltpu.roll(u, n, 0), 0.0)


def _shift_up(u, n):
    rows = lax.broadcasted_iota(jnp.int32, u.shape, 0)
    return jnp.where(rows < S - n, pltpu.roll(u, S - n, 0), 0.0)


def conv_fwd(proj, cw8, cb, name):
    def body(bg_ref, cg_ref, hc_ref, w_ref, b_ref, o_ref):
        u = cg_ref[...] * hc_ref[...]
        w = w_ref[...]
        y = w[0:1, :] * _shift_down(u, 2) + w[1:2, :] * _shift_down(u, 1) + w[2:3, :] * u + b_ref[...]
        o_ref[...] = bg_ref[...] * y

    return pl.pallas_call(
        body, name=name, grid=(CW // LANES,),
        in_specs=[_proj_cols(0), _proj_cols(CW), _proj_cols(2 * CW), CONV_W_SPEC, CONV_B_SPEC],
        out_specs=CONV_OUT_SPEC, out_shape=jax.ShapeDtypeStruct((S, CW), F32),
        compiler_params=_cparams(("parallel",)),
    )(proj, proj, proj, cw8, cb)


def conv_bwd(proj, dout, cw8, cb, name):
    def body(bg_ref, cg_ref, hc_ref, do_ref, w_ref, b_ref, dbg_ref, dcg_ref, dhc_ref, dw_ref, db_ref):
        cg, hc, do = cg_ref[...], hc_ref[...], do_ref[...]
        w = w_ref[...]
        u = cg * hc
        u1, u2 = _shift_down(u, 1), _shift_down(u, 2)
        y = w[0:1, :] * u2 + w[1:2, :] * u1 + w[2:3, :] * u + b_ref[...]
        dbg_ref[...] = do * y
        dy = do * bg_ref[...]
        db_ref[...] = jnp.sum(dy, axis=0, keepdims=True)
        dw_ref[...] = jnp.concatenate(
            [jnp.sum(dy * u2, axis=0, keepdims=True), jnp.sum(dy * u1, axis=0, keepdims=True),
             jnp.sum(dy * u, axis=0, keepdims=True), jnp.zeros((5, LANES), F32)], axis=0)
        du = w[2:3, :] * dy + w[1:2, :] * _shift_up(dy, 1) + w[0:1, :] * _shift_up(dy, 2)
        dcg_ref[...] = du * hc
        dhc_ref[...] = du * cg

    full = jax.ShapeDtypeStruct((S, CW), F32)
    return pl.pallas_call(
        body, name=name, grid=(CW // LANES,),
        in_specs=[_proj_cols(0), _proj_cols(CW), _proj_cols(2 * CW), CONV_DOUT_SPEC, CONV_W_SPEC, CONV_B_SPEC],
        out_specs=[CONV_OUT_SPEC, CONV_OUT_SPEC, CONV_OUT_SPEC, CONV_W_SPEC, CONV_B_SPEC],
        out_shape=[full, full, full, jax.ShapeDtypeStruct((8, CW), F32), jax.ShapeDtypeStruct((1, CW), F32)],
        compiler_params=_cparams(("parallel",)),
    )(proj, proj, proj, dout, cw8, cb)


GELU_K = math.sqrt(2.0 / math.pi)
GELU_C = 0.044715


def _gelu(x):
    return 0.5 * x * (1.0 + jnp.tanh(GELU_K * (x + GELU_C * (x * x * x))))


def _gelu_grad(x):
    t = jnp.tanh(GELU_K * (x + GELU_C * (x * x * x)))
    return 0.5 * (1.0 + t) + 0.5 * x * (1.0 - t * t) * (GELU_K * (1.0 + 3.0 * GELU_C * (x * x)))


def _sg_masks():
    row = lax.broadcasted_iota(jnp.int32, (T, T), 0)
    col = lax.broadcasted_iota(jnp.int32, (T, T), 1)
    causal = jnp.right_shift(row, 6) >= jnp.right_shift(col, 6)
    head_of_col = jnp.right_shift(lax.broadcasted_iota(jnp.int32, (T, CW), 1), 6)
    return causal, head_of_col


def _sg_weights(sw_ref, causal):
    return [jnp.where(causal, sw_ref[h], 0.0).astype(BF16) for h in range(SG_HEADS)]


def _sg_mixed(vnb, weights, bias, head_of_col):
    mixed = bias
    for h in range(SG_HEADS):
        mh = jnp.dot(weights[h], vnb, preferred_element_type=F32)
        mixed = mixed + jnp.where(head_of_col == h, mh, 0.0)
    return mixed


SG_WINDOWS = 4
SG_ROWS = SG_WINDOWS * T
SG_U_SPEC = pl.BlockSpec((SG_ROWS, CW), lambda n: (n, 3))
SG_V_SPEC = pl.BlockSpec((SG_ROWS, CW), lambda n: (n, 4))
SG_ROW_SPEC = pl.BlockSpec((SG_ROWS, CW), lambda n: (n, 0))
SG_DOUT_SPEC = pl.BlockSpec((SG_ROWS, CW), lambda n: (n, 3))
SG_G_SPEC = pl.BlockSpec((1, CW), lambda n: (0, 0))
SG_W_SPEC = pl.BlockSpec((SG_HEADS, T, T), lambda n: (0, 0, 0))
SG_BIAS_SPEC = pl.BlockSpec((T, CW), lambda n: (0, 0))


def sg_fwd(proj, gn, sw, bias, name):
    def body(u_ref, v_ref, g_ref, sw_ref, bias_ref, o_ref):
        causal, head_of_col = _sg_masks()
        weights = _sg_weights(sw_ref, causal)
        for wdw in range(SG_WINDOWS):
            rows = pl.ds(wdw * T, T)
            gv = _gelu(v_ref[rows, :])
            rstd = lax.rsqrt(jnp.mean(gv * gv, axis=-1, keepdims=True) + EPS)
            vnb = ((gv * rstd) * g_ref[...]).astype(BF16)
            mixed = _sg_mixed(vnb, weights, bias_ref[...], head_of_col)
            o_ref[rows, :] = _gelu(u_ref[rows, :]) * mixed

    return pl.pallas_call(
        body, name=name, grid=(S // SG_ROWS,),
        in_specs=[SG_U_SPEC, SG_V_SPEC, SG_G_SPEC, SG_W_SPEC, SG_BIAS_SPEC],
        out_specs=SG_ROW_SPEC, out_shape=jax.ShapeDtypeStruct((S, CW), F32),
        compiler_params=_cparams(("parallel",)),
    )(proj, proj, gn, sw, bias)


def sg_bwd(proj, dout, gn, sw, bias, name):
    def body(u_ref, v_ref, do_ref, g_ref, sw_ref, bias_ref, du_ref, dv_ref, dg_ref, dsw_ref, dbias_ref):
        @pl.when(pl.program_id(0) == 0)
        def _():
            dg_ref[...] = jnp.zeros_like(dg_ref)
            dsw_ref[...] = jnp.zeros_like(dsw_ref)
            dbias_ref[...] = jnp.zeros_like(dbias_ref)

        causal, head_of_col = _sg_masks()
        weights = _sg_weights(sw_ref, causal)
        gnv = g_ref[...]
        for wdw in range(SG_WINDOWS):
            rows = pl.ds(wdw * T, T)
            uv, vv, do = u_ref[rows, :], v_ref[rows, :], do_ref[rows, :]
            gv = _gelu(vv)
            rstd = lax.rsqrt(jnp.mean(gv * gv, axis=-1, keepdims=True) + EPS)
            xhat = gv * rstd
            vnb = (xhat * gnv).astype(BF16)
            mixed = _sg_mixed(vnb, weights, bias_ref[...], head_of_col)
            du_ref[rows, :] = (do * mixed) * _gelu_grad(uv)
            dmix = do * _gelu(uv)
            dbias_ref[...] += dmix
            dmixb = dmix.astype(BF16)
            dvn = jnp.zeros((T, CW), F32)
            for h in range(SG_HEADS):
                dvh = lax.dot_general(weights[h], dmixb, (((0,), (0,)), ((), ())), preferred_element_type=F32)
                dvn = dvn + jnp.where(head_of_col == h, dvh, 0.0)
                dmh = jnp.where(head_of_col == h, dmixb, jnp.zeros_like(dmixb))
                dwh = lax.dot_general(dmh, vnb, (((1,), (1,)), ((), ())), preferred_element_type=F32)
                dsw_ref[h] += jnp.where(causal, dwh, 0.0)
            dg_ref[...] += jnp.sum(dvn * xhat, axis=0, keepdims=True)
            dxhat = dvn * gnv
            dgv = rstd * (dxhat - xhat * jnp.mean(dxhat * xhat, axis=-1, keepdims=True))
            dv_ref[rows, :] = dgv * _gelu_grad(vv)

    full = jax.ShapeDtypeStruct((S, CW), F32)
    return pl.pallas_call(
        body, name=name, grid=(S // SG_ROWS,),
        in_specs=[SG_U_SPEC, SG_V_SPEC, SG_DOUT_SPEC, SG_G_SPEC, SG_W_SPEC, SG_BIAS_SPEC],
        out_specs=[SG_ROW_SPEC, SG_ROW_SPEC, SG_G_SPEC, SG_W_SPEC, SG_BIAS_SPEC],
        out_shape=[full, full, jax.ShapeDtypeStruct((1, CW), F32),
                   jax.ShapeDtypeStruct((SG_HEADS, T, T), F32), jax.ShapeDtypeStruct((T, CW), F32)],
        compiler_params=_cparams(("arbitrary",)),
    )(proj, proj, dout, gn, sw, bias)


ADA_COLS = NMOD * D // NDEV


def ada_fwd(c_all, ada_w, ada_b_mine, name):
    def body(c_ref, w_ref, b_ref, o_ref, ca_ref):
        cv = c_ref[...]
        ca = cv * (1.0 / (1.0 + jnp.exp(-cv)))
        ca_ref[...] = ca
        cab = ca.astype(BF16)
        for l in range(L):
            o_ref[l] = jnp.dot(cab, w_ref[l].astype(BF16), preferred_element_type=F32) + b_ref[l]

    return pl.pallas_call(
        body, name=name,
        out_shape=[jax.ShapeDtypeStruct((L, NDEV, ADA_COLS), F32), jax.ShapeDtypeStruct((NDEV, D), F32)],
        compiler_params=_cparams(),
    )(c_all, ada_w, ada_b_mine)


def ada_bwd(ca, dmod_cols, name):
    def body(ca_ref, dm_ref, o_ref):
        cab = ca_ref[...].astype(BF16)
        for l in range(L):
            o_ref[l] = lax.dot_general(cab, dm_ref[l].astype(BF16), (((0,), (0,)), ((), ())),
                                       preferred_element_type=F32)

    return pl.pallas_call(
        body, name=name, out_shape=jax.ShapeDtypeStruct((L, D, ADA_COLS), F32),
        compiler_params=_cparams(),
    )(ca, dmod_cols)


def _adamw(w, g, m, v):
    m = B1 * m + (1.0 - B1) * g
    v = B2 * v + (1.0 - B2) * (g * g)
    m_hat = m / BC1
    v_hat = v / BC2
    delta = -LR * (m_hat / (jnp.sqrt(v_hat) + AEPS) + WD * w)
    return delta, m, v


VEC_ROWS_PER_LAYER = 8
VEC_FINAL_ROW = L * VEC_ROWS_PER_LAYER
VEC_ROWS = VEC_FINAL_ROW + 8
W256_TAPS, W256_CONV_B, W256_GN = 0, 8, 9
W256_ROWS_PER_LAYER = 16


def small_update(vec_all, w256_all, sb_all, sw_all, params, name):
    n_par = len(params)

    def body(*refs):
        vec_ref, w256_ref, sb_ref = refs[:3]
        sw_refs = refs[3:3 + L]
        par_refs = [refs[3 + L + 3 * k:3 + L + 3 * k + 3] for k in range(n_par)]
        out = refs[3 + L + 3 * n_par:]
        out_par = [out[4 * k:4 * k + 4] for k in range(n_par)]
        loss_ref, taps_ref = out[4 * n_par:]

        def total(ref, idx):
            acc = ref[(0,) + idx].astype(F32)
            for d in range(1, NDEV):
                acc = acc + ref[(d,) + idx].astype(F32)
            return acc

        def update(k, region, g):
            w_ref, m_ref, v_ref = par_refs[k]
            g_ref, d_ref, nm_ref, nv_ref = out_par[k]
            delta, nm, nv = _adamw(w_ref[region], g, m_ref[region], v_ref[region])
            g_ref[region] = g
            d_ref[region] = delta
            nm_ref[region] = nm
            nv_ref[region] = nv

        for l in range(L):
            base = l * VEC_ROWS_PER_LAYER
            for k in range(NMOD):
                update(0, (slice(l, l + 1), slice(k * D, (k + 1) * D)), total(vec_ref, (slice(base + k, base + k + 1),)))
            update(1, (slice(l, l + 1),), total(vec_ref, (slice(base + 6, base + 7),)))
            update(2, (slice(l, l + 1),), total(vec_ref, (slice(base + 7, base + 8),)))
            wbase = l * W256_ROWS_PER_LAYER
            update(4, (slice(l, l + 1),), total(w256_ref, (slice(wbase + W256_CONV_B, wbase + W256_CONV_B + 1),)))
            update(5, (slice(l, l + 1),), total(w256_ref, (slice(wbase + W256_GN, wbase + W256_GN + 1),)))
            update(6, (l,), total(sw_refs[l], ()))
            update(7, (l,), total(sb_ref, (slice(l * SG_HEADS, (l + 1) * SG_HEADS),)))
            taps_ref[l] = total(w256_ref, (slice(wbase + W256_TAPS, wbase + W256_TAPS + 8),))
        update(3, (slice(0, 1),), total(vec_ref, (slice(VEC_FINAL_ROW, VEC_FINAL_ROW + 1),)))
        loss_ref[...] = total(vec_ref, (slice(VEC_FINAL_ROW + 1, VEC_FINAL_ROW + 2), slice(0, LANES)))

    out_shape = []
    for w, _, _ in params:
        out_shape += [jax.ShapeDtypeStruct(w.shape, F32)] * 4
    out_shape += [jax.ShapeDtypeStruct((1, LANES), F32), jax.ShapeDtypeStruct((L, 8, CW), F32)]
    outs = pl.pallas_call(body, name=name, out_shape=out_shape, compiler_params=_cparams())(
        vec_all, w256_all, sb_all, *sw_all, *[a for p in params for a in p])
    return [outs[4 * k:4 * k + 4] for k in range(n_par)], outs[4 * n_par:]


def adamw_plain(w, g, m, v, tr, name):
    rows, cols = w.shape
    spec = pl.BlockSpec((tr, cols), lambda i: (i, 0))

    def body(w_ref, g_ref, m_ref, v_ref, d_ref, nm_ref, nv_ref):
        delta, nm, nv = _adamw(w_ref[...], g_ref[...], m_ref[...], v_ref[...])
        d_ref[...] = delta
        nm_ref[...] = nm
        nv_ref[...] = nv

    shp = jax.ShapeDtypeStruct((rows, cols), F32)
    return pl.pallas_call(
        body, name=name, grid=(rows // tr,), in_specs=[spec] * 4, out_specs=[spec] * 3,
        out_shape=[shp, shp, shp], compiler_params=_cparams(("parallel",)),
    )(w, g, m, v)


def adamw_reduce(w, parts, m, v, tr, name, tie=None):
    _, rows, cols = w.shape
    spec = pl.BlockSpec((None, tr, cols), lambda l, i: (l, i, 0))
    pspecs = [pl.BlockSpec((NDEV, tr, cols), lambda l, i, k=k: (0, jnp.where(l == k, i, 0), 0)) for k in range(L)]

    ties = [] if tie is None else [tie]

    def body(w_ref, p0_ref, p1_ref, m_ref, v_ref, *rest):
        g_ref, d_ref, nm_ref, nv_ref = rest[len(ties):]
        first_layer = pl.program_id(0) == 0
        g = jnp.zeros((tr, cols), F32)
        for d in range(NDEV):
            g = g + jnp.where(first_layer, p0_ref[d], p1_ref[d]).astype(F32)
        delta, nm, nv = _adamw(w_ref[...], g, m_ref[...], v_ref[...])
        g_ref[...] = g
        d_ref[...] = delta
        nm_ref[...] = nm
        nv_ref[...] = nv

    shp = jax.ShapeDtypeStruct(w.shape, F32)
    return pl.pallas_call(
        body, name=name, grid=(L, rows // tr),
        in_specs=[spec] + pspecs + [spec, spec] + [pl.BlockSpec(t.shape, lambda l, i: (0, 0)) for t in ties],
        out_specs=[spec] * 4, out_shape=[shp] * 4, compiler_params=_cparams(("parallel", "parallel")),
    )(w, *parts, m, v, *ties)


SHARD_IN = PROJ // NDEV


def shards_to_columns(shards, name):
    tr = 256

    def body(i_ref, o_ref):
        for d in range(NDEV):
            o_ref[:, d * SHARD_IN:(d + 1) * SHARD_IN] = i_ref[d]

    return pl.pallas_call(
        body, name=name, grid=(D // tr,),
        in_specs=[pl.BlockSpec((NDEV, tr, SHARD_IN), lambda i: (0, i, 0))],
        out_specs=pl.BlockSpec((tr, PROJ), lambda i: (i, 0)),
        out_shape=jax.ShapeDtypeStruct((D, PROJ), shards.dtype), compiler_params=_cparams(("parallel",)),
    )(shards)


def columns_to_shards(mat, name):
    tr = 256

    def body(i_ref, o_ref):
        for d in range(NDEV):
            o_ref[d] = i_ref[:, d * SHARD_IN:(d + 1) * SHARD_IN]

    return pl.pallas_call(
        body, name=name, grid=(D // tr,),
        in_specs=[pl.BlockSpec((tr, PROJ), lambda i: (i, 0))],
        out_specs=pl.BlockSpec((NDEV, tr, SHARD_IN), lambda i: (0, i, 0)),
        out_shape=jax.ShapeDtypeStruct((NDEV, D, SHARD_IN), mat.dtype), compiler_params=_cparams(("parallel",)),
    )(mat)


def _pad_rows(flat, rows):
    return jnp.pad(flat, (0, rows * LANES - flat.shape[0])).reshape(rows, LANES)


def kernel(x, c, ada_w, ada_b, norm_mix_g, norm_mlp_g, w_in, conv_w, conv_b, gmlp_norm_g, spatial_w, spatial_b, w_out, mlp_w1, mlp_w2, final_norm_g, loss_target, m_ada_w, m_ada_b, m_norm_mix_g, m_norm_mlp_g, m_w_in, m_conv_w, m_conv_b, m_gmlp_norm_g, m_spatial_w, m_spatial_b, m_w_out, m_mlp_w1, m_mlp_w2, m_final_norm_g, v_ada_w, v_ada_b, v_norm_mix_g, v_norm_mlp_g, v_w_in, v_conv_w, v_conv_b, v_gmlp_norm_g, v_spatial_w, v_spatial_b, v_w_out, v_mlp_w1, v_mlp_w2, v_final_norm_g):
    me = _lin(_my_pos())
    x0 = x[0]
    target = loss_target[0]
    conv_shard = conv_w.shape[-1]

    w_in_b, w_out_b, w1_b, w2_b = [w.astype(BF16) for w in (w_in, w_out, mlp_w1, mlp_w2)]
    pack0 = _pad_rows(jnp.concatenate([c.reshape(-1), conv_w.reshape(-1)]), 16)
    g0, gw_in0 = run_comm(Gather([pack0, w_in_b[0]]), "gather_first")
    g0 = g0.reshape(NDEV, 16 * LANES)
    c_all = g0[:, :D]
    conv_full = (g0[:, D:D + L * 3 * conv_shard].reshape(NDEV, L, 3, conv_shard)
                 .transpose(1, 2, 0, 3).reshape(L, 3, CW))


    W_in = [shards_to_columns(gw_in0, "w_in_columns0"), None]
    W_out, W1, W2 = [None] * L, [None] * L, [None] * L

    ada_b_mine = lax.dynamic_slice(ada_b, (0, me * ADA_COLS), (L, ADA_COLS)).reshape(L, 1, ADA_COLS)
    mod_part, c_act = ada_fwd(c_all, ada_w, ada_b_mine, "ada_fwd")
    gmod = run_comm(Gather([mod_part]), "gather_mod")[0]
    mod = lax.dynamic_index_in_dim(gmod, me, axis=2, keepdims=False)
    mod = mod.transpose(1, 0, 2).reshape(L, NMOD, 1, D)
    early_weights, token = start_copies([w_out_b[0]], me, "gather_early0_start", True, after=gmod)
    mod = tied(mod, token)

    cw8 = jnp.pad(conv_full, ((0, 0), (0, 5), (0, 0)))
    sg_bias = jnp.repeat(spatial_b.transpose(0, 2, 1), HD, axis=2)

    saved = []
    xl = x0
    for l in range(L):
        sh_m, sc_m, g_m, sh_f, sc_f, g_f = [mod[l, k] for k in range(NMOD)]
        h1 = normmod_fwd(xl, norm_mix_g[l:l + 1], sc_m, sh_m, f"norm_mix_fwd{l}")
        if l > 0:
            W_in[l] = shards_to_columns(finish_copies(w_in_handle, xl, f"gather_w_in{l}_wait")[0],
                                        f"w_in_columns{l}")
        qkv = mm_layer("proj_qkv", l, h1, W_in[l], out_dtypes=[BF16], cols=(0, QKV))[0]
        proj = mm_layer("proj_rest", l, h1, W_in[l], out_dtypes=[F32], cols=(QKV, REST))[0]
        a_out, a_tot, gw2, gw1 = attn_fwd(qkv, f"attn_fwd{l}", comm=Gather([w2_b[l], w1_b[l]]))
        gw_out, = finish_copies(early_weights, a_out, f"gather_early{l}_wait")
        W_out[l] = gw_out.reshape(D, D)
        W1[l] = gw1
        W2[l] = gw2.reshape(DFF, D)
        if l + 1 < L:
            w_in_handle, token = start_copies([w_in_b[l + 1]], me, f"gather_w_in{l + 1}_start", True, after=a_out)
            early_weights, token = start_copies([w_out_b[l + 1]], me, f"gather_early{l + 1}_start", True, after=token)
            g_m = tied(g_m, token)
        c_out = conv_fwd(proj, cw8[l], conv_b[l:l + 1], f"conv_fwd{l}")
        s_out = sg_fwd(proj, gmlp_norm_g[l:l + 1], spatial_w[l], sg_bias[l], f"sg_fwd{l}")
        cat = jnp.concatenate([a_out, c_out.astype(BF16), s_out.astype(BF16)], axis=1)
        mix, x1, h2 = mm_layer("mix", l, cat, W_out[l], out_dtypes=[F32, F32, BF16], epilogue=_residual_then_norm,
                               extras=[(xl, "tile"), (g_m, "col"), (norm_mlp_g[l:l + 1], "col"), (sc_f, "col"),
                                       (sh_f, "col")])
        ra, r = mm_layer("mlp_up", l, h2, W1[l], out_dtypes=[BF16, BF16], b_blocks=True,
                         epilogue=lambda acc: (jnp.maximum(acc, 0.0), jnp.square(jnp.maximum(acc, 0.0))))
        m2, x2 = mm_layer("mlp_down", l, r, W2[l], out_dtypes=[F32, F32],
                          epilogue=lambda acc, xr, g: (acc, xr + g * acc), extras=[(x1, "tile"), (g_f, "col")])
        saved.append(dict(x=xl, h1=h1, proj=proj, qkv=qkv, a_tot=a_tot, cat=cat, mix=mix,
                          x1=x1, h2=h2, ra=ra, r=r, m2=m2))
        xl = x2

    dx, loss_part, d_final_g, dm2, dg_f = loss_head(xl, target, final_norm_g.reshape(1, D),
                                                    (saved[L - 1]["m2"], mod[L - 1, NMOD - 1]), "loss_head")

    p_in, p_out, p_w1, p_w2 = [None] * L, [None] * L, [None] * L, [None] * L
    w_in_grads = [None] * L
    vec_rows, d_norm_mix, d_norm_mlp = [None] * L, [None] * L, [None] * L
    dcw8, d_conv_b, d_gn, d_sw, d_sb = [None] * L, [None] * L, [None] * L, [None] * L, [None] * L
    late_grads = [None] * L
    for l in reversed(range(L)):
        sv = saved[l]
        sh_m, sc_m, g_m, sh_f, sc_f, g_f = [mod[l, k] for k in range(NMOD)]
        da = mm_layer("mlp_down_dgrad", l, dm2, W2[l], out_dtypes=[BF16], trans_b=True,
                      epilogue=lambda acc, rav: (acc * (2.0 * rav.astype(F32)),), extras=[(sv["ra"], "tile")])[0]
        dW2 = mm_layer("mlp_down_wgrad", l, sv["r"], dm2, out_dtypes=[BF16], trans_a=True)[0]
        dW1 = mm_layer("mlp_up_wgrad", l, sv["h2"], da, out_dtypes=[BF16], trans_a=True, out_blocks=True)[0]
        dh2 = mm_layer("mlp_up_dgrad", l, da, W1[l], out_dtypes=[F32], trans_b=True, b_blocks=True)[0]
        dx1, dsc_f, dsh_f, d_norm_mlp[l], dmix, dg_m = normmod_bwd(
            sv["x1"], dh2, dx, norm_mlp_g[l:l + 1], sc_f, f"norm_mlp_bwd{l}", gate_next=(sv["mix"], g_m))
        dcat = mm_layer("mix_dgrad", l, dmix, W_out[l], out_dtypes=[F32], trans_b=True)[0]
        dW_out = mm_layer("mix_wgrad", l, sv["cat"], dmix, out_dtypes=[BF16], trans_a=True)[0]
        pieces_w2, pieces_out = dW2.reshape(NDEV, DFF // NDEV, D), dW_out.reshape(NDEV, D // NDEV, D)
        ride, late = ([pieces_w2, pieces_out], dW1) if l == L - 1 else ([pieces_w2, dW1], pieces_out)
        dq, dk, dv, *arrived = attn_bwd(sv["qkv"], dcat, sv["a_tot"], f"attn_bwd{l}", comm=Exchange(ride))
        p_w2[l] = arrived[0]
        (p_out if l == L - 1 else p_w1)[l] = arrived[1]
        late_grads[l], late_token = start_copies([late], me, f"exchange_late{l}_start", False, after=dq)
        dbg, dcg, dhc, dcw8[l], d_conv_b[l] = conv_bwd(sv["proj"], dcat, cw8[l], conv_b[l:l + 1], f"conv_bwd{l}")
        dus, dvs, d_gn[l], dsw, dbias = sg_bwd(sv["proj"], dcat, gmlp_norm_g[l:l + 1], spatial_w[l], sg_bias[l],
                                               f"sg_bwd{l}")
        d_sw[l] = dsw.astype(BF16)
        d_sb[l] = dbias.reshape(T, SG_HEADS, HD).sum(axis=2).T
        dproj = jnp.concatenate([dq, dk, dv, dbg, dcg, dhc, dus, dvs], axis=1).astype(BF16)
        dW_in = mm_layer("proj_wgrad", l, sv["h1"], dproj, out_dtypes=[BF16], trans_a=True,
                         extras=[(late_token, "tie")])[0]
        pieces = columns_to_shards(dW_in, f"w_in_grad_shards{l}")
        w_in_grads[l], token = start_copies([pieces], me, f"exchange_w_in{l}_start", False)
        dh1 = mm_layer("proj_dgrad", l, dproj, W_in[l], out_dtypes=[F32], trans_b=True, extras=[(token, "tie")])[0]
        below = (saved[l - 1]["m2"], mod[l - 1, NMOD - 1]) if l > 0 else None
        dx, dsc_m, dsh_m, d_norm_mix[l], *gated_below = normmod_bwd(
            sv["x"], dh1, dx1, tied(norm_mix_g[l:l + 1], token), sc_m, f"norm_mix_bwd{l}", gate_next=below)
        vec_rows[l] = [dsh_m, dsc_m, dg_m, dsh_f, dsc_f, dg_f, d_norm_mix[l], d_norm_mlp[l]]
        if l > 0:
            dm2, dg_f = gated_below

    grad_x = dx.reshape(1, S, D)

    g_w2, d_w2, nm_w2, nv_w2 = adamw_reduce(mlp_w2, p_w2, m_mlp_w2, v_mlp_w2, 256, "adamw_mlp_w2", tie=token)
    p_w1[L - 1] = finish_copies(late_grads[L - 1], d_w2, f"exchange_late{L - 1}_wait")[0]
    g_w1, d_w1, nm_w1, nv_w1 = adamw_reduce(mlp_w1, p_w1, m_mlp_w1, v_mlp_w1, 256, "adamw_mlp_w1", tie=token)

    vec_pack = jnp.concatenate([row for l in range(L) for row in vec_rows[l]]
                               + [d_final_g, loss_part, jnp.zeros((VEC_ROWS - VEC_FINAL_ROW - 2, D), F32)], axis=0)
    vec_pack, _ = lax.optimization_barrier((vec_pack, (d_w1, d_w2)))
    w256_pack = jnp.concatenate([blk for l in range(L) for blk in (
        dcw8[l], d_conv_b[l], d_gn[l], jnp.zeros((W256_ROWS_PER_LAYER - W256_GN - 1, CW), F32))], axis=0)
    vec_all, w256_all, sb_all, *sw_all = run_comm(
        Gather([vec_pack, w256_pack, jnp.concatenate(d_sb, axis=0)] + d_sw), "gather_small_grads")

    dmod_all = (vec_all[:, :VEC_FINAL_ROW].reshape(NDEV, L, VEC_ROWS_PER_LAYER, D)[:, :, :NMOD]
                .reshape(NDEV, L, NMOD * D))
    dmod_cols = lax.dynamic_slice(dmod_all, (0, 0, me * ADA_COLS), (NDEV, L, ADA_COLS)).transpose(1, 0, 2)
    g_ada_w = ada_bwd(c_act, dmod_cols, "ada_bwd")

    flat2 = lambda t: t.reshape(L * D, ADA_COLS)
    d_ada_w, nm_ada_w, nv_ada_w = [t.reshape(L, D, ADA_COLS) for t in adamw_plain(
        flat2(ada_w), flat2(g_ada_w), flat2(m_ada_w), flat2(v_ada_w), 256, "adamw_ada_w")]

    after = jnp.concatenate([t.reshape(-1)[:1] for t in (d_w1, d_w2, d_ada_w)])
    p_in = [finish_copies(w_in_grads[l], after, f"exchange_w_in{l}_wait")[0] for l in range(L)]
    p_out[0] = finish_copies(late_grads[0], after, "exchange_late0_wait")[0]
    g_w_in, d_w_in, nm_w_in, nv_w_in = adamw_reduce(w_in, p_in, m_w_in, v_w_in, 256, "adamw_w_in")
    g_w_out, d_w_out, nm_w_out, nv_w_out = adamw_reduce(w_out, p_out, m_w_out, v_w_out, 128, "adamw_w_out")

    as_row = lambda t: t.reshape(1, D)
    small_params = [(ada_b, m_ada_b, v_ada_b), (norm_mix_g, m_norm_mix_g, v_norm_mix_g),
                    (norm_mlp_g, m_norm_mlp_g, v_norm_mlp_g),
                    (as_row(final_norm_g), as_row(m_final_norm_g), as_row(v_final_norm_g)),
                    (conv_b, m_conv_b, v_conv_b), (gmlp_norm_g, m_gmlp_norm_g, v_gmlp_norm_g),
                    (spatial_w, m_spatial_w, v_spatial_w), (spatial_b, m_spatial_b, v_spatial_b)]
    updated, (loss_sum, taps_sum) = small_update(vec_all, w256_all, sb_all, sw_all, small_params, "small_update")
    loss = loss_sum[0, 0]
    u_ada_b, u_norm_mix, u_norm_mlp, u_final, u_conv_b, u_gn, u_sw, u_sb = updated
    u_final = [t.reshape(D) for t in u_final]
    g_conv_w = lax.dynamic_slice(taps_sum, (0, 0, me * conv_shard), (L, 3, conv_shard))
    flat_cw = lambda t: t.reshape(L * 3, conv_shard)
    u_conv_w = [g_conv_w] + [t.reshape(L, 3, conv_shard) for t in adamw_plain(
        flat_cw(conv_w), flat_cw(g_conv_w), flat_cw(m_conv_w), flat_cw(v_conv_w), L * 3, "adamw_conv_w")]
    small_sets = [u_ada_b, u_norm_mix, u_norm_mlp, u_conv_w, u_conv_b, u_gn, u_sw, u_sb, u_final]
    small_g, sd, snm, snv = [[u[k] for u in small_sets] for k in range(4)]

    def ordered(big, small):
        ada, win, wout, w1, w2 = big
        return [ada, small[0], small[1], small[2], win, small[3], small[4], small[5], small[6], small[7],
                wout, w1, w2, small[8]]

    grads = ordered([g_ada_w, g_w_in, g_w_out, g_w1, g_w2], small_g)
    deltas = ordered([d_ada_w, d_w_in, d_w_out, d_w1, d_w2], sd)
    new_m = ordered([nm_ada_w, nm_w_in, nm_w_out, nm_w1, nm_w2], snm)
    new_v = ordered([nv_ada_w, nv_w_in, nv_w_out, nv_w1, nv_w2], snv)
    return (loss, grad_x, *grads, *deltas, *new_m, *new_v)
```

```python
import functools
import math

import jax
import jax.numpy as jnp
from jax import lax
from jax.experimental import pallas as pl
from jax.experimental.pallas import tpu as pltpu

F32 = jnp.float32
BF16 = jnp.bfloat16
MESH = pl.DeviceIdType.MESH

S = 2048
D = 1024
L = 2
NDEV = 8
HD = 64
NH = 8
PROJ = 2816
DFF = 4096
NMOD = 6
EPS = 1e-6
T = 128
SG_HEADS = 4
LANES = 128
CW = 256
QKV = 3 * NH * HD
REST = PROJ - QKV

LR, B1, B2, AEPS, WD, STEP = 0.001, 0.9, 0.999, 1e-08, 0.01, 10
BC1 = 1.0 - B1 ** STEP
BC2 = 1.0 - B2 ** STEP

VMEM_LIMIT = 48 * 1024 * 1024

HBM_SPEC = pl.BlockSpec(memory_space=pltpu.HBM)


def _cparams(sem=None):
    return pltpu.CompilerParams(dimension_semantics=sem, vmem_limit_bytes=VMEM_LIMIT)


def _my_pos():
    return lax.axis_index("x"), lax.axis_index("y"), lax.axis_index("c")


def _lin(p):
    return 4 * p[0] + 2 * p[1] + p[2]


class Gather:
    def __init__(self, arrs):
        self.arrs = list(arrs)
        n = len(self.arrs)
        self.out_shape = [jax.ShapeDtypeStruct((NDEV,) + a.shape, a.dtype) for a in self.arrs]
        self.scratch = [pltpu.SemaphoreType.DMA((n, 7)), pltpu.SemaphoreType.DMA((n, 7)),
                        pltpu.SemaphoreType.DMA((n,))]

    def phases(self, ins, outs, sems):
        n = len(self.arrs)
        send_sems, recv_sems, local_sems = sems
        x, y, c = _my_pos()
        me, sibling = (x, y, c), (x, y, 1 - c)
        chips = [(1 - x, y), (x, 1 - y), (1 - x, 1 - y)]

        def copy(a, k, block, to, src=None):
            slot = outs[a].at[_lin(block)]
            return pltpu.make_async_remote_copy(
                src_ref=slot if src is None else src, dst_ref=slot,
                send_sem=send_sems.at[a, k], recv_sem=recv_sems.at[a, k],
                device_id=to, device_id_type=MESH)

        def mine(a):
            return pltpu.make_async_copy(ins[a], outs[a].at[_lin(me)], local_sems.at[a])

        def first(a):
            return [copy(a, 0, me, sibling, src=ins[a])] + [
                copy(a, 1 + j, me, (*chip, c), src=ins[a]) for j, chip in enumerate(chips)]

        def passed(a):
            return [copy(a, 4 + j, (*chip, c), sibling) for j, chip in enumerate(chips)]

        def start():
            for a in range(n):
                mine(a).start()
                for cp in first(a):
                    cp.start()

        def relay():
            for j, chip in enumerate(chips):
                for a in range(n):
                    copy(a, 1 + j, (*chip, c), me).wait_recv()
                    passed(a)[j].start()

        def finish():
            for a in range(n):
                copy(a, 0, sibling, me).wait_recv()
            for j, chip in enumerate(chips):
                for a in range(n):
                    copy(a, 4 + j, (*chip, 1 - c), me).wait_recv()
            for a in range(n):
                for cp in first(a) + passed(a):
                    cp.wait_send()
                mine(a).wait()

        return start, relay, finish


class Exchange:
    def __init__(self, arrs):
        self.arrs = list(arrs)
        n = len(self.arrs)
        self.out_shape = [jax.ShapeDtypeStruct(a.shape, a.dtype) for a in self.arrs]
        self.scratch = [pltpu.SemaphoreType.DMA((n, 7)), pltpu.SemaphoreType.DMA((n, 7)),
                        pltpu.SemaphoreType.DMA((n,))]

    def phases(self, ins, outs, sems):
        n = len(self.arrs)
        send_sems, recv_sems, local_sems = sems
        x, y, c = _my_pos()
        me = (x, y, c)

        def peer(mask):
            return (1 - x if mask & 4 else x, 1 - y if mask & 2 else y, 1 - c if mask & 1 else c)

        def copy(a, mask):
            return pltpu.make_async_remote_copy(
                src_ref=ins[a].at[_lin(peer(mask))], dst_ref=outs[a].at[_lin(me)],
                send_sem=send_sems.at[a, mask - 1], recv_sem=recv_sems.at[a, mask - 1],
                device_id=peer(mask), device_id_type=MESH)

        def arrival(a, mask):
            return pltpu.make_async_remote_copy(
                src_ref=ins[a].at[_lin(me)], dst_ref=outs[a].at[_lin(peer(mask))],
                send_sem=send_sems.at[a, mask - 1], recv_sem=recv_sems.at[a, mask - 1],
                device_id=peer(mask), device_id_type=MESH)

        def mine(a):
            return pltpu.make_async_copy(ins[a].at[_lin(me)], outs[a].at[_lin(me)], local_sems.at[a])

        def start():
            for a in range(n):
                mine(a).start()
            for mask in (4, 2, 6, 1, 5, 3, 7):
                for a in range(n):
                    copy(a, mask).start()

        def relay():
            pass

        def finish():
            for mask in range(1, 8):
                for a in range(n):
                    arrival(a, mask).wait_recv()
            for mask in range(1, 8):
                for a in range(n):
                    copy(a, mask).wait_send()
            for a in range(n):
                mine(a).wait()

        return start, relay, finish


def run_comm(plan, name):
    n = len(plan.arrs)

    def body(*refs):
        start, relay, finish = plan.phases(refs[:n], refs[n:2 * n], refs[2 * n:])
        start()
        relay()
        finish()

    outs = pl.pallas_call(
        body, name=name, out_shape=plan.out_shape,
        in_specs=[HBM_SPEC] * n, out_specs=[HBM_SPEC] * n, scratch_shapes=plan.scratch,
    )(*plan.arrs)
    return list(outs)


SEM_SPEC = pl.BlockSpec(memory_space=pltpu.SEMAPHORE)
DATAFLOW = pltpu.SideEffectType.DATAFLOW_SIDE_EFFECTING


def _peer_copies(src_ref, land_ref, send_sems, recv_sems, first, same_block):
    x, y, c = _my_pos()
    me = (x, y, c)
    sends, arrivals = [], []
    for mask in (4, 2, 6, 1, 5, 3, 7):
        peer = (1 - x if mask & 4 else x, 1 - y if mask & 2 else y, 1 - c if mask & 1 else c)
        sends.append(pltpu.make_async_remote_copy(
            src_ref=src_ref if same_block else src_ref.at[_lin(peer)], dst_ref=land_ref.at[_lin(me)],
            send_sem=send_sems.at[first + mask - 1], recv_sem=recv_sems.at[first + mask - 1], device_id=peer,
            device_id_type=MESH))
        arrivals.append(pltpu.make_async_remote_copy(
            src_ref=src_ref if same_block else src_ref.at[_lin(me)], dst_ref=land_ref.at[_lin(peer)],
            send_sem=send_sems.at[first + mask - 1], recv_sem=recv_sems.at[first + mask - 1], device_id=peer,
            device_id_type=MESH))
    return sends, arrivals


def start_copies(srcs, me, name, same_block, after=None):
    n = len(srcs)
    landings = []
    for src in srcs:
        own = src[None] if same_block else lax.dynamic_index_in_dim(src, me, axis=0, keepdims=True)
        landings.append(lax.dynamic_update_slice(lax.empty((NDEV,) + own.shape[1:], src.dtype), own,
                                                 (me,) + (0,) * (own.ndim - 1)))

    def body(*refs):
        send_sems, recv_sems = refs[-2 * n - 3], refs[-2 * n - 2]
        token = refs[-1]
        for k in range(n):
            sends, _ = _peer_copies(refs[2 * k], refs[2 * k + 1], send_sems, recv_sems, 7 * k, same_block)
            for cp in sends:
                cp.start()
        token[...] = jnp.zeros_like(token)

    hbm = lambda a: pltpu.HBM(a.shape, a.dtype)
    pairs = [a for pair in zip(srcs, landings) for a in pair]
    extra = [] if after is None else [after]
    sems = pltpu.SemaphoreType.DMA((7 * n,))
    send_sems, recv_sems, *thru, token = pl.pallas_call(
        body, name=name,
        out_shape=(sems, sems, *[hbm(a) for a in pairs], jax.ShapeDtypeStruct((8, LANES), F32)),
        in_specs=[HBM_SPEC] * (2 * n) + [pl.BlockSpec(memory_space=pl.ANY)] * len(extra),
        out_specs=(SEM_SPEC, SEM_SPEC, *[HBM_SPEC] * (2 * n), pl.BlockSpec(memory_space=pltpu.VMEM)),
        input_output_aliases={k: 2 + k for k in range(2 * n)},
        compiler_params=pltpu.CompilerParams(has_side_effects=DATAFLOW),
    )(*[pltpu.with_memory_space_constraint(a, pltpu.HBM) for a in pairs], *extra)
    return (send_sems, recv_sems, thru, same_block), token


def finish_copies(handle, after, name):
    send_sems, recv_sems, thru, same_block = handle
    n = len(thru) // 2

    def body(*refs):
        send_sems, recv_sems = refs[2 * n], refs[2 * n + 1]
        for k in range(n):
            sends, arrivals = _peer_copies(refs[2 * k], refs[2 * k + 1], send_sems, recv_sems, 7 * k, same_block)
            for cp in sends:
                cp.wait_send()
            for cp in arrivals:
                cp.wait_recv()

    hbm = lambda a: pltpu.HBM(a.shape, a.dtype)
    outs = pl.pallas_call(
        body, name=name, out_shape=tuple(hbm(a) for a in thru),
        in_specs=[HBM_SPEC] * (2 * n) + [SEM_SPEC, SEM_SPEC, pl.BlockSpec(memory_space=pl.ANY)],
        out_specs=tuple([HBM_SPEC] * (2 * n)), input_output_aliases={k: k for k in range(2 * n)},
        compiler_params=pltpu.CompilerParams(has_side_effects=DATAFLOW),
    )(*thru, send_sems, recv_sems, after)
    return [outs[2 * k + 1] for k in range(n)]


def tied(x, token):
    return x + token[0:1, 0:1].astype(x.dtype)


MM_TILES = {
    "proj_qkv": (S, 512), "proj_rest": (S, 256), "mix": (512, D), "mlp_up": (S, 512), "mlp_down": (1024, 256),
    "mlp_down_dgrad": (S, 1024), "mlp_down_wgrad": (1024, 1024), "mlp_up_wgrad": (1024, 512),
    "mlp_up_dgrad": (1024, 512), "mix_dgrad": (1024, 512), "mix_wgrad": (512, 1024),
    "proj_wgrad": (1024, PROJ // 2), "proj_dgrad": (1024, 512),
}


def mm_layer(kind, l, a, b, **kw):
    tm, tn = MM_TILES[kind]
    return mm(a, b, tm=tm, tn=tn, name=f"{kind}{l}", **kw)


def mm(a, b, *, tm, tn, out_dtypes, epilogue=None, extras=(), name, trans_a=False, trans_b=False,
       cols=None, b_blocks=False, out_blocks=False):
    if trans_a:
        kdim, m = a.shape
    else:
        m, kdim = a.shape
    shard = b.shape[-1] if b_blocks else None
    if b_blocks:
        full = (b.shape[1], NDEV * shard)
    else:
        full = b.shape
    first, ncols = cols if cols is not None else (0, full[0] if trans_b else full[1])
    assert full[1 if trans_b else 0] == kdim and m % tm == 0 and ncols % tn == 0 and first % tn == 0
    j0 = first // tn
    if trans_a:
        a_spec = pl.BlockSpec((kdim, tm), lambda i, j: (0, i))
    else:
        a_spec = pl.BlockSpec((tm, kdim), lambda i, j: (i, 0))
    if b_blocks and trans_b:
        b_spec = pl.BlockSpec((NDEV, tn, shard), lambda i, j: (0, j0 + j, 0))
    elif b_blocks:
        assert tn == shard
        b_spec = pl.BlockSpec((None, kdim, tn), lambda i, j: (j0 + j, 0, 0))
    elif trans_b:
        b_spec = pl.BlockSpec((tn, kdim), lambda i, j: (j0 + j, 0))
    else:
        b_spec = pl.BlockSpec((kdim, tn), lambda i, j: (0, j0 + j))
    if out_blocks:
        assert tn * NDEV == ncols
        out_spec = pl.BlockSpec((None, tm, tn), lambda i, j: (j, i, 0))
        out_dims = (NDEV, m, tn)
    else:
        out_spec = pl.BlockSpec((tm, tn), lambda i, j: (i, j))
        out_dims = (m, ncols)
    ex_specs = []
    for arr, kind in extras:
        if kind == "tile":
            ex_specs.append(pl.BlockSpec((tm, tn), lambda i, j: (i, j)))
        elif kind == "col":
            ex_specs.append(pl.BlockSpec((1, tn), lambda i, j: (0, j)))
        else:
            ex_specs.append(pl.BlockSpec(arr.shape, lambda i, j: (0, 0)))
    n_ex, n_out = len(extras), len(out_dtypes)
    used = [k for k, (_, kind) in enumerate(extras) if kind != "tie"]

    def body(a_ref, b_ref, *rest):
        ex_refs, out_refs = rest[:n_ex], rest[n_ex:]
        if trans_a:
            acc = lax.dot_general(a_ref[...], b_ref[...], (((0,), (0,)), ((), ())),
                                  preferred_element_type=F32)
        elif trans_b and b_blocks:
            acc = jnp.zeros((tm, tn), F32)
            for d in range(NDEV):
                acc = acc + lax.dot_general(a_ref[:, d * shard:(d + 1) * shard], b_ref[d],
                                            (((1,), (1,)), ((), ())), preferred_element_type=F32)
        elif trans_b:
            acc = lax.dot_general(a_ref[...], b_ref[...], (((1,), (1,)), ((), ())),
                                  preferred_element_type=F32)
        else:
            acc = jnp.dot(a_ref[...], b_ref[...], preferred_element_type=F32)
        outs = (acc,) if epilogue is None else epilogue(acc, *[ex_refs[k][...] for k in used])
        for o_ref, val in zip(out_refs, outs):
            o_ref[...] = val.astype(o_ref.dtype)

    outs = pl.pallas_call(
        body, name=name, grid=(m // tm, ncols // tn),
        in_specs=[a_spec, b_spec] + ex_specs,
        out_specs=[out_spec for _ in range(n_out)],
        out_shape=[jax.ShapeDtypeStruct(out_dims, dt) for dt in out_dtypes],
        compiler_params=_cparams(("parallel", "parallel")),
    )(a, b, *[arr for arr, _ in extras])
    return list(outs)


TR = 512

ROW_SPEC = pl.BlockSpec((TR, D), lambda i: (i, 0))
VEC_SPEC = pl.BlockSpec((1, D), lambda i: (0, 0))


def _residual_then_norm(acc, xr, gate, g, sc, sh):
    x_new = xr + gate * acc
    rstd = lax.rsqrt(jnp.mean(x_new * x_new, axis=-1, keepdims=True) + EPS)
    return acc, x_new, ((x_new * rstd) * g) * (1.0 + sc) + sh


def normmod_fwd(x, g, sc, sh, name):
    def body(x_ref, g_ref, sc_ref, sh_ref, o_ref):
        xv = x_ref[...]
        rstd = lax.rsqrt(jnp.mean(xv * xv, axis=-1, keepdims=True) + EPS)
        n = (xv * rstd) * g_ref[...]
        o_ref[...] = (n * (1.0 + sc_ref[...]) + sh_ref[...]).astype(o_ref.dtype)

    return pl.pallas_call(
        body, name=name, grid=(S // TR,),
        in_specs=[ROW_SPEC, VEC_SPEC, VEC_SPEC, VEC_SPEC], out_specs=ROW_SPEC,
        out_shape=jax.ShapeDtypeStruct((S, D), BF16),
        compiler_params=_cparams(("parallel",)),
    )(x, g, sc, sh)


def _gate_next(dxv, refs):
    br_ref, gate_ref, dbr_ref, dgate_ref = refs

    @pl.when(pl.program_id(0) == 0)
    def _():
        dgate_ref[...] = jnp.zeros_like(dgate_ref)

    dbr_ref[...] = (dxv * gate_ref[...]).astype(dbr_ref.dtype)
    dgate_ref[...] += jnp.sum(dxv * br_ref[...], axis=0, keepdims=True)


GATE_NEXT_IN = [ROW_SPEC, VEC_SPEC]
GATE_NEXT_OUT = [ROW_SPEC, VEC_SPEC]
GATE_NEXT_SHAPES = [jax.ShapeDtypeStruct((S, D), BF16), jax.ShapeDtypeStruct((1, D), F32)]


def normmod_bwd(x, dh, dres, g, sc, name, gate_next=None):
    nxt = 2 if gate_next else 0

    def body(x_ref, dh_ref, dres_ref, g_ref, sc_ref, *rest):
        nxt_in, (dx_ref, dsc_ref, dsh_ref, dg_ref), nxt_out = rest[:nxt], rest[nxt:nxt + 4], rest[nxt + 4:]

        @pl.when(pl.program_id(0) == 0)
        def _():
            dsc_ref[...] = jnp.zeros_like(dsc_ref)
            dsh_ref[...] = jnp.zeros_like(dsh_ref)
            dg_ref[...] = jnp.zeros_like(dg_ref)

        xv, dh = x_ref[...], dh_ref[...]
        gv = g_ref[...]
        rstd = lax.rsqrt(jnp.mean(xv * xv, axis=-1, keepdims=True) + EPS)
        xhat = xv * rstd
        dn = dh * (1.0 + sc_ref[...])
        dxhat = dn * gv
        dxv = dres_ref[...] + rstd * (dxhat - xhat * jnp.mean(dxhat * xhat, axis=-1, keepdims=True))
        dx_ref[...] = dxv
        dsc_ref[...] += jnp.sum(dh * (xhat * gv), axis=0, keepdims=True)
        dsh_ref[...] += jnp.sum(dh, axis=0, keepdims=True)
        dg_ref[...] += jnp.sum(dn * xhat, axis=0, keepdims=True)
        if gate_next:
            _gate_next(dxv, nxt_in + nxt_out)

    vec_out = jax.ShapeDtypeStruct((1, D), F32)
    on = bool(gate_next)
    return pl.pallas_call(
        body, name=name, grid=(S // TR,),
        in_specs=[ROW_SPEC, ROW_SPEC, ROW_SPEC, VEC_SPEC, VEC_SPEC] + GATE_NEXT_IN * on,
        out_specs=[ROW_SPEC, VEC_SPEC, VEC_SPEC, VEC_SPEC] + GATE_NEXT_OUT * on,
        out_shape=[jax.ShapeDtypeStruct((S, D), F32), vec_out, vec_out, vec_out] + GATE_NEXT_SHAPES * on,
        compiler_params=_cparams(("arbitrary",)),
    )(x, dh, dres, g, sc, *(gate_next or ()))


def loss_head(x, target, g, gate_next, name):
    def body(x_ref, t_ref, g_ref, br_ref, gate_ref, dx_ref, loss_ref, dg_ref, dbr_ref, dgate_ref):
        @pl.when(pl.program_id(0) == 0)
        def _():
            loss_ref[...] = jnp.zeros_like(loss_ref)
            dg_ref[...] = jnp.zeros_like(dg_ref)

        xv, gv = x_ref[...], g_ref[...]
        rstd = lax.rsqrt(jnp.mean(xv * xv, axis=-1, keepdims=True) + EPS)
        xhat = xv * rstd
        err = xhat * gv - t_ref[...]
        loss_ref[...] += jnp.sum(err * err) * (0.5 / D)
        dy = err * (1.0 / D)
        dg_ref[...] += jnp.sum(dy * xhat, axis=0, keepdims=True)
        dxhat = dy * gv
        dxv = rstd * (dxhat - xhat * jnp.mean(dxhat * xhat, axis=-1, keepdims=True))
        dx_ref[...] = dxv
        _gate_next(dxv, (br_ref, gate_ref, dbr_ref, dgate_ref))

    return pl.pallas_call(
        body, name=name, grid=(S // TR,),
        in_specs=[ROW_SPEC, ROW_SPEC, VEC_SPEC] + GATE_NEXT_IN,
        out_specs=[ROW_SPEC, VEC_SPEC, VEC_SPEC] + GATE_NEXT_OUT,
        out_shape=[jax.ShapeDtypeStruct((S, D), F32), jax.ShapeDtypeStruct((1, D), F32),
                   jax.ShapeDtypeStruct((1, D), F32)] + GATE_NEXT_SHAPES,
        compiler_params=_cparams(("arbitrary",)),
    )(x, target, g, *gate_next)


TQ = 512
RS = 128
NSUB = TQ // RS
TK = 128


def _dot_hilo(a, tri_twice):
    hi = a.astype(BF16)
    lo = (a - hi.astype(F32)).astype(BF16)
    return jnp.dot(jnp.concatenate([hi, lo], axis=1), tri_twice, preferred_element_type=F32)


def _log_stay(z):
    neg = -z
    return jnp.minimum(neg, 0.0) - jnp.log(1.0 + jnp.exp(jnp.minimum(z, neg)))


def _tri_and_ones(kind):
    row = jnp.bitwise_and(lax.broadcasted_iota(jnp.int32, (2 * TK, 2 * TK), 0), TK - 1)
    col = lax.broadcasted_iota(jnp.int32, (2 * TK, 2 * TK), 1)
    tri = {"after": row > col, "upto": row <= col, "before": row < col}[kind]
    return jnp.logical_or(col >= TK, tri).astype(BF16)


NPAIR = NH // 2
SCALE = HD ** -0.5


def _pair_specs(first_block):
    rows = pl.BlockSpec((TQ, LANES), lambda p, i: (i, first_block + p))
    whole = pl.BlockSpec((S, LANES), lambda p, i: (0, first_block + p))
    return rows, whole


Q_ROWS_SPEC, _ = _pair_specs(0)
_, K_ALL_SPEC = _pair_specs(NPAIR)
_, V_ALL_SPEC = _pair_specs(2 * NPAIR)
PAIR_ROWS_SPEC = pl.BlockSpec((TQ, LANES), lambda p, i: (i, p))
PAIR_ALL_SPEC = pl.BlockSpec((S, LANES), lambda p, i: (0, p))
PAIR_TOTAL_SPEC = pl.BlockSpec((2, TQ, TK), lambda p, i: (p, i, 0))


def _head_halves(x):
    first = lax.broadcasted_iota(jnp.int32, x.shape, 1) < HD
    zero = jnp.zeros_like(x)
    return jnp.where(first, x, zero), jnp.where(first, zero, x)


def _join_heads(a, b):
    return jnp.where(lax.broadcasted_iota(jnp.int32, a.shape, 1) < HD, a, b)


def _comm_hooks(comm, refs, n_in, n_out, n_scratch):
    nc = len(comm.arrs) if comm is not None else 0
    ins, cin = refs[:n_in], refs[n_in:n_in + nc]
    outs = refs[n_in + nc:n_in + nc + n_out]
    cout = refs[n_in + nc + n_out:n_in + 2 * nc + n_out]
    scratch = refs[n_in + 2 * nc + n_out:n_in + 2 * nc + n_out + n_scratch]
    sems = refs[n_in + 2 * nc + n_out + n_scratch:]
    phases = comm.phases(cin, cout, sems) if comm is not None else None
    return ins, outs, scratch, phases


def _with_comm(comm, in_specs, out_specs, out_shape, operands, scratch):
    if comm is None:
        return dict(in_specs=in_specs, out_specs=out_specs, out_shape=out_shape, scratch_shapes=scratch), operands
    nc = len(comm.arrs)
    return dict(in_specs=in_specs + [HBM_SPEC] * nc, out_specs=out_specs + [HBM_SPEC] * nc,
                out_shape=out_shape + comm.out_shape, scratch_shapes=scratch + comm.scratch), operands + comm.arrs


def attn_fwd(qkv, name, comm=None):
    n_steps = S // TQ

    def body(*refs):
        (q_ref, k_ref, v_ref), (o_ref, r_ref), (acc_ref, z_even, z_odd, w_ref), phases = _comm_hooks(
            comm, refs, 3, 2, 4)
        p = pl.program_id(0)
        i = pl.program_id(1)
        if phases is not None:
            pl.when(jnp.logical_and(p == 0, i == 0))(phases[0])
            pl.when(jnp.logical_and(p == NPAIR - 1, i == n_steps - 1))(phases[1])
        chains = [(sub, h) for sub in range(NSUB) for h in range(2)]
        q_sub = [_head_halves(q_ref[pl.ds(sub * RS, RS), :] * SCALE) for sub in range(NSUB)]
        after = _tri_and_ones("after")
        below_diagonal = (lax.broadcasted_iota(jnp.int32, (RS, TK), 1)
                          < lax.broadcasted_iota(jnp.int32, (RS, TK), 0))
        base = i * NSUB
        all_subs = list(range(NSUB))

        acc_ref[...] = jnp.zeros_like(acc_ref)
        r_ref[...] = jnp.zeros_like(r_ref)
        w_ref[...] = jnp.zeros_like(w_ref)

        def key_rows(block):
            return pl.ds(pl.multiple_of(block * TK, TK), TK)

        def store_scores(z_ref, block, subs):
            kb = k_ref[key_rows(block), :]
            for c, (sub, h) in enumerate(chains):
                if sub in subs:
                    z_ref[c] = lax.dot_general(q_sub[sub][h], kb, (((1,), (1,)), ((), ())),
                                               preferred_element_type=F32)

        def add_weighted_values(block, subs):
            vb = v_ref[key_rows(block), :]
            for sub in subs:
                acc_ref[pl.ds(sub * RS, RS), :] += _join_heads(*[
                    jnp.dot(w_ref[2 * sub + h], vb, preferred_element_type=F32) for h in range(2)])

        def step(block, z_ref, z_next_ref, subs, diagonal_sub, prev_subs, next_subs):
            if prev_subs:
                add_weighted_values(block + 1, prev_subs)
            if next_subs:
                store_scores(z_next_ref, jnp.maximum(block - 1, 0), next_subs)
            active = [(c, sub, h) for c, (sub, h) in enumerate(chains) if sub in subs]
            ls, sums = {}, {}
            for c, sub, h in active:
                ls[c] = _log_stay(z_ref[c])
                sums[c] = _dot_hilo(jnp.where(below_diagonal, ls[c], 0.0) if sub == diagonal_sub else ls[c], after)
            for c, sub, h in active:
                rows = pl.ds(sub * RS, RS)
                later = r_ref[h, rows, :]
                w = jnp.exp(z_ref[c] + ls[c] + (sums[c][:, :TK] + later))
                if sub == diagonal_sub:
                    w = jnp.where(below_diagonal, w, 0.0)
                w_ref[c] = w.astype(BF16)
                r_ref[h, rows, :] = later + sums[c][:, TK:]

        store_scores(z_even, base + NSUB - 1, [NSUB - 1])
        buffers = (z_even, z_odd)
        for j in reversed(range(NSUB)):
            subs = all_subs[j:]
            step(base + j, buffers[0], buffers[1], subs, j, all_subs[j + 1:], all_subs[j - 1:] if j else all_subs)
            buffers = buffers[::-1]
        assert buffers[0] is z_even

        @pl.loop(0, base // 2)
        def _(pair):
            block = base - 1 - 2 * pair
            step(block, z_even, z_odd, all_subs, None, all_subs, all_subs)
            step(block - 1, z_odd, z_even, all_subs, None, all_subs, all_subs)

        add_weighted_values(0, all_subs)
        o_ref[...] = acc_ref[...].astype(o_ref.dtype)
        if phases is not None:
            pl.when(jnp.logical_and(p == NPAIR - 1, i == n_steps - 1))(phases[2])

    kwargs, operands = _with_comm(
        comm, [Q_ROWS_SPEC, K_ALL_SPEC, V_ALL_SPEC], [PAIR_ROWS_SPEC, PAIR_TOTAL_SPEC],
        [jax.ShapeDtypeStruct((S, NH * HD), BF16), jax.ShapeDtypeStruct((NH, S, TK), F32)], [qkv, qkv, qkv],
        [pltpu.VMEM((TQ, LANES), F32), pltpu.VMEM((2 * NSUB, RS, TK), F32), pltpu.VMEM((2 * NSUB, RS, TK), F32),
         pltpu.VMEM((2 * NSUB, RS, TK), BF16)])
    return pl.pallas_call(
        body, name=name, grid=(NPAIR, n_steps),
        compiler_params=_cparams(("arbitrary", "arbitrary")), **kwargs,
    )(*operands)


def attn_bwd(qkv, dout, totals, name, comm=None):
    n_steps = S // TQ

    def body(*refs):
        ((q_ref, k_ref, v_ref, do_ref, r_ref), (dq_ref, dk_ref, dv_ref),
         (z_even, z_odd, dw_even, dw_odd, before_ref, dbefore_ref, dz_ref, w_ref), phases) = _comm_hooks(
            comm, refs, 5, 3, 8)
        p = pl.program_id(0)
        i = pl.program_id(1)
        if phases is not None:
            pl.when(jnp.logical_and(p == 0, i == 0))(phases[0])
            pl.when(jnp.logical_and(p == NPAIR - 1, i == n_steps - 2))(phases[1])

        @pl.when(i == 0)
        def _():
            dk_ref[...] = jnp.zeros_like(dk_ref)
            dv_ref[...] = jnp.zeros_like(dv_ref)

        chains = [(sub, h) for sub in range(NSUB) for h in range(2)]
        nch = len(chains)
        qb = q_ref[...]
        dob = do_ref[...].astype(BF16)
        q_sub = [_head_halves(qb[sub * RS:(sub + 1) * RS] * SCALE) for sub in range(NSUB)]
        do_sub = [_head_halves(dob[sub * RS:(sub + 1) * RS]) for sub in range(NSUB)]
        upto = _tri_and_ones("upto")
        before_tri = _tri_and_ones("before")
        below_diagonal = (lax.broadcasted_iota(jnp.int32, (RS, TK), 1)
                          < lax.broadcasted_iota(jnp.int32, (RS, TK), 0))
        contract_lanes = (((1,), (1,)), ((), ()))
        contract_rows = (((0,), (0,)), ((), ()))
        base = i * NSUB
        all_subs = list(range(NSUB))

        def key_rows(block):
            return pl.ds(pl.multiple_of(block * TK, TK), TK)

        def store_products(bufs, block, subs):
            z_ref, dw_ref = bufs
            kb = k_ref[key_rows(block), :]
            vb = v_ref[key_rows(block), :]
            for c, (sub, h) in enumerate(chains):
                if sub in subs:
                    z_ref[c] = lax.dot_general(q_sub[sub][h], kb, contract_lanes, preferred_element_type=F32)
                    dw_ref[c] = lax.dot_general(do_sub[sub][h], vb, contract_lanes, preferred_element_type=F32)

        def add_gradients(block, subs):
            kb = k_ref[key_rows(block), :]
            for sub in subs:
                rows = pl.ds(sub * RS, RS)
                dq_ref[rows, :] += _join_heads(*[jnp.dot(dz_ref[h, rows, :], kb, preferred_element_type=F32)
                                                 for h in range(2)])
            dk_ref[key_rows(block), :] += _join_heads(*[
                lax.dot_general(dz_ref[h], qb, contract_rows, preferred_element_type=F32) for h in range(2)])
            dv_ref[key_rows(block), :] += _join_heads(*[
                lax.dot_general(w_ref[h], dob, contract_rows, preferred_element_type=F32) for h in range(2)])

        for ref in (dq_ref, before_ref, dbefore_ref, dz_ref, w_ref):
            ref[...] = jnp.zeros_like(ref)
        even, odd = (z_even, dw_even), (z_odd, dw_odd)
        store_products(even, 0, all_subs)

        def step(block, bufs, next_bufs, subs, diagonal_sub, prev_subs, next_subs):
            z_ref, dw_ref = bufs
            add_gradients(jnp.maximum(block - 1, 0), prev_subs)
            for sub in prev_subs:
                if sub not in subs:
                    dz_ref[:, pl.ds(sub * RS, RS), :] = jnp.zeros((2, RS, TK), BF16)
                    w_ref[:, pl.ds(sub * RS, RS), :] = jnp.zeros((2, RS, TK), BF16)
            if next_subs:
                store_products(next_bufs, block + 1, next_subs)
            active = [(c, sub, h) for c, (sub, h) in enumerate(chains) if sub in subs]
            ls, sums, dl, dsums = {}, {}, {}, {}
            for c, sub, h in active:
                ls[c] = _log_stay(z_ref[c])
                sums[c] = _dot_hilo(jnp.where(below_diagonal, ls[c], 0.0) if sub == diagonal_sub else ls[c], upto)
            for c, sub, h in active:
                rows = pl.ds(sub * RS, RS)
                before = before_ref[c]
                log_after = r_ref[h, rows, :] - (sums[c][:, :TK] + before)
                w = jnp.exp((z_ref[c] + ls[c]) + log_after)
                if sub == diagonal_sub:
                    w = jnp.where(below_diagonal, w, 0.0)
                dl[c] = dw_ref[c] * w
                dsums[c] = _dot_hilo(dl[c], before_tri)
                w_ref[h, rows, :] = w.astype(BF16)
                before_ref[c] = before + sums[c][:, TK:]
            for c, sub, h in active:
                rows = pl.ds(sub * RS, RS)
                dbefore = dbefore_ref[c]
                beta = jnp.exp(z_ref[c] + ls[c])
                if sub == diagonal_sub:
                    beta = jnp.where(below_diagonal, beta, 0.0)
                dstay = dsums[c][:, :TK] + dbefore
                dz_ref[h, rows, :] = ((dl[c] - beta * (dl[c] + dstay)) * SCALE).astype(BF16)
                dbefore_ref[c] = dbefore + dsums[c][:, TK:]

        @pl.loop(0, base // 2)
        def _(pair):
            step(2 * pair, even, odd, all_subs, None, all_subs, all_subs)
            step(2 * pair + 1, odd, even, all_subs, None, all_subs, all_subs)

        bufs = (even, odd)
        for j in range(NSUB):
            step(base + j, bufs[0], bufs[1], all_subs[j:], j, all_subs[j - 1:] if j else all_subs, all_subs[j + 1:])
            bufs = bufs[::-1]

        add_gradients(base + NSUB - 1, all_subs[NSUB - 1:])
        if phases is not None:
            pl.when(jnp.logical_and(p == NPAIR - 1, i == n_steps - 1))(phases[2])

    full = jax.ShapeDtypeStruct((S, NH * HD), F32)
    kwargs, operands = _with_comm(
        comm, [Q_ROWS_SPEC, K_ALL_SPEC, V_ALL_SPEC, PAIR_ROWS_SPEC, PAIR_TOTAL_SPEC],
        [PAIR_ROWS_SPEC, PAIR_ALL_SPEC, PAIR_ALL_SPEC], [full, full, full], [qkv, qkv, qkv, dout, totals],
        [pltpu.VMEM((2 * NSUB, RS, TK), F32)] * 6 + [pltpu.VMEM((2, TQ, TK), BF16)] * 2)
    return pl.pallas_call(
        body, name=name, grid=(NPAIR, n_steps),
        compiler_params=_cparams(("arbitrary", "arbitrary")), **kwargs,
    )(*operands)


def _proj_cols(first_col):
    base = first_col // LANES
    return pl.BlockSpec((S, LANES), lambda j: (0, base + j))


CONV_OUT_SPEC = pl.BlockSpec((S, LANES), lambda j: (0, j))
CONV_DOUT_SPEC = pl.BlockSpec((S, LANES), lambda j: (0, (NH * HD) // LANES + j))
CONV_W_SPEC = pl.BlockSpec((8, LANES), lambda j: (0, j))
CONV_B_SPEC = pl.BlockSpec((1, LANES), lambda j: (0, j))


def _shift_down(u, n):
    rows = lax.broadcasted_iota(jnp.int32, u.shape, 0)
    return jnp.where(rows >= n, pltpu.roll(u, n, 0), 0.0)


def _shift_up(u, n):
    rows = lax.broadcasted_iota(jnp.int32, u.shape, 0)
    return jnp.where(rows < S - n, pltpu.roll(u, S - n, 0), 0.0)


def conv_fwd(proj, cw8, cb, name):
    def body(bg_ref, cg_ref, hc_ref, w_ref, b_ref, o_ref):
        u = cg_ref[...] * hc_ref[...]
        w = w_ref[...]
        y = w[0:1, :] * _shift_down(u, 2) + w[1:2, :] * _shift_down(u, 1) + w[2:3, :] * u + b_ref[...]
        o_ref[...] = bg_ref[...] * y

    return pl.pallas_call(
        body, name=name, grid=(CW // LANES,),
        in_specs=[_proj_cols(0), _proj_cols(CW), _proj_cols(2 * CW), CONV_W_SPEC, CONV_B_SPEC],
        out_specs=CONV_OUT_SPEC, out_shape=jax.ShapeDtypeStruct((S, CW), F32),
        compiler_params=_cparams(("parallel",)),
    )(proj, proj, proj, cw8, cb)


def conv_bwd(proj, dout, cw8, cb, name):
    def body(bg_ref, cg_ref, hc_ref, do_ref, w_ref, b_ref, dbg_ref, dcg_ref, dhc_ref, dw_ref, db_ref):
        cg, hc, do = cg_ref[...], hc_ref[...], do_ref[...]
        w = w_ref[...]
        u = cg * hc
        u1, u2 = _shift_down(u, 1), _shift_down(u, 2)
        y = w[0:1, :] * u2 + w[1:2, :] * u1 + w[2:3, :] * u + b_ref[...]
        dbg_ref[...] = do * y
        dy = do * bg_ref[...]
        db_ref[...] = jnp.sum(dy, axis=0, keepdims=True)
        dw_ref[...] = jnp.concatenate(
            [jnp.sum(dy * u2, axis=0, keepdims=True), jnp.sum(dy * u1, axis=0, keepdims=True),
             jnp.sum(dy * u, axis=0, keepdims=True), jnp.zeros((5, LANES), F32)], axis=0)
        du = w[2:3, :] * dy + w[1:2, :] * _shift_up(dy, 1) + w[0:1, :] * _shift_up(dy, 2)
        dcg_ref[...] = du * hc
        dhc_ref[...] = du * cg

    full = jax.ShapeDtypeStruct((S, CW), F32)
    return pl.pallas_call(
        body, name=name, grid=(CW // LANES,),
        in_specs=[_proj_cols(0), _proj_cols(CW), _proj_cols(2 * CW), CONV_DOUT_SPEC, CONV_W_SPEC, CONV_B_SPEC],
        out_specs=[CONV_OUT_SPEC, CONV_OUT_SPEC, CONV_OUT_SPEC, CONV_W_SPEC, CONV_B_SPEC],
        out_shape=[full, full, full, jax.ShapeDtypeStruct((8, CW), F32), jax.ShapeDtypeStruct((1, CW), F32)],
        compiler_params=_cparams(("parallel",)),
    )(proj, proj, proj, dout, cw8, cb)


GELU_K = math.sqrt(2.0 / math.pi)
GELU_C = 0.044715


def _gelu(x):
    return 0.5 * x * (1.0 + jnp.tanh(GELU_K * (x + GELU_C * (x * x * x))))


def _gelu_grad(x):
    t = jnp.tanh(GELU_K * (x + GELU_C * (x * x * x)))
    return 0.5 * (1.0 + t) + 0.5 * x * (1.0 - t * t) * (GELU_K * (1.0 + 3.0 * GELU_C * (x * x)))


def _sg_masks():
    row = lax.broadcasted_iota(jnp.int32, (T, T), 0)
    col = lax.broadcasted_iota(jnp.int32, (T, T), 1)
    causal = jnp.right_shift(row, 6) >= jnp.right_shift(col, 6)
    head_of_col = jnp.right_shift(lax.broadcasted_iota(jnp.int32, (T, CW), 1), 6)
    return causal, head_of_col


def _sg_weights(sw_ref, causal):
    return [jnp.where(causal, sw_ref[h], 0.0).astype(BF16) for h in range(SG_HEADS)]


def _sg_mixed(vnb, weights, bias, head_of_col):
    mixed = bias
    for h in range(SG_HEADS):
        mh = jnp.dot(weights[h], vnb, preferred_element_type=F32)
        mixed = mixed + jnp.where(head_of_col == h, mh, 0.0)
    return mixed


SG_WINDOWS = 4
SG_ROWS = SG_WINDOWS * T
SG_U_SPEC = pl.BlockSpec((SG_ROWS, CW), lambda n: (n, 3))
SG_V_SPEC = pl.BlockSpec((SG_ROWS, CW), lambda n: (n, 4))
SG_ROW_SPEC = pl.BlockSpec((SG_ROWS, CW), lambda n: (n, 0))
SG_DOUT_SPEC = pl.BlockSpec((SG_ROWS, CW), lambda n: (n, 3))
SG_G_SPEC = pl.BlockSpec((1, CW), lambda n: (0, 0))
SG_W_SPEC = pl.BlockSpec((SG_HEADS, T, T), lambda n: (0, 0, 0))
SG_BIAS_SPEC = pl.BlockSpec((T, CW), lambda n: (0, 0))


def sg_fwd(proj, gn, sw, bias, name):
    def body(u_ref, v_ref, g_ref, sw_ref, bias_ref, o_ref):
        causal, head_of_col = _sg_masks()
        weights = _sg_weights(sw_ref, causal)
        for wdw in range(SG_WINDOWS):
            rows = pl.ds(wdw * T, T)
            gv = _gelu(v_ref[rows, :])
            rstd = lax.rsqrt(jnp.mean(gv * gv, axis=-1, keepdims=True) + EPS)
            vnb = ((gv * rstd) * g_ref[...]).astype(BF16)
            mixed = _sg_mixed(vnb, weights, bias_ref[...], head_of_col)
            o_ref[rows, :] = _gelu(u_ref[rows, :]) * mixed

    return pl.pallas_call(
        body, name=name, grid=(S // SG_ROWS,),
        in_specs=[SG_U_SPEC, SG_V_SPEC, SG_G_SPEC, SG_W_SPEC, SG_BIAS_SPEC],
        out_specs=SG_ROW_SPEC, out_shape=jax.ShapeDtypeStruct((S, CW), F32),
        compiler_params=_cparams(("parallel",)),
    )(proj, proj, gn, sw, bias)


def sg_bwd(proj, dout, gn, sw, bias, name):
    def body(u_ref, v_ref, do_ref, g_ref, sw_ref, bias_ref, du_ref, dv_ref, dg_ref, dsw_ref, dbias_ref):
        @pl.when(pl.program_id(0) == 0)
        def _():
            dg_ref[...] = jnp.zeros_like(dg_ref)
            dsw_ref[...] = jnp.zeros_like(dsw_ref)
            dbias_ref[...] = jnp.zeros_like(dbias_ref)

        causal, head_of_col = _sg_masks()
        weights = _sg_weights(sw_ref, causal)
        gnv = g_ref[...]
        for wdw in range(SG_WINDOWS):
            rows = pl.ds(wdw * T, T)
            uv, vv, do = u_ref[rows, :], v_ref[rows, :], do_ref[rows, :]
            gv = _gelu(vv)
            rstd = lax.rsqrt(jnp.mean(gv * gv, axis=-1, keepdims=True) + EPS)
            xhat = gv * rstd
            vnb = (xhat * gnv).astype(BF16)
            mixed = _sg_mixed(vnb, weights, bias_ref[...], head_of_col)
            du_ref[rows, :] = (do * mixed) * _gelu_grad(uv)
            dmix = do * _gelu(uv)
            dbias_ref[...] += dmix
            dmixb = dmix.astype(BF16)
            dvn = jnp.zeros((T, CW), F32)
            for h in range(SG_HEADS):
                dvh = lax.dot_general(weights[h], dmixb, (((0,), (0,)), ((), ())), preferred_element_type=F32)
                dvn = dvn + jnp.where(head_of_col == h, dvh, 0.0)
                dmh = jnp.where(head_of_col == h, dmixb, jnp.zeros_like(dmixb))
                dwh = lax.dot_general(dmh, vnb, (((1,), (1,)), ((), ())), preferred_element_type=F32)
                dsw_ref[h] += jnp.where(causal, dwh, 0.0)
            dg_ref[...] += jnp.sum(dvn * xhat, axis=0, keepdims=True)
            dxhat = dvn * gnv
            dgv = rstd * (dxhat - xhat * jnp.mean(dxhat * xhat, axis=-1, keepdims=True))
            dv_ref[rows, :] = dgv * _gelu_grad(vv)

    full = jax.ShapeDtypeStruct((S, CW), F32)
    return pl.pallas_call(
        body, name=name, grid=(S // SG_ROWS,),
        in_specs=[SG_U_SPEC, SG_V_SPEC, SG_DOUT_SPEC, SG_G_SPEC, SG_W_SPEC, SG_BIAS_SPEC],
        out_specs=[SG_ROW_SPEC, SG_ROW_SPEC, SG_G_SPEC, SG_W_SPEC, SG_BIAS_SPEC],
        out_shape=[full, full, jax.ShapeDtypeStruct((1, CW), F32),
                   jax.ShapeDtypeStruct((SG_HEADS, T, T), F32), jax.ShapeDtypeStruct((T, CW), F32)],
        compiler_params=_cparams(("arbitrary",)),
    )(proj, proj, dout, gn, sw, bias)


ADA_COLS = NMOD * D // NDEV


def ada_fwd(c_all, ada_w, ada_b_mine, name):
    def body(c_ref, w_ref, b_ref, o_ref, ca_ref):
        cv = c_ref[...]
        ca = cv * (1.0 / (1.0 + jnp.exp(-cv)))
        ca_ref[...] = ca
        cab = ca.astype(BF16)
        for l in range(L):
            o_ref[l] = jnp.dot(cab, w_ref[l].astype(BF16), preferred_element_type=F32) + b_ref[l]

    return pl.pallas_call(
        body, name=name,
        out_shape=[jax.ShapeDtypeStruct((L, NDEV, ADA_COLS), F32), jax.ShapeDtypeStruct((NDEV, D), F32)],
        compiler_params=_cparams(),
    )(c_all, ada_w, ada_b_mine)


def ada_bwd(ca, dmod_cols, name):
    def body(ca_ref, dm_ref, o_ref):
        cab = ca_ref[...].astype(BF16)
        for l in range(L):
            o_ref[l] = lax.dot_general(cab, dm_ref[l].astype(BF16), (((0,), (0,)), ((), ())),
                                       preferred_element_type=F32)

    return pl.pallas_call(
        body, name=name, out_shape=jax.ShapeDtypeStruct((L, D, ADA_COLS), F32),
        compiler_params=_cparams(),
    )(ca, dmod_cols)


def _adamw(w, g, m, v):
    m = B1 * m + (1.0 - B1) * g
    v = B2 * v + (1.0 - B2) * (g * g)
    m_hat = m / BC1
    v_hat = v / BC2
    delta = -LR * (m_hat / (jnp.sqrt(v_hat) + AEPS) + WD * w)
    return delta, m, v


VEC_ROWS_PER_LAYER = 8
VEC_FINAL_ROW = L * VEC_ROWS_PER_LAYER
VEC_ROWS = VEC_FINAL_ROW + 8
W256_TAPS, W256_CONV_B, W256_GN = 0, 8, 9
W256_ROWS_PER_LAYER = 16


def small_update(vec_all, w256_all, sb_all, sw_all, params, name):
    n_par = len(params)

    def body(*refs):
        vec_ref, w256_ref, sb_ref = refs[:3]
        sw_refs = refs[3:3 + L]
        par_refs = [refs[3 + L + 3 * k:3 + L + 3 * k + 3] for k in range(n_par)]
        out = refs[3 + L + 3 * n_par:]
        out_par = [out[4 * k:4 * k + 4] for k in range(n_par)]
        loss_ref, taps_ref = out[4 * n_par:]

        def total(ref, idx):
            acc = ref[(0,) + idx].astype(F32)
            for d in range(1, NDEV):
                acc = acc + ref[(d,) + idx].astype(F32)
            return acc

        def update(k, region, g):
            w_ref, m_ref, v_ref = par_refs[k]
            g_ref, d_ref, nm_ref, nv_ref = out_par[k]
            delta, nm, nv = _adamw(w_ref[region], g, m_ref[region], v_ref[region])
            g_ref[region] = g
            d_ref[region] = delta
            nm_ref[region] = nm
            nv_ref[region] = nv

        for l in range(L):
            base = l * VEC_ROWS_PER_LAYER
            for k in range(NMOD):
                update(0, (slice(l, l + 1), slice(k * D, (k + 1) * D)), total(vec_ref, (slice(base + k, base + k + 1),)))
            update(1, (slice(l, l + 1),), total(vec_ref, (slice(base + 6, base + 7),)))
            update(2, (slice(l, l + 1),), total(vec_ref, (slice(base + 7, base + 8),)))
            wbase = l * W256_ROWS_PER_LAYER
            update(4, (slice(l, l + 1),), total(w256_ref, (slice(wbase + W256_CONV_B, wbase + W256_CONV_B + 1),)))
            update(5, (slice(l, l + 1),), total(w256_ref, (slice(wbase + W256_GN, wbase + W256_GN + 1),)))
            update(6, (l,), total(sw_refs[l], ()))
            update(7, (l,), total(sb_ref, (slice(l * SG_HEADS, (l + 1) * SG_HEADS),)))
            taps_ref[l] = total(w256_ref, (slice(wbase + W256_TAPS, wbase + W256_TAPS + 8),))
        update(3, (slice(0, 1),), total(vec_ref, (slice(VEC_FINAL_ROW, VEC_FINAL_ROW + 1),)))
        loss_ref[...] = total(vec_ref, (slice(VEC_FINAL_ROW + 1, VEC_FINAL_ROW + 2), slice(0, LANES)))

    out_shape = []
    for w, _, _ in params:
        out_shape += [jax.ShapeDtypeStruct(w.shape, F32)] * 4
    out_shape += [jax.ShapeDtypeStruct((1, LANES), F32), jax.ShapeDtypeStruct((L, 8, CW), F32)]
    outs = pl.pallas_call(body, name=name, out_shape=out_shape, compiler_params=_cparams())(
        vec_all, w256_all, sb_all, *sw_all, *[a for p in params for a in p])
    return [outs[4 * k:4 * k + 4] for k in range(n_par)], outs[4 * n_par:]


def adamw_plain(w, g, m, v, tr, name):
    rows, cols = w.shape
    spec = pl.BlockSpec((tr, cols), lambda i: (i, 0))

    def body(w_ref, g_ref, m_ref, v_ref, d_ref, nm_ref, nv_ref):
        delta, nm, nv = _adamw(w_ref[...], g_ref[...], m_ref[...], v_ref[...])
        d_ref[...] = delta
        nm_ref[...] = nm
        nv_ref[...] = nv

    shp = jax.ShapeDtypeStruct((rows, cols), F32)
    return pl.pallas_call(
        body, name=name, grid=(rows // tr,), in_specs=[spec] * 4, out_specs=[spec] * 3,
        out_shape=[shp, shp, shp], compiler_params=_cparams(("parallel",)),
    )(w, g, m, v)


def adamw_reduce(w, parts, m, v, tr, name, tie=None):
    _, rows, cols = w.shape
    spec = pl.BlockSpec((None, tr, cols), lambda l, i: (l, i, 0))
    pspecs = [pl.BlockSpec((NDEV, tr, cols), lambda l, i, k=k: (0, jnp.where(l == k, i, 0), 0)) for k in range(L)]

    ties = [] if tie is None else [tie]

    def body(w_ref, p0_ref, p1_ref, m_ref, v_ref, *rest):
        g_ref, d_ref, nm_ref, nv_ref = rest[len(ties):]
        first_layer = pl.program_id(0) == 0
        g = jnp.zeros((tr, cols), F32)
        for d in range(NDEV):
            g = g + jnp.where(first_layer, p0_ref[d], p1_ref[d]).astype(F32)
        delta, nm, nv = _adamw(w_ref[...], g, m_ref[...], v_ref[...])
        g_ref[...] = g
        d_ref[...] = delta
        nm_ref[...] = nm
        nv_ref[...] = nv

    shp = jax.ShapeDtypeStruct(w.shape, F32)
    return pl.pallas_call(
        body, name=name, grid=(L, rows // tr),
        in_specs=[spec] + pspecs + [spec, spec] + [pl.BlockSpec(t.shape, lambda l, i: (0, 0)) for t in ties],
        out_specs=[spec] * 4, out_shape=[shp] * 4, compiler_params=_cparams(("parallel", "parallel")),
    )(w, *parts, m, v, *ties)


SHARD_IN = PROJ // NDEV


def shards_to_columns(shards, name):
    tr = 256

    def body(i_ref, o_ref):
        for d in range(NDEV):
            o_ref[:, d * SHARD_IN:(d + 1) * SHARD_IN] = i_ref[d]

    return pl.pallas_call(
        body, name=name, grid=(D // tr,),
        in_specs=[pl.BlockSpec((NDEV, tr, SHARD_IN), lambda i: (0, i, 0))],
        out_specs=pl.BlockSpec((tr, PROJ), lambda i: (i, 0)),
        out_shape=jax.ShapeDtypeStruct((D, PROJ), shards.dtype), compiler_params=_cparams(("parallel",)),
    )(shards)


def columns_to_shards(mat, name):
    tr = 256

    def body(i_ref, o_ref):
        for d in range(NDEV):
            o_ref[d] = i_ref[:, d * SHARD_IN:(d + 1) * SHARD_IN]

    return pl.pallas_call(
        body, name=name, grid=(D // tr,),
        in_specs=[pl.BlockSpec((tr, PROJ), lambda i: (i, 0))],
        out_specs=pl.BlockSpec((NDEV, tr, SHARD_IN), lambda i: (0, i, 0)),
        out_shape=jax.ShapeDtypeStruct((NDEV, D, SHARD_IN), mat.dtype), compiler_params=_cparams(("parallel",)),
    )(mat)


def _pad_rows(flat, rows):
    return jnp.pad(flat, (0, rows * LANES - flat.shape[0])).reshape(rows, LANES)


def kernel(x, c, ada_w, ada_b, norm_mix_g, norm_mlp_g, w_in, conv_w, conv_b, gmlp_norm_g, spatial_w, spatial_b, w_out, mlp_w1, mlp_w2, final_norm_g, loss_target, m_ada_w, m_ada_b, m_norm_mix_g, m_norm_mlp_g, m_w_in, m_conv_w, m_conv_b, m_gmlp_norm_g, m_spatial_w, m_spatial_b, m_w_out, m_mlp_w1, m_mlp_w2, m_final_norm_g, v_ada_w, v_ada_b, v_norm_mix_g, v_norm_mlp_g, v_w_in, v_conv_w, v_conv_b, v_gmlp_norm_g, v_spatial_w, v_spatial_b, v_w_out, v_mlp_w1, v_mlp_w2, v_final_norm_g):
    me = _lin(_my_pos())
    x0 = x[0]
    target = loss_target[0]
    conv_shard = conv_w.shape[-1]

    w_in_b, w_out_b, w1_b, w2_b = [w.astype(BF16) for w in (w_in, w_out, mlp_w1, mlp_w2)]
    pack0 = _pad_rows(jnp.concatenate([c.reshape(-1), conv_w.reshape(-1)]), 16)
    g0, gw_in0 = run_comm(Gather([pack0, w_in_b[0]]), "gather_first")
    g0 = g0.reshape(NDEV, 16 * LANES)
    c_all = g0[:, :D]
    conv_full = (g0[:, D:D + L * 3 * conv_shard].reshape(NDEV, L, 3, conv_shard)
                 .transpose(1, 2, 0, 3).reshape(L, 3, CW))


    W_in = [shards_to_columns(gw_in0, "w_in_columns0"), None]
    W_out, W1, W2 = [None] * L, [None] * L, [None] * L

    ada_b_mine = lax.dynamic_slice(ada_b, (0, me * ADA_COLS), (L, ADA_COLS)).reshape(L, 1, ADA_COLS)
    mod_part, c_act = ada_fwd(c_all, ada_w, ada_b_mine, "ada_fwd")
    gmod = run_comm(Gather([mod_part]), "gather_mod")[0]
    mod = lax.dynamic_index_in_dim(gmod, me, axis=2, keepdims=False)
    mod = mod.transpose(1, 0, 2).reshape(L, NMOD, 1, D)
    early_weights, token = start_copies([w_out_b[0]], me, "gather_early0_start", True, after=gmod)
    mod = tied(mod, token)

    cw8 = jnp.pad(conv_full, ((0, 0), (0, 5), (0, 0)))
    sg_bias = jnp.repeat(spatial_b.transpose(0, 2, 1), HD, axis=2)

    saved = []
    xl = x0
    for l in range(L):
        sh_m, sc_m, g_m, sh_f, sc_f, g_f = [mod[l, k] for k in range(NMOD)]
        h1 = normmod_fwd(xl, norm_mix_g[l:l + 1], sc_m, sh_m, f"norm_mix_fwd{l}")
        if l > 0:
            W_in[l] = shards_to_columns(finish_copies(w_in_handle, xl, f"gather_w_in{l}_wait")[0],
                                        f"w_in_columns{l}")
        qkv = mm_layer("proj_qkv", l, h1, W_in[l], out_dtypes=[BF16], cols=(0, QKV))[0]
        proj = mm_layer("proj_rest", l, h1, W_in[l], out_dtypes=[F32], cols=(QKV, REST))[0]
        a_out, a_tot, gw2, gw1 = attn_fwd(qkv, f"attn_fwd{l}", comm=Gather([w2_b[l], w1_b[l]]))
        gw_out, = finish_copies(early_weights, a_out, f"gather_early{l}_wait")
        W_out[l] = gw_out.reshape(D, D)
        W1[l] = gw1
        W2[l] = gw2.reshape(DFF, D)
        if l + 1 < L:
            w_in_handle, token = start_copies([w_in_b[l + 1]], me, f"gather_w_in{l + 1}_start", True, after=a_out)
            early_weights, token = start_copies([w_out_b[l + 1]], me, f"gather_early{l + 1}_start", True, after=token)
            g_m = tied(g_m, token)
        c_out = conv_fwd(proj, cw8[l], conv_b[l:l + 1], f"conv_fwd{l}")
        s_out = sg_fwd(proj, gmlp_norm_g[l:l + 1], spatial_w[l], sg_bias[l], f"sg_fwd{l}")
        cat = jnp.concatenate([a_out, c_out.astype(BF16), s_out.astype(BF16)], axis=1)
        mix, x1, h2 = mm_layer("mix", l, cat, W_out[l], out_dtypes=[BF16, F32, BF16], epilogue=_residual_then_norm,
                               extras=[(xl, "tile"), (g_m, "col"), (norm_mlp_g[l:l + 1], "col"), (sc_f, "col"),
                                       (sh_f, "col")])
        ra, r = mm_layer("mlp_up", l, h2, W1[l], out_dtypes=[BF16, BF16], b_blocks=True,
                         epilogue=lambda acc: (jnp.maximum(acc, 0.0), jnp.square(jnp.maximum(acc, 0.0))))
        m2, x2 = mm_layer("mlp_down", l, r, W2[l], out_dtypes=[BF16, F32],
                          epilogue=lambda acc, xr, g: (acc, xr + g * acc), extras=[(x1, "tile"), (g_f, "col")])
        saved.append(dict(x=xl, h1=h1, proj=proj, qkv=qkv, a_tot=a_tot, cat=cat, mix=mix,
                          x1=x1, h2=h2, ra=ra, r=r, m2=m2))
        xl = x2

    dx, loss_part, d_final_g, dm2, dg_f = loss_head(xl, target, final_norm_g.reshape(1, D),
                                                    (saved[L - 1]["m2"], mod[L - 1, NMOD - 1]), "loss_head")

    p_in, p_out, p_w1, p_w2 = [None] * L, [None] * L, [None] * L, [None] * L
    w_in_grads = [None] * L
    vec_rows, d_norm_mix, d_norm_mlp = [None] * L, [None] * L, [None] * L
    dcw8, d_conv_b, d_gn, d_sw, d_sb = [None] * L, [None] * L, [None] * L, [None] * L, [None] * L
    late_grads = [None] * L
    for l in reversed(range(L)):
        sv = saved[l]
        sh_m, sc_m, g_m, sh_f, sc_f, g_f = [mod[l, k] for k in range(NMOD)]
        da = mm_layer("mlp_down_dgrad", l, dm2, W2[l], out_dtypes=[BF16], trans_b=True,
                      epilogue=lambda acc, rav: (acc * (2.0 * rav.astype(F32)),), extras=[(sv["ra"], "tile")])[0]
        dW2 = mm_layer("mlp_down_wgrad", l, sv["r"], dm2, out_dtypes=[BF16], trans_a=True)[0]
        dW1 = mm_layer("mlp_up_wgrad", l, sv["h2"], da, out_dtypes=[BF16], trans_a=True, out_blocks=True)[0]
        dh2 = mm_layer("mlp_up_dgrad", l, da, W1[l], out_dtypes=[F32], trans_b=True, b_blocks=True)[0]
        dx1, dsc_f, dsh_f, d_norm_mlp[l], dmix, dg_m = normmod_bwd(
            sv["x1"], dh2, dx, norm_mlp_g[l:l + 1], sc_f, f"norm_mlp_bwd{l}", gate_next=(sv["mix"], g_m))
        dcat = mm_layer("mix_dgrad", l, dmix, W_out[l], out_dtypes=[F32], trans_b=True)[0]
        dW_out = mm_layer("mix_wgrad", l, sv["cat"], dmix, out_dtypes=[BF16], trans_a=True)[0]
        pieces_w2, pieces_out = dW2.reshape(NDEV, DFF // NDEV, D), dW_out.reshape(NDEV, D // NDEV, D)
        ride, late = ([pieces_w2, pieces_out], dW1) if l == L - 1 else ([pieces_w2, dW1], pieces_out)
        dq, dk, dv, *arrived = attn_bwd(sv["qkv"], dcat, sv["a_tot"], f"attn_bwd{l}", comm=Exchange(ride))
        p_w2[l] = arrived[0]
        (p_out if l == L - 1 else p_w1)[l] = arrived[1]
        late_grads[l], late_token = start_copies([late], me, f"exchange_late{l}_start", False, after=dq)
        dbg, dcg, dhc, dcw8[l], d_conv_b[l] = conv_bwd(sv["proj"], dcat, cw8[l], conv_b[l:l + 1], f"conv_bwd{l}")
        dus, dvs, d_gn[l], dsw, dbias = sg_bwd(sv["proj"], dcat, gmlp_norm_g[l:l + 1], spatial_w[l], sg_bias[l],
                                               f"sg_bwd{l}")
        d_sw[l] = dsw.astype(BF16)
        d_sb[l] = dbias.reshape(T, SG_HEADS, HD).sum(axis=2).T
        dproj = jnp.concatenate([dq, dk, dv, dbg, dcg, dhc, dus, dvs], axis=1).astype(BF16)
        dW_in = mm_layer("proj_wgrad", l, sv["h1"], dproj, out_dtypes=[BF16], trans_a=True,
                         extras=[(late_token, "tie")])[0]
        pieces = columns_to_shards(dW_in, f"w_in_grad_shards{l}")
        w_in_grads[l], token = start_copies([pieces], me, f"exchange_w_in{l}_start", False)
        dh1 = mm_layer("proj_dgrad", l, dproj, W_in[l], out_dtypes=[F32], trans_b=True, extras=[(token, "tie")])[0]
        below = (saved[l - 1]["m2"], mod[l - 1, NMOD - 1]) if l > 0 else None
        dx, dsc_m, dsh_m, d_norm_mix[l], *gated_below = normmod_bwd(
            sv["x"], dh1, dx1, tied(norm_mix_g[l:l + 1], token), sc_m, f"norm_mix_bwd{l}", gate_next=below)
        vec_rows[l] = [dsh_m, dsc_m, dg_m, dsh_f, dsc_f, dg_f, d_norm_mix[l], d_norm_mlp[l]]
        if l > 0:
            dm2, dg_f = gated_below

    grad_x = dx.reshape(1, S, D)

    g_w2, d_w2, nm_w2, nv_w2 = adamw_reduce(mlp_w2, p_w2, m_mlp_w2, v_mlp_w2, 256, "adamw_mlp_w2", tie=token)
    p_w1[L - 1] = finish_copies(late_grads[L - 1], d_w2, f"exchange_late{L - 1}_wait")[0]
    g_w1, d_w1, nm_w1, nv_w1 = adamw_reduce(mlp_w1, p_w1, m_mlp_w1, v_mlp_w1, 256, "adamw_mlp_w1", tie=token)

    vec_pack = jnp.concatenate([row for l in range(L) for row in vec_rows[l]]
                               + [d_final_g, loss_part, jnp.zeros((VEC_ROWS - VEC_FINAL_ROW - 2, D), F32)], axis=0)
    vec_pack, _ = lax.optimization_barrier((vec_pack, (d_w1, d_w2)))
    w256_pack = jnp.concatenate([blk for l in range(L) for blk in (
        dcw8[l], d_conv_b[l], d_gn[l], jnp.zeros((W256_ROWS_PER_LAYER - W256_GN - 1, CW), F32))], axis=0)
    vec_all, w256_all, sb_all, *sw_all = run_comm(
        Gather([vec_pack, w256_pack, jnp.concatenate(d_sb, axis=0)] + d_sw), "gather_small_grads")

    dmod_all = (vec_all[:, :VEC_FINAL_ROW].reshape(NDEV, L, VEC_ROWS_PER_LAYER, D)[:, :, :NMOD]
                .reshape(NDEV, L, NMOD * D))
    dmod_cols = lax.dynamic_slice(dmod_all, (0, 0, me * ADA_COLS), (NDEV, L, ADA_COLS)).transpose(1, 0, 2)
    g_ada_w = ada_bwd(c_act, dmod_cols, "ada_bwd")

    flat2 = lambda t: t.reshape(L * D, ADA_COLS)
    d_ada_w, nm_ada_w, nv_ada_w = [t.reshape(L, D, ADA_COLS) for t in adamw_plain(
        flat2(ada_w), flat2(g_ada_w), flat2(m_ada_w), flat2(v_ada_w), 256, "adamw_ada_w")]

    after = jnp.concatenate([t.reshape(-1)[:1] for t in (d_w1, d_w2, d_ada_w)])
    p_in = [finish_copies(w_in_grads[l], after, f"exchange_w_in{l}_wait")[0] for l in range(L)]
    p_out[0] = finish_copies(late_grads[0], after, "exchange_late0_wait")[0]
    g_w_in, d_w_in, nm_w_in, nv_w_in = adamw_reduce(w_in, p_in, m_w_in, v_w_in, 256, "adamw_w_in")
    g_w_out, d_w_out, nm_w_out, nv_w_out = adamw_reduce(w_out, p_out, m_w_out, v_w_out, 128, "adamw_w_out")

    as_row = lambda t: t.reshape(1, D)
    small_params = [(ada_b, m_ada_b, v_ada_b), (norm_mix_g, m_norm_mix_g, v_norm_mix_g),
                    (norm_mlp_g, m_norm_mlp_g, v_norm_mlp_g),
                    (as_row(final_norm_g), as_row(m_final_norm_g), as_row(v_final_norm_g)),
                    (conv_b, m_conv_b, v_conv_b), (gmlp_norm_g, m_gmlp_norm_g, v_gmlp_norm_g),
                    (spatial_w, m_spatial_w, v_spatial_w), (spatial_b, m_spatial_b, v_spatial_b)]
    updated, (loss_sum, taps_sum) = small_update(vec_all, w256_all, sb_all, sw_all, small_params, "small_update")
    loss = loss_sum[0, 0]
    u_ada_b, u_norm_mix, u_norm_mlp, u_final, u_conv_b, u_gn, u_sw, u_sb = updated
    u_final = [t.reshape(D) for t in u_final]
    g_conv_w = lax.dynamic_slice(taps_sum, (0, 0, me * conv_shard), (L, 3, conv_shard))
    flat_cw = lambda t: t.reshape(L * 3, conv_shard)
    u_conv_w = [g_conv_w] + [t.reshape(L, 3, conv_shard) for t in adamw_plain(
        flat_cw(conv_w), flat_cw(g_conv_w), flat_cw(m_conv_w), flat_cw(v_conv_w), L * 3, "adamw_conv_w")]
    small_sets = [u_ada_b, u_norm_mix, u_norm_mlp, u_conv_w, u_conv_b, u_gn, u_sw, u_sb, u_final]
    small_g, sd, snm, snv = [[u[k] for u in small_sets] for k in range(4)]

    def ordered(big, small):
        ada, win, wout, w1, w2 = big
        return [ada, small[0], small[1], small[2], win, small[3], small[4], small[5], small[6], small[7],
                wout, w1, w2, small[8]]

    grads = ordered([g_ada_w, g_w_in, g_w_out, g_w1, g_w2], small_g)
    deltas = ordered([d_ada_w, d_w_in, d_w_out, d_w1, d_w2], sd)
    new_m = ordered([nm_ada_w, nm_w_in, nm_w_out, nm_w1, nm_w2], snm)
    new_v = ordered([nv_ada_w, nv_w_in, nv_w_out, nv_w1, nv_w2], snv)
    return (loss, grad_x, *grads, *deltas, *new_m, *new_v)
```

```python
import functools
import math

import jax
import jax.numpy as jnp
from jax import lax
from jax.experimental import pallas as pl
from jax.experimental.pallas import tpu as pltpu

F32 = jnp.float32
BF16 = jnp.bfloat16
MESH = pl.DeviceIdType.MESH

S = 2048
D = 1024
L = 2
NDEV = 8
HD = 64
NH = 8
PROJ = 2816
DFF = 4096
NMOD = 6
EPS = 1e-6
T = 128
SG_HEADS = 4
LANES = 128
CW = 256
QKV = 3 * NH * HD
REST = PROJ - QKV

LR, B1, B2, AEPS, WD, STEP = 0.001, 0.9, 0.999, 1e-08, 0.01, 10
BC1 = 1.0 - B1 ** STEP
BC2 = 1.0 - B2 ** STEP

VMEM_LIMIT = 48 * 1024 * 1024

HBM_SPEC = pl.BlockSpec(memory_space=pltpu.HBM)


def _cparams(sem=None):
    return pltpu.CompilerParams(dimension_semantics=sem, vmem_limit_bytes=VMEM_LIMIT)


def _my_pos():
    return lax.axis_index("x"), lax.axis_index("y"), lax.axis_index("c")


def _lin(p):
    return 4 * p[0] + 2 * p[1] + p[2]


class Gather:
    def __init__(self, arrs):
        self.arrs = list(arrs)
        n = len(self.arrs)
        self.out_shape = [jax.ShapeDtypeStruct((NDEV,) + a.shape, a.dtype) for a in self.arrs]
        self.scratch = [pltpu.SemaphoreType.DMA((n, 7)), pltpu.SemaphoreType.DMA((n, 7)),
                        pltpu.SemaphoreType.DMA((n,))]

    def phases(self, ins, outs, sems):
        n = len(self.arrs)
        send_sems, recv_sems, local_sems = sems
        x, y, c = _my_pos()
        me, sibling = (x, y, c), (x, y, 1 - c)
        chips = [(1 - x, y), (x, 1 - y), (1 - x, 1 - y)]

        def copy(a, k, block, to, src=None):
            slot = outs[a].at[_lin(block)]
            return pltpu.make_async_remote_copy(
                src_ref=slot if src is None else src, dst_ref=slot,
                send_sem=send_sems.at[a, k], recv_sem=recv_sems.at[a, k],
                device_id=to, device_id_type=MESH)

        def mine(a):
            return pltpu.make_async_copy(ins[a], outs[a].at[_lin(me)], local_sems.at[a])

        def first(a):
            return [copy(a, 0, me, sibling, src=ins[a])] + [
                copy(a, 1 + j, me, (*chip, c), src=ins[a]) for j, chip in enumerate(chips)]

        def passed(a):
            return [copy(a, 4 + j, (*chip, c), sibling) for j, chip in enumerate(chips)]

        def start():
            for a in range(n):
                mine(a).start()
                for cp in first(a):
                    cp.start()

        def relay():
            for j, chip in enumerate(chips):
                for a in range(n):
                    copy(a, 1 + j, (*chip, c), me).wait_recv()
                    passed(a)[j].start()

        def finish():
            for a in range(n):
                copy(a, 0, sibling, me).wait_recv()
            for j, chip in enumerate(chips):
                for a in range(n):
                    copy(a, 4 + j, (*chip, 1 - c), me).wait_recv()
            for a in range(n):
                for cp in first(a) + passed(a):
                    cp.wait_send()
                mine(a).wait()

        return start, relay, finish


class Exchange:
    def __init__(self, arrs):
        self.arrs = list(arrs)
        n = len(self.arrs)
        self.out_shape = [jax.ShapeDtypeStruct(a.shape, a.dtype) for a in self.arrs]
        self.scratch = [pltpu.SemaphoreType.DMA((n, 7)), pltpu.SemaphoreType.DMA((n, 7)),
                        pltpu.SemaphoreType.DMA((n,))]

    def phases(self, ins, outs, sems):
        n = len(self.arrs)
        send_sems, recv_sems, local_sems = sems
        x, y, c = _my_pos()
        me = (x, y, c)

        def peer(mask):
            return (1 - x if mask & 4 else x, 1 - y if mask & 2 else y, 1 - c if mask & 1 else c)

        def copy(a, mask):
            return pltpu.make_async_remote_copy(
                src_ref=ins[a].at[_lin(peer(mask))], dst_ref=outs[a].at[_lin(me)],
                send_sem=send_sems.at[a, mask - 1], recv_sem=recv_sems.at[a, mask - 1],
                device_id=peer(mask), device_id_type=MESH)

        def arrival(a, mask):
            return pltpu.make_async_remote_copy(
                src_ref=ins[a].at[_lin(me)], dst_ref=outs[a].at[_lin(peer(mask))],
                send_sem=send_sems.at[a, mask - 1], recv_sem=recv_sems.at[a, mask - 1],
                device_id=peer(mask), device_id_type=MESH)

        def mine(a):
            return pltpu.make_async_copy(ins[a].at[_lin(me)], outs[a].at[_lin(me)], local_sems.at[a])

        def start():
            for a in range(n):
                mine(a).start()
            for mask in (4, 2, 6, 1, 5, 3, 7):
                for a in range(n):
                    copy(a, mask).start()

        def relay():
            pass

        def finish():
            for mask in range(1, 8):
                for a in range(n):
                    arrival(a, mask).wait_recv()
            for mask in range(1, 8):
                for a in range(n):
                    copy(a, mask).wait_send()
            for a in range(n):
                mine(a).wait()

        return start, relay, finish


def run_comm(plan, name):
    n = len(plan.arrs)

    def body(*refs):
        start, relay, finish = plan.phases(refs[:n], refs[n:2 * n], refs[2 * n:])
        start()
        relay()
        finish()

    outs = pl.pallas_call(
        body, name=name, out_shape=plan.out_shape,
        in_specs=[HBM_SPEC] * n, out_specs=[HBM_SPEC] * n, scratch_shapes=plan.scratch,
    )(*plan.arrs)
    return list(outs)


SEM_SPEC = pl.BlockSpec(memory_space=pltpu.SEMAPHORE)
DATAFLOW = pltpu.SideEffectType.DATAFLOW_SIDE_EFFECTING


def _peer_copies(src_ref, land_ref, send_sems, recv_sems, first, same_block):
    x, y, c = _my_pos()
    me = (x, y, c)
    sends, arrivals = [], []
    for mask in (4, 2, 6, 1, 5, 3, 7):
        peer = (1 - x if mask & 4 else x, 1 - y if mask & 2 else y, 1 - c if mask & 1 else c)
        sends.append(pltpu.make_async_remote_copy(
            src_ref=src_ref if same_block else src_ref.at[_lin(peer)], dst_ref=land_ref.at[_lin(me)],
            send_sem=send_sems.at[first + mask - 1], recv_sem=recv_sems.at[first + mask - 1], device_id=peer,
            device_id_type=MESH))
        arrivals.append(pltpu.make_async_remote_copy(
            src_ref=src_ref if same_block else src_ref.at[_lin(me)], dst_ref=land_ref.at[_lin(peer)],
            send_sem=send_sems.at[first + mask - 1], recv_sem=recv_sems.at[first + mask - 1], device_id=peer,
            device_id_type=MESH))
    return sends, arrivals


def start_copies(srcs, me, name, same_block, after=None):
    n = len(srcs)
    landings = []
    for src in srcs:
        own = src[None] if same_block else lax.dynamic_index_in_dim(src, me, axis=0, keepdims=True)
        landings.append(lax.dynamic_update_slice(lax.empty((NDEV,) + own.shape[1:], src.dtype), own,
                                                 (me,) + (0,) * (own.ndim - 1)))

    def body(*refs):
        send_sems, recv_sems = refs[-2 * n - 3], refs[-2 * n - 2]
        token = refs[-1]
        for k in range(n):
            sends, _ = _peer_copies(refs[2 * k], refs[2 * k + 1], send_sems, recv_sems, 7 * k, same_block)
            for cp in sends:
                cp.start()
        token[...] = jnp.zeros_like(token)

    hbm = lambda a: pltpu.HBM(a.shape, a.dtype)
    pairs = [a for pair in zip(srcs, landings) for a in pair]
    extra = [] if after is None else [after]
    sems = pltpu.SemaphoreType.DMA((7 * n,))
    send_sems, recv_sems, *thru, token = pl.pallas_call(
        body, name=name,
        out_shape=(sems, sems, *[hbm(a) for a in pairs], jax.ShapeDtypeStruct((8, LANES), F32)),
        in_specs=[HBM_SPEC] * (2 * n) + [pl.BlockSpec(memory_space=pl.ANY)] * len(extra),
        out_specs=(SEM_SPEC, SEM_SPEC, *[HBM_SPEC] * (2 * n), pl.BlockSpec(memory_space=pltpu.VMEM)),
        input_output_aliases={k: 2 + k for k in range(2 * n)},
        compiler_params=pltpu.CompilerParams(has_side_effects=DATAFLOW),
    )(*[pltpu.with_memory_space_constraint(a, pltpu.HBM) for a in pairs], *extra)
    return (send_sems, recv_sems, thru, same_block), token


def finish_copies(handle, after, name):
    send_sems, recv_sems, thru, same_block = handle
    n = len(thru) // 2

    def body(*refs):
        send_sems, recv_sems = refs[2 * n], refs[2 * n + 1]
        for k in range(n):
            sends, arrivals = _peer_copies(refs[2 * k], refs[2 * k + 1], send_sems, recv_sems, 7 * k, same_block)
            for cp in sends:
                cp.wait_send()
            for cp in arrivals:
                cp.wait_recv()

    hbm = lambda a: pltpu.HBM(a.shape, a.dtype)
    outs = pl.pallas_call(
        body, name=name, out_shape=tuple(hbm(a) for a in thru),
        in_specs=[HBM_SPEC] * (2 * n) + [SEM_SPEC, SEM_SPEC, pl.BlockSpec(memory_space=pl.ANY)],
        out_specs=tuple([HBM_SPEC] * (2 * n)), input_output_aliases={k: k for k in range(2 * n)},
        compiler_params=pltpu.CompilerParams(has_side_effects=DATAFLOW),
    )(*thru, send_sems, recv_sems, after)
    return [outs[2 * k + 1] for k in range(n)]


def tied(x, token):
    return x + token[0:1, 0:1].astype(x.dtype)


MM_TILES = {
    "proj_qkv": (S, 512), "proj_rest": (S, 256), "mix": (512, D), "mlp_up": (S, 512), "mlp_down": (1024, 256),
    "mlp_down_dgrad": (S, 1024), "mlp_down_wgrad": (1024, 1024), "mlp_up_wgrad": (1024, 512),
    "mlp_up_dgrad": (1024, 512), "mix_dgrad": (1024, 512), "mix_wgrad": (512, 1024),
    "proj_wgrad": (1024, PROJ // 2), "proj_dgrad": (1024, 512),
}


def mm_layer(kind, l, a, b, **kw):
    tm, tn = MM_TILES[kind]
    return mm(a, b, tm=tm, tn=tn, name=f"{kind}{l}", **kw)


def mm(a, b, *, tm, tn, out_dtypes, epilogue=None, extras=(), name, trans_a=False, trans_b=False,
       cols=None, b_blocks=False, out_blocks=False):
    if trans_a:
        kdim, m = a.shape
    else:
        m, kdim = a.shape
    shard = b.shape[-1] if b_blocks else None
    if b_blocks:
        full = (b.shape[1], NDEV * shard)
    else:
        full = b.shape
    first, ncols = cols if cols is not None else (0, full[0] if trans_b else full[1])
    assert full[1 if trans_b else 0] == kdim and m % tm == 0 and ncols % tn == 0 and first % tn == 0
    j0 = first // tn
    if trans_a:
        a_spec = pl.BlockSpec((kdim, tm), lambda i, j: (0, i))
    else:
        a_spec = pl.BlockSpec((tm, kdim), lambda i, j: (i, 0))
    if b_blocks and trans_b:
        b_spec = pl.BlockSpec((NDEV, tn, shard), lambda i, j: (0, j0 + j, 0))
    elif b_blocks:
        assert tn == shard
        b_spec = pl.BlockSpec((None, kdim, tn), lambda i, j: (j0 + j, 0, 0))
    elif trans_b:
        b_spec = pl.BlockSpec((tn, kdim), lambda i, j: (j0 + j, 0))
    else:
        b_spec = pl.BlockSpec((kdim, tn), lambda i, j: (0, j0 + j))
    if out_blocks:
        assert tn * NDEV == ncols
        out_spec = pl.BlockSpec((None, tm, tn), lambda i, j: (j, i, 0))
        out_dims = (NDEV, m, tn)
    else:
        out_spec = pl.BlockSpec((tm, tn), lambda i, j: (i, j))
        out_dims = (m, ncols)
    ex_specs = []
    for arr, kind in extras:
        if kind == "tile":
            ex_specs.append(pl.BlockSpec((tm, tn), lambda i, j: (i, j)))
        elif kind == "col":
            ex_specs.append(pl.BlockSpec((1, tn), lambda i, j: (0, j)))
        else:
            ex_specs.append(pl.BlockSpec(arr.shape, lambda i, j: (0, 0)))
    n_ex, n_out = len(extras), len(out_dtypes)
    used = [k for k, (_, kind) in enumerate(extras) if kind != "tie"]

    def body(a_ref, b_ref, *rest):
        ex_refs, out_refs = rest[:n_ex], rest[n_ex:]
        if trans_a:
            acc = lax.dot_general(a_ref[...], b_ref[...], (((0,), (0,)), ((), ())),
                                  preferred_element_type=F32)
        elif trans_b and b_blocks:
            acc = jnp.zeros((tm, tn), F32)
            for d in range(NDEV):
                acc = acc + lax.dot_general(a_ref[:, d * shard:(d + 1) * shard], b_ref[d],
                                            (((1,), (1,)), ((), ())), preferred_element_type=F32)
        elif trans_b:
            acc = lax.dot_general(a_ref[...], b_ref[...], (((1,), (1,)), ((), ())),
                                  preferred_element_type=F32)
        else:
            acc = jnp.dot(a_ref[...], b_ref[...], preferred_element_type=F32)
        outs = (acc,) if epilogue is None else epilogue(acc, *[ex_refs[k][...] for k in used])
        for o_ref, val in zip(out_refs, outs):
            o_ref[...] = val.astype(o_ref.dtype)

    outs = pl.pallas_call(
        body, name=name, grid=(m // tm, ncols // tn),
        in_specs=[a_spec, b_spec] + ex_specs,
        out_specs=[out_spec for _ in range(n_out)],
        out_shape=[jax.ShapeDtypeStruct(out_dims, dt) for dt in out_dtypes],
        compiler_params=_cparams(("parallel", "parallel")),
    )(a, b, *[arr for arr, _ in extras])
    return list(outs)


TR = 512

ROW_SPEC = pl.BlockSpec((TR, D), lambda i: (i, 0))
VEC_SPEC = pl.BlockSpec((1, D), lambda i: (0, 0))


def _residual_then_norm(acc, xr, gate, g, sc, sh):
    x_new = xr + gate * acc
    rstd = lax.rsqrt(jnp.mean(x_new * x_new, axis=-1, keepdims=True) + EPS)
    return acc, x_new, ((x_new * rstd) * g) * (1.0 + sc) + sh


def normmod_fwd(x, g, sc, sh, name):
    def body(x_ref, g_ref, sc_ref, sh_ref, o_ref):
        xv = x_ref[...]
        rstd = lax.rsqrt(jnp.mean(xv * xv, axis=-1, keepdims=True) + EPS)
        n = (xv * rstd) * g_ref[...]
        o_ref[...] = (n * (1.0 + sc_ref[...]) + sh_ref[...]).astype(o_ref.dtype)

    return pl.pallas_call(
        body, name=name, grid=(S // TR,),
        in_specs=[ROW_SPEC, VEC_SPEC, VEC_SPEC, VEC_SPEC], out_specs=ROW_SPEC,
        out_shape=jax.ShapeDtypeStruct((S, D), BF16),
        compiler_params=_cparams(("parallel",)),
    )(x, g, sc, sh)


def _gate_next(dxv, refs):
    br_ref, gate_ref, dbr_ref, dgate_ref = refs

    @pl.when(pl.program_id(0) == 0)
    def _():
        dgate_ref[...] = jnp.zeros_like(dgate_ref)

    dbr_ref[...] = (dxv * gate_ref[...]).astype(dbr_ref.dtype)
    dgate_ref[...] += jnp.sum(dxv * br_ref[...], axis=0, keepdims=True)


GATE_NEXT_IN = [ROW_SPEC, VEC_SPEC]
GATE_NEXT_OUT = [ROW_SPEC, VEC_SPEC]
GATE_NEXT_SHAPES = [jax.ShapeDtypeStruct((S, D), BF16), jax.ShapeDtypeStruct((1, D), F32)]


def normmod_bwd(x, dh, dres, g, sc, name, gate_next=None):
    nxt = 2 if gate_next else 0

    def body(x_ref, dh_ref, dres_ref, g_ref, sc_ref, *rest):
        nxt_in, (dx_ref, dsc_ref, dsh_ref, dg_ref), nxt_out = rest[:nxt], rest[nxt:nxt + 4], rest[nxt + 4:]

        @pl.when(pl.program_id(0) == 0)
        def _():
            dsc_ref[...] = jnp.zeros_like(dsc_ref)
            dsh_ref[...] = jnp.zeros_like(dsh_ref)
            dg_ref[...] = jnp.zeros_like(dg_ref)

        xv, dh = x_ref[...], dh_ref[...]
        gv = g_ref[...]
        rstd = lax.rsqrt(jnp.mean(xv * xv, axis=-1, keepdims=True) + EPS)
        xhat = xv * rstd
        dn = dh * (1.0 + sc_ref[...])
        dxhat = dn * gv
        dxv = dres_ref[...] + rstd * (dxhat - xhat * jnp.mean(dxhat * xhat, axis=-1, keepdims=True))
        dx_ref[...] = dxv
        dsc_ref[...] += jnp.sum(dh * (xhat * gv), axis=0, keepdims=True)
        dsh_ref[...] += jnp.sum(dh, axis=0, keepdims=True)
        dg_ref[...] += jnp.sum(dn * xhat, axis=0, keepdims=True)
        if gate_next:
            _gate_next(dxv, nxt_in + nxt_out)

    vec_out = jax.ShapeDtypeStruct((1, D), F32)
    on = bool(gate_next)
    return pl.pallas_call(
        body, name=name, grid=(S // TR,),
        in_specs=[ROW_SPEC, ROW_SPEC, ROW_SPEC, VEC_SPEC, VEC_SPEC] + GATE_NEXT_IN * on,
        out_specs=[ROW_SPEC, VEC_SPEC, VEC_SPEC, VEC_SPEC] + GATE_NEXT_OUT * on,
        out_shape=[jax.ShapeDtypeStruct((S, D), F32), vec_out, vec_out, vec_out] + GATE_NEXT_SHAPES * on,
        compiler_params=_cparams(("arbitrary",)),
    )(x, dh, dres, g, sc, *(gate_next or ()))


def loss_head(x, target, g, gate_next, name):
    def body(x_ref, t_ref, g_ref, br_ref, gate_ref, dx_ref, loss_ref, dg_ref, dbr_ref, dgate_ref):
        @pl.when(pl.program_id(0) == 0)
        def _():
            loss_ref[...] = jnp.zeros_like(loss_ref)
            dg_ref[...] = jnp.zeros_like(dg_ref)

        xv, gv = x_ref[...], g_ref[...]
        rstd = lax.rsqrt(jnp.mean(xv * xv, axis=-1, keepdims=True) + EPS)
        xhat = xv * rstd
        err = xhat * gv - t_ref[...]
        loss_ref[...] += jnp.sum(err * err) * (0.5 / D)
        dy = err * (1.0 / D)
        dg_ref[...] += jnp.sum(dy * xhat, axis=0, keepdims=True)
        dxhat = dy * gv
        dxv = rstd * (dxhat - xhat * jnp.mean(dxhat * xhat, axis=-1, keepdims=True))
        dx_ref[...] = dxv
        _gate_next(dxv, (br_ref, gate_ref, dbr_ref, dgate_ref))

    return pl.pallas_call(
        body, name=name, grid=(S // TR,),
        in_specs=[ROW_SPEC, ROW_SPEC, VEC_SPEC] + GATE_NEXT_IN,
        out_specs=[ROW_SPEC, VEC_SPEC, VEC_SPEC] + GATE_NEXT_OUT,
        out_shape=[jax.ShapeDtypeStruct((S, D), F32), jax.ShapeDtypeStruct((1, D), F32),
                   jax.ShapeDtypeStruct((1, D), F32)] + GATE_NEXT_SHAPES,
        compiler_params=_cparams(("arbitrary",)),
    )(x, target, g, *gate_next)


TQ = 512
RS = 128
NSUB = TQ // RS
TK = 128


def _dot_hilo(a, tri_twice):
    hi = a.astype(BF16)
    lo = (a - hi.astype(F32)).astype(BF16)
    return jnp.dot(jnp.concatenate([hi, lo], axis=1), tri_twice, preferred_element_type=F32)


def _log_stay(z):
    neg = -z
    return jnp.minimum(neg, 0.0) - jnp.log(1.0 + jnp.exp(jnp.minimum(z, neg)))


def _tri_and_ones(kind):
    row = jnp.bitwise_and(lax.broadcasted_iota(jnp.int32, (2 * TK, 2 * TK), 0), TK - 1)
    col = lax.broadcasted_iota(jnp.int32, (2 * TK, 2 * TK), 1)
    tri = {"after": row > col, "upto": row <= col, "before": row < col}[kind]
    return jnp.logical_or(col >= TK, tri).astype(BF16)


NPAIR = NH // 2
SCALE = HD ** -0.5


def _pair_specs(first_block):
    rows = pl.BlockSpec((TQ, LANES), lambda p, i: (i, first_block + p))
    whole = pl.BlockSpec((S, LANES), lambda p, i: (0, first_block + p))
    return rows, whole


Q_ROWS_SPEC, _ = _pair_specs(0)
_, K_ALL_SPEC = _pair_specs(NPAIR)
_, V_ALL_SPEC = _pair_specs(2 * NPAIR)
PAIR_ROWS_SPEC = pl.BlockSpec((TQ, LANES), lambda p, i: (i, p))
PAIR_ALL_SPEC = pl.BlockSpec((S, LANES), lambda p, i: (0, p))
PAIR_TOTAL_SPEC = pl.BlockSpec((2, TQ, TK), lambda p, i: (p, i, 0))


def _head_halves(x):
    first = lax.broadcasted_iota(jnp.int32, x.shape, 1) < HD
    zero = jnp.zeros_like(x)
    return jnp.where(first, x, zero), jnp.where(first, zero, x)


def _join_heads(a, b):
    return jnp.where(lax.broadcasted_iota(jnp.int32, a.shape, 1) < HD, a, b)


def _comm_hooks(comm, refs, n_in, n_out, n_scratch):
    nc = len(comm.arrs) if comm is not None else 0
    ins, cin = refs[:n_in], refs[n_in:n_in + nc]
    outs = refs[n_in + nc:n_in + nc + n_out]
    cout = refs[n_in + nc + n_out:n_in + 2 * nc + n_out]
    scratch = refs[n_in + 2 * nc + n_out:n_in + 2 * nc + n_out + n_scratch]
    sems = refs[n_in + 2 * nc + n_out + n_scratch:]
    phases = comm.phases(cin, cout, sems) if comm is not None else None
    return ins, outs, scratch, phases


def _with_comm(comm, in_specs, out_specs, out_shape, operands, scratch):
    if comm is None:
        return dict(in_specs=in_specs, out_specs=out_specs, out_shape=out_shape, scratch_shapes=scratch), operands
    nc = len(comm.arrs)
    return dict(in_specs=in_specs + [HBM_SPEC] * nc, out_specs=out_specs + [HBM_SPEC] * nc,
                out_shape=out_shape + comm.out_shape, scratch_shapes=scratch + comm.scratch), operands + comm.arrs


def attn_fwd(qkv, name, comm=None):
    n_steps = S // TQ

    def body(*refs):
        (q_ref, k_ref, v_ref), (o_ref, r_ref), (acc_ref, z_even, z_odd, w_ref), phases = _comm_hooks(
            comm, refs, 3, 2, 4)
        p = pl.program_id(0)
        i = pl.program_id(1)
        if phases is not None:
            pl.when(jnp.logical_and(p == 0, i == 0))(phases[0])
            pl.when(jnp.logical_and(p == NPAIR - 1, i == n_steps - 1))(phases[1])
        chains = [(sub, h) for sub in range(NSUB) for h in range(2)]
        q_sub = [_head_halves(q_ref[pl.ds(sub * RS, RS), :] * SCALE) for sub in range(NSUB)]
        after = _tri_and_ones("after")
        below_diagonal = (lax.broadcasted_iota(jnp.int32, (RS, TK), 1)
                          < lax.broadcasted_iota(jnp.int32, (RS, TK), 0))
        base = i * NSUB
        all_subs = list(range(NSUB))

        acc_ref[...] = jnp.zeros_like(acc_ref)
        r_ref[...] = jnp.zeros_like(r_ref)
        w_ref[...] = jnp.zeros_like(w_ref)

        def key_rows(block):
            return pl.ds(pl.multiple_of(block * TK, TK), TK)

        def store_scores(z_ref, block, subs):
            kb = k_ref[key_rows(block), :]
            for c, (sub, h) in enumerate(chains):
                if sub in subs:
                    z_ref[c] = lax.dot_general(q_sub[sub][h], kb, (((1,), (1,)), ((), ())),
                                               preferred_element_type=F32)

        def add_weighted_values(block, subs):
            vb = v_ref[key_rows(block), :]
            for sub in subs:
                acc_ref[pl.ds(sub * RS, RS), :] += _join_heads(*[
                    jnp.dot(w_ref[2 * sub + h], vb, preferred_element_type=F32) for h in range(2)])

        def step(block, z_ref, z_next_ref, subs, diagonal_sub, prev_subs, next_subs):
            if prev_subs:
                add_weighted_values(block + 1, prev_subs)
            if next_subs:
                store_scores(z_next_ref, jnp.maximum(block - 1, 0), next_subs)
            active = [(c, sub, h) for c, (sub, h) in enumerate(chains) if sub in subs]
            ls, sums = {}, {}
            for c, sub, h in active:
                ls[c] = _log_stay(z_ref[c])
                sums[c] = _dot_hilo(jnp.where(below_diagonal, ls[c], 0.0) if sub == diagonal_sub else ls[c], after)
            for c, sub, h in active:
                rows = pl.ds(sub * RS, RS)
                later = r_ref[h, rows, :]
                w = jnp.exp(z_ref[c] + ls[c] + (sums[c][:, :TK] + later))
                if sub == diagonal_sub:
                    w = jnp.where(below_diagonal, w, 0.0)
                w_ref[c] = w.astype(BF16)
                r_ref[h, rows, :] = later + sums[c][:, TK:]

        store_scores(z_even, base + NSUB - 1, [NSUB - 1])
        buffers = (z_even, z_odd)
        for j in reversed(range(NSUB)):
            subs = all_subs[j:]
            step(base + j, buffers[0], buffers[1], subs, j, all_subs[j + 1:], all_subs[j - 1:] if j else all_subs)
            buffers = buffers[::-1]
        assert buffers[0] is z_even

        @pl.loop(0, base // 2)
        def _(pair):
            block = base - 1 - 2 * pair
            step(block, z_even, z_odd, all_subs, None, all_subs, all_subs)
            step(block - 1, z_odd, z_even, all_subs, None, all_subs, all_subs)

        add_weighted_values(0, all_subs)
        o_ref[...] = acc_ref[...].astype(o_ref.dtype)
        if phases is not None:
            pl.when(jnp.logical_and(p == NPAIR - 1, i == n_steps - 1))(phases[2])

    kwargs, operands = _with_comm(
        comm, [Q_ROWS_SPEC, K_ALL_SPEC, V_ALL_SPEC], [PAIR_ROWS_SPEC, PAIR_TOTAL_SPEC],
        [jax.ShapeDtypeStruct((S, NH * HD), BF16), jax.ShapeDtypeStruct((NH, S, TK), F32)], [qkv, qkv, qkv],
        [pltpu.VMEM((TQ, LANES), F32), pltpu.VMEM((2 * NSUB, RS, TK), F32), pltpu.VMEM((2 * NSUB, RS, TK), F32),
         pltpu.VMEM((2 * NSUB, RS, TK), BF16)])
    return pl.pallas_call(
        body, name=name, grid=(NPAIR, n_steps),
        compiler_params=_cparams(("arbitrary", "arbitrary")), **kwargs,
    )(*operands)


def attn_bwd(qkv, dout, totals, name, comm=None):
    n_steps = S // TQ

    def body(*refs):
        ((q_ref, k_ref, v_ref, do_ref, r_ref), (dq_out, dk_out, dv_out),
         (z_even, z_odd, dw_even, dw_odd, before_ref, dbefore_ref, dz_ref, w_ref, dq_ref, dk_ref, dv_ref),
         phases) = _comm_hooks(comm, refs, 5, 3, 11)
        p = pl.program_id(0)
        i = pl.program_id(1)
        if phases is not None:
            pl.when(jnp.logical_and(p == 0, i == 0))(phases[0])
            pl.when(jnp.logical_and(p == NPAIR - 1, i == n_steps - 2))(phases[1])

        @pl.when(i == 0)
        def _():
            dk_ref[...] = jnp.zeros_like(dk_ref)
            dv_ref[...] = jnp.zeros_like(dv_ref)

        chains = [(sub, h) for sub in range(NSUB) for h in range(2)]
        nch = len(chains)
        qb = q_ref[...]
        dob = do_ref[...].astype(BF16)
        q_sub = [_head_halves(qb[sub * RS:(sub + 1) * RS] * SCALE) for sub in range(NSUB)]
        do_sub = [_head_halves(dob[sub * RS:(sub + 1) * RS]) for sub in range(NSUB)]
        upto = _tri_and_ones("upto")
        before_tri = _tri_and_ones("before")
        below_diagonal = (lax.broadcasted_iota(jnp.int32, (RS, TK), 1)
                          < lax.broadcasted_iota(jnp.int32, (RS, TK), 0))
        contract_lanes = (((1,), (1,)), ((), ()))
        contract_rows = (((0,), (0,)), ((), ()))
        base = i * NSUB
        all_subs = list(range(NSUB))

        def key_rows(block):
            return pl.ds(pl.multiple_of(block * TK, TK), TK)

        def store_products(bufs, block, subs):
            z_ref, dw_ref = bufs
            kb = k_ref[key_rows(block), :]
            vb = v_ref[key_rows(block), :]
            for c, (sub, h) in enumerate(chains):
                if sub in subs:
                    z_ref[c] = lax.dot_general(q_sub[sub][h], kb, contract_lanes, preferred_element_type=F32)
                    dw_ref[c] = lax.dot_general(do_sub[sub][h], vb, contract_lanes, preferred_element_type=F32)

        def add_gradients(block, subs):
            kb = k_ref[key_rows(block), :]
            for sub in subs:
                rows = pl.ds(sub * RS, RS)
                dq_ref[rows, :] += _join_heads(*[jnp.dot(dz_ref[h, rows, :], kb, preferred_element_type=F32)
                                                 for h in range(2)])
            dk_ref[key_rows(block), :] += _join_heads(*[
                lax.dot_general(dz_ref[h], qb, contract_rows, preferred_element_type=F32) for h in range(2)])
            dv_ref[key_rows(block), :] += _join_heads(*[
                lax.dot_general(w_ref[h], dob, contract_rows, preferred_element_type=F32) for h in range(2)])

        for ref in (dq_ref, before_ref, dbefore_ref, dz_ref, w_ref):
            ref[...] = jnp.zeros_like(ref)
        even, odd = (z_even, dw_even), (z_odd, dw_odd)
        store_products(even, 0, all_subs)

        def step(block, bufs, next_bufs, subs, diagonal_sub, prev_subs, next_subs):
            z_ref, dw_ref = bufs
            add_gradients(jnp.maximum(block - 1, 0), prev_subs)
            for sub in prev_subs:
                if sub not in subs:
                    dz_ref[:, pl.ds(sub * RS, RS), :] = jnp.zeros((2, RS, TK), BF16)
                    w_ref[:, pl.ds(sub * RS, RS), :] = jnp.zeros((2, RS, TK), BF16)
            if next_subs:
                store_products(next_bufs, block + 1, next_subs)
            active = [(c, sub, h) for c, (sub, h) in enumerate(chains) if sub in subs]
            ls, sums, dl, dsums = {}, {}, {}, {}
            for c, sub, h in active:
                ls[c] = _log_stay(z_ref[c])
                sums[c] = _dot_hilo(jnp.where(below_diagonal, ls[c], 0.0) if sub == diagonal_sub else ls[c], upto)
            for c, sub, h in active:
                rows = pl.ds(sub * RS, RS)
                before = before_ref[c]
                log_after = r_ref[h, rows, :] - (sums[c][:, :TK] + before)
                w = jnp.exp((z_ref[c] + ls[c]) + log_after)
                if sub == diagonal_sub:
                    w = jnp.where(below_diagonal, w, 0.0)
                dl[c] = dw_ref[c] * w
                dsums[c] = _dot_hilo(dl[c], before_tri)
                w_ref[h, rows, :] = w.astype(BF16)
                before_ref[c] = before + sums[c][:, TK:]
            for c, sub, h in active:
                rows = pl.ds(sub * RS, RS)
                dbefore = dbefore_ref[c]
                beta = jnp.exp(z_ref[c] + ls[c])
                if sub == diagonal_sub:
                    beta = jnp.where(below_diagonal, beta, 0.0)
                dstay = dsums[c][:, :TK] + dbefore
                dz_ref[h, rows, :] = ((dl[c] - beta * (dl[c] + dstay)) * SCALE).astype(BF16)
                dbefore_ref[c] = dbefore + dsums[c][:, TK:]

        @pl.loop(0, base // 2)
        def _(pair):
            step(2 * pair, even, odd, all_subs, None, all_subs, all_subs)
            step(2 * pair + 1, odd, even, all_subs, None, all_subs, all_subs)

        bufs = (even, odd)
        for j in range(NSUB):
            step(base + j, bufs[0], bufs[1], all_subs[j:], j, all_subs[j - 1:] if j else all_subs, all_subs[j + 1:])
            bufs = bufs[::-1]

        add_gradients(base + NSUB - 1, all_subs[NSUB - 1:])
        dq_out[...] = dq_ref[...].astype(dq_out.dtype)

        @pl.when(i == n_steps - 1)
        def _():
            dk_out[...] = dk_ref[...].astype(dk_out.dtype)
            dv_out[...] = dv_ref[...].astype(dv_out.dtype)

        if phases is not None:
            pl.when(jnp.logical_and(p == NPAIR - 1, i == n_steps - 1))(phases[2])

    full = jax.ShapeDtypeStruct((S, NH * HD), BF16)
    kwargs, operands = _with_comm(
        comm, [Q_ROWS_SPEC, K_ALL_SPEC, V_ALL_SPEC, PAIR_ROWS_SPEC, PAIR_TOTAL_SPEC],
        [PAIR_ROWS_SPEC, PAIR_ALL_SPEC, PAIR_ALL_SPEC], [full, full, full], [qkv, qkv, qkv, dout, totals],
        [pltpu.VMEM((2 * NSUB, RS, TK), F32)] * 6 + [pltpu.VMEM((2, TQ, TK), BF16)] * 2
        + [pltpu.VMEM((TQ, LANES), F32), pltpu.VMEM((S, LANES), F32), pltpu.VMEM((S, LANES), F32)])
    return pl.pallas_call(
        body, name=name, grid=(NPAIR, n_steps),
        compiler_params=_cparams(("arbitrary", "arbitrary")), **kwargs,
    )(*operands)


def _proj_cols(first_col):
    base = first_col // LANES
    return pl.BlockSpec((S, LANES), lambda j: (0, base + j))


CONV_OUT_SPEC = pl.BlockSpec((S, LANES), lambda j: (0, j))
CONV_DOUT_SPEC = pl.BlockSpec((S, LANES), lambda j: (0, (NH * HD) // LANES + j))
CONV_W_SPEC = pl.BlockSpec((8, LANES), lambda j: (0, j))
CONV_B_SPEC = pl.BlockSpec((1, LANES), lambda j: (0, j))


def _shift_down(u, n):
    rows = lax.broadcasted_iota(jnp.int32, u.shape, 0)
    return jnp.where(rows >= n, pltpu.roll(u, n, 0), 0.0)


def _shift_up(u, n):
    rows = lax.broadcasted_iota(jnp.int32, u.shape, 0)
    return jnp.where(rows < S - n, pltpu.roll(u, S - n, 0), 0.0)


def conv_fwd(proj, cw8, cb, name):
    def body(bg_ref, cg_ref, hc_ref, w_ref, b_ref, o_ref):
        u = cg_ref[...] * hc_ref[...]
        w = w_ref[...]
        y = w[0:1, :] * _shift_down(u, 2) + w[1:2, :] * _shift_down(u, 1) + w[2:3, :] * u + b_ref[...]
        o_ref[...] = bg_ref[...] * y

    return pl.pallas_call(
        body, name=name, grid=(CW // LANES,),
        in_specs=[_proj_cols(0), _proj_cols(CW), _proj_cols(2 * CW), CONV_W_SPEC, CONV_B_SPEC],
        out_specs=CONV_OUT_SPEC, out_shape=jax.ShapeDtypeStruct((S, CW), F32),
        compiler_params=_cparams(("parallel",)),
    )(proj, proj, proj, cw8, cb)


def conv_bwd(proj, dout, cw8, cb, name):
    def body(bg_ref, cg_ref, hc_ref, do_ref, w_ref, b_ref, dbg_ref, dcg_ref, dhc_ref, dw_ref, db_ref):
        cg, hc, do = cg_ref[...], hc_ref[...], do_ref[...]
        w = w_ref[...]
        u = cg * hc
        u1, u2 = _shift_down(u, 1), _shift_down(u, 2)
        y = w[0:1, :] * u2 + w[1:2, :] * u1 + w[2:3, :] * u + b_ref[...]
        dbg_ref[...] = (do * y).astype(dbg_ref.dtype)
        dy = do * bg_ref[...]
        db_ref[...] = jnp.sum(dy, axis=0, keepdims=True)
        dw_ref[...] = jnp.concatenate(
            [jnp.sum(dy * u2, axis=0, keepdims=True), jnp.sum(dy * u1, axis=0, keepdims=True),
             jnp.sum(dy * u, axis=0, keepdims=True), jnp.zeros((5, LANES), F32)], axis=0)
        du = w[2:3, :] * dy + w[1:2, :] * _shift_up(dy, 1) + w[0:1, :] * _shift_up(dy, 2)
        dcg_ref[...] = (du * hc).astype(dcg_ref.dtype)
        dhc_ref[...] = (du * cg).astype(dhc_ref.dtype)

    full = jax.ShapeDtypeStruct((S, CW), BF16)
    return pl.pallas_call(
        body, name=name, grid=(CW // LANES,),
        in_specs=[_proj_cols(0), _proj_cols(CW), _proj_cols(2 * CW), CONV_DOUT_SPEC, CONV_W_SPEC, CONV_B_SPEC],
        out_specs=[CONV_OUT_SPEC, CONV_OUT_SPEC, CONV_OUT_SPEC, CONV_W_SPEC, CONV_B_SPEC],
        out_shape=[full, full, full, jax.ShapeDtypeStruct((8, CW), F32), jax.ShapeDtypeStruct((1, CW), F32)],
        compiler_params=_cparams(("parallel",)),
    )(proj, proj, proj, dout, cw8, cb)


GELU_K = math.sqrt(2.0 / math.pi)
GELU_C = 0.044715


def _gelu(x):
    return 0.5 * x * (1.0 + jnp.tanh(GELU_K * (x + GELU_C * (x * x * x))))


def _gelu_grad(x):
    t = jnp.tanh(GELU_K * (x + GELU_C * (x * x * x)))
    return 0.5 * (1.0 + t) + 0.5 * x * (1.0 - t * t) * (GELU_K * (1.0 + 3.0 * GELU_C * (x * x)))


def _sg_masks():
    row = lax.broadcasted_iota(jnp.int32, (T, T), 0)
    col = lax.broadcasted_iota(jnp.int32, (T, T), 1)
    causal = jnp.right_shift(row, 6) >= jnp.right_shift(col, 6)
    head_of_col = jnp.right_shift(lax.broadcasted_iota(jnp.int32, (T, CW), 1), 6)
    return causal, head_of_col


def _sg_weights(sw_ref, causal):
    return [jnp.where(causal, sw_ref[h], 0.0).astype(BF16) for h in range(SG_HEADS)]


def _sg_mixed(vnb, weights, bias, head_of_col):
    mixed = bias
    for h in range(SG_HEADS):
        mh = jnp.dot(weights[h], vnb, preferred_element_type=F32)
        mixed = mixed + jnp.where(head_of_col == h, mh, 0.0)
    return mixed


SG_WINDOWS = 4
SG_ROWS = SG_WINDOWS * T
SG_U_SPEC = pl.BlockSpec((SG_ROWS, CW), lambda n: (n, 3))
SG_V_SPEC = pl.BlockSpec((SG_ROWS, CW), lambda n: (n, 4))
SG_ROW_SPEC = pl.BlockSpec((SG_ROWS, CW), lambda n: (n, 0))
SG_DOUT_SPEC = pl.BlockSpec((SG_ROWS, CW), lambda n: (n, 3))
SG_G_SPEC = pl.BlockSpec((1, CW), lambda n: (0, 0))
SG_W_SPEC = pl.BlockSpec((SG_HEADS, T, T), lambda n: (0, 0, 0))
SG_BIAS_SPEC = pl.BlockSpec((T, CW), lambda n: (0, 0))


def sg_fwd(proj, gn, sw, bias, name):
    def body(u_ref, v_ref, g_ref, sw_ref, bias_ref, o_ref):
        causal, head_of_col = _sg_masks()
        weights = _sg_weights(sw_ref, causal)
        for wdw in range(SG_WINDOWS):
            rows = pl.ds(wdw * T, T)
            gv = _gelu(v_ref[rows, :])
            rstd = lax.rsqrt(jnp.mean(gv * gv, axis=-1, keepdims=True) + EPS)
            vnb = ((gv * rstd) * g_ref[...]).astype(BF16)
            mixed = _sg_mixed(vnb, weights, bias_ref[...], head_of_col)
            o_ref[rows, :] = _gelu(u_ref[rows, :]) * mixed

    return pl.pallas_call(
        body, name=name, grid=(S // SG_ROWS,),
        in_specs=[SG_U_SPEC, SG_V_SPEC, SG_G_SPEC, SG_W_SPEC, SG_BIAS_SPEC],
        out_specs=SG_ROW_SPEC, out_shape=jax.ShapeDtypeStruct((S, CW), F32),
        compiler_params=_cparams(("parallel",)),
    )(proj, proj, gn, sw, bias)


def sg_bwd(proj, dout, gn, sw, bias, name):
    def body(u_ref, v_ref, do_ref, g_ref, sw_ref, bias_ref, du_ref, dv_ref, dg_ref, dsw_ref, dbias_ref):
        @pl.when(pl.program_id(0) == 0)
        def _():
            dg_ref[...] = jnp.zeros_like(dg_ref)
            dsw_ref[...] = jnp.zeros_like(dsw_ref)
            dbias_ref[...] = jnp.zeros_like(dbias_ref)

        causal, head_of_col = _sg_masks()
        weights = _sg_weights(sw_ref, causal)
        gnv = g_ref[...]
        for wdw in range(SG_WINDOWS):
            rows = pl.ds(wdw * T, T)
            uv, vv, do = u_ref[rows, :], v_ref[rows, :], do_ref[rows, :]
            gv = _gelu(vv)
            rstd = lax.rsqrt(jnp.mean(gv * gv, axis=-1, keepdims=True) + EPS)
            xhat = gv * rstd
            vnb = (xhat * gnv).astype(BF16)
            mixed = _sg_mixed(vnb, weights, bias_ref[...], head_of_col)
            du_ref[rows, :] = ((do * mixed) * _gelu_grad(uv)).astype(du_ref.dtype)
            dmix = do * _gelu(uv)
            dbias_ref[...] += dmix
            dmixb = dmix.astype(BF16)
            dvn = jnp.zeros((T, CW), F32)
            for h in range(SG_HEADS):
                dvh = lax.dot_general(weights[h], dmixb, (((0,), (0,)), ((), ())), preferred_element_type=F32)
                dvn = dvn + jnp.where(head_of_col == h, dvh, 0.0)
                dmh = jnp.where(head_of_col == h, dmixb, jnp.zeros_like(dmixb))
                dwh = lax.dot_general(dmh, vnb, (((1,), (1,)), ((), ())), preferred_element_type=F32)
                dsw_ref[h] += jnp.where(causal, dwh, 0.0)
            dg_ref[...] += jnp.sum(dvn * xhat, axis=0, keepdims=True)
            dxhat = dvn * gnv
            dgv = rstd * (dxhat - xhat * jnp.mean(dxhat * xhat, axis=-1, keepdims=True))
            dv_ref[rows, :] = (dgv * _gelu_grad(vv)).astype(dv_ref.dtype)

    full = jax.ShapeDtypeStruct((S, CW), BF16)
    return pl.pallas_call(
        body, name=name, grid=(S // SG_ROWS,),
        in_specs=[SG_U_SPEC, SG_V_SPEC, SG_DOUT_SPEC, SG_G_SPEC, SG_W_SPEC, SG_BIAS_SPEC],
        out_specs=[SG_ROW_SPEC, SG_ROW_SPEC, SG_G_SPEC, SG_W_SPEC, SG_BIAS_SPEC],
        out_shape=[full, full, jax.ShapeDtypeStruct((1, CW), F32),
                   jax.ShapeDtypeStruct((SG_HEADS, T, T), F32), jax.ShapeDtypeStruct((T, CW), F32)],
        compiler_params=_cparams(("arbitrary",)),
    )(proj, proj, dout, gn, sw, bias)


ADA_COLS = NMOD * D // NDEV


def ada_fwd(c_all, ada_w, ada_b_mine, name):
    def body(c_ref, w_ref, b_ref, o_ref, ca_ref):
        cv = c_ref[...]
        ca = cv * (1.0 / (1.0 + jnp.exp(-cv)))
        ca_ref[...] = ca
        cab = ca.astype(BF16)
        for l in range(L):
            o_ref[l] = jnp.dot(cab, w_ref[l].astype(BF16), preferred_element_type=F32) + b_ref[l]

    return pl.pallas_call(
        body, name=name,
        out_shape=[jax.ShapeDtypeStruct((L, NDEV, ADA_COLS), F32), jax.ShapeDtypeStruct((NDEV, D), F32)],
        compiler_params=_cparams(),
    )(c_all, ada_w, ada_b_mine)


def ada_bwd(ca, dmod_cols, name):
    def body(ca_ref, dm_ref, o_ref):
        cab = ca_ref[...].astype(BF16)
        for l in range(L):
            o_ref[l] = lax.dot_general(cab, dm_ref[l].astype(BF16), (((0,), (0,)), ((), ())),
                                       preferred_element_type=F32)

    return pl.pallas_call(
        body, name=name, out_shape=jax.ShapeDtypeStruct((L, D, ADA_COLS), F32),
        compiler_params=_cparams(),
    )(ca, dmod_cols)


def _adamw(w, g, m, v):
    m = B1 * m + (1.0 - B1) * g
    v = B2 * v + (1.0 - B2) * (g * g)
    m_hat = m / BC1
    v_hat = v / BC2
    delta = -LR * (m_hat / (jnp.sqrt(v_hat) + AEPS) + WD * w)
    return delta, m, v


VEC_ROWS_PER_LAYER = 8
VEC_FINAL_ROW = L * VEC_ROWS_PER_LAYER
VEC_ROWS = VEC_FINAL_ROW + 8
W256_TAPS, W256_CONV_B, W256_GN = 0, 8, 9
W256_ROWS_PER_LAYER = 16


def small_update(vec_all, w256_all, sb_all, sw_all, params, name):
    n_par = len(params)

    def body(*refs):
        vec_ref, w256_ref, sb_ref = refs[:3]
        sw_refs = refs[3:3 + L]
        par_refs = [refs[3 + L + 3 * k:3 + L + 3 * k + 3] for k in range(n_par)]
        out = refs[3 + L + 3 * n_par:]
        out_par = [out[4 * k:4 * k + 4] for k in range(n_par)]
        loss_ref, taps_ref = out[4 * n_par:]

        def total(ref, idx):
            acc = ref[(0,) + idx].astype(F32)
            for d in range(1, NDEV):
                acc = acc + ref[(d,) + idx].astype(F32)
            return acc

        def update(k, region, g):
            w_ref, m_ref, v_ref = par_refs[k]
            g_ref, d_ref, nm_ref, nv_ref = out_par[k]
            delta, nm, nv = _adamw(w_ref[region], g, m_ref[region], v_ref[region])
            g_ref[region] = g
            d_ref[region] = delta
            nm_ref[region] = nm
            nv_ref[region] = nv

        for l in range(L):
            base = l * VEC_ROWS_PER_LAYER
            for k in range(NMOD):
                update(0, (slice(l, l + 1), slice(k * D, (k + 1) * D)), total(vec_ref, (slice(base + k, base + k + 1),)))
            update(1, (slice(l, l + 1),), total(vec_ref, (slice(base + 6, base + 7),)))
            update(2, (slice(l, l + 1),), total(vec_ref, (slice(base + 7, base + 8),)))
            wbase = l * W256_ROWS_PER_LAYER
            update(4, (slice(l, l + 1),), total(w256_ref, (slice(wbase + W256_CONV_B, wbase + W256_CONV_B + 1),)))
            update(5, (slice(l, l + 1),), total(w256_ref, (slice(wbase + W256_GN, wbase + W256_GN + 1),)))
            update(6, (l,), total(sw_refs[l], ()))
            update(7, (l,), total(sb_ref, (slice(l * SG_HEADS, (l + 1) * SG_HEADS),)))
            taps_ref[l] = total(w256_ref, (slice(wbase + W256_TAPS, wbase + W256_TAPS + 8),))
        update(3, (slice(0, 1),), total(vec_ref, (slice(VEC_FINAL_ROW, VEC_FINAL_ROW + 1),)))
        loss_ref[...] = total(vec_ref, (slice(VEC_FINAL_ROW + 1, VEC_FINAL_ROW + 2), slice(0, LANES)))

    out_shape = []
    for w, _, _ in params:
        out_shape += [jax.ShapeDtypeStruct(w.shape, F32)] * 4
    out_shape += [jax.ShapeDtypeStruct((1, LANES), F32), jax.ShapeDtypeStruct((L, 8, CW), F32)]
    outs = pl.pallas_call(body, name=name, out_shape=out_shape, compiler_params=_cparams())(
        vec_all, w256_all, sb_all, *sw_all, *[a for p in params for a in p])
    return [outs[4 * k:4 * k + 4] for k in range(n_par)], outs[4 * n_par:]


def adamw_plain(w, g, m, v, tr, name):
    rows, cols = w.shape
    spec = pl.BlockSpec((tr, cols), lambda i: (i, 0))

    def body(w_ref, g_ref, m_ref, v_ref, d_ref, nm_ref, nv_ref):
        delta, nm, nv = _adamw(w_ref[...], g_ref[...], m_ref[...], v_ref[...])
        d_ref[...] = delta
        nm_ref[...] = nm
        nv_ref[...] = nv

    shp = jax.ShapeDtypeStruct((rows, cols), F32)
    return pl.pallas_call(
        body, name=name, grid=(rows // tr,), in_specs=[spec] * 4, out_specs=[spec] * 3,
        out_shape=[shp, shp, shp], compiler_params=_cparams(("parallel",)),
    )(w, g, m, v)


def adamw_reduce(w, parts, m, v, tr, name, tie=None):
    _, rows, cols = w.shape
    spec = pl.BlockSpec((None, tr, cols), lambda l, i: (l, i, 0))
    pspecs = [pl.BlockSpec((NDEV, tr, cols), lambda l, i, k=k: (0, jnp.where(l == k, i, 0), 0)) for k in range(L)]

    ties = [] if tie is None else [tie]

    def body(w_ref, p0_ref, p1_ref, m_ref, v_ref, *rest):
        g_ref, d_ref, nm_ref, nv_ref = rest[len(ties):]
        first_layer = pl.program_id(0) == 0
        g = jnp.zeros((tr, cols), F32)
        for d in range(NDEV):
            g = g + jnp.where(first_layer, p0_ref[d], p1_ref[d]).astype(F32)
        delta, nm, nv = _adamw(w_ref[...], g, m_ref[...], v_ref[...])
        g_ref[...] = g
        d_ref[...] = delta
        nm_ref[...] = nm
        nv_ref[...] = nv

    shp = jax.ShapeDtypeStruct(w.shape, F32)
    return pl.pallas_call(
        body, name=name, grid=(L, rows // tr),
        in_specs=[spec] + pspecs + [spec, spec] + [pl.BlockSpec(t.shape, lambda l, i: (0, 0)) for t in ties],
        out_specs=[spec] * 4, out_shape=[shp] * 4, compiler_params=_cparams(("parallel", "parallel")),
    )(w, *parts, m, v, *ties)


SHARD_IN = PROJ // NDEV


def shards_to_columns(shards, name):
    tr = 256

    def body(i_ref, o_ref):
        for d in range(NDEV):
            o_ref[:, d * SHARD_IN:(d + 1) * SHARD_IN] = i_ref[d]

    return pl.pallas_call(
        body, name=name, grid=(D // tr,),
        in_specs=[pl.BlockSpec((NDEV, tr, SHARD_IN), lambda i: (0, i, 0))],
        out_specs=pl.BlockSpec((tr, PROJ), lambda i: (i, 0)),
        out_shape=jax.ShapeDtypeStruct((D, PROJ), shards.dtype), compiler_params=_cparams(("parallel",)),
    )(shards)


def columns_to_shards(mat, name):
    tr = 256

    def body(i_ref, o_ref):
        for d in range(NDEV):
            o_ref[d] = i_ref[:, d * SHARD_IN:(d + 1) * SHARD_IN]

    return pl.pallas_call(
        body, name=name, grid=(D // tr,),
        in_specs=[pl.BlockSpec((tr, PROJ), lambda i: (i, 0))],
        out_specs=pl.BlockSpec((NDEV, tr, SHARD_IN), lambda i: (0, i, 0)),
        out_shape=jax.ShapeDtypeStruct((NDEV, D, SHARD_IN), mat.dtype), compiler_params=_cparams(("parallel",)),
    )(mat)


def _pad_rows(flat, rows):
    return jnp.pad(flat, (0, rows * LANES - flat.shape[0])).reshape(rows, LANES)


def kernel(x, c, ada_w, ada_b, norm_mix_g, norm_mlp_g, w_in, conv_w, conv_b, gmlp_norm_g, spatial_w, spatial_b, w_out, mlp_w1, mlp_w2, final_norm_g, loss_target, m_ada_w, m_ada_b, m_norm_mix_g, m_norm_mlp_g, m_w_in, m_conv_w, m_conv_b, m_gmlp_norm_g, m_spatial_w, m_spatial_b, m_w_out, m_mlp_w1, m_mlp_w2, m_final_norm_g, v_ada_w, v_ada_b, v_norm_mix_g, v_norm_mlp_g, v_w_in, v_conv_w, v_conv_b, v_gmlp_norm_g, v_spatial_w, v_spatial_b, v_w_out, v_mlp_w1, v_mlp_w2, v_final_norm_g):
    me = _lin(_my_pos())
    x0 = x[0]
    target = loss_target[0]
    conv_shard = conv_w.shape[-1]

    w_in_b, w_out_b, w1_b, w2_b = [w.astype(BF16) for w in (w_in, w_out, mlp_w1, mlp_w2)]
    pack0 = _pad_rows(jnp.concatenate([c.reshape(-1), conv_w.reshape(-1)]), 16)
    g0, gw_in0 = run_comm(Gather([pack0, w_in_b[0]]), "gather_first")
    g0 = g0.reshape(NDEV, 16 * LANES)
    c_all = g0[:, :D]
    conv_full = (g0[:, D:D + L * 3 * conv_shard].reshape(NDEV, L, 3, conv_shard)
                 .transpose(1, 2, 0, 3).reshape(L, 3, CW))


    W_in = [shards_to_columns(gw_in0, "w_in_columns0"), None]
    W_out, W1, W2 = [None] * L, [None] * L, [None] * L

    ada_b_mine = lax.dynamic_slice(ada_b, (0, me * ADA_COLS), (L, ADA_COLS)).reshape(L, 1, ADA_COLS)
    mod_part, c_act = ada_fwd(c_all, ada_w, ada_b_mine, "ada_fwd")
    gmod = run_comm(Gather([mod_part]), "gather_mod")[0]
    mod = lax.dynamic_index_in_dim(gmod, me, axis=2, keepdims=False)
    mod = mod.transpose(1, 0, 2).reshape(L, NMOD, 1, D)
    early_weights, token = start_copies([w_out_b[0]], me, "gather_early0_start", True, after=gmod)
    mod = tied(mod, token)

    cw8 = jnp.pad(conv_full, ((0, 0), (0, 5), (0, 0)))
    sg_bias = jnp.repeat(spatial_b.transpose(0, 2, 1), HD, axis=2)

    saved = []
    xl = x0
    for l in range(L):
        sh_m, sc_m, g_m, sh_f, sc_f, g_f = [mod[l, k] for k in range(NMOD)]
        h1 = normmod_fwd(xl, norm_mix_g[l:l + 1], sc_m, sh_m, f"norm_mix_fwd{l}")
        if l > 0:
            W_in[l] = shards_to_columns(finish_copies(w_in_handle, xl, f"gather_w_in{l}_wait")[0],
                                        f"w_in_columns{l}")
        qkv = mm_layer("proj_qkv", l, h1, W_in[l], out_dtypes=[BF16], cols=(0, QKV))[0]
        proj = mm_layer("proj_rest", l, h1, W_in[l], out_dtypes=[F32], cols=(QKV, REST))[0]
        a_out, a_tot, gw2, gw1 = attn_fwd(qkv, f"attn_fwd{l}", comm=Gather([w2_b[l], w1_b[l]]))
        gw_out, = finish_copies(early_weights, a_out, f"gather_early{l}_wait")
        W_out[l] = gw_out.reshape(D, D)
        W1[l] = gw1
        W2[l] = gw2.reshape(DFF, D)
        if l + 1 < L:
            w_in_handle, token = start_copies([w_in_b[l + 1]], me, f"gather_w_in{l + 1}_start", True, after=a_out)
            early_weights, token = start_copies([w_out_b[l + 1]], me, f"gather_early{l + 1}_start", True, after=token)
            g_m = tied(g_m, token)
        c_out = conv_fwd(proj, cw8[l], conv_b[l:l + 1], f"conv_fwd{l}")
        s_out = sg_fwd(proj, gmlp_norm_g[l:l + 1], spatial_w[l], sg_bias[l], f"sg_fwd{l}")
        cat = jnp.concatenate([a_out, c_out.astype(BF16), s_out.astype(BF16)], axis=1)
        mix, x1, h2 = mm_layer("mix", l, cat, W_out[l], out_dtypes=[F32, F32, BF16], epilogue=_residual_then_norm,
                               extras=[(xl, "tile"), (g_m, "col"), (norm_mlp_g[l:l + 1], "col"), (sc_f, "col"),
                                       (sh_f, "col")])
        ra, r = mm_layer("mlp_up", l, h2, W1[l], out_dtypes=[BF16, BF16], b_blocks=True,
                         epilogue=lambda acc: (jnp.maximum(acc, 0.0), jnp.square(jnp.maximum(acc, 0.0))))
        m2, x2 = mm_layer("mlp_down", l, r, W2[l], out_dtypes=[F32, F32],
                          epilogue=lambda acc, xr, g: (acc, xr + g * acc), extras=[(x1, "tile"), (g_f, "col")])
        saved.append(dict(x=xl, h1=h1, proj=proj, qkv=qkv, a_tot=a_tot, cat=cat, mix=mix,
                          x1=x1, h2=h2, ra=ra, r=r, m2=m2))
        xl = x2

    dx, loss_part, d_final_g, dm2, dg_f = loss_head(xl, target, final_norm_g.reshape(1, D),
                                                    (saved[L - 1]["m2"], mod[L - 1, NMOD - 1]), "loss_head")

    p_in, p_out, p_w1, p_w2 = [None] * L, [None] * L, [None] * L, [None] * L
    w_in_grads = [None] * L
    vec_rows, d_norm_mix, d_norm_mlp = [None] * L, [None] * L, [None] * L
    dcw8, d_conv_b, d_gn, d_sw, d_sb = [None] * L, [None] * L, [None] * L, [None] * L, [None] * L
    late_grads = [None] * L
    for l in reversed(range(L)):
        sv = saved[l]
        sh_m, sc_m, g_m, sh_f, sc_f, g_f = [mod[l, k] for k in range(NMOD)]
        da = mm_layer("mlp_down_dgrad", l, dm2, W2[l], out_dtypes=[BF16], trans_b=True,
                      epilogue=lambda acc, rav: (acc * (2.0 * rav.astype(F32)),), extras=[(sv["ra"], "tile")])[0]
        dW2 = mm_layer("mlp_down_wgrad", l, sv["r"], dm2, out_dtypes=[BF16], trans_a=True)[0]
        dW1 = mm_layer("mlp_up_wgrad", l, sv["h2"], da, out_dtypes=[BF16], trans_a=True, out_blocks=True)[0]
        dh2 = mm_layer("mlp_up_dgrad", l, da, W1[l], out_dtypes=[F32], trans_b=True, b_blocks=True)[0]
        dx1, dsc_f, dsh_f, d_norm_mlp[l], dmix, dg_m = normmod_bwd(
            sv["x1"], dh2, dx, norm_mlp_g[l:l + 1], sc_f, f"norm_mlp_bwd{l}", gate_next=(sv["mix"], g_m))
        dcat = mm_layer("mix_dgrad", l, dmix, W_out[l], out_dtypes=[F32], trans_b=True)[0]
        dW_out = mm_layer("mix_wgrad", l, sv["cat"], dmix, out_dtypes=[BF16], trans_a=True)[0]
        pieces_w2, pieces_out = dW2.reshape(NDEV, DFF // NDEV, D), dW_out.reshape(NDEV, D // NDEV, D)
        ride, late = ([pieces_w2, pieces_out], dW1) if l == L - 1 else ([pieces_w2, dW1], pieces_out)
        dq, dk, dv, *arrived = attn_bwd(sv["qkv"], dcat, sv["a_tot"], f"attn_bwd{l}", comm=Exchange(ride))
        p_w2[l] = arrived[0]
        (p_out if l == L - 1 else p_w1)[l] = arrived[1]
        late_grads[l], late_token = start_copies([late], me, f"exchange_late{l}_start", False, after=dq)
        dbg, dcg, dhc, dcw8[l], d_conv_b[l] = conv_bwd(sv["proj"], dcat, cw8[l], conv_b[l:l + 1], f"conv_bwd{l}")
        dus, dvs, d_gn[l], dsw, dbias = sg_bwd(sv["proj"], dcat, gmlp_norm_g[l:l + 1], spatial_w[l], sg_bias[l],
                                               f"sg_bwd{l}")
        d_sw[l] = dsw.astype(BF16)
        d_sb[l] = dbias.reshape(T, SG_HEADS, HD).sum(axis=2).T
        dproj = jnp.concatenate([dq, dk, dv, dbg, dcg, dhc, dus, dvs], axis=1).astype(BF16)
        dW_in = mm_layer("proj_wgrad", l, sv["h1"], dproj, out_dtypes=[BF16], trans_a=True,
                         extras=[(late_token, "tie")])[0]
        pieces = columns_to_shards(dW_in, f"w_in_grad_shards{l}")
        w_in_grads[l], token = start_copies([pieces], me, f"exchange_w_in{l}_start", False)
        dh1 = mm_layer("proj_dgrad", l, dproj, W_in[l], out_dtypes=[F32], trans_b=True, extras=[(token, "tie")])[0]
        below = (saved[l - 1]["m2"], mod[l - 1, NMOD - 1]) if l > 0 else None
        dx, dsc_m, dsh_m, d_norm_mix[l], *gated_below = normmod_bwd(
            sv["x"], dh1, dx1, tied(norm_mix_g[l:l + 1], token), sc_m, f"norm_mix_bwd{l}", gate_next=below)
        vec_rows[l] = [dsh_m, dsc_m, dg_m, dsh_f, dsc_f, dg_f, d_norm_mix[l], d_norm_mlp[l]]
        if l > 0:
            dm2, dg_f = gated_below

    grad_x = dx.reshape(1, S, D)

    g_w2, d_w2, nm_w2, nv_w2 = adamw_reduce(mlp_w2, p_w2, m_mlp_w2, v_mlp_w2, 256, "adamw_mlp_w2", tie=token)
    p_w1[L - 1] = finish_copies(late_grads[L - 1], d_w2, f"exchange_late{L - 1}_wait")[0]
    g_w1, d_w1, nm_w1, nv_w1 = adamw_reduce(mlp_w1, p_w1, m_mlp_w1, v_mlp_w1, 256, "adamw_mlp_w1", tie=token)

    vec_pack = jnp.concatenate([row for l in range(L) for row in vec_rows[l]]
                               + [d_final_g, loss_part, jnp.zeros((VEC_ROWS - VEC_FINAL_ROW - 2, D), F32)], axis=0)
    vec_pack, _ = lax.optimization_barrier((vec_pack, (d_w1, d_w2)))
    w256_pack = jnp.concatenate([blk for l in range(L) for blk in (
        dcw8[l], d_conv_b[l], d_gn[l], jnp.zeros((W256_ROWS_PER_LAYER - W256_GN - 1, CW), F32))], axis=0)
    vec_all, w256_all, sb_all, *sw_all = run_comm(
        Gather([vec_pack, w256_pack, jnp.concatenate(d_sb, axis=0)] + d_sw), "gather_small_grads")

    dmod_all = (vec_all[:, :VEC_FINAL_ROW].reshape(NDEV, L, VEC_ROWS_PER_LAYER, D)[:, :, :NMOD]
                .reshape(NDEV, L, NMOD * D))
    dmod_cols = lax.dynamic_slice(dmod_all, (0, 0, me * ADA_COLS), (NDEV, L, ADA_COLS)).transpose(1, 0, 2)
    g_ada_w = ada_bwd(c_act, dmod_cols, "ada_bwd")

    flat2 = lambda t: t.reshape(L * D, ADA_COLS)
    d_ada_w, nm_ada_w, nv_ada_w = [t.reshape(L, D, ADA_COLS) for t in adamw_plain(
        flat2(ada_w), flat2(g_ada_w), flat2(m_ada_w), flat2(v_ada_w), 256, "adamw_ada_w")]

    after = jnp.concatenate([t.reshape(-1)[:1] for t in (d_w1, d_w2, d_ada_w)])
    p_in = [finish_copies(w_in_grads[l], after, f"exchange_w_in{l}_wait")[0] for l in range(L)]
    p_out[0] = finish_copies(late_grads[0], after, "exchange_late0_wait")[0]
    g_w_in, d_w_in, nm_w_in, nv_w_in = adamw_reduce(w_in, p_in, m_w_in, v_w_in, 256, "adamw_w_in")
    g_w_out, d_w_out, nm_w_out, nv_w_out = adamw_reduce(w_out, p_out, m_w_out, v_w_out, 128, "adamw_w_out")

    as_row = lambda t: t.reshape(1, D)
    small_params = [(ada_b, m_ada_b, v_ada_b), (norm_mix_g, m_norm_mix_g, v_norm_mix_g),
                    (norm_mlp_g, m_norm_mlp_g, v_norm_mlp_g),
                    (as_row(final_norm_g), as_row(m_final_norm_g), as_row(v_final_norm_g)),
                    (conv_b, m_conv_b, v_conv_b), (gmlp_norm_g, m_gmlp_norm_g, v_gmlp_norm_g),
                    (spatial_w, m_spatial_w, v_spatial_w), (spatial_b, m_spatial_b, v_spatial_b)]
    updated, (loss_sum, taps_sum) = small_update(vec_all, w256_all, sb_all, sw_all, small_params, "small_update")
    loss = loss_sum[0, 0]
    u_ada_b, u_norm_mix, u_norm_mlp, u_final, u_conv_b, u_gn, u_sw, u_sb = updated
    u_final = [t.reshape(D) for t in u_final]
    g_conv_w = lax.dynamic_slice(taps_sum, (0, 0, me * conv_shard), (L, 3, conv_shard))
    flat_cw = lambda t: t.reshape(L * 3, conv_shard)
    u_conv_w = [g_conv_w] + [t.reshape(L, 3, conv_shard) for t in adamw_plain(
        flat_cw(conv_w), flat_cw(g_conv_w), flat_cw(m_conv_w), flat_cw(v_conv_w), L * 3, "adamw_conv_w")]
    small_sets = [u_ada_b, u_norm_mix, u_norm_mlp, u_conv_w, u_conv_b, u_gn, u_sw, u_sb, u_final]
    small_g, sd, snm, snv = [[u[k] for u in small_sets] for k in range(4)]

    def ordered(big, small):
        ada, win, wout, w1, w2 = big
        return [ada, small[0], small[1], small[2], win, small[3], small[4], small[5], small[6], small[7],
                wout, w1, w2, small[8]]

    grads = ordered([g_ada_w, g_w_in, g_w_out, g_w1, g_w2], small_g)
    deltas = ordered([d_ada_w, d_w_in, d_w_out, d_w1, d_w2], sd)
    new_m = ordered([nm_ada_w, nm_w_in, nm_w_out, nm_w1, nm_w2], snm)
    new_v = ordered([nv_ada_w, nv_w_in, nv_w_out, nv_w1, nv_w2], snv)
    return (loss, grad_x, *grads, *deltas, *new_m, *new_v)
```

```python
import functools
import math

import jax
import jax.numpy as jnp
from jax import lax
from jax.experimental import pallas as pl
from jax.experimental.pallas import tpu as pltpu

F32 = jnp.float32
BF16 = jnp.bfloat16
MESH = pl.DeviceIdType.MESH

S = 2048
D = 1024
L = 2
NDEV = 8
HD = 64
NH = 8
PROJ = 2816
DFF = 4096
NMOD = 6
EPS = 1e-6
T = 128
SG_HEADS = 4
LANES = 128
CW = 256
QKV = 3 * NH * HD
REST = PROJ - QKV

LR, B1, B2, AEPS, WD, STEP = 0.001, 0.9, 0.999, 1e-08, 0.01, 10
BC1 = 1.0 - B1 ** STEP
BC2 = 1.0 - B2 ** STEP

VMEM_LIMIT = 48 * 1024 * 1024

HBM_SPEC = pl.BlockSpec(memory_space=pltpu.HBM)


def _cparams(sem=None):
    return pltpu.CompilerParams(dimension_semantics=sem, vmem_limit_bytes=VMEM_LIMIT)


def _my_pos():
    return lax.axis_index("x"), lax.axis_index("y"), lax.axis_index("c")


def _lin(p):
    return 4 * p[0] + 2 * p[1] + p[2]


class Gather:
    def __init__(self, arrs):
        self.arrs = list(arrs)
        n = len(self.arrs)
        self.out_shape = [jax.ShapeDtypeStruct((NDEV,) + a.shape, a.dtype) for a in self.arrs]
        self.scratch = [pltpu.SemaphoreType.DMA((n, 7)), pltpu.SemaphoreType.DMA((n, 7)),
                        pltpu.SemaphoreType.DMA((n,))]

    def phases(self, ins, outs, sems):
        n = len(self.arrs)
        send_sems, recv_sems, local_sems = sems
        x, y, c = _my_pos()
        me, sibling = (x, y, c), (x, y, 1 - c)
        chips = [(1 - x, y), (x, 1 - y), (1 - x, 1 - y)]

        def copy(a, k, block, to, src=None):
            slot = outs[a].at[_lin(block)]
            return pltpu.make_async_remote_copy(
                src_ref=slot if src is None else src, dst_ref=slot,
                send_sem=send_sems.at[a, k], recv_sem=recv_sems.at[a, k],
                device_id=to, device_id_type=MESH)

        def mine(a):
            return pltpu.make_async_copy(ins[a], outs[a].at[_lin(me)], local_sems.at[a])

        def first(a):
            return [copy(a, 0, me, sibling, src=ins[a])] + [
                copy(a, 1 + j, me, (*chip, c), src=ins[a]) for j, chip in enumerate(chips)]

        def passed(a):
            return [copy(a, 4 + j, (*chip, c), sibling) for j, chip in enumerate(chips)]

        def start():
            for a in range(n):
                mine(a).start()
                for cp in first(a):
                    cp.start()

        def relay():
            for j, chip in enumerate(chips):
                for a in range(n):
                    copy(a, 1 + j, (*chip, c), me).wait_recv()
                    passed(a)[j].start()

        def finish():
            for a in range(n):
                copy(a, 0, sibling, me).wait_recv()
            for j, chip in enumerate(chips):
                for a in range(n):
                    copy(a, 4 + j, (*chip, 1 - c), me).wait_recv()
            for a in range(n):
                for cp in first(a) + passed(a):
                    cp.wait_send()
                mine(a).wait()

        return start, relay, finish


class Exchange:
    def __init__(self, arrs):
        self.arrs = list(arrs)
        n = len(self.arrs)
        self.out_shape = [jax.ShapeDtypeStruct(a.shape, a.dtype) for a in self.arrs]
        self.scratch = [pltpu.SemaphoreType.DMA((n, 7)), pltpu.SemaphoreType.DMA((n, 7)),
                        pltpu.SemaphoreType.DMA((n,))]

    def phases(self, ins, outs, sems):
        n = len(self.arrs)
        send_sems, recv_sems, local_sems = sems
        x, y, c = _my_pos()
        me = (x, y, c)

        def peer(mask):
            return (1 - x if mask & 4 else x, 1 - y if mask & 2 else y, 1 - c if mask & 1 else c)

        def copy(a, mask):
            return pltpu.make_async_remote_copy(
                src_ref=ins[a].at[_lin(peer(mask))], dst_ref=outs[a].at[_lin(me)],
                send_sem=send_sems.at[a, mask - 1], recv_sem=recv_sems.at[a, mask - 1],
                device_id=peer(mask), device_id_type=MESH)

        def arrival(a, mask):
            return pltpu.make_async_remote_copy(
                src_ref=ins[a].at[_lin(me)], dst_ref=outs[a].at[_lin(peer(mask))],
                send_sem=send_sems.at[a, mask - 1], recv_sem=recv_sems.at[a, mask - 1],
                device_id=peer(mask), device_id_type=MESH)

        def mine(a):
            return pltpu.make_async_copy(ins[a].at[_lin(me)], outs[a].at[_lin(me)], local_sems.at[a])

        def start():
            for a in range(n):
                mine(a).start()
            for mask in (4, 2, 6, 1, 5, 3, 7):
                for a in range(n):
                    copy(a, mask).start()

        def relay():
            pass

        def finish():
            for mask in range(1, 8):
                for a in range(n):
                    arrival(a, mask).wait_recv()
            for mask in range(1, 8):
                for a in range(n):
                    copy(a, mask).wait_send()
            for a in range(n):
                mine(a).wait()

        return start, relay, finish


def run_comm(plan, name):
    n = len(plan.arrs)

    def body(*refs):
        start, relay, finish = plan.phases(refs[:n], refs[n:2 * n], refs[2 * n:])
        start()
        relay()
        finish()

    outs = pl.pallas_call(
        body, name=name, out_shape=plan.out_shape,
        in_specs=[HBM_SPEC] * n, out_specs=[HBM_SPEC] * n, scratch_shapes=plan.scratch,
    )(*plan.arrs)
    return list(outs)


SEM_SPEC = pl.BlockSpec(memory_space=pltpu.SEMAPHORE)
DATAFLOW = pltpu.SideEffectType.DATAFLOW_SIDE_EFFECTING


def _peer_copies(src_ref, land_ref, send_sems, recv_sems, first, same_block):
    x, y, c = _my_pos()
    me = (x, y, c)
    sends, arrivals = [], []
    for mask in (4, 2, 6, 1, 5, 3, 7):
        peer = (1 - x if mask & 4 else x, 1 - y if mask & 2 else y, 1 - c if mask & 1 else c)
        sends.append(pltpu.make_async_remote_copy(
            src_ref=src_ref if same_block else src_ref.at[_lin(peer)], dst_ref=land_ref.at[_lin(me)],
            send_sem=send_sems.at[first + mask - 1], recv_sem=recv_sems.at[first + mask - 1], device_id=peer,
            device_id_type=MESH))
        arrivals.append(pltpu.make_async_remote_copy(
            src_ref=src_ref if same_block else src_ref.at[_lin(me)], dst_ref=land_ref.at[_lin(peer)],
            send_sem=send_sems.at[first + mask - 1], recv_sem=recv_sems.at[first + mask - 1], device_id=peer,
            device_id_type=MESH))
    return sends, arrivals


def start_copies(srcs, me, name, same_block, after=None):
    n = len(srcs)
    landings = []
    for src in srcs:
        own = src[None] if same_block else lax.dynamic_index_in_dim(src, me, axis=0, keepdims=True)
        landings.append(lax.dynamic_update_slice(lax.empty((NDEV,) + own.shape[1:], src.dtype), own,
                                                 (me,) + (0,) * (own.ndim - 1)))

    def body(*refs):
        send_sems, recv_sems = refs[-2 * n - 3], refs[-2 * n - 2]
        token = refs[-1]
        for k in range(n):
            sends, _ = _peer_copies(refs[2 * k], refs[2 * k + 1], send_sems, recv_sems, 7 * k, same_block)
            for cp in sends:
                cp.start()
        token[...] = jnp.zeros_like(token)

    hbm = lambda a: pltpu.HBM(a.shape, a.dtype)
    pairs = [a for pair in zip(srcs, landings) for a in pair]
    extra = [] if after is None else [after]
    sems = pltpu.SemaphoreType.DMA((7 * n,))
    send_sems, recv_sems, *thru, token = pl.pallas_call(
        body, name=name,
        out_shape=(sems, sems, *[hbm(a) for a in pairs], jax.ShapeDtypeStruct((8, LANES), F32)),
        in_specs=[HBM_SPEC] * (2 * n) + [pl.BlockSpec(memory_space=pl.ANY)] * len(extra),
        out_specs=(SEM_SPEC, SEM_SPEC, *[HBM_SPEC] * (2 * n), pl.BlockSpec(memory_space=pltpu.VMEM)),
        input_output_aliases={k: 2 + k for k in range(2 * n)},
        compiler_params=pltpu.CompilerParams(has_side_effects=DATAFLOW),
    )(*[pltpu.with_memory_space_constraint(a, pltpu.HBM) for a in pairs], *extra)
    return (send_sems, recv_sems, thru, same_block), token


def finish_copies(handle, after, name):
    send_sems, recv_sems, thru, same_block = handle
    n = len(thru) // 2

    def body(*refs):
        send_sems, recv_sems = refs[2 * n], refs[2 * n + 1]
        for k in range(n):
            sends, arrivals = _peer_copies(refs[2 * k], refs[2 * k + 1], send_sems, recv_sems, 7 * k, same_block)
            for cp in sends:
                cp.wait_send()
            for cp in arrivals:
                cp.wait_recv()

    hbm = lambda a: pltpu.HBM(a.shape, a.dtype)
    outs = pl.pallas_call(
        body, name=name, out_shape=tuple(hbm(a) for a in thru),
        in_specs=[HBM_SPEC] * (2 * n) + [SEM_SPEC, SEM_SPEC, pl.BlockSpec(memory_space=pl.ANY)],
        out_specs=tuple([HBM_SPEC] * (2 * n)), input_output_aliases={k: k for k in range(2 * n)},
        compiler_params=pltpu.CompilerParams(has_side_effects=DATAFLOW),
    )(*thru, send_sems, recv_sems, after)
    return [outs[2 * k + 1] for k in range(n)]


def tied(x, token):
    return x + token[0:1, 0:1].astype(x.dtype)


MM_TILES = {
    "proj_qkv": (S, 512), "proj_rest": (S, 256), "mix": (512, D), "mlp_up": (S, 512), "mlp_down": (1024, 256),
    "mlp_down_dgrad": (S, 1024), "mlp_down_wgrad": (1024, 1024), "mlp_up_wgrad": (1024, 512),
    "mlp_up_dgrad": (1024, 512), "mix_dgrad": (1024, 512), "mix_wgrad": (512, 1024),
    "proj_wgrad": (1024, PROJ // 2), "proj_dgrad": (1024, 512),
}


def mm_layer(kind, l, a, b, **kw):
    tm, tn = MM_TILES[kind]
    return mm(a, b, tm=tm, tn=tn, name=f"{kind}{l}", **kw)


def mm(a, b, *, tm, tn, out_dtypes, epilogue=None, extras=(), name, trans_a=False, trans_b=False,
       cols=None, b_blocks=False, out_blocks=False):
    if trans_a:
        kdim, m = a.shape
    else:
        m, kdim = a.shape
    shard = b.shape[-1] if b_blocks else None
    if b_blocks:
        full = (b.shape[1], NDEV * shard)
    else:
        full = b.shape
    first, ncols = cols if cols is not None else (0, full[0] if trans_b else full[1])
    assert full[1 if trans_b else 0] == kdim and m % tm == 0 and ncols % tn == 0 and first % tn == 0
    j0 = first // tn
    if trans_a:
        a_spec = pl.BlockSpec((kdim, tm), lambda i, j: (0, i))
    else:
        a_spec = pl.BlockSpec((tm, kdim), lambda i, j: (i, 0))
    if b_blocks and trans_b:
        b_spec = pl.BlockSpec((NDEV, tn, shard), lambda i, j: (0, j0 + j, 0))
    elif b_blocks:
        assert tn == shard
        b_spec = pl.BlockSpec((None, kdim, tn), lambda i, j: (j0 + j, 0, 0))
    elif trans_b:
        b_spec = pl.BlockSpec((tn, kdim), lambda i, j: (j0 + j, 0))
    else:
        b_spec = pl.BlockSpec((kdim, tn), lambda i, j: (0, j0 + j))
    if out_blocks:
        assert tn * NDEV == ncols
        out_spec = pl.BlockSpec((None, tm, tn), lambda i, j: (j, i, 0))
        out_dims = (NDEV, m, tn)
    else:
        out_spec = pl.BlockSpec((tm, tn), lambda i, j: (i, j))
        out_dims = (m, ncols)
    ex_specs = []
    for arr, kind in extras:
        if kind == "tile":
            ex_specs.append(pl.BlockSpec((tm, tn), lambda i, j: (i, j)))
        elif kind == "col":
            ex_specs.append(pl.BlockSpec((1, tn), lambda i, j: (0, j)))
        else:
            ex_specs.append(pl.BlockSpec(arr.shape, lambda i, j: (0, 0)))
    n_ex, n_out = len(extras), len(out_dtypes)
    used = [k for k, (_, kind) in enumerate(extras) if kind != "tie"]

    def body(a_ref, b_ref, *rest):
        ex_refs, out_refs = rest[:n_ex], rest[n_ex:]
        if trans_a:
            acc = lax.dot_general(a_ref[...], b_ref[...], (((0,), (0,)), ((), ())),
                                  preferred_element_type=F32)
        elif trans_b and b_blocks:
            acc = jnp.zeros((tm, tn), F32)
            for d in range(NDEV):
                acc = acc + lax.dot_general(a_ref[:, d * shard:(d + 1) * shard], b_ref[d],
                                            (((1,), (1,)), ((), ())), preferred_element_type=F32)
        elif trans_b:
            acc = lax.dot_general(a_ref[...], b_ref[...], (((1,), (1,)), ((), ())),
                                  preferred_element_type=F32)
        else:
            acc = jnp.dot(a_ref[...], b_ref[...], preferred_element_type=F32)
        outs = (acc,) if epilogue is None else epilogue(acc, *[ex_refs[k][...] for k in used])
        for o_ref, val in zip(out_refs, outs):
            o_ref[...] = val.astype(o_ref.dtype)

    outs = pl.pallas_call(
        body, name=name, grid=(m // tm, ncols // tn),
        in_specs=[a_spec, b_spec] + ex_specs,
        out_specs=[out_spec for _ in range(n_out)],
        out_shape=[jax.ShapeDtypeStruct(out_dims, dt) for dt in out_dtypes],
        compiler_params=_cparams(("parallel", "parallel")),
    )(a, b, *[arr for arr, _ in extras])
    return list(outs)


TR = 512

ROW_SPEC = pl.BlockSpec((TR, D), lambda i: (i, 0))
VEC_SPEC = pl.BlockSpec((1, D), lambda i: (0, 0))


def _residual_then_norm(acc, xr, gate, g, sc, sh):
    x_new = xr + gate * acc
    rstd = lax.rsqrt(jnp.mean(x_new * x_new, axis=-1, keepdims=True) + EPS)
    return acc, x_new, ((x_new * rstd) * g) * (1.0 + sc) + sh


def normmod_fwd(x, g, sc, sh, name):
    def body(x_ref, g_ref, sc_ref, sh_ref, o_ref):
        xv = x_ref[...]
        rstd = lax.rsqrt(jnp.mean(xv * xv, axis=-1, keepdims=True) + EPS)
        n = (xv * rstd) * g_ref[...]
        o_ref[...] = (n * (1.0 + sc_ref[...]) + sh_ref[...]).astype(o_ref.dtype)

    return pl.pallas_call(
        body, name=name, grid=(S // TR,),
        in_specs=[ROW_SPEC, VEC_SPEC, VEC_SPEC, VEC_SPEC], out_specs=ROW_SPEC,
        out_shape=jax.ShapeDtypeStruct((S, D), BF16),
        compiler_params=_cparams(("parallel",)),
    )(x, g, sc, sh)


def _gate_next(dxv, refs):
    br_ref, gate_ref, dbr_ref, dgate_ref = refs

    @pl.when(pl.program_id(0) == 0)
    def _():
        dgate_ref[...] = jnp.zeros_like(dgate_ref)

    dbr_ref[...] = (dxv * gate_ref[...]).astype(dbr_ref.dtype)
    dgate_ref[...] += jnp.sum(dxv * br_ref[...], axis=0, keepdims=True)


GATE_NEXT_IN = [ROW_SPEC, VEC_SPEC]
GATE_NEXT_OUT = [ROW_SPEC, VEC_SPEC]
GATE_NEXT_SHAPES = [jax.ShapeDtypeStruct((S, D), BF16), jax.ShapeDtypeStruct((1, D), F32)]


def normmod_bwd(x, dh, dres, g, sc, name, gate_next=None):
    nxt = 2 if gate_next else 0

    def body(x_ref, dh_ref, dres_ref, g_ref, sc_ref, *rest):
        nxt_in, (dx_ref, dsc_ref, dsh_ref, dg_ref), nxt_out = rest[:nxt], rest[nxt:nxt + 4], rest[nxt + 4:]

        @pl.when(pl.program_id(0) == 0)
        def _():
            dsc_ref[...] = jnp.zeros_like(dsc_ref)
            dsh_ref[...] = jnp.zeros_like(dsh_ref)
            dg_ref[...] = jnp.zeros_like(dg_ref)

        xv, dh = x_ref[...], dh_ref[...]
        gv = g_ref[...]
        rstd = lax.rsqrt(jnp.mean(xv * xv, axis=-1, keepdims=True) + EPS)
        xhat = xv * rstd
        dn = dh * (1.0 + sc_ref[...])
        dxhat = dn * gv
        dxv = dres_ref[...] + rstd * (dxhat - xhat * jnp.mean(dxhat * xhat, axis=-1, keepdims=True))
        dx_ref[...] = dxv
        dsc_ref[...] += jnp.sum(dh * (xhat * gv), axis=0, keepdims=True)
        dsh_ref[...] += jnp.sum(dh, axis=0, keepdims=True)
        dg_ref[...] += jnp.sum(dn * xhat, axis=0, keepdims=True)
        if gate_next:
            _gate_next(dxv, nxt_in + nxt_out)

    vec_out = jax.ShapeDtypeStruct((1, D), F32)
    on = bool(gate_next)
    return pl.pallas_call(
        body, name=name, grid=(S // TR,),
        in_specs=[ROW_SPEC, ROW_SPEC, ROW_SPEC, VEC_SPEC, VEC_SPEC] + GATE_NEXT_IN * on,
        out_specs=[ROW_SPEC, VEC_SPEC, VEC_SPEC, VEC_SPEC] + GATE_NEXT_OUT * on,
        out_shape=[jax.ShapeDtypeStruct((S, D), F32), vec_out, vec_out, vec_out] + GATE_NEXT_SHAPES * on,
        compiler_params=_cparams(("arbitrary",)),
    )(x, dh, dres, g, sc, *(gate_next or ()))


def loss_head(x, target, g, gate_next, name):
    def body(x_ref, t_ref, g_ref, br_ref, gate_ref, dx_ref, loss_ref, dg_ref, dbr_ref, dgate_ref):
        @pl.when(pl.program_id(0) == 0)
        def _():
            loss_ref[...] = jnp.zeros_like(loss_ref)
            dg_ref[...] = jnp.zeros_like(dg_ref)

        xv, gv = x_ref[...], g_ref[...]
        rstd = lax.rsqrt(jnp.mean(xv * xv, axis=-1, keepdims=True) + EPS)
        xhat = xv * rstd
        err = xhat * gv - t_ref[...]
        loss_ref[...] += jnp.sum(err * err) * (0.5 / D)
        dy = err * (1.0 / D)
        dg_ref[...] += jnp.sum(dy * xhat, axis=0, keepdims=True)
        dxhat = dy * gv
        dxv = rstd * (dxhat - xhat * jnp.mean(dxhat * xhat, axis=-1, keepdims=True))
        dx_ref[...] = dxv
        _gate_next(dxv, (br_ref, gate_ref, dbr_ref, dgate_ref))

    return pl.pallas_call(
        body, name=name, grid=(S // TR,),
        in_specs=[ROW_SPEC, ROW_SPEC, VEC_SPEC] + GATE_NEXT_IN,
        out_specs=[ROW_SPEC, VEC_SPEC, VEC_SPEC] + GATE_NEXT_OUT,
        out_shape=[jax.ShapeDtypeStruct((S, D), F32), jax.ShapeDtypeStruct((1, D), F32),
                   jax.ShapeDtypeStruct((1, D), F32)] + GATE_NEXT_SHAPES,
        compiler_params=_cparams(("arbitrary",)),
    )(x, target, g, *gate_next)


TQ = 512
RS = 128
NSUB = TQ // RS
TK = 128


def _dot_hilo(a, tri_twice):
    hi = a.astype(BF16)
    lo = (a - hi.astype(F32)).astype(BF16)
    return jnp.dot(jnp.concatenate([hi, lo], axis=1), tri_twice, preferred_element_type=F32)


def _log_stay(z):
    neg = -z
    return jnp.minimum(neg, 0.0) - jnp.log(1.0 + jnp.exp(jnp.minimum(z, neg)))


def _tri_and_ones(kind):
    row = jnp.bitwise_and(lax.broadcasted_iota(jnp.int32, (2 * TK, 2 * TK), 0), TK - 1)
    col = lax.broadcasted_iota(jnp.int32, (2 * TK, 2 * TK), 1)
    tri = {"after": row > col, "upto": row <= col, "before": row < col}[kind]
    return jnp.logical_or(col >= TK, tri).astype(BF16)


NPAIR = NH // 2
SCALE = HD ** -0.5


def _pair_specs(first_block):
    rows = pl.BlockSpec((TQ, LANES), lambda p, i: (i, first_block + p))
    whole = pl.BlockSpec((S, LANES), lambda p, i: (0, first_block + p))
    return rows, whole


Q_ROWS_SPEC, _ = _pair_specs(0)
_, K_ALL_SPEC = _pair_specs(NPAIR)
_, V_ALL_SPEC = _pair_specs(2 * NPAIR)
PAIR_ROWS_SPEC = pl.BlockSpec((TQ, LANES), lambda p, i: (i, p))
PAIR_ALL_SPEC = pl.BlockSpec((S, LANES), lambda p, i: (0, p))
PAIR_TOTAL_SPEC = pl.BlockSpec((2, TQ, TK), lambda p, i: (p, i, 0))


def _head_halves(x):
    first = lax.broadcasted_iota(jnp.int32, x.shape, 1) < HD
    zero = jnp.zeros_like(x)
    return jnp.where(first, x, zero), jnp.where(first, zero, x)


def _join_heads(a, b):
    return jnp.where(lax.broadcasted_iota(jnp.int32, a.shape, 1) < HD, a, b)


def _comm_hooks(comm, refs, n_in, n_out, n_scratch):
    nc = len(comm.arrs) if comm is not None else 0
    ins, cin = refs[:n_in], refs[n_in:n_in + nc]
    outs = refs[n_in + nc:n_in + nc + n_out]
    cout = refs[n_in + nc + n_out:n_in + 2 * nc + n_out]
    scratch = refs[n_in + 2 * nc + n_out:n_in + 2 * nc + n_out + n_scratch]
    sems = refs[n_in + 2 * nc + n_out + n_scratch:]
    phases = comm.phases(cin, cout, sems) if comm is not None else None
    return ins, outs, scratch, phases


def _with_comm(comm, in_specs, out_specs, out_shape, operands, scratch):
    if comm is None:
        return dict(in_specs=in_specs, out_specs=out_specs, out_shape=out_shape, scratch_shapes=scratch), operands
    nc = len(comm.arrs)
    return dict(in_specs=in_specs + [HBM_SPEC] * nc, out_specs=out_specs + [HBM_SPEC] * nc,
                out_shape=out_shape + comm.out_shape, scratch_shapes=scratch + comm.scratch), operands + comm.arrs


def attn_fwd(qkv, name, comm=None):
    n_steps = S // TQ

    def body(*refs):
        (q_ref, k_ref, v_ref), (o_ref, r_ref), (acc_ref, z_even, z_odd, w_ref), phases = _comm_hooks(
            comm, refs, 3, 2, 4)
        p = pl.program_id(0)
        i = pl.program_id(1)
        if phases is not None:
            pl.when(jnp.logical_and(p == 0, i == 0))(phases[0])
            pl.when(jnp.logical_and(p == NPAIR - 1, i == n_steps - 1))(phases[1])
        chains = [(sub, h) for sub in range(NSUB) for h in range(2)]
        q_sub = [_head_halves(q_ref[pl.ds(sub * RS, RS), :] * SCALE) for sub in range(NSUB)]
        after = _tri_and_ones("after")
        below_diagonal = (lax.broadcasted_iota(jnp.int32, (RS, TK), 1)
                          < lax.broadcasted_iota(jnp.int32, (RS, TK), 0))
        base = i * NSUB
        all_subs = list(range(NSUB))

        acc_ref[...] = jnp.zeros_like(acc_ref)
        r_ref[...] = jnp.zeros_like(r_ref)
        w_ref[...] = jnp.zeros_like(w_ref)

        def key_rows(block):
            return pl.ds(pl.multiple_of(block * TK, TK), TK)

        def store_scores(z_ref, block, subs):
            kb = k_ref[key_rows(block), :]
            for c, (sub, h) in enumerate(chains):
                if sub in subs:
                    z_ref[c] = lax.dot_general(q_sub[sub][h], kb, (((1,), (1,)), ((), ())),
                                               preferred_element_type=F32)

        def add_weighted_values(block, subs):
            vb = v_ref[key_rows(block), :]
            for sub in subs:
                acc_ref[pl.ds(sub * RS, RS), :] += _join_heads(*[
                    jnp.dot(w_ref[2 * sub + h], vb, preferred_element_type=F32) for h in range(2)])

        def step(block, z_ref, z_next_ref, subs, diagonal_sub, prev_subs, next_subs):
            if prev_subs:
                add_weighted_values(block + 1, prev_subs)
            if next_subs:
                store_scores(z_next_ref, jnp.maximum(block - 1, 0), next_subs)
            active = [(c, sub, h) for c, (sub, h) in enumerate(chains) if sub in subs]
            ls, sums = {}, {}
            for c, sub, h in active:
                ls[c] = _log_stay(z_ref[c])
                sums[c] = _dot_hilo(jnp.where(below_diagonal, ls[c], 0.0) if sub == diagonal_sub else ls[c], after)
            for c, sub, h in active:
                rows = pl.ds(sub * RS, RS)
                later = r_ref[h, rows, :]
                w = jnp.exp(z_ref[c] + ls[c] + (sums[c][:, :TK] + later))
                if sub == diagonal_sub:
                    w = jnp.where(below_diagonal, w, 0.0)
                w_ref[c] = w.astype(BF16)
                r_ref[h, rows, :] = later + sums[c][:, TK:]

        store_scores(z_even, base + NSUB - 1, [NSUB - 1])
        buffers = (z_even, z_odd)
        for j in reversed(range(NSUB)):
            subs = all_subs[j:]
            step(base + j, buffers[0], buffers[1], subs, j, all_subs[j + 1:], all_subs[j - 1:] if j else all_subs)
            buffers = buffers[::-1]
        assert buffers[0] is z_even

        @pl.loop(0, base // 2)
        def _(pair):
            block = base - 1 - 2 * pair
            step(block, z_even, z_odd, all_subs, None, all_subs, all_subs)
            step(block - 1, z_odd, z_even, all_subs, None, all_subs, all_subs)

        add_weighted_values(0, all_subs)
        o_ref[...] = acc_ref[...].astype(o_ref.dtype)
        if phases is not None:
            pl.when(jnp.logical_and(p == NPAIR - 1, i == n_steps - 1))(phases[2])

    kwargs, operands = _with_comm(
        comm, [Q_ROWS_SPEC, K_ALL_SPEC, V_ALL_SPEC], [PAIR_ROWS_SPEC, PAIR_TOTAL_SPEC],
        [jax.ShapeDtypeStruct((S, NH * HD), BF16), jax.ShapeDtypeStruct((NH, S, TK), F32)], [qkv, qkv, qkv],
        [pltpu.VMEM((TQ, LANES), F32), pltpu.VMEM((2 * NSUB, RS, TK), F32), pltpu.VMEM((2 * NSUB, RS, TK), F32),
         pltpu.VMEM((2 * NSUB, RS, TK), BF16)])
    return pl.pallas_call(
        body, name=name, grid=(NPAIR, n_steps),
        compiler_params=_cparams(("arbitrary", "arbitrary")), **kwargs,
    )(*operands)


def attn_bwd(qkv, dout, totals, name, comm=None):
    n_steps = S // TQ

    def body(*refs):
        ((q_ref, k_ref, v_ref, do_ref, r_ref), (dq_out, dk_out, dv_out),
         (z_even, z_odd, dw_even, dw_odd, before_ref, dbefore_ref, dz_ref, w_ref, dq_ref, dk_ref, dv_ref),
         phases) = _comm_hooks(comm, refs, 5, 3, 11)
        p = pl.program_id(0)
        i = pl.program_id(1)
        if phases is not None:
            pl.when(jnp.logical_and(p == 0, i == 0))(phases[0])
            pl.when(jnp.logical_and(p == NPAIR - 1, i == n_steps - 2))(phases[1])

        @pl.when(i == 0)
        def _():
            dk_ref[...] = jnp.zeros_like(dk_ref)
            dv_ref[...] = jnp.zeros_like(dv_ref)

        chains = [(sub, h) for sub in range(NSUB) for h in range(2)]
        nch = len(chains)
        qb = q_ref[...]
        dob = do_ref[...].astype(BF16)
        q_sub = [_head_halves(qb[sub * RS:(sub + 1) * RS] * SCALE) for sub in range(NSUB)]
        do_sub = [_head_halves(dob[sub * RS:(sub + 1) * RS]) for sub in range(NSUB)]
        upto = _tri_and_ones("upto")
        before_tri = _tri_and_ones("before")
        below_diagonal = (lax.broadcasted_iota(jnp.int32, (RS, TK), 1)
                          < lax.broadcasted_iota(jnp.int32, (RS, TK), 0))
        contract_lanes = (((1,), (1,)), ((), ()))
        contract_rows = (((0,), (0,)), ((), ()))
        base = i * NSUB
        all_subs = list(range(NSUB))

        def key_rows(block):
            return pl.ds(pl.multiple_of(block * TK, TK), TK)

        def store_products(bufs, block, subs):
            z_ref, dw_ref = bufs
            kb = k_ref[key_rows(block), :]
            vb = v_ref[key_rows(block), :]
            for c, (sub, h) in enumerate(chains):
                if sub in subs:
                    z_ref[c] = lax.dot_general(q_sub[sub][h], kb, contract_lanes, preferred_element_type=F32)
                    dw_ref[c] = lax.dot_general(do_sub[sub][h], vb, contract_lanes, preferred_element_type=F32)

        def add_gradients(block, subs):
            kb = k_ref[key_rows(block), :]
            for sub in subs:
                rows = pl.ds(sub * RS, RS)
                dq_ref[rows, :] += _join_heads(*[jnp.dot(dz_ref[h, rows, :], kb, preferred_element_type=F32)
                                                 for h in range(2)])
            dk_ref[key_rows(block), :] += _join_heads(*[
                lax.dot_general(dz_ref[h], qb, contract_rows, preferred_element_type=F32) for h in range(2)])
            dv_ref[key_rows(block), :] += _join_heads(*[
                lax.dot_general(w_ref[h], dob, contract_rows, preferred_element_type=F32) for h in range(2)])

        for ref in (dq_ref, before_ref, dbefore_ref, dz_ref, w_ref):
            ref[...] = jnp.zeros_like(ref)
        even, odd = (z_even, dw_even), (z_odd, dw_odd)
        store_products(even, 0, all_subs)

        def step(block, bufs, next_bufs, subs, diagonal_sub, prev_subs, next_subs):
            z_ref, dw_ref = bufs
            add_gradients(jnp.maximum(block - 1, 0), prev_subs)
            for sub in prev_subs:
                if sub not in subs:
                    dz_ref[:, pl.ds(sub * RS, RS), :] = jnp.zeros((2, RS, TK), BF16)
                    w_ref[:, pl.ds(sub * RS, RS), :] = jnp.zeros((2, RS, TK), BF16)
            if next_subs:
                store_products(next_bufs, block + 1, next_subs)
            active = [(c, sub, h) for c, (sub, h) in enumerate(chains) if sub in subs]
            ls, sums, dl, dsums = {}, {}, {}, {}
            for c, sub, h in active:
                ls[c] = _log_stay(z_ref[c])
                sums[c] = _dot_hilo(jnp.where(below_diagonal, ls[c], 0.0) if sub == diagonal_sub else ls[c], upto)
            for c, sub, h in active:
                rows = pl.ds(sub * RS, RS)
                before = before_ref[c]
                log_after = r_ref[h, rows, :] - (sums[c][:, :TK] + before)
                w = jnp.exp((z_ref[c] + ls[c]) + log_after)
                if sub == diagonal_sub:
                    w = jnp.where(below_diagonal, w, 0.0)
                dl[c] = dw_ref[c] * w
                dsums[c] = _dot_hilo(dl[c], before_tri)
                w_ref[h, rows, :] = w.astype(BF16)
                before_ref[c] = before + sums[c][:, TK:]
            for c, sub, h in active:
                rows = pl.ds(sub * RS, RS)
                dbefore = dbefore_ref[c]
                beta = jnp.exp(z_ref[c] + ls[c])
                if sub == diagonal_sub:
                    beta = jnp.where(below_diagonal, beta, 0.0)
                dstay = dsums[c][:, :TK] + dbefore
                dz_ref[h, rows, :] = ((dl[c] - beta * (dl[c] + dstay)) * SCALE).astype(BF16)
                dbefore_ref[c] = dbefore + dsums[c][:, TK:]

        @pl.loop(0, base // 2)
        def _(pair):
            step(2 * pair, even, odd, all_subs, None, all_subs, all_subs)
            step(2 * pair + 1, odd, even, all_subs, None, all_subs, all_subs)

        bufs = (even, odd)
        for j in range(NSUB):
            step(base + j, bufs[0], bufs[1], all_subs[j:], j, all_subs[j - 1:] if j else all_subs, all_subs[j + 1:])
            bufs = bufs[::-1]

        add_gradients(base + NSUB - 1, all_subs[NSUB - 1:])
        dq_out[...] = dq_ref[...].astype(dq_out.dtype)

        @pl.when(i == n_steps - 1)
        def _():
            dk_out[...] = dk_ref[...].astype(dk_out.dtype)
            dv_out[...] = dv_ref[...].astype(dv_out.dtype)

        if phases is not None:
            pl.when(jnp.logical_and(p == NPAIR - 1, i == n_steps - 1))(phases[2])

    full = jax.ShapeDtypeStruct((S, NH * HD), BF16)
    kwargs, operands = _with_comm(
        comm, [Q_ROWS_SPEC, K_ALL_SPEC, V_ALL_SPEC, PAIR_ROWS_SPEC, PAIR_TOTAL_SPEC],
        [PAIR_ROWS_SPEC, PAIR_ALL_SPEC, PAIR_ALL_SPEC], [full, full, full], [qkv, qkv, qkv, dout, totals],
        [pltpu.VMEM((2 * NSUB, RS, TK), F32)] * 6 + [pltpu.VMEM((2, TQ, TK), BF16)] * 2
        + [pltpu.VMEM((TQ, LANES), F32), pltpu.VMEM((S, LANES), F32), pltpu.VMEM((S, LANES), F32)])
    return pl.pallas_call(
        body, name=name, grid=(NPAIR, n_steps),
        compiler_params=_cparams(("arbitrary", "arbitrary")), **kwargs,
    )(*operands)


def _proj_cols(first_col):
    base = first_col // LANES
    return pl.BlockSpec((S, LANES), lambda j: (0, base + j))


CONV_OUT_SPEC = pl.BlockSpec((S, LANES), lambda j: (0, j))
CONV_DOUT_SPEC = pl.BlockSpec((S, LANES), lambda j: (0, (NH * HD) // LANES + j))
CONV_W_SPEC = pl.BlockSpec((8, LANES), lambda j: (0, j))
CONV_B_SPEC = pl.BlockSpec((1, LANES), lambda j: (0, j))


def _shift_down(u, n):
    rows = lax.broadcasted_iota(jnp.int32, u.shape, 0)
    return jnp.where(rows >= n, pltpu.roll(u, n, 0), 0.0)


def _shift_up(u, n):
    rows = lax.broadcasted_iota(jnp.int32, u.shape, 0)
    return jnp.where(rows < S - n, pltpu.roll(u, S - n, 0), 0.0)


def conv_fwd(proj, cw8, cb, name):
    def body(bg_ref, cg_ref, hc_ref, w_ref, b_ref, o_ref):
        u = cg_ref[...] * hc_ref[...]
        w = w_ref[...]
        y = w[0:1, :] * _shift_down(u, 2) + w[1:2, :] * _shift_down(u, 1) + w[2:3, :] * u + b_ref[...]
        o_ref[...] = bg_ref[...] * y

    return pl.pallas_call(
        body, name=name, grid=(CW // LANES,),
        in_specs=[_proj_cols(0), _proj_cols(CW), _proj_cols(2 * CW), CONV_W_SPEC, CONV_B_SPEC],
        out_specs=CONV_OUT_SPEC, out_shape=jax.ShapeDtypeStruct((S, CW), F32),
        compiler_params=_cparams(("parallel",)),
    )(proj, proj, proj, cw8, cb)


def conv_bwd(proj, dout, cw8, cb, name):
    def body(bg_ref, cg_ref, hc_ref, do_ref, w_ref, b_ref, dbg_ref, dcg_ref, dhc_ref, dw_ref, db_ref):
        cg, hc, do = cg_ref[...], hc_ref[...], do_ref[...]
        w = w_ref[...]
        u = cg * hc
        u1, u2 = _shift_down(u, 1), _shift_down(u, 2)
        y = w[0:1, :] * u2 + w[1:2, :] * u1 + w[2:3, :] * u + b_ref[...]
        dbg_ref[...] = (do * y).astype(dbg_ref.dtype)
        dy = do * bg_ref[...]
        db_ref[...] = jnp.sum(dy, axis=0, keepdims=True)
        dw_ref[...] = jnp.concatenate(
            [jnp.sum(dy * u2, axis=0, keepdims=True), jnp.sum(dy * u1, axis=0, keepdims=True),
             jnp.sum(dy * u, axis=0, keepdims=True), jnp.zeros((5, LANES), F32)], axis=0)
        du = w[2:3, :] * dy + w[1:2, :] * _shift_up(dy, 1) + w[0:1, :] * _shift_up(dy, 2)
        dcg_ref[...] = (du * hc).astype(dcg_ref.dtype)
        dhc_ref[...] = (du * cg).astype(dhc_ref.dtype)

    full = jax.ShapeDtypeStruct((S, CW), BF16)
    return pl.pallas_call(
        body, name=name, grid=(CW // LANES,),
        in_specs=[_proj_cols(0), _proj_cols(CW), _proj_cols(2 * CW), CONV_DOUT_SPEC, CONV_W_SPEC, CONV_B_SPEC],
        out_specs=[CONV_OUT_SPEC, CONV_OUT_SPEC, CONV_OUT_SPEC, CONV_W_SPEC, CONV_B_SPEC],
        out_shape=[full, full, full, jax.ShapeDtypeStruct((8, CW), F32), jax.ShapeDtypeStruct((1, CW), F32)],
        compiler_params=_cparams(("parallel",)),
    )(proj, proj, proj, dout, cw8, cb)


GELU_K = math.sqrt(2.0 / math.pi)
GELU_C = 0.044715


def _gelu(x):
    return 0.5 * x * (1.0 + jnp.tanh(GELU_K * (x + GELU_C * (x * x * x))))


def _gelu_grad(x):
    t = jnp.tanh(GELU_K * (x + GELU_C * (x * x * x)))
    return 0.5 * (1.0 + t) + 0.5 * x * (1.0 - t * t) * (GELU_K * (1.0 + 3.0 * GELU_C * (x * x)))


def _sg_masks():
    row = lax.broadcasted_iota(jnp.int32, (T, T), 0)
    col = lax.broadcasted_iota(jnp.int32, (T, T), 1)
    causal = jnp.right_shift(row, 6) >= jnp.right_shift(col, 6)
    head_of_col = jnp.right_shift(lax.broadcasted_iota(jnp.int32, (T, CW), 1), 6)
    return causal, head_of_col


def _sg_weights(sw_ref, causal):
    return [jnp.where(causal, sw_ref[h], 0.0).astype(BF16) for h in range(SG_HEADS)]


def _sg_mixed(vnb, weights, bias, head_of_col):
    mixed = bias
    for h in range(SG_HEADS):
        mh = jnp.dot(weights[h], vnb, preferred_element_type=F32)
        mixed = mixed + jnp.where(head_of_col == h, mh, 0.0)
    return mixed


SG_WINDOWS = 4
SG_ROWS = SG_WINDOWS * T
SG_U_SPEC = pl.BlockSpec((SG_ROWS, CW), lambda n: (n, 3))
SG_V_SPEC = pl.BlockSpec((SG_ROWS, CW), lambda n: (n, 4))
SG_ROW_SPEC = pl.BlockSpec((SG_ROWS, CW), lambda n: (n, 0))
SG_DOUT_SPEC = pl.BlockSpec((SG_ROWS, CW), lambda n: (n, 3))
SG_G_SPEC = pl.BlockSpec((1, CW), lambda n: (0, 0))
SG_W_SPEC = pl.BlockSpec((SG_HEADS, T, T), lambda n: (0, 0, 0))
SG_BIAS_SPEC = pl.BlockSpec((T, CW), lambda n: (0, 0))


def sg_fwd(proj, gn, sw, bias, name):
    def body(u_ref, v_ref, g_ref, sw_ref, bias_ref, o_ref):
        causal, head_of_col = _sg_masks()
        weights = _sg_weights(sw_ref, causal)
        for wdw in range(SG_WINDOWS):
            rows = pl.ds(wdw * T, T)
            gv = _gelu(v_ref[rows, :])
            rstd = lax.rsqrt(jnp.mean(gv * gv, axis=-1, keepdims=True) + EPS)
            vnb = ((gv * rstd) * g_ref[...]).astype(BF16)
            mixed = _sg_mixed(vnb, weights, bias_ref[...], head_of_col)
            o_ref[rows, :] = _gelu(u_ref[rows, :]) * mixed

    return pl.pallas_call(
        body, name=name, grid=(S // SG_ROWS,),
        in_specs=[SG_U_SPEC, SG_V_SPEC, SG_G_SPEC, SG_W_SPEC, SG_BIAS_SPEC],
        out_specs=SG_ROW_SPEC, out_shape=jax.ShapeDtypeStruct((S, CW), F32),
        compiler_params=_cparams(("parallel",)),
    )(proj, proj, gn, sw, bias)


def sg_bwd(proj, dout, gn, sw, bias, name):
    def body(u_ref, v_ref, do_ref, g_ref, sw_ref, bias_ref, du_ref, dv_ref, dg_ref, dsw_ref, dbias_ref):
        @pl.when(pl.program_id(0) == 0)
        def _():
            dg_ref[...] = jnp.zeros_like(dg_ref)
            dsw_ref[...] = jnp.zeros_like(dsw_ref)
            dbias_ref[...] = jnp.zeros_like(dbias_ref)

        causal, head_of_col = _sg_masks()
        weights = _sg_weights(sw_ref, causal)
        gnv = g_ref[...]
        for wdw in range(SG_WINDOWS):
            rows = pl.ds(wdw * T, T)
            uv, vv, do = u_ref[rows, :], v_ref[rows, :], do_ref[rows, :]
            gv = _gelu(vv)
            rstd = lax.rsqrt(jnp.mean(gv * gv, axis=-1, keepdims=True) + EPS)
            xhat = gv * rstd
            vnb = (xhat * gnv).astype(BF16)
            mixed = _sg_mixed(vnb, weights, bias_ref[...], head_of_col)
            du_ref[rows, :] = ((do * mixed) * _gelu_grad(uv)).astype(du_ref.dtype)
            dmix = do * _gelu(uv)
            dbias_ref[...] += dmix
            dmixb = dmix.astype(BF16)
            dvn = jnp.zeros((T, CW), F32)
            for h in range(SG_HEADS):
                dvh = lax.dot_general(weights[h], dmixb, (((0,), (0,)), ((), ())), preferred_element_type=F32)
                dvn = dvn + jnp.where(head_of_col == h, dvh, 0.0)
                dmh = jnp.where(head_of_col == h, dmixb, jnp.zeros_like(dmixb))
                dwh = lax.dot_general(dmh, vnb, (((1,), (1,)), ((), ())), preferred_element_type=F32)
                dsw_ref[h] += jnp.where(causal, dwh, 0.0)
            dg_ref[...] += jnp.sum(dvn * xhat, axis=0, keepdims=True)
            dxhat = dvn * gnv
            dgv = rstd * (dxhat - xhat * jnp.mean(dxhat * xhat, axis=-1, keepdims=True))
            dv_ref[rows, :] = (dgv * _gelu_grad(vv)).astype(dv_ref.dtype)

    full = jax.ShapeDtypeStruct((S, CW), BF16)
    return pl.pallas_call(
        body, name=name, grid=(S // SG_ROWS,),
        in_specs=[SG_U_SPEC, SG_V_SPEC, SG_DOUT_SPEC, SG_G_SPEC, SG_W_SPEC, SG_BIAS_SPEC],
        out_specs=[SG_ROW_SPEC, SG_ROW_SPEC, SG_G_SPEC, SG_W_SPEC, SG_BIAS_SPEC],
        out_shape=[full, full, jax.ShapeDtypeStruct((1, CW), F32),
                   jax.ShapeDtypeStruct((SG_HEADS, T, T), F32), jax.ShapeDtypeStruct((T, CW), F32)],
        compiler_params=_cparams(("arbitrary",)),
    )(proj, proj, dout, gn, sw, bias)


ADA_COLS = NMOD * D // NDEV


def ada_fwd(c_all, ada_w, ada_b_mine, name):
    def body(c_ref, w_ref, b_ref, o_ref, ca_ref):
        cv = c_ref[...]
        ca = cv * (1.0 / (1.0 + jnp.exp(-cv)))
        ca_ref[...] = ca
        cab = ca.astype(BF16)
        for l in range(L):
            o_ref[l] = jnp.dot(cab, w_ref[l].astype(BF16), preferred_element_type=F32) + b_ref[l]

    return pl.pallas_call(
        body, name=name,
        out_shape=[jax.ShapeDtypeStruct((L, NDEV, ADA_COLS), F32), jax.ShapeDtypeStruct((NDEV, D), F32)],
        compiler_params=_cparams(),
    )(c_all, ada_w, ada_b_mine)


def ada_bwd(ca, dmod_cols, name):
    def body(ca_ref, dm_ref, o_ref):
        cab = ca_ref[...].astype(BF16)
        for l in range(L):
            o_ref[l] = lax.dot_general(cab, dm_ref[l].astype(BF16), (((0,), (0,)), ((), ())),
                                       preferred_element_type=F32)

    return pl.pallas_call(
        body, name=name, out_shape=jax.ShapeDtypeStruct((L, D, ADA_COLS), F32),
        compiler_params=_cparams(),
    )(ca, dmod_cols)


def _adamw(w, g, m, v):
    m = B1 * m + (1.0 - B1) * g
    v = B2 * v + (1.0 - B2) * (g * g)
    m_hat = m / BC1
    v_hat = v / BC2
    delta = -LR * (m_hat / (jnp.sqrt(v_hat) + AEPS) + WD * w)
    return delta, m, v


VEC_ROWS_PER_LAYER = 8
VEC_FINAL_ROW = L * VEC_ROWS_PER_LAYER
VEC_ROWS = VEC_FINAL_ROW + 8
W256_TAPS, W256_CONV_B, W256_GN = 0, 8, 9
W256_ROWS_PER_LAYER = 16


def small_update(vec_all, w256_all, sb_all, sw_all, params, name):
    n_par = len(params)

    def body(*refs):
        vec_ref, w256_ref, sb_ref = refs[:3]
        sw_refs = refs[3:3 + L]
        par_refs = [refs[3 + L + 3 * k:3 + L + 3 * k + 3] for k in range(n_par)]
        out = refs[3 + L + 3 * n_par:]
        out_par = [out[4 * k:4 * k + 4] for k in range(n_par)]
        loss_ref, taps_ref = out[4 * n_par:]

        def total(ref, idx):
            acc = ref[(0,) + idx].astype(F32)
            for d in range(1, NDEV):
                acc = acc + ref[(d,) + idx].astype(F32)
            return acc

        def update(k, region, g):
            w_ref, m_ref, v_ref = par_refs[k]
            g_ref, d_ref, nm_ref, nv_ref = out_par[k]
            delta, nm, nv = _adamw(w_ref[region], g, m_ref[region], v_ref[region])
            g_ref[region] = g
            d_ref[region] = delta
            nm_ref[region] = nm
            nv_ref[region] = nv

        for l in range(L):
            base = l * VEC_ROWS_PER_LAYER
            for k in range(NMOD):
                update(0, (slice(l, l + 1), slice(k * D, (k + 1) * D)), total(vec_ref, (slice(base + k, base + k + 1),)))
            update(1, (slice(l, l + 1),), total(vec_ref, (slice(base + 6, base + 7),)))
            update(2, (slice(l, l + 1),), total(vec_ref, (slice(base + 7, base + 8),)))
            wbase = l * W256_ROWS_PER_LAYER
            update(4, (slice(l, l + 1),), total(w256_ref, (slice(wbase + W256_CONV_B, wbase + W256_CONV_B + 1),)))
            update(5, (slice(l, l + 1),), total(w256_ref, (slice(wbase + W256_GN, wbase + W256_GN + 1),)))
            update(6, (l,), total(sw_refs[l], ()))
            update(7, (l,), total(sb_ref, (slice(l * SG_HEADS, (l + 1) * SG_HEADS),)))
            taps_ref[l] = total(w256_ref, (slice(wbase + W256_TAPS, wbase + W256_TAPS + 8),))
        update(3, (slice(0, 1),), total(vec_ref, (slice(VEC_FINAL_ROW, VEC_FINAL_ROW + 1),)))
        loss_ref[...] = total(vec_ref, (slice(VEC_FINAL_ROW + 1, VEC_FINAL_ROW + 2), slice(0, LANES)))

    out_shape = []
    for w, _, _ in params:
        out_shape += [jax.ShapeDtypeStruct(w.shape, F32)] * 4
    out_shape += [jax.ShapeDtypeStruct((1, LANES), F32), jax.ShapeDtypeStruct((L, 8, CW), F32)]
    outs = pl.pallas_call(body, name=name, out_shape=out_shape, compiler_params=_cparams())(
        vec_all, w256_all, sb_all, *sw_all, *[a for p in params for a in p])
    return [outs[4 * k:4 * k + 4] for k in range(n_par)], outs[4 * n_par:]


def adamw_plain(w, g, m, v, tr, name):
    rows, cols = w.shape
    spec = pl.BlockSpec((tr, cols), lambda i: (i, 0))

    def body(w_ref, g_ref, m_ref, v_ref, d_ref, nm_ref, nv_ref):
        delta, nm, nv = _adamw(w_ref[...], g_ref[...], m_ref[...], v_ref[...])
        d_ref[...] = delta
        nm_ref[...] = nm
        nv_ref[...] = nv

    shp = jax.ShapeDtypeStruct((rows, cols), F32)
    return pl.pallas_call(
        body, name=name, grid=(rows // tr,), in_specs=[spec] * 4, out_specs=[spec] * 3,
        out_shape=[shp, shp, shp], compiler_params=_cparams(("parallel",)),
    )(w, g, m, v)


def adamw_reduce(w, parts, m, v, tr, name, tie=None):
    _, rows, cols = w.shape
    spec = pl.BlockSpec((None, tr, cols), lambda l, i: (l, i, 0))
    pspecs = [pl.BlockSpec((NDEV, tr, cols), lambda l, i, k=k: (0, jnp.where(l == k, i, 0), 0)) for k in range(L)]

    ties = [] if tie is None else [tie]

    def body(w_ref, p0_ref, p1_ref, m_ref, v_ref, *rest):
        g_ref, d_ref, nm_ref, nv_ref = rest[len(ties):]
        first_layer = pl.program_id(0) == 0
        g = jnp.zeros((tr, cols), F32)
        for d in range(NDEV):
            g = g + jnp.where(first_layer, p0_ref[d], p1_ref[d]).astype(F32)
        delta, nm, nv = _adamw(w_ref[...], g, m_ref[...], v_ref[...])
        g_ref[...] = g
        d_ref[...] = delta
        nm_ref[...] = nm
        nv_ref[...] = nv

    shp = jax.ShapeDtypeStruct(w.shape, F32)
    return pl.pallas_call(
        body, name=name, grid=(L, rows // tr),
        in_specs=[spec] + pspecs + [spec, spec] + [pl.BlockSpec(t.shape, lambda l, i: (0, 0)) for t in ties],
        out_specs=[spec] * 4, out_shape=[shp] * 4, compiler_params=_cparams(("parallel", "parallel")),
    )(w, *parts, m, v, *ties)


SHARD_IN = PROJ // NDEV


def shards_to_columns(shards, name):
    tr = 256

    def body(i_ref, o_ref):
        for d in range(NDEV):
            o_ref[:, d * SHARD_IN:(d + 1) * SHARD_IN] = i_ref[d]

    return pl.pallas_call(
        body, name=name, grid=(D // tr,),
        in_specs=[pl.BlockSpec((NDEV, tr, SHARD_IN), lambda i: (0, i, 0))],
        out_specs=pl.BlockSpec((tr, PROJ), lambda i: (i, 0)),
        out_shape=jax.ShapeDtypeStruct((D, PROJ), shards.dtype), compiler_params=_cparams(("parallel",)),
    )(shards)


def columns_to_shards(mat, name):
    tr = 256

    def body(i_ref, o_ref):
        for d in range(NDEV):
            o_ref[d] = i_ref[:, d * SHARD_IN:(d + 1) * SHARD_IN]

    return pl.pallas_call(
        body, name=name, grid=(D // tr,),
        in_specs=[pl.BlockSpec((tr, PROJ), lambda i: (i, 0))],
        out_specs=pl.BlockSpec((NDEV, tr, SHARD_IN), lambda i: (0, i, 0)),
        out_shape=jax.ShapeDtypeStruct((NDEV, D, SHARD_IN), mat.dtype), compiler_params=_cparams(("parallel",)),
    )(mat)


def _pad_rows(flat, rows):
    return jnp.pad(flat, (0, rows * LANES - flat.shape[0])).reshape(rows, LANES)


def kernel(x, c, ada_w, ada_b, norm_mix_g, norm_mlp_g, w_in, conv_w, conv_b, gmlp_norm_g, spatial_w, spatial_b, w_out, mlp_w1, mlp_w2, final_norm_g, loss_target, m_ada_w, m_ada_b, m_norm_mix_g, m_norm_mlp_g, m_w_in, m_conv_w, m_conv_b, m_gmlp_norm_g, m_spatial_w, m_spatial_b, m_w_out, m_mlp_w1, m_mlp_w2, m_final_norm_g, v_ada_w, v_ada_b, v_norm_mix_g, v_norm_mlp_g, v_w_in, v_conv_w, v_conv_b, v_gmlp_norm_g, v_spatial_w, v_spatial_b, v_w_out, v_mlp_w1, v_mlp_w2, v_final_norm_g):
    me = _lin(_my_pos())
    x0 = x[0]
    target = loss_target[0]
    conv_shard = conv_w.shape[-1]

    w_in_b, w_out_b, w1_b, w2_b = [w.astype(BF16) for w in (w_in, w_out, mlp_w1, mlp_w2)]
    pack0 = _pad_rows(jnp.concatenate([c.reshape(-1), conv_w.reshape(-1)]), 16)
    g0, gw_in0 = run_comm(Gather([pack0, w_in_b[0]]), "gather_first")
    g0 = g0.reshape(NDEV, 16 * LANES)
    c_all = g0[:, :D]
    conv_full = (g0[:, D:D + L * 3 * conv_shard].reshape(NDEV, L, 3, conv_shard)
                 .transpose(1, 2, 0, 3).reshape(L, 3, CW))


    W_in = [shards_to_columns(gw_in0, "w_in_columns0"), None]
    W_out, W1, W2 = [None] * L, [None] * L, [None] * L

    ada_b_mine = lax.dynamic_slice(ada_b, (0, me * ADA_COLS), (L, ADA_COLS)).reshape(L, 1, ADA_COLS)
    mod_part, c_act = ada_fwd(c_all, ada_w, ada_b_mine, "ada_fwd")
    gmod = run_comm(Gather([mod_part]), "gather_mod")[0]
    mod = lax.dynamic_index_in_dim(gmod, me, axis=2, keepdims=False)
    mod = mod.transpose(1, 0, 2).reshape(L, NMOD, 1, D)
    early_weights, token = start_copies([w_out_b[0]], me, "gather_early0_start", True, after=gmod)
    mod = tied(mod, token)

    cw8 = jnp.pad(conv_full, ((0, 0), (0, 5), (0, 0)))
    sg_bias = jnp.repeat(spatial_b.transpose(0, 2, 1), HD, axis=2)

    saved = []
    xl = x0
    for l in range(L):
        sh_m, sc_m, g_m, sh_f, sc_f, g_f = [mod[l, k] for k in range(NMOD)]
        h1 = normmod_fwd(xl, norm_mix_g[l:l + 1], sc_m, sh_m, f"norm_mix_fwd{l}")
        if l > 0:
            W_in[l] = shards_to_columns(finish_copies(w_in_handle, xl, f"gather_w_in{l}_wait")[0],
                                        f"w_in_columns{l}")
        qkv = mm_layer("proj_qkv", l, h1, W_in[l], out_dtypes=[BF16], cols=(0, QKV))[0]
        proj = mm_layer("proj_rest", l, h1, W_in[l], out_dtypes=[F32], cols=(QKV, REST))[0]
        a_out, a_tot, gw2, gw1 = attn_fwd(qkv, f"attn_fwd{l}", comm=Gather([w2_b[l], w1_b[l]]))
        gw_out, = finish_copies(early_weights, a_out, f"gather_early{l}_wait")
        W_out[l] = gw_out.reshape(D, D)
        W1[l] = gw1
        W2[l] = gw2.reshape(DFF, D)
        if l + 1 < L:
            w_in_handle, token = start_copies([w_in_b[l + 1]], me, f"gather_w_in{l + 1}_start", True, after=a_out)
            early_weights, token = start_copies([w_out_b[l + 1]], me, f"gather_early{l + 1}_start", True, after=token)
            g_m = tied(g_m, token)
        c_out = conv_fwd(proj, cw8[l], conv_b[l:l + 1], f"conv_fwd{l}")
        s_out = sg_fwd(proj, gmlp_norm_g[l:l + 1], spatial_w[l], sg_bias[l], f"sg_fwd{l}")
        cat = jnp.concatenate([a_out, c_out.astype(BF16), s_out.astype(BF16)], axis=1)
        mix, x1, h2 = mm_layer("mix", l, cat, W_out[l], out_dtypes=[F32, F32, BF16], epilogue=_residual_then_norm,
                               extras=[(xl, "tile"), (g_m, "col"), (norm_mlp_g[l:l + 1], "col"), (sc_f, "col"),
                                       (sh_f, "col")])
        ra, r = mm_layer("mlp_up", l, h2, W1[l], out_dtypes=[BF16, BF16], b_blocks=True,
                         epilogue=lambda acc: (jnp.maximum(acc, 0.0), jnp.square(jnp.maximum(acc, 0.0))))
        m2, x2 = mm_layer("mlp_down", l, r, W2[l], out_dtypes=[F32, F32],
                          epilogue=lambda acc, xr, g: (acc, xr + g * acc), extras=[(x1, "tile"), (g_f, "col")])
        saved.append(dict(x=xl, h1=h1, proj=proj, qkv=qkv, a_tot=a_tot, cat=cat, mix=mix,
                          x1=x1, h2=h2, ra=ra, r=r, m2=m2))
        xl = x2

    dx, loss_part, d_final_g, dm2, dg_f = loss_head(xl, target, final_norm_g.reshape(1, D),
                                                    (saved[L - 1]["m2"], mod[L - 1, NMOD - 1]), "loss_head")

    p_in, p_out, p_w1, p_w2 = [None] * L, [None] * L, [None] * L, [None] * L
    w_in_grads = [None] * L
    vec_rows, d_norm_mix, d_norm_mlp = [None] * L, [None] * L, [None] * L
    dcw8, d_conv_b, d_gn, d_sw, d_sb = [None] * L, [None] * L, [None] * L, [None] * L, [None] * L
    late_grads = [None] * L
    for l in reversed(range(L)):
        sv = saved[l]
        sh_m, sc_m, g_m, sh_f, sc_f, g_f = [mod[l, k] for k in range(NMOD)]
        da = mm_layer("mlp_down_dgrad", l, dm2, W2[l], out_dtypes=[BF16], trans_b=True,
                      epilogue=lambda acc, rav: (acc * (2.0 * rav.astype(F32)),), extras=[(sv["ra"], "tile")])[0]
        dW2 = mm_layer("mlp_down_wgrad", l, sv["r"], dm2, out_dtypes=[BF16], trans_a=True)[0]
        dW1 = mm_layer("mlp_up_wgrad", l, sv["h2"], da, out_dtypes=[BF16], trans_a=True, out_blocks=True)[0]
        ties = []
        if l == 0:
            w1_head_start, w1_token = start_copies([dW1], me, "exchange_w1_0_start", False)
            ties = [(w1_token, "tie")]
        dh2 = mm_layer("mlp_up_dgrad", l, da, W1[l], out_dtypes=[F32], trans_b=True, b_blocks=True, extras=ties)[0]
        dx1, dsc_f, dsh_f, d_norm_mlp[l], dmix, dg_m = normmod_bwd(
            sv["x1"], dh2, dx, norm_mlp_g[l:l + 1], sc_f, f"norm_mlp_bwd{l}", gate_next=(sv["mix"], g_m))
        dcat = mm_layer("mix_dgrad", l, dmix, W_out[l], out_dtypes=[F32], trans_b=True)[0]
        dW_out = mm_layer("mix_wgrad", l, sv["cat"], dmix, out_dtypes=[BF16], trans_a=True)[0]
        pieces_w2, pieces_out = dW2.reshape(NDEV, DFF // NDEV, D), dW_out.reshape(NDEV, D // NDEV, D)
        ride, late = ([pieces_w2, pieces_out], dW1) if l == L - 1 else ([pieces_w2], pieces_out)
        dq, dk, dv, *arrived = attn_bwd(sv["qkv"], dcat, sv["a_tot"], f"attn_bwd{l}", comm=Exchange(ride))
        p_w2[l] = arrived[0]
        if l == L - 1:
            p_out[l] = arrived[1]
        late_grads[l], late_token = start_copies([late], me, f"exchange_late{l}_start", False, after=dq)
        dbg, dcg, dhc, dcw8[l], d_conv_b[l] = conv_bwd(sv["proj"], dcat, cw8[l], conv_b[l:l + 1], f"conv_bwd{l}")
        dus, dvs, d_gn[l], dsw, dbias = sg_bwd(sv["proj"], dcat, gmlp_norm_g[l:l + 1], spatial_w[l], sg_bias[l],
                                               f"sg_bwd{l}")
        d_sw[l] = dsw.astype(BF16)
        d_sb[l] = dbias.reshape(T, SG_HEADS, HD).sum(axis=2).T
        dproj = jnp.concatenate([dq, dk, dv, dbg, dcg, dhc, dus, dvs], axis=1).astype(BF16)
        dW_in = mm_layer("proj_wgrad", l, sv["h1"], dproj, out_dtypes=[BF16], trans_a=True,
                         extras=[(late_token, "tie")])[0]
        pieces = columns_to_shards(dW_in, f"w_in_grad_shards{l}")
        w_in_grads[l], token = start_copies([pieces], me, f"exchange_w_in{l}_start", False)
        dh1 = mm_layer("proj_dgrad", l, dproj, W_in[l], out_dtypes=[F32], trans_b=True, extras=[(token, "tie")])[0]
        below = (saved[l - 1]["m2"], mod[l - 1, NMOD - 1]) if l > 0 else None
        dx, dsc_m, dsh_m, d_norm_mix[l], *gated_below = normmod_bwd(
            sv["x"], dh1, dx1, tied(norm_mix_g[l:l + 1], token), sc_m, f"norm_mix_bwd{l}", gate_next=below)
        vec_rows[l] = [dsh_m, dsc_m, dg_m, dsh_f, dsc_f, dg_f, d_norm_mix[l], d_norm_mlp[l]]
        if l > 0:
            dm2, dg_f = gated_below

    grad_x = dx.reshape(1, S, D)

    g_w2, d_w2, nm_w2, nv_w2 = adamw_reduce(mlp_w2, p_w2, m_mlp_w2, v_mlp_w2, 256, "adamw_mlp_w2", tie=token)
    p_w1[L - 1] = finish_copies(late_grads[L - 1], d_w2, f"exchange_late{L - 1}_wait")[0]
    p_w1[0] = finish_copies(w1_head_start, d_w2, "exchange_w1_0_wait")[0]
    g_w1, d_w1, nm_w1, nv_w1 = adamw_reduce(mlp_w1, p_w1, m_mlp_w1, v_mlp_w1, 256, "adamw_mlp_w1", tie=token)

    vec_pack = jnp.concatenate([row for l in range(L) for row in vec_rows[l]]
                               + [d_final_g, loss_part, jnp.zeros((VEC_ROWS - VEC_FINAL_ROW - 2, D), F32)], axis=0)
    vec_pack, _ = lax.optimization_barrier((vec_pack, (d_w1, d_w2)))
    w256_pack = jnp.concatenate([blk for l in range(L) for blk in (
        dcw8[l], d_conv_b[l], d_gn[l], jnp.zeros((W256_ROWS_PER_LAYER - W256_GN - 1, CW), F32))], axis=0)
    vec_all, w256_all, sb_all, *sw_all = run_comm(
        Gather([vec_pack, w256_pack, jnp.concatenate(d_sb, axis=0)] + d_sw), "gather_small_grads")

    dmod_all = (vec_all[:, :VEC_FINAL_ROW].reshape(NDEV, L, VEC_ROWS_PER_LAYER, D)[:, :, :NMOD]
                .reshape(NDEV, L, NMOD * D))
    dmod_cols = lax.dynamic_slice(dmod_all, (0, 0, me * ADA_COLS), (NDEV, L, ADA_COLS)).transpose(1, 0, 2)
    g_ada_w = ada_bwd(c_act, dmod_cols, "ada_bwd")

    flat2 = lambda t: t.reshape(L * D, ADA_COLS)
    d_ada_w, nm_ada_w, nv_ada_w = [t.reshape(L, D, ADA_COLS) for t in adamw_plain(
        flat2(ada_w), flat2(g_ada_w), flat2(m_ada_w), flat2(v_ada_w), 256, "adamw_ada_w")]

    after = jnp.concatenate([t.reshape(-1)[:1] for t in (d_w1, d_w2, d_ada_w)])
    p_in = [finish_copies(w_in_grads[l], after, f"exchange_w_in{l}_wait")[0] for l in range(L)]
    p_out[0] = finish_copies(late_grads[0], after, "exchange_late0_wait")[0]
    g_w_in, d_w_in, nm_w_in, nv_w_in = adamw_reduce(w_in, p_in, m_w_in, v_w_in, 256, "adamw_w_in")
    g_w_out, d_w_out, nm_w_out, nv_w_out = adamw_reduce(w_out, p_out, m_w_out, v_w_out, 128, "adamw_w_out")

    as_row = lambda t: t.reshape(1, D)
    small_params = [(ada_b, m_ada_b, v_ada_b), (norm_mix_g, m_norm_mix_g, v_norm_mix_g),
                    (norm_mlp_g, m_norm_mlp_g, v_norm_mlp_g),
                    (as_row(final_norm_g), as_row(m_final_norm_g), as_row(v_final_norm_g)),
                    (conv_b, m_conv_b, v_conv_b), (gmlp_norm_g, m_gmlp_norm_g, v_gmlp_norm_g),
                    (spatial_w, m_spatial_w, v_spatial_w), (spatial_b, m_spatial_b, v_spatial_b)]
    updated, (loss_sum, taps_sum) = small_update(vec_all, w256_all, sb_all, sw_all, small_params, "small_update")
    loss = loss_sum[0, 0]
    u_ada_b, u_norm_mix, u_norm_mlp, u_final, u_conv_b, u_gn, u_sw, u_sb = updated
    u_final = [t.reshape(D) for t in u_final]
    g_conv_w = lax.dynamic_slice(taps_sum, (0, 0, me * conv_shard), (L, 3, conv_shard))
    flat_cw = lambda t: t.reshape(L * 3, conv_shard)
    u_conv_w = [g_conv_w] + [t.reshape(L, 3, conv_shard) for t in adamw_plain(
        flat_cw(conv_w), flat_cw(g_conv_w), flat_cw(m_conv_w), flat_cw(v_conv_w), L * 3, "adamw_conv_w")]
    small_sets = [u_ada_b, u_norm_mix, u_norm_mlp, u_conv_w, u_conv_b, u_gn, u_sw, u_sb, u_final]
    small_g, sd, snm, snv = [[u[k] for u in small_sets] for k in range(4)]

    def ordered(big, small):
        ada, win, wout, w1, w2 = big
        return [ada, small[0], small[1], small[2], win, small[3], small[4], small[5], small[6], small[7],
                wout, w1, w2, small[8]]

    grads = ordered([g_ada_w, g_w_in, g_w_out, g_w1, g_w2], small_g)
    deltas = ordered([d_ada_w, d_w_in, d_w_out, d_w1, d_w2], sd)
    new_m = ordered([nm_ada_w, nm_w_in, nm_w_out, nm_w1, nm_w2], snm)
    new_v = ordered([nv_ada_w, nv_w_in, nv_w_out, nv_w1, nv_w2], snv)
    return (loss, grad_x, *grads, *deltas, *new_m, *new_v)
```

```python
import functools
import math

import jax
import jax.numpy as jnp
from jax import lax
from jax.experimental import pallas as pl
from jax.experimental.pallas import tpu as pltpu

F32 = jnp.float32
BF16 = jnp.bfloat16
MESH = pl.DeviceIdType.MESH

S = 2048
D = 1024
L = 2
NDEV = 8
HD = 64
NH = 8
PROJ = 2816
DFF = 4096
NMOD = 6
EPS = 1e-6
T = 128
SG_HEADS = 4
LANES = 128
CW = 256
QKV = 3 * NH * HD
REST = PROJ - QKV

LR, B1, B2, AEPS, WD, STEP = 0.001, 0.9, 0.999, 1e-08, 0.01, 10
BC1 = 1.0 - B1 ** STEP
BC2 = 1.0 - B2 ** STEP

VMEM_LIMIT = 48 * 1024 * 1024

HBM_SPEC = pl.BlockSpec(memory_space=pltpu.HBM)


def _cparams(sem=None):
    return pltpu.CompilerParams(dimension_semantics=sem, vmem_limit_bytes=VMEM_LIMIT)


def _my_pos():
    return lax.axis_index("x"), lax.axis_index("y"), lax.axis_index("c")


def _lin(p):
    return 4 * p[0] + 2 * p[1] + p[2]


class Gather:
    def __init__(self, arrs):
        self.arrs = list(arrs)
        n = len(self.arrs)
        self.out_shape = [jax.ShapeDtypeStruct((NDEV,) + a.shape, a.dtype) for a in self.arrs]
        self.scratch = [pltpu.SemaphoreType.DMA((n, 7)), pltpu.SemaphoreType.DMA((n, 7)),
                        pltpu.SemaphoreType.DMA((n,))]

    def phases(self, ins, outs, sems):
        n = len(self.arrs)
        send_sems, recv_sems, local_sems = sems
        x, y, c = _my_pos()
        me, sibling = (x, y, c), (x, y, 1 - c)
        chips = [(1 - x, y), (x, 1 - y), (1 - x, 1 - y)]

        def copy(a, k, block, to, src=None):
            slot = outs[a].at[_lin(block)]
            return pltpu.make_async_remote_copy(
                src_ref=slot if src is None else src, dst_ref=slot,
                send_sem=send_sems.at[a, k], recv_sem=recv_sems.at[a, k],
                device_id=to, device_id_type=MESH)

        def mine(a):
            return pltpu.make_async_copy(ins[a], outs[a].at[_lin(me)], local_sems.at[a])

        def first(a):
            return [copy(a, 0, me, sibling, src=ins[a])] + [
                copy(a, 1 + j, me, (*chip, c), src=ins[a]) for j, chip in enumerate(chips)]

        def passed(a):
            return [copy(a, 4 + j, (*chip, c), sibling) for j, chip in enumerate(chips)]

        def start():
            for a in range(n):
                mine(a).start()
                for cp in first(a):
                    cp.start()

        def relay():
            for j, chip in enumerate(chips):
                for a in range(n):
                    copy(a, 1 + j, (*chip, c), me).wait_recv()
                    passed(a)[j].start()

        def finish():
            for a in range(n):
                copy(a, 0, sibling, me).wait_recv()
            for j, chip in enumerate(chips):
                for a in range(n):
                    copy(a, 4 + j, (*chip, 1 - c), me).wait_recv()
            for a in range(n):
                for cp in first(a) + passed(a):
                    cp.wait_send()
                mine(a).wait()

        return start, relay, finish


class Exchange:
    def __init__(self, arrs):
        self.arrs = list(arrs)
        n = len(self.arrs)
        self.out_shape = [jax.ShapeDtypeStruct(a.shape, a.dtype) for a in self.arrs]
        self.scratch = [pltpu.SemaphoreType.DMA((n, 7)), pltpu.SemaphoreType.DMA((n, 7)),
                        pltpu.SemaphoreType.DMA((n,))]

    def phases(self, ins, outs, sems):
        n = len(self.arrs)
        send_sems, recv_sems, local_sems = sems
        x, y, c = _my_pos()
        me = (x, y, c)

        def peer(mask):
            return (1 - x if mask & 4 else x, 1 - y if mask & 2 else y, 1 - c if mask & 1 else c)

        def copy(a, mask):
            return pltpu.make_async_remote_copy(
                src_ref=ins[a].at[_lin(peer(mask))], dst_ref=outs[a].at[_lin(me)],
                send_sem=send_sems.at[a, mask - 1], recv_sem=recv_sems.at[a, mask - 1],
                device_id=peer(mask), device_id_type=MESH)

        def arrival(a, mask):
            return pltpu.make_async_remote_copy(
                src_ref=ins[a].at[_lin(me)], dst_ref=outs[a].at[_lin(peer(mask))],
                send_sem=send_sems.at[a, mask - 1], recv_sem=recv_sems.at[a, mask - 1],
                device_id=peer(mask), device_id_type=MESH)

        def mine(a):
            return pltpu.make_async_copy(ins[a].at[_lin(me)], outs[a].at[_lin(me)], local_sems.at[a])

        def start():
            for a in range(n):
                mine(a).start()
            for mask in (4, 2, 6, 1, 5, 3, 7):
                for a in range(n):
                    copy(a, mask).start()

        def relay():
            pass

        def finish():
            for mask in range(1, 8):
                for a in range(n):
                    arrival(a, mask).wait_recv()
            for mask in range(1, 8):
                for a in range(n):
                    copy(a, mask).wait_send()
            for a in range(n):
                mine(a).wait()

        return start, relay, finish


def run_comm(plan, name):
    n = len(plan.arrs)

    def body(*refs):
        start, relay, finish = plan.phases(refs[:n], refs[n:2 * n], refs[2 * n:])
        start()
        relay()
        finish()

    outs = pl.pallas_call(
        body, name=name, out_shape=plan.out_shape,
        in_specs=[HBM_SPEC] * n, out_specs=[HBM_SPEC] * n, scratch_shapes=plan.scratch,
    )(*plan.arrs)
    return list(outs)


SEM_SPEC = pl.BlockSpec(memory_space=pltpu.SEMAPHORE)
DATAFLOW = pltpu.SideEffectType.DATAFLOW_SIDE_EFFECTING


def _peer_copies(src_ref, land_ref, send_sems, recv_sems, first, same_block):
    x, y, c = _my_pos()
    me = (x, y, c)
    sends, arrivals = [], []
    for mask in (4, 2, 6, 1, 5, 3, 7):
        peer = (1 - x if mask & 4 else x, 1 - y if mask & 2 else y, 1 - c if mask & 1 else c)
        sends.append(pltpu.make_async_remote_copy(
            src_ref=src_ref if same_block else src_ref.at[_lin(peer)], dst_ref=land_ref.at[_lin(me)],
            send_sem=send_sems.at[first + mask - 1], recv_sem=recv_sems.at[first + mask - 1], device_id=peer,
            device_id_type=MESH))
        arrivals.append(pltpu.make_async_remote_copy(
            src_ref=src_ref if same_block else src_ref.at[_lin(me)], dst_ref=land_ref.at[_lin(peer)],
            send_sem=send_sems.at[first + mask - 1], recv_sem=recv_sems.at[first + mask - 1], device_id=peer,
            device_id_type=MESH))
    return sends, arrivals


def start_copies(srcs, me, name, same_block, after=None):
    n = len(srcs)
    landings = []
    for src in srcs:
        own = src[None] if same_block else lax.dynamic_index_in_dim(src, me, axis=0, keepdims=True)
        landings.append(lax.dynamic_update_slice(lax.empty((NDEV,) + own.shape[1:], src.dtype), own,
                                                 (me,) + (0,) * (own.ndim - 1)))

    def body(*refs):
        send_sems, recv_sems = refs[-2 * n - 3], refs[-2 * n - 2]
        token = refs[-1]
        for k in range(n):
            sends, _ = _peer_copies(refs[2 * k], refs[2 * k + 1], send_sems, recv_sems, 7 * k, same_block)
            for cp in sends:
                cp.start()
        token[...] = jnp.zeros_like(token)

    hbm = lambda a: pltpu.HBM(a.shape, a.dtype)
    pairs = [a for pair in zip(srcs, landings) for a in pair]
    extra = [] if after is None else [after]
    sems = pltpu.SemaphoreType.DMA((7 * n,))
    send_sems, recv_sems, *thru, token = pl.pallas_call(
        body, name=name,
        out_shape=(sems, sems, *[hbm(a) for a in pairs], jax.ShapeDtypeStruct((8, LANES), F32)),
        in_specs=[HBM_SPEC] * (2 * n) + [pl.BlockSpec(memory_space=pl.ANY)] * len(extra),
        out_specs=(SEM_SPEC, SEM_SPEC, *[HBM_SPEC] * (2 * n), pl.BlockSpec(memory_space=pltpu.VMEM)),
        input_output_aliases={k: 2 + k for k in range(2 * n)},
        compiler_params=pltpu.CompilerParams(has_side_effects=DATAFLOW),
    )(*[pltpu.with_memory_space_constraint(a, pltpu.HBM) for a in pairs], *extra)
    return (send_sems, recv_sems, thru, same_block), token


def finish_copies(handle, after, name):
    send_sems, recv_sems, thru, same_block = handle
    n = len(thru) // 2

    def body(*refs):
        send_sems, recv_sems = refs[2 * n], refs[2 * n + 1]
        for k in range(n):
            sends, arrivals = _peer_copies(refs[2 * k], refs[2 * k + 1], send_sems, recv_sems, 7 * k, same_block)
            for cp in sends:
                cp.wait_send()
            for cp in arrivals:
                cp.wait_recv()

    hbm = lambda a: pltpu.HBM(a.shape, a.dtype)
    outs = pl.pallas_call(
        body, name=name, out_shape=tuple(hbm(a) for a in thru),
        in_specs=[HBM_SPEC] * (2 * n) + [SEM_SPEC, SEM_SPEC, pl.BlockSpec(memory_space=pl.ANY)],
        out_specs=tuple([HBM_SPEC] * (2 * n)), input_output_aliases={k: k for k in range(2 * n)},
        compiler_params=pltpu.CompilerParams(has_side_effects=DATAFLOW),
    )(*thru, send_sems, recv_sems, after)
    return [outs[2 * k + 1] for k in range(n)]


def tied(x, token):
    return x + token[0:1, 0:1].astype(x.dtype)


MM_TILES = {
    "proj_qkv": (S, 512), "proj_rest": (S, 256), "mix": (512, D), "mlp_up": (S, 512), "mlp_down": (1024, 256),
    "mlp_down_dgrad": (S, 1024), "mlp_down_wgrad": (1024, 1024), "mlp_up_wgrad": (1024, 512),
    "mlp_up_dgrad": (1024, 512), "mix_dgrad": (1024, 512), "mix_wgrad": (512, 1024),
    "proj_wgrad": (1024, PROJ // 2), "proj_dgrad": (1024, 512),
}


def mm_layer(kind, l, a, b, **kw):
    tm, tn = MM_TILES[kind]
    return mm(a, b, tm=tm, tn=tn, name=f"{kind}{l}", **kw)


def mm(a, b, *, tm, tn, out_dtypes, epilogue=None, extras=(), name, trans_a=False, trans_b=False,
       cols=None, b_blocks=False, out_blocks=False):
    if trans_a:
        kdim, m = a.shape
    else:
        m, kdim = a.shape
    shard = b.shape[-1] if b_blocks else None
    if b_blocks:
        full = (b.shape[1], NDEV * shard)
    else:
        full = b.shape
    first, ncols = cols if cols is not None else (0, full[0] if trans_b else full[1])
    assert full[1 if trans_b else 0] == kdim and m % tm == 0 and ncols % tn == 0 and first % tn == 0
    j0 = first // tn
    if trans_a:
        a_spec = pl.BlockSpec((kdim, tm), lambda i, j: (0, i))
    else:
        a_spec = pl.BlockSpec((tm, kdim), lambda i, j: (i, 0))
    if b_blocks and trans_b:
        b_spec = pl.BlockSpec((NDEV, tn, shard), lambda i, j: (0, j0 + j, 0))
    elif b_blocks:
        assert tn == shard
        b_spec = pl.BlockSpec((None, kdim, tn), lambda i, j: (j0 + j, 0, 0))
    elif trans_b:
        b_spec = pl.BlockSpec((tn, kdim), lambda i, j: (j0 + j, 0))
    else:
        b_spec = pl.BlockSpec((kdim, tn), lambda i, j: (0, j0 + j))
    if out_blocks:
        assert tn * NDEV == ncols
        out_spec = pl.BlockSpec((None, tm, tn), lambda i, j: (j, i, 0))
        out_dims = (NDEV, m, tn)
    else:
        out_spec = pl.BlockSpec((tm, tn), lambda i, j: (i, j))
        out_dims = (m, ncols)
    ex_specs = []
    for arr, kind in extras:
        if kind == "tile":
            ex_specs.append(pl.BlockSpec((tm, tn), lambda i, j: (i, j)))
        elif kind == "col":
            ex_specs.append(pl.BlockSpec((1, tn), lambda i, j: (0, j)))
        else:
            ex_specs.append(pl.BlockSpec(arr.shape, lambda i, j: (0, 0)))
    n_ex, n_out = len(extras), len(out_dtypes)
    used = [k for k, (_, kind) in enumerate(extras) if kind != "tie"]

    def body(a_ref, b_ref, *rest):
        ex_refs, out_refs = rest[:n_ex], rest[n_ex:]
        if trans_a:
            acc = lax.dot_general(a_ref[...], b_ref[...], (((0,), (0,)), ((), ())),
                                  preferred_element_type=F32)
        elif trans_b and b_blocks:
            acc = jnp.zeros((tm, tn), F32)
            for d in range(NDEV):
                acc = acc + lax.dot_general(a_ref[:, d * shard:(d + 1) * shard], b_ref[d],
                                            (((1,), (1,)), ((), ())), preferred_element_type=F32)
        elif trans_b:
            acc = lax.dot_general(a_ref[...], b_ref[...], (((1,), (1,)), ((), ())),
                                  preferred_element_type=F32)
        else:
            acc = jnp.dot(a_ref[...], b_ref[...], preferred_element_type=F32)
        outs = (acc,) if epilogue is None else epilogue(acc, *[ex_refs[k][...] for k in used])
        for o_ref, val in zip(out_refs, outs):
            o_ref[...] = val.astype(o_ref.dtype)

    outs = pl.pallas_call(
        body, name=name, grid=(m // tm, ncols // tn),
        in_specs=[a_spec, b_spec] + ex_specs,
        out_specs=[out_spec for _ in range(n_out)],
        out_shape=[jax.ShapeDtypeStruct(out_dims, dt) for dt in out_dtypes],
        compiler_params=_cparams(("parallel", "parallel")),
    )(a, b, *[arr for arr, _ in extras])
    return list(outs)


TR = 512

ROW_SPEC = pl.BlockSpec((TR, D), lambda i: (i, 0))
VEC_SPEC = pl.BlockSpec((1, D), lambda i: (0, 0))


def _residual_then_norm(acc, xr, gate, g, sc, sh):
    x_new = xr + gate * acc
    rstd = lax.rsqrt(jnp.mean(x_new * x_new, axis=-1, keepdims=True) + EPS)
    return acc, x_new, ((x_new * rstd) * g) * (1.0 + sc) + sh


def normmod_fwd(x, g, sc, sh, name):
    def body(x_ref, g_ref, sc_ref, sh_ref, o_ref):
        xv = x_ref[...]
        rstd = lax.rsqrt(jnp.mean(xv * xv, axis=-1, keepdims=True) + EPS)
        n = (xv * rstd) * g_ref[...]
        o_ref[...] = (n * (1.0 + sc_ref[...]) + sh_ref[...]).astype(o_ref.dtype)

    return pl.pallas_call(
        body, name=name, grid=(S // TR,),
        in_specs=[ROW_SPEC, VEC_SPEC, VEC_SPEC, VEC_SPEC], out_specs=ROW_SPEC,
        out_shape=jax.ShapeDtypeStruct((S, D), BF16),
        compiler_params=_cparams(("parallel",)),
    )(x, g, sc, sh)


def _gate_next(dxv, refs):
    br_ref, gate_ref, dbr_ref, dgate_ref = refs

    @pl.when(pl.program_id(0) == 0)
    def _():
        dgate_ref[...] = jnp.zeros_like(dgate_ref)

    dbr_ref[...] = (dxv * gate_ref[...]).astype(dbr_ref.dtype)
    dgate_ref[...] += jnp.sum(dxv * br_ref[...], axis=0, keepdims=True)


GATE_NEXT_IN = [ROW_SPEC, VEC_SPEC]
GATE_NEXT_OUT = [ROW_SPEC, VEC_SPEC]
GATE_NEXT_SHAPES = [jax.ShapeDtypeStruct((S, D), BF16), jax.ShapeDtypeStruct((1, D), F32)]


def normmod_bwd(x, dh, dres, g, sc, name, gate_next=None):
    nxt = 2 if gate_next else 0

    def body(x_ref, dh_ref, dres_ref, g_ref, sc_ref, *rest):
        nxt_in, (dx_ref, dsc_ref, dsh_ref, dg_ref), nxt_out = rest[:nxt], rest[nxt:nxt + 4], rest[nxt + 4:]

        @pl.when(pl.program_id(0) == 0)
        def _():
            dsc_ref[...] = jnp.zeros_like(dsc_ref)
            dsh_ref[...] = jnp.zeros_like(dsh_ref)
            dg_ref[...] = jnp.zeros_like(dg_ref)

        xv, dh = x_ref[...], dh_ref[...]
        gv = g_ref[...]
        rstd = lax.rsqrt(jnp.mean(xv * xv, axis=-1, keepdims=True) + EPS)
        xhat = xv * rstd
        dn = dh * (1.0 + sc_ref[...])
        dxhat = dn * gv
        dxv = dres_ref[...] + rstd * (dxhat - xhat * jnp.mean(dxhat * xhat, axis=-1, keepdims=True))
        dx_ref[...] = dxv
        dsc_ref[...] += jnp.sum(dh * (xhat * gv), axis=0, keepdims=True)
        dsh_ref[...] += jnp.sum(dh, axis=0, keepdims=True)
        dg_ref[...] += jnp.sum(dn * xhat, axis=0, keepdims=True)
        if gate_next:
            _gate_next(dxv, nxt_in + nxt_out)

    vec_out = jax.ShapeDtypeStruct((1, D), F32)
    on = bool(gate_next)
    return pl.pallas_call(
        body, name=name, grid=(S // TR,),
        in_specs=[ROW_SPEC, ROW_SPEC, ROW_SPEC, VEC_SPEC, VEC_SPEC] + GATE_NEXT_IN * on,
        out_specs=[ROW_SPEC, VEC_SPEC, VEC_SPEC, VEC_SPEC] + GATE_NEXT_OUT * on,
        out_shape=[jax.ShapeDtypeStruct((S, D), F32), vec_out, vec_out, vec_out] + GATE_NEXT_SHAPES * on,
        compiler_params=_cparams(("arbitrary",)),
    )(x, dh, dres, g, sc, *(gate_next or ()))


def loss_head(x, target, g, gate_next, name):
    def body(x_ref, t_ref, g_ref, br_ref, gate_ref, dx_ref, loss_ref, dg_ref, dbr_ref, dgate_ref):
        @pl.when(pl.program_id(0) == 0)
        def _():
            loss_ref[...] = jnp.zeros_like(loss_ref)
            dg_ref[...] = jnp.zeros_like(dg_ref)

        xv, gv = x_ref[...], g_ref[...]
        rstd = lax.rsqrt(jnp.mean(xv * xv, axis=-1, keepdims=True) + EPS)
        xhat = xv * rstd
        err = xhat * gv - t_ref[...]
        loss_ref[...] += jnp.sum(err * err) * (0.5 / D)
        dy = err * (1.0 / D)
        dg_ref[...] += jnp.sum(dy * xhat, axis=0, keepdims=True)
        dxhat = dy * gv
        dxv = rstd * (dxhat - xhat * jnp.mean(dxhat * xhat, axis=-1, keepdims=True))
        dx_ref[...] = dxv
        _gate_next(dxv, (br_ref, gate_ref, dbr_ref, dgate_ref))

    return pl.pallas_call(
        body, name=name, grid=(S // TR,),
        in_specs=[ROW_SPEC, ROW_SPEC, VEC_SPEC] + GATE_NEXT_IN,
        out_specs=[ROW_SPEC, VEC_SPEC, VEC_SPEC] + GATE_NEXT_OUT,
        out_shape=[jax.ShapeDtypeStruct((S, D), F32), jax.ShapeDtypeStruct((1, D), F32),
                   jax.ShapeDtypeStruct((1, D), F32)] + GATE_NEXT_SHAPES,
        compiler_params=_cparams(("arbitrary",)),
    )(x, target, g, *gate_next)


TQ = 512
RS = 128
NSUB = TQ // RS
TK = 128


def _dot_hilo(a, tri_twice):
    hi = a.astype(BF16)
    lo = (a - hi.astype(F32)).astype(BF16)
    return jnp.dot(jnp.concatenate([hi, lo], axis=1), tri_twice, preferred_element_type=F32)


def _log_stay(z):
    neg = -z
    return jnp.minimum(neg, 0.0) - jnp.log(1.0 + jnp.exp(jnp.minimum(z, neg)))


def _tri_and_ones(kind):
    row = jnp.bitwise_and(lax.broadcasted_iota(jnp.int32, (2 * TK, 2 * TK), 0), TK - 1)
    col = lax.broadcasted_iota(jnp.int32, (2 * TK, 2 * TK), 1)
    tri = {"after": row > col, "upto": row <= col, "before": row < col}[kind]
    return jnp.logical_or(col >= TK, tri).astype(BF16)


NPAIR = NH // 2
SCALE = HD ** -0.5


def _pair_specs(first_block):
    rows = pl.BlockSpec((TQ, LANES), lambda p, i: (i, first_block + p))
    whole = pl.BlockSpec((S, LANES), lambda p, i: (0, first_block + p))
    return rows, whole


Q_ROWS_SPEC, _ = _pair_specs(0)
_, K_ALL_SPEC = _pair_specs(NPAIR)
_, V_ALL_SPEC = _pair_specs(2 * NPAIR)
PAIR_ROWS_SPEC = pl.BlockSpec((TQ, LANES), lambda p, i: (i, p))
PAIR_ALL_SPEC = pl.BlockSpec((S, LANES), lambda p, i: (0, p))
PAIR_TOTAL_SPEC = pl.BlockSpec((2, TQ, TK), lambda p, i: (p, i, 0))


def _head_halves(x):
    first = lax.broadcasted_iota(jnp.int32, x.shape, 1) < HD
    zero = jnp.zeros_like(x)
    return jnp.where(first, x, zero), jnp.where(first, zero, x)


def _join_heads(a, b):
    return jnp.where(lax.broadcasted_iota(jnp.int32, a.shape, 1) < HD, a, b)


def _comm_hooks(comm, refs, n_in, n_out, n_scratch):
    nc = len(comm.arrs) if comm is not None else 0
    ins, cin = refs[:n_in], refs[n_in:n_in + nc]
    outs = refs[n_in + nc:n_in + nc + n_out]
    cout = refs[n_in + nc + n_out:n_in + 2 * nc + n_out]
    scratch = refs[n_in + 2 * nc + n_out:n_in + 2 * nc + n_out + n_scratch]
    sems = refs[n_in + 2 * nc + n_out + n_scratch:]
    phases = comm.phases(cin, cout, sems) if comm is not None else None
    return ins, outs, scratch, phases


def _with_comm(comm, in_specs, out_specs, out_shape, operands, scratch):
    if comm is None:
        return dict(in_specs=in_specs, out_specs=out_specs, out_shape=out_shape, scratch_shapes=scratch), operands
    nc = len(comm.arrs)
    return dict(in_specs=in_specs + [HBM_SPEC] * nc, out_specs=out_specs + [HBM_SPEC] * nc,
                out_shape=out_shape + comm.out_shape, scratch_shapes=scratch + comm.scratch), operands + comm.arrs


def attn_fwd(qkv, name, comm=None):
    n_steps = S // TQ

    def body(*refs):
        (q_ref, k_ref, v_ref), (o_ref, r_ref), (acc_ref, z_even, z_odd, w_ref), phases = _comm_hooks(
            comm, refs, 3, 2, 4)
        p = pl.program_id(0)
        i = pl.program_id(1)
        if phases is not None:
            pl.when(jnp.logical_and(p == 0, i == 0))(phases[0])
            pl.when(jnp.logical_and(p == NPAIR - 1, i == n_steps - 1))(phases[1])
        chains = [(sub, h) for sub in range(NSUB) for h in range(2)]
        q_sub = [_head_halves(q_ref[pl.ds(sub * RS, RS), :] * SCALE) for sub in range(NSUB)]
        after = _tri_and_ones("after")
        below_diagonal = (lax.broadcasted_iota(jnp.int32, (RS, TK), 1)
                          < lax.broadcasted_iota(jnp.int32, (RS, TK), 0))
        base = i * NSUB
        all_subs = list(range(NSUB))

        acc_ref[...] = jnp.zeros_like(acc_ref)
        r_ref[...] = jnp.zeros_like(r_ref)
        w_ref[...] = jnp.zeros_like(w_ref)

        def key_rows(block):
            return pl.ds(pl.multiple_of(block * TK, TK), TK)

        def store_scores(z_ref, block, subs):
            kb = k_ref[key_rows(block), :]
            for c, (sub, h) in enumerate(chains):
                if sub in subs:
                    z_ref[c] = lax.dot_general(q_sub[sub][h], kb, (((1,), (1,)), ((), ())),
                                               preferred_element_type=F32)

        def add_weighted_values(block, subs):
            vb = v_ref[key_rows(block), :]
            for sub in subs:
                acc_ref[pl.ds(sub * RS, RS), :] += _join_heads(*[
                    jnp.dot(w_ref[2 * sub + h], vb, preferred_element_type=F32) for h in range(2)])

        def step(block, z_ref, z_next_ref, subs, diagonal_sub, prev_subs, next_subs):
            if prev_subs:
                add_weighted_values(block + 1, prev_subs)
            if next_subs:
                store_scores(z_next_ref, jnp.maximum(block - 1, 0), next_subs)
            active = [(c, sub, h) for c, (sub, h) in enumerate(chains) if sub in subs]
            ls, sums = {}, {}
            for c, sub, h in active:
                ls[c] = _log_stay(z_ref[c])
                sums[c] = _dot_hilo(jnp.where(below_diagonal, ls[c], 0.0) if sub == diagonal_sub else ls[c], after)
            for c, sub, h in active:
                rows = pl.ds(sub * RS, RS)
                later = r_ref[h, rows, :]
                w = jnp.exp(z_ref[c] + ls[c] + (sums[c][:, :TK] + later))
                if sub == diagonal_sub:
                    w = jnp.where(below_diagonal, w, 0.0)
                w_ref[c] = w.astype(BF16)
                r_ref[h, rows, :] = later + sums[c][:, TK:]

        store_scores(z_even, base + NSUB - 1, [NSUB - 1])
        buffers = (z_even, z_odd)
        for j in reversed(range(NSUB)):
            subs = all_subs[j:]
            step(base + j, buffers[0], buffers[1], subs, j, all_subs[j + 1:], all_subs[j - 1:] if j else all_subs)
            buffers = buffers[::-1]
        assert buffers[0] is z_even

        @pl.loop(0, base // 2)
        def _(pair):
            block = base - 1 - 2 * pair
            step(block, z_even, z_odd, all_subs, None, all_subs, all_subs)
            step(block - 1, z_odd, z_even, all_subs, None, all_subs, all_subs)

        add_weighted_values(0, all_subs)
        o_ref[...] = acc_ref[...].astype(o_ref.dtype)
        if phases is not None:
            pl.when(jnp.logical_and(p == NPAIR - 1, i == n_steps - 1))(phases[2])

    kwargs, operands = _with_comm(
        comm, [Q_ROWS_SPEC, K_ALL_SPEC, V_ALL_SPEC], [PAIR_ROWS_SPEC, PAIR_TOTAL_SPEC],
        [jax.ShapeDtypeStruct((S, NH * HD), BF16), jax.ShapeDtypeStruct((NH, S, TK), F32)], [qkv, qkv, qkv],
        [pltpu.VMEM((TQ, LANES), F32), pltpu.VMEM((2 * NSUB, RS, TK), F32), pltpu.VMEM((2 * NSUB, RS, TK), F32),
         pltpu.VMEM((2 * NSUB, RS, TK), BF16)])
    return pl.pallas_call(
        body, name=name, grid=(NPAIR, n_steps),
        compiler_params=_cparams(("arbitrary", "arbitrary")), **kwargs,
    )(*operands)


def attn_bwd(qkv, dout, totals, name, comm=None):
    n_steps = S // TQ

    def body(*refs):
        ((q_ref, k_ref, v_ref, do_ref, r_ref), (dq_out, dk_out, dv_out),
         (z_even, z_odd, dw_even, dw_odd, before_ref, dbefore_ref, dz_ref, w_ref, dq_ref, dk_ref, dv_ref),
         phases) = _comm_hooks(comm, refs, 5, 3, 11)
        p = pl.program_id(0)
        i = pl.program_id(1)
        if phases is not None:
            pl.when(jnp.logical_and(p == 0, i == 0))(phases[0])
            pl.when(jnp.logical_and(p == NPAIR - 1, i == n_steps - 2))(phases[1])

        @pl.when(i == 0)
        def _():
            dk_ref[...] = jnp.zeros_like(dk_ref)
            dv_ref[...] = jnp.zeros_like(dv_ref)

        chains = [(sub, h) for sub in range(NSUB) for h in range(2)]
        nch = len(chains)
        qb = q_ref[...]
        dob = do_ref[...].astype(BF16)
        q_sub = [_head_halves(qb[sub * RS:(sub + 1) * RS] * SCALE) for sub in range(NSUB)]
        do_sub = [_head_halves(dob[sub * RS:(sub + 1) * RS]) for sub in range(NSUB)]
        upto = _tri_and_ones("upto")
        before_tri = _tri_and_ones("before")
        below_diagonal = (lax.broadcasted_iota(jnp.int32, (RS, TK), 1)
                          < lax.broadcasted_iota(jnp.int32, (RS, TK), 0))
        contract_lanes = (((1,), (1,)), ((), ()))
        contract_rows = (((0,), (0,)), ((), ()))
        base = i * NSUB
        all_subs = list(range(NSUB))

        def key_rows(block):
            return pl.ds(pl.multiple_of(block * TK, TK), TK)

        def store_products(bufs, block, subs):
            z_ref, dw_ref = bufs
            kb = k_ref[key_rows(block), :]
            vb = v_ref[key_rows(block), :]
            for c, (sub, h) in enumerate(chains):
                if sub in subs:
                    z_ref[c] = lax.dot_general(q_sub[sub][h], kb, contract_lanes, preferred_element_type=F32)
                    dw_ref[c] = lax.dot_general(do_sub[sub][h], vb, contract_lanes, preferred_element_type=F32)

        def add_gradients(block, subs):
            kb = k_ref[key_rows(block), :]
            for sub in subs:
                rows = pl.ds(sub * RS, RS)
                dq_ref[rows, :] += _join_heads(*[jnp.dot(dz_ref[h, rows, :], kb, preferred_element_type=F32)
                                                 for h in range(2)])
            dk_ref[key_rows(block), :] += _join_heads(*[
                lax.dot_general(dz_ref[h], qb, contract_rows, preferred_element_type=F32) for h in range(2)])
            dv_ref[key_rows(block), :] += _join_heads(*[
                lax.dot_general(w_ref[h], dob, contract_rows, preferred_element_type=F32) for h in range(2)])

        for ref in (dq_ref, before_ref, dbefore_ref, dz_ref, w_ref):
            ref[...] = jnp.zeros_like(ref)
        even, odd = (z_even, dw_even), (z_odd, dw_odd)
        store_products(even, 0, all_subs)

        def step(block, bufs, next_bufs, subs, diagonal_sub, prev_subs, next_subs):
            z_ref, dw_ref = bufs
            add_gradients(jnp.maximum(block - 1, 0), prev_subs)
            for sub in prev_subs:
                if sub not in subs:
                    dz_ref[:, pl.ds(sub * RS, RS), :] = jnp.zeros((2, RS, TK), BF16)
                    w_ref[:, pl.ds(sub * RS, RS), :] = jnp.zeros((2, RS, TK), BF16)
            if next_subs:
                store_products(next_bufs, block + 1, next_subs)
            active = [(c, sub, h) for c, (sub, h) in enumerate(chains) if sub in subs]
            ls, sums, dl, dsums = {}, {}, {}, {}
            for c, sub, h in active:
                ls[c] = _log_stay(z_ref[c])
                sums[c] = _dot_hilo(jnp.where(below_diagonal, ls[c], 0.0) if sub == diagonal_sub else ls[c], upto)
            for c, sub, h in active:
                rows = pl.ds(sub * RS, RS)
                before = before_ref[c]
                log_after = r_ref[h, rows, :] - (sums[c][:, :TK] + before)
                w = jnp.exp((z_ref[c] + ls[c]) + log_after)
                if sub == diagonal_sub:
                    w = jnp.where(below_diagonal, w, 0.0)
                dl[c] = dw_ref[c] * w
                dsums[c] = _dot_hilo(dl[c], before_tri)
                w_ref[h, rows, :] = w.astype(BF16)
                before_ref[c] = before + sums[c][:, TK:]
            for c, sub, h in active:
                rows = pl.ds(sub * RS, RS)
                dbefore = dbefore_ref[c]
                beta = jnp.exp(z_ref[c] + ls[c])
                if sub == diagonal_sub:
                    beta = jnp.where(below_diagonal, beta, 0.0)
                dstay = dsums[c][:, :TK] + dbefore
                dz_ref[h, rows, :] = ((dl[c] - beta * (dl[c] + dstay)) * SCALE).astype(BF16)
                dbefore_ref[c] = dbefore + dsums[c][:, TK:]

        @pl.loop(0, base // 2)
        def _(pair):
            step(2 * pair, even, odd, all_subs, None, all_subs, all_subs)
            step(2 * pair + 1, odd, even, all_subs, None, all_subs, all_subs)

        bufs = (even, odd)
        for j in range(NSUB):
            step(base + j, bufs[0], bufs[1], all_subs[j:], j, all_subs[j - 1:] if j else all_subs, all_subs[j + 1:])
            bufs = bufs[::-1]

        add_gradients(base + NSUB - 1, all_subs[NSUB - 1:])
        dq_out[...] = dq_ref[...].astype(dq_out.dtype)

        @pl.when(i == n_steps - 1)
        def _():
            dk_out[...] = dk_ref[...].astype(dk_out.dtype)
            dv_out[...] = dv_ref[...].astype(dv_out.dtype)

        if phases is not None:
            pl.when(jnp.logical_and(p == NPAIR - 1, i == n_steps - 1))(phases[2])

    full = jax.ShapeDtypeStruct((S, NH * HD), BF16)
    kwargs, operands = _with_comm(
        comm, [Q_ROWS_SPEC, K_ALL_SPEC, V_ALL_SPEC, PAIR_ROWS_SPEC, PAIR_TOTAL_SPEC],
        [PAIR_ROWS_SPEC, PAIR_ALL_SPEC, PAIR_ALL_SPEC], [full, full, full], [qkv, qkv, qkv, dout, totals],
        [pltpu.VMEM((2 * NSUB, RS, TK), F32)] * 6 + [pltpu.VMEM((2, TQ, TK), BF16)] * 2
        + [pltpu.VMEM((TQ, LANES), F32), pltpu.VMEM((S, LANES), F32), pltpu.VMEM((S, LANES), F32)])
    return pl.pallas_call(
        body, name=name, grid=(NPAIR, n_steps),
        compiler_params=_cparams(("arbitrary", "arbitrary")), **kwargs,
    )(*operands)


def _proj_cols(first_col):
    base = first_col // LANES
    return pl.BlockSpec((S, LANES), lambda j: (0, base + j))


CONV_OUT_SPEC = pl.BlockSpec((S, LANES), lambda j: (0, j))
CONV_DOUT_SPEC = pl.BlockSpec((S, LANES), lambda j: (0, (NH * HD) // LANES + j))
CONV_W_SPEC = pl.BlockSpec((8, LANES), lambda j: (0, j))
CONV_B_SPEC = pl.BlockSpec((1, LANES), lambda j: (0, j))


def _shift_down(u, n):
    rows = lax.broadcasted_iota(jnp.int32, u.shape, 0)
    return jnp.where(rows >= n, pltpu.roll(u, n, 0), 0.0)


def _shift_up(u, n):
    rows = lax.broadcasted_iota(jnp.int32, u.shape, 0)
    return jnp.where(rows < S - n, pltpu.roll(u, S - n, 0), 0.0)


def conv_fwd(proj, cw8, cb, name):
    def body(bg_ref, cg_ref, hc_ref, w_ref, b_ref, o_ref):
        u = cg_ref[...] * hc_ref[...]
        w = w_ref[...]
        y = w[0:1, :] * _shift_down(u, 2) + w[1:2, :] * _shift_down(u, 1) + w[2:3, :] * u + b_ref[...]
        o_ref[...] = bg_ref[...] * y

    return pl.pallas_call(
        body, name=name, grid=(CW // LANES,),
        in_specs=[_proj_cols(0), _proj_cols(CW), _proj_cols(2 * CW), CONV_W_SPEC, CONV_B_SPEC],
        out_specs=CONV_OUT_SPEC, out_shape=jax.ShapeDtypeStruct((S, CW), F32),
        compiler_params=_cparams(("parallel",)),
    )(proj, proj, proj, cw8, cb)


def conv_bwd(proj, dout, cw8, cb, name):
    def body(bg_ref, cg_ref, hc_ref, do_ref, w_ref, b_ref, dbg_ref, dcg_ref, dhc_ref, dw_ref, db_ref):
        cg, hc, do = cg_ref[...], hc_ref[...], do_ref[...]
        w = w_ref[...]
        u = cg * hc
        u1, u2 = _shift_down(u, 1), _shift_down(u, 2)
        y = w[0:1, :] * u2 + w[1:2, :] * u1 + w[2:3, :] * u + b_ref[...]
        dbg_ref[...] = (do * y).astype(dbg_ref.dtype)
        dy = do * bg_ref[...]
        db_ref[...] = jnp.sum(dy, axis=0, keepdims=True)
        dw_ref[...] = jnp.concatenate(
            [jnp.sum(dy * u2, axis=0, keepdims=True), jnp.sum(dy * u1, axis=0, keepdims=True),
             jnp.sum(dy * u, axis=0, keepdims=True), jnp.zeros((5, LANES), F32)], axis=0)
        du = w[2:3, :] * dy + w[1:2, :] * _shift_up(dy, 1) + w[0:1, :] * _shift_up(dy, 2)
        dcg_ref[...] = (du * hc).astype(dcg_ref.dtype)
        dhc_ref[...] = (du * cg).astype(dhc_ref.dtype)

    full = jax.ShapeDtypeStruct((S, CW), BF16)
    return pl.pallas_call(
        body, name=name, grid=(CW // LANES,),
        in_specs=[_proj_cols(0), _proj_cols(CW), _proj_cols(2 * CW), CONV_DOUT_SPEC, CONV_W_SPEC, CONV_B_SPEC],
        out_specs=[CONV_OUT_SPEC, CONV_OUT_SPEC, CONV_OUT_SPEC, CONV_W_SPEC, CONV_B_SPEC],
        out_shape=[full, full, full, jax.ShapeDtypeStruct((8, CW), F32), jax.ShapeDtypeStruct((1, CW), F32)],
        compiler_params=_cparams(("parallel",)),
    )(proj, proj, proj, dout, cw8, cb)


GELU_K = math.sqrt(2.0 / math.pi)
GELU_C = 0.044715


def _gelu(x):
    return 0.5 * x * (1.0 + jnp.tanh(GELU_K * (x + GELU_C * (x * x * x))))


def _gelu_grad(x):
    t = jnp.tanh(GELU_K * (x + GELU_C * (x * x * x)))
    return 0.5 * (1.0 + t) + 0.5 * x * (1.0 - t * t) * (GELU_K * (1.0 + 3.0 * GELU_C * (x * x)))


def _sg_masks():
    row = lax.broadcasted_iota(jnp.int32, (T, T), 0)
    col = lax.broadcasted_iota(jnp.int32, (T, T), 1)
    causal = jnp.right_shift(row, 6) >= jnp.right_shift(col, 6)
    head_of_col = jnp.right_shift(lax.broadcasted_iota(jnp.int32, (T, CW), 1), 6)
    return causal, head_of_col


def _sg_weights(sw_ref, causal):
    return [jnp.where(causal, sw_ref[h], 0.0).astype(BF16) for h in range(SG_HEADS)]


def _sg_mixed(vnb, weights, bias, head_of_col):
    mixed = bias
    for h in range(SG_HEADS):
        mh = jnp.dot(weights[h], vnb, preferred_element_type=F32)
        mixed = mixed + jnp.where(head_of_col == h, mh, 0.0)
    return mixed


SG_WINDOWS = 4
SG_ROWS = SG_WINDOWS * T
SG_U_SPEC = pl.BlockSpec((SG_ROWS, CW), lambda n: (n, 3))
SG_V_SPEC = pl.BlockSpec((SG_ROWS, CW), lambda n: (n, 4))
SG_ROW_SPEC = pl.BlockSpec((SG_ROWS, CW), lambda n: (n, 0))
SG_DOUT_SPEC = pl.BlockSpec((SG_ROWS, CW), lambda n: (n, 3))
SG_G_SPEC = pl.BlockSpec((1, CW), lambda n: (0, 0))
SG_W_SPEC = pl.BlockSpec((SG_HEADS, T, T), lambda n: (0, 0, 0))
SG_BIAS_SPEC = pl.BlockSpec((T, CW), lambda n: (0, 0))


def sg_fwd(proj, gn, sw, bias, name):
    def body(u_ref, v_ref, g_ref, sw_ref, bias_ref, o_ref):
        causal, head_of_col = _sg_masks()
        weights = _sg_weights(sw_ref, causal)
        for wdw in range(SG_WINDOWS):
            rows = pl.ds(wdw * T, T)
            gv = _gelu(v_ref[rows, :])
            rstd = lax.rsqrt(jnp.mean(gv * gv, axis=-1, keepdims=True) + EPS)
            vnb = ((gv * rstd) * g_ref[...]).astype(BF16)
            mixed = _sg_mixed(vnb, weights, bias_ref[...], head_of_col)
            o_ref[rows, :] = _gelu(u_ref[rows, :]) * mixed

    return pl.pallas_call(
        body, name=name, grid=(S // SG_ROWS,),
        in_specs=[SG_U_SPEC, SG_V_SPEC, SG_G_SPEC, SG_W_SPEC, SG_BIAS_SPEC],
        out_specs=SG_ROW_SPEC, out_shape=jax.ShapeDtypeStruct((S, CW), F32),
        compiler_params=_cparams(("parallel",)),
    )(proj, proj, gn, sw, bias)


def sg_bwd(proj, dout, gn, sw, bias, name):
    def body(u_ref, v_ref, do_ref, g_ref, sw_ref, bias_ref, du_ref, dv_ref, dg_ref, dsw_ref, dbias_ref):
        @pl.when(pl.program_id(0) == 0)
        def _():
            dg_ref[...] = jnp.zeros_like(dg_ref)
            dsw_ref[...] = jnp.zeros_like(dsw_ref)
            dbias_ref[...] = jnp.zeros_like(dbias_ref)

        causal, head_of_col = _sg_masks()
        weights = _sg_weights(sw_ref, causal)
        gnv = g_ref[...]
        for wdw in range(SG_WINDOWS):
            rows = pl.ds(wdw * T, T)
            uv, vv, do = u_ref[rows, :], v_ref[rows, :], do_ref[rows, :]
            gv = _gelu(vv)
            rstd = lax.rsqrt(jnp.mean(gv * gv, axis=-1, keepdims=True) + EPS)
            xhat = gv * rstd
            vnb = (xhat * gnv).astype(BF16)
            mixed = _sg_mixed(vnb, weights, bias_ref[...], head_of_col)
            du_ref[rows, :] = ((do * mixed) * _gelu_grad(uv)).astype(du_ref.dtype)
            dmix = do * _gelu(uv)
            dbias_ref[...] += dmix
            dmixb = dmix.astype(BF16)
            dvn = jnp.zeros((T, CW), F32)
            for h in range(SG_HEADS):
                dvh = lax.dot_general(weights[h], dmixb, (((0,), (0,)), ((), ())), preferred_element_type=F32)
                dvn = dvn + jnp.where(head_of_col == h, dvh, 0.0)
                dmh = jnp.where(head_of_col == h, dmixb, jnp.zeros_like(dmixb))
                dwh = lax.dot_general(dmh, vnb, (((1,), (1,)), ((), ())), preferred_element_type=F32)
                dsw_ref[h] += jnp.where(causal, dwh, 0.0)
            dg_ref[...] += jnp.sum(dvn * xhat, axis=0, keepdims=True)
            dxhat = dvn * gnv
            dgv = rstd * (dxhat - xhat * jnp.mean(dxhat * xhat, axis=-1, keepdims=True))
            dv_ref[rows, :] = (dgv * _gelu_grad(vv)).astype(dv_ref.dtype)

    full = jax.ShapeDtypeStruct((S, CW), BF16)
    return pl.pallas_call(
        body, name=name, grid=(S // SG_ROWS,),
        in_specs=[SG_U_SPEC, SG_V_SPEC, SG_DOUT_SPEC, SG_G_SPEC, SG_W_SPEC, SG_BIAS_SPEC],
        out_specs=[SG_ROW_SPEC, SG_ROW_SPEC, SG_G_SPEC, SG_W_SPEC, SG_BIAS_SPEC],
        out_shape=[full, full, jax.ShapeDtypeStruct((1, CW), F32),
                   jax.ShapeDtypeStruct((SG_HEADS, T, T), F32), jax.ShapeDtypeStruct((T, CW), F32)],
        compiler_params=_cparams(("arbitrary",)),
    )(proj, proj, dout, gn, sw, bias)


ADA_COLS = NMOD * D // NDEV


def ada_fwd(c_all, ada_w, ada_b_mine, name):
    def body(c_ref, w_ref, b_ref, o_ref, ca_ref):
        cv = c_ref[...]
        ca = cv * (1.0 / (1.0 + jnp.exp(-cv)))
        ca_ref[...] = ca
        cab = ca.astype(BF16)
        for l in range(L):
            o_ref[l] = jnp.dot(cab, w_ref[l].astype(BF16), preferred_element_type=F32) + b_ref[l]

    return pl.pallas_call(
        body, name=name,
        out_shape=[jax.ShapeDtypeStruct((L, NDEV, ADA_COLS), F32), jax.ShapeDtypeStruct((NDEV, D), F32)],
        compiler_params=_cparams(),
    )(c_all, ada_w, ada_b_mine)


def ada_bwd(ca, dmod_cols, name):
    def body(ca_ref, dm_ref, o_ref):
        cab = ca_ref[...].astype(BF16)
        for l in range(L):
            o_ref[l] = lax.dot_general(cab, dm_ref[l].astype(BF16), (((0,), (0,)), ((), ())),
                                       preferred_element_type=F32)

    return pl.pallas_call(
        body, name=name, out_shape=jax.ShapeDtypeStruct((L, D, ADA_COLS), F32),
        compiler_params=_cparams(),
    )(ca, dmod_cols)


def _adamw(w, g, m, v):
    m = B1 * m + (1.0 - B1) * g
    v = B2 * v + (1.0 - B2) * (g * g)
    m_hat = m / BC1
    v_hat = v / BC2
    delta = -LR * (m_hat / (jnp.sqrt(v_hat) + AEPS) + WD * w)
    return delta, m, v


VEC_ROWS_PER_LAYER = 8
VEC_FINAL_ROW = L * VEC_ROWS_PER_LAYER
VEC_ROWS = VEC_FINAL_ROW + 8
W256_TAPS, W256_CONV_B, W256_GN = 0, 8, 9
W256_ROWS_PER_LAYER = 16


def small_update(vec_all, w256_all, sb_all, sw_all, params, name):
    n_par = len(params)

    def body(*refs):
        vec_ref, w256_ref, sb_ref = refs[:3]
        sw_refs = refs[3:3 + L]
        par_refs = [refs[3 + L + 3 * k:3 + L + 3 * k + 3] for k in range(n_par)]
        out = refs[3 + L + 3 * n_par:]
        out_par = [out[4 * k:4 * k + 4] for k in range(n_par)]
        loss_ref, taps_ref = out[4 * n_par:]

        def total(ref, idx):
            acc = ref[(0,) + idx].astype(F32)
            for d in range(1, NDEV):
                acc = acc + ref[(d,) + idx].astype(F32)
            return acc

        def update(k, region, g):
            w_ref, m_ref, v_ref = par_refs[k]
            g_ref, d_ref, nm_ref, nv_ref = out_par[k]
            delta, nm, nv = _adamw(w_ref[region], g, m_ref[region], v_ref[region])
            g_ref[region] = g
            d_ref[region] = delta
            nm_ref[region] = nm
            nv_ref[region] = nv

        for l in range(L):
            base = l * VEC_ROWS_PER_LAYER
            for k in range(NMOD):
                update(0, (slice(l, l + 1), slice(k * D, (k + 1) * D)), total(vec_ref, (slice(base + k, base + k + 1),)))
            update(1, (slice(l, l + 1),), total(vec_ref, (slice(base + 6, base + 7),)))
            update(2, (slice(l, l + 1),), total(vec_ref, (slice(base + 7, base + 8),)))
            wbase = l * W256_ROWS_PER_LAYER
            update(4, (slice(l, l + 1),), total(w256_ref, (slice(wbase + W256_CONV_B, wbase + W256_CONV_B + 1),)))
            update(5, (slice(l, l + 1),), total(w256_ref, (slice(wbase + W256_GN, wbase + W256_GN + 1),)))
            update(6, (l,), total(sw_refs[l], ()))
            update(7, (l,), total(sb_ref, (slice(l * SG_HEADS, (l + 1) * SG_HEADS),)))
            taps_ref[l] = total(w256_ref, (slice(wbase + W256_TAPS, wbase + W256_TAPS + 8),))
        update(3, (slice(0, 1),), total(vec_ref, (slice(VEC_FINAL_ROW, VEC_FINAL_ROW + 1),)))
        loss_ref[...] = total(vec_ref, (slice(VEC_FINAL_ROW + 1, VEC_FINAL_ROW + 2), slice(0, LANES)))

    out_shape = []
    for w, _, _ in params:
        out_shape += [jax.ShapeDtypeStruct(w.shape, F32)] * 4
    out_shape += [jax.ShapeDtypeStruct((1, LANES), F32), jax.ShapeDtypeStruct((L, 8, CW), F32)]
    outs = pl.pallas_call(body, name=name, out_shape=out_shape, compiler_params=_cparams())(
        vec_all, w256_all, sb_all, *sw_all, *[a for p in params for a in p])
    return [outs[4 * k:4 * k + 4] for k in range(n_par)], outs[4 * n_par:]


def adamw_plain(w, g, m, v, tr, name):
    rows, cols = w.shape
    spec = pl.BlockSpec((tr, cols), lambda i: (i, 0))

    def body(w_ref, g_ref, m_ref, v_ref, d_ref, nm_ref, nv_ref):
        delta, nm, nv = _adamw(w_ref[...], g_ref[...], m_ref[...], v_ref[...])
        d_ref[...] = delta
        nm_ref[...] = nm
        nv_ref[...] = nv

    shp = jax.ShapeDtypeStruct((rows, cols), F32)
    return pl.pallas_call(
        body, name=name, grid=(rows // tr,), in_specs=[spec] * 4, out_specs=[spec] * 3,
        out_shape=[shp, shp, shp], compiler_params=_cparams(("parallel",)),
    )(w, g, m, v)


def adamw_reduce(w, parts, m, v, tr, name, tie=None):
    _, rows, cols = w.shape
    spec = pl.BlockSpec((None, tr, cols), lambda l, i: (l, i, 0))
    pspecs = [pl.BlockSpec((NDEV, tr, cols), lambda l, i, k=k: (0, jnp.where(l == k, i, 0), 0)) for k in range(L)]

    ties = [] if tie is None else [tie]

    def body(w_ref, p0_ref, p1_ref, m_ref, v_ref, *rest):
        g_ref, d_ref, nm_ref, nv_ref = rest[len(ties):]
        first_layer = pl.program_id(0) == 0
        g = jnp.zeros((tr, cols), F32)
        for d in range(NDEV):
            g = g + jnp.where(first_layer, p0_ref[d], p1_ref[d]).astype(F32)
        delta, nm, nv = _adamw(w_ref[...], g, m_ref[...], v_ref[...])
        g_ref[...] = g
        d_ref[...] = delta
        nm_ref[...] = nm
        nv_ref[...] = nv

    shp = jax.ShapeDtypeStruct(w.shape, F32)
    return pl.pallas_call(
        body, name=name, grid=(L, rows // tr),
        in_specs=[spec] + pspecs + [spec, spec] + [pl.BlockSpec(t.shape, lambda l, i: (0, 0)) for t in ties],
        out_specs=[spec] * 4, out_shape=[shp] * 4, compiler_params=_cparams(("parallel", "parallel")),
    )(w, *parts, m, v, *ties)


SHARD_IN = PROJ // NDEV


def shards_to_columns(shards, name):
    tr = 256

    def body(i_ref, o_ref):
        for d in range(NDEV):
            o_ref[:, d * SHARD_IN:(d + 1) * SHARD_IN] = i_ref[d]

    return pl.pallas_call(
        body, name=name, grid=(D // tr,),
        in_specs=[pl.BlockSpec((NDEV, tr, SHARD_IN), lambda i: (0, i, 0))],
        out_specs=pl.BlockSpec((tr, PROJ), lambda i: (i, 0)),
        out_shape=jax.ShapeDtypeStruct((D, PROJ), shards.dtype), compiler_params=_cparams(("parallel",)),
    )(shards)


def columns_to_shards(mat, name):
    tr = 256

    def body(i_ref, o_ref):
        for d in range(NDEV):
            o_ref[d] = i_ref[:, d * SHARD_IN:(d + 1) * SHARD_IN]

    return pl.pallas_call(
        body, name=name, grid=(D // tr,),
        in_specs=[pl.BlockSpec((tr, PROJ), lambda i: (i, 0))],
        out_specs=pl.BlockSpec((NDEV, tr, SHARD_IN), lambda i: (0, i, 0)),
        out_shape=jax.ShapeDtypeStruct((NDEV, D, SHARD_IN), mat.dtype), compiler_params=_cparams(("parallel",)),
    )(mat)


def _pad_rows(flat, rows):
    return jnp.pad(flat, (0, rows * LANES - flat.shape[0])).reshape(rows, LANES)


def kernel(x, c, ada_w, ada_b, norm_mix_g, norm_mlp_g, w_in, conv_w, conv_b, gmlp_norm_g, spatial_w, spatial_b, w_out, mlp_w1, mlp_w2, final_norm_g, loss_target, m_ada_w, m_ada_b, m_norm_mix_g, m_norm_mlp_g, m_w_in, m_conv_w, m_conv_b, m_gmlp_norm_g, m_spatial_w, m_spatial_b, m_w_out, m_mlp_w1, m_mlp_w2, m_final_norm_g, v_ada_w, v_ada_b, v_norm_mix_g, v_norm_mlp_g, v_w_in, v_conv_w, v_conv_b, v_gmlp_norm_g, v_spatial_w, v_spatial_b, v_w_out, v_mlp_w1, v_mlp_w2, v_final_norm_g):
    me = _lin(_my_pos())
    x0 = x[0]
    target = loss_target[0]
    conv_shard = conv_w.shape[-1]

    w_in_b, w_out_b, w1_b, w2_b = [w.astype(BF16) for w in (w_in, w_out, mlp_w1, mlp_w2)]
    pack0 = _pad_rows(jnp.concatenate([c.reshape(-1), conv_w.reshape(-1)]), 16)
    g0, gw_in0 = run_comm(Gather([pack0, w_in_b[0]]), "gather_first")
    g0 = g0.reshape(NDEV, 16 * LANES)
    c_all = g0[:, :D]
    conv_full = (g0[:, D:D + L * 3 * conv_shard].reshape(NDEV, L, 3, conv_shard)
                 .transpose(1, 2, 0, 3).reshape(L, 3, CW))


    W_in = [shards_to_columns(gw_in0, "w_in_columns0"), None]
    W_out, W1, W2 = [None] * L, [None] * L, [None] * L

    ada_b_mine = lax.dynamic_slice(ada_b, (0, me * ADA_COLS), (L, ADA_COLS)).reshape(L, 1, ADA_COLS)
    mod_part, c_act = ada_fwd(c_all, ada_w, ada_b_mine, "ada_fwd")
    gmod = run_comm(Gather([mod_part]), "gather_mod")[0]
    mod = lax.dynamic_index_in_dim(gmod, me, axis=2, keepdims=False)
    mod = mod.transpose(1, 0, 2).reshape(L, NMOD, 1, D)
    early_weights, token = start_copies([w_out_b[0]], me, "gather_early0_start", True, after=gmod)
    mod = tied(mod, token)

    cw8 = jnp.pad(conv_full, ((0, 0), (0, 5), (0, 0)))
    sg_bias = jnp.repeat(spatial_b.transpose(0, 2, 1), HD, axis=2)

    saved = []
    xl = x0
    for l in range(L):
        sh_m, sc_m, g_m, sh_f, sc_f, g_f = [mod[l, k] for k in range(NMOD)]
        h1 = normmod_fwd(xl, norm_mix_g[l:l + 1], sc_m, sh_m, f"norm_mix_fwd{l}")
        if l > 0:
            W_in[l] = shards_to_columns(finish_copies(w_in_handle, xl, f"gather_w_in{l}_wait")[0],
                                        f"w_in_columns{l}")
        qkv = mm_layer("proj_qkv", l, h1, W_in[l], out_dtypes=[BF16], cols=(0, QKV))[0]
        proj = mm_layer("proj_rest", l, h1, W_in[l], out_dtypes=[F32], cols=(QKV, REST))[0]
        a_out, a_tot, gw2, gw1 = attn_fwd(qkv, f"attn_fwd{l}", comm=Gather([w2_b[l], w1_b[l]]))
        gw_out, = finish_copies(early_weights, a_out, f"gather_early{l}_wait")
        W_out[l] = gw_out.reshape(D, D)
        W1[l] = gw1
        W2[l] = gw2.reshape(DFF, D)
        if l + 1 < L:
            w_in_handle, token = start_copies([w_in_b[l + 1]], me, f"gather_w_in{l + 1}_start", True, after=a_out)
            early_weights, token = start_copies([w_out_b[l + 1]], me, f"gather_early{l + 1}_start", True, after=token)
            g_m = tied(g_m, token)
        c_out = conv_fwd(proj, cw8[l], conv_b[l:l + 1], f"conv_fwd{l}")
        s_out = sg_fwd(proj, gmlp_norm_g[l:l + 1], spatial_w[l], sg_bias[l], f"sg_fwd{l}")
        cat = jnp.concatenate([a_out, c_out.astype(BF16), s_out.astype(BF16)], axis=1)
        mix, x1, h2 = mm_layer("mix", l, cat, W_out[l], out_dtypes=[F32, F32, BF16], epilogue=_residual_then_norm,
                               extras=[(xl, "tile"), (g_m, "col"), (norm_mlp_g[l:l + 1], "col"), (sc_f, "col"),
                                       (sh_f, "col")])
        ra, r = mm_layer("mlp_up", l, h2, W1[l], out_dtypes=[BF16, BF16], b_blocks=True,
                         epilogue=lambda acc: (jnp.maximum(acc, 0.0), jnp.square(jnp.maximum(acc, 0.0))))
        m2, x2 = mm_layer("mlp_down", l, r, W2[l], out_dtypes=[F32, F32],
                          epilogue=lambda acc, xr, g: (acc, xr + g * acc), extras=[(x1, "tile"), (g_f, "col")])
        saved.append(dict(x=xl, h1=h1, proj=proj, qkv=qkv, a_tot=a_tot, cat=cat, mix=mix,
                          x1=x1, h2=h2, ra=ra, r=r, m2=m2))
        xl = x2

    dx, loss_part, d_final_g, dm2, dg_f = loss_head(xl, target, final_norm_g.reshape(1, D),
                                                    (saved[L - 1]["m2"], mod[L - 1, NMOD - 1]), "loss_head")

    p_in, p_out, p_w1, p_w2 = [None] * L, [None] * L, [None] * L, [None] * L
    grads_in_flight = [None] * L
    vec_rows, d_norm_mix, d_norm_mlp = [None] * L, [None] * L, [None] * L
    dcw8, d_conv_b, d_gn, d_sw, d_sb = [None] * L, [None] * L, [None] * L, [None] * L, [None] * L
    for l in reversed(range(L)):
        sv = saved[l]
        sh_m, sc_m, g_m, sh_f, sc_f, g_f = [mod[l, k] for k in range(NMOD)]
        da = mm_layer("mlp_down_dgrad", l, dm2, W2[l], out_dtypes=[BF16], trans_b=True,
                      epilogue=lambda acc, rav: (acc * (2.0 * rav.astype(F32)),), extras=[(sv["ra"], "tile")])[0]
        dW2 = mm_layer("mlp_down_wgrad", l, sv["r"], dm2, out_dtypes=[BF16], trans_a=True)[0]
        dW1 = mm_layer("mlp_up_wgrad", l, sv["h2"], da, out_dtypes=[BF16], trans_a=True, out_blocks=True)[0]
        dh2 = mm_layer("mlp_up_dgrad", l, da, W1[l], out_dtypes=[F32], trans_b=True, b_blocks=True)[0]
        dx1, dsc_f, dsh_f, d_norm_mlp[l], dmix, dg_m = normmod_bwd(
            sv["x1"], dh2, dx, norm_mlp_g[l:l + 1], sc_f, f"norm_mlp_bwd{l}", gate_next=(sv["mix"], g_m))
        dcat = mm_layer("mix_dgrad", l, dmix, W_out[l], out_dtypes=[F32], trans_b=True)[0]
        dW_out = mm_layer("mix_wgrad", l, sv["cat"], dmix, out_dtypes=[BF16], trans_a=True)[0]
        pieces_w2, pieces_out = dW2.reshape(NDEV, DFF // NDEV, D), dW_out.reshape(NDEV, D // NDEV, D)
        ride, late = ([pieces_w2, pieces_out], dW1) if l == L - 1 else ([pieces_w2, dW1], pieces_out)
        dq, dk, dv, *arrived = attn_bwd(sv["qkv"], dcat, sv["a_tot"], f"attn_bwd{l}", comm=Exchange(ride))
        p_w2[l] = arrived[0]
        (p_out if l == L - 1 else p_w1)[l] = arrived[1]
        dbg, dcg, dhc, dcw8[l], d_conv_b[l] = conv_bwd(sv["proj"], dcat, cw8[l], conv_b[l:l + 1], f"conv_bwd{l}")
        dus, dvs, d_gn[l], dsw, dbias = sg_bwd(sv["proj"], dcat, gmlp_norm_g[l:l + 1], spatial_w[l], sg_bias[l],
                                               f"sg_bwd{l}")
        d_sw[l] = dsw.astype(BF16)
        d_sb[l] = dbias.reshape(T, SG_HEADS, HD).sum(axis=2).T
        dproj = jnp.concatenate([dq, dk, dv, dbg, dcg, dhc, dus, dvs], axis=1).astype(BF16)
        dW_in = mm_layer("proj_wgrad", l, sv["h1"], dproj, out_dtypes=[BF16], trans_a=True)[0]
        pieces = columns_to_shards(dW_in, f"w_in_grad_shards{l}")
        grads_in_flight[l], token = start_copies([late, pieces], me, f"exchange_tail{l}_start", False)
        dh1 = mm_layer("proj_dgrad", l, dproj, W_in[l], out_dtypes=[F32], trans_b=True, extras=[(token, "tie")])[0]
        below = (saved[l - 1]["m2"], mod[l - 1, NMOD - 1]) if l > 0 else None
        dx, dsc_m, dsh_m, d_norm_mix[l], *gated_below = normmod_bwd(
            sv["x"], dh1, dx1, tied(norm_mix_g[l:l + 1], token), sc_m, f"norm_mix_bwd{l}", gate_next=below)
        vec_rows[l] = [dsh_m, dsc_m, dg_m, dsh_f, dsc_f, dg_f, d_norm_mix[l], d_norm_mlp[l]]
        if l > 0:
            dm2, dg_f = gated_below

    grad_x = dx.reshape(1, S, D)

    g_w2, d_w2, nm_w2, nv_w2 = adamw_reduce(mlp_w2, p_w2, m_mlp_w2, v_mlp_w2, 256, "adamw_mlp_w2", tie=token)
    p_w1[L - 1], p_in[L - 1] = finish_copies(grads_in_flight[L - 1], d_w2, f"exchange_tail{L - 1}_wait")
    g_w1, d_w1, nm_w1, nv_w1 = adamw_reduce(mlp_w1, p_w1, m_mlp_w1, v_mlp_w1, 256, "adamw_mlp_w1", tie=token)

    vec_pack = jnp.concatenate([row for l in range(L) for row in vec_rows[l]]
                               + [d_final_g, loss_part, jnp.zeros((VEC_ROWS - VEC_FINAL_ROW - 2, D), F32)], axis=0)
    vec_pack, _ = lax.optimization_barrier((vec_pack, (d_w1, d_w2)))
    w256_pack = jnp.concatenate([blk for l in range(L) for blk in (
        dcw8[l], d_conv_b[l], d_gn[l], jnp.zeros((W256_ROWS_PER_LAYER - W256_GN - 1, CW), F32))], axis=0)
    vec_all, w256_all, sb_all, *sw_all = run_comm(
        Gather([vec_pack, w256_pack, jnp.concatenate(d_sb, axis=0)] + d_sw), "gather_small_grads")

    dmod_all = (vec_all[:, :VEC_FINAL_ROW].reshape(NDEV, L, VEC_ROWS_PER_LAYER, D)[:, :, :NMOD]
                .reshape(NDEV, L, NMOD * D))
    dmod_cols = lax.dynamic_slice(dmod_all, (0, 0, me * ADA_COLS), (NDEV, L, ADA_COLS)).transpose(1, 0, 2)
    g_ada_w = ada_bwd(c_act, dmod_cols, "ada_bwd")

    flat2 = lambda t: t.reshape(L * D, ADA_COLS)
    d_ada_w, nm_ada_w, nv_ada_w = [t.reshape(L, D, ADA_COLS) for t in adamw_plain(
        flat2(ada_w), flat2(g_ada_w), flat2(m_ada_w), flat2(v_ada_w), 256, "adamw_ada_w")]

    after = jnp.concatenate([t.reshape(-1)[:1] for t in (d_w1, d_w2, d_ada_w)])
    p_out[0], p_in[0] = finish_copies(grads_in_flight[0], after, "exchange_tail0_wait")
    g_w_in, d_w_in, nm_w_in, nv_w_in = adamw_reduce(w_in, p_in, m_w_in, v_w_in, 256, "adamw_w_in")
    g_w_out, d_w_out, nm_w_out, nv_w_out = adamw_reduce(w_out, p_out, m_w_out, v_w_out, 128, "adamw_w_out")

    as_row = lambda t: t.reshape(1, D)
    small_params = [(ada_b, m_ada_b, v_ada_b), (norm_mix_g, m_norm_mix_g, v_norm_mix_g),
                    (norm_mlp_g, m_norm_mlp_g, v_norm_mlp_g),
                    (as_row(final_norm_g), as_row(m_final_norm_g), as_row(v_final_norm_g)),
                    (conv_b, m_conv_b, v_conv_b), (gmlp_norm_g, m_gmlp_norm_g, v_gmlp_norm_g),
                    (spatial_w, m_spatial_w, v_spatial_w), (spatial_b, m_spatial_b, v_spatial_b)]
    updated, (loss_sum, taps_sum) = small_update(vec_all, w256_all, sb_all, sw_all, small_params, "small_update")
    loss = loss_sum[0, 0]
    u_ada_b, u_norm_mix, u_norm_mlp, u_final, u_conv_b, u_gn, u_sw, u_sb = updated
    u_final = [t.reshape(D) for t in u_final]
    g_conv_w = lax.dynamic_slice(taps_sum, (0, 0, me * conv_shard), (L, 3, conv_shard))
    flat_cw = lambda t: t.reshape(L * 3, conv_shard)
    u_conv_w = [g_conv_w] + [t.reshape(L, 3, conv_shard) for t in adamw_plain(
        flat_cw(conv_w), flat_cw(g_conv_w), flat_cw(m_conv_w), flat_cw(v_conv_w), L * 3, "adamw_conv_w")]
    small_sets = [u_ada_b, u_norm_mix, u_norm_mlp, u_conv_w, u_conv_b, u_gn, u_sw, u_sb, u_final]
    small_g, sd, snm, snv = [[u[k] for u in small_sets] for k in range(4)]

    def ordered(big, small):
        ada, win, wout, w1, w2 = big
        return [ada, small[0], small[1], small[2], win, small[3], small[4], small[5], small[6], small[7],
                wout, w1, w2, small[8]]

    grads = ordered([g_ada_w, g_w_in, g_w_out, g_w1, g_w2], small_g)
    deltas = ordered([d_ada_w, d_w_in, d_w_out, d_w1, d_w2], sd)
    new_m = ordered([nm_ada_w, nm_w_in, nm_w_out, nm_w1, nm_w2], snm)
    new_v = ordered([nv_ada_w, nv_w_in, nv_w_out, nv_w1, nv_w2], snv)
    return (loss, grad_x, *grads, *deltas, *new_m, *new_v)
```

```python
import functools
import math

import jax
import jax.numpy as jnp
from jax import lax
from jax.experimental import pallas as pl
from jax.experimental.pallas import tpu as pltpu

F32 = jnp.float32
BF16 = jnp.bfloat16
MESH = pl.DeviceIdType.MESH

S = 2048
D = 1024
L = 2
NDEV = 8
HD = 64
NH = 8
PROJ = 2816
DFF = 4096
NMOD = 6
EPS = 1e-6
T = 128
SG_HEADS = 4
LANES = 128
CW = 256
QKV = 3 * NH * HD
REST = PROJ - QKV

LR, B1, B2, AEPS, WD, STEP = 0.001, 0.9, 0.999, 1e-08, 0.01, 10
BC1 = 1.0 - B1 ** STEP
BC2 = 1.0 - B2 ** STEP

VMEM_LIMIT = 48 * 1024 * 1024

HBM_SPEC = pl.BlockSpec(memory_space=pltpu.HBM)


def _cparams(sem=None):
    return pltpu.CompilerParams(dimension_semantics=sem, vmem_limit_bytes=VMEM_LIMIT)


def _my_pos():
    return lax.axis_index("x"), lax.axis_index("y"), lax.axis_index("c")


def _lin(p):
    return 4 * p[0] + 2 * p[1] + p[2]


class Gather:
    def __init__(self, arrs):
        self.arrs = list(arrs)
        n = len(self.arrs)
        self.out_shape = [jax.ShapeDtypeStruct((NDEV,) + a.shape, a.dtype) for a in self.arrs]
        self.scratch = [pltpu.SemaphoreType.DMA((n, 7)), pltpu.SemaphoreType.DMA((n, 7)),
                        pltpu.SemaphoreType.DMA((n,))]

    def phases(self, ins, outs, sems):
        n = len(self.arrs)
        send_sems, recv_sems, local_sems = sems
        x, y, c = _my_pos()
        me, sibling = (x, y, c), (x, y, 1 - c)
        chips = [(1 - x, y), (x, 1 - y), (1 - x, 1 - y)]

        def copy(a, k, block, to, src=None):
            slot = outs[a].at[_lin(block)]
            return pltpu.make_async_remote_copy(
                src_ref=slot if src is None else src, dst_ref=slot,
                send_sem=send_sems.at[a, k], recv_sem=recv_sems.at[a, k],
                device_id=to, device_id_type=MESH)

        def mine(a):
            return pltpu.make_async_copy(ins[a], outs[a].at[_lin(me)], local_sems.at[a])

        def first(a):
            return [copy(a, 0, me, sibling, src=ins[a])] + [
                copy(a, 1 + j, me, (*chip, c), src=ins[a]) for j, chip in enumerate(chips)]

        def passed(a):
            return [copy(a, 4 + j, (*chip, c), sibling) for j, chip in enumerate(chips)]

        def start():
            for a in range(n):
                mine(a).start()
                for cp in first(a):
                    cp.start()

        def relay():
            for j, chip in enumerate(chips):
                for a in range(n):
                    copy(a, 1 + j, (*chip, c), me).wait_recv()
                    passed(a)[j].start()

        def finish():
            for a in range(n):
                copy(a, 0, sibling, me).wait_recv()
            for j, chip in enumerate(chips):
                for a in range(n):
                    copy(a, 4 + j, (*chip, 1 - c), me).wait_recv()
            for a in range(n):
                for cp in first(a) + passed(a):
                    cp.wait_send()
                mine(a).wait()

        return start, relay, finish


class Exchange:
    def __init__(self, arrs):
        self.arrs = list(arrs)
        n = len(self.arrs)
        self.out_shape = [jax.ShapeDtypeStruct(a.shape, a.dtype) for a in self.arrs]
        self.scratch = [pltpu.SemaphoreType.DMA((n, 7)), pltpu.SemaphoreType.DMA((n, 7)),
                        pltpu.SemaphoreType.DMA((n,))]

    def phases(self, ins, outs, sems):
        n = len(self.arrs)
        send_sems, recv_sems, local_sems = sems
        x, y, c = _my_pos()
        me = (x, y, c)

        def peer(mask):
            return (1 - x if mask & 4 else x, 1 - y if mask & 2 else y, 1 - c if mask & 1 else c)

        def copy(a, mask):
            return pltpu.make_async_remote_copy(
                src_ref=ins[a].at[_lin(peer(mask))], dst_ref=outs[a].at[_lin(me)],
                send_sem=send_sems.at[a, mask - 1], recv_sem=recv_sems.at[a, mask - 1],
                device_id=peer(mask), device_id_type=MESH)

        def arrival(a, mask):
            return pltpu.make_async_remote_copy(
                src_ref=ins[a].at[_lin(me)], dst_ref=outs[a].at[_lin(peer(mask))],
                send_sem=send_sems.at[a, mask - 1], recv_sem=recv_sems.at[a, mask - 1],
                device_id=peer(mask), device_id_type=MESH)

        def mine(a):
            return pltpu.make_async_copy(ins[a].at[_lin(me)], outs[a].at[_lin(me)], local_sems.at[a])

        def start():
            for a in range(n):
                mine(a).start()
            for mask in (4, 2, 6, 1, 5, 3, 7):
                for a in range(n):
                    copy(a, mask).start()

        def relay():
            pass

        def finish():
            for mask in range(1, 8):
                for a in range(n):
                    arrival(a, mask).wait_recv()
            for mask in range(1, 8):
                for a in range(n):
                    copy(a, mask).wait_send()
            for a in range(n):
                mine(a).wait()

        return start, relay, finish


def run_comm(plan, name):
    n = len(plan.arrs)

    def body(*refs):
        start, relay, finish = plan.phases(refs[:n], refs[n:2 * n], refs[2 * n:])
        start()
        relay()
        finish()

    outs = pl.pallas_call(
        body, name=name, out_shape=plan.out_shape,
        in_specs=[HBM_SPEC] * n, out_specs=[HBM_SPEC] * n, scratch_shapes=plan.scratch,
    )(*plan.arrs)
    return list(outs)


SEM_SPEC = pl.BlockSpec(memory_space=pltpu.SEMAPHORE)
DATAFLOW = pltpu.SideEffectType.DATAFLOW_SIDE_EFFECTING


def _peer_copies(src_ref, land_ref, send_sems, recv_sems, first, same_block):
    x, y, c = _my_pos()
    me = (x, y, c)
    sends, arrivals = [], []
    for mask in (4, 2, 6, 1, 5, 3, 7):
        peer = (1 - x if mask & 4 else x, 1 - y if mask & 2 else y, 1 - c if mask & 1 else c)
        sends.append(pltpu.make_async_remote_copy(
            src_ref=src_ref if same_block else src_ref.at[_lin(peer)], dst_ref=land_ref.at[_lin(me)],
            send_sem=send_sems.at[first + mask - 1], recv_sem=recv_sems.at[first + mask - 1], device_id=peer,
            device_id_type=MESH))
        arrivals.append(pltpu.make_async_remote_copy(
            src_ref=src_ref if same_block else src_ref.at[_lin(me)], dst_ref=land_ref.at[_lin(peer)],
            send_sem=send_sems.at[first + mask - 1], recv_sem=recv_sems.at[first + mask - 1], device_id=peer,
            device_id_type=MESH))
    return sends, arrivals


def start_copies(srcs, me, name, same_block, after=None):
    n = len(srcs)
    landings = []
    for src in srcs:
        own = src[None] if same_block else lax.dynamic_index_in_dim(src, me, axis=0, keepdims=True)
        landings.append(lax.dynamic_update_slice(lax.empty((NDEV,) + own.shape[1:], src.dtype), own,
                                                 (me,) + (0,) * (own.ndim - 1)))

    def body(*refs):
        send_sems, recv_sems = refs[-2 * n - 3], refs[-2 * n - 2]
        token = refs[-1]
        for k in range(n):
            sends, _ = _peer_copies(refs[2 * k], refs[2 * k + 1], send_sems, recv_sems, 7 * k, same_block)
            for cp in sends:
                cp.start()
        token[...] = jnp.zeros_like(token)

    hbm = lambda a: pltpu.HBM(a.shape, a.dtype)
    pairs = [a for pair in zip(srcs, landings) for a in pair]
    extra = [] if after is None else [after]
    sems = pltpu.SemaphoreType.DMA((7 * n,))
    send_sems, recv_sems, *thru, token = pl.pallas_call(
        body, name=name,
        out_shape=(sems, sems, *[hbm(a) for a in pairs], jax.ShapeDtypeStruct((8, LANES), F32)),
        in_specs=[HBM_SPEC] * (2 * n) + [pl.BlockSpec(memory_space=pl.ANY)] * len(extra),
        out_specs=(SEM_SPEC, SEM_SPEC, *[HBM_SPEC] * (2 * n), pl.BlockSpec(memory_space=pltpu.VMEM)),
        input_output_aliases={k: 2 + k for k in range(2 * n)},
        compiler_params=pltpu.CompilerParams(has_side_effects=DATAFLOW),
    )(*[pltpu.with_memory_space_constraint(a, pltpu.HBM) for a in pairs], *extra)
    return (send_sems, recv_sems, thru, same_block), token


def finish_copies(handle, after, name):
    send_sems, recv_sems, thru, same_block = handle
    n = len(thru) // 2

    def body(*refs):
        send_sems, recv_sems = refs[2 * n], refs[2 * n + 1]
        for k in range(n):
            sends, arrivals = _peer_copies(refs[2 * k], refs[2 * k + 1], send_sems, recv_sems, 7 * k, same_block)
            for cp in sends:
                cp.wait_send()
            for cp in arrivals:
                cp.wait_recv()

    hbm = lambda a: pltpu.HBM(a.shape, a.dtype)
    outs = pl.pallas_call(
        body, name=name, out_shape=tuple(hbm(a) for a in thru),
        in_specs=[HBM_SPEC] * (2 * n) + [SEM_SPEC, SEM_SPEC, pl.BlockSpec(memory_space=pl.ANY)],
        out_specs=tuple([HBM_SPEC] * (2 * n)), input_output_aliases={k: k for k in range(2 * n)},
        compiler_params=pltpu.CompilerParams(has_side_effects=DATAFLOW),
    )(*thru, send_sems, recv_sems, after)
    return [outs[2 * k + 1] for k in range(n)]


def tied(x, token):
    return x + token[0:1, 0:1].astype(x.dtype)


MM_TILES = {
    "proj_qkv": (S, 512), "proj_rest": (S, 256), "mix": (512, D), "mlp_up": (S, 512), "mlp_down": (1024, 256),
    "mlp_down_dgrad": (S, 1024), "mlp_down_wgrad": (1024, 1024), "mlp_up_wgrad": (1024, 512),
    "mlp_up_dgrad": (1024, 512), "mix_dgrad": (1024, 512), "mix_wgrad": (512, 1024),
    "proj_wgrad": (1024, PROJ // 2), "proj_dgrad": (1024, 512),
}


def mm_layer(kind, l, a, b, **kw):
    tm, tn = MM_TILES[kind]
    return mm(a, b, tm=tm, tn=tn, name=f"{kind}{l}", **kw)


def mm(a, b, *, tm, tn, out_dtypes, epilogue=None, extras=(), name, trans_a=False, trans_b=False,
       cols=None, b_blocks=False, out_blocks=False):
    if trans_a:
        kdim, m = a.shape
    else:
        m, kdim = a.shape
    shard = b.shape[-1] if b_blocks else None
    if b_blocks:
        full = (b.shape[1], NDEV * shard)
    else:
        full = b.shape
    first, ncols = cols if cols is not None else (0, full[0] if trans_b else full[1])
    assert full[1 if trans_b else 0] == kdim and m % tm == 0 and ncols % tn == 0 and first % tn == 0
    j0 = first // tn
    if trans_a:
        a_spec = pl.BlockSpec((kdim, tm), lambda i, j: (0, i))
    else:
        a_spec = pl.BlockSpec((tm, kdim), lambda i, j: (i, 0))
    if b_blocks and trans_b:
        b_spec = pl.BlockSpec((NDEV, tn, shard), lambda i, j: (0, j0 + j, 0))
    elif b_blocks:
        assert tn == shard
        b_spec = pl.BlockSpec((None, kdim, tn), lambda i, j: (j0 + j, 0, 0))
    elif trans_b:
        b_spec = pl.BlockSpec((tn, kdim), lambda i, j: (j0 + j, 0))
    else:
        b_spec = pl.BlockSpec((kdim, tn), lambda i, j: (0, j0 + j))
    if out_blocks:
        assert tn * NDEV == ncols
        out_spec = pl.BlockSpec((None, tm, tn), lambda i, j: (j, i, 0))
        out_dims = (NDEV, m, tn)
    else:
        out_spec = pl.BlockSpec((tm, tn), lambda i, j: (i, j))
        out_dims = (m, ncols)
    ex_specs = []
    for arr, kind in extras:
        if kind == "tile":
            ex_specs.append(pl.BlockSpec((tm, tn), lambda i, j: (i, j)))
        elif kind == "col":
            ex_specs.append(pl.BlockSpec((1, tn), lambda i, j: (0, j)))
        else:
            ex_specs.append(pl.BlockSpec(arr.shape, lambda i, j: (0, 0)))
    n_ex, n_out = len(extras), len(out_dtypes)
    used = [k for k, (_, kind) in enumerate(extras) if kind != "tie"]

    def body(a_ref, b_ref, *rest):
        ex_refs, out_refs = rest[:n_ex], rest[n_ex:]
        if trans_a:
            acc = lax.dot_general(a_ref[...], b_ref[...], (((0,), (0,)), ((), ())),
                                  preferred_element_type=F32)
        elif trans_b and b_blocks:
            acc = jnp.zeros((tm, tn), F32)
            for d in range(NDEV):
                acc = acc + lax.dot_general(a_ref[:, d * shard:(d + 1) * shard], b_ref[d],
                                            (((1,), (1,)), ((), ())), preferred_element_type=F32)
        elif trans_b:
            acc = lax.dot_general(a_ref[...], b_ref[...], (((1,), (1,)), ((), ())),
                                  preferred_element_type=F32)
        else:
            acc = jnp.dot(a_ref[...], b_ref[...], preferred_element_type=F32)
        outs = (acc,) if epilogue is None else epilogue(acc, *[ex_refs[k][...] for k in used])
        for o_ref, val in zip(out_refs, outs):
            o_ref[...] = val.astype(o_ref.dtype)

    outs = pl.pallas_call(
        body, name=name, grid=(m // tm, ncols // tn),
        in_specs=[a_spec, b_spec] + ex_specs,
        out_specs=[out_spec for _ in range(n_out)],
        out_shape=[jax.ShapeDtypeStruct(out_dims, dt) for dt in out_dtypes],
        compiler_params=_cparams(("parallel", "parallel")),
    )(a, b, *[arr for arr, _ in extras])
    return list(outs)


TR = 512

ROW_SPEC = pl.BlockSpec((TR, D), lambda i: (i, 0))
VEC_SPEC = pl.BlockSpec((1, D), lambda i: (0, 0))


def _residual_then_norm(acc, xr, gate, g, sc, sh):
    x_new = xr + gate * acc
    rstd = lax.rsqrt(jnp.mean(x_new * x_new, axis=-1, keepdims=True) + EPS)
    return acc, x_new, ((x_new * rstd) * g) * (1.0 + sc) + sh


def normmod_fwd(x, g, sc, sh, name):
    def body(x_ref, g_ref, sc_ref, sh_ref, o_ref):
        xv = x_ref[...]
        rstd = lax.rsqrt(jnp.mean(xv * xv, axis=-1, keepdims=True) + EPS)
        n = (xv * rstd) * g_ref[...]
        o_ref[...] = (n * (1.0 + sc_ref[...]) + sh_ref[...]).astype(o_ref.dtype)

    return pl.pallas_call(
        body, name=name, grid=(S // TR,),
        in_specs=[ROW_SPEC, VEC_SPEC, VEC_SPEC, VEC_SPEC], out_specs=ROW_SPEC,
        out_shape=jax.ShapeDtypeStruct((S, D), BF16),
        compiler_params=_cparams(("parallel",)),
    )(x, g, sc, sh)


def _gate_next(dxv, refs):
    br_ref, gate_ref, dbr_ref, dgate_ref = refs

    @pl.when(pl.program_id(0) == 0)
    def _():
        dgate_ref[...] = jnp.zeros_like(dgate_ref)

    dbr_ref[...] = (dxv * gate_ref[...]).astype(dbr_ref.dtype)
    dgate_ref[...] += jnp.sum(dxv * br_ref[...], axis=0, keepdims=True)


GATE_NEXT_IN = [ROW_SPEC, VEC_SPEC]
GATE_NEXT_OUT = [ROW_SPEC, VEC_SPEC]
GATE_NEXT_SHAPES = [jax.ShapeDtypeStruct((S, D), BF16), jax.ShapeDtypeStruct((1, D), F32)]


def normmod_bwd(x, dh, dres, g, sc, name, gate_next=None):
    nxt = 2 if gate_next else 0

    def body(x_ref, dh_ref, dres_ref, g_ref, sc_ref, *rest):
        nxt_in, (dx_ref, dsc_ref, dsh_ref, dg_ref), nxt_out = rest[:nxt], rest[nxt:nxt + 4], rest[nxt + 4:]

        @pl.when(pl.program_id(0) == 0)
        def _():
            dsc_ref[...] = jnp.zeros_like(dsc_ref)
            dsh_ref[...] = jnp.zeros_like(dsh_ref)
            dg_ref[...] = jnp.zeros_like(dg_ref)

        xv, dh = x_ref[...], dh_ref[...]
        gv = g_ref[...]
        rstd = lax.rsqrt(jnp.mean(xv * xv, axis=-1, keepdims=True) + EPS)
        xhat = xv * rstd
        dn = dh * (1.0 + sc_ref[...])
        dxhat = dn * gv
        dxv = dres_ref[...] + rstd * (dxhat - xhat * jnp.mean(dxhat * xhat, axis=-1, keepdims=True))
        dx_ref[...] = dxv
        dsc_ref[...] += jnp.sum(dh * (xhat * gv), axis=0, keepdims=True)
        dsh_ref[...] += jnp.sum(dh, axis=0, keepdims=True)
        dg_ref[...] += jnp.sum(dn * xhat, axis=0, keepdims=True)
        if gate_next:
            _gate_next(dxv, nxt_in + nxt_out)

    vec_out = jax.ShapeDtypeStruct((1, D), F32)
    on = bool(gate_next)
    return pl.pallas_call(
        body, name=name, grid=(S // TR,),
        in_specs=[ROW_SPEC, ROW_SPEC, ROW_SPEC, VEC_SPEC, VEC_SPEC] + GATE_NEXT_IN * on,
        out_specs=[ROW_SPEC, VEC_SPEC, VEC_SPEC, VEC_SPEC] + GATE_NEXT_OUT * on,
        out_shape=[jax.ShapeDtypeStruct((S, D), F32), vec_out, vec_out, vec_out] + GATE_NEXT_SHAPES * on,
        compiler_params=_cparams(("arbitrary",)),
    )(x, dh, dres, g, sc, *(gate_next or ()))


def loss_head(x, target, g, gate_next, name):
    def body(x_ref, t_ref, g_ref, br_ref, gate_ref, dx_ref, loss_ref, dg_ref, dbr_ref, dgate_ref):
        @pl.when(pl.program_id(0) == 0)
        def _():
            loss_ref[...] = jnp.zeros_like(loss_ref)
            dg_ref[...] = jnp.zeros_like(dg_ref)

        xv, gv = x_ref[...], g_ref[...]
        rstd = lax.rsqrt(jnp.mean(xv * xv, axis=-1, keepdims=True) + EPS)
        xhat = xv * rstd
        err = xhat * gv - t_ref[...]
        loss_ref[...] += jnp.sum(err * err) * (0.5 / D)
        dy = err * (1.0 / D)
        dg_ref[...] += jnp.sum(dy * xhat, axis=0, keepdims=True)
        dxhat = dy * gv
        dxv = rstd * (dxhat - xhat * jnp.mean(dxhat * xhat, axis=-1, keepdims=True))
        dx_ref[...] = dxv
        _gate_next(dxv, (br_ref, gate_ref, dbr_ref, dgate_ref))

    return pl.pallas_call(
        body, name=name, grid=(S // TR,),
        in_specs=[ROW_SPEC, ROW_SPEC, VEC_SPEC] + GATE_NEXT_IN,
        out_specs=[ROW_SPEC, VEC_SPEC, VEC_SPEC] + GATE_NEXT_OUT,
        out_shape=[jax.ShapeDtypeStruct((S, D), F32), jax.ShapeDtypeStruct((1, D), F32),
                   jax.ShapeDtypeStruct((1, D), F32)] + GATE_NEXT_SHAPES,
        compiler_params=_cparams(("arbitrary",)),
    )(x, target, g, *gate_next)


TQ = 512
RS = 128
NSUB = TQ // RS
TK = 128


def _dot_hilo(a, tri_twice):
    hi = a.astype(BF16)
    lo = (a - hi.astype(F32)).astype(BF16)
    return jnp.dot(jnp.concatenate([hi, lo], axis=1), tri_twice, preferred_element_type=F32)


def _log_stay(z):
    neg = -z
    return jnp.minimum(neg, 0.0) - jnp.log(1.0 + jnp.exp(jnp.minimum(z, neg)))


def _tri_and_ones(kind):
    row = jnp.bitwise_and(lax.broadcasted_iota(jnp.int32, (2 * TK, 2 * TK), 0), TK - 1)
    col = lax.broadcasted_iota(jnp.int32, (2 * TK, 2 * TK), 1)
    tri = {"after": row > col, "upto": row <= col, "before": row < col}[kind]
    return jnp.logical_or(col >= TK, tri).astype(BF16)


NPAIR = NH // 2
SCALE = HD ** -0.5


def _pair_specs(first_block):
    rows = pl.BlockSpec((TQ, LANES), lambda p, i: (i, first_block + p))
    whole = pl.BlockSpec((S, LANES), lambda p, i: (0, first_block + p))
    return rows, whole


Q_ROWS_SPEC, _ = _pair_specs(0)
_, K_ALL_SPEC = _pair_specs(NPAIR)
_, V_ALL_SPEC = _pair_specs(2 * NPAIR)
PAIR_ROWS_SPEC = pl.BlockSpec((TQ, LANES), lambda p, i: (i, p))
PAIR_ALL_SPEC = pl.BlockSpec((S, LANES), lambda p, i: (0, p))
PAIR_TOTAL_SPEC = pl.BlockSpec((2, TQ, TK), lambda p, i: (p, i, 0))


def _head_halves(x):
    first = lax.broadcasted_iota(jnp.int32, x.shape, 1) < HD
    zero = jnp.zeros_like(x)
    return jnp.where(first, x, zero), jnp.where(first, zero, x)


def _join_heads(a, b):
    return jnp.where(lax.broadcasted_iota(jnp.int32, a.shape, 1) < HD, a, b)


def _comm_hooks(comm, refs, n_in, n_out, n_scratch):
    nc = len(comm.arrs) if comm is not None else 0
    ins, cin = refs[:n_in], refs[n_in:n_in + nc]
    outs = refs[n_in + nc:n_in + nc + n_out]
    cout = refs[n_in + nc + n_out:n_in + 2 * nc + n_out]
    scratch = refs[n_in + 2 * nc + n_out:n_in + 2 * nc + n_out + n_scratch]
    sems = refs[n_in + 2 * nc + n_out + n_scratch:]
    phases = comm.phases(cin, cout, sems) if comm is not None else None
    return ins, outs, scratch, phases


def _with_comm(comm, in_specs, out_specs, out_shape, operands, scratch):
    if comm is None:
        return dict(in_specs=in_specs, out_specs=out_specs, out_shape=out_shape, scratch_shapes=scratch), operands
    nc = len(comm.arrs)
    return dict(in_specs=in_specs + [HBM_SPEC] * nc, out_specs=out_specs + [HBM_SPEC] * nc,
                out_shape=out_shape + comm.out_shape, scratch_shapes=scratch + comm.scratch), operands + comm.arrs


def attn_fwd(qkv, name, comm=None):
    n_steps = S // TQ

    def body(*refs):
        (q_ref, k_ref, v_ref), (o_ref, r_ref), (acc_ref, z_even, z_odd, w_ref), phases = _comm_hooks(
            comm, refs, 3, 2, 4)
        p = pl.program_id(0)
        i = pl.program_id(1)
        if phases is not None:
            pl.when(jnp.logical_and(p == 0, i == 0))(phases[0])
            pl.when(jnp.logical_and(p == NPAIR - 1, i == n_steps - 1))(phases[1])
        chains = [(sub, h) for sub in range(NSUB) for h in range(2)]
        q_sub = [_head_halves(q_ref[pl.ds(sub * RS, RS), :] * SCALE) for sub in range(NSUB)]
        after = _tri_and_ones("after")
        below_diagonal = (lax.broadcasted_iota(jnp.int32, (RS, TK), 1)
                          < lax.broadcasted_iota(jnp.int32, (RS, TK), 0))
        base = i * NSUB
        all_subs = list(range(NSUB))

        acc_ref[...] = jnp.zeros_like(acc_ref)
        r_ref[...] = jnp.zeros_like(r_ref)
        w_ref[...] = jnp.zeros_like(w_ref)

        def key_rows(block):
            return pl.ds(pl.multiple_of(block * TK, TK), TK)

        def store_scores(z_ref, block, subs):
            kb = k_ref[key_rows(block), :]
            for c, (sub, h) in enumerate(chains):
                if sub in subs:
                    z_ref[c] = lax.dot_general(q_sub[sub][h], kb, (((1,), (1,)), ((), ())),
                                               preferred_element_type=F32)

        def add_weighted_values(block, subs):
            vb = v_ref[key_rows(block), :]
            for sub in subs:
                acc_ref[pl.ds(sub * RS, RS), :] += _join_heads(*[
                    jnp.dot(w_ref[2 * sub + h], vb, preferred_element_type=F32) for h in range(2)])

        def step(block, z_ref, z_next_ref, subs, diagonal_sub, prev_subs, next_subs):
            if prev_subs:
                add_weighted_values(block + 1, prev_subs)
            if next_subs:
                store_scores(z_next_ref, jnp.maximum(block - 1, 0), next_subs)
            active = [(c, sub, h) for c, (sub, h) in enumerate(chains) if sub in subs]
            ls, sums = {}, {}
            for c, sub, h in active:
                ls[c] = _log_stay(z_ref[c])
                sums[c] = _dot_hilo(jnp.where(below_diagonal, ls[c], 0.0) if sub == diagonal_sub else ls[c], after)
            for c, sub, h in active:
                rows = pl.ds(sub * RS, RS)
                later = r_ref[h, rows, :]
                w = jnp.exp(z_ref[c] + ls[c] + (sums[c][:, :TK] + later))
                if sub == diagonal_sub:
                    w = jnp.where(below_diagonal, w, 0.0)
                w_ref[c] = w.astype(BF16)
                r_ref[h, rows, :] = later + sums[c][:, TK:]

        store_scores(z_even, base + NSUB - 1, [NSUB - 1])
        buffers = (z_even, z_odd)
        for j in reversed(range(NSUB)):
            subs = all_subs[j:]
            step(base + j, buffers[0], buffers[1], subs, j, all_subs[j + 1:], all_subs[j - 1:] if j else all_subs)
            buffers = buffers[::-1]
        assert buffers[0] is z_even

        @pl.loop(0, base // 2)
        def _(pair):
            block = base - 1 - 2 * pair
            step(block, z_even, z_odd, all_subs, None, all_subs, all_subs)
            step(block - 1, z_odd, z_even, all_subs, None, all_subs, all_subs)

        add_weighted_values(0, all_subs)
        o_ref[...] = acc_ref[...].astype(o_ref.dtype)
        if phases is not None:
            pl.when(jnp.logical_and(p == NPAIR - 1, i == n_steps - 1))(phases[2])

    kwargs, operands = _with_comm(
        comm, [Q_ROWS_SPEC, K_ALL_SPEC, V_ALL_SPEC], [PAIR_ROWS_SPEC, PAIR_TOTAL_SPEC],
        [jax.ShapeDtypeStruct((S, NH * HD), BF16), jax.ShapeDtypeStruct((NH, S, TK), F32)], [qkv, qkv, qkv],
        [pltpu.VMEM((TQ, LANES), F32), pltpu.VMEM((2 * NSUB, RS, TK), F32), pltpu.VMEM((2 * NSUB, RS, TK), F32),
         pltpu.VMEM((2 * NSUB, RS, TK), BF16)])
    return pl.pallas_call(
        body, name=name, grid=(NPAIR, n_steps),
        compiler_params=_cparams(("arbitrary", "arbitrary")), **kwargs,
    )(*operands)


def attn_bwd(qkv, dout, totals, name, comm=None):
    n_steps = S // TQ

    def body(*refs):
        ((q_ref, k_ref, v_ref, do_ref, r_ref), (dq_out, dk_out, dv_out),
         (z_even, z_odd, dw_even, dw_odd, before_ref, dbefore_ref, dz_ref, w_ref, dq_ref, dk_ref, dv_ref),
         phases) = _comm_hooks(comm, refs, 5, 3, 11)
        p = pl.program_id(0)
        i = pl.program_id(1)
        if phases is not None:
            pl.when(jnp.logical_and(p == 0, i == 0))(phases[0])
            pl.when(jnp.logical_and(p == NPAIR - 1, i == n_steps - 2))(phases[1])

        @pl.when(i == 0)
        def _():
            dk_ref[...] = jnp.zeros_like(dk_ref)
            dv_ref[...] = jnp.zeros_like(dv_ref)

        chains = [(sub, h) for sub in range(NSUB) for h in range(2)]
        nch = len(chains)
        qb = q_ref[...]
        dob = do_ref[...].astype(BF16)
        q_sub = [_head_halves(qb[sub * RS:(sub + 1) * RS] * SCALE) for sub in range(NSUB)]
        do_sub = [_head_halves(dob[sub * RS:(sub + 1) * RS]) for sub in range(NSUB)]
        upto = _tri_and_ones("upto")
        before_tri = _tri_and_ones("before")
        below_diagonal = (lax.broadcasted_iota(jnp.int32, (RS, TK), 1)
                          < lax.broadcasted_iota(jnp.int32, (RS, TK), 0))
        contract_lanes = (((1,), (1,)), ((), ()))
        contract_rows = (((0,), (0,)), ((), ()))
        base = i * NSUB
        all_subs = list(range(NSUB))

        def key_rows(block):
            return pl.ds(pl.multiple_of(block * TK, TK), TK)

        def store_products(bufs, block, subs):
            z_ref, dw_ref = bufs
            kb = k_ref[key_rows(block), :]
            vb = v_ref[key_rows(block), :]
            for c, (sub, h) in enumerate(chains):
                if sub in subs:
                    z_ref[c] = lax.dot_general(q_sub[sub][h], kb, contract_lanes, preferred_element_type=F32)
                    dw_ref[c] = lax.dot_general(do_sub[sub][h], vb, contract_lanes, preferred_element_type=F32)

        def add_gradients(block, subs):
            kb = k_ref[key_rows(block), :]
            for sub in subs:
                rows = pl.ds(sub * RS, RS)
                dq_ref[rows, :] += _join_heads(*[jnp.dot(dz_ref[h, rows, :], kb, preferred_element_type=F32)
                                                 for h in range(2)])
            dk_ref[key_rows(block), :] += _join_heads(*[
                lax.dot_general(dz_ref[h], qb, contract_rows, preferred_element_type=F32) for h in range(2)])
            dv_ref[key_rows(block), :] += _join_heads(*[
                lax.dot_general(w_ref[h], dob, contract_rows, preferred_element_type=F32) for h in range(2)])

        for ref in (dq_ref, before_ref, dbefore_ref, dz_ref, w_ref):
            ref[...] = jnp.zeros_like(ref)
        even, odd = (z_even, dw_even), (z_odd, dw_odd)
        store_products(even, 0, all_subs)

        def step(block, bufs, next_bufs, subs, diagonal_sub, prev_subs, next_subs):
            z_ref, dw_ref = bufs
            add_gradients(jnp.maximum(block - 1, 0), prev_subs)
            for sub in prev_subs:
                if sub not in subs:
                    dz_ref[:, pl.ds(sub * RS, RS), :] = jnp.zeros((2, RS, TK), BF16)
                    w_ref[:, pl.ds(sub * RS, RS), :] = jnp.zeros((2, RS, TK), BF16)
            if next_subs:
                store_products(next_bufs, block + 1, next_subs)
            active = [(c, sub, h) for c, (sub, h) in enumerate(chains) if sub in subs]
            ls, sums, dl, dsums = {}, {}, {}, {}
            for c, sub, h in active:
                ls[c] = _log_stay(z_ref[c])
                sums[c] = _dot_hilo(jnp.where(below_diagonal, ls[c], 0.0) if sub == diagonal_sub else ls[c], upto)
            for c, sub, h in active:
                rows = pl.ds(sub * RS, RS)
                before = before_ref[c]
                log_after = r_ref[h, rows, :] - (sums[c][:, :TK] + before)
                w = jnp.exp((z_ref[c] + ls[c]) + log_after)
                if sub == diagonal_sub:
                    w = jnp.where(below_diagonal, w, 0.0)
                dl[c] = dw_ref[c] * w
                dsums[c] = _dot_hilo(dl[c], before_tri)
                w_ref[h, rows, :] = w.astype(BF16)
                before_ref[c] = before + sums[c][:, TK:]
            for c, sub, h in active:
                rows = pl.ds(sub * RS, RS)
                dbefore = dbefore_ref[c]
                beta = jnp.exp(z_ref[c] + ls[c])
                if sub == diagonal_sub:
                    beta = jnp.where(below_diagonal, beta, 0.0)
                dstay = dsums[c][:, :TK] + dbefore
                dz_ref[h, rows, :] = ((dl[c] - beta * (dl[c] + dstay)) * SCALE).astype(BF16)
                dbefore_ref[c] = dbefore + dsums[c][:, TK:]

        @pl.loop(0, base // 2)
        def _(pair):
            step(2 * pair, even, odd, all_subs, None, all_subs, all_subs)
            step(2 * pair + 1, odd, even, all_subs, None, all_subs, all_subs)

        bufs = (even, odd)
        for j in range(NSUB):
            step(base + j, bufs[0], bufs[1], all_subs[j:], j, all_subs[j - 1:] if j else all_subs, all_subs[j + 1:])
            bufs = bufs[::-1]

        add_gradients(base + NSUB - 1, all_subs[NSUB - 1:])
        dq_out[...] = dq_ref[...].astype(dq_out.dtype)

        @pl.when(i == n_steps - 1)
        def _():
            dk_out[...] = dk_ref[...].astype(dk_out.dtype)
            dv_out[...] = dv_ref[...].astype(dv_out.dtype)

        if phases is not None:
            pl.when(jnp.logical_and(p == NPAIR - 1, i == n_steps - 1))(phases[2])

    full = jax.ShapeDtypeStruct((S, NH * HD), BF16)
    kwargs, operands = _with_comm(
        comm, [Q_ROWS_SPEC, K_ALL_SPEC, V_ALL_SPEC, PAIR_ROWS_SPEC, PAIR_TOTAL_SPEC],
        [PAIR_ROWS_SPEC, PAIR_ALL_SPEC, PAIR_ALL_SPEC], [full, full, full], [qkv, qkv, qkv, dout, totals],
        [pltpu.VMEM((2 * NSUB, RS, TK), F32)] * 6 + [pltpu.VMEM((2, TQ, TK), BF16)] * 2
        + [pltpu.VMEM((TQ, LANES), F32), pltpu.VMEM((S, LANES), F32), pltpu.VMEM((S, LANES), F32)])
    return pl.pallas_call(
        body, name=name, grid=(NPAIR, n_steps),
        compiler_params=_cparams(("arbitrary", "arbitrary")), **kwargs,
    )(*operands)


def _proj_cols(first_col):
    base = first_col // LANES
    return pl.BlockSpec((S, LANES), lambda j: (0, base + j))


CONV_OUT_SPEC = pl.BlockSpec((S, LANES), lambda j: (0, j))
CONV_DOUT_SPEC = pl.BlockSpec((S, LANES), lambda j: (0, (NH * HD) // LANES + j))
CONV_W_SPEC = pl.BlockSpec((8, LANES), lambda j: (0, j))
CONV_B_SPEC = pl.BlockSpec((1, LANES), lambda j: (0, j))


def _shift_down(u, n):
    rows = lax.broadcasted_iota(jnp.int32, u.shape, 0)
    return jnp.where(rows >= n, pltpu.roll(u, n, 0), 0.0)


def _shift_up(u, n):
    rows = lax.broadcasted_iota(jnp.int32, u.shape, 0)
    return jnp.where(rows < S - n, pltpu.roll(u, S - n, 0), 0.0)


def conv_fwd(proj, cw8, cb, name):
    def body(bg_ref, cg_ref, hc_ref, w_ref, b_ref, o_ref):
        u = cg_ref[...] * hc_ref[...]
        w = w_ref[...]
        y = w[0:1, :] * _shift_down(u, 2) + w[1:2, :] * _shift_down(u, 1) + w[2:3, :] * u + b_ref[...]
        o_ref[...] = bg_ref[...] * y

    return pl.pallas_call(
        body, name=name, grid=(CW // LANES,),
        in_specs=[_proj_cols(0), _proj_cols(CW), _proj_cols(2 * CW), CONV_W_SPEC, CONV_B_SPEC],
        out_specs=CONV_OUT_SPEC, out_shape=jax.ShapeDtypeStruct((S, CW), F32),
        compiler_params=_cparams(("parallel",)),
    )(proj, proj, proj, cw8, cb)


def conv_bwd(proj, dout, cw8, cb, name):
    def body(bg_ref, cg_ref, hc_ref, do_ref, w_ref, b_ref, dbg_ref, dcg_ref, dhc_ref, dw_ref, db_ref):
        cg, hc, do = cg_ref[...], hc_ref[...], do_ref[...]
        w = w_ref[...]
        u = cg * hc
        u1, u2 = _shift_down(u, 1), _shift_down(u, 2)
        y = w[0:1, :] * u2 + w[1:2, :] * u1 + w[2:3, :] * u + b_ref[...]
        dbg_ref[...] = (do * y).astype(dbg_ref.dtype)
        dy = do * bg_ref[...]
        db_ref[...] = jnp.sum(dy, axis=0, keepdims=True)
        dw_ref[...] = jnp.concatenate(
            [jnp.sum(dy * u2, axis=0, keepdims=True), jnp.sum(dy * u1, axis=0, keepdims=True),
             jnp.sum(dy * u, axis=0, keepdims=True), jnp.zeros((5, LANES), F32)], axis=0)
        du = w[2:3, :] * dy + w[1:2, :] * _shift_up(dy, 1) + w[0:1, :] * _shift_up(dy, 2)
        dcg_ref[...] = (du * hc).astype(dcg_ref.dtype)
        dhc_ref[...] = (du * cg).astype(dhc_ref.dtype)

    full = jax.ShapeDtypeStruct((S, CW), BF16)
    return pl.pallas_call(
        body, name=name, grid=(CW // LANES,),
        in_specs=[_proj_cols(0), _proj_cols(CW), _proj_cols(2 * CW), CONV_DOUT_SPEC, CONV_W_SPEC, CONV_B_SPEC],
        out_specs=[CONV_OUT_SPEC, CONV_OUT_SPEC, CONV_OUT_SPEC, CONV_W_SPEC, CONV_B_SPEC],
        out_shape=[full, full, full, jax.ShapeDtypeStruct((8, CW), F32), jax.ShapeDtypeStruct((1, CW), F32)],
        compiler_params=_cparams(("parallel",)),
    )(proj, proj, proj, dout, cw8, cb)


GELU_K = math.sqrt(2.0 / math.pi)
GELU_C = 0.044715


def _gelu(x):
    return 0.5 * x * (1.0 + jnp.tanh(GELU_K * (x + GELU_C * (x * x * x))))


def _gelu_grad(x):
    t = jnp.tanh(GELU_K * (x + GELU_C * (x * x * x)))
    return 0.5 * (1.0 + t) + 0.5 * x * (1.0 - t * t) * (GELU_K * (1.0 + 3.0 * GELU_C * (x * x)))


def _sg_masks():
    row = lax.broadcasted_iota(jnp.int32, (T, T), 0)
    col = lax.broadcasted_iota(jnp.int32, (T, T), 1)
    causal = jnp.right_shift(row, 6) >= jnp.right_shift(col, 6)
    head_of_col = jnp.right_shift(lax.broadcasted_iota(jnp.int32, (T, CW), 1), 6)
    return causal, head_of_col


def _sg_weights(sw_ref, causal):
    return [jnp.where(causal, sw_ref[h], 0.0).astype(BF16) for h in range(SG_HEADS)]


def _sg_mixed(vnb, weights, bias, head_of_col):
    mixed = bias
    for h in range(SG_HEADS):
        mh = jnp.dot(weights[h], vnb, preferred_element_type=F32)
        mixed = mixed + jnp.where(head_of_col == h, mh, 0.0)
    return mixed


SG_WINDOWS = 4
SG_ROWS = SG_WINDOWS * T
SG_U_SPEC = pl.BlockSpec((SG_ROWS, CW), lambda n: (n, 3))
SG_V_SPEC = pl.BlockSpec((SG_ROWS, CW), lambda n: (n, 4))
SG_ROW_SPEC = pl.BlockSpec((SG_ROWS, CW), lambda n: (n, 0))
SG_DOUT_SPEC = pl.BlockSpec((SG_ROWS, CW), lambda n: (n, 3))
SG_G_SPEC = pl.BlockSpec((1, CW), lambda n: (0, 0))
SG_W_SPEC = pl.BlockSpec((SG_HEADS, T, T), lambda n: (0, 0, 0))
SG_BIAS_SPEC = pl.BlockSpec((T, CW), lambda n: (0, 0))


def sg_fwd(proj, gn, sw, bias, name):
    def body(u_ref, v_ref, g_ref, sw_ref, bias_ref, o_ref):
        causal, head_of_col = _sg_masks()
        weights = _sg_weights(sw_ref, causal)
        for wdw in range(SG_WINDOWS):
            rows = pl.ds(wdw * T, T)
            gv = _gelu(v_ref[rows, :])
            rstd = lax.rsqrt(jnp.mean(gv * gv, axis=-1, keepdims=True) + EPS)
            vnb = ((gv * rstd) * g_ref[...]).astype(BF16)
            mixed = _sg_mixed(vnb, weights, bias_ref[...], head_of_col)
            o_ref[rows, :] = _gelu(u_ref[rows, :]) * mixed

    return pl.pallas_call(
        body, name=name, grid=(S // SG_ROWS,),
        in_specs=[SG_U_SPEC, SG_V_SPEC, SG_G_SPEC, SG_W_SPEC, SG_BIAS_SPEC],
        out_specs=SG_ROW_SPEC, out_shape=jax.ShapeDtypeStruct((S, CW), F32),
        compiler_params=_cparams(("parallel",)),
    )(proj, proj, gn, sw, bias)


def sg_bwd(proj, dout, gn, sw, bias, name):
    def body(u_ref, v_ref, do_ref, g_ref, sw_ref, bias_ref, du_ref, dv_ref, dg_ref, dsw_ref, dbias_ref):
        @pl.when(pl.program_id(0) == 0)
        def _():
            dg_ref[...] = jnp.zeros_like(dg_ref)
            dsw_ref[...] = jnp.zeros_like(dsw_ref)
            dbias_ref[...] = jnp.zeros_like(dbias_ref)

        causal, head_of_col = _sg_masks()
        weights = _sg_weights(sw_ref, causal)
        gnv = g_ref[...]
        for wdw in range(SG_WINDOWS):
            rows = pl.ds(wdw * T, T)
            uv, vv, do = u_ref[rows, :], v_ref[rows, :], do_ref[rows, :]
            gv = _gelu(vv)
            rstd = lax.rsqrt(jnp.mean(gv * gv, axis=-1, keepdims=True) + EPS)
            xhat = gv * rstd
            vnb = (xhat * gnv).astype(BF16)
            mixed = _sg_mixed(vnb, weights, bias_ref[...], head_of_col)
            du_ref[rows, :] = ((do * mixed) * _gelu_grad(uv)).astype(du_ref.dtype)
            dmix = do * _gelu(uv)
            dbias_ref[...] += dmix
            dmixb = dmix.astype(BF16)
            dvn = jnp.zeros((T, CW), F32)
            for h in range(SG_HEADS):
                dvh = lax.dot_general(weights[h], dmixb, (((0,), (0,)), ((), ())), preferred_element_type=F32)
                dvn = dvn + jnp.where(head_of_col == h, dvh, 0.0)
                dmh = jnp.where(head_of_col == h, dmixb, jnp.zeros_like(dmixb))
                dwh = lax.dot_general(dmh, vnb, (((1,), (1,)), ((), ())), preferred_element_type=F32)
                dsw_ref[h] += jnp.where(causal, dwh, 0.0)
            dg_ref[...] += jnp.sum(dvn * xhat, axis=0, keepdims=True)
            dxhat = dvn * gnv
            dgv = rstd * (dxhat - xhat * jnp.mean(dxhat * xhat, axis=-1, keepdims=True))
            dv_ref[rows, :] = (dgv * _gelu_grad(vv)).astype(dv_ref.dtype)

    full = jax.ShapeDtypeStruct((S, CW), BF16)
    return pl.pallas_call(
        body, name=name, grid=(S // SG_ROWS,),
        in_specs=[SG_U_SPEC, SG_V_SPEC, SG_DOUT_SPEC, SG_G_SPEC, SG_W_SPEC, SG_BIAS_SPEC],
        out_specs=[SG_ROW_SPEC, SG_ROW_SPEC, SG_G_SPEC, SG_W_SPEC, SG_BIAS_SPEC],
        out_shape=[full, full, jax.ShapeDtypeStruct((1, CW), F32),
                   jax.ShapeDtypeStruct((SG_HEADS, T, T), F32), jax.ShapeDtypeStruct((T, CW), F32)],
        compiler_params=_cparams(("arbitrary",)),
    )(proj, proj, dout, gn, sw, bias)


ADA_COLS = NMOD * D // NDEV


def ada_fwd(c_all, ada_w, ada_b_mine, name):
    def body(c_ref, w_ref, b_ref, o_ref, ca_ref):
        cv = c_ref[...]
        ca = cv * (1.0 / (1.0 + jnp.exp(-cv)))
        ca_ref[...] = ca
        cab = ca.astype(BF16)
        for l in range(L):
            o_ref[l] = jnp.dot(cab, w_ref[l].astype(BF16), preferred_element_type=F32) + b_ref[l]

    return pl.pallas_call(
        body, name=name,
        out_shape=[jax.ShapeDtypeStruct((L, NDEV, ADA_COLS), F32), jax.ShapeDtypeStruct((NDEV, D), F32)],
        compiler_params=_cparams(),
    )(c_all, ada_w, ada_b_mine)


def ada_bwd(ca, dmod_cols, name):
    def body(ca_ref, dm_ref, o_ref):
        cab = ca_ref[...].astype(BF16)
        for l in range(L):
            o_ref[l] = lax.dot_general(cab, dm_ref[l].astype(BF16), (((0,), (0,)), ((), ())),
                                       preferred_element_type=F32)

    return pl.pallas_call(
        body, name=name, out_shape=jax.ShapeDtypeStruct((L, D, ADA_COLS), F32),
        compiler_params=_cparams(),
    )(ca, dmod_cols)


def _adamw(w, g, m, v):
    m = B1 * m + (1.0 - B1) * g
    v = B2 * v + (1.0 - B2) * (g * g)
    m_hat = m / BC1
    v_hat = v / BC2
    delta = -LR * (m_hat / (jnp.sqrt(v_hat) + AEPS) + WD * w)
    return delta, m, v


VEC_ROWS_PER_LAYER = 8
VEC_FINAL_ROW = L * VEC_ROWS_PER_LAYER
VEC_ROWS = VEC_FINAL_ROW + 8
W256_TAPS, W256_CONV_B, W256_GN = 0, 8, 9
W256_ROWS_PER_LAYER = 16


def small_update(vec_all, w256_all, sb_all, sw_all, params, name):
    n_par = len(params)

    def body(*refs):
        vec_ref, w256_ref, sb_ref = refs[:3]
        sw_refs = refs[3:3 + L]
        par_refs = [refs[3 + L + 3 * k:3 + L + 3 * k + 3] for k in range(n_par)]
        out = refs[3 + L + 3 * n_par:]
        out_par = [out[4 * k:4 * k + 4] for k in range(n_par)]
        loss_ref, taps_ref = out[4 * n_par:]

        def total(ref, idx):
            acc = ref[(0,) + idx].astype(F32)
            for d in range(1, NDEV):
                acc = acc + ref[(d,) + idx].astype(F32)
            return acc

        def update(k, region, g):
            w_ref, m_ref, v_ref = par_refs[k]
            g_ref, d_ref, nm_ref, nv_ref = out_par[k]
            delta, nm, nv = _adamw(w_ref[region], g, m_ref[region], v_ref[region])
            g_ref[region] = g
            d_ref[region] = delta
            nm_ref[region] = nm
            nv_ref[region] = nv

        for l in range(L):
            base = l * VEC_ROWS_PER_LAYER
            for k in range(NMOD):
                update(0, (slice(l, l + 1), slice(k * D, (k + 1) * D)), total(vec_ref, (slice(base + k, base + k + 1),)))
            update(1, (slice(l, l + 1),), total(vec_ref, (slice(base + 6, base + 7),)))
            update(2, (slice(l, l + 1),), total(vec_ref, (slice(base + 7, base + 8),)))
            wbase = l * W256_ROWS_PER_LAYER
            update(4, (slice(l, l + 1),), total(w256_ref, (slice(wbase + W256_CONV_B, wbase + W256_CONV_B + 1),)))
            update(5, (slice(l, l + 1),), total(w256_ref, (slice(wbase + W256_GN, wbase + W256_GN + 1),)))
            update(6, (l,), total(sw_refs[l], ()))
            update(7, (l,), total(sb_ref, (slice(l * SG_HEADS, (l + 1) * SG_HEADS),)))
            taps_ref[l] = total(w256_ref, (slice(wbase + W256_TAPS, wbase + W256_TAPS + 8),))
        update(3, (slice(0, 1),), total(vec_ref, (slice(VEC_FINAL_ROW, VEC_FINAL_ROW + 1),)))
        loss_ref[...] = total(vec_ref, (slice(VEC_FINAL_ROW + 1, VEC_FINAL_ROW + 2), slice(0, LANES)))

    out_shape = []
    for w, _, _ in params:
        out_shape += [jax.ShapeDtypeStruct(w.shape, F32)] * 4
    out_shape += [jax.ShapeDtypeStruct((1, LANES), F32), jax.ShapeDtypeStruct((L, 8, CW), F32)]
    outs = pl.pallas_call(body, name=name, out_shape=out_shape, compiler_params=_cparams())(
        vec_all, w256_all, sb_all, *sw_all, *[a for p in params for a in p])
    return [outs[4 * k:4 * k + 4] for k in range(n_par)], outs[4 * n_par:]


def adamw_plain(w, g, m, v, tr, name):
    rows, cols = w.shape
    spec = pl.BlockSpec((tr, cols), lambda i: (i, 0))

    def body(w_ref, g_ref, m_ref, v_ref, d_ref, nm_ref, nv_ref):
        delta, nm, nv = _adamw(w_ref[...], g_ref[...], m_ref[...], v_ref[...])
        d_ref[...] = delta
        nm_ref[...] = nm
        nv_ref[...] = nv

    shp = jax.ShapeDtypeStruct((rows, cols), F32)
    return pl.pallas_call(
        body, name=name, grid=(rows // tr,), in_specs=[spec] * 4, out_specs=[spec] * 3,
        out_shape=[shp, shp, shp], compiler_params=_cparams(("parallel",)),
    )(w, g, m, v)


def adamw_reduce(w, parts, m, v, tr, name, tie=None):
    _, rows, cols = w.shape
    spec = pl.BlockSpec((None, tr, cols), lambda l, i: (l, i, 0))
    pspecs = [pl.BlockSpec((NDEV, tr, cols), lambda l, i, k=k: (0, jnp.where(l == k, i, 0), 0)) for k in range(L)]

    ties = [] if tie is None else [tie]

    def body(w_ref, p0_ref, p1_ref, m_ref, v_ref, *rest):
        g_ref, d_ref, nm_ref, nv_ref = rest[len(ties):]
        first_layer = pl.program_id(0) == 0
        g = jnp.zeros((tr, cols), F32)
        for d in range(NDEV):
            g = g + jnp.where(first_layer, p0_ref[d], p1_ref[d]).astype(F32)
        delta, nm, nv = _adamw(w_ref[...], g, m_ref[...], v_ref[...])
        g_ref[...] = g
        d_ref[...] = delta
        nm_ref[...] = nm
        nv_ref[...] = nv

    shp = jax.ShapeDtypeStruct(w.shape, F32)
    return pl.pallas_call(
        body, name=name, grid=(L, rows // tr),
        in_specs=[spec] + pspecs + [spec, spec] + [pl.BlockSpec(t.shape, lambda l, i: (0, 0)) for t in ties],
        out_specs=[spec] * 4, out_shape=[shp] * 4, compiler_params=_cparams(("parallel", "parallel")),
    )(w, *parts, m, v, *ties)


SHARD_IN = PROJ // NDEV


def shards_to_columns(shards, name):
    tr = 256

    def body(i_ref, o_ref):
        for d in range(NDEV):
            o_ref[:, d * SHARD_IN:(d + 1) * SHARD_IN] = i_ref[d]

    return pl.pallas_call(
        body, name=name, grid=(D // tr,),
        in_specs=[pl.BlockSpec((NDEV, tr, SHARD_IN), lambda i: (0, i, 0))],
        out_specs=pl.BlockSpec((tr, PROJ), lambda i: (i, 0)),
        out_shape=jax.ShapeDtypeStruct((D, PROJ), shards.dtype), compiler_params=_cparams(("parallel",)),
    )(shards)


def columns_to_shards(mat, name):
    tr = 256

    def body(i_ref, o_ref):
        for d in range(NDEV):
            o_ref[d] = i_ref[:, d * SHARD_IN:(d + 1) * SHARD_IN]

    return pl.pallas_call(
        body, name=name, grid=(D // tr,),
        in_specs=[pl.BlockSpec((tr, PROJ), lambda i: (i, 0))],
        out_specs=pl.BlockSpec((NDEV, tr, SHARD_IN), lambda i: (0, i, 0)),
        out_shape=jax.ShapeDtypeStruct((NDEV, D, SHARD_IN), mat.dtype), compiler_params=_cparams(("parallel",)),
    )(mat)


def _pad_rows(flat, rows):
    return jnp.pad(flat, (0, rows * LANES - flat.shape[0])).reshape(rows, LANES)


def kernel(x, c, ada_w, ada_b, norm_mix_g, norm_mlp_g, w_in, conv_w, conv_b, gmlp_norm_g, spatial_w, spatial_b, w_out, mlp_w1, mlp_w2, final_norm_g, loss_target, m_ada_w, m_ada_b, m_norm_mix_g, m_norm_mlp_g, m_w_in, m_conv_w, m_conv_b, m_gmlp_norm_g, m_spatial_w, m_spatial_b, m_w_out, m_mlp_w1, m_mlp_w2, m_final_norm_g, v_ada_w, v_ada_b, v_norm_mix_g, v_norm_mlp_g, v_w_in, v_conv_w, v_conv_b, v_gmlp_norm_g, v_spatial_w, v_spatial_b, v_w_out, v_mlp_w1, v_mlp_w2, v_final_norm_g):
    me = _lin(_my_pos())
    x0 = x[0]
    target = loss_target[0]
    conv_shard = conv_w.shape[-1]

    w_in_b, w_out_b, w1_b, w2_b = [w.astype(BF16) for w in (w_in, w_out, mlp_w1, mlp_w2)]
    pack0 = _pad_rows(jnp.concatenate([c.reshape(-1), conv_w.reshape(-1)]), 16)
    g0, gw_in0 = run_comm(Gather([pack0, w_in_b[0]]), "gather_first")
    g0 = g0.reshape(NDEV, 16 * LANES)
    c_all = g0[:, :D]
    conv_full = (g0[:, D:D + L * 3 * conv_shard].reshape(NDEV, L, 3, conv_shard)
                 .transpose(1, 2, 0, 3).reshape(L, 3, CW))


    W_in = [shards_to_columns(gw_in0, "w_in_columns0"), None]
    W_out, W1, W2 = [None] * L, [None] * L, [None] * L

    ada_b_mine = lax.dynamic_slice(ada_b, (0, me * ADA_COLS), (L, ADA_COLS)).reshape(L, 1, ADA_COLS)
    mod_part, c_act = ada_fwd(c_all, ada_w, ada_b_mine, "ada_fwd")
    gmod = run_comm(Gather([mod_part]), "gather_mod")[0]
    mod = lax.dynamic_index_in_dim(gmod, me, axis=2, keepdims=False)
    mod = mod.transpose(1, 0, 2).reshape(L, NMOD, 1, D)
    early_weights, token = start_copies([w_out_b[0]], me, "gather_early0_start", True, after=gmod)
    mod = tied(mod, token)

    cw8 = jnp.pad(conv_full, ((0, 0), (0, 5), (0, 0)))
    sg_bias = jnp.repeat(spatial_b.transpose(0, 2, 1), HD, axis=2)

    saved = []
    xl = x0
    for l in range(L):
        sh_m, sc_m, g_m, sh_f, sc_f, g_f = [mod[l, k] for k in range(NMOD)]
        h1 = normmod_fwd(xl, norm_mix_g[l:l + 1], sc_m, sh_m, f"norm_mix_fwd{l}")
        if l > 0:
            gw_in, gw_out = finish_copies(early_weights, xl, f"gather_early{l}_wait")
            W_in[l] = shards_to_columns(gw_in, f"w_in_columns{l}")
        qkv = mm_layer("proj_qkv", l, h1, W_in[l], out_dtypes=[BF16], cols=(0, QKV))[0]
        proj = mm_layer("proj_rest", l, h1, W_in[l], out_dtypes=[F32], cols=(QKV, REST))[0]
        a_out, a_tot, gw2, gw1 = attn_fwd(qkv, f"attn_fwd{l}", comm=Gather([w2_b[l], w1_b[l]]))
        if l == 0:
            gw_out, = finish_copies(early_weights, a_out, f"gather_early{l}_wait")
        W_out[l] = gw_out.reshape(D, D)
        W1[l] = gw1
        W2[l] = gw2.reshape(DFF, D)
        if l + 1 < L:
            early_weights, token = start_copies([w_in_b[l + 1], w_out_b[l + 1]], me, f"gather_early{l + 1}_start", True,
                                                after=a_out)
            g_m = tied(g_m, token)
        c_out = conv_fwd(proj, cw8[l], conv_b[l:l + 1], f"conv_fwd{l}")
        s_out = sg_fwd(proj, gmlp_norm_g[l:l + 1], spatial_w[l], sg_bias[l], f"sg_fwd{l}")
        cat = jnp.concatenate([a_out, c_out.astype(BF16), s_out.astype(BF16)], axis=1)
        mix, x1, h2 = mm_layer("mix", l, cat, W_out[l], out_dtypes=[F32, F32, BF16], epilogue=_residual_then_norm,
                               extras=[(xl, "tile"), (g_m, "col"), (norm_mlp_g[l:l + 1], "col"), (sc_f, "col"),
                                       (sh_f, "col")])
        ra, r = mm_layer("mlp_up", l, h2, W1[l], out_dtypes=[BF16, BF16], b_blocks=True,
                         epilogue=lambda acc: (jnp.maximum(acc, 0.0), jnp.square(jnp.maximum(acc, 0.0))))
        m2, x2 = mm_layer("mlp_down", l, r, W2[l], out_dtypes=[F32, F32],
                          epilogue=lambda acc, xr, g: (acc, xr + g * acc), extras=[(x1, "tile"), (g_f, "col")])
        saved.append(dict(x=xl, h1=h1, proj=proj, qkv=qkv, a_tot=a_tot, cat=cat, mix=mix,
                          x1=x1, h2=h2, ra=ra, r=r, m2=m2))
        xl = x2

    dx, loss_part, d_final_g, dm2, dg_f = loss_head(xl, target, final_norm_g.reshape(1, D),
                                                    (saved[L - 1]["m2"], mod[L - 1, NMOD - 1]), "loss_head")

    p_in, p_out, p_w1, p_w2 = [None] * L, [None] * L, [None] * L, [None] * L
    grads_in_flight = [None] * L
    vec_rows, d_norm_mix, d_norm_mlp = [None] * L, [None] * L, [None] * L
    dcw8, d_conv_b, d_gn, d_sw, d_sb = [None] * L, [None] * L, [None] * L, [None] * L, [None] * L
    for l in reversed(range(L)):
        sv = saved[l]
        sh_m, sc_m, g_m, sh_f, sc_f, g_f = [mod[l, k] for k in range(NMOD)]
        da = mm_layer("mlp_down_dgrad", l, dm2, W2[l], out_dtypes=[BF16], trans_b=True,
                      epilogue=lambda acc, rav: (acc * (2.0 * rav.astype(F32)),), extras=[(sv["ra"], "tile")])[0]
        dW2 = mm_layer("mlp_down_wgrad", l, sv["r"], dm2, out_dtypes=[BF16], trans_a=True)[0]
        dW1 = mm_layer("mlp_up_wgrad", l, sv["h2"], da, out_dtypes=[BF16], trans_a=True, out_blocks=True)[0]
        dh2 = mm_layer("mlp_up_dgrad", l, da, W1[l], out_dtypes=[F32], trans_b=True, b_blocks=True)[0]
        dx1, dsc_f, dsh_f, d_norm_mlp[l], dmix, dg_m = normmod_bwd(
            sv["x1"], dh2, dx, norm_mlp_g[l:l + 1], sc_f, f"norm_mlp_bwd{l}", gate_next=(sv["mix"], g_m))
        dcat = mm_layer("mix_dgrad", l, dmix, W_out[l], out_dtypes=[F32], trans_b=True)[0]
        dW_out = mm_layer("mix_wgrad", l, sv["cat"], dmix, out_dtypes=[BF16], trans_a=True)[0]
        pieces_w2, pieces_out = dW2.reshape(NDEV, DFF // NDEV, D), dW_out.reshape(NDEV, D // NDEV, D)
        ride, late = ([pieces_w2, pieces_out], dW1) if l == L - 1 else ([pieces_w2, dW1], pieces_out)
        dq, dk, dv, *arrived = attn_bwd(sv["qkv"], dcat, sv["a_tot"], f"attn_bwd{l}", comm=Exchange(ride))
        p_w2[l] = arrived[0]
        (p_out if l == L - 1 else p_w1)[l] = arrived[1]
        dbg, dcg, dhc, dcw8[l], d_conv_b[l] = conv_bwd(sv["proj"], dcat, cw8[l], conv_b[l:l + 1], f"conv_bwd{l}")
        dus, dvs, d_gn[l], dsw, dbias = sg_bwd(sv["proj"], dcat, gmlp_norm_g[l:l + 1], spatial_w[l], sg_bias[l],
                                               f"sg_bwd{l}")
        d_sw[l] = dsw.astype(BF16)
        d_sb[l] = dbias.reshape(T, SG_HEADS, HD).sum(axis=2).T
        dproj = jnp.concatenate([dq, dk, dv, dbg, dcg, dhc, dus, dvs], axis=1).astype(BF16)
        dW_in = mm_layer("proj_wgrad", l, sv["h1"], dproj, out_dtypes=[BF16], trans_a=True)[0]
        pieces = columns_to_shards(dW_in, f"w_in_grad_shards{l}")
        grads_in_flight[l], token = start_copies([late, pieces], me, f"exchange_tail{l}_start", False)
        dh1 = mm_layer("proj_dgrad", l, dproj, W_in[l], out_dtypes=[F32], trans_b=True, extras=[(token, "tie")])[0]
        below = (saved[l - 1]["m2"], mod[l - 1, NMOD - 1]) if l > 0 else None
        dx, dsc_m, dsh_m, d_norm_mix[l], *gated_below = normmod_bwd(
            sv["x"], dh1, dx1, tied(norm_mix_g[l:l + 1], token), sc_m, f"norm_mix_bwd{l}", gate_next=below)
        vec_rows[l] = [dsh_m, dsc_m, dg_m, dsh_f, dsc_f, dg_f, d_norm_mix[l], d_norm_mlp[l]]
        if l > 0:
            dm2, dg_f = gated_below

    grad_x = dx.reshape(1, S, D)

    g_w2, d_w2, nm_w2, nv_w2 = adamw_reduce(mlp_w2, p_w2, m_mlp_w2, v_mlp_w2, 256, "adamw_mlp_w2", tie=token)
    p_w1[L - 1], p_in[L - 1] = finish_copies(grads_in_flight[L - 1], d_w2, f"exchange_tail{L - 1}_wait")
    g_w1, d_w1, nm_w1, nv_w1 = adamw_reduce(mlp_w1, p_w1, m_mlp_w1, v_mlp_w1, 256, "adamw_mlp_w1", tie=token)

    vec_pack = jnp.concatenate([row for l in range(L) for row in vec_rows[l]]
                               + [d_final_g, loss_part, jnp.zeros((VEC_ROWS - VEC_FINAL_ROW - 2, D), F32)], axis=0)
    vec_pack, _ = lax.optimization_barrier((vec_pack, (d_w1, d_w2)))
    w256_pack = jnp.concatenate([blk for l in range(L) for blk in (
        dcw8[l], d_conv_b[l], d_gn[l], jnp.zeros((W256_ROWS_PER_LAYER - W256_GN - 1, CW), F32))], axis=0)
    vec_all, w256_all, sb_all, *sw_all = run_comm(
        Gather([vec_pack, w256_pack, jnp.concatenate(d_sb, axis=0)] + d_sw), "gather_small_grads")

    dmod_all = (vec_all[:, :VEC_FINAL_ROW].reshape(NDEV, L, VEC_ROWS_PER_LAYER, D)[:, :, :NMOD]
                .reshape(NDEV, L, NMOD * D))
    dmod_cols = lax.dynamic_slice(dmod_all, (0, 0, me * ADA_COLS), (NDEV, L, ADA_COLS)).transpose(1, 0, 2)
    g_ada_w = ada_bwd(c_act, dmod_cols, "ada_bwd")

    flat2 = lambda t: t.reshape(L * D, ADA_COLS)
    d_ada_w, nm_ada_w, nv_ada_w = [t.reshape(L, D, ADA_COLS) for t in adamw_plain(
        flat2(ada_w), flat2(g_ada_w), flat2(m_ada_w), flat2(v_ada_w), 256, "adamw_ada_w")]

    after = jnp.concatenate([t.reshape(-1)[:1] for t in (d_w1, d_w2, d_ada_w)])
    p_out[0], p_in[0] = finish_copies(grads_in_flight[0], after, "exchange_tail0_wait")
    g_w_in, d_w_in, nm_w_in, nv_w_in = adamw_reduce(w_in, p_in, m_w_in, v_w_in, 256, "adamw_w_in")
    g_w_out, d_w_out, nm_w_out, nv_w_out = adamw_reduce(w_out, p_out, m_w_out, v_w_out, 128, "adamw_w_out")

    as_row = lambda t: t.reshape(1, D)
    small_params = [(ada_b, m_ada_b, v_ada_b), (norm_mix_g, m_norm_mix_g, v_norm_mix_g),
                    (norm_mlp_g, m_norm_mlp_g, v_norm_mlp_g),
                    (as_row(final_norm_g), as_row(m_final_norm_g), as_row(v_final_norm_g)),
                    (conv_b, m_conv_b, v_conv_b), (gmlp_norm_g, m_gmlp_norm_g, v_gmlp_norm_g),
                    (spatial_w, m_spatial_w, v_spatial_w), (spatial_b, m_spatial_b, v_spatial_b)]
    updated, (loss_sum, taps_sum) = small_update(vec_all, w256_all, sb_all, sw_all, small_params, "small_update")
    loss = loss_sum[0, 0]
    u_ada_b, u_norm_mix, u_norm_mlp, u_final, u_conv_b, u_gn, u_sw, u_sb = updated
    u_final = [t.reshape(D) for t in u_final]
    g_conv_w = lax.dynamic_slice(taps_sum, (0, 0, me * conv_shard), (L, 3, conv_shard))
    flat_cw = lambda t: t.reshape(L * 3, conv_shard)
    u_conv_w = [g_conv_w] + [t.reshape(L, 3, conv_shard) for t in adamw_plain(
        flat_cw(conv_w), flat_cw(g_conv_w), flat_cw(m_conv_w), flat_cw(v_conv_w), L * 3, "adamw_conv_w")]
    small_sets = [u_ada_b, u_norm_mix, u_norm_mlp, u_conv_w, u_conv_b, u_gn, u_sw, u_sb, u_final]
    small_g, sd, snm, snv = [[u[k] for u in small_sets] for k in range(4)]

    def ordered(big, small):
        ada, win, wout, w1, w2 = big
        return [ada, small[0], small[1], small[2], win, small[3], small[4], small[5], small[6], small[7],
                wout, w1, w2, small[8]]

    grads = ordered([g_ada_w, g_w_in, g_w_out, g_w1, g_w2], small_g)
    deltas = ordered([d_ada_w, d_w_in, d_w_out, d_w1, d_w2], sd)
    new_m = ordered([nm_ada_w, nm_w_in, nm_w_out, nm_w1, nm_w2], snm)
    new_v = ordered([nv_ada_w, nv_w_in, nv_w_out, nv_w1, nv_w2], snv)
    return (loss, grad_x, *grads, *deltas, *new_m, *new_v)
```

```python
import functools
import math

import jax
import jax.numpy as jnp
from jax import lax
from jax.experimental import pallas as pl
from jax.experimental.pallas import tpu as pltpu

F32 = jnp.float32
BF16 = jnp.bfloat16
MESH = pl.DeviceIdType.MESH

S = 2048
D = 1024
L = 2
NDEV = 8
HD = 64
NH = 8
PROJ = 2816
DFF = 4096
NMOD = 6
EPS = 1e-6
T = 128
SG_HEADS = 4
LANES = 128
CW = 256
QKV = 3 * NH * HD
REST = PROJ - QKV

LR, B1, B2, AEPS, WD, STEP = 0.001, 0.9, 0.999, 1e-08, 0.01, 10
BC1 = 1.0 - B1 ** STEP
BC2 = 1.0 - B2 ** STEP

VMEM_LIMIT = 48 * 1024 * 1024

HBM_SPEC = pl.BlockSpec(memory_space=pltpu.HBM)


def _cparams(sem=None):
    return pltpu.CompilerParams(dimension_semantics=sem, vmem_limit_bytes=VMEM_LIMIT)


def _my_pos():
    return lax.axis_index("x"), lax.axis_index("y"), lax.axis_index("c")


def _lin(p):
    return 4 * p[0] + 2 * p[1] + p[2]


class Gather:
    def __init__(self, arrs):
        self.arrs = list(arrs)
        n = len(self.arrs)
        self.out_shape = [jax.ShapeDtypeStruct((NDEV,) + a.shape, a.dtype) for a in self.arrs]
        self.scratch = [pltpu.SemaphoreType.DMA((n, 7)), pltpu.SemaphoreType.DMA((n, 7)),
                        pltpu.SemaphoreType.DMA((n,))]

    def phases(self, ins, outs, sems):
        n = len(self.arrs)
        send_sems, recv_sems, local_sems = sems
        x, y, c = _my_pos()
        me, sibling = (x, y, c), (x, y, 1 - c)
        chips = [(1 - x, y), (x, 1 - y), (1 - x, 1 - y)]

        def copy(a, k, block, to, src=None):
            slot = outs[a].at[_lin(block)]
            return pltpu.make_async_remote_copy(
                src_ref=slot if src is None else src, dst_ref=slot,
                send_sem=send_sems.at[a, k], recv_sem=recv_sems.at[a, k],
                device_id=to, device_id_type=MESH)

        def mine(a):
            return pltpu.make_async_copy(ins[a], outs[a].at[_lin(me)], local_sems.at[a])

        def first(a):
            return [copy(a, 0, me, sibling, src=ins[a])] + [
                copy(a, 1 + j, me, (*chip, c), src=ins[a]) for j, chip in enumerate(chips)]

        def passed(a):
            return [copy(a, 4 + j, (*chip, c), sibling) for j, chip in enumerate(chips)]

        def start():
            for a in range(n):
                mine(a).start()
                for cp in first(a):
                    cp.start()

        def relay():
            for j, chip in enumerate(chips):
                for a in range(n):
                    copy(a, 1 + j, (*chip, c), me).wait_recv()
                    passed(a)[j].start()

        def finish():
            for a in range(n):
                copy(a, 0, sibling, me).wait_recv()
            for j, chip in enumerate(chips):
                for a in range(n):
                    copy(a, 4 + j, (*chip, 1 - c), me).wait_recv()
            for a in range(n):
                for cp in first(a) + passed(a):
                    cp.wait_send()
                mine(a).wait()

        return start, relay, finish


class Exchange:
    def __init__(self, arrs):
        self.arrs = list(arrs)
        n = len(self.arrs)
        self.out_shape = [jax.ShapeDtypeStruct(a.shape, a.dtype) for a in self.arrs]
        self.scratch = [pltpu.SemaphoreType.DMA((n, 7)), pltpu.SemaphoreType.DMA((n, 7)),
                        pltpu.SemaphoreType.DMA((n,))]

    def phases(self, ins, outs, sems):
        n = len(self.arrs)
        send_sems, recv_sems, local_sems = sems
        x, y, c = _my_pos()
        me = (x, y, c)

        def peer(mask):
            return (1 - x if mask & 4 else x, 1 - y if mask & 2 else y, 1 - c if mask & 1 else c)

        def copy(a, mask):
            return pltpu.make_async_remote_copy(
                src_ref=ins[a].at[_lin(peer(mask))], dst_ref=outs[a].at[_lin(me)],
                send_sem=send_sems.at[a, mask - 1], recv_sem=recv_sems.at[a, mask - 1],
                device_id=peer(mask), device_id_type=MESH)

        def arrival(a, mask):
            return pltpu.make_async_remote_copy(
                src_ref=ins[a].at[_lin(me)], dst_ref=outs[a].at[_lin(peer(mask))],
                send_sem=send_sems.at[a, mask - 1], recv_sem=recv_sems.at[a, mask - 1],
                device_id=peer(mask), device_id_type=MESH)

        def mine(a):
            return pltpu.make_async_copy(ins[a].at[_lin(me)], outs[a].at[_lin(me)], local_sems.at[a])

        def start():
            for a in range(n):
                mine(a).start()
            for mask in (4, 2, 6, 1, 5, 3, 7):
                for a in range(n):
                    copy(a, mask).start()

        def relay():
            pass

        def finish():
            for mask in range(1, 8):
                for a in range(n):
                    arrival(a, mask).wait_recv()
            for mask in range(1, 8):
                for a in range(n):
                    copy(a, mask).wait_send()
            for a in range(n):
                mine(a).wait()

        return start, relay, finish


def run_comm(plan, name):
    n = len(plan.arrs)

    def body(*refs):
        start, relay, finish = plan.phases(refs[:n], refs[n:2 * n], refs[2 * n:])
        start()
        relay()
        finish()

    outs = pl.pallas_call(
        body, name=name, out_shape=plan.out_shape,
        in_specs=[HBM_SPEC] * n, out_specs=[HBM_SPEC] * n, scratch_shapes=plan.scratch,
    )(*plan.arrs)
    return list(outs)


SEM_SPEC = pl.BlockSpec(memory_space=pltpu.SEMAPHORE)
DATAFLOW = pltpu.SideEffectType.DATAFLOW_SIDE_EFFECTING


def _peer_copies(src_ref, land_ref, send_sems, recv_sems, first, same_block):
    x, y, c = _my_pos()
    me = (x, y, c)
    sends, arrivals = [], []
    for mask in (4, 2, 6, 1, 5, 3, 7):
        peer = (1 - x if mask & 4 else x, 1 - y if mask & 2 else y, 1 - c if mask & 1 else c)
        sends.append(pltpu.make_async_remote_copy(
            src_ref=src_ref if same_block else src_ref.at[_lin(peer)], dst_ref=land_ref.at[_lin(me)],
            send_sem=send_sems.at[first + mask - 1], recv_sem=recv_sems.at[first + mask - 1], device_id=peer,
            device_id_type=MESH))
        arrivals.append(pltpu.make_async_remote_copy(
            src_ref=src_ref if same_block else src_ref.at[_lin(me)], dst_ref=land_ref.at[_lin(peer)],
            send_sem=send_sems.at[first + mask - 1], recv_sem=recv_sems.at[first + mask - 1], device_id=peer,
            device_id_type=MESH))
    return sends, arrivals


def start_copies(srcs, me, name, same_block, after=None):
    n = len(srcs)
    landings = []
    for src in srcs:
        own = src[None] if same_block else lax.dynamic_index_in_dim(src, me, axis=0, keepdims=True)
        landings.append(lax.dynamic_update_slice(lax.empty((NDEV,) + own.shape[1:], src.dtype), own,
                                                 (me,) + (0,) * (own.ndim - 1)))

    def body(*refs):
        send_sems, recv_sems = refs[-2 * n - 3], refs[-2 * n - 2]
        token = refs[-1]
        for k in range(n):
            sends, _ = _peer_copies(refs[2 * k], refs[2 * k + 1], send_sems, recv_sems, 7 * k, same_block)
            for cp in sends:
                cp.start()
        token[...] = jnp.zeros_like(token)

    hbm = lambda a: pltpu.HBM(a.shape, a.dtype)
    pairs = [a for pair in zip(srcs, landings) for a in pair]
    extra = [] if after is None else [after]
    sems = pltpu.SemaphoreType.DMA((7 * n,))
    send_sems, recv_sems, *thru, token = pl.pallas_call(
        body, name=name,
        out_shape=(sems, sems, *[hbm(a) for a in pairs], jax.ShapeDtypeStruct((8, LANES), F32)),
        in_specs=[HBM_SPEC] * (2 * n) + [pl.BlockSpec(memory_space=pl.ANY)] * len(extra),
        out_specs=(SEM_SPEC, SEM_SPEC, *[HBM_SPEC] * (2 * n), pl.BlockSpec(memory_space=pltpu.VMEM)),
        input_output_aliases={k: 2 + k for k in range(2 * n)},
        compiler_params=pltpu.CompilerParams(has_side_effects=DATAFLOW),
    )(*[pltpu.with_memory_space_constraint(a, pltpu.HBM) for a in pairs], *extra)
    return (send_sems, recv_sems, thru, same_block), token


def finish_copies(handle, after, name):
    send_sems, recv_sems, thru, same_block = handle
    n = len(thru) // 2

    def body(*refs):
        send_sems, recv_sems = refs[2 * n], refs[2 * n + 1]
        for k in range(n):
            sends, arrivals = _peer_copies(refs[2 * k], refs[2 * k + 1], send_sems, recv_sems, 7 * k, same_block)
            for cp in sends:
                cp.wait_send()
            for cp in arrivals:
                cp.wait_recv()

    hbm = lambda a: pltpu.HBM(a.shape, a.dtype)
    outs = pl.pallas_call(
        body, name=name, out_shape=tuple(hbm(a) for a in thru),
        in_specs=[HBM_SPEC] * (2 * n) + [SEM_SPEC, SEM_SPEC, pl.BlockSpec(memory_space=pl.ANY)],
        out_specs=tuple([HBM_SPEC] * (2 * n)), input_output_aliases={k: k for k in range(2 * n)},
        compiler_params=pltpu.CompilerParams(has_side_effects=DATAFLOW),
    )(*thru, send_sems, recv_sems, after)
    return [outs[2 * k + 1] for k in range(n)]


def tied(x, token):
    return x + token[0:1, 0:1].astype(x.dtype)


MM_TILES = {
    "proj_qkv": (S, 512), "proj_rest": (S, 256), "mix": (512, D), "mlp_up": (S, 512), "mlp_down": (1024, 256),
    "mlp_down_dgrad": (S, 1024), "mlp_down_wgrad": (2048, 1024), "mlp_up_wgrad": (1024, 512),
    "mlp_up_dgrad": (1024, 512), "mix_dgrad": (S, 512), "mix_wgrad": (1024, 1024),
    "proj_wgrad": (1024, PROJ // 2), "proj_dgrad": (1024, 512),
}


def mm_layer(kind, l, a, b, **kw):
    tm, tn = MM_TILES[kind]
    return mm(a, b, tm=tm, tn=tn, name=f"{kind}{l}", **kw)


def mm(a, b, *, tm, tn, out_dtypes, epilogue=None, extras=(), name, trans_a=False, trans_b=False,
       cols=None, b_blocks=False, out_blocks=False):
    if trans_a:
        kdim, m = a.shape
    else:
        m, kdim = a.shape
    shard = b.shape[-1] if b_blocks else None
    if b_blocks:
        full = (b.shape[1], NDEV * shard)
    else:
        full = b.shape
    first, ncols = cols if cols is not None else (0, full[0] if trans_b else full[1])
    assert full[1 if trans_b else 0] == kdim and m % tm == 0 and ncols % tn == 0 and first % tn == 0
    j0 = first // tn
    if trans_a:
        a_spec = pl.BlockSpec((kdim, tm), lambda i, j: (0, i))
    else:
        a_spec = pl.BlockSpec((tm, kdim), lambda i, j: (i, 0))
    if b_blocks and trans_b:
        b_spec = pl.BlockSpec((NDEV, tn, shard), lambda i, j: (0, j0 + j, 0))
    elif b_blocks:
        assert tn == shard
        b_spec = pl.BlockSpec((None, kdim, tn), lambda i, j: (j0 + j, 0, 0))
    elif trans_b:
        b_spec = pl.BlockSpec((tn, kdim), lambda i, j: (j0 + j, 0))
    else:
        b_spec = pl.BlockSpec((kdim, tn), lambda i, j: (0, j0 + j))
    if out_blocks:
        assert tn * NDEV == ncols
        out_spec = pl.BlockSpec((None, tm, tn), lambda i, j: (j, i, 0))
        out_dims = (NDEV, m, tn)
    else:
        out_spec = pl.BlockSpec((tm, tn), lambda i, j: (i, j))
        out_dims = (m, ncols)
    ex_specs = []
    for arr, kind in extras:
        if kind == "tile":
            ex_specs.append(pl.BlockSpec((tm, tn), lambda i, j: (i, j)))
        elif kind == "col":
            ex_specs.append(pl.BlockSpec((1, tn), lambda i, j: (0, j)))
        else:
            ex_specs.append(pl.BlockSpec(arr.shape, lambda i, j: (0, 0)))
    n_ex, n_out = len(extras), len(out_dtypes)
    used = [k for k, (_, kind) in enumerate(extras) if kind != "tie"]

    def body(a_ref, b_ref, *rest):
        ex_refs, out_refs = rest[:n_ex], rest[n_ex:]
        if trans_a:
            acc = lax.dot_general(a_ref[...], b_ref[...], (((0,), (0,)), ((), ())),
                                  preferred_element_type=F32)
        elif trans_b and b_blocks:
            acc = jnp.zeros((tm, tn), F32)
            for d in range(NDEV):
                acc = acc + lax.dot_general(a_ref[:, d * shard:(d + 1) * shard], b_ref[d],
                                            (((1,), (1,)), ((), ())), preferred_element_type=F32)
        elif trans_b:
            acc = lax.dot_general(a_ref[...], b_ref[...], (((1,), (1,)), ((), ())),
                                  preferred_element_type=F32)
        else:
            acc = jnp.dot(a_ref[...], b_ref[...], preferred_element_type=F32)
        outs = (acc,) if epilogue is None else epilogue(acc, *[ex_refs[k][...] for k in used])
        for o_ref, val in zip(out_refs, outs):
            o_ref[...] = val.astype(o_ref.dtype)

    outs = pl.pallas_call(
        body, name=name, grid=(m // tm, ncols // tn),
        in_specs=[a_spec, b_spec] + ex_specs,
        out_specs=[out_spec for _ in range(n_out)],
        out_shape=[jax.ShapeDtypeStruct(out_dims, dt) for dt in out_dtypes],
        compiler_params=_cparams(("parallel", "parallel")),
    )(a, b, *[arr for arr, _ in extras])
    return list(outs)


TR = 512

ROW_SPEC = pl.BlockSpec((TR, D), lambda i: (i, 0))
VEC_SPEC = pl.BlockSpec((1, D), lambda i: (0, 0))


def _residual_then_norm(acc, xr, gate, g, sc, sh):
    x_new = xr + gate * acc
    rstd = lax.rsqrt(jnp.mean(x_new * x_new, axis=-1, keepdims=True) + EPS)
    return acc, x_new, ((x_new * rstd) * g) * (1.0 + sc) + sh


def normmod_fwd(x, g, sc, sh, name):
    def body(x_ref, g_ref, sc_ref, sh_ref, o_ref):
        xv = x_ref[...]
        rstd = lax.rsqrt(jnp.mean(xv * xv, axis=-1, keepdims=True) + EPS)
        n = (xv * rstd) * g_ref[...]
        o_ref[...] = (n * (1.0 + sc_ref[...]) + sh_ref[...]).astype(o_ref.dtype)

    return pl.pallas_call(
        body, name=name, grid=(S // TR,),
        in_specs=[ROW_SPEC, VEC_SPEC, VEC_SPEC, VEC_SPEC], out_specs=ROW_SPEC,
        out_shape=jax.ShapeDtypeStruct((S, D), BF16),
        compiler_params=_cparams(("parallel",)),
    )(x, g, sc, sh)


def _gate_next(dxv, refs):
    br_ref, gate_ref, dbr_ref, dgate_ref = refs

    @pl.when(pl.program_id(0) == 0)
    def _():
        dgate_ref[...] = jnp.zeros_like(dgate_ref)

    dbr_ref[...] = (dxv * gate_ref[...]).astype(dbr_ref.dtype)
    dgate_ref[...] += jnp.sum(dxv * br_ref[...], axis=0, keepdims=True)


GATE_NEXT_IN = [ROW_SPEC, VEC_SPEC]
GATE_NEXT_OUT = [ROW_SPEC, VEC_SPEC]
GATE_NEXT_SHAPES = [jax.ShapeDtypeStruct((S, D), BF16), jax.ShapeDtypeStruct((1, D), F32)]


def normmod_bwd(x, dh, dres, g, sc, name, gate_next=None):
    nxt = 2 if gate_next else 0

    def body(x_ref, dh_ref, dres_ref, g_ref, sc_ref, *rest):
        nxt_in, (dx_ref, dsc_ref, dsh_ref, dg_ref), nxt_out = rest[:nxt], rest[nxt:nxt + 4], rest[nxt + 4:]

        @pl.when(pl.program_id(0) == 0)
        def _():
            dsc_ref[...] = jnp.zeros_like(dsc_ref)
            dsh_ref[...] = jnp.zeros_like(dsh_ref)
            dg_ref[...] = jnp.zeros_like(dg_ref)

        xv, dh = x_ref[...], dh_ref[...]
        gv = g_ref[...]
        rstd = lax.rsqrt(jnp.mean(xv * xv, axis=-1, keepdims=True) + EPS)
        xhat = xv * rstd
        dn = dh * (1.0 + sc_ref[...])
        dxhat = dn * gv
        dxv = dres_ref[...] + rstd * (dxhat - xhat * jnp.mean(dxhat * xhat, axis=-1, keepdims=True))
        dx_ref[...] = dxv
        dsc_ref[...] += jnp.sum(dh * (xhat * gv), axis=0, keepdims=True)
        dsh_ref[...] += jnp.sum(dh, axis=0, keepdims=True)
        dg_ref[...] += jnp.sum(dn * xhat, axis=0, keepdims=True)
        if gate_next:
            _gate_next(dxv, nxt_in + nxt_out)

    vec_out = jax.ShapeDtypeStruct((1, D), F32)
    on = bool(gate_next)
    return pl.pallas_call(
        body, name=name, grid=(S // TR,),
        in_specs=[ROW_SPEC, ROW_SPEC, ROW_SPEC, VEC_SPEC, VEC_SPEC] + GATE_NEXT_IN * on,
        out_specs=[ROW_SPEC, VEC_SPEC, VEC_SPEC, VEC_SPEC] + GATE_NEXT_OUT * on,
        out_shape=[jax.ShapeDtypeStruct((S, D), F32), vec_out, vec_out, vec_out] + GATE_NEXT_SHAPES * on,
        compiler_params=_cparams(("arbitrary",)),
    )(x, dh, dres, g, sc, *(gate_next or ()))


def loss_head(x, target, g, gate_next, name):
    def body(x_ref, t_ref, g_ref, br_ref, gate_ref, dx_ref, loss_ref, dg_ref, dbr_ref, dgate_ref):
        @pl.when(pl.program_id(0) == 0)
        def _():
            loss_ref[...] = jnp.zeros_like(loss_ref)
            dg_ref[...] = jnp.zeros_like(dg_ref)

        xv, gv = x_ref[...], g_ref[...]
        rstd = lax.rsqrt(jnp.mean(xv * xv, axis=-1, keepdims=True) + EPS)
        xhat = xv * rstd
        err = xhat * gv - t_ref[...]
        loss_ref[...] += jnp.sum(err * err) * (0.5 / D)
        dy = err * (1.0 / D)
        dg_ref[...] += jnp.sum(dy * xhat, axis=0, keepdims=True)
        dxhat = dy * gv
        dxv = rstd * (dxhat - xhat * jnp.mean(dxhat * xhat, axis=-1, keepdims=True))
        dx_ref[...] = dxv
        _gate_next(dxv, (br_ref, gate_ref, dbr_ref, dgate_ref))

    return pl.pallas_call(
        body, name=name, grid=(S // TR,),
        in_specs=[ROW_SPEC, ROW_SPEC, VEC_SPEC] + GATE_NEXT_IN,
        out_specs=[ROW_SPEC, VEC_SPEC, VEC_SPEC] + GATE_NEXT_OUT,
        out_shape=[jax.ShapeDtypeStruct((S, D), F32), jax.ShapeDtypeStruct((1, D), F32),
                   jax.ShapeDtypeStruct((1, D), F32)] + GATE_NEXT_SHAPES,
        compiler_params=_cparams(("arbitrary",)),
    )(x, target, g, *gate_next)


TQ = 512
RS = 128
NSUB = TQ // RS
TK = 128


def _dot_hilo(a, tri_twice):
    hi = a.astype(BF16)
    lo = (a - hi.astype(F32)).astype(BF16)
    return jnp.dot(jnp.concatenate([hi, lo], axis=1), tri_twice, preferred_element_type=F32)


def _log_stay(z):
    neg = -z
    return jnp.minimum(neg, 0.0) - jnp.log(1.0 + jnp.exp(jnp.minimum(z, neg)))


def _tri_and_ones(kind):
    row = jnp.bitwise_and(lax.broadcasted_iota(jnp.int32, (2 * TK, 2 * TK), 0), TK - 1)
    col = lax.broadcasted_iota(jnp.int32, (2 * TK, 2 * TK), 1)
    tri = {"after": row > col, "upto": row <= col, "before": row < col}[kind]
    return jnp.logical_or(col >= TK, tri).astype(BF16)


NPAIR = NH // 2
SCALE = HD ** -0.5


def _pair_specs(first_block):
    rows = pl.BlockSpec((TQ, LANES), lambda p, i: (i, first_block + p))
    whole = pl.BlockSpec((S, LANES), lambda p, i: (0, first_block + p))
    return rows, whole


Q_ROWS_SPEC, _ = _pair_specs(0)
_, K_ALL_SPEC = _pair_specs(NPAIR)
_, V_ALL_SPEC = _pair_specs(2 * NPAIR)
PAIR_ROWS_SPEC = pl.BlockSpec((TQ, LANES), lambda p, i: (i, p))
PAIR_ALL_SPEC = pl.BlockSpec((S, LANES), lambda p, i: (0, p))
PAIR_TOTAL_SPEC = pl.BlockSpec((2, TQ, TK), lambda p, i: (p, i, 0))


def _head_halves(x):
    first = lax.broadcasted_iota(jnp.int32, x.shape, 1) < HD
    zero = jnp.zeros_like(x)
    return jnp.where(first, x, zero), jnp.where(first, zero, x)


def _join_heads(a, b):
    return jnp.where(lax.broadcasted_iota(jnp.int32, a.shape, 1) < HD, a, b)


def _comm_hooks(comm, refs, n_in, n_out, n_scratch):
    nc = len(comm.arrs) if comm is not None else 0
    ins, cin = refs[:n_in], refs[n_in:n_in + nc]
    outs = refs[n_in + nc:n_in + nc + n_out]
    cout = refs[n_in + nc + n_out:n_in + 2 * nc + n_out]
    scratch = refs[n_in + 2 * nc + n_out:n_in + 2 * nc + n_out + n_scratch]
    sems = refs[n_in + 2 * nc + n_out + n_scratch:]
    phases = comm.phases(cin, cout, sems) if comm is not None else None
    return ins, outs, scratch, phases


def _with_comm(comm, in_specs, out_specs, out_shape, operands, scratch):
    if comm is None:
        return dict(in_specs=in_specs, out_specs=out_specs, out_shape=out_shape, scratch_shapes=scratch), operands
    nc = len(comm.arrs)
    return dict(in_specs=in_specs + [HBM_SPEC] * nc, out_specs=out_specs + [HBM_SPEC] * nc,
                out_shape=out_shape + comm.out_shape, scratch_shapes=scratch + comm.scratch), operands + comm.arrs


def attn_fwd(qkv, name, comm=None):
    n_steps = S // TQ

    def body(*refs):
        (q_ref, k_ref, v_ref), (o_ref, r_ref), (acc_ref, z_even, z_odd, w_ref), phases = _comm_hooks(
            comm, refs, 3, 2, 4)
        p = pl.program_id(0)
        i = pl.program_id(1)
        if phases is not None:
            pl.when(jnp.logical_and(p == 0, i == 0))(phases[0])
            pl.when(jnp.logical_and(p == NPAIR - 1, i == n_steps - 1))(phases[1])
        chains = [(sub, h) for sub in range(NSUB) for h in range(2)]
        q_sub = [_head_halves(q_ref[pl.ds(sub * RS, RS), :] * SCALE) for sub in range(NSUB)]
        after = _tri_and_ones("after")
        below_diagonal = (lax.broadcasted_iota(jnp.int32, (RS, TK), 1)
                          < lax.broadcasted_iota(jnp.int32, (RS, TK), 0))
        base = i * NSUB
        all_subs = list(range(NSUB))

        acc_ref[...] = jnp.zeros_like(acc_ref)
        r_ref[...] = jnp.zeros_like(r_ref)
        w_ref[...] = jnp.zeros_like(w_ref)

        def key_rows(block):
            return pl.ds(pl.multiple_of(block * TK, TK), TK)

        def store_scores(z_ref, block, subs):
            kb = k_ref[key_rows(block), :]
            for c, (sub, h) in enumerate(chains):
                if sub in subs:
                    z_ref[c] = lax.dot_general(q_sub[sub][h], kb, (((1,), (1,)), ((), ())),
                                               preferred_element_type=F32)

        def add_weighted_values(block, subs):
            vb = v_ref[key_rows(block), :]
            for sub in subs:
                acc_ref[pl.ds(sub * RS, RS), :] += _join_heads(*[
                    jnp.dot(w_ref[2 * sub + h], vb, preferred_element_type=F32) for h in range(2)])

        def step(block, z_ref, z_next_ref, subs, diagonal_sub, prev_subs, next_subs):
            if prev_subs:
                add_weighted_values(block + 1, prev_subs)
            if next_subs:
                store_scores(z_next_ref, jnp.maximum(block - 1, 0), next_subs)
            active = [(c, sub, h) for c, (sub, h) in enumerate(chains) if sub in subs]
            ls, sums = {}, {}
            for c, sub, h in active:
                ls[c] = _log_stay(z_ref[c])
                sums[c] = _dot_hilo(jnp.where(below_diagonal, ls[c], 0.0) if sub == diagonal_sub else ls[c], after)
            for c, sub, h in active:
                rows = pl.ds(sub * RS, RS)
                later = r_ref[h, rows, :]
                w = jnp.exp(z_ref[c] + ls[c] + (sums[c][:, :TK] + later))
                if sub == diagonal_sub:
                    w = jnp.where(below_diagonal, w, 0.0)
                w_ref[c] = w.astype(BF16)
                r_ref[h, rows, :] = later + sums[c][:, TK:]

        store_scores(z_even, base + NSUB - 1, [NSUB - 1])
        buffers = (z_even, z_odd)
        for j in reversed(range(NSUB)):
            subs = all_subs[j:]
            step(base + j, buffers[0], buffers[1], subs, j, all_subs[j + 1:], all_subs[j - 1:] if j else all_subs)
            buffers = buffers[::-1]
        assert buffers[0] is z_even

        @pl.loop(0, base // 2)
        def _(pair):
            block = base - 1 - 2 * pair
            step(block, z_even, z_odd, all_subs, None, all_subs, all_subs)
            step(block - 1, z_odd, z_even, all_subs, None, all_subs, all_subs)

        add_weighted_values(0, all_subs)
        o_ref[...] = acc_ref[...].astype(o_ref.dtype)
        if phases is not None:
            pl.when(jnp.logical_and(p == NPAIR - 1, i == n_steps - 1))(phases[2])

    kwargs, operands = _with_comm(
        comm, [Q_ROWS_SPEC, K_ALL_SPEC, V_ALL_SPEC], [PAIR_ROWS_SPEC, PAIR_TOTAL_SPEC],
        [jax.ShapeDtypeStruct((S, NH * HD), BF16), jax.ShapeDtypeStruct((NH, S, TK), F32)], [qkv, qkv, qkv],
        [pltpu.VMEM((TQ, LANES), F32), pltpu.VMEM((2 * NSUB, RS, TK), F32), pltpu.VMEM((2 * NSUB, RS, TK), F32),
         pltpu.VMEM((2 * NSUB, RS, TK), BF16)])
    return pl.pallas_call(
        body, name=name, grid=(NPAIR, n_steps),
        compiler_params=_cparams(("arbitrary", "arbitrary")), **kwargs,
    )(*operands)


def attn_bwd(qkv, dout, totals, name, comm=None):
    n_steps = S // TQ

    def body(*refs):
        ((q_ref, k_ref, v_ref, do_ref, r_ref), (dq_out, dk_out, dv_out),
         (z_even, z_odd, dw_even, dw_odd, before_ref, dbefore_ref, dz_ref, w_ref, dq_ref, dk_ref, dv_ref),
         phases) = _comm_hooks(comm, refs, 5, 3, 11)
        p = pl.program_id(0)
        i = pl.program_id(1)
        if phases is not None:
            pl.when(jnp.logical_and(p == 0, i == 0))(phases[0])
            pl.when(jnp.logical_and(p == NPAIR - 1, i == n_steps - 2))(phases[1])

        @pl.when(i == 0)
        def _():
            dk_ref[...] = jnp.zeros_like(dk_ref)
            dv_ref[...] = jnp.zeros_like(dv_ref)

        chains = [(sub, h) for sub in range(NSUB) for h in range(2)]
        nch = len(chains)
        qb = q_ref[...]
        dob = do_ref[...].astype(BF16)
        q_sub = [_head_halves(qb[sub * RS:(sub + 1) * RS] * SCALE) for sub in range(NSUB)]
        do_sub = [_head_halves(dob[sub * RS:(sub + 1) * RS]) for sub in range(NSUB)]
        upto = _tri_and_ones("upto")
        before_tri = _tri_and_ones("before")
        below_diagonal = (lax.broadcasted_iota(jnp.int32, (RS, TK), 1)
                          < lax.broadcasted_iota(jnp.int32, (RS, TK), 0))
        contract_lanes = (((1,), (1,)), ((), ()))
        contract_rows = (((0,), (0,)), ((), ()))
        base = i * NSUB
        all_subs = list(range(NSUB))

        def key_rows(block):
            return pl.ds(pl.multiple_of(block * TK, TK), TK)

        def store_products(bufs, block, subs):
            z_ref, dw_ref = bufs
            kb = k_ref[key_rows(block), :]
            vb = v_ref[key_rows(block), :]
            for c, (sub, h) in enumerate(chains):
                if sub in subs:
                    z_ref[c] = lax.dot_general(q_sub[sub][h], kb, contract_lanes, preferred_element_type=F32)
                    dw_ref[c] = lax.dot_general(do_sub[sub][h], vb, contract_lanes, preferred_element_type=F32)

        def add_gradients(block, subs):
            kb = k_ref[key_rows(block), :]
            for sub in subs:
                rows = pl.ds(sub * RS, RS)
                dq_ref[rows, :] += _join_heads(*[jnp.dot(dz_ref[h, rows, :], kb, preferred_element_type=F32)
                                                 for h in range(2)])
            dk_ref[key_rows(block), :] += _join_heads(*[
                lax.dot_general(dz_ref[h], qb, contract_rows, preferred_element_type=F32) for h in range(2)])
            dv_ref[key_rows(block), :] += _join_heads(*[
                lax.dot_general(w_ref[h], dob, contract_rows, preferred_element_type=F32) for h in range(2)])

        for ref in (dq_ref, before_ref, dbefore_ref, dz_ref, w_ref):
            ref[...] = jnp.zeros_like(ref)
        even, odd = (z_even, dw_even), (z_odd, dw_odd)
        store_products(even, 0, all_subs)

        def step(block, bufs, next_bufs, subs, diagonal_sub, prev_subs, next_subs):
            z_ref, dw_ref = bufs
            add_gradients(jnp.maximum(block - 1, 0), prev_subs)
            for sub in prev_subs:
                if sub not in subs:
                    dz_ref[:, pl.ds(sub * RS, RS), :] = jnp.zeros((2, RS, TK), BF16)
                    w_ref[:, pl.ds(sub * RS, RS), :] = jnp.zeros((2, RS, TK), BF16)
            if next_subs:
                store_products(next_bufs, block + 1, next_subs)
            active = [(c, sub, h) for c, (sub, h) in enumerate(chains) if sub in subs]
            ls, sums, dl, dsums = {}, {}, {}, {}
            for c, sub, h in active:
                ls[c] = _log_stay(z_ref[c])
                sums[c] = _dot_hilo(jnp.where(below_diagonal, ls[c], 0.0) if sub == diagonal_sub else ls[c], upto)
            for c, sub, h in active:
                rows = pl.ds(sub * RS, RS)
                before = before_ref[c]
                log_after = r_ref[h, rows, :] - (sums[c][:, :TK] + before)
                w = jnp.exp((z_ref[c] + ls[c]) + log_after)
                if sub == diagonal_sub:
                    w = jnp.where(below_diagonal, w, 0.0)
                dl[c] = dw_ref[c] * w
                dsums[c] = _dot_hilo(dl[c], before_tri)
                w_ref[h, rows, :] = w.astype(BF16)
                before_ref[c] = before + sums[c][:, TK:]
            for c, sub, h in active:
                rows = pl.ds(sub * RS, RS)
                dbefore = dbefore_ref[c]
                beta = jnp.exp(z_ref[c] + ls[c])
                if sub == diagonal_sub:
                    beta = jnp.where(below_diagonal, beta, 0.0)
                dstay = dsums[c][:, :TK] + dbefore
                dz_ref[h, rows, :] = ((dl[c] - beta * (dl[c] + dstay)) * SCALE).astype(BF16)
                dbefore_ref[c] = dbefore + dsums[c][:, TK:]

        @pl.loop(0, base // 2)
        def _(pair):
            step(2 * pair, even, odd, all_subs, None, all_subs, all_subs)
            step(2 * pair + 1, odd, even, all_subs, None, all_subs, all_subs)

        bufs = (even, odd)
        for j in range(NSUB):
            step(base + j, bufs[0], bufs[1], all_subs[j:], j, all_subs[j - 1:] if j else all_subs, all_subs[j + 1:])
            bufs = bufs[::-1]

        add_gradients(base + NSUB - 1, all_subs[NSUB - 1:])
        dq_out[...] = dq_ref[...].astype(dq_out.dtype)

        @pl.when(i == n_steps - 1)
        def _():
            dk_out[...] = dk_ref[...].astype(dk_out.dtype)
            dv_out[...] = dv_ref[...].astype(dv_out.dtype)

        if phases is not None:
            pl.when(jnp.logical_and(p == NPAIR - 1, i == n_steps - 1))(phases[2])

    full = jax.ShapeDtypeStruct((S, NH * HD), BF16)
    kwargs, operands = _with_comm(
        comm, [Q_ROWS_SPEC, K_ALL_SPEC, V_ALL_SPEC, PAIR_ROWS_SPEC, PAIR_TOTAL_SPEC],
        [PAIR_ROWS_SPEC, PAIR_ALL_SPEC, PAIR_ALL_SPEC], [full, full, full], [qkv, qkv, qkv, dout, totals],
        [pltpu.VMEM((2 * NSUB, RS, TK), F32)] * 6 + [pltpu.VMEM((2, TQ, TK), BF16)] * 2
        + [pltpu.VMEM((TQ, LANES), F32), pltpu.VMEM((S, LANES), F32), pltpu.VMEM((S, LANES), F32)])
    return pl.pallas_call(
        body, name=name, grid=(NPAIR, n_steps),
        compiler_params=_cparams(("arbitrary", "arbitrary")), **kwargs,
    )(*operands)


def _proj_cols(first_col):
    base = first_col // LANES
    return pl.BlockSpec((S, LANES), lambda j: (0, base + j))


CONV_OUT_SPEC = pl.BlockSpec((S, LANES), lambda j: (0, j))
CONV_DOUT_SPEC = pl.BlockSpec((S, LANES), lambda j: (0, (NH * HD) // LANES + j))
CONV_W_SPEC = pl.BlockSpec((8, LANES), lambda j: (0, j))
CONV_B_SPEC = pl.BlockSpec((1, LANES), lambda j: (0, j))


def _shift_down(u, n):
    rows = lax.broadcasted_iota(jnp.int32, u.shape, 0)
    return jnp.where(rows >= n, pltpu.roll(u, n, 0), 0.0)


def _shift_up(u, n):
    rows = lax.broadcasted_iota(jnp.int32, u.shape, 0)
    return jnp.where(rows < S - n, pltpu.roll(u, S - n, 0), 0.0)


def conv_fwd(proj, cw8, cb, name):
    def body(bg_ref, cg_ref, hc_ref, w_ref, b_ref, o_ref):
        u = cg_ref[...] * hc_ref[...]
        w = w_ref[...]
        y = w[0:1, :] * _shift_down(u, 2) + w[1:2, :] * _shift_down(u, 1) + w[2:3, :] * u + b_ref[...]
        o_ref[...] = bg_ref[...] * y

    return pl.pallas_call(
        body, name=name, grid=(CW // LANES,),
        in_specs=[_proj_cols(0), _proj_cols(CW), _proj_cols(2 * CW), CONV_W_SPEC, CONV_B_SPEC],
        out_specs=CONV_OUT_SPEC, out_shape=jax.ShapeDtypeStruct((S, CW), F32),
        compiler_params=_cparams(("parallel",)),
    )(proj, proj, proj, cw8, cb)


def conv_bwd(proj, dout, cw8, cb, name):
    def body(bg_ref, cg_ref, hc_ref, do_ref, w_ref, b_ref, dbg_ref, dcg_ref, dhc_ref, dw_ref, db_ref):
        cg, hc, do = cg_ref[...], hc_ref[...], do_ref[...]
        w = w_ref[...]
        u = cg * hc
        u1, u2 = _shift_down(u, 1), _shift_down(u, 2)
        y = w[0:1, :] * u2 + w[1:2, :] * u1 + w[2:3, :] * u + b_ref[...]
        dbg_ref[...] = (do * y).astype(dbg_ref.dtype)
        dy = do * bg_ref[...]
        db_ref[...] = jnp.sum(dy, axis=0, keepdims=True)
        dw_ref[...] = jnp.concatenate(
            [jnp.sum(dy * u2, axis=0, keepdims=True), jnp.sum(dy * u1, axis=0, keepdims=True),
             jnp.sum(dy * u, axis=0, keepdims=True), jnp.zeros((5, LANES), F32)], axis=0)
        du = w[2:3, :] * dy + w[1:2, :] * _shift_up(dy, 1) + w[0:1, :] * _shift_up(dy, 2)
        dcg_ref[...] = (du * hc).astype(dcg_ref.dtype)
        dhc_ref[...] = (du * cg).astype(dhc_ref.dtype)

    full = jax.ShapeDtypeStruct((S, CW), BF16)
    return pl.pallas_call(
        body, name=name, grid=(CW // LANES,),
        in_specs=[_proj_cols(0), _proj_cols(CW), _proj_cols(2 * CW), CONV_DOUT_SPEC, CONV_W_SPEC, CONV_B_SPEC],
        out_specs=[CONV_OUT_SPEC, CONV_OUT_SPEC, CONV_OUT_SPEC, CONV_W_SPEC, CONV_B_SPEC],
        out_shape=[full, full, full, jax.ShapeDtypeStruct((8, CW), F32), jax.ShapeDtypeStruct((1, CW), F32)],
        compiler_params=_cparams(("parallel",)),
    )(proj, proj, proj, dout, cw8, cb)


GELU_K = math.sqrt(2.0 / math.pi)
GELU_C = 0.044715


def _gelu(x):
    return 0.5 * x * (1.0 + jnp.tanh(GELU_K * (x + GELU_C * (x * x * x))))


def _gelu_grad(x):
    t = jnp.tanh(GELU_K * (x + GELU_C * (x * x * x)))
    return 0.5 * (1.0 + t) + 0.5 * x * (1.0 - t * t) * (GELU_K * (1.0 + 3.0 * GELU_C * (x * x)))


def _sg_masks():
    row = lax.broadcasted_iota(jnp.int32, (T, T), 0)
    col = lax.broadcasted_iota(jnp.int32, (T, T), 1)
    causal = jnp.right_shift(row, 6) >= jnp.right_shift(col, 6)
    head_of_col = jnp.right_shift(lax.broadcasted_iota(jnp.int32, (T, CW), 1), 6)
    return causal, head_of_col


def _sg_weights(sw_ref, causal):
    return [jnp.where(causal, sw_ref[h], 0.0).astype(BF16) for h in range(SG_HEADS)]


def _sg_mixed(vnb, weights, bias, head_of_col):
    mixed = bias
    for h in range(SG_HEADS):
        mh = jnp.dot(weights[h], vnb, preferred_element_type=F32)
        mixed = mixed + jnp.where(head_of_col == h, mh, 0.0)
    return mixed


SG_WINDOWS = 4
SG_ROWS = SG_WINDOWS * T
SG_U_SPEC = pl.BlockSpec((SG_ROWS, CW), lambda n: (n, 3))
SG_V_SPEC = pl.BlockSpec((SG_ROWS, CW), lambda n: (n, 4))
SG_ROW_SPEC = pl.BlockSpec((SG_ROWS, CW), lambda n: (n, 0))
SG_DOUT_SPEC = pl.BlockSpec((SG_ROWS, CW), lambda n: (n, 3))
SG_G_SPEC = pl.BlockSpec((1, CW), lambda n: (0, 0))
SG_W_SPEC = pl.BlockSpec((SG_HEADS, T, T), lambda n: (0, 0, 0))
SG_BIAS_SPEC = pl.BlockSpec((T, CW), lambda n: (0, 0))


def sg_fwd(proj, gn, sw, bias, name):
    def body(u_ref, v_ref, g_ref, sw_ref, bias_ref, o_ref):
        causal, head_of_col = _sg_masks()
        weights = _sg_weights(sw_ref, causal)
        for wdw in range(SG_WINDOWS):
            rows = pl.ds(wdw * T, T)
            gv = _gelu(v_ref[rows, :])
            rstd = lax.rsqrt(jnp.mean(gv * gv, axis=-1, keepdims=True) + EPS)
            vnb = ((gv * rstd) * g_ref[...]).astype(BF16)
            mixed = _sg_mixed(vnb, weights, bias_ref[...], head_of_col)
            o_ref[rows, :] = _gelu(u_ref[rows, :]) * mixed

    return pl.pallas_call(
        body, name=name, grid=(S // SG_ROWS,),
        in_specs=[SG_U_SPEC, SG_V_SPEC, SG_G_SPEC, SG_W_SPEC, SG_BIAS_SPEC],
        out_specs=SG_ROW_SPEC, out_shape=jax.ShapeDtypeStruct((S, CW), F32),
        compiler_params=_cparams(("parallel",)),
    )(proj, proj, gn, sw, bias)


def sg_bwd(proj, dout, gn, sw, bias, name):
    def body(u_ref, v_ref, do_ref, g_ref, sw_ref, bias_ref, du_ref, dv_ref, dg_ref, dsw_ref, dbias_ref):
        @pl.when(pl.program_id(0) == 0)
        def _():
            dg_ref[...] = jnp.zeros_like(dg_ref)
            dsw_ref[...] = jnp.zeros_like(dsw_ref)
            dbias_ref[...] = jnp.zeros_like(dbias_ref)

        causal, head_of_col = _sg_masks()
        weights = _sg_weights(sw_ref, causal)
        gnv = g_ref[...]
        for wdw in range(SG_WINDOWS):
            rows = pl.ds(wdw * T, T)
            uv, vv, do = u_ref[rows, :], v_ref[rows, :], do_ref[rows, :]
            gv = _gelu(vv)
            rstd = lax.rsqrt(jnp.mean(gv * gv, axis=-1, keepdims=True) + EPS)
            xhat = gv * rstd
            vnb = (xhat * gnv).astype(BF16)
            mixed = _sg_mixed(vnb, weights, bias_ref[...], head_of_col)
            du_ref[rows, :] = ((do * mixed) * _gelu_grad(uv)).astype(du_ref.dtype)
            dmix = do * _gelu(uv)
            dbias_ref[...] += dmix
            dmixb = dmix.astype(BF16)
            dvn = jnp.zeros((T, CW), F32)
            for h in range(SG_HEADS):
                dvh = lax.dot_general(weights[h], dmixb, (((0,), (0,)), ((), ())), preferred_element_type=F32)
                dvn = dvn + jnp.where(head_of_col == h, dvh, 0.0)
                dmh = jnp.where(head_of_col == h, dmixb, jnp.zeros_like(dmixb))
                dwh = lax.dot_general(dmh, vnb, (((1,), (1,)), ((), ())), preferred_element_type=F32)
                dsw_ref[h] += jnp.where(causal, dwh, 0.0)
            dg_ref[...] += jnp.sum(dvn * xhat, axis=0, keepdims=True)
            dxhat = dvn * gnv
            dgv = rstd * (dxhat - xhat * jnp.mean(dxhat * xhat, axis=-1, keepdims=True))
            dv_ref[rows, :] = (dgv * _gelu_grad(vv)).astype(dv_ref.dtype)

    full = jax.ShapeDtypeStruct((S, CW), BF16)
    return pl.pallas_call(
        body, name=name, grid=(S // SG_ROWS,),
        in_specs=[SG_U_SPEC, SG_V_SPEC, SG_DOUT_SPEC, SG_G_SPEC, SG_W_SPEC, SG_BIAS_SPEC],
        out_specs=[SG_ROW_SPEC, SG_ROW_SPEC, SG_G_SPEC, SG_W_SPEC, SG_BIAS_SPEC],
        out_shape=[full, full, jax.ShapeDtypeStruct((1, CW), F32),
                   jax.ShapeDtypeStruct((SG_HEADS, T, T), F32), jax.ShapeDtypeStruct((T, CW), F32)],
        compiler_params=_cparams(("arbitrary",)),
    )(proj, proj, dout, gn, sw, bias)


ADA_COLS = NMOD * D // NDEV


def ada_fwd(c_all, ada_w, ada_b_mine, name):
    def body(c_ref, w_ref, b_ref, o_ref, ca_ref):
        cv = c_ref[...]
        ca = cv * (1.0 / (1.0 + jnp.exp(-cv)))
        ca_ref[...] = ca
        cab = ca.astype(BF16)
        for l in range(L):
            o_ref[l] = jnp.dot(cab, w_ref[l].astype(BF16), preferred_element_type=F32) + b_ref[l]

    return pl.pallas_call(
        body, name=name,
        out_shape=[jax.ShapeDtypeStruct((L, NDEV, ADA_COLS), F32), jax.ShapeDtypeStruct((NDEV, D), F32)],
        compiler_params=_cparams(),
    )(c_all, ada_w, ada_b_mine)


def ada_bwd(ca, dmod_cols, name):
    def body(ca_ref, dm_ref, o_ref):
        cab = ca_ref[...].astype(BF16)
        for l in range(L):
            o_ref[l] = lax.dot_general(cab, dm_ref[l].astype(BF16), (((0,), (0,)), ((), ())),
                                       preferred_element_type=F32)

    return pl.pallas_call(
        body, name=name, out_shape=jax.ShapeDtypeStruct((L, D, ADA_COLS), F32),
        compiler_params=_cparams(),
    )(ca, dmod_cols)


def _adamw(w, g, m, v):
    m = B1 * m + (1.0 - B1) * g
    v = B2 * v + (1.0 - B2) * (g * g)
    m_hat = m / BC1
    v_hat = v / BC2
    delta = -LR * (m_hat / (jnp.sqrt(v_hat) + AEPS) + WD * w)
    return delta, m, v


VEC_ROWS_PER_LAYER = 8
VEC_FINAL_ROW = L * VEC_ROWS_PER_LAYER
VEC_ROWS = VEC_FINAL_ROW + 8
W256_TAPS, W256_CONV_B, W256_GN = 0, 8, 9
W256_ROWS_PER_LAYER = 16


def small_update(vec_all, w256_all, sb_all, sw_all, params, name):
    n_par = len(params)

    def body(*refs):
        vec_ref, w256_ref, sb_ref = refs[:3]
        sw_refs = refs[3:3 + L]
        par_refs = [refs[3 + L + 3 * k:3 + L + 3 * k + 3] for k in range(n_par)]
        out = refs[3 + L + 3 * n_par:]
        out_par = [out[4 * k:4 * k + 4] for k in range(n_par)]
        loss_ref, taps_ref = out[4 * n_par:]

        def total(ref, idx):
            acc = ref[(0,) + idx].astype(F32)
            for d in range(1, NDEV):
                acc = acc + ref[(d,) + idx].astype(F32)
            return acc

        def update(k, region, g):
            w_ref, m_ref, v_ref = par_refs[k]
            g_ref, d_ref, nm_ref, nv_ref = out_par[k]
            delta, nm, nv = _adamw(w_ref[region], g, m_ref[region], v_ref[region])
            g_ref[region] = g
            d_ref[region] = delta
            nm_ref[region] = nm
            nv_ref[region] = nv

        for l in range(L):
            base = l * VEC_ROWS_PER_LAYER
            for k in range(NMOD):
                update(0, (slice(l, l + 1), slice(k * D, (k + 1) * D)), total(vec_ref, (slice(base + k, base + k + 1),)))
            update(1, (slice(l, l + 1),), total(vec_ref, (slice(base + 6, base + 7),)))
            update(2, (slice(l, l + 1),), total(vec_ref, (slice(base + 7, base + 8),)))
            wbase = l * W256_ROWS_PER_LAYER
            update(4, (slice(l, l + 1),), total(w256_ref, (slice(wbase + W256_CONV_B, wbase + W256_CONV_B + 1),)))
            update(5, (slice(l, l + 1),), total(w256_ref, (slice(wbase + W256_GN, wbase + W256_GN + 1),)))
            update(6, (l,), total(sw_refs[l], ()))
            update(7, (l,), total(sb_ref, (slice(l * SG_HEADS, (l + 1) * SG_HEADS),)))
            taps_ref[l] = total(w256_ref, (slice(wbase + W256_TAPS, wbase + W256_TAPS + 8),))
        update(3, (slice(0, 1),), total(vec_ref, (slice(VEC_FINAL_ROW, VEC_FINAL_ROW + 1),)))
        loss_ref[...] = total(vec_ref, (slice(VEC_FINAL_ROW + 1, VEC_FINAL_ROW + 2), slice(0, LANES)))

    out_shape = []
    for w, _, _ in params:
        out_shape += [jax.ShapeDtypeStruct(w.shape, F32)] * 4
    out_shape += [jax.ShapeDtypeStruct((1, LANES), F32), jax.ShapeDtypeStruct((L, 8, CW), F32)]
    outs = pl.pallas_call(body, name=name, out_shape=out_shape, compiler_params=_cparams())(
        vec_all, w256_all, sb_all, *sw_all, *[a for p in params for a in p])
    return [outs[4 * k:4 * k + 4] for k in range(n_par)], outs[4 * n_par:]


def adamw_plain(w, g, m, v, tr, name):
    rows, cols = w.shape
    spec = pl.BlockSpec((tr, cols), lambda i: (i, 0))

    def body(w_ref, g_ref, m_ref, v_ref, d_ref, nm_ref, nv_ref):
        delta, nm, nv = _adamw(w_ref[...], g_ref[...], m_ref[...], v_ref[...])
        d_ref[...] = delta
        nm_ref[...] = nm
        nv_ref[...] = nv

    shp = jax.ShapeDtypeStruct((rows, cols), F32)
    return pl.pallas_call(
        body, name=name, grid=(rows // tr,), in_specs=[spec] * 4, out_specs=[spec] * 3,
        out_shape=[shp, shp, shp], compiler_params=_cparams(("parallel",)),
    )(w, g, m, v)


def adamw_reduce(w, parts, m, v, tr, name, tie=None):
    _, rows, cols = w.shape
    spec = pl.BlockSpec((None, tr, cols), lambda l, i: (l, i, 0))
    pspecs = [pl.BlockSpec((NDEV, tr, cols), lambda l, i, k=k: (0, jnp.where(l == k, i, 0), 0)) for k in range(L)]

    ties = [] if tie is None else [tie]

    def body(w_ref, p0_ref, p1_ref, m_ref, v_ref, *rest):
        g_ref, d_ref, nm_ref, nv_ref = rest[len(ties):]
        first_layer = pl.program_id(0) == 0
        g = jnp.zeros((tr, cols), F32)
        for d in range(NDEV):
            g = g + jnp.where(first_layer, p0_ref[d], p1_ref[d]).astype(F32)
        delta, nm, nv = _adamw(w_ref[...], g, m_ref[...], v_ref[...])
        g_ref[...] = g
        d_ref[...] = delta
        nm_ref[...] = nm
        nv_ref[...] = nv

    shp = jax.ShapeDtypeStruct(w.shape, F32)
    return pl.pallas_call(
        body, name=name, grid=(L, rows // tr),
        in_specs=[spec] + pspecs + [spec, spec] + [pl.BlockSpec(t.shape, lambda l, i: (0, 0)) for t in ties],
        out_specs=[spec] * 4, out_shape=[shp] * 4, compiler_params=_cparams(("parallel", "parallel")),
    )(w, *parts, m, v, *ties)


SHARD_IN = PROJ // NDEV


def shards_to_columns(shards, name):
    tr = 256

    def body(i_ref, o_ref):
        for d in range(NDEV):
            o_ref[:, d * SHARD_IN:(d + 1) * SHARD_IN] = i_ref[d]

    return pl.pallas_call(
        body, name=name, grid=(D // tr,),
        in_specs=[pl.BlockSpec((NDEV, tr, SHARD_IN), lambda i: (0, i, 0))],
        out_specs=pl.BlockSpec((tr, PROJ), lambda i: (i, 0)),
        out_shape=jax.ShapeDtypeStruct((D, PROJ), shards.dtype), compiler_params=_cparams(("parallel",)),
    )(shards)


def columns_to_shards(mat, name):
    tr = 256

    def body(i_ref, o_ref):
        for d in range(NDEV):
            o_ref[d] = i_ref[:, d * SHARD_IN:(d + 1) * SHARD_IN]

    return pl.pallas_call(
        body, name=name, grid=(D // tr,),
        in_specs=[pl.BlockSpec((tr, PROJ), lambda i: (i, 0))],
        out_specs=pl.BlockSpec((NDEV, tr, SHARD_IN), lambda i: (0, i, 0)),
        out_shape=jax.ShapeDtypeStruct((NDEV, D, SHARD_IN), mat.dtype), compiler_params=_cparams(("parallel",)),
    )(mat)


def _pad_rows(flat, rows):
    return jnp.pad(flat, (0, rows * LANES - flat.shape[0])).reshape(rows, LANES)


def kernel(x, c, ada_w, ada_b, norm_mix_g, norm_mlp_g, w_in, conv_w, conv_b, gmlp_norm_g, spatial_w, spatial_b, w_out, mlp_w1, mlp_w2, final_norm_g, loss_target, m_ada_w, m_ada_b, m_norm_mix_g, m_norm_mlp_g, m_w_in, m_conv_w, m_conv_b, m_gmlp_norm_g, m_spatial_w, m_spatial_b, m_w_out, m_mlp_w1, m_mlp_w2, m_final_norm_g, v_ada_w, v_ada_b, v_norm_mix_g, v_norm_mlp_g, v_w_in, v_conv_w, v_conv_b, v_gmlp_norm_g, v_spatial_w, v_spatial_b, v_w_out, v_mlp_w1, v_mlp_w2, v_final_norm_g):
    me = _lin(_my_pos())
    x0 = x[0]
    target = loss_target[0]
    conv_shard = conv_w.shape[-1]

    w_in_b, w_out_b, w1_b, w2_b = [w.astype(BF16) for w in (w_in, w_out, mlp_w1, mlp_w2)]
    pack0 = _pad_rows(jnp.concatenate([c.reshape(-1), conv_w.reshape(-1)]), 16)
    g0, gw_in0 = run_comm(Gather([pack0, w_in_b[0]]), "gather_first")
    g0 = g0.reshape(NDEV, 16 * LANES)
    c_all = g0[:, :D]
    conv_full = (g0[:, D:D + L * 3 * conv_shard].reshape(NDEV, L, 3, conv_shard)
                 .transpose(1, 2, 0, 3).reshape(L, 3, CW))


    W_in = [shards_to_columns(gw_in0, "w_in_columns0"), None]
    W_out, W1, W2 = [None] * L, [None] * L, [None] * L

    ada_b_mine = lax.dynamic_slice(ada_b, (0, me * ADA_COLS), (L, ADA_COLS)).reshape(L, 1, ADA_COLS)
    mod_part, c_act = ada_fwd(c_all, ada_w, ada_b_mine, "ada_fwd")
    gmod = run_comm(Gather([mod_part]), "gather_mod")[0]
    mod = lax.dynamic_index_in_dim(gmod, me, axis=2, keepdims=False)
    mod = mod.transpose(1, 0, 2).reshape(L, NMOD, 1, D)
    early_weights, token = start_copies([w_out_b[0]], me, "gather_early0_start", True, after=gmod)
    mod = tied(mod, token)

    cw8 = jnp.pad(conv_full, ((0, 0), (0, 5), (0, 0)))
    sg_bias = jnp.repeat(spatial_b.transpose(0, 2, 1), HD, axis=2)

    saved = []
    xl = x0
    for l in range(L):
        sh_m, sc_m, g_m, sh_f, sc_f, g_f = [mod[l, k] for k in range(NMOD)]
        h1 = normmod_fwd(xl, norm_mix_g[l:l + 1], sc_m, sh_m, f"norm_mix_fwd{l}")
        if l > 0:
            gw_in, gw_out = finish_copies(early_weights, xl, f"gather_early{l}_wait")
            W_in[l] = shards_to_columns(gw_in, f"w_in_columns{l}")
        qkv = mm_layer("proj_qkv", l, h1, W_in[l], out_dtypes=[BF16], cols=(0, QKV))[0]
        proj = mm_layer("proj_rest", l, h1, W_in[l], out_dtypes=[F32], cols=(QKV, REST))[0]
        a_out, a_tot, gw2, gw1 = attn_fwd(qkv, f"attn_fwd{l}", comm=Gather([w2_b[l], w1_b[l]]))
        if l == 0:
            gw_out, = finish_copies(early_weights, a_out, f"gather_early{l}_wait")
        W_out[l] = gw_out.reshape(D, D)
        W1[l] = gw1
        W2[l] = gw2.reshape(DFF, D)
        if l + 1 < L:
            early_weights, token = start_copies([w_in_b[l + 1], w_out_b[l + 1]], me, f"gather_early{l + 1}_start", True,
                                                after=a_out)
            g_m = tied(g_m, token)
        c_out = conv_fwd(proj, cw8[l], conv_b[l:l + 1], f"conv_fwd{l}")
        s_out = sg_fwd(proj, gmlp_norm_g[l:l + 1], spatial_w[l], sg_bias[l], f"sg_fwd{l}")
        cat = jnp.concatenate([a_out, c_out.astype(BF16), s_out.astype(BF16)], axis=1)
        mix, x1, h2 = mm_layer("mix", l, cat, W_out[l], out_dtypes=[F32, F32, BF16], epilogue=_residual_then_norm,
                               extras=[(xl, "tile"), (g_m, "col"), (norm_mlp_g[l:l + 1], "col"), (sc_f, "col"),
                                       (sh_f, "col")])
        ra, r = mm_layer("mlp_up", l, h2, W1[l], out_dtypes=[BF16, BF16], b_blocks=True,
                         epilogue=lambda acc: (jnp.maximum(acc, 0.0), jnp.square(jnp.maximum(acc, 0.0))))
        m2, x2 = mm_layer("mlp_down", l, r, W2[l], out_dtypes=[F32, F32],
                          epilogue=lambda acc, xr, g: (acc, xr + g * acc), extras=[(x1, "tile"), (g_f, "col")])
        saved.append(dict(x=xl, h1=h1, proj=proj, qkv=qkv, a_tot=a_tot, cat=cat, mix=mix,
                          x1=x1, h2=h2, ra=ra, r=r, m2=m2))
        xl = x2

    dx, loss_part, d_final_g, dm2, dg_f = loss_head(xl, target, final_norm_g.reshape(1, D),
                                                    (saved[L - 1]["m2"], mod[L - 1, NMOD - 1]), "loss_head")

    p_in, p_out, p_w1, p_w2 = [None] * L, [None] * L, [None] * L, [None] * L
    grads_in_flight = [None] * L
    vec_rows, d_norm_mix, d_norm_mlp = [None] * L, [None] * L, [None] * L
    dcw8, d_conv_b, d_gn, d_sw, d_sb = [None] * L, [None] * L, [None] * L, [None] * L, [None] * L
    for l in reversed(range(L)):
        sv = saved[l]
        sh_m, sc_m, g_m, sh_f, sc_f, g_f = [mod[l, k] for k in range(NMOD)]
        da = mm_layer("mlp_down_dgrad", l, dm2, W2[l], out_dtypes=[BF16], trans_b=True,
                      epilogue=lambda acc, rav: (acc * (2.0 * rav.astype(F32)),), extras=[(sv["ra"], "tile")])[0]
        dW2 = mm_layer("mlp_down_wgrad", l, sv["r"], dm2, out_dtypes=[BF16], trans_a=True)[0]
        dW1 = mm_layer("mlp_up_wgrad", l, sv["h2"], da, out_dtypes=[BF16], trans_a=True, out_blocks=True)[0]
        dh2 = mm_layer("mlp_up_dgrad", l, da, W1[l], out_dtypes=[F32], trans_b=True, b_blocks=True)[0]
        dx1, dsc_f, dsh_f, d_norm_mlp[l], dmix, dg_m = normmod_bwd(
            sv["x1"], dh2, dx, norm_mlp_g[l:l + 1], sc_f, f"norm_mlp_bwd{l}", gate_next=(sv["mix"], g_m))
        dcat = mm_layer("mix_dgrad", l, dmix, W_out[l], out_dtypes=[F32], trans_b=True)[0]
        dW_out = mm_layer("mix_wgrad", l, sv["cat"], dmix, out_dtypes=[BF16], trans_a=True)[0]
        pieces_w2, pieces_out = dW2.reshape(NDEV, DFF // NDEV, D), dW_out.reshape(NDEV, D // NDEV, D)
        ride, late = ([pieces_w2, pieces_out], dW1) if l == L - 1 else ([pieces_w2, dW1], pieces_out)
        dq, dk, dv, *arrived = attn_bwd(sv["qkv"], dcat, sv["a_tot"], f"attn_bwd{l}", comm=Exchange(ride))
        p_w2[l] = arrived[0]
        (p_out if l == L - 1 else p_w1)[l] = arrived[1]
        dbg, dcg, dhc, dcw8[l], d_conv_b[l] = conv_bwd(sv["proj"], dcat, cw8[l], conv_b[l:l + 1], f"conv_bwd{l}")
        dus, dvs, d_gn[l], dsw, dbias = sg_bwd(sv["proj"], dcat, gmlp_norm_g[l:l + 1], spatial_w[l], sg_bias[l],
                                               f"sg_bwd{l}")
        d_sw[l] = dsw.astype(BF16)
        d_sb[l] = dbias.reshape(T, SG_HEADS, HD).sum(axis=2).T
        dproj = jnp.concatenate([dq, dk, dv, dbg, dcg, dhc, dus, dvs], axis=1).astype(BF16)
        dW_in = mm_layer("proj_wgrad", l, sv["h1"], dproj, out_dtypes=[BF16], trans_a=True)[0]
        pieces = columns_to_shards(dW_in, f"w_in_grad_shards{l}")
        grads_in_flight[l], token = start_copies([late, pieces], me, f"exchange_tail{l}_start", False)
        dh1 = mm_layer("proj_dgrad", l, dproj, W_in[l], out_dtypes=[F32], trans_b=True, extras=[(token, "tie")])[0]
        below = (saved[l - 1]["m2"], mod[l - 1, NMOD - 1]) if l > 0 else None
        dx, dsc_m, dsh_m, d_norm_mix[l], *gated_below = normmod_bwd(
            sv["x"], dh1, dx1, tied(norm_mix_g[l:l + 1], token), sc_m, f"norm_mix_bwd{l}", gate_next=below)
        vec_rows[l] = [dsh_m, dsc_m, dg_m, dsh_f, dsc_f, dg_f, d_norm_mix[l], d_norm_mlp[l]]
        if l > 0:
            dm2, dg_f = gated_below

    grad_x = dx.reshape(1, S, D)

    g_w2, d_w2, nm_w2, nv_w2 = adamw_reduce(mlp_w2, p_w2, m_mlp_w2, v_mlp_w2, 256, "adamw_mlp_w2", tie=token)
    p_w1[L - 1], p_in[L - 1] = finish_copies(grads_in_flight[L - 1], d_w2, f"exchange_tail{L - 1}_wait")
    g_w1, d_w1, nm_w1, nv_w1 = adamw_reduce(mlp_w1, p_w1, m_mlp_w1, v_mlp_w1, 256, "adamw_mlp_w1", tie=token)

    vec_pack = jnp.concatenate([row for l in range(L) for row in vec_rows[l]]
                               + [d_final_g, loss_part, jnp.zeros((VEC_ROWS - VEC_FINAL_ROW - 2, D), F32)], axis=0)
    vec_pack, _ = lax.optimization_barrier((vec_pack, (d_w1, d_w2)))
    w256_pack = jnp.concatenate([blk for l in range(L) for blk in (
        dcw8[l], d_conv_b[l], d_gn[l], jnp.zeros((W256_ROWS_PER_LAYER - W256_GN - 1, CW), F32))], axis=0)
    vec_all, w256_all, sb_all, *sw_all = run_comm(
        Gather([vec_pack, w256_pack, jnp.concatenate(d_sb, axis=0)] + d_sw), "gather_small_grads")

    dmod_all = (vec_all[:, :VEC_FINAL_ROW].reshape(NDEV, L, VEC_ROWS_PER_LAYER, D)[:, :, :NMOD]
                .reshape(NDEV, L, NMOD * D))
    dmod_cols = lax.dynamic_slice(dmod_all, (0, 0, me * ADA_COLS), (NDEV, L, ADA_COLS)).transpose(1, 0, 2)
    g_ada_w = ada_bwd(c_act, dmod_cols, "ada_bwd")

    flat2 = lambda t: t.reshape(L * D, ADA_COLS)
    d_ada_w, nm_ada_w, nv_ada_w = [t.reshape(L, D, ADA_COLS) for t in adamw_plain(
        flat2(ada_w), flat2(g_ada_w), flat2(m_ada_w), flat2(v_ada_w), 256, "adamw_ada_w")]

    after = jnp.concatenate([t.reshape(-1)[:1] for t in (d_w1, d_w2, d_ada_w)])
    p_out[0], p_in[0] = finish_copies(grads_in_flight[0], after, "exchange_tail0_wait")
    g_w_in, d_w_in, nm_w_in, nv_w_in = adamw_reduce(w_in, p_in, m_w_in, v_w_in, 256, "adamw_w_in")
    g_w_out, d_w_out, nm_w_out, nv_w_out = adamw_reduce(w_out, p_out, m_w_out, v_w_out, 128, "adamw_w_out")

    as_row = lambda t: t.reshape(1, D)
    small_params = [(ada_b, m_ada_b, v_ada_b), (norm_mix_g, m_norm_mix_g, v_norm_mix_g),
                    (norm_mlp_g, m_norm_mlp_g, v_norm_mlp_g),
                    (as_row(final_norm_g), as_row(m_final_norm_g), as_row(v_final_norm_g)),
                    (conv_b, m_conv_b, v_conv_b), (gmlp_norm_g, m_gmlp_norm_g, v_gmlp_norm_g),
                    (spatial_w, m_spatial_w, v_spatial_w), (spatial_b, m_spatial_b, v_spatial_b)]
    updated, (loss_sum, taps_sum) = small_update(vec_all, w256_all, sb_all, sw_all, small_params, "small_update")
    loss = loss_sum[0, 0]
    u_ada_b, u_norm_mix, u_norm_mlp, u_final, u_conv_b, u_gn, u_sw, u_sb = updated
    u_final = [t.reshape(D) for t in u_final]
    g_conv_w = lax.dynamic_slice(taps_sum, (0, 0, me * conv_shard), (L, 3, conv_shard))
    flat_cw = lambda t: t.reshape(L * 3, conv_shard)
    u_conv_w = [g_conv_w] + [t.reshape(L, 3, conv_shard) for t in adamw_plain(
        flat_cw(conv_w), flat_cw(g_conv_w), flat_cw(m_conv_w), flat_cw(v_conv_w), L * 3, "adamw_conv_w")]
    small_sets = [u_ada_b, u_norm_mix, u_norm_mlp, u_conv_w, u_conv_b, u_gn, u_sw, u_sb, u_final]
    small_g, sd, snm, snv = [[u[k] for u in small_sets] for k in range(4)]

    def ordered(big, small):
        ada, win, wout, w1, w2 = big
        return [ada, small[0], small[1], small[2], win, small[3], small[4], small[5], small[6], small[7],
                wout, w1, w2, small[8]]

    grads = ordered([g_ada_w, g_w_in, g_w_out, g_w1, g_w2], small_g)
    deltas = ordered([d_ada_w, d_w_in, d_w_out, d_w1, d_w2], sd)
    new_m = ordered([nm_ada_w, nm_w_in, nm_w_out, nm_w1, nm_w2], snm)
    new_v = ordered([nv_ada_w, nv_w_in, nv_w_out, nv_w1, nv_w2], snv)
    return (loss, grad_x, *grads, *deltas, *new_m, *new_v)
```

```python
import functools
import math

import jax
import jax.numpy as jnp
from jax import lax
from jax.experimental import pallas as pl
from jax.experimental.pallas import tpu as pltpu

F32 = jnp.float32
BF16 = jnp.bfloat16
MESH = pl.DeviceIdType.MESH

S = 2048
D = 1024
L = 2
NDEV = 8
HD = 64
NH = 8
PROJ = 2816
DFF = 4096
NMOD = 6
EPS = 1e-6
T = 128
SG_HEADS = 4
LANES = 128
CW = 256
QKV = 3 * NH * HD
REST = PROJ - QKV

LR, B1, B2, AEPS, WD, STEP = 0.001, 0.9, 0.999, 1e-08, 0.01, 10
BC1 = 1.0 - B1 ** STEP
BC2 = 1.0 - B2 ** STEP

VMEM_LIMIT = 48 * 1024 * 1024

HBM_SPEC = pl.BlockSpec(memory_space=pltpu.HBM)


def _cparams(sem=None):
    return pltpu.CompilerParams(dimension_semantics=sem, vmem_limit_bytes=VMEM_LIMIT)


def _my_pos():
    return lax.axis_index("x"), lax.axis_index("y"), lax.axis_index("c")


def _lin(p):
    return 4 * p[0] + 2 * p[1] + p[2]


class Gather:
    def __init__(self, arrs):
        self.arrs = list(arrs)
        n = len(self.arrs)
        self.out_shape = [jax.ShapeDtypeStruct((NDEV,) + a.shape, a.dtype) for a in self.arrs]
        self.scratch = [pltpu.SemaphoreType.DMA((n, 7)), pltpu.SemaphoreType.DMA((n, 7)),
                        pltpu.SemaphoreType.DMA((n,))]

    def phases(self, ins, outs, sems):
        n = len(self.arrs)
        send_sems, recv_sems, local_sems = sems
        x, y, c = _my_pos()
        me, sibling = (x, y, c), (x, y, 1 - c)
        chips = [(1 - x, y), (x, 1 - y), (1 - x, 1 - y)]

        def copy(a, k, block, to, src=None):
            slot = outs[a].at[_lin(block)]
            return pltpu.make_async_remote_copy(
                src_ref=slot if src is None else src, dst_ref=slot,
                send_sem=send_sems.at[a, k], recv_sem=recv_sems.at[a, k],
                device_id=to, device_id_type=MESH)

        def mine(a):
            return pltpu.make_async_copy(ins[a], outs[a].at[_lin(me)], local_sems.at[a])

        def first(a):
            return [copy(a, 0, me, sibling, src=ins[a])] + [
                copy(a, 1 + j, me, (*chip, c), src=ins[a]) for j, chip in enumerate(chips)]

        def passed(a):
            return [copy(a, 4 + j, (*chip, c), sibling) for j, chip in enumerate(chips)]

        def start():
            for a in range(n):
                mine(a).start()
                for cp in first(a):
                    cp.start()

        def relay():
            for j, chip in enumerate(chips):
                for a in range(n):
                    copy(a, 1 + j, (*chip, c), me).wait_recv()
                    passed(a)[j].start()

        def finish():
            for a in range(n):
                copy(a, 0, sibling, me).wait_recv()
            for j, chip in enumerate(chips):
                for a in range(n):
                    copy(a, 4 + j, (*chip, 1 - c), me).wait_recv()
            for a in range(n):
                for cp in first(a) + passed(a):
                    cp.wait_send()
                mine(a).wait()

        return start, relay, finish


class Exchange:
    def __init__(self, arrs):
        self.arrs = list(arrs)
        n = len(self.arrs)
        self.out_shape = [jax.ShapeDtypeStruct(a.shape, a.dtype) for a in self.arrs]
        self.scratch = [pltpu.SemaphoreType.DMA((n, 7)), pltpu.SemaphoreType.DMA((n, 7)),
                        pltpu.SemaphoreType.DMA((n,))]

    def phases(self, ins, outs, sems):
        n = len(self.arrs)
        send_sems, recv_sems, local_sems = sems
        x, y, c = _my_pos()
        me = (x, y, c)

        def peer(mask):
            return (1 - x if mask & 4 else x, 1 - y if mask & 2 else y, 1 - c if mask & 1 else c)

        def copy(a, mask):
            return pltpu.make_async_remote_copy(
                src_ref=ins[a].at[_lin(peer(mask))], dst_ref=outs[a].at[_lin(me)],
                send_sem=send_sems.at[a, mask - 1], recv_sem=recv_sems.at[a, mask - 1],
                device_id=peer(mask), device_id_type=MESH)

        def arrival(a, mask):
            return pltpu.make_async_remote_copy(
                src_ref=ins[a].at[_lin(me)], dst_ref=outs[a].at[_lin(peer(mask))],
                send_sem=send_sems.at[a, mask - 1], recv_sem=recv_sems.at[a, mask - 1],
                device_id=peer(mask), device_id_type=MESH)

        def mine(a):
            return pltpu.make_async_copy(ins[a].at[_lin(me)], outs[a].at[_lin(me)], local_sems.at[a])

        def start():
            for a in range(n):
                mine(a).start()
            for mask in (4, 2, 6, 1, 5, 3, 7):
                for a in range(n):
                    copy(a, mask).start()

        def relay():
            pass

        def finish():
            for mask in range(1, 8):
                for a in range(n):
                    arrival(a, mask).wait_recv()
            for mask in range(1, 8):
                for a in range(n):
                    copy(a, mask).wait_send()
            for a in range(n):
                mine(a).wait()

        return start, relay, finish


def run_comm(plan, name):
    n = len(plan.arrs)

    def body(*refs):
        start, relay, finish = plan.phases(refs[:n], refs[n:2 * n], refs[2 * n:])
        start()
        relay()
        finish()

    outs = pl.pallas_call(
        body, name=name, out_shape=plan.out_shape,
        in_specs=[HBM_SPEC] * n, out_specs=[HBM_SPEC] * n, scratch_shapes=plan.scratch,
    )(*plan.arrs)
    return list(outs)


SEM_SPEC = pl.BlockSpec(memory_space=pltpu.SEMAPHORE)
DATAFLOW = pltpu.SideEffectType.DATAFLOW_SIDE_EFFECTING


def _peer_copies(src_ref, land_ref, send_sems, recv_sems, first, same_block):
    x, y, c = _my_pos()
    me = (x, y, c)
    sends, arrivals = [], []
    for mask in (4, 2, 6, 1, 5, 3, 7):
        peer = (1 - x if mask & 4 else x, 1 - y if mask & 2 else y, 1 - c if mask & 1 else c)
        sends.append(pltpu.make_async_remote_copy(
            src_ref=src_ref if same_block else src_ref.at[_lin(peer)], dst_ref=land_ref.at[_lin(me)],
            send_sem=send_sems.at[first + mask - 1], recv_sem=recv_sems.at[first + mask - 1], device_id=peer,
            device_id_type=MESH))
        arrivals.append(pltpu.make_async_remote_copy(
            src_ref=src_ref if same_block else src_ref.at[_lin(me)], dst_ref=land_ref.at[_lin(peer)],
            send_sem=send_sems.at[first + mask - 1], recv_sem=recv_sems.at[first + mask - 1], device_id=peer,
            device_id_type=MESH))
    return sends, arrivals


def start_copies(srcs, me, name, same_block, after=None):
    n = len(srcs)
    landings = []
    for src in srcs:
        own = src[None] if same_block else lax.dynamic_index_in_dim(src, me, axis=0, keepdims=True)
        landings.append(lax.dynamic_update_slice(lax.empty((NDEV,) + own.shape[1:], src.dtype), own,
                                                 (me,) + (0,) * (own.ndim - 1)))

    def body(*refs):
        send_sems, recv_sems = refs[-2 * n - 3], refs[-2 * n - 2]
        token = refs[-1]
        for k in range(n):
            sends, _ = _peer_copies(refs[2 * k], refs[2 * k + 1], send_sems, recv_sems, 7 * k, same_block)
            for cp in sends:
                cp.start()
        token[...] = jnp.zeros_like(token)

    hbm = lambda a: pltpu.HBM(a.shape, a.dtype)
    pairs = [a for pair in zip(srcs, landings) for a in pair]
    extra = [] if after is None else [after]
    sems = pltpu.SemaphoreType.DMA((7 * n,))
    send_sems, recv_sems, *thru, token = pl.pallas_call(
        body, name=name,
        out_shape=(sems, sems, *[hbm(a) for a in pairs], jax.ShapeDtypeStruct((8, LANES), F32)),
        in_specs=[HBM_SPEC] * (2 * n) + [pl.BlockSpec(memory_space=pl.ANY)] * len(extra),
        out_specs=(SEM_SPEC, SEM_SPEC, *[HBM_SPEC] * (2 * n), pl.BlockSpec(memory_space=pltpu.VMEM)),
        input_output_aliases={k: 2 + k for k in range(2 * n)},
        compiler_params=pltpu.CompilerParams(has_side_effects=DATAFLOW),
    )(*[pltpu.with_memory_space_constraint(a, pltpu.HBM) for a in pairs], *extra)
    return (send_sems, recv_sems, thru, same_block), token


def finish_copies(handle, after, name):
    send_sems, recv_sems, thru, same_block = handle
    n = len(thru) // 2

    def body(*refs):
        send_sems, recv_sems = refs[2 * n], refs[2 * n + 1]
        for k in range(n):
            sends, arrivals = _peer_copies(refs[2 * k], refs[2 * k + 1], send_sems, recv_sems, 7 * k, same_block)
            for cp in sends:
                cp.wait_send()
            for cp in arrivals:
                cp.wait_recv()

    hbm = lambda a: pltpu.HBM(a.shape, a.dtype)
    outs = pl.pallas_call(
        body, name=name, out_shape=tuple(hbm(a) for a in thru),
        in_specs=[HBM_SPEC] * (2 * n) + [SEM_SPEC, SEM_SPEC, pl.BlockSpec(memory_space=pl.ANY)],
        out_specs=tuple([HBM_SPEC] * (2 * n)), input_output_aliases={k: k for k in range(2 * n)},
        compiler_params=pltpu.CompilerParams(has_side_effects=DATAFLOW),
    )(*thru, send_sems, recv_sems, after)
    return [outs[2 * k + 1] for k in range(n)]


def tied(x, token):
    return x + token[0:1, 0:1].astype(x.dtype)


MM_TILES = {
    "proj_qkv": (S, 512), "proj_rest": (S, 256), "mix": (512, D), "mlp_up": (S, 512), "mlp_down": (1024, 256),
    "mlp_down_dgrad": (S, 1024), "mlp_down_wgrad": (1024, 1024), "mlp_up_wgrad": (1024, 512),
    "mlp_up_dgrad": (1024, 512), "mix_dgrad": (1024, 512), "mix_wgrad": (512, 1024),
    "proj_wgrad": (1024, PROJ // 2), "proj_dgrad": (1024, 512),
}


def mm_layer(kind, l, a, b, **kw):
    tm, tn = MM_TILES[kind]
    return mm(a, b, tm=tm, tn=tn, name=f"{kind}{l}", **kw)


def mm(a, b, *, tm, tn, out_dtypes, epilogue=None, extras=(), name, trans_a=False, trans_b=False,
       cols=None, b_blocks=False, out_blocks=False):
    if trans_a:
        kdim, m = a.shape
    else:
        m, kdim = a.shape
    shard = b.shape[-1] if b_blocks else None
    if b_blocks:
        full = (b.shape[1], NDEV * shard)
    else:
        full = b.shape
    first, ncols = cols if cols is not None else (0, full[0] if trans_b else full[1])
    assert full[1 if trans_b else 0] == kdim and m % tm == 0 and ncols % tn == 0 and first % tn == 0
    j0 = first // tn
    if trans_a:
        a_spec = pl.BlockSpec((kdim, tm), lambda i, j: (0, i))
    else:
        a_spec = pl.BlockSpec((tm, kdim), lambda i, j: (i, 0))
    if b_blocks and trans_b:
        b_spec = pl.BlockSpec((NDEV, tn, shard), lambda i, j: (0, j0 + j, 0))
    elif b_blocks:
        assert tn == shard
        b_spec = pl.BlockSpec((None, kdim, tn), lambda i, j: (j0 + j, 0, 0))
    elif trans_b:
        b_spec = pl.BlockSpec((tn, kdim), lambda i, j: (j0 + j, 0))
    else:
        b_spec = pl.BlockSpec((kdim, tn), lambda i, j: (0, j0 + j))
    if out_blocks:
        assert tn * NDEV == ncols
        out_spec = pl.BlockSpec((None, tm, tn), lambda i, j: (j, i, 0))
        out_dims = (NDEV, m, tn)
    else:
        out_spec = pl.BlockSpec((tm, tn), lambda i, j: (i, j))
        out_dims = (m, ncols)
    ex_specs = []
    for arr, kind in extras:
        if kind == "tile":
            ex_specs.append(pl.BlockSpec((tm, tn), lambda i, j: (i, j)))
        elif kind == "col":
            ex_specs.append(pl.BlockSpec((1, tn), lambda i, j: (0, j)))
        else:
            ex_specs.append(pl.BlockSpec(arr.shape, lambda i, j: (0, 0)))
    n_ex, n_out = len(extras), len(out_dtypes)
    used = [k for k, (_, kind) in enumerate(extras) if kind != "tie"]

    def body(a_ref, b_ref, *rest):
        ex_refs, out_refs = rest[:n_ex], rest[n_ex:]
        if trans_a:
            acc = lax.dot_general(a_ref[...], b_ref[...], (((0,), (0,)), ((), ())),
                                  preferred_element_type=F32)
        elif trans_b and b_blocks:
            acc = jnp.zeros((tm, tn), F32)
            for d in range(NDEV):
                acc = acc + lax.dot_general(a_ref[:, d * shard:(d + 1) * shard], b_ref[d],
                                            (((1,), (1,)), ((), ())), preferred_element_type=F32)
        elif trans_b:
            acc = lax.dot_general(a_ref[...], b_ref[...], (((1,), (1,)), ((), ())),
                                  preferred_element_type=F32)
        else:
            acc = jnp.dot(a_ref[...], b_ref[...], preferred_element_type=F32)
        outs = (acc,) if epilogue is None else epilogue(acc, *[ex_refs[k][...] for k in used])
        for o_ref, val in zip(out_refs, outs):
            o_ref[...] = val.astype(o_ref.dtype)

    outs = pl.pallas_call(
        body, name=name, grid=(m // tm, ncols // tn),
        in_specs=[a_spec, b_spec] + ex_specs,
        out_specs=[out_spec for _ in range(n_out)],
        out_shape=[jax.ShapeDtypeStruct(out_dims, dt) for dt in out_dtypes],
        compiler_params=_cparams(("parallel", "parallel")),
    )(a, b, *[arr for arr, _ in extras])
    return list(outs)


TR = 512

ROW_SPEC = pl.BlockSpec((TR, D), lambda i: (i, 0))
VEC_SPEC = pl.BlockSpec((1, D), lambda i: (0, 0))


def _residual_then_norm(acc, xr, gate, g, sc, sh):
    x_new = xr + gate * acc
    rstd = lax.rsqrt(jnp.mean(x_new * x_new, axis=-1, keepdims=True) + EPS)
    return acc, x_new, ((x_new * rstd) * g) * (1.0 + sc) + sh


def normmod_fwd(x, g, sc, sh, name):
    def body(x_ref, g_ref, sc_ref, sh_ref, o_ref):
        xv = x_ref[...]
        rstd = lax.rsqrt(jnp.mean(xv * xv, axis=-1, keepdims=True) + EPS)
        n = (xv * rstd) * g_ref[...]
        o_ref[...] = (n * (1.0 + sc_ref[...]) + sh_ref[...]).astype(o_ref.dtype)

    return pl.pallas_call(
        body, name=name, grid=(S // TR,),
        in_specs=[ROW_SPEC, VEC_SPEC, VEC_SPEC, VEC_SPEC], out_specs=ROW_SPEC,
        out_shape=jax.ShapeDtypeStruct((S, D), BF16),
        compiler_params=_cparams(("parallel",)),
    )(x, g, sc, sh)


def _gate_next(dxv, refs):
    br_ref, gate_ref, dbr_ref, dgate_ref = refs

    @pl.when(pl.program_id(0) == 0)
    def _():
        dgate_ref[...] = jnp.zeros_like(dgate_ref)

    dbr_ref[...] = (dxv * gate_ref[...]).astype(dbr_ref.dtype)
    dgate_ref[...] += jnp.sum(dxv * br_ref[...], axis=0, keepdims=True)


GATE_NEXT_IN = [ROW_SPEC, VEC_SPEC]
GATE_NEXT_OUT = [ROW_SPEC, VEC_SPEC]
GATE_NEXT_SHAPES = [jax.ShapeDtypeStruct((S, D), BF16), jax.ShapeDtypeStruct((1, D), F32)]


def normmod_bwd(x, dh, dres, g, sc, name, gate_next=None):
    nxt = 2 if gate_next else 0

    def body(x_ref, dh_ref, dres_ref, g_ref, sc_ref, *rest):
        nxt_in, (dx_ref, dsc_ref, dsh_ref, dg_ref), nxt_out = rest[:nxt], rest[nxt:nxt + 4], rest[nxt + 4:]

        @pl.when(pl.program_id(0) == 0)
        def _():
            dsc_ref[...] = jnp.zeros_like(dsc_ref)
            dsh_ref[...] = jnp.zeros_like(dsh_ref)
            dg_ref[...] = jnp.zeros_like(dg_ref)

        xv, dh = x_ref[...], dh_ref[...]
        gv = g_ref[...]
        rstd = lax.rsqrt(jnp.mean(xv * xv, axis=-1, keepdims=True) + EPS)
        xhat = xv * rstd
        dn = dh * (1.0 + sc_ref[...])
        dxhat = dn * gv
        dxv = dres_ref[...] + rstd * (dxhat - xhat * jnp.mean(dxhat * xhat, axis=-1, keepdims=True))
        dx_ref[...] = dxv
        dsc_ref[...] += jnp.sum(dh * (xhat * gv), axis=0, keepdims=True)
        dsh_ref[...] += jnp.sum(dh, axis=0, keepdims=True)
        dg_ref[...] += jnp.sum(dn * xhat, axis=0, keepdims=True)
        if gate_next:
            _gate_next(dxv, nxt_in + nxt_out)

    vec_out = jax.ShapeDtypeStruct((1, D), F32)
    on = bool(gate_next)
    return pl.pallas_call(
        body, name=name, grid=(S // TR,),
        in_specs=[ROW_SPEC, ROW_SPEC, ROW_SPEC, VEC_SPEC, VEC_SPEC] + GATE_NEXT_IN * on,
        out_specs=[ROW_SPEC, VEC_SPEC, VEC_SPEC, VEC_SPEC] + GATE_NEXT_OUT * on,
        out_shape=[jax.ShapeDtypeStruct((S, D), F32), vec_out, vec_out, vec_out] + GATE_NEXT_SHAPES * on,
        compiler_params=_cparams(("arbitrary",)),
    )(x, dh, dres, g, sc, *(gate_next or ()))


def loss_head(x, target, g, gate_next, name):
    def body(x_ref, t_ref, g_ref, br_ref, gate_ref, dx_ref, loss_ref, dg_ref, dbr_ref, dgate_ref):
        @pl.when(pl.program_id(0) == 0)
        def _():
            loss_ref[...] = jnp.zeros_like(loss_ref)
            dg_ref[...] = jnp.zeros_like(dg_ref)

        xv, gv = x_ref[...], g_ref[...]
        rstd = lax.rsqrt(jnp.mean(xv * xv, axis=-1, keepdims=True) + EPS)
        xhat = xv * rstd
        err = xhat * gv - t_ref[...]
        loss_ref[...] += jnp.sum(err * err) * (0.5 / D)
        dy = err * (1.0 / D)
        dg_ref[...] += jnp.sum(dy * xhat, axis=0, keepdims=True)
        dxhat = dy * gv
        dxv = rstd * (dxhat - xhat * jnp.mean(dxhat * xhat, axis=-1, keepdims=True))
        dx_ref[...] = dxv
        _gate_next(dxv, (br_ref, gate_ref, dbr_ref, dgate_ref))

    return pl.pallas_call(
        body, name=name, grid=(S // TR,),
        in_specs=[ROW_SPEC, ROW_SPEC, VEC_SPEC] + GATE_NEXT_IN,
        out_specs=[ROW_SPEC, VEC_SPEC, VEC_SPEC] + GATE_NEXT_OUT,
        out_shape=[jax.ShapeDtypeStruct((S, D), F32), jax.ShapeDtypeStruct((1, D), F32),
                   jax.ShapeDtypeStruct((1, D), F32)] + GATE_NEXT_SHAPES,
        compiler_params=_cparams(("arbitrary",)),
    )(x, target, g, *gate_next)


TQ = 512
RS = 128
NSUB = TQ // RS
TK = 128


def _dot_hilo(a, tri_twice):
    hi = a.astype(BF16)
    lo = (a - hi.astype(F32)).astype(BF16)
    return jnp.dot(jnp.concatenate([hi, lo], axis=1), tri_twice, preferred_element_type=F32)


def _log_stay(z):
    neg = -z
    return jnp.minimum(neg, 0.0) - jnp.log(1.0 + jnp.exp(jnp.minimum(z, neg)))


def _tri_and_ones(kind):
    row = jnp.bitwise_and(lax.broadcasted_iota(jnp.int32, (2 * TK, 2 * TK), 0), TK - 1)
    col = lax.broadcasted_iota(jnp.int32, (2 * TK, 2 * TK), 1)
    tri = {"after": row > col, "upto": row <= col, "before": row < col}[kind]
    return jnp.logical_or(col >= TK, tri).astype(BF16)


NPAIR = NH // 2
SCALE = HD ** -0.5


def _pair_specs(first_block):
    rows = pl.BlockSpec((TQ, LANES), lambda p, i: (i, first_block + p))
    whole = pl.BlockSpec((S, LANES), lambda p, i: (0, first_block + p))
    return rows, whole


Q_ROWS_SPEC, _ = _pair_specs(0)
_, K_ALL_SPEC = _pair_specs(NPAIR)
_, V_ALL_SPEC = _pair_specs(2 * NPAIR)
PAIR_ROWS_SPEC = pl.BlockSpec((TQ, LANES), lambda p, i: (i, p))
PAIR_ALL_SPEC = pl.BlockSpec((S, LANES), lambda p, i: (0, p))
PAIR_TOTAL_SPEC = pl.BlockSpec((2, TQ, TK), lambda p, i: (p, i, 0))


def _head_halves(x):
    first = lax.broadcasted_iota(jnp.int32, x.shape, 1) < HD
    zero = jnp.zeros_like(x)
    return jnp.where(first, x, zero), jnp.where(first, zero, x)


def _join_heads(a, b):
    return jnp.where(lax.broadcasted_iota(jnp.int32, a.shape, 1) < HD, a, b)


def _comm_hooks(comm, refs, n_in, n_out, n_scratch):
    nc = len(comm.arrs) if comm is not None else 0
    ins, cin = refs[:n_in], refs[n_in:n_in + nc]
    outs = refs[n_in + nc:n_in + nc + n_out]
    cout = refs[n_in + nc + n_out:n_in + 2 * nc + n_out]
    scratch = refs[n_in + 2 * nc + n_out:n_in + 2 * nc + n_out + n_scratch]
    sems = refs[n_in + 2 * nc + n_out + n_scratch:]
    phases = comm.phases(cin, cout, sems) if comm is not None else None
    return ins, outs, scratch, phases


def _with_comm(comm, in_specs, out_specs, out_shape, operands, scratch):
    if comm is None:
        return dict(in_specs=in_specs, out_specs=out_specs, out_shape=out_shape, scratch_shapes=scratch), operands
    nc = len(comm.arrs)
    return dict(in_specs=in_specs + [HBM_SPEC] * nc, out_specs=out_specs + [HBM_SPEC] * nc,
                out_shape=out_shape + comm.out_shape, scratch_shapes=scratch + comm.scratch), operands + comm.arrs


def attn_fwd(qkv, name, comm=None):
    n_steps = S // TQ

    def body(*refs):
        (q_ref, k_ref, v_ref), (o_ref, r_ref), (acc_ref, z_even, z_odd, w_ref), phases = _comm_hooks(
            comm, refs, 3, 2, 4)
        p = pl.program_id(0)
        i = pl.program_id(1)
        if phases is not None:
            pl.when(jnp.logical_and(p == 0, i == 0))(phases[0])
            pl.when(jnp.logical_and(p == NPAIR - 1, i == n_steps - 1))(phases[1])
        chains = [(sub, h) for sub in range(NSUB) for h in range(2)]
        q_sub = [_head_halves(q_ref[pl.ds(sub * RS, RS), :] * SCALE) for sub in range(NSUB)]
        after = _tri_and_ones("after")
        below_diagonal = (lax.broadcasted_iota(jnp.int32, (RS, TK), 1)
                          < lax.broadcasted_iota(jnp.int32, (RS, TK), 0))
        base = i * NSUB
        all_subs = list(range(NSUB))

        acc_ref[...] = jnp.zeros_like(acc_ref)
        r_ref[...] = jnp.zeros_like(r_ref)
        w_ref[...] = jnp.zeros_like(w_ref)

        def key_rows(block):
            return pl.ds(pl.multiple_of(block * TK, TK), TK)

        def store_scores(z_ref, block, subs):
            kb = k_ref[key_rows(block), :]
            for c, (sub, h) in enumerate(chains):
                if sub in subs:
                    z_ref[c] = lax.dot_general(q_sub[sub][h], kb, (((1,), (1,)), ((), ())),
                                               preferred_element_type=F32)

        def add_weighted_values(block, subs):
            vb = v_ref[key_rows(block), :]
            for sub in subs:
                acc_ref[pl.ds(sub * RS, RS), :] += _join_heads(*[
                    jnp.dot(w_ref[2 * sub + h], vb, preferred_element_type=F32) for h in range(2)])

        def step(block, z_ref, z_next_ref, subs, diagonal_sub, prev_subs, next_subs):
            if prev_subs:
                add_weighted_values(block + 1, prev_subs)
            if next_subs:
                store_scores(z_next_ref, jnp.maximum(block - 1, 0), next_subs)
            active = [(c, sub, h) for c, (sub, h) in enumerate(chains) if sub in subs]
            ls, sums = {}, {}
            for c, sub, h in active:
                ls[c] = _log_stay(z_ref[c])
                sums[c] = _dot_hilo(jnp.where(below_diagonal, ls[c], 0.0) if sub == diagonal_sub else ls[c], after)
            for c, sub, h in active:
                rows = pl.ds(sub * RS, RS)
                later = r_ref[h, rows, :]
                w = jnp.exp(z_ref[c] + ls[c] + (sums[c][:, :TK] + later))
                if sub == diagonal_sub:
                    w = jnp.where(below_diagonal, w, 0.0)
                w_ref[c] = w.astype(BF16)
                r_ref[h, rows, :] = later + sums[c][:, TK:]

        store_scores(z_even, base + NSUB - 1, [NSUB - 1])
        buffers = (z_even, z_odd)
        for j in reversed(range(NSUB)):
            subs = all_subs[j:]
            step(base + j, buffers[0], buffers[1], subs, j, all_subs[j + 1:], all_subs[j - 1:] if j else all_subs)
            buffers = buffers[::-1]
        assert buffers[0] is z_even

        @pl.loop(0, base // 2)
        def _(pair):
            block = base - 1 - 2 * pair
            step(block, z_even, z_odd, all_subs, None, all_subs, all_subs)
            step(block - 1, z_odd, z_even, all_subs, None, all_subs, all_subs)

        add_weighted_values(0, all_subs)
        o_ref[...] = acc_ref[...].astype(o_ref.dtype)
        if phases is not None:
            pl.when(jnp.logical_and(p == NPAIR - 1, i == n_steps - 1))(phases[2])

    kwargs, operands = _with_comm(
        comm, [Q_ROWS_SPEC, K_ALL_SPEC, V_ALL_SPEC], [PAIR_ROWS_SPEC, PAIR_TOTAL_SPEC],
        [jax.ShapeDtypeStruct((S, NH * HD), BF16), jax.ShapeDtypeStruct((NH, S, TK), F32)], [qkv, qkv, qkv],
        [pltpu.VMEM((TQ, LANES), F32), pltpu.VMEM((2 * NSUB, RS, TK), F32), pltpu.VMEM((2 * NSUB, RS, TK), F32),
         pltpu.VMEM((2 * NSUB, RS, TK), BF16)])
    return pl.pallas_call(
        body, name=name, grid=(NPAIR, n_steps),
        compiler_params=_cparams(("arbitrary", "arbitrary")), **kwargs,
    )(*operands)


def attn_bwd(qkv, dout, totals, name, comm=None):
    n_steps = S // TQ

    def body(*refs):
        ((q_ref, k_ref, v_ref, do_ref, r_ref), (dq_out, dk_out, dv_out),
         (z_even, z_odd, dw_even, dw_odd, before_ref, dbefore_ref, dz_ref, w_ref, dq_ref, dk_ref, dv_ref),
         phases) = _comm_hooks(comm, refs, 5, 3, 11)
        p = pl.program_id(0)
        i = pl.program_id(1)
        if phases is not None:
            pl.when(jnp.logical_and(p == 0, i == 0))(phases[0])
            pl.when(jnp.logical_and(p == NPAIR - 1, i == n_steps - 2))(phases[1])

        @pl.when(i == 0)
        def _():
            dk_ref[...] = jnp.zeros_like(dk_ref)
            dv_ref[...] = jnp.zeros_like(dv_ref)

        chains = [(sub, h) for sub in range(NSUB) for h in range(2)]
        nch = len(chains)
        qb = q_ref[...]
        dob = do_ref[...].astype(BF16)
        q_sub = [_head_halves(qb[sub * RS:(sub + 1) * RS] * SCALE) for sub in range(NSUB)]
        do_sub = [_head_halves(dob[sub * RS:(sub + 1) * RS]) for sub in range(NSUB)]
        upto = _tri_and_ones("upto")
        before_tri = _tri_and_ones("before")
        below_diagonal = (lax.broadcasted_iota(jnp.int32, (RS, TK), 1)
                          < lax.broadcasted_iota(jnp.int32, (RS, TK), 0))
        contract_lanes = (((1,), (1,)), ((), ()))
        contract_rows = (((0,), (0,)), ((), ()))
        base = i * NSUB
        all_subs = list(range(NSUB))

        def key_rows(block):
            return pl.ds(pl.multiple_of(block * TK, TK), TK)

        def store_products(bufs, block, subs):
            z_ref, dw_ref = bufs
            kb = k_ref[key_rows(block), :]
            vb = v_ref[key_rows(block), :]
            for c, (sub, h) in enumerate(chains):
                if sub in subs:
                    z_ref[c] = lax.dot_general(q_sub[sub][h], kb, contract_lanes, preferred_element_type=F32)
                    dw_ref[c] = lax.dot_general(do_sub[sub][h], vb, contract_lanes, preferred_element_type=F32)

        def add_gradients(block, subs):
            kb = k_ref[key_rows(block), :]
            for sub in subs:
                rows = pl.ds(sub * RS, RS)
                dq_ref[rows, :] += _join_heads(*[jnp.dot(dz_ref[h, rows, :], kb, preferred_element_type=F32)
                                                 for h in range(2)])
            dk_ref[key_rows(block), :] += _join_heads(*[
                lax.dot_general(dz_ref[h], qb, contract_rows, preferred_element_type=F32) for h in range(2)])
            dv_ref[key_rows(block), :] += _join_heads(*[
                lax.dot_general(w_ref[h], dob, contract_rows, preferred_element_type=F32) for h in range(2)])

        for ref in (dq_ref, before_ref, dbefore_ref, dz_ref, w_ref):
            ref[...] = jnp.zeros_like(ref)
        even, odd = (z_even, dw_even), (z_odd, dw_odd)
        store_products(even, 0, all_subs)

        def step(block, bufs, next_bufs, subs, diagonal_sub, prev_subs, next_subs):
            z_ref, dw_ref = bufs
            add_gradients(jnp.maximum(block - 1, 0), prev_subs)
            for sub in prev_subs:
                if sub not in subs:
                    dz_ref[:, pl.ds(sub * RS, RS), :] = jnp.zeros((2, RS, TK), BF16)
                    w_ref[:, pl.ds(sub * RS, RS), :] = jnp.zeros((2, RS, TK), BF16)
            if next_subs:
                store_products(next_bufs, block + 1, next_subs)
            active = [(c, sub, h) for c, (sub, h) in enumerate(chains) if sub in subs]
            ls, sums, dl, dsums = {}, {}, {}, {}
            for c, sub, h in active:
                ls[c] = _log_stay(z_ref[c])
                sums[c] = _dot_hilo(jnp.where(below_diagonal, ls[c], 0.0) if sub == diagonal_sub else ls[c], upto)
            for c, sub, h in active:
                rows = pl.ds(sub * RS, RS)
                before = before_ref[c]
                log_after = r_ref[h, rows, :] - (sums[c][:, :TK] + before)
                w = jnp.exp((z_ref[c] + ls[c]) + log_after)
                if sub == diagonal_sub:
                    w = jnp.where(below_diagonal, w, 0.0)
                dl[c] = dw_ref[c] * w
                dsums[c] = _dot_hilo(dl[c], before_tri)
                w_ref[h, rows, :] = w.astype(BF16)
                before_ref[c] = before + sums[c][:, TK:]
            for c, sub, h in active:
                rows = pl.ds(sub * RS, RS)
                dbefore = dbefore_ref[c]
                beta = jnp.exp(z_ref[c] + ls[c])
                if sub == diagonal_sub:
                    beta = jnp.where(below_diagonal, beta, 0.0)
                dstay = dsums[c][:, :TK] + dbefore
                dz_ref[h, rows, :] = ((dl[c] - beta * (dl[c] + dstay)) * SCALE).astype(BF16)
                dbefore_ref[c] = dbefore + dsums[c][:, TK:]

        @pl.loop(0, base // 2)
        def _(pair):
            step(2 * pair, even, odd, all_subs, None, all_subs, all_subs)
            step(2 * pair + 1, odd, even, all_subs, None, all_subs, all_subs)

        bufs = (even, odd)
        for j in range(NSUB):
            step(base + j, bufs[0], bufs[1], all_subs[j:], j, all_subs[j - 1:] if j else all_subs, all_subs[j + 1:])
            bufs = bufs[::-1]

        add_gradients(base + NSUB - 1, all_subs[NSUB - 1:])
        dq_out[...] = dq_ref[...].astype(dq_out.dtype)

        @pl.when(i == n_steps - 1)
        def _():
            dk_out[...] = dk_ref[...].astype(dk_out.dtype)
            dv_out[...] = dv_ref[...].astype(dv_out.dtype)

        if phases is not None:
            pl.when(jnp.logical_and(p == NPAIR - 1, i == n_steps - 1))(phases[2])

    full = jax.ShapeDtypeStruct((S, NH * HD), BF16)
    kwargs, operands = _with_comm(
        comm, [Q_ROWS_SPEC, K_ALL_SPEC, V_ALL_SPEC, PAIR_ROWS_SPEC, PAIR_TOTAL_SPEC],
        [PAIR_ROWS_SPEC, PAIR_ALL_SPEC, PAIR_ALL_SPEC], [full, full, full], [qkv, qkv, qkv, dout, totals],
        [pltpu.VMEM((2 * NSUB, RS, TK), F32)] * 6 + [pltpu.VMEM((2, TQ, TK), BF16)] * 2
        + [pltpu.VMEM((TQ, LANES), F32), pltpu.VMEM((S, LANES), F32), pltpu.VMEM((S, LANES), F32)])
    return pl.pallas_call(
        body, name=name, grid=(NPAIR, n_steps),
        compiler_params=_cparams(("arbitrary", "arbitrary")), **kwargs,
    )(*operands)


def _proj_cols(first_col):
    base = first_col // LANES
    return pl.BlockSpec((S, LANES), lambda j: (0, base + j))


CONV_OUT_SPEC = pl.BlockSpec((S, LANES), lambda j: (0, j))
CONV_DOUT_SPEC = pl.BlockSpec((S, LANES), lambda j: (0, (NH * HD) // LANES + j))
CONV_W_SPEC = pl.BlockSpec((8, LANES), lambda j: (0, j))
CONV_B_SPEC = pl.BlockSpec((1, LANES), lambda j: (0, j))


def _shift_down(u, n):
    rows = lax.broadcasted_iota(jnp.int32, u.shape, 0)
    return jnp.where(rows >= n, pltpu.roll(u, n, 0), 0.0)


def _shift_up(u, n):
    rows = lax.broadcasted_iota(jnp.int32, u.shape, 0)
    return jnp.where(rows < S - n, pltpu.roll(u, S - n, 0), 0.0)


def conv_fwd(proj, cw8, cb, name):
    def body(bg_ref, cg_ref, hc_ref, w_ref, b_ref, o_ref):
        u = cg_ref[...] * hc_ref[...]
        w = w_ref[...]
        y = w[0:1, :] * _shift_down(u, 2) + w[1:2, :] * _shift_down(u, 1) + w[2:3, :] * u + b_ref[...]
        o_ref[...] = bg_ref[...] * y

    return pl.pallas_call(
        body, name=name, grid=(CW // LANES,),
        in_specs=[_proj_cols(0), _proj_cols(CW), _proj_cols(2 * CW), CONV_W_SPEC, CONV_B_SPEC],
        out_specs=CONV_OUT_SPEC, out_shape=jax.ShapeDtypeStruct((S, CW), F32),
        compiler_params=_cparams(("parallel",)),
    )(proj, proj, proj, cw8, cb)


def conv_bwd(proj, dout, cw8, cb, name):
    def body(bg_ref, cg_ref, hc_ref, do_ref, w_ref, b_ref, dbg_ref, dcg_ref, dhc_ref, dw_ref, db_ref):
        cg, hc, do = cg_ref[...], hc_ref[...], do_ref[...]
        w = w_ref[...]
        u = cg * hc
        u1, u2 = _shift_down(u, 1), _shift_down(u, 2)
        y = w[0:1, :] * u2 + w[1:2, :] * u1 + w[2:3, :] * u + b_ref[...]
        dbg_ref[...] = (do * y).astype(dbg_ref.dtype)
        dy = do * bg_ref[...]
        db_ref[...] = jnp.sum(dy, axis=0, keepdims=True)
        dw_ref[...] = jnp.concatenate(
            [jnp.sum(dy * u2, axis=0, keepdims=True), jnp.sum(dy * u1, axis=0, keepdims=True),
             jnp.sum(dy * u, axis=0, keepdims=True), jnp.zeros((5, LANES), F32)], axis=0)
        du = w[2:3, :] * dy + w[1:2, :] * _shift_up(dy, 1) + w[0:1, :] * _shift_up(dy, 2)
        dcg_ref[...] = (du * hc).astype(dcg_ref.dtype)
        dhc_ref[...] = (du * cg).astype(dhc_ref.dtype)

    full = jax.ShapeDtypeStruct((S, CW), BF16)
    return pl.pallas_call(
        body, name=name, grid=(CW // LANES,),
        in_specs=[_proj_cols(0), _proj_cols(CW), _proj_cols(2 * CW), CONV_DOUT_SPEC, CONV_W_SPEC, CONV_B_SPEC],
        out_specs=[CONV_OUT_SPEC, CONV_OUT_SPEC, CONV_OUT_SPEC, CONV_W_SPEC, CONV_B_SPEC],
        out_shape=[full, full, full, jax.ShapeDtypeStruct((8, CW), F32), jax.ShapeDtypeStruct((1, CW), F32)],
        compiler_params=_cparams(("parallel",)),
    )(proj, proj, proj, dout, cw8, cb)


GELU_K = math.sqrt(2.0 / math.pi)
GELU_C = 0.044715


def _gelu(x):
    return 0.5 * x * (1.0 + jnp.tanh(GELU_K * (x + GELU_C * (x * x * x))))


def _gelu_grad(x):
    t = jnp.tanh(GELU_K * (x + GELU_C * (x * x * x)))
    return 0.5 * (1.0 + t) + 0.5 * x * (1.0 - t * t) * (GELU_K * (1.0 + 3.0 * GELU_C * (x * x)))


def _sg_masks():
    row = lax.broadcasted_iota(jnp.int32, (T, T), 0)
    col = lax.broadcasted_iota(jnp.int32, (T, T), 1)
    causal = jnp.right_shift(row, 6) >= jnp.right_shift(col, 6)
    head_of_col = jnp.right_shift(lax.broadcasted_iota(jnp.int32, (T, CW), 1), 6)
    return causal, head_of_col


def _sg_weights(sw_ref, causal):
    return [jnp.where(causal, sw_ref[h], 0.0).astype(BF16) for h in range(SG_HEADS)]


def _sg_mixed(vnb, weights, bias, head_of_col):
    mixed = bias
    for h in range(SG_HEADS):
        mh = jnp.dot(weights[h], vnb, preferred_element_type=F32)
        mixed = mixed + jnp.where(head_of_col == h, mh, 0.0)
    return mixed


SG_WINDOWS = 4
SG_ROWS = SG_WINDOWS * T
SG_U_SPEC = pl.BlockSpec((SG_ROWS, CW), lambda n: (n, 3))
SG_V_SPEC = pl.BlockSpec((SG_ROWS, CW), lambda n: (n, 4))
SG_ROW_SPEC = pl.BlockSpec((SG_ROWS, CW), lambda n: (n, 0))
SG_DOUT_SPEC = pl.BlockSpec((SG_ROWS, CW), lambda n: (n, 3))
SG_G_SPEC = pl.BlockSpec((1, CW), lambda n: (0, 0))
SG_W_SPEC = pl.BlockSpec((SG_HEADS, T, T), lambda n: (0, 0, 0))
SG_BIAS_SPEC = pl.BlockSpec((T, CW), lambda n: (0, 0))


def sg_fwd(proj, gn, sw, bias, name):
    def body(u_ref, v_ref, g_ref, sw_ref, bias_ref, o_ref):
        causal, head_of_col = _sg_masks()
        weights = _sg_weights(sw_ref, causal)
        for wdw in range(SG_WINDOWS):
            rows = pl.ds(wdw * T, T)
            gv = _gelu(v_ref[rows, :])
            rstd = lax.rsqrt(jnp.mean(gv * gv, axis=-1, keepdims=True) + EPS)
            vnb = ((gv * rstd) * g_ref[...]).astype(BF16)
            mixed = _sg_mixed(vnb, weights, bias_ref[...], head_of_col)
            o_ref[rows, :] = _gelu(u_ref[rows, :]) * mixed

    return pl.pallas_call(
        body, name=name, grid=(S // SG_ROWS,),
        in_specs=[SG_U_SPEC, SG_V_SPEC, SG_G_SPEC, SG_W_SPEC, SG_BIAS_SPEC],
        out_specs=SG_ROW_SPEC, out_shape=jax.ShapeDtypeStruct((S, CW), F32),
        compiler_params=_cparams(("parallel",)),
    )(proj, proj, gn, sw, bias)


def sg_bwd(proj, dout, gn, sw, bias, name):
    def body(u_ref, v_ref, do_ref, g_ref, sw_ref, bias_ref, du_ref, dv_ref, dg_ref, dsw_ref, dbias_ref):
        @pl.when(pl.program_id(0) == 0)
        def _():
            dg_ref[...] = jnp.zeros_like(dg_ref)
            dsw_ref[...] = jnp.zeros_like(dsw_ref)
            dbias_ref[...] = jnp.zeros_like(dbias_ref)

        causal, head_of_col = _sg_masks()
        weights = _sg_weights(sw_ref, causal)
        gnv = g_ref[...]
        for wdw in range(SG_WINDOWS):
            rows = pl.ds(wdw * T, T)
            uv, vv, do = u_ref[rows, :], v_ref[rows, :], do_ref[rows, :]
            gv = _gelu(vv)
            rstd = lax.rsqrt(jnp.mean(gv * gv, axis=-1, keepdims=True) + EPS)
            xhat = gv * rstd
            vnb = (xhat * gnv).astype(BF16)
            mixed = _sg_mixed(vnb, weights, bias_ref[...], head_of_col)
            du_ref[rows, :] = ((do * mixed) * _gelu_grad(uv)).astype(du_ref.dtype)
            dmix = do * _gelu(uv)
            dbias_ref[...] += dmix
            dmixb = dmix.astype(BF16)
            dvn = jnp.zeros((T, CW), F32)
            for h in range(SG_HEADS):
                dvh = lax.dot_general(weights[h], dmixb, (((0,), (0,)), ((), ())), preferred_element_type=F32)
                dvn = dvn + jnp.where(head_of_col == h, dvh, 0.0)
                dmh = jnp.where(head_of_col == h, dmixb, jnp.zeros_like(dmixb))
                dwh = lax.dot_general(dmh, vnb, (((1,), (1,)), ((), ())), preferred_element_type=F32)
                dsw_ref[h] += jnp.where(causal, dwh, 0.0)
            dg_ref[...] += jnp.sum(dvn * xhat, axis=0, keepdims=True)
            dxhat = dvn * gnv
            dgv = rstd * (dxhat - xhat * jnp.mean(dxhat * xhat, axis=-1, keepdims=True))
            dv_ref[rows, :] = (dgv * _gelu_grad(vv)).astype(dv_ref.dtype)

    full = jax.ShapeDtypeStruct((S, CW), BF16)
    return pl.pallas_call(
        body, name=name, grid=(S // SG_ROWS,),
        in_specs=[SG_U_SPEC, SG_V_SPEC, SG_DOUT_SPEC, SG_G_SPEC, SG_W_SPEC, SG_BIAS_SPEC],
        out_specs=[SG_ROW_SPEC, SG_ROW_SPEC, SG_G_SPEC, SG_W_SPEC, SG_BIAS_SPEC],
        out_shape=[full, full, jax.ShapeDtypeStruct((1, CW), F32),
                   jax.ShapeDtypeStruct((SG_HEADS, T, T), F32), jax.ShapeDtypeStruct((T, CW), F32)],
        compiler_params=_cparams(("arbitrary",)),
    )(proj, proj, dout, gn, sw, bias)


ADA_COLS = NMOD * D // NDEV


def ada_fwd(c_all, ada_w, ada_b_mine, name):
    def body(c_ref, w_ref, b_ref, o_ref, ca_ref):
        cv = c_ref[...]
        ca = cv * (1.0 / (1.0 + jnp.exp(-cv)))
        ca_ref[...] = ca
        cab = ca.astype(BF16)
        for l in range(L):
            o_ref[l] = jnp.dot(cab, w_ref[l].astype(BF16), preferred_element_type=F32) + b_ref[l]

    return pl.pallas_call(
        body, name=name,
        out_shape=[jax.ShapeDtypeStruct((L, NDEV, ADA_COLS), F32), jax.ShapeDtypeStruct((NDEV, D), F32)],
        compiler_params=_cparams(),
    )(c_all, ada_w, ada_b_mine)


def ada_bwd(ca, dmod_cols, name):
    def body(ca_ref, dm_ref, o_ref):
        cab = ca_ref[...].astype(BF16)
        for l in range(L):
            o_ref[l] = lax.dot_general(cab, dm_ref[l].astype(BF16), (((0,), (0,)), ((), ())),
                                       preferred_element_type=F32)

    return pl.pallas_call(
        body, name=name, out_shape=jax.ShapeDtypeStruct((L, D, ADA_COLS), F32),
        compiler_params=_cparams(),
    )(ca, dmod_cols)


def _adamw(w, g, m, v):
    m = B1 * m + (1.0 - B1) * g
    v = B2 * v + (1.0 - B2) * (g * g)
    m_hat = m / BC1
    v_hat = v / BC2
    delta = -LR * (m_hat / (jnp.sqrt(v_hat) + AEPS) + WD * w)
    return delta, m, v


VEC_ROWS_PER_LAYER = 8
VEC_FINAL_ROW = L * VEC_ROWS_PER_LAYER
VEC_ROWS = VEC_FINAL_ROW + 8
W256_TAPS, W256_CONV_B, W256_GN = 0, 8, 9
W256_ROWS_PER_LAYER = 16


def small_update(vec_all, w256_all, sb_all, sw_all, params, name):
    n_par = len(params)

    def body(*refs):
        vec_ref, w256_ref, sb_ref = refs[:3]
        sw_refs = refs[3:3 + L]
        par_refs = [refs[3 + L + 3 * k:3 + L + 3 * k + 3] for k in range(n_par)]
        out = refs[3 + L + 3 * n_par:]
        out_par = [out[4 * k:4 * k + 4] for k in range(n_par)]
        loss_ref, taps_ref = out[4 * n_par:]

        def total(ref, idx):
            acc = ref[(0,) + idx].astype(F32)
            for d in range(1, NDEV):
                acc = acc + ref[(d,) + idx].astype(F32)
            return acc

        def update(k, region, g):
            w_ref, m_ref, v_ref = par_refs[k]
            g_ref, d_ref, nm_ref, nv_ref = out_par[k]
            delta, nm, nv = _adamw(w_ref[region], g, m_ref[region], v_ref[region])
            g_ref[region] = g
            d_ref[region] = delta
            nm_ref[region] = nm
            nv_ref[region] = nv

        for l in range(L):
            base = l * VEC_ROWS_PER_LAYER
            for k in range(NMOD):
                update(0, (slice(l, l + 1), slice(k * D, (k + 1) * D)), total(vec_ref, (slice(base + k, base + k + 1),)))
            update(1, (slice(l, l + 1),), total(vec_ref, (slice(base + 6, base + 7),)))
            update(2, (slice(l, l + 1),), total(vec_ref, (slice(base + 7, base + 8),)))
            wbase = l * W256_ROWS_PER_LAYER
            update(4, (slice(l, l + 1),), total(w256_ref, (slice(wbase + W256_CONV_B, wbase + W256_CONV_B + 1),)))
            update(5, (slice(l, l + 1),), total(w256_ref, (slice(wbase + W256_GN, wbase + W256_GN + 1),)))
            update(6, (l,), total(sw_refs[l], ()))
            update(7, (l,), total(sb_ref, (slice(l * SG_HEADS, (l + 1) * SG_HEADS),)))
            taps_ref[l] = total(w256_ref, (slice(wbase + W256_TAPS, wbase + W256_TAPS + 8),))
        update(3, (slice(0, 1),), total(vec_ref, (slice(VEC_FINAL_ROW, VEC_FINAL_ROW + 1),)))
        loss_ref[...] = total(vec_ref, (slice(VEC_FINAL_ROW + 1, VEC_FINAL_ROW + 2), slice(0, LANES)))

    out_shape = []
    for w, _, _ in params:
        out_shape += [jax.ShapeDtypeStruct(w.shape, F32)] * 4
    out_shape += [jax.ShapeDtypeStruct((1, LANES), F32), jax.ShapeDtypeStruct((L, 8, CW), F32)]
    outs = pl.pallas_call(body, name=name, out_shape=out_shape, compiler_params=_cparams())(
        vec_all, w256_all, sb_all, *sw_all, *[a for p in params for a in p])
    return [outs[4 * k:4 * k + 4] for k in range(n_par)], outs[4 * n_par:]


def adamw_plain(w, g, m, v, tr, name):
    rows, cols = w.shape
    spec = pl.BlockSpec((tr, cols), lambda i: (i, 0))

    def body(w_ref, g_ref, m_ref, v_ref, d_ref, nm_ref, nv_ref):
        delta, nm, nv = _adamw(w_ref[...], g_ref[...], m_ref[...], v_ref[...])
        d_ref[...] = delta
        nm_ref[...] = nm
        nv_ref[...] = nv

    shp = jax.ShapeDtypeStruct((rows, cols), F32)
    return pl.pallas_call(
        body, name=name, grid=(rows // tr,), in_specs=[spec] * 4, out_specs=[spec] * 3,
        out_shape=[shp, shp, shp], compiler_params=_cparams(("parallel",)),
    )(w, g, m, v)


def adamw_reduce(w, parts, m, v, tr, name, tie=None, comm=None):
    _, rows, cols = w.shape
    n_rows = rows // tr
    spec = pl.BlockSpec((None, tr, cols), lambda l, i: (l, i, 0))
    pspecs = [pl.BlockSpec((NDEV, tr, cols), lambda l, i, k=k: (0, jnp.where(l == k, i, 0), 0)) for k in range(L)]
    ties = [] if tie is None else [tie]
    n_in = 3 + L + len(ties)

    def body(*refs):
        ins, (g_ref, d_ref, nm_ref, nv_ref), _, phases = _comm_hooks(comm, refs, n_in, 4, 0)
        w_ref, p0_ref, p1_ref, m_ref, v_ref = ins[:5]
        layer, i = pl.program_id(0), pl.program_id(1)
        if phases is not None:
            pl.when(jnp.logical_and(layer == 0, i == 0))(phases[0])
            pl.when(jnp.logical_and(layer == L - 1, i == 0))(phases[1])
        first_layer = layer == 0
        g = jnp.zeros((tr, cols), F32)
        for d in range(NDEV):
            g = g + jnp.where(first_layer, p0_ref[d], p1_ref[d]).astype(F32)
        delta, nm, nv = _adamw(w_ref[...], g, m_ref[...], v_ref[...])
        g_ref[...] = g
        d_ref[...] = delta
        nm_ref[...] = nm
        nv_ref[...] = nv
        if phases is not None:
            pl.when(jnp.logical_and(layer == L - 1, i == n_rows - 1))(phases[2])

    shp = jax.ShapeDtypeStruct(w.shape, F32)
    kwargs, operands = _with_comm(
        comm, [spec] + pspecs + [spec, spec] + [pl.BlockSpec(t.shape, lambda l, i: (0, 0)) for t in ties],
        [spec] * 4, [shp] * 4, [w, *parts, m, v, *ties], [])
    semantics = ("parallel", "parallel") if comm is None else ("arbitrary", "arbitrary")
    return pl.pallas_call(body, name=name, grid=(L, n_rows), compiler_params=_cparams(semantics), **kwargs)(*operands)


SHARD_IN = PROJ // NDEV


def shards_to_columns(shards, name):
    tr = 256

    def body(i_ref, o_ref):
        for d in range(NDEV):
            o_ref[:, d * SHARD_IN:(d + 1) * SHARD_IN] = i_ref[d]

    return pl.pallas_call(
        body, name=name, grid=(D // tr,),
        in_specs=[pl.BlockSpec((NDEV, tr, SHARD_IN), lambda i: (0, i, 0))],
        out_specs=pl.BlockSpec((tr, PROJ), lambda i: (i, 0)),
        out_shape=jax.ShapeDtypeStruct((D, PROJ), shards.dtype), compiler_params=_cparams(("parallel",)),
    )(shards)


def columns_to_shards(mat, name):
    tr = 256

    def body(i_ref, o_ref):
        for d in range(NDEV):
            o_ref[d] = i_ref[:, d * SHARD_IN:(d + 1) * SHARD_IN]

    return pl.pallas_call(
        body, name=name, grid=(D // tr,),
        in_specs=[pl.BlockSpec((tr, PROJ), lambda i: (i, 0))],
        out_specs=pl.BlockSpec((NDEV, tr, SHARD_IN), lambda i: (0, i, 0)),
        out_shape=jax.ShapeDtypeStruct((NDEV, D, SHARD_IN), mat.dtype), compiler_params=_cparams(("parallel",)),
    )(mat)


def _pad_rows(flat, rows):
    return jnp.pad(flat, (0, rows * LANES - flat.shape[0])).reshape(rows, LANES)


def kernel(x, c, ada_w, ada_b, norm_mix_g, norm_mlp_g, w_in, conv_w, conv_b, gmlp_norm_g, spatial_w, spatial_b, w_out, mlp_w1, mlp_w2, final_norm_g, loss_target, m_ada_w, m_ada_b, m_norm_mix_g, m_norm_mlp_g, m_w_in, m_conv_w, m_conv_b, m_gmlp_norm_g, m_spatial_w, m_spatial_b, m_w_out, m_mlp_w1, m_mlp_w2, m_final_norm_g, v_ada_w, v_ada_b, v_norm_mix_g, v_norm_mlp_g, v_w_in, v_conv_w, v_conv_b, v_gmlp_norm_g, v_spatial_w, v_spatial_b, v_w_out, v_mlp_w1, v_mlp_w2, v_final_norm_g):
    me = _lin(_my_pos())
    x0 = x[0]
    target = loss_target[0]
    conv_shard = conv_w.shape[-1]

    w_in_b, w_out_b, w1_b, w2_b = [w.astype(BF16) for w in (w_in, w_out, mlp_w1, mlp_w2)]
    pack0 = _pad_rows(jnp.concatenate([c.reshape(-1), conv_w.reshape(-1)]), 16)
    g0, gw_in0 = run_comm(Gather([pack0, w_in_b[0]]), "gather_first")
    g0 = g0.reshape(NDEV, 16 * LANES)
    c_all = g0[:, :D]
    conv_full = (g0[:, D:D + L * 3 * conv_shard].reshape(NDEV, L, 3, conv_shard)
                 .transpose(1, 2, 0, 3).reshape(L, 3, CW))


    W_in = [shards_to_columns(gw_in0, "w_in_columns0"), None]
    W_out, W1, W2 = [None] * L, [None] * L, [None] * L

    ada_b_mine = lax.dynamic_slice(ada_b, (0, me * ADA_COLS), (L, ADA_COLS)).reshape(L, 1, ADA_COLS)
    mod_part, c_act = ada_fwd(c_all, ada_w, ada_b_mine, "ada_fwd")
    gmod = run_comm(Gather([mod_part]), "gather_mod")[0]
    mod = lax.dynamic_index_in_dim(gmod, me, axis=2, keepdims=False)
    mod = mod.transpose(1, 0, 2).reshape(L, NMOD, 1, D)
    early_weights, token = start_copies([w_out_b[0]], me, "gather_early0_start", True, after=gmod)
    mod = tied(mod, token)

    cw8 = jnp.pad(conv_full, ((0, 0), (0, 5), (0, 0)))
    sg_bias = jnp.repeat(spatial_b.transpose(0, 2, 1), HD, axis=2)

    saved = []
    xl = x0
    for l in range(L):
        sh_m, sc_m, g_m, sh_f, sc_f, g_f = [mod[l, k] for k in range(NMOD)]
        h1 = normmod_fwd(xl, norm_mix_g[l:l + 1], sc_m, sh_m, f"norm_mix_fwd{l}")
        if l > 0:
            gw_in, gw_out = finish_copies(early_weights, xl, f"gather_early{l}_wait")
            W_in[l] = shards_to_columns(gw_in, f"w_in_columns{l}")
        qkv = mm_layer("proj_qkv", l, h1, W_in[l], out_dtypes=[BF16], cols=(0, QKV))[0]
        proj = mm_layer("proj_rest", l, h1, W_in[l], out_dtypes=[F32], cols=(QKV, REST))[0]
        a_out, a_tot, gw2, gw1 = attn_fwd(qkv, f"attn_fwd{l}", comm=Gather([w2_b[l], w1_b[l]]))
        if l == 0:
            gw_out, = finish_copies(early_weights, a_out, f"gather_early{l}_wait")
        W_out[l] = gw_out.reshape(D, D)
        W1[l] = gw1
        W2[l] = gw2.reshape(DFF, D)
        if l + 1 < L:
            early_weights, token = start_copies([w_in_b[l + 1], w_out_b[l + 1]], me, f"gather_early{l + 1}_start", True,
                                                after=a_out)
            g_m = tied(g_m, token)
        c_out = conv_fwd(proj, cw8[l], conv_b[l:l + 1], f"conv_fwd{l}")
        s_out = sg_fwd(proj, gmlp_norm_g[l:l + 1], spatial_w[l], sg_bias[l], f"sg_fwd{l}")
        cat = jnp.concatenate([a_out, c_out.astype(BF16), s_out.astype(BF16)], axis=1)
        mix, x1, h2 = mm_layer("mix", l, cat, W_out[l], out_dtypes=[F32, F32, BF16], epilogue=_residual_then_norm,
                               extras=[(xl, "tile"), (g_m, "col"), (norm_mlp_g[l:l + 1], "col"), (sc_f, "col"),
                                       (sh_f, "col")])
        ra, r = mm_layer("mlp_up", l, h2, W1[l], out_dtypes=[BF16, BF16], b_blocks=True,
                         epilogue=lambda acc: (jnp.maximum(acc, 0.0), jnp.square(jnp.maximum(acc, 0.0))))
        m2, x2 = mm_layer("mlp_down", l, r, W2[l], out_dtypes=[F32, F32],
                          epilogue=lambda acc, xr, g: (acc, xr + g * acc), extras=[(x1, "tile"), (g_f, "col")])
        saved.append(dict(x=xl, h1=h1, proj=proj, qkv=qkv, a_tot=a_tot, cat=cat, mix=mix,
                          x1=x1, h2=h2, ra=ra, r=r, m2=m2))
        xl = x2

    dx, loss_part, d_final_g, dm2, dg_f = loss_head(xl, target, final_norm_g.reshape(1, D),
                                                    (saved[L - 1]["m2"], mod[L - 1, NMOD - 1]), "loss_head")

    p_in, p_out, p_w1, p_w2 = [None] * L, [None] * L, [None] * L, [None] * L
    grads_in_flight = [None] * L
    vec_rows, d_norm_mix, d_norm_mlp = [None] * L, [None] * L, [None] * L
    dcw8, d_conv_b, d_gn, d_sw, d_sb = [None] * L, [None] * L, [None] * L, [None] * L, [None] * L
    for l in reversed(range(L)):
        sv = saved[l]
        sh_m, sc_m, g_m, sh_f, sc_f, g_f = [mod[l, k] for k in range(NMOD)]
        da = mm_layer("mlp_down_dgrad", l, dm2, W2[l], out_dtypes=[BF16], trans_b=True,
                      epilogue=lambda acc, rav: (acc * (2.0 * rav.astype(F32)),), extras=[(sv["ra"], "tile")])[0]
        dW2 = mm_layer("mlp_down_wgrad", l, sv["r"], dm2, out_dtypes=[BF16], trans_a=True)[0]
        dW1 = mm_layer("mlp_up_wgrad", l, sv["h2"], da, out_dtypes=[BF16], trans_a=True, out_blocks=True)[0]
        dh2 = mm_layer("mlp_up_dgrad", l, da, W1[l], out_dtypes=[F32], trans_b=True, b_blocks=True)[0]
        dx1, dsc_f, dsh_f, d_norm_mlp[l], dmix, dg_m = normmod_bwd(
            sv["x1"], dh2, dx, norm_mlp_g[l:l + 1], sc_f, f"norm_mlp_bwd{l}", gate_next=(sv["mix"], g_m))
        dcat = mm_layer("mix_dgrad", l, dmix, W_out[l], out_dtypes=[F32], trans_b=True)[0]
        dW_out = mm_layer("mix_wgrad", l, sv["cat"], dmix, out_dtypes=[BF16], trans_a=True)[0]
        pieces_w2, pieces_out = dW2.reshape(NDEV, DFF // NDEV, D), dW_out.reshape(NDEV, D // NDEV, D)
        ride, late = ([pieces_w2, pieces_out], dW1) if l == L - 1 else ([pieces_w2, dW1], pieces_out)
        dq, dk, dv, *arrived = attn_bwd(sv["qkv"], dcat, sv["a_tot"], f"attn_bwd{l}", comm=Exchange(ride))
        p_w2[l] = arrived[0]
        (p_out if l == L - 1 else p_w1)[l] = arrived[1]
        dbg, dcg, dhc, dcw8[l], d_conv_b[l] = conv_bwd(sv["proj"], dcat, cw8[l], conv_b[l:l + 1], f"conv_bwd{l}")
        dus, dvs, d_gn[l], dsw, dbias = sg_bwd(sv["proj"], dcat, gmlp_norm_g[l:l + 1], spatial_w[l], sg_bias[l],
                                               f"sg_bwd{l}")
        d_sw[l] = dsw.astype(BF16)
        d_sb[l] = dbias.reshape(T, SG_HEADS, HD).sum(axis=2).T
        dproj = jnp.concatenate([dq, dk, dv, dbg, dcg, dhc, dus, dvs], axis=1).astype(BF16)
        dW_in = mm_layer("proj_wgrad", l, sv["h1"], dproj, out_dtypes=[BF16], trans_a=True)[0]
        pieces = columns_to_shards(dW_in, f"w_in_grad_shards{l}")
        grads_in_flight[l], token = start_copies([late, pieces], me, f"exchange_tail{l}_start", False)
        dh1 = mm_layer("proj_dgrad", l, dproj, W_in[l], out_dtypes=[F32], trans_b=True, extras=[(token, "tie")])[0]
        below = (saved[l - 1]["m2"], mod[l - 1, NMOD - 1]) if l > 0 else None
        dx, dsc_m, dsh_m, d_norm_mix[l], *gated_below = normmod_bwd(
            sv["x"], dh1, dx1, tied(norm_mix_g[l:l + 1], token), sc_m, f"norm_mix_bwd{l}", gate_next=below)
        vec_rows[l] = [dsh_m, dsc_m, dg_m, dsh_f, dsc_f, dg_f, d_norm_mix[l], d_norm_mlp[l]]
        if l > 0:
            dm2, dg_f = gated_below

    grad_x = dx.reshape(1, S, D)

    g_w2, d_w2, nm_w2, nv_w2 = adamw_reduce(mlp_w2, p_w2, m_mlp_w2, v_mlp_w2, 256, "adamw_mlp_w2", tie=token)
    p_w1[L - 1], p_in[L - 1] = finish_copies(grads_in_flight[L - 1], d_w2, f"exchange_tail{L - 1}_wait")

    vec_pack = jnp.concatenate([row for l in range(L) for row in vec_rows[l]]
                               + [d_final_g, loss_part, jnp.zeros((VEC_ROWS - VEC_FINAL_ROW - 2, D), F32)], axis=0)
    vec_pack, _ = lax.optimization_barrier((vec_pack, d_w2))
    w256_pack = jnp.concatenate([blk for l in range(L) for blk in (
        dcw8[l], d_conv_b[l], d_gn[l], jnp.zeros((W256_ROWS_PER_LAYER - W256_GN - 1, CW), F32))], axis=0)
    small_gather = Gather([vec_pack, w256_pack, jnp.concatenate(d_sb, axis=0)] + d_sw)
    g_w1, d_w1, nm_w1, nv_w1, vec_all, w256_all, sb_all, *sw_all = adamw_reduce(
        mlp_w1, p_w1, m_mlp_w1, v_mlp_w1, 256, "adamw_mlp_w1", tie=token, comm=small_gather)

    dmod_all = (vec_all[:, :VEC_FINAL_ROW].reshape(NDEV, L, VEC_ROWS_PER_LAYER, D)[:, :, :NMOD]
                .reshape(NDEV, L, NMOD * D))
    dmod_cols = lax.dynamic_slice(dmod_all, (0, 0, me * ADA_COLS), (NDEV, L, ADA_COLS)).transpose(1, 0, 2)
    g_ada_w = ada_bwd(c_act, dmod_cols, "ada_bwd")

    flat2 = lambda t: t.reshape(L * D, ADA_COLS)
    d_ada_w, nm_ada_w, nv_ada_w = [t.reshape(L, D, ADA_COLS) for t in adamw_plain(
        flat2(ada_w), flat2(g_ada_w), flat2(m_ada_w), flat2(v_ada_w), 256, "adamw_ada_w")]

    after = jnp.concatenate([t.reshape(-1)[:1] for t in (d_w1, d_w2, d_ada_w)])
    p_out[0], p_in[0] = finish_copies(grads_in_flight[0], after, "exchange_tail0_wait")
    g_w_in, d_w_in, nm_w_in, nv_w_in = adamw_reduce(w_in, p_in, m_w_in, v_w_in, 256, "adamw_w_in")
    g_w_out, d_w_out, nm_w_out, nv_w_out = adamw_reduce(w_out, p_out, m_w_out, v_w_out, 128, "adamw_w_out")

    as_row = lambda t: t.reshape(1, D)
    small_params = [(ada_b, m_ada_b, v_ada_b), (norm_mix_g, m_norm_mix_g, v_norm_mix_g),
                    (norm_mlp_g, m_norm_mlp_g, v_norm_mlp_g),
                    (as_row(final_norm_g), as_row(m_final_norm_g), as_row(v_final_norm_g)),
                    (conv_b, m_conv_b, v_conv_b), (gmlp_norm_g, m_gmlp_norm_g, v_gmlp_norm_g),
                    (spatial_w, m_spatial_w, v_spatial_w), (spatial_b, m_spatial_b, v_spatial_b)]
    updated, (loss_sum, taps_sum) = small_update(vec_all, w256_all, sb_all, sw_all, small_params, "small_update")
    loss = loss_sum[0, 0]
    u_ada_b, u_norm_mix, u_norm_mlp, u_final, u_conv_b, u_gn, u_sw, u_sb = updated
    u_final = [t.reshape(D) for t in u_final]
    g_conv_w = lax.dynamic_slice(taps_sum, (0, 0, me * conv_shard), (L, 3, conv_shard))
    flat_cw = lambda t: t.reshape(L * 3, conv_shard)
    u_conv_w = [g_conv_w] + [t.reshape(L, 3, conv_shard) for t in adamw_plain(
        flat_cw(conv_w), flat_cw(g_conv_w), flat_cw(m_conv_w), flat_cw(v_conv_w), L * 3, "adamw_conv_w")]
    small_sets = [u_ada_b, u_norm_mix, u_norm_mlp, u_conv_w, u_conv_b, u_gn, u_sw, u_sb, u_final]
    small_g, sd, snm, snv = [[u[k] for u in small_sets] for k in range(4)]

    def ordered(big, small):
        ada, win, wout, w1, w2 = big
        return [ada, small[0], small[1], small[2], win, small[3], small[4], small[5], small[6], small[7],
                wout, w1, w2, small[8]]

    grads = ordered([g_ada_w, g_w_in, g_w_out, g_w1, g_w2], small_g)
    deltas = ordered([d_ada_w, d_w_in, d_w_out, d_w1, d_w2], sd)
    new_m = ordered([nm_ada_w, nm_w_in, nm_w_out, nm_w1, nm_w2], snm)
    new_v = ordered([nv_ada_w, nv_w_in, nv_w_out, nv_w1, nv_w2], snv)
    return (loss, grad_x, *grads, *deltas, *new_m, *new_v)
```

```python
import functools
import math

import jax
import jax.numpy as jnp
from jax import lax
from jax.experimental import pallas as pl
from jax.experimental.pallas import tpu as pltpu
from jax.experimental.pallas import tpu_sc as plsc

F32 = jnp.float32
BF16 = jnp.bfloat16
MESH = pl.DeviceIdType.MESH

S = 2048
D = 1024
L = 2
NDEV = 8
HD = 64
NH = 8
PROJ = 2816
DFF = 4096
NMOD = 6
EPS = 1e-6
T = 128
SG_HEADS = 4
LANES = 128
CW = 256
QKV = 3 * NH * HD
REST = PROJ - QKV

LR, B1, B2, AEPS, WD, STEP = 0.001, 0.9, 0.999, 1e-08, 0.01, 10
BC1 = 1.0 - B1 ** STEP
BC2 = 1.0 - B2 ** STEP

VMEM_LIMIT = 48 * 1024 * 1024

HBM_SPEC = pl.BlockSpec(memory_space=pltpu.HBM)


def _cparams(sem=None):
    return pltpu.CompilerParams(dimension_semantics=sem, vmem_limit_bytes=VMEM_LIMIT)


def _my_pos():
    return lax.axis_index("x"), lax.axis_index("y"), lax.axis_index("c")


def _lin(p):
    return 4 * p[0] + 2 * p[1] + p[2]


class Gather:
    def __init__(self, arrs):
        self.arrs = list(arrs)
        n = len(self.arrs)
        self.out_shape = [jax.ShapeDtypeStruct((NDEV,) + a.shape, a.dtype) for a in self.arrs]
        self.scratch = [pltpu.SemaphoreType.DMA((n, 7)), pltpu.SemaphoreType.DMA((n, 7)),
                        pltpu.SemaphoreType.DMA((n,))]

    def phases(self, ins, outs, sems):
        n = len(self.arrs)
        send_sems, recv_sems, local_sems = sems
        x, y, c = _my_pos()
        me, sibling = (x, y, c), (x, y, 1 - c)
        chips = [(1 - x, y), (x, 1 - y), (1 - x, 1 - y)]

        def copy(a, k, block, to, src=None):
            slot = outs[a].at[_lin(block)]
            return pltpu.make_async_remote_copy(
                src_ref=slot if src is None else src, dst_ref=slot,
                send_sem=send_sems.at[a, k], recv_sem=recv_sems.at[a, k],
                device_id=to, device_id_type=MESH)

        def mine(a):
            return pltpu.make_async_copy(ins[a], outs[a].at[_lin(me)], local_sems.at[a])

        def first(a):
            return [copy(a, 0, me, sibling, src=ins[a])] + [
                copy(a, 1 + j, me, (*chip, c), src=ins[a]) for j, chip in enumerate(chips)]

        def passed(a):
            return [copy(a, 4 + j, (*chip, c), sibling) for j, chip in enumerate(chips)]

        def start():
            for a in range(n):
                mine(a).start()
                for cp in first(a):
                    cp.start()

        def relay():
            for j, chip in enumerate(chips):
                for a in range(n):
                    copy(a, 1 + j, (*chip, c), me).wait_recv()
                    passed(a)[j].start()

        def finish():
            for a in range(n):
                copy(a, 0, sibling, me).wait_recv()
            for j, chip in enumerate(chips):
                for a in range(n):
                    copy(a, 4 + j, (*chip, 1 - c), me).wait_recv()
            for a in range(n):
                for cp in first(a) + passed(a):
                    cp.wait_send()
                mine(a).wait()

        return start, relay, finish


class Exchange:
    def __init__(self, arrs):
        self.arrs = list(arrs)
        n = len(self.arrs)
        self.out_shape = [jax.ShapeDtypeStruct(a.shape, a.dtype) for a in self.arrs]
        self.scratch = [pltpu.SemaphoreType.DMA((n, 7)), pltpu.SemaphoreType.DMA((n, 7)),
                        pltpu.SemaphoreType.DMA((n,))]

    def phases(self, ins, outs, sems):
        n = len(self.arrs)
        send_sems, recv_sems, local_sems = sems
        x, y, c = _my_pos()
        me = (x, y, c)

        def peer(mask):
            return (1 - x if mask & 4 else x, 1 - y if mask & 2 else y, 1 - c if mask & 1 else c)

        def copy(a, mask):
            return pltpu.make_async_remote_copy(
                src_ref=ins[a].at[_lin(peer(mask))], dst_ref=outs[a].at[_lin(me)],
                send_sem=send_sems.at[a, mask - 1], recv_sem=recv_sems.at[a, mask - 1],
                device_id=peer(mask), device_id_type=MESH)

        def arrival(a, mask):
            return pltpu.make_async_remote_copy(
                src_ref=ins[a].at[_lin(me)], dst_ref=outs[a].at[_lin(peer(mask))],
                send_sem=send_sems.at[a, mask - 1], recv_sem=recv_sems.at[a, mask - 1],
                device_id=peer(mask), device_id_type=MESH)

        def mine(a):
            return pltpu.make_async_copy(ins[a].at[_lin(me)], outs[a].at[_lin(me)], local_sems.at[a])

        def start():
            for a in range(n):
                mine(a).start()
            for mask in (4, 2, 6, 1, 5, 3, 7):
                for a in range(n):
                    copy(a, mask).start()

        def relay():
            pass

        def finish():
            for mask in range(1, 8):
                for a in range(n):
                    arrival(a, mask).wait_recv()
            for mask in range(1, 8):
                for a in range(n):
                    copy(a, mask).wait_send()
            for a in range(n):
                mine(a).wait()

        return start, relay, finish


def run_comm(plan, name):
    n = len(plan.arrs)

    def body(*refs):
        start, relay, finish = plan.phases(refs[:n], refs[n:2 * n], refs[2 * n:])
        start()
        relay()
        finish()

    outs = pl.pallas_call(
        body, name=name, out_shape=plan.out_shape,
        in_specs=[HBM_SPEC] * n, out_specs=[HBM_SPEC] * n, scratch_shapes=plan.scratch,
    )(*plan.arrs)
    return list(outs)


SEM_SPEC = pl.BlockSpec(memory_space=pltpu.SEMAPHORE)
DATAFLOW = pltpu.SideEffectType.DATAFLOW_SIDE_EFFECTING


def _peer_copies(src_ref, land_ref, send_sems, recv_sems, first, same_block):
    x, y, c = _my_pos()
    me = (x, y, c)
    sends, arrivals = [], []
    for mask in (4, 2, 6, 1, 5, 3, 7):
        peer = (1 - x if mask & 4 else x, 1 - y if mask & 2 else y, 1 - c if mask & 1 else c)
        sends.append(pltpu.make_async_remote_copy(
            src_ref=src_ref if same_block else src_ref.at[_lin(peer)], dst_ref=land_ref.at[_lin(me)],
            send_sem=send_sems.at[first + mask - 1], recv_sem=recv_sems.at[first + mask - 1], device_id=peer,
            device_id_type=MESH))
        arrivals.append(pltpu.make_async_remote_copy(
            src_ref=src_ref if same_block else src_ref.at[_lin(me)], dst_ref=land_ref.at[_lin(peer)],
            send_sem=send_sems.at[first + mask - 1], recv_sem=recv_sems.at[first + mask - 1], device_id=peer,
            device_id_type=MESH))
    return sends, arrivals


def start_copies(srcs, me, name, same_block, after=None):
    n = len(srcs)
    landings = []
    for src in srcs:
        own = src[None] if same_block else lax.dynamic_index_in_dim(src, me, axis=0, keepdims=True)
        landings.append(lax.dynamic_update_slice(lax.empty((NDEV,) + own.shape[1:], src.dtype), own,
                                                 (me,) + (0,) * (own.ndim - 1)))

    def body(*refs):
        send_sems, recv_sems = refs[-2 * n - 3], refs[-2 * n - 2]
        token = refs[-1]
        for k in range(n):
            sends, _ = _peer_copies(refs[2 * k], refs[2 * k + 1], send_sems, recv_sems, 7 * k, same_block)
            for cp in sends:
                cp.start()
        token[...] = jnp.zeros_like(token)

    hbm = lambda a: pltpu.HBM(a.shape, a.dtype)
    pairs = [a for pair in zip(srcs, landings) for a in pair]
    extra = [] if after is None else [after]
    sems = pltpu.SemaphoreType.DMA((7 * n,))
    send_sems, recv_sems, *thru, token = pl.pallas_call(
        body, name=name,
        out_shape=(sems, sems, *[hbm(a) for a in pairs], jax.ShapeDtypeStruct((8, LANES), F32)),
        in_specs=[HBM_SPEC] * (2 * n) + [pl.BlockSpec(memory_space=pl.ANY)] * len(extra),
        out_specs=(SEM_SPEC, SEM_SPEC, *[HBM_SPEC] * (2 * n), pl.BlockSpec(memory_space=pltpu.VMEM)),
        input_output_aliases={k: 2 + k for k in range(2 * n)},
        compiler_params=pltpu.CompilerParams(has_side_effects=DATAFLOW),
    )(*[pltpu.with_memory_space_constraint(a, pltpu.HBM) for a in pairs], *extra)
    return (send_sems, recv_sems, thru, same_block), token


def finish_copies(handle, after, name):
    send_sems, recv_sems, thru, same_block = handle
    n = len(thru) // 2

    def body(*refs):
        send_sems, recv_sems = refs[2 * n], refs[2 * n + 1]
        for k in range(n):
            sends, arrivals = _peer_copies(refs[2 * k], refs[2 * k + 1], send_sems, recv_sems, 7 * k, same_block)
            for cp in sends:
                cp.wait_send()
            for cp in arrivals:
                cp.wait_recv()

    hbm = lambda a: pltpu.HBM(a.shape, a.dtype)
    outs = pl.pallas_call(
        body, name=name, out_shape=tuple(hbm(a) for a in thru),
        in_specs=[HBM_SPEC] * (2 * n) + [SEM_SPEC, SEM_SPEC, pl.BlockSpec(memory_space=pl.ANY)],
        out_specs=tuple([HBM_SPEC] * (2 * n)), input_output_aliases={k: k for k in range(2 * n)},
        compiler_params=pltpu.CompilerParams(has_side_effects=DATAFLOW),
    )(*thru, send_sems, recv_sems, after)
    return [outs[2 * k + 1] for k in range(n)]


def tied(x, token):
    return x + token[0:1, 0:1].astype(x.dtype)


MM_TILES = {
    "proj_qkv": (S, 512), "proj_rest": (S, 256), "mix": (512, D), "mlp_up": (S, 512), "mlp_down": (1024, 256),
    "mlp_down_dgrad": (S, 1024), "mlp_down_wgrad": (1024, 1024), "mlp_up_wgrad": (1024, 512),
    "mlp_up_dgrad": (1024, 512), "mix_dgrad": (1024, 512), "mix_wgrad": (512, 1024),
    "proj_wgrad": (1024, PROJ // 2), "proj_dgrad": (1024, 512),
}


def mm_layer(kind, l, a, b, **kw):
    tm, tn = MM_TILES[kind]
    return mm(a, b, tm=tm, tn=tn, name=f"{kind}{l}", **kw)


def mm(a, b, *, tm, tn, out_dtypes, epilogue=None, extras=(), name, trans_a=False, trans_b=False,
       cols=None, b_blocks=False, out_blocks=False):
    if trans_a:
        kdim, m = a.shape
    else:
        m, kdim = a.shape
    shard = b.shape[-1] if b_blocks else None
    if b_blocks:
        full = (b.shape[1], NDEV * shard)
    else:
        full = b.shape
    first, ncols = cols if cols is not None else (0, full[0] if trans_b else full[1])
    assert full[1 if trans_b else 0] == kdim and m % tm == 0 and ncols % tn == 0 and first % tn == 0
    j0 = first // tn
    if trans_a:
        a_spec = pl.BlockSpec((kdim, tm), lambda i, j: (0, i))
    else:
        a_spec = pl.BlockSpec((tm, kdim), lambda i, j: (i, 0))
    if b_blocks and trans_b:
        b_spec = pl.BlockSpec((NDEV, tn, shard), lambda i, j: (0, j0 + j, 0))
    elif b_blocks:
        assert tn == shard
        b_spec = pl.BlockSpec((None, kdim, tn), lambda i, j: (j0 + j, 0, 0))
    elif trans_b:
        b_spec = pl.BlockSpec((tn, kdim), lambda i, j: (j0 + j, 0))
    else:
        b_spec = pl.BlockSpec((kdim, tn), lambda i, j: (0, j0 + j))
    if out_blocks:
        assert tn * NDEV == ncols
        out_spec = pl.BlockSpec((None, tm, tn), lambda i, j: (j, i, 0))
        out_dims = (NDEV, m, tn)
    else:
        out_spec = pl.BlockSpec((tm, tn), lambda i, j: (i, j))
        out_dims = (m, ncols)
    ex_specs = []
    for arr, kind in extras:
        if kind == "tile":
            ex_specs.append(pl.BlockSpec((tm, tn), lambda i, j: (i, j)))
        elif kind == "col":
            ex_specs.append(pl.BlockSpec((1, tn), lambda i, j: (0, j)))
        else:
            ex_specs.append(pl.BlockSpec(arr.shape, lambda i, j: (0, 0)))
    n_ex, n_out = len(extras), len(out_dtypes)
    used = [k for k, (_, kind) in enumerate(extras) if kind != "tie"]

    def body(a_ref, b_ref, *rest):
        ex_refs, out_refs = rest[:n_ex], rest[n_ex:]
        if trans_a:
            acc = lax.dot_general(a_ref[...], b_ref[...], (((0,), (0,)), ((), ())),
                                  preferred_element_type=F32)
        elif trans_b and b_blocks:
            acc = jnp.zeros((tm, tn), F32)
            for d in range(NDEV):
                acc = acc + lax.dot_general(a_ref[:, d * shard:(d + 1) * shard], b_ref[d],
                                            (((1,), (1,)), ((), ())), preferred_element_type=F32)
        elif trans_b:
            acc = lax.dot_general(a_ref[...], b_ref[...], (((1,), (1,)), ((), ())),
                                  preferred_element_type=F32)
        else:
            acc = jnp.dot(a_ref[...], b_ref[...], preferred_element_type=F32)
        outs = (acc,) if epilogue is None else epilogue(acc, *[ex_refs[k][...] for k in used])
        for o_ref, val in zip(out_refs, outs):
            o_ref[...] = val.astype(o_ref.dtype)

    outs = pl.pallas_call(
        body, name=name, grid=(m // tm, ncols // tn),
        in_specs=[a_spec, b_spec] + ex_specs,
        out_specs=[out_spec for _ in range(n_out)],
        out_shape=[jax.ShapeDtypeStruct(out_dims, dt) for dt in out_dtypes],
        compiler_params=_cparams(("parallel", "parallel")),
    )(a, b, *[arr for arr, _ in extras])
    return list(outs)


TR = 512

ROW_SPEC = pl.BlockSpec((TR, D), lambda i: (i, 0))
VEC_SPEC = pl.BlockSpec((1, D), lambda i: (0, 0))


def _residual_then_norm(acc, xr, gate, g, sc, sh):
    x_new = xr + gate * acc
    rstd = lax.rsqrt(jnp.mean(x_new * x_new, axis=-1, keepdims=True) + EPS)
    return acc, x_new, ((x_new * rstd) * g) * (1.0 + sc) + sh


def normmod_fwd(x, g, sc, sh, name):
    def body(x_ref, g_ref, sc_ref, sh_ref, o_ref):
        xv = x_ref[...]
        rstd = lax.rsqrt(jnp.mean(xv * xv, axis=-1, keepdims=True) + EPS)
        n = (xv * rstd) * g_ref[...]
        o_ref[...] = (n * (1.0 + sc_ref[...]) + sh_ref[...]).astype(o_ref.dtype)

    return pl.pallas_call(
        body, name=name, grid=(S // TR,),
        in_specs=[ROW_SPEC, VEC_SPEC, VEC_SPEC, VEC_SPEC], out_specs=ROW_SPEC,
        out_shape=jax.ShapeDtypeStruct((S, D), BF16),
        compiler_params=_cparams(("parallel",)),
    )(x, g, sc, sh)


def _gate_next(dxv, refs):
    br_ref, gate_ref, dbr_ref, dgate_ref = refs

    @pl.when(pl.program_id(0) == 0)
    def _():
        dgate_ref[...] = jnp.zeros_like(dgate_ref)

    dbr_ref[...] = (dxv * gate_ref[...]).astype(dbr_ref.dtype)
    dgate_ref[...] += jnp.sum(dxv * br_ref[...], axis=0, keepdims=True)


GATE_NEXT_IN = [ROW_SPEC, VEC_SPEC]
GATE_NEXT_OUT = [ROW_SPEC, VEC_SPEC]
GATE_NEXT_SHAPES = [jax.ShapeDtypeStruct((S, D), BF16), jax.ShapeDtypeStruct((1, D), F32)]


def normmod_bwd(x, dh, dres, g, sc, name, gate_next=None):
    nxt = 2 if gate_next else 0

    def body(x_ref, dh_ref, dres_ref, g_ref, sc_ref, *rest):
        nxt_in, (dx_ref, dsc_ref, dsh_ref, dg_ref), nxt_out = rest[:nxt], rest[nxt:nxt + 4], rest[nxt + 4:]

        @pl.when(pl.program_id(0) == 0)
        def _():
            dsc_ref[...] = jnp.zeros_like(dsc_ref)
            dsh_ref[...] = jnp.zeros_like(dsh_ref)
            dg_ref[...] = jnp.zeros_like(dg_ref)

        xv, dh = x_ref[...], dh_ref[...]
        gv = g_ref[...]
        rstd = lax.rsqrt(jnp.mean(xv * xv, axis=-1, keepdims=True) + EPS)
        xhat = xv * rstd
        dn = dh * (1.0 + sc_ref[...])
        dxhat = dn * gv
        dxv = dres_ref[...] + rstd * (dxhat - xhat * jnp.mean(dxhat * xhat, axis=-1, keepdims=True))
        dx_ref[...] = dxv
        dsc_ref[...] += jnp.sum(dh * (xhat * gv), axis=0, keepdims=True)
        dsh_ref[...] += jnp.sum(dh, axis=0, keepdims=True)
        dg_ref[...] += jnp.sum(dn * xhat, axis=0, keepdims=True)
        if gate_next:
            _gate_next(dxv, nxt_in + nxt_out)

    vec_out = jax.ShapeDtypeStruct((1, D), F32)
    on = bool(gate_next)
    return pl.pallas_call(
        body, name=name, grid=(S // TR,),
        in_specs=[ROW_SPEC, ROW_SPEC, ROW_SPEC, VEC_SPEC, VEC_SPEC] + GATE_NEXT_IN * on,
        out_specs=[ROW_SPEC, VEC_SPEC, VEC_SPEC, VEC_SPEC] + GATE_NEXT_OUT * on,
        out_shape=[jax.ShapeDtypeStruct((S, D), F32), vec_out, vec_out, vec_out] + GATE_NEXT_SHAPES * on,
        compiler_params=_cparams(("arbitrary",)),
    )(x, dh, dres, g, sc, *(gate_next or ()))


def loss_head(x, target, g, gate_next, name):
    def body(x_ref, t_ref, g_ref, br_ref, gate_ref, dx_ref, loss_ref, dg_ref, dbr_ref, dgate_ref):
        @pl.when(pl.program_id(0) == 0)
        def _():
            loss_ref[...] = jnp.zeros_like(loss_ref)
            dg_ref[...] = jnp.zeros_like(dg_ref)

        xv, gv = x_ref[...], g_ref[...]
        rstd = lax.rsqrt(jnp.mean(xv * xv, axis=-1, keepdims=True) + EPS)
        xhat = xv * rstd
        err = xhat * gv - t_ref[...]
        loss_ref[...] += jnp.sum(err * err) * (0.5 / D)
        dy = err * (1.0 / D)
        dg_ref[...] += jnp.sum(dy * xhat, axis=0, keepdims=True)
        dxhat = dy * gv
        dxv = rstd * (dxhat - xhat * jnp.mean(dxhat * xhat, axis=-1, keepdims=True))
        dx_ref[...] = dxv
        _gate_next(dxv, (br_ref, gate_ref, dbr_ref, dgate_ref))

    return pl.pallas_call(
        body, name=name, grid=(S // TR,),
        in_specs=[ROW_SPEC, ROW_SPEC, VEC_SPEC] + GATE_NEXT_IN,
        out_specs=[ROW_SPEC, VEC_SPEC, VEC_SPEC] + GATE_NEXT_OUT,
        out_shape=[jax.ShapeDtypeStruct((S, D), F32), jax.ShapeDtypeStruct((1, D), F32),
                   jax.ShapeDtypeStruct((1, D), F32)] + GATE_NEXT_SHAPES,
        compiler_params=_cparams(("arbitrary",)),
    )(x, target, g, *gate_next)


TQ = 512
RS = 128
NSUB = TQ // RS
TK = 128


def _dot_hilo(a, tri_twice):
    hi = a.astype(BF16)
    lo = (a - hi.astype(F32)).astype(BF16)
    return jnp.dot(jnp.concatenate([hi, lo], axis=1), tri_twice, preferred_element_type=F32)


def _log_stay(z):
    neg = -z
    return jnp.minimum(neg, 0.0) - jnp.log(1.0 + jnp.exp(jnp.minimum(z, neg)))


def _tri_and_ones(kind):
    row = jnp.bitwise_and(lax.broadcasted_iota(jnp.int32, (2 * TK, 2 * TK), 0), TK - 1)
    col = lax.broadcasted_iota(jnp.int32, (2 * TK, 2 * TK), 1)
    tri = {"after": row > col, "upto": row <= col, "before": row < col}[kind]
    return jnp.logical_or(col >= TK, tri).astype(BF16)


NPAIR = NH // 2
SCALE = HD ** -0.5


def _pair_specs(first_block):
    rows = pl.BlockSpec((TQ, LANES), lambda p, i: (i, first_block + p))
    whole = pl.BlockSpec((S, LANES), lambda p, i: (0, first_block + p))
    return rows, whole


Q_ROWS_SPEC, _ = _pair_specs(0)
_, K_ALL_SPEC = _pair_specs(NPAIR)
_, V_ALL_SPEC = _pair_specs(2 * NPAIR)
PAIR_ROWS_SPEC = pl.BlockSpec((TQ, LANES), lambda p, i: (i, p))
PAIR_ALL_SPEC = pl.BlockSpec((S, LANES), lambda p, i: (0, p))
PAIR_TOTAL_SPEC = pl.BlockSpec((2, TQ, TK), lambda p, i: (p, i, 0))


def _head_halves(x):
    first = lax.broadcasted_iota(jnp.int32, x.shape, 1) < HD
    zero = jnp.zeros_like(x)
    return jnp.where(first, x, zero), jnp.where(first, zero, x)


def _join_heads(a, b):
    return jnp.where(lax.broadcasted_iota(jnp.int32, a.shape, 1) < HD, a, b)


def _comm_hooks(comm, refs, n_in, n_out, n_scratch):
    nc = len(comm.arrs) if comm is not None else 0
    ins, cin = refs[:n_in], refs[n_in:n_in + nc]
    outs = refs[n_in + nc:n_in + nc + n_out]
    cout = refs[n_in + nc + n_out:n_in + 2 * nc + n_out]
    scratch = refs[n_in + 2 * nc + n_out:n_in + 2 * nc + n_out + n_scratch]
    sems = refs[n_in + 2 * nc + n_out + n_scratch:]
    phases = comm.phases(cin, cout, sems) if comm is not None else None
    return ins, outs, scratch, phases


def _with_comm(comm, in_specs, out_specs, out_shape, operands, scratch):
    if comm is None:
        return dict(in_specs=in_specs, out_specs=out_specs, out_shape=out_shape, scratch_shapes=scratch), operands
    nc = len(comm.arrs)
    return dict(in_specs=in_specs + [HBM_SPEC] * nc, out_specs=out_specs + [HBM_SPEC] * nc,
                out_shape=out_shape + comm.out_shape, scratch_shapes=scratch + comm.scratch), operands + comm.arrs


def attn_fwd(qkv, name, comm=None):
    n_steps = S // TQ

    def body(*refs):
        (q_ref, k_ref, v_ref), (o_ref, r_ref), (acc_ref, z_even, z_odd, w_ref), phases = _comm_hooks(
            comm, refs, 3, 2, 4)
        p = pl.program_id(0)
        i = pl.program_id(1)
        if phases is not None:
            pl.when(jnp.logical_and(p == 0, i == 0))(phases[0])
            pl.when(jnp.logical_and(p == NPAIR - 1, i == n_steps - 1))(phases[1])
        chains = [(sub, h) for sub in range(NSUB) for h in range(2)]
        q_sub = [_head_halves(q_ref[pl.ds(sub * RS, RS), :] * SCALE) for sub in range(NSUB)]
        after = _tri_and_ones("after")
        below_diagonal = (lax.broadcasted_iota(jnp.int32, (RS, TK), 1)
                          < lax.broadcasted_iota(jnp.int32, (RS, TK), 0))
        base = i * NSUB
        all_subs = list(range(NSUB))

        acc_ref[...] = jnp.zeros_like(acc_ref)
        r_ref[...] = jnp.zeros_like(r_ref)
        w_ref[...] = jnp.zeros_like(w_ref)

        def key_rows(block):
            return pl.ds(pl.multiple_of(block * TK, TK), TK)

        def store_scores(z_ref, block, subs):
            kb = k_ref[key_rows(block), :]
            for c, (sub, h) in enumerate(chains):
                if sub in subs:
                    z_ref[c] = lax.dot_general(q_sub[sub][h], kb, (((1,), (1,)), ((), ())),
                                               preferred_element_type=F32)

        def add_weighted_values(block, subs):
            vb = v_ref[key_rows(block), :]
            for sub in subs:
                acc_ref[pl.ds(sub * RS, RS), :] += _join_heads(*[
                    jnp.dot(w_ref[2 * sub + h], vb, preferred_element_type=F32) for h in range(2)])

        def step(block, z_ref, z_next_ref, subs, diagonal_sub, prev_subs, next_subs):
            if prev_subs:
                add_weighted_values(block + 1, prev_subs)
            if next_subs:
                store_scores(z_next_ref, jnp.maximum(block - 1, 0), next_subs)
            active = [(c, sub, h) for c, (sub, h) in enumerate(chains) if sub in subs]
            ls, sums = {}, {}
            for c, sub, h in active:
                ls[c] = _log_stay(z_ref[c])
                sums[c] = _dot_hilo(jnp.where(below_diagonal, ls[c], 0.0) if sub == diagonal_sub else ls[c], after)
            for c, sub, h in active:
                rows = pl.ds(sub * RS, RS)
                later = r_ref[h, rows, :]
                w = jnp.exp(z_ref[c] + ls[c] + (sums[c][:, :TK] + later))
                if sub == diagonal_sub:
                    w = jnp.where(below_diagonal, w, 0.0)
                w_ref[c] = w.astype(BF16)
                r_ref[h, rows, :] = later + sums[c][:, TK:]

        store_scores(z_even, base + NSUB - 1, [NSUB - 1])
        buffers = (z_even, z_odd)
        for j in reversed(range(NSUB)):
            subs = all_subs[j:]
            step(base + j, buffers[0], buffers[1], subs, j, all_subs[j + 1:], all_subs[j - 1:] if j else all_subs)
            buffers = buffers[::-1]
        assert buffers[0] is z_even

        @pl.loop(0, base // 2)
        def _(pair):
            block = base - 1 - 2 * pair
            step(block, z_even, z_odd, all_subs, None, all_subs, all_subs)
            step(block - 1, z_odd, z_even, all_subs, None, all_subs, all_subs)

        add_weighted_values(0, all_subs)
        o_ref[...] = acc_ref[...].astype(o_ref.dtype)
        if phases is not None:
            pl.when(jnp.logical_and(p == NPAIR - 1, i == n_steps - 1))(phases[2])

    kwargs, operands = _with_comm(
        comm, [Q_ROWS_SPEC, K_ALL_SPEC, V_ALL_SPEC], [PAIR_ROWS_SPEC, PAIR_TOTAL_SPEC],
        [jax.ShapeDtypeStruct((S, NH * HD), BF16), jax.ShapeDtypeStruct((NH, S, TK), F32)], [qkv, qkv, qkv],
        [pltpu.VMEM((TQ, LANES), F32), pltpu.VMEM((2 * NSUB, RS, TK), F32), pltpu.VMEM((2 * NSUB, RS, TK), F32),
         pltpu.VMEM((2 * NSUB, RS, TK), BF16)])
    return pl.pallas_call(
        body, name=name, grid=(NPAIR, n_steps),
        compiler_params=_cparams(("arbitrary", "arbitrary")), **kwargs,
    )(*operands)


def attn_bwd(qkv, dout, totals, name, comm=None):
    n_steps = S // TQ

    def body(*refs):
        ((q_ref, k_ref, v_ref, do_ref, r_ref), (dq_out, dk_out, dv_out),
         (z_even, z_odd, dw_even, dw_odd, before_ref, dbefore_ref, dz_ref, w_ref, dq_ref, dk_ref, dv_ref),
         phases) = _comm_hooks(comm, refs, 5, 3, 11)
        p = pl.program_id(0)
        i = pl.program_id(1)
        if phases is not None:
            pl.when(jnp.logical_and(p == 0, i == 0))(phases[0])
            pl.when(jnp.logical_and(p == NPAIR - 1, i == n_steps - 2))(phases[1])

        @pl.when(i == 0)
        def _():
            dk_ref[...] = jnp.zeros_like(dk_ref)
            dv_ref[...] = jnp.zeros_like(dv_ref)

        chains = [(sub, h) for sub in range(NSUB) for h in range(2)]
        nch = len(chains)
        qb = q_ref[...]
        dob = do_ref[...].astype(BF16)
        q_sub = [_head_halves(qb[sub * RS:(sub + 1) * RS] * SCALE) for sub in range(NSUB)]
        do_sub = [_head_halves(dob[sub * RS:(sub + 1) * RS]) for sub in range(NSUB)]
        upto = _tri_and_ones("upto")
        before_tri = _tri_and_ones("before")
        below_diagonal = (lax.broadcasted_iota(jnp.int32, (RS, TK), 1)
                          < lax.broadcasted_iota(jnp.int32, (RS, TK), 0))
        contract_lanes = (((1,), (1,)), ((), ()))
        contract_rows = (((0,), (0,)), ((), ()))
        base = i * NSUB
        all_subs = list(range(NSUB))

        def key_rows(block):
            return pl.ds(pl.multiple_of(block * TK, TK), TK)

        def store_products(bufs, block, subs):
            z_ref, dw_ref = bufs
            kb = k_ref[key_rows(block), :]
            vb = v_ref[key_rows(block), :]
            for c, (sub, h) in enumerate(chains):
                if sub in subs:
                    z_ref[c] = lax.dot_general(q_sub[sub][h], kb, contract_lanes, preferred_element_type=F32)
                    dw_ref[c] = lax.dot_general(do_sub[sub][h], vb, contract_lanes, preferred_element_type=F32)

        def add_gradients(block, subs):
            kb = k_ref[key_rows(block), :]
            for sub in subs:
                rows = pl.ds(sub * RS, RS)
                dq_ref[rows, :] += _join_heads(*[jnp.dot(dz_ref[h, rows, :], kb, preferred_element_type=F32)
                                                 for h in range(2)])
            dk_ref[key_rows(block), :] += _join_heads(*[
                lax.dot_general(dz_ref[h], qb, contract_rows, preferred_element_type=F32) for h in range(2)])
            dv_ref[key_rows(block), :] += _join_heads(*[
                lax.dot_general(w_ref[h], dob, contract_rows, preferred_element_type=F32) for h in range(2)])

        for ref in (dq_ref, before_ref, dbefore_ref, dz_ref, w_ref):
            ref[...] = jnp.zeros_like(ref)
        even, odd = (z_even, dw_even), (z_odd, dw_odd)
        store_products(even, 0, all_subs)

        def step(block, bufs, next_bufs, subs, diagonal_sub, prev_subs, next_subs):
            z_ref, dw_ref = bufs
            add_gradients(jnp.maximum(block - 1, 0), prev_subs)
            for sub in prev_subs:
                if sub not in subs:
                    dz_ref[:, pl.ds(sub * RS, RS), :] = jnp.zeros((2, RS, TK), BF16)
                    w_ref[:, pl.ds(sub * RS, RS), :] = jnp.zeros((2, RS, TK), BF16)
            if next_subs:
                store_products(next_bufs, block + 1, next_subs)
            active = [(c, sub, h) for c, (sub, h) in enumerate(chains) if sub in subs]
            ls, sums, dl, dsums = {}, {}, {}, {}
            for c, sub, h in active:
                ls[c] = _log_stay(z_ref[c])
                sums[c] = _dot_hilo(jnp.where(below_diagonal, ls[c], 0.0) if sub == diagonal_sub else ls[c], upto)
            for c, sub, h in active:
                rows = pl.ds(sub * RS, RS)
                before = before_ref[c]
                log_after = r_ref[h, rows, :] - (sums[c][:, :TK] + before)
                w = jnp.exp((z_ref[c] + ls[c]) + log_after)
                if sub == diagonal_sub:
                    w = jnp.where(below_diagonal, w, 0.0)
                dl[c] = dw_ref[c] * w
                dsums[c] = _dot_hilo(dl[c], before_tri)
                w_ref[h, rows, :] = w.astype(BF16)
                before_ref[c] = before + sums[c][:, TK:]
            for c, sub, h in active:
                rows = pl.ds(sub * RS, RS)
                dbefore = dbefore_ref[c]
                beta = jnp.exp(z_ref[c] + ls[c])
                if sub == diagonal_sub:
                    beta = jnp.where(below_diagonal, beta, 0.0)
                dstay = dsums[c][:, :TK] + dbefore
                dz_ref[h, rows, :] = ((dl[c] - beta * (dl[c] + dstay)) * SCALE).astype(BF16)
                dbefore_ref[c] = dbefore + dsums[c][:, TK:]

        @pl.loop(0, base // 2)
        def _(pair):
            step(2 * pair, even, odd, all_subs, None, all_subs, all_subs)
            step(2 * pair + 1, odd, even, all_subs, None, all_subs, all_subs)

        bufs = (even, odd)
        for j in range(NSUB):
            step(base + j, bufs[0], bufs[1], all_subs[j:], j, all_subs[j - 1:] if j else all_subs, all_subs[j + 1:])
            bufs = bufs[::-1]

        add_gradients(base + NSUB - 1, all_subs[NSUB - 1:])
        dq_out[...] = dq_ref[...].astype(dq_out.dtype)

        @pl.when(i == n_steps - 1)
        def _():
            dk_out[...] = dk_ref[...].astype(dk_out.dtype)
            dv_out[...] = dv_ref[...].astype(dv_out.dtype)

        if phases is not None:
            pl.when(jnp.logical_and(p == NPAIR - 1, i == n_steps - 1))(phases[2])

    full = jax.ShapeDtypeStruct((S, NH * HD), BF16)
    kwargs, operands = _with_comm(
        comm, [Q_ROWS_SPEC, K_ALL_SPEC, V_ALL_SPEC, PAIR_ROWS_SPEC, PAIR_TOTAL_SPEC],
        [PAIR_ROWS_SPEC, PAIR_ALL_SPEC, PAIR_ALL_SPEC], [full, full, full], [qkv, qkv, qkv, dout, totals],
        [pltpu.VMEM((2 * NSUB, RS, TK), F32)] * 6 + [pltpu.VMEM((2, TQ, TK), BF16)] * 2
        + [pltpu.VMEM((TQ, LANES), F32), pltpu.VMEM((S, LANES), F32), pltpu.VMEM((S, LANES), F32)])
    return pl.pallas_call(
        body, name=name, grid=(NPAIR, n_steps),
        compiler_params=_cparams(("arbitrary", "arbitrary")), **kwargs,
    )(*operands)


def _proj_cols(first_col):
    base = first_col // LANES
    return pl.BlockSpec((S, LANES), lambda j: (0, base + j))


CONV_OUT_SPEC = pl.BlockSpec((S, LANES), lambda j: (0, j))
CONV_DOUT_SPEC = pl.BlockSpec((S, LANES), lambda j: (0, (NH * HD) // LANES + j))
CONV_W_SPEC = pl.BlockSpec((8, LANES), lambda j: (0, j))
CONV_B_SPEC = pl.BlockSpec((1, LANES), lambda j: (0, j))


def _shift_down(u, n):
    rows = lax.broadcasted_iota(jnp.int32, u.shape, 0)
    return jnp.where(rows >= n, pltpu.roll(u, n, 0), 0.0)


def _shift_up(u, n):
    rows = lax.broadcasted_iota(jnp.int32, u.shape, 0)
    return jnp.where(rows < S - n, pltpu.roll(u, S - n, 0), 0.0)


def conv_fwd(proj, cw8, cb, name):
    def body(bg_ref, cg_ref, hc_ref, w_ref, b_ref, o_ref):
        u = cg_ref[...] * hc_ref[...]
        w = w_ref[...]
        y = w[0:1, :] * _shift_down(u, 2) + w[1:2, :] * _shift_down(u, 1) + w[2:3, :] * u + b_ref[...]
        o_ref[...] = bg_ref[...] * y

    return pl.pallas_call(
        body, name=name, grid=(CW // LANES,),
        in_specs=[_proj_cols(0), _proj_cols(CW), _proj_cols(2 * CW), CONV_W_SPEC, CONV_B_SPEC],
        out_specs=CONV_OUT_SPEC, out_shape=jax.ShapeDtypeStruct((S, CW), F32),
        compiler_params=_cparams(("parallel",)),
    )(proj, proj, proj, cw8, cb)


def conv_bwd(proj, dout, cw8, cb, name):
    def body(bg_ref, cg_ref, hc_ref, do_ref, w_ref, b_ref, dbg_ref, dcg_ref, dhc_ref, dw_ref, db_ref):
        cg, hc, do = cg_ref[...], hc_ref[...], do_ref[...]
        w = w_ref[...]
        u = cg * hc
        u1, u2 = _shift_down(u, 1), _shift_down(u, 2)
        y = w[0:1, :] * u2 + w[1:2, :] * u1 + w[2:3, :] * u + b_ref[...]
        dbg_ref[...] = (do * y).astype(dbg_ref.dtype)
        dy = do * bg_ref[...]
        db_ref[...] = jnp.sum(dy, axis=0, keepdims=True)
        dw_ref[...] = jnp.concatenate(
            [jnp.sum(dy * u2, axis=0, keepdims=True), jnp.sum(dy * u1, axis=0, keepdims=True),
             jnp.sum(dy * u, axis=0, keepdims=True), jnp.zeros((5, LANES), F32)], axis=0)
        du = w[2:3, :] * dy + w[1:2, :] * _shift_up(dy, 1) + w[0:1, :] * _shift_up(dy, 2)
        dcg_ref[...] = (du * hc).astype(dcg_ref.dtype)
        dhc_ref[...] = (du * cg).astype(dhc_ref.dtype)

    full = jax.ShapeDtypeStruct((S, CW), BF16)
    return pl.pallas_call(
        body, name=name, grid=(CW // LANES,),
        in_specs=[_proj_cols(0), _proj_cols(CW), _proj_cols(2 * CW), CONV_DOUT_SPEC, CONV_W_SPEC, CONV_B_SPEC],
        out_specs=[CONV_OUT_SPEC, CONV_OUT_SPEC, CONV_OUT_SPEC, CONV_W_SPEC, CONV_B_SPEC],
        out_shape=[full, full, full, jax.ShapeDtypeStruct((8, CW), F32), jax.ShapeDtypeStruct((1, CW), F32)],
        compiler_params=_cparams(("parallel",)),
    )(proj, proj, proj, dout, cw8, cb)


GELU_K = math.sqrt(2.0 / math.pi)
GELU_C = 0.044715


def _gelu(x):
    return 0.5 * x * (1.0 + jnp.tanh(GELU_K * (x + GELU_C * (x * x * x))))


def _gelu_grad(x):
    t = jnp.tanh(GELU_K * (x + GELU_C * (x * x * x)))
    return 0.5 * (1.0 + t) + 0.5 * x * (1.0 - t * t) * (GELU_K * (1.0 + 3.0 * GELU_C * (x * x)))


def _sg_masks():
    row = lax.broadcasted_iota(jnp.int32, (T, T), 0)
    col = lax.broadcasted_iota(jnp.int32, (T, T), 1)
    causal = jnp.right_shift(row, 6) >= jnp.right_shift(col, 6)
    head_of_col = jnp.right_shift(lax.broadcasted_iota(jnp.int32, (T, CW), 1), 6)
    return causal, head_of_col


def _sg_weights(sw_ref, causal):
    return [jnp.where(causal, sw_ref[h], 0.0).astype(BF16) for h in range(SG_HEADS)]


def _sg_mixed(vnb, weights, bias, head_of_col):
    mixed = bias
    for h in range(SG_HEADS):
        mh = jnp.dot(weights[h], vnb, preferred_element_type=F32)
        mixed = mixed + jnp.where(head_of_col == h, mh, 0.0)
    return mixed


SG_WINDOWS = 4
SG_ROWS = SG_WINDOWS * T
SG_U_SPEC = pl.BlockSpec((SG_ROWS, CW), lambda n: (n, 3))
SG_V_SPEC = pl.BlockSpec((SG_ROWS, CW), lambda n: (n, 4))
SG_ROW_SPEC = pl.BlockSpec((SG_ROWS, CW), lambda n: (n, 0))
SG_DOUT_SPEC = pl.BlockSpec((SG_ROWS, CW), lambda n: (n, 3))
SG_G_SPEC = pl.BlockSpec((1, CW), lambda n: (0, 0))
SG_W_SPEC = pl.BlockSpec((SG_HEADS, T, T), lambda n: (0, 0, 0))
SG_BIAS_SPEC = pl.BlockSpec((T, CW), lambda n: (0, 0))


def sg_fwd(proj, gn, sw, bias, name):
    def body(u_ref, v_ref, g_ref, sw_ref, bias_ref, o_ref):
        causal, head_of_col = _sg_masks()
        weights = _sg_weights(sw_ref, causal)
        for wdw in range(SG_WINDOWS):
            rows = pl.ds(wdw * T, T)
            gv = _gelu(v_ref[rows, :])
            rstd = lax.rsqrt(jnp.mean(gv * gv, axis=-1, keepdims=True) + EPS)
            vnb = ((gv * rstd) * g_ref[...]).astype(BF16)
            mixed = _sg_mixed(vnb, weights, bias_ref[...], head_of_col)
            o_ref[rows, :] = _gelu(u_ref[rows, :]) * mixed

    return pl.pallas_call(
        body, name=name, grid=(S // SG_ROWS,),
        in_specs=[SG_U_SPEC, SG_V_SPEC, SG_G_SPEC, SG_W_SPEC, SG_BIAS_SPEC],
        out_specs=SG_ROW_SPEC, out_shape=jax.ShapeDtypeStruct((S, CW), F32),
        compiler_params=_cparams(("parallel",)),
    )(proj, proj, gn, sw, bias)


def sg_bwd(proj, dout, gn, sw, bias, name):
    def body(u_ref, v_ref, do_ref, g_ref, sw_ref, bias_ref, du_ref, dv_ref, dg_ref, dsw_ref, dbias_ref):
        @pl.when(pl.program_id(0) == 0)
        def _():
            dg_ref[...] = jnp.zeros_like(dg_ref)
            dsw_ref[...] = jnp.zeros_like(dsw_ref)
            dbias_ref[...] = jnp.zeros_like(dbias_ref)

        causal, head_of_col = _sg_masks()
        weights = _sg_weights(sw_ref, causal)
        gnv = g_ref[...]
        for wdw in range(SG_WINDOWS):
            rows = pl.ds(wdw * T, T)
            uv, vv, do = u_ref[rows, :], v_ref[rows, :], do_ref[rows, :]
            gv = _gelu(vv)
            rstd = lax.rsqrt(jnp.mean(gv * gv, axis=-1, keepdims=True) + EPS)
            xhat = gv * rstd
            vnb = (xhat * gnv).astype(BF16)
            mixed = _sg_mixed(vnb, weights, bias_ref[...], head_of_col)
            du_ref[rows, :] = ((do * mixed) * _gelu_grad(uv)).astype(du_ref.dtype)
            dmix = do * _gelu(uv)
            dbias_ref[...] += dmix
            dmixb = dmix.astype(BF16)
            dvn = jnp.zeros((T, CW), F32)
            for h in range(SG_HEADS):
                dvh = lax.dot_general(weights[h], dmixb, (((0,), (0,)), ((), ())), preferred_element_type=F32)
                dvn = dvn + jnp.where(head_of_col == h, dvh, 0.0)
                dmh = jnp.where(head_of_col == h, dmixb, jnp.zeros_like(dmixb))
                dwh = lax.dot_general(dmh, vnb, (((1,), (1,)), ((), ())), preferred_element_type=F32)
                dsw_ref[h] += jnp.where(causal, dwh, 0.0)
            dg_ref[...] += jnp.sum(dvn * xhat, axis=0, keepdims=True)
            dxhat = dvn * gnv
            dgv = rstd * (dxhat - xhat * jnp.mean(dxhat * xhat, axis=-1, keepdims=True))
            dv_ref[rows, :] = (dgv * _gelu_grad(vv)).astype(dv_ref.dtype)

    full = jax.ShapeDtypeStruct((S, CW), BF16)
    return pl.pallas_call(
        body, name=name, grid=(S // SG_ROWS,),
        in_specs=[SG_U_SPEC, SG_V_SPEC, SG_DOUT_SPEC, SG_G_SPEC, SG_W_SPEC, SG_BIAS_SPEC],
        out_specs=[SG_ROW_SPEC, SG_ROW_SPEC, SG_G_SPEC, SG_W_SPEC, SG_BIAS_SPEC],
        out_shape=[full, full, jax.ShapeDtypeStruct((1, CW), F32),
                   jax.ShapeDtypeStruct((SG_HEADS, T, T), F32), jax.ShapeDtypeStruct((T, CW), F32)],
        compiler_params=_cparams(("arbitrary",)),
    )(proj, proj, dout, gn, sw, bias)


ADA_COLS = NMOD * D // NDEV


def ada_fwd(c_all, ada_w, ada_b_mine, name):
    def body(c_ref, w_ref, b_ref, o_ref, ca_ref):
        cv = c_ref[...]
        ca = cv * (1.0 / (1.0 + jnp.exp(-cv)))
        ca_ref[...] = ca
        cab = ca.astype(BF16)
        for l in range(L):
            o_ref[l] = jnp.dot(cab, w_ref[l].astype(BF16), preferred_element_type=F32) + b_ref[l]

    return pl.pallas_call(
        body, name=name,
        out_shape=[jax.ShapeDtypeStruct((L, NDEV, ADA_COLS), F32), jax.ShapeDtypeStruct((NDEV, D), F32)],
        compiler_params=_cparams(),
    )(c_all, ada_w, ada_b_mine)


def ada_bwd(ca, dmod_cols, name):
    def body(ca_ref, dm_ref, o_ref):
        cab = ca_ref[...].astype(BF16)
        for l in range(L):
            o_ref[l] = lax.dot_general(cab, dm_ref[l].astype(BF16), (((0,), (0,)), ((), ())),
                                       preferred_element_type=F32)

    return pl.pallas_call(
        body, name=name, out_shape=jax.ShapeDtypeStruct((L, D, ADA_COLS), F32),
        compiler_params=_cparams(),
    )(ca, dmod_cols)


def _adamw(w, g, m, v):
    m = B1 * m + (1.0 - B1) * g
    v = B2 * v + (1.0 - B2) * (g * g)
    m_hat = m / BC1
    v_hat = v / BC2
    delta = -LR * (m_hat / (jnp.sqrt(v_hat) + AEPS) + WD * w)
    return delta, m, v


VEC_ROWS_PER_LAYER = 8
VEC_FINAL_ROW = L * VEC_ROWS_PER_LAYER
VEC_ROWS = VEC_FINAL_ROW + 8
W256_TAPS, W256_CONV_B, W256_GN = 0, 8, 9
W256_ROWS_PER_LAYER = 16


def small_update(vec_all, w256_all, sb_all, sw_all, params, name):
    n_par = len(params)

    def body(*refs):
        vec_ref, w256_ref, sb_ref = refs[:3]
        sw_refs = refs[3:3 + L]
        par_refs = [refs[3 + L + 3 * k:3 + L + 3 * k + 3] for k in range(n_par)]
        out = refs[3 + L + 3 * n_par:]
        out_par = [out[4 * k:4 * k + 4] for k in range(n_par)]
        loss_ref, taps_ref = out[4 * n_par:]

        def total(ref, idx):
            acc = ref[(0,) + idx].astype(F32)
            for d in range(1, NDEV):
                acc = acc + ref[(d,) + idx].astype(F32)
            return acc

        def update(k, region, g):
            w_ref, m_ref, v_ref = par_refs[k]
            g_ref, d_ref, nm_ref, nv_ref = out_par[k]
            delta, nm, nv = _adamw(w_ref[region], g, m_ref[region], v_ref[region])
            g_ref[region] = g
            d_ref[region] = delta
            nm_ref[region] = nm
            nv_ref[region] = nv

        for l in range(L):
            base = l * VEC_ROWS_PER_LAYER
            for k in range(NMOD):
                update(0, (slice(l, l + 1), slice(k * D, (k + 1) * D)), total(vec_ref, (slice(base + k, base + k + 1),)))
            update(1, (slice(l, l + 1),), total(vec_ref, (slice(base + 6, base + 7),)))
            update(2, (slice(l, l + 1),), total(vec_ref, (slice(base + 7, base + 8),)))
            wbase = l * W256_ROWS_PER_LAYER
            update(4, (slice(l, l + 1),), total(w256_ref, (slice(wbase + W256_CONV_B, wbase + W256_CONV_B + 1),)))
            update(5, (slice(l, l + 1),), total(w256_ref, (slice(wbase + W256_GN, wbase + W256_GN + 1),)))
            update(6, (l,), total(sw_refs[l], ()))
            update(7, (l,), total(sb_ref, (slice(l * SG_HEADS, (l + 1) * SG_HEADS),)))
            taps_ref[l] = total(w256_ref, (slice(wbase + W256_TAPS, wbase + W256_TAPS + 8),))
        update(3, (slice(0, 1),), total(vec_ref, (slice(VEC_FINAL_ROW, VEC_FINAL_ROW + 1),)))
        loss_ref[...] = total(vec_ref, (slice(VEC_FINAL_ROW + 1, VEC_FINAL_ROW + 2), slice(0, LANES)))

    out_shape = []
    for w, _, _ in params:
        out_shape += [jax.ShapeDtypeStruct(w.shape, F32)] * 4
    out_shape += [jax.ShapeDtypeStruct((1, LANES), F32), jax.ShapeDtypeStruct((L, 8, CW), F32)]
    outs = pl.pallas_call(body, name=name, out_shape=out_shape, compiler_params=_cparams())(
        vec_all, w256_all, sb_all, *sw_all, *[a for p in params for a in p])
    return [outs[4 * k:4 * k + 4] for k in range(n_par)], outs[4 * n_par:]


def adamw_plain(w, g, m, v, tr, name):
    rows, cols = w.shape
    spec = pl.BlockSpec((tr, cols), lambda i: (i, 0))

    def body(w_ref, g_ref, m_ref, v_ref, d_ref, nm_ref, nv_ref):
        delta, nm, nv = _adamw(w_ref[...], g_ref[...], m_ref[...], v_ref[...])
        d_ref[...] = delta
        nm_ref[...] = nm
        nv_ref[...] = nv

    shp = jax.ShapeDtypeStruct((rows, cols), F32)
    return pl.pallas_call(
        body, name=name, grid=(rows // tr,), in_specs=[spec] * 4, out_specs=[spec] * 3,
        out_shape=[shp, shp, shp], compiler_params=_cparams(("parallel",)),
    )(w, g, m, v)


SC_TILES = 32
SC_ROWS = 8


def adamw_sparsecore(w, g, m, v, name):
    rows, cols = w.shape
    per_tile = rows // SC_TILES
    assert rows % (SC_TILES * SC_ROWS) == 0 and cols % 16 == 0

    def body(w_hbm, g_hbm, m_hbm, v_hbm, d_hbm, nm_hbm, nv_hbm, w_buf, g_buf, m_buf, v_buf):
        tile = lax.axis_index("subcore") * 2 + lax.axis_index("core")

        @pl.loop(0, per_tile, step=SC_ROWS)
        def _(off):
            span = pl.ds(tile * per_tile + off, SC_ROWS)
            pltpu.sync_copy(w_hbm.at[span], w_buf)
            pltpu.sync_copy(g_hbm.at[span], g_buf)
            pltpu.sync_copy(m_hbm.at[span], m_buf)
            pltpu.sync_copy(v_hbm.at[span], v_buf)
            for r in range(SC_ROWS):
                @pl.loop(0, cols, step=16)
                def _(c):
                    s = pl.ds(c, 16)
                    gv = g_buf[r, s]
                    mv = B1 * m_buf[r, s] + (1.0 - B1) * gv
                    vv = B2 * v_buf[r, s] + (1.0 - B2) * (gv * gv)
                    w_buf[r, s] = -LR * ((mv / BC1) / (jnp.sqrt(vv / BC2) + AEPS) + WD * w_buf[r, s])
                    m_buf[r, s] = mv
                    v_buf[r, s] = vv
            pltpu.sync_copy(w_buf, d_hbm.at[span])
            pltpu.sync_copy(m_buf, nm_hbm.at[span])
            pltpu.sync_copy(v_buf, nv_hbm.at[span])

    shp = jax.ShapeDtypeStruct((rows, cols), F32)
    return pl.kernel(
        body, name=name, out_type=[shp, shp, shp],
        mesh=plsc.VectorSubcoreMesh(core_axis_name="core", subcore_axis_name="subcore"),
        scratch_types=[pltpu.VMEM((SC_ROWS, cols), F32)] * 4,
    )(w, g, m, v)


def adamw_reduce(w, parts, m, v, tr, name, tie=None):
    _, rows, cols = w.shape
    spec = pl.BlockSpec((None, tr, cols), lambda l, i: (l, i, 0))
    pspecs = [pl.BlockSpec((NDEV, tr, cols), lambda l, i, k=k: (0, jnp.where(l == k, i, 0), 0)) for k in range(L)]

    ties = [] if tie is None else [tie]

    def body(w_ref, p0_ref, p1_ref, m_ref, v_ref, *rest):
        g_ref, d_ref, nm_ref, nv_ref = rest[len(ties):]
        first_layer = pl.program_id(0) == 0
        g = jnp.zeros((tr, cols), F32)
        for d in range(NDEV):
            g = g + jnp.where(first_layer, p0_ref[d], p1_ref[d]).astype(F32)
        delta, nm, nv = _adamw(w_ref[...], g, m_ref[...], v_ref[...])
        g_ref[...] = g
        d_ref[...] = delta
        nm_ref[...] = nm
        nv_ref[...] = nv

    shp = jax.ShapeDtypeStruct(w.shape, F32)
    return pl.pallas_call(
        body, name=name, grid=(L, rows // tr),
        in_specs=[spec] + pspecs + [spec, spec] + [pl.BlockSpec(t.shape, lambda l, i: (0, 0)) for t in ties],
        out_specs=[spec] * 4, out_shape=[shp] * 4, compiler_params=_cparams(("parallel", "parallel")),
    )(w, *parts, m, v, *ties)


SHARD_IN = PROJ // NDEV


def shards_to_columns(shards, name):
    tr = 256

    def body(i_ref, o_ref):
        for d in range(NDEV):
            o_ref[:, d * SHARD_IN:(d + 1) * SHARD_IN] = i_ref[d]

    return pl.pallas_call(
        body, name=name, grid=(D // tr,),
        in_specs=[pl.BlockSpec((NDEV, tr, SHARD_IN), lambda i: (0, i, 0))],
        out_specs=pl.BlockSpec((tr, PROJ), lambda i: (i, 0)),
        out_shape=jax.ShapeDtypeStruct((D, PROJ), shards.dtype), compiler_params=_cparams(("parallel",)),
    )(shards)


def columns_to_shards(mat, name):
    tr = 256

    def body(i_ref, o_ref):
        for d in range(NDEV):
            o_ref[d] = i_ref[:, d * SHARD_IN:(d + 1) * SHARD_IN]

    return pl.pallas_call(
        body, name=name, grid=(D // tr,),
        in_specs=[pl.BlockSpec((tr, PROJ), lambda i: (i, 0))],
        out_specs=pl.BlockSpec((NDEV, tr, SHARD_IN), lambda i: (0, i, 0)),
        out_shape=jax.ShapeDtypeStruct((NDEV, D, SHARD_IN), mat.dtype), compiler_params=_cparams(("parallel",)),
    )(mat)


def _pad_rows(flat, rows):
    return jnp.pad(flat, (0, rows * LANES - flat.shape[0])).reshape(rows, LANES)


def kernel(x, c, ada_w, ada_b, norm_mix_g, norm_mlp_g, w_in, conv_w, conv_b, gmlp_norm_g, spatial_w, spatial_b, w_out, mlp_w1, mlp_w2, final_norm_g, loss_target, m_ada_w, m_ada_b, m_norm_mix_g, m_norm_mlp_g, m_w_in, m_conv_w, m_conv_b, m_gmlp_norm_g, m_spatial_w, m_spatial_b, m_w_out, m_mlp_w1, m_mlp_w2, m_final_norm_g, v_ada_w, v_ada_b, v_norm_mix_g, v_norm_mlp_g, v_w_in, v_conv_w, v_conv_b, v_gmlp_norm_g, v_spatial_w, v_spatial_b, v_w_out, v_mlp_w1, v_mlp_w2, v_final_norm_g):
    me = _lin(_my_pos())
    x0 = x[0]
    target = loss_target[0]
    conv_shard = conv_w.shape[-1]

    w_in_b, w_out_b, w1_b, w2_b = [w.astype(BF16) for w in (w_in, w_out, mlp_w1, mlp_w2)]
    pack0 = _pad_rows(jnp.concatenate([c.reshape(-1), conv_w.reshape(-1)]), 16)
    g0, gw_in0 = run_comm(Gather([pack0, w_in_b[0]]), "gather_first")
    g0 = g0.reshape(NDEV, 16 * LANES)
    c_all = g0[:, :D]
    conv_full = (g0[:, D:D + L * 3 * conv_shard].reshape(NDEV, L, 3, conv_shard)
                 .transpose(1, 2, 0, 3).reshape(L, 3, CW))


    W_in = [shards_to_columns(gw_in0, "w_in_columns0"), None]
    W_out, W1, W2 = [None] * L, [None] * L, [None] * L

    ada_b_mine = lax.dynamic_slice(ada_b, (0, me * ADA_COLS), (L, ADA_COLS)).reshape(L, 1, ADA_COLS)
    mod_part, c_act = ada_fwd(c_all, ada_w, ada_b_mine, "ada_fwd")
    gmod = run_comm(Gather([mod_part]), "gather_mod")[0]
    mod = lax.dynamic_index_in_dim(gmod, me, axis=2, keepdims=False)
    mod = mod.transpose(1, 0, 2).reshape(L, NMOD, 1, D)
    early_weights, token = start_copies([w_out_b[0]], me, "gather_early0_start", True, after=gmod)
    mod = tied(mod, token)

    cw8 = jnp.pad(conv_full, ((0, 0), (0, 5), (0, 0)))
    sg_bias = jnp.repeat(spatial_b.transpose(0, 2, 1), HD, axis=2)

    saved = []
    xl = x0
    for l in range(L):
        sh_m, sc_m, g_m, sh_f, sc_f, g_f = [mod[l, k] for k in range(NMOD)]
        h1 = normmod_fwd(xl, norm_mix_g[l:l + 1], sc_m, sh_m, f"norm_mix_fwd{l}")
        if l > 0:
            gw_in, gw_out = finish_copies(early_weights, xl, f"gather_early{l}_wait")
            W_in[l] = shards_to_columns(gw_in, f"w_in_columns{l}")
        qkv = mm_layer("proj_qkv", l, h1, W_in[l], out_dtypes=[BF16], cols=(0, QKV))[0]
        proj = mm_layer("proj_rest", l, h1, W_in[l], out_dtypes=[F32], cols=(QKV, REST))[0]
        a_out, a_tot, gw2, gw1 = attn_fwd(qkv, f"attn_fwd{l}", comm=Gather([w2_b[l], w1_b[l]]))
        if l == 0:
            gw_out, = finish_copies(early_weights, a_out, f"gather_early{l}_wait")
        W_out[l] = gw_out.reshape(D, D)
        W1[l] = gw1
        W2[l] = gw2.reshape(DFF, D)
        if l + 1 < L:
            early_weights, token = start_copies([w_in_b[l + 1], w_out_b[l + 1]], me, f"gather_early{l + 1}_start", True,
                                                after=a_out)
            g_m = tied(g_m, token)
        c_out = conv_fwd(proj, cw8[l], conv_b[l:l + 1], f"conv_fwd{l}")
        s_out = sg_fwd(proj, gmlp_norm_g[l:l + 1], spatial_w[l], sg_bias[l], f"sg_fwd{l}")
        cat = jnp.concatenate([a_out, c_out.astype(BF16), s_out.astype(BF16)], axis=1)
        mix, x1, h2 = mm_layer("mix", l, cat, W_out[l], out_dtypes=[F32, F32, BF16], epilogue=_residual_then_norm,
                               extras=[(xl, "tile"), (g_m, "col"), (norm_mlp_g[l:l + 1], "col"), (sc_f, "col"),
                                       (sh_f, "col")])
        ra, r = mm_layer("mlp_up", l, h2, W1[l], out_dtypes=[BF16, BF16], b_blocks=True,
                         epilogue=lambda acc: (jnp.maximum(acc, 0.0), jnp.square(jnp.maximum(acc, 0.0))))
        m2, x2 = mm_layer("mlp_down", l, r, W2[l], out_dtypes=[F32, F32],
                          epilogue=lambda acc, xr, g: (acc, xr + g * acc), extras=[(x1, "tile"), (g_f, "col")])
        saved.append(dict(x=xl, h1=h1, proj=proj, qkv=qkv, a_tot=a_tot, cat=cat, mix=mix,
                          x1=x1, h2=h2, ra=ra, r=r, m2=m2))
        xl = x2

    dx, loss_part, d_final_g, dm2, dg_f = loss_head(xl, target, final_norm_g.reshape(1, D),
                                                    (saved[L - 1]["m2"], mod[L - 1, NMOD - 1]), "loss_head")

    p_in, p_out, p_w1, p_w2 = [None] * L, [None] * L, [None] * L, [None] * L
    grads_in_flight = [None] * L
    vec_rows, d_norm_mix, d_norm_mlp = [None] * L, [None] * L, [None] * L
    dcw8, d_conv_b, d_gn, d_sw, d_sb = [None] * L, [None] * L, [None] * L, [None] * L, [None] * L
    for l in reversed(range(L)):
        sv = saved[l]
        sh_m, sc_m, g_m, sh_f, sc_f, g_f = [mod[l, k] for k in range(NMOD)]
        da = mm_layer("mlp_down_dgrad", l, dm2, W2[l], out_dtypes=[BF16], trans_b=True,
                      epilogue=lambda acc, rav: (acc * (2.0 * rav.astype(F32)),), extras=[(sv["ra"], "tile")])[0]
        dW2 = mm_layer("mlp_down_wgrad", l, sv["r"], dm2, out_dtypes=[BF16], trans_a=True)[0]
        dW1 = mm_layer("mlp_up_wgrad", l, sv["h2"], da, out_dtypes=[BF16], trans_a=True, out_blocks=True)[0]
        dh2 = mm_layer("mlp_up_dgrad", l, da, W1[l], out_dtypes=[F32], trans_b=True, b_blocks=True)[0]
        dx1, dsc_f, dsh_f, d_norm_mlp[l], dmix, dg_m = normmod_bwd(
            sv["x1"], dh2, dx, norm_mlp_g[l:l + 1], sc_f, f"norm_mlp_bwd{l}", gate_next=(sv["mix"], g_m))
        dcat = mm_layer("mix_dgrad", l, dmix, W_out[l], out_dtypes=[F32], trans_b=True)[0]
        dW_out = mm_layer("mix_wgrad", l, sv["cat"], dmix, out_dtypes=[BF16], trans_a=True)[0]
        pieces_w2, pieces_out = dW2.reshape(NDEV, DFF // NDEV, D), dW_out.reshape(NDEV, D // NDEV, D)
        ride, late = ([pieces_w2, pieces_out], dW1) if l == L - 1 else ([pieces_w2, dW1], pieces_out)
        dq, dk, dv, *arrived = attn_bwd(sv["qkv"], dcat, sv["a_tot"], f"attn_bwd{l}", comm=Exchange(ride))
        p_w2[l] = arrived[0]
        (p_out if l == L - 1 else p_w1)[l] = arrived[1]
        dbg, dcg, dhc, dcw8[l], d_conv_b[l] = conv_bwd(sv["proj"], dcat, cw8[l], conv_b[l:l + 1], f"conv_bwd{l}")
        dus, dvs, d_gn[l], dsw, dbias = sg_bwd(sv["proj"], dcat, gmlp_norm_g[l:l + 1], spatial_w[l], sg_bias[l],
                                               f"sg_bwd{l}")
        d_sw[l] = dsw.astype(BF16)
        d_sb[l] = dbias.reshape(T, SG_HEADS, HD).sum(axis=2).T
        dproj = jnp.concatenate([dq, dk, dv, dbg, dcg, dhc, dus, dvs], axis=1).astype(BF16)
        dW_in = mm_layer("proj_wgrad", l, sv["h1"], dproj, out_dtypes=[BF16], trans_a=True)[0]
        pieces = columns_to_shards(dW_in, f"w_in_grad_shards{l}")
        grads_in_flight[l], token = start_copies([late, pieces], me, f"exchange_tail{l}_start", False)
        dh1 = mm_layer("proj_dgrad", l, dproj, W_in[l], out_dtypes=[F32], trans_b=True, extras=[(token, "tie")])[0]
        below = (saved[l - 1]["m2"], mod[l - 1, NMOD - 1]) if l > 0 else None
        dx, dsc_m, dsh_m, d_norm_mix[l], *gated_below = normmod_bwd(
            sv["x"], dh1, dx1, tied(norm_mix_g[l:l + 1], token), sc_m, f"norm_mix_bwd{l}", gate_next=below)
        vec_rows[l] = [dsh_m, dsc_m, dg_m, dsh_f, dsc_f, dg_f, d_norm_mix[l], d_norm_mlp[l]]
        if l > 0:
            dm2, dg_f = gated_below

    grad_x = dx.reshape(1, S, D)

    g_w2, d_w2, nm_w2, nv_w2 = adamw_reduce(mlp_w2, p_w2, m_mlp_w2, v_mlp_w2, 256, "adamw_mlp_w2", tie=token)
    p_w1[L - 1], p_in[L - 1] = finish_copies(grads_in_flight[L - 1], d_w2, f"exchange_tail{L - 1}_wait")
    g_w1, d_w1, nm_w1, nv_w1 = adamw_reduce(mlp_w1, p_w1, m_mlp_w1, v_mlp_w1, 256, "adamw_mlp_w1", tie=token)

    vec_pack = jnp.concatenate([row for l in range(L) for row in vec_rows[l]]
                               + [d_final_g, loss_part, jnp.zeros((VEC_ROWS - VEC_FINAL_ROW - 2, D), F32)], axis=0)
    vec_pack, _ = lax.optimization_barrier((vec_pack, (d_w1, d_w2)))
    w256_pack = jnp.concatenate([blk for l in range(L) for blk in (
        dcw8[l], d_conv_b[l], d_gn[l], jnp.zeros((W256_ROWS_PER_LAYER - W256_GN - 1, CW), F32))], axis=0)
    vec_all, w256_all, sb_all, *sw_all = run_comm(
        Gather([vec_pack, w256_pack, jnp.concatenate(d_sb, axis=0)] + d_sw), "gather_small_grads")

    dmod_all = (vec_all[:, :VEC_FINAL_ROW].reshape(NDEV, L, VEC_ROWS_PER_LAYER, D)[:, :, :NMOD]
                .reshape(NDEV, L, NMOD * D))
    dmod_cols = lax.dynamic_slice(dmod_all, (0, 0, me * ADA_COLS), (NDEV, L, ADA_COLS)).transpose(1, 0, 2)
    g_ada_w = ada_bwd(c_act, dmod_cols, "ada_bwd")

    flat2 = lambda t: t.reshape(L * D, ADA_COLS)
    d_ada_w, nm_ada_w, nv_ada_w = [t.reshape(L, D, ADA_COLS) for t in adamw_sparsecore(
        flat2(ada_w), flat2(g_ada_w), flat2(m_ada_w), flat2(v_ada_w), "adamw_ada_w")]

    after = jnp.concatenate([t.reshape(-1)[:1] for t in (d_w1, d_w2, g_ada_w)])
    p_out[0], p_in[0] = finish_copies(grads_in_flight[0], after, "exchange_tail0_wait")
    g_w_in, d_w_in, nm_w_in, nv_w_in = adamw_reduce(w_in, p_in, m_w_in, v_w_in, 256, "adamw_w_in")
    g_w_out, d_w_out, nm_w_out, nv_w_out = adamw_reduce(w_out, p_out, m_w_out, v_w_out, 128, "adamw_w_out")

    as_row = lambda t: t.reshape(1, D)
    small_params = [(ada_b, m_ada_b, v_ada_b), (norm_mix_g, m_norm_mix_g, v_norm_mix_g),
                    (norm_mlp_g, m_norm_mlp_g, v_norm_mlp_g),
                    (as_row(final_norm_g), as_row(m_final_norm_g), as_row(v_final_norm_g)),
                    (conv_b, m_conv_b, v_conv_b), (gmlp_norm_g, m_gmlp_norm_g, v_gmlp_norm_g),
                    (spatial_w, m_spatial_w, v_spatial_w), (spatial_b, m_spatial_b, v_spatial_b)]
    updated, (loss_sum, taps_sum) = small_update(vec_all, w256_all, sb_all, sw_all, small_params, "small_update")
    loss = loss_sum[0, 0]
    u_ada_b, u_norm_mix, u_norm_mlp, u_final, u_conv_b, u_gn, u_sw, u_sb = updated
    u_final = [t.reshape(D) for t in u_final]
    g_conv_w = lax.dynamic_slice(taps_sum, (0, 0, me * conv_shard), (L, 3, conv_shard))
    flat_cw = lambda t: t.reshape(L * 3, conv_shard)
    u_conv_w = [g_conv_w] + [t.reshape(L, 3, conv_shard) for t in adamw_plain(
        flat_cw(conv_w), flat_cw(g_conv_w), flat_cw(m_conv_w), flat_cw(v_conv_w), L * 3, "adamw_conv_w")]
    small_sets = [u_ada_b, u_norm_mix, u_norm_mlp, u_conv_w, u_conv_b, u_gn, u_sw, u_sb, u_final]
    small_g, sd, snm, snv = [[u[k] for u in small_sets] for k in range(4)]

    def ordered(big, small):
        ada, win, wout, w1, w2 = big
        return [ada, small[0], small[1], small[2], win, small[3], small[4], small[5], small[6], small[7],
                wout, w1, w2, small[8]]

    grads = ordered([g_ada_w, g_w_in, g_w_out, g_w1, g_w2], small_g)
    deltas = ordered([d_ada_w, d_w_in, d_w_out, d_w1, d_w2], sd)
    new_m = ordered([nm_ada_w, nm_w_in, nm_w_out, nm_w1, nm_w2], snm)
    new_v = ordered([nv_ada_w, nv_w_in, nv_w_out, nv_w1, nv_w2], snv)
    return (loss, grad_x, *grads, *deltas, *new_m, *new_v)
```
